```python
import math
import jax, jax.numpy as jnp
from jax import lax
import numpy as np

D_MODEL = 1024
BATCH = 8
SEQ = 4096
DEPTH = 1

HEAD_DIM = 64
MIX_WIDTH = D_MODEL
POOL_WIDTH = MIX_WIDTH // 4
POOL_GROUPS = 4
POOL_GROUP_DIM = POOL_WIDTH // POOL_GROUPS
POOL_WINDOWS = (2, 4, 8, 16)
FOX_WIDTH = MIX_WIDTH // 2
FOX_HEADS = FOX_WIDTH // HEAD_DIM
MEM_HEADS = 4
MEM_WIDTH = MEM_HEADS * HEAD_DIM
N_MEM = 256
Q_BLOCK = 128
EPS = 1e-6
SPLIT_SIZES = (POOL_WIDTH, POOL_WIDTH, FOX_WIDTH, FOX_WIDTH, FOX_WIDTH, FOX_HEADS, FOX_WIDTH, MEM_WIDTH, MEM_WIDTH)
IN_WIDTH = sum(SPLIT_SIZES)

kernel_name = "hymba_pool_fox_memory_layer"


def rms_norm(x, g):
    xf = x.astype(jnp.float32)
    y = xf * lax.rsqrt(jnp.mean(xf * xf, axis=-1, keepdims=True) + EPS)
    return (y * g.astype(jnp.float32)).astype(x.dtype)


def split_cols(proj):
    parts, off = [], 0
    for w in SPLIT_SIZES:
        parts.append(proj[..., off:off + w])
        off += w
    return parts


def to_heads(t, n_heads):
    b, s, _ = t.shape
    return t.reshape(b, s, n_heads, HEAD_DIM).transpose(0, 2, 1, 3)


def from_heads(t):
    b, h, s, d = t.shape
    return t.transpose(0, 2, 1, 3).reshape(b, s, h * d)


def pool_mixer(u, w_pool, scale):
    b, s, _ = u.shape
    uf = u.reshape(b, s, POOL_GROUPS, POOL_GROUP_DIM).astype(jnp.float32)
    csum = jnp.cumsum(uf, axis=1)
    pos = jnp.arange(1, s + 1, dtype=jnp.float32)
    pooled = []
    for g, w in enumerate(POOL_WINDOWS):
        cg = csum[:, :, g]
        lag = jnp.pad(cg, ((0, 0), (w, 0), (0, 0)))[:, :s]
        count = jnp.minimum(pos, float(w))
        pooled.append((cg - lag) / count[None, :, None])
    pooled = jnp.stack(pooled, axis=2)
    d = (pooled - uf).astype(u.dtype)
    y = jnp.einsum('bsgc,gce->bsge', d, w_pool).reshape(b, s, POOL_WIDTH)
    return y * scale


def fox_attention(q, k, v, logf):
    b, h, s, dh = q.shape
    n_blocks = s // Q_BLOCK
    F = jnp.cumsum(logf.astype(jnp.float32), axis=-1)
    qb = q.reshape(b, h, n_blocks, Q_BLOCK, dh).transpose(2, 0, 1, 3, 4)
    Fb = F.reshape(b, h, n_blocks, Q_BLOCK).transpose(2, 0, 1, 3)
    pos_k = jnp.arange(s)
    scale = 1.0 / math.sqrt(dh)

    def block(args):
        qi, Fi, i = args
        logits = jnp.einsum('bhqd,bhkd->bhqk', qi, k).astype(jnp.float32) * scale
        logits = logits + Fi[..., None] - F[:, :, None, :]
        pos_q = i * Q_BLOCK + jnp.arange(Q_BLOCK)
        causal = pos_k[None, :] <= pos_q[:, None]
        logits = jnp.where(causal, logits, -1e30)
        p = jax.nn.softmax(logits, axis=-1)
        return jnp.einsum('bhqk,bhkd->bhqd', p.astype(v.dtype), v)

    out = lax.map(block, (qb, Fb, jnp.arange(n_blocks)))
    return out.transpose(1, 2, 0, 3, 4).reshape(b, h, s, dh)


def memory_attention(q, k, v):
    scale = 1.0 / math.sqrt(q.shape[-1])
    logits = jnp.einsum('bhqd,bhmd->bhqm', q, k).astype(jnp.float32) * scale
    p = jax.nn.softmax(logits, axis=-1)
    return jnp.einsum('bhqm,bhmd->bhqd', p.astype(v.dtype), v)


def _fwd_setup_inputs(seed: int = 0) -> dict:
    key = jax.random.key(seed)
    ks = jax.random.split(key, 16)
    f32 = jnp.float32
    x = jax.random.normal(ks[0], (BATCH, SEQ, D_MODEL), f32)
    mem = jax.random.normal(ks[1], (BATCH, N_MEM, D_MODEL), f32)
    norm_g = 1.0 + 0.02 * jax.random.normal(ks[2], (DEPTH, D_MODEL), f32)
    w_in = jax.random.normal(ks[3], (DEPTH, D_MODEL, IN_WIDTH), f32) * D_MODEL ** -0.5
    b_f = 1.0 + 4.0 * jax.random.uniform(ks[4], (DEPTH, FOX_HEADS), f32)
    w_pool = jax.random.normal(ks[5], (DEPTH, POOL_GROUPS, POOL_GROUP_DIM, POOL_GROUP_DIM), f32) * POOL_GROUP_DIM ** -0.5
    pool_scale = 1.0 + 0.02 * jax.random.normal(ks[6], (DEPTH, POOL_WIDTH), f32)
    fox_q_g = 1.0 + 0.02 * jax.random.normal(ks[7], (DEPTH, HEAD_DIM), f32)
    fox_k_g = 1.0 + 0.02 * jax.random.normal(ks[8], (DEPTH, HEAD_DIM), f32)
    mem_norm_g = 1.0 + 0.02 * jax.random.normal(ks[9], (DEPTH, D_MODEL), f32)
    w_mem_kv = jax.random.normal(ks[10], (DEPTH, D_MODEL, 2 * MEM_WIDTH), f32) * D_MODEL ** -0.5
    mem_q_g = 1.0 + 0.02 * jax.random.normal(ks[11], (DEPTH, HEAD_DIM), f32)
    mem_k_g = 1.0 + 0.02 * jax.random.normal(ks[12], (DEPTH, HEAD_DIM), f32)
    w_out = jax.random.normal(ks[13], (DEPTH, MIX_WIDTH, D_MODEL), f32) * MIX_WIDTH ** -0.5
    return {"x": x, "mem": mem, "norm_g": norm_g, "w_in": w_in, "b_f": b_f,
            "w_pool": w_pool, "pool_scale": pool_scale, "fox_q_g": fox_q_g,
            "fox_k_g": fox_k_g, "mem_norm_g": mem_norm_g, "w_mem_kv": w_mem_kv,
            "mem_q_g": mem_q_g, "mem_k_g": mem_k_g, "w_out": w_out}


def _fwd_reference(x, mem, norm_g, w_in, b_f, w_pool, pool_scale, fox_q_g, fox_k_g,
              mem_norm_g, w_mem_kv, mem_q_g, mem_k_g, w_out):
    for l in range(DEPTH):
        h = rms_norm(x, norm_g[l])
        proj = jnp.einsum('bsd,de->bse', h, w_in[l])
        u_a, g_a, q_b, k_b, v_b, f_b, g_b, q_m, g_m = split_cols(proj)

        y_a = pool_mixer(u_a, w_pool[l], pool_scale[l])

        q = rms_norm(to_heads(q_b, FOX_HEADS), fox_q_g[l])
        k = rms_norm(to_heads(k_b, FOX_HEADS), fox_k_g[l])
        v = to_heads(v_b, FOX_HEADS)
        logf = jax.nn.log_sigmoid((f_b + b_f[l]).astype(jnp.float32)).transpose(0, 2, 1)
        y_b = from_heads(fox_attention(q, k, v, logf))

        mem_n = rms_norm(mem, mem_norm_g[l])
        kv = jnp.einsum('bmd,de->bme', mem_n, w_mem_kv[l])
        k_m = rms_norm(to_heads(kv[..., :MEM_WIDTH], MEM_HEADS), mem_k_g[l])
        v_m = to_heads(kv[..., MEM_WIDTH:], MEM_HEADS)
        q_mh = rms_norm(to_heads(q_m, MEM_HEADS), mem_q_g[l])
        y_m = from_heads(memory_attention(q_mh, k_m, v_m))

        mixed = jnp.concatenate([y_a * jax.nn.silu(g_a),
                                 y_b * jax.nn.silu(g_b),
                                 y_m * jax.nn.silu(g_m)], axis=-1)
        x = x + jnp.einsum('bse,ed->bsd', mixed, w_out[l])
    return x


import jax as _jax
import jax.numpy as _jnp

TWIN_FORMAT = 'train_step'
FWD_PARAMS = ['x', 'mem', 'norm_g', 'w_in', 'b_f', 'w_pool', 'pool_scale', 'fox_q_g', 'fox_k_g', 'mem_norm_g', 'w_mem_kv', 'mem_q_g', 'mem_k_g', 'w_out']
TWIN_WEIGHTS = ['norm_g', 'w_in', 'b_f', 'w_pool', 'pool_scale', 'fox_q_g', 'fox_k_g', 'mem_norm_g', 'w_mem_kv', 'mem_q_g', 'mem_k_g', 'w_out']
TWIN_DIFF_INPUT = 'x'
TWIN_INPUTS = ['x', 'mem', 'norm_g', 'w_in', 'b_f', 'w_pool', 'pool_scale', 'fox_q_g', 'fox_k_g', 'mem_norm_g', 'w_mem_kv', 'mem_q_g', 'mem_k_g', 'w_out', 'loss_target', 'm_norm_g', 'm_w_in', 'm_b_f', 'm_w_pool', 'm_pool_scale', 'm_fox_q_g', 'm_fox_k_g', 'm_mem_norm_g', 'm_w_mem_kv', 'm_mem_q_g', 'm_mem_k_g', 'm_w_out', 'v_norm_g', 'v_w_in', 'v_b_f', 'v_w_pool', 'v_pool_scale', 'v_fox_q_g', 'v_fox_k_g', 'v_mem_norm_g', 'v_w_mem_kv', 'v_mem_q_g', 'v_mem_k_g', 'v_w_out']
TWIN_OUTPUTS = ['loss', 'grad_x', 'grad_norm_g', 'grad_w_in', 'grad_b_f', 'grad_w_pool', 'grad_pool_scale', 'grad_fox_q_g', 'grad_fox_k_g', 'grad_mem_norm_g', 'grad_w_mem_kv', 'grad_mem_q_g', 'grad_mem_k_g', 'grad_w_out', 'delta_norm_g', 'delta_w_in', 'delta_b_f', 'delta_w_pool', 'delta_pool_scale', 'delta_fox_q_g', 'delta_fox_k_g', 'delta_mem_norm_g', 'delta_w_mem_kv', 'delta_mem_q_g', 'delta_mem_k_g', 'delta_w_out', 'new_m_norm_g', 'new_m_w_in', 'new_m_b_f', 'new_m_w_pool', 'new_m_pool_scale', 'new_m_fox_q_g', 'new_m_fox_k_g', 'new_m_mem_norm_g', 'new_m_w_mem_kv', 'new_m_mem_q_g', 'new_m_mem_k_g', 'new_m_w_out', 'new_v_norm_g', 'new_v_w_in', 'new_v_b_f', 'new_v_w_pool', 'new_v_pool_scale', 'new_v_fox_q_g', 'new_v_fox_k_g', 'new_v_mem_norm_g', 'new_v_w_mem_kv', 'new_v_mem_q_g', 'new_v_mem_k_g', 'new_v_w_out']
TWIN_LEAF_KINDS = {'loss': 'loss', 'grad_x': 'grad_x', 'grad_norm_g': 'grad_w', 'grad_w_in': 'grad_w', 'grad_b_f': 'grad_w', 'grad_w_pool': 'grad_w', 'grad_pool_scale': 'grad_w', 'grad_fox_q_g': 'grad_w', 'grad_fox_k_g': 'grad_w', 'grad_mem_norm_g': 'grad_w', 'grad_w_mem_kv': 'grad_w', 'grad_mem_q_g': 'grad_w', 'grad_mem_k_g': 'grad_w', 'grad_w_out': 'grad_w', 'delta_norm_g': 'delta_w', 'delta_w_in': 'delta_w', 'delta_b_f': 'delta_w', 'delta_w_pool': 'delta_w', 'delta_pool_scale': 'delta_w', 'delta_fox_q_g': 'delta_w', 'delta_fox_k_g': 'delta_w', 'delta_mem_norm_g': 'delta_w', 'delta_w_mem_kv': 'delta_w', 'delta_mem_q_g': 'delta_w', 'delta_mem_k_g': 'delta_w', 'delta_w_out': 'delta_w', 'new_m_norm_g': 'new_m', 'new_m_w_in': 'new_m', 'new_m_b_f': 'new_m', 'new_m_w_pool': 'new_m', 'new_m_pool_scale': 'new_m', 'new_m_fox_q_g': 'new_m', 'new_m_fox_k_g': 'new_m', 'new_m_mem_norm_g': 'new_m', 'new_m_w_mem_kv': 'new_m', 'new_m_mem_q_g': 'new_m', 'new_m_mem_k_g': 'new_m', 'new_m_w_out': 'new_m', 'new_v_norm_g': 'new_v', 'new_v_w_in': 'new_v', 'new_v_b_f': 'new_v', 'new_v_w_pool': 'new_v', 'new_v_pool_scale': 'new_v', 'new_v_fox_q_g': 'new_v', 'new_v_fox_k_g': 'new_v', 'new_v_mem_norm_g': 'new_v', 'new_v_w_mem_kv': 'new_v', 'new_v_mem_q_g': 'new_v', 'new_v_mem_k_g': 'new_v', 'new_v_w_out': 'new_v'}


def _forward(args):
    return _fwd_reference(*[args[k] for k in FWD_PARAMS])


def _output_shape():
    out = _jax.eval_shape(lambda: _forward(_fwd_setup_inputs(0)))
    return out.shape, out.dtype

N_MICROBATCH = 1
ADAM_LR = 0.001
ADAM_B1 = 0.9
ADAM_B2 = 0.999
ADAM_EPS = 1e-08
ADAM_WD = 0.01
ADAM_STEP = 10
PER_EXAMPLE_BATCH_AXIS = {'x': 0, 'mem': 0, 'loss_target': 0}
SHARED_INPUTS = []
_WEIGHT_DTYPES = {'norm_g': _jnp.float32, 'w_in': _jnp.float32, 'b_f': _jnp.float32, 'w_pool': _jnp.float32, 'pool_scale': _jnp.float32, 'fox_q_g': _jnp.float32, 'fox_k_g': _jnp.float32, 'mem_norm_g': _jnp.float32, 'w_mem_kv': _jnp.float32, 'mem_q_g': _jnp.float32, 'mem_k_g': _jnp.float32, 'w_out': _jnp.float32}
MOMENT_SCALE = {'norm_g': 6.264293e+00, 'w_in': 1.301425e-01, 'b_f': 3.127945e+01, 'w_pool': 9.244533e-01, 'pool_scale': 8.798907e+00, 'fox_q_g': 4.023393e+00, 'fox_k_g': 4.027224e+00, 'mem_norm_g': 2.691007e-02, 'w_mem_kv': 1.984339e-02, 'mem_q_g': 4.414029e-01, 'mem_k_g': 4.378973e-01, 'w_out': 1.233164e-01}


def _to_microbatches(a, axis):
    t = _jnp.moveaxis(a, axis, 0)
    t = t.reshape((N_MICROBATCH, t.shape[0] // N_MICROBATCH) + t.shape[1:])
    return _jnp.moveaxis(t, 1, axis + 1)


def setup_inputs(seed: int = 0) -> dict:
    inp = _fwd_setup_inputs(seed)
    key = _jax.random.fold_in(_jax.random.key(seed), 7919)
    shape, _ = _output_shape()
    out = dict(inp)
    out["loss_target"] = _jax.random.normal(_jax.random.fold_in(key, 0), shape, _jnp.float32)
    for i, name in enumerate(TWIN_WEIGHTS):
        w = inp[name].astype(_jnp.float32)
        if MOMENT_SCALE is None:
            s = _jnp.sqrt(_jnp.mean(_jnp.square(w)) + 1e-30)
        else:
            s = MOMENT_SCALE[name]
        km, kv = _jax.random.split(_jax.random.fold_in(key, i + 1))
        out[name] = w
        out["m_" + name] = s * _jax.random.normal(km, w.shape, _jnp.float32)
        out["v_" + name] = (s * s) * _jax.random.uniform(kv, w.shape, _jnp.float32, 0.5, 1.5)
    if N_MICROBATCH > 1:
        for name, axis in PER_EXAMPLE_BATCH_AXIS.items():
            out[name] = _to_microbatches(out[name], axis)
    return {'x': out['x'], 'mem': out['mem'], 'norm_g': out['norm_g'], 'w_in': out['w_in'], 'b_f': out['b_f'], 'w_pool': out['w_pool'], 'pool_scale': out['pool_scale'], 'fox_q_g': out['fox_q_g'], 'fox_k_g': out['fox_k_g'], 'mem_norm_g': out['mem_norm_g'], 'w_mem_kv': out['w_mem_kv'], 'mem_q_g': out['mem_q_g'], 'mem_k_g': out['mem_k_g'], 'w_out': out['w_out'], 'loss_target': out['loss_target'], 'm_norm_g': out['m_norm_g'], 'm_w_in': out['m_w_in'], 'm_b_f': out['m_b_f'], 'm_w_pool': out['m_w_pool'], 'm_pool_scale': out['m_pool_scale'], 'm_fox_q_g': out['m_fox_q_g'], 'm_fox_k_g': out['m_fox_k_g'], 'm_mem_norm_g': out['m_mem_norm_g'], 'm_w_mem_kv': out['m_w_mem_kv'], 'm_mem_q_g': out['m_mem_q_g'], 'm_mem_k_g': out['m_mem_k_g'], 'm_w_out': out['m_w_out'], 'v_norm_g': out['v_norm_g'], 'v_w_in': out['v_w_in'], 'v_b_f': out['v_b_f'], 'v_w_pool': out['v_w_pool'], 'v_pool_scale': out['v_pool_scale'], 'v_fox_q_g': out['v_fox_q_g'], 'v_fox_k_g': out['v_fox_k_g'], 'v_mem_norm_g': out['v_mem_norm_g'], 'v_w_mem_kv': out['v_w_mem_kv'], 'v_mem_q_g': out['v_mem_q_g'], 'v_mem_k_g': out['v_mem_k_g'], 'v_w_out': out['v_w_out']}


def _loss(weights, diff, rest, loss_target):
    with _jax.named_scope("forward"):
        args = {**rest, TWIN_DIFF_INPUT: diff, **{k: w.astype(_WEIGHT_DTYPES[k]) for k, w in weights.items()}}
        y = _forward(args)
    with _jax.named_scope("loss_head"):
        err = _jnp.square(y.astype(_jnp.float32) - loss_target)
        return 0.5 * _jnp.sum(_jnp.mean(err, axis=-1)) if err.ndim else 0.5 * err


def _adamw(w, g, m, v):
    m = ADAM_B1 * m + (1.0 - ADAM_B1) * g
    v = ADAM_B2 * v + (1.0 - ADAM_B2) * _jnp.square(g)
    m_hat = m / (1.0 - ADAM_B1 ** ADAM_STEP)
    v_hat = v / (1.0 - ADAM_B2 ** ADAM_STEP)
    delta = -ADAM_LR * (m_hat / (_jnp.sqrt(v_hat) + ADAM_EPS) + ADAM_WD * w)
    return delta, m, v


def reference(x, mem, norm_g, w_in, b_f, w_pool, pool_scale, fox_q_g, fox_k_g, mem_norm_g, w_mem_kv, mem_q_g, mem_k_g, w_out, loss_target, m_norm_g, m_w_in, m_b_f, m_w_pool, m_pool_scale, m_fox_q_g, m_fox_k_g, m_mem_norm_g, m_w_mem_kv, m_mem_q_g, m_mem_k_g, m_w_out, v_norm_g, v_w_in, v_b_f, v_w_pool, v_pool_scale, v_fox_q_g, v_fox_k_g, v_mem_norm_g, v_w_mem_kv, v_mem_q_g, v_mem_k_g, v_w_out):
    given = dict(x=x, mem=mem, norm_g=norm_g, w_in=w_in, b_f=b_f, w_pool=w_pool, pool_scale=pool_scale, fox_q_g=fox_q_g, fox_k_g=fox_k_g, mem_norm_g=mem_norm_g, w_mem_kv=w_mem_kv, mem_q_g=mem_q_g, mem_k_g=mem_k_g, w_out=w_out, loss_target=loss_target, m_norm_g=m_norm_g, m_w_in=m_w_in, m_b_f=m_b_f, m_w_pool=m_w_pool, m_pool_scale=m_pool_scale, m_fox_q_g=m_fox_q_g, m_fox_k_g=m_fox_k_g, m_mem_norm_g=m_mem_norm_g, m_w_mem_kv=m_w_mem_kv, m_mem_q_g=m_mem_q_g, m_mem_k_g=m_mem_k_g, m_w_out=m_w_out, v_norm_g=v_norm_g, v_w_in=v_w_in, v_b_f=v_b_f, v_w_pool=v_w_pool, v_pool_scale=v_pool_scale, v_fox_q_g=v_fox_q_g, v_fox_k_g=v_fox_k_g, v_mem_norm_g=v_mem_norm_g, v_w_mem_kv=v_w_mem_kv, v_mem_q_g=v_mem_q_g, v_mem_k_g=v_mem_k_g, v_w_out=v_w_out)
    weights = {n: given[n] for n in TWIN_WEIGHTS}
    shared = {n: given[n] for n in SHARED_INPUTS}
    per_example = {n: given[n] for n in ['x', 'mem']}
    grad_fn = _jax.value_and_grad(_loss, argnums=(0, 1))

    def one_microbatch(ex, loss_target):
        ex = dict(ex)
        diff = ex.pop(TWIN_DIFF_INPUT)
        return grad_fn(weights, diff, {**shared, **ex}, loss_target)

    if N_MICROBATCH == 1:
        loss, (grad_w, grad_x) = one_microbatch(per_example, given["loss_target"])
    else:
        def body(carry, xs):
            loss_sum, grad_sum = carry
            l_k, (gw_k, gx_k) = one_microbatch(xs[0], xs[1])
            with _jax.named_scope("update"):
                return (loss_sum + l_k, _jax.tree.map(_jnp.add, grad_sum, gw_k)), gx_k

        init = (_jnp.zeros((), _jnp.float32), _jax.tree.map(_jnp.zeros_like, weights))
        (loss, grad_w), grad_x = _jax.lax.scan(body, init, (per_example, given["loss_target"]))
    with _jax.named_scope("update"):
        delta_w, new_m, new_v = {}, {}, {}
        for n in TWIN_WEIGHTS:
            delta_w[n], new_m[n], new_v[n] = _adamw(weights[n], grad_w[n], given["m_" + n], given["v_" + n])
    return (loss, grad_x, *[grad_w[n] for n in TWIN_WEIGHTS], *[delta_w[n] for n in TWIN_WEIGHTS],
            *[new_m[n] for n in TWIN_WEIGHTS], *[new_v[n] for n in TWIN_WEIGHTS])
```

```python
import functools

import jax
import jax.numpy as jnp
from jax import lax
from jax.experimental import pallas as pl
from jax.experimental.pallas import tpu as pltpu

F32 = jnp.float32
BF16 = jnp.bfloat16
MESH = pl.DeviceIdType.MESH

D_MODEL = 1024
HEAD_DIM = 64
POOL_WIDTH = 256
FOX_WIDTH = 512
FOX_HEADS = 8
MEM_WIDTH = 256
N_MEM = 256
IN_WIDTH = 3080
EPS = 1e-6
ATT_SCALE = 0.125

ADAM_LR = 0.001
ADAM_B1 = 0.9
ADAM_B2 = 0.999
ADAM_EPS = 1e-08
ADAM_WD = 0.01
ADAM_STEP = 10

LANES = 128
PA_LO, QB_LO, KB_LO, VB_LO, GB_LO, PM_LO, FB_LO, PROJ_PAD = 0, 512, 1024, 1536, 2048, 2560, 3072, 3200
F_ORIG_LO = 2048

TILE = 512
VMEM_LIMIT = 56 * 1024 * 1024

PACK_W = 1024
PACK_IN, PACK_KV, PACK_OUT = 770, 128, 256
PACK_ROWS = 1184
PACK_HALF = PACK_ROWS // 2

SM_NORM, SM_MEMNORM, SM_PSCALE, SM_BF, SM_FQ, SM_FK, SM_MQ, SM_MK, SM_WPOOL, SM_LOSS, SMALL_ROWS = (
    0, 8, 16, 18, 19, 20, 21, 22, 23, 151, 152)


def _params(n_grid=1, vmem=VMEM_LIMIT):
    return pltpu.CompilerParams(dimension_semantics=("arbitrary",) * n_grid, vmem_limit_bytes=vmem)


def _rows(t, w):
    return pl.BlockSpec((t, w), lambda i: (i, 0))


def _rows_rev(t, w, n):
    return pl.BlockSpec((t, w), lambda i: (n - 1 - i, 0))


def _full(shape):
    return pl.BlockSpec(shape, lambda i: (0,) * len(shape))


def _sig(x):
    return 1.0 / (1.0 + jnp.exp(-x))


def _lane_lo(shape):
    return lax.broadcasted_iota(jnp.int32, shape, 1) < HEAD_DIM


def _pair_sum(v, lo):
    s0 = jnp.sum(jnp.where(lo, v, 0.0), axis=-1, keepdims=True)
    s1 = jnp.sum(jnp.where(lo, 0.0, v), axis=-1, keepdims=True)
    return jnp.where(lo, s0, s1)


def _head_rms(blk, lo):
    return lax.rsqrt(_pair_sum(blk * blk, lo) * (1.0 / HEAD_DIM) + EPS)


def _head_norm_bwd(dyn, xhat, rr, g, lo):
    a = dyn * g
    return rr * (a - xhat * (_pair_sum(xhat * a, lo) * (1.0 / HEAD_DIM)))


def _fold_heads(acc):
    tot = acc[:, 0:LANES]
    for p in range(1, acc.shape[1] // LANES):
        tot = tot + acc[:, p * LANES:(p + 1) * LANES]
    return tot + pltpu.roll(tot, HEAD_DIM, axis=1)


def _lane_pick(v, lane, idx):
    return jnp.sum(jnp.where(lane == idx, v, 0.0), axis=-1, keepdims=True)


NT = (((1,), (1,)), ((), ()))
TN = (((0,), (0,)), ((), ()))


def _dot(a, b, dims=None):
    if dims is None:
        return jnp.dot(a, b, preferred_element_type=F32)
    return lax.dot_general(a, b, dims, preferred_element_type=F32)


def _my_place():
    return lax.axis_index("x"), lax.axis_index("y"), lax.axis_index("c")


def _all_gather_halves(pack):
    h = pack.shape[0] // 2
    w = pack.shape[1]

    def body(x_ref, out_ref, send_sems, recv_sems, local_sem):
        x, y, c = _my_place()
        me, sibling = (x, y, c), (x, y, 1 - c)
        chips = [(1 - x, y), (x, 1 - y), (1 - x, 1 - y)]
        mine_src = x_ref.at[pl.ds(pl.multiple_of(c * h, 16), h), :]

        def blk(px, py, pc):
            return out_ref.at[4 * px + 2 * py + pc]

        def copy(k, block, to, src=None):
            return pltpu.make_async_remote_copy(
                src_ref=blk(*block) if src is None else src, dst_ref=blk(*block),
                send_sem=send_sems.at[k], recv_sem=recv_sems.at[k], device_id=to, device_id_type=MESH)

        mine = pltpu.make_async_copy(mine_src, blk(*me), local_sem)
        mine.start()
        first = [copy(0, me, sibling, src=mine_src)]
        first += [copy(1 + j, me, (*chip, c), src=mine_src) for j, chip in enumerate(chips)]
        for cp in first:
            cp.start()
        passed = [copy(4 + j, (*chip, c), sibling) for j, chip in enumerate(chips)]
        for j, chip in enumerate(chips):
            copy(1 + j, (*chip, c), me).wait_recv()
            passed[j].start()
        copy(0, sibling, me).wait_recv()
        for j, chip in enumerate(chips):
            copy(4 + j, (*chip, 1 - c), me).wait_recv()
        for cp in first + passed:
            cp.wait_send()
        mine.wait()

    return pl.pallas_call(
        body, name="weights_all_gather",
        out_shape=jax.ShapeDtypeStruct((8, h, w), pack.dtype),
        in_specs=[pl.BlockSpec(memory_space=pl.ANY)],
        out_specs=pl.BlockSpec(memory_space=pl.ANY),
        scratch_shapes=[pltpu.SemaphoreType.DMA((7,)), pltpu.SemaphoreType.DMA((7,)), pltpu.SemaphoreType.DMA],
    )(pack)


def _grad_reduce(gpack, small):
    h = gpack.shape[1] // 2
    w = gpack.shape[2]
    r = small.shape[0]
    chunks = [(lo, min(160, h - lo)) for lo in range(0, h, 160)]

    def body(g_ref, small_ref, out_ref, small_out_ref,
             recv_a, own_a, send_b, recv_b, fin, small_recv, send_sems, recv_sems, local_sems):
        x, y, c = _my_place()
        chip = 2 * x + y
        me_lin = 4 * x + 2 * y + c
        sibling = (x, y, 1 - c)
        my_rows = pl.ds(pl.multiple_of(c * h, 8), h)
        sib_rows = pl.ds(pl.multiple_of((1 - c) * h, 8), h)

        to_sib = pltpu.make_async_remote_copy(
            src_ref=g_ref.at[:, sib_rows, :], dst_ref=recv_a, send_sem=send_sems.at[0], recv_sem=recv_sems.at[0],
            device_id=sibling, device_id_type=MESH)
        to_sib.start()
        own = pltpu.make_async_copy(g_ref.at[:, my_rows, :], own_a, local_sems.at[0])
        own.start()

        small_copies = []
        for k in range(1, 8):
            peer = (me_lin + k) % 8
            cp = pltpu.make_async_remote_copy(
                src_ref=small_ref, dst_ref=small_recv.at[me_lin], send_sem=send_sems.at[4 + k],
                recv_sem=recv_sems.at[4 + k], device_id=(peer // 4, (peer // 2) % 2, peer % 2), device_id_type=MESH)
            cp.start()
            small_copies.append(cp)

        own.wait()
        to_sib.wait_recv()
        for j in range(4):
            for lo, size in chunks:
                rows = pl.ds(lo, size)
                send_b[j, rows, :] = (own_a[j, rows, :] + recv_a[j, rows, :]).astype(BF16)

        chip_copies = []
        for k in range(1, 4):
            dest = (chip + k) % 4
            cp = pltpu.make_async_remote_copy(
                src_ref=send_b.at[dest], dst_ref=recv_b.at[chip], send_sem=send_sems.at[k], recv_sem=recv_sems.at[k],
                device_id=(dest // 2, dest % 2, c), device_id_type=MESH)
            cp.start()
            chip_copies.append(cp)
        keep = pltpu.make_async_copy(send_b.at[chip], recv_b.at[chip], local_sems.at[1])
        keep.start()
        keep.wait()
        for cp in chip_copies:
            cp.wait_recv()
        for lo, size in chunks:
            rows = pl.ds(lo, size)
            tot = recv_b[0, rows, :].astype(F32) + recv_b[1, rows, :].astype(F32)
            tot = tot + recv_b[2, rows, :].astype(F32)
            fin[rows, :] = tot + recv_b[3, rows, :].astype(F32)

        give = pltpu.make_async_remote_copy(
            src_ref=fin, dst_ref=out_ref.at[my_rows, :], send_sem=send_sems.at[4], recv_sem=recv_sems.at[4],
            device_id=sibling, device_id_type=MESH)
        give.start()
        mine = pltpu.make_async_copy(fin, out_ref.at[my_rows, :], local_sems.at[0])
        mine.start()

        for cp in small_copies:
            cp.wait_recv()
        small_recv[me_lin] = small_ref[...]
        tot = small_recv[0]
        for d in range(1, 8):
            tot = tot + small_recv[d]
        small_out_ref[...] = tot

        give.wait_recv()
        mine.wait()
        to_sib.wait_send()
        give.wait_send()
        for cp in chip_copies + small_copies:
            cp.wait_send()

    return pl.pallas_call(
        body, name="grad_reduce",
        out_shape=(jax.ShapeDtypeStruct((2 * h, w), F32), jax.ShapeDtypeStruct((r, LANES), F32)),
        in_specs=[pl.BlockSpec(memory_space=pl.ANY), pl.BlockSpec(memory_space=pltpu.VMEM)],
        out_specs=(pl.BlockSpec(memory_space=pl.ANY), pl.BlockSpec(memory_space=pltpu.VMEM)),
        scratch_shapes=[
            pltpu.VMEM((4, h, w), F32), pltpu.VMEM((4, h, w), F32),
            pltpu.VMEM((4, h, w), BF16), pltpu.VMEM((4, h, w), BF16),
            pltpu.VMEM((h, w), F32), pltpu.VMEM((8, r, LANES), F32),
            pltpu.SemaphoreType.DMA((12,)), pltpu.SemaphoreType.DMA((12,)), pltpu.SemaphoreType.DMA((2,)),
        ],
        compiler_params=pltpu.CompilerParams(vmem_limit_bytes=VMEM_LIMIT),
    )(gpack, small)


def _mem_fwd(mem, mem_norm_g, w_kv, mk_g):
    n = mem.shape[0]

    def body(mem_ref, g_ref, w_ref, kg_ref, mn_ref, kv_ref, kn_ref, vm_ref):
        xm = mem_ref[...]
        rr = lax.rsqrt(jnp.mean(xm * xm, axis=-1, keepdims=True) + EPS)
        mnb = ((xm * rr) * g_ref[...]).astype(BF16)
        mn_ref[...] = mnb
        kv = _dot(mnb, w_ref[...])
        kv_ref[...] = kv
        lo = _lane_lo((n, LANES))
        for p in range(MEM_WIDTH // LANES):
            sl = slice(p * LANES, (p + 1) * LANES)
            kb = kv[:, sl]
            kn_ref[:, sl] = ((kb * _head_rms(kb, lo)) * kg_ref[:, sl]).astype(BF16)
        vm_ref[...] = kv[:, MEM_WIDTH:].astype(BF16)

    return pl.pallas_call(
        body, name="mem_fwd",
        out_shape=(jax.ShapeDtypeStruct((n, D_MODEL), BF16), jax.ShapeDtypeStruct((n, 2 * MEM_WIDTH), F32),
                   jax.ShapeDtypeStruct((n, MEM_WIDTH), BF16), jax.ShapeDtypeStruct((n, MEM_WIDTH), BF16)),
        compiler_params=pltpu.CompilerParams(vmem_limit_bytes=VMEM_LIMIT),
    )(mem, mem_norm_g, w_kv, mk_g)


def _fwd_in(x, norm_g, wp, bf_pad, fq_g, fk_g):
    s = x.shape[0]
    t = TILE
    n = s // t

    def body(x_ref, ng_ref, wp_ref, bf_ref, qg_ref, kg_ref,
             h_ref, pa_ref, qk_ref, qs_ref, kn_ref, v_ref, gb_ref, pm_ref, fb_ref, fcol_ref, ft_ref, carry_ref):
        @pl.when(pl.program_id(0) == 0)
        def _():
            carry_ref[...] = jnp.zeros_like(carry_ref)

        xv = x_ref[...]
        rr = lax.rsqrt(jnp.mean(xv * xv, axis=-1, keepdims=True) + EPS)
        hb = ((xv * rr) * ng_ref[...]).astype(BF16)
        h_ref[...] = hb

        def proj(lo, hi):
            return _dot(hb, wp_ref[:, lo:hi])

        pa_ref[...] = proj(PA_LO, QB_LO)
        lo = _lane_lo((t, LANES))
        for seg, g_ref, out_ref, scale in ((QB_LO, qg_ref, qs_ref, ATT_SCALE), (KB_LO, kg_ref, kn_ref, 1.0)):
            raw = proj(seg, seg + FOX_WIDTH)
            qk_ref[:, seg - QB_LO:seg - QB_LO + FOX_WIDTH] = raw
            for p in range(FOX_WIDTH // LANES):
                sl = slice(p * LANES, (p + 1) * LANES)
                blk = raw[:, sl]
                out_ref[:, sl] = (((blk * _head_rms(blk, lo)) * g_ref[:, sl]) * scale).astype(BF16)
        v_ref[...] = proj(VB_LO, GB_LO).astype(BF16)
        gb_ref[...] = proj(GB_LO, PM_LO)
        pm_ref[...] = proj(PM_LO, FB_LO)
        fb = proj(FB_LO, PROJ_PAD)
        fb_ref[...] = fb

        lane = lax.broadcasted_iota(jnp.int32, (t, LANES), 1)
        row = lax.broadcasted_iota(jnp.int32, (t, LANES), 0)
        z = fb + bf_ref[...]
        lf = -(jnp.maximum(-z, 0.0) + jnp.log1p(jnp.exp(-jnp.abs(z))))
        lf = jnp.where(lane < FOX_HEADS, lf, 0.0)
        sh = 1
        while sh < t:
            lf = lf + jnp.where(row >= sh, pltpu.roll(lf, sh, axis=0), 0.0)
            sh *= 2
        fcum = lf + carry_ref[...]
        fcol_ref[...] = fcum
        carry_ref[...] = fcol_ref[t - 1:t, :]
        ft_ref[0] = fcum.T[0:8, :]

    outs = (
        jax.ShapeDtypeStruct((s, D_MODEL), BF16),
        jax.ShapeDtypeStruct((s, 512), F32),
        jax.ShapeDtypeStruct((s, 2 * FOX_WIDTH), F32),
        jax.ShapeDtypeStruct((s, FOX_WIDTH), BF16),
        jax.ShapeDtypeStruct((s, FOX_WIDTH), BF16),
        jax.ShapeDtypeStruct((s, FOX_WIDTH), BF16),
        jax.ShapeDtypeStruct((s, FOX_WIDTH), F32),
        jax.ShapeDtypeStruct((s, 512), F32),
        jax.ShapeDtypeStruct((s, LANES), F32),
        jax.ShapeDtypeStruct((s, LANES), F32),
        jax.ShapeDtypeStruct((n, 8, t), F32),
    )
    return pl.pallas_call(
        body, name="fwd_in", grid=(n,), out_shape=outs,
        in_specs=[_rows(t, D_MODEL), _full((1, D_MODEL)), _full((D_MODEL, PROJ_PAD)), _full((1, LANES)),
                  _full((1, FOX_WIDTH)), _full((1, FOX_WIDTH))],
        out_specs=(_rows(t, D_MODEL), _rows(t, 512), _rows(t, 2 * FOX_WIDTH), _rows(t, FOX_WIDTH),
                   _rows(t, FOX_WIDTH), _rows(t, FOX_WIDTH), _rows(t, FOX_WIDTH), _rows(t, 512),
                   _rows(t, LANES), _rows(t, LANES), pl.BlockSpec((1, 8, t), lambda i: (i, 0, 0))),
        scratch_shapes=[pltpu.VMEM((1, LANES), F32)],
        compiler_params=_params(),
    )(x, norm_g, wp, bf_pad, fq_g, fk_g)


POOL_HALO = 16


def _pool_window(lane):
    return jnp.where(lane < 64, 2.0, jnp.where(lane < 128, 4.0, jnp.where(lane < 192, 8.0, 16.0)))


def _pool_pick(lane, s2, s4, s8, s16):
    return jnp.where(lane < 64, s2, jnp.where(lane < 128, s4, jnp.where(lane < 192, s8, s16)))


def _pool_fwd(pa, wbd, pscale):
    s = pa.shape[0]
    t = TILE
    n = s // t
    ext = t + POOL_HALO

    def body(pa_ref, w_ref, sc_ref, ma_ref, d_ref, ext_ref):
        i = pl.program_id(0)

        @pl.when(i == 0)
        def _():
            ext_ref[0:POOL_HALO, :] = jnp.zeros((POOL_HALO, POOL_WIDTH), F32)

        u = pa_ref[:, 0:POOL_WIDTH]
        ext_ref[POOL_HALO:ext, :] = u
        e = ext_ref[...]
        s2 = e + pltpu.roll(e, 1, axis=0)
        s4 = s2 + pltpu.roll(s2, 2, axis=0)
        s8 = s4 + pltpu.roll(s4, 4, axis=0)
        s16 = s8 + pltpu.roll(s8, 8, axis=0)
        lane_e = lax.broadcasted_iota(jnp.int32, (ext, POOL_WIDTH), 1)
        win = _pool_pick(lane_e, s2, s4, s8, s16)[POOL_HALO:ext, :]
        lane = lax.broadcasted_iota(jnp.int32, (t, POOL_WIDTH), 1)
        pos = (lax.broadcasted_iota(jnp.int32, (t, POOL_WIDTH), 0) + (i * t + 1)).astype(F32)
        d = win / jnp.minimum(pos, _pool_window(lane)) - u
        db = d.astype(BF16)
        d_ref[...] = db
        ya = _dot(db, w_ref[...]) * sc_ref[...]
        ga = pa_ref[:, POOL_WIDTH:2 * POOL_WIDTH]
        ma_ref[...] = (ya * (ga * _sig(ga))).astype(BF16)
        ext_ref[0:POOL_HALO, :] = ext_ref[t:ext, :]

    return pl.pallas_call(
        body, name="pool_fwd", grid=(n,),
        out_shape=(jax.ShapeDtypeStruct((s, POOL_WIDTH), BF16), jax.ShapeDtypeStruct((s, POOL_WIDTH), BF16)),
        in_specs=[_rows(t, 512), _full((POOL_WIDTH, POOL_WIDTH)), _full((1, POOL_WIDTH))],
        out_specs=(_rows(t, POOL_WIDTH), _rows(t, POOL_WIDTH)),
        scratch_shapes=[pltpu.VMEM((ext, POOL_WIDTH), F32)],
        compiler_params=_params(),
    )(pa, wbd, pscale)


def _mem_softmax(qm, kp):
    sc = _dot(qm, kp, NT)
    e = jnp.exp(sc - jnp.max(sc, axis=-1, keepdims=True))
    return e * (1.0 / jnp.sum(e, axis=-1, keepdims=True))


def _mem_attn_fwd(pm, kmn, vmb, mq_g):
    s = pm.shape[0]
    t = TILE
    n = s // t

    def body(pm_ref, k_ref, v_ref, g_ref, mm_ref):
        lo = _lane_lo((t, LANES))
        for p in range(MEM_WIDTH // LANES):
            sl = slice(p * LANES, (p + 1) * LANES)
            qb = pm_ref[:, sl]
            qs = (((qb * _head_rms(qb, lo)) * g_ref[:, sl]) * ATT_SCALE).astype(BF16)
            kp = k_ref[:, sl]
            vp = v_ref[:, sl]
            outs = []
            for hh in range(2):
                msk = lo if hh == 0 else jnp.logical_not(lo)
                prob = _mem_softmax(jnp.where(msk, qs, jnp.zeros_like(qs)), kp)
                outs.append(_dot(prob.astype(BF16), vp))
            o = jnp.where(lo, outs[0], outs[1])
            gm = pm_ref[:, MEM_WIDTH + p * LANES:MEM_WIDTH + (p + 1) * LANES]
            mm_ref[:, sl] = (o * (gm * _sig(gm))).astype(BF16)

    return pl.pallas_call(
        body, name="mem_attn_fwd", grid=(n,),
        out_shape=jax.ShapeDtypeStruct((s, MEM_WIDTH), BF16),
        in_specs=[_rows(t, 512), _full((N_MEM, MEM_WIDTH)), _full((N_MEM, MEM_WIDTH)), _full((1, MEM_WIDTH))],
        out_specs=_rows(t, MEM_WIDTH),
        compiler_params=_params(),
    )(pm, kmn, vmb, mq_g)


def _fox_fwd(qs, kn, v, ft, fcol, gb):
    s = qs.shape[0]
    t = TILE
    n = s // t

    def body(qs_ref, kn_ref, v_ref, ft_ref, fq_ref, gb_ref, o_ref, mb_ref, rcol_ref):
        i = pl.program_id(0)
        lane = lax.broadcasted_iota(jnp.int32, (t, LANES), 1)
        lo = lane < HEAD_DIM
        lane1 = lax.broadcasted_iota(jnp.int32, (1, LANES), 1)
        causal = lax.broadcasted_iota(jnp.int32, (t, t), 1) <= lax.broadcasted_iota(jnp.int32, (t, t), 0)
        frow0 = fq_ref[0:1, :]
        rcol = jnp.zeros((t, LANES), F32)
        for p in range(FOX_WIDTH // LANES):
            sl = slice(p * LANES, (p + 1) * LANES)
            q2 = qs_ref[:, sl]
            outs = []
            for hh in range(2):
                h = 2 * p + hh
                msk = lo if hh == 0 else jnp.logical_not(lo)
                qm = jnp.where(msk, q2, jnp.zeros_like(q2))
                fref = _lane_pick(frow0, lane1, h)

                def step(j, carry, masked, qm=qm, fref=fref, h=h, sl=sl):
                    m, l, acc = carry
                    rows = pl.ds(pl.multiple_of(j * t, t), t)
                    sc = _dot(qm, kn_ref[rows, sl], NT) - (ft_ref[j, h:h + 1, :] - fref)
                    if masked:
                        sc = jnp.where(causal, sc, -1e30)
                    m_new = jnp.maximum(m, jnp.max(sc, axis=-1, keepdims=True))
                    alpha = jnp.exp(m - m_new)
                    e = jnp.exp(sc - m_new)
                    l = alpha * l + jnp.sum(e, axis=-1, keepdims=True)
                    acc = alpha * acc + _dot(e.astype(BF16), v_ref[rows, sl])
                    return m_new, l, acc

                init = (jnp.full((t, 1), -1e30, F32), jnp.zeros((t, 1), F32), jnp.zeros((t, LANES), F32))
                carry = lax.fori_loop(0, i, functools.partial(step, masked=False), init)
                m, l, acc = step(i, carry, masked=True)
                outs.append(acc * (1.0 / l))
                rcol = jnp.where(lane == h, m + jnp.log(l) - fref, rcol)
            o = jnp.where(lo, outs[0], outs[1])
            o_ref[:, sl] = o
            g = gb_ref[:, sl]
            mb_ref[:, sl] = (o * (g * _sig(g))).astype(BF16)
        rcol_ref[...] = rcol

    return pl.pallas_call(
        body, name="fox_fwd", grid=(n,),
        out_shape=(jax.ShapeDtypeStruct((s, FOX_WIDTH), F32), jax.ShapeDtypeStruct((s, FOX_WIDTH), BF16),
                   jax.ShapeDtypeStruct((s, LANES), F32)),
        in_specs=[_rows(t, FOX_WIDTH), _full((s, FOX_WIDTH)), _full((s, FOX_WIDTH)), _full((n, 8, t)),
                  _rows(t, LANES), _rows(t, FOX_WIDTH)],
        out_specs=(_rows(t, FOX_WIDTH), _rows(t, FOX_WIDTH), _rows(t, LANES)),
        compiler_params=_params(),
    )(qs, kn, v, ft, fcol, gb)


def _out_loss(x, tgt, ma, mb, mm, wout):
    s = x.shape[0]
    t = TILE
    n = s // t

    def body(x_ref, t_ref, ma_ref, mb_ref, mm_ref, w_ref, dy_ref, dma_ref, dmb_ref, dmm_ref, dw_ref, loss_ref, mix_ref):
        @pl.when(pl.program_id(0) == 0)
        def _():
            dw_ref[...] = jnp.zeros_like(dw_ref)
            loss_ref[...] = jnp.zeros_like(loss_ref)

        mix_ref[:, 0:256] = ma_ref[...]
        mix_ref[:, 256:768] = mb_ref[...]
        mix_ref[:, 768:1024] = mm_ref[...]
        mix = mix_ref[...]
        err = (x_ref[...] + _dot(mix, w_ref[...])) - t_ref[...]
        row_mean = jnp.sum(err * err, axis=-1, keepdims=True) * (1.0 / D_MODEL)
        loss_ref[...] += 0.5 * jnp.sum(row_mean, axis=0, keepdims=True)
        dy = err * (1.0 / D_MODEL)
        dy_ref[...] = dy
        dyb = dy.astype(BF16)
        dmix = _dot(dyb, w_ref[...], NT)
        dma_ref[...] = dmix[:, 0:256]
        dmb_ref[...] = dmix[:, 256:768]
        dmm_ref[...] = dmix[:, 768:1024]
        dw_ref[...] += _dot(mix, dyb, TN)

    return pl.pallas_call(
        body, name="out_loss", grid=(n,),
        out_shape=(jax.ShapeDtypeStruct((s, D_MODEL), F32), jax.ShapeDtypeStruct((s, 256), F32),
                   jax.ShapeDtypeStruct((s, 512), F32), jax.ShapeDtypeStruct((s, 256), F32),
                   jax.ShapeDtypeStruct((D_MODEL, D_MODEL), F32), jax.ShapeDtypeStruct((1, LANES), F32)),
        in_specs=[_rows(t, D_MODEL), _rows(t, D_MODEL), _rows(t, 256), _rows(t, 512), _rows(t, 256),
                  _full((D_MODEL, D_MODEL))],
        out_specs=(_rows(t, D_MODEL), _rows(t, 256), _rows(t, 512), _rows(t, 256), _full((D_MODEL, D_MODEL)),
                   _full((1, LANES))),
        scratch_shapes=[pltpu.VMEM((t, D_MODEL), BF16)],
        compiler_params=_params(),
    )(x, tgt, ma, mb, mm, wout)


def _mem_attn_bwd(pm, dmm, kmn, vmb, mq_g):
    s = pm.shape[0]
    t = TILE
    n = s // t

    def body(pm_ref, dmm_ref, k_ref, v_ref, g_ref, dpm_ref, dk_ref, dv_ref, dg_ref, gacc_ref):
        @pl.when(pl.program_id(0) == 0)
        def _():
            dk_ref[...] = jnp.zeros_like(dk_ref)
            dv_ref[...] = jnp.zeros_like(dv_ref)
            gacc_ref[...] = jnp.zeros_like(gacc_ref)

        lo = _lane_lo((t, LANES))
        for p in range(MEM_WIDTH // LANES):
            sl = slice(p * LANES, (p + 1) * LANES)
            qb = pm_ref[:, sl]
            rr = _head_rms(qb, lo)
            qhat = qb * rr
            g = g_ref[:, sl]
            qs = ((qhat * g) * ATT_SCALE).astype(BF16)
            gm = pm_ref[:, MEM_WIDTH + p * LANES:MEM_WIDTH + (p + 1) * LANES]
            sg = _sig(gm)
            dmo = dmm_ref[:, sl]
            d_o = dmo * (gm * sg)
            kp = k_ref[:, sl]
            vp = v_ref[:, sl]
            outs, dqs = [], []
            for hh in range(2):
                msk = lo if hh == 0 else jnp.logical_not(lo)
                qm = jnp.where(msk, qs, jnp.zeros_like(qs))
                prob = _mem_softmax(qm, kp)
                pb = prob.astype(BF16)
                outs.append(_dot(pb, vp))
                dom = jnp.where(msk, d_o, 0.0).astype(BF16)
                dp = _dot(dom, vp, NT)
                ds = (prob * (dp - jnp.sum(prob * dp, axis=-1, keepdims=True))).astype(BF16)
                dqs.append(_dot(ds, kp))
                dk_ref[:, sl] += _dot(ds, qm, TN)
                dv_ref[:, sl] += _dot(pb, dom, TN)
            o = jnp.where(lo, outs[0], outs[1])
            dqn = jnp.where(lo, dqs[0], dqs[1]) * ATT_SCALE
            dpm_ref[:, sl] = _head_norm_bwd(dqn, qhat, rr, g, lo).astype(BF16)
            dpm_ref[:, MEM_WIDTH + p * LANES:MEM_WIDTH + (p + 1) * LANES] = (
                dmo * o * (sg * (1.0 + gm * (1.0 - sg)))).astype(BF16)
            gacc_ref[:, sl] += jnp.sum(dqn * qhat, axis=0, keepdims=True)

        @pl.when(pl.program_id(0) == n - 1)
        def _():
            dg_ref[...] = _fold_heads(gacc_ref[...])

    return pl.pallas_call(
        body, name="mem_attn_bwd", grid=(n,),
        out_shape=(jax.ShapeDtypeStruct((s, 512), BF16), jax.ShapeDtypeStruct((N_MEM, MEM_WIDTH), F32),
                   jax.ShapeDtypeStruct((N_MEM, MEM_WIDTH), F32), jax.ShapeDtypeStruct((1, LANES), F32)),
        in_specs=[_rows(t, 512), _rows(t, MEM_WIDTH), _full((N_MEM, MEM_WIDTH)), _full((N_MEM, MEM_WIDTH)),
                  _full((1, MEM_WIDTH))],
        out_specs=(_rows(t, 512), _full((N_MEM, MEM_WIDTH)), _full((N_MEM, MEM_WIDTH)), _full((1, LANES))),
        scratch_shapes=[pltpu.VMEM((1, MEM_WIDTH), F32)],
        compiler_params=_params(),
    )(pm, dmm, kmn, vmb, mq_g)


def _mem_bwd(dkn, dvm, kv, mnb, mem, w_kv, mk_g, mem_norm_g):
    n = mem.shape[0]

    def body(dkn_ref, dvm_ref, kv_ref, mn_ref, mem_ref, w_ref, kg_ref, g_ref, dw_ref, dg_ref, dkg_ref, dkv_ref):
        lo = _lane_lo((n, LANES))
        gacc = []
        for p in range(MEM_WIDTH // LANES):
            sl = slice(p * LANES, (p + 1) * LANES)
            kb = kv_ref[:, sl]
            rr = _head_rms(kb, lo)
            khat = kb * rr
            dk = dkn_ref[:, sl]
            dkv_ref[:, sl] = _head_norm_bwd(dk, khat, rr, kg_ref[:, sl], lo).astype(BF16)
            gacc.append(jnp.sum(dk * khat, axis=0, keepdims=True))
        dkg_ref[...] = _fold_heads(jnp.concatenate(gacc, axis=1))
        dkv_ref[:, MEM_WIDTH:] = dvm_ref[...].astype(BF16)
        dkv = dkv_ref[...]
        dw_ref[...] = _dot(mn_ref[...], dkv, TN)
        dmn = _dot(dkv, w_ref[...], NT)
        xm = mem_ref[...]
        rr = lax.rsqrt(jnp.mean(xm * xm, axis=-1, keepdims=True) + EPS)
        dg_ref[...] = jnp.sum(dmn * (xm * rr), axis=0, keepdims=True)

    return pl.pallas_call(
        body, name="mem_bwd",
        out_shape=(jax.ShapeDtypeStruct((D_MODEL, 2 * MEM_WIDTH), F32), jax.ShapeDtypeStruct((1, D_MODEL), F32),
                   jax.ShapeDtypeStruct((1, LANES), F32)),
        scratch_shapes=[pltpu.VMEM((n, 2 * MEM_WIDTH), BF16)],
        compiler_params=pltpu.CompilerParams(vmem_limit_bytes=VMEM_LIMIT),
    )(dkn, dvm, kv, mnb, mem, w_kv, mk_g, mem_norm_g)


def _pool_bwd(pa, db, dma, wbd, pscale):
    s = pa.shape[0]
    t = TILE
    n = s // t
    ext = t + POOL_HALO

    def body(pa_ref, d_ref, dma_ref, w_ref, sc_ref, dpa_ref, dw_ref, dsc_ref, ext_ref):
        i = pl.program_id(0)

        @pl.when(i == 0)
        def _():
            dw_ref[...] = jnp.zeros_like(dw_ref)
            dsc_ref[...] = jnp.zeros_like(dsc_ref)
            ext_ref[t:ext, :] = jnp.zeros((POOL_HALO, POOL_WIDTH), F32)

        dbv = d_ref[...]
        z = _dot(dbv, w_ref[...])
        ga = pa_ref[:, POOL_WIDTH:2 * POOL_WIDTH]
        sg = _sig(ga)
        dma_v = dma_ref[...]
        dya = dma_v * (ga * sg)
        dpa_ref[:, POOL_WIDTH:2 * POOL_WIDTH] = (dma_v * (z * sc_ref[...]) * (sg * (1.0 + ga * (1.0 - sg)))).astype(BF16)
        dsc_ref[...] += jnp.sum(dya * z, axis=0, keepdims=True)
        dzb = (dya * sc_ref[...]).astype(BF16)
        dw_ref[...] += _dot(dbv, dzb, TN)
        dd = _dot(dzb, w_ref[...], NT)
        lane = lax.broadcasted_iota(jnp.int32, (t, POOL_WIDTH), 1)
        pos = (lax.broadcasted_iota(jnp.int32, (t, POOL_WIDTH), 0) + ((n - 1 - i) * t + 1)).astype(F32)
        ext_ref[0:t, :] = dd / jnp.minimum(pos, _pool_window(lane))
        e = ext_ref[...]
        s2 = e + pltpu.roll(e, ext - 1, axis=0)
        s4 = s2 + pltpu.roll(s2, ext - 2, axis=0)
        s8 = s4 + pltpu.roll(s4, ext - 4, axis=0)
        s16 = s8 + pltpu.roll(s8, ext - 8, axis=0)
        lane_e = lax.broadcasted_iota(jnp.int32, (ext, POOL_WIDTH), 1)
        win = _pool_pick(lane_e, s2, s4, s8, s16)[0:t, :]
        dpa_ref[:, 0:POOL_WIDTH] = (win - dd).astype(BF16)
        ext_ref[t:ext, :] = ext_ref[0:POOL_HALO, :]

    return pl.pallas_call(
        body, name="pool_bwd", grid=(n,),
        out_shape=(jax.ShapeDtypeStruct((s, 512), BF16), jax.ShapeDtypeStruct((POOL_WIDTH, POOL_WIDTH), F32),
                   jax.ShapeDtypeStruct((1, POOL_WIDTH), F32)),
        in_specs=[_rows_rev(t, 512, n), _rows_rev(t, POOL_WIDTH, n), _rows_rev(t, POOL_WIDTH, n),
                  _full((POOL_WIDTH, POOL_WIDTH)), _full((1, POOL_WIDTH))],
        out_specs=(_rows_rev(t, 512, n), _full((POOL_WIDTH, POOL_WIDTH)), _full((1, POOL_WIDTH))),
        scratch_shapes=[pltpu.VMEM((ext, POOL_WIDTH), F32)],
        compiler_params=_params(),
    )(pa, db, dma, wbd, pscale)


def _fox_prep(dmb, gb, o, rcol):
    s = dmb.shape[0]
    t = TILE
    n = s // t

    def body(dmb_ref, gb_ref, o_ref, r_ref, do_ref, dgb_ref, rd_ref):
        lane = lax.broadcasted_iota(jnp.int32, (t, LANES), 1)
        lo = lane < HEAD_DIM
        col = r_ref[...]
        for p in range(FOX_WIDTH // LANES):
            sl = slice(p * LANES, (p + 1) * LANES)
            g = gb_ref[:, sl]
            sg = _sig(g)
            dm = dmb_ref[:, sl]
            ov = o_ref[:, sl]
            d_o = dm * (g * sg)
            do_ref[:, sl] = d_o.astype(BF16)
            dgb_ref[:, sl] = (dm * ov * (sg * (1.0 + g * (1.0 - sg)))).astype(BF16)
            prod = d_o * ov
            col = jnp.where(lane == 8 + 2 * p, jnp.sum(jnp.where(lo, prod, 0.0), axis=-1, keepdims=True), col)
            col = jnp.where(lane == 9 + 2 * p, jnp.sum(jnp.where(lo, 0.0, prod), axis=-1, keepdims=True), col)
        rd_ref[0] = col.T[0:16, :]

    return pl.pallas_call(
        body, name="fox_prep", grid=(n,),
        out_shape=(jax.ShapeDtypeStruct((s, FOX_WIDTH), BF16), jax.ShapeDtypeStruct((s, FOX_WIDTH), BF16),
                   jax.ShapeDtypeStruct((n, 16, t), F32)),
        in_specs=[_rows(t, FOX_WIDTH), _rows(t, FOX_WIDTH), _rows(t, FOX_WIDTH), _rows(t, LANES)],
        out_specs=(_rows(t, FOX_WIDTH), _rows(t, FOX_WIDTH), pl.BlockSpec((1, 16, t), lambda i: (i, 0, 0))),
        compiler_params=_params(),
    )(dmb, gb, o, rcol)


def _fox_bwd(kn, v, fcol, qs, dob, rd):
    s = kn.shape[0]
    t = TILE
    n = s // t

    def body(kn_ref, v_ref, fc_ref, qs_ref, do_ref, rd_ref, dkn_ref, dv_ref, dfc_ref, dqs_ref, drs_ref):
        j = pl.program_id(0)

        @pl.when(j == 0)
        def _():
            dqs_ref[...] = jnp.zeros_like(dqs_ref)
            drs_ref[...] = jnp.zeros_like(drs_ref)

        lane = lax.broadcasted_iota(jnp.int32, (t, LANES), 1)
        lo = lane < HEAD_DIM
        lane1 = lax.broadcasted_iota(jnp.int32, (1, LANES), 1)
        causal = lax.broadcasted_iota(jnp.int32, (t, t), 0) <= lax.broadcasted_iota(jnp.int32, (t, t), 1)
        fc = fc_ref[...]
        frow0 = fc_ref[0:1, :]
        dfc = jnp.zeros((t, LANES), F32)
        for p in range(FOX_WIDTH // LANES):
            sl = slice(p * LANES, (p + 1) * LANES)
            k2 = kn_ref[:, sl]
            v2 = v_ref[:, sl]
            dks = []
            dv_pair = jnp.zeros((t, LANES), F32)
            for hh in range(2):
                h = 2 * p + hh
                msk = lo if hh == 0 else jnp.logical_not(lo)
                km = jnp.where(msk, k2, jnp.zeros_like(k2))
                vm = jnp.where(msk, v2, jnp.zeros_like(v2))
                fr = _lane_pick(frow0, lane1, h)
                gk = _lane_pick(fc, lane, h) - fr

                def step(i, carry, masked, km=km, vm=vm, fr=fr, gk=gk, h=h, sl=sl, msk=msk):
                    dk_a, dv_a, cs = carry
                    rows = pl.ds(pl.multiple_of(i * t, t), t)
                    qb = qs_ref[rows, sl]
                    d_o = do_ref[rows, sl]
                    arg = (_dot(km, qb, NT) - gk) - (rd_ref[i, h:h + 1, :] + fr)
                    if masked:
                        arg = jnp.where(causal, arg, -1e30)
                    pt = jnp.exp(arg)
                    dst32 = pt * (_dot(vm, d_o, NT) - rd_ref[i, 8 + h:9 + h, :])
                    cs = cs + jnp.sum(dst32, axis=-1, keepdims=True)
                    drs_ref[i, h:h + 1, :] += jnp.sum(dst32, axis=0, keepdims=True)
                    dst = dst32.astype(BF16)
                    dv_a = dv_a + _dot(pt.astype(BF16), jnp.where(msk, d_o, jnp.zeros_like(d_o)))
                    dk_a = dk_a + _dot(dst, jnp.where(msk, qb, jnp.zeros_like(qb)))
                    dqs_ref[rows, sl] += _dot(dst, km, TN)
                    return dk_a, dv_a, cs

                zero = jnp.zeros((t, LANES), F32)
                carry = step(j, (zero, zero, jnp.zeros((t, 1), F32)), masked=True)
                dk_a, dv_a, cs = lax.fori_loop(j + 1, n, functools.partial(step, masked=False), carry)
                dks.append(dk_a)
                dv_pair = dv_pair + dv_a
                dfc = jnp.where(lane == h, cs, dfc)
            dkn_ref[:, sl] = jnp.where(lo, dks[0], dks[1])
            dv_ref[:, sl] = dv_pair.astype(BF16)
        dfc_ref[...] = dfc

    return pl.pallas_call(
        body, name="fox_bwd", grid=(n,),
        out_shape=(jax.ShapeDtypeStruct((s, FOX_WIDTH), F32), jax.ShapeDtypeStruct((s, FOX_WIDTH), BF16),
                   jax.ShapeDtypeStruct((s, LANES), F32), jax.ShapeDtypeStruct((s, FOX_WIDTH), F32),
                   jax.ShapeDtypeStruct((n, 8, t), F32)),
        in_specs=[_rows(t, FOX_WIDTH), _rows(t, FOX_WIDTH), _rows(t, LANES), _full((s, FOX_WIDTH)),
                  _full((s, FOX_WIDTH)), _full((n, 16, t))],
        out_specs=(_rows(t, FOX_WIDTH), _rows(t, FOX_WIDTH), _rows(t, LANES), _full((s, FOX_WIDTH)),
                   _full((n, 8, t))),
        compiler_params=_params(),
    )(kn, v, fcol, qs, dob, rd)


def _fox_post(dqs, dkn, dfc, drs, qk, fb, bf_pad, fq_g, fk_g):
    s = dqs.shape[0]
    t = TILE
    n = s // t

    def body(dqs_ref, dkn_ref, dfc_ref, drs_ref, qk_ref, fb_ref, bf_ref, qg_ref, kg_ref,
             dqk_ref, dfb_ref, dqg_ref, dkg_ref, dbf_ref, qacc_ref, kacc_ref, carry_ref):
        i = pl.program_id(0)

        @pl.when(i == 0)
        def _():
            qacc_ref[...] = jnp.zeros_like(qacc_ref)
            kacc_ref[...] = jnp.zeros_like(kacc_ref)
            dbf_ref[...] = jnp.zeros_like(dbf_ref)
            carry_ref[...] = jnp.zeros_like(carry_ref)

        lo = _lane_lo((t, LANES))
        for off, d_ref, g_ref, acc_ref, scale in ((0, dqs_ref, qg_ref, qacc_ref, ATT_SCALE),
                                                  (FOX_WIDTH, dkn_ref, kg_ref, kacc_ref, 1.0)):
            for p in range(FOX_WIDTH // LANES):
                sl = slice(p * LANES, (p + 1) * LANES)
                raw = qk_ref[:, off + p * LANES:off + (p + 1) * LANES]
                rr = _head_rms(raw, lo)
                xhat = raw * rr
                dn = d_ref[:, sl] * scale
                dqk_ref[:, off + p * LANES:off + (p + 1) * LANES] = _head_norm_bwd(
                    dn, xhat, rr, g_ref[:, sl], lo).astype(BF16)
                acc_ref[:, sl] += jnp.sum(dn * xhat, axis=0, keepdims=True)

        lane = lax.broadcasted_iota(jnp.int32, (t, LANES), 1)
        row = lax.broadcasted_iota(jnp.int32, (t, LANES), 0)
        rows_h = jnp.concatenate([drs_ref[0], jnp.zeros((LANES - FOX_HEADS, t), F32)], axis=0)
        acc = rows_h.T - dfc_ref[...]
        sh = 1
        while sh < t:
            acc = acc + jnp.where(row < t - sh, pltpu.roll(acc, t - sh, axis=0), 0.0)
            sh *= 2
        dlogf = acc + carry_ref[...]
        dfb_ref[...] = dlogf
        carry_ref[...] = dfb_ref[0:1, :]
        z = fb_ref[...] + bf_ref[...]
        dz = jnp.where(lane < FOX_HEADS, dlogf * (1.0 / (1.0 + jnp.exp(z))), 0.0)
        dfb_ref[...] = dz
        dbf_ref[...] += jnp.sum(dz, axis=0, keepdims=True)

        @pl.when(i == n - 1)
        def _():
            dqg_ref[...] = _fold_heads(qacc_ref[...])
            dkg_ref[...] = _fold_heads(kacc_ref[...])

    return pl.pallas_call(
        body, name="fox_post", grid=(n,),
        out_shape=(jax.ShapeDtypeStruct((s, 2 * FOX_WIDTH), BF16), jax.ShapeDtypeStruct((s, LANES), F32),
                   jax.ShapeDtypeStruct((1, LANES), F32), jax.ShapeDtypeStruct((1, LANES), F32),
                   jax.ShapeDtypeStruct((1, LANES), F32)),
        in_specs=[_rows_rev(t, FOX_WIDTH, n), _rows_rev(t, FOX_WIDTH, n), _rows_rev(t, LANES, n),
                  pl.BlockSpec((1, FOX_HEADS, t), lambda i: (n - 1 - i, 0, 0)),
                  _rows_rev(t, 2 * FOX_WIDTH, n), _rows_rev(t, LANES, n), _full((1, LANES)),
                  _full((1, FOX_WIDTH)), _full((1, FOX_WIDTH))],
        out_specs=(_rows_rev(t, 2 * FOX_WIDTH, n), _rows_rev(t, LANES, n), _full((1, LANES)), _full((1, LANES)),
                   _full((1, LANES))),
        scratch_shapes=[pltpu.VMEM((1, FOX_WIDTH), F32), pltpu.VMEM((1, FOX_WIDTH), F32), pltpu.VMEM((1, LANES), F32)],
        compiler_params=_params(),
    )(dqs, dkn, dfc, drs, qk, fb, bf_pad, fq_g, fk_g)


def _assemble_dproj(dp_ref, dpa_ref, dqk_ref, dv_ref, dgb_ref, dpm_ref, dfb_ref):
    dp_ref[:, PA_LO:QB_LO] = dpa_ref[...]
    dp_ref[:, QB_LO:VB_LO] = dqk_ref[...]
    dp_ref[:, VB_LO:GB_LO] = dv_ref[...]
    dp_ref[:, GB_LO:PM_LO] = dgb_ref[...]
    dp_ref[:, PM_LO:FB_LO] = dpm_ref[...]
    dp_ref[:, FB_LO:PROJ_PAD] = dfb_ref[...].astype(BF16)


def _dproj_specs(t):
    return [_rows(t, 512), _rows(t, 2 * FOX_WIDTH), _rows(t, FOX_WIDTH), _rows(t, FOX_WIDTH), _rows(t, 512),
            _rows(t, LANES)]


def _in_bwd_x(x, dy, norm_g, wp, dparts):
    s = x.shape[0]
    t = TILE
    n = s // t

    def body(x_ref, dy_ref, g_ref, wp_ref, dpa_ref, dqk_ref, dv_ref, dgb_ref, dpm_ref, dfb_ref, gx_ref, dg_ref, dp_ref):
        @pl.when(pl.program_id(0) == 0)
        def _():
            dg_ref[...] = jnp.zeros_like(dg_ref)

        _assemble_dproj(dp_ref, dpa_ref, dqk_ref, dv_ref, dgb_ref, dpm_ref, dfb_ref)
        dh = _dot(dp_ref[...], wp_ref[...], NT)
        xv = x_ref[...]
        rr = lax.rsqrt(jnp.mean(xv * xv, axis=-1, keepdims=True) + EPS)
        xhat = xv * rr
        a = dh * g_ref[...]
        gx_ref[...] = dy_ref[...] + rr * (a - xhat * jnp.mean(xhat * a, axis=-1, keepdims=True))
        dg_ref[...] += jnp.sum(dh * xhat, axis=0, keepdims=True)

    return pl.pallas_call(
        body, name="in_bwd_x", grid=(n,),
        out_shape=(jax.ShapeDtypeStruct((s, D_MODEL), F32), jax.ShapeDtypeStruct((1, D_MODEL), F32)),
        in_specs=[_rows(t, D_MODEL), _rows(t, D_MODEL), _full((1, D_MODEL)), _full((D_MODEL, PROJ_PAD))] + _dproj_specs(t),
        out_specs=(_rows(t, D_MODEL), _full((1, D_MODEL))),
        scratch_shapes=[pltpu.VMEM((t, PROJ_PAD), BF16)],
        compiler_params=_params(),
    )(x, dy, norm_g, wp, *dparts)


def _in_bwd_w(hb, dparts):
    s = hb.shape[0]
    t = TILE
    n = s // t

    def body(h_ref, dpa_ref, dqk_ref, dv_ref, dgb_ref, dpm_ref, dfb_ref, dw_ref, dp_ref):
        @pl.when(pl.program_id(0) == 0)
        def _():
            dw_ref[...] = jnp.zeros_like(dw_ref)

        _assemble_dproj(dp_ref, dpa_ref, dqk_ref, dv_ref, dgb_ref, dpm_ref, dfb_ref)
        dw_ref[...] += _dot(h_ref[...], dp_ref[...], TN)

    return pl.pallas_call(
        body, name="in_bwd_w", grid=(n,),
        out_shape=jax.ShapeDtypeStruct((D_MODEL, PROJ_PAD), F32),
        in_specs=[_rows(t, D_MODEL)] + _dproj_specs(t),
        out_specs=_full((D_MODEL, PROJ_PAD)),
        scratch_shapes=[pltpu.VMEM((t, PROJ_PAD), BF16)],
        compiler_params=_params(),
    )(hb, *dparts)


def _adamw(name, w, g, m, v):
    rows, cols = w.shape
    t = 256 if rows > 256 and rows % 256 == 0 else rows
    n = rows // t

    def body(w_ref, g_ref, m_ref, v_ref, d_ref, nm_ref, nv_ref):
        gv = g_ref[...]
        nm = ADAM_B1 * m_ref[...] + (1.0 - ADAM_B1) * gv
        nv = ADAM_B2 * v_ref[...] + (1.0 - ADAM_B2) * (gv * gv)
        m_hat = nm / (1.0 - ADAM_B1 ** ADAM_STEP)
        v_hat = nv / (1.0 - ADAM_B2 ** ADAM_STEP)
        d_ref[...] = -ADAM_LR * (m_hat / (jnp.sqrt(v_hat) + ADAM_EPS) + ADAM_WD * w_ref[...])
        nm_ref[...] = nm
        nv_ref[...] = nv

    spec = _rows(t, cols)
    return pl.pallas_call(
        body, name=name, grid=(n,),
        out_shape=(jax.ShapeDtypeStruct((rows, cols), F32),) * 3,
        in_specs=[spec] * 4, out_specs=(spec,) * 3,
        compiler_params=_params(),
    )(w, g, m, v)


def _pack_chip(w_in_s, w_kv_s, w_out_s):
    return jnp.concatenate([
        w_in_s.reshape(PACK_IN, PACK_W), w_kv_s.reshape(PACK_KV, PACK_W), w_out_s,
        jnp.zeros((PACK_ROWS - PACK_IN - PACK_KV - PACK_OUT, PACK_W), w_in_s.dtype)], axis=0)


def _unpack_chips(packs):
    w_in = packs[:, 0:PACK_IN].reshape(4, D_MODEL, IN_WIDTH // 4).transpose(1, 0, 2).reshape(D_MODEL, IN_WIDTH)
    w_kv = packs[:, PACK_IN:PACK_IN + PACK_KV].reshape(D_MODEL, 2 * MEM_WIDTH)
    w_out = packs[:, PACK_IN + PACK_KV:PACK_IN + PACK_KV + PACK_OUT].reshape(D_MODEL, D_MODEL)
    return w_in, w_kv, w_out


def _pad_proj_cols(w_in):
    return jnp.concatenate([
        w_in[:, 0:F_ORIG_LO], w_in[:, F_ORIG_LO + FOX_HEADS:], w_in[:, F_ORIG_LO:F_ORIG_LO + FOX_HEADS],
        jnp.zeros((w_in.shape[0], PROJ_PAD - IN_WIDTH), w_in.dtype)], axis=1)


def _unpad_proj_cols(wp):
    return jnp.concatenate([wp[:, 0:F_ORIG_LO], wp[:, FB_LO:FB_LO + FOX_HEADS], wp[:, F_ORIG_LO:FB_LO]], axis=1)


def _tile_heads(g, n):
    return jnp.tile(g.reshape(1, HEAD_DIM), (1, n))


def _block_diag(w_pool):
    wbd = jnp.zeros((POOL_WIDTH, POOL_WIDTH), w_pool.dtype)
    for g in range(4):
        wbd = wbd.at[g * 64:(g + 1) * 64, g * 64:(g + 1) * 64].set(w_pool[g])
    return wbd


def _small_rows(a):
    flat = a.reshape(-1)
    pad = (-flat.shape[0]) % LANES
    return jnp.pad(flat, (0, pad)).reshape(-1, LANES)


def _pack_small(norm_g, mem_norm_g, pool_scale, b_f, fq, fk, mq, mk, w_pool, loss_row):
    return jnp.concatenate([_small_rows(a) for a in (norm_g, mem_norm_g, pool_scale, b_f, fq, fk, mq, mk, w_pool)]
                           + [loss_row], axis=0)


def _unpack_small(p):
    return (p[SM_NORM:SM_MEMNORM].reshape(1, D_MODEL), p[SM_MEMNORM:SM_PSCALE].reshape(1, D_MODEL),
            p[SM_PSCALE:SM_BF].reshape(1, POOL_WIDTH), p[SM_BF, 0:FOX_HEADS].reshape(1, FOX_HEADS),
            p[SM_FQ, 0:HEAD_DIM].reshape(1, HEAD_DIM), p[SM_FK, 0:HEAD_DIM].reshape(1, HEAD_DIM),
            p[SM_MQ, 0:HEAD_DIM].reshape(1, HEAD_DIM), p[SM_MK, 0:HEAD_DIM].reshape(1, HEAD_DIM),
            p[SM_WPOOL:SM_LOSS].reshape(1, 4, 64, 64))


def kernel(x, mem, norm_g, w_in, b_f, w_pool, pool_scale, fox_q_g, fox_k_g, mem_norm_g, w_mem_kv, mem_q_g, mem_k_g, w_out, loss_target, m_norm_g, m_w_in, m_b_f, m_w_pool, m_pool_scale, m_fox_q_g, m_fox_k_g, m_mem_norm_g, m_w_mem_kv, m_mem_q_g, m_mem_k_g, m_w_out, v_norm_g, v_w_in, v_b_f, v_w_pool, v_pool_scale, v_fox_q_g, v_fox_k_g, v_mem_norm_g, v_w_mem_kv, v_mem_q_g, v_mem_k_g, v_w_out):
    packs = _all_gather_halves(_pack_chip(w_in[0], w_mem_kv[0], w_out[0]).astype(BF16))
    w_in_b, w_kv_b, w_out_b = _unpack_chips(packs.reshape(4, PACK_ROWS, PACK_W))
    grad_x, dw_in, dw_kv, dw_out, small = _local_grads(
        x[0], mem[0], loss_target[0], w_in_b, w_kv_b, w_out_b, norm_g, b_f, w_pool, pool_scale, fox_q_g, fox_k_g,
        mem_norm_g, mem_q_g, mem_k_g)

    dw_in = dw_in.reshape(D_MODEL, 4, IN_WIDTH // 4).transpose(1, 0, 2)
    gpack = jnp.concatenate([
        dw_in.reshape(4, PACK_IN, PACK_W), dw_kv.reshape(4, PACK_KV, PACK_W), dw_out.reshape(4, PACK_OUT, PACK_W),
        jnp.zeros((4, PACK_ROWS - PACK_IN - PACK_KV - PACK_OUT, PACK_W), F32)], axis=1)
    gshard, gsmall = _grad_reduce(gpack, small)

    g_w_in = gshard[0:PACK_IN].reshape(1, D_MODEL, IN_WIDTH // 4)
    g_w_kv = gshard[PACK_IN:PACK_IN + PACK_KV].reshape(1, D_MODEL // 4, 2 * MEM_WIDTH)
    g_w_out = gshard[PACK_IN + PACK_KV:PACK_IN + PACK_KV + PACK_OUT].reshape(1, D_MODEL // 4, D_MODEL)
    (g_norm, g_memnorm, g_pscale, g_bf, g_fq, g_fk, g_mq, g_mk, g_wpool) = _unpack_small(gsmall)
    loss = gsmall[SM_LOSS, 0]

    zero_row = jnp.zeros((1, LANES), F32)
    smalls = [_pack_small(*leaves, zero_row) for leaves in (
        (norm_g, mem_norm_g, pool_scale, b_f, fox_q_g, fox_k_g, mem_q_g, mem_k_g, w_pool),
        (m_norm_g, m_mem_norm_g, m_pool_scale, m_b_f, m_fox_q_g, m_fox_k_g, m_mem_q_g, m_mem_k_g, m_w_pool),
        (v_norm_g, v_mem_norm_g, v_pool_scale, v_b_f, v_fox_q_g, v_fox_k_g, v_mem_q_g, v_mem_k_g, v_w_pool))]
    g_small_pack = _pack_small(g_norm, g_memnorm, g_pscale, g_bf, g_fq, g_fk, g_mq, g_mk, g_wpool, zero_row)
    upd_small = [_unpack_small(a) for a in _adamw("adamw_small", smalls[0], g_small_pack, smalls[1], smalls[2])]
    upd_in = [a[None] for a in _adamw("adamw_w_in", w_in[0], g_w_in[0], m_w_in[0], v_w_in[0])]
    upd_kv = [a[None] for a in _adamw("adamw_w_mem_kv", w_mem_kv[0], g_w_kv[0], m_w_mem_kv[0], v_w_mem_kv[0])]
    upd_out = [a[None] for a in _adamw("adamw_w_out", w_out[0], g_w_out[0], m_w_out[0], v_w_out[0])]

    def leaves(k):
        sm = upd_small[k]
        return (sm[0], upd_in[k], sm[3], sm[8], sm[2], sm[4], sm[5], sm[1], upd_kv[k], sm[6], sm[7], upd_out[k])

    grads = (g_norm, g_w_in, g_bf, g_wpool, g_pscale, g_fq, g_fk, g_memnorm, g_w_kv, g_mq, g_mk, g_w_out)
    return (loss, grad_x[None], *grads, *leaves(0), *leaves(1), *leaves(2))


def _local_grads(xs, mems, tgt, w_in_b, w_kv_b, w_out_b, norm_g, b_f, w_pool, pool_scale, fox_q_g, fox_k_g,
                 mem_norm_g, mem_q_g, mem_k_g):
    wp = _pad_proj_cols(w_in_b)
    wbd = _block_diag(w_pool[0]).astype(BF16)
    bf_pad = jnp.pad(b_f, ((0, 0), (0, LANES - FOX_HEADS)))
    fq_g, fk_g = _tile_heads(fox_q_g, FOX_HEADS), _tile_heads(fox_k_g, FOX_HEADS)
    mq_g, mk_g = _tile_heads(mem_q_g, 4), _tile_heads(mem_k_g, 4)

    mnb, kv, kmn, vmb = _mem_fwd(mems, mem_norm_g, w_kv_b, mk_g)
    hb, pa, qk, qs, kn, vb, gb, pm, fb, fcol, ft = _fwd_in(xs, norm_g, wp, bf_pad, fq_g, fk_g)
    ma, db = _pool_fwd(pa, wbd, pool_scale)
    mm = _mem_attn_fwd(pm, kmn, vmb, mq_g)
    o, mb, rcol = _fox_fwd(qs, kn, vb, ft, fcol, gb)
    dy, dma, dmb, dmm, dw_out, loss_row = _out_loss(xs, tgt, ma, mb, mm, w_out_b)

    dpm, dkmn, dvm, dmq_g = _mem_attn_bwd(pm, dmm, kmn, vmb, mq_g)
    dw_kv, dmemnorm_g, dmk_g = _mem_bwd(dkmn, dvm, kv, mnb, mems, w_kv_b, mk_g, mem_norm_g)
    dpa, dwbd, dpscale = _pool_bwd(pa, db, dma, wbd, pool_scale)
    dob, dgb, rd = _fox_prep(dmb, gb, o, rcol)
    dkn, dvb, dfc, dqs, drs = _fox_bwd(kn, vb, fcol, qs, dob, rd)
    dqk, dfb, dfq_g, dfk_g, dbf = _fox_post(dqs, dkn, dfc, drs, qk, fb, bf_pad, fq_g, fk_g)
    dparts = (dpa, dqk, dvb, dgb, dpm, dfb)
    grad_x, dnorm_g = _in_bwd_x(xs, dy, norm_g, wp, dparts)
    dwp = _in_bwd_w(hb, dparts)

    dw_pool = jnp.stack([dwbd[g * 64:(g + 1) * 64, g * 64:(g + 1) * 64] for g in range(4)])
    small = _pack_small(dnorm_g, dmemnorm_g, dpscale, dbf[:, 0:FOX_HEADS], dfq_g[:, 0:HEAD_DIM], dfk_g[:, 0:HEAD_DIM],
                        dmq_g[:, 0:HEAD_DIM], dmk_g[:, 0:HEAD_DIM], dw_pool, loss_row)
    return grad_x, _unpad_proj_cols(dwp), dw_kv, dw_out, small
```

```python
import functools

import jax
import jax.numpy as jnp
from jax import lax
from jax.experimental import pallas as pl
from jax.experimental.pallas import tpu as pltpu

F32 = jnp.float32
BF16 = jnp.bfloat16
MESH = pl.DeviceIdType.MESH

D_MODEL = 1024
HEAD_DIM = 64
POOL_WIDTH = 256
FOX_WIDTH = 512
FOX_HEADS = 8
MEM_WIDTH = 256
N_MEM = 256
IN_WIDTH = 3080
EPS = 1e-6
ATT_SCALE = 0.125

ADAM_LR = 0.001
ADAM_B1 = 0.9
ADAM_B2 = 0.999
ADAM_EPS = 1e-08
ADAM_WD = 0.01
ADAM_STEP = 10

LANES = 128
PA_LO, QB_LO, KB_LO, VB_LO, GB_LO, PM_LO, FB_LO, PROJ_PAD = 0, 512, 1024, 1536, 2048, 2560, 3072, 3200
F_ORIG_LO = 2048

TILE = 512
VMEM_LIMIT = 56 * 1024 * 1024

VEC_LEAVES = (("norm_g", 0, 1024), ("mem_norm_g", 1, 1024), ("pool_scale", 2, 256), ("b_f", 3, 8),
              ("fox_q_g", 4, 64), ("fox_k_g", 5, 64), ("mem_q_g", 6, 64), ("mem_k_g", 7, 64))
VEC_LOSS_ROW = 8
VEC_ROWS = 16
POOL_ROWS = 256


def _params(n_grid=1, vmem=VMEM_LIMIT):
    return pltpu.CompilerParams(dimension_semantics=("arbitrary",) * n_grid, vmem_limit_bytes=vmem)


def _rows(t, w):
    return pl.BlockSpec((t, w), lambda i: (i, 0))


def _rows_rev(t, w, n):
    return pl.BlockSpec((t, w), lambda i: (n - 1 - i, 0))


def _full(shape):
    return pl.BlockSpec(shape, lambda i: (0,) * len(shape))


def _sig(x):
    return 1.0 / (1.0 + jnp.exp(-x))


def _lane_lo(shape):
    return lax.broadcasted_iota(jnp.int32, shape, 1) < HEAD_DIM


def _pair_sum(v, lo):
    s0 = jnp.sum(jnp.where(lo, v, 0.0), axis=-1, keepdims=True)
    s1 = jnp.sum(jnp.where(lo, 0.0, v), axis=-1, keepdims=True)
    return jnp.where(lo, s0, s1)


def _head_rms(blk, lo):
    return lax.rsqrt(_pair_sum(blk * blk, lo) * (1.0 / HEAD_DIM) + EPS)


def _head_norm_bwd(dyn, xhat, rr, g, lo):
    a = dyn * g
    return rr * (a - xhat * (_pair_sum(xhat * a, lo) * (1.0 / HEAD_DIM)))


def _fold_heads(acc):
    tot = acc[:, 0:LANES]
    for p in range(1, acc.shape[1] // LANES):
        tot = tot + acc[:, p * LANES:(p + 1) * LANES]
    return tot + pltpu.roll(tot, HEAD_DIM, axis=1)


def _lane_pick(v, lane, idx):
    return jnp.sum(jnp.where(lane == idx, v, 0.0), axis=-1, keepdims=True)


NT = (((1,), (1,)), ((), ()))
TN = (((0,), (0,)), ((), ()))


def _dot(a, b, dims=None):
    if dims is None:
        return jnp.dot(a, b, preferred_element_type=F32)
    return lax.dot_general(a, b, dims, preferred_element_type=F32)


def _my_place():
    return lax.axis_index("x"), lax.axis_index("y"), lax.axis_index("c")


def _all_gather_weights(shards):
    n = len(shards)
    dims = [(a.shape[0] // 2, a.shape[1]) for a in shards]

    def body(*refs):
        ins, outs = refs[0:n], refs[n:2 * n]
        f32_bufs, bf_bufs = refs[2 * n:3 * n], refs[3 * n:4 * n]
        send_sems, recv_sems, local_sems = refs[4 * n:]
        x, y, c = _my_place()
        me, sibling = (x, y, c), (x, y, 1 - c)
        chips = [(1 - x, y), (x, 1 - y), (1 - x, 1 - y)]

        loads = []
        for a in range(n):
            h = dims[a][0]
            cp = pltpu.make_async_copy(ins[a].at[pl.ds(pl.multiple_of(c * h, 16), h), :], f32_bufs[a], local_sems.at[a])
            cp.start()
            loads.append(cp)

        def blk(a, px, py, pc):
            return outs[a].at[4 * px + 2 * py + pc]

        def copy(a, k, block, to, src=None):
            return pltpu.make_async_remote_copy(
                src_ref=blk(a, *block) if src is None else src, dst_ref=blk(a, *block),
                send_sem=send_sems.at[7 * a + k], recv_sem=recv_sems.at[7 * a + k], device_id=to, device_id_type=MESH)

        first, keeps = [], []
        for a in range(n):
            loads[a].wait()
            bf_bufs[a][...] = f32_bufs[a][...].astype(BF16)
            keep = pltpu.make_async_copy(bf_bufs[a], blk(a, *me), local_sems.at[n + a])
            keep.start()
            keeps.append(keep)
            mine = [copy(a, 0, me, sibling, src=bf_bufs[a])]
            mine += [copy(a, 1 + j, me, (*chip, c), src=bf_bufs[a]) for j, chip in enumerate(chips)]
            for cp in mine:
                cp.start()
            first += mine
        passed = []
        for a in range(n):
            for j, chip in enumerate(chips):
                copy(a, 1 + j, (*chip, c), me).wait_recv()
                cp = copy(a, 4 + j, (*chip, c), sibling)
                cp.start()
                passed.append(cp)
        for a in range(n):
            copy(a, 0, sibling, me).wait_recv()
            for j, chip in enumerate(chips):
                copy(a, 4 + j, (*chip, 1 - c), me).wait_recv()
        for cp in first + passed:
            cp.wait_send()
        for keep in keeps:
            keep.wait()

    any_spec = pl.BlockSpec(memory_space=pl.ANY)
    return pl.pallas_call(
        body, name="weights_all_gather",
        out_shape=tuple(jax.ShapeDtypeStruct((8, h, w), BF16) for h, w in dims),
        in_specs=[any_spec] * n, out_specs=(any_spec,) * n,
        scratch_shapes=[pltpu.VMEM(d, F32) for d in dims] + [pltpu.VMEM(d, BF16) for d in dims] + [
            pltpu.SemaphoreType.DMA((7 * n,)), pltpu.SemaphoreType.DMA((7 * n,)), pltpu.SemaphoreType.DMA((2 * n,))],
        compiler_params=pltpu.CompilerParams(vmem_limit_bytes=VMEM_LIMIT),
    )(*shards)


def _grad_reduce(gparts, vec_leaves, loss_row, dw4):
    n = len(gparts)
    dims = [(g.shape[1] // 2, g.shape[2]) for g in gparts]
    nv = len(vec_leaves)
    row_step = 128

    def body(*refs):
        g_refs = refs[0:n]
        leaf_refs = refs[n:n + nv]
        loss_ref, dw4_ref = refs[n + nv:n + nv + 2]
        o = n + nv + 2
        out_refs = refs[o:o + n]
        vec_out, dw4_out = refs[o + n:o + n + 2]
        s0 = o + n + 2
        recv_a, own_a = refs[s0:s0 + n], refs[s0 + n:s0 + 2 * n]
        send_b, recv_b = refs[s0 + 2 * n:s0 + 3 * n], refs[s0 + 3 * n:s0 + 4 * n]
        fin = refs[s0 + 4 * n:s0 + 5 * n]
        vec_mine, vec_recv, dw4_recv, send_sems, recv_sems, local_sems = refs[s0 + 5 * n:]

        x, y, c = _my_place()
        chip = 2 * x + y
        me_lin = 4 * x + 2 * y + c
        sibling = (x, y, 1 - c)

        def rows_of(a, core):
            return pl.ds(pl.multiple_of(core * dims[a][0], 16), dims[a][0])

        to_sib, own = [], []
        for a in range(n):
            cp = pltpu.make_async_remote_copy(
                src_ref=g_refs[a].at[:, rows_of(a, 1 - c), :], dst_ref=recv_a[a], send_sem=send_sems.at[5 * a],
                recv_sem=recv_sems.at[5 * a], device_id=sibling, device_id_type=MESH)
            cp.start()
            to_sib.append(cp)
            cp = pltpu.make_async_copy(g_refs[a].at[:, rows_of(a, c), :], own_a[a], local_sems.at[a])
            cp.start()
            own.append(cp)

        vec_mine[...] = jnp.zeros_like(vec_mine)
        for (_, row, width), ref in zip(VEC_LEAVES, leaf_refs):
            vec_mine[row:row + 1, 0:ref.shape[1]] = ref[...]
        vec_mine[VEC_LOSS_ROW:VEC_LOSS_ROW + 1, 0:LANES] = loss_ref[...]
        small_copies = []
        for k in range(1, 8):
            peer = (me_lin + k) % 8
            to = (peer // 4, (peer // 2) % 2, peer % 2)
            for src, dst, base in ((vec_mine, vec_recv, 5 * n), (dw4_ref, dw4_recv, 5 * n + 7)):
                cp = pltpu.make_async_remote_copy(
                    src_ref=src, dst_ref=dst.at[me_lin], send_sem=send_sems.at[base + k - 1],
                    recv_sem=recv_sems.at[base + k - 1], device_id=to, device_id_type=MESH)
                cp.start()
                small_copies.append(cp)

        chip_copies = []
        for a in range(n):
            h = dims[a][0]
            own[a].wait()
            to_sib[a].wait_recv()
            for j in range(4):
                for lo in range(0, h, row_step):
                    rows = pl.ds(lo, min(row_step, h - lo))
                    send_b[a][j, rows, :] = (own_a[a][j, rows, :] + recv_a[a][j, rows, :]).astype(BF16)
            for k in range(1, 4):
                dest = (chip + k) % 4
                cp = pltpu.make_async_remote_copy(
                    src_ref=send_b[a].at[dest], dst_ref=recv_b[a].at[chip], send_sem=send_sems.at[5 * a + k],
                    recv_sem=recv_sems.at[5 * a + k], device_id=(dest // 2, dest % 2, c), device_id_type=MESH)
                cp.start()
                chip_copies.append(cp)
            keep = pltpu.make_async_copy(send_b[a].at[chip], recv_b[a].at[chip], local_sems.at[n + a])
            keep.start()
            keep.wait()

        give, mine = [], []
        for a in range(n):
            h = dims[a][0]
            for cp in chip_copies[3 * a:3 * a + 3]:
                cp.wait_recv()
            for lo in range(0, h, row_step):
                rows = pl.ds(lo, min(row_step, h - lo))
                tot = recv_b[a][0, rows, :].astype(F32) + recv_b[a][1, rows, :].astype(F32)
                tot = tot + recv_b[a][2, rows, :].astype(F32)
                fin[a][rows, :] = tot + recv_b[a][3, rows, :].astype(F32)
            cp = pltpu.make_async_remote_copy(
                src_ref=fin[a], dst_ref=out_refs[a].at[rows_of(a, c), :], send_sem=send_sems.at[5 * a + 4],
                recv_sem=recv_sems.at[5 * a + 4], device_id=sibling, device_id_type=MESH)
            cp.start()
            give.append(cp)
            cp = pltpu.make_async_copy(fin[a], out_refs[a].at[rows_of(a, c), :], local_sems.at[a])
            cp.start()
            mine.append(cp)

        for cp in small_copies:
            cp.wait_recv()
        vec_recv[me_lin] = vec_mine[...]
        dw4_recv[me_lin] = dw4_ref[...]
        vtot, wtot = vec_recv[0], dw4_recv[0]
        for d in range(1, 8):
            vtot = vtot + vec_recv[d]
            wtot = wtot + dw4_recv[d]
        vec_out[...] = vtot
        dw4_out[...] = wtot

        for a in range(n):
            give[a].wait_recv()
            mine[a].wait()
            to_sib[a].wait_send()
            give[a].wait_send()
        for cp in chip_copies + small_copies:
            cp.wait_send()

    any_spec = pl.BlockSpec(memory_space=pl.ANY)
    vmem_spec = pl.BlockSpec(memory_space=pltpu.VMEM)
    n_sems = 5 * n + 14
    scratch = []
    for dtype, lead in ((F32, (4,)), (F32, (4,)), (BF16, (4,)), (BF16, (4,)), (F32, ())):
        scratch += [pltpu.VMEM(lead + d, dtype) for d in dims]
    scratch += [pltpu.VMEM((VEC_ROWS, D_MODEL), F32), pltpu.VMEM((8, VEC_ROWS, D_MODEL), F32),
                pltpu.VMEM((8,) + dw4.shape, F32),
                pltpu.SemaphoreType.DMA((n_sems,)), pltpu.SemaphoreType.DMA((n_sems,)), pltpu.SemaphoreType.DMA((2 * n,))]
    return pl.pallas_call(
        body, name="grad_reduce",
        out_shape=tuple(jax.ShapeDtypeStruct((2 * h, w), F32) for h, w in dims) + (
            jax.ShapeDtypeStruct((VEC_ROWS, D_MODEL), F32), jax.ShapeDtypeStruct(dw4.shape, F32)),
        in_specs=[any_spec] * n + [vmem_spec] * (nv + 2),
        out_specs=(any_spec,) * n + (vmem_spec, vmem_spec),
        scratch_shapes=scratch,
        compiler_params=pltpu.CompilerParams(vmem_limit_bytes=VMEM_LIMIT),
    )(*gparts, *vec_leaves, loss_row, dw4)


def _mem_fwd(mem, mem_norm_g, w_kv, mk_g):
    n = mem.shape[0]

    def body(mem_ref, g_ref, w_ref, kg_ref, mn_ref, kv_ref, kn_ref, vm_ref):
        xm = mem_ref[...]
        rr = lax.rsqrt(jnp.mean(xm * xm, axis=-1, keepdims=True) + EPS)
        mnb = ((xm * rr) * g_ref[...]).astype(BF16)
        mn_ref[...] = mnb
        kv = _dot(mnb, w_ref[...])
        kv_ref[...] = kv
        lo = _lane_lo((n, LANES))
        for p in range(MEM_WIDTH // LANES):
            sl = slice(p * LANES, (p + 1) * LANES)
            kb = kv[:, sl]
            kn_ref[:, sl] = ((kb * _head_rms(kb, lo)) * kg_ref[:, sl]).astype(BF16)
        vm_ref[...] = kv[:, MEM_WIDTH:].astype(BF16)

    return pl.pallas_call(
        body, name="mem_fwd",
        out_shape=(jax.ShapeDtypeStruct((n, D_MODEL), BF16), jax.ShapeDtypeStruct((n, 2 * MEM_WIDTH), F32),
                   jax.ShapeDtypeStruct((n, MEM_WIDTH), BF16), jax.ShapeDtypeStruct((n, MEM_WIDTH), BF16)),
        compiler_params=pltpu.CompilerParams(vmem_limit_bytes=VMEM_LIMIT),
    )(mem, mem_norm_g, w_kv, mk_g)


def _fwd_in(x, norm_g, wp, bf_pad, fq_g, fk_g):
    s = x.shape[0]
    t = TILE
    n = s // t

    def body(x_ref, ng_ref, wp_ref, bf_ref, qg_ref, kg_ref,
             h_ref, pa_ref, qk_ref, qs_ref, kn_ref, v_ref, gb_ref, pm_ref, fb_ref, fcol_ref, ft_ref, carry_ref):
        @pl.when(pl.program_id(0) == 0)
        def _():
            carry_ref[...] = jnp.zeros_like(carry_ref)

        xv = x_ref[...]
        rr = lax.rsqrt(jnp.mean(xv * xv, axis=-1, keepdims=True) + EPS)
        hb = ((xv * rr) * ng_ref[...]).astype(BF16)
        h_ref[...] = hb

        def proj(lo, hi):
            return _dot(hb, wp_ref[:, lo:hi])

        pa_ref[...] = proj(PA_LO, QB_LO)
        lo = _lane_lo((t, LANES))
        for seg, g_ref, out_ref, scale in ((QB_LO, qg_ref, qs_ref, ATT_SCALE), (KB_LO, kg_ref, kn_ref, 1.0)):
            raw = proj(seg, seg + FOX_WIDTH)
            qk_ref[:, seg - QB_LO:seg - QB_LO + FOX_WIDTH] = raw
            for p in range(FOX_WIDTH // LANES):
                sl = slice(p * LANES, (p + 1) * LANES)
                blk = raw[:, sl]
                out_ref[:, sl] = (((blk * _head_rms(blk, lo)) * g_ref[:, sl]) * scale).astype(BF16)
        v_ref[...] = proj(VB_LO, GB_LO).astype(BF16)
        gb_ref[...] = proj(GB_LO, PM_LO)
        pm_ref[...] = proj(PM_LO, FB_LO)
        fb = proj(FB_LO, PROJ_PAD)
        fb_ref[...] = fb

        lane = lax.broadcasted_iota(jnp.int32, (t, LANES), 1)
        row = lax.broadcasted_iota(jnp.int32, (t, LANES), 0)
        z = fb + bf_ref[...]
        lf = -(jnp.maximum(-z, 0.0) + jnp.log1p(jnp.exp(-jnp.abs(z))))
        lf = jnp.where(lane < FOX_HEADS, lf, 0.0)
        sh = 1
        while sh < t:
            lf = lf + jnp.where(row >= sh, pltpu.roll(lf, sh, axis=0), 0.0)
            sh *= 2
        fcum = lf + carry_ref[...]
        fcol_ref[...] = fcum
        carry_ref[...] = fcol_ref[t - 1:t, :]
        ft_ref[0] = fcum.T[0:8, :]

    outs = (
        jax.ShapeDtypeStruct((s, D_MODEL), BF16),
        jax.ShapeDtypeStruct((s, 512), F32),
        jax.ShapeDtypeStruct((s, 2 * FOX_WIDTH), F32),
        jax.ShapeDtypeStruct((s, FOX_WIDTH), BF16),
        jax.ShapeDtypeStruct((s, FOX_WIDTH), BF16),
        jax.ShapeDtypeStruct((s, FOX_WIDTH), BF16),
        jax.ShapeDtypeStruct((s, FOX_WIDTH), F32),
        jax.ShapeDtypeStruct((s, 512), F32),
        jax.ShapeDtypeStruct((s, LANES), F32),
        jax.ShapeDtypeStruct((s, LANES), F32),
        jax.ShapeDtypeStruct((n, 8, t), F32),
    )
    return pl.pallas_call(
        body, name="fwd_in", grid=(n,), out_shape=outs,
        in_specs=[_rows(t, D_MODEL), _full((1, D_MODEL)), _full((D_MODEL, PROJ_PAD)), _full((1, LANES)),
                  _full((1, FOX_WIDTH)), _full((1, FOX_WIDTH))],
        out_specs=(_rows(t, D_MODEL), _rows(t, 512), _rows(t, 2 * FOX_WIDTH), _rows(t, FOX_WIDTH),
                   _rows(t, FOX_WIDTH), _rows(t, FOX_WIDTH), _rows(t, FOX_WIDTH), _rows(t, 512),
                   _rows(t, LANES), _rows(t, LANES), pl.BlockSpec((1, 8, t), lambda i: (i, 0, 0))),
        scratch_shapes=[pltpu.VMEM((1, LANES), F32)],
        compiler_params=_params(),
    )(x, norm_g, wp, bf_pad, fq_g, fk_g)


POOL_HALO = 16


def _pool_window(lane):
    return jnp.where(lane < 64, 2.0, jnp.where(lane < 128, 4.0, jnp.where(lane < 192, 8.0, 16.0)))


def _pool_pick(lane, s2, s4, s8, s16):
    return jnp.where(lane < 64, s2, jnp.where(lane < 128, s4, jnp.where(lane < 192, s8, s16)))


def _group_onehot(shape, row_is_group_lane):
    r = lax.broadcasted_iota(jnp.int32, shape, 0)
    c = lax.broadcasted_iota(jnp.int32, shape, 1)
    hit = (r % HEAD_DIM == c) if row_is_group_lane else (c % HEAD_DIM == r)
    return jnp.where(hit, 1.0, 0.0).astype(F32)


def _same_group(shape):
    r = lax.broadcasted_iota(jnp.int32, shape, 0)
    c = lax.broadcasted_iota(jnp.int32, shape, 1)
    return (r // HEAD_DIM) == (c // HEAD_DIM)


def _pool_block_diag(w4):
    spread = jnp.dot(w4, _group_onehot((HEAD_DIM, POOL_WIDTH), False), preferred_element_type=F32,
                     precision=lax.Precision.HIGHEST)
    return jnp.where(_same_group((POOL_WIDTH, POOL_WIDTH)), spread, 0.0).astype(BF16)


def _pool_fwd(pa, w4, pscale):
    s = pa.shape[0]
    t = TILE
    n = s // t
    ext = t + POOL_HALO

    def body(pa_ref, w4_ref, sc_ref, ma_ref, d_ref, ext_ref, w_ref):
        i = pl.program_id(0)

        @pl.when(i == 0)
        def _():
            ext_ref[0:POOL_HALO, :] = jnp.zeros((POOL_HALO, POOL_WIDTH), F32)
            w_ref[...] = _pool_block_diag(w4_ref[...])

        u = pa_ref[:, 0:POOL_WIDTH]
        ext_ref[POOL_HALO:ext, :] = u
        e = ext_ref[...]
        s2 = e + pltpu.roll(e, 1, axis=0)
        s4 = s2 + pltpu.roll(s2, 2, axis=0)
        s8 = s4 + pltpu.roll(s4, 4, axis=0)
        s16 = s8 + pltpu.roll(s8, 8, axis=0)
        lane_e = lax.broadcasted_iota(jnp.int32, (ext, POOL_WIDTH), 1)
        win = _pool_pick(lane_e, s2, s4, s8, s16)[POOL_HALO:ext, :]
        lane = lax.broadcasted_iota(jnp.int32, (t, POOL_WIDTH), 1)
        pos = (lax.broadcasted_iota(jnp.int32, (t, POOL_WIDTH), 0) + (i * t + 1)).astype(F32)
        d = win / jnp.minimum(pos, _pool_window(lane)) - u
        db = d.astype(BF16)
        d_ref[...] = db
        ya = _dot(db, w_ref[...]) * sc_ref[...]
        ga = pa_ref[:, POOL_WIDTH:2 * POOL_WIDTH]
        ma_ref[...] = (ya * (ga * _sig(ga))).astype(BF16)
        ext_ref[0:POOL_HALO, :] = ext_ref[t:ext, :]

    return pl.pallas_call(
        body, name="pool_fwd", grid=(n,),
        out_shape=(jax.ShapeDtypeStruct((s, POOL_WIDTH), BF16), jax.ShapeDtypeStruct((s, POOL_WIDTH), BF16)),
        in_specs=[_rows(t, 512), _full((POOL_ROWS, HEAD_DIM)), _full((1, POOL_WIDTH))],
        out_specs=(_rows(t, POOL_WIDTH), _rows(t, POOL_WIDTH)),
        scratch_shapes=[pltpu.VMEM((ext, POOL_WIDTH), F32), pltpu.VMEM((POOL_WIDTH, POOL_WIDTH), BF16)],
        compiler_params=_params(),
    )(pa, w4, pscale)


def _mem_softmax(qm, kp):
    sc = _dot(qm, kp, NT)
    e = jnp.exp(sc - jnp.max(sc, axis=-1, keepdims=True))
    return e * (1.0 / jnp.sum(e, axis=-1, keepdims=True))


def _mem_attn_fwd(pm, kmn, vmb, mq_g):
    s = pm.shape[0]
    t = TILE
    n = s // t

    def body(pm_ref, k_ref, v_ref, g_ref, mm_ref):
        lo = _lane_lo((t, LANES))
        for p in range(MEM_WIDTH // LANES):
            sl = slice(p * LANES, (p + 1) * LANES)
            qb = pm_ref[:, sl]
            qs = (((qb * _head_rms(qb, lo)) * g_ref[:, sl]) * ATT_SCALE).astype(BF16)
            kp = k_ref[:, sl]
            vp = v_ref[:, sl]
            outs = []
            for hh in range(2):
                msk = lo if hh == 0 else jnp.logical_not(lo)
                prob = _mem_softmax(jnp.where(msk, qs, jnp.zeros_like(qs)), kp)
                outs.append(_dot(prob.astype(BF16), vp))
            o = jnp.where(lo, outs[0], outs[1])
            gm = pm_ref[:, MEM_WIDTH + p * LANES:MEM_WIDTH + (p + 1) * LANES]
            mm_ref[:, sl] = (o * (gm * _sig(gm))).astype(BF16)

    return pl.pallas_call(
        body, name="mem_attn_fwd", grid=(n,),
        out_shape=jax.ShapeDtypeStruct((s, MEM_WIDTH), BF16),
        in_specs=[_rows(t, 512), _full((N_MEM, MEM_WIDTH)), _full((N_MEM, MEM_WIDTH)), _full((1, MEM_WIDTH))],
        out_specs=_rows(t, MEM_WIDTH),
        compiler_params=_params(),
    )(pm, kmn, vmb, mq_g)


def _fox_fwd(qs, kn, v, ft, fcol, gb):
    s = qs.shape[0]
    t = TILE
    n = s // t

    def body(qs_ref, kn_ref, v_ref, ft_ref, fq_ref, gb_ref, o_ref, mb_ref, rcol_ref):
        i = pl.program_id(0)
        lane = lax.broadcasted_iota(jnp.int32, (t, LANES), 1)
        lo = lane < HEAD_DIM
        lane1 = lax.broadcasted_iota(jnp.int32, (1, LANES), 1)
        causal = lax.broadcasted_iota(jnp.int32, (t, t), 1) <= lax.broadcasted_iota(jnp.int32, (t, t), 0)
        frow0 = fq_ref[0:1, :]
        rcol = jnp.zeros((t, LANES), F32)
        for p in range(FOX_WIDTH // LANES):
            sl = slice(p * LANES, (p + 1) * LANES)
            q2 = qs_ref[:, sl]
            outs = []
            for hh in range(2):
                h = 2 * p + hh
                msk = lo if hh == 0 else jnp.logical_not(lo)
                qm = jnp.where(msk, q2, jnp.zeros_like(q2))
                fref = _lane_pick(frow0, lane1, h)

                def step(j, carry, masked, qm=qm, fref=fref, h=h, sl=sl):
                    m, l, acc = carry
                    rows = pl.ds(pl.multiple_of(j * t, t), t)
                    sc = _dot(qm, kn_ref[rows, sl], NT) - (ft_ref[j, h:h + 1, :] - fref)
                    if masked:
                        sc = jnp.where(causal, sc, -1e30)
                    m_new = jnp.maximum(m, jnp.max(sc, axis=-1, keepdims=True))
                    alpha = jnp.exp(m - m_new)
                    e = jnp.exp(sc - m_new)
                    l = alpha * l + jnp.sum(e, axis=-1, keepdims=True)
                    acc = alpha * acc + _dot(e.astype(BF16), v_ref[rows, sl])
                    return m_new, l, acc

                init = (jnp.full((t, 1), -1e30, F32), jnp.zeros((t, 1), F32), jnp.zeros((t, LANES), F32))
                carry = lax.fori_loop(0, i, functools.partial(step, masked=False), init)
                m, l, acc = step(i, carry, masked=True)
                outs.append(acc * (1.0 / l))
                rcol = jnp.where(lane == h, m + jnp.log(l) - fref, rcol)
            o = jnp.where(lo, outs[0], outs[1])
            o_ref[:, sl] = o
            g = gb_ref[:, sl]
            mb_ref[:, sl] = (o * (g * _sig(g))).astype(BF16)
        rcol_ref[...] = rcol

    return pl.pallas_call(
        body, name="fox_fwd", grid=(n,),
        out_shape=(jax.ShapeDtypeStruct((s, FOX_WIDTH), F32), jax.ShapeDtypeStruct((s, FOX_WIDTH), BF16),
                   jax.ShapeDtypeStruct((s, LANES), F32)),
        in_specs=[_rows(t, FOX_WIDTH), _full((s, FOX_WIDTH)), _full((s, FOX_WIDTH)), _full((n, 8, t)),
                  _rows(t, LANES), _rows(t, FOX_WIDTH)],
        out_specs=(_rows(t, FOX_WIDTH), _rows(t, FOX_WIDTH), _rows(t, LANES)),
        compiler_params=_params(),
    )(qs, kn, v, ft, fcol, gb)


def _out_loss(x, tgt, ma, mb, mm, wout):
    s = x.shape[0]
    t = TILE
    n = s // t

    def body(x_ref, t_ref, ma_ref, mb_ref, mm_ref, w_ref, dy_ref, dma_ref, dmb_ref, dmm_ref, dw_ref, loss_ref, mix_ref):
        @pl.when(pl.program_id(0) == 0)
        def _():
            dw_ref[...] = jnp.zeros_like(dw_ref)
            loss_ref[...] = jnp.zeros_like(loss_ref)

        mix_ref[:, 0:256] = ma_ref[...]
        mix_ref[:, 256:768] = mb_ref[...]
        mix_ref[:, 768:1024] = mm_ref[...]
        mix = mix_ref[...]
        err = (x_ref[...] + _dot(mix, w_ref[...])) - t_ref[...]
        row_mean = jnp.sum(err * err, axis=-1, keepdims=True) * (1.0 / D_MODEL)
        loss_ref[...] += 0.5 * jnp.sum(row_mean, axis=0, keepdims=True)
        dy = err * (1.0 / D_MODEL)
        dy_ref[...] = dy
        dyb = dy.astype(BF16)
        dmix = _dot(dyb, w_ref[...], NT)
        dma_ref[...] = dmix[:, 0:256]
        dmb_ref[...] = dmix[:, 256:768]
        dmm_ref[...] = dmix[:, 768:1024]
        dw_ref[...] += _dot(mix, dyb, TN)

    return pl.pallas_call(
        body, name="out_loss", grid=(n,),
        out_shape=(jax.ShapeDtypeStruct((s, D_MODEL), F32), jax.ShapeDtypeStruct((s, 256), F32),
                   jax.ShapeDtypeStruct((s, 512), F32), jax.ShapeDtypeStruct((s, 256), F32),
                   jax.ShapeDtypeStruct((D_MODEL, D_MODEL), F32), jax.ShapeDtypeStruct((1, LANES), F32)),
        in_specs=[_rows(t, D_MODEL), _rows(t, D_MODEL), _rows(t, 256), _rows(t, 512), _rows(t, 256),
                  _full((D_MODEL, D_MODEL))],
        out_specs=(_rows(t, D_MODEL), _rows(t, 256), _rows(t, 512), _rows(t, 256), _full((D_MODEL, D_MODEL)),
                   _full((1, LANES))),
        scratch_shapes=[pltpu.VMEM((t, D_MODEL), BF16)],
        compiler_params=_params(),
    )(x, tgt, ma, mb, mm, wout)


def _mem_attn_bwd(pm, dmm, kmn, vmb, mq_g):
    s = pm.shape[0]
    t = TILE
    n = s // t

    def body(pm_ref, dmm_ref, k_ref, v_ref, g_ref, dpm_ref, dk_ref, dv_ref, dg_ref, gacc_ref):
        @pl.when(pl.program_id(0) == 0)
        def _():
            dk_ref[...] = jnp.zeros_like(dk_ref)
            dv_ref[...] = jnp.zeros_like(dv_ref)
            gacc_ref[...] = jnp.zeros_like(gacc_ref)

        lo = _lane_lo((t, LANES))
        for p in range(MEM_WIDTH // LANES):
            sl = slice(p * LANES, (p + 1) * LANES)
            qb = pm_ref[:, sl]
            rr = _head_rms(qb, lo)
            qhat = qb * rr
            g = g_ref[:, sl]
            qs = ((qhat * g) * ATT_SCALE).astype(BF16)
            gm = pm_ref[:, MEM_WIDTH + p * LANES:MEM_WIDTH + (p + 1) * LANES]
            sg = _sig(gm)
            dmo = dmm_ref[:, sl]
            d_o = dmo * (gm * sg)
            kp = k_ref[:, sl]
            vp = v_ref[:, sl]
            outs, dqs = [], []
            for hh in range(2):
                msk = lo if hh == 0 else jnp.logical_not(lo)
                qm = jnp.where(msk, qs, jnp.zeros_like(qs))
                prob = _mem_softmax(qm, kp)
                pb = prob.astype(BF16)
                outs.append(_dot(pb, vp))
                dom = jnp.where(msk, d_o, 0.0).astype(BF16)
                dp = _dot(dom, vp, NT)
                ds = (prob * (dp - jnp.sum(prob * dp, axis=-1, keepdims=True))).astype(BF16)
                dqs.append(_dot(ds, kp))
                dk_ref[:, sl] += _dot(ds, qm, TN)
                dv_ref[:, sl] += _dot(pb, dom, TN)
            o = jnp.where(lo, outs[0], outs[1])
            dqn = jnp.where(lo, dqs[0], dqs[1]) * ATT_SCALE
            dpm_ref[:, sl] = _head_norm_bwd(dqn, qhat, rr, g, lo).astype(BF16)
            dpm_ref[:, MEM_WIDTH + p * LANES:MEM_WIDTH + (p + 1) * LANES] = (
                dmo * o * (sg * (1.0 + gm * (1.0 - sg)))).astype(BF16)
            gacc_ref[:, sl] += jnp.sum(dqn * qhat, axis=0, keepdims=True)

        @pl.when(pl.program_id(0) == n - 1)
        def _():
            dg_ref[...] = _fold_heads(gacc_ref[...])

    return pl.pallas_call(
        body, name="mem_attn_bwd", grid=(n,),
        out_shape=(jax.ShapeDtypeStruct((s, 512), BF16), jax.ShapeDtypeStruct((N_MEM, MEM_WIDTH), F32),
                   jax.ShapeDtypeStruct((N_MEM, MEM_WIDTH), F32), jax.ShapeDtypeStruct((1, LANES), F32)),
        in_specs=[_rows(t, 512), _rows(t, MEM_WIDTH), _full((N_MEM, MEM_WIDTH)), _full((N_MEM, MEM_WIDTH)),
                  _full((1, MEM_WIDTH))],
        out_specs=(_rows(t, 512), _full((N_MEM, MEM_WIDTH)), _full((N_MEM, MEM_WIDTH)), _full((1, LANES))),
        scratch_shapes=[pltpu.VMEM((1, MEM_WIDTH), F32)],
        compiler_params=_params(),
    )(pm, dmm, kmn, vmb, mq_g)


def _mem_bwd(dkn, dvm, kv, mnb, mem, w_kv, mk_g, mem_norm_g):
    n = mem.shape[0]

    def body(dkn_ref, dvm_ref, kv_ref, mn_ref, mem_ref, w_ref, kg_ref, g_ref, dw_ref, dg_ref, dkg_ref, dkv_ref):
        lo = _lane_lo((n, LANES))
        gacc = []
        for p in range(MEM_WIDTH // LANES):
            sl = slice(p * LANES, (p + 1) * LANES)
            kb = kv_ref[:, sl]
            rr = _head_rms(kb, lo)
            khat = kb * rr
            dk = dkn_ref[:, sl]
            dkv_ref[:, sl] = _head_norm_bwd(dk, khat, rr, kg_ref[:, sl], lo).astype(BF16)
            gacc.append(jnp.sum(dk * khat, axis=0, keepdims=True))
        dkg_ref[...] = _fold_heads(jnp.concatenate(gacc, axis=1))
        dkv_ref[:, MEM_WIDTH:] = dvm_ref[...].astype(BF16)
        dkv = dkv_ref[...]
        dw_ref[...] = _dot(mn_ref[...], dkv, TN)
        dmn = _dot(dkv, w_ref[...], NT)
        xm = mem_ref[...]
        rr = lax.rsqrt(jnp.mean(xm * xm, axis=-1, keepdims=True) + EPS)
        dg_ref[...] = jnp.sum(dmn * (xm * rr), axis=0, keepdims=True)

    return pl.pallas_call(
        body, name="mem_bwd",
        out_shape=(jax.ShapeDtypeStruct((D_MODEL, 2 * MEM_WIDTH), F32), jax.ShapeDtypeStruct((1, D_MODEL), F32),
                   jax.ShapeDtypeStruct((1, LANES), F32)),
        scratch_shapes=[pltpu.VMEM((n, 2 * MEM_WIDTH), BF16)],
        compiler_params=pltpu.CompilerParams(vmem_limit_bytes=VMEM_LIMIT),
    )(dkn, dvm, kv, mnb, mem, w_kv, mk_g, mem_norm_g)


def _pool_bwd(pa, db, dma, w4, pscale):
    s = pa.shape[0]
    t = TILE
    n = s // t
    ext = t + POOL_HALO

    def body(pa_ref, d_ref, dma_ref, w4_ref, sc_ref, dpa_ref, dw4_ref, dsc_ref, ext_ref, w_ref, dw_ref):
        i = pl.program_id(0)

        @pl.when(i == 0)
        def _():
            dw_ref[...] = jnp.zeros_like(dw_ref)
            dsc_ref[...] = jnp.zeros_like(dsc_ref)
            ext_ref[t:ext, :] = jnp.zeros((POOL_HALO, POOL_WIDTH), F32)
            w_ref[...] = _pool_block_diag(w4_ref[...])

        dbv = d_ref[...]
        z = _dot(dbv, w_ref[...])
        ga = pa_ref[:, POOL_WIDTH:2 * POOL_WIDTH]
        sg = _sig(ga)
        dma_v = dma_ref[...]
        dya = dma_v * (ga * sg)
        dpa_ref[:, POOL_WIDTH:2 * POOL_WIDTH] = (dma_v * (z * sc_ref[...]) * (sg * (1.0 + ga * (1.0 - sg)))).astype(BF16)
        dsc_ref[...] += jnp.sum(dya * z, axis=0, keepdims=True)
        dzb = (dya * sc_ref[...]).astype(BF16)
        dw_ref[...] += _dot(dbv, dzb, TN)
        dd = _dot(dzb, w_ref[...], NT)
        lane = lax.broadcasted_iota(jnp.int32, (t, POOL_WIDTH), 1)
        pos = (lax.broadcasted_iota(jnp.int32, (t, POOL_WIDTH), 0) + ((n - 1 - i) * t + 1)).astype(F32)
        ext_ref[0:t, :] = dd / jnp.minimum(pos, _pool_window(lane))
        e = ext_ref[...]
        s2 = e + pltpu.roll(e, ext - 1, axis=0)
        s4 = s2 + pltpu.roll(s2, ext - 2, axis=0)
        s8 = s4 + pltpu.roll(s4, ext - 4, axis=0)
        s16 = s8 + pltpu.roll(s8, ext - 8, axis=0)
        lane_e = lax.broadcasted_iota(jnp.int32, (ext, POOL_WIDTH), 1)
        win = _pool_pick(lane_e, s2, s4, s8, s16)[0:t, :]
        dpa_ref[:, 0:POOL_WIDTH] = (win - dd).astype(BF16)
        ext_ref[t:ext, :] = ext_ref[0:POOL_HALO, :]

        @pl.when(i == n - 1)
        def _():
            own = jnp.where(_same_group((POOL_WIDTH, POOL_WIDTH)), dw_ref[...], 0.0)
            dw4_ref[...] = jnp.dot(own, _group_onehot((POOL_WIDTH, HEAD_DIM), True), preferred_element_type=F32,
                                   precision=lax.Precision.HIGHEST)

    return pl.pallas_call(
        body, name="pool_bwd", grid=(n,),
        out_shape=(jax.ShapeDtypeStruct((s, 512), BF16), jax.ShapeDtypeStruct((POOL_ROWS, HEAD_DIM), F32),
                   jax.ShapeDtypeStruct((1, POOL_WIDTH), F32)),
        in_specs=[_rows_rev(t, 512, n), _rows_rev(t, POOL_WIDTH, n), _rows_rev(t, POOL_WIDTH, n),
                  _full((POOL_ROWS, HEAD_DIM)), _full((1, POOL_WIDTH))],
        out_specs=(_rows_rev(t, 512, n), _full((POOL_ROWS, HEAD_DIM)), _full((1, POOL_WIDTH))),
        scratch_shapes=[pltpu.VMEM((ext, POOL_WIDTH), F32), pltpu.VMEM((POOL_WIDTH, POOL_WIDTH), BF16),
                        pltpu.VMEM((POOL_WIDTH, POOL_WIDTH), F32)],
        compiler_params=_params(),
    )(pa, db, dma, w4, pscale)


def _fox_prep(dmb, gb, o, rcol):
    s = dmb.shape[0]
    t = TILE
    n = s // t

    def body(dmb_ref, gb_ref, o_ref, r_ref, do_ref, dgb_ref, rd_ref):
        lane = lax.broadcasted_iota(jnp.int32, (t, LANES), 1)
        lo = lane < HEAD_DIM
        col = r_ref[...]
        for p in range(FOX_WIDTH // LANES):
            sl = slice(p * LANES, (p + 1) * LANES)
            g = gb_ref[:, sl]
            sg = _sig(g)
            dm = dmb_ref[:, sl]
            ov = o_ref[:, sl]
            d_o = dm * (g * sg)
            do_ref[:, sl] = d_o.astype(BF16)
            dgb_ref[:, sl] = (dm * ov * (sg * (1.0 + g * (1.0 - sg)))).astype(BF16)
            prod = d_o * ov
            col = jnp.where(lane == 8 + 2 * p, jnp.sum(jnp.where(lo, prod, 0.0), axis=-1, keepdims=True), col)
            col = jnp.where(lane == 9 + 2 * p, jnp.sum(jnp.where(lo, 0.0, prod), axis=-1, keepdims=True), col)
        rd_ref[0] = col.T[0:16, :]

    return pl.pallas_call(
        body, name="fox_prep", grid=(n,),
        out_shape=(jax.ShapeDtypeStruct((s, FOX_WIDTH), BF16), jax.ShapeDtypeStruct((s, FOX_WIDTH), BF16),
                   jax.ShapeDtypeStruct((n, 16, t), F32)),
        in_specs=[_rows(t, FOX_WIDTH), _rows(t, FOX_WIDTH), _rows(t, FOX_WIDTH), _rows(t, LANES)],
        out_specs=(_rows(t, FOX_WIDTH), _rows(t, FOX_WIDTH), pl.BlockSpec((1, 16, t), lambda i: (i, 0, 0))),
        compiler_params=_params(),
    )(dmb, gb, o, rcol)


def _fox_bwd(kn, v, fcol, qs, dob, rd):
    s = kn.shape[0]
    t = TILE
    n = s // t

    def body(kn_ref, v_ref, fc_ref, qs_ref, do_ref, rd_ref, dkn_ref, dv_ref, dfc_ref, dqs_ref, drs_ref):
        j = pl.program_id(0)

        @pl.when(j == 0)
        def _():
            dqs_ref[...] = jnp.zeros_like(dqs_ref)
            drs_ref[...] = jnp.zeros_like(drs_ref)

        lane = lax.broadcasted_iota(jnp.int32, (t, LANES), 1)
        lo = lane < HEAD_DIM
        lane1 = lax.broadcasted_iota(jnp.int32, (1, LANES), 1)
        causal = lax.broadcasted_iota(jnp.int32, (t, t), 0) <= lax.broadcasted_iota(jnp.int32, (t, t), 1)
        fc = fc_ref[...]
        frow0 = fc_ref[0:1, :]
        dfc = jnp.zeros((t, LANES), F32)
        for p in range(FOX_WIDTH // LANES):
            sl = slice(p * LANES, (p + 1) * LANES)
            k2 = kn_ref[:, sl]
            v2 = v_ref[:, sl]
            dks = []
            dv_pair = jnp.zeros((t, LANES), F32)
            for hh in range(2):
                h = 2 * p + hh
                msk = lo if hh == 0 else jnp.logical_not(lo)
                km = jnp.where(msk, k2, jnp.zeros_like(k2))
                vm = jnp.where(msk, v2, jnp.zeros_like(v2))
                fr = _lane_pick(frow0, lane1, h)
                gk = _lane_pick(fc, lane, h) - fr

                def step(i, carry, masked, km=km, vm=vm, fr=fr, gk=gk, h=h, sl=sl, msk=msk):
                    dk_a, dv_a, cs = carry
                    rows = pl.ds(pl.multiple_of(i * t, t), t)
                    qb = qs_ref[rows, sl]
                    d_o = do_ref[rows, sl]
                    arg = (_dot(km, qb, NT) - gk) - (rd_ref[i, h:h + 1, :] + fr)
                    if masked:
                        arg = jnp.where(causal, arg, -1e30)
                    pt = jnp.exp(arg)
                    dst32 = pt * (_dot(vm, d_o, NT) - rd_ref[i, 8 + h:9 + h, :])
                    cs = cs + jnp.sum(dst32, axis=-1, keepdims=True)
                    drs_ref[i, h:h + 1, :] += jnp.sum(dst32, axis=0, keepdims=True)
                    dst = dst32.astype(BF16)
                    dv_a = dv_a + _dot(pt.astype(BF16), jnp.where(msk, d_o, jnp.zeros_like(d_o)))
                    dk_a = dk_a + _dot(dst, jnp.where(msk, qb, jnp.zeros_like(qb)))
                    dqs_ref[rows, sl] += _dot(dst, km, TN)
                    return dk_a, dv_a, cs

                zero = jnp.zeros((t, LANES), F32)
                carry = step(j, (zero, zero, jnp.zeros((t, 1), F32)), masked=True)
                dk_a, dv_a, cs = lax.fori_loop(j + 1, n, functools.partial(step, masked=False), carry)
                dks.append(dk_a)
                dv_pair = dv_pair + dv_a
                dfc = jnp.where(lane == h, cs, dfc)
            dkn_ref[:, sl] = jnp.where(lo, dks[0], dks[1])
            dv_ref[:, sl] = dv_pair.astype(BF16)
        dfc_ref[...] = dfc

    return pl.pallas_call(
        body, name="fox_bwd", grid=(n,),
        out_shape=(jax.ShapeDtypeStruct((s, FOX_WIDTH), F32), jax.ShapeDtypeStruct((s, FOX_WIDTH), BF16),
                   jax.ShapeDtypeStruct((s, LANES), F32), jax.ShapeDtypeStruct((s, FOX_WIDTH), F32),
                   jax.ShapeDtypeStruct((n, 8, t), F32)),
        in_specs=[_rows(t, FOX_WIDTH), _rows(t, FOX_WIDTH), _rows(t, LANES), _full((s, FOX_WIDTH)),
                  _full((s, FOX_WIDTH)), _full((n, 16, t))],
        out_specs=(_rows(t, FOX_WIDTH), _rows(t, FOX_WIDTH), _rows(t, LANES), _full((s, FOX_WIDTH)),
                   _full((n, 8, t))),
        compiler_params=_params(),
    )(kn, v, fcol, qs, dob, rd)


def _fox_post(dqs, dkn, dfc, drs, qk, fb, bf_pad, fq_g, fk_g):
    s = dqs.shape[0]
    t = TILE
    n = s // t

    def body(dqs_ref, dkn_ref, dfc_ref, drs_ref, qk_ref, fb_ref, bf_ref, qg_ref, kg_ref,
             dqk_ref, dfb_ref, dqg_ref, dkg_ref, dbf_ref, qacc_ref, kacc_ref, carry_ref):
        i = pl.program_id(0)

        @pl.when(i == 0)
        def _():
            qacc_ref[...] = jnp.zeros_like(qacc_ref)
            kacc_ref[...] = jnp.zeros_like(kacc_ref)
            dbf_ref[...] = jnp.zeros_like(dbf_ref)
            carry_ref[...] = jnp.zeros_like(carry_ref)

        lo = _lane_lo((t, LANES))
        for off, d_ref, g_ref, acc_ref, scale in ((0, dqs_ref, qg_ref, qacc_ref, ATT_SCALE),
                                                  (FOX_WIDTH, dkn_ref, kg_ref, kacc_ref, 1.0)):
            for p in range(FOX_WIDTH // LANES):
                sl = slice(p * LANES, (p + 1) * LANES)
                raw = qk_ref[:, off + p * LANES:off + (p + 1) * LANES]
                rr = _head_rms(raw, lo)
                xhat = raw * rr
                dn = d_ref[:, sl] * scale
                dqk_ref[:, off + p * LANES:off + (p + 1) * LANES] = _head_norm_bwd(
                    dn, xhat, rr, g_ref[:, sl], lo).astype(BF16)
                acc_ref[:, sl] += jnp.sum(dn * xhat, axis=0, keepdims=True)

        lane = lax.broadcasted_iota(jnp.int32, (t, LANES), 1)
        row = lax.broadcasted_iota(jnp.int32, (t, LANES), 0)
        rows_h = jnp.concatenate([drs_ref[0], jnp.zeros((LANES - FOX_HEADS, t), F32)], axis=0)
        acc = rows_h.T - dfc_ref[...]
        sh = 1
        while sh < t:
            acc = acc + jnp.where(row < t - sh, pltpu.roll(acc, t - sh, axis=0), 0.0)
            sh *= 2
        dlogf = acc + carry_ref[...]
        dfb_ref[...] = dlogf
        carry_ref[...] = dfb_ref[0:1, :]
        z = fb_ref[...] + bf_ref[...]
        dz = jnp.where(lane < FOX_HEADS, dlogf * (1.0 / (1.0 + jnp.exp(z))), 0.0)
        dfb_ref[...] = dz
        dbf_ref[...] += jnp.sum(dz, axis=0, keepdims=True)

        @pl.when(i == n - 1)
        def _():
            dqg_ref[...] = _fold_heads(qacc_ref[...])
            dkg_ref[...] = _fold_heads(kacc_ref[...])

    return pl.pallas_call(
        body, name="fox_post", grid=(n,),
        out_shape=(jax.ShapeDtypeStruct((s, 2 * FOX_WIDTH), BF16), jax.ShapeDtypeStruct((s, LANES), F32),
                   jax.ShapeDtypeStruct((1, LANES), F32), jax.ShapeDtypeStruct((1, LANES), F32),
                   jax.ShapeDtypeStruct((1, LANES), F32)),
        in_specs=[_rows_rev(t, FOX_WIDTH, n), _rows_rev(t, FOX_WIDTH, n), _rows_rev(t, LANES, n),
                  pl.BlockSpec((1, FOX_HEADS, t), lambda i: (n - 1 - i, 0, 0)),
                  _rows_rev(t, 2 * FOX_WIDTH, n), _rows_rev(t, LANES, n), _full((1, LANES)),
                  _full((1, FOX_WIDTH)), _full((1, FOX_WIDTH))],
        out_specs=(_rows_rev(t, 2 * FOX_WIDTH, n), _rows_rev(t, LANES, n), _full((1, LANES)), _full((1, LANES)),
                   _full((1, LANES))),
        scratch_shapes=[pltpu.VMEM((1, FOX_WIDTH), F32), pltpu.VMEM((1, FOX_WIDTH), F32), pltpu.VMEM((1, LANES), F32)],
        compiler_params=_params(),
    )(dqs, dkn, dfc, drs, qk, fb, bf_pad, fq_g, fk_g)


def _assemble_dproj(dp_ref, dpa_ref, dqk_ref, dv_ref, dgb_ref, dpm_ref, dfb_ref):
    dp_ref[:, PA_LO:QB_LO] = dpa_ref[...]
    dp_ref[:, QB_LO:VB_LO] = dqk_ref[...]
    dp_ref[:, VB_LO:GB_LO] = dv_ref[...]
    dp_ref[:, GB_LO:PM_LO] = dgb_ref[...]
    dp_ref[:, PM_LO:FB_LO] = dpm_ref[...]
    dp_ref[:, FB_LO:PROJ_PAD] = dfb_ref[...].astype(BF16)


def _dproj_specs(t):
    return [_rows(t, 512), _rows(t, 2 * FOX_WIDTH), _rows(t, FOX_WIDTH), _rows(t, FOX_WIDTH), _rows(t, 512),
            _rows(t, LANES)]


def _in_bwd_x(x, dy, norm_g, wp, dparts):
    s = x.shape[0]
    t = TILE
    n = s // t

    def body(x_ref, dy_ref, g_ref, wp_ref, dpa_ref, dqk_ref, dv_ref, dgb_ref, dpm_ref, dfb_ref, gx_ref, dg_ref, dp_ref):
        @pl.when(pl.program_id(0) == 0)
        def _():
            dg_ref[...] = jnp.zeros_like(dg_ref)

        _assemble_dproj(dp_ref, dpa_ref, dqk_ref, dv_ref, dgb_ref, dpm_ref, dfb_ref)
        dh = _dot(dp_ref[...], wp_ref[...], NT)
        xv = x_ref[...]
        rr = lax.rsqrt(jnp.mean(xv * xv, axis=-1, keepdims=True) + EPS)
        xhat = xv * rr
        a = dh * g_ref[...]
        gx_ref[...] = dy_ref[...] + rr * (a - xhat * jnp.mean(xhat * a, axis=-1, keepdims=True))
        dg_ref[...] += jnp.sum(dh * xhat, axis=0, keepdims=True)

    return pl.pallas_call(
        body, name="in_bwd_x", grid=(n,),
        out_shape=(jax.ShapeDtypeStruct((s, D_MODEL), F32), jax.ShapeDtypeStruct((1, D_MODEL), F32)),
        in_specs=[_rows(t, D_MODEL), _rows(t, D_MODEL), _full((1, D_MODEL)), _full((D_MODEL, PROJ_PAD))] + _dproj_specs(t),
        out_specs=(_rows(t, D_MODEL), _full((1, D_MODEL))),
        scratch_shapes=[pltpu.VMEM((t, PROJ_PAD), BF16)],
        compiler_params=_params(),
    )(x, dy, norm_g, wp, *dparts)


def _in_bwd_w(hb, dparts):
    s = hb.shape[0]
    t = TILE
    n = s // t

    def body(h_ref, dpa_ref, dqk_ref, dv_ref, dgb_ref, dpm_ref, dfb_ref, dw_ref, dp_ref):
        @pl.when(pl.program_id(0) == 0)
        def _():
            dw_ref[...] = jnp.zeros_like(dw_ref)

        _assemble_dproj(dp_ref, dpa_ref, dqk_ref, dv_ref, dgb_ref, dpm_ref, dfb_ref)
        dw_ref[...] += _dot(h_ref[...], dp_ref[...], TN)

    return pl.pallas_call(
        body, name="in_bwd_w", grid=(n,),
        out_shape=jax.ShapeDtypeStruct((D_MODEL, PROJ_PAD), F32),
        in_specs=[_rows(t, D_MODEL)] + _dproj_specs(t),
        out_specs=_full((D_MODEL, PROJ_PAD)),
        scratch_shapes=[pltpu.VMEM((t, PROJ_PAD), BF16)],
        compiler_params=_params(),
    )(hb, *dparts)


def _adamw_math(w_ref, gv, m_ref, v_ref, d_ref, nm_ref, nv_ref):
    nm = ADAM_B1 * m_ref[...] + (1.0 - ADAM_B1) * gv
    nv = ADAM_B2 * v_ref[...] + (1.0 - ADAM_B2) * (gv * gv)
    m_hat = nm / (1.0 - ADAM_B1 ** ADAM_STEP)
    v_hat = nv / (1.0 - ADAM_B2 ** ADAM_STEP)
    d_ref[...] = -ADAM_LR * (m_hat / (jnp.sqrt(v_hat) + ADAM_EPS) + ADAM_WD * w_ref[...])
    nm_ref[...] = nm
    nv_ref[...] = nv


def _adamw(name, w, g, m, v):
    rows, cols = w.shape
    t = 256 if rows > 256 and rows % 256 == 0 else rows
    n = rows // t

    def body(w_ref, g_ref, m_ref, v_ref, d_ref, nm_ref, nv_ref):
        _adamw_math(w_ref, g_ref[...], m_ref, v_ref, d_ref, nm_ref, nv_ref)

    spec = _rows(t, cols)
    return pl.pallas_call(
        body, name=name, grid=(n,),
        out_shape=(jax.ShapeDtypeStruct((rows, cols), F32),) * 3,
        in_specs=[spec] * 4, out_specs=(spec,) * 3,
        compiler_params=_params(),
    )(w, g, m, v)


def _adamw_small(vec, dw4, leaves, pool):
    nl = len(VEC_LEAVES) + 1

    def body(*refs):
        vec_ref, dw4_ref = refs[0:2]
        wmv = refs[2:2 + 3 * nl]
        loss_ref = refs[2 + 3 * nl]
        outs = refs[3 + 3 * nl:]
        loss_ref[...] = vec_ref[VEC_LOSS_ROW:VEC_LOSS_ROW + 1, 0:1]
        for k in range(nl):
            if k < nl - 1:
                _, row, width = VEC_LEAVES[k]
                gv = vec_ref[row:row + 1, 0:width]
            else:
                gv = dw4_ref[...]
            w_ref, m_ref, v_ref = wmv[3 * k:3 * k + 3]
            g_ref, d_ref, nm_ref, nv_ref = outs[4 * k:4 * k + 4]
            g_ref[...] = gv
            _adamw_math(w_ref, gv, m_ref, v_ref, d_ref, nm_ref, nv_ref)

    shapes = [jax.ShapeDtypeStruct((1, width), F32) for _, _, width in VEC_LEAVES] + [
        jax.ShapeDtypeStruct(dw4.shape, F32)]
    flat_in = [a for triple in list(leaves) + [pool] for a in triple]
    res = pl.pallas_call(
        body, name="adamw_small",
        out_shape=(jax.ShapeDtypeStruct((1, 1), F32),) + tuple(s for s in shapes for _ in range(4)),
        compiler_params=pltpu.CompilerParams(vmem_limit_bytes=VMEM_LIMIT),
    )(vec, dw4, *flat_in)
    per = [res[1 + 4 * k:5 + 4 * k] for k in range(nl)]
    return res[0], [p[0] for p in per], [p[1] for p in per], [p[2] for p in per], [p[3] for p in per]


def _full_w_in_padded(shards):
    w_in = jnp.concatenate([shards[j] for j in range(4)], axis=1)
    return jnp.concatenate([
        w_in[:, 0:F_ORIG_LO], w_in[:, F_ORIG_LO + FOX_HEADS:], w_in[:, F_ORIG_LO:F_ORIG_LO + FOX_HEADS],
        jnp.zeros((w_in.shape[0], PROJ_PAD - IN_WIDTH), w_in.dtype)], axis=1)


def _shard_padded_cols(dwp):
    full = jnp.concatenate([dwp[:, 0:F_ORIG_LO], dwp[:, FB_LO:FB_LO + FOX_HEADS], dwp[:, F_ORIG_LO:FB_LO]], axis=1)
    cols = IN_WIDTH // 4
    return jnp.stack([full[:, j * cols:(j + 1) * cols] for j in range(4)])


def _tile_heads(g, n):
    return jnp.tile(g.reshape(1, HEAD_DIM), (1, n))


def kernel(x, mem, norm_g, w_in, b_f, w_pool, pool_scale, fox_q_g, fox_k_g, mem_norm_g, w_mem_kv, mem_q_g, mem_k_g, w_out, loss_target, m_norm_g, m_w_in, m_b_f, m_w_pool, m_pool_scale, m_fox_q_g, m_fox_k_g, m_mem_norm_g, m_w_mem_kv, m_mem_q_g, m_mem_k_g, m_w_out, v_norm_g, v_w_in, v_b_f, v_w_pool, v_pool_scale, v_fox_q_g, v_fox_k_g, v_mem_norm_g, v_w_mem_kv, v_mem_q_g, v_mem_k_g, v_w_out):
    g_in, g_kv, g_out = _all_gather_weights([w_in[0], w_mem_kv[0], w_out[0]])
    w_in_b = g_in.reshape(4, D_MODEL, IN_WIDTH // 4)
    w_kv_b = g_kv.reshape(D_MODEL, 2 * MEM_WIDTH)
    w_out_b = g_out.reshape(D_MODEL, D_MODEL)
    w4 = w_pool.reshape(POOL_ROWS, HEAD_DIM)
    grad_x, dwp, dw_kv, dw_out, vec_leaves, loss_row, dw4 = _local_grads(
        x[0], mem[0], loss_target[0], w_in_b, w_kv_b, w_out_b, norm_g, b_f, w4, pool_scale, fox_q_g, fox_k_g,
        mem_norm_g, mem_q_g, mem_k_g)

    gparts = [_shard_padded_cols(dwp), dw_kv.reshape(4, D_MODEL // 4, 2 * MEM_WIDTH),
              dw_out.reshape(4, D_MODEL // 4, D_MODEL)]
    g_w_in, g_w_kv, g_w_out, vec, dw4_sum = _grad_reduce(gparts, vec_leaves, loss_row, dw4)

    small_wmv = [(norm_g, m_norm_g, v_norm_g), (mem_norm_g, m_mem_norm_g, v_mem_norm_g),
                 (pool_scale, m_pool_scale, v_pool_scale), (b_f, m_b_f, v_b_f), (fox_q_g, m_fox_q_g, v_fox_q_g),
                 (fox_k_g, m_fox_k_g, v_fox_k_g), (mem_q_g, m_mem_q_g, v_mem_q_g), (mem_k_g, m_mem_k_g, v_mem_k_g)]
    pool_wmv = tuple(a.reshape(POOL_ROWS, HEAD_DIM) for a in (w_pool, m_w_pool, v_w_pool))
    loss, *small_out = _adamw_small(vec, dw4_sum, small_wmv, pool_wmv)
    big = [[g_w_in[None], g_w_kv[None], g_w_out[None]]]
    upd = [_adamw("adamw_w_in", w_in[0], g_w_in, m_w_in[0], v_w_in[0]),
           _adamw("adamw_w_mem_kv", w_mem_kv[0], g_w_kv, m_w_mem_kv[0], v_w_mem_kv[0]),
           _adamw("adamw_w_out", w_out[0], g_w_out, m_w_out[0], v_w_out[0])]
    big += [[u[k][None] for u in upd] for k in range(3)]

    def leaves(k):
        sm = small_out[k]
        b_in, b_kv, b_out = big[k]
        return (sm[0], b_in, sm[3], sm[8].reshape(w_pool.shape), sm[2], sm[4], sm[5], sm[1], b_kv, sm[6], sm[7], b_out)

    return (loss.reshape(()), grad_x[None], *leaves(0), *leaves(1), *leaves(2), *leaves(3))


def _local_grads(xs, mems, tgt, w_in_b, w_kv_b, w_out_b, norm_g, b_f, w4, pool_scale, fox_q_g, fox_k_g,
                 mem_norm_g, mem_q_g, mem_k_g):
    wp = _full_w_in_padded(w_in_b)
    bf_pad = jnp.pad(b_f, ((0, 0), (0, LANES - FOX_HEADS)))
    fq_g, fk_g = _tile_heads(fox_q_g, FOX_HEADS), _tile_heads(fox_k_g, FOX_HEADS)
    mq_g, mk_g = _tile_heads(mem_q_g, 4), _tile_heads(mem_k_g, 4)

    mnb, kv, kmn, vmb = _mem_fwd(mems, mem_norm_g, w_kv_b, mk_g)
    hb, pa, qk, qs, kn, vb, gb, pm, fb, fcol, ft = _fwd_in(xs, norm_g, wp, bf_pad, fq_g, fk_g)
    ma, db = _pool_fwd(pa, w4, pool_scale)
    mm = _mem_attn_fwd(pm, kmn, vmb, mq_g)
    o, mb, rcol = _fox_fwd(qs, kn, vb, ft, fcol, gb)
    dy, dma, dmb, dmm, dw_out, loss_row = _out_loss(xs, tgt, ma, mb, mm, w_out_b)

    dpm, dkmn, dvm, dmq_g = _mem_attn_bwd(pm, dmm, kmn, vmb, mq_g)
    dw_kv, dmemnorm_g, dmk_g = _mem_bwd(dkmn, dvm, kv, mnb, mems, w_kv_b, mk_g, mem_norm_g)
    dpa, dw4, dpscale = _pool_bwd(pa, db, dma, w4, pool_scale)
    dob, dgb, rd = _fox_prep(dmb, gb, o, rcol)
    dkn, dvb, dfc, dqs, drs = _fox_bwd(kn, vb, fcol, qs, dob, rd)
    dqk, dfb, dfq_g, dfk_g, dbf = _fox_post(dqs, dkn, dfc, drs, qk, fb, bf_pad, fq_g, fk_g)
    dparts = (dpa, dqk, dvb, dgb, dpm, dfb)
    grad_x, dnorm_g = _in_bwd_x(xs, dy, norm_g, wp, dparts)
    dwp = _in_bwd_w(hb, dparts)

    vec_leaves = (dnorm_g, dmemnorm_g, dpscale, dbf, dfq_g, dfk_g, dmq_g, dmk_g)
    return grad_x, dwp, dw_kv, dw_out, vec_leaves, loss_row, dw4
```

```python
import functools

import jax
import jax.numpy as jnp
from jax import lax
from jax.experimental import pallas as pl
from jax.experimental.pallas import tpu as pltpu

F32 = jnp.float32
BF16 = jnp.bfloat16
MESH = pl.DeviceIdType.MESH

D_MODEL = 1024
HEAD_DIM = 64
POOL_WIDTH = 256
FOX_WIDTH = 512
FOX_HEADS = 8
MEM_WIDTH = 256
N_MEM = 256
IN_WIDTH = 3080
EPS = 1e-6
ATT_SCALE = 0.125

ADAM_LR = 0.001
ADAM_B1 = 0.9
ADAM_B2 = 0.999
ADAM_EPS = 1e-08
ADAM_WD = 0.01
ADAM_STEP = 10

LANES = 128
PA_LO, QB_LO, KB_LO, VB_LO, GB_LO, PM_LO, FB_LO, PROJ_PAD = 0, 512, 1024, 1536, 2048, 2560, 3072, 3200
F_ORIG_LO = 2048

TILE = 512
VMEM_LIMIT = 56 * 1024 * 1024

VEC_LEAVES = (("norm_g", 0, 1024), ("mem_norm_g", 1, 1024), ("pool_scale", 2, 256), ("b_f", 3, 8),
              ("fox_q_g", 4, 64), ("fox_k_g", 5, 64), ("mem_q_g", 6, 64), ("mem_k_g", 7, 64))
VEC_LOSS_ROW = 8
VEC_ROWS = 16
POOL_ROWS = 256


def _params(n_grid=1, vmem=VMEM_LIMIT):
    return pltpu.CompilerParams(dimension_semantics=("arbitrary",) * n_grid, vmem_limit_bytes=vmem)


def _rows(t, w):
    return pl.BlockSpec((t, w), lambda i: (i, 0))


def _rows_rev(t, w, n):
    return pl.BlockSpec((t, w), lambda i: (n - 1 - i, 0))


def _full(shape):
    return pl.BlockSpec(shape, lambda i: (0,) * len(shape))


def _sig(x):
    return 1.0 / (1.0 + jnp.exp(-x))


def _lane_lo(shape):
    return lax.broadcasted_iota(jnp.int32, shape, 1) < HEAD_DIM


def _pair_sum(v, lo):
    s0 = jnp.sum(jnp.where(lo, v, 0.0), axis=-1, keepdims=True)
    s1 = jnp.sum(jnp.where(lo, 0.0, v), axis=-1, keepdims=True)
    return jnp.where(lo, s0, s1)


def _head_rms(blk, lo):
    return lax.rsqrt(_pair_sum(blk * blk, lo) * (1.0 / HEAD_DIM) + EPS)


def _head_norm_bwd(dyn, xhat, rr, g, lo):
    a = dyn * g
    return rr * (a - xhat * (_pair_sum(xhat * a, lo) * (1.0 / HEAD_DIM)))


def _fold_heads(acc):
    tot = acc[:, 0:LANES]
    for p in range(1, acc.shape[1] // LANES):
        tot = tot + acc[:, p * LANES:(p + 1) * LANES]
    return tot + pltpu.roll(tot, HEAD_DIM, axis=1)


def _lane_pick(v, lane, idx):
    return jnp.sum(jnp.where(lane == idx, v, 0.0), axis=-1, keepdims=True)


NT = (((1,), (1,)), ((), ()))
TN = (((0,), (0,)), ((), ()))


def _dot(a, b, dims=None):
    if dims is None:
        return jnp.dot(a, b, preferred_element_type=F32)
    return lax.dot_general(a, b, dims, preferred_element_type=F32)


def _my_place():
    return lax.axis_index("x"), lax.axis_index("y"), lax.axis_index("c")


def _half_dims(shape, axis):
    return (shape[0] // 2, shape[1]) if axis == 0 else (shape[0], shape[1] // 2)


def _half_of(ref, axis, core, lead=False):
    rows, cols = ref.shape[-2:]
    if axis == 0:
        idx = (pl.ds(pl.multiple_of(core * (rows // 2), 16), rows // 2), slice(None))
    else:
        idx = (slice(None), pl.ds(pl.multiple_of(core * (cols // 2), LANES), cols // 2))
    return ref.at[(slice(None),) + idx] if lead else ref.at[idx]


def _all_gather_weights(shards, axes):
    n = len(shards)
    dims = [_half_dims(a.shape, axis) for a, axis in zip(shards, axes)]

    def body(*refs):
        ins, outs = refs[0:n], refs[n:2 * n]
        f32_bufs, bf_bufs = refs[2 * n:3 * n], refs[3 * n:4 * n]
        send_sems, recv_sems, local_sems = refs[4 * n:]
        x, y, c = _my_place()
        me, sibling = (x, y, c), (x, y, 1 - c)
        chips = [(1 - x, y), (x, 1 - y), (1 - x, 1 - y)]

        loads = []
        for a in range(n):
            cp = pltpu.make_async_copy(_half_of(ins[a], axes[a], c), f32_bufs[a], local_sems.at[a])
            cp.start()
            loads.append(cp)

        def blk(a, px, py, pc):
            return outs[a].at[4 * px + 2 * py + pc]

        def copy(a, k, block, to, src=None):
            return pltpu.make_async_remote_copy(
                src_ref=blk(a, *block) if src is None else src, dst_ref=blk(a, *block),
                send_sem=send_sems.at[7 * a + k], recv_sem=recv_sems.at[7 * a + k], device_id=to, device_id_type=MESH)

        first, keeps = [], []
        for a in range(n):
            loads[a].wait()
            bf_bufs[a][...] = f32_bufs[a][...].astype(BF16)
            keep = pltpu.make_async_copy(bf_bufs[a], blk(a, *me), local_sems.at[n + a])
            keep.start()
            keeps.append(keep)
            mine = [copy(a, 0, me, sibling, src=bf_bufs[a])]
            mine += [copy(a, 1 + j, me, (*chip, c), src=bf_bufs[a]) for j, chip in enumerate(chips)]
            for cp in mine:
                cp.start()
            first += mine
        passed = []
        for a in range(n):
            for j, chip in enumerate(chips):
                copy(a, 1 + j, (*chip, c), me).wait_recv()
                cp = copy(a, 4 + j, (*chip, c), sibling)
                cp.start()
                passed.append(cp)
        for a in range(n):
            copy(a, 0, sibling, me).wait_recv()
            for j, chip in enumerate(chips):
                copy(a, 4 + j, (*chip, 1 - c), me).wait_recv()
        for cp in first + passed:
            cp.wait_send()
        for keep in keeps:
            keep.wait()

    any_spec = pl.BlockSpec(memory_space=pl.ANY)
    return pl.pallas_call(
        body, name="weights_all_gather",
        out_shape=tuple(jax.ShapeDtypeStruct((8, h, w), BF16) for h, w in dims),
        in_specs=[any_spec] * n, out_specs=(any_spec,) * n,
        scratch_shapes=[pltpu.VMEM(d, F32) for d in dims] + [pltpu.VMEM(d, BF16) for d in dims] + [
            pltpu.SemaphoreType.DMA((7 * n,)), pltpu.SemaphoreType.DMA((7 * n,)), pltpu.SemaphoreType.DMA((2 * n,))],
        compiler_params=pltpu.CompilerParams(vmem_limit_bytes=VMEM_LIMIT),
    )(*shards)


def _grad_reduce(gparts, axes, vec_leaves, loss_row, dw4):
    n = len(gparts)
    dims = [_half_dims(g.shape[1:], axis) for g, axis in zip(gparts, axes)]
    nv = len(vec_leaves)

    def body(*refs):
        g_refs = refs[0:n]
        leaf_refs = refs[n:n + nv]
        loss_ref, dw4_ref = refs[n + nv:n + nv + 2]
        o = n + nv + 2
        out_refs = refs[o:o + n]
        vec_out, dw4_out = refs[o + n:o + n + 2]
        s0 = o + n + 2
        recv_a, own_a = refs[s0:s0 + n], refs[s0 + n:s0 + 2 * n]
        send_b, recv_b = refs[s0 + 2 * n:s0 + 3 * n], refs[s0 + 3 * n:s0 + 4 * n]
        fin = refs[s0 + 4 * n:s0 + 5 * n]
        vec_mine, vec_recv, dw4_recv, send_sems, recv_sems, local_sems = refs[s0 + 5 * n:]

        x, y, c = _my_place()
        chip = 2 * x + y
        me_lin = 4 * x + 2 * y + c
        sibling = (x, y, 1 - c)

        to_sib, own = [], []
        for a in range(n):
            cp = pltpu.make_async_remote_copy(
                src_ref=_half_of(g_refs[a], axes[a], 1 - c, lead=True), dst_ref=recv_a[a], send_sem=send_sems.at[5 * a],
                recv_sem=recv_sems.at[5 * a], device_id=sibling, device_id_type=MESH)
            cp.start()
            to_sib.append(cp)
            cp = pltpu.make_async_copy(_half_of(g_refs[a], axes[a], c, lead=True), own_a[a], local_sems.at[a])
            cp.start()
            own.append(cp)

        vec_mine[...] = jnp.zeros_like(vec_mine)
        for (_, row, width), ref in zip(VEC_LEAVES, leaf_refs):
            vec_mine[row:row + 1, 0:ref.shape[1]] = ref[...]
        vec_mine[VEC_LOSS_ROW:VEC_LOSS_ROW + 1, 0:LANES] = loss_ref[...]
        small_copies = []
        for k in range(1, 8):
            peer = (me_lin + k) % 8
            to = (peer // 4, (peer // 2) % 2, peer % 2)
            for src, dst, base in ((vec_mine, vec_recv, 5 * n), (dw4_ref, dw4_recv, 5 * n + 7)):
                cp = pltpu.make_async_remote_copy(
                    src_ref=src, dst_ref=dst.at[me_lin], send_sem=send_sems.at[base + k - 1],
                    recv_sem=recv_sems.at[base + k - 1], device_id=to, device_id_type=MESH)
                cp.start()
                small_copies.append(cp)

        chip_copies = []
        for a in range(n):
            own[a].wait()
            to_sib[a].wait_recv()
            for j in range(4):
                send_b[a][j] = (own_a[a][j] + recv_a[a][j]).astype(BF16)
            for k in range(1, 4):
                dest = (chip + k) % 4
                cp = pltpu.make_async_remote_copy(
                    src_ref=send_b[a].at[dest], dst_ref=recv_b[a].at[chip], send_sem=send_sems.at[5 * a + k],
                    recv_sem=recv_sems.at[5 * a + k], device_id=(dest // 2, dest % 2, c), device_id_type=MESH)
                cp.start()
                chip_copies.append(cp)
            keep = pltpu.make_async_copy(send_b[a].at[chip], recv_b[a].at[chip], local_sems.at[n + a])
            keep.start()
            keep.wait()

        give, mine = [], []
        for a in range(n):
            for cp in chip_copies[3 * a:3 * a + 3]:
                cp.wait_recv()
            tot = recv_b[a][0].astype(F32) + recv_b[a][1].astype(F32)
            tot = tot + recv_b[a][2].astype(F32)
            fin[a][...] = tot + recv_b[a][3].astype(F32)
            cp = pltpu.make_async_remote_copy(
                src_ref=fin[a], dst_ref=_half_of(out_refs[a], axes[a], c), send_sem=send_sems.at[5 * a + 4],
                recv_sem=recv_sems.at[5 * a + 4], device_id=sibling, device_id_type=MESH)
            cp.start()
            give.append(cp)
            cp = pltpu.make_async_copy(fin[a], _half_of(out_refs[a], axes[a], c), local_sems.at[a])
            cp.start()
            mine.append(cp)

        for cp in small_copies:
            cp.wait_recv()
        vec_recv[me_lin] = vec_mine[...]
        dw4_recv[me_lin] = dw4_ref[...]
        vtot, wtot = vec_recv[0], dw4_recv[0]
        for d in range(1, 8):
            vtot = vtot + vec_recv[d]
            wtot = wtot + dw4_recv[d]
        vec_out[...] = vtot
        dw4_out[...] = wtot

        for a in range(n):
            give[a].wait_recv()
            mine[a].wait()
            to_sib[a].wait_send()
            give[a].wait_send()
        for cp in chip_copies + small_copies:
            cp.wait_send()

    any_spec = pl.BlockSpec(memory_space=pl.ANY)
    vmem_spec = pl.BlockSpec(memory_space=pltpu.VMEM)
    n_sems = 5 * n + 14
    scratch = []
    for dtype, lead in ((F32, (4,)), (F32, (4,)), (BF16, (4,)), (BF16, (4,)), (F32, ())):
        scratch += [pltpu.VMEM(lead + d, dtype) for d in dims]
    scratch += [pltpu.VMEM((VEC_ROWS, D_MODEL), F32), pltpu.VMEM((8, VEC_ROWS, D_MODEL), F32),
                pltpu.VMEM((8,) + dw4.shape, F32),
                pltpu.SemaphoreType.DMA((n_sems,)), pltpu.SemaphoreType.DMA((n_sems,)), pltpu.SemaphoreType.DMA((2 * n,))]
    return pl.pallas_call(
        body, name="grad_reduce",
        out_shape=tuple(jax.ShapeDtypeStruct(g.shape[1:], F32) for g in gparts) + (
            jax.ShapeDtypeStruct((VEC_ROWS, D_MODEL), F32), jax.ShapeDtypeStruct(dw4.shape, F32)),
        in_specs=[any_spec] * n + [vmem_spec] * (nv + 2),
        out_specs=(any_spec,) * n + (vmem_spec, vmem_spec),
        scratch_shapes=scratch,
        compiler_params=pltpu.CompilerParams(vmem_limit_bytes=VMEM_LIMIT),
    )(*gparts, *vec_leaves, loss_row, dw4)


def _mem_fwd(mem, mem_norm_g, w_kv, mk_g):
    n = mem.shape[0]

    def body(mem_ref, g_ref, w_ref, kg_ref, mn_ref, kv_ref, kn_ref, vm_ref):
        xm = mem_ref[...]
        rr = lax.rsqrt(jnp.mean(xm * xm, axis=-1, keepdims=True) + EPS)
        mnb = ((xm * rr) * g_ref[...]).astype(BF16)
        mn_ref[...] = mnb
        kv = _dot(mnb, w_ref[...])
        kv_ref[...] = kv
        lo = _lane_lo((n, LANES))
        for p in range(MEM_WIDTH // LANES):
            sl = slice(p * LANES, (p + 1) * LANES)
            kb = kv[:, sl]
            kn_ref[:, sl] = ((kb * _head_rms(kb, lo)) * kg_ref[:, sl]).astype(BF16)
        vm_ref[...] = kv[:, MEM_WIDTH:].astype(BF16)

    return pl.pallas_call(
        body, name="mem_fwd",
        out_shape=(jax.ShapeDtypeStruct((n, D_MODEL), BF16), jax.ShapeDtypeStruct((n, 2 * MEM_WIDTH), F32),
                   jax.ShapeDtypeStruct((n, MEM_WIDTH), BF16), jax.ShapeDtypeStruct((n, MEM_WIDTH), BF16)),
        compiler_params=pltpu.CompilerParams(vmem_limit_bytes=VMEM_LIMIT),
    )(mem, mem_norm_g, w_kv, mk_g)


def _fwd_in(x, norm_g, wp, bf_pad, fq_g, fk_g):
    s = x.shape[0]
    t = TILE
    n = s // t

    def body(x_ref, ng_ref, wp_ref, bf_ref, qg_ref, kg_ref,
             h_ref, pa_ref, qk_ref, qs_ref, kn_ref, v_ref, gb_ref, pm_ref, fb_ref, fcol_ref, ft_ref, carry_ref):
        @pl.when(pl.program_id(0) == 0)
        def _():
            carry_ref[...] = jnp.zeros_like(carry_ref)

        xv = x_ref[...]
        rr = lax.rsqrt(jnp.mean(xv * xv, axis=-1, keepdims=True) + EPS)
        hb = ((xv * rr) * ng_ref[...]).astype(BF16)
        h_ref[...] = hb

        def proj(lo, hi):
            return _dot(hb, wp_ref[lo:hi, :], NT)

        pa_ref[...] = proj(PA_LO, QB_LO)
        lo = _lane_lo((t, LANES))
        for seg, g_ref, out_ref, scale in ((QB_LO, qg_ref, qs_ref, ATT_SCALE), (KB_LO, kg_ref, kn_ref, 1.0)):
            raw = proj(seg, seg + FOX_WIDTH)
            qk_ref[:, seg - QB_LO:seg - QB_LO + FOX_WIDTH] = raw
            for p in range(FOX_WIDTH // LANES):
                sl = slice(p * LANES, (p + 1) * LANES)
                blk = raw[:, sl]
                out_ref[:, sl] = (((blk * _head_rms(blk, lo)) * g_ref[:, sl]) * scale).astype(BF16)
        v_ref[...] = proj(VB_LO, GB_LO).astype(BF16)
        gb_ref[...] = proj(GB_LO, PM_LO)
        pm_ref[...] = proj(PM_LO, FB_LO)
        fb = proj(FB_LO, PROJ_PAD)
        fb_ref[...] = fb

        lane = lax.broadcasted_iota(jnp.int32, (t, LANES), 1)
        row = lax.broadcasted_iota(jnp.int32, (t, LANES), 0)
        z = fb + bf_ref[...]
        lf = -(jnp.maximum(-z, 0.0) + jnp.log1p(jnp.exp(-jnp.abs(z))))
        lf = jnp.where(lane < FOX_HEADS, lf, 0.0)
        sh = 1
        while sh < t:
            lf = lf + jnp.where(row >= sh, pltpu.roll(lf, sh, axis=0), 0.0)
            sh *= 2
        fcum = lf + carry_ref[...]
        fcol_ref[...] = fcum
        carry_ref[...] = fcol_ref[t - 1:t, :]
        ft_ref[0] = fcum.T[0:8, :]

    outs = (
        jax.ShapeDtypeStruct((s, D_MODEL), BF16),
        jax.ShapeDtypeStruct((s, 512), F32),
        jax.ShapeDtypeStruct((s, 2 * FOX_WIDTH), F32),
        jax.ShapeDtypeStruct((s, FOX_WIDTH), BF16),
        jax.ShapeDtypeStruct((s, FOX_WIDTH), BF16),
        jax.ShapeDtypeStruct((s, FOX_WIDTH), BF16),
        jax.ShapeDtypeStruct((s, FOX_WIDTH), F32),
        jax.ShapeDtypeStruct((s, 512), F32),
        jax.ShapeDtypeStruct((s, LANES), F32),
        jax.ShapeDtypeStruct((s, LANES), F32),
        jax.ShapeDtypeStruct((n, 8, t), F32),
    )
    return pl.pallas_call(
        body, name="fwd_in", grid=(n,), out_shape=outs,
        in_specs=[_rows(t, D_MODEL), _full((1, D_MODEL)), _full((PROJ_PAD, D_MODEL)), _full((1, LANES)),
                  _full((1, FOX_WIDTH)), _full((1, FOX_WIDTH))],
        out_specs=(_rows(t, D_MODEL), _rows(t, 512), _rows(t, 2 * FOX_WIDTH), _rows(t, FOX_WIDTH),
                   _rows(t, FOX_WIDTH), _rows(t, FOX_WIDTH), _rows(t, FOX_WIDTH), _rows(t, 512),
                   _rows(t, LANES), _rows(t, LANES), pl.BlockSpec((1, 8, t), lambda i: (i, 0, 0))),
        scratch_shapes=[pltpu.VMEM((1, LANES), F32)],
        compiler_params=_params(),
    )(x, norm_g, wp, bf_pad, fq_g, fk_g)


POOL_HALO = 16


def _pool_window(lane):
    return jnp.where(lane < 64, 2.0, jnp.where(lane < 128, 4.0, jnp.where(lane < 192, 8.0, 16.0)))


def _pool_pick(lane, s2, s4, s8, s16):
    return jnp.where(lane < 64, s2, jnp.where(lane < 128, s4, jnp.where(lane < 192, s8, s16)))


def _group_onehot(shape, row_is_group_lane):
    r = lax.broadcasted_iota(jnp.int32, shape, 0)
    c = lax.broadcasted_iota(jnp.int32, shape, 1)
    hit = (r % HEAD_DIM == c) if row_is_group_lane else (c % HEAD_DIM == r)
    return jnp.where(hit, 1.0, 0.0).astype(F32)


def _same_group(shape):
    r = lax.broadcasted_iota(jnp.int32, shape, 0)
    c = lax.broadcasted_iota(jnp.int32, shape, 1)
    return (r // HEAD_DIM) == (c // HEAD_DIM)


def _pool_block_diag(w4):
    spread = jnp.dot(w4, _group_onehot((HEAD_DIM, POOL_WIDTH), False), preferred_element_type=F32,
                     precision=lax.Precision.HIGHEST)
    return jnp.where(_same_group((POOL_WIDTH, POOL_WIDTH)), spread, 0.0).astype(BF16)


def _pool_fwd(pa, w4, pscale):
    s = pa.shape[0]
    t = TILE
    n = s // t
    ext = t + POOL_HALO

    def body(pa_ref, w4_ref, sc_ref, ma_ref, d_ref, ext_ref, w_ref):
        i = pl.program_id(0)

        @pl.when(i == 0)
        def _():
            ext_ref[0:POOL_HALO, :] = jnp.zeros((POOL_HALO, POOL_WIDTH), F32)
            w_ref[...] = _pool_block_diag(w4_ref[...])

        u = pa_ref[:, 0:POOL_WIDTH]
        ext_ref[POOL_HALO:ext, :] = u
        e = ext_ref[...]
        s2 = e + pltpu.roll(e, 1, axis=0)
        s4 = s2 + pltpu.roll(s2, 2, axis=0)
        s8 = s4 + pltpu.roll(s4, 4, axis=0)
        s16 = s8 + pltpu.roll(s8, 8, axis=0)
        lane_e = lax.broadcasted_iota(jnp.int32, (ext, POOL_WIDTH), 1)
        win = _pool_pick(lane_e, s2, s4, s8, s16)[POOL_HALO:ext, :]
        lane = lax.broadcasted_iota(jnp.int32, (t, POOL_WIDTH), 1)
        pos = (lax.broadcasted_iota(jnp.int32, (t, POOL_WIDTH), 0) + (i * t + 1)).astype(F32)
        d = win / jnp.minimum(pos, _pool_window(lane)) - u
        db = d.astype(BF16)
        d_ref[...] = db
        ya = _dot(db, w_ref[...]) * sc_ref[...]
        ga = pa_ref[:, POOL_WIDTH:2 * POOL_WIDTH]
        ma_ref[...] = (ya * (ga * _sig(ga))).astype(BF16)
        ext_ref[0:POOL_HALO, :] = ext_ref[t:ext, :]

    return pl.pallas_call(
        body, name="pool_fwd", grid=(n,),
        out_shape=(jax.ShapeDtypeStruct((s, POOL_WIDTH), BF16), jax.ShapeDtypeStruct((s, POOL_WIDTH), BF16)),
        in_specs=[_rows(t, 512), _full((POOL_ROWS, HEAD_DIM)), _full((1, POOL_WIDTH))],
        out_specs=(_rows(t, POOL_WIDTH), _rows(t, POOL_WIDTH)),
        scratch_shapes=[pltpu.VMEM((ext, POOL_WIDTH), F32), pltpu.VMEM((POOL_WIDTH, POOL_WIDTH), BF16)],
        compiler_params=_params(),
    )(pa, w4, pscale)


def _mem_softmax(qm, kp):
    sc = _dot(qm, kp, NT)
    e = jnp.exp(sc - jnp.max(sc, axis=-1, keepdims=True))
    return e * (1.0 / jnp.sum(e, axis=-1, keepdims=True))


def _mem_attn_fwd(pm, kmn, vmb, mq_g):
    s = pm.shape[0]
    t = TILE
    n = s // t

    def body(pm_ref, k_ref, v_ref, g_ref, mm_ref):
        lo = _lane_lo((t, LANES))
        for p in range(MEM_WIDTH // LANES):
            sl = slice(p * LANES, (p + 1) * LANES)
            qb = pm_ref[:, sl]
            qs = (((qb * _head_rms(qb, lo)) * g_ref[:, sl]) * ATT_SCALE).astype(BF16)
            kp = k_ref[:, sl]
            vp = v_ref[:, sl]
            outs = []
            for hh in range(2):
                msk = lo if hh == 0 else jnp.logical_not(lo)
                prob = _mem_softmax(jnp.where(msk, qs, jnp.zeros_like(qs)), kp)
                outs.append(_dot(prob.astype(BF16), vp))
            o = jnp.where(lo, outs[0], outs[1])
            gm = pm_ref[:, MEM_WIDTH + p * LANES:MEM_WIDTH + (p + 1) * LANES]
            mm_ref[:, sl] = (o * (gm * _sig(gm))).astype(BF16)

    return pl.pallas_call(
        body, name="mem_attn_fwd", grid=(n,),
        out_shape=jax.ShapeDtypeStruct((s, MEM_WIDTH), BF16),
        in_specs=[_rows(t, 512), _full((N_MEM, MEM_WIDTH)), _full((N_MEM, MEM_WIDTH)), _full((1, MEM_WIDTH))],
        out_specs=_rows(t, MEM_WIDTH),
        compiler_params=_params(),
    )(pm, kmn, vmb, mq_g)


def _fox_fwd(qs, kn, v, ft, fcol, gb):
    s = qs.shape[0]
    t = TILE
    n = s // t

    def body(qs_ref, kn_ref, v_ref, ft_ref, fq_ref, gb_ref, o_ref, mb_ref, rcol_ref):
        i = pl.program_id(0)
        lane = lax.broadcasted_iota(jnp.int32, (t, LANES), 1)
        lo = lane < HEAD_DIM
        lane1 = lax.broadcasted_iota(jnp.int32, (1, LANES), 1)
        causal = lax.broadcasted_iota(jnp.int32, (t, t), 1) <= lax.broadcasted_iota(jnp.int32, (t, t), 0)
        frow0 = fq_ref[0:1, :]
        rcol = jnp.zeros((t, LANES), F32)
        for p in range(FOX_WIDTH // LANES):
            sl = slice(p * LANES, (p + 1) * LANES)
            q2 = qs_ref[:, sl]
            qms = (jnp.where(lo, q2, jnp.zeros_like(q2)), jnp.where(lo, jnp.zeros_like(q2), q2))
            frefs = (_lane_pick(frow0, lane1, 2 * p), _lane_pick(frow0, lane1, 2 * p + 1))

            def step(j, carry, masked, p=p, sl=sl, qms=qms, frefs=frefs):
                rows = pl.ds(pl.multiple_of(j * t, t), t)
                kb = kn_ref[rows, sl]
                vb = v_ref[rows, sl]
                new = []
                for hh in range(2):
                    m, l, acc = carry[hh]
                    sc = _dot(qms[hh], kb, NT) - (ft_ref[j, 2 * p + hh:2 * p + hh + 1, :] - frefs[hh])
                    if masked:
                        sc = jnp.where(causal, sc, -1e30)
                    m_new = jnp.maximum(m, jnp.max(sc, axis=-1, keepdims=True))
                    alpha = jnp.exp(m - m_new)
                    e = jnp.exp(sc - m_new)
                    l = alpha * l + jnp.sum(e, axis=-1, keepdims=True)
                    acc = alpha * acc + _dot(e.astype(BF16), vb)
                    new.append((m_new, l, acc))
                return tuple(new)

            init = (jnp.full((t, 1), -1e30, F32), jnp.zeros((t, 1), F32), jnp.zeros((t, LANES), F32))
            carry = lax.fori_loop(0, i, functools.partial(step, masked=False), (init, init))
            outs = []
            for hh, (m, l, acc) in enumerate(step(i, carry, masked=True)):
                outs.append(acc * (1.0 / l))
                rcol = jnp.where(lane == 2 * p + hh, m + jnp.log(l) - frefs[hh], rcol)
            o = jnp.where(lo, outs[0], outs[1])
            o_ref[:, sl] = o
            g = gb_ref[:, sl]
            mb_ref[:, sl] = (o * (g * _sig(g))).astype(BF16)
        rcol_ref[...] = rcol

    return pl.pallas_call(
        body, name="fox_fwd", grid=(n,),
        out_shape=(jax.ShapeDtypeStruct((s, FOX_WIDTH), F32), jax.ShapeDtypeStruct((s, FOX_WIDTH), BF16),
                   jax.ShapeDtypeStruct((s, LANES), F32)),
        in_specs=[_rows(t, FOX_WIDTH), _full((s, FOX_WIDTH)), _full((s, FOX_WIDTH)), _full((n, 8, t)),
                  _rows(t, LANES), _rows(t, FOX_WIDTH)],
        out_specs=(_rows(t, FOX_WIDTH), _rows(t, FOX_WIDTH), _rows(t, LANES)),
        compiler_params=_params(),
    )(qs, kn, v, ft, fcol, gb)


def _out_loss(x, tgt, ma, mb, mm, wout):
    s = x.shape[0]
    t = TILE
    n = s // t

    def body(x_ref, t_ref, ma_ref, mb_ref, mm_ref, w_ref, dy_ref, dma_ref, dmb_ref, dmm_ref, dw_ref, loss_ref, mix_ref):
        @pl.when(pl.program_id(0) == 0)
        def _():
            dw_ref[...] = jnp.zeros_like(dw_ref)
            loss_ref[...] = jnp.zeros_like(loss_ref)

        mix_ref[:, 0:256] = ma_ref[...]
        mix_ref[:, 256:768] = mb_ref[...]
        mix_ref[:, 768:1024] = mm_ref[...]
        mix = mix_ref[...]
        err = (x_ref[...] + _dot(mix, w_ref[...])) - t_ref[...]
        row_mean = jnp.sum(err * err, axis=-1, keepdims=True) * (1.0 / D_MODEL)
        loss_ref[...] += 0.5 * jnp.sum(row_mean, axis=0, keepdims=True)
        dy = err * (1.0 / D_MODEL)
        dy_ref[...] = dy
        dyb = dy.astype(BF16)
        dmix = _dot(dyb, w_ref[...], NT)
        dma_ref[...] = dmix[:, 0:256]
        dmb_ref[...] = dmix[:, 256:768]
        dmm_ref[...] = dmix[:, 768:1024]
        dw_ref[...] += _dot(mix, dyb, TN)

    return pl.pallas_call(
        body, name="out_loss", grid=(n,),
        out_shape=(jax.ShapeDtypeStruct((s, D_MODEL), F32), jax.ShapeDtypeStruct((s, 256), F32),
                   jax.ShapeDtypeStruct((s, 512), F32), jax.ShapeDtypeStruct((s, 256), F32),
                   jax.ShapeDtypeStruct((D_MODEL, D_MODEL), F32), jax.ShapeDtypeStruct((1, LANES), F32)),
        in_specs=[_rows(t, D_MODEL), _rows(t, D_MODEL), _rows(t, 256), _rows(t, 512), _rows(t, 256),
                  _full((D_MODEL, D_MODEL))],
        out_specs=(_rows(t, D_MODEL), _rows(t, 256), _rows(t, 512), _rows(t, 256), _full((D_MODEL, D_MODEL)),
                   _full((1, LANES))),
        scratch_shapes=[pltpu.VMEM((t, D_MODEL), BF16)],
        compiler_params=_params(),
    )(x, tgt, ma, mb, mm, wout)


def _mem_attn_bwd(pm, dmm, kmn, vmb, mq_g):
    s = pm.shape[0]
    t = TILE
    n = s // t

    def body(pm_ref, dmm_ref, k_ref, v_ref, g_ref, dpm_ref, dk_ref, dv_ref, dg_ref, gacc_ref):
        @pl.when(pl.program_id(0) == 0)
        def _():
            dk_ref[...] = jnp.zeros_like(dk_ref)
            dv_ref[...] = jnp.zeros_like(dv_ref)
            gacc_ref[...] = jnp.zeros_like(gacc_ref)

        lo = _lane_lo((t, LANES))
        for p in range(MEM_WIDTH // LANES):
            sl = slice(p * LANES, (p + 1) * LANES)
            qb = pm_ref[:, sl]
            rr = _head_rms(qb, lo)
            qhat = qb * rr
            g = g_ref[:, sl]
            qs = ((qhat * g) * ATT_SCALE).astype(BF16)
            gm = pm_ref[:, MEM_WIDTH + p * LANES:MEM_WIDTH + (p + 1) * LANES]
            sg = _sig(gm)
            dmo = dmm_ref[:, sl]
            d_o = dmo * (gm * sg)
            kp = k_ref[:, sl]
            vp = v_ref[:, sl]
            outs, dqs = [], []
            for hh in range(2):
                msk = lo if hh == 0 else jnp.logical_not(lo)
                qm = jnp.where(msk, qs, jnp.zeros_like(qs))
                prob = _mem_softmax(qm, kp)
                pb = prob.astype(BF16)
                outs.append(_dot(pb, vp))
                dom = jnp.where(msk, d_o, 0.0).astype(BF16)
                dp = _dot(dom, vp, NT)
                ds = (prob * (dp - jnp.sum(prob * dp, axis=-1, keepdims=True))).astype(BF16)
                dqs.append(_dot(ds, kp))
                dk_ref[:, sl] += _dot(ds, qm, TN)
                dv_ref[:, sl] += _dot(pb, dom, TN)
            o = jnp.where(lo, outs[0], outs[1])
            dqn = jnp.where(lo, dqs[0], dqs[1]) * ATT_SCALE
            dpm_ref[:, sl] = _head_norm_bwd(dqn, qhat, rr, g, lo).astype(BF16)
            dpm_ref[:, MEM_WIDTH + p * LANES:MEM_WIDTH + (p + 1) * LANES] = (
                dmo * o * (sg * (1.0 + gm * (1.0 - sg)))).astype(BF16)
            gacc_ref[:, sl] += jnp.sum(dqn * qhat, axis=0, keepdims=True)

        @pl.when(pl.program_id(0) == n - 1)
        def _():
            dg_ref[...] = _fold_heads(gacc_ref[...])

    return pl.pallas_call(
        body, name="mem_attn_bwd", grid=(n,),
        out_shape=(jax.ShapeDtypeStruct((s, 512), BF16), jax.ShapeDtypeStruct((N_MEM, MEM_WIDTH), F32),
                   jax.ShapeDtypeStruct((N_MEM, MEM_WIDTH), F32), jax.ShapeDtypeStruct((1, LANES), F32)),
        in_specs=[_rows(t, 512), _rows(t, MEM_WIDTH), _full((N_MEM, MEM_WIDTH)), _full((N_MEM, MEM_WIDTH)),
                  _full((1, MEM_WIDTH))],
        out_specs=(_rows(t, 512), _full((N_MEM, MEM_WIDTH)), _full((N_MEM, MEM_WIDTH)), _full((1, LANES))),
        scratch_shapes=[pltpu.VMEM((1, MEM_WIDTH), F32)],
        compiler_params=_params(),
    )(pm, dmm, kmn, vmb, mq_g)


def _mem_bwd(dkn, dvm, kv, mnb, mem, w_kv, mk_g, mem_norm_g):
    n = mem.shape[0]

    def body(dkn_ref, dvm_ref, kv_ref, mn_ref, mem_ref, w_ref, kg_ref, g_ref, dw_ref, dg_ref, dkg_ref, dkv_ref):
        lo = _lane_lo((n, LANES))
        gacc = []
        for p in range(MEM_WIDTH // LANES):
            sl = slice(p * LANES, (p + 1) * LANES)
            kb = kv_ref[:, sl]
            rr = _head_rms(kb, lo)
            khat = kb * rr
            dk = dkn_ref[:, sl]
            dkv_ref[:, sl] = _head_norm_bwd(dk, khat, rr, kg_ref[:, sl], lo).astype(BF16)
            gacc.append(jnp.sum(dk * khat, axis=0, keepdims=True))
        dkg_ref[...] = _fold_heads(jnp.concatenate(gacc, axis=1))
        dkv_ref[:, MEM_WIDTH:] = dvm_ref[...].astype(BF16)
        dkv = dkv_ref[...]
        dw_ref[...] = _dot(mn_ref[...], dkv, TN)
        dmn = _dot(dkv, w_ref[...], NT)
        xm = mem_ref[...]
        rr = lax.rsqrt(jnp.mean(xm * xm, axis=-1, keepdims=True) + EPS)
        dg_ref[...] = jnp.sum(dmn * (xm * rr), axis=0, keepdims=True)

    return pl.pallas_call(
        body, name="mem_bwd",
        out_shape=(jax.ShapeDtypeStruct((D_MODEL, 2 * MEM_WIDTH), F32), jax.ShapeDtypeStruct((1, D_MODEL), F32),
                   jax.ShapeDtypeStruct((1, LANES), F32)),
        scratch_shapes=[pltpu.VMEM((n, 2 * MEM_WIDTH), BF16)],
        compiler_params=pltpu.CompilerParams(vmem_limit_bytes=VMEM_LIMIT),
    )(dkn, dvm, kv, mnb, mem, w_kv, mk_g, mem_norm_g)


def _pool_bwd(pa, db, dma, w4, pscale):
    s = pa.shape[0]
    t = TILE
    n = s // t
    ext = t + POOL_HALO

    def body(pa_ref, d_ref, dma_ref, w4_ref, sc_ref, dpa_ref, dw4_ref, dsc_ref, ext_ref, w_ref, dw_ref):
        i = pl.program_id(0)

        @pl.when(i == 0)
        def _():
            dw_ref[...] = jnp.zeros_like(dw_ref)
            dsc_ref[...] = jnp.zeros_like(dsc_ref)
            ext_ref[t:ext, :] = jnp.zeros((POOL_HALO, POOL_WIDTH), F32)
            w_ref[...] = _pool_block_diag(w4_ref[...])

        dbv = d_ref[...]
        z = _dot(dbv, w_ref[...])
        ga = pa_ref[:, POOL_WIDTH:2 * POOL_WIDTH]
        sg = _sig(ga)
        dma_v = dma_ref[...]
        dya = dma_v * (ga * sg)
        dpa_ref[:, POOL_WIDTH:2 * POOL_WIDTH] = (dma_v * (z * sc_ref[...]) * (sg * (1.0 + ga * (1.0 - sg)))).astype(BF16)
        dsc_ref[...] += jnp.sum(dya * z, axis=0, keepdims=True)
        dzb = (dya * sc_ref[...]).astype(BF16)
        dw_ref[...] += _dot(dbv, dzb, TN)
        dd = _dot(dzb, w_ref[...], NT)
        lane = lax.broadcasted_iota(jnp.int32, (t, POOL_WIDTH), 1)
        pos = (lax.broadcasted_iota(jnp.int32, (t, POOL_WIDTH), 0) + ((n - 1 - i) * t + 1)).astype(F32)
        ext_ref[0:t, :] = dd / jnp.minimum(pos, _pool_window(lane))
        e = ext_ref[...]
        s2 = e + pltpu.roll(e, ext - 1, axis=0)
        s4 = s2 + pltpu.roll(s2, ext - 2, axis=0)
        s8 = s4 + pltpu.roll(s4, ext - 4, axis=0)
        s16 = s8 + pltpu.roll(s8, ext - 8, axis=0)
        lane_e = lax.broadcasted_iota(jnp.int32, (ext, POOL_WIDTH), 1)
        win = _pool_pick(lane_e, s2, s4, s8, s16)[0:t, :]
        dpa_ref[:, 0:POOL_WIDTH] = (win - dd).astype(BF16)
        ext_ref[t:ext, :] = ext_ref[0:POOL_HALO, :]

        @pl.when(i == n - 1)
        def _():
            own = jnp.where(_same_group((POOL_WIDTH, POOL_WIDTH)), dw_ref[...], 0.0)
            dw4_ref[...] = jnp.dot(own, _group_onehot((POOL_WIDTH, HEAD_DIM), True), preferred_element_type=F32,
                                   precision=lax.Precision.HIGHEST)

    return pl.pallas_call(
        body, name="pool_bwd", grid=(n,),
        out_shape=(jax.ShapeDtypeStruct((s, 512), BF16), jax.ShapeDtypeStruct((POOL_ROWS, HEAD_DIM), F32),
                   jax.ShapeDtypeStruct((1, POOL_WIDTH), F32)),
        in_specs=[_rows_rev(t, 512, n), _rows_rev(t, POOL_WIDTH, n), _rows_rev(t, POOL_WIDTH, n),
                  _full((POOL_ROWS, HEAD_DIM)), _full((1, POOL_WIDTH))],
        out_specs=(_rows_rev(t, 512, n), _full((POOL_ROWS, HEAD_DIM)), _full((1, POOL_WIDTH))),
        scratch_shapes=[pltpu.VMEM((ext, POOL_WIDTH), F32), pltpu.VMEM((POOL_WIDTH, POOL_WIDTH), BF16),
                        pltpu.VMEM((POOL_WIDTH, POOL_WIDTH), F32)],
        compiler_params=_params(),
    )(pa, db, dma, w4, pscale)


def _fox_prep(dmb, gb, o, rcol):
    s = dmb.shape[0]
    t = TILE
    n = s // t

    def body(dmb_ref, gb_ref, o_ref, r_ref, do_ref, dgb_ref, rd_ref):
        lane = lax.broadcasted_iota(jnp.int32, (t, LANES), 1)
        lo = lane < HEAD_DIM
        col = r_ref[...]
        for p in range(FOX_WIDTH // LANES):
            sl = slice(p * LANES, (p + 1) * LANES)
            g = gb_ref[:, sl]
            sg = _sig(g)
            dm = dmb_ref[:, sl]
            ov = o_ref[:, sl]
            d_o = dm * (g * sg)
            do_ref[:, sl] = d_o.astype(BF16)
            dgb_ref[:, sl] = (dm * ov * (sg * (1.0 + g * (1.0 - sg)))).astype(BF16)
            prod = d_o * ov
            col = jnp.where(lane == 8 + 2 * p, jnp.sum(jnp.where(lo, prod, 0.0), axis=-1, keepdims=True), col)
            col = jnp.where(lane == 9 + 2 * p, jnp.sum(jnp.where(lo, 0.0, prod), axis=-1, keepdims=True), col)
        rd_ref[0] = col.T[0:16, :]

    return pl.pallas_call(
        body, name="fox_prep", grid=(n,),
        out_shape=(jax.ShapeDtypeStruct((s, FOX_WIDTH), BF16), jax.ShapeDtypeStruct((s, FOX_WIDTH), BF16),
                   jax.ShapeDtypeStruct((n, 16, t), F32)),
        in_specs=[_rows(t, FOX_WIDTH), _rows(t, FOX_WIDTH), _rows(t, FOX_WIDTH), _rows(t, LANES)],
        out_specs=(_rows(t, FOX_WIDTH), _rows(t, FOX_WIDTH), pl.BlockSpec((1, 16, t), lambda i: (i, 0, 0))),
        compiler_params=_params(),
    )(dmb, gb, o, rcol)


def _fox_bwd(kn, v, fcol, qs, dob, rd):
    s = kn.shape[0]
    t = TILE
    n = s // t

    def body(kn_ref, v_ref, fc_ref, qs_ref, do_ref, rd_ref, dkn_ref, dv_ref, dfc_ref, dqs_ref, drs_ref):
        j = pl.program_id(0)

        @pl.when(j == 0)
        def _():
            dqs_ref[...] = jnp.zeros_like(dqs_ref)
            drs_ref[...] = jnp.zeros_like(drs_ref)

        lane = lax.broadcasted_iota(jnp.int32, (t, LANES), 1)
        lo = lane < HEAD_DIM
        lane1 = lax.broadcasted_iota(jnp.int32, (1, LANES), 1)
        causal = lax.broadcasted_iota(jnp.int32, (t, t), 0) <= lax.broadcasted_iota(jnp.int32, (t, t), 1)
        fc = fc_ref[...]
        frow0 = fc_ref[0:1, :]
        dfc = jnp.zeros((t, LANES), F32)
        for p in range(FOX_WIDTH // LANES):
            sl = slice(p * LANES, (p + 1) * LANES)
            k2 = kn_ref[:, sl]
            v2 = v_ref[:, sl]
            zb = jnp.zeros_like(k2)
            kms = (jnp.where(lo, k2, zb), jnp.where(lo, zb, k2))
            vms = (jnp.where(lo, v2, zb), jnp.where(lo, zb, v2))
            frs = (_lane_pick(frow0, lane1, 2 * p), _lane_pick(frow0, lane1, 2 * p + 1))
            gks = (_lane_pick(fc, lane, 2 * p) - frs[0], _lane_pick(fc, lane, 2 * p + 1) - frs[1])

            def step(i, carry, masked, p=p, sl=sl, kms=kms, vms=vms, frs=frs, gks=gks):
                dk_a, dv_a, cs0, cs1 = carry
                cs = [cs0, cs1]
                rows = pl.ds(pl.multiple_of(i * t, t), t)
                qb = qs_ref[rows, sl]
                d_o = do_ref[rows, sl]
                zq = jnp.zeros_like(qb)
                dq = None
                for hh in range(2):
                    h = 2 * p + hh
                    arg = (_dot(kms[hh], qb, NT) - gks[hh]) - (rd_ref[i, h:h + 1, :] + frs[hh])
                    if masked:
                        arg = jnp.where(causal, arg, -1e30)
                    pt = jnp.exp(arg)
                    dst32 = pt * (_dot(vms[hh], d_o, NT) - rd_ref[i, 8 + h:9 + h, :])
                    cs[hh] = cs[hh] + jnp.sum(dst32, axis=-1, keepdims=True)
                    drs_ref[i, h:h + 1, :] += jnp.sum(dst32, axis=0, keepdims=True)
                    dst = dst32.astype(BF16)
                    dv_a = dv_a + _dot(pt.astype(BF16), jnp.where(lo, d_o, zq) if hh == 0 else jnp.where(lo, zq, d_o))
                    dk_a = dk_a + _dot(dst, jnp.where(lo, qb, zq) if hh == 0 else jnp.where(lo, zq, qb))
                    part = _dot(dst, kms[hh], TN)
                    dq = part if dq is None else dq + part
                dqs_ref[rows, sl] += dq
                return dk_a, dv_a, cs[0], cs[1]

            zero = jnp.zeros((t, LANES), F32)
            zcol = jnp.zeros((t, 1), F32)
            carry = step(j, (zero, zero, zcol, zcol), masked=True)
            dk_a, dv_a, cs0, cs1 = lax.fori_loop(j + 1, n, functools.partial(step, masked=False), carry)
            dkn_ref[:, sl] = dk_a
            dv_ref[:, sl] = dv_a.astype(BF16)
            dfc = jnp.where(lane == 2 * p, cs0, jnp.where(lane == 2 * p + 1, cs1, dfc))
        dfc_ref[...] = dfc

    return pl.pallas_call(
        body, name="fox_bwd", grid=(n,),
        out_shape=(jax.ShapeDtypeStruct((s, FOX_WIDTH), F32), jax.ShapeDtypeStruct((s, FOX_WIDTH), BF16),
                   jax.ShapeDtypeStruct((s, LANES), F32), jax.ShapeDtypeStruct((s, FOX_WIDTH), F32),
                   jax.ShapeDtypeStruct((n, 8, t), F32)),
        in_specs=[_rows(t, FOX_WIDTH), _rows(t, FOX_WIDTH), _rows(t, LANES), _full((s, FOX_WIDTH)),
                  _full((s, FOX_WIDTH)), _full((n, 16, t))],
        out_specs=(_rows(t, FOX_WIDTH), _rows(t, FOX_WIDTH), _rows(t, LANES), _full((s, FOX_WIDTH)),
                   _full((n, 8, t))),
        compiler_params=_params(),
    )(kn, v, fcol, qs, dob, rd)


def _fox_post(dqs, dkn, dfc, drs, qk, fb, bf_pad, fq_g, fk_g):
    s = dqs.shape[0]
    t = TILE
    n = s // t

    def body(dqs_ref, dkn_ref, dfc_ref, drs_ref, qk_ref, fb_ref, bf_ref, qg_ref, kg_ref,
             dqk_ref, dfb_ref, dqg_ref, dkg_ref, dbf_ref, qacc_ref, kacc_ref, carry_ref):
        i = pl.program_id(0)

        @pl.when(i == 0)
        def _():
            qacc_ref[...] = jnp.zeros_like(qacc_ref)
            kacc_ref[...] = jnp.zeros_like(kacc_ref)
            dbf_ref[...] = jnp.zeros_like(dbf_ref)
            carry_ref[...] = jnp.zeros_like(carry_ref)

        lo = _lane_lo((t, LANES))
        for off, d_ref, g_ref, acc_ref, scale in ((0, dqs_ref, qg_ref, qacc_ref, ATT_SCALE),
                                                  (FOX_WIDTH, dkn_ref, kg_ref, kacc_ref, 1.0)):
            for p in range(FOX_WIDTH // LANES):
                sl = slice(p * LANES, (p + 1) * LANES)
                raw = qk_ref[:, off + p * LANES:off + (p + 1) * LANES]
                rr = _head_rms(raw, lo)
                xhat = raw * rr
                dn = d_ref[:, sl] * scale
                dqk_ref[:, off + p * LANES:off + (p + 1) * LANES] = _head_norm_bwd(
                    dn, xhat, rr, g_ref[:, sl], lo).astype(BF16)
                acc_ref[:, sl] += jnp.sum(dn * xhat, axis=0, keepdims=True)

        lane = lax.broadcasted_iota(jnp.int32, (t, LANES), 1)
        row = lax.broadcasted_iota(jnp.int32, (t, LANES), 0)
        rows_h = jnp.concatenate([drs_ref[0], jnp.zeros((LANES - FOX_HEADS, t), F32)], axis=0)
        acc = rows_h.T - dfc_ref[...]
        sh = 1
        while sh < t:
            acc = acc + jnp.where(row < t - sh, pltpu.roll(acc, t - sh, axis=0), 0.0)
            sh *= 2
        dlogf = acc + carry_ref[...]
        dfb_ref[...] = dlogf
        carry_ref[...] = dfb_ref[0:1, :]
        z = fb_ref[...] + bf_ref[...]
        dz = jnp.where(lane < FOX_HEADS, dlogf * (1.0 / (1.0 + jnp.exp(z))), 0.0)
        dfb_ref[...] = dz
        dbf_ref[...] += jnp.sum(dz, axis=0, keepdims=True)

        @pl.when(i == n - 1)
        def _():
            dqg_ref[...] = _fold_heads(qacc_ref[...])
            dkg_ref[...] = _fold_heads(kacc_ref[...])

    return pl.pallas_call(
        body, name="fox_post", grid=(n,),
        out_shape=(jax.ShapeDtypeStruct((s, 2 * FOX_WIDTH), BF16), jax.ShapeDtypeStruct((s, LANES), F32),
                   jax.ShapeDtypeStruct((1, LANES), F32), jax.ShapeDtypeStruct((1, LANES), F32),
                   jax.ShapeDtypeStruct((1, LANES), F32)),
        in_specs=[_rows_rev(t, FOX_WIDTH, n), _rows_rev(t, FOX_WIDTH, n), _rows_rev(t, LANES, n),
                  pl.BlockSpec((1, FOX_HEADS, t), lambda i: (n - 1 - i, 0, 0)),
                  _rows_rev(t, 2 * FOX_WIDTH, n), _rows_rev(t, LANES, n), _full((1, LANES)),
                  _full((1, FOX_WIDTH)), _full((1, FOX_WIDTH))],
        out_specs=(_rows_rev(t, 2 * FOX_WIDTH, n), _rows_rev(t, LANES, n), _full((1, LANES)), _full((1, LANES)),
                   _full((1, LANES))),
        scratch_shapes=[pltpu.VMEM((1, FOX_WIDTH), F32), pltpu.VMEM((1, FOX_WIDTH), F32), pltpu.VMEM((1, LANES), F32)],
        compiler_params=_params(),
    )(dqs, dkn, dfc, drs, qk, fb, bf_pad, fq_g, fk_g)


def _assemble_dproj(dp_ref, dpa_ref, dqk_ref, dv_ref, dgb_ref, dpm_ref, dfb_ref):
    dp_ref[:, PA_LO:QB_LO] = dpa_ref[...]
    dp_ref[:, QB_LO:VB_LO] = dqk_ref[...]
    dp_ref[:, VB_LO:GB_LO] = dv_ref[...]
    dp_ref[:, GB_LO:PM_LO] = dgb_ref[...]
    dp_ref[:, PM_LO:FB_LO] = dpm_ref[...]
    dp_ref[:, FB_LO:PROJ_PAD] = dfb_ref[...].astype(BF16)


def _dproj_specs(t):
    return [_rows(t, 512), _rows(t, 2 * FOX_WIDTH), _rows(t, FOX_WIDTH), _rows(t, FOX_WIDTH), _rows(t, 512),
            _rows(t, LANES)]


def _in_bwd_x(x, dy, norm_g, wp, dparts):
    s = x.shape[0]
    t = TILE
    n = s // t

    def body(x_ref, dy_ref, g_ref, wp_ref, dpa_ref, dqk_ref, dv_ref, dgb_ref, dpm_ref, dfb_ref, gx_ref, dg_ref, dp_ref):
        @pl.when(pl.program_id(0) == 0)
        def _():
            dg_ref[...] = jnp.zeros_like(dg_ref)

        _assemble_dproj(dp_ref, dpa_ref, dqk_ref, dv_ref, dgb_ref, dpm_ref, dfb_ref)
        dh = _dot(dp_ref[...], wp_ref[...])
        xv = x_ref[...]
        rr = lax.rsqrt(jnp.mean(xv * xv, axis=-1, keepdims=True) + EPS)
        xhat = xv * rr
        a = dh * g_ref[...]
        gx_ref[...] = dy_ref[...] + rr * (a - xhat * jnp.mean(xhat * a, axis=-1, keepdims=True))
        dg_ref[...] += jnp.sum(dh * xhat, axis=0, keepdims=True)

    return pl.pallas_call(
        body, name="in_bwd_x", grid=(n,),
        out_shape=(jax.ShapeDtypeStruct((s, D_MODEL), F32), jax.ShapeDtypeStruct((1, D_MODEL), F32)),
        in_specs=[_rows(t, D_MODEL), _rows(t, D_MODEL), _full((1, D_MODEL)), _full((PROJ_PAD, D_MODEL))] + _dproj_specs(t),
        out_specs=(_rows(t, D_MODEL), _full((1, D_MODEL))),
        scratch_shapes=[pltpu.VMEM((t, PROJ_PAD), BF16)],
        compiler_params=_params(),
    )(x, dy, norm_g, wp, *dparts)


def _in_bwd_w(hb, dparts):
    s = hb.shape[0]
    t = TILE
    n = s // t

    def body(h_ref, dpa_ref, dqk_ref, dv_ref, dgb_ref, dpm_ref, dfb_ref, dw_ref, dp_ref):
        @pl.when(pl.program_id(0) == 0)
        def _():
            dw_ref[...] = jnp.zeros_like(dw_ref)

        _assemble_dproj(dp_ref, dpa_ref, dqk_ref, dv_ref, dgb_ref, dpm_ref, dfb_ref)
        dw_ref[...] += _dot(dp_ref[...], h_ref[...], TN)

    return pl.pallas_call(
        body, name="in_bwd_w", grid=(n,),
        out_shape=jax.ShapeDtypeStruct((PROJ_PAD, D_MODEL), F32),
        in_specs=[_rows(t, D_MODEL)] + _dproj_specs(t),
        out_specs=_full((PROJ_PAD, D_MODEL)),
        scratch_shapes=[pltpu.VMEM((t, PROJ_PAD), BF16)],
        compiler_params=_params(),
    )(hb, *dparts)


def _adamw_math(w_ref, gv, m_ref, v_ref, d_ref, nm_ref, nv_ref):
    nm = ADAM_B1 * m_ref[...] + (1.0 - ADAM_B1) * gv
    nv = ADAM_B2 * v_ref[...] + (1.0 - ADAM_B2) * (gv * gv)
    m_hat = nm / (1.0 - ADAM_B1 ** ADAM_STEP)
    v_hat = nv / (1.0 - ADAM_B2 ** ADAM_STEP)
    d_ref[...] = -ADAM_LR * (m_hat / (jnp.sqrt(v_hat) + ADAM_EPS) + ADAM_WD * w_ref[...])
    nm_ref[...] = nm
    nv_ref[...] = nv


def _adamw(name, w, g, m, v):
    rows, cols = w.shape
    tc = 256 if rows * cols > 256 * 1024 else cols
    n = cols // tc

    def body(w_ref, g_ref, m_ref, v_ref, d_ref, nm_ref, nv_ref):
        _adamw_math(w_ref, g_ref[...], m_ref, v_ref, d_ref, nm_ref, nv_ref)

    spec = pl.BlockSpec((rows, tc), lambda i: (0, i))
    return pl.pallas_call(
        body, name=name, grid=(n,),
        out_shape=(jax.ShapeDtypeStruct((rows, cols), F32),) * 3,
        in_specs=[spec] * 4, out_specs=(spec,) * 3,
        compiler_params=_params(),
    )(w, g, m, v)


def _adamw_small(vec, dw4, leaves, pool):
    nl = len(VEC_LEAVES) + 1

    def body(*refs):
        vec_ref, dw4_ref = refs[0:2]
        wmv = refs[2:2 + 3 * nl]
        loss_ref = refs[2 + 3 * nl]
        outs = refs[3 + 3 * nl:]
        loss_ref[...] = vec_ref[VEC_LOSS_ROW:VEC_LOSS_ROW + 1, 0:1]
        for k in range(nl):
            if k < nl - 1:
                _, row, width = VEC_LEAVES[k]
                gv = vec_ref[row:row + 1, 0:width]
            else:
                gv = dw4_ref[...]
            w_ref, m_ref, v_ref = wmv[3 * k:3 * k + 3]
            g_ref, d_ref, nm_ref, nv_ref = outs[4 * k:4 * k + 4]
            g_ref[...] = gv
            _adamw_math(w_ref, gv, m_ref, v_ref, d_ref, nm_ref, nv_ref)

    shapes = [jax.ShapeDtypeStruct((1, width), F32) for _, _, width in VEC_LEAVES] + [
        jax.ShapeDtypeStruct(dw4.shape, F32)]
    flat_in = [a for triple in list(leaves) + [pool] for a in triple]
    res = pl.pallas_call(
        body, name="adamw_small",
        out_shape=(jax.ShapeDtypeStruct((1, 1), F32),) + tuple(s for s in shapes for _ in range(4)),
        compiler_params=pltpu.CompilerParams(vmem_limit_bytes=VMEM_LIMIT),
    )(vec, dw4, *flat_in)
    per = [res[1 + 4 * k:5 + 4 * k] for k in range(nl)]
    return res[0], [p[0] for p in per], [p[1] for p in per], [p[2] for p in per], [p[3] for p in per]


def _full_w_in_padded(halves):
    cols = IN_WIDTH // 4
    w_t = halves.reshape(4, 2, cols, D_MODEL // 2).transpose(0, 2, 1, 3).reshape(IN_WIDTH, D_MODEL)
    return jnp.concatenate([
        w_t[0:F_ORIG_LO], w_t[F_ORIG_LO + FOX_HEADS:], w_t[F_ORIG_LO:F_ORIG_LO + FOX_HEADS],
        jnp.zeros((PROJ_PAD - IN_WIDTH, D_MODEL), w_t.dtype)], axis=0)


def _shard_padded_rows(dwp):
    full = jnp.concatenate([dwp[0:F_ORIG_LO], dwp[FB_LO:FB_LO + FOX_HEADS], dwp[F_ORIG_LO:FB_LO]], axis=0)
    return full.reshape(4, IN_WIDTH // 4, D_MODEL)


def _tile_heads(g, n):
    return jnp.tile(g.reshape(1, HEAD_DIM), (1, n))


def kernel(x, mem, norm_g, w_in, b_f, w_pool, pool_scale, fox_q_g, fox_k_g, mem_norm_g, w_mem_kv, mem_q_g, mem_k_g, w_out, loss_target, m_norm_g, m_w_in, m_b_f, m_w_pool, m_pool_scale, m_fox_q_g, m_fox_k_g, m_mem_norm_g, m_w_mem_kv, m_mem_q_g, m_mem_k_g, m_w_out, v_norm_g, v_w_in, v_b_f, v_w_pool, v_pool_scale, v_fox_q_g, v_fox_k_g, v_mem_norm_g, v_w_mem_kv, v_mem_q_g, v_mem_k_g, v_w_out):
    w_in_t, m_w_in_t, v_w_in_t = w_in[0].T, m_w_in[0].T, v_w_in[0].T
    axes = (1, 0, 0)

    g_in, g_kv, g_out = _all_gather_weights([w_in_t, w_mem_kv[0], w_out[0]], axes)
    w_kv_b = g_kv.reshape(D_MODEL, 2 * MEM_WIDTH)
    w_out_b = g_out.reshape(D_MODEL, D_MODEL)
    w4 = w_pool.reshape(POOL_ROWS, HEAD_DIM)
    grad_x, dwp, dw_kv, dw_out, vec_leaves, loss_row, dw4 = _local_grads(
        x[0], mem[0], loss_target[0], g_in, w_kv_b, w_out_b, norm_g, b_f, w4, pool_scale, fox_q_g, fox_k_g,
        mem_norm_g, mem_q_g, mem_k_g)

    gparts = [_shard_padded_rows(dwp), dw_kv.reshape(4, D_MODEL // 4, 2 * MEM_WIDTH),
              dw_out.reshape(4, D_MODEL // 4, D_MODEL)]
    g_w_in_t, g_w_kv, g_w_out, vec, dw4_sum = _grad_reduce(gparts, axes, vec_leaves, loss_row, dw4)

    small_wmv = [(norm_g, m_norm_g, v_norm_g), (mem_norm_g, m_mem_norm_g, v_mem_norm_g),
                 (pool_scale, m_pool_scale, v_pool_scale), (b_f, m_b_f, v_b_f), (fox_q_g, m_fox_q_g, v_fox_q_g),
                 (fox_k_g, m_fox_k_g, v_fox_k_g), (mem_q_g, m_mem_q_g, v_mem_q_g), (mem_k_g, m_mem_k_g, v_mem_k_g)]
    pool_wmv = tuple(a.reshape(POOL_ROWS, HEAD_DIM) for a in (w_pool, m_w_pool, v_w_pool))
    loss, *small_out = _adamw_small(vec, dw4_sum, small_wmv, pool_wmv)
    big = [[g_w_in_t.T[None], g_w_kv[None], g_w_out[None]]]
    upd = [[a.T for a in _adamw("adamw_w_in", w_in_t, g_w_in_t, m_w_in_t, v_w_in_t)],
           _adamw("adamw_w_mem_kv", w_mem_kv[0], g_w_kv, m_w_mem_kv[0], v_w_mem_kv[0]),
           _adamw("adamw_w_out", w_out[0], g_w_out, m_w_out[0], v_w_out[0])]
    big += [[u[k][None] for u in upd] for k in range(3)]

    def leaves(k):
        sm = small_out[k]
        b_in, b_kv, b_out = big[k]
        return (sm[0], b_in, sm[3], sm[8].reshape(w_pool.shape), sm[2], sm[4], sm[5], sm[1], b_kv, sm[6], sm[7], b_out)

    return (loss.reshape(()), grad_x[None], *leaves(0), *leaves(1), *leaves(2), *leaves(3))


def _local_grads(xs, mems, tgt, w_in_b, w_kv_b, w_out_b, norm_g, b_f, w4, pool_scale, fox_q_g, fox_k_g,
                 mem_norm_g, mem_q_g, mem_k_g):
    wp = _full_w_in_padded(w_in_b)
    bf_pad = jnp.pad(b_f, ((0, 0), (0, LANES - FOX_HEADS)))
    fq_g, fk_g = _tile_heads(fox_q_g, FOX_HEADS), _tile_heads(fox_k_g, FOX_HEADS)
    mq_g, mk_g = _tile_heads(mem_q_g, 4), _tile_heads(mem_k_g, 4)

    mnb, kv, kmn, vmb = _mem_fwd(mems, mem_norm_g, w_kv_b, mk_g)
    hb, pa, qk, qs, kn, vb, gb, pm, fb, fcol, ft = _fwd_in(xs, norm_g, wp, bf_pad, fq_g, fk_g)
    ma, db = _pool_fwd(pa, w4, pool_scale)
    mm = _mem_attn_fwd(pm, kmn, vmb, mq_g)
    o, mb, rcol = _fox_fwd(qs, kn, vb, ft, fcol, gb)
    dy, dma, dmb, dmm, dw_out, loss_row = _out_loss(xs, tgt, ma, mb, mm, w_out_b)

    dpm, dkmn, dvm, dmq_g = _mem_attn_bwd(pm, dmm, kmn, vmb, mq_g)
    dw_kv, dmemnorm_g, dmk_g = _mem_bwd(dkmn, dvm, kv, mnb, mems, w_kv_b, mk_g, mem_norm_g)
    dpa, dw4, dpscale = _pool_bwd(pa, db, dma, w4, pool_scale)
    dob, dgb, rd = _fox_prep(dmb, gb, o, rcol)
    dkn, dvb, dfc, dqs, drs = _fox_bwd(kn, vb, fcol, qs, dob, rd)
    dqk, dfb, dfq_g, dfk_g, dbf = _fox_post(dqs, dkn, dfc, drs, qk, fb, bf_pad, fq_g, fk_g)
    dparts = (dpa, dqk, dvb, dgb, dpm, dfb)
    grad_x, dnorm_g = _in_bwd_x(xs, dy, norm_g, wp, dparts)
    dwp = _in_bwd_w(hb, dparts)

    vec_leaves = (dnorm_g, dmemnorm_g, dpscale, dbf, dfq_g, dfk_g, dmq_g, dmk_g)
    return grad_x, dwp, dw_kv, dw_out, vec_leaves, loss_row, dw4
```

```python
import functools

import jax
import jax.numpy as jnp
from jax import lax
from jax.experimental import pallas as pl
from jax.experimental.pallas import tpu as pltpu

F32 = jnp.float32
BF16 = jnp.bfloat16
MESH = pl.DeviceIdType.MESH

D_MODEL = 1024
HEAD_DIM = 64
POOL_WIDTH = 256
FOX_WIDTH = 512
FOX_HEADS = 8
MEM_WIDTH = 256
N_MEM = 256
IN_WIDTH = 3080
EPS = 1e-6
ATT_SCALE = 0.125

ADAM_LR = 0.001
ADAM_B1 = 0.9
ADAM_B2 = 0.999
ADAM_EPS = 1e-08
ADAM_WD = 0.01
ADAM_STEP = 10

LANES = 128
PA_LO, QB_LO, KB_LO, VB_LO, GB_LO, PM_LO, FB_LO, PROJ_PAD = 0, 512, 1024, 1536, 2048, 2560, 3072, 3200
F_ORIG_LO = 2048

TILE = 512
VMEM_LIMIT = 56 * 1024 * 1024

VEC_LEAVES = (("norm_g", 0, 1024), ("mem_norm_g", 1, 1024), ("pool_scale", 2, 256), ("b_f", 3, 8),
              ("fox_q_g", 4, 64), ("fox_k_g", 5, 64), ("mem_q_g", 6, 64), ("mem_k_g", 7, 64))
VEC_LOSS_ROW = 8
VEC_ROWS = 16
POOL_ROWS = 256


def _params(n_grid=1, vmem=VMEM_LIMIT):
    return pltpu.CompilerParams(dimension_semantics=("arbitrary",) * n_grid, vmem_limit_bytes=vmem)


def _rows(t, w):
    return pl.BlockSpec((t, w), lambda i: (i, 0))


def _rows_rev(t, w, n):
    return pl.BlockSpec((t, w), lambda i: (n - 1 - i, 0))


def _full(shape):
    return pl.BlockSpec(shape, lambda i: (0,) * len(shape))


def _sig(x):
    return 1.0 / (1.0 + jnp.exp(-x))


def _lane_lo(shape):
    return lax.broadcasted_iota(jnp.int32, shape, 1) < HEAD_DIM


def _pair_sum(v, lo):
    s0 = jnp.sum(jnp.where(lo, v, 0.0), axis=-1, keepdims=True)
    s1 = jnp.sum(jnp.where(lo, 0.0, v), axis=-1, keepdims=True)
    return jnp.where(lo, s0, s1)


def _head_rms(blk, lo):
    return lax.rsqrt(_pair_sum(blk * blk, lo) * (1.0 / HEAD_DIM) + EPS)


def _head_norm_bwd(dyn, xhat, rr, g, lo):
    a = dyn * g
    return rr * (a - xhat * (_pair_sum(xhat * a, lo) * (1.0 / HEAD_DIM)))


def _fold_heads(acc):
    tot = acc[:, 0:LANES]
    for p in range(1, acc.shape[1] // LANES):
        tot = tot + acc[:, p * LANES:(p + 1) * LANES]
    return tot + pltpu.roll(tot, HEAD_DIM, axis=1)


def _lane_pick(v, lane, idx):
    return jnp.sum(jnp.where(lane == idx, v, 0.0), axis=-1, keepdims=True)


NT = (((1,), (1,)), ((), ()))
TN = (((0,), (0,)), ((), ()))


def _dot(a, b, dims=None):
    if dims is None:
        return jnp.dot(a, b, preferred_element_type=F32)
    return lax.dot_general(a, b, dims, preferred_element_type=F32)


def _my_place():
    return lax.axis_index("x"), lax.axis_index("y"), lax.axis_index("c")


def _half_dims(shape, axis):
    return (shape[0] // 2, shape[1]) if axis == 0 else (shape[0], shape[1] // 2)


def _half_of(ref, axis, core, lead=False):
    rows, cols = ref.shape[-2:]
    if axis == 0:
        idx = (pl.ds(pl.multiple_of(core * (rows // 2), 16), rows // 2), slice(None))
    else:
        idx = (slice(None), pl.ds(pl.multiple_of(core * (cols // 2), LANES), cols // 2))
    return ref.at[(slice(None),) + idx] if lead else ref.at[idx]


def _all_gather_weights(shards, axes):
    n = len(shards)
    dims = [_half_dims(a.shape, axis) for a, axis in zip(shards, axes)]

    def body(*refs):
        ins, outs = refs[0:n], refs[n:2 * n]
        f32_bufs, bf_bufs = refs[2 * n:3 * n], refs[3 * n:4 * n]
        send_sems, recv_sems, local_sems = refs[4 * n:]
        x, y, c = _my_place()
        me, sibling = (x, y, c), (x, y, 1 - c)
        chips = [(1 - x, y), (x, 1 - y), (1 - x, 1 - y)]

        loads = []
        for a in range(n):
            cp = pltpu.make_async_copy(_half_of(ins[a], axes[a], c), f32_bufs[a], local_sems.at[a])
            cp.start()
            loads.append(cp)

        def blk(a, px, py, pc):
            return outs[a].at[4 * px + 2 * py + pc]

        def copy(a, k, block, to, src=None):
            return pltpu.make_async_remote_copy(
                src_ref=blk(a, *block) if src is None else src, dst_ref=blk(a, *block),
                send_sem=send_sems.at[7 * a + k], recv_sem=recv_sems.at[7 * a + k], device_id=to, device_id_type=MESH)

        first, keeps = [], []
        for a in range(n):
            loads[a].wait()
            bf_bufs[a][...] = f32_bufs[a][...].astype(BF16)
            keep = pltpu.make_async_copy(bf_bufs[a], blk(a, *me), local_sems.at[n + a])
            keep.start()
            keeps.append(keep)
            mine = [copy(a, 0, me, sibling, src=bf_bufs[a])]
            mine += [copy(a, 1 + j, me, (*chip, c), src=bf_bufs[a]) for j, chip in enumerate(chips)]
            for cp in mine:
                cp.start()
            first += mine
        passed = []
        for a in range(n):
            for j, chip in enumerate(chips):
                copy(a, 1 + j, (*chip, c), me).wait_recv()
                cp = copy(a, 4 + j, (*chip, c), sibling)
                cp.start()
                passed.append(cp)
        for a in range(n):
            copy(a, 0, sibling, me).wait_recv()
            for j, chip in enumerate(chips):
                copy(a, 4 + j, (*chip, 1 - c), me).wait_recv()
        for cp in first + passed:
            cp.wait_send()
        for keep in keeps:
            keep.wait()

    any_spec = pl.BlockSpec(memory_space=pl.ANY)
    return pl.pallas_call(
        body, name="weights_all_gather",
        out_shape=tuple(jax.ShapeDtypeStruct((8, h, w), BF16) for h, w in dims),
        in_specs=[any_spec] * n, out_specs=(any_spec,) * n,
        scratch_shapes=[pltpu.VMEM(d, F32) for d in dims] + [pltpu.VMEM(d, BF16) for d in dims] + [
            pltpu.SemaphoreType.DMA((7 * n,)), pltpu.SemaphoreType.DMA((7 * n,)), pltpu.SemaphoreType.DMA((2 * n,))],
        compiler_params=pltpu.CompilerParams(vmem_limit_bytes=VMEM_LIMIT),
    )(*shards)


def _grad_reduce(gparts, axes, vec_leaves, loss_row, dw4):
    n = len(gparts)
    dims = [_half_dims(g.shape[1:], axis) for g, axis in zip(gparts, axes)]
    nv = len(vec_leaves)

    def body(*refs):
        g_refs = refs[0:n]
        leaf_refs = refs[n:n + nv]
        loss_ref, dw4_ref = refs[n + nv:n + nv + 2]
        o = n + nv + 2
        out_refs = refs[o:o + n]
        vec_out, dw4_out = refs[o + n:o + n + 2]
        s0 = o + n + 2
        recv_a, own_a = refs[s0:s0 + n], refs[s0 + n:s0 + 2 * n]
        send_b, recv_b = refs[s0 + 2 * n:s0 + 3 * n], refs[s0 + 3 * n:s0 + 4 * n]
        fin = refs[s0 + 4 * n:s0 + 5 * n]
        vec_mine, vec_recv, dw4_recv, send_sems, recv_sems, local_sems = refs[s0 + 5 * n:]

        x, y, c = _my_place()
        chip = 2 * x + y
        me_lin = 4 * x + 2 * y + c
        sibling = (x, y, 1 - c)

        to_sib, own = [], []
        for a in range(n):
            cp = pltpu.make_async_remote_copy(
                src_ref=_half_of(g_refs[a], axes[a], 1 - c, lead=True), dst_ref=recv_a[a], send_sem=send_sems.at[5 * a],
                recv_sem=recv_sems.at[5 * a], device_id=sibling, device_id_type=MESH)
            cp.start()
            to_sib.append(cp)
            cp = pltpu.make_async_copy(_half_of(g_refs[a], axes[a], c, lead=True), own_a[a], local_sems.at[a])
            cp.start()
            own.append(cp)

        vec_mine[...] = jnp.zeros_like(vec_mine)
        for (_, row, width), ref in zip(VEC_LEAVES, leaf_refs):
            vec_mine[row:row + 1, 0:ref.shape[1]] = ref[...]
        vec_mine[VEC_LOSS_ROW:VEC_LOSS_ROW + 1, 0:LANES] = loss_ref[...]
        small_copies = []
        for k in range(1, 8):
            peer = (me_lin + k) % 8
            to = (peer // 4, (peer // 2) % 2, peer % 2)
            for src, dst, base in ((vec_mine, vec_recv, 5 * n), (dw4_ref, dw4_recv, 5 * n + 7)):
                cp = pltpu.make_async_remote_copy(
                    src_ref=src, dst_ref=dst.at[me_lin], send_sem=send_sems.at[base + k - 1],
                    recv_sem=recv_sems.at[base + k - 1], device_id=to, device_id_type=MESH)
                cp.start()
                small_copies.append(cp)

        chip_copies = []
        for a in range(n):
            own[a].wait()
            to_sib[a].wait_recv()
            for j in range(4):
                send_b[a][j] = (own_a[a][j] + recv_a[a][j]).astype(BF16)
            for k in range(1, 4):
                dest = (chip + k) % 4
                cp = pltpu.make_async_remote_copy(
                    src_ref=send_b[a].at[dest], dst_ref=recv_b[a].at[chip], send_sem=send_sems.at[5 * a + k],
                    recv_sem=recv_sems.at[5 * a + k], device_id=(dest // 2, dest % 2, c), device_id_type=MESH)
                cp.start()
                chip_copies.append(cp)
            keep = pltpu.make_async_copy(send_b[a].at[chip], recv_b[a].at[chip], local_sems.at[n + a])
            keep.start()
            keep.wait()

        give, mine = [], []
        for a in range(n):
            for cp in chip_copies[3 * a:3 * a + 3]:
                cp.wait_recv()
            tot = recv_b[a][0].astype(F32) + recv_b[a][1].astype(F32)
            tot = tot + recv_b[a][2].astype(F32)
            fin[a][...] = tot + recv_b[a][3].astype(F32)
            cp = pltpu.make_async_remote_copy(
                src_ref=fin[a], dst_ref=_half_of(out_refs[a], axes[a], c), send_sem=send_sems.at[5 * a + 4],
                recv_sem=recv_sems.at[5 * a + 4], device_id=sibling, device_id_type=MESH)
            cp.start()
            give.append(cp)
            cp = pltpu.make_async_copy(fin[a], _half_of(out_refs[a], axes[a], c), local_sems.at[a])
            cp.start()
            mine.append(cp)

        for cp in small_copies:
            cp.wait_recv()
        vec_recv[me_lin] = vec_mine[...]
        dw4_recv[me_lin] = dw4_ref[...]
        vtot, wtot = vec_recv[0], dw4_recv[0]
        for d in range(1, 8):
            vtot = vtot + vec_recv[d]
            wtot = wtot + dw4_recv[d]
        vec_out[...] = vtot
        dw4_out[...] = wtot

        for a in range(n):
            give[a].wait_recv()
            mine[a].wait()
            to_sib[a].wait_send()
            give[a].wait_send()
        for cp in chip_copies + small_copies:
            cp.wait_send()

    any_spec = pl.BlockSpec(memory_space=pl.ANY)
    vmem_spec = pl.BlockSpec(memory_space=pltpu.VMEM)
    n_sems = 5 * n + 14
    scratch = []
    for dtype, lead in ((F32, (4,)), (F32, (4,)), (BF16, (4,)), (BF16, (4,)), (F32, ())):
        scratch += [pltpu.VMEM(lead + d, dtype) for d in dims]
    scratch += [pltpu.VMEM((VEC_ROWS, D_MODEL), F32), pltpu.VMEM((8, VEC_ROWS, D_MODEL), F32),
                pltpu.VMEM((8,) + dw4.shape, F32),
                pltpu.SemaphoreType.DMA((n_sems,)), pltpu.SemaphoreType.DMA((n_sems,)), pltpu.SemaphoreType.DMA((2 * n,))]
    return pl.pallas_call(
        body, name="grad_reduce",
        out_shape=tuple(jax.ShapeDtypeStruct(g.shape[1:], F32) for g in gparts) + (
            jax.ShapeDtypeStruct((VEC_ROWS, D_MODEL), F32), jax.ShapeDtypeStruct(dw4.shape, F32)),
        in_specs=[any_spec] * n + [vmem_spec] * (nv + 2),
        out_specs=(any_spec,) * n + (vmem_spec, vmem_spec),
        scratch_shapes=scratch,
        compiler_params=pltpu.CompilerParams(vmem_limit_bytes=VMEM_LIMIT),
    )(*gparts, *vec_leaves, loss_row, dw4)


def _mem_fwd(mem, mem_norm_g, w_kv, mk_g):
    n = mem.shape[0]

    def body(mem_ref, g_ref, w_ref, kg_ref, mn_ref, kv_ref, kn_ref, vm_ref):
        xm = mem_ref[...]
        rr = lax.rsqrt(jnp.mean(xm * xm, axis=-1, keepdims=True) + EPS)
        mnb = ((xm * rr) * g_ref[...]).astype(BF16)
        mn_ref[...] = mnb
        kv = _dot(mnb, w_ref[...])
        kv_ref[...] = kv
        lo = _lane_lo((n, LANES))
        for p in range(MEM_WIDTH // LANES):
            sl = slice(p * LANES, (p + 1) * LANES)
            kb = kv[:, sl]
            kn_ref[:, sl] = ((kb * _head_rms(kb, lo)) * kg_ref[:, sl]).astype(BF16)
        vm_ref[...] = kv[:, MEM_WIDTH:].astype(BF16)

    return pl.pallas_call(
        body, name="mem_fwd",
        out_shape=(jax.ShapeDtypeStruct((n, D_MODEL), BF16), jax.ShapeDtypeStruct((n, 2 * MEM_WIDTH), F32),
                   jax.ShapeDtypeStruct((n, MEM_WIDTH), BF16), jax.ShapeDtypeStruct((n, MEM_WIDTH), BF16)),
        compiler_params=pltpu.CompilerParams(vmem_limit_bytes=VMEM_LIMIT),
    )(mem, mem_norm_g, w_kv, mk_g)


AUG_LO = 64
SUM_LANE = 67
HEAD_BLOCKS = FOX_HEADS * LANES


def _ones3(lane):
    return jnp.where((lane >= AUG_LO) & (lane < AUG_LO + 3), 1.0, 0.0)


def _split3(col, lane):
    hi = col.astype(BF16).astype(F32)
    rest = col - hi
    mid = rest.astype(BF16).astype(F32)
    return jnp.where(lane == AUG_LO, hi, jnp.where(lane == AUG_LO + 1, mid, jnp.where(lane == AUG_LO + 2, rest - mid, 0.0)))


def _head_block(pair_blk, hh, lo, extras):
    src = pair_blk if hh == 0 else pltpu.roll(pair_blk, HEAD_DIM, axis=1)
    return jnp.where(lo, src, extras).astype(BF16)


def _pair_block(blk0, blk1, lo):
    return jnp.where(lo, blk0, pltpu.roll(blk1, HEAD_DIM, axis=1))


def _fwd_in(x, norm_g, wp, bf_pad, fq_g, fk_g):
    s = x.shape[0]
    t = TILE
    n = s // t

    def body(x_ref, ng_ref, wp_ref, bf_ref, qg_ref, kg_ref,
             h_ref, pa_ref, qk_ref, qa_ref, ka_ref, va_ref, gb_ref, pm_ref, fb_ref, carry_ref, fcol_ref):
        @pl.when(pl.program_id(0) == 0)
        def _():
            carry_ref[...] = jnp.zeros_like(carry_ref)

        xv = x_ref[...]
        rr = lax.rsqrt(jnp.mean(xv * xv, axis=-1, keepdims=True) + EPS)
        hb = ((xv * rr) * ng_ref[...]).astype(BF16)
        h_ref[...] = hb

        def proj(lo, hi):
            return _dot(hb, wp_ref[lo:hi, :], NT)

        pa_ref[...] = proj(PA_LO, QB_LO)
        gb_ref[...] = proj(GB_LO, PM_LO)
        pm_ref[...] = proj(PM_LO, FB_LO)
        fb = proj(FB_LO, PROJ_PAD)
        fb_ref[...] = fb

        lane = lax.broadcasted_iota(jnp.int32, (t, LANES), 1)
        row = lax.broadcasted_iota(jnp.int32, (t, LANES), 0)
        lo = lane < HEAD_DIM
        z = fb + bf_ref[...]
        lf = -(jnp.maximum(-z, 0.0) + jnp.log1p(jnp.exp(-jnp.abs(z))))
        lf = jnp.where(lane < FOX_HEADS, lf, 0.0)
        sh = 1
        while sh < t:
            lf = lf + jnp.where(row >= sh, pltpu.roll(lf, sh, axis=0), 0.0)
            sh *= 2
        fcum = lf + carry_ref[...]
        fcol_ref[...] = fcum
        carry_ref[...] = fcol_ref[t - 1:t, :]

        ones3 = _ones3(lane)
        one_sum = jnp.where(lane == SUM_LANE, 1.0, 0.0)
        for seg, g_ref, out_ref, scale in ((QB_LO, qg_ref, qa_ref, ATT_SCALE), (KB_LO, kg_ref, ka_ref, 1.0)):
            raw = proj(seg, seg + FOX_WIDTH)
            qk_ref[:, seg - QB_LO:seg - QB_LO + FOX_WIDTH] = raw
            for p in range(FOX_WIDTH // LANES):
                sl = slice(p * LANES, (p + 1) * LANES)
                blk = raw[:, sl]
                normed = ((blk * _head_rms(blk, lo)) * g_ref[:, sl]) * scale
                for hh in range(2):
                    h = 2 * p + hh
                    if seg == QB_LO:
                        extras = ones3
                    else:
                        extras = _split3(-_lane_pick(fcum, lane, h), lane) + one_sum
                    out_ref[:, h * LANES:(h + 1) * LANES] = _head_block(normed, hh, lo, extras)
        vraw = proj(VB_LO, GB_LO)
        for h in range(FOX_HEADS):
            va_ref[:, h * LANES:(h + 1) * LANES] = _head_block(vraw[:, (h // 2) * LANES:(h // 2 + 1) * LANES], h % 2, lo, ones3)

    outs = (
        jax.ShapeDtypeStruct((s, D_MODEL), BF16),
        jax.ShapeDtypeStruct((s, 512), F32),
        jax.ShapeDtypeStruct((s, 2 * FOX_WIDTH), F32),
        jax.ShapeDtypeStruct((s, HEAD_BLOCKS), BF16),
        jax.ShapeDtypeStruct((s, HEAD_BLOCKS), BF16),
        jax.ShapeDtypeStruct((s, HEAD_BLOCKS), BF16),
        jax.ShapeDtypeStruct((s, FOX_WIDTH), F32),
        jax.ShapeDtypeStruct((s, 512), F32),
        jax.ShapeDtypeStruct((s, LANES), F32),
    )
    return pl.pallas_call(
        body, name="fwd_in", grid=(n,), out_shape=outs,
        in_specs=[_rows(t, D_MODEL), _full((1, D_MODEL)), _full((PROJ_PAD, D_MODEL)), _full((1, LANES)),
                  _full((1, FOX_WIDTH)), _full((1, FOX_WIDTH))],
        out_specs=(_rows(t, D_MODEL), _rows(t, 512), _rows(t, 2 * FOX_WIDTH), _rows(t, HEAD_BLOCKS),
                   _rows(t, HEAD_BLOCKS), _rows(t, HEAD_BLOCKS), _rows(t, FOX_WIDTH), _rows(t, 512),
                   _rows(t, LANES)),
        scratch_shapes=[pltpu.VMEM((1, LANES), F32), pltpu.VMEM((t, LANES), F32)],
        compiler_params=_params(),
    )(x, norm_g, wp, bf_pad, fq_g, fk_g)


POOL_HALO = 16


def _pool_window(lane):
    return jnp.where(lane < 64, 2.0, jnp.where(lane < 128, 4.0, jnp.where(lane < 192, 8.0, 16.0)))


def _pool_pick(lane, s2, s4, s8, s16):
    return jnp.where(lane < 64, s2, jnp.where(lane < 128, s4, jnp.where(lane < 192, s8, s16)))


def _group_onehot(shape, row_is_group_lane):
    r = lax.broadcasted_iota(jnp.int32, shape, 0)
    c = lax.broadcasted_iota(jnp.int32, shape, 1)
    hit = (r % HEAD_DIM == c) if row_is_group_lane else (c % HEAD_DIM == r)
    return jnp.where(hit, 1.0, 0.0).astype(F32)


def _same_group(shape):
    r = lax.broadcasted_iota(jnp.int32, shape, 0)
    c = lax.broadcasted_iota(jnp.int32, shape, 1)
    return (r // HEAD_DIM) == (c // HEAD_DIM)


def _pool_block_diag(w4):
    spread = jnp.dot(w4, _group_onehot((HEAD_DIM, POOL_WIDTH), False), preferred_element_type=F32,
                     precision=lax.Precision.HIGHEST)
    return jnp.where(_same_group((POOL_WIDTH, POOL_WIDTH)), spread, 0.0).astype(BF16)


def _pool_fwd(pa, w4, pscale):
    s = pa.shape[0]
    t = TILE
    n = s // t
    ext = t + POOL_HALO

    def body(pa_ref, w4_ref, sc_ref, ma_ref, d_ref, ext_ref, w_ref):
        i = pl.program_id(0)

        @pl.when(i == 0)
        def _():
            ext_ref[0:POOL_HALO, :] = jnp.zeros((POOL_HALO, POOL_WIDTH), F32)
            w_ref[...] = _pool_block_diag(w4_ref[...])

        u = pa_ref[:, 0:POOL_WIDTH]
        ext_ref[POOL_HALO:ext, :] = u
        e = ext_ref[...]
        s2 = e + pltpu.roll(e, 1, axis=0)
        s4 = s2 + pltpu.roll(s2, 2, axis=0)
        s8 = s4 + pltpu.roll(s4, 4, axis=0)
        s16 = s8 + pltpu.roll(s8, 8, axis=0)
        lane_e = lax.broadcasted_iota(jnp.int32, (ext, POOL_WIDTH), 1)
        win = _pool_pick(lane_e, s2, s4, s8, s16)[POOL_HALO:ext, :]
        lane = lax.broadcasted_iota(jnp.int32, (t, POOL_WIDTH), 1)
        pos = (lax.broadcasted_iota(jnp.int32, (t, POOL_WIDTH), 0) + (i * t + 1)).astype(F32)
        d = win / jnp.minimum(pos, _pool_window(lane)) - u
        db = d.astype(BF16)
        d_ref[...] = db
        ya = _dot(db, w_ref[...]) * sc_ref[...]
        ga = pa_ref[:, POOL_WIDTH:2 * POOL_WIDTH]
        ma_ref[...] = (ya * (ga * _sig(ga))).astype(BF16)
        ext_ref[0:POOL_HALO, :] = ext_ref[t:ext, :]

    return pl.pallas_call(
        body, name="pool_fwd", grid=(n,),
        out_shape=(jax.ShapeDtypeStruct((s, POOL_WIDTH), BF16), jax.ShapeDtypeStruct((s, POOL_WIDTH), BF16)),
        in_specs=[_rows(t, 512), _full((POOL_ROWS, HEAD_DIM)), _full((1, POOL_WIDTH))],
        out_specs=(_rows(t, POOL_WIDTH), _rows(t, POOL_WIDTH)),
        scratch_shapes=[pltpu.VMEM((ext, POOL_WIDTH), F32), pltpu.VMEM((POOL_WIDTH, POOL_WIDTH), BF16)],
        compiler_params=_params(),
    )(pa, w4, pscale)


def _mem_softmax(qm, kp):
    sc = _dot(qm, kp, NT)
    e = jnp.exp(sc - jnp.max(sc, axis=-1, keepdims=True))
    return e * (1.0 / jnp.sum(e, axis=-1, keepdims=True))


def _mem_attn_fwd(pm, kmn, vmb, mq_g):
    s = pm.shape[0]
    t = TILE
    n = s // t

    def body(pm_ref, k_ref, v_ref, g_ref, mm_ref):
        lo = _lane_lo((t, LANES))
        for p in range(MEM_WIDTH // LANES):
            sl = slice(p * LANES, (p + 1) * LANES)
            qb = pm_ref[:, sl]
            qs = (((qb * _head_rms(qb, lo)) * g_ref[:, sl]) * ATT_SCALE).astype(BF16)
            kp = k_ref[:, sl]
            vp = v_ref[:, sl]
            outs = []
            for hh in range(2):
                msk = lo if hh == 0 else jnp.logical_not(lo)
                prob = _mem_softmax(jnp.where(msk, qs, jnp.zeros_like(qs)), kp)
                outs.append(_dot(prob.astype(BF16), vp))
            o = jnp.where(lo, outs[0], outs[1])
            gm = pm_ref[:, MEM_WIDTH + p * LANES:MEM_WIDTH + (p + 1) * LANES]
            mm_ref[:, sl] = (o * (gm * _sig(gm))).astype(BF16)

    return pl.pallas_call(
        body, name="mem_attn_fwd", grid=(n,),
        out_shape=jax.ShapeDtypeStruct((s, MEM_WIDTH), BF16),
        in_specs=[_rows(t, 512), _full((N_MEM, MEM_WIDTH)), _full((N_MEM, MEM_WIDTH)), _full((1, MEM_WIDTH))],
        out_specs=_rows(t, MEM_WIDTH),
        compiler_params=_params(),
    )(pm, kmn, vmb, mq_g)


def _fox_fwd(qa, ka, va, gb):
    s = qa.shape[0]
    t = TILE
    n = s // t
    pair_w = 2 * LANES

    def body(qa_ref, ka_ref, va_ref, gb_ref, o_ref, mb_ref, r_ref):
        i = pl.program_id(1)
        lane = lax.broadcasted_iota(jnp.int32, (t, LANES), 1)
        lo = lane < HEAD_DIM
        causal = lax.broadcasted_iota(jnp.int32, (t, t), 1) <= lax.broadcasted_iota(jnp.int32, (t, t), 0)
        qas = (qa_ref[:, 0:LANES], qa_ref[:, LANES:pair_w])

        def step(j, carry, masked):
            rows = pl.ds(pl.multiple_of(j * t, t), t)
            new = []
            for hh in range(2):
                cols = slice(hh * LANES, (hh + 1) * LANES)
                m, acc = carry[hh]
                sc = _dot(qas[hh], ka_ref[rows, cols], NT)
                if masked:
                    sc = jnp.where(causal, sc, -1e30)
                m_new = jnp.maximum(m, jnp.max(sc, axis=-1, keepdims=True))
                acc = jnp.exp(m - m_new) * acc + _dot(jnp.exp(sc - m_new).astype(BF16), va_ref[rows, cols])
                new.append((m_new, acc))
            return tuple(new)

        init = (jnp.full((t, 1), -1e30, F32), jnp.zeros((t, LANES), F32))
        carry = lax.fori_loop(0, i, functools.partial(step, masked=False), (init, init))
        outs = []
        rcol = jnp.zeros((t, LANES), F32)
        for hh, (m, acc) in enumerate(step(i, carry, masked=True)):
            l = _lane_pick(acc, lane, AUG_LO)
            outs.append(acc * (1.0 / l))
            rcol = jnp.where(lane == hh, m + jnp.log(l), rcol)
        o = _pair_block(outs[0], outs[1], lo)
        o_ref[...] = o
        g = gb_ref[...]
        mb_ref[...] = (o * (g * _sig(g))).astype(BF16)
        r_ref[0] = rcol

    return pl.pallas_call(
        body, name="fox_fwd", grid=(FOX_HEADS // 2, n),
        out_shape=(jax.ShapeDtypeStruct((s, FOX_WIDTH), F32), jax.ShapeDtypeStruct((s, FOX_WIDTH), BF16),
                   jax.ShapeDtypeStruct((FOX_HEADS // 2, s, LANES), F32)),
        in_specs=[pl.BlockSpec((t, pair_w), lambda p, i: (i, p)), pl.BlockSpec((s, pair_w), lambda p, i: (0, p)),
                  pl.BlockSpec((s, pair_w), lambda p, i: (0, p)), pl.BlockSpec((t, LANES), lambda p, i: (i, p))],
        out_specs=(pl.BlockSpec((t, LANES), lambda p, i: (i, p)), pl.BlockSpec((t, LANES), lambda p, i: (i, p)),
                   pl.BlockSpec((1, t, LANES), lambda p, i: (p, i, 0))),
        compiler_params=_params(2),
    )(qa, ka, va, gb)


def _out_loss(x, tgt, ma, mb, mm, wout):
    s = x.shape[0]
    t = TILE
    n = s // t

    def body(x_ref, t_ref, ma_ref, mb_ref, mm_ref, w_ref, dy_ref, dma_ref, dmb_ref, dmm_ref, dw_ref, loss_ref, mix_ref):
        @pl.when(pl.program_id(0) == 0)
        def _():
            dw_ref[...] = jnp.zeros_like(dw_ref)
            loss_ref[...] = jnp.zeros_like(loss_ref)

        mix_ref[:, 0:256] = ma_ref[...]
        mix_ref[:, 256:768] = mb_ref[...]
        mix_ref[:, 768:1024] = mm_ref[...]
        mix = mix_ref[...]
        err = (x_ref[...] + _dot(mix, w_ref[...])) - t_ref[...]
        row_mean = jnp.sum(err * err, axis=-1, keepdims=True) * (1.0 / D_MODEL)
        loss_ref[...] += 0.5 * jnp.sum(row_mean, axis=0, keepdims=True)
        dy = err * (1.0 / D_MODEL)
        dy_ref[...] = dy
        dyb = dy.astype(BF16)
        dmix = _dot(dyb, w_ref[...], NT)
        dma_ref[...] = dmix[:, 0:256]
        dmb_ref[...] = dmix[:, 256:768]
        dmm_ref[...] = dmix[:, 768:1024]
        dw_ref[...] += _dot(mix, dyb, TN)

    return pl.pallas_call(
        body, name="out_loss", grid=(n,),
        out_shape=(jax.ShapeDtypeStruct((s, D_MODEL), F32), jax.ShapeDtypeStruct((s, 256), F32),
                   jax.ShapeDtypeStruct((s, 512), F32), jax.ShapeDtypeStruct((s, 256), F32),
                   jax.ShapeDtypeStruct((D_MODEL, D_MODEL), F32), jax.ShapeDtypeStruct((1, LANES), F32)),
        in_specs=[_rows(t, D_MODEL), _rows(t, D_MODEL), _rows(t, 256), _rows(t, 512), _rows(t, 256),
                  _full((D_MODEL, D_MODEL))],
        out_specs=(_rows(t, D_MODEL), _rows(t, 256), _rows(t, 512), _rows(t, 256), _full((D_MODEL, D_MODEL)),
                   _full((1, LANES))),
        scratch_shapes=[pltpu.VMEM((t, D_MODEL), BF16)],
        compiler_params=_params(),
    )(x, tgt, ma, mb, mm, wout)


def _mem_attn_bwd(pm, dmm, kmn, vmb, mq_g):
    s = pm.shape[0]
    t = TILE
    n = s // t

    def body(pm_ref, dmm_ref, k_ref, v_ref, g_ref, dpm_ref, dk_ref, dv_ref, dg_ref, gacc_ref):
        @pl.when(pl.program_id(0) == 0)
        def _():
            dk_ref[...] = jnp.zeros_like(dk_ref)
            dv_ref[...] = jnp.zeros_like(dv_ref)
            gacc_ref[...] = jnp.zeros_like(gacc_ref)

        lo = _lane_lo((t, LANES))
        for p in range(MEM_WIDTH // LANES):
            sl = slice(p * LANES, (p + 1) * LANES)
            qb = pm_ref[:, sl]
            rr = _head_rms(qb, lo)
            qhat = qb * rr
            g = g_ref[:, sl]
            qs = ((qhat * g) * ATT_SCALE).astype(BF16)
            gm = pm_ref[:, MEM_WIDTH + p * LANES:MEM_WIDTH + (p + 1) * LANES]
            sg = _sig(gm)
            dmo = dmm_ref[:, sl]
            d_o = dmo * (gm * sg)
            kp = k_ref[:, sl]
            vp = v_ref[:, sl]
            outs, dqs = [], []
            for hh in range(2):
                msk = lo if hh == 0 else jnp.logical_not(lo)
                qm = jnp.where(msk, qs, jnp.zeros_like(qs))
                prob = _mem_softmax(qm, kp)
                pb = prob.astype(BF16)
                outs.append(_dot(pb, vp))
                dom = jnp.where(msk, d_o, 0.0).astype(BF16)
                dp = _dot(dom, vp, NT)
                ds = (prob * (dp - jnp.sum(prob * dp, axis=-1, keepdims=True))).astype(BF16)
                dqs.append(_dot(ds, kp))
                dk_ref[:, sl] += _dot(ds, qm, TN)
                dv_ref[:, sl] += _dot(pb, dom, TN)
            o = jnp.where(lo, outs[0], outs[1])
            dqn = jnp.where(lo, dqs[0], dqs[1]) * ATT_SCALE
            dpm_ref[:, sl] = _head_norm_bwd(dqn, qhat, rr, g, lo).astype(BF16)
            dpm_ref[:, MEM_WIDTH + p * LANES:MEM_WIDTH + (p + 1) * LANES] = (
                dmo * o * (sg * (1.0 + gm * (1.0 - sg)))).astype(BF16)
            gacc_ref[:, sl] += jnp.sum(dqn * qhat, axis=0, keepdims=True)

        @pl.when(pl.program_id(0) == n - 1)
        def _():
            dg_ref[...] = _fold_heads(gacc_ref[...])

    return pl.pallas_call(
        body, name="mem_attn_bwd", grid=(n,),
        out_shape=(jax.ShapeDtypeStruct((s, 512), BF16), jax.ShapeDtypeStruct((N_MEM, MEM_WIDTH), F32),
                   jax.ShapeDtypeStruct((N_MEM, MEM_WIDTH), F32), jax.ShapeDtypeStruct((1, LANES), F32)),
        in_specs=[_rows(t, 512), _rows(t, MEM_WIDTH), _full((N_MEM, MEM_WIDTH)), _full((N_MEM, MEM_WIDTH)),
                  _full((1, MEM_WIDTH))],
        out_specs=(_rows(t, 512), _full((N_MEM, MEM_WIDTH)), _full((N_MEM, MEM_WIDTH)), _full((1, LANES))),
        scratch_shapes=[pltpu.VMEM((1, MEM_WIDTH), F32)],
        compiler_params=_params(),
    )(pm, dmm, kmn, vmb, mq_g)


def _mem_bwd(dkn, dvm, kv, mnb, mem, w_kv, mk_g, mem_norm_g):
    n = mem.shape[0]

    def body(dkn_ref, dvm_ref, kv_ref, mn_ref, mem_ref, w_ref, kg_ref, g_ref, dw_ref, dg_ref, dkg_ref, dkv_ref):
        lo = _lane_lo((n, LANES))
        gacc = []
        for p in range(MEM_WIDTH // LANES):
            sl = slice(p * LANES, (p + 1) * LANES)
            kb = kv_ref[:, sl]
            rr = _head_rms(kb, lo)
            khat = kb * rr
            dk = dkn_ref[:, sl]
            dkv_ref[:, sl] = _head_norm_bwd(dk, khat, rr, kg_ref[:, sl], lo).astype(BF16)
            gacc.append(jnp.sum(dk * khat, axis=0, keepdims=True))
        dkg_ref[...] = _fold_heads(jnp.concatenate(gacc, axis=1))
        dkv_ref[:, MEM_WIDTH:] = dvm_ref[...].astype(BF16)
        dkv = dkv_ref[...]
        dw_ref[...] = _dot(mn_ref[...], dkv, TN)
        dmn = _dot(dkv, w_ref[...], NT)
        xm = mem_ref[...]
        rr = lax.rsqrt(jnp.mean(xm * xm, axis=-1, keepdims=True) + EPS)
        dg_ref[...] = jnp.sum(dmn * (xm * rr), axis=0, keepdims=True)

    return pl.pallas_call(
        body, name="mem_bwd",
        out_shape=(jax.ShapeDtypeStruct((D_MODEL, 2 * MEM_WIDTH), F32), jax.ShapeDtypeStruct((1, D_MODEL), F32),
                   jax.ShapeDtypeStruct((1, LANES), F32)),
        scratch_shapes=[pltpu.VMEM((n, 2 * MEM_WIDTH), BF16)],
        compiler_params=pltpu.CompilerParams(vmem_limit_bytes=VMEM_LIMIT),
    )(dkn, dvm, kv, mnb, mem, w_kv, mk_g, mem_norm_g)


def _pool_bwd(pa, db, dma, w4, pscale):
    s = pa.shape[0]
    t = TILE
    n = s // t
    ext = t + POOL_HALO

    def body(pa_ref, d_ref, dma_ref, w4_ref, sc_ref, dpa_ref, dw4_ref, dsc_ref, ext_ref, w_ref, dw_ref):
        i = pl.program_id(0)

        @pl.when(i == 0)
        def _():
            dw_ref[...] = jnp.zeros_like(dw_ref)
            dsc_ref[...] = jnp.zeros_like(dsc_ref)
            ext_ref[t:ext, :] = jnp.zeros((POOL_HALO, POOL_WIDTH), F32)
            w_ref[...] = _pool_block_diag(w4_ref[...])

        dbv = d_ref[...]
        z = _dot(dbv, w_ref[...])
        ga = pa_ref[:, POOL_WIDTH:2 * POOL_WIDTH]
        sg = _sig(ga)
        dma_v = dma_ref[...]
        dya = dma_v * (ga * sg)
        dpa_ref[:, POOL_WIDTH:2 * POOL_WIDTH] = (dma_v * (z * sc_ref[...]) * (sg * (1.0 + ga * (1.0 - sg)))).astype(BF16)
        dsc_ref[...] += jnp.sum(dya * z, axis=0, keepdims=True)
        dzb = (dya * sc_ref[...]).astype(BF16)
        dw_ref[...] += _dot(dbv, dzb, TN)
        dd = _dot(dzb, w_ref[...], NT)
        lane = lax.broadcasted_iota(jnp.int32, (t, POOL_WIDTH), 1)
        pos = (lax.broadcasted_iota(jnp.int32, (t, POOL_WIDTH), 0) + ((n - 1 - i) * t + 1)).astype(F32)
        ext_ref[0:t, :] = dd / jnp.minimum(pos, _pool_window(lane))
        e = ext_ref[...]
        s2 = e + pltpu.roll(e, ext - 1, axis=0)
        s4 = s2 + pltpu.roll(s2, ext - 2, axis=0)
        s8 = s4 + pltpu.roll(s4, ext - 4, axis=0)
        s16 = s8 + pltpu.roll(s8, ext - 8, axis=0)
        lane_e = lax.broadcasted_iota(jnp.int32, (ext, POOL_WIDTH), 1)
        win = _pool_pick(lane_e, s2, s4, s8, s16)[0:t, :]
        dpa_ref[:, 0:POOL_WIDTH] = (win - dd).astype(BF16)
        ext_ref[t:ext, :] = ext_ref[0:POOL_HALO, :]

        @pl.when(i == n - 1)
        def _():
            own = jnp.where(_same_group((POOL_WIDTH, POOL_WIDTH)), dw_ref[...], 0.0)
            dw4_ref[...] = jnp.dot(own, _group_onehot((POOL_WIDTH, HEAD_DIM), True), preferred_element_type=F32,
                                   precision=lax.Precision.HIGHEST)

    return pl.pallas_call(
        body, name="pool_bwd", grid=(n,),
        out_shape=(jax.ShapeDtypeStruct((s, 512), BF16), jax.ShapeDtypeStruct((POOL_ROWS, HEAD_DIM), F32),
                   jax.ShapeDtypeStruct((1, POOL_WIDTH), F32)),
        in_specs=[_rows_rev(t, 512, n), _rows_rev(t, POOL_WIDTH, n), _rows_rev(t, POOL_WIDTH, n),
                  _full((POOL_ROWS, HEAD_DIM)), _full((1, POOL_WIDTH))],
        out_specs=(_rows_rev(t, 512, n), _full((POOL_ROWS, HEAD_DIM)), _full((1, POOL_WIDTH))),
        scratch_shapes=[pltpu.VMEM((ext, POOL_WIDTH), F32), pltpu.VMEM((POOL_WIDTH, POOL_WIDTH), BF16),
                        pltpu.VMEM((POOL_WIDTH, POOL_WIDTH), F32)],
        compiler_params=_params(),
    )(pa, db, dma, w4, pscale)


def _fox_prep(dmb, gb, o, r4):
    s = dmb.shape[0]
    t = TILE
    n = s // t
    pairs = FOX_HEADS // 2

    def body(dmb_ref, gb_ref, o_ref, r_ref, doa_ref, dgb_ref, rr_ref):
        lane = lax.broadcasted_iota(jnp.int32, (t, LANES), 1)
        lo = lane < HEAD_DIM
        for p in range(pairs):
            sl = slice(p * LANES, (p + 1) * LANES)
            g = gb_ref[:, sl]
            sg = _sig(g)
            dm = dmb_ref[:, sl]
            ov = o_ref[:, sl]
            d_o = dm * (g * sg)
            dgb_ref[:, sl] = (dm * ov * (sg * (1.0 + g * (1.0 - sg)))).astype(BF16)
            prod = d_o * ov
            deltas = (jnp.sum(jnp.where(lo, prod, 0.0), axis=-1, keepdims=True),
                      jnp.sum(jnp.where(lo, 0.0, prod), axis=-1, keepdims=True))
            for hh in range(2):
                h = 2 * p + hh
                doa_ref[:, h * LANES:(h + 1) * LANES] = _head_block(d_o, hh, lo, _split3(-deltas[hh], lane))
            rr_ref[p, 0] = r_ref[p].T[0:8, :]

    return pl.pallas_call(
        body, name="fox_prep", grid=(n,),
        out_shape=(jax.ShapeDtypeStruct((s, HEAD_BLOCKS), BF16), jax.ShapeDtypeStruct((s, FOX_WIDTH), BF16),
                   jax.ShapeDtypeStruct((pairs, n, 8, t), F32)),
        in_specs=[_rows(t, FOX_WIDTH), _rows(t, FOX_WIDTH), _rows(t, FOX_WIDTH),
                  pl.BlockSpec((pairs, t, LANES), lambda i: (0, i, 0))],
        out_specs=(_rows(t, HEAD_BLOCKS), _rows(t, FOX_WIDTH), pl.BlockSpec((pairs, 1, 8, t), lambda i: (0, i, 0, 0))),
        compiler_params=_params(),
    )(dmb, gb, o, r4)


def _fox_bwd(ka, va, qa, doa, rr):
    s = ka.shape[0]
    t = TILE
    n = s // t
    pair_w = 2 * LANES

    def body(ka_ref, va_ref, qa_ref, doa_ref, rr_ref, dka_ref, dva_ref, dqa_ref):
        j = pl.program_id(1)

        @pl.when(j == 0)
        def _():
            dqa_ref[...] = jnp.zeros_like(dqa_ref)

        causal = lax.broadcasted_iota(jnp.int32, (t, t), 0) <= lax.broadcasted_iota(jnp.int32, (t, t), 1)
        kas = (ka_ref[:, 0:LANES], ka_ref[:, LANES:pair_w])
        vas = (va_ref[:, 0:LANES], va_ref[:, LANES:pair_w])

        def step(i, carry, masked):
            rows = pl.ds(pl.multiple_of(i * t, t), t)
            new = []
            for hh in range(2):
                cols = slice(hh * LANES, (hh + 1) * LANES)
                dk_a, dv_a = carry[hh]
                qb = qa_ref[rows, cols]
                d_o = doa_ref[rows, cols]
                arg = _dot(kas[hh], qb, NT) - rr_ref[0, i, hh:hh + 1, :]
                if masked:
                    arg = jnp.where(causal, arg, -1e30)
                pt = jnp.exp(arg)
                dst = (pt * _dot(vas[hh], d_o, NT)).astype(BF16)
                dv_a = dv_a + _dot(pt.astype(BF16), d_o)
                dk_a = dk_a + _dot(dst, qb)
                dqa_ref[rows, cols] += _dot(dst, kas[hh], TN)
                new.append((dk_a, dv_a))
            return tuple(new)

        zero = jnp.zeros((t, LANES), F32)
        carry = step(j, ((zero, zero), (zero, zero)), masked=True)
        res = lax.fori_loop(j + 1, n, functools.partial(step, masked=False), carry)
        for hh in range(2):
            cols = slice(hh * LANES, (hh + 1) * LANES)
            dka_ref[:, cols] = res[hh][0]
            dva_ref[:, cols] = res[hh][1]

    tile_spec = pl.BlockSpec((t, pair_w), lambda p, j: (j, p))
    full_spec = pl.BlockSpec((s, pair_w), lambda p, j: (0, p))
    return pl.pallas_call(
        body, name="fox_bwd", grid=(FOX_HEADS // 2, n),
        out_shape=(jax.ShapeDtypeStruct((s, HEAD_BLOCKS), F32),) * 3,
        in_specs=[tile_spec, tile_spec, full_spec, full_spec,
                  pl.BlockSpec((1, n, 8, t), lambda p, j: (p, 0, 0, 0))],
        out_specs=(tile_spec, tile_spec, full_spec),
        compiler_params=_params(2),
    )(ka, va, qa, doa, rr)


def _fox_post(dqa, dka, dva, qk, fb, bf_pad, fq_g, fk_g):
    s = dqa.shape[0]
    t = TILE
    n = s // t

    def body(dqa_ref, dka_ref, dva_ref, qk_ref, fb_ref, bf_ref, qg_ref, kg_ref,
             dqk_ref, dv_ref, dfb_ref, dqg_ref, dkg_ref, dbf_ref, qacc_ref, kacc_ref, carry_ref):
        i = pl.program_id(0)

        @pl.when(i == 0)
        def _():
            qacc_ref[...] = jnp.zeros_like(qacc_ref)
            kacc_ref[...] = jnp.zeros_like(kacc_ref)
            dbf_ref[...] = jnp.zeros_like(dbf_ref)
            carry_ref[...] = jnp.zeros_like(carry_ref)

        lane = lax.broadcasted_iota(jnp.int32, (t, LANES), 1)
        row = lax.broadcasted_iota(jnp.int32, (t, LANES), 0)
        lo = lane < HEAD_DIM

        def head_blocks(ref, p):
            return ref[:, 2 * p * LANES:(2 * p + 1) * LANES], ref[:, (2 * p + 1) * LANES:(2 * p + 2) * LANES]

        acc = jnp.zeros((t, LANES), F32)
        for p in range(FOX_WIDTH // LANES):
            sl = slice(p * LANES, (p + 1) * LANES)
            dq0, dq1 = head_blocks(dqa_ref, p)
            dk0, dk1 = head_blocks(dka_ref, p)
            dv0, dv1 = head_blocks(dva_ref, p)
            dv_ref[:, sl] = _pair_block(dv0, dv1, lo).astype(BF16)
            for hh, (dq_h, dk_h) in enumerate(((dq0, dk0), (dq1, dk1))):
                d_f = _lane_pick(dq_h, lane, SUM_LANE) - _lane_pick(dk_h, lane, AUG_LO)
                acc = jnp.where(lane == 2 * p + hh, d_f, acc)
            for off, pair, g_ref, acc_ref, scale in ((0, _pair_block(dq0, dq1, lo), qg_ref, qacc_ref, ATT_SCALE),
                                                     (FOX_WIDTH, _pair_block(dk0, dk1, lo), kg_ref, kacc_ref, 1.0)):
                raw = qk_ref[:, off + p * LANES:off + (p + 1) * LANES]
                rr = _head_rms(raw, lo)
                xhat = raw * rr
                dn = pair * scale
                dqk_ref[:, off + p * LANES:off + (p + 1) * LANES] = _head_norm_bwd(
                    dn, xhat, rr, g_ref[:, sl], lo).astype(BF16)
                acc_ref[:, sl] += jnp.sum(dn * xhat, axis=0, keepdims=True)

        sh = 1
        while sh < t:
            acc = acc + jnp.where(row < t - sh, pltpu.roll(acc, t - sh, axis=0), 0.0)
            sh *= 2
        dlogf = acc + carry_ref[...]
        dfb_ref[...] = dlogf
        carry_ref[...] = dfb_ref[0:1, :]
        z = fb_ref[...] + bf_ref[...]
        dz = jnp.where(lane < FOX_HEADS, dlogf * (1.0 / (1.0 + jnp.exp(z))), 0.0)
        dfb_ref[...] = dz
        dbf_ref[...] += jnp.sum(dz, axis=0, keepdims=True)

        @pl.when(i == n - 1)
        def _():
            dqg_ref[...] = _fold_heads(qacc_ref[...])
            dkg_ref[...] = _fold_heads(kacc_ref[...])

    return pl.pallas_call(
        body, name="fox_post", grid=(n,),
        out_shape=(jax.ShapeDtypeStruct((s, 2 * FOX_WIDTH), BF16), jax.ShapeDtypeStruct((s, FOX_WIDTH), BF16),
                   jax.ShapeDtypeStruct((s, LANES), F32), jax.ShapeDtypeStruct((1, LANES), F32),
                   jax.ShapeDtypeStruct((1, LANES), F32), jax.ShapeDtypeStruct((1, LANES), F32)),
        in_specs=[_rows_rev(t, HEAD_BLOCKS, n), _rows_rev(t, HEAD_BLOCKS, n), _rows_rev(t, HEAD_BLOCKS, n),
                  _rows_rev(t, 2 * FOX_WIDTH, n), _rows_rev(t, LANES, n), _full((1, LANES)),
                  _full((1, FOX_WIDTH)), _full((1, FOX_WIDTH))],
        out_specs=(_rows_rev(t, 2 * FOX_WIDTH, n), _rows_rev(t, FOX_WIDTH, n), _rows_rev(t, LANES, n),
                   _full((1, LANES)), _full((1, LANES)), _full((1, LANES))),
        scratch_shapes=[pltpu.VMEM((1, FOX_WIDTH), F32), pltpu.VMEM((1, FOX_WIDTH), F32), pltpu.VMEM((1, LANES), F32)],
        compiler_params=_params(),
    )(dqa, dka, dva, qk, fb, bf_pad, fq_g, fk_g)


def _assemble_dproj(dp_ref, dpa_ref, dqk_ref, dv_ref, dgb_ref, dpm_ref, dfb_ref):
    dp_ref[:, PA_LO:QB_LO] = dpa_ref[...]
    dp_ref[:, QB_LO:VB_LO] = dqk_ref[...]
    dp_ref[:, VB_LO:GB_LO] = dv_ref[...]
    dp_ref[:, GB_LO:PM_LO] = dgb_ref[...]
    dp_ref[:, PM_LO:FB_LO] = dpm_ref[...]
    dp_ref[:, FB_LO:PROJ_PAD] = dfb_ref[...].astype(BF16)


def _dproj_specs(t):
    return [_rows(t, 512), _rows(t, 2 * FOX_WIDTH), _rows(t, FOX_WIDTH), _rows(t, FOX_WIDTH), _rows(t, 512),
            _rows(t, LANES)]


def _in_bwd_x(x, dy, norm_g, wp, dparts):
    s = x.shape[0]
    t = TILE
    n = s // t

    def body(x_ref, dy_ref, g_ref, wp_ref, dpa_ref, dqk_ref, dv_ref, dgb_ref, dpm_ref, dfb_ref, gx_ref, dg_ref, dp_ref):
        @pl.when(pl.program_id(0) == 0)
        def _():
            dg_ref[...] = jnp.zeros_like(dg_ref)

        _assemble_dproj(dp_ref, dpa_ref, dqk_ref, dv_ref, dgb_ref, dpm_ref, dfb_ref)
        dh = _dot(dp_ref[...], wp_ref[...])
        xv = x_ref[...]
        rr = lax.rsqrt(jnp.mean(xv * xv, axis=-1, keepdims=True) + EPS)
        xhat = xv * rr
        a = dh * g_ref[...]
        gx_ref[...] = dy_ref[...] + rr * (a - xhat * jnp.mean(xhat * a, axis=-1, keepdims=True))
        dg_ref[...] += jnp.sum(dh * xhat, axis=0, keepdims=True)

    return pl.pallas_call(
        body, name="in_bwd_x", grid=(n,),
        out_shape=(jax.ShapeDtypeStruct((s, D_MODEL), F32), jax.ShapeDtypeStruct((1, D_MODEL), F32)),
        in_specs=[_rows(t, D_MODEL), _rows(t, D_MODEL), _full((1, D_MODEL)), _full((PROJ_PAD, D_MODEL))] + _dproj_specs(t),
        out_specs=(_rows(t, D_MODEL), _full((1, D_MODEL))),
        scratch_shapes=[pltpu.VMEM((t, PROJ_PAD), BF16)],
        compiler_params=_params(),
    )(x, dy, norm_g, wp, *dparts)


def _in_bwd_w(hb, dparts):
    s = hb.shape[0]
    t = TILE
    n = s // t
    f_hi = F_ORIG_LO + FOX_HEADS

    def body(h_ref, dpa_ref, dqk_ref, dv_ref, dgb_ref, dpm_ref, dfb_ref, dw_ref):
        @pl.when(pl.program_id(0) == 0)
        def _():
            dw_ref[...] = jnp.zeros_like(dw_ref)

        hv = h_ref[...]
        for lo, ref in ((0, dpa_ref), (QB_LO, dqk_ref), (VB_LO, dv_ref), (f_hi, dgb_ref), (f_hi + FOX_WIDTH, dpm_ref)):
            dw_ref[lo:lo + ref.shape[1], :] += _dot(ref[...], hv, TN)
        dw_ref[F_ORIG_LO:f_hi, :] += _dot(dfb_ref[...].astype(BF16), hv, TN)[0:FOX_HEADS, :]

    return pl.pallas_call(
        body, name="in_bwd_w", grid=(n,),
        out_shape=jax.ShapeDtypeStruct((IN_WIDTH, D_MODEL), F32),
        in_specs=[_rows(t, D_MODEL)] + _dproj_specs(t),
        out_specs=_full((IN_WIDTH, D_MODEL)),
        compiler_params=_params(),
    )(hb, *dparts)


def _adamw_math(w_ref, gv, m_ref, v_ref, d_ref, nm_ref, nv_ref):
    nm = ADAM_B1 * m_ref[...] + (1.0 - ADAM_B1) * gv
    nv = ADAM_B2 * v_ref[...] + (1.0 - ADAM_B2) * (gv * gv)
    m_hat = nm / (1.0 - ADAM_B1 ** ADAM_STEP)
    v_hat = nv / (1.0 - ADAM_B2 ** ADAM_STEP)
    d_ref[...] = -ADAM_LR * (m_hat / (jnp.sqrt(v_hat) + ADAM_EPS) + ADAM_WD * w_ref[...])
    nm_ref[...] = nm
    nv_ref[...] = nv


def _adamw(name, w, g, m, v):
    rows, cols = w.shape
    tc = 256 if rows * cols > 256 * 1024 else cols
    n = cols // tc

    def body(w_ref, g_ref, m_ref, v_ref, d_ref, nm_ref, nv_ref):
        _adamw_math(w_ref, g_ref[...], m_ref, v_ref, d_ref, nm_ref, nv_ref)

    spec = pl.BlockSpec((rows, tc), lambda i: (0, i))
    return pl.pallas_call(
        body, name=name, grid=(n,),
        out_shape=(jax.ShapeDtypeStruct((rows, cols), F32),) * 3,
        in_specs=[spec] * 4, out_specs=(spec,) * 3,
        compiler_params=_params(),
    )(w, g, m, v)


def _adamw_small(vec, dw4, leaves, pool):
    nl = len(VEC_LEAVES) + 1

    def body(*refs):
        vec_ref, dw4_ref = refs[0:2]
        wmv = refs[2:2 + 3 * nl]
        loss_ref = refs[2 + 3 * nl]
        outs = refs[3 + 3 * nl:]
        loss_ref[...] = vec_ref[VEC_LOSS_ROW:VEC_LOSS_ROW + 1, 0:1]
        for k in range(nl):
            if k < nl - 1:
                _, row, width = VEC_LEAVES[k]
                gv = vec_ref[row:row + 1, 0:width]
            else:
                gv = dw4_ref[...]
            w_ref, m_ref, v_ref = wmv[3 * k:3 * k + 3]
            g_ref, d_ref, nm_ref, nv_ref = outs[4 * k:4 * k + 4]
            g_ref[...] = gv
            _adamw_math(w_ref, gv, m_ref, v_ref, d_ref, nm_ref, nv_ref)

    shapes = [jax.ShapeDtypeStruct((1, width), F32) for _, _, width in VEC_LEAVES] + [
        jax.ShapeDtypeStruct(dw4.shape, F32)]
    flat_in = [a for triple in list(leaves) + [pool] for a in triple]
    res = pl.pallas_call(
        body, name="adamw_small",
        out_shape=(jax.ShapeDtypeStruct((1, 1), F32),) + tuple(s for s in shapes for _ in range(4)),
        compiler_params=pltpu.CompilerParams(vmem_limit_bytes=VMEM_LIMIT),
    )(vec, dw4, *flat_in)
    per = [res[1 + 4 * k:5 + 4 * k] for k in range(nl)]
    return res[0], [p[0] for p in per], [p[1] for p in per], [p[2] for p in per], [p[3] for p in per]


def _full_w_in_padded(halves):
    cols = IN_WIDTH // 4
    w_t = halves.reshape(4, 2, cols, D_MODEL // 2).transpose(0, 2, 1, 3).reshape(IN_WIDTH, D_MODEL)
    return jnp.concatenate([
        w_t[0:F_ORIG_LO], w_t[F_ORIG_LO + FOX_HEADS:], w_t[F_ORIG_LO:F_ORIG_LO + FOX_HEADS],
        jnp.zeros((PROJ_PAD - IN_WIDTH, D_MODEL), w_t.dtype)], axis=0)


def _tile_heads(g, n):
    return jnp.tile(g.reshape(1, HEAD_DIM), (1, n))


def kernel(x, mem, norm_g, w_in, b_f, w_pool, pool_scale, fox_q_g, fox_k_g, mem_norm_g, w_mem_kv, mem_q_g, mem_k_g, w_out, loss_target, m_norm_g, m_w_in, m_b_f, m_w_pool, m_pool_scale, m_fox_q_g, m_fox_k_g, m_mem_norm_g, m_w_mem_kv, m_mem_q_g, m_mem_k_g, m_w_out, v_norm_g, v_w_in, v_b_f, v_w_pool, v_pool_scale, v_fox_q_g, v_fox_k_g, v_mem_norm_g, v_w_mem_kv, v_mem_q_g, v_mem_k_g, v_w_out):
    w_in_t, m_w_in_t, v_w_in_t = w_in[0].T, m_w_in[0].T, v_w_in[0].T
    axes = (1, 0, 0)

    g_in, g_kv, g_out = _all_gather_weights([w_in_t, w_mem_kv[0], w_out[0]], axes)
    w_kv_b = g_kv.reshape(D_MODEL, 2 * MEM_WIDTH)
    w_out_b = g_out.reshape(D_MODEL, D_MODEL)
    w4 = w_pool.reshape(POOL_ROWS, HEAD_DIM)
    grad_x, dwp, dw_kv, dw_out, vec_leaves, loss_row, dw4 = _local_grads(
        x[0], mem[0], loss_target[0], g_in, w_kv_b, w_out_b, norm_g, b_f, w4, pool_scale, fox_q_g, fox_k_g,
        mem_norm_g, mem_q_g, mem_k_g)

    gparts = [dwp.reshape(4, IN_WIDTH // 4, D_MODEL), dw_kv.reshape(4, D_MODEL // 4, 2 * MEM_WIDTH),
              dw_out.reshape(4, D_MODEL // 4, D_MODEL)]
    g_w_in_t, g_w_kv, g_w_out, vec, dw4_sum = _grad_reduce(gparts, axes, vec_leaves, loss_row, dw4)

    small_wmv = [(norm_g, m_norm_g, v_norm_g), (mem_norm_g, m_mem_norm_g, v_mem_norm_g),
                 (pool_scale, m_pool_scale, v_pool_scale), (b_f, m_b_f, v_b_f), (fox_q_g, m_fox_q_g, v_fox_q_g),
                 (fox_k_g, m_fox_k_g, v_fox_k_g), (mem_q_g, m_mem_q_g, v_mem_q_g), (mem_k_g, m_mem_k_g, v_mem_k_g)]
    pool_wmv = tuple(a.reshape(POOL_ROWS, HEAD_DIM) for a in (w_pool, m_w_pool, v_w_pool))
    loss, *small_out = _adamw_small(vec, dw4_sum, small_wmv, pool_wmv)
    big = [[g_w_in_t.T[None], g_w_kv[None], g_w_out[None]]]
    upd = [[a.T for a in _adamw("adamw_w_in", w_in_t, g_w_in_t, m_w_in_t, v_w_in_t)],
           _adamw("adamw_w_mem_kv", w_mem_kv[0], g_w_kv, m_w_mem_kv[0], v_w_mem_kv[0]),
           _adamw("adamw_w_out", w_out[0], g_w_out, m_w_out[0], v_w_out[0])]
    big += [[u[k][None] for u in upd] for k in range(3)]

    def leaves(k):
        sm = small_out[k]
        b_in, b_kv, b_out = big[k]
        return (sm[0], b_in, sm[3], sm[8].reshape(w_pool.shape), sm[2], sm[4], sm[5], sm[1], b_kv, sm[6], sm[7], b_out)

    return (loss.reshape(()), grad_x[None], *leaves(0), *leaves(1), *leaves(2), *leaves(3))


def _local_grads(xs, mems, tgt, w_in_b, w_kv_b, w_out_b, norm_g, b_f, w4, pool_scale, fox_q_g, fox_k_g,
                 mem_norm_g, mem_q_g, mem_k_g):
    wp = _full_w_in_padded(w_in_b)
    bf_pad = jnp.pad(b_f, ((0, 0), (0, LANES - FOX_HEADS)))
    fq_g, fk_g = _tile_heads(fox_q_g, FOX_HEADS), _tile_heads(fox_k_g, FOX_HEADS)
    mq_g, mk_g = _tile_heads(mem_q_g, 4), _tile_heads(mem_k_g, 4)

    mnb, kv, kmn, vmb = _mem_fwd(mems, mem_norm_g, w_kv_b, mk_g)
    hb, pa, qk, qa, ka, va, gb, pm, fb = _fwd_in(xs, norm_g, wp, bf_pad, fq_g, fk_g)
    ma, db = _pool_fwd(pa, w4, pool_scale)
    mm = _mem_attn_fwd(pm, kmn, vmb, mq_g)
    o, mb, r4 = _fox_fwd(qa, ka, va, gb)
    dy, dma, dmb, dmm, dw_out, loss_row = _out_loss(xs, tgt, ma, mb, mm, w_out_b)

    dpm, dkmn, dvm, dmq_g = _mem_attn_bwd(pm, dmm, kmn, vmb, mq_g)
    dw_kv, dmemnorm_g, dmk_g = _mem_bwd(dkmn, dvm, kv, mnb, mems, w_kv_b, mk_g, mem_norm_g)
    dpa, dw4, dpscale = _pool_bwd(pa, db, dma, w4, pool_scale)
    doa, dgb, rr = _fox_prep(dmb, gb, o, r4)
    dka, dva, dqa = _fox_bwd(ka, va, qa, doa, rr)
    dqk, dvb, dfb, dfq_g, dfk_g, dbf = _fox_post(dqa, dka, dva, qk, fb, bf_pad, fq_g, fk_g)
    dparts = (dpa, dqk, dvb, dgb, dpm, dfb)
    grad_x, dnorm_g = _in_bwd_x(xs, dy, norm_g, wp, dparts)
    dwp = _in_bwd_w(hb, dparts)

    vec_leaves = (dnorm_g, dmemnorm_g, dpscale, dbf, dfq_g, dfk_g, dmq_g, dmk_g)
    return grad_x, dwp, dw_kv, dw_out, vec_leaves, loss_row, dw4
```

```python
import functools

import jax
import jax.numpy as jnp
from jax import lax
from jax.experimental import pallas as pl
from jax.experimental.pallas import tpu as pltpu

F32 = jnp.float32
BF16 = jnp.bfloat16
MESH = pl.DeviceIdType.MESH

D_MODEL = 1024
HEAD_DIM = 64
POOL_WIDTH = 256
FOX_WIDTH = 512
FOX_HEADS = 8
MEM_WIDTH = 256
N_MEM = 256
IN_WIDTH = 3080
EPS = 1e-6
ATT_SCALE = 0.125

ADAM_LR = 0.001
ADAM_B1 = 0.9
ADAM_B2 = 0.999
ADAM_EPS = 1e-08
ADAM_WD = 0.01
ADAM_STEP = 10

LANES = 128
PA_LO, QB_LO, KB_LO, VB_LO, GB_LO, PM_LO, FB_LO, PROJ_PAD = 0, 512, 1024, 1536, 2048, 2560, 3072, 3200
F_ORIG_LO = 2048

TILE = 512
VMEM_LIMIT = 56 * 1024 * 1024

VEC_LEAVES = (("norm_g", 0, 1024), ("mem_norm_g", 1, 1024), ("pool_scale", 2, 256), ("b_f", 3, 8),
              ("fox_q_g", 4, 64), ("fox_k_g", 5, 64), ("mem_q_g", 6, 64), ("mem_k_g", 7, 64))
VEC_LOSS_ROW = 8
VEC_ROWS = 16
POOL_ROWS = 256


def _params(n_grid=1, vmem=VMEM_LIMIT):
    return pltpu.CompilerParams(dimension_semantics=("arbitrary",) * n_grid, vmem_limit_bytes=vmem)


def _rows(t, w):
    return pl.BlockSpec((t, w), lambda i: (i, 0))


def _rows_rev(t, w, n):
    return pl.BlockSpec((t, w), lambda i: (n - 1 - i, 0))


def _full(shape):
    return pl.BlockSpec(shape, lambda i: (0,) * len(shape))


def _sig(x):
    return 1.0 / (1.0 + jnp.exp(-x))


def _lane_lo(shape):
    return lax.broadcasted_iota(jnp.int32, shape, 1) < HEAD_DIM


def _pair_sum(v, lo):
    s0 = jnp.sum(jnp.where(lo, v, 0.0), axis=-1, keepdims=True)
    s1 = jnp.sum(jnp.where(lo, 0.0, v), axis=-1, keepdims=True)
    return jnp.where(lo, s0, s1)


def _head_rms(blk, lo):
    return lax.rsqrt(_pair_sum(blk * blk, lo) * (1.0 / HEAD_DIM) + EPS)


def _head_norm_bwd(dyn, xhat, rr, g, lo):
    a = dyn * g
    return rr * (a - xhat * (_pair_sum(xhat * a, lo) * (1.0 / HEAD_DIM)))


def _fold_heads(acc):
    tot = acc[:, 0:LANES]
    for p in range(1, acc.shape[1] // LANES):
        tot = tot + acc[:, p * LANES:(p + 1) * LANES]
    return tot + pltpu.roll(tot, HEAD_DIM, axis=1)


def _lane_pick(v, lane, idx):
    return jnp.sum(jnp.where(lane == idx, v, 0.0), axis=-1, keepdims=True)


NT = (((1,), (1,)), ((), ()))
TN = (((0,), (0,)), ((), ()))


def _dot(a, b, dims=None):
    if dims is None:
        return jnp.dot(a, b, preferred_element_type=F32)
    return lax.dot_general(a, b, dims, preferred_element_type=F32)


def _my_place():
    return lax.axis_index("x"), lax.axis_index("y"), lax.axis_index("c")


def _half_dims(shape, axis):
    return (shape[0] // 2, shape[1]) if axis == 0 else (shape[0], shape[1] // 2)


def _half_of(ref, axis, core, lead=False):
    rows, cols = ref.shape[-2:]
    if axis == 0:
        idx = (pl.ds(pl.multiple_of(core * (rows // 2), 16), rows // 2), slice(None))
    else:
        idx = (slice(None), pl.ds(pl.multiple_of(core * (cols // 2), LANES), cols // 2))
    return ref.at[(slice(None),) + idx] if lead else ref.at[idx]


def _all_gather_weights(shards, axes):
    n = len(shards)
    dims = [_half_dims(a.shape, axis) for a, axis in zip(shards, axes)]

    def body(*refs):
        ins, outs = refs[0:n], refs[n:2 * n]
        f32_bufs, bf_bufs = refs[2 * n:3 * n], refs[3 * n:4 * n]
        send_sems, recv_sems, local_sems = refs[4 * n:]
        x, y, c = _my_place()
        me, sibling = (x, y, c), (x, y, 1 - c)
        chips = [(1 - x, y), (x, 1 - y), (1 - x, 1 - y)]

        loads = []
        for a in range(n):
            cp = pltpu.make_async_copy(_half_of(ins[a], axes[a], c), f32_bufs[a], local_sems.at[a])
            cp.start()
            loads.append(cp)

        def blk(a, px, py, pc):
            return outs[a].at[4 * px + 2 * py + pc]

        def copy(a, k, block, to, src=None):
            return pltpu.make_async_remote_copy(
                src_ref=blk(a, *block) if src is None else src, dst_ref=blk(a, *block),
                send_sem=send_sems.at[7 * a + k], recv_sem=recv_sems.at[7 * a + k], device_id=to, device_id_type=MESH)

        first, keeps = [], []
        for a in range(n):
            loads[a].wait()
            bf_bufs[a][...] = f32_bufs[a][...].astype(BF16)
            keep = pltpu.make_async_copy(bf_bufs[a], blk(a, *me), local_sems.at[n + a])
            keep.start()
            keeps.append(keep)
            mine = [copy(a, 0, me, sibling, src=bf_bufs[a])]
            mine += [copy(a, 1 + j, me, (*chip, c), src=bf_bufs[a]) for j, chip in enumerate(chips)]
            for cp in mine:
                cp.start()
            first += mine
        passed = []
        for a in range(n):
            for j, chip in enumerate(chips):
                copy(a, 1 + j, (*chip, c), me).wait_recv()
                cp = copy(a, 4 + j, (*chip, c), sibling)
                cp.start()
                passed.append(cp)
        for a in range(n):
            copy(a, 0, sibling, me).wait_recv()
            for j, chip in enumerate(chips):
                copy(a, 4 + j, (*chip, 1 - c), me).wait_recv()
        for cp in first + passed:
            cp.wait_send()
        for keep in keeps:
            keep.wait()

    any_spec = pl.BlockSpec(memory_space=pl.ANY)
    return pl.pallas_call(
        body, name="weights_all_gather",
        out_shape=tuple(jax.ShapeDtypeStruct((8, h, w), BF16) for h, w in dims),
        in_specs=[any_spec] * n, out_specs=(any_spec,) * n,
        scratch_shapes=[pltpu.VMEM(d, F32) for d in dims] + [pltpu.VMEM(d, BF16) for d in dims] + [
            pltpu.SemaphoreType.DMA((7 * n,)), pltpu.SemaphoreType.DMA((7 * n,)), pltpu.SemaphoreType.DMA((2 * n,))],
        compiler_params=pltpu.CompilerParams(vmem_limit_bytes=VMEM_LIMIT),
    )(*shards)


def _grad_reduce(gparts, axes, vec_leaves, loss_row, dw4):
    n = len(gparts)
    dims = [_half_dims(g.shape[1:], axis) for g, axis in zip(gparts, axes)]
    nv = len(vec_leaves)

    def body(*refs):
        g_refs = refs[0:n]
        leaf_refs = refs[n:n + nv]
        loss_ref, dw4_ref = refs[n + nv:n + nv + 2]
        o = n + nv + 2
        out_refs = refs[o:o + n]
        vec_out, dw4_out = refs[o + n:o + n + 2]
        s0 = o + n + 2
        recv_a, own_a = refs[s0:s0 + n], refs[s0 + n:s0 + 2 * n]
        send_b, recv_b = refs[s0 + 2 * n:s0 + 3 * n], refs[s0 + 3 * n:s0 + 4 * n]
        fin = refs[s0 + 4 * n:s0 + 5 * n]
        vec_mine, vec_recv, dw4_recv, send_sems, recv_sems, local_sems = refs[s0 + 5 * n:]

        x, y, c = _my_place()
        chip = 2 * x + y
        me_lin = 4 * x + 2 * y + c
        sibling = (x, y, 1 - c)

        to_sib, own = [], []
        for a in range(n):
            cp = pltpu.make_async_remote_copy(
                src_ref=_half_of(g_refs[a], axes[a], 1 - c, lead=True), dst_ref=recv_a[a], send_sem=send_sems.at[5 * a],
                recv_sem=recv_sems.at[5 * a], device_id=sibling, device_id_type=MESH)
            cp.start()
            to_sib.append(cp)
            cp = pltpu.make_async_copy(_half_of(g_refs[a], axes[a], c, lead=True), own_a[a], local_sems.at[a])
            cp.start()
            own.append(cp)

        vec_mine[...] = jnp.zeros_like(vec_mine)
        for (_, row, width), ref in zip(VEC_LEAVES, leaf_refs):
            vec_mine[row:row + 1, 0:ref.shape[1]] = ref[...]
        vec_mine[VEC_LOSS_ROW:VEC_LOSS_ROW + 1, 0:LANES] = loss_ref[...]
        small_copies = []
        for k in range(1, 8):
            peer = (me_lin + k) % 8
            to = (peer // 4, (peer // 2) % 2, peer % 2)
            for src, dst, base in ((vec_mine, vec_recv, 5 * n), (dw4_ref, dw4_recv, 5 * n + 7)):
                cp = pltpu.make_async_remote_copy(
                    src_ref=src, dst_ref=dst.at[me_lin], send_sem=send_sems.at[base + k - 1],
                    recv_sem=recv_sems.at[base + k - 1], device_id=to, device_id_type=MESH)
                cp.start()
                small_copies.append(cp)

        chip_copies = []
        for a in range(n):
            own[a].wait()
            to_sib[a].wait_recv()
            for j in range(4):
                send_b[a][j] = (own_a[a][j] + recv_a[a][j]).astype(BF16)
            for k in range(1, 4):
                dest = (chip + k) % 4
                cp = pltpu.make_async_remote_copy(
                    src_ref=send_b[a].at[dest], dst_ref=recv_b[a].at[chip], send_sem=send_sems.at[5 * a + k],
                    recv_sem=recv_sems.at[5 * a + k], device_id=(dest // 2, dest % 2, c), device_id_type=MESH)
                cp.start()
                chip_copies.append(cp)
            keep = pltpu.make_async_copy(send_b[a].at[chip], recv_b[a].at[chip], local_sems.at[n + a])
            keep.start()
            keep.wait()

        give, mine = [], []
        for a in range(n):
            for cp in chip_copies[3 * a:3 * a + 3]:
                cp.wait_recv()
            tot = recv_b[a][0].astype(F32) + recv_b[a][1].astype(F32)
            tot = tot + recv_b[a][2].astype(F32)
            fin[a][...] = tot + recv_b[a][3].astype(F32)
            cp = pltpu.make_async_remote_copy(
                src_ref=fin[a], dst_ref=_half_of(out_refs[a], axes[a], c), send_sem=send_sems.at[5 * a + 4],
                recv_sem=recv_sems.at[5 * a + 4], device_id=sibling, device_id_type=MESH)
            cp.start()
            give.append(cp)
            cp = pltpu.make_async_copy(fin[a], _half_of(out_refs[a], axes[a], c), local_sems.at[a])
            cp.start()
            mine.append(cp)

        for cp in small_copies:
            cp.wait_recv()
        vec_recv[me_lin] = vec_mine[...]
        dw4_recv[me_lin] = dw4_ref[...]
        vtot, wtot = vec_recv[0], dw4_recv[0]
        for d in range(1, 8):
            vtot = vtot + vec_recv[d]
            wtot = wtot + dw4_recv[d]
        vec_out[...] = vtot
        dw4_out[...] = wtot

        for a in range(n):
            give[a].wait_recv()
            mine[a].wait()
            to_sib[a].wait_send()
            give[a].wait_send()
        for cp in chip_copies + small_copies:
            cp.wait_send()

    any_spec = pl.BlockSpec(memory_space=pl.ANY)
    vmem_spec = pl.BlockSpec(memory_space=pltpu.VMEM)
    n_sems = 5 * n + 14
    scratch = []
    for dtype, lead in ((F32, (4,)), (F32, (4,)), (BF16, (4,)), (BF16, (4,)), (F32, ())):
        scratch += [pltpu.VMEM(lead + d, dtype) for d in dims]
    scratch += [pltpu.VMEM((VEC_ROWS, D_MODEL), F32), pltpu.VMEM((8, VEC_ROWS, D_MODEL), F32),
                pltpu.VMEM((8,) + dw4.shape, F32),
                pltpu.SemaphoreType.DMA((n_sems,)), pltpu.SemaphoreType.DMA((n_sems,)), pltpu.SemaphoreType.DMA((2 * n,))]
    return pl.pallas_call(
        body, name="grad_reduce",
        out_shape=tuple(jax.ShapeDtypeStruct(g.shape[1:], F32) for g in gparts) + (
            jax.ShapeDtypeStruct((VEC_ROWS, D_MODEL), F32), jax.ShapeDtypeStruct(dw4.shape, F32)),
        in_specs=[any_spec] * n + [vmem_spec] * (nv + 2),
        out_specs=(any_spec,) * n + (vmem_spec, vmem_spec),
        scratch_shapes=scratch,
        compiler_params=pltpu.CompilerParams(vmem_limit_bytes=VMEM_LIMIT),
    )(*gparts, *vec_leaves, loss_row, dw4)


def _mem_fwd(mem, mem_norm_g, w_kv, mk_g):
    n = mem.shape[0]

    def body(mem_ref, g_ref, w_ref, kg_ref, mn_ref, kv_ref, kn_ref, vm_ref):
        xm = mem_ref[...]
        rr = lax.rsqrt(jnp.mean(xm * xm, axis=-1, keepdims=True) + EPS)
        mnb = ((xm * rr) * g_ref[...]).astype(BF16)
        mn_ref[...] = mnb
        kv = _dot(mnb, w_ref[...])
        kv_ref[...] = kv
        lo = _lane_lo((n, LANES))
        for p in range(MEM_WIDTH // LANES):
            sl = slice(p * LANES, (p + 1) * LANES)
            kb = kv[:, sl]
            kn_ref[:, sl] = ((kb * _head_rms(kb, lo)) * kg_ref[:, sl]).astype(BF16)
        vm_ref[...] = kv[:, MEM_WIDTH:].astype(BF16)

    return pl.pallas_call(
        body, name="mem_fwd",
        out_shape=(jax.ShapeDtypeStruct((n, D_MODEL), BF16), jax.ShapeDtypeStruct((n, 2 * MEM_WIDTH), F32),
                   jax.ShapeDtypeStruct((n, MEM_WIDTH), BF16), jax.ShapeDtypeStruct((n, MEM_WIDTH), BF16)),
        compiler_params=pltpu.CompilerParams(vmem_limit_bytes=VMEM_LIMIT),
    )(mem, mem_norm_g, w_kv, mk_g)


AUG_LO = 64
KEY_SUM_LANE = 72
QUERY_SUM_LANE = 80
HEAD_BLOCKS = FOX_HEADS * LANES


def _ones3(lane):
    return jnp.where((lane >= AUG_LO) & (lane < AUG_LO + 3), 1.0, 0.0)


def _spread3(cols):
    hi = cols.astype(BF16)
    rest = cols - hi.astype(F32)
    mid = rest.astype(BF16)
    low = (rest - mid.astype(F32)).astype(BF16)
    r = lax.broadcasted_iota(jnp.int32, (LANES, HEAD_BLOCKS), 0)
    c = lax.broadcasted_iota(jnp.int32, (LANES, HEAD_BLOCKS), 1)
    out = None
    for k, part in enumerate((hi, mid, low)):
        term = _dot(part, jnp.where(c == r * LANES + (AUG_LO + k), 1.0, 0.0).astype(BF16))
        out = term if out is None else out + term
    return out


def _head_block(pair_blk, hh, lo, extras):
    src = pair_blk if hh == 0 else pltpu.roll(pair_blk, HEAD_DIM, axis=1)
    return jnp.where(lo, src, extras).astype(BF16)


def _pair_block(blk0, blk1, lo):
    return jnp.where(lo, blk0, pltpu.roll(blk1, HEAD_DIM, axis=1))


def _fwd_in(x, norm_g, wp, bf_pad, fq_g, fk_g):
    s = x.shape[0]
    t = TILE
    n = s // t

    def body(x_ref, ng_ref, wp_ref, bf_ref, qg_ref, kg_ref,
             h_ref, pa_ref, qk_ref, qa_ref, ka_ref, va_ref, gb_ref, pm_ref, fb_ref, carry_ref, fcol_ref):
        @pl.when(pl.program_id(0) == 0)
        def _():
            carry_ref[...] = jnp.zeros_like(carry_ref)

        xv = x_ref[...]
        rr = lax.rsqrt(jnp.mean(xv * xv, axis=-1, keepdims=True) + EPS)
        hb = ((xv * rr) * ng_ref[...]).astype(BF16)
        h_ref[...] = hb

        def proj(lo, hi):
            return _dot(hb, wp_ref[lo:hi, :], NT)

        pa_ref[...] = proj(PA_LO, QB_LO)
        gb_ref[...] = proj(GB_LO, PM_LO)
        pm_ref[...] = proj(PM_LO, FB_LO)
        fb = proj(FB_LO, PROJ_PAD)
        fb_ref[...] = fb

        lane = lax.broadcasted_iota(jnp.int32, (t, LANES), 1)
        row = lax.broadcasted_iota(jnp.int32, (t, LANES), 0)
        lo = lane < HEAD_DIM
        z = fb + bf_ref[...]
        lf = -(jnp.maximum(-z, 0.0) + jnp.log1p(jnp.exp(-jnp.abs(z))))
        lf = jnp.where(lane < FOX_HEADS, lf, 0.0)
        sh = 1
        while sh < t:
            lf = lf + jnp.where(row >= sh, pltpu.roll(lf, sh, axis=0), 0.0)
            sh *= 2
        fcum = lf + carry_ref[...]
        fcol_ref[...] = fcum
        carry_ref[...] = fcol_ref[t - 1:t, :]

        ones3 = _ones3(lane)
        minus_f = _spread3(-fcum)
        for seg, g_ref, out_ref, scale in ((QB_LO, qg_ref, qa_ref, ATT_SCALE), (KB_LO, kg_ref, ka_ref, 1.0)):
            raw = proj(seg, seg + FOX_WIDTH)
            qk_ref[:, seg - QB_LO:seg - QB_LO + FOX_WIDTH] = raw
            for p in range(FOX_WIDTH // LANES):
                sl = slice(p * LANES, (p + 1) * LANES)
                blk = raw[:, sl]
                normed = ((blk * _head_rms(blk, lo)) * g_ref[:, sl]) * scale
                for hh in range(2):
                    h = 2 * p + hh
                    if seg == QB_LO:
                        extras = jnp.where(lane == QUERY_SUM_LANE + h, 1.0, ones3)
                    else:
                        extras = jnp.where(lane == KEY_SUM_LANE + h, 1.0, minus_f[:, h * LANES:(h + 1) * LANES])
                    out_ref[:, h * LANES:(h + 1) * LANES] = _head_block(normed, hh, lo, extras)
        vraw = proj(VB_LO, GB_LO)
        for h in range(FOX_HEADS):
            va_ref[:, h * LANES:(h + 1) * LANES] = _head_block(vraw[:, (h // 2) * LANES:(h // 2 + 1) * LANES], h % 2, lo, ones3)

    outs = (
        jax.ShapeDtypeStruct((s, D_MODEL), BF16),
        jax.ShapeDtypeStruct((s, 512), F32),
        jax.ShapeDtypeStruct((s, 2 * FOX_WIDTH), F32),
        jax.ShapeDtypeStruct((s, HEAD_BLOCKS), BF16),
        jax.ShapeDtypeStruct((s, HEAD_BLOCKS), BF16),
        jax.ShapeDtypeStruct((s, HEAD_BLOCKS), BF16),
        jax.ShapeDtypeStruct((s, FOX_WIDTH), F32),
        jax.ShapeDtypeStruct((s, 512), F32),
        jax.ShapeDtypeStruct((s, LANES), F32),
    )
    return pl.pallas_call(
        body, name="fwd_in", grid=(n,), out_shape=outs,
        in_specs=[_rows(t, D_MODEL), _full((1, D_MODEL)), _full((PROJ_PAD, D_MODEL)), _full((1, LANES)),
                  _full((1, FOX_WIDTH)), _full((1, FOX_WIDTH))],
        out_specs=(_rows(t, D_MODEL), _rows(t, 512), _rows(t, 2 * FOX_WIDTH), _rows(t, HEAD_BLOCKS),
                   _rows(t, HEAD_BLOCKS), _rows(t, HEAD_BLOCKS), _rows(t, FOX_WIDTH), _rows(t, 512),
                   _rows(t, LANES)),
        scratch_shapes=[pltpu.VMEM((1, LANES), F32), pltpu.VMEM((t, LANES), F32)],
        compiler_params=_params(),
    )(x, norm_g, wp, bf_pad, fq_g, fk_g)


POOL_HALO = 16


def _pool_window(lane):
    return jnp.where(lane < 64, 2.0, jnp.where(lane < 128, 4.0, jnp.where(lane < 192, 8.0, 16.0)))


def _pool_pick(lane, s2, s4, s8, s16):
    return jnp.where(lane < 64, s2, jnp.where(lane < 128, s4, jnp.where(lane < 192, s8, s16)))


def _group_onehot(shape, row_is_group_lane):
    r = lax.broadcasted_iota(jnp.int32, shape, 0)
    c = lax.broadcasted_iota(jnp.int32, shape, 1)
    hit = (r % HEAD_DIM == c) if row_is_group_lane else (c % HEAD_DIM == r)
    return jnp.where(hit, 1.0, 0.0).astype(F32)


def _same_group(shape):
    r = lax.broadcasted_iota(jnp.int32, shape, 0)
    c = lax.broadcasted_iota(jnp.int32, shape, 1)
    return (r // HEAD_DIM) == (c // HEAD_DIM)


def _pool_block_diag(w4):
    spread = jnp.dot(w4, _group_onehot((HEAD_DIM, POOL_WIDTH), False), preferred_element_type=F32,
                     precision=lax.Precision.HIGHEST)
    return jnp.where(_same_group((POOL_WIDTH, POOL_WIDTH)), spread, 0.0).astype(BF16)


def _pool_fwd(pa, w4, pscale):
    s = pa.shape[0]
    t = TILE
    n = s // t
    ext = t + POOL_HALO

    def body(pa_ref, w4_ref, sc_ref, ma_ref, d_ref, ext_ref, w_ref):
        i = pl.program_id(0)

        @pl.when(i == 0)
        def _():
            ext_ref[0:POOL_HALO, :] = jnp.zeros((POOL_HALO, POOL_WIDTH), F32)
            w_ref[...] = _pool_block_diag(w4_ref[...])

        u = pa_ref[:, 0:POOL_WIDTH]
        ext_ref[POOL_HALO:ext, :] = u
        e = ext_ref[...]
        s2 = e + pltpu.roll(e, 1, axis=0)
        s4 = s2 + pltpu.roll(s2, 2, axis=0)
        s8 = s4 + pltpu.roll(s4, 4, axis=0)
        s16 = s8 + pltpu.roll(s8, 8, axis=0)
        lane_e = lax.broadcasted_iota(jnp.int32, (ext, POOL_WIDTH), 1)
        win = _pool_pick(lane_e, s2, s4, s8, s16)[POOL_HALO:ext, :]
        lane = lax.broadcasted_iota(jnp.int32, (t, POOL_WIDTH), 1)
        pos = (lax.broadcasted_iota(jnp.int32, (t, POOL_WIDTH), 0) + (i * t + 1)).astype(F32)
        d = win / jnp.minimum(pos, _pool_window(lane)) - u
        db = d.astype(BF16)
        d_ref[...] = db
        ya = _dot(db, w_ref[...]) * sc_ref[...]
        ga = pa_ref[:, POOL_WIDTH:2 * POOL_WIDTH]
        ma_ref[...] = (ya * (ga * _sig(ga))).astype(BF16)
        ext_ref[0:POOL_HALO, :] = ext_ref[t:ext, :]

    return pl.pallas_call(
        body, name="pool_fwd", grid=(n,),
        out_shape=(jax.ShapeDtypeStruct((s, POOL_WIDTH), BF16), jax.ShapeDtypeStruct((s, POOL_WIDTH), BF16)),
        in_specs=[_rows(t, 512), _full((POOL_ROWS, HEAD_DIM)), _full((1, POOL_WIDTH))],
        out_specs=(_rows(t, POOL_WIDTH), _rows(t, POOL_WIDTH)),
        scratch_shapes=[pltpu.VMEM((ext, POOL_WIDTH), F32), pltpu.VMEM((POOL_WIDTH, POOL_WIDTH), BF16)],
        compiler_params=_params(),
    )(pa, w4, pscale)


def _mem_softmax(qm, kp):
    sc = _dot(qm, kp, NT)
    e = jnp.exp(sc - jnp.max(sc, axis=-1, keepdims=True))
    return e * (1.0 / jnp.sum(e, axis=-1, keepdims=True))


def _mem_attn_fwd(pm, kmn, vmb, mq_g):
    s = pm.shape[0]
    t = TILE
    n = s // t

    def body(pm_ref, k_ref, v_ref, g_ref, mm_ref):
        lo = _lane_lo((t, LANES))
        for p in range(MEM_WIDTH // LANES):
            sl = slice(p * LANES, (p + 1) * LANES)
            qb = pm_ref[:, sl]
            qs = (((qb * _head_rms(qb, lo)) * g_ref[:, sl]) * ATT_SCALE).astype(BF16)
            kp = k_ref[:, sl]
            vp = v_ref[:, sl]
            outs = []
            for hh in range(2):
                msk = lo if hh == 0 else jnp.logical_not(lo)
                prob = _mem_softmax(jnp.where(msk, qs, jnp.zeros_like(qs)), kp)
                outs.append(_dot(prob.astype(BF16), vp))
            o = jnp.where(lo, outs[0], outs[1])
            gm = pm_ref[:, MEM_WIDTH + p * LANES:MEM_WIDTH + (p + 1) * LANES]
            mm_ref[:, sl] = (o * (gm * _sig(gm))).astype(BF16)

    return pl.pallas_call(
        body, name="mem_attn_fwd", grid=(n,),
        out_shape=jax.ShapeDtypeStruct((s, MEM_WIDTH), BF16),
        in_specs=[_rows(t, 512), _full((N_MEM, MEM_WIDTH)), _full((N_MEM, MEM_WIDTH)), _full((1, MEM_WIDTH))],
        out_specs=_rows(t, MEM_WIDTH),
        compiler_params=_params(),
    )(pm, kmn, vmb, mq_g)


def _fox_fwd(qa, ka, va, gb):
    s = qa.shape[0]
    t = TILE
    n = s // t
    pair_w = 2 * LANES

    def body(qa_ref, ka_ref, va_ref, gb_ref, o_ref, mb_ref, r_ref):
        i = pl.program_id(1)
        lane = lax.broadcasted_iota(jnp.int32, (t, LANES), 1)
        lo = lane < HEAD_DIM
        causal = lax.broadcasted_iota(jnp.int32, (t, t), 1) <= lax.broadcasted_iota(jnp.int32, (t, t), 0)
        qas = (qa_ref[:, 0:LANES], qa_ref[:, LANES:pair_w])

        def step(j, carry, masked):
            rows = pl.ds(pl.multiple_of(j * t, t), t)
            new = []
            for hh in range(2):
                cols = slice(hh * LANES, (hh + 1) * LANES)
                m, acc = carry[hh]
                sc = _dot(qas[hh], ka_ref[rows, cols], NT)
                if masked:
                    sc = jnp.where(causal, sc, -1e30)
                m_new = jnp.maximum(m, jnp.max(sc, axis=-1, keepdims=True))
                acc = jnp.exp(m - m_new) * acc + _dot(jnp.exp(sc - m_new).astype(BF16), va_ref[rows, cols])
                new.append((m_new, acc))
            return tuple(new)

        init = (jnp.full((t, 1), -1e30, F32), jnp.zeros((t, LANES), F32))
        carry = lax.fori_loop(0, i, functools.partial(step, masked=False), (init, init))
        outs = []
        rcol = jnp.zeros((t, LANES), F32)
        for hh, (m, acc) in enumerate(step(i, carry, masked=True)):
            l = _lane_pick(acc, lane, AUG_LO)
            outs.append(acc * (1.0 / l))
            rcol = jnp.where(lane == hh, m + jnp.log(l), rcol)
        o = _pair_block(outs[0], outs[1], lo)
        o_ref[...] = o
        g = gb_ref[...]
        mb_ref[...] = (o * (g * _sig(g))).astype(BF16)
        r_ref[0] = rcol

    return pl.pallas_call(
        body, name="fox_fwd", grid=(FOX_HEADS // 2, n),
        out_shape=(jax.ShapeDtypeStruct((s, FOX_WIDTH), F32), jax.ShapeDtypeStruct((s, FOX_WIDTH), BF16),
                   jax.ShapeDtypeStruct((FOX_HEADS // 2, s, LANES), F32)),
        in_specs=[pl.BlockSpec((t, pair_w), lambda p, i: (i, p)), pl.BlockSpec((s, pair_w), lambda p, i: (0, p)),
                  pl.BlockSpec((s, pair_w), lambda p, i: (0, p)), pl.BlockSpec((t, LANES), lambda p, i: (i, p))],
        out_specs=(pl.BlockSpec((t, LANES), lambda p, i: (i, p)), pl.BlockSpec((t, LANES), lambda p, i: (i, p)),
                   pl.BlockSpec((1, t, LANES), lambda p, i: (p, i, 0))),
        compiler_params=_params(2),
    )(qa, ka, va, gb)


def _out_loss(x, tgt, ma, mb, mm, wout):
    s = x.shape[0]
    t = TILE
    n = s // t

    def body(x_ref, t_ref, ma_ref, mb_ref, mm_ref, w_ref, dy_ref, dma_ref, dmb_ref, dmm_ref, dw_ref, loss_ref, mix_ref):
        @pl.when(pl.program_id(0) == 0)
        def _():
            dw_ref[...] = jnp.zeros_like(dw_ref)
            loss_ref[...] = jnp.zeros_like(loss_ref)

        mix_ref[:, 0:256] = ma_ref[...]
        mix_ref[:, 256:768] = mb_ref[...]
        mix_ref[:, 768:1024] = mm_ref[...]
        mix = mix_ref[...]
        err = (x_ref[...] + _dot(mix, w_ref[...])) - t_ref[...]
        row_mean = jnp.sum(err * err, axis=-1, keepdims=True) * (1.0 / D_MODEL)
        loss_ref[...] += 0.5 * jnp.sum(row_mean, axis=0, keepdims=True)
        dy = err * (1.0 / D_MODEL)
        dy_ref[...] = dy
        dyb = dy.astype(BF16)
        dmix = _dot(dyb, w_ref[...], NT)
        dma_ref[...] = dmix[:, 0:256]
        dmb_ref[...] = dmix[:, 256:768]
        dmm_ref[...] = dmix[:, 768:1024]
        dw_ref[...] += _dot(mix, dyb, TN)

    return pl.pallas_call(
        body, name="out_loss", grid=(n,),
        out_shape=(jax.ShapeDtypeStruct((s, D_MODEL), F32), jax.ShapeDtypeStruct((s, 256), F32),
                   jax.ShapeDtypeStruct((s, 512), F32), jax.ShapeDtypeStruct((s, 256), F32),
                   jax.ShapeDtypeStruct((D_MODEL, D_MODEL), F32), jax.ShapeDtypeStruct((1, LANES), F32)),
        in_specs=[_rows(t, D_MODEL), _rows(t, D_MODEL), _rows(t, 256), _rows(t, 512), _rows(t, 256),
                  _full((D_MODEL, D_MODEL))],
        out_specs=(_rows(t, D_MODEL), _rows(t, 256), _rows(t, 512), _rows(t, 256), _full((D_MODEL, D_MODEL)),
                   _full((1, LANES))),
        scratch_shapes=[pltpu.VMEM((t, D_MODEL), BF16)],
        compiler_params=_params(),
    )(x, tgt, ma, mb, mm, wout)


def _mem_attn_bwd(pm, dmm, kmn, vmb, mq_g):
    s = pm.shape[0]
    t = TILE
    n = s // t

    def body(pm_ref, dmm_ref, k_ref, v_ref, g_ref, dpm_ref, dk_ref, dv_ref, dg_ref, gacc_ref):
        @pl.when(pl.program_id(0) == 0)
        def _():
            dk_ref[...] = jnp.zeros_like(dk_ref)
            dv_ref[...] = jnp.zeros_like(dv_ref)
            gacc_ref[...] = jnp.zeros_like(gacc_ref)

        lo = _lane_lo((t, LANES))
        for p in range(MEM_WIDTH // LANES):
            sl = slice(p * LANES, (p + 1) * LANES)
            qb = pm_ref[:, sl]
            rr = _head_rms(qb, lo)
            qhat = qb * rr
            g = g_ref[:, sl]
            qs = ((qhat * g) * ATT_SCALE).astype(BF16)
            gm = pm_ref[:, MEM_WIDTH + p * LANES:MEM_WIDTH + (p + 1) * LANES]
            sg = _sig(gm)
            dmo = dmm_ref[:, sl]
            d_o = dmo * (gm * sg)
            kp = k_ref[:, sl]
            vp = v_ref[:, sl]
            outs, dqs = [], []
            for hh in range(2):
                msk = lo if hh == 0 else jnp.logical_not(lo)
                qm = jnp.where(msk, qs, jnp.zeros_like(qs))
                prob = _mem_softmax(qm, kp)
                pb = prob.astype(BF16)
                outs.append(_dot(pb, vp))
                dom = jnp.where(msk, d_o, 0.0).astype(BF16)
                dp = _dot(dom, vp, NT)
                ds = (prob * (dp - jnp.sum(prob * dp, axis=-1, keepdims=True))).astype(BF16)
                dqs.append(_dot(ds, kp))
                dk_ref[:, sl] += _dot(ds, qm, TN)
                dv_ref[:, sl] += _dot(pb, dom, TN)
            o = jnp.where(lo, outs[0], outs[1])
            dqn = jnp.where(lo, dqs[0], dqs[1]) * ATT_SCALE
            dpm_ref[:, sl] = _head_norm_bwd(dqn, qhat, rr, g, lo).astype(BF16)
            dpm_ref[:, MEM_WIDTH + p * LANES:MEM_WIDTH + (p + 1) * LANES] = (
                dmo * o * (sg * (1.0 + gm * (1.0 - sg)))).astype(BF16)
            gacc_ref[:, sl] += jnp.sum(dqn * qhat, axis=0, keepdims=True)

        @pl.when(pl.program_id(0) == n - 1)
        def _():
            dg_ref[...] = _fold_heads(gacc_ref[...])

    return pl.pallas_call(
        body, name="mem_attn_bwd", grid=(n,),
        out_shape=(jax.ShapeDtypeStruct((s, 512), BF16), jax.ShapeDtypeStruct((N_MEM, MEM_WIDTH), F32),
                   jax.ShapeDtypeStruct((N_MEM, MEM_WIDTH), F32), jax.ShapeDtypeStruct((1, LANES), F32)),
        in_specs=[_rows(t, 512), _rows(t, MEM_WIDTH), _full((N_MEM, MEM_WIDTH)), _full((N_MEM, MEM_WIDTH)),
                  _full((1, MEM_WIDTH))],
        out_specs=(_rows(t, 512), _full((N_MEM, MEM_WIDTH)), _full((N_MEM, MEM_WIDTH)), _full((1, LANES))),
        scratch_shapes=[pltpu.VMEM((1, MEM_WIDTH), F32)],
        compiler_params=_params(),
    )(pm, dmm, kmn, vmb, mq_g)


def _mem_bwd(dkn, dvm, kv, mnb, mem, w_kv, mk_g, mem_norm_g):
    n = mem.shape[0]

    def body(dkn_ref, dvm_ref, kv_ref, mn_ref, mem_ref, w_ref, kg_ref, g_ref, dw_ref, dg_ref, dkg_ref, dkv_ref):
        lo = _lane_lo((n, LANES))
        gacc = []
        for p in range(MEM_WIDTH // LANES):
            sl = slice(p * LANES, (p + 1) * LANES)
            kb = kv_ref[:, sl]
            rr = _head_rms(kb, lo)
            khat = kb * rr
            dk = dkn_ref[:, sl]
            dkv_ref[:, sl] = _head_norm_bwd(dk, khat, rr, kg_ref[:, sl], lo).astype(BF16)
            gacc.append(jnp.sum(dk * khat, axis=0, keepdims=True))
        dkg_ref[...] = _fold_heads(jnp.concatenate(gacc, axis=1))
        dkv_ref[:, MEM_WIDTH:] = dvm_ref[...].astype(BF16)
        dkv = dkv_ref[...]
        dw_ref[...] = _dot(mn_ref[...], dkv, TN)
        dmn = _dot(dkv, w_ref[...], NT)
        xm = mem_ref[...]
        rr = lax.rsqrt(jnp.mean(xm * xm, axis=-1, keepdims=True) + EPS)
        dg_ref[...] = jnp.sum(dmn * (xm * rr), axis=0, keepdims=True)

    return pl.pallas_call(
        body, name="mem_bwd",
        out_shape=(jax.ShapeDtypeStruct((D_MODEL, 2 * MEM_WIDTH), F32), jax.ShapeDtypeStruct((1, D_MODEL), F32),
                   jax.ShapeDtypeStruct((1, LANES), F32)),
        scratch_shapes=[pltpu.VMEM((n, 2 * MEM_WIDTH), BF16)],
        compiler_params=pltpu.CompilerParams(vmem_limit_bytes=VMEM_LIMIT),
    )(dkn, dvm, kv, mnb, mem, w_kv, mk_g, mem_norm_g)


def _pool_bwd(pa, db, dma, w4, pscale):
    s = pa.shape[0]
    t = TILE
    n = s // t
    ext = t + POOL_HALO

    def body(pa_ref, d_ref, dma_ref, w4_ref, sc_ref, dpa_ref, dw4_ref, dsc_ref, ext_ref, w_ref, dw_ref):
        i = pl.program_id(0)

        @pl.when(i == 0)
        def _():
            dw_ref[...] = jnp.zeros_like(dw_ref)
            dsc_ref[...] = jnp.zeros_like(dsc_ref)
            ext_ref[t:ext, :] = jnp.zeros((POOL_HALO, POOL_WIDTH), F32)
            w_ref[...] = _pool_block_diag(w4_ref[...])

        dbv = d_ref[...]
        z = _dot(dbv, w_ref[...])
        ga = pa_ref[:, POOL_WIDTH:2 * POOL_WIDTH]
        sg = _sig(ga)
        dma_v = dma_ref[...]
        dya = dma_v * (ga * sg)
        dpa_ref[:, POOL_WIDTH:2 * POOL_WIDTH] = (dma_v * (z * sc_ref[...]) * (sg * (1.0 + ga * (1.0 - sg)))).astype(BF16)
        dsc_ref[...] += jnp.sum(dya * z, axis=0, keepdims=True)
        dzb = (dya * sc_ref[...]).astype(BF16)
        dw_ref[...] += _dot(dbv, dzb, TN)
        dd = _dot(dzb, w_ref[...], NT)
        lane = lax.broadcasted_iota(jnp.int32, (t, POOL_WIDTH), 1)
        pos = (lax.broadcasted_iota(jnp.int32, (t, POOL_WIDTH), 0) + ((n - 1 - i) * t + 1)).astype(F32)
        ext_ref[0:t, :] = dd / jnp.minimum(pos, _pool_window(lane))
        e = ext_ref[...]
        s2 = e + pltpu.roll(e, ext - 1, axis=0)
        s4 = s2 + pltpu.roll(s2, ext - 2, axis=0)
        s8 = s4 + pltpu.roll(s4, ext - 4, axis=0)
        s16 = s8 + pltpu.roll(s8, ext - 8, axis=0)
        lane_e = lax.broadcasted_iota(jnp.int32, (ext, POOL_WIDTH), 1)
        win = _pool_pick(lane_e, s2, s4, s8, s16)[0:t, :]
        dpa_ref[:, 0:POOL_WIDTH] = (win - dd).astype(BF16)
        ext_ref[t:ext, :] = ext_ref[0:POOL_HALO, :]

        @pl.when(i == n - 1)
        def _():
            own = jnp.where(_same_group((POOL_WIDTH, POOL_WIDTH)), dw_ref[...], 0.0)
            dw4_ref[...] = jnp.dot(own, _group_onehot((POOL_WIDTH, HEAD_DIM), True), preferred_element_type=F32,
                                   precision=lax.Precision.HIGHEST)

    return pl.pallas_call(
        body, name="pool_bwd", grid=(n,),
        out_shape=(jax.ShapeDtypeStruct((s, 512), BF16), jax.ShapeDtypeStruct((POOL_ROWS, HEAD_DIM), F32),
                   jax.ShapeDtypeStruct((1, POOL_WIDTH), F32)),
        in_specs=[_rows_rev(t, 512, n), _rows_rev(t, POOL_WIDTH, n), _rows_rev(t, POOL_WIDTH, n),
                  _full((POOL_ROWS, HEAD_DIM)), _full((1, POOL_WIDTH))],
        out_specs=(_rows_rev(t, 512, n), _full((POOL_ROWS, HEAD_DIM)), _full((1, POOL_WIDTH))),
        scratch_shapes=[pltpu.VMEM((ext, POOL_WIDTH), F32), pltpu.VMEM((POOL_WIDTH, POOL_WIDTH), BF16),
                        pltpu.VMEM((POOL_WIDTH, POOL_WIDTH), F32)],
        compiler_params=_params(),
    )(pa, db, dma, w4, pscale)


def _fox_prep(dmb, gb, o, r4):
    s = dmb.shape[0]
    t = TILE
    n = s // t
    pairs = FOX_HEADS // 2

    def body(dmb_ref, gb_ref, o_ref, r_ref, doa_ref, dgb_ref, rr_ref):
        lane = lax.broadcasted_iota(jnp.int32, (t, LANES), 1)
        lo = lane < HEAD_DIM
        d_os = []
        delta = jnp.zeros((t, LANES), F32)
        for p in range(pairs):
            sl = slice(p * LANES, (p + 1) * LANES)
            g = gb_ref[:, sl]
            sg = _sig(g)
            dm = dmb_ref[:, sl]
            ov = o_ref[:, sl]
            d_o = dm * (g * sg)
            d_os.append(d_o)
            dgb_ref[:, sl] = (dm * ov * (sg * (1.0 + g * (1.0 - sg)))).astype(BF16)
            prod = d_o * ov
            delta = jnp.where(lane == 2 * p, jnp.sum(jnp.where(lo, prod, 0.0), axis=-1, keepdims=True), delta)
            delta = jnp.where(lane == 2 * p + 1, jnp.sum(jnp.where(lo, 0.0, prod), axis=-1, keepdims=True), delta)
            rr_ref[p, 0] = r_ref[p].T[0:8, :]
        minus_delta = _spread3(-delta)
        for h in range(FOX_HEADS):
            blk = slice(h * LANES, (h + 1) * LANES)
            doa_ref[:, blk] = _head_block(d_os[h // 2], h % 2, lo, minus_delta[:, blk])

    return pl.pallas_call(
        body, name="fox_prep", grid=(n,),
        out_shape=(jax.ShapeDtypeStruct((s, HEAD_BLOCKS), BF16), jax.ShapeDtypeStruct((s, FOX_WIDTH), BF16),
                   jax.ShapeDtypeStruct((pairs, n, 8, t), F32)),
        in_specs=[_rows(t, FOX_WIDTH), _rows(t, FOX_WIDTH), _rows(t, FOX_WIDTH),
                  pl.BlockSpec((pairs, t, LANES), lambda i: (0, i, 0))],
        out_specs=(_rows(t, HEAD_BLOCKS), _rows(t, FOX_WIDTH), pl.BlockSpec((pairs, 1, 8, t), lambda i: (0, i, 0, 0))),
        compiler_params=_params(),
    )(dmb, gb, o, r4)


def _fox_bwd(ka, va, qa, doa, rr):
    s = ka.shape[0]
    t = TILE
    n = s // t
    pair_w = 2 * LANES

    def body(ka_ref, va_ref, qa_ref, doa_ref, rr_ref, dka_ref, dva_ref, dqa_ref):
        j = pl.program_id(1)

        @pl.when(j == 0)
        def _():
            dqa_ref[...] = jnp.zeros_like(dqa_ref)

        causal = lax.broadcasted_iota(jnp.int32, (t, t), 0) <= lax.broadcasted_iota(jnp.int32, (t, t), 1)
        kas = (ka_ref[:, 0:LANES], ka_ref[:, LANES:pair_w])
        vas = (va_ref[:, 0:LANES], va_ref[:, LANES:pair_w])

        def step(i, carry, masked):
            rows = pl.ds(pl.multiple_of(i * t, t), t)
            new = []
            for hh in range(2):
                cols = slice(hh * LANES, (hh + 1) * LANES)
                dk_a, dv_a = carry[hh]
                qb = qa_ref[rows, cols]
                d_o = doa_ref[rows, cols]
                arg = _dot(kas[hh], qb, NT) - rr_ref[0, i, hh:hh + 1, :]
                if masked:
                    arg = jnp.where(causal, arg, -1e30)
                pt = jnp.exp(arg)
                dst = (pt * _dot(vas[hh], d_o, NT)).astype(BF16)
                dv_a = dv_a + _dot(pt.astype(BF16), d_o)
                dk_a = dk_a + _dot(dst, qb)
                dqa_ref[rows, cols] += _dot(dst, kas[hh], TN)
                new.append((dk_a, dv_a))
            return tuple(new)

        zero = jnp.zeros((t, LANES), F32)
        carry = step(j, ((zero, zero), (zero, zero)), masked=True)
        res = lax.fori_loop(j + 1, n, functools.partial(step, masked=False), carry)
        for hh in range(2):
            cols = slice(hh * LANES, (hh + 1) * LANES)
            dka_ref[:, cols] = res[hh][0]
            dva_ref[:, cols] = res[hh][1]

    tile_spec = pl.BlockSpec((t, pair_w), lambda p, j: (j, p))
    full_spec = pl.BlockSpec((s, pair_w), lambda p, j: (0, p))
    return pl.pallas_call(
        body, name="fox_bwd", grid=(FOX_HEADS // 2, n),
        out_shape=(jax.ShapeDtypeStruct((s, HEAD_BLOCKS), F32),) * 3,
        in_specs=[tile_spec, tile_spec, full_spec, full_spec,
                  pl.BlockSpec((1, n, 8, t), lambda p, j: (p, 0, 0, 0))],
        out_specs=(tile_spec, tile_spec, full_spec),
        compiler_params=_params(2),
    )(ka, va, qa, doa, rr)


def _fox_post(dqa, dka, dva, qk, fb, bf_pad, fq_g, fk_g):
    s = dqa.shape[0]
    t = TILE
    n = s // t

    def body(dqa_ref, dka_ref, dva_ref, qk_ref, fb_ref, bf_ref, qg_ref, kg_ref,
             dqk_ref, dv_ref, dfb_ref, dqg_ref, dkg_ref, dbf_ref, qacc_ref, kacc_ref, carry_ref):
        i = pl.program_id(0)

        @pl.when(i == 0)
        def _():
            qacc_ref[...] = jnp.zeros_like(qacc_ref)
            kacc_ref[...] = jnp.zeros_like(kacc_ref)
            dbf_ref[...] = jnp.zeros_like(dbf_ref)
            carry_ref[...] = jnp.zeros_like(carry_ref)

        lane = lax.broadcasted_iota(jnp.int32, (t, LANES), 1)
        row = lax.broadcasted_iota(jnp.int32, (t, LANES), 0)
        lo = lane < HEAD_DIM

        def head_blocks(ref, p):
            return ref[:, 2 * p * LANES:(2 * p + 1) * LANES], ref[:, (2 * p + 1) * LANES:(2 * p + 2) * LANES]

        dq_sum = jnp.zeros((t, LANES), F32)
        dk_sum = jnp.zeros((t, LANES), F32)
        for p in range(FOX_WIDTH // LANES):
            sl = slice(p * LANES, (p + 1) * LANES)
            dq0, dq1 = head_blocks(dqa_ref, p)
            dk0, dk1 = head_blocks(dka_ref, p)
            dv0, dv1 = head_blocks(dva_ref, p)
            dv_ref[:, sl] = _pair_block(dv0, dv1, lo).astype(BF16)
            dq_sum = dq_sum + (dq0 + dq1)
            dk_sum = dk_sum + (dk0 + dk1)
            for off, pair, g_ref, acc_ref, scale in ((0, _pair_block(dq0, dq1, lo), qg_ref, qacc_ref, ATT_SCALE),
                                                     (FOX_WIDTH, _pair_block(dk0, dk1, lo), kg_ref, kacc_ref, 1.0)):
                raw = qk_ref[:, off + p * LANES:off + (p + 1) * LANES]
                rr = _head_rms(raw, lo)
                xhat = raw * rr
                dn = pair * scale
                dqk_ref[:, off + p * LANES:off + (p + 1) * LANES] = _head_norm_bwd(
                    dn, xhat, rr, g_ref[:, sl], lo).astype(BF16)
                acc_ref[:, sl] += jnp.sum(dn * xhat, axis=0, keepdims=True)

        acc = (pltpu.roll(dq_sum, LANES - KEY_SUM_LANE, axis=1) - pltpu.roll(dk_sum, LANES - QUERY_SUM_LANE, axis=1))
        acc = jnp.where(lane < FOX_HEADS, acc, 0.0)
        sh = 1
        while sh < t:
            acc = acc + jnp.where(row < t - sh, pltpu.roll(acc, t - sh, axis=0), 0.0)
            sh *= 2
        dlogf = acc + carry_ref[...]
        dfb_ref[...] = dlogf
        carry_ref[...] = dfb_ref[0:1, :]
        z = fb_ref[...] + bf_ref[...]
        dz = jnp.where(lane < FOX_HEADS, dlogf * (1.0 / (1.0 + jnp.exp(z))), 0.0)
        dfb_ref[...] = dz
        dbf_ref[...] += jnp.sum(dz, axis=0, keepdims=True)

        @pl.when(i == n - 1)
        def _():
            dqg_ref[...] = _fold_heads(qacc_ref[...])
            dkg_ref[...] = _fold_heads(kacc_ref[...])

    return pl.pallas_call(
        body, name="fox_post", grid=(n,),
        out_shape=(jax.ShapeDtypeStruct((s, 2 * FOX_WIDTH), BF16), jax.ShapeDtypeStruct((s, FOX_WIDTH), BF16),
                   jax.ShapeDtypeStruct((s, LANES), F32), jax.ShapeDtypeStruct((1, LANES), F32),
                   jax.ShapeDtypeStruct((1, LANES), F32), jax.ShapeDtypeStruct((1, LANES), F32)),
        in_specs=[_rows_rev(t, HEAD_BLOCKS, n), _rows_rev(t, HEAD_BLOCKS, n), _rows_rev(t, HEAD_BLOCKS, n),
                  _rows_rev(t, 2 * FOX_WIDTH, n), _rows_rev(t, LANES, n), _full((1, LANES)),
                  _full((1, FOX_WIDTH)), _full((1, FOX_WIDTH))],
        out_specs=(_rows_rev(t, 2 * FOX_WIDTH, n), _rows_rev(t, FOX_WIDTH, n), _rows_rev(t, LANES, n),
                   _full((1, LANES)), _full((1, LANES)), _full((1, LANES))),
        scratch_shapes=[pltpu.VMEM((1, FOX_WIDTH), F32), pltpu.VMEM((1, FOX_WIDTH), F32), pltpu.VMEM((1, LANES), F32)],
        compiler_params=_params(),
    )(dqa, dka, dva, qk, fb, bf_pad, fq_g, fk_g)


def _assemble_dproj(dp_ref, dpa_ref, dqk_ref, dv_ref, dgb_ref, dpm_ref, dfb_ref):
    dp_ref[:, PA_LO:QB_LO] = dpa_ref[...]
    dp_ref[:, QB_LO:VB_LO] = dqk_ref[...]
    dp_ref[:, VB_LO:GB_LO] = dv_ref[...]
    dp_ref[:, GB_LO:PM_LO] = dgb_ref[...]
    dp_ref[:, PM_LO:FB_LO] = dpm_ref[...]
    dp_ref[:, FB_LO:PROJ_PAD] = dfb_ref[...].astype(BF16)


def _dproj_specs(t):
    return [_rows(t, 512), _rows(t, 2 * FOX_WIDTH), _rows(t, FOX_WIDTH), _rows(t, FOX_WIDTH), _rows(t, 512),
            _rows(t, LANES)]


def _in_bwd_x(x, dy, norm_g, wp, dparts):
    s = x.shape[0]
    t = TILE
    n = s // t

    def body(x_ref, dy_ref, g_ref, wp_ref, dpa_ref, dqk_ref, dv_ref, dgb_ref, dpm_ref, dfb_ref, gx_ref, dg_ref, dp_ref):
        @pl.when(pl.program_id(0) == 0)
        def _():
            dg_ref[...] = jnp.zeros_like(dg_ref)

        _assemble_dproj(dp_ref, dpa_ref, dqk_ref, dv_ref, dgb_ref, dpm_ref, dfb_ref)
        dh = _dot(dp_ref[...], wp_ref[...])
        xv = x_ref[...]
        rr = lax.rsqrt(jnp.mean(xv * xv, axis=-1, keepdims=True) + EPS)
        xhat = xv * rr
        a = dh * g_ref[...]
        gx_ref[...] = dy_ref[...] + rr * (a - xhat * jnp.mean(xhat * a, axis=-1, keepdims=True))
        dg_ref[...] += jnp.sum(dh * xhat, axis=0, keepdims=True)

    return pl.pallas_call(
        body, name="in_bwd_x", grid=(n,),
        out_shape=(jax.ShapeDtypeStruct((s, D_MODEL), F32), jax.ShapeDtypeStruct((1, D_MODEL), F32)),
        in_specs=[_rows(t, D_MODEL), _rows(t, D_MODEL), _full((1, D_MODEL)), _full((PROJ_PAD, D_MODEL))] + _dproj_specs(t),
        out_specs=(_rows(t, D_MODEL), _full((1, D_MODEL))),
        scratch_shapes=[pltpu.VMEM((t, PROJ_PAD), BF16)],
        compiler_params=_params(),
    )(x, dy, norm_g, wp, *dparts)


def _in_bwd_w(hb, dparts):
    s = hb.shape[0]
    t = TILE
    n = s // t
    f_hi = F_ORIG_LO + FOX_HEADS

    def body(h_ref, dpa_ref, dqk_ref, dv_ref, dgb_ref, dpm_ref, dfb_ref, dw_ref):
        @pl.when(pl.program_id(0) == 0)
        def _():
            dw_ref[...] = jnp.zeros_like(dw_ref)

        hv = h_ref[...]
        for lo, ref in ((0, dpa_ref), (QB_LO, dqk_ref), (VB_LO, dv_ref), (f_hi, dgb_ref), (f_hi + FOX_WIDTH, dpm_ref)):
            dw_ref[lo:lo + ref.shape[1], :] += _dot(ref[...], hv, TN)
        dw_ref[F_ORIG_LO:f_hi, :] += _dot(dfb_ref[...].astype(BF16), hv, TN)[0:FOX_HEADS, :]

    return pl.pallas_call(
        body, name="in_bwd_w", grid=(n,),
        out_shape=jax.ShapeDtypeStruct((IN_WIDTH, D_MODEL), F32),
        in_specs=[_rows(t, D_MODEL)] + _dproj_specs(t),
        out_specs=_full((IN_WIDTH, D_MODEL)),
        compiler_params=_params(),
    )(hb, *dparts)


def _adamw_math(w_ref, gv, m_ref, v_ref, d_ref, nm_ref, nv_ref):
    nm = ADAM_B1 * m_ref[...] + (1.0 - ADAM_B1) * gv
    nv = ADAM_B2 * v_ref[...] + (1.0 - ADAM_B2) * (gv * gv)
    m_hat = nm / (1.0 - ADAM_B1 ** ADAM_STEP)
    v_hat = nv / (1.0 - ADAM_B2 ** ADAM_STEP)
    d_ref[...] = -ADAM_LR * (m_hat / (jnp.sqrt(v_hat) + ADAM_EPS) + ADAM_WD * w_ref[...])
    nm_ref[...] = nm
    nv_ref[...] = nv


def _adamw_tiles(name, w, g, m, v):
    rows = w.shape[0]
    tr = next(d for d in range(128, 0, -1) if rows % d == 0)
    n = rows // tr

    def body(w_ref, g_ref, m_ref, v_ref, d_ref, nm_ref, nv_ref):
        _adamw_math(w_ref, g_ref[...], m_ref, v_ref, d_ref, nm_ref, nv_ref)

    spec = pl.BlockSpec((tr,) + w.shape[1:], lambda i: (i, 0, 0))
    return pl.pallas_call(
        body, name=name, grid=(n,),
        out_shape=(jax.ShapeDtypeStruct(w.shape, F32),) * 3,
        in_specs=[spec] * 4, out_specs=(spec,) * 3,
        compiler_params=_params(),
    )(w, g, m, v)


def _adamw(name, w, g, m, v):
    rows, cols = w.shape
    tc = 256 if rows * cols > 256 * 1024 else cols
    n = cols // tc

    def body(w_ref, g_ref, m_ref, v_ref, d_ref, nm_ref, nv_ref):
        _adamw_math(w_ref, g_ref[...], m_ref, v_ref, d_ref, nm_ref, nv_ref)

    spec = pl.BlockSpec((rows, tc), lambda i: (0, i))
    return pl.pallas_call(
        body, name=name, grid=(n,),
        out_shape=(jax.ShapeDtypeStruct((rows, cols), F32),) * 3,
        in_specs=[spec] * 4, out_specs=(spec,) * 3,
        compiler_params=_params(),
    )(w, g, m, v)


def _adamw_small(vec, dw4, leaves, pool):
    nl = len(VEC_LEAVES) + 1

    def body(*refs):
        vec_ref, dw4_ref = refs[0:2]
        wmv = refs[2:2 + 3 * nl]
        loss_ref = refs[2 + 3 * nl]
        outs = refs[3 + 3 * nl:]
        loss_ref[...] = vec_ref[VEC_LOSS_ROW:VEC_LOSS_ROW + 1, 0:1]
        for k in range(nl):
            if k < nl - 1:
                _, row, width = VEC_LEAVES[k]
                gv = vec_ref[row:row + 1, 0:width]
            else:
                gv = dw4_ref[...]
            w_ref, m_ref, v_ref = wmv[3 * k:3 * k + 3]
            g_ref, d_ref, nm_ref, nv_ref = outs[4 * k:4 * k + 4]
            g_ref[...] = gv
            _adamw_math(w_ref, gv, m_ref, v_ref, d_ref, nm_ref, nv_ref)

    shapes = [jax.ShapeDtypeStruct((1, width), F32) for _, _, width in VEC_LEAVES] + [
        jax.ShapeDtypeStruct(dw4.shape, F32)]
    flat_in = [a for triple in list(leaves) + [pool] for a in triple]
    res = pl.pallas_call(
        body, name="adamw_small",
        out_shape=(jax.ShapeDtypeStruct((1, 1), F32),) + tuple(s for s in shapes for _ in range(4)),
        compiler_params=pltpu.CompilerParams(vmem_limit_bytes=VMEM_LIMIT),
    )(vec, dw4, *flat_in)
    per = [res[1 + 4 * k:5 + 4 * k] for k in range(nl)]
    return res[0], [p[0] for p in per], [p[1] for p in per], [p[2] for p in per], [p[3] for p in per]


def _full_w_in_padded(halves):
    cols = IN_WIDTH // 4
    w_t = halves.reshape(4, 2, cols, D_MODEL // 2).transpose(0, 2, 1, 3).reshape(IN_WIDTH, D_MODEL)
    return jnp.concatenate([
        w_t[0:F_ORIG_LO], w_t[F_ORIG_LO + FOX_HEADS:], w_t[F_ORIG_LO:F_ORIG_LO + FOX_HEADS],
        jnp.zeros((PROJ_PAD - IN_WIDTH, D_MODEL), w_t.dtype)], axis=0)


def _tile_heads(g, n):
    return jnp.tile(g.reshape(1, HEAD_DIM), (1, n))


def kernel(x, mem, norm_g, w_in, b_f, w_pool, pool_scale, fox_q_g, fox_k_g, mem_norm_g, w_mem_kv, mem_q_g, mem_k_g, w_out, loss_target, m_norm_g, m_w_in, m_b_f, m_w_pool, m_pool_scale, m_fox_q_g, m_fox_k_g, m_mem_norm_g, m_w_mem_kv, m_mem_q_g, m_mem_k_g, m_w_out, v_norm_g, v_w_in, v_b_f, v_w_pool, v_pool_scale, v_fox_q_g, v_fox_k_g, v_mem_norm_g, v_w_mem_kv, v_mem_q_g, v_mem_k_g, v_w_out):
    w_in_t, m_w_in_t, v_w_in_t = w_in[0].T, m_w_in[0].T, v_w_in[0].T
    axes = (1, 0, 0)

    g_in, g_kv, g_out = _all_gather_weights([w_in_t, w_mem_kv[0], w_out[0]], axes)
    w_kv_b = g_kv.reshape(D_MODEL, 2 * MEM_WIDTH)
    w_out_b = g_out.reshape(D_MODEL, D_MODEL)
    w4 = w_pool.reshape(POOL_ROWS, HEAD_DIM)
    grad_x, dwp, dw_kv, dw_out, vec_leaves, loss_row, dw4 = _local_grads(
        x[0], mem[0], loss_target[0], g_in, w_kv_b, w_out_b, norm_g, b_f, w4, pool_scale, fox_q_g, fox_k_g,
        mem_norm_g, mem_q_g, mem_k_g)

    gparts = [dwp.reshape(4, IN_WIDTH // 4, D_MODEL), dw_kv.reshape(4, D_MODEL // 4, 2 * MEM_WIDTH),
              dw_out.reshape(4, D_MODEL // 4, D_MODEL)]
    g_w_in_t, g_w_kv, g_w_out, vec, dw4_sum = _grad_reduce(gparts, axes, vec_leaves, loss_row, dw4)

    small_wmv = [(norm_g, m_norm_g, v_norm_g), (mem_norm_g, m_mem_norm_g, v_mem_norm_g),
                 (pool_scale, m_pool_scale, v_pool_scale), (b_f, m_b_f, v_b_f), (fox_q_g, m_fox_q_g, v_fox_q_g),
                 (fox_k_g, m_fox_k_g, v_fox_k_g), (mem_q_g, m_mem_q_g, v_mem_q_g), (mem_k_g, m_mem_k_g, v_mem_k_g)]
    pool_wmv = tuple(a.reshape(POOL_ROWS, HEAD_DIM) for a in (w_pool, m_w_pool, v_w_pool))
    loss, *small_out = _adamw_small(vec, dw4_sum, small_wmv, pool_wmv)
    def tiles(a):
        return a.reshape(a.shape[0], 8, LANES)

    def untile(a):
        return a.reshape(a.shape[0], D_MODEL).T

    big = [[g_w_in_t.T[None], g_w_kv[None], g_w_out[None]]]
    def tiles_of_entry(a):
        return jnp.transpose(a, (2, 0, 1)).reshape(a.shape[2], 8, LANES)

    upd = [[untile(a) for a in _adamw_tiles("adamw_w_in", tiles_of_entry(w_in), tiles(g_w_in_t),
                                           tiles_of_entry(m_w_in), tiles_of_entry(v_w_in))],
           _adamw("adamw_w_mem_kv", w_mem_kv[0], g_w_kv, m_w_mem_kv[0], v_w_mem_kv[0]),
           _adamw("adamw_w_out", w_out[0], g_w_out, m_w_out[0], v_w_out[0])]
    big += [[u[k][None] for u in upd] for k in range(3)]

    def leaves(k):
        sm = small_out[k]
        b_in, b_kv, b_out = big[k]
        return (sm[0], b_in, sm[3], sm[8].reshape(w_pool.shape), sm[2], sm[4], sm[5], sm[1], b_kv, sm[6], sm[7], b_out)

    return (loss.reshape(()), grad_x[None], *leaves(0), *leaves(1), *leaves(2), *leaves(3))


def _local_grads(xs, mems, tgt, w_in_b, w_kv_b, w_out_b, norm_g, b_f, w4, pool_scale, fox_q_g, fox_k_g,
                 mem_norm_g, mem_q_g, mem_k_g):
    wp = _full_w_in_padded(w_in_b)
    bf_pad = jnp.pad(b_f, ((0, 0), (0, LANES - FOX_HEADS)))
    fq_g, fk_g = _tile_heads(fox_q_g, FOX_HEADS), _tile_heads(fox_k_g, FOX_HEADS)
    mq_g, mk_g = _tile_heads(mem_q_g, 4), _tile_heads(mem_k_g, 4)

    mnb, kv, kmn, vmb = _mem_fwd(mems, mem_norm_g, w_kv_b, mk_g)
    hb, pa, qk, qa, ka, va, gb, pm, fb = _fwd_in(xs, norm_g, wp, bf_pad, fq_g, fk_g)
    ma, db = _pool_fwd(pa, w4, pool_scale)
    mm = _mem_attn_fwd(pm, kmn, vmb, mq_g)
    o, mb, r4 = _fox_fwd(qa, ka, va, gb)
    dy, dma, dmb, dmm, dw_out, loss_row = _out_loss(xs, tgt, ma, mb, mm, w_out_b)

    dpm, dkmn, dvm, dmq_g = _mem_attn_bwd(pm, dmm, kmn, vmb, mq_g)
    dw_kv, dmemnorm_g, dmk_g = _mem_bwd(dkmn, dvm, kv, mnb, mems, w_kv_b, mk_g, mem_norm_g)
    dpa, dw4, dpscale = _pool_bwd(pa, db, dma, w4, pool_scale)
    doa, dgb, rr = _fox_prep(dmb, gb, o, r4)
    dka, dva, dqa = _fox_bwd(ka, va, qa, doa, rr)
    dqk, dvb, dfb, dfq_g, dfk_g, dbf = _fox_post(dqa, dka, dva, qk, fb, bf_pad, fq_g, fk_g)
    dparts = (dpa, dqk, dvb, dgb, dpm, dfb)
    grad_x, dnorm_g = _in_bwd_x(xs, dy, norm_g, wp, dparts)
    dwp = _in_bwd_w(hb, dparts)

    vec_leaves = (dnorm_g, dmemnorm_g, dpscale, dbf, dfq_g, dfk_g, dmq_g, dmk_g)
    return grad_x, dwp, dw_kv, dw_out, vec_leaves, loss_row, dw4
```

```python
import functools

import jax
import jax.numpy as jnp
from jax import lax
from jax.experimental import pallas as pl
from jax.experimental.pallas import tpu as pltpu

F32 = jnp.float32
BF16 = jnp.bfloat16
MESH = pl.DeviceIdType.MESH

D_MODEL = 1024
HEAD_DIM = 64
POOL_WIDTH = 256
FOX_WIDTH = 512
FOX_HEADS = 8
MEM_WIDTH = 256
N_MEM = 256
IN_WIDTH = 3080
EPS = 1e-6
ATT_SCALE = 0.125

ADAM_LR = 0.001
ADAM_B1 = 0.9
ADAM_B2 = 0.999
ADAM_EPS = 1e-08
ADAM_WD = 0.01
ADAM_STEP = 10

LANES = 128
PA_LO, QB_LO, KB_LO, VB_LO, GB_LO, PM_LO, FB_LO, PROJ_PAD = 0, 512, 1024, 1536, 2048, 2560, 3072, 3200
F_ORIG_LO = 2048

TILE = 512
VMEM_LIMIT = 56 * 1024 * 1024

VEC_LEAVES = (("norm_g", 0, 1024), ("mem_norm_g", 1, 1024), ("pool_scale", 2, 256), ("b_f", 3, 8),
              ("fox_q_g", 4, 64), ("fox_k_g", 5, 64), ("mem_q_g", 6, 64), ("mem_k_g", 7, 64))
VEC_LOSS_ROW = 8
VEC_ROWS = 16
POOL_ROWS = 256


def _params(n_grid=1, vmem=VMEM_LIMIT):
    return pltpu.CompilerParams(dimension_semantics=("arbitrary",) * n_grid, vmem_limit_bytes=vmem)


def _rows(t, w):
    return pl.BlockSpec((t, w), lambda i: (i, 0))


def _rows_rev(t, w, n):
    return pl.BlockSpec((t, w), lambda i: (n - 1 - i, 0))


def _full(shape):
    return pl.BlockSpec(shape, lambda i: (0,) * len(shape))


def _sig(x):
    return 1.0 / (1.0 + jnp.exp(-x))


def _lane_lo(shape):
    return lax.broadcasted_iota(jnp.int32, shape, 1) < HEAD_DIM


def _pair_sum(v, lo):
    s0 = jnp.sum(jnp.where(lo, v, 0.0), axis=-1, keepdims=True)
    s1 = jnp.sum(jnp.where(lo, 0.0, v), axis=-1, keepdims=True)
    return jnp.where(lo, s0, s1)


def _head_rms(blk, lo):
    return lax.rsqrt(_pair_sum(blk * blk, lo) * (1.0 / HEAD_DIM) + EPS)


def _head_norm_bwd(dyn, xhat, rr, g, lo):
    a = dyn * g
    return rr * (a - xhat * (_pair_sum(xhat * a, lo) * (1.0 / HEAD_DIM)))


def _fold_heads(acc):
    tot = acc[:, 0:LANES]
    for p in range(1, acc.shape[1] // LANES):
        tot = tot + acc[:, p * LANES:(p + 1) * LANES]
    return tot + pltpu.roll(tot, HEAD_DIM, axis=1)


def _lane_pick(v, lane, idx):
    return jnp.sum(jnp.where(lane == idx, v, 0.0), axis=-1, keepdims=True)


NT = (((1,), (1,)), ((), ()))
TN = (((0,), (0,)), ((), ()))


def _dot(a, b, dims=None):
    if dims is None:
        return jnp.dot(a, b, preferred_element_type=F32)
    return lax.dot_general(a, b, dims, preferred_element_type=F32)


def _my_place():
    return lax.axis_index("x"), lax.axis_index("y"), lax.axis_index("c")


def _half_dims(shape, axis):
    return (shape[0] // 2, shape[1]) if axis == 0 else (shape[0], shape[1] // 2)


def _half_of(ref, axis, core, lead=False):
    rows, cols = ref.shape[-2:]
    if axis == 0:
        idx = (pl.ds(pl.multiple_of(core * (rows // 2), 16), rows // 2), slice(None))
    else:
        idx = (slice(None), pl.ds(pl.multiple_of(core * (cols // 2), LANES), cols // 2))
    return ref.at[(slice(None),) + idx] if lead else ref.at[idx]


def _all_gather_weights(shards, axes):
    n = len(shards)
    dims = [_half_dims(a.shape, axis) for a, axis in zip(shards, axes)]

    def body(*refs):
        ins, outs = refs[0:n], refs[n:2 * n]
        f32_bufs, bf_bufs = refs[2 * n:3 * n], refs[3 * n:4 * n]
        send_sems, recv_sems, local_sems = refs[4 * n:]
        x, y, c = _my_place()
        me, sibling = (x, y, c), (x, y, 1 - c)
        chips = [(1 - x, y), (x, 1 - y), (1 - x, 1 - y)]

        loads = []
        for a in range(n):
            cp = pltpu.make_async_copy(_half_of(ins[a], axes[a], c), f32_bufs[a], local_sems.at[a])
            cp.start()
            loads.append(cp)

        def blk(a, px, py, pc):
            return outs[a].at[4 * px + 2 * py + pc]

        def copy(a, k, block, to, src=None):
            return pltpu.make_async_remote_copy(
                src_ref=blk(a, *block) if src is None else src, dst_ref=blk(a, *block),
                send_sem=send_sems.at[7 * a + k], recv_sem=recv_sems.at[7 * a + k], device_id=to, device_id_type=MESH)

        first, keeps = [], []
        for a in range(n):
            loads[a].wait()
            bf_bufs[a][...] = f32_bufs[a][...].astype(BF16)
            keep = pltpu.make_async_copy(bf_bufs[a], blk(a, *me), local_sems.at[n + a])
            keep.start()
            keeps.append(keep)
            mine = [copy(a, 0, me, sibling, src=bf_bufs[a])]
            mine += [copy(a, 1 + j, me, (*chip, c), src=bf_bufs[a]) for j, chip in enumerate(chips)]
            for cp in mine:
                cp.start()
            first += mine
        passed = []
        for a in range(n):
            for j, chip in enumerate(chips):
                copy(a, 1 + j, (*chip, c), me).wait_recv()
                cp = copy(a, 4 + j, (*chip, c), sibling)
                cp.start()
                passed.append(cp)
        for a in range(n):
            copy(a, 0, sibling, me).wait_recv()
            for j, chip in enumerate(chips):
                copy(a, 4 + j, (*chip, 1 - c), me).wait_recv()
        for cp in first + passed:
            cp.wait_send()
        for keep in keeps:
            keep.wait()

    any_spec = pl.BlockSpec(memory_space=pl.ANY)
    return pl.pallas_call(
        body, name="weights_all_gather",
        out_shape=tuple(jax.ShapeDtypeStruct((8, h, w), BF16) for h, w in dims),
        in_specs=[any_spec] * n, out_specs=(any_spec,) * n,
        scratch_shapes=[pltpu.VMEM(d, F32) for d in dims] + [pltpu.VMEM(d, BF16) for d in dims] + [
            pltpu.SemaphoreType.DMA((7 * n,)), pltpu.SemaphoreType.DMA((7 * n,)), pltpu.SemaphoreType.DMA((2 * n,))],
        compiler_params=pltpu.CompilerParams(vmem_limit_bytes=VMEM_LIMIT),
    )(*shards)


def _grad_reduce(gparts, axes, vec_leaves, loss_row, dw4):
    n = len(gparts)
    dims = [_half_dims(g.shape[1:], axis) for g, axis in zip(gparts, axes)]
    nv = len(vec_leaves)

    def body(*refs):
        g_refs = refs[0:n]
        leaf_refs = refs[n:n + nv]
        loss_ref, dw4_ref = refs[n + nv:n + nv + 2]
        o = n + nv + 2
        out_refs = refs[o:o + n]
        vec_out, dw4_out = refs[o + n:o + n + 2]
        s0 = o + n + 2
        recv_a, own_a = refs[s0:s0 + n], refs[s0 + n:s0 + 2 * n]
        send_b, recv_b = refs[s0 + 2 * n:s0 + 3 * n], refs[s0 + 3 * n:s0 + 4 * n]
        fin = refs[s0 + 4 * n:s0 + 5 * n]
        vec_mine, vec_recv, dw4_recv, send_sems, recv_sems, local_sems = refs[s0 + 5 * n:]

        x, y, c = _my_place()
        chip = 2 * x + y
        me_lin = 4 * x + 2 * y + c
        sibling = (x, y, 1 - c)

        to_sib, own = [], []
        for a in range(n):
            cp = pltpu.make_async_remote_copy(
                src_ref=_half_of(g_refs[a], axes[a], 1 - c, lead=True), dst_ref=recv_a[a], send_sem=send_sems.at[5 * a],
                recv_sem=recv_sems.at[5 * a], device_id=sibling, device_id_type=MESH)
            cp.start()
            to_sib.append(cp)
            cp = pltpu.make_async_copy(_half_of(g_refs[a], axes[a], c, lead=True), own_a[a], local_sems.at[a])
            cp.start()
            own.append(cp)

        vec_mine[...] = jnp.zeros_like(vec_mine)
        for (_, row, width), ref in zip(VEC_LEAVES, leaf_refs):
            vec_mine[row:row + 1, 0:ref.shape[1]] = ref[...]
        vec_mine[VEC_LOSS_ROW:VEC_LOSS_ROW + 1, 0:LANES] = loss_ref[...]
        small_copies = []
        for k in range(1, 8):
            peer = (me_lin + k) % 8
            to = (peer // 4, (peer // 2) % 2, peer % 2)
            for src, dst, base in ((vec_mine, vec_recv, 5 * n), (dw4_ref, dw4_recv, 5 * n + 7)):
                cp = pltpu.make_async_remote_copy(
                    src_ref=src, dst_ref=dst.at[me_lin], send_sem=send_sems.at[base + k - 1],
                    recv_sem=recv_sems.at[base + k - 1], device_id=to, device_id_type=MESH)
                cp.start()
                small_copies.append(cp)

        chip_copies = []
        for a in range(n):
            own[a].wait()
            to_sib[a].wait_recv()
            for j in range(4):
                send_b[a][j] = (own_a[a][j] + recv_a[a][j]).astype(BF16)
            for k in range(1, 4):
                dest = (chip + k) % 4
                cp = pltpu.make_async_remote_copy(
                    src_ref=send_b[a].at[dest], dst_ref=recv_b[a].at[chip], send_sem=send_sems.at[5 * a + k],
                    recv_sem=recv_sems.at[5 * a + k], device_id=(dest // 2, dest % 2, c), device_id_type=MESH)
                cp.start()
                chip_copies.append(cp)
            keep = pltpu.make_async_copy(send_b[a].at[chip], recv_b[a].at[chip], local_sems.at[n + a])
            keep.start()
            keep.wait()

        give, mine = [], []
        for a in range(n):
            for cp in chip_copies[3 * a:3 * a + 3]:
                cp.wait_recv()
            tot = recv_b[a][0].astype(F32) + recv_b[a][1].astype(F32)
            tot = tot + recv_b[a][2].astype(F32)
            fin[a][...] = tot + recv_b[a][3].astype(F32)
            cp = pltpu.make_async_remote_copy(
                src_ref=fin[a], dst_ref=_half_of(out_refs[a], axes[a], c), send_sem=send_sems.at[5 * a + 4],
                recv_sem=recv_sems.at[5 * a + 4], device_id=sibling, device_id_type=MESH)
            cp.start()
            give.append(cp)
            cp = pltpu.make_async_copy(fin[a], _half_of(out_refs[a], axes[a], c), local_sems.at[a])
            cp.start()
            mine.append(cp)

        for cp in small_copies:
            cp.wait_recv()
        vec_recv[me_lin] = vec_mine[...]
        dw4_recv[me_lin] = dw4_ref[...]
        vtot, wtot = vec_recv[0], dw4_recv[0]
        for d in range(1, 8):
            vtot = vtot + vec_recv[d]
            wtot = wtot + dw4_recv[d]
        vec_out[...] = vtot
        dw4_out[...] = wtot

        for a in range(n):
            give[a].wait_recv()
            mine[a].wait()
            to_sib[a].wait_send()
            give[a].wait_send()
        for cp in chip_copies + small_copies:
            cp.wait_send()

    any_spec = pl.BlockSpec(memory_space=pl.ANY)
    vmem_spec = pl.BlockSpec(memory_space=pltpu.VMEM)
    n_sems = 5 * n + 14
    scratch = []
    for dtype, lead in ((F32, (4,)), (F32, (4,)), (BF16, (4,)), (BF16, (4,)), (F32, ())):
        scratch += [pltpu.VMEM(lead + d, dtype) for d in dims]
    scratch += [pltpu.VMEM((VEC_ROWS, D_MODEL), F32), pltpu.VMEM((8, VEC_ROWS, D_MODEL), F32),
                pltpu.VMEM((8,) + dw4.shape, F32),
                pltpu.SemaphoreType.DMA((n_sems,)), pltpu.SemaphoreType.DMA((n_sems,)), pltpu.SemaphoreType.DMA((2 * n,))]
    return pl.pallas_call(
        body, name="grad_reduce",
        out_shape=tuple(jax.ShapeDtypeStruct(g.shape[1:], F32) for g in gparts) + (
            jax.ShapeDtypeStruct((VEC_ROWS, D_MODEL), F32), jax.ShapeDtypeStruct(dw4.shape, F32)),
        in_specs=[any_spec] * n + [vmem_spec] * (nv + 2),
        out_specs=(any_spec,) * n + (vmem_spec, vmem_spec),
        scratch_shapes=scratch,
        compiler_params=pltpu.CompilerParams(vmem_limit_bytes=VMEM_LIMIT),
    )(*gparts, *vec_leaves, loss_row, dw4)


def _mem_fwd(mem, mem_norm_g, w_kv, mk_g):
    n = mem.shape[0]

    def body(mem_ref, g_ref, w_ref, kg_ref, mn_ref, kv_ref, kn_ref, vm_ref):
        xm = mem_ref[...]
        rr = lax.rsqrt(jnp.mean(xm * xm, axis=-1, keepdims=True) + EPS)
        mnb = ((xm * rr) * g_ref[...]).astype(BF16)
        mn_ref[...] = mnb
        kv = _dot(mnb, w_ref[...])
        kv_ref[...] = kv
        lo = _lane_lo((n, LANES))
        for p in range(MEM_WIDTH // LANES):
            sl = slice(p * LANES, (p + 1) * LANES)
            kb = kv[:, sl]
            kn_ref[:, sl] = ((kb * _head_rms(kb, lo)) * kg_ref[:, sl]).astype(BF16)
        vm_ref[...] = kv[:, MEM_WIDTH:].astype(BF16)

    return pl.pallas_call(
        body, name="mem_fwd",
        out_shape=(jax.ShapeDtypeStruct((n, D_MODEL), BF16), jax.ShapeDtypeStruct((n, 2 * MEM_WIDTH), F32),
                   jax.ShapeDtypeStruct((n, MEM_WIDTH), BF16), jax.ShapeDtypeStruct((n, MEM_WIDTH), BF16)),
        compiler_params=pltpu.CompilerParams(vmem_limit_bytes=VMEM_LIMIT),
    )(mem, mem_norm_g, w_kv, mk_g)


AUG_LO = 64
KEY_SUM_LANE = 72
QUERY_SUM_LANE = 80
HEAD_BLOCKS = FOX_HEADS * LANES


def _ones3(lane):
    return jnp.where((lane >= AUG_LO) & (lane < AUG_LO + 3), 1.0, 0.0)


def _spread3(cols):
    hi = cols.astype(BF16)
    rest = cols - hi.astype(F32)
    mid = rest.astype(BF16)
    low = (rest - mid.astype(F32)).astype(BF16)
    r = lax.broadcasted_iota(jnp.int32, (LANES, HEAD_BLOCKS), 0)
    c = lax.broadcasted_iota(jnp.int32, (LANES, HEAD_BLOCKS), 1)
    out = None
    for k, part in enumerate((hi, mid, low)):
        term = _dot(part, jnp.where(c == r * LANES + (AUG_LO + k), 1.0, 0.0).astype(BF16))
        out = term if out is None else out + term
    return out


def _head_block(pair_blk, hh, lo, extras):
    src = pair_blk if hh == 0 else pltpu.roll(pair_blk, HEAD_DIM, axis=1)
    return jnp.where(lo, src, extras).astype(BF16)


def _pair_block(blk0, blk1, lo):
    return jnp.where(lo, blk0, pltpu.roll(blk1, HEAD_DIM, axis=1))


def _fwd_in(x, norm_g, wp, bf_pad, fq_g, fk_g):
    s = x.shape[0]
    t = TILE
    n = s // t

    def body(x_ref, ng_ref, wp_ref, bf_ref, qg_ref, kg_ref,
             h_ref, pa_ref, qk_ref, qa_ref, ka_ref, va_ref, gb_ref, pm_ref, fb_ref, carry_ref, fcol_ref):
        @pl.when(pl.program_id(0) == 0)
        def _():
            carry_ref[...] = jnp.zeros_like(carry_ref)

        xv = x_ref[...]
        rr = lax.rsqrt(jnp.mean(xv * xv, axis=-1, keepdims=True) + EPS)
        hb = ((xv * rr) * ng_ref[...]).astype(BF16)
        h_ref[...] = hb

        def proj(lo, hi):
            return _dot(hb, wp_ref[lo:hi, :], NT)

        pa_ref[...] = proj(PA_LO, QB_LO)
        gb_ref[...] = proj(GB_LO, PM_LO)
        pm_ref[...] = proj(PM_LO, FB_LO)
        fb = proj(FB_LO, PROJ_PAD)
        fb_ref[...] = fb

        lane = lax.broadcasted_iota(jnp.int32, (t, LANES), 1)
        row = lax.broadcasted_iota(jnp.int32, (t, LANES), 0)
        lo = lane < HEAD_DIM
        z = fb + bf_ref[...]
        lf = -(jnp.maximum(-z, 0.0) + jnp.log1p(jnp.exp(-jnp.abs(z))))
        lf = jnp.where(lane < FOX_HEADS, lf, 0.0)
        sh = 1
        while sh < t:
            lf = lf + jnp.where(row >= sh, pltpu.roll(lf, sh, axis=0), 0.0)
            sh *= 2
        fcum = lf + carry_ref[...]
        fcol_ref[...] = fcum
        carry_ref[...] = fcol_ref[t - 1:t, :]

        ones3 = _ones3(lane)
        minus_f = _spread3(-fcum)
        for seg, g_ref, out_ref, scale in ((QB_LO, qg_ref, qa_ref, ATT_SCALE), (KB_LO, kg_ref, ka_ref, 1.0)):
            raw = proj(seg, seg + FOX_WIDTH)
            qk_ref[:, seg - QB_LO:seg - QB_LO + FOX_WIDTH] = raw
            for p in range(FOX_WIDTH // LANES):
                sl = slice(p * LANES, (p + 1) * LANES)
                blk = raw[:, sl]
                normed = ((blk * _head_rms(blk, lo)) * g_ref[:, sl]) * scale
                for hh in range(2):
                    h = 2 * p + hh
                    if seg == QB_LO:
                        extras = jnp.where(lane == QUERY_SUM_LANE + h, 1.0, ones3)
                    else:
                        extras = jnp.where(lane == KEY_SUM_LANE + h, 1.0, minus_f[:, h * LANES:(h + 1) * LANES])
                    out_ref[:, h * LANES:(h + 1) * LANES] = _head_block(normed, hh, lo, extras)
        vraw = proj(VB_LO, GB_LO)
        for h in range(FOX_HEADS):
            va_ref[:, h * LANES:(h + 1) * LANES] = _head_block(vraw[:, (h // 2) * LANES:(h // 2 + 1) * LANES], h % 2, lo, ones3)

    outs = (
        jax.ShapeDtypeStruct((s, D_MODEL), BF16),
        jax.ShapeDtypeStruct((s, 512), F32),
        jax.ShapeDtypeStruct((s, 2 * FOX_WIDTH), F32),
        jax.ShapeDtypeStruct((s, HEAD_BLOCKS), BF16),
        jax.ShapeDtypeStruct((s, HEAD_BLOCKS), BF16),
        jax.ShapeDtypeStruct((s, HEAD_BLOCKS), BF16),
        jax.ShapeDtypeStruct((s, FOX_WIDTH), F32),
        jax.ShapeDtypeStruct((s, 512), F32),
        jax.ShapeDtypeStruct((s, LANES), F32),
    )
    return pl.pallas_call(
        body, name="fwd_in", grid=(n,), out_shape=outs,
        in_specs=[_rows(t, D_MODEL), _full((1, D_MODEL)), _full((PROJ_PAD, D_MODEL)), _full((1, LANES)),
                  _full((1, FOX_WIDTH)), _full((1, FOX_WIDTH))],
        out_specs=(_rows(t, D_MODEL), _rows(t, 512), _rows(t, 2 * FOX_WIDTH), _rows(t, HEAD_BLOCKS),
                   _rows(t, HEAD_BLOCKS), _rows(t, HEAD_BLOCKS), _rows(t, FOX_WIDTH), _rows(t, 512),
                   _rows(t, LANES)),
        scratch_shapes=[pltpu.VMEM((1, LANES), F32), pltpu.VMEM((t, LANES), F32)],
        compiler_params=_params(),
    )(x, norm_g, wp, bf_pad, fq_g, fk_g)


POOL_HALO = 16


def _pool_window(lane):
    return jnp.where(lane < 64, 2.0, jnp.where(lane < 128, 4.0, jnp.where(lane < 192, 8.0, 16.0)))


def _pool_pick(lane, s2, s4, s8, s16):
    return jnp.where(lane < 64, s2, jnp.where(lane < 128, s4, jnp.where(lane < 192, s8, s16)))


def _group_onehot(shape, row_is_group_lane):
    r = lax.broadcasted_iota(jnp.int32, shape, 0)
    c = lax.broadcasted_iota(jnp.int32, shape, 1)
    hit = (r % HEAD_DIM == c) if row_is_group_lane else (c % HEAD_DIM == r)
    return jnp.where(hit, 1.0, 0.0).astype(F32)


def _same_group(shape):
    r = lax.broadcasted_iota(jnp.int32, shape, 0)
    c = lax.broadcasted_iota(jnp.int32, shape, 1)
    return (r // HEAD_DIM) == (c // HEAD_DIM)


def _pool_block_diag(w4):
    spread = jnp.dot(w4, _group_onehot((HEAD_DIM, POOL_WIDTH), False), preferred_element_type=F32,
                     precision=lax.Precision.HIGHEST)
    return jnp.where(_same_group((POOL_WIDTH, POOL_WIDTH)), spread, 0.0).astype(BF16)


def _pool_fwd(pa, w4, pscale):
    s = pa.shape[0]
    t = TILE
    n = s // t
    ext = t + POOL_HALO

    def body(pa_ref, w4_ref, sc_ref, ma_ref, d_ref, ext_ref, w_ref):
        i = pl.program_id(0)

        @pl.when(i == 0)
        def _():
            ext_ref[0:POOL_HALO, :] = jnp.zeros((POOL_HALO, POOL_WIDTH), F32)
            w_ref[...] = _pool_block_diag(w4_ref[...])

        u = pa_ref[:, 0:POOL_WIDTH]
        ext_ref[POOL_HALO:ext, :] = u
        e = ext_ref[...]
        s2 = e + pltpu.roll(e, 1, axis=0)
        s4 = s2 + pltpu.roll(s2, 2, axis=0)
        s8 = s4 + pltpu.roll(s4, 4, axis=0)
        s16 = s8 + pltpu.roll(s8, 8, axis=0)
        lane_e = lax.broadcasted_iota(jnp.int32, (ext, POOL_WIDTH), 1)
        win = _pool_pick(lane_e, s2, s4, s8, s16)[POOL_HALO:ext, :]
        lane = lax.broadcasted_iota(jnp.int32, (t, POOL_WIDTH), 1)
        pos = (lax.broadcasted_iota(jnp.int32, (t, POOL_WIDTH), 0) + (i * t + 1)).astype(F32)
        d = win / jnp.minimum(pos, _pool_window(lane)) - u
        db = d.astype(BF16)
        d_ref[...] = db
        ya = _dot(db, w_ref[...]) * sc_ref[...]
        ga = pa_ref[:, POOL_WIDTH:2 * POOL_WIDTH]
        ma_ref[...] = (ya * (ga * _sig(ga))).astype(BF16)
        ext_ref[0:POOL_HALO, :] = ext_ref[t:ext, :]

    return pl.pallas_call(
        body, name="pool_fwd", grid=(n,),
        out_shape=(jax.ShapeDtypeStruct((s, POOL_WIDTH), BF16), jax.ShapeDtypeStruct((s, POOL_WIDTH), BF16)),
        in_specs=[_rows(t, 512), _full((POOL_ROWS, HEAD_DIM)), _full((1, POOL_WIDTH))],
        out_specs=(_rows(t, POOL_WIDTH), _rows(t, POOL_WIDTH)),
        scratch_shapes=[pltpu.VMEM((ext, POOL_WIDTH), F32), pltpu.VMEM((POOL_WIDTH, POOL_WIDTH), BF16)],
        compiler_params=_params(),
    )(pa, w4, pscale)


def _mem_softmax(qm, kp):
    sc = _dot(qm, kp, NT)
    e = jnp.exp(sc - jnp.max(sc, axis=-1, keepdims=True))
    return e * (1.0 / jnp.sum(e, axis=-1, keepdims=True))


def _mem_attn_fwd(pm, kmn, vmb, mq_g):
    s = pm.shape[0]
    t = TILE
    n = s // t

    def body(pm_ref, k_ref, v_ref, g_ref, mm_ref):
        lo = _lane_lo((t, LANES))
        for p in range(MEM_WIDTH // LANES):
            sl = slice(p * LANES, (p + 1) * LANES)
            qb = pm_ref[:, sl]
            qs = (((qb * _head_rms(qb, lo)) * g_ref[:, sl]) * ATT_SCALE).astype(BF16)
            kp = k_ref[:, sl]
            vp = v_ref[:, sl]
            outs = []
            for hh in range(2):
                msk = lo if hh == 0 else jnp.logical_not(lo)
                prob = _mem_softmax(jnp.where(msk, qs, jnp.zeros_like(qs)), kp)
                outs.append(_dot(prob.astype(BF16), vp))
            o = jnp.where(lo, outs[0], outs[1])
            gm = pm_ref[:, MEM_WIDTH + p * LANES:MEM_WIDTH + (p + 1) * LANES]
            mm_ref[:, sl] = (o * (gm * _sig(gm))).astype(BF16)

    return pl.pallas_call(
        body, name="mem_attn_fwd", grid=(n,),
        out_shape=jax.ShapeDtypeStruct((s, MEM_WIDTH), BF16),
        in_specs=[_rows(t, 512), _full((N_MEM, MEM_WIDTH)), _full((N_MEM, MEM_WIDTH)), _full((1, MEM_WIDTH))],
        out_specs=_rows(t, MEM_WIDTH),
        compiler_params=_params(),
    )(pm, kmn, vmb, mq_g)


def _fox_fwd(qa, ka, va, gb):
    s = qa.shape[0]
    t = TILE
    n = s // t
    pair_w = 2 * LANES

    def body(qa_ref, ka_ref, va_ref, gb_ref, o_ref, mb_ref, r_ref):
        i = pl.program_id(1)
        lane = lax.broadcasted_iota(jnp.int32, (t, LANES), 1)
        lo = lane < HEAD_DIM
        causal = lax.broadcasted_iota(jnp.int32, (t, t), 1) <= lax.broadcasted_iota(jnp.int32, (t, t), 0)
        qas = (qa_ref[:, 0:LANES], qa_ref[:, LANES:pair_w])

        def step(j, carry, masked):
            rows = pl.ds(pl.multiple_of(j * t, t), t)
            new = []
            for hh in range(2):
                cols = slice(hh * LANES, (hh + 1) * LANES)
                m, acc = carry[hh]
                sc = _dot(qas[hh], ka_ref[rows, cols], NT)
                if masked:
                    sc = jnp.where(causal, sc, -1e30)
                m_new = jnp.maximum(m, jnp.max(sc, axis=-1, keepdims=True))
                acc = jnp.exp(m - m_new) * acc + _dot(jnp.exp(sc - m_new).astype(BF16), va_ref[rows, cols])
                new.append((m_new, acc))
            return tuple(new)

        init = (jnp.full((t, 1), -1e30, F32), jnp.zeros((t, LANES), F32))
        carry = lax.fori_loop(0, i, functools.partial(step, masked=False), (init, init))
        outs = []
        rcol = jnp.zeros((t, LANES), F32)
        for hh, (m, acc) in enumerate(step(i, carry, masked=True)):
            l = _lane_pick(acc, lane, AUG_LO)
            outs.append(acc * (1.0 / l))
            rcol = jnp.where(lane == hh, m + jnp.log(l), rcol)
        o = _pair_block(outs[0], outs[1], lo)
        o_ref[...] = o
        g = gb_ref[...]
        mb_ref[...] = (o * (g * _sig(g))).astype(BF16)
        r_ref[0] = rcol

    return pl.pallas_call(
        body, name="fox_fwd", grid=(FOX_HEADS // 2, n),
        out_shape=(jax.ShapeDtypeStruct((s, FOX_WIDTH), F32), jax.ShapeDtypeStruct((s, FOX_WIDTH), BF16),
                   jax.ShapeDtypeStruct((FOX_HEADS // 2, s, LANES), F32)),
        in_specs=[pl.BlockSpec((t, pair_w), lambda p, i: (i, p)), pl.BlockSpec((s, pair_w), lambda p, i: (0, p)),
                  pl.BlockSpec((s, pair_w), lambda p, i: (0, p)), pl.BlockSpec((t, LANES), lambda p, i: (i, p))],
        out_specs=(pl.BlockSpec((t, LANES), lambda p, i: (i, p)), pl.BlockSpec((t, LANES), lambda p, i: (i, p)),
                   pl.BlockSpec((1, t, LANES), lambda p, i: (p, i, 0))),
        compiler_params=_params(2),
    )(qa, ka, va, gb)


def _out_loss(x, tgt, ma, mb, mm, wout):
    s = x.shape[0]
    t = TILE
    n = s // t

    def body(x_ref, t_ref, ma_ref, mb_ref, mm_ref, w_ref, dy_ref, dma_ref, dmb_ref, dmm_ref, dw_ref, loss_ref, mix_ref):
        @pl.when(pl.program_id(0) == 0)
        def _():
            dw_ref[...] = jnp.zeros_like(dw_ref)
            loss_ref[...] = jnp.zeros_like(loss_ref)

        mix_ref[:, 0:256] = ma_ref[...]
        mix_ref[:, 256:768] = mb_ref[...]
        mix_ref[:, 768:1024] = mm_ref[...]
        mix = mix_ref[...]
        err = (x_ref[...] + _dot(mix, w_ref[...])) - t_ref[...]
        row_mean = jnp.sum(err * err, axis=-1, keepdims=True) * (1.0 / D_MODEL)
        loss_ref[...] += 0.5 * jnp.sum(row_mean, axis=0, keepdims=True)
        dy = err * (1.0 / D_MODEL)
        dy_ref[...] = dy
        dyb = dy.astype(BF16)
        dmix = _dot(dyb, w_ref[...], NT)
        dma_ref[...] = dmix[:, 0:256]
        dmb_ref[...] = dmix[:, 256:768]
        dmm_ref[...] = dmix[:, 768:1024]
        dw_ref[...] += _dot(mix, dyb, TN)

    return pl.pallas_call(
        body, name="out_loss", grid=(n,),
        out_shape=(jax.ShapeDtypeStruct((s, D_MODEL), F32), jax.ShapeDtypeStruct((s, 256), F32),
                   jax.ShapeDtypeStruct((s, 512), F32), jax.ShapeDtypeStruct((s, 256), F32),
                   jax.ShapeDtypeStruct((D_MODEL, D_MODEL), F32), jax.ShapeDtypeStruct((1, LANES), F32)),
        in_specs=[_rows(t, D_MODEL), _rows(t, D_MODEL), _rows(t, 256), _rows(t, 512), _rows(t, 256),
                  _full((D_MODEL, D_MODEL))],
        out_specs=(_rows(t, D_MODEL), _rows(t, 256), _rows(t, 512), _rows(t, 256), _full((D_MODEL, D_MODEL)),
                   _full((1, LANES))),
        scratch_shapes=[pltpu.VMEM((t, D_MODEL), BF16)],
        compiler_params=_params(),
    )(x, tgt, ma, mb, mm, wout)


def _mem_attn_bwd(pm, dmm, kmn, vmb, mq_g):
    s = pm.shape[0]
    t = TILE
    n = s // t

    def body(pm_ref, dmm_ref, k_ref, v_ref, g_ref, dpm_ref, dk_ref, dv_ref, dg_ref, gacc_ref):
        @pl.when(pl.program_id(0) == 0)
        def _():
            dk_ref[...] = jnp.zeros_like(dk_ref)
            dv_ref[...] = jnp.zeros_like(dv_ref)
            gacc_ref[...] = jnp.zeros_like(gacc_ref)

        lo = _lane_lo((t, LANES))
        for p in range(MEM_WIDTH // LANES):
            sl = slice(p * LANES, (p + 1) * LANES)
            qb = pm_ref[:, sl]
            rr = _head_rms(qb, lo)
            qhat = qb * rr
            g = g_ref[:, sl]
            qs = ((qhat * g) * ATT_SCALE).astype(BF16)
            gm = pm_ref[:, MEM_WIDTH + p * LANES:MEM_WIDTH + (p + 1) * LANES]
            sg = _sig(gm)
            dmo = dmm_ref[:, sl]
            d_o = dmo * (gm * sg)
            kp = k_ref[:, sl]
            vp = v_ref[:, sl]
            outs, dqs = [], []
            for hh in range(2):
                msk = lo if hh == 0 else jnp.logical_not(lo)
                qm = jnp.where(msk, qs, jnp.zeros_like(qs))
                prob = _mem_softmax(qm, kp)
                pb = prob.astype(BF16)
                outs.append(_dot(pb, vp))
                dom = jnp.where(msk, d_o, 0.0).astype(BF16)
                dp = _dot(dom, vp, NT)
                ds = (prob * (dp - jnp.sum(prob * dp, axis=-1, keepdims=True))).astype(BF16)
                dqs.append(_dot(ds, kp))
                dk_ref[:, sl] += _dot(ds, qm, TN)
                dv_ref[:, sl] += _dot(pb, dom, TN)
            o = jnp.where(lo, outs[0], outs[1])
            dqn = jnp.where(lo, dqs[0], dqs[1]) * ATT_SCALE
            dpm_ref[:, sl] = _head_norm_bwd(dqn, qhat, rr, g, lo).astype(BF16)
            dpm_ref[:, MEM_WIDTH + p * LANES:MEM_WIDTH + (p + 1) * LANES] = (
                dmo * o * (sg * (1.0 + gm * (1.0 - sg)))).astype(BF16)
            gacc_ref[:, sl] += jnp.sum(dqn * qhat, axis=0, keepdims=True)

        @pl.when(pl.program_id(0) == n - 1)
        def _():
            dg_ref[...] = _fold_heads(gacc_ref[...])

    return pl.pallas_call(
        body, name="mem_attn_bwd", grid=(n,),
        out_shape=(jax.ShapeDtypeStruct((s, 512), BF16), jax.ShapeDtypeStruct((N_MEM, MEM_WIDTH), F32),
                   jax.ShapeDtypeStruct((N_MEM, MEM_WIDTH), F32), jax.ShapeDtypeStruct((1, LANES), F32)),
        in_specs=[_rows(t, 512), _rows(t, MEM_WIDTH), _full((N_MEM, MEM_WIDTH)), _full((N_MEM, MEM_WIDTH)),
                  _full((1, MEM_WIDTH))],
        out_specs=(_rows(t, 512), _full((N_MEM, MEM_WIDTH)), _full((N_MEM, MEM_WIDTH)), _full((1, LANES))),
        scratch_shapes=[pltpu.VMEM((1, MEM_WIDTH), F32)],
        compiler_params=_params(),
    )(pm, dmm, kmn, vmb, mq_g)


def _mem_bwd(dkn, dvm, kv, mnb, mem, w_kv, mk_g, mem_norm_g):
    n = mem.shape[0]

    def body(dkn_ref, dvm_ref, kv_ref, mn_ref, mem_ref, w_ref, kg_ref, g_ref, dw_ref, dg_ref, dkg_ref, dkv_ref):
        lo = _lane_lo((n, LANES))
        gacc = []
        for p in range(MEM_WIDTH // LANES):
            sl = slice(p * LANES, (p + 1) * LANES)
            kb = kv_ref[:, sl]
            rr = _head_rms(kb, lo)
            khat = kb * rr
            dk = dkn_ref[:, sl]
            dkv_ref[:, sl] = _head_norm_bwd(dk, khat, rr, kg_ref[:, sl], lo).astype(BF16)
            gacc.append(jnp.sum(dk * khat, axis=0, keepdims=True))
        dkg_ref[...] = _fold_heads(jnp.concatenate(gacc, axis=1))
        dkv_ref[:, MEM_WIDTH:] = dvm_ref[...].astype(BF16)
        dkv = dkv_ref[...]
        dw_ref[...] = _dot(mn_ref[...], dkv, TN)
        dmn = _dot(dkv, w_ref[...], NT)
        xm = mem_ref[...]
        rr = lax.rsqrt(jnp.mean(xm * xm, axis=-1, keepdims=True) + EPS)
        dg_ref[...] = jnp.sum(dmn * (xm * rr), axis=0, keepdims=True)

    return pl.pallas_call(
        body, name="mem_bwd",
        out_shape=(jax.ShapeDtypeStruct((D_MODEL, 2 * MEM_WIDTH), F32), jax.ShapeDtypeStruct((1, D_MODEL), F32),
                   jax.ShapeDtypeStruct((1, LANES), F32)),
        scratch_shapes=[pltpu.VMEM((n, 2 * MEM_WIDTH), BF16)],
        compiler_params=pltpu.CompilerParams(vmem_limit_bytes=VMEM_LIMIT),
    )(dkn, dvm, kv, mnb, mem, w_kv, mk_g, mem_norm_g)


def _pool_bwd(pa, db, dma, w4, pscale):
    s = pa.shape[0]
    t = TILE
    n = s // t
    ext = t + POOL_HALO

    def body(pa_ref, d_ref, dma_ref, w4_ref, sc_ref, dpa_ref, dw4_ref, dsc_ref, ext_ref, w_ref, dw_ref):
        i = pl.program_id(0)

        @pl.when(i == 0)
        def _():
            dw_ref[...] = jnp.zeros_like(dw_ref)
            dsc_ref[...] = jnp.zeros_like(dsc_ref)
            ext_ref[t:ext, :] = jnp.zeros((POOL_HALO, POOL_WIDTH), F32)
            w_ref[...] = _pool_block_diag(w4_ref[...])

        dbv = d_ref[...]
        z = _dot(dbv, w_ref[...])
        ga = pa_ref[:, POOL_WIDTH:2 * POOL_WIDTH]
        sg = _sig(ga)
        dma_v = dma_ref[...]
        dya = dma_v * (ga * sg)
        dpa_ref[:, POOL_WIDTH:2 * POOL_WIDTH] = (dma_v * (z * sc_ref[...]) * (sg * (1.0 + ga * (1.0 - sg)))).astype(BF16)
        dsc_ref[...] += jnp.sum(dya * z, axis=0, keepdims=True)
        dzb = (dya * sc_ref[...]).astype(BF16)
        dw_ref[...] += _dot(dbv, dzb, TN)
        dd = _dot(dzb, w_ref[...], NT)
        lane = lax.broadcasted_iota(jnp.int32, (t, POOL_WIDTH), 1)
        pos = (lax.broadcasted_iota(jnp.int32, (t, POOL_WIDTH), 0) + ((n - 1 - i) * t + 1)).astype(F32)
        ext_ref[0:t, :] = dd / jnp.minimum(pos, _pool_window(lane))
        e = ext_ref[...]
        s2 = e + pltpu.roll(e, ext - 1, axis=0)
        s4 = s2 + pltpu.roll(s2, ext - 2, axis=0)
        s8 = s4 + pltpu.roll(s4, ext - 4, axis=0)
        s16 = s8 + pltpu.roll(s8, ext - 8, axis=0)
        lane_e = lax.broadcasted_iota(jnp.int32, (ext, POOL_WIDTH), 1)
        win = _pool_pick(lane_e, s2, s4, s8, s16)[0:t, :]
        dpa_ref[:, 0:POOL_WIDTH] = (win - dd).astype(BF16)
        ext_ref[t:ext, :] = ext_ref[0:POOL_HALO, :]

        @pl.when(i == n - 1)
        def _():
            own = jnp.where(_same_group((POOL_WIDTH, POOL_WIDTH)), dw_ref[...], 0.0)
            dw4_ref[...] = jnp.dot(own, _group_onehot((POOL_WIDTH, HEAD_DIM), True), preferred_element_type=F32,
                                   precision=lax.Precision.HIGHEST)

    return pl.pallas_call(
        body, name="pool_bwd", grid=(n,),
        out_shape=(jax.ShapeDtypeStruct((s, 512), BF16), jax.ShapeDtypeStruct((POOL_ROWS, HEAD_DIM), F32),
                   jax.ShapeDtypeStruct((1, POOL_WIDTH), F32)),
        in_specs=[_rows_rev(t, 512, n), _rows_rev(t, POOL_WIDTH, n), _rows_rev(t, POOL_WIDTH, n),
                  _full((POOL_ROWS, HEAD_DIM)), _full((1, POOL_WIDTH))],
        out_specs=(_rows_rev(t, 512, n), _full((POOL_ROWS, HEAD_DIM)), _full((1, POOL_WIDTH))),
        scratch_shapes=[pltpu.VMEM((ext, POOL_WIDTH), F32), pltpu.VMEM((POOL_WIDTH, POOL_WIDTH), BF16),
                        pltpu.VMEM((POOL_WIDTH, POOL_WIDTH), F32)],
        compiler_params=_params(),
    )(pa, db, dma, w4, pscale)


def _fox_prep(dmb, gb, o, r4):
    s = dmb.shape[0]
    t = TILE
    n = s // t
    pairs = FOX_HEADS // 2

    def body(dmb_ref, gb_ref, o_ref, r_ref, doa_ref, dgb_ref, rr_ref):
        lane = lax.broadcasted_iota(jnp.int32, (t, LANES), 1)
        lo = lane < HEAD_DIM
        d_os = []
        delta = jnp.zeros((t, LANES), F32)
        for p in range(pairs):
            sl = slice(p * LANES, (p + 1) * LANES)
            g = gb_ref[:, sl]
            sg = _sig(g)
            dm = dmb_ref[:, sl]
            ov = o_ref[:, sl]
            d_o = dm * (g * sg)
            d_os.append(d_o)
            dgb_ref[:, sl] = (dm * ov * (sg * (1.0 + g * (1.0 - sg)))).astype(BF16)
            prod = d_o * ov
            delta = jnp.where(lane == 2 * p, jnp.sum(jnp.where(lo, prod, 0.0), axis=-1, keepdims=True), delta)
            delta = jnp.where(lane == 2 * p + 1, jnp.sum(jnp.where(lo, 0.0, prod), axis=-1, keepdims=True), delta)
            rr_ref[p, 0] = r_ref[p].T[0:8, :]
        minus_delta = _spread3(-delta)
        for h in range(FOX_HEADS):
            blk = slice(h * LANES, (h + 1) * LANES)
            doa_ref[:, blk] = _head_block(d_os[h // 2], h % 2, lo, minus_delta[:, blk])

    return pl.pallas_call(
        body, name="fox_prep", grid=(n,),
        out_shape=(jax.ShapeDtypeStruct((s, HEAD_BLOCKS), BF16), jax.ShapeDtypeStruct((s, FOX_WIDTH), BF16),
                   jax.ShapeDtypeStruct((pairs, n, 8, t), F32)),
        in_specs=[_rows(t, FOX_WIDTH), _rows(t, FOX_WIDTH), _rows(t, FOX_WIDTH),
                  pl.BlockSpec((pairs, t, LANES), lambda i: (0, i, 0))],
        out_specs=(_rows(t, HEAD_BLOCKS), _rows(t, FOX_WIDTH), pl.BlockSpec((pairs, 1, 8, t), lambda i: (0, i, 0, 0))),
        compiler_params=_params(),
    )(dmb, gb, o, r4)


def _fox_bwd(ka, va, qa, doa, rr):
    s = ka.shape[0]
    t = TILE
    n = s // t
    pair_w = 2 * LANES

    def body(ka_ref, va_ref, qa_ref, doa_ref, rr_ref, dka_ref, dva_ref, dqa_ref):
        j = pl.program_id(1)

        @pl.when(j == 0)
        def _():
            dqa_ref[...] = jnp.zeros_like(dqa_ref)

        causal = lax.broadcasted_iota(jnp.int32, (t, t), 0) <= lax.broadcasted_iota(jnp.int32, (t, t), 1)
        kas = (ka_ref[:, 0:LANES], ka_ref[:, LANES:pair_w])
        vas = (va_ref[:, 0:LANES], va_ref[:, LANES:pair_w])

        def step(i, carry, masked):
            rows = pl.ds(pl.multiple_of(i * t, t), t)
            new = []
            for hh in range(2):
                cols = slice(hh * LANES, (hh + 1) * LANES)
                dk_a, dv_a = carry[hh]
                qb = qa_ref[rows, cols]
                d_o = doa_ref[rows, cols]
                arg = _dot(kas[hh], qb, NT) - rr_ref[0, i, hh:hh + 1, :]
                if masked:
                    arg = jnp.where(causal, arg, -1e30)
                pt = jnp.exp(arg)
                dst = (pt * _dot(vas[hh], d_o, NT)).astype(BF16)
                dv_a = dv_a + _dot(pt.astype(BF16), d_o)
                dk_a = dk_a + _dot(dst, qb)
                dqa_ref[rows, cols] += _dot(dst, kas[hh], TN)
                new.append((dk_a, dv_a))
            return tuple(new)

        zero = jnp.zeros((t, LANES), F32)
        carry = step(j, ((zero, zero), (zero, zero)), masked=True)
        res = lax.fori_loop(j + 1, n, functools.partial(step, masked=False), carry)
        for hh in range(2):
            cols = slice(hh * LANES, (hh + 1) * LANES)
            dka_ref[:, cols] = res[hh][0]
            dva_ref[:, cols] = res[hh][1]

    tile_spec = pl.BlockSpec((t, pair_w), lambda p, j: (j, p))
    full_spec = pl.BlockSpec((s, pair_w), lambda p, j: (0, p))
    return pl.pallas_call(
        body, name="fox_bwd", grid=(FOX_HEADS // 2, n),
        out_shape=(jax.ShapeDtypeStruct((s, HEAD_BLOCKS), F32),) * 3,
        in_specs=[tile_spec, tile_spec, full_spec, full_spec,
                  pl.BlockSpec((1, n, 8, t), lambda p, j: (p, 0, 0, 0))],
        out_specs=(tile_spec, tile_spec, full_spec),
        compiler_params=_params(2),
    )(ka, va, qa, doa, rr)


def _fox_post(dqa, dka, dva, qk, fb, bf_pad, fq_g, fk_g):
    s = dqa.shape[0]
    t = TILE
    n = s // t

    def body(dqa_ref, dka_ref, dva_ref, qk_ref, fb_ref, bf_ref, qg_ref, kg_ref,
             dqk_ref, dv_ref, dfb_ref, dqg_ref, dkg_ref, dbf_ref, qacc_ref, kacc_ref, carry_ref):
        i = pl.program_id(0)

        @pl.when(i == 0)
        def _():
            qacc_ref[...] = jnp.zeros_like(qacc_ref)
            kacc_ref[...] = jnp.zeros_like(kacc_ref)
            dbf_ref[...] = jnp.zeros_like(dbf_ref)
            carry_ref[...] = jnp.zeros_like(carry_ref)

        lane = lax.broadcasted_iota(jnp.int32, (t, LANES), 1)
        row = lax.broadcasted_iota(jnp.int32, (t, LANES), 0)
        lo = lane < HEAD_DIM

        def head_blocks(ref, p):
            return ref[:, 2 * p * LANES:(2 * p + 1) * LANES], ref[:, (2 * p + 1) * LANES:(2 * p + 2) * LANES]

        dq_sum = jnp.zeros((t, LANES), F32)
        dk_sum = jnp.zeros((t, LANES), F32)
        for p in range(FOX_WIDTH // LANES):
            sl = slice(p * LANES, (p + 1) * LANES)
            dq0, dq1 = head_blocks(dqa_ref, p)
            dk0, dk1 = head_blocks(dka_ref, p)
            dv0, dv1 = head_blocks(dva_ref, p)
            dv_ref[:, sl] = _pair_block(dv0, dv1, lo).astype(BF16)
            dq_sum = dq_sum + (dq0 + dq1)
            dk_sum = dk_sum + (dk0 + dk1)
            for off, pair, g_ref, acc_ref, scale in ((0, _pair_block(dq0, dq1, lo), qg_ref, qacc_ref, ATT_SCALE),
                                                     (FOX_WIDTH, _pair_block(dk0, dk1, lo), kg_ref, kacc_ref, 1.0)):
                raw = qk_ref[:, off + p * LANES:off + (p + 1) * LANES]
                rr = _head_rms(raw, lo)
                xhat = raw * rr
                dn = pair * scale
                dqk_ref[:, off + p * LANES:off + (p + 1) * LANES] = _head_norm_bwd(
                    dn, xhat, rr, g_ref[:, sl], lo).astype(BF16)
                acc_ref[:, sl] += jnp.sum(dn * xhat, axis=0, keepdims=True)

        acc = (pltpu.roll(dq_sum, LANES - KEY_SUM_LANE, axis=1) - pltpu.roll(dk_sum, LANES - QUERY_SUM_LANE, axis=1))
        acc = jnp.where(lane < FOX_HEADS, acc, 0.0)
        sh = 1
        while sh < t:
            acc = acc + jnp.where(row < t - sh, pltpu.roll(acc, t - sh, axis=0), 0.0)
            sh *= 2
        dlogf = acc + carry_ref[...]
        dfb_ref[...] = dlogf
        carry_ref[...] = dfb_ref[0:1, :]
        z = fb_ref[...] + bf_ref[...]
        dz = jnp.where(lane < FOX_HEADS, dlogf * (1.0 / (1.0 + jnp.exp(z))), 0.0)
        dfb_ref[...] = dz
        dbf_ref[...] += jnp.sum(dz, axis=0, keepdims=True)

        @pl.when(i == n - 1)
        def _():
            dqg_ref[...] = _fold_heads(qacc_ref[...])
            dkg_ref[...] = _fold_heads(kacc_ref[...])

    return pl.pallas_call(
        body, name="fox_post", grid=(n,),
        out_shape=(jax.ShapeDtypeStruct((s, 2 * FOX_WIDTH), BF16), jax.ShapeDtypeStruct((s, FOX_WIDTH), BF16),
                   jax.ShapeDtypeStruct((s, LANES), F32), jax.ShapeDtypeStruct((1, LANES), F32),
                   jax.ShapeDtypeStruct((1, LANES), F32), jax.ShapeDtypeStruct((1, LANES), F32)),
        in_specs=[_rows_rev(t, HEAD_BLOCKS, n), _rows_rev(t, HEAD_BLOCKS, n), _rows_rev(t, HEAD_BLOCKS, n),
                  _rows_rev(t, 2 * FOX_WIDTH, n), _rows_rev(t, LANES, n), _full((1, LANES)),
                  _full((1, FOX_WIDTH)), _full((1, FOX_WIDTH))],
        out_specs=(_rows_rev(t, 2 * FOX_WIDTH, n), _rows_rev(t, FOX_WIDTH, n), _rows_rev(t, LANES, n),
                   _full((1, LANES)), _full((1, LANES)), _full((1, LANES))),
        scratch_shapes=[pltpu.VMEM((1, FOX_WIDTH), F32), pltpu.VMEM((1, FOX_WIDTH), F32), pltpu.VMEM((1, LANES), F32)],
        compiler_params=_params(),
    )(dqa, dka, dva, qk, fb, bf_pad, fq_g, fk_g)


def _assemble_dproj(dp_ref, dpa_ref, dqk_ref, dv_ref, dgb_ref, dpm_ref, dfb_ref):
    dp_ref[:, PA_LO:QB_LO] = dpa_ref[...]
    dp_ref[:, QB_LO:VB_LO] = dqk_ref[...]
    dp_ref[:, VB_LO:GB_LO] = dv_ref[...]
    dp_ref[:, GB_LO:PM_LO] = dgb_ref[...]
    dp_ref[:, PM_LO:FB_LO] = dpm_ref[...]
    dp_ref[:, FB_LO:PROJ_PAD] = dfb_ref[...].astype(BF16)


def _dproj_specs(t):
    return [_rows(t, 512), _rows(t, 2 * FOX_WIDTH), _rows(t, FOX_WIDTH), _rows(t, FOX_WIDTH), _rows(t, 512),
            _rows(t, LANES)]


def _in_bwd_x(x, dy, norm_g, wp, dparts):
    s = x.shape[0]
    t = TILE
    n = s // t

    def body(x_ref, dy_ref, g_ref, wp_ref, dpa_ref, dqk_ref, dv_ref, dgb_ref, dpm_ref, dfb_ref, gx_ref, dg_ref, dp_ref):
        @pl.when(pl.program_id(0) == 0)
        def _():
            dg_ref[...] = jnp.zeros_like(dg_ref)

        _assemble_dproj(dp_ref, dpa_ref, dqk_ref, dv_ref, dgb_ref, dpm_ref, dfb_ref)
        dh = _dot(dp_ref[...], wp_ref[...])
        xv = x_ref[...]
        rr = lax.rsqrt(jnp.mean(xv * xv, axis=-1, keepdims=True) + EPS)
        xhat = xv * rr
        a = dh * g_ref[...]
        gx_ref[...] = dy_ref[...] + rr * (a - xhat * jnp.mean(xhat * a, axis=-1, keepdims=True))
        dg_ref[...] += jnp.sum(dh * xhat, axis=0, keepdims=True)

    return pl.pallas_call(
        body, name="in_bwd_x", grid=(n,),
        out_shape=(jax.ShapeDtypeStruct((s, D_MODEL), F32), jax.ShapeDtypeStruct((1, D_MODEL), F32)),
        in_specs=[_rows(t, D_MODEL), _rows(t, D_MODEL), _full((1, D_MODEL)), _full((PROJ_PAD, D_MODEL))] + _dproj_specs(t),
        out_specs=(_rows(t, D_MODEL), _full((1, D_MODEL))),
        scratch_shapes=[pltpu.VMEM((t, PROJ_PAD), BF16)],
        compiler_params=_params(),
    )(x, dy, norm_g, wp, *dparts)


def _in_bwd_w(hb, dparts):
    s = hb.shape[0]
    t = TILE
    n = s // t
    f_hi = F_ORIG_LO + FOX_HEADS

    def body(h_ref, dpa_ref, dqk_ref, dv_ref, dgb_ref, dpm_ref, dfb_ref, dw_ref):
        @pl.when(pl.program_id(0) == 0)
        def _():
            dw_ref[...] = jnp.zeros_like(dw_ref)

        hv = h_ref[...]
        for lo, ref in ((0, dpa_ref), (QB_LO, dqk_ref), (VB_LO, dv_ref), (f_hi, dgb_ref), (f_hi + FOX_WIDTH, dpm_ref)):
            dw_ref[lo:lo + ref.shape[1], :] += _dot(ref[...], hv, TN)
        dw_ref[F_ORIG_LO:f_hi, :] += _dot(dfb_ref[...].astype(BF16), hv, TN)[0:FOX_HEADS, :]

    return pl.pallas_call(
        body, name="in_bwd_w", grid=(n,),
        out_shape=jax.ShapeDtypeStruct((IN_WIDTH, D_MODEL), F32),
        in_specs=[_rows(t, D_MODEL)] + _dproj_specs(t),
        out_specs=_full((IN_WIDTH, D_MODEL)),
        compiler_params=_params(),
    )(hb, *dparts)


def _adamw_math(w_ref, gv, m_ref, v_ref, d_ref, nm_ref, nv_ref):
    nm = ADAM_B1 * m_ref[...] + (1.0 - ADAM_B1) * gv
    nv = ADAM_B2 * v_ref[...] + (1.0 - ADAM_B2) * (gv * gv)
    m_hat = nm / (1.0 - ADAM_B1 ** ADAM_STEP)
    v_hat = nv / (1.0 - ADAM_B2 ** ADAM_STEP)
    d_ref[...] = -ADAM_LR * (m_hat / (jnp.sqrt(v_hat) + ADAM_EPS) + ADAM_WD * w_ref[...])
    nm_ref[...] = nm
    nv_ref[...] = nv


def _adamw(name, w, g, m, v):
    rows, cols = w.shape
    tc = 256 if rows * cols > 256 * 1024 else cols
    n = cols // tc

    def body(w_ref, g_ref, m_ref, v_ref, d_ref, nm_ref, nv_ref):
        _adamw_math(w_ref, g_ref[...], m_ref, v_ref, d_ref, nm_ref, nv_ref)

    spec = pl.BlockSpec((rows, tc), lambda i: (0, i))
    return pl.pallas_call(
        body, name=name, grid=(n,),
        out_shape=(jax.ShapeDtypeStruct((rows, cols), F32),) * 3,
        in_specs=[spec] * 4, out_specs=(spec,) * 3,
        compiler_params=_params(),
    )(w, g, m, v)


def _adamw_small(vec, dw4, leaves, pool):
    nl = len(VEC_LEAVES) + 1

    def body(*refs):
        vec_ref, dw4_ref = refs[0:2]
        wmv = refs[2:2 + 3 * nl]
        loss_ref = refs[2 + 3 * nl]
        outs = refs[3 + 3 * nl:]
        loss_ref[...] = vec_ref[VEC_LOSS_ROW:VEC_LOSS_ROW + 1, 0:1]
        for k in range(nl):
            if k < nl - 1:
                _, row, width = VEC_LEAVES[k]
                gv = vec_ref[row:row + 1, 0:width]
            else:
                gv = dw4_ref[...]
            w_ref, m_ref, v_ref = wmv[3 * k:3 * k + 3]
            g_ref, d_ref, nm_ref, nv_ref = outs[4 * k:4 * k + 4]
            g_ref[...] = gv
            _adamw_math(w_ref, gv, m_ref, v_ref, d_ref, nm_ref, nv_ref)

    shapes = [jax.ShapeDtypeStruct((1, width), F32) for _, _, width in VEC_LEAVES] + [
        jax.ShapeDtypeStruct(dw4.shape, F32)]
    flat_in = [a for triple in list(leaves) + [pool] for a in triple]
    res = pl.pallas_call(
        body, name="adamw_small",
        out_shape=(jax.ShapeDtypeStruct((1, 1), F32),) + tuple(s for s in shapes for _ in range(4)),
        compiler_params=pltpu.CompilerParams(vmem_limit_bytes=VMEM_LIMIT),
    )(vec, dw4, *flat_in)
    per = [res[1 + 4 * k:5 + 4 * k] for k in range(nl)]
    return res[0], [p[0] for p in per], [p[1] for p in per], [p[2] for p in per], [p[3] for p in per]


def _full_w_in_padded(halves):
    cols = IN_WIDTH // 4
    w_t = halves.reshape(4, 2, cols, D_MODEL // 2).transpose(0, 2, 1, 3).reshape(IN_WIDTH, D_MODEL)
    return jnp.concatenate([
        w_t[0:F_ORIG_LO], w_t[F_ORIG_LO + FOX_HEADS:], w_t[F_ORIG_LO:F_ORIG_LO + FOX_HEADS],
        jnp.zeros((PROJ_PAD - IN_WIDTH, D_MODEL), w_t.dtype)], axis=0)


def _tile_heads(g, n):
    return jnp.tile(g.reshape(1, HEAD_DIM), (1, n))


def kernel(x, mem, norm_g, w_in, b_f, w_pool, pool_scale, fox_q_g, fox_k_g, mem_norm_g, w_mem_kv, mem_q_g, mem_k_g, w_out, loss_target, m_norm_g, m_w_in, m_b_f, m_w_pool, m_pool_scale, m_fox_q_g, m_fox_k_g, m_mem_norm_g, m_w_mem_kv, m_mem_q_g, m_mem_k_g, m_w_out, v_norm_g, v_w_in, v_b_f, v_w_pool, v_pool_scale, v_fox_q_g, v_fox_k_g, v_mem_norm_g, v_w_mem_kv, v_mem_q_g, v_mem_k_g, v_w_out):
    w_in_t, m_w_in_t, v_w_in_t = w_in[0].T, m_w_in[0].T, v_w_in[0].T
    axes = (1, 0, 0)

    g_in, g_kv, g_out = _all_gather_weights([w_in_t, w_mem_kv[0], w_out[0]], axes)
    w_kv_b = g_kv.reshape(D_MODEL, 2 * MEM_WIDTH)
    w_out_b = g_out.reshape(D_MODEL, D_MODEL)
    w4 = w_pool.reshape(POOL_ROWS, HEAD_DIM)
    grad_x, dwp, dw_kv, dw_out, vec_leaves, loss_row, dw4 = _local_grads(
        x[0], mem[0], loss_target[0], g_in, w_kv_b, w_out_b, norm_g, b_f, w4, pool_scale, fox_q_g, fox_k_g,
        mem_norm_g, mem_q_g, mem_k_g)

    gparts = [dwp.reshape(4, IN_WIDTH // 4, D_MODEL), dw_kv.reshape(4, D_MODEL // 4, 2 * MEM_WIDTH),
              dw_out.reshape(4, D_MODEL // 4, D_MODEL)]
    g_w_in_t, g_w_kv, g_w_out, vec, dw4_sum = _grad_reduce(gparts, axes, vec_leaves, loss_row, dw4)

    small_wmv = [(norm_g, m_norm_g, v_norm_g), (mem_norm_g, m_mem_norm_g, v_mem_norm_g),
                 (pool_scale, m_pool_scale, v_pool_scale), (b_f, m_b_f, v_b_f), (fox_q_g, m_fox_q_g, v_fox_q_g),
                 (fox_k_g, m_fox_k_g, v_fox_k_g), (mem_q_g, m_mem_q_g, v_mem_q_g), (mem_k_g, m_mem_k_g, v_mem_k_g)]
    pool_wmv = tuple(a.reshape(POOL_ROWS, HEAD_DIM) for a in (w_pool, m_w_pool, v_w_pool))
    loss, *small_out = _adamw_small(vec, dw4_sum, small_wmv, pool_wmv)
    big = [[g_w_in_t.T[None], g_w_kv[None], g_w_out[None]]]
    upd = [[a.T for a in _adamw("adamw_w_in", w_in_t, g_w_in_t, m_w_in_t, v_w_in_t)],
           _adamw("adamw_w_mem_kv", w_mem_kv[0], g_w_kv, m_w_mem_kv[0], v_w_mem_kv[0]),
           _adamw("adamw_w_out", w_out[0], g_w_out, m_w_out[0], v_w_out[0])]
    big += [[u[k][None] for u in upd] for k in range(3)]

    def leaves(k):
        sm = small_out[k]
        b_in, b_kv, b_out = big[k]
        return (sm[0], b_in, sm[3], sm[8].reshape(w_pool.shape), sm[2], sm[4], sm[5], sm[1], b_kv, sm[6], sm[7], b_out)

    return (loss.reshape(()), grad_x[None], *leaves(0), *leaves(1), *leaves(2), *leaves(3))


def _local_grads(xs, mems, tgt, w_in_b, w_kv_b, w_out_b, norm_g, b_f, w4, pool_scale, fox_q_g, fox_k_g,
                 mem_norm_g, mem_q_g, mem_k_g):
    wp = _full_w_in_padded(w_in_b)
    bf_pad = jnp.pad(b_f, ((0, 0), (0, LANES - FOX_HEADS)))
    fq_g, fk_g = _tile_heads(fox_q_g, FOX_HEADS), _tile_heads(fox_k_g, FOX_HEADS)
    mq_g, mk_g = _tile_heads(mem_q_g, 4), _tile_heads(mem_k_g, 4)

    mnb, kv, kmn, vmb = _mem_fwd(mems, mem_norm_g, w_kv_b, mk_g)
    hb, pa, qk, qa, ka, va, gb, pm, fb = _fwd_in(xs, norm_g, wp, bf_pad, fq_g, fk_g)
    ma, db = _pool_fwd(pa, w4, pool_scale)
    mm = _mem_attn_fwd(pm, kmn, vmb, mq_g)
    o, mb, r4 = _fox_fwd(qa, ka, va, gb)
    dy, dma, dmb, dmm, dw_out, loss_row = _out_loss(xs, tgt, ma, mb, mm, w_out_b)

    dpm, dkmn, dvm, dmq_g = _mem_attn_bwd(pm, dmm, kmn, vmb, mq_g)
    dw_kv, dmemnorm_g, dmk_g = _mem_bwd(dkmn, dvm, kv, mnb, mems, w_kv_b, mk_g, mem_norm_g)
    dpa, dw4, dpscale = _pool_bwd(pa, db, dma, w4, pool_scale)
    doa, dgb, rr = _fox_prep(dmb, gb, o, r4)
    dka, dva, dqa = _fox_bwd(ka, va, qa, doa, rr)
    dqk, dvb, dfb, dfq_g, dfk_g, dbf = _fox_post(dqa, dka, dva, qk, fb, bf_pad, fq_g, fk_g)
    dparts = (dpa, dqk, dvb, dgb, dpm, dfb)
    grad_x, dnorm_g = _in_bwd_x(xs, dy, norm_g, wp, dparts)
    dwp = _in_bwd_w(hb, dparts)

    vec_leaves = (dnorm_g, dmemnorm_g, dpscale, dbf, dfq_g, dfk_g, dmq_g, dmk_g)
    return grad_x, dwp, dw_kv, dw_out, vec_leaves, loss_row, dw4
```

```python
import functools

import jax
import jax.numpy as jnp
from jax import lax
from jax.experimental import pallas as pl
from jax.experimental.pallas import tpu as pltpu

F32 = jnp.float32
BF16 = jnp.bfloat16
MESH = pl.DeviceIdType.MESH

D_MODEL = 1024
HEAD_DIM = 64
POOL_WIDTH = 256
FOX_WIDTH = 512
FOX_HEADS = 8
MEM_WIDTH = 256
N_MEM = 256
IN_WIDTH = 3080
EPS = 1e-6
ATT_SCALE = 0.125

ADAM_LR = 0.001
ADAM_B1 = 0.9
ADAM_B2 = 0.999
ADAM_EPS = 1e-08
ADAM_WD = 0.01
ADAM_STEP = 10

LANES = 128
PA_LO, QB_LO, KB_LO, VB_LO, GB_LO, PM_LO, FB_LO, PROJ_PAD = 0, 512, 1024, 1536, 2048, 2560, 3072, 3200
F_ORIG_LO = 2048

TILE = 512
VMEM_LIMIT = 56 * 1024 * 1024

VEC_LEAVES = (("norm_g", 0, 1024), ("mem_norm_g", 1, 1024), ("pool_scale", 2, 256), ("b_f", 3, 8),
              ("fox_q_g", 4, 64), ("fox_k_g", 5, 64), ("mem_q_g", 6, 64), ("mem_k_g", 7, 64))
VEC_LOSS_ROW = 8
VEC_ROWS = 16
POOL_ROWS = 256


def _params(n_grid=1, vmem=VMEM_LIMIT):
    return pltpu.CompilerParams(dimension_semantics=("arbitrary",) * n_grid, vmem_limit_bytes=vmem)


def _rows(t, w):
    return pl.BlockSpec((t, w), lambda i: (i, 0))


def _rows_rev(t, w, n):
    return pl.BlockSpec((t, w), lambda i: (n - 1 - i, 0))


def _full(shape):
    return pl.BlockSpec(shape, lambda i: (0,) * len(shape))


def _sig(x):
    return 1.0 / (1.0 + jnp.exp(-x))


def _lane_lo(shape):
    return lax.broadcasted_iota(jnp.int32, shape, 1) < HEAD_DIM


def _pair_sum(v, lo):
    s0 = jnp.sum(jnp.where(lo, v, 0.0), axis=-1, keepdims=True)
    s1 = jnp.sum(jnp.where(lo, 0.0, v), axis=-1, keepdims=True)
    return jnp.where(lo, s0, s1)


def _head_rms(blk, lo):
    return lax.rsqrt(_pair_sum(blk * blk, lo) * (1.0 / HEAD_DIM) + EPS)


def _head_norm_bwd(dyn, xhat, rr, g, lo):
    a = dyn * g
    return rr * (a - xhat * (_pair_sum(xhat * a, lo) * (1.0 / HEAD_DIM)))


def _fold_heads(acc):
    tot = acc[:, 0:LANES]
    for p in range(1, acc.shape[1] // LANES):
        tot = tot + acc[:, p * LANES:(p + 1) * LANES]
    return tot + pltpu.roll(tot, HEAD_DIM, axis=1)


def _lane_pick(v, lane, idx):
    return jnp.sum(jnp.where(lane == idx, v, 0.0), axis=-1, keepdims=True)


NT = (((1,), (1,)), ((), ()))
TN = (((0,), (0,)), ((), ()))


def _dot(a, b, dims=None):
    if dims is None:
        return jnp.dot(a, b, preferred_element_type=F32)
    return lax.dot_general(a, b, dims, preferred_element_type=F32)


def _my_place():
    return lax.axis_index("x"), lax.axis_index("y"), lax.axis_index("c")


def _half_dims(shape, axis):
    return (shape[0] // 2, shape[1]) if axis == 0 else (shape[0], shape[1] // 2)


def _half_of(ref, axis, core, lead=False):
    rows, cols = ref.shape[-2:]
    if axis == 0:
        idx = (pl.ds(pl.multiple_of(core * (rows // 2), 16), rows // 2), slice(None))
    else:
        idx = (slice(None), pl.ds(pl.multiple_of(core * (cols // 2), LANES), cols // 2))
    return ref.at[(slice(None),) + idx] if lead else ref.at[idx]


def _all_gather_weights(shards, axes):
    n = len(shards)
    dims = [_half_dims(a.shape, axis) for a, axis in zip(shards, axes)]

    def body(*refs):
        ins, outs = refs[0:n], refs[n:2 * n]
        f32_bufs, bf_bufs = refs[2 * n:3 * n], refs[3 * n:4 * n]
        send_sems, recv_sems, local_sems = refs[4 * n:]
        x, y, c = _my_place()
        me, sibling = (x, y, c), (x, y, 1 - c)
        chips = [(1 - x, y), (x, 1 - y), (1 - x, 1 - y)]

        loads = []
        for a in range(n):
            cp = pltpu.make_async_copy(_half_of(ins[a], axes[a], c), f32_bufs[a], local_sems.at[a])
            cp.start()
            loads.append(cp)

        def blk(a, px, py, pc):
            return outs[a].at[4 * px + 2 * py + pc]

        def copy(a, k, block, to, src=None):
            return pltpu.make_async_remote_copy(
                src_ref=blk(a, *block) if src is None else src, dst_ref=blk(a, *block),
                send_sem=send_sems.at[7 * a + k], recv_sem=recv_sems.at[7 * a + k], device_id=to, device_id_type=MESH)

        first, keeps = [], []
        for a in range(n):
            loads[a].wait()
            bf_bufs[a][...] = f32_bufs[a][...].astype(BF16)
            keep = pltpu.make_async_copy(bf_bufs[a], blk(a, *me), local_sems.at[n + a])
            keep.start()
            keeps.append(keep)
            mine = [copy(a, 0, me, sibling, src=bf_bufs[a])]
            mine += [copy(a, 1 + j, me, (*chip, c), src=bf_bufs[a]) for j, chip in enumerate(chips)]
            for cp in mine:
                cp.start()
            first += mine
        passed = []
        for a in range(n):
            for j, chip in enumerate(chips):
                copy(a, 1 + j, (*chip, c), me).wait_recv()
                cp = copy(a, 4 + j, (*chip, c), sibling)
                cp.start()
                passed.append(cp)
        for a in range(n):
            copy(a, 0, sibling, me).wait_recv()
            for j, chip in enumerate(chips):
                copy(a, 4 + j, (*chip, 1 - c), me).wait_recv()
        for cp in first + passed:
            cp.wait_send()
        for keep in keeps:
            keep.wait()

    any_spec = pl.BlockSpec(memory_space=pl.ANY)
    return pl.pallas_call(
        body, name="weights_all_gather",
        out_shape=tuple(jax.ShapeDtypeStruct((8, h, w), BF16) for h, w in dims),
        in_specs=[any_spec] * n, out_specs=(any_spec,) * n,
        scratch_shapes=[pltpu.VMEM(d, F32) for d in dims] + [pltpu.VMEM(d, BF16) for d in dims] + [
            pltpu.SemaphoreType.DMA((7 * n,)), pltpu.SemaphoreType.DMA((7 * n,)), pltpu.SemaphoreType.DMA((2 * n,))],
        compiler_params=pltpu.CompilerParams(vmem_limit_bytes=VMEM_LIMIT),
    )(*shards)


class _ShardReduce:
    def __init__(self, g_refs, out_refs, axes, bufs, send_sems, recv_sems, local_sems):
        self.g_refs, self.out_refs, self.axes = g_refs, out_refs, axes
        self.recv_a, self.own_a, self.send_b, self.recv_b, self.fin = bufs
        self.send_sems, self.recv_sems, self.local_sems = send_sems, recv_sems, local_sems
        self.n = len(g_refs)
        x, y, self.c = _my_place()
        self.chip = 2 * x + y
        self.sibling = (x, y, 1 - self.c)

    @staticmethod
    def scratch(gparts, axes):
        dims = [_half_dims(g.shape[1:], axis) for g, axis in zip(gparts, axes)]
        shapes = []
        for dtype, lead in ((F32, (4,)), (F32, (4,)), (BF16, (4,)), (BF16, (4,)), (F32, ())):
            shapes += [pltpu.VMEM(lead + d, dtype) for d in dims]
        return shapes

    def _to_sibling(self, a):
        return pltpu.make_async_remote_copy(
            src_ref=_half_of(self.g_refs[a], self.axes[a], 1 - self.c, lead=True), dst_ref=self.recv_a[a],
            send_sem=self.send_sems.at[5 * a], recv_sem=self.recv_sems.at[5 * a], device_id=self.sibling,
            device_id_type=MESH)

    def _own(self, a):
        return pltpu.make_async_copy(_half_of(self.g_refs[a], self.axes[a], self.c, lead=True), self.own_a[a],
                                     self.local_sems.at[a])

    def _to_chip(self, a, k):
        dest = (self.chip + k) % 4
        return pltpu.make_async_remote_copy(
            src_ref=self.send_b[a].at[dest], dst_ref=self.recv_b[a].at[self.chip], send_sem=self.send_sems.at[5 * a + k],
            recv_sem=self.recv_sems.at[5 * a + k], device_id=(dest // 2, dest % 2, self.c), device_id_type=MESH)

    def _give(self, a):
        return pltpu.make_async_remote_copy(
            src_ref=self.fin[a], dst_ref=_half_of(self.out_refs[a], self.axes[a], self.c),
            send_sem=self.send_sems.at[5 * a + 4], recv_sem=self.recv_sems.at[5 * a + 4], device_id=self.sibling,
            device_id_type=MESH)

    def _mine(self, a):
        return pltpu.make_async_copy(self.fin[a], _half_of(self.out_refs[a], self.axes[a], self.c),
                                     self.local_sems.at[a])

    def exchange_with_sibling(self):
        for a in range(self.n):
            self._to_sibling(a).start()
            self._own(a).start()

    def send_to_chips(self):
        for a in range(self.n):
            self._own(a).wait()
            self._to_sibling(a).wait_recv()
            for j in range(4):
                self.send_b[a][j] = (self.own_a[a][j] + self.recv_a[a][j]).astype(BF16)
            for k in range(1, 4):
                self._to_chip(a, k).start()
            keep = pltpu.make_async_copy(self.send_b[a].at[self.chip], self.recv_b[a].at[self.chip],
                                         self.local_sems.at[self.n + a])
            keep.start()
            keep.wait()

    def sum_and_share(self):
        for a in range(self.n):
            for k in range(1, 4):
                self._to_chip(a, k).wait_recv()
            tot = self.recv_b[a][0].astype(F32) + self.recv_b[a][1].astype(F32)
            tot = tot + self.recv_b[a][2].astype(F32)
            self.fin[a][...] = tot + self.recv_b[a][3].astype(F32)
            self._give(a).start()
            self._mine(a).start()

    def finish(self):
        for a in range(self.n):
            self._give(a).wait_recv()
            self._mine(a).wait()
            self._to_sibling(a).wait_send()
            self._give(a).wait_send()
            for k in range(1, 4):
                self._to_chip(a, k).wait_send()


def _mem_fwd(mem, mem_norm_g, w_kv, mk_g):
    n = mem.shape[0]

    def body(mem_ref, g_ref, w_ref, kg_ref, mn_ref, kv_ref, kn_ref, vm_ref):
        xm = mem_ref[...]
        rr = lax.rsqrt(jnp.mean(xm * xm, axis=-1, keepdims=True) + EPS)
        mnb = ((xm * rr) * g_ref[...]).astype(BF16)
        mn_ref[...] = mnb
        kv = _dot(mnb, w_ref[...])
        kv_ref[...] = kv
        lo = _lane_lo((n, LANES))
        for p in range(MEM_WIDTH // LANES):
            sl = slice(p * LANES, (p + 1) * LANES)
            kb = kv[:, sl]
            kn_ref[:, sl] = ((kb * _head_rms(kb, lo)) * kg_ref[:, sl]).astype(BF16)
        vm_ref[...] = kv[:, MEM_WIDTH:].astype(BF16)

    return pl.pallas_call(
        body, name="mem_fwd",
        out_shape=(jax.ShapeDtypeStruct((n, D_MODEL), BF16), jax.ShapeDtypeStruct((n, 2 * MEM_WIDTH), F32),
                   jax.ShapeDtypeStruct((n, MEM_WIDTH), BF16), jax.ShapeDtypeStruct((n, MEM_WIDTH), BF16)),
        compiler_params=pltpu.CompilerParams(vmem_limit_bytes=VMEM_LIMIT),
    )(mem, mem_norm_g, w_kv, mk_g)


AUG_LO = 64
KEY_SUM_LANE = 72
QUERY_SUM_LANE = 80
HEAD_BLOCKS = FOX_HEADS * LANES


def _ones3(lane):
    return jnp.where((lane >= AUG_LO) & (lane < AUG_LO + 3), 1.0, 0.0)


def _spread3(cols):
    hi = cols.astype(BF16)
    rest = cols - hi.astype(F32)
    mid = rest.astype(BF16)
    low = (rest - mid.astype(F32)).astype(BF16)
    r = lax.broadcasted_iota(jnp.int32, (LANES, HEAD_BLOCKS), 0)
    c = lax.broadcasted_iota(jnp.int32, (LANES, HEAD_BLOCKS), 1)
    out = None
    for k, part in enumerate((hi, mid, low)):
        term = _dot(part, jnp.where(c == r * LANES + (AUG_LO + k), 1.0, 0.0).astype(BF16))
        out = term if out is None else out + term
    return out


def _head_block(pair_blk, hh, lo, extras):
    src = pair_blk if hh == 0 else pltpu.roll(pair_blk, HEAD_DIM, axis=1)
    return jnp.where(lo, src, extras).astype(BF16)


def _pair_block(blk0, blk1, lo):
    return jnp.where(lo, blk0, pltpu.roll(blk1, HEAD_DIM, axis=1))


def _fwd_in(x, norm_g, wp, bf_pad, fq_g, fk_g):
    s = x.shape[0]
    t = TILE
    n = s // t

    def body(x_ref, ng_ref, wp_ref, bf_ref, qg_ref, kg_ref,
             h_ref, pa_ref, qk_ref, qa_ref, ka_ref, va_ref, gb_ref, pm_ref, fb_ref, carry_ref, fcol_ref):
        @pl.when(pl.program_id(0) == 0)
        def _():
            carry_ref[...] = jnp.zeros_like(carry_ref)

        xv = x_ref[...]
        rr = lax.rsqrt(jnp.mean(xv * xv, axis=-1, keepdims=True) + EPS)
        hb = ((xv * rr) * ng_ref[...]).astype(BF16)
        h_ref[...] = hb

        def proj(lo, hi):
            return _dot(hb, wp_ref[lo:hi, :], NT)

        pa_ref[...] = proj(PA_LO, QB_LO)
        gb_ref[...] = proj(GB_LO, PM_LO)
        pm_ref[...] = proj(PM_LO, FB_LO)
        fb = proj(FB_LO, PROJ_PAD)
        fb_ref[...] = fb

        lane = lax.broadcasted_iota(jnp.int32, (t, LANES), 1)
        row = lax.broadcasted_iota(jnp.int32, (t, LANES), 0)
        lo = lane < HEAD_DIM
        z = fb + bf_ref[...]
        lf = -(jnp.maximum(-z, 0.0) + jnp.log1p(jnp.exp(-jnp.abs(z))))
        lf = jnp.where(lane < FOX_HEADS, lf, 0.0)
        sh = 1
        while sh < t:
            lf = lf + jnp.where(row >= sh, pltpu.roll(lf, sh, axis=0), 0.0)
            sh *= 2
        fcum = lf + carry_ref[...]
        fcol_ref[...] = fcum
        carry_ref[...] = fcol_ref[t - 1:t, :]

        ones3 = _ones3(lane)
        minus_f = _spread3(-fcum)
        for seg, g_ref, out_ref, scale in ((QB_LO, qg_ref, qa_ref, ATT_SCALE), (KB_LO, kg_ref, ka_ref, 1.0)):
            raw = proj(seg, seg + FOX_WIDTH)
            qk_ref[:, seg - QB_LO:seg - QB_LO + FOX_WIDTH] = raw
            for p in range(FOX_WIDTH // LANES):
                sl = slice(p * LANES, (p + 1) * LANES)
                blk = raw[:, sl]
                normed = ((blk * _head_rms(blk, lo)) * g_ref[:, sl]) * scale
                for hh in range(2):
                    h = 2 * p + hh
                    if seg == QB_LO:
                        extras = jnp.where(lane == QUERY_SUM_LANE + h, 1.0, ones3)
                    else:
                        extras = jnp.where(lane == KEY_SUM_LANE + h, 1.0, minus_f[:, h * LANES:(h + 1) * LANES])
                    out_ref[:, h * LANES:(h + 1) * LANES] = _head_block(normed, hh, lo, extras)
        vraw = proj(VB_LO, GB_LO)
        for h in range(FOX_HEADS):
            va_ref[:, h * LANES:(h + 1) * LANES] = _head_block(vraw[:, (h // 2) * LANES:(h // 2 + 1) * LANES], h % 2, lo, ones3)

    outs = (
        jax.ShapeDtypeStruct((s, D_MODEL), BF16),
        jax.ShapeDtypeStruct((s, 512), F32),
        jax.ShapeDtypeStruct((s, 2 * FOX_WIDTH), F32),
        jax.ShapeDtypeStruct((s, HEAD_BLOCKS), BF16),
        jax.ShapeDtypeStruct((s, HEAD_BLOCKS), BF16),
        jax.ShapeDtypeStruct((s, HEAD_BLOCKS), BF16),
        jax.ShapeDtypeStruct((s, FOX_WIDTH), F32),
        jax.ShapeDtypeStruct((s, 512), F32),
        jax.ShapeDtypeStruct((s, LANES), F32),
    )
    return pl.pallas_call(
        body, name="fwd_in", grid=(n,), out_shape=outs,
        in_specs=[_rows(t, D_MODEL), _full((1, D_MODEL)), _full((PROJ_PAD, D_MODEL)), _full((1, LANES)),
                  _full((1, FOX_WIDTH)), _full((1, FOX_WIDTH))],
        out_specs=(_rows(t, D_MODEL), _rows(t, 512), _rows(t, 2 * FOX_WIDTH), _rows(t, HEAD_BLOCKS),
                   _rows(t, HEAD_BLOCKS), _rows(t, HEAD_BLOCKS), _rows(t, FOX_WIDTH), _rows(t, 512),
                   _rows(t, LANES)),
        scratch_shapes=[pltpu.VMEM((1, LANES), F32), pltpu.VMEM((t, LANES), F32)],
        compiler_params=_params(),
    )(x, norm_g, wp, bf_pad, fq_g, fk_g)


POOL_HALO = 16


def _pool_window(lane):
    return jnp.where(lane < 64, 2.0, jnp.where(lane < 128, 4.0, jnp.where(lane < 192, 8.0, 16.0)))


def _pool_pick(lane, s2, s4, s8, s16):
    return jnp.where(lane < 64, s2, jnp.where(lane < 128, s4, jnp.where(lane < 192, s8, s16)))


def _group_onehot(shape, row_is_group_lane):
    r = lax.broadcasted_iota(jnp.int32, shape, 0)
    c = lax.broadcasted_iota(jnp.int32, shape, 1)
    hit = (r % HEAD_DIM == c) if row_is_group_lane else (c % HEAD_DIM == r)
    return jnp.where(hit, 1.0, 0.0).astype(F32)


def _same_group(shape):
    r = lax.broadcasted_iota(jnp.int32, shape, 0)
    c = lax.broadcasted_iota(jnp.int32, shape, 1)
    return (r // HEAD_DIM) == (c // HEAD_DIM)


def _pool_block_diag(w4):
    spread = jnp.dot(w4, _group_onehot((HEAD_DIM, POOL_WIDTH), False), preferred_element_type=F32,
                     precision=lax.Precision.HIGHEST)
    return jnp.where(_same_group((POOL_WIDTH, POOL_WIDTH)), spread, 0.0).astype(BF16)


def _pool_fwd(pa, w4, pscale):
    s = pa.shape[0]
    t = TILE
    n = s // t
    ext = t + POOL_HALO

    def body(pa_ref, w4_ref, sc_ref, ma_ref, d_ref, ext_ref, w_ref):
        i = pl.program_id(0)

        @pl.when(i == 0)
        def _():
            ext_ref[0:POOL_HALO, :] = jnp.zeros((POOL_HALO, POOL_WIDTH), F32)
            w_ref[...] = _pool_block_diag(w4_ref[...])

        u = pa_ref[:, 0:POOL_WIDTH]
        ext_ref[POOL_HALO:ext, :] = u
        e = ext_ref[...]
        s2 = e + pltpu.roll(e, 1, axis=0)
        s4 = s2 + pltpu.roll(s2, 2, axis=0)
        s8 = s4 + pltpu.roll(s4, 4, axis=0)
        s16 = s8 + pltpu.roll(s8, 8, axis=0)
        lane_e = lax.broadcasted_iota(jnp.int32, (ext, POOL_WIDTH), 1)
        win = _pool_pick(lane_e, s2, s4, s8, s16)[POOL_HALO:ext, :]
        lane = lax.broadcasted_iota(jnp.int32, (t, POOL_WIDTH), 1)
        pos = (lax.broadcasted_iota(jnp.int32, (t, POOL_WIDTH), 0) + (i * t + 1)).astype(F32)
        d = win / jnp.minimum(pos, _pool_window(lane)) - u
        db = d.astype(BF16)
        d_ref[...] = db
        ya = _dot(db, w_ref[...]) * sc_ref[...]
        ga = pa_ref[:, POOL_WIDTH:2 * POOL_WIDTH]
        ma_ref[...] = (ya * (ga * _sig(ga))).astype(BF16)
        ext_ref[0:POOL_HALO, :] = ext_ref[t:ext, :]

    return pl.pallas_call(
        body, name="pool_fwd", grid=(n,),
        out_shape=(jax.ShapeDtypeStruct((s, POOL_WIDTH), BF16), jax.ShapeDtypeStruct((s, POOL_WIDTH), BF16)),
        in_specs=[_rows(t, 512), _full((POOL_ROWS, HEAD_DIM)), _full((1, POOL_WIDTH))],
        out_specs=(_rows(t, POOL_WIDTH), _rows(t, POOL_WIDTH)),
        scratch_shapes=[pltpu.VMEM((ext, POOL_WIDTH), F32), pltpu.VMEM((POOL_WIDTH, POOL_WIDTH), BF16)],
        compiler_params=_params(),
    )(pa, w4, pscale)


def _mem_softmax(qm, kp):
    sc = _dot(qm, kp, NT)
    e = jnp.exp(sc - jnp.max(sc, axis=-1, keepdims=True))
    return e * (1.0 / jnp.sum(e, axis=-1, keepdims=True))


def _mem_attn_fwd(pm, kmn, vmb, mq_g):
    s = pm.shape[0]
    t = TILE
    n = s // t

    def body(pm_ref, k_ref, v_ref, g_ref, mm_ref):
        lo = _lane_lo((t, LANES))
        for p in range(MEM_WIDTH // LANES):
            sl = slice(p * LANES, (p + 1) * LANES)
            qb = pm_ref[:, sl]
            qs = (((qb * _head_rms(qb, lo)) * g_ref[:, sl]) * ATT_SCALE).astype(BF16)
            kp = k_ref[:, sl]
            vp = v_ref[:, sl]
            outs = []
            for hh in range(2):
                msk = lo if hh == 0 else jnp.logical_not(lo)
                prob = _mem_softmax(jnp.where(msk, qs, jnp.zeros_like(qs)), kp)
                outs.append(_dot(prob.astype(BF16), vp))
            o = jnp.where(lo, outs[0], outs[1])
            gm = pm_ref[:, MEM_WIDTH + p * LANES:MEM_WIDTH + (p + 1) * LANES]
            mm_ref[:, sl] = (o * (gm * _sig(gm))).astype(BF16)

    return pl.pallas_call(
        body, name="mem_attn_fwd", grid=(n,),
        out_shape=jax.ShapeDtypeStruct((s, MEM_WIDTH), BF16),
        in_specs=[_rows(t, 512), _full((N_MEM, MEM_WIDTH)), _full((N_MEM, MEM_WIDTH)), _full((1, MEM_WIDTH))],
        out_specs=_rows(t, MEM_WIDTH),
        compiler_params=_params(),
    )(pm, kmn, vmb, mq_g)


def _fox_fwd(qa, ka, va, gb):
    s = qa.shape[0]
    t = TILE
    n = s // t
    pair_w = 2 * LANES

    def body(qa_ref, ka_ref, va_ref, gb_ref, o_ref, mb_ref, r_ref):
        i = pl.program_id(1)
        lane = lax.broadcasted_iota(jnp.int32, (t, LANES), 1)
        lo = lane < HEAD_DIM
        causal = lax.broadcasted_iota(jnp.int32, (t, t), 1) <= lax.broadcasted_iota(jnp.int32, (t, t), 0)
        qas = (qa_ref[:, 0:LANES], qa_ref[:, LANES:pair_w])

        def step(j, carry, masked):
            rows = pl.ds(pl.multiple_of(j * t, t), t)
            new = []
            for hh in range(2):
                cols = slice(hh * LANES, (hh + 1) * LANES)
                m, acc = carry[hh]
                sc = _dot(qas[hh], ka_ref[rows, cols], NT)
                if masked:
                    sc = jnp.where(causal, sc, -1e30)
                m_new = jnp.maximum(m, jnp.max(sc, axis=-1, keepdims=True))
                acc = jnp.exp(m - m_new) * acc + _dot(jnp.exp(sc - m_new).astype(BF16), va_ref[rows, cols])
                new.append((m_new, acc))
            return tuple(new)

        init = (jnp.full((t, 1), -1e30, F32), jnp.zeros((t, LANES), F32))
        carry = lax.fori_loop(0, i, functools.partial(step, masked=False), (init, init))
        outs = []
        rcol = jnp.zeros((t, LANES), F32)
        for hh, (m, acc) in enumerate(step(i, carry, masked=True)):
            l = _lane_pick(acc, lane, AUG_LO)
            outs.append(acc * (1.0 / l))
            rcol = jnp.where(lane == hh, m + jnp.log(l), rcol)
        o = _pair_block(outs[0], outs[1], lo)
        o_ref[...] = o
        g = gb_ref[...]
        mb_ref[...] = (o * (g * _sig(g))).astype(BF16)
        r_ref[0] = rcol

    return pl.pallas_call(
        body, name="fox_fwd", grid=(FOX_HEADS // 2, n),
        out_shape=(jax.ShapeDtypeStruct((s, FOX_WIDTH), F32), jax.ShapeDtypeStruct((s, FOX_WIDTH), BF16),
                   jax.ShapeDtypeStruct((FOX_HEADS // 2, s, LANES), F32)),
        in_specs=[pl.BlockSpec((t, pair_w), lambda p, i: (i, p)), pl.BlockSpec((s, pair_w), lambda p, i: (0, p)),
                  pl.BlockSpec((s, pair_w), lambda p, i: (0, p)), pl.BlockSpec((t, LANES), lambda p, i: (i, p))],
        out_specs=(pl.BlockSpec((t, LANES), lambda p, i: (i, p)), pl.BlockSpec((t, LANES), lambda p, i: (i, p)),
                   pl.BlockSpec((1, t, LANES), lambda p, i: (p, i, 0))),
        compiler_params=_params(2),
    )(qa, ka, va, gb)


def _out_loss(x, tgt, ma, mb, mm, wout):
    s = x.shape[0]
    t = TILE
    n = s // t

    def body(x_ref, t_ref, ma_ref, mb_ref, mm_ref, w_ref, dy_ref, dma_ref, dmb_ref, dmm_ref, dw_ref, loss_ref, mix_ref):
        @pl.when(pl.program_id(0) == 0)
        def _():
            dw_ref[...] = jnp.zeros_like(dw_ref)
            loss_ref[...] = jnp.zeros_like(loss_ref)

        mix_ref[:, 0:256] = ma_ref[...]
        mix_ref[:, 256:768] = mb_ref[...]
        mix_ref[:, 768:1024] = mm_ref[...]
        mix = mix_ref[...]
        err = (x_ref[...] + _dot(mix, w_ref[...])) - t_ref[...]
        row_mean = jnp.sum(err * err, axis=-1, keepdims=True) * (1.0 / D_MODEL)
        loss_ref[...] += 0.5 * jnp.sum(row_mean, axis=0, keepdims=True)
        dy = err * (1.0 / D_MODEL)
        dy_ref[...] = dy
        dyb = dy.astype(BF16)
        dmix = _dot(dyb, w_ref[...], NT)
        dma_ref[...] = dmix[:, 0:256]
        dmb_ref[...] = dmix[:, 256:768]
        dmm_ref[...] = dmix[:, 768:1024]
        dw_ref[...] += _dot(mix, dyb, TN)

    return pl.pallas_call(
        body, name="out_loss", grid=(n,),
        out_shape=(jax.ShapeDtypeStruct((s, D_MODEL), F32), jax.ShapeDtypeStruct((s, 256), F32),
                   jax.ShapeDtypeStruct((s, 512), F32), jax.ShapeDtypeStruct((s, 256), F32),
                   jax.ShapeDtypeStruct((D_MODEL, D_MODEL), F32), jax.ShapeDtypeStruct((1, LANES), F32)),
        in_specs=[_rows(t, D_MODEL), _rows(t, D_MODEL), _rows(t, 256), _rows(t, 512), _rows(t, 256),
                  _full((D_MODEL, D_MODEL))],
        out_specs=(_rows(t, D_MODEL), _rows(t, 256), _rows(t, 512), _rows(t, 256), _full((D_MODEL, D_MODEL)),
                   _full((1, LANES))),
        scratch_shapes=[pltpu.VMEM((t, D_MODEL), BF16)],
        compiler_params=_params(),
    )(x, tgt, ma, mb, mm, wout)


def _mem_attn_bwd(pm, dmm, kmn, vmb, mq_g):
    s = pm.shape[0]
    t = TILE
    n = s // t

    def body(pm_ref, dmm_ref, k_ref, v_ref, g_ref, dpm_ref, dk_ref, dv_ref, dg_ref, gacc_ref):
        @pl.when(pl.program_id(0) == 0)
        def _():
            dk_ref[...] = jnp.zeros_like(dk_ref)
            dv_ref[...] = jnp.zeros_like(dv_ref)
            gacc_ref[...] = jnp.zeros_like(gacc_ref)

        lo = _lane_lo((t, LANES))
        for p in range(MEM_WIDTH // LANES):
            sl = slice(p * LANES, (p + 1) * LANES)
            qb = pm_ref[:, sl]
            rr = _head_rms(qb, lo)
            qhat = qb * rr
            g = g_ref[:, sl]
            qs = ((qhat * g) * ATT_SCALE).astype(BF16)
            gm = pm_ref[:, MEM_WIDTH + p * LANES:MEM_WIDTH + (p + 1) * LANES]
            sg = _sig(gm)
            dmo = dmm_ref[:, sl]
            d_o = dmo * (gm * sg)
            kp = k_ref[:, sl]
            vp = v_ref[:, sl]
            outs, dqs = [], []
            for hh in range(2):
                msk = lo if hh == 0 else jnp.logical_not(lo)
                qm = jnp.where(msk, qs, jnp.zeros_like(qs))
                prob = _mem_softmax(qm, kp)
                pb = prob.astype(BF16)
                outs.append(_dot(pb, vp))
                dom = jnp.where(msk, d_o, 0.0).astype(BF16)
                dp = _dot(dom, vp, NT)
                ds = (prob * (dp - jnp.sum(prob * dp, axis=-1, keepdims=True))).astype(BF16)
                dqs.append(_dot(ds, kp))
                dk_ref[:, sl] += _dot(ds, qm, TN)
                dv_ref[:, sl] += _dot(pb, dom, TN)
            o = jnp.where(lo, outs[0], outs[1])
            dqn = jnp.where(lo, dqs[0], dqs[1]) * ATT_SCALE
            dpm_ref[:, sl] = _head_norm_bwd(dqn, qhat, rr, g, lo).astype(BF16)
            dpm_ref[:, MEM_WIDTH + p * LANES:MEM_WIDTH + (p + 1) * LANES] = (
                dmo * o * (sg * (1.0 + gm * (1.0 - sg)))).astype(BF16)
            gacc_ref[:, sl] += jnp.sum(dqn * qhat, axis=0, keepdims=True)

        @pl.when(pl.program_id(0) == n - 1)
        def _():
            dg_ref[...] = _fold_heads(gacc_ref[...])

    return pl.pallas_call(
        body, name="mem_attn_bwd", grid=(n,),
        out_shape=(jax.ShapeDtypeStruct((s, 512), BF16), jax.ShapeDtypeStruct((N_MEM, MEM_WIDTH), F32),
                   jax.ShapeDtypeStruct((N_MEM, MEM_WIDTH), F32), jax.ShapeDtypeStruct((1, LANES), F32)),
        in_specs=[_rows(t, 512), _rows(t, MEM_WIDTH), _full((N_MEM, MEM_WIDTH)), _full((N_MEM, MEM_WIDTH)),
                  _full((1, MEM_WIDTH))],
        out_specs=(_rows(t, 512), _full((N_MEM, MEM_WIDTH)), _full((N_MEM, MEM_WIDTH)), _full((1, LANES))),
        scratch_shapes=[pltpu.VMEM((1, MEM_WIDTH), F32)],
        compiler_params=_params(),
    )(pm, dmm, kmn, vmb, mq_g)


def _mem_bwd(dkn, dvm, kv, mnb, mem, w_kv, mk_g, mem_norm_g):
    n = mem.shape[0]

    def body(dkn_ref, dvm_ref, kv_ref, mn_ref, mem_ref, w_ref, kg_ref, g_ref, dw_ref, dg_ref, dkg_ref, dkv_ref):
        lo = _lane_lo((n, LANES))
        gacc = []
        for p in range(MEM_WIDTH // LANES):
            sl = slice(p * LANES, (p + 1) * LANES)
            kb = kv_ref[:, sl]
            rr = _head_rms(kb, lo)
            khat = kb * rr
            dk = dkn_ref[:, sl]
            dkv_ref[:, sl] = _head_norm_bwd(dk, khat, rr, kg_ref[:, sl], lo).astype(BF16)
            gacc.append(jnp.sum(dk * khat, axis=0, keepdims=True))
        dkg_ref[...] = _fold_heads(jnp.concatenate(gacc, axis=1))
        dkv_ref[:, MEM_WIDTH:] = dvm_ref[...].astype(BF16)
        dkv = dkv_ref[...]
        dw_ref[...] = _dot(mn_ref[...], dkv, TN)
        dmn = _dot(dkv, w_ref[...], NT)
        xm = mem_ref[...]
        rr = lax.rsqrt(jnp.mean(xm * xm, axis=-1, keepdims=True) + EPS)
        dg_ref[...] = jnp.sum(dmn * (xm * rr), axis=0, keepdims=True)

    return pl.pallas_call(
        body, name="mem_bwd",
        out_shape=(jax.ShapeDtypeStruct((D_MODEL, 2 * MEM_WIDTH), F32), jax.ShapeDtypeStruct((1, D_MODEL), F32),
                   jax.ShapeDtypeStruct((1, LANES), F32)),
        scratch_shapes=[pltpu.VMEM((n, 2 * MEM_WIDTH), BF16)],
        compiler_params=pltpu.CompilerParams(vmem_limit_bytes=VMEM_LIMIT),
    )(dkn, dvm, kv, mnb, mem, w_kv, mk_g, mem_norm_g)


def _pool_bwd(pa, db, dma, w4, pscale):
    s = pa.shape[0]
    t = TILE
    n = s // t
    ext = t + POOL_HALO

    def body(pa_ref, d_ref, dma_ref, w4_ref, sc_ref, dpa_ref, dw4_ref, dsc_ref, ext_ref, w_ref, dw_ref):
        i = pl.program_id(0)

        @pl.when(i == 0)
        def _():
            dw_ref[...] = jnp.zeros_like(dw_ref)
            dsc_ref[...] = jnp.zeros_like(dsc_ref)
            ext_ref[t:ext, :] = jnp.zeros((POOL_HALO, POOL_WIDTH), F32)
            w_ref[...] = _pool_block_diag(w4_ref[...])

        dbv = d_ref[...]
        z = _dot(dbv, w_ref[...])
        ga = pa_ref[:, POOL_WIDTH:2 * POOL_WIDTH]
        sg = _sig(ga)
        dma_v = dma_ref[...]
        dya = dma_v * (ga * sg)
        dpa_ref[:, POOL_WIDTH:2 * POOL_WIDTH] = (dma_v * (z * sc_ref[...]) * (sg * (1.0 + ga * (1.0 - sg)))).astype(BF16)
        dsc_ref[...] += jnp.sum(dya * z, axis=0, keepdims=True)
        dzb = (dya * sc_ref[...]).astype(BF16)
        dw_ref[...] += _dot(dbv, dzb, TN)
        dd = _dot(dzb, w_ref[...], NT)
        lane = lax.broadcasted_iota(jnp.int32, (t, POOL_WIDTH), 1)
        pos = (lax.broadcasted_iota(jnp.int32, (t, POOL_WIDTH), 0) + ((n - 1 - i) * t + 1)).astype(F32)
        ext_ref[0:t, :] = dd / jnp.minimum(pos, _pool_window(lane))
        e = ext_ref[...]
        s2 = e + pltpu.roll(e, ext - 1, axis=0)
        s4 = s2 + pltpu.roll(s2, ext - 2, axis=0)
        s8 = s4 + pltpu.roll(s4, ext - 4, axis=0)
        s16 = s8 + pltpu.roll(s8, ext - 8, axis=0)
        lane_e = lax.broadcasted_iota(jnp.int32, (ext, POOL_WIDTH), 1)
        win = _pool_pick(lane_e, s2, s4, s8, s16)[0:t, :]
        dpa_ref[:, 0:POOL_WIDTH] = (win - dd).astype(BF16)
        ext_ref[t:ext, :] = ext_ref[0:POOL_HALO, :]

        @pl.when(i == n - 1)
        def _():
            own = jnp.where(_same_group((POOL_WIDTH, POOL_WIDTH)), dw_ref[...], 0.0)
            dw4_ref[...] = jnp.dot(own, _group_onehot((POOL_WIDTH, HEAD_DIM), True), preferred_element_type=F32,
                                   precision=lax.Precision.HIGHEST)

    return pl.pallas_call(
        body, name="pool_bwd", grid=(n,),
        out_shape=(jax.ShapeDtypeStruct((s, 512), BF16), jax.ShapeDtypeStruct((POOL_ROWS, HEAD_DIM), F32),
                   jax.ShapeDtypeStruct((1, POOL_WIDTH), F32)),
        in_specs=[_rows_rev(t, 512, n), _rows_rev(t, POOL_WIDTH, n), _rows_rev(t, POOL_WIDTH, n),
                  _full((POOL_ROWS, HEAD_DIM)), _full((1, POOL_WIDTH))],
        out_specs=(_rows_rev(t, 512, n), _full((POOL_ROWS, HEAD_DIM)), _full((1, POOL_WIDTH))),
        scratch_shapes=[pltpu.VMEM((ext, POOL_WIDTH), F32), pltpu.VMEM((POOL_WIDTH, POOL_WIDTH), BF16),
                        pltpu.VMEM((POOL_WIDTH, POOL_WIDTH), F32)],
        compiler_params=_params(),
    )(pa, db, dma, w4, pscale)


def _fox_prep(dmb, gb, o, r4):
    s = dmb.shape[0]
    t = TILE
    n = s // t
    pairs = FOX_HEADS // 2

    def body(dmb_ref, gb_ref, o_ref, r_ref, doa_ref, dgb_ref, rr_ref):
        lane = lax.broadcasted_iota(jnp.int32, (t, LANES), 1)
        lo = lane < HEAD_DIM
        d_os = []
        delta = jnp.zeros((t, LANES), F32)
        for p in range(pairs):
            sl = slice(p * LANES, (p + 1) * LANES)
            g = gb_ref[:, sl]
            sg = _sig(g)
            dm = dmb_ref[:, sl]
            ov = o_ref[:, sl]
            d_o = dm * (g * sg)
            d_os.append(d_o)
            dgb_ref[:, sl] = (dm * ov * (sg * (1.0 + g * (1.0 - sg)))).astype(BF16)
            prod = d_o * ov
            delta = jnp.where(lane == 2 * p, jnp.sum(jnp.where(lo, prod, 0.0), axis=-1, keepdims=True), delta)
            delta = jnp.where(lane == 2 * p + 1, jnp.sum(jnp.where(lo, 0.0, prod), axis=-1, keepdims=True), delta)
            rr_ref[p, 0] = r_ref[p].T[0:8, :]
        minus_delta = _spread3(-delta)
        for h in range(FOX_HEADS):
            blk = slice(h * LANES, (h + 1) * LANES)
            doa_ref[:, blk] = _head_block(d_os[h // 2], h % 2, lo, minus_delta[:, blk])

    return pl.pallas_call(
        body, name="fox_prep", grid=(n,),
        out_shape=(jax.ShapeDtypeStruct((s, HEAD_BLOCKS), BF16), jax.ShapeDtypeStruct((s, FOX_WIDTH), BF16),
                   jax.ShapeDtypeStruct((pairs, n, 8, t), F32)),
        in_specs=[_rows(t, FOX_WIDTH), _rows(t, FOX_WIDTH), _rows(t, FOX_WIDTH),
                  pl.BlockSpec((pairs, t, LANES), lambda i: (0, i, 0))],
        out_specs=(_rows(t, HEAD_BLOCKS), _rows(t, FOX_WIDTH), pl.BlockSpec((pairs, 1, 8, t), lambda i: (0, i, 0, 0))),
        compiler_params=_params(),
    )(dmb, gb, o, r4)


def _fox_bwd(ka, va, qa, doa, rr):
    s = ka.shape[0]
    t = TILE
    n = s // t
    pair_w = 2 * LANES

    def body(ka_ref, va_ref, qa_ref, doa_ref, rr_ref, dka_ref, dva_ref, dqa_ref):
        j = pl.program_id(1)

        @pl.when(j == 0)
        def _():
            dqa_ref[...] = jnp.zeros_like(dqa_ref)

        causal = lax.broadcasted_iota(jnp.int32, (t, t), 0) <= lax.broadcasted_iota(jnp.int32, (t, t), 1)
        kas = (ka_ref[:, 0:LANES], ka_ref[:, LANES:pair_w])
        vas = (va_ref[:, 0:LANES], va_ref[:, LANES:pair_w])

        def step(i, carry, masked):
            rows = pl.ds(pl.multiple_of(i * t, t), t)
            new = []
            for hh in range(2):
                cols = slice(hh * LANES, (hh + 1) * LANES)
                dk_a, dv_a = carry[hh]
                qb = qa_ref[rows, cols]
                d_o = doa_ref[rows, cols]
                arg = _dot(kas[hh], qb, NT) - rr_ref[0, i, hh:hh + 1, :]
                if masked:
                    arg = jnp.where(causal, arg, -1e30)
                pt = jnp.exp(arg)
                dst = (pt * _dot(vas[hh], d_o, NT)).astype(BF16)
                dv_a = dv_a + _dot(pt.astype(BF16), d_o)
                dk_a = dk_a + _dot(dst, qb)
                dqa_ref[rows, cols] += _dot(dst, kas[hh], TN)
                new.append((dk_a, dv_a))
            return tuple(new)

        zero = jnp.zeros((t, LANES), F32)
        carry = step(j, ((zero, zero), (zero, zero)), masked=True)
        res = lax.fori_loop(j + 1, n, functools.partial(step, masked=False), carry)
        for hh in range(2):
            cols = slice(hh * LANES, (hh + 1) * LANES)
            dka_ref[:, cols] = res[hh][0]
            dva_ref[:, cols] = res[hh][1]

    tile_spec = pl.BlockSpec((t, pair_w), lambda p, j: (j, p))
    full_spec = pl.BlockSpec((s, pair_w), lambda p, j: (0, p))
    return pl.pallas_call(
        body, name="fox_bwd", grid=(FOX_HEADS // 2, n),
        out_shape=(jax.ShapeDtypeStruct((s, HEAD_BLOCKS), F32),) * 3,
        in_specs=[tile_spec, tile_spec, full_spec, full_spec,
                  pl.BlockSpec((1, n, 8, t), lambda p, j: (p, 0, 0, 0))],
        out_specs=(tile_spec, tile_spec, full_spec),
        compiler_params=_params(2),
    )(ka, va, qa, doa, rr)


def _fox_post(dqa, dka, dva, qk, fb, bf_pad, fq_g, fk_g):
    s = dqa.shape[0]
    t = TILE
    n = s // t

    def body(dqa_ref, dka_ref, dva_ref, qk_ref, fb_ref, bf_ref, qg_ref, kg_ref,
             dqk_ref, dv_ref, dfb_ref, dqg_ref, dkg_ref, dbf_ref, qacc_ref, kacc_ref, carry_ref):
        i = pl.program_id(0)

        @pl.when(i == 0)
        def _():
            qacc_ref[...] = jnp.zeros_like(qacc_ref)
            kacc_ref[...] = jnp.zeros_like(kacc_ref)
            dbf_ref[...] = jnp.zeros_like(dbf_ref)
            carry_ref[...] = jnp.zeros_like(carry_ref)

        lane = lax.broadcasted_iota(jnp.int32, (t, LANES), 1)
        row = lax.broadcasted_iota(jnp.int32, (t, LANES), 0)
        lo = lane < HEAD_DIM

        def head_blocks(ref, p):
            return ref[:, 2 * p * LANES:(2 * p + 1) * LANES], ref[:, (2 * p + 1) * LANES:(2 * p + 2) * LANES]

        dq_sum = jnp.zeros((t, LANES), F32)
        dk_sum = jnp.zeros((t, LANES), F32)
        for p in range(FOX_WIDTH // LANES):
            sl = slice(p * LANES, (p + 1) * LANES)
            dq0, dq1 = head_blocks(dqa_ref, p)
            dk0, dk1 = head_blocks(dka_ref, p)
            dv0, dv1 = head_blocks(dva_ref, p)
            dv_ref[:, sl] = _pair_block(dv0, dv1, lo).astype(BF16)
            dq_sum = dq_sum + (dq0 + dq1)
            dk_sum = dk_sum + (dk0 + dk1)
            for off, pair, g_ref, acc_ref, scale in ((0, _pair_block(dq0, dq1, lo), qg_ref, qacc_ref, ATT_SCALE),
                                                     (FOX_WIDTH, _pair_block(dk0, dk1, lo), kg_ref, kacc_ref, 1.0)):
                raw = qk_ref[:, off + p * LANES:off + (p + 1) * LANES]
                rr = _head_rms(raw, lo)
                xhat = raw * rr
                dn = pair * scale
                dqk_ref[:, off + p * LANES:off + (p + 1) * LANES] = _head_norm_bwd(
                    dn, xhat, rr, g_ref[:, sl], lo).astype(BF16)
                acc_ref[:, sl] += jnp.sum(dn * xhat, axis=0, keepdims=True)

        acc = (pltpu.roll(dq_sum, LANES - KEY_SUM_LANE, axis=1) - pltpu.roll(dk_sum, LANES - QUERY_SUM_LANE, axis=1))
        acc = jnp.where(lane < FOX_HEADS, acc, 0.0)
        sh = 1
        while sh < t:
            acc = acc + jnp.where(row < t - sh, pltpu.roll(acc, t - sh, axis=0), 0.0)
            sh *= 2
        dlogf = acc + carry_ref[...]
        dfb_ref[...] = dlogf
        carry_ref[...] = dfb_ref[0:1, :]
        z = fb_ref[...] + bf_ref[...]
        dz = jnp.where(lane < FOX_HEADS, dlogf * (1.0 / (1.0 + jnp.exp(z))), 0.0)
        dfb_ref[...] = dz
        dbf_ref[...] += jnp.sum(dz, axis=0, keepdims=True)

        @pl.when(i == n - 1)
        def _():
            dqg_ref[...] = _fold_heads(qacc_ref[...])
            dkg_ref[...] = _fold_heads(kacc_ref[...])

    return pl.pallas_call(
        body, name="fox_post", grid=(n,),
        out_shape=(jax.ShapeDtypeStruct((s, 2 * FOX_WIDTH), BF16), jax.ShapeDtypeStruct((s, FOX_WIDTH), BF16),
                   jax.ShapeDtypeStruct((s, LANES), F32), jax.ShapeDtypeStruct((1, LANES), F32),
                   jax.ShapeDtypeStruct((1, LANES), F32), jax.ShapeDtypeStruct((1, LANES), F32)),
        in_specs=[_rows_rev(t, HEAD_BLOCKS, n), _rows_rev(t, HEAD_BLOCKS, n), _rows_rev(t, HEAD_BLOCKS, n),
                  _rows_rev(t, 2 * FOX_WIDTH, n), _rows_rev(t, LANES, n), _full((1, LANES)),
                  _full((1, FOX_WIDTH)), _full((1, FOX_WIDTH))],
        out_specs=(_rows_rev(t, 2 * FOX_WIDTH, n), _rows_rev(t, FOX_WIDTH, n), _rows_rev(t, LANES, n),
                   _full((1, LANES)), _full((1, LANES)), _full((1, LANES))),
        scratch_shapes=[pltpu.VMEM((1, FOX_WIDTH), F32), pltpu.VMEM((1, FOX_WIDTH), F32), pltpu.VMEM((1, LANES), F32)],
        compiler_params=_params(),
    )(dqa, dka, dva, qk, fb, bf_pad, fq_g, fk_g)


def _assemble_dproj(dp_ref, dpa_ref, dqk_ref, dv_ref, dgb_ref, dpm_ref, dfb_ref):
    dp_ref[:, PA_LO:QB_LO] = dpa_ref[...]
    dp_ref[:, QB_LO:VB_LO] = dqk_ref[...]
    dp_ref[:, VB_LO:GB_LO] = dv_ref[...]
    dp_ref[:, GB_LO:PM_LO] = dgb_ref[...]
    dp_ref[:, PM_LO:FB_LO] = dpm_ref[...]
    dp_ref[:, FB_LO:PROJ_PAD] = dfb_ref[...].astype(BF16)


def _dproj_specs(t):
    return [_rows(t, 512), _rows(t, 2 * FOX_WIDTH), _rows(t, FOX_WIDTH), _rows(t, FOX_WIDTH), _rows(t, 512),
            _rows(t, LANES)]


IN_BWD_X_TILE = 256


def _in_bwd_x(x, dy, norm_g, wp, dparts, gparts, axes, smalls):
    s = x.shape[0]
    t = IN_BWD_X_TILE
    n = s // t
    na = len(gparts)
    n_dp = len(dparts)
    send_step = 4
    vec_leaves, loss_row, dw4 = smalls if smalls is not None else ((), None, None)
    nv = len(vec_leaves)
    n_small = nv + 2 if smalls is not None else 0

    def body(*refs):
        x_ref, dy_ref, g_ref, wp_ref = refs[0:4]
        dp_parts = refs[4:4 + n_dp]
        o = 4 + n_dp
        g_refs = refs[o:o + na]
        small_in = refs[o + na:o + na + n_small]
        o += na + n_small
        gx_ref, dg_ref = refs[o:o + 2]
        out_refs = refs[o + 2:o + 2 + na]
        small_out = refs[o + 2 + na:o + 2 + na + (2 if smalls is not None else 0)]
        o += 2 + na + len(small_out)
        dp_ref = refs[o]
        bufs = tuple(refs[o + 1 + k * na:o + 1 + (k + 1) * na] for k in range(5))
        rest = refs[o + 1 + 5 * na:]

        i = pl.program_id(0)
        if na or smalls is not None:
            send_sems, recv_sems, local_sems = rest[-3:]
        red = _ShardReduce(g_refs, out_refs, axes, bufs, send_sems, recv_sems, local_sems) if na else None

        @pl.when(i == 0)
        def _():
            dg_ref[...] = jnp.zeros_like(dg_ref)
            if red is not None:
                red.exchange_with_sibling()

        if red is not None:
            pl.when(i == send_step)(red.send_to_chips)

        _assemble_dproj(dp_ref, *dp_parts)
        dh = _dot(dp_ref[...], wp_ref[...])
        xv = x_ref[...]
        rr = lax.rsqrt(jnp.mean(xv * xv, axis=-1, keepdims=True) + EPS)
        xhat = xv * rr
        scaled = dh * g_ref[...]
        gx_ref[...] = dy_ref[...] + rr * (scaled - xhat * jnp.mean(xhat * scaled, axis=-1, keepdims=True))
        dg_ref[...] += jnp.sum(dh * xhat, axis=0, keepdims=True)

        def small_all_reduce():
            leaf_refs, (loss_ref, dw4_ref) = small_in[0:nv], small_in[nv:]
            vec_out, dw4_out = small_out
            vec_mine, vec_recv, dw4_recv = rest[0:3]
            cx, cy, c = _my_place()
            me_lin = 4 * cx + 2 * cy + c

            def copy(k, src, dst, base):
                peer = (me_lin + k) % 8
                return pltpu.make_async_remote_copy(
                    src_ref=src, dst_ref=dst.at[me_lin], send_sem=send_sems.at[base + k - 1],
                    recv_sem=recv_sems.at[base + k - 1], device_id=(peer // 4, (peer // 2) % 2, peer % 2),
                    device_id_type=MESH)

            vec_mine[...] = jnp.zeros_like(vec_mine)
            vec_mine[0:1, :] = dg_ref[...]
            for (_, row, _), ref in zip(VEC_LEAVES[1:], leaf_refs):
                vec_mine[row:row + 1, 0:ref.shape[1]] = ref[...]
            vec_mine[VEC_LOSS_ROW:VEC_LOSS_ROW + 1, 0:LANES] = loss_ref[...]
            copies = [copy(k, src, dst, base) for k in range(1, 8)
                      for src, dst, base in ((vec_mine, vec_recv, 5 * na), (dw4_ref, dw4_recv, 5 * na + 7))]
            for cp in copies:
                cp.start()
            for cp in copies:
                cp.wait_recv()
            vec_recv[me_lin] = vec_mine[...]
            dw4_recv[me_lin] = dw4_ref[...]
            vtot, wtot = vec_recv[0], dw4_recv[0]
            for d in range(1, 8):
                vtot = vtot + vec_recv[d]
                wtot = wtot + dw4_recv[d]
            vec_out[...] = vtot
            dw4_out[...] = wtot
            for cp in copies:
                cp.wait_send()

        @pl.when(i == n - 1)
        def _():
            if red is not None:
                red.sum_and_share()
            if smalls is not None:
                small_all_reduce()
            if red is not None:
                red.finish()

    any_spec = pl.BlockSpec(memory_space=pl.ANY)
    scratch = [pltpu.VMEM((t, PROJ_PAD), BF16)] + _ShardReduce.scratch(gparts, axes)
    out_shape = [jax.ShapeDtypeStruct((s, D_MODEL), F32), jax.ShapeDtypeStruct((1, D_MODEL), F32)]
    out_shape += [jax.ShapeDtypeStruct(g.shape[1:], F32) for g in gparts]
    out_specs = [_rows(t, D_MODEL), _full((1, D_MODEL))] + [any_spec] * na
    small_args = []
    if smalls is not None:
        small_args = [*vec_leaves, loss_row, dw4]
        out_shape += [jax.ShapeDtypeStruct((VEC_ROWS, D_MODEL), F32), jax.ShapeDtypeStruct(dw4.shape, F32)]
        out_specs += [_full((VEC_ROWS, D_MODEL)), _full(dw4.shape)]
        scratch += [pltpu.VMEM((VEC_ROWS, D_MODEL), F32), pltpu.VMEM((8, VEC_ROWS, D_MODEL), F32),
                    pltpu.VMEM((8,) + dw4.shape, F32)]
    if na or smalls is not None:
        n_sems = 5 * na + 14
        scratch += [pltpu.SemaphoreType.DMA((n_sems,)), pltpu.SemaphoreType.DMA((n_sems,)),
                    pltpu.SemaphoreType.DMA((max(2 * na, 1),))]
    return pl.pallas_call(
        body, name="in_bwd_x", grid=(n,), out_shape=tuple(out_shape),
        in_specs=[_rows(t, D_MODEL), _rows(t, D_MODEL), _full((1, D_MODEL)),
                  pl.BlockSpec((PROJ_PAD, D_MODEL), lambda i: (0, 0), pipeline_mode=pl.Buffered(1))]
        + _dproj_specs(t) + [any_spec] * na + [_full(a.shape) for a in small_args],
        out_specs=tuple(out_specs), scratch_shapes=scratch, compiler_params=_params(),
    )(x, dy, norm_g, wp, *dparts, *gparts, *small_args)


def _in_bwd_w(hb, dparts, gparts, axes):
    s = hb.shape[0]
    t = TILE
    n = s // t
    na = len(gparts)
    f_hi = F_ORIG_LO + FOX_HEADS
    send_step = 1

    def body(*refs):
        h_ref, dpa_ref, dqk_ref, dv_ref, dgb_ref, dpm_ref, dfb_ref = refs[0:7]
        g_refs = refs[7:7 + na]
        dw_ref = refs[7 + na]
        out_refs = refs[8 + na:8 + 2 * na]
        o = 8 + 2 * na
        bufs = tuple(refs[o + k * na:o + (k + 1) * na] for k in range(5))
        i = pl.program_id(0)
        red = _ShardReduce(g_refs, out_refs, axes, bufs, *refs[o + 5 * na:]) if na else None

        @pl.when(i == 0)
        def _():
            dw_ref[...] = jnp.zeros_like(dw_ref)
            if red is not None:
                red.exchange_with_sibling()

        if red is not None:
            pl.when(i == send_step)(red.send_to_chips)

        hv = h_ref[...]
        for lo, ref in ((0, dpa_ref), (QB_LO, dqk_ref), (VB_LO, dv_ref), (f_hi, dgb_ref), (f_hi + FOX_WIDTH, dpm_ref)):
            dw_ref[lo:lo + ref.shape[1], :] += _dot(ref[...], hv, TN)
        dw_ref[F_ORIG_LO:f_hi, :] += _dot(dfb_ref[...].astype(BF16), hv, TN)[0:FOX_HEADS, :]

        if red is not None:
            @pl.when(i == n - 1)
            def _():
                red.sum_and_share()
                red.finish()

    any_spec = pl.BlockSpec(memory_space=pl.ANY)
    scratch = _ShardReduce.scratch(gparts, axes)
    if na:
        scratch += [pltpu.SemaphoreType.DMA((5 * na,)), pltpu.SemaphoreType.DMA((5 * na,)),
                    pltpu.SemaphoreType.DMA((2 * na,))]
    return pl.pallas_call(
        body, name="in_bwd_w", grid=(n,),
        out_shape=(jax.ShapeDtypeStruct((IN_WIDTH, D_MODEL), F32),)
        + tuple(jax.ShapeDtypeStruct(g.shape[1:], F32) for g in gparts),
        in_specs=[_rows(t, D_MODEL)] + _dproj_specs(t) + [any_spec] * na,
        out_specs=(_full((IN_WIDTH, D_MODEL)),) + (any_spec,) * na,
        scratch_shapes=scratch, compiler_params=_params(),
    )(hb, *dparts, *gparts)


def _adamw_math(w_ref, gv, m_ref, v_ref, d_ref, nm_ref, nv_ref):
    nm = ADAM_B1 * m_ref[...] + (1.0 - ADAM_B1) * gv
    nv = ADAM_B2 * v_ref[...] + (1.0 - ADAM_B2) * (gv * gv)
    m_hat = nm / (1.0 - ADAM_B1 ** ADAM_STEP)
    v_hat = nv / (1.0 - ADAM_B2 ** ADAM_STEP)
    d_ref[...] = -ADAM_LR * (m_hat / (jnp.sqrt(v_hat) + ADAM_EPS) + ADAM_WD * w_ref[...])
    nm_ref[...] = nm
    nv_ref[...] = nv


def _adamw(name, w, g, m, v):
    rows, cols = w.shape
    tc = 256 if rows * cols > 256 * 1024 else cols
    n = cols // tc

    def body(w_ref, g_ref, m_ref, v_ref, d_ref, nm_ref, nv_ref):
        _adamw_math(w_ref, g_ref[...], m_ref, v_ref, d_ref, nm_ref, nv_ref)

    spec = pl.BlockSpec((rows, tc), lambda i: (0, i))
    return pl.pallas_call(
        body, name=name, grid=(n,),
        out_shape=(jax.ShapeDtypeStruct((rows, cols), F32),) * 3,
        in_specs=[spec] * 4, out_specs=(spec,) * 3,
        compiler_params=_params(),
    )(w, g, m, v)


def _adamw_small(vec, dw4, leaves, pool):
    nl = len(VEC_LEAVES) + 1

    def body(*refs):
        vec_ref, dw4_ref = refs[0:2]
        wmv = refs[2:2 + 3 * nl]
        loss_ref = refs[2 + 3 * nl]
        outs = refs[3 + 3 * nl:]
        loss_ref[...] = vec_ref[VEC_LOSS_ROW:VEC_LOSS_ROW + 1, 0:1]
        for k in range(nl):
            if k < nl - 1:
                _, row, width = VEC_LEAVES[k]
                gv = vec_ref[row:row + 1, 0:width]
            else:
                gv = dw4_ref[...]
            w_ref, m_ref, v_ref = wmv[3 * k:3 * k + 3]
            g_ref, d_ref, nm_ref, nv_ref = outs[4 * k:4 * k + 4]
            g_ref[...] = gv
            _adamw_math(w_ref, gv, m_ref, v_ref, d_ref, nm_ref, nv_ref)

    shapes = [jax.ShapeDtypeStruct((1, width), F32) for _, _, width in VEC_LEAVES] + [
        jax.ShapeDtypeStruct(dw4.shape, F32)]
    flat_in = [a for triple in list(leaves) + [pool] for a in triple]
    res = pl.pallas_call(
        body, name="adamw_small",
        out_shape=(jax.ShapeDtypeStruct((1, 1), F32),) + tuple(s for s in shapes for _ in range(4)),
        compiler_params=pltpu.CompilerParams(vmem_limit_bytes=VMEM_LIMIT),
    )(vec, dw4, *flat_in)
    per = [res[1 + 4 * k:5 + 4 * k] for k in range(nl)]
    return res[0], [p[0] for p in per], [p[1] for p in per], [p[2] for p in per], [p[3] for p in per]


def _full_w_in_padded(halves):
    cols = IN_WIDTH // 4
    w_t = halves.reshape(4, 2, cols, D_MODEL // 2).transpose(0, 2, 1, 3).reshape(IN_WIDTH, D_MODEL)
    return jnp.concatenate([
        w_t[0:F_ORIG_LO], w_t[F_ORIG_LO + FOX_HEADS:], w_t[F_ORIG_LO:F_ORIG_LO + FOX_HEADS],
        jnp.zeros((PROJ_PAD - IN_WIDTH, D_MODEL), w_t.dtype)], axis=0)


def _tile_heads(g, n):
    return jnp.tile(g.reshape(1, HEAD_DIM), (1, n))


def kernel(x, mem, norm_g, w_in, b_f, w_pool, pool_scale, fox_q_g, fox_k_g, mem_norm_g, w_mem_kv, mem_q_g, mem_k_g, w_out, loss_target, m_norm_g, m_w_in, m_b_f, m_w_pool, m_pool_scale, m_fox_q_g, m_fox_k_g, m_mem_norm_g, m_w_mem_kv, m_mem_q_g, m_mem_k_g, m_w_out, v_norm_g, v_w_in, v_b_f, v_w_pool, v_pool_scale, v_fox_q_g, v_fox_k_g, v_mem_norm_g, v_w_mem_kv, v_mem_q_g, v_mem_k_g, v_w_out):
    w_in_t, m_w_in_t, v_w_in_t = w_in[0].T, m_w_in[0].T, v_w_in[0].T
    axes = (1, 0, 0)

    g_in, g_kv, g_out = _all_gather_weights([w_in_t, w_mem_kv[0], w_out[0]], axes)
    w_kv_b = g_kv.reshape(D_MODEL, 2 * MEM_WIDTH)
    w_out_b = g_out.reshape(D_MODEL, D_MODEL)
    w4 = w_pool.reshape(POOL_ROWS, HEAD_DIM)
    dy, wp, hb, dparts, dw_kv, dw_out, vec_leaves, loss_row, dw4 = _local_partials(
        x[0], mem[0], loss_target[0], g_in, w_kv_b, w_out_b, norm_g, b_f, w4, pool_scale, fox_q_g, fox_k_g,
        mem_norm_g, mem_q_g, mem_k_g)

    early = [dw_kv.reshape(4, D_MODEL // 4, 2 * MEM_WIDTH), dw_out.reshape(4, D_MODEL // 4, D_MODEL)]
    dwp, g_w_kv, g_w_out = _in_bwd_w(hb, dparts, early, axes[1:])
    grad_x, _, g_w_in_t, vec, dw4_sum = _in_bwd_x(
        x[0], dy, norm_g, wp, dparts, [dwp.reshape(4, IN_WIDTH // 4, D_MODEL)], axes[0:1], (vec_leaves, loss_row, dw4))

    small_wmv = [(norm_g, m_norm_g, v_norm_g), (mem_norm_g, m_mem_norm_g, v_mem_norm_g),
                 (pool_scale, m_pool_scale, v_pool_scale), (b_f, m_b_f, v_b_f), (fox_q_g, m_fox_q_g, v_fox_q_g),
                 (fox_k_g, m_fox_k_g, v_fox_k_g), (mem_q_g, m_mem_q_g, v_mem_q_g), (mem_k_g, m_mem_k_g, v_mem_k_g)]
    pool_wmv = tuple(a.reshape(POOL_ROWS, HEAD_DIM) for a in (w_pool, m_w_pool, v_w_pool))
    loss, *small_out = _adamw_small(vec, dw4_sum, small_wmv, pool_wmv)
    big = [[g_w_in_t.T[None], g_w_kv[None], g_w_out[None]]]
    upd = [[a.T for a in _adamw("adamw_w_in", w_in_t, g_w_in_t, m_w_in_t, v_w_in_t)],
           _adamw("adamw_w_mem_kv", w_mem_kv[0], g_w_kv, m_w_mem_kv[0], v_w_mem_kv[0]),
           _adamw("adamw_w_out", w_out[0], g_w_out, m_w_out[0], v_w_out[0])]
    big += [[u[k][None] for u in upd] for k in range(3)]

    def leaves(k):
        sm = small_out[k]
        b_in, b_kv, b_out = big[k]
        return (sm[0], b_in, sm[3], sm[8].reshape(w_pool.shape), sm[2], sm[4], sm[5], sm[1], b_kv, sm[6], sm[7], b_out)

    return (loss.reshape(()), grad_x[None], *leaves(0), *leaves(1), *leaves(2), *leaves(3))


def _local_partials(xs, mems, tgt, w_in_b, w_kv_b, w_out_b, norm_g, b_f, w4, pool_scale, fox_q_g, fox_k_g,
                    mem_norm_g, mem_q_g, mem_k_g):
    wp = _full_w_in_padded(w_in_b)
    bf_pad = jnp.pad(b_f, ((0, 0), (0, LANES - FOX_HEADS)))
    fq_g, fk_g = _tile_heads(fox_q_g, FOX_HEADS), _tile_heads(fox_k_g, FOX_HEADS)
    mq_g, mk_g = _tile_heads(mem_q_g, 4), _tile_heads(mem_k_g, 4)

    mnb, kv, kmn, vmb = _mem_fwd(mems, mem_norm_g, w_kv_b, mk_g)
    hb, pa, qk, qa, ka, va, gb, pm, fb = _fwd_in(xs, norm_g, wp, bf_pad, fq_g, fk_g)
    ma, db = _pool_fwd(pa, w4, pool_scale)
    mm = _mem_attn_fwd(pm, kmn, vmb, mq_g)
    o, mb, r4 = _fox_fwd(qa, ka, va, gb)
    dy, dma, dmb, dmm, dw_out, loss_row = _out_loss(xs, tgt, ma, mb, mm, w_out_b)

    dpm, dkmn, dvm, dmq_g = _mem_attn_bwd(pm, dmm, kmn, vmb, mq_g)
    dw_kv, dmemnorm_g, dmk_g = _mem_bwd(dkmn, dvm, kv, mnb, mems, w_kv_b, mk_g, mem_norm_g)
    dpa, dw4, dpscale = _pool_bwd(pa, db, dma, w4, pool_scale)
    doa, dgb, rr = _fox_prep(dmb, gb, o, r4)
    dka, dva, dqa = _fox_bwd(ka, va, qa, doa, rr)
    dqk, dvb, dfb, dfq_g, dfk_g, dbf = _fox_post(dqa, dka, dva, qk, fb, bf_pad, fq_g, fk_g)
    dparts = (dpa, dqk, dvb, dgb, dpm, dfb)
    leaves = (dmemnorm_g, dpscale, dbf, dfq_g, dfk_g, dmq_g, dmk_g)
    return dy, wp, hb, dparts, dw_kv, dw_out, leaves, loss_row, dw4
```

```python
import functools

import jax
import jax.numpy as jnp
from jax import lax
from jax.experimental import pallas as pl
from jax.experimental.pallas import tpu as pltpu

F32 = jnp.float32
BF16 = jnp.bfloat16
MESH = pl.DeviceIdType.MESH

D_MODEL = 1024
HEAD_DIM = 64
POOL_WIDTH = 256
FOX_WIDTH = 512
FOX_HEADS = 8
MEM_WIDTH = 256
N_MEM = 256
IN_WIDTH = 3080
EPS = 1e-6
ATT_SCALE = 0.125

ADAM_LR = 0.001
ADAM_B1 = 0.9
ADAM_B2 = 0.999
ADAM_EPS = 1e-08
ADAM_WD = 0.01
ADAM_STEP = 10

LANES = 128
PA_LO, QB_LO, KB_LO, VB_LO, GB_LO, PM_LO, FB_LO, PROJ_PAD = 0, 512, 1024, 1536, 2048, 2560, 3072, 3200
F_ORIG_LO = 2048

TILE = 512
VMEM_LIMIT = 56 * 1024 * 1024

VEC_LEAVES = (("norm_g", 0, 1024), ("mem_norm_g", 1, 1024), ("pool_scale", 2, 256), ("b_f", 3, 8),
              ("fox_q_g", 4, 64), ("fox_k_g", 5, 64), ("mem_q_g", 6, 64), ("mem_k_g", 7, 64))
VEC_LOSS_ROW = 8
VEC_ROWS = 16
POOL_ROWS = 256


def _params(n_grid=1, vmem=VMEM_LIMIT):
    return pltpu.CompilerParams(dimension_semantics=("arbitrary",) * n_grid, vmem_limit_bytes=vmem)


def _rows(t, w):
    return pl.BlockSpec((t, w), lambda i: (i, 0))


def _rows_rev(t, w, n):
    return pl.BlockSpec((t, w), lambda i: (n - 1 - i, 0))


def _full(shape):
    return pl.BlockSpec(shape, lambda i: (0,) * len(shape))


def _sig(x):
    return 1.0 / (1.0 + jnp.exp(-x))


def _lane_lo(shape):
    return lax.broadcasted_iota(jnp.int32, shape, 1) < HEAD_DIM


def _pair_sum(v, lo):
    s0 = jnp.sum(jnp.where(lo, v, 0.0), axis=-1, keepdims=True)
    s1 = jnp.sum(jnp.where(lo, 0.0, v), axis=-1, keepdims=True)
    return jnp.where(lo, s0, s1)


def _head_rms(blk, lo):
    return lax.rsqrt(_pair_sum(blk * blk, lo) * (1.0 / HEAD_DIM) + EPS)


def _head_norm_bwd(dyn, xhat, rr, g, lo):
    a = dyn * g
    return rr * (a - xhat * (_pair_sum(xhat * a, lo) * (1.0 / HEAD_DIM)))


def _fold_heads(acc):
    tot = acc[:, 0:LANES]
    for p in range(1, acc.shape[1] // LANES):
        tot = tot + acc[:, p * LANES:(p + 1) * LANES]
    return tot + pltpu.roll(tot, HEAD_DIM, axis=1)


def _lane_pick(v, lane, idx):
    return jnp.sum(jnp.where(lane == idx, v, 0.0), axis=-1, keepdims=True)


NT = (((1,), (1,)), ((), ()))
TN = (((0,), (0,)), ((), ()))


def _dot(a, b, dims=None):
    if dims is None:
        return jnp.dot(a, b, preferred_element_type=F32)
    return lax.dot_general(a, b, dims, preferred_element_type=F32)


def _my_place():
    return lax.axis_index("x"), lax.axis_index("y"), lax.axis_index("c")


def _half_dims(shape, axis):
    return (shape[0] // 2, shape[1]) if axis == 0 else (shape[0], shape[1] // 2)


def _half_of(ref, axis, core, lead=False):
    rows, cols = ref.shape[-2:]
    if axis == 0:
        idx = (pl.ds(pl.multiple_of(core * (rows // 2), 16), rows // 2), slice(None))
    else:
        idx = (slice(None), pl.ds(pl.multiple_of(core * (cols // 2), LANES), cols // 2))
    return ref.at[(slice(None),) + idx] if lead else ref.at[idx]


class _HalfGather:
    def __init__(self, ins, outs, axes, f32_bufs, bf_bufs, send_sems, recv_sems, local_sems):
        self.ins, self.outs, self.axes = ins, outs, axes
        self.f32_bufs, self.bf_bufs = f32_bufs, bf_bufs
        self.send_sems, self.recv_sems, self.local_sems = send_sems, recv_sems, local_sems
        self.n = len(ins)
        x, y, self.c = _my_place()
        self.me, self.sibling = (x, y, self.c), (x, y, 1 - self.c)
        self.chips = [(1 - x, y), (x, 1 - y), (1 - x, 1 - y)]

    @staticmethod
    def scratch(shards, axes):
        dims = [_half_dims(a.shape, axis) for a, axis in zip(shards, axes)]
        n = len(shards)
        return [pltpu.VMEM(d, F32) for d in dims] + [pltpu.VMEM(d, BF16) for d in dims] + [
            pltpu.SemaphoreType.DMA((7 * n,)), pltpu.SemaphoreType.DMA((7 * n,)), pltpu.SemaphoreType.DMA((2 * n,))]

    @staticmethod
    def out_shapes(shards, axes):
        return tuple(jax.ShapeDtypeStruct((8,) + _half_dims(a.shape, axis), BF16) for a, axis in zip(shards, axes))

    def _blk(self, a, px, py, pc):
        return self.outs[a].at[4 * px + 2 * py + pc]

    def _copy(self, a, k, block, to, src=None):
        return pltpu.make_async_remote_copy(
            src_ref=self._blk(a, *block) if src is None else src, dst_ref=self._blk(a, *block),
            send_sem=self.send_sems.at[7 * a + k], recv_sem=self.recv_sems.at[7 * a + k], device_id=to,
            device_id_type=MESH)

    def _keep(self, a):
        return pltpu.make_async_copy(self.bf_bufs[a], self._blk(a, *self.me), self.local_sems.at[self.n + a])

    def _first(self, a):
        mine = [self._copy(a, 0, self.me, self.sibling, src=self.bf_bufs[a])]
        return mine + [self._copy(a, 1 + j, self.me, (*chip, self.c), src=self.bf_bufs[a])
                       for j, chip in enumerate(self.chips)]

    def send_mine(self):
        loads = [pltpu.make_async_copy(_half_of(self.ins[a], self.axes[a], self.c), self.f32_bufs[a],
                                       self.local_sems.at[a]) for a in range(self.n)]
        for cp in loads:
            cp.start()
        for a in range(self.n):
            loads[a].wait()
            self.bf_bufs[a][...] = self.f32_bufs[a][...].astype(BF16)
            self._keep(a).start()
            for cp in self._first(a):
                cp.start()

    def pass_on(self):
        for a in range(self.n):
            for j, chip in enumerate(self.chips):
                self._copy(a, 1 + j, (*chip, self.c), self.me).wait_recv()
                self._copy(a, 4 + j, (*chip, self.c), self.sibling).start()

    def finish(self):
        for a in range(self.n):
            self._copy(a, 0, self.sibling, self.me).wait_recv()
            for j, chip in enumerate(self.chips):
                self._copy(a, 4 + j, (*chip, 1 - self.c), self.me).wait_recv()
        for a in range(self.n):
            for cp in self._first(a):
                cp.wait_send()
            for j, chip in enumerate(self.chips):
                self._copy(a, 4 + j, (*chip, self.c), self.sibling).wait_send()
            self._keep(a).wait()


def _all_gather_weights(shards, axes):
    n = len(shards)

    def body(*refs):
        gather = _HalfGather(refs[0:n], refs[n:2 * n], axes, refs[2 * n:3 * n], refs[3 * n:4 * n], *refs[4 * n:])
        gather.send_mine()
        gather.pass_on()
        gather.finish()

    any_spec = pl.BlockSpec(memory_space=pl.ANY)
    return pl.pallas_call(
        body, name="weights_all_gather", out_shape=_HalfGather.out_shapes(shards, axes),
        in_specs=[any_spec] * n, out_specs=(any_spec,) * n, scratch_shapes=_HalfGather.scratch(shards, axes),
        compiler_params=pltpu.CompilerParams(vmem_limit_bytes=VMEM_LIMIT),
    )(*shards)


class _ShardReduce:
    def __init__(self, g_refs, out_refs, axes, bufs, send_sems, recv_sems, local_sems):
        self.g_refs, self.out_refs, self.axes = g_refs, out_refs, axes
        self.recv_a, self.own_a, self.send_b, self.recv_b, self.fin = bufs
        self.send_sems, self.recv_sems, self.local_sems = send_sems, recv_sems, local_sems
        self.n = len(g_refs)
        x, y, self.c = _my_place()
        self.chip = 2 * x + y
        self.sibling = (x, y, 1 - self.c)

    @staticmethod
    def scratch(gparts, axes):
        dims = [_half_dims(g.shape[1:], axis) for g, axis in zip(gparts, axes)]
        shapes = []
        for dtype, lead in ((F32, (4,)), (F32, (4,)), (BF16, (4,)), (BF16, (4,)), (F32, ())):
            shapes += [pltpu.VMEM(lead + d, dtype) for d in dims]
        return shapes

    def _to_sibling(self, a):
        return pltpu.make_async_remote_copy(
            src_ref=_half_of(self.g_refs[a], self.axes[a], 1 - self.c, lead=True), dst_ref=self.recv_a[a],
            send_sem=self.send_sems.at[5 * a], recv_sem=self.recv_sems.at[5 * a], device_id=self.sibling,
            device_id_type=MESH)

    def _own(self, a):
        return pltpu.make_async_copy(_half_of(self.g_refs[a], self.axes[a], self.c, lead=True), self.own_a[a],
                                     self.local_sems.at[a])

    def _to_chip(self, a, k):
        dest = (self.chip + k) % 4
        return pltpu.make_async_remote_copy(
            src_ref=self.send_b[a].at[dest], dst_ref=self.recv_b[a].at[self.chip], send_sem=self.send_sems.at[5 * a + k],
            recv_sem=self.recv_sems.at[5 * a + k], device_id=(dest // 2, dest % 2, self.c), device_id_type=MESH)

    def _give(self, a):
        return pltpu.make_async_remote_copy(
            src_ref=self.fin[a], dst_ref=_half_of(self.out_refs[a], self.axes[a], self.c),
            send_sem=self.send_sems.at[5 * a + 4], recv_sem=self.recv_sems.at[5 * a + 4], device_id=self.sibling,
            device_id_type=MESH)

    def _mine(self, a):
        return pltpu.make_async_copy(self.fin[a], _half_of(self.out_refs[a], self.axes[a], self.c),
                                     self.local_sems.at[a])

    def exchange_with_sibling(self):
        for a in range(self.n):
            self._to_sibling(a).start()
            self._own(a).start()

    def send_to_chips(self):
        for a in range(self.n):
            self._own(a).wait()
            self._to_sibling(a).wait_recv()
            for j in range(4):
                self.send_b[a][j] = (self.own_a[a][j] + self.recv_a[a][j]).astype(BF16)
            for k in range(1, 4):
                self._to_chip(a, k).start()
            keep = pltpu.make_async_copy(self.send_b[a].at[self.chip], self.recv_b[a].at[self.chip],
                                         self.local_sems.at[self.n + a])
            keep.start()
            keep.wait()

    def sum_and_share(self):
        for a in range(self.n):
            for k in range(1, 4):
                self._to_chip(a, k).wait_recv()
            tot = self.recv_b[a][0].astype(F32) + self.recv_b[a][1].astype(F32)
            tot = tot + self.recv_b[a][2].astype(F32)
            self.fin[a][...] = tot + self.recv_b[a][3].astype(F32)
            self._give(a).start()
            self._mine(a).start()

    def finish(self):
        for a in range(self.n):
            self._give(a).wait_recv()
            self._mine(a).wait()
            self._to_sibling(a).wait_send()
            self._give(a).wait_send()
            for k in range(1, 4):
                self._to_chip(a, k).wait_send()


def _mem_fwd(mem, mem_norm_g, w_kv, mk_g):
    n = mem.shape[0]

    def body(mem_ref, g_ref, w_ref, kg_ref, mn_ref, kv_ref, kn_ref, vm_ref):
        xm = mem_ref[...]
        rr = lax.rsqrt(jnp.mean(xm * xm, axis=-1, keepdims=True) + EPS)
        mnb = ((xm * rr) * g_ref[...]).astype(BF16)
        mn_ref[...] = mnb
        kv = _dot(mnb, w_ref[...])
        kv_ref[...] = kv
        lo = _lane_lo((n, LANES))
        for p in range(MEM_WIDTH // LANES):
            sl = slice(p * LANES, (p + 1) * LANES)
            kb = kv[:, sl]
            kn_ref[:, sl] = ((kb * _head_rms(kb, lo)) * kg_ref[:, sl]).astype(BF16)
        vm_ref[...] = kv[:, MEM_WIDTH:].astype(BF16)

    return pl.pallas_call(
        body, name="mem_fwd",
        out_shape=(jax.ShapeDtypeStruct((n, D_MODEL), BF16), jax.ShapeDtypeStruct((n, 2 * MEM_WIDTH), F32),
                   jax.ShapeDtypeStruct((n, MEM_WIDTH), BF16), jax.ShapeDtypeStruct((n, MEM_WIDTH), BF16)),
        compiler_params=pltpu.CompilerParams(vmem_limit_bytes=VMEM_LIMIT),
    )(mem, mem_norm_g, w_kv, mk_g)


AUG_LO = 64
KEY_SUM_LANE = 72
QUERY_SUM_LANE = 80
HEAD_BLOCKS = FOX_HEADS * LANES


def _ones3(lane):
    return jnp.where((lane >= AUG_LO) & (lane < AUG_LO + 3), 1.0, 0.0)


def _spread3(cols):
    hi = cols.astype(BF16)
    rest = cols - hi.astype(F32)
    mid = rest.astype(BF16)
    low = (rest - mid.astype(F32)).astype(BF16)
    r = lax.broadcasted_iota(jnp.int32, (LANES, HEAD_BLOCKS), 0)
    c = lax.broadcasted_iota(jnp.int32, (LANES, HEAD_BLOCKS), 1)
    out = None
    for k, part in enumerate((hi, mid, low)):
        term = _dot(part, jnp.where(c == r * LANES + (AUG_LO + k), 1.0, 0.0).astype(BF16))
        out = term if out is None else out + term
    return out


def _head_block(pair_blk, hh, lo, extras):
    src = pair_blk if hh == 0 else pltpu.roll(pair_blk, HEAD_DIM, axis=1)
    return jnp.where(lo, src, extras).astype(BF16)


def _pair_block(blk0, blk1, lo):
    return jnp.where(lo, blk0, pltpu.roll(blk1, HEAD_DIM, axis=1))


def _fwd_in(x, norm_g, wp, bf_pad, fq_g, fk_g):
    s = x.shape[0]
    t = TILE
    n = s // t

    def body(x_ref, ng_ref, wp_ref, bf_ref, qg_ref, kg_ref,
             h_ref, pa_ref, qk_ref, qa_ref, ka_ref, va_ref, gb_ref, pm_ref, fb_ref, carry_ref, fcol_ref):
        @pl.when(pl.program_id(0) == 0)
        def _():
            carry_ref[...] = jnp.zeros_like(carry_ref)

        xv = x_ref[...]
        rr = lax.rsqrt(jnp.mean(xv * xv, axis=-1, keepdims=True) + EPS)
        hb = ((xv * rr) * ng_ref[...]).astype(BF16)
        h_ref[...] = hb

        def proj(lo, hi):
            return _dot(hb, wp_ref[lo:hi, :], NT)

        pa_ref[...] = proj(PA_LO, QB_LO)
        gb_ref[...] = proj(GB_LO, PM_LO)
        pm_ref[...] = proj(PM_LO, FB_LO)
        fb = proj(FB_LO, PROJ_PAD)
        fb_ref[...] = fb

        lane = lax.broadcasted_iota(jnp.int32, (t, LANES), 1)
        row = lax.broadcasted_iota(jnp.int32, (t, LANES), 0)
        lo = lane < HEAD_DIM
        z = fb + bf_ref[...]
        lf = -(jnp.maximum(-z, 0.0) + jnp.log1p(jnp.exp(-jnp.abs(z))))
        lf = jnp.where(lane < FOX_HEADS, lf, 0.0)
        sh = 1
        while sh < t:
            lf = lf + jnp.where(row >= sh, pltpu.roll(lf, sh, axis=0), 0.0)
            sh *= 2
        fcum = lf + carry_ref[...]
        fcol_ref[...] = fcum
        carry_ref[...] = fcol_ref[t - 1:t, :]

        ones3 = _ones3(lane)
        minus_f = _spread3(-fcum)
        for seg, g_ref, out_ref, scale in ((QB_LO, qg_ref, qa_ref, ATT_SCALE), (KB_LO, kg_ref, ka_ref, 1.0)):
            raw = proj(seg, seg + FOX_WIDTH)
            qk_ref[:, seg - QB_LO:seg - QB_LO + FOX_WIDTH] = raw
            for p in range(FOX_WIDTH // LANES):
                sl = slice(p * LANES, (p + 1) * LANES)
                blk = raw[:, sl]
                normed = ((blk * _head_rms(blk, lo)) * g_ref[:, sl]) * scale
                for hh in range(2):
                    h = 2 * p + hh
                    if seg == QB_LO:
                        extras = jnp.where(lane == QUERY_SUM_LANE + h, 1.0, ones3)
                    else:
                        extras = jnp.where(lane == KEY_SUM_LANE + h, 1.0, minus_f[:, h * LANES:(h + 1) * LANES])
                    out_ref[:, h * LANES:(h + 1) * LANES] = _head_block(normed, hh, lo, extras)
        vraw = proj(VB_LO, GB_LO)
        for h in range(FOX_HEADS):
            va_ref[:, h * LANES:(h + 1) * LANES] = _head_block(vraw[:, (h // 2) * LANES:(h // 2 + 1) * LANES], h % 2, lo, ones3)

    outs = (
        jax.ShapeDtypeStruct((s, D_MODEL), BF16),
        jax.ShapeDtypeStruct((s, 512), F32),
        jax.ShapeDtypeStruct((s, 2 * FOX_WIDTH), F32),
        jax.ShapeDtypeStruct((s, HEAD_BLOCKS), BF16),
        jax.ShapeDtypeStruct((s, HEAD_BLOCKS), BF16),
        jax.ShapeDtypeStruct((s, HEAD_BLOCKS), BF16),
        jax.ShapeDtypeStruct((s, FOX_WIDTH), F32),
        jax.ShapeDtypeStruct((s, 512), F32),
        jax.ShapeDtypeStruct((s, LANES), F32),
    )
    return pl.pallas_call(
        body, name="fwd_in", grid=(n,), out_shape=outs,
        in_specs=[_rows(t, D_MODEL), _full((1, D_MODEL)), _full((PROJ_PAD, D_MODEL)), _full((1, LANES)),
                  _full((1, FOX_WIDTH)), _full((1, FOX_WIDTH))],
        out_specs=(_rows(t, D_MODEL), _rows(t, 512), _rows(t, 2 * FOX_WIDTH), _rows(t, HEAD_BLOCKS),
                   _rows(t, HEAD_BLOCKS), _rows(t, HEAD_BLOCKS), _rows(t, FOX_WIDTH), _rows(t, 512),
                   _rows(t, LANES)),
        scratch_shapes=[pltpu.VMEM((1, LANES), F32), pltpu.VMEM((t, LANES), F32)],
        compiler_params=_params(),
    )(x, norm_g, wp, bf_pad, fq_g, fk_g)


POOL_HALO = 16


def _pool_window(lane):
    return jnp.where(lane < 64, 2.0, jnp.where(lane < 128, 4.0, jnp.where(lane < 192, 8.0, 16.0)))


def _pool_pick(lane, s2, s4, s8, s16):
    return jnp.where(lane < 64, s2, jnp.where(lane < 128, s4, jnp.where(lane < 192, s8, s16)))


def _group_onehot(shape, row_is_group_lane):
    r = lax.broadcasted_iota(jnp.int32, shape, 0)
    c = lax.broadcasted_iota(jnp.int32, shape, 1)
    hit = (r % HEAD_DIM == c) if row_is_group_lane else (c % HEAD_DIM == r)
    return jnp.where(hit, 1.0, 0.0).astype(F32)


def _same_group(shape):
    r = lax.broadcasted_iota(jnp.int32, shape, 0)
    c = lax.broadcasted_iota(jnp.int32, shape, 1)
    return (r // HEAD_DIM) == (c // HEAD_DIM)


def _pool_block_diag(w4):
    spread = jnp.dot(w4, _group_onehot((HEAD_DIM, POOL_WIDTH), False), preferred_element_type=F32,
                     precision=lax.Precision.HIGHEST)
    return jnp.where(_same_group((POOL_WIDTH, POOL_WIDTH)), spread, 0.0).astype(BF16)


def _pool_fwd(pa, w4, pscale):
    s = pa.shape[0]
    t = TILE
    n = s // t
    ext = t + POOL_HALO

    def body(pa_ref, w4_ref, sc_ref, ma_ref, d_ref, ext_ref, w_ref):
        i = pl.program_id(0)

        @pl.when(i == 0)
        def _():
            ext_ref[0:POOL_HALO, :] = jnp.zeros((POOL_HALO, POOL_WIDTH), F32)
            w_ref[...] = _pool_block_diag(w4_ref[...])

        u = pa_ref[:, 0:POOL_WIDTH]
        ext_ref[POOL_HALO:ext, :] = u
        e = ext_ref[...]
        s2 = e + pltpu.roll(e, 1, axis=0)
        s4 = s2 + pltpu.roll(s2, 2, axis=0)
        s8 = s4 + pltpu.roll(s4, 4, axis=0)
        s16 = s8 + pltpu.roll(s8, 8, axis=0)
        lane_e = lax.broadcasted_iota(jnp.int32, (ext, POOL_WIDTH), 1)
        win = _pool_pick(lane_e, s2, s4, s8, s16)[POOL_HALO:ext, :]
        lane = lax.broadcasted_iota(jnp.int32, (t, POOL_WIDTH), 1)
        pos = (lax.broadcasted_iota(jnp.int32, (t, POOL_WIDTH), 0) + (i * t + 1)).astype(F32)
        d = win / jnp.minimum(pos, _pool_window(lane)) - u
        db = d.astype(BF16)
        d_ref[...] = db
        ya = _dot(db, w_ref[...]) * sc_ref[...]
        ga = pa_ref[:, POOL_WIDTH:2 * POOL_WIDTH]
        ma_ref[...] = (ya * (ga * _sig(ga))).astype(BF16)
        ext_ref[0:POOL_HALO, :] = ext_ref[t:ext, :]

    return pl.pallas_call(
        body, name="pool_fwd", grid=(n,),
        out_shape=(jax.ShapeDtypeStruct((s, POOL_WIDTH), BF16), jax.ShapeDtypeStruct((s, POOL_WIDTH), BF16)),
        in_specs=[_rows(t, 512), _full((POOL_ROWS, HEAD_DIM)), _full((1, POOL_WIDTH))],
        out_specs=(_rows(t, POOL_WIDTH), _rows(t, POOL_WIDTH)),
        scratch_shapes=[pltpu.VMEM((ext, POOL_WIDTH), F32), pltpu.VMEM((POOL_WIDTH, POOL_WIDTH), BF16)],
        compiler_params=_params(),
    )(pa, w4, pscale)


def _mem_softmax(qm, kp):
    sc = _dot(qm, kp, NT)
    e = jnp.exp(sc - jnp.max(sc, axis=-1, keepdims=True))
    return e * (1.0 / jnp.sum(e, axis=-1, keepdims=True))


def _mem_attn_fwd(pm, kmn, vmb, mq_g):
    s = pm.shape[0]
    t = TILE
    n = s // t

    def body(pm_ref, k_ref, v_ref, g_ref, mm_ref):
        lo = _lane_lo((t, LANES))
        for p in range(MEM_WIDTH // LANES):
            sl = slice(p * LANES, (p + 1) * LANES)
            qb = pm_ref[:, sl]
            qs = (((qb * _head_rms(qb, lo)) * g_ref[:, sl]) * ATT_SCALE).astype(BF16)
            kp = k_ref[:, sl]
            vp = v_ref[:, sl]
            outs = []
            for hh in range(2):
                msk = lo if hh == 0 else jnp.logical_not(lo)
                prob = _mem_softmax(jnp.where(msk, qs, jnp.zeros_like(qs)), kp)
                outs.append(_dot(prob.astype(BF16), vp))
            o = jnp.where(lo, outs[0], outs[1])
            gm = pm_ref[:, MEM_WIDTH + p * LANES:MEM_WIDTH + (p + 1) * LANES]
            mm_ref[:, sl] = (o * (gm * _sig(gm))).astype(BF16)

    return pl.pallas_call(
        body, name="mem_attn_fwd", grid=(n,),
        out_shape=jax.ShapeDtypeStruct((s, MEM_WIDTH), BF16),
        in_specs=[_rows(t, 512), _full((N_MEM, MEM_WIDTH)), _full((N_MEM, MEM_WIDTH)), _full((1, MEM_WIDTH))],
        out_specs=_rows(t, MEM_WIDTH),
        compiler_params=_params(),
    )(pm, kmn, vmb, mq_g)


FOX_FWD_HEADS = 4


def _fox_fwd(qa, ka, va, gb):
    s = qa.shape[0]
    t = TILE
    n = s // t
    heads = FOX_FWD_HEADS
    pairs = heads // 2
    group_w = heads * LANES

    def body(qa_ref, ka_ref, va_ref, gb_ref, o_ref, mb_ref, r_ref):
        i = pl.program_id(1)
        lane = lax.broadcasted_iota(jnp.int32, (t, LANES), 1)
        lo = lane < HEAD_DIM
        causal = lax.broadcasted_iota(jnp.int32, (t, t), 1) <= lax.broadcasted_iota(jnp.int32, (t, t), 0)
        qas = [qa_ref[:, hh * LANES:(hh + 1) * LANES] for hh in range(heads)]

        def step(j, carry, masked):
            rows = pl.ds(pl.multiple_of(j * t, t), t)
            new = []
            for hh in range(heads):
                cols = slice(hh * LANES, (hh + 1) * LANES)
                m, acc = carry[hh]
                sc = _dot(qas[hh], ka_ref[rows, cols], NT)
                if masked:
                    sc = jnp.where(causal, sc, -1e30)
                m_new = jnp.maximum(m, jnp.max(sc, axis=-1, keepdims=True))
                acc = jnp.exp(m - m_new) * acc + _dot(jnp.exp(sc - m_new).astype(BF16), va_ref[rows, cols])
                new.append((m_new, acc))
            return tuple(new)

        init = (jnp.full((t, 1), -1e30, F32), jnp.zeros((t, LANES), F32))
        carry = lax.fori_loop(0, i, functools.partial(step, masked=False), (init,) * heads)
        res = step(i, carry, masked=True)
        for p in range(pairs):
            outs = []
            rcol = jnp.zeros((t, LANES), F32)
            for hh in range(2):
                m, acc = res[2 * p + hh]
                l = _lane_pick(acc, lane, AUG_LO)
                outs.append(acc * (1.0 / l))
                rcol = jnp.where(lane == hh, m + jnp.log(l), rcol)
            o = _pair_block(outs[0], outs[1], lo)
            sl = slice(p * LANES, (p + 1) * LANES)
            o_ref[:, sl] = o
            g = gb_ref[:, sl]
            mb_ref[:, sl] = (o * (g * _sig(g))).astype(BF16)
            r_ref[p] = rcol

    tile_spec = pl.BlockSpec((t, pairs * LANES), lambda p, i: (i, p))
    full_spec = pl.BlockSpec((s, group_w), lambda p, i: (0, p))
    return pl.pallas_call(
        body, name="fox_fwd", grid=(FOX_HEADS // heads, n),
        out_shape=(jax.ShapeDtypeStruct((s, FOX_WIDTH), F32), jax.ShapeDtypeStruct((s, FOX_WIDTH), BF16),
                   jax.ShapeDtypeStruct((FOX_HEADS // 2, s, LANES), F32)),
        in_specs=[pl.BlockSpec((t, group_w), lambda p, i: (i, p)), full_spec, full_spec, tile_spec],
        out_specs=(tile_spec, tile_spec, pl.BlockSpec((pairs, t, LANES), lambda p, i: (p, i, 0))),
        compiler_params=_params(2),
    )(qa, ka, va, gb)


def _out_loss(x, tgt, ma, mb, mm, wout):
    s = x.shape[0]
    t = TILE
    n = s // t

    def body(x_ref, t_ref, ma_ref, mb_ref, mm_ref, w_ref, dy_ref, dma_ref, dmb_ref, dmm_ref, dw_ref, loss_ref, mix_ref):
        @pl.when(pl.program_id(0) == 0)
        def _():
            dw_ref[...] = jnp.zeros_like(dw_ref)
            loss_ref[...] = jnp.zeros_like(loss_ref)

        mix_ref[:, 0:256] = ma_ref[...]
        mix_ref[:, 256:768] = mb_ref[...]
        mix_ref[:, 768:1024] = mm_ref[...]
        mix = mix_ref[...]
        err = (x_ref[...] + _dot(mix, w_ref[...])) - t_ref[...]
        row_mean = jnp.sum(err * err, axis=-1, keepdims=True) * (1.0 / D_MODEL)
        loss_ref[...] += 0.5 * jnp.sum(row_mean, axis=0, keepdims=True)
        dy = err * (1.0 / D_MODEL)
        dy_ref[...] = dy
        dyb = dy.astype(BF16)
        dmix = _dot(dyb, w_ref[...], NT)
        dma_ref[...] = dmix[:, 0:256]
        dmb_ref[...] = dmix[:, 256:768]
        dmm_ref[...] = dmix[:, 768:1024]
        dw_ref[...] += _dot(mix, dyb, TN)

    return pl.pallas_call(
        body, name="out_loss", grid=(n,),
        out_shape=(jax.ShapeDtypeStruct((s, D_MODEL), F32), jax.ShapeDtypeStruct((s, 256), F32),
                   jax.ShapeDtypeStruct((s, 512), F32), jax.ShapeDtypeStruct((s, 256), F32),
                   jax.ShapeDtypeStruct((D_MODEL, D_MODEL), F32), jax.ShapeDtypeStruct((1, LANES), F32)),
        in_specs=[_rows(t, D_MODEL), _rows(t, D_MODEL), _rows(t, 256), _rows(t, 512), _rows(t, 256),
                  _full((D_MODEL, D_MODEL))],
        out_specs=(_rows(t, D_MODEL), _rows(t, 256), _rows(t, 512), _rows(t, 256), _full((D_MODEL, D_MODEL)),
                   _full((1, LANES))),
        scratch_shapes=[pltpu.VMEM((t, D_MODEL), BF16)],
        compiler_params=_params(),
    )(x, tgt, ma, mb, mm, wout)


def _mem_attn_bwd(pm, dmm, kmn, vmb, mq_g):
    s = pm.shape[0]
    t = TILE
    n = s // t

    def body(pm_ref, dmm_ref, k_ref, v_ref, g_ref, dpm_ref, dk_ref, dv_ref, dg_ref, gacc_ref):
        @pl.when(pl.program_id(0) == 0)
        def _():
            dk_ref[...] = jnp.zeros_like(dk_ref)
            dv_ref[...] = jnp.zeros_like(dv_ref)
            gacc_ref[...] = jnp.zeros_like(gacc_ref)

        lo = _lane_lo((t, LANES))
        for p in range(MEM_WIDTH // LANES):
            sl = slice(p * LANES, (p + 1) * LANES)
            qb = pm_ref[:, sl]
            rr = _head_rms(qb, lo)
            qhat = qb * rr
            g = g_ref[:, sl]
            qs = ((qhat * g) * ATT_SCALE).astype(BF16)
            gm = pm_ref[:, MEM_WIDTH + p * LANES:MEM_WIDTH + (p + 1) * LANES]
            sg = _sig(gm)
            dmo = dmm_ref[:, sl]
            d_o = dmo * (gm * sg)
            kp = k_ref[:, sl]
            vp = v_ref[:, sl]
            outs, dqs = [], []
            for hh in range(2):
                msk = lo if hh == 0 else jnp.logical_not(lo)
                qm = jnp.where(msk, qs, jnp.zeros_like(qs))
                prob = _mem_softmax(qm, kp)
                pb = prob.astype(BF16)
                outs.append(_dot(pb, vp))
                dom = jnp.where(msk, d_o, 0.0).astype(BF16)
                dp = _dot(dom, vp, NT)
                ds = (prob * (dp - jnp.sum(prob * dp, axis=-1, keepdims=True))).astype(BF16)
                dqs.append(_dot(ds, kp))
                dk_ref[:, sl] += _dot(ds, qm, TN)
                dv_ref[:, sl] += _dot(pb, dom, TN)
            o = jnp.where(lo, outs[0], outs[1])
            dqn = jnp.where(lo, dqs[0], dqs[1]) * ATT_SCALE
            dpm_ref[:, sl] = _head_norm_bwd(dqn, qhat, rr, g, lo).astype(BF16)
            dpm_ref[:, MEM_WIDTH + p * LANES:MEM_WIDTH + (p + 1) * LANES] = (
                dmo * o * (sg * (1.0 + gm * (1.0 - sg)))).astype(BF16)
            gacc_ref[:, sl] += jnp.sum(dqn * qhat, axis=0, keepdims=True)

        @pl.when(pl.program_id(0) == n - 1)
        def _():
            dg_ref[...] = _fold_heads(gacc_ref[...])

    return pl.pallas_call(
        body, name="mem_attn_bwd", grid=(n,),
        out_shape=(jax.ShapeDtypeStruct((s, 512), BF16), jax.ShapeDtypeStruct((N_MEM, MEM_WIDTH), F32),
                   jax.ShapeDtypeStruct((N_MEM, MEM_WIDTH), F32), jax.ShapeDtypeStruct((1, LANES), F32)),
        in_specs=[_rows(t, 512), _rows(t, MEM_WIDTH), _full((N_MEM, MEM_WIDTH)), _full((N_MEM, MEM_WIDTH)),
                  _full((1, MEM_WIDTH))],
        out_specs=(_rows(t, 512), _full((N_MEM, MEM_WIDTH)), _full((N_MEM, MEM_WIDTH)), _full((1, LANES))),
        scratch_shapes=[pltpu.VMEM((1, MEM_WIDTH), F32)],
        compiler_params=_params(),
    )(pm, dmm, kmn, vmb, mq_g)


def _mem_bwd(dkn, dvm, kv, mnb, mem, w_kv, mk_g, mem_norm_g):
    n = mem.shape[0]

    def body(dkn_ref, dvm_ref, kv_ref, mn_ref, mem_ref, w_ref, kg_ref, g_ref, dw_ref, dg_ref, dkg_ref, dkv_ref):
        lo = _lane_lo((n, LANES))
        gacc = []
        for p in range(MEM_WIDTH // LANES):
            sl = slice(p * LANES, (p + 1) * LANES)
            kb = kv_ref[:, sl]
            rr = _head_rms(kb, lo)
            khat = kb * rr
            dk = dkn_ref[:, sl]
            dkv_ref[:, sl] = _head_norm_bwd(dk, khat, rr, kg_ref[:, sl], lo).astype(BF16)
            gacc.append(jnp.sum(dk * khat, axis=0, keepdims=True))
        dkg_ref[...] = _fold_heads(jnp.concatenate(gacc, axis=1))
        dkv_ref[:, MEM_WIDTH:] = dvm_ref[...].astype(BF16)
        dkv = dkv_ref[...]
        dw_ref[...] = _dot(mn_ref[...], dkv, TN)
        dmn = _dot(dkv, w_ref[...], NT)
        xm = mem_ref[...]
        rr = lax.rsqrt(jnp.mean(xm * xm, axis=-1, keepdims=True) + EPS)
        dg_ref[...] = jnp.sum(dmn * (xm * rr), axis=0, keepdims=True)

    return pl.pallas_call(
        body, name="mem_bwd",
        out_shape=(jax.ShapeDtypeStruct((D_MODEL, 2 * MEM_WIDTH), F32), jax.ShapeDtypeStruct((1, D_MODEL), F32),
                   jax.ShapeDtypeStruct((1, LANES), F32)),
        scratch_shapes=[pltpu.VMEM((n, 2 * MEM_WIDTH), BF16)],
        compiler_params=pltpu.CompilerParams(vmem_limit_bytes=VMEM_LIMIT),
    )(dkn, dvm, kv, mnb, mem, w_kv, mk_g, mem_norm_g)


def _pool_bwd(pa, db, dma, w4, pscale):
    s = pa.shape[0]
    t = TILE
    n = s // t
    ext = t + POOL_HALO

    def body(pa_ref, d_ref, dma_ref, w4_ref, sc_ref, dpa_ref, dw4_ref, dsc_ref, ext_ref, w_ref, dw_ref):
        i = pl.program_id(0)

        @pl.when(i == 0)
        def _():
            dw_ref[...] = jnp.zeros_like(dw_ref)
            dsc_ref[...] = jnp.zeros_like(dsc_ref)
            ext_ref[t:ext, :] = jnp.zeros((POOL_HALO, POOL_WIDTH), F32)
            w_ref[...] = _pool_block_diag(w4_ref[...])

        dbv = d_ref[...]
        z = _dot(dbv, w_ref[...])
        ga = pa_ref[:, POOL_WIDTH:2 * POOL_WIDTH]
        sg = _sig(ga)
        dma_v = dma_ref[...]
        dya = dma_v * (ga * sg)
        dpa_ref[:, POOL_WIDTH:2 * POOL_WIDTH] = (dma_v * (z * sc_ref[...]) * (sg * (1.0 + ga * (1.0 - sg)))).astype(BF16)
        dsc_ref[...] += jnp.sum(dya * z, axis=0, keepdims=True)
        dzb = (dya * sc_ref[...]).astype(BF16)
        dw_ref[...] += _dot(dbv, dzb, TN)
        dd = _dot(dzb, w_ref[...], NT)
        lane = lax.broadcasted_iota(jnp.int32, (t, POOL_WIDTH), 1)
        pos = (lax.broadcasted_iota(jnp.int32, (t, POOL_WIDTH), 0) + ((n - 1 - i) * t + 1)).astype(F32)
        ext_ref[0:t, :] = dd / jnp.minimum(pos, _pool_window(lane))
        e = ext_ref[...]
        s2 = e + pltpu.roll(e, ext - 1, axis=0)
        s4 = s2 + pltpu.roll(s2, ext - 2, axis=0)
        s8 = s4 + pltpu.roll(s4, ext - 4, axis=0)
        s16 = s8 + pltpu.roll(s8, ext - 8, axis=0)
        lane_e = lax.broadcasted_iota(jnp.int32, (ext, POOL_WIDTH), 1)
        win = _pool_pick(lane_e, s2, s4, s8, s16)[0:t, :]
        dpa_ref[:, 0:POOL_WIDTH] = (win - dd).astype(BF16)
        ext_ref[t:ext, :] = ext_ref[0:POOL_HALO, :]

        @pl.when(i == n - 1)
        def _():
            own = jnp.where(_same_group((POOL_WIDTH, POOL_WIDTH)), dw_ref[...], 0.0)
            dw4_ref[...] = jnp.dot(own, _group_onehot((POOL_WIDTH, HEAD_DIM), True), preferred_element_type=F32,
                                   precision=lax.Precision.HIGHEST)

    return pl.pallas_call(
        body, name="pool_bwd", grid=(n,),
        out_shape=(jax.ShapeDtypeStruct((s, 512), BF16), jax.ShapeDtypeStruct((POOL_ROWS, HEAD_DIM), F32),
                   jax.ShapeDtypeStruct((1, POOL_WIDTH), F32)),
        in_specs=[_rows_rev(t, 512, n), _rows_rev(t, POOL_WIDTH, n), _rows_rev(t, POOL_WIDTH, n),
                  _full((POOL_ROWS, HEAD_DIM)), _full((1, POOL_WIDTH))],
        out_specs=(_rows_rev(t, 512, n), _full((POOL_ROWS, HEAD_DIM)), _full((1, POOL_WIDTH))),
        scratch_shapes=[pltpu.VMEM((ext, POOL_WIDTH), F32), pltpu.VMEM((POOL_WIDTH, POOL_WIDTH), BF16),
                        pltpu.VMEM((POOL_WIDTH, POOL_WIDTH), F32)],
        compiler_params=_params(),
    )(pa, db, dma, w4, pscale)


def _fox_prep(dmb, gb, o, r4):
    s = dmb.shape[0]
    t = TILE
    n = s // t
    pairs = FOX_HEADS // 2

    def body(dmb_ref, gb_ref, o_ref, r_ref, doa_ref, dgb_ref, rr_ref):
        lane = lax.broadcasted_iota(jnp.int32, (t, LANES), 1)
        lo = lane < HEAD_DIM
        d_os = []
        delta = jnp.zeros((t, LANES), F32)
        for p in range(pairs):
            sl = slice(p * LANES, (p + 1) * LANES)
            g = gb_ref[:, sl]
            sg = _sig(g)
            dm = dmb_ref[:, sl]
            ov = o_ref[:, sl]
            d_o = dm * (g * sg)
            d_os.append(d_o)
            dgb_ref[:, sl] = (dm * ov * (sg * (1.0 + g * (1.0 - sg)))).astype(BF16)
            prod = d_o * ov
            delta = jnp.where(lane == 2 * p, jnp.sum(jnp.where(lo, prod, 0.0), axis=-1, keepdims=True), delta)
            delta = jnp.where(lane == 2 * p + 1, jnp.sum(jnp.where(lo, 0.0, prod), axis=-1, keepdims=True), delta)
            rr_ref[p, 0] = r_ref[p].T[0:8, :]
        minus_delta = _spread3(-delta)
        for h in range(FOX_HEADS):
            blk = slice(h * LANES, (h + 1) * LANES)
            doa_ref[:, blk] = _head_block(d_os[h // 2], h % 2, lo, minus_delta[:, blk])

    return pl.pallas_call(
        body, name="fox_prep", grid=(n,),
        out_shape=(jax.ShapeDtypeStruct((s, HEAD_BLOCKS), BF16), jax.ShapeDtypeStruct((s, FOX_WIDTH), BF16),
                   jax.ShapeDtypeStruct((pairs, n, 8, t), F32)),
        in_specs=[_rows(t, FOX_WIDTH), _rows(t, FOX_WIDTH), _rows(t, FOX_WIDTH),
                  pl.BlockSpec((pairs, t, LANES), lambda i: (0, i, 0))],
        out_specs=(_rows(t, HEAD_BLOCKS), _rows(t, FOX_WIDTH), pl.BlockSpec((pairs, 1, 8, t), lambda i: (0, i, 0, 0))),
        compiler_params=_params(),
    )(dmb, gb, o, r4)


FOX_BWD_HEADS = 4


def _fox_bwd(ka, va, qa, doa, rr):
    s = ka.shape[0]
    t = TILE
    n = s // t
    heads = FOX_BWD_HEADS
    group_w = heads * LANES

    def body(ka_ref, va_ref, qa_ref, doa_ref, rr_ref, dka_ref, dva_ref, dqa_ref):
        j = pl.program_id(1)

        @pl.when(j == 0)
        def _():
            dqa_ref[...] = jnp.zeros_like(dqa_ref)

        causal = lax.broadcasted_iota(jnp.int32, (t, t), 0) <= lax.broadcasted_iota(jnp.int32, (t, t), 1)
        kas = [ka_ref[:, hh * LANES:(hh + 1) * LANES] for hh in range(heads)]
        vas = [va_ref[:, hh * LANES:(hh + 1) * LANES] for hh in range(heads)]

        def step(i, carry, masked):
            rows = pl.ds(pl.multiple_of(i * t, t), t)
            new = []
            for hh in range(heads):
                cols = slice(hh * LANES, (hh + 1) * LANES)
                dk_a, dv_a = carry[hh]
                qb = qa_ref[rows, cols]
                d_o = doa_ref[rows, cols]
                arg = _dot(kas[hh], qb, NT) - rr_ref[hh // 2, i, hh % 2:hh % 2 + 1, :]
                if masked:
                    arg = jnp.where(causal, arg, -1e30)
                pt = jnp.exp(arg)
                dst = (pt * _dot(vas[hh], d_o, NT)).astype(BF16)
                dv_a = dv_a + _dot(pt.astype(BF16), d_o)
                dk_a = dk_a + _dot(dst, qb)
                dqa_ref[rows, cols] += _dot(dst, kas[hh], TN)
                new.append((dk_a, dv_a))
            return tuple(new)

        zero = jnp.zeros((t, LANES), F32)
        carry = step(j, ((zero, zero),) * heads, masked=True)
        res = lax.fori_loop(j + 1, n, functools.partial(step, masked=False), carry)
        for hh in range(heads):
            cols = slice(hh * LANES, (hh + 1) * LANES)
            dka_ref[:, cols] = res[hh][0]
            dva_ref[:, cols] = res[hh][1]

    tile_spec = pl.BlockSpec((t, group_w), lambda p, j: (j, p))
    full_spec = pl.BlockSpec((s, group_w), lambda p, j: (0, p))
    return pl.pallas_call(
        body, name="fox_bwd", grid=(FOX_HEADS // heads, n),
        out_shape=(jax.ShapeDtypeStruct((s, HEAD_BLOCKS), F32),) * 3,
        in_specs=[tile_spec, tile_spec, full_spec, full_spec,
                  pl.BlockSpec((heads // 2, n, 8, t), lambda p, j: (p, 0, 0, 0))],
        out_specs=(tile_spec, tile_spec, full_spec),
        compiler_params=_params(2),
    )(ka, va, qa, doa, rr)


def _fox_post(dqa, dka, dva, qk, fb, bf_pad, fq_g, fk_g):
    s = dqa.shape[0]
    t = TILE
    n = s // t

    def body(dqa_ref, dka_ref, dva_ref, qk_ref, fb_ref, bf_ref, qg_ref, kg_ref,
             dqk_ref, dv_ref, dfb_ref, dqg_ref, dkg_ref, dbf_ref, qacc_ref, kacc_ref, carry_ref):
        i = pl.program_id(0)

        @pl.when(i == 0)
        def _():
            qacc_ref[...] = jnp.zeros_like(qacc_ref)
            kacc_ref[...] = jnp.zeros_like(kacc_ref)
            dbf_ref[...] = jnp.zeros_like(dbf_ref)
            carry_ref[...] = jnp.zeros_like(carry_ref)

        lane = lax.broadcasted_iota(jnp.int32, (t, LANES), 1)
        row = lax.broadcasted_iota(jnp.int32, (t, LANES), 0)
        lo = lane < HEAD_DIM

        def head_blocks(ref, p):
            return ref[:, 2 * p * LANES:(2 * p + 1) * LANES], ref[:, (2 * p + 1) * LANES:(2 * p + 2) * LANES]

        dq_sum = jnp.zeros((t, LANES), F32)
        dk_sum = jnp.zeros((t, LANES), F32)
        for p in range(FOX_WIDTH // LANES):
            sl = slice(p * LANES, (p + 1) * LANES)
            dq0, dq1 = head_blocks(dqa_ref, p)
            dk0, dk1 = head_blocks(dka_ref, p)
            dv0, dv1 = head_blocks(dva_ref, p)
            dv_ref[:, sl] = _pair_block(dv0, dv1, lo).astype(BF16)
            dq_sum = dq_sum + (dq0 + dq1)
            dk_sum = dk_sum + (dk0 + dk1)
            for off, pair, g_ref, acc_ref, scale in ((0, _pair_block(dq0, dq1, lo), qg_ref, qacc_ref, ATT_SCALE),
                                                     (FOX_WIDTH, _pair_block(dk0, dk1, lo), kg_ref, kacc_ref, 1.0)):
                raw = qk_ref[:, off + p * LANES:off + (p + 1) * LANES]
                rr = _head_rms(raw, lo)
                xhat = raw * rr
                dn = pair * scale
                dqk_ref[:, off + p * LANES:off + (p + 1) * LANES] = _head_norm_bwd(
                    dn, xhat, rr, g_ref[:, sl], lo).astype(BF16)
                acc_ref[:, sl] += jnp.sum(dn * xhat, axis=0, keepdims=True)

        acc = (pltpu.roll(dq_sum, LANES - KEY_SUM_LANE, axis=1) - pltpu.roll(dk_sum, LANES - QUERY_SUM_LANE, axis=1))
        acc = jnp.where(lane < FOX_HEADS, acc, 0.0)
        sh = 1
        while sh < t:
            acc = acc + jnp.where(row < t - sh, pltpu.roll(acc, t - sh, axis=0), 0.0)
            sh *= 2
        dlogf = acc + carry_ref[...]
        dfb_ref[...] = dlogf
        carry_ref[...] = dfb_ref[0:1, :]
        z = fb_ref[...] + bf_ref[...]
        dz = jnp.where(lane < FOX_HEADS, dlogf * (1.0 / (1.0 + jnp.exp(z))), 0.0)
        dfb_ref[...] = dz
        dbf_ref[...] += jnp.sum(dz, axis=0, keepdims=True)

        @pl.when(i == n - 1)
        def _():
            dqg_ref[...] = _fold_heads(qacc_ref[...])
            dkg_ref[...] = _fold_heads(kacc_ref[...])

    return pl.pallas_call(
        body, name="fox_post", grid=(n,),
        out_shape=(jax.ShapeDtypeStruct((s, 2 * FOX_WIDTH), BF16), jax.ShapeDtypeStruct((s, FOX_WIDTH), BF16),
                   jax.ShapeDtypeStruct((s, LANES), F32), jax.ShapeDtypeStruct((1, LANES), F32),
                   jax.ShapeDtypeStruct((1, LANES), F32), jax.ShapeDtypeStruct((1, LANES), F32)),
        in_specs=[_rows_rev(t, HEAD_BLOCKS, n), _rows_rev(t, HEAD_BLOCKS, n), _rows_rev(t, HEAD_BLOCKS, n),
                  _rows_rev(t, 2 * FOX_WIDTH, n), _rows_rev(t, LANES, n), _full((1, LANES)),
                  _full((1, FOX_WIDTH)), _full((1, FOX_WIDTH))],
        out_specs=(_rows_rev(t, 2 * FOX_WIDTH, n), _rows_rev(t, FOX_WIDTH, n), _rows_rev(t, LANES, n),
                   _full((1, LANES)), _full((1, LANES)), _full((1, LANES))),
        scratch_shapes=[pltpu.VMEM((1, FOX_WIDTH), F32), pltpu.VMEM((1, FOX_WIDTH), F32), pltpu.VMEM((1, LANES), F32)],
        compiler_params=_params(),
    )(dqa, dka, dva, qk, fb, bf_pad, fq_g, fk_g)


def _assemble_dproj(dp_ref, dpa_ref, dqk_ref, dv_ref, dgb_ref, dpm_ref, dfb_ref):
    dp_ref[:, PA_LO:QB_LO] = dpa_ref[...]
    dp_ref[:, QB_LO:VB_LO] = dqk_ref[...]
    dp_ref[:, VB_LO:GB_LO] = dv_ref[...]
    dp_ref[:, GB_LO:PM_LO] = dgb_ref[...]
    dp_ref[:, PM_LO:FB_LO] = dpm_ref[...]
    dp_ref[:, FB_LO:PROJ_PAD] = dfb_ref[...].astype(BF16)


def _dproj_specs(t):
    return [_rows(t, 512), _rows(t, 2 * FOX_WIDTH), _rows(t, FOX_WIDTH), _rows(t, FOX_WIDTH), _rows(t, 512),
            _rows(t, LANES)]


IN_BWD_X_TILE = 256


def _in_bwd_x(x, dy, norm_g, wp, dparts, gparts, axes, smalls):
    s = x.shape[0]
    t = IN_BWD_X_TILE
    n = s // t
    na = len(gparts)
    n_dp = len(dparts)
    send_step = 4
    vec_leaves, loss_row, dw4 = smalls if smalls is not None else ((), None, None)
    nv = len(vec_leaves)
    n_small = nv + 2 if smalls is not None else 0

    def body(*refs):
        x_ref, dy_ref, g_ref, wp_ref = refs[0:4]
        dp_parts = refs[4:4 + n_dp]
        o = 4 + n_dp
        g_refs = refs[o:o + na]
        small_in = refs[o + na:o + na + n_small]
        o += na + n_small
        gx_ref, dg_ref = refs[o:o + 2]
        out_refs = refs[o + 2:o + 2 + na]
        small_out = refs[o + 2 + na:o + 2 + na + (2 if smalls is not None else 0)]
        o += 2 + na + len(small_out)
        dp_ref = refs[o]
        bufs = tuple(refs[o + 1 + k * na:o + 1 + (k + 1) * na] for k in range(5))
        rest = refs[o + 1 + 5 * na:]

        i = pl.program_id(0)
        if na or smalls is not None:
            send_sems, recv_sems, local_sems = rest[-3:]
        red = _ShardReduce(g_refs, out_refs, axes, bufs, send_sems, recv_sems, local_sems) if na else None

        @pl.when(i == 0)
        def _():
            dg_ref[...] = jnp.zeros_like(dg_ref)
            if red is not None:
                red.exchange_with_sibling()

        if red is not None:
            pl.when(i == send_step)(red.send_to_chips)

        _assemble_dproj(dp_ref, *dp_parts)
        dh = _dot(dp_ref[...], wp_ref[...])
        xv = x_ref[...]
        rr = lax.rsqrt(jnp.mean(xv * xv, axis=-1, keepdims=True) + EPS)
        xhat = xv * rr
        scaled = dh * g_ref[...]
        gx_ref[...] = dy_ref[...] + rr * (scaled - xhat * jnp.mean(xhat * scaled, axis=-1, keepdims=True))
        dg_ref[...] += jnp.sum(dh * xhat, axis=0, keepdims=True)

        def small_all_reduce():
            leaf_refs, (loss_ref, dw4_ref) = small_in[0:nv], small_in[nv:]
            vec_out, dw4_out = small_out
            vec_mine, vec_recv, dw4_recv = rest[0:3]
            cx, cy, c = _my_place()
            me_lin = 4 * cx + 2 * cy + c

            def copy(k, src, dst, base):
                peer = (me_lin + k) % 8
                return pltpu.make_async_remote_copy(
                    src_ref=src, dst_ref=dst.at[me_lin], send_sem=send_sems.at[base + k - 1],
                    recv_sem=recv_sems.at[base + k - 1], device_id=(peer // 4, (peer // 2) % 2, peer % 2),
                    device_id_type=MESH)

            vec_mine[...] = jnp.zeros_like(vec_mine)
            vec_mine[0:1, :] = dg_ref[...]
            for (_, row, _), ref in zip(VEC_LEAVES[1:], leaf_refs):
                vec_mine[row:row + 1, 0:ref.shape[1]] = ref[...]
            vec_mine[VEC_LOSS_ROW:VEC_LOSS_ROW + 1, 0:LANES] = loss_ref[...]
            copies = [copy(k, src, dst, base) for k in range(1, 8)
                      for src, dst, base in ((vec_mine, vec_recv, 5 * na), (dw4_ref, dw4_recv, 5 * na + 7))]
            for cp in copies:
                cp.start()
            for cp in copies:
                cp.wait_recv()
            vec_recv[me_lin] = vec_mine[...]
            dw4_recv[me_lin] = dw4_ref[...]
            vtot, wtot = vec_recv[0], dw4_recv[0]
            for d in range(1, 8):
                vtot = vtot + vec_recv[d]
                wtot = wtot + dw4_recv[d]
            vec_out[...] = vtot
            dw4_out[...] = wtot
            for cp in copies:
                cp.wait_send()

        @pl.when(i == n - 1)
        def _():
            if red is not None:
                red.sum_and_share()
            if smalls is not None:
                small_all_reduce()
            if red is not None:
                red.finish()

    any_spec = pl.BlockSpec(memory_space=pl.ANY)
    scratch = [pltpu.VMEM((t, PROJ_PAD), BF16)] + _ShardReduce.scratch(gparts, axes)
    out_shape = [jax.ShapeDtypeStruct((s, D_MODEL), F32), jax.ShapeDtypeStruct((1, D_MODEL), F32)]
    out_shape += [jax.ShapeDtypeStruct(g.shape[1:], F32) for g in gparts]
    out_specs = [_rows(t, D_MODEL), _full((1, D_MODEL))] + [any_spec] * na
    small_args = []
    if smalls is not None:
        small_args = [*vec_leaves, loss_row, dw4]
        out_shape += [jax.ShapeDtypeStruct((VEC_ROWS, D_MODEL), F32), jax.ShapeDtypeStruct(dw4.shape, F32)]
        out_specs += [_full((VEC_ROWS, D_MODEL)), _full(dw4.shape)]
        scratch += [pltpu.VMEM((VEC_ROWS, D_MODEL), F32), pltpu.VMEM((8, VEC_ROWS, D_MODEL), F32),
                    pltpu.VMEM((8,) + dw4.shape, F32)]
    if na or smalls is not None:
        n_sems = 5 * na + 14
        scratch += [pltpu.SemaphoreType.DMA((n_sems,)), pltpu.SemaphoreType.DMA((n_sems,)),
                    pltpu.SemaphoreType.DMA((max(2 * na, 1),))]
    return pl.pallas_call(
        body, name="in_bwd_x", grid=(n,), out_shape=tuple(out_shape),
        in_specs=[_rows(t, D_MODEL), _rows(t, D_MODEL), _full((1, D_MODEL)),
                  pl.BlockSpec((PROJ_PAD, D_MODEL), lambda i: (0, 0), pipeline_mode=pl.Buffered(1))]
        + _dproj_specs(t) + [any_spec] * na + [_full(a.shape) for a in small_args],
        out_specs=tuple(out_specs), scratch_shapes=scratch, compiler_params=_params(),
    )(x, dy, norm_g, wp, *dparts, *gparts, *small_args)


def _in_bwd_w(hb, dparts, gparts, axes):
    s = hb.shape[0]
    t = TILE
    n = s // t
    na = len(gparts)
    f_hi = F_ORIG_LO + FOX_HEADS
    send_step = 1

    def body(*refs):
        h_ref, dpa_ref, dqk_ref, dv_ref, dgb_ref, dpm_ref, dfb_ref = refs[0:7]
        g_refs = refs[7:7 + na]
        dw_ref = refs[7 + na]
        out_refs = refs[8 + na:8 + 2 * na]
        o = 8 + 2 * na
        bufs = tuple(refs[o + k * na:o + (k + 1) * na] for k in range(5))
        i = pl.program_id(0)
        red = _ShardReduce(g_refs, out_refs, axes, bufs, *refs[o + 5 * na:]) if na else None

        @pl.when(i == 0)
        def _():
            dw_ref[...] = jnp.zeros_like(dw_ref)
            if red is not None:
                red.exchange_with_sibling()

        if red is not None:
            pl.when(i == send_step)(red.send_to_chips)

        hv = h_ref[...]
        for lo, ref in ((0, dpa_ref), (QB_LO, dqk_ref), (VB_LO, dv_ref), (f_hi, dgb_ref), (f_hi + FOX_WIDTH, dpm_ref)):
            dw_ref[lo:lo + ref.shape[1], :] += _dot(ref[...], hv, TN)
        dw_ref[F_ORIG_LO:f_hi, :] += _dot(dfb_ref[...].astype(BF16), hv, TN)[0:FOX_HEADS, :]

        if red is not None:
            @pl.when(i == n - 1)
            def _():
                red.sum_and_share()
                red.finish()

    any_spec = pl.BlockSpec(memory_space=pl.ANY)
    scratch = _ShardReduce.scratch(gparts, axes)
    if na:
        scratch += [pltpu.SemaphoreType.DMA((5 * na,)), pltpu.SemaphoreType.DMA((5 * na,)),
                    pltpu.SemaphoreType.DMA((2 * na,))]
    return pl.pallas_call(
        body, name="in_bwd_w", grid=(n,),
        out_shape=(jax.ShapeDtypeStruct((IN_WIDTH, D_MODEL), F32),)
        + tuple(jax.ShapeDtypeStruct(g.shape[1:], F32) for g in gparts),
        in_specs=[_rows(t, D_MODEL)] + _dproj_specs(t) + [any_spec] * na,
        out_specs=(_full((IN_WIDTH, D_MODEL)),) + (any_spec,) * na,
        scratch_shapes=scratch, compiler_params=_params(),
    )(hb, *dparts, *gparts)


def _adamw_math(w_ref, gv, m_ref, v_ref, d_ref, nm_ref, nv_ref):
    nm = ADAM_B1 * m_ref[...] + (1.0 - ADAM_B1) * gv
    nv = ADAM_B2 * v_ref[...] + (1.0 - ADAM_B2) * (gv * gv)
    m_hat = nm / (1.0 - ADAM_B1 ** ADAM_STEP)
    v_hat = nv / (1.0 - ADAM_B2 ** ADAM_STEP)
    d_ref[...] = -ADAM_LR * (m_hat / (jnp.sqrt(v_hat) + ADAM_EPS) + ADAM_WD * w_ref[...])
    nm_ref[...] = nm
    nv_ref[...] = nv


def _adamw(name, w, g, m, v):
    rows, cols = w.shape
    tc = 256 if rows * cols > 256 * 1024 else cols
    n = cols // tc

    def body(w_ref, g_ref, m_ref, v_ref, d_ref, nm_ref, nv_ref):
        _adamw_math(w_ref, g_ref[...], m_ref, v_ref, d_ref, nm_ref, nv_ref)

    spec = pl.BlockSpec((rows, tc), lambda i: (0, i))
    return pl.pallas_call(
        body, name=name, grid=(n,),
        out_shape=(jax.ShapeDtypeStruct((rows, cols), F32),) * 3,
        in_specs=[spec] * 4, out_specs=(spec,) * 3,
        compiler_params=_params(),
    )(w, g, m, v)


def _adamw_small(vec, dw4, leaves, pool):
    nl = len(VEC_LEAVES) + 1

    def body(*refs):
        vec_ref, dw4_ref = refs[0:2]
        wmv = refs[2:2 + 3 * nl]
        loss_ref = refs[2 + 3 * nl]
        outs = refs[3 + 3 * nl:]
        loss_ref[...] = vec_ref[VEC_LOSS_ROW:VEC_LOSS_ROW + 1, 0:1]
        for k in range(nl):
            if k < nl - 1:
                _, row, width = VEC_LEAVES[k]
                gv = vec_ref[row:row + 1, 0:width]
            else:
                gv = dw4_ref[...]
            w_ref, m_ref, v_ref = wmv[3 * k:3 * k + 3]
            g_ref, d_ref, nm_ref, nv_ref = outs[4 * k:4 * k + 4]
            g_ref[...] = gv
            _adamw_math(w_ref, gv, m_ref, v_ref, d_ref, nm_ref, nv_ref)

    shapes = [jax.ShapeDtypeStruct((1, width), F32) for _, _, width in VEC_LEAVES] + [
        jax.ShapeDtypeStruct(dw4.shape, F32)]
    flat_in = [a for triple in list(leaves) + [pool] for a in triple]
    res = pl.pallas_call(
        body, name="adamw_small",
        out_shape=(jax.ShapeDtypeStruct((1, 1), F32),) + tuple(s for s in shapes for _ in range(4)),
        compiler_params=pltpu.CompilerParams(vmem_limit_bytes=VMEM_LIMIT),
    )(vec, dw4, *flat_in)
    per = [res[1 + 4 * k:5 + 4 * k] for k in range(nl)]
    return res[0], [p[0] for p in per], [p[1] for p in per], [p[2] for p in per], [p[3] for p in per]


def _full_w_in_padded(halves):
    cols = IN_WIDTH // 4
    w_t = halves.reshape(4, 2, cols, D_MODEL // 2).transpose(0, 2, 1, 3).reshape(IN_WIDTH, D_MODEL)
    return jnp.concatenate([
        w_t[0:F_ORIG_LO], w_t[F_ORIG_LO + FOX_HEADS:], w_t[F_ORIG_LO:F_ORIG_LO + FOX_HEADS],
        jnp.zeros((PROJ_PAD - IN_WIDTH, D_MODEL), w_t.dtype)], axis=0)


def _tile_heads(g, n):
    return jnp.tile(g.reshape(1, HEAD_DIM), (1, n))


def kernel(x, mem, norm_g, w_in, b_f, w_pool, pool_scale, fox_q_g, fox_k_g, mem_norm_g, w_mem_kv, mem_q_g, mem_k_g, w_out, loss_target, m_norm_g, m_w_in, m_b_f, m_w_pool, m_pool_scale, m_fox_q_g, m_fox_k_g, m_mem_norm_g, m_w_mem_kv, m_mem_q_g, m_mem_k_g, m_w_out, v_norm_g, v_w_in, v_b_f, v_w_pool, v_pool_scale, v_fox_q_g, v_fox_k_g, v_mem_norm_g, v_w_mem_kv, v_mem_q_g, v_mem_k_g, v_w_out):
    w_in_t, m_w_in_t, v_w_in_t = w_in[0].T, m_w_in[0].T, v_w_in[0].T
    axes = (1, 0, 0)

    g_in, g_kv, g_out = _all_gather_weights([w_in_t, w_mem_kv[0], w_out[0]], axes)
    wp = _full_w_in_padded(g_in)
    tiled = _tiled_params(b_f, fox_q_g, fox_k_g, mem_q_g, mem_k_g)
    fwd = _fwd_in(x[0], norm_g, wp, *tiled[0:3])
    w_kv_b = g_kv.reshape(D_MODEL, 2 * MEM_WIDTH)
    w_out_b = g_out.reshape(D_MODEL, D_MODEL)
    w4 = w_pool.reshape(POOL_ROWS, HEAD_DIM)
    dy, hb, dparts, dw_kv, dw_out, vec_leaves, loss_row, dw4 = _local_partials(
        x[0], mem[0], loss_target[0], fwd, w_kv_b, w_out_b, tiled, w4, pool_scale, mem_norm_g)

    early = [dw_kv.reshape(4, D_MODEL // 4, 2 * MEM_WIDTH), dw_out.reshape(4, D_MODEL // 4, D_MODEL)]
    dwp, g_w_kv, g_w_out = _in_bwd_w(hb, dparts, early, axes[1:])
    grad_x, _, g_w_in_t, vec, dw4_sum = _in_bwd_x(
        x[0], dy, norm_g, wp, dparts, [dwp.reshape(4, IN_WIDTH // 4, D_MODEL)], axes[0:1], (vec_leaves, loss_row, dw4))

    small_wmv = [(norm_g, m_norm_g, v_norm_g), (mem_norm_g, m_mem_norm_g, v_mem_norm_g),
                 (pool_scale, m_pool_scale, v_pool_scale), (b_f, m_b_f, v_b_f), (fox_q_g, m_fox_q_g, v_fox_q_g),
                 (fox_k_g, m_fox_k_g, v_fox_k_g), (mem_q_g, m_mem_q_g, v_mem_q_g), (mem_k_g, m_mem_k_g, v_mem_k_g)]
    pool_wmv = tuple(a.reshape(POOL_ROWS, HEAD_DIM) for a in (w_pool, m_w_pool, v_w_pool))
    loss, *small_out = _adamw_small(vec, dw4_sum, small_wmv, pool_wmv)
    big = [[g_w_in_t.T[None], g_w_kv[None], g_w_out[None]]]
    upd = [[a.T for a in _adamw("adamw_w_in", w_in_t, g_w_in_t, m_w_in_t, v_w_in_t)],
           _adamw("adamw_w_mem_kv", w_mem_kv[0], g_w_kv, m_w_mem_kv[0], v_w_mem_kv[0]),
           _adamw("adamw_w_out", w_out[0], g_w_out, m_w_out[0], v_w_out[0])]
    big += [[u[k][None] for u in upd] for k in range(3)]

    def leaves(k):
        sm = small_out[k]
        b_in, b_kv, b_out = big[k]
        return (sm[0], b_in, sm[3], sm[8].reshape(w_pool.shape), sm[2], sm[4], sm[5], sm[1], b_kv, sm[6], sm[7], b_out)

    return (loss.reshape(()), grad_x[None], *leaves(0), *leaves(1), *leaves(2), *leaves(3))


def _tiled_params(b_f, fox_q_g, fox_k_g, mem_q_g, mem_k_g):
    return (jnp.pad(b_f, ((0, 0), (0, LANES - FOX_HEADS))), _tile_heads(fox_q_g, FOX_HEADS),
            _tile_heads(fox_k_g, FOX_HEADS), _tile_heads(mem_q_g, 4), _tile_heads(mem_k_g, 4))


def _local_partials(xs, mems, tgt, fwd, w_kv_b, w_out_b, tiled, w4, pool_scale, mem_norm_g):
    hb, pa, qk, qa, ka, va, gb, pm, fb = fwd
    bf_pad, fq_g, fk_g, mq_g, mk_g = tiled

    mnb, kv, kmn, vmb = _mem_fwd(mems, mem_norm_g, w_kv_b, mk_g)
    ma, db = _pool_fwd(pa, w4, pool_scale)
    mm = _mem_attn_fwd(pm, kmn, vmb, mq_g)
    o, mb, r4 = _fox_fwd(qa, ka, va, gb)
    dy, dma, dmb, dmm, dw_out, loss_row = _out_loss(xs, tgt, ma, mb, mm, w_out_b)

    dpm, dkmn, dvm, dmq_g = _mem_attn_bwd(pm, dmm, kmn, vmb, mq_g)
    dw_kv, dmemnorm_g, dmk_g = _mem_bwd(dkmn, dvm, kv, mnb, mems, w_kv_b, mk_g, mem_norm_g)
    dpa, dw4, dpscale = _pool_bwd(pa, db, dma, w4, pool_scale)
    doa, dgb, rr = _fox_prep(dmb, gb, o, r4)
    dka, dva, dqa = _fox_bwd(ka, va, qa, doa, rr)
    dqk, dvb, dfb, dfq_g, dfk_g, dbf = _fox_post(dqa, dka, dva, qk, fb, bf_pad, fq_g, fk_g)
    dparts = (dpa, dqk, dvb, dgb, dpm, dfb)
    leaves = (dmemnorm_g, dpscale, dbf, dfq_g, dfk_g, dmq_g, dmk_g)
    return dy, hb, dparts, dw_kv, dw_out, leaves, loss_row, dw4
```

```python
import functools

import jax
import jax.numpy as jnp
from jax import lax
from jax.experimental import pallas as pl
from jax.experimental.pallas import tpu as pltpu

F32 = jnp.float32
BF16 = jnp.bfloat16
MESH = pl.DeviceIdType.MESH

D_MODEL = 1024
HEAD_DIM = 64
POOL_WIDTH = 256
FOX_WIDTH = 512
FOX_HEADS = 8
MEM_WIDTH = 256
N_MEM = 256
IN_WIDTH = 3080
EPS = 1e-6
ATT_SCALE = 0.125

ADAM_LR = 0.001
ADAM_B1 = 0.9
ADAM_B2 = 0.999
ADAM_EPS = 1e-08
ADAM_WD = 0.01
ADAM_STEP = 10

LANES = 128
PA_LO, QB_LO, KB_LO, VB_LO, GB_LO, PM_LO, FB_LO, PROJ_PAD = 0, 512, 1024, 1536, 2048, 2560, 3072, 3200
F_ORIG_LO = 2048

TILE = 512
VMEM_LIMIT = 56 * 1024 * 1024

VEC_LEAVES = (("norm_g", 0, 1024), ("mem_norm_g", 1, 1024), ("pool_scale", 2, 256), ("b_f", 3, 8),
              ("fox_q_g", 4, 64), ("fox_k_g", 5, 64), ("mem_q_g", 6, 64), ("mem_k_g", 7, 64))
VEC_LOSS_ROW = 8
VEC_ROWS = 16
POOL_ROWS = 256


def _params(n_grid=1, vmem=VMEM_LIMIT):
    return pltpu.CompilerParams(dimension_semantics=("arbitrary",) * n_grid, vmem_limit_bytes=vmem)


def _rows(t, w):
    return pl.BlockSpec((t, w), lambda i: (i, 0))


def _rows_rev(t, w, n):
    return pl.BlockSpec((t, w), lambda i: (n - 1 - i, 0))


def _full(shape):
    return pl.BlockSpec(shape, lambda i: (0,) * len(shape))


def _sig(x):
    return 1.0 / (1.0 + jnp.exp(-x))


def _lane_lo(shape):
    return lax.broadcasted_iota(jnp.int32, shape, 1) < HEAD_DIM


def _pair_sum(v, lo):
    s0 = jnp.sum(jnp.where(lo, v, 0.0), axis=-1, keepdims=True)
    s1 = jnp.sum(jnp.where(lo, 0.0, v), axis=-1, keepdims=True)
    return jnp.where(lo, s0, s1)


def _head_rms(blk, lo):
    return lax.rsqrt(_pair_sum(blk * blk, lo) * (1.0 / HEAD_DIM) + EPS)


def _head_norm_bwd(dyn, xhat, rr, g, lo):
    a = dyn * g
    return rr * (a - xhat * (_pair_sum(xhat * a, lo) * (1.0 / HEAD_DIM)))


def _fold_heads(acc):
    tot = acc[:, 0:LANES]
    for p in range(1, acc.shape[1] // LANES):
        tot = tot + acc[:, p * LANES:(p + 1) * LANES]
    return tot + pltpu.roll(tot, HEAD_DIM, axis=1)


def _lane_pick(v, lane, idx):
    return jnp.sum(jnp.where(lane == idx, v, 0.0), axis=-1, keepdims=True)


NT = (((1,), (1,)), ((), ()))
TN = (((0,), (0,)), ((), ()))


def _dot(a, b, dims=None):
    if dims is None:
        return jnp.dot(a, b, preferred_element_type=F32)
    return lax.dot_general(a, b, dims, preferred_element_type=F32)


def _my_place():
    return lax.axis_index("x"), lax.axis_index("y"), lax.axis_index("c")


def _half_dims(shape, axis):
    return (shape[0] // 2, shape[1]) if axis == 0 else (shape[0], shape[1] // 2)


def _half_of(ref, axis, core, lead=False):
    rows, cols = ref.shape[-2:]
    if axis == 0:
        idx = (pl.ds(pl.multiple_of(core * (rows // 2), 16), rows // 2), slice(None))
    else:
        idx = (slice(None), pl.ds(pl.multiple_of(core * (cols // 2), LANES), cols // 2))
    return ref.at[(slice(None),) + idx] if lead else ref.at[idx]


class _HalfGather:
    def __init__(self, ins, outs, axes, f32_bufs, bf_bufs, send_sems, recv_sems, local_sems):
        self.ins, self.outs, self.axes = ins, outs, axes
        self.f32_bufs, self.bf_bufs = f32_bufs, bf_bufs
        self.send_sems, self.recv_sems, self.local_sems = send_sems, recv_sems, local_sems
        self.n = len(ins)
        x, y, self.c = _my_place()
        self.me, self.sibling = (x, y, self.c), (x, y, 1 - self.c)
        self.chips = [(1 - x, y), (x, 1 - y), (1 - x, 1 - y)]

    @staticmethod
    def scratch(shards, axes):
        dims = [_half_dims(a.shape, axis) for a, axis in zip(shards, axes)]
        n = len(shards)
        return [pltpu.VMEM(d, F32) for d in dims] + [pltpu.VMEM(d, BF16) for d in dims] + [
            pltpu.SemaphoreType.DMA((7 * n,)), pltpu.SemaphoreType.DMA((7 * n,)), pltpu.SemaphoreType.DMA((2 * n,))]

    @staticmethod
    def out_shapes(shards, axes):
        return tuple(jax.ShapeDtypeStruct((8,) + _half_dims(a.shape, axis), BF16) for a, axis in zip(shards, axes))

    def _blk(self, a, px, py, pc):
        return self.outs[a].at[4 * px + 2 * py + pc]

    def _copy(self, a, k, block, to, src=None):
        return pltpu.make_async_remote_copy(
            src_ref=self._blk(a, *block) if src is None else src, dst_ref=self._blk(a, *block),
            send_sem=self.send_sems.at[7 * a + k], recv_sem=self.recv_sems.at[7 * a + k], device_id=to,
            device_id_type=MESH)

    def _keep(self, a):
        return pltpu.make_async_copy(self.bf_bufs[a], self._blk(a, *self.me), self.local_sems.at[self.n + a])

    def _first(self, a):
        mine = [self._copy(a, 0, self.me, self.sibling, src=self.bf_bufs[a])]
        return mine + [self._copy(a, 1 + j, self.me, (*chip, self.c), src=self.bf_bufs[a])
                       for j, chip in enumerate(self.chips)]

    def send_mine(self):
        loads = [pltpu.make_async_copy(_half_of(self.ins[a], self.axes[a], self.c), self.f32_bufs[a],
                                       self.local_sems.at[a]) for a in range(self.n)]
        for cp in loads:
            cp.start()
        for a in range(self.n):
            loads[a].wait()
            self.bf_bufs[a][...] = self.f32_bufs[a][...].astype(BF16)
            self._keep(a).start()
            for cp in self._first(a):
                cp.start()

    def pass_on(self):
        for a in range(self.n):
            for j, chip in enumerate(self.chips):
                self._copy(a, 1 + j, (*chip, self.c), self.me).wait_recv()
                self._copy(a, 4 + j, (*chip, self.c), self.sibling).start()

    def finish(self):
        for a in range(self.n):
            self._copy(a, 0, self.sibling, self.me).wait_recv()
            for j, chip in enumerate(self.chips):
                self._copy(a, 4 + j, (*chip, 1 - self.c), self.me).wait_recv()
        for a in range(self.n):
            for cp in self._first(a):
                cp.wait_send()
            for j, chip in enumerate(self.chips):
                self._copy(a, 4 + j, (*chip, self.c), self.sibling).wait_send()
            self._keep(a).wait()


def _all_gather_weights(shards, axes):
    n = len(shards)

    def body(*refs):
        gather = _HalfGather(refs[0:n], refs[n:2 * n], axes, refs[2 * n:3 * n], refs[3 * n:4 * n], *refs[4 * n:])
        gather.send_mine()
        gather.pass_on()
        gather.finish()

    any_spec = pl.BlockSpec(memory_space=pl.ANY)
    return pl.pallas_call(
        body, name="weights_all_gather", out_shape=_HalfGather.out_shapes(shards, axes),
        in_specs=[any_spec] * n, out_specs=(any_spec,) * n, scratch_shapes=_HalfGather.scratch(shards, axes),
        compiler_params=pltpu.CompilerParams(vmem_limit_bytes=VMEM_LIMIT),
    )(*shards)


class _ShardReduce:
    SEMS = 8
    LOCAL = 5

    def __init__(self, g_refs, out_refs, axes, bufs, send_sems, recv_sems, local_sems):
        self.g_refs, self.out_refs, self.axes = g_refs, out_refs, axes
        self.recv_a, self.own_a, self.send_b, self.recv_b, self.fin = bufs
        self.send_sems, self.recv_sems, self.local_sems = send_sems, recv_sems, local_sems
        self.n = len(g_refs)
        x, y, self.c = _my_place()
        self.chip = 2 * x + y
        self.sibling = (x, y, 1 - self.c)

    @staticmethod
    def scratch(gparts, axes):
        dims = [_half_dims(g.shape[1:], axis) for g, axis in zip(gparts, axes)]
        shapes = []
        for dtype, lead in ((F32, (4,)), (F32, (4,)), (BF16, (4,)), (BF16, (4,)), (F32, ())):
            shapes += [pltpu.VMEM(lead + d, dtype) for d in dims]
        return shapes

    def _to_sibling(self, a, j):
        return pltpu.make_async_remote_copy(
            src_ref=_half_of(self.g_refs[a].at[j], self.axes[a], 1 - self.c), dst_ref=self.recv_a[a].at[j],
            send_sem=self.send_sems.at[self.SEMS * a + j], recv_sem=self.recv_sems.at[self.SEMS * a + j], device_id=self.sibling,
            device_id_type=MESH)

    def _own(self, a, j):
        return pltpu.make_async_copy(_half_of(self.g_refs[a].at[j], self.axes[a], self.c), self.own_a[a].at[j],
                                     self.local_sems.at[self.LOCAL * a + j])

    def _to_chip(self, a, k):
        dest = (self.chip + k) % 4
        return pltpu.make_async_remote_copy(
            src_ref=self.send_b[a].at[dest], dst_ref=self.recv_b[a].at[self.chip],
            send_sem=self.send_sems.at[self.SEMS * a + 3 + k], recv_sem=self.recv_sems.at[self.SEMS * a + 3 + k],
            device_id=(dest // 2, dest % 2, self.c), device_id_type=MESH)

    def _give(self, a):
        return pltpu.make_async_remote_copy(
            src_ref=self.fin[a], dst_ref=_half_of(self.out_refs[a], self.axes[a], self.c),
            send_sem=self.send_sems.at[self.SEMS * a + 7], recv_sem=self.recv_sems.at[self.SEMS * a + 7], device_id=self.sibling,
            device_id_type=MESH)

    def _mine(self, a):
        return pltpu.make_async_copy(self.fin[a], _half_of(self.out_refs[a], self.axes[a], self.c),
                                     self.local_sems.at[self.LOCAL * a])

    def exchange_with_sibling(self):
        for k in (1, 2, 3, 0):
            j = (self.chip + k) % 4
            for a in range(self.n):
                self._to_sibling(a, j).start()
                self._own(a, j).start()

    def _chip_partial(self, a, j):
        self._own(a, j).wait()
        self._to_sibling(a, j).wait_recv()
        self.send_b[a][j] = (self.own_a[a][j] + self.recv_a[a][j]).astype(BF16)

    def send_to_chip(self, k):
        for a in range(self.n):
            self._chip_partial(a, (self.chip + k) % 4)
            self._to_chip(a, k).start()

    def keep_mine(self):
        for a in range(self.n):
            self._chip_partial(a, self.chip)
            keep = pltpu.make_async_copy(self.send_b[a].at[self.chip], self.recv_b[a].at[self.chip],
                                         self.local_sems.at[self.LOCAL * a + 4])
            keep.start()
            keep.wait()

    def sum_and_share(self):
        for a in range(self.n):
            for k in range(1, 4):
                self._to_chip(a, k).wait_recv()
            tot = self.recv_b[a][0].astype(F32) + self.recv_b[a][1].astype(F32)
            tot = tot + self.recv_b[a][2].astype(F32)
            self.fin[a][...] = tot + self.recv_b[a][3].astype(F32)
            self._give(a).start()
            self._mine(a).start()

    def finish(self):
        for a in range(self.n):
            self._give(a).wait_recv()
            self._mine(a).wait()
            self._give(a).wait_send()
            for j in range(4):
                self._to_sibling(a, j).wait_send()
            for k in range(1, 4):
                self._to_chip(a, k).wait_send()


def _mem_fwd(mem, mem_norm_g, w_kv, mk_g):
    n = mem.shape[0]

    def body(mem_ref, g_ref, w_ref, kg_ref, mn_ref, kv_ref, kn_ref, vm_ref):
        xm = mem_ref[...]
        rr = lax.rsqrt(jnp.mean(xm * xm, axis=-1, keepdims=True) + EPS)
        mnb = ((xm * rr) * g_ref[...]).astype(BF16)
        mn_ref[...] = mnb
        kv = _dot(mnb, w_ref[...])
        kv_ref[...] = kv
        lo = _lane_lo((n, LANES))
        for p in range(MEM_WIDTH // LANES):
            sl = slice(p * LANES, (p + 1) * LANES)
            kb = kv[:, sl]
            kn_ref[:, sl] = ((kb * _head_rms(kb, lo)) * kg_ref[:, sl]).astype(BF16)
        vm_ref[...] = kv[:, MEM_WIDTH:].astype(BF16)

    return pl.pallas_call(
        body, name="mem_fwd",
        out_shape=(jax.ShapeDtypeStruct((n, D_MODEL), BF16), jax.ShapeDtypeStruct((n, 2 * MEM_WIDTH), F32),
                   jax.ShapeDtypeStruct((n, MEM_WIDTH), BF16), jax.ShapeDtypeStruct((n, MEM_WIDTH), BF16)),
        compiler_params=pltpu.CompilerParams(vmem_limit_bytes=VMEM_LIMIT),
    )(mem, mem_norm_g, w_kv, mk_g)


AUG_LO = 64
KEY_SUM_LANE = 72
QUERY_SUM_LANE = 80
HEAD_BLOCKS = FOX_HEADS * LANES


def _ones3(lane):
    return jnp.where((lane >= AUG_LO) & (lane < AUG_LO + 3), 1.0, 0.0)


def _spread3(cols):
    hi = cols.astype(BF16)
    rest = cols - hi.astype(F32)
    mid = rest.astype(BF16)
    low = (rest - mid.astype(F32)).astype(BF16)
    r = lax.broadcasted_iota(jnp.int32, (LANES, HEAD_BLOCKS), 0)
    c = lax.broadcasted_iota(jnp.int32, (LANES, HEAD_BLOCKS), 1)
    out = None
    for k, part in enumerate((hi, mid, low)):
        term = _dot(part, jnp.where(c == r * LANES + (AUG_LO + k), 1.0, 0.0).astype(BF16))
        out = term if out is None else out + term
    return out


def _head_block(pair_blk, hh, lo, extras):
    src = pair_blk if hh == 0 else pltpu.roll(pair_blk, HEAD_DIM, axis=1)
    return jnp.where(lo, src, extras).astype(BF16)


def _pair_block(blk0, blk1, lo):
    return jnp.where(lo, blk0, pltpu.roll(blk1, HEAD_DIM, axis=1))


def _fwd_in(x, norm_g, wp, bf_pad, fq_g, fk_g):
    s = x.shape[0]
    t = TILE
    n = s // t

    def body(x_ref, ng_ref, wp_ref, bf_ref, qg_ref, kg_ref,
             h_ref, pa_ref, qk_ref, qa_ref, ka_ref, va_ref, gb_ref, pm_ref, fb_ref, carry_ref, fcol_ref):
        @pl.when(pl.program_id(0) == 0)
        def _():
            carry_ref[...] = jnp.zeros_like(carry_ref)

        xv = x_ref[...]
        rr = lax.rsqrt(jnp.mean(xv * xv, axis=-1, keepdims=True) + EPS)
        hb = ((xv * rr) * ng_ref[...]).astype(BF16)
        h_ref[...] = hb

        def proj(lo, hi):
            return _dot(hb, wp_ref[lo:hi, :], NT)

        pa_ref[...] = proj(PA_LO, QB_LO)
        gb_ref[...] = proj(GB_LO, PM_LO)
        pm_ref[...] = proj(PM_LO, FB_LO)
        fb = proj(FB_LO, PROJ_PAD)
        fb_ref[...] = fb

        lane = lax.broadcasted_iota(jnp.int32, (t, LANES), 1)
        row = lax.broadcasted_iota(jnp.int32, (t, LANES), 0)
        lo = lane < HEAD_DIM
        z = fb + bf_ref[...]
        lf = -(jnp.maximum(-z, 0.0) + jnp.log1p(jnp.exp(-jnp.abs(z))))
        lf = jnp.where(lane < FOX_HEADS, lf, 0.0)
        sh = 1
        while sh < t:
            lf = lf + jnp.where(row >= sh, pltpu.roll(lf, sh, axis=0), 0.0)
            sh *= 2
        fcum = lf + carry_ref[...]
        fcol_ref[...] = fcum
        carry_ref[...] = fcol_ref[t - 1:t, :]

        ones3 = _ones3(lane)
        minus_f = _spread3(-fcum)
        for seg, g_ref, out_ref, scale in ((QB_LO, qg_ref, qa_ref, ATT_SCALE), (KB_LO, kg_ref, ka_ref, 1.0)):
            raw = proj(seg, seg + FOX_WIDTH)
            qk_ref[:, seg - QB_LO:seg - QB_LO + FOX_WIDTH] = raw
            for p in range(FOX_WIDTH // LANES):
                sl = slice(p * LANES, (p + 1) * LANES)
                blk = raw[:, sl]
                normed = ((blk * _head_rms(blk, lo)) * g_ref[:, sl]) * scale
                for hh in range(2):
                    h = 2 * p + hh
                    if seg == QB_LO:
                        extras = jnp.where(lane == QUERY_SUM_LANE + h, 1.0, ones3)
                    else:
                        extras = jnp.where(lane == KEY_SUM_LANE + h, 1.0, minus_f[:, h * LANES:(h + 1) * LANES])
                    out_ref[:, h * LANES:(h + 1) * LANES] = _head_block(normed, hh, lo, extras)
        vraw = proj(VB_LO, GB_LO)
        for h in range(FOX_HEADS):
            va_ref[:, h * LANES:(h + 1) * LANES] = _head_block(vraw[:, (h // 2) * LANES:(h // 2 + 1) * LANES], h % 2, lo, ones3)

    outs = (
        jax.ShapeDtypeStruct((s, D_MODEL), BF16),
        jax.ShapeDtypeStruct((s, 512), F32),
        jax.ShapeDtypeStruct((s, 2 * FOX_WIDTH), F32),
        jax.ShapeDtypeStruct((s, HEAD_BLOCKS), BF16),
        jax.ShapeDtypeStruct((s, HEAD_BLOCKS), BF16),
        jax.ShapeDtypeStruct((s, HEAD_BLOCKS), BF16),
        jax.ShapeDtypeStruct((s, FOX_WIDTH), F32),
        jax.ShapeDtypeStruct((s, 512), F32),
        jax.ShapeDtypeStruct((s, LANES), F32),
    )
    return pl.pallas_call(
        body, name="fwd_in", grid=(n,), out_shape=outs,
        in_specs=[_rows(t, D_MODEL), _full((1, D_MODEL)), _full((PROJ_PAD, D_MODEL)), _full((1, LANES)),
                  _full((1, FOX_WIDTH)), _full((1, FOX_WIDTH))],
        out_specs=(_rows(t, D_MODEL), _rows(t, 512), _rows(t, 2 * FOX_WIDTH), _rows(t, HEAD_BLOCKS),
                   _rows(t, HEAD_BLOCKS), _rows(t, HEAD_BLOCKS), _rows(t, FOX_WIDTH), _rows(t, 512),
                   _rows(t, LANES)),
        scratch_shapes=[pltpu.VMEM((1, LANES), F32), pltpu.VMEM((t, LANES), F32)],
        compiler_params=_params(),
    )(x, norm_g, wp, bf_pad, fq_g, fk_g)


POOL_HALO = 16


def _pool_window(lane):
    return jnp.where(lane < 64, 2.0, jnp.where(lane < 128, 4.0, jnp.where(lane < 192, 8.0, 16.0)))


def _pool_pick(lane, s2, s4, s8, s16):
    return jnp.where(lane < 64, s2, jnp.where(lane < 128, s4, jnp.where(lane < 192, s8, s16)))


def _group_onehot(shape, row_is_group_lane):
    r = lax.broadcasted_iota(jnp.int32, shape, 0)
    c = lax.broadcasted_iota(jnp.int32, shape, 1)
    hit = (r % HEAD_DIM == c) if row_is_group_lane else (c % HEAD_DIM == r)
    return jnp.where(hit, 1.0, 0.0).astype(F32)


def _same_group(shape):
    r = lax.broadcasted_iota(jnp.int32, shape, 0)
    c = lax.broadcasted_iota(jnp.int32, shape, 1)
    return (r // HEAD_DIM) == (c // HEAD_DIM)


def _pool_block_diag(w4):
    spread = jnp.dot(w4, _group_onehot((HEAD_DIM, POOL_WIDTH), False), preferred_element_type=F32,
                     precision=lax.Precision.HIGHEST)
    return jnp.where(_same_group((POOL_WIDTH, POOL_WIDTH)), spread, 0.0).astype(BF16)


def _pool_fwd(pa, w4, pscale):
    s = pa.shape[0]
    t = TILE
    n = s // t
    ext = t + POOL_HALO

    def body(pa_ref, w4_ref, sc_ref, ma_ref, d_ref, ext_ref, w_ref):
        i = pl.program_id(0)

        @pl.when(i == 0)
        def _():
            ext_ref[0:POOL_HALO, :] = jnp.zeros((POOL_HALO, POOL_WIDTH), F32)
            w_ref[...] = _pool_block_diag(w4_ref[...])

        u = pa_ref[:, 0:POOL_WIDTH]
        ext_ref[POOL_HALO:ext, :] = u
        e = ext_ref[...]
        s2 = e + pltpu.roll(e, 1, axis=0)
        s4 = s2 + pltpu.roll(s2, 2, axis=0)
        s8 = s4 + pltpu.roll(s4, 4, axis=0)
        s16 = s8 + pltpu.roll(s8, 8, axis=0)
        lane_e = lax.broadcasted_iota(jnp.int32, (ext, POOL_WIDTH), 1)
        win = _pool_pick(lane_e, s2, s4, s8, s16)[POOL_HALO:ext, :]
        lane = lax.broadcasted_iota(jnp.int32, (t, POOL_WIDTH), 1)
        pos = (lax.broadcasted_iota(jnp.int32, (t, POOL_WIDTH), 0) + (i * t + 1)).astype(F32)
        d = win / jnp.minimum(pos, _pool_window(lane)) - u
        db = d.astype(BF16)
        d_ref[...] = db
        ya = _dot(db, w_ref[...]) * sc_ref[...]
        ga = pa_ref[:, POOL_WIDTH:2 * POOL_WIDTH]
        ma_ref[...] = (ya * (ga * _sig(ga))).astype(BF16)
        ext_ref[0:POOL_HALO, :] = ext_ref[t:ext, :]

    return pl.pallas_call(
        body, name="pool_fwd", grid=(n,),
        out_shape=(jax.ShapeDtypeStruct((s, POOL_WIDTH), BF16), jax.ShapeDtypeStruct((s, POOL_WIDTH), BF16)),
        in_specs=[_rows(t, 512), _full((POOL_ROWS, HEAD_DIM)), _full((1, POOL_WIDTH))],
        out_specs=(_rows(t, POOL_WIDTH), _rows(t, POOL_WIDTH)),
        scratch_shapes=[pltpu.VMEM((ext, POOL_WIDTH), F32), pltpu.VMEM((POOL_WIDTH, POOL_WIDTH), BF16)],
        compiler_params=_params(),
    )(pa, w4, pscale)


def _mem_softmax(qm, kp):
    sc = _dot(qm, kp, NT)
    e = jnp.exp(sc - jnp.max(sc, axis=-1, keepdims=True))
    return e * (1.0 / jnp.sum(e, axis=-1, keepdims=True))


def _mem_attn_fwd(pm, kmn, vmb, mq_g):
    s = pm.shape[0]
    t = TILE
    n = s // t

    def body(pm_ref, k_ref, v_ref, g_ref, mm_ref):
        lo = _lane_lo((t, LANES))
        for p in range(MEM_WIDTH // LANES):
            sl = slice(p * LANES, (p + 1) * LANES)
            qb = pm_ref[:, sl]
            qs = (((qb * _head_rms(qb, lo)) * g_ref[:, sl]) * ATT_SCALE).astype(BF16)
            kp = k_ref[:, sl]
            vp = v_ref[:, sl]
            outs = []
            for hh in range(2):
                msk = lo if hh == 0 else jnp.logical_not(lo)
                prob = _mem_softmax(jnp.where(msk, qs, jnp.zeros_like(qs)), kp)
                outs.append(_dot(prob.astype(BF16), vp))
            o = jnp.where(lo, outs[0], outs[1])
            gm = pm_ref[:, MEM_WIDTH + p * LANES:MEM_WIDTH + (p + 1) * LANES]
            mm_ref[:, sl] = (o * (gm * _sig(gm))).astype(BF16)

    return pl.pallas_call(
        body, name="mem_attn_fwd", grid=(n,),
        out_shape=jax.ShapeDtypeStruct((s, MEM_WIDTH), BF16),
        in_specs=[_rows(t, 512), _full((N_MEM, MEM_WIDTH)), _full((N_MEM, MEM_WIDTH)), _full((1, MEM_WIDTH))],
        out_specs=_rows(t, MEM_WIDTH),
        compiler_params=_params(),
    )(pm, kmn, vmb, mq_g)


FOX_FWD_HEADS = 4


def _fox_fwd(qa, ka, va, gb):
    s = qa.shape[0]
    t = TILE
    n = s // t
    heads = FOX_FWD_HEADS
    pairs = heads // 2
    group_w = heads * LANES

    def body(qa_ref, ka_ref, va_ref, gb_ref, o_ref, mb_ref, r_ref):
        i = pl.program_id(1)
        lane = lax.broadcasted_iota(jnp.int32, (t, LANES), 1)
        lo = lane < HEAD_DIM
        causal = lax.broadcasted_iota(jnp.int32, (t, t), 1) <= lax.broadcasted_iota(jnp.int32, (t, t), 0)
        qas = [qa_ref[:, hh * LANES:(hh + 1) * LANES] for hh in range(heads)]

        def step(j, carry, masked):
            rows = pl.ds(pl.multiple_of(j * t, t), t)
            new = []
            for hh in range(heads):
                cols = slice(hh * LANES, (hh + 1) * LANES)
                m, acc = carry[hh]
                sc = _dot(qas[hh], ka_ref[rows, cols], NT)
                if masked:
                    sc = jnp.where(causal, sc, -1e30)
                m_new = jnp.maximum(m, jnp.max(sc, axis=-1, keepdims=True))
                acc = jnp.exp(m - m_new) * acc + _dot(jnp.exp(sc - m_new).astype(BF16), va_ref[rows, cols])
                new.append((m_new, acc))
            return tuple(new)

        init = (jnp.full((t, 1), -1e30, F32), jnp.zeros((t, LANES), F32))
        carry = lax.fori_loop(0, i, functools.partial(step, masked=False), (init,) * heads)
        res = step(i, carry, masked=True)
        for p in range(pairs):
            outs = []
            rcol = jnp.zeros((t, LANES), F32)
            for hh in range(2):
                m, acc = res[2 * p + hh]
                l = _lane_pick(acc, lane, AUG_LO)
                outs.append(acc * (1.0 / l))
                rcol = jnp.where(lane == hh, m + jnp.log(l), rcol)
            o = _pair_block(outs[0], outs[1], lo)
            sl = slice(p * LANES, (p + 1) * LANES)
            o_ref[:, sl] = o
            g = gb_ref[:, sl]
            mb_ref[:, sl] = (o * (g * _sig(g))).astype(BF16)
            r_ref[p] = rcol

    tile_spec = pl.BlockSpec((t, pairs * LANES), lambda p, i: (i, p))
    full_spec = pl.BlockSpec((s, group_w), lambda p, i: (0, p))
    return pl.pallas_call(
        body, name="fox_fwd", grid=(FOX_HEADS // heads, n),
        out_shape=(jax.ShapeDtypeStruct((s, FOX_WIDTH), F32), jax.ShapeDtypeStruct((s, FOX_WIDTH), BF16),
                   jax.ShapeDtypeStruct((FOX_HEADS // 2, s, LANES), F32)),
        in_specs=[pl.BlockSpec((t, group_w), lambda p, i: (i, p)), full_spec, full_spec, tile_spec],
        out_specs=(tile_spec, tile_spec, pl.BlockSpec((pairs, t, LANES), lambda p, i: (p, i, 0))),
        compiler_params=_params(2),
    )(qa, ka, va, gb)


def _out_loss(x, tgt, ma, mb, mm, wout):
    s = x.shape[0]
    t = TILE
    n = s // t

    def body(x_ref, t_ref, ma_ref, mb_ref, mm_ref, w_ref, dy_ref, dma_ref, dmb_ref, dmm_ref, dw_ref, loss_ref, mix_ref):
        @pl.when(pl.program_id(0) == 0)
        def _():
            dw_ref[...] = jnp.zeros_like(dw_ref)
            loss_ref[...] = jnp.zeros_like(loss_ref)

        mix_ref[:, 0:256] = ma_ref[...]
        mix_ref[:, 256:768] = mb_ref[...]
        mix_ref[:, 768:1024] = mm_ref[...]
        mix = mix_ref[...]
        err = (x_ref[...] + _dot(mix, w_ref[...])) - t_ref[...]
        row_mean = jnp.sum(err * err, axis=-1, keepdims=True) * (1.0 / D_MODEL)
        loss_ref[...] += 0.5 * jnp.sum(row_mean, axis=0, keepdims=True)
        dy = err * (1.0 / D_MODEL)
        dy_ref[...] = dy
        dyb = dy.astype(BF16)
        dmix = _dot(dyb, w_ref[...], NT)
        dma_ref[...] = dmix[:, 0:256]
        dmb_ref[...] = dmix[:, 256:768]
        dmm_ref[...] = dmix[:, 768:1024]
        dw_ref[...] += _dot(mix, dyb, TN)

    return pl.pallas_call(
        body, name="out_loss", grid=(n,),
        out_shape=(jax.ShapeDtypeStruct((s, D_MODEL), F32), jax.ShapeDtypeStruct((s, 256), F32),
                   jax.ShapeDtypeStruct((s, 512), F32), jax.ShapeDtypeStruct((s, 256), F32),
                   jax.ShapeDtypeStruct((D_MODEL, D_MODEL), F32), jax.ShapeDtypeStruct((1, LANES), F32)),
        in_specs=[_rows(t, D_MODEL), _rows(t, D_MODEL), _rows(t, 256), _rows(t, 512), _rows(t, 256),
                  _full((D_MODEL, D_MODEL))],
        out_specs=(_rows(t, D_MODEL), _rows(t, 256), _rows(t, 512), _rows(t, 256), _full((D_MODEL, D_MODEL)),
                   _full((1, LANES))),
        scratch_shapes=[pltpu.VMEM((t, D_MODEL), BF16)],
        compiler_params=_params(),
    )(x, tgt, ma, mb, mm, wout)


def _mem_attn_bwd(pm, dmm, kmn, vmb, mq_g):
    s = pm.shape[0]
    t = TILE
    n = s // t

    def body(pm_ref, dmm_ref, k_ref, v_ref, g_ref, dpm_ref, dk_ref, dv_ref, dg_ref, gacc_ref):
        @pl.when(pl.program_id(0) == 0)
        def _():
            dk_ref[...] = jnp.zeros_like(dk_ref)
            dv_ref[...] = jnp.zeros_like(dv_ref)
            gacc_ref[...] = jnp.zeros_like(gacc_ref)

        lo = _lane_lo((t, LANES))
        for p in range(MEM_WIDTH // LANES):
            sl = slice(p * LANES, (p + 1) * LANES)
            qb = pm_ref[:, sl]
            rr = _head_rms(qb, lo)
            qhat = qb * rr
            g = g_ref[:, sl]
            qs = ((qhat * g) * ATT_SCALE).astype(BF16)
            gm = pm_ref[:, MEM_WIDTH + p * LANES:MEM_WIDTH + (p + 1) * LANES]
            sg = _sig(gm)
            dmo = dmm_ref[:, sl]
            d_o = dmo * (gm * sg)
            kp = k_ref[:, sl]
            vp = v_ref[:, sl]
            outs, dqs = [], []
            for hh in range(2):
                msk = lo if hh == 0 else jnp.logical_not(lo)
                qm = jnp.where(msk, qs, jnp.zeros_like(qs))
                prob = _mem_softmax(qm, kp)
                pb = prob.astype(BF16)
                outs.append(_dot(pb, vp))
                dom = jnp.where(msk, d_o, 0.0).astype(BF16)
                dp = _dot(dom, vp, NT)
                ds = (prob * (dp - jnp.sum(prob * dp, axis=-1, keepdims=True))).astype(BF16)
                dqs.append(_dot(ds, kp))
                dk_ref[:, sl] += _dot(ds, qm, TN)
                dv_ref[:, sl] += _dot(pb, dom, TN)
            o = jnp.where(lo, outs[0], outs[1])
            dqn = jnp.where(lo, dqs[0], dqs[1]) * ATT_SCALE
            dpm_ref[:, sl] = _head_norm_bwd(dqn, qhat, rr, g, lo).astype(BF16)
            dpm_ref[:, MEM_WIDTH + p * LANES:MEM_WIDTH + (p + 1) * LANES] = (
                dmo * o * (sg * (1.0 + gm * (1.0 - sg)))).astype(BF16)
            gacc_ref[:, sl] += jnp.sum(dqn * qhat, axis=0, keepdims=True)

        @pl.when(pl.program_id(0) == n - 1)
        def _():
            dg_ref[...] = _fold_heads(gacc_ref[...])

    return pl.pallas_call(
        body, name="mem_attn_bwd", grid=(n,),
        out_shape=(jax.ShapeDtypeStruct((s, 512), BF16), jax.ShapeDtypeStruct((N_MEM, MEM_WIDTH), F32),
                   jax.ShapeDtypeStruct((N_MEM, MEM_WIDTH), F32), jax.ShapeDtypeStruct((1, LANES), F32)),
        in_specs=[_rows(t, 512), _rows(t, MEM_WIDTH), _full((N_MEM, MEM_WIDTH)), _full((N_MEM, MEM_WIDTH)),
                  _full((1, MEM_WIDTH))],
        out_specs=(_rows(t, 512), _full((N_MEM, MEM_WIDTH)), _full((N_MEM, MEM_WIDTH)), _full((1, LANES))),
        scratch_shapes=[pltpu.VMEM((1, MEM_WIDTH), F32)],
        compiler_params=_params(),
    )(pm, dmm, kmn, vmb, mq_g)


def _mem_bwd(dkn, dvm, kv, mnb, mem, w_kv, mk_g, mem_norm_g):
    n = mem.shape[0]

    def body(dkn_ref, dvm_ref, kv_ref, mn_ref, mem_ref, w_ref, kg_ref, g_ref, dw_ref, dg_ref, dkg_ref, dkv_ref):
        lo = _lane_lo((n, LANES))
        gacc = []
        for p in range(MEM_WIDTH // LANES):
            sl = slice(p * LANES, (p + 1) * LANES)
            kb = kv_ref[:, sl]
            rr = _head_rms(kb, lo)
            khat = kb * rr
            dk = dkn_ref[:, sl]
            dkv_ref[:, sl] = _head_norm_bwd(dk, khat, rr, kg_ref[:, sl], lo).astype(BF16)
            gacc.append(jnp.sum(dk * khat, axis=0, keepdims=True))
        dkg_ref[...] = _fold_heads(jnp.concatenate(gacc, axis=1))
        dkv_ref[:, MEM_WIDTH:] = dvm_ref[...].astype(BF16)
        dkv = dkv_ref[...]
        dw_ref[...] = _dot(mn_ref[...], dkv, TN)
        dmn = _dot(dkv, w_ref[...], NT)
        xm = mem_ref[...]
        rr = lax.rsqrt(jnp.mean(xm * xm, axis=-1, keepdims=True) + EPS)
        dg_ref[...] = jnp.sum(dmn * (xm * rr), axis=0, keepdims=True)

    return pl.pallas_call(
        body, name="mem_bwd",
        out_shape=(jax.ShapeDtypeStruct((D_MODEL, 2 * MEM_WIDTH), F32), jax.ShapeDtypeStruct((1, D_MODEL), F32),
                   jax.ShapeDtypeStruct((1, LANES), F32)),
        scratch_shapes=[pltpu.VMEM((n, 2 * MEM_WIDTH), BF16)],
        compiler_params=pltpu.CompilerParams(vmem_limit_bytes=VMEM_LIMIT),
    )(dkn, dvm, kv, mnb, mem, w_kv, mk_g, mem_norm_g)


def _pool_bwd(pa, db, dma, w4, pscale):
    s = pa.shape[0]
    t = TILE
    n = s // t
    ext = t + POOL_HALO

    def body(pa_ref, d_ref, dma_ref, w4_ref, sc_ref, dpa_ref, dw4_ref, dsc_ref, ext_ref, w_ref, dw_ref):
        i = pl.program_id(0)

        @pl.when(i == 0)
        def _():
            dw_ref[...] = jnp.zeros_like(dw_ref)
            dsc_ref[...] = jnp.zeros_like(dsc_ref)
            ext_ref[t:ext, :] = jnp.zeros((POOL_HALO, POOL_WIDTH), F32)
            w_ref[...] = _pool_block_diag(w4_ref[...])

        dbv = d_ref[...]
        z = _dot(dbv, w_ref[...])
        ga = pa_ref[:, POOL_WIDTH:2 * POOL_WIDTH]
        sg = _sig(ga)
        dma_v = dma_ref[...]
        dya = dma_v * (ga * sg)
        dpa_ref[:, POOL_WIDTH:2 * POOL_WIDTH] = (dma_v * (z * sc_ref[...]) * (sg * (1.0 + ga * (1.0 - sg)))).astype(BF16)
        dsc_ref[...] += jnp.sum(dya * z, axis=0, keepdims=True)
        dzb = (dya * sc_ref[...]).astype(BF16)
        dw_ref[...] += _dot(dbv, dzb, TN)
        dd = _dot(dzb, w_ref[...], NT)
        lane = lax.broadcasted_iota(jnp.int32, (t, POOL_WIDTH), 1)
        pos = (lax.broadcasted_iota(jnp.int32, (t, POOL_WIDTH), 0) + ((n - 1 - i) * t + 1)).astype(F32)
        ext_ref[0:t, :] = dd / jnp.minimum(pos, _pool_window(lane))
        e = ext_ref[...]
        s2 = e + pltpu.roll(e, ext - 1, axis=0)
        s4 = s2 + pltpu.roll(s2, ext - 2, axis=0)
        s8 = s4 + pltpu.roll(s4, ext - 4, axis=0)
        s16 = s8 + pltpu.roll(s8, ext - 8, axis=0)
        lane_e = lax.broadcasted_iota(jnp.int32, (ext, POOL_WIDTH), 1)
        win = _pool_pick(lane_e, s2, s4, s8, s16)[0:t, :]
        dpa_ref[:, 0:POOL_WIDTH] = (win - dd).astype(BF16)
        ext_ref[t:ext, :] = ext_ref[0:POOL_HALO, :]

        @pl.when(i == n - 1)
        def _():
            own = jnp.where(_same_group((POOL_WIDTH, POOL_WIDTH)), dw_ref[...], 0.0)
            dw4_ref[...] = jnp.dot(own, _group_onehot((POOL_WIDTH, HEAD_DIM), True), preferred_element_type=F32,
                                   precision=lax.Precision.HIGHEST)

    return pl.pallas_call(
        body, name="pool_bwd", grid=(n,),
        out_shape=(jax.ShapeDtypeStruct((s, 512), BF16), jax.ShapeDtypeStruct((POOL_ROWS, HEAD_DIM), F32),
                   jax.ShapeDtypeStruct((1, POOL_WIDTH), F32)),
        in_specs=[_rows_rev(t, 512, n), _rows_rev(t, POOL_WIDTH, n), _rows_rev(t, POOL_WIDTH, n),
                  _full((POOL_ROWS, HEAD_DIM)), _full((1, POOL_WIDTH))],
        out_specs=(_rows_rev(t, 512, n), _full((POOL_ROWS, HEAD_DIM)), _full((1, POOL_WIDTH))),
        scratch_shapes=[pltpu.VMEM((ext, POOL_WIDTH), F32), pltpu.VMEM((POOL_WIDTH, POOL_WIDTH), BF16),
                        pltpu.VMEM((POOL_WIDTH, POOL_WIDTH), F32)],
        compiler_params=_params(),
    )(pa, db, dma, w4, pscale)


def _fox_prep(dmb, gb, o, r4):
    s = dmb.shape[0]
    t = TILE
    n = s // t
    pairs = FOX_HEADS // 2

    def body(dmb_ref, gb_ref, o_ref, r_ref, doa_ref, dgb_ref, rr_ref):
        lane = lax.broadcasted_iota(jnp.int32, (t, LANES), 1)
        lo = lane < HEAD_DIM
        d_os = []
        delta = jnp.zeros((t, LANES), F32)
        for p in range(pairs):
            sl = slice(p * LANES, (p + 1) * LANES)
            g = gb_ref[:, sl]
            sg = _sig(g)
            dm = dmb_ref[:, sl]
            ov = o_ref[:, sl]
            d_o = dm * (g * sg)
            d_os.append(d_o)
            dgb_ref[:, sl] = (dm * ov * (sg * (1.0 + g * (1.0 - sg)))).astype(BF16)
            prod = d_o * ov
            delta = jnp.where(lane == 2 * p, jnp.sum(jnp.where(lo, prod, 0.0), axis=-1, keepdims=True), delta)
            delta = jnp.where(lane == 2 * p + 1, jnp.sum(jnp.where(lo, 0.0, prod), axis=-1, keepdims=True), delta)
            rr_ref[p, 0] = r_ref[p].T[0:8, :]
        minus_delta = _spread3(-delta)
        for h in range(FOX_HEADS):
            blk = slice(h * LANES, (h + 1) * LANES)
            doa_ref[:, blk] = _head_block(d_os[h // 2], h % 2, lo, minus_delta[:, blk])

    return pl.pallas_call(
        body, name="fox_prep", grid=(n,),
        out_shape=(jax.ShapeDtypeStruct((s, HEAD_BLOCKS), BF16), jax.ShapeDtypeStruct((s, FOX_WIDTH), BF16),
                   jax.ShapeDtypeStruct((pairs, n, 8, t), F32)),
        in_specs=[_rows(t, FOX_WIDTH), _rows(t, FOX_WIDTH), _rows(t, FOX_WIDTH),
                  pl.BlockSpec((pairs, t, LANES), lambda i: (0, i, 0))],
        out_specs=(_rows(t, HEAD_BLOCKS), _rows(t, FOX_WIDTH), pl.BlockSpec((pairs, 1, 8, t), lambda i: (0, i, 0, 0))),
        compiler_params=_params(),
    )(dmb, gb, o, r4)


FOX_BWD_HEADS = 4


def _fox_bwd(ka, va, qa, doa, rr):
    s = ka.shape[0]
    t = TILE
    n = s // t
    heads = FOX_BWD_HEADS
    group_w = heads * LANES

    def body(ka_ref, va_ref, qa_ref, doa_ref, rr_ref, dka_ref, dva_ref, dqa_ref):
        j = pl.program_id(1)

        @pl.when(j == 0)
        def _():
            dqa_ref[...] = jnp.zeros_like(dqa_ref)

        causal = lax.broadcasted_iota(jnp.int32, (t, t), 0) <= lax.broadcasted_iota(jnp.int32, (t, t), 1)
        kas = [ka_ref[:, hh * LANES:(hh + 1) * LANES] for hh in range(heads)]
        vas = [va_ref[:, hh * LANES:(hh + 1) * LANES] for hh in range(heads)]

        def step(i, carry, masked):
            rows = pl.ds(pl.multiple_of(i * t, t), t)
            new = []
            for hh in range(heads):
                cols = slice(hh * LANES, (hh + 1) * LANES)
                dk_a, dv_a = carry[hh]
                qb = qa_ref[rows, cols]
                d_o = doa_ref[rows, cols]
                arg = _dot(kas[hh], qb, NT) - rr_ref[hh // 2, i, hh % 2:hh % 2 + 1, :]
                if masked:
                    arg = jnp.where(causal, arg, -1e30)
                pt = jnp.exp(arg)
                dst = (pt * _dot(vas[hh], d_o, NT)).astype(BF16)
                dv_a = dv_a + _dot(pt.astype(BF16), d_o)
                dk_a = dk_a + _dot(dst, qb)
                dqa_ref[rows, cols] += _dot(dst, kas[hh], TN)
                new.append((dk_a, dv_a))
            return tuple(new)

        zero = jnp.zeros((t, LANES), F32)
        carry = step(j, ((zero, zero),) * heads, masked=True)
        res = lax.fori_loop(j + 1, n, functools.partial(step, masked=False), carry)
        for hh in range(heads):
            cols = slice(hh * LANES, (hh + 1) * LANES)
            dka_ref[:, cols] = res[hh][0]
            dva_ref[:, cols] = res[hh][1]

    tile_spec = pl.BlockSpec((t, group_w), lambda p, j: (j, p))
    full_spec = pl.BlockSpec((s, group_w), lambda p, j: (0, p))
    return pl.pallas_call(
        body, name="fox_bwd", grid=(FOX_HEADS // heads, n),
        out_shape=(jax.ShapeDtypeStruct((s, HEAD_BLOCKS), F32),) * 3,
        in_specs=[tile_spec, tile_spec, full_spec, full_spec,
                  pl.BlockSpec((heads // 2, n, 8, t), lambda p, j: (p, 0, 0, 0))],
        out_specs=(tile_spec, tile_spec, full_spec),
        compiler_params=_params(2),
    )(ka, va, qa, doa, rr)


def _fox_post(dqa, dka, dva, qk, fb, bf_pad, fq_g, fk_g):
    s = dqa.shape[0]
    t = TILE
    n = s // t

    def body(dqa_ref, dka_ref, dva_ref, qk_ref, fb_ref, bf_ref, qg_ref, kg_ref,
             dqk_ref, dv_ref, dfb_ref, dqg_ref, dkg_ref, dbf_ref, qacc_ref, kacc_ref, carry_ref):
        i = pl.program_id(0)

        @pl.when(i == 0)
        def _():
            qacc_ref[...] = jnp.zeros_like(qacc_ref)
            kacc_ref[...] = jnp.zeros_like(kacc_ref)
            dbf_ref[...] = jnp.zeros_like(dbf_ref)
            carry_ref[...] = jnp.zeros_like(carry_ref)

        lane = lax.broadcasted_iota(jnp.int32, (t, LANES), 1)
        row = lax.broadcasted_iota(jnp.int32, (t, LANES), 0)
        lo = lane < HEAD_DIM

        def head_blocks(ref, p):
            return ref[:, 2 * p * LANES:(2 * p + 1) * LANES], ref[:, (2 * p + 1) * LANES:(2 * p + 2) * LANES]

        dq_sum = jnp.zeros((t, LANES), F32)
        dk_sum = jnp.zeros((t, LANES), F32)
        for p in range(FOX_WIDTH // LANES):
            sl = slice(p * LANES, (p + 1) * LANES)
            dq0, dq1 = head_blocks(dqa_ref, p)
            dk0, dk1 = head_blocks(dka_ref, p)
            dv0, dv1 = head_blocks(dva_ref, p)
            dv_ref[:, sl] = _pair_block(dv0, dv1, lo).astype(BF16)
            dq_sum = dq_sum + (dq0 + dq1)
            dk_sum = dk_sum + (dk0 + dk1)
            for off, pair, g_ref, acc_ref, scale in ((0, _pair_block(dq0, dq1, lo), qg_ref, qacc_ref, ATT_SCALE),
                                                     (FOX_WIDTH, _pair_block(dk0, dk1, lo), kg_ref, kacc_ref, 1.0)):
                raw = qk_ref[:, off + p * LANES:off + (p + 1) * LANES]
                rr = _head_rms(raw, lo)
                xhat = raw * rr
                dn = pair * scale
                dqk_ref[:, off + p * LANES:off + (p + 1) * LANES] = _head_norm_bwd(
                    dn, xhat, rr, g_ref[:, sl], lo).astype(BF16)
                acc_ref[:, sl] += jnp.sum(dn * xhat, axis=0, keepdims=True)

        acc = (pltpu.roll(dq_sum, LANES - KEY_SUM_LANE, axis=1) - pltpu.roll(dk_sum, LANES - QUERY_SUM_LANE, axis=1))
        acc = jnp.where(lane < FOX_HEADS, acc, 0.0)
        sh = 1
        while sh < t:
            acc = acc + jnp.where(row < t - sh, pltpu.roll(acc, t - sh, axis=0), 0.0)
            sh *= 2
        dlogf = acc + carry_ref[...]
        dfb_ref[...] = dlogf
        carry_ref[...] = dfb_ref[0:1, :]
        z = fb_ref[...] + bf_ref[...]
        dz = jnp.where(lane < FOX_HEADS, dlogf * (1.0 / (1.0 + jnp.exp(z))), 0.0)
        dfb_ref[...] = dz
        dbf_ref[...] += jnp.sum(dz, axis=0, keepdims=True)

        @pl.when(i == n - 1)
        def _():
            dqg_ref[...] = _fold_heads(qacc_ref[...])
            dkg_ref[...] = _fold_heads(kacc_ref[...])

    return pl.pallas_call(
        body, name="fox_post", grid=(n,),
        out_shape=(jax.ShapeDtypeStruct((s, 2 * FOX_WIDTH), BF16), jax.ShapeDtypeStruct((s, FOX_WIDTH), BF16),
                   jax.ShapeDtypeStruct((s, LANES), F32), jax.ShapeDtypeStruct((1, LANES), F32),
                   jax.ShapeDtypeStruct((1, LANES), F32), jax.ShapeDtypeStruct((1, LANES), F32)),
        in_specs=[_rows_rev(t, HEAD_BLOCKS, n), _rows_rev(t, HEAD_BLOCKS, n), _rows_rev(t, HEAD_BLOCKS, n),
                  _rows_rev(t, 2 * FOX_WIDTH, n), _rows_rev(t, LANES, n), _full((1, LANES)),
                  _full((1, FOX_WIDTH)), _full((1, FOX_WIDTH))],
        out_specs=(_rows_rev(t, 2 * FOX_WIDTH, n), _rows_rev(t, FOX_WIDTH, n), _rows_rev(t, LANES, n),
                   _full((1, LANES)), _full((1, LANES)), _full((1, LANES))),
        scratch_shapes=[pltpu.VMEM((1, FOX_WIDTH), F32), pltpu.VMEM((1, FOX_WIDTH), F32), pltpu.VMEM((1, LANES), F32)],
        compiler_params=_params(),
    )(dqa, dka, dva, qk, fb, bf_pad, fq_g, fk_g)


def _assemble_dproj(dp_ref, dpa_ref, dqk_ref, dv_ref, dgb_ref, dpm_ref, dfb_ref):
    dp_ref[:, PA_LO:QB_LO] = dpa_ref[...]
    dp_ref[:, QB_LO:VB_LO] = dqk_ref[...]
    dp_ref[:, VB_LO:GB_LO] = dv_ref[...]
    dp_ref[:, GB_LO:PM_LO] = dgb_ref[...]
    dp_ref[:, PM_LO:FB_LO] = dpm_ref[...]
    dp_ref[:, FB_LO:PROJ_PAD] = dfb_ref[...].astype(BF16)


def _dproj_specs(t):
    return [_rows(t, 512), _rows(t, 2 * FOX_WIDTH), _rows(t, FOX_WIDTH), _rows(t, FOX_WIDTH), _rows(t, 512),
            _rows(t, LANES)]


IN_BWD_X_TILE = 256


def _in_bwd_x(x, dy, norm_g, wp, dparts, gparts, axes, smalls):
    s = x.shape[0]
    t = IN_BWD_X_TILE
    n = s // t
    na = len(gparts)
    n_dp = len(dparts)
    vec_leaves, loss_row, dw4 = smalls if smalls is not None else ((), None, None)
    nv = len(vec_leaves)
    n_small = nv + 2 if smalls is not None else 0
    small_base = _ShardReduce.SEMS * na

    def body(*refs):
        x_ref, dy_ref, g_ref, wp_ref = refs[0:4]
        dp_parts = refs[4:4 + n_dp]
        o = 4 + n_dp
        g_refs = refs[o:o + na]
        small_in = refs[o + na:o + na + n_small]
        o += na + n_small
        gx_ref, dg_ref = refs[o:o + 2]
        out_refs = refs[o + 2:o + 2 + na]
        small_out = refs[o + 2 + na:o + 2 + na + (2 if smalls is not None else 0)]
        o += 2 + na + len(small_out)
        dp_ref = refs[o]
        bufs = tuple(refs[o + 1 + k * na:o + 1 + (k + 1) * na] for k in range(5))
        rest = refs[o + 1 + 5 * na:]

        i = pl.program_id(0)
        if na or smalls is not None:
            send_sems, recv_sems, local_sems = rest[-3:]
        red = _ShardReduce(g_refs, out_refs, axes, bufs, send_sems, recv_sems, local_sems) if na else None

        @pl.when(i == 0)
        def _():
            dg_ref[...] = jnp.zeros_like(dg_ref)
            if red is not None:
                red.exchange_with_sibling()

        if red is not None:
            for k in (1, 2, 3):
                pl.when(i == k)(functools.partial(red.send_to_chip, k))
            pl.when(i == 4)(red.keep_mine)

        _assemble_dproj(dp_ref, *dp_parts)
        dh = _dot(dp_ref[...], wp_ref[...])
        xv = x_ref[...]
        rr = lax.rsqrt(jnp.mean(xv * xv, axis=-1, keepdims=True) + EPS)
        xhat = xv * rr
        scaled = dh * g_ref[...]
        gx_ref[...] = dy_ref[...] + rr * (scaled - xhat * jnp.mean(xhat * scaled, axis=-1, keepdims=True))
        dg_ref[...] += jnp.sum(dh * xhat, axis=0, keepdims=True)

        def small_all_reduce():
            leaf_refs, (loss_ref, dw4_ref) = small_in[0:nv], small_in[nv:]
            vec_out, dw4_out = small_out
            vec_mine, vec_recv, dw4_recv = rest[0:3]
            cx, cy, c = _my_place()
            me_lin = 4 * cx + 2 * cy + c

            def copy(k, src, dst, base):
                peer = (me_lin + k) % 8
                return pltpu.make_async_remote_copy(
                    src_ref=src, dst_ref=dst.at[me_lin], send_sem=send_sems.at[base + k - 1],
                    recv_sem=recv_sems.at[base + k - 1], device_id=(peer // 4, (peer // 2) % 2, peer % 2),
                    device_id_type=MESH)

            vec_mine[...] = jnp.zeros_like(vec_mine)
            vec_mine[0:1, :] = dg_ref[...]
            for (_, row, _), ref in zip(VEC_LEAVES[1:], leaf_refs):
                vec_mine[row:row + 1, 0:ref.shape[1]] = ref[...]
            vec_mine[VEC_LOSS_ROW:VEC_LOSS_ROW + 1, 0:LANES] = loss_ref[...]
            copies = [copy(k, src, dst, base) for k in range(1, 8)
                      for src, dst, base in ((vec_mine, vec_recv, small_base), (dw4_ref, dw4_recv, small_base + 7))]
            for cp in copies:
                cp.start()
            for cp in copies:
                cp.wait_recv()
            vec_recv[me_lin] = vec_mine[...]
            dw4_recv[me_lin] = dw4_ref[...]
            vtot, wtot = vec_recv[0], dw4_recv[0]
            for d in range(1, 8):
                vtot = vtot + vec_recv[d]
                wtot = wtot + dw4_recv[d]
            vec_out[...] = vtot
            dw4_out[...] = wtot
            for cp in copies:
                cp.wait_send()

        @pl.when(i == n - 1)
        def _():
            if red is not None:
                red.sum_and_share()
            if smalls is not None:
                small_all_reduce()
            if red is not None:
                red.finish()

    any_spec = pl.BlockSpec(memory_space=pl.ANY)
    scratch = [pltpu.VMEM((t, PROJ_PAD), BF16)] + _ShardReduce.scratch(gparts, axes)
    out_shape = [jax.ShapeDtypeStruct((s, D_MODEL), F32), jax.ShapeDtypeStruct((1, D_MODEL), F32)]
    out_shape += [jax.ShapeDtypeStruct(g.shape[1:], F32) for g in gparts]
    out_specs = [_rows(t, D_MODEL), _full((1, D_MODEL))] + [any_spec] * na
    small_args = []
    if smalls is not None:
        small_args = [*vec_leaves, loss_row, dw4]
        out_shape += [jax.ShapeDtypeStruct((VEC_ROWS, D_MODEL), F32), jax.ShapeDtypeStruct(dw4.shape, F32)]
        out_specs += [_full((VEC_ROWS, D_MODEL)), _full(dw4.shape)]
        scratch += [pltpu.VMEM((VEC_ROWS, D_MODEL), F32), pltpu.VMEM((8, VEC_ROWS, D_MODEL), F32),
                    pltpu.VMEM((8,) + dw4.shape, F32)]
    if na or smalls is not None:
        n_sems = small_base + 14
        scratch += [pltpu.SemaphoreType.DMA((n_sems,)), pltpu.SemaphoreType.DMA((n_sems,)),
                    pltpu.SemaphoreType.DMA((max(_ShardReduce.LOCAL * na, 1),))]
    return pl.pallas_call(
        body, name="in_bwd_x", grid=(n,), out_shape=tuple(out_shape),
        in_specs=[_rows(t, D_MODEL), _rows(t, D_MODEL), _full((1, D_MODEL)),
                  pl.BlockSpec((PROJ_PAD, D_MODEL), lambda i: (0, 0), pipeline_mode=pl.Buffered(1))]
        + _dproj_specs(t) + [any_spec] * na + [_full(a.shape) for a in small_args],
        out_specs=tuple(out_specs), scratch_shapes=scratch, compiler_params=_params(),
    )(x, dy, norm_g, wp, *dparts, *gparts, *small_args)


def _in_bwd_w(hb, dparts, gparts, axes):
    s = hb.shape[0]
    t = TILE
    n = s // t
    na = len(gparts)
    f_hi = F_ORIG_LO + FOX_HEADS

    def body(*refs):
        h_ref, dpa_ref, dqk_ref, dv_ref, dgb_ref, dpm_ref, dfb_ref = refs[0:7]
        g_refs = refs[7:7 + na]
        dw_ref = refs[7 + na]
        out_refs = refs[8 + na:8 + 2 * na]
        o = 8 + 2 * na
        bufs = tuple(refs[o + k * na:o + (k + 1) * na] for k in range(5))
        i = pl.program_id(0)
        red = _ShardReduce(g_refs, out_refs, axes, bufs, *refs[o + 5 * na:]) if na else None

        @pl.when(i == 0)
        def _():
            dw_ref[...] = jnp.zeros_like(dw_ref)
            if red is not None:
                red.exchange_with_sibling()

        if red is not None:
            @pl.when(i == 1)
            def _():
                for k in (1, 2, 3):
                    red.send_to_chip(k)
                red.keep_mine()

        hv = h_ref[...]
        for lo, ref in ((0, dpa_ref), (QB_LO, dqk_ref), (VB_LO, dv_ref), (f_hi, dgb_ref), (f_hi + FOX_WIDTH, dpm_ref)):
            dw_ref[lo:lo + ref.shape[1], :] += _dot(ref[...], hv, TN)
        dw_ref[F_ORIG_LO:f_hi, :] += _dot(dfb_ref[...].astype(BF16), hv, TN)[0:FOX_HEADS, :]

        if red is not None:
            @pl.when(i == n - 1)
            def _():
                red.sum_and_share()
                red.finish()

    any_spec = pl.BlockSpec(memory_space=pl.ANY)
    scratch = _ShardReduce.scratch(gparts, axes)
    if na:
        scratch += [pltpu.SemaphoreType.DMA((_ShardReduce.SEMS * na,)), pltpu.SemaphoreType.DMA((_ShardReduce.SEMS * na,)),
                    pltpu.SemaphoreType.DMA((_ShardReduce.LOCAL * na,))]
    return pl.pallas_call(
        body, name="in_bwd_w", grid=(n,),
        out_shape=(jax.ShapeDtypeStruct((IN_WIDTH, D_MODEL), F32),)
        + tuple(jax.ShapeDtypeStruct(g.shape[1:], F32) for g in gparts),
        in_specs=[_rows(t, D_MODEL)] + _dproj_specs(t) + [any_spec] * na,
        out_specs=(_full((IN_WIDTH, D_MODEL)),) + (any_spec,) * na,
        scratch_shapes=scratch, compiler_params=_params(),
    )(hb, *dparts, *gparts)


def _adamw_math(w_ref, gv, m_ref, v_ref, d_ref, nm_ref, nv_ref):
    nm = ADAM_B1 * m_ref[...] + (1.0 - ADAM_B1) * gv
    nv = ADAM_B2 * v_ref[...] + (1.0 - ADAM_B2) * (gv * gv)
    m_hat = nm / (1.0 - ADAM_B1 ** ADAM_STEP)
    v_hat = nv / (1.0 - ADAM_B2 ** ADAM_STEP)
    d_ref[...] = -ADAM_LR * (m_hat / (jnp.sqrt(v_hat) + ADAM_EPS) + ADAM_WD * w_ref[...])
    nm_ref[...] = nm
    nv_ref[...] = nv


def _adamw(name, w, g, m, v):
    rows, cols = w.shape
    tc = 256 if rows * cols > 256 * 1024 else cols
    n = cols // tc

    def body(w_ref, g_ref, m_ref, v_ref, d_ref, nm_ref, nv_ref):
        _adamw_math(w_ref, g_ref[...], m_ref, v_ref, d_ref, nm_ref, nv_ref)

    spec = pl.BlockSpec((rows, tc), lambda i: (0, i))
    return pl.pallas_call(
        body, name=name, grid=(n,),
        out_shape=(jax.ShapeDtypeStruct((rows, cols), F32),) * 3,
        in_specs=[spec] * 4, out_specs=(spec,) * 3,
        compiler_params=_params(),
    )(w, g, m, v)


def _adamw_small(vec, dw4, leaves, pool):
    nl = len(VEC_LEAVES) + 1

    def body(*refs):
        vec_ref, dw4_ref = refs[0:2]
        wmv = refs[2:2 + 3 * nl]
        loss_ref = refs[2 + 3 * nl]
        outs = refs[3 + 3 * nl:]
        loss_ref[...] = vec_ref[VEC_LOSS_ROW:VEC_LOSS_ROW + 1, 0:1]
        for k in range(nl):
            if k < nl - 1:
                _, row, width = VEC_LEAVES[k]
                gv = vec_ref[row:row + 1, 0:width]
            else:
                gv = dw4_ref[...]
            w_ref, m_ref, v_ref = wmv[3 * k:3 * k + 3]
            g_ref, d_ref, nm_ref, nv_ref = outs[4 * k:4 * k + 4]
            g_ref[...] = gv
            _adamw_math(w_ref, gv, m_ref, v_ref, d_ref, nm_ref, nv_ref)

    shapes = [jax.ShapeDtypeStruct((1, width), F32) for _, _, width in VEC_LEAVES] + [
        jax.ShapeDtypeStruct(dw4.shape, F32)]
    flat_in = [a for triple in list(leaves) + [pool] for a in triple]
    res = pl.pallas_call(
        body, name="adamw_small",
        out_shape=(jax.ShapeDtypeStruct((1, 1), F32),) + tuple(s for s in shapes for _ in range(4)),
        compiler_params=pltpu.CompilerParams(vmem_limit_bytes=VMEM_LIMIT),
    )(vec, dw4, *flat_in)
    per = [res[1 + 4 * k:5 + 4 * k] for k in range(nl)]
    return res[0], [p[0] for p in per], [p[1] for p in per], [p[2] for p in per], [p[3] for p in per]


def _full_w_in_padded(halves):
    cols = IN_WIDTH // 4
    w_t = halves.reshape(4, 2, cols, D_MODEL // 2).transpose(0, 2, 1, 3).reshape(IN_WIDTH, D_MODEL)
    return jnp.concatenate([
        w_t[0:F_ORIG_LO], w_t[F_ORIG_LO + FOX_HEADS:], w_t[F_ORIG_LO:F_ORIG_LO + FOX_HEADS],
        jnp.zeros((PROJ_PAD - IN_WIDTH, D_MODEL), w_t.dtype)], axis=0)


def _tile_heads(g, n):
    return jnp.tile(g.reshape(1, HEAD_DIM), (1, n))


def kernel(x, mem, norm_g, w_in, b_f, w_pool, pool_scale, fox_q_g, fox_k_g, mem_norm_g, w_mem_kv, mem_q_g, mem_k_g, w_out, loss_target, m_norm_g, m_w_in, m_b_f, m_w_pool, m_pool_scale, m_fox_q_g, m_fox_k_g, m_mem_norm_g, m_w_mem_kv, m_mem_q_g, m_mem_k_g, m_w_out, v_norm_g, v_w_in, v_b_f, v_w_pool, v_pool_scale, v_fox_q_g, v_fox_k_g, v_mem_norm_g, v_w_mem_kv, v_mem_q_g, v_mem_k_g, v_w_out):
    w_in_t, m_w_in_t, v_w_in_t = w_in[0].T, m_w_in[0].T, v_w_in[0].T
    axes = (1, 0, 0)

    g_in, g_kv, g_out = _all_gather_weights([w_in_t, w_mem_kv[0], w_out[0]], axes)
    wp = _full_w_in_padded(g_in)
    tiled = _tiled_params(b_f, fox_q_g, fox_k_g, mem_q_g, mem_k_g)
    fwd = _fwd_in(x[0], norm_g, wp, *tiled[0:3])
    w_kv_b = g_kv.reshape(D_MODEL, 2 * MEM_WIDTH)
    w_out_b = g_out.reshape(D_MODEL, D_MODEL)
    w4 = w_pool.reshape(POOL_ROWS, HEAD_DIM)
    dy, hb, dparts, dw_kv, dw_out, vec_leaves, loss_row, dw4 = _local_partials(
        x[0], mem[0], loss_target[0], fwd, w_kv_b, w_out_b, tiled, w4, pool_scale, mem_norm_g)

    early = [dw_kv.reshape(4, D_MODEL // 4, 2 * MEM_WIDTH), dw_out.reshape(4, D_MODEL // 4, D_MODEL)]
    dwp, g_w_kv, g_w_out = _in_bwd_w(hb, dparts, early, axes[1:])
    grad_x, _, g_w_in_t, vec, dw4_sum = _in_bwd_x(
        x[0], dy, norm_g, wp, dparts, [dwp.reshape(4, IN_WIDTH // 4, D_MODEL)], axes[0:1], (vec_leaves, loss_row, dw4))

    small_wmv = [(norm_g, m_norm_g, v_norm_g), (mem_norm_g, m_mem_norm_g, v_mem_norm_g),
                 (pool_scale, m_pool_scale, v_pool_scale), (b_f, m_b_f, v_b_f), (fox_q_g, m_fox_q_g, v_fox_q_g),
                 (fox_k_g, m_fox_k_g, v_fox_k_g), (mem_q_g, m_mem_q_g, v_mem_q_g), (mem_k_g, m_mem_k_g, v_mem_k_g)]
    pool_wmv = tuple(a.reshape(POOL_ROWS, HEAD_DIM) for a in (w_pool, m_w_pool, v_w_pool))
    loss, *small_out = _adamw_small(vec, dw4_sum, small_wmv, pool_wmv)
    big = [[g_w_in_t.T[None], g_w_kv[None], g_w_out[None]]]
    upd = [[a.T for a in _adamw("adamw_w_in", w_in_t, g_w_in_t, m_w_in_t, v_w_in_t)],
           _adamw("adamw_w_mem_kv", w_mem_kv[0], g_w_kv, m_w_mem_kv[0], v_w_mem_kv[0]),
           _adamw("adamw_w_out", w_out[0], g_w_out, m_w_out[0], v_w_out[0])]
    big += [[u[k][None] for u in upd] for k in range(3)]

    def leaves(k):
        sm = small_out[k]
        b_in, b_kv, b_out = big[k]
        return (sm[0], b_in, sm[3], sm[8].reshape(w_pool.shape), sm[2], sm[4], sm[5], sm[1], b_kv, sm[6], sm[7], b_out)

    return (loss.reshape(()), grad_x[None], *leaves(0), *leaves(1), *leaves(2), *leaves(3))


def _tiled_params(b_f, fox_q_g, fox_k_g, mem_q_g, mem_k_g):
    return (jnp.pad(b_f, ((0, 0), (0, LANES - FOX_HEADS))), _tile_heads(fox_q_g, FOX_HEADS),
            _tile_heads(fox_k_g, FOX_HEADS), _tile_heads(mem_q_g, 4), _tile_heads(mem_k_g, 4))


def _local_partials(xs, mems, tgt, fwd, w_kv_b, w_out_b, tiled, w4, pool_scale, mem_norm_g):
    hb, pa, qk, qa, ka, va, gb, pm, fb = fwd
    bf_pad, fq_g, fk_g, mq_g, mk_g = tiled

    mnb, kv, kmn, vmb = _mem_fwd(mems, mem_norm_g, w_kv_b, mk_g)
    ma, db = _pool_fwd(pa, w4, pool_scale)
    mm = _mem_attn_fwd(pm, kmn, vmb, mq_g)
    o, mb, r4 = _fox_fwd(qa, ka, va, gb)
    dy, dma, dmb, dmm, dw_out, loss_row = _out_loss(xs, tgt, ma, mb, mm, w_out_b)

    dpm, dkmn, dvm, dmq_g = _mem_attn_bwd(pm, dmm, kmn, vmb, mq_g)
    dw_kv, dmemnorm_g, dmk_g = _mem_bwd(dkmn, dvm, kv, mnb, mems, w_kv_b, mk_g, mem_norm_g)
    dpa, dw4, dpscale = _pool_bwd(pa, db, dma, w4, pool_scale)
    doa, dgb, rr = _fox_prep(dmb, gb, o, r4)
    dka, dva, dqa = _fox_bwd(ka, va, qa, doa, rr)
    dqk, dvb, dfb, dfq_g, dfk_g, dbf = _fox_post(dqa, dka, dva, qk, fb, bf_pad, fq_g, fk_g)
    dparts = (dpa, dqk, dvb, dgb, dpm, dfb)
    leaves = (dmemnorm_g, dpscale, dbf, dfq_g, dfk_g, dmq_g, dmk_g)
    return dy, hb, dparts, dw_kv, dw_out, leaves, loss_row, dw4
```

```python
import functools

import jax
import jax.numpy as jnp
from jax import lax
from jax.experimental import pallas as pl
from jax.experimental.pallas import tpu as pltpu

F32 = jnp.float32
BF16 = jnp.bfloat16
MESH = pl.DeviceIdType.MESH

D_MODEL = 1024
HEAD_DIM = 64
POOL_WIDTH = 256
FOX_WIDTH = 512
FOX_HEADS = 8
MEM_WIDTH = 256
N_MEM = 256
IN_WIDTH = 3080
EPS = 1e-6
ATT_SCALE = 0.125

ADAM_LR = 0.001
ADAM_B1 = 0.9
ADAM_B2 = 0.999
ADAM_EPS = 1e-08
ADAM_WD = 0.01
ADAM_STEP = 10

LANES = 128
PA_LO, QB_LO, KB_LO, VB_LO, GB_LO, PM_LO, FB_LO, PROJ_PAD = 0, 512, 1024, 1536, 2048, 2560, 3072, 3200
F_ORIG_LO = 2048

TILE = 512
VMEM_LIMIT = 56 * 1024 * 1024

VEC_LEAVES = (("norm_g", 0, 1024), ("mem_norm_g", 1, 1024), ("pool_scale", 2, 256), ("b_f", 3, 8),
              ("fox_q_g", 4, 64), ("fox_k_g", 5, 64), ("mem_q_g", 6, 64), ("mem_k_g", 7, 64))
VEC_LOSS_ROW = 8
VEC_ROWS = 16
POOL_ROWS = 256


def _params(n_grid=1, vmem=VMEM_LIMIT):
    return pltpu.CompilerParams(dimension_semantics=("arbitrary",) * n_grid, vmem_limit_bytes=vmem)


def _rows(t, w):
    return pl.BlockSpec((t, w), lambda i: (i, 0))


def _rows_rev(t, w, n):
    return pl.BlockSpec((t, w), lambda i: (n - 1 - i, 0))


def _full(shape):
    return pl.BlockSpec(shape, lambda i: (0,) * len(shape))


def _sig(x):
    return 1.0 / (1.0 + jnp.exp(-x))


def _lane_lo(shape):
    return lax.broadcasted_iota(jnp.int32, shape, 1) < HEAD_DIM


def _pair_sum(v, lo):
    s0 = jnp.sum(jnp.where(lo, v, 0.0), axis=-1, keepdims=True)
    s1 = jnp.sum(jnp.where(lo, 0.0, v), axis=-1, keepdims=True)
    return jnp.where(lo, s0, s1)


def _head_rms(blk, lo):
    return lax.rsqrt(_pair_sum(blk * blk, lo) * (1.0 / HEAD_DIM) + EPS)


def _head_norm_bwd(dyn, xhat, rr, g, lo):
    a = dyn * g
    return rr * (a - xhat * (_pair_sum(xhat * a, lo) * (1.0 / HEAD_DIM)))


def _fold_heads(acc):
    tot = acc[:, 0:LANES]
    for p in range(1, acc.shape[1] // LANES):
        tot = tot + acc[:, p * LANES:(p + 1) * LANES]
    return tot + pltpu.roll(tot, HEAD_DIM, axis=1)


def _lane_pick(v, lane, idx):
    return jnp.sum(jnp.where(lane == idx, v, 0.0), axis=-1, keepdims=True)


NT = (((1,), (1,)), ((), ()))
TN = (((0,), (0,)), ((), ()))


def _dot(a, b, dims=None):
    if dims is None:
        return jnp.dot(a, b, preferred_element_type=F32)
    return lax.dot_general(a, b, dims, preferred_element_type=F32)


def _my_place():
    return lax.axis_index("x"), lax.axis_index("y"), lax.axis_index("c")


def _half_dims(shape, axis):
    return (shape[0] // 2, shape[1]) if axis == 0 else (shape[0], shape[1] // 2)


def _half_of(ref, axis, core, lead=False):
    rows, cols = ref.shape[-2:]
    if axis == 0:
        idx = (pl.ds(pl.multiple_of(core * (rows // 2), 16), rows // 2), slice(None))
    else:
        idx = (slice(None), pl.ds(pl.multiple_of(core * (cols // 2), LANES), cols // 2))
    return ref.at[(slice(None),) + idx] if lead else ref.at[idx]


class _HalfGather:
    def __init__(self, ins, outs, axes, f32_bufs, bf_bufs, send_sems, recv_sems, local_sems):
        self.ins, self.outs, self.axes = ins, outs, axes
        self.f32_bufs, self.bf_bufs = f32_bufs, bf_bufs
        self.send_sems, self.recv_sems, self.local_sems = send_sems, recv_sems, local_sems
        self.n = len(ins)
        x, y, self.c = _my_place()
        self.me, self.sibling = (x, y, self.c), (x, y, 1 - self.c)
        self.chips = [(1 - x, y), (x, 1 - y), (1 - x, 1 - y)]

    @staticmethod
    def scratch(shards, axes):
        dims = [_half_dims(a.shape, axis) for a, axis in zip(shards, axes)]
        n = len(shards)
        return [pltpu.VMEM(d, F32) for d in dims] + [pltpu.VMEM(d, BF16) for d in dims] + [
            pltpu.SemaphoreType.DMA((7 * n,)), pltpu.SemaphoreType.DMA((7 * n,)), pltpu.SemaphoreType.DMA((2 * n,))]

    @staticmethod
    def out_shapes(shards, axes):
        return tuple(jax.ShapeDtypeStruct((8,) + _half_dims(a.shape, axis), BF16) for a, axis in zip(shards, axes))

    def _blk(self, a, px, py, pc):
        return self.outs[a].at[4 * px + 2 * py + pc]

    def _copy(self, a, k, block, to, src=None):
        return pltpu.make_async_remote_copy(
            src_ref=self._blk(a, *block) if src is None else src, dst_ref=self._blk(a, *block),
            send_sem=self.send_sems.at[7 * a + k], recv_sem=self.recv_sems.at[7 * a + k], device_id=to,
            device_id_type=MESH)

    def _keep(self, a):
        return pltpu.make_async_copy(self.bf_bufs[a], self._blk(a, *self.me), self.local_sems.at[self.n + a])

    def _first(self, a):
        mine = [self._copy(a, 0, self.me, self.sibling, src=self.bf_bufs[a])]
        return mine + [self._copy(a, 1 + j, self.me, (*chip, self.c), src=self.bf_bufs[a])
                       for j, chip in enumerate(self.chips)]

    def send_mine(self):
        loads = [pltpu.make_async_copy(_half_of(self.ins[a], self.axes[a], self.c), self.f32_bufs[a],
                                       self.local_sems.at[a]) for a in range(self.n)]
        for cp in loads:
            cp.start()
        for a in range(self.n):
            loads[a].wait()
            self.bf_bufs[a][...] = self.f32_bufs[a][...].astype(BF16)
            self._keep(a).start()
            for cp in self._first(a):
                cp.start()

    def pass_on(self):
        for a in range(self.n):
            for j, chip in enumerate(self.chips):
                self._copy(a, 1 + j, (*chip, self.c), self.me).wait_recv()
                self._copy(a, 4 + j, (*chip, self.c), self.sibling).start()

    def finish(self):
        for a in range(self.n):
            self._copy(a, 0, self.sibling, self.me).wait_recv()
            for j, chip in enumerate(self.chips):
                self._copy(a, 4 + j, (*chip, 1 - self.c), self.me).wait_recv()
        for a in range(self.n):
            for cp in self._first(a):
                cp.wait_send()
            for j, chip in enumerate(self.chips):
                self._copy(a, 4 + j, (*chip, self.c), self.sibling).wait_send()
            self._keep(a).wait()


def _all_gather_weights(shards, axes):
    n = len(shards)

    def body(*refs):
        gather = _HalfGather(refs[0:n], refs[n:2 * n], axes, refs[2 * n:3 * n], refs[3 * n:4 * n], *refs[4 * n:])
        gather.send_mine()
        gather.pass_on()
        gather.finish()

    any_spec = pl.BlockSpec(memory_space=pl.ANY)
    return pl.pallas_call(
        body, name="weights_all_gather", out_shape=_HalfGather.out_shapes(shards, axes),
        in_specs=[any_spec] * n, out_specs=(any_spec,) * n, scratch_shapes=_HalfGather.scratch(shards, axes),
        compiler_params=pltpu.CompilerParams(vmem_limit_bytes=VMEM_LIMIT),
    )(*shards)


class _ShardReduce:
    SEMS = 8
    LOCAL = 5

    def __init__(self, g_refs, out_refs, axes, bufs, send_sems, recv_sems, local_sems):
        self.g_refs, self.out_refs, self.axes = g_refs, out_refs, axes
        self.recv_a, self.own_a, self.send_b, self.recv_b, self.fin = bufs
        self.send_sems, self.recv_sems, self.local_sems = send_sems, recv_sems, local_sems
        self.n = len(g_refs)
        x, y, self.c = _my_place()
        self.chip = 2 * x + y
        self.sibling = (x, y, 1 - self.c)

    @staticmethod
    def scratch(gparts, axes):
        dims = [_half_dims(g.shape[1:], axis) for g, axis in zip(gparts, axes)]
        shapes = []
        for dtype, lead in ((F32, (4,)), (F32, (4,)), (BF16, (4,)), (BF16, (4,)), (F32, ())):
            shapes += [pltpu.VMEM(lead + d, dtype) for d in dims]
        return shapes

    def _to_sibling(self, a, j):
        return pltpu.make_async_remote_copy(
            src_ref=_half_of(self.g_refs[a].at[j], self.axes[a], 1 - self.c), dst_ref=self.recv_a[a].at[j],
            send_sem=self.send_sems.at[self.SEMS * a + j], recv_sem=self.recv_sems.at[self.SEMS * a + j], device_id=self.sibling,
            device_id_type=MESH)

    def _own(self, a, j):
        return pltpu.make_async_copy(_half_of(self.g_refs[a].at[j], self.axes[a], self.c), self.own_a[a].at[j],
                                     self.local_sems.at[self.LOCAL * a + j])

    def _to_chip(self, a, k):
        dest = (self.chip + k) % 4
        return pltpu.make_async_remote_copy(
            src_ref=self.send_b[a].at[dest], dst_ref=self.recv_b[a].at[self.chip],
            send_sem=self.send_sems.at[self.SEMS * a + 3 + k], recv_sem=self.recv_sems.at[self.SEMS * a + 3 + k],
            device_id=(dest // 2, dest % 2, self.c), device_id_type=MESH)

    def _give(self, a):
        return pltpu.make_async_remote_copy(
            src_ref=self.fin[a], dst_ref=_half_of(self.out_refs[a], self.axes[a], self.c),
            send_sem=self.send_sems.at[self.SEMS * a + 7], recv_sem=self.recv_sems.at[self.SEMS * a + 7], device_id=self.sibling,
            device_id_type=MESH)

    def _mine(self, a):
        return pltpu.make_async_copy(self.fin[a], _half_of(self.out_refs[a], self.axes[a], self.c),
                                     self.local_sems.at[self.LOCAL * a])

    def exchange_with_sibling(self):
        for k in (1, 2, 3, 0):
            j = (self.chip + k) % 4
            for a in range(self.n):
                self._to_sibling(a, j).start()
                self._own(a, j).start()

    def _chip_partial(self, a, j):
        self._own(a, j).wait()
        self._to_sibling(a, j).wait_recv()
        self.send_b[a][j] = (self.own_a[a][j] + self.recv_a[a][j]).astype(BF16)

    def send_to_chip(self, k):
        for a in range(self.n):
            self._chip_partial(a, (self.chip + k) % 4)
            self._to_chip(a, k).start()

    def keep_mine(self):
        for a in range(self.n):
            self._chip_partial(a, self.chip)
            keep = pltpu.make_async_copy(self.send_b[a].at[self.chip], self.recv_b[a].at[self.chip],
                                         self.local_sems.at[self.LOCAL * a + 4])
            keep.start()
            keep.wait()

    def sum_and_share(self):
        for a in range(self.n):
            for k in range(1, 4):
                self._to_chip(a, k).wait_recv()
            tot = self.recv_b[a][0].astype(F32) + self.recv_b[a][1].astype(F32)
            tot = tot + self.recv_b[a][2].astype(F32)
            self.fin[a][...] = tot + self.recv_b[a][3].astype(F32)
            self._give(a).start()
            self._mine(a).start()

    def finish(self):
        for a in range(self.n):
            self._give(a).wait_recv()
            self._mine(a).wait()
            self._give(a).wait_send()
            for j in range(4):
                self._to_sibling(a, j).wait_send()
            for k in range(1, 4):
                self._to_chip(a, k).wait_send()


def _mem_fwd(mem, mem_norm_g, w_kv, mk_g):
    n = mem.shape[0]

    def body(mem_ref, g_ref, w_ref, kg_ref, mn_ref, kv_ref, kn_ref, vm_ref):
        xm = mem_ref[...]
        rr = lax.rsqrt(jnp.mean(xm * xm, axis=-1, keepdims=True) + EPS)
        mnb = ((xm * rr) * g_ref[...]).astype(BF16)
        mn_ref[...] = mnb
        kv = _dot(mnb, w_ref[...])
        kv_ref[...] = kv
        lo = _lane_lo((n, LANES))
        for p in range(MEM_WIDTH // LANES):
            sl = slice(p * LANES, (p + 1) * LANES)
            kb = kv[:, sl]
            kn_ref[:, sl] = ((kb * _head_rms(kb, lo)) * kg_ref[:, sl]).astype(BF16)
        vm_ref[...] = kv[:, MEM_WIDTH:].astype(BF16)

    return pl.pallas_call(
        body, name="mem_fwd",
        out_shape=(jax.ShapeDtypeStruct((n, D_MODEL), BF16), jax.ShapeDtypeStruct((n, 2 * MEM_WIDTH), F32),
                   jax.ShapeDtypeStruct((n, MEM_WIDTH), BF16), jax.ShapeDtypeStruct((n, MEM_WIDTH), BF16)),
        compiler_params=pltpu.CompilerParams(vmem_limit_bytes=VMEM_LIMIT),
    )(mem, mem_norm_g, w_kv, mk_g)


AUG_LO = 64
KEY_SUM_LANE = 72
QUERY_SUM_LANE = 80
HEAD_BLOCKS = FOX_HEADS * LANES


def _ones3(lane):
    return jnp.where((lane >= AUG_LO) & (lane < AUG_LO + 3), 1.0, 0.0)


def _spread3(cols):
    hi = cols.astype(BF16)
    rest = cols - hi.astype(F32)
    mid = rest.astype(BF16)
    low = (rest - mid.astype(F32)).astype(BF16)
    r = lax.broadcasted_iota(jnp.int32, (LANES, HEAD_BLOCKS), 0)
    c = lax.broadcasted_iota(jnp.int32, (LANES, HEAD_BLOCKS), 1)
    out = None
    for k, part in enumerate((hi, mid, low)):
        term = _dot(part, jnp.where(c == r * LANES + (AUG_LO + k), 1.0, 0.0).astype(BF16))
        out = term if out is None else out + term
    return out


def _head_block(pair_blk, hh, lo, extras):
    src = pair_blk if hh == 0 else pltpu.roll(pair_blk, HEAD_DIM, axis=1)
    return jnp.where(lo, src, extras).astype(BF16)


def _pair_block(blk0, blk1, lo):
    return jnp.where(lo, blk0, pltpu.roll(blk1, HEAD_DIM, axis=1))


def _fwd_in(x, norm_g, wp, bf_pad, fq_g, fk_g):
    s = x.shape[0]
    t = TILE
    n = s // t

    def body(x_ref, ng_ref, wp_ref, bf_ref, qg_ref, kg_ref,
             h_ref, pa_ref, qk_ref, qa_ref, ka_ref, va_ref, gb_ref, pm_ref, fb_ref, carry_ref, fcol_ref):
        @pl.when(pl.program_id(0) == 0)
        def _():
            carry_ref[...] = jnp.zeros_like(carry_ref)

        xv = x_ref[...]
        rr = lax.rsqrt(jnp.mean(xv * xv, axis=-1, keepdims=True) + EPS)
        hb = ((xv * rr) * ng_ref[...]).astype(BF16)
        h_ref[...] = hb

        def proj(lo, hi):
            return _dot(hb, wp_ref[lo:hi, :], NT)

        pa_ref[...] = proj(PA_LO, QB_LO)
        gb_ref[...] = proj(GB_LO, PM_LO)
        pm_ref[...] = proj(PM_LO, FB_LO)
        fb = proj(FB_LO, PROJ_PAD)
        fb_ref[...] = fb

        lane = lax.broadcasted_iota(jnp.int32, (t, LANES), 1)
        row = lax.broadcasted_iota(jnp.int32, (t, LANES), 0)
        lo = lane < HEAD_DIM
        z = fb + bf_ref[...]
        lf = -(jnp.maximum(-z, 0.0) + jnp.log1p(jnp.exp(-jnp.abs(z))))
        lf = jnp.where(lane < FOX_HEADS, lf, 0.0)
        sh = 1
        while sh < t:
            lf = lf + jnp.where(row >= sh, pltpu.roll(lf, sh, axis=0), 0.0)
            sh *= 2
        fcum = lf + carry_ref[...]
        fcol_ref[...] = fcum
        carry_ref[...] = fcol_ref[t - 1:t, :]

        ones3 = _ones3(lane)
        minus_f = _spread3(-fcum)
        for seg, g_ref, out_ref, scale in ((QB_LO, qg_ref, qa_ref, ATT_SCALE), (KB_LO, kg_ref, ka_ref, 1.0)):
            raw = proj(seg, seg + FOX_WIDTH)
            qk_ref[:, seg - QB_LO:seg - QB_LO + FOX_WIDTH] = raw
            for p in range(FOX_WIDTH // LANES):
                sl = slice(p * LANES, (p + 1) * LANES)
                blk = raw[:, sl]
                normed = ((blk * _head_rms(blk, lo)) * g_ref[:, sl]) * scale
                for hh in range(2):
                    h = 2 * p + hh
                    if seg == QB_LO:
                        extras = jnp.where(lane == QUERY_SUM_LANE + h, 1.0, ones3)
                    else:
                        extras = jnp.where(lane == KEY_SUM_LANE + h, 1.0, minus_f[:, h * LANES:(h + 1) * LANES])
                    out_ref[:, h * LANES:(h + 1) * LANES] = _head_block(normed, hh, lo, extras)
        vraw = proj(VB_LO, GB_LO)
        for h in range(FOX_HEADS):
            va_ref[:, h * LANES:(h + 1) * LANES] = _head_block(vraw[:, (h // 2) * LANES:(h // 2 + 1) * LANES], h % 2, lo, ones3)

    outs = (
        jax.ShapeDtypeStruct((s, D_MODEL), BF16),
        jax.ShapeDtypeStruct((s, 512), F32),
        jax.ShapeDtypeStruct((s, 2 * FOX_WIDTH), F32),
        jax.ShapeDtypeStruct((s, HEAD_BLOCKS), BF16),
        jax.ShapeDtypeStruct((s, HEAD_BLOCKS), BF16),
        jax.ShapeDtypeStruct((s, HEAD_BLOCKS), BF16),
        jax.ShapeDtypeStruct((s, FOX_WIDTH), F32),
        jax.ShapeDtypeStruct((s, 512), F32),
        jax.ShapeDtypeStruct((s, LANES), F32),
    )
    return pl.pallas_call(
        body, name="fwd_in", grid=(n,), out_shape=outs,
        in_specs=[_rows(t, D_MODEL), _full((1, D_MODEL)), _full((PROJ_PAD, D_MODEL)), _full((1, LANES)),
                  _full((1, FOX_WIDTH)), _full((1, FOX_WIDTH))],
        out_specs=(_rows(t, D_MODEL), _rows(t, 512), _rows(t, 2 * FOX_WIDTH), _rows(t, HEAD_BLOCKS),
                   _rows(t, HEAD_BLOCKS), _rows(t, HEAD_BLOCKS), _rows(t, FOX_WIDTH), _rows(t, 512),
                   _rows(t, LANES)),
        scratch_shapes=[pltpu.VMEM((1, LANES), F32), pltpu.VMEM((t, LANES), F32)],
        compiler_params=_params(),
    )(x, norm_g, wp, bf_pad, fq_g, fk_g)


POOL_HALO = 16


def _pool_window(lane):
    return jnp.where(lane < 64, 2.0, jnp.where(lane < 128, 4.0, jnp.where(lane < 192, 8.0, 16.0)))


def _pool_pick(lane, s2, s4, s8, s16):
    return jnp.where(lane < 64, s2, jnp.where(lane < 128, s4, jnp.where(lane < 192, s8, s16)))


def _group_onehot(shape, row_is_group_lane):
    r = lax.broadcasted_iota(jnp.int32, shape, 0)
    c = lax.broadcasted_iota(jnp.int32, shape, 1)
    hit = (r % HEAD_DIM == c) if row_is_group_lane else (c % HEAD_DIM == r)
    return jnp.where(hit, 1.0, 0.0).astype(F32)


def _same_group(shape):
    r = lax.broadcasted_iota(jnp.int32, shape, 0)
    c = lax.broadcasted_iota(jnp.int32, shape, 1)
    return (r // HEAD_DIM) == (c // HEAD_DIM)


def _pool_block_diag(w4):
    spread = jnp.dot(w4, _group_onehot((HEAD_DIM, POOL_WIDTH), False), preferred_element_type=F32,
                     precision=lax.Precision.HIGHEST)
    return jnp.where(_same_group((POOL_WIDTH, POOL_WIDTH)), spread, 0.0).astype(BF16)


def _pool_fwd(pa, w4, pscale):
    s = pa.shape[0]
    t = TILE
    n = s // t
    ext = t + POOL_HALO

    def body(pa_ref, w4_ref, sc_ref, ma_ref, d_ref, ext_ref, w_ref):
        i = pl.program_id(0)

        @pl.when(i == 0)
        def _():
            ext_ref[0:POOL_HALO, :] = jnp.zeros((POOL_HALO, POOL_WIDTH), F32)
            w_ref[...] = _pool_block_diag(w4_ref[...])

        u = pa_ref[:, 0:POOL_WIDTH]
        ext_ref[POOL_HALO:ext, :] = u
        e = ext_ref[...]
        s2 = e + pltpu.roll(e, 1, axis=0)
        s4 = s2 + pltpu.roll(s2, 2, axis=0)
        s8 = s4 + pltpu.roll(s4, 4, axis=0)
        s16 = s8 + pltpu.roll(s8, 8, axis=0)
        lane_e = lax.broadcasted_iota(jnp.int32, (ext, POOL_WIDTH), 1)
        win = _pool_pick(lane_e, s2, s4, s8, s16)[POOL_HALO:ext, :]
        lane = lax.broadcasted_iota(jnp.int32, (t, POOL_WIDTH), 1)
        pos = (lax.broadcasted_iota(jnp.int32, (t, POOL_WIDTH), 0) + (i * t + 1)).astype(F32)
        d = win / jnp.minimum(pos, _pool_window(lane)) - u
        db = d.astype(BF16)
        d_ref[...] = db
        ya = _dot(db, w_ref[...]) * sc_ref[...]
        ga = pa_ref[:, POOL_WIDTH:2 * POOL_WIDTH]
        ma_ref[...] = (ya * (ga * _sig(ga))).astype(BF16)
        ext_ref[0:POOL_HALO, :] = ext_ref[t:ext, :]

    return pl.pallas_call(
        body, name="pool_fwd", grid=(n,),
        out_shape=(jax.ShapeDtypeStruct((s, POOL_WIDTH), BF16), jax.ShapeDtypeStruct((s, POOL_WIDTH), BF16)),
        in_specs=[_rows(t, 512), _full((POOL_ROWS, HEAD_DIM)), _full((1, POOL_WIDTH))],
        out_specs=(_rows(t, POOL_WIDTH), _rows(t, POOL_WIDTH)),
        scratch_shapes=[pltpu.VMEM((ext, POOL_WIDTH), F32), pltpu.VMEM((POOL_WIDTH, POOL_WIDTH), BF16)],
        compiler_params=_params(),
    )(pa, w4, pscale)


def _mem_softmax(qm, kp):
    sc = _dot(qm, kp, NT)
    e = jnp.exp(sc - jnp.max(sc, axis=-1, keepdims=True))
    return e * (1.0 / jnp.sum(e, axis=-1, keepdims=True))


def _mem_attn_fwd(pm, kmn, vmb, mq_g):
    s = pm.shape[0]
    t = TILE
    n = s // t

    def body(pm_ref, k_ref, v_ref, g_ref, mm_ref):
        lo = _lane_lo((t, LANES))
        for p in range(MEM_WIDTH // LANES):
            sl = slice(p * LANES, (p + 1) * LANES)
            qb = pm_ref[:, sl]
            qs = (((qb * _head_rms(qb, lo)) * g_ref[:, sl]) * ATT_SCALE).astype(BF16)
            kp = k_ref[:, sl]
            vp = v_ref[:, sl]
            outs = []
            for hh in range(2):
                msk = lo if hh == 0 else jnp.logical_not(lo)
                prob = _mem_softmax(jnp.where(msk, qs, jnp.zeros_like(qs)), kp)
                outs.append(_dot(prob.astype(BF16), vp))
            o = jnp.where(lo, outs[0], outs[1])
            gm = pm_ref[:, MEM_WIDTH + p * LANES:MEM_WIDTH + (p + 1) * LANES]
            mm_ref[:, sl] = (o * (gm * _sig(gm))).astype(BF16)

    return pl.pallas_call(
        body, name="mem_attn_fwd", grid=(n,),
        out_shape=jax.ShapeDtypeStruct((s, MEM_WIDTH), BF16),
        in_specs=[_rows(t, 512), _full((N_MEM, MEM_WIDTH)), _full((N_MEM, MEM_WIDTH)), _full((1, MEM_WIDTH))],
        out_specs=_rows(t, MEM_WIDTH),
        compiler_params=_params(),
    )(pm, kmn, vmb, mq_g)


FOX_FWD_HEADS = 4


def _fox_fwd(qa, ka, va, gb):
    s = qa.shape[0]
    t = TILE
    n = s // t
    heads = FOX_FWD_HEADS
    pairs = heads // 2
    group_w = heads * LANES

    def body(qa_ref, ka_ref, va_ref, gb_ref, o_ref, mb_ref, r_ref):
        i = pl.program_id(1)
        lane = lax.broadcasted_iota(jnp.int32, (t, LANES), 1)
        lo = lane < HEAD_DIM
        causal = lax.broadcasted_iota(jnp.int32, (t, t), 1) <= lax.broadcasted_iota(jnp.int32, (t, t), 0)
        qas = [qa_ref[:, hh * LANES:(hh + 1) * LANES] for hh in range(heads)]

        def step(j, carry, masked):
            rows = pl.ds(pl.multiple_of(j * t, t), t)
            new = []
            for hh in range(heads):
                cols = slice(hh * LANES, (hh + 1) * LANES)
                m, acc = carry[hh]
                sc = _dot(qas[hh], ka_ref[rows, cols], NT)
                if masked:
                    sc = jnp.where(causal, sc, -1e30)
                m_new = jnp.maximum(m, jnp.max(sc, axis=-1, keepdims=True))
                acc = jnp.exp(m - m_new) * acc + _dot(jnp.exp(sc - m_new).astype(BF16), va_ref[rows, cols])
                new.append((m_new, acc))
            return tuple(new)

        init = (jnp.full((t, 1), -1e30, F32), jnp.zeros((t, LANES), F32))
        carry = lax.fori_loop(0, i, functools.partial(step, masked=False), (init,) * heads)
        res = step(i, carry, masked=True)
        for p in range(pairs):
            outs = []
            rcol = jnp.zeros((t, LANES), F32)
            for hh in range(2):
                m, acc = res[2 * p + hh]
                l = _lane_pick(acc, lane, AUG_LO)
                outs.append(acc * (1.0 / l))
                rcol = jnp.where(lane == hh, m + jnp.log(l), rcol)
            o = _pair_block(outs[0], outs[1], lo)
            sl = slice(p * LANES, (p + 1) * LANES)
            o_ref[:, sl] = o
            g = gb_ref[:, sl]
            mb_ref[:, sl] = (o * (g * _sig(g))).astype(BF16)
            r_ref[p] = rcol

    tile_spec = pl.BlockSpec((t, pairs * LANES), lambda p, i: (i, p))
    full_spec = pl.BlockSpec((s, group_w), lambda p, i: (0, p))
    return pl.pallas_call(
        body, name="fox_fwd", grid=(FOX_HEADS // heads, n),
        out_shape=(jax.ShapeDtypeStruct((s, FOX_WIDTH), F32), jax.ShapeDtypeStruct((s, FOX_WIDTH), BF16),
                   jax.ShapeDtypeStruct((FOX_HEADS // 2, s, LANES), F32)),
        in_specs=[pl.BlockSpec((t, group_w), lambda p, i: (i, p)), full_spec, full_spec, tile_spec],
        out_specs=(tile_spec, tile_spec, pl.BlockSpec((pairs, t, LANES), lambda p, i: (p, i, 0))),
        compiler_params=_params(2),
    )(qa, ka, va, gb)


def _out_loss(x, tgt, ma, mb, mm, wout, gb, o, r4):
    s = x.shape[0]
    t = TILE
    n = s // t
    pairs = FOX_HEADS // 2

    def body(x_ref, t_ref, ma_ref, mb_ref, mm_ref, w_ref, gb_ref, o_ref, r_ref,
             dy_ref, dma_ref, dmm_ref, dw_ref, loss_ref, doa_ref, dgb_ref, rr_ref, mix_ref):
        @pl.when(pl.program_id(0) == 0)
        def _():
            dw_ref[...] = jnp.zeros_like(dw_ref)
            loss_ref[...] = jnp.zeros_like(loss_ref)

        mix_ref[:, 0:256] = ma_ref[...]
        mix_ref[:, 256:768] = mb_ref[...]
        mix_ref[:, 768:1024] = mm_ref[...]
        mix = mix_ref[...]
        err = (x_ref[...] + _dot(mix, w_ref[...])) - t_ref[...]
        row_mean = jnp.sum(err * err, axis=-1, keepdims=True) * (1.0 / D_MODEL)
        loss_ref[...] += 0.5 * jnp.sum(row_mean, axis=0, keepdims=True)
        dy = err * (1.0 / D_MODEL)
        dy_ref[...] = dy
        dyb = dy.astype(BF16)
        dmix = _dot(dyb, w_ref[...], NT)
        dma_ref[...] = dmix[:, 0:256]
        dmm_ref[...] = dmix[:, 768:1024]
        dw_ref[...] += _dot(mix, dyb, TN)

        lane = lax.broadcasted_iota(jnp.int32, (t, LANES), 1)
        lo = lane < HEAD_DIM
        d_os = []
        delta = jnp.zeros((t, LANES), F32)
        for p in range(pairs):
            sl = slice(p * LANES, (p + 1) * LANES)
            g = gb_ref[:, sl]
            sg = _sig(g)
            dm = dmix[:, 256 + p * LANES:256 + (p + 1) * LANES]
            ov = o_ref[:, sl]
            d_o = dm * (g * sg)
            d_os.append(d_o)
            dgb_ref[:, sl] = (dm * ov * (sg * (1.0 + g * (1.0 - sg)))).astype(BF16)
            prod = d_o * ov
            delta = jnp.where(lane == 2 * p, jnp.sum(jnp.where(lo, prod, 0.0), axis=-1, keepdims=True), delta)
            delta = jnp.where(lane == 2 * p + 1, jnp.sum(jnp.where(lo, 0.0, prod), axis=-1, keepdims=True), delta)
            rr_ref[p, 0] = r_ref[p].T[0:8, :]
        minus_delta = _spread3(-delta)
        for h in range(FOX_HEADS):
            blk = slice(h * LANES, (h + 1) * LANES)
            doa_ref[:, blk] = _head_block(d_os[h // 2], h % 2, lo, minus_delta[:, blk])

    return pl.pallas_call(
        body, name="out_loss", grid=(n,),
        out_shape=(jax.ShapeDtypeStruct((s, D_MODEL), F32), jax.ShapeDtypeStruct((s, 256), F32),
                   jax.ShapeDtypeStruct((s, 256), F32), jax.ShapeDtypeStruct((D_MODEL, D_MODEL), F32),
                   jax.ShapeDtypeStruct((1, LANES), F32), jax.ShapeDtypeStruct((s, HEAD_BLOCKS), BF16),
                   jax.ShapeDtypeStruct((s, FOX_WIDTH), BF16), jax.ShapeDtypeStruct((pairs, n, 8, t), F32)),
        in_specs=[_rows(t, D_MODEL), _rows(t, D_MODEL), _rows(t, 256), _rows(t, 512), _rows(t, 256),
                  _full((D_MODEL, D_MODEL)), _rows(t, FOX_WIDTH), _rows(t, FOX_WIDTH),
                  pl.BlockSpec((pairs, t, LANES), lambda i: (0, i, 0))],
        out_specs=(_rows(t, D_MODEL), _rows(t, 256), _rows(t, 256), _full((D_MODEL, D_MODEL)), _full((1, LANES)),
                   _rows(t, HEAD_BLOCKS), _rows(t, FOX_WIDTH), pl.BlockSpec((pairs, 1, 8, t), lambda i: (0, i, 0, 0))),
        scratch_shapes=[pltpu.VMEM((t, D_MODEL), BF16)],
        compiler_params=_params(),
    )(x, tgt, ma, mb, mm, wout, gb, o, r4)


def _mem_attn_bwd(pm, dmm, kmn, vmb, mq_g):
    s = pm.shape[0]
    t = TILE
    n = s // t

    def body(pm_ref, dmm_ref, k_ref, v_ref, g_ref, dpm_ref, dk_ref, dv_ref, dg_ref, gacc_ref):
        @pl.when(pl.program_id(0) == 0)
        def _():
            dk_ref[...] = jnp.zeros_like(dk_ref)
            dv_ref[...] = jnp.zeros_like(dv_ref)
            gacc_ref[...] = jnp.zeros_like(gacc_ref)

        lo = _lane_lo((t, LANES))
        for p in range(MEM_WIDTH // LANES):
            sl = slice(p * LANES, (p + 1) * LANES)
            qb = pm_ref[:, sl]
            rr = _head_rms(qb, lo)
            qhat = qb * rr
            g = g_ref[:, sl]
            qs = ((qhat * g) * ATT_SCALE).astype(BF16)
            gm = pm_ref[:, MEM_WIDTH + p * LANES:MEM_WIDTH + (p + 1) * LANES]
            sg = _sig(gm)
            dmo = dmm_ref[:, sl]
            d_o = dmo * (gm * sg)
            kp = k_ref[:, sl]
            vp = v_ref[:, sl]
            outs, dqs = [], []
            for hh in range(2):
                msk = lo if hh == 0 else jnp.logical_not(lo)
                qm = jnp.where(msk, qs, jnp.zeros_like(qs))
                prob = _mem_softmax(qm, kp)
                pb = prob.astype(BF16)
                outs.append(_dot(pb, vp))
                dom = jnp.where(msk, d_o, 0.0).astype(BF16)
                dp = _dot(dom, vp, NT)
                ds = (prob * (dp - jnp.sum(prob * dp, axis=-1, keepdims=True))).astype(BF16)
                dqs.append(_dot(ds, kp))
                dk_ref[:, sl] += _dot(ds, qm, TN)
                dv_ref[:, sl] += _dot(pb, dom, TN)
            o = jnp.where(lo, outs[0], outs[1])
            dqn = jnp.where(lo, dqs[0], dqs[1]) * ATT_SCALE
            dpm_ref[:, sl] = _head_norm_bwd(dqn, qhat, rr, g, lo).astype(BF16)
            dpm_ref[:, MEM_WIDTH + p * LANES:MEM_WIDTH + (p + 1) * LANES] = (
                dmo * o * (sg * (1.0 + gm * (1.0 - sg)))).astype(BF16)
            gacc_ref[:, sl] += jnp.sum(dqn * qhat, axis=0, keepdims=True)

        @pl.when(pl.program_id(0) == n - 1)
        def _():
            dg_ref[...] = _fold_heads(gacc_ref[...])

    return pl.pallas_call(
        body, name="mem_attn_bwd", grid=(n,),
        out_shape=(jax.ShapeDtypeStruct((s, 512), BF16), jax.ShapeDtypeStruct((N_MEM, MEM_WIDTH), F32),
                   jax.ShapeDtypeStruct((N_MEM, MEM_WIDTH), F32), jax.ShapeDtypeStruct((1, LANES), F32)),
        in_specs=[_rows(t, 512), _rows(t, MEM_WIDTH), _full((N_MEM, MEM_WIDTH)), _full((N_MEM, MEM_WIDTH)),
                  _full((1, MEM_WIDTH))],
        out_specs=(_rows(t, 512), _full((N_MEM, MEM_WIDTH)), _full((N_MEM, MEM_WIDTH)), _full((1, LANES))),
        scratch_shapes=[pltpu.VMEM((1, MEM_WIDTH), F32)],
        compiler_params=_params(),
    )(pm, dmm, kmn, vmb, mq_g)


def _mem_bwd(dkn, dvm, kv, mnb, mem, w_kv, mk_g, mem_norm_g):
    n = mem.shape[0]

    def body(dkn_ref, dvm_ref, kv_ref, mn_ref, mem_ref, w_ref, kg_ref, g_ref, dw_ref, dg_ref, dkg_ref, dkv_ref):
        lo = _lane_lo((n, LANES))
        gacc = []
        for p in range(MEM_WIDTH // LANES):
            sl = slice(p * LANES, (p + 1) * LANES)
            kb = kv_ref[:, sl]
            rr = _head_rms(kb, lo)
            khat = kb * rr
            dk = dkn_ref[:, sl]
            dkv_ref[:, sl] = _head_norm_bwd(dk, khat, rr, kg_ref[:, sl], lo).astype(BF16)
            gacc.append(jnp.sum(dk * khat, axis=0, keepdims=True))
        dkg_ref[...] = _fold_heads(jnp.concatenate(gacc, axis=1))
        dkv_ref[:, MEM_WIDTH:] = dvm_ref[...].astype(BF16)
        dkv = dkv_ref[...]
        dw_ref[...] = _dot(mn_ref[...], dkv, TN)
        dmn = _dot(dkv, w_ref[...], NT)
        xm = mem_ref[...]
        rr = lax.rsqrt(jnp.mean(xm * xm, axis=-1, keepdims=True) + EPS)
        dg_ref[...] = jnp.sum(dmn * (xm * rr), axis=0, keepdims=True)

    return pl.pallas_call(
        body, name="mem_bwd",
        out_shape=(jax.ShapeDtypeStruct((D_MODEL, 2 * MEM_WIDTH), F32), jax.ShapeDtypeStruct((1, D_MODEL), F32),
                   jax.ShapeDtypeStruct((1, LANES), F32)),
        scratch_shapes=[pltpu.VMEM((n, 2 * MEM_WIDTH), BF16)],
        compiler_params=pltpu.CompilerParams(vmem_limit_bytes=VMEM_LIMIT),
    )(dkn, dvm, kv, mnb, mem, w_kv, mk_g, mem_norm_g)


def _pool_bwd(pa, db, dma, w4, pscale):
    s = pa.shape[0]
    t = TILE
    n = s // t
    ext = t + POOL_HALO

    def body(pa_ref, d_ref, dma_ref, w4_ref, sc_ref, dpa_ref, dw4_ref, dsc_ref, ext_ref, w_ref, dw_ref):
        i = pl.program_id(0)

        @pl.when(i == 0)
        def _():
            dw_ref[...] = jnp.zeros_like(dw_ref)
            dsc_ref[...] = jnp.zeros_like(dsc_ref)
            ext_ref[t:ext, :] = jnp.zeros((POOL_HALO, POOL_WIDTH), F32)
            w_ref[...] = _pool_block_diag(w4_ref[...])

        dbv = d_ref[...]
        z = _dot(dbv, w_ref[...])
        ga = pa_ref[:, POOL_WIDTH:2 * POOL_WIDTH]
        sg = _sig(ga)
        dma_v = dma_ref[...]
        dya = dma_v * (ga * sg)
        dpa_ref[:, POOL_WIDTH:2 * POOL_WIDTH] = (dma_v * (z * sc_ref[...]) * (sg * (1.0 + ga * (1.0 - sg)))).astype(BF16)
        dsc_ref[...] += jnp.sum(dya * z, axis=0, keepdims=True)
        dzb = (dya * sc_ref[...]).astype(BF16)
        dw_ref[...] += _dot(dbv, dzb, TN)
        dd = _dot(dzb, w_ref[...], NT)
        lane = lax.broadcasted_iota(jnp.int32, (t, POOL_WIDTH), 1)
        pos = (lax.broadcasted_iota(jnp.int32, (t, POOL_WIDTH), 0) + ((n - 1 - i) * t + 1)).astype(F32)
        ext_ref[0:t, :] = dd / jnp.minimum(pos, _pool_window(lane))
        e = ext_ref[...]
        s2 = e + pltpu.roll(e, ext - 1, axis=0)
        s4 = s2 + pltpu.roll(s2, ext - 2, axis=0)
        s8 = s4 + pltpu.roll(s4, ext - 4, axis=0)
        s16 = s8 + pltpu.roll(s8, ext - 8, axis=0)
        lane_e = lax.broadcasted_iota(jnp.int32, (ext, POOL_WIDTH), 1)
        win = _pool_pick(lane_e, s2, s4, s8, s16)[0:t, :]
        dpa_ref[:, 0:POOL_WIDTH] = (win - dd).astype(BF16)
        ext_ref[t:ext, :] = ext_ref[0:POOL_HALO, :]

        @pl.when(i == n - 1)
        def _():
            own = jnp.where(_same_group((POOL_WIDTH, POOL_WIDTH)), dw_ref[...], 0.0)
            dw4_ref[...] = jnp.dot(own, _group_onehot((POOL_WIDTH, HEAD_DIM), True), preferred_element_type=F32,
                                   precision=lax.Precision.HIGHEST)

    return pl.pallas_call(
        body, name="pool_bwd", grid=(n,),
        out_shape=(jax.ShapeDtypeStruct((s, 512), BF16), jax.ShapeDtypeStruct((POOL_ROWS, HEAD_DIM), F32),
                   jax.ShapeDtypeStruct((1, POOL_WIDTH), F32)),
        in_specs=[_rows_rev(t, 512, n), _rows_rev(t, POOL_WIDTH, n), _rows_rev(t, POOL_WIDTH, n),
                  _full((POOL_ROWS, HEAD_DIM)), _full((1, POOL_WIDTH))],
        out_specs=(_rows_rev(t, 512, n), _full((POOL_ROWS, HEAD_DIM)), _full((1, POOL_WIDTH))),
        scratch_shapes=[pltpu.VMEM((ext, POOL_WIDTH), F32), pltpu.VMEM((POOL_WIDTH, POOL_WIDTH), BF16),
                        pltpu.VMEM((POOL_WIDTH, POOL_WIDTH), F32)],
        compiler_params=_params(),
    )(pa, db, dma, w4, pscale)


FOX_BWD_HEADS = 4


def _fox_bwd(ka, va, qa, doa, rr):
    s = ka.shape[0]
    t = TILE
    n = s // t
    heads = FOX_BWD_HEADS
    group_w = heads * LANES

    def body(ka_ref, va_ref, qa_ref, doa_ref, rr_ref, dka_ref, dva_ref, dqa_ref):
        j = pl.program_id(1)

        @pl.when(j == 0)
        def _():
            dqa_ref[...] = jnp.zeros_like(dqa_ref)

        causal = lax.broadcasted_iota(jnp.int32, (t, t), 0) <= lax.broadcasted_iota(jnp.int32, (t, t), 1)
        kas = [ka_ref[:, hh * LANES:(hh + 1) * LANES] for hh in range(heads)]
        vas = [va_ref[:, hh * LANES:(hh + 1) * LANES] for hh in range(heads)]

        def step(i, carry, masked):
            rows = pl.ds(pl.multiple_of(i * t, t), t)
            new = []
            for hh in range(heads):
                cols = slice(hh * LANES, (hh + 1) * LANES)
                dk_a, dv_a = carry[hh]
                qb = qa_ref[rows, cols]
                d_o = doa_ref[rows, cols]
                arg = _dot(kas[hh], qb, NT) - rr_ref[hh // 2, i, hh % 2:hh % 2 + 1, :]
                if masked:
                    arg = jnp.where(causal, arg, -1e30)
                pt = jnp.exp(arg)
                dst = (pt * _dot(vas[hh], d_o, NT)).astype(BF16)
                dv_a = dv_a + _dot(pt.astype(BF16), d_o)
                dk_a = dk_a + _dot(dst, qb)
                dqa_ref[rows, cols] += _dot(dst, kas[hh], TN)
                new.append((dk_a, dv_a))
            return tuple(new)

        zero = jnp.zeros((t, LANES), F32)
        carry = step(j, ((zero, zero),) * heads, masked=True)
        res = lax.fori_loop(j + 1, n, functools.partial(step, masked=False), carry)
        for hh in range(heads):
            cols = slice(hh * LANES, (hh + 1) * LANES)
            dka_ref[:, cols] = res[hh][0]
            dva_ref[:, cols] = res[hh][1]

    tile_spec = pl.BlockSpec((t, group_w), lambda p, j: (j, p))
    full_spec = pl.BlockSpec((s, group_w), lambda p, j: (0, p))
    return pl.pallas_call(
        body, name="fox_bwd", grid=(FOX_HEADS // heads, n),
        out_shape=(jax.ShapeDtypeStruct((s, HEAD_BLOCKS), F32),) * 3,
        in_specs=[tile_spec, tile_spec, full_spec, full_spec,
                  pl.BlockSpec((heads // 2, n, 8, t), lambda p, j: (p, 0, 0, 0))],
        out_specs=(tile_spec, tile_spec, full_spec),
        compiler_params=_params(2),
    )(ka, va, qa, doa, rr)


def _fox_post(dqa, dka, dva, qk, fb, bf_pad, fq_g, fk_g):
    s = dqa.shape[0]
    t = TILE
    n = s // t

    def body(dqa_ref, dka_ref, dva_ref, qk_ref, fb_ref, bf_ref, qg_ref, kg_ref,
             dqk_ref, dv_ref, dfb_ref, dqg_ref, dkg_ref, dbf_ref, qacc_ref, kacc_ref, carry_ref):
        i = pl.program_id(0)

        @pl.when(i == 0)
        def _():
            qacc_ref[...] = jnp.zeros_like(qacc_ref)
            kacc_ref[...] = jnp.zeros_like(kacc_ref)
            dbf_ref[...] = jnp.zeros_like(dbf_ref)
            carry_ref[...] = jnp.zeros_like(carry_ref)

        lane = lax.broadcasted_iota(jnp.int32, (t, LANES), 1)
        row = lax.broadcasted_iota(jnp.int32, (t, LANES), 0)
        lo = lane < HEAD_DIM

        def head_blocks(ref, p):
            return ref[:, 2 * p * LANES:(2 * p + 1) * LANES], ref[:, (2 * p + 1) * LANES:(2 * p + 2) * LANES]

        dq_sum = jnp.zeros((t, LANES), F32)
        dk_sum = jnp.zeros((t, LANES), F32)
        for p in range(FOX_WIDTH // LANES):
            sl = slice(p * LANES, (p + 1) * LANES)
            dq0, dq1 = head_blocks(dqa_ref, p)
            dk0, dk1 = head_blocks(dka_ref, p)
            dv0, dv1 = head_blocks(dva_ref, p)
            dv_ref[:, sl] = _pair_block(dv0, dv1, lo).astype(BF16)
            dq_sum = dq_sum + (dq0 + dq1)
            dk_sum = dk_sum + (dk0 + dk1)
            for off, pair, g_ref, acc_ref, scale in ((0, _pair_block(dq0, dq1, lo), qg_ref, qacc_ref, ATT_SCALE),
                                                     (FOX_WIDTH, _pair_block(dk0, dk1, lo), kg_ref, kacc_ref, 1.0)):
                raw = qk_ref[:, off + p * LANES:off + (p + 1) * LANES]
                rr = _head_rms(raw, lo)
                xhat = raw * rr
                dn = pair * scale
                dqk_ref[:, off + p * LANES:off + (p + 1) * LANES] = _head_norm_bwd(
                    dn, xhat, rr, g_ref[:, sl], lo).astype(BF16)
                acc_ref[:, sl] += jnp.sum(dn * xhat, axis=0, keepdims=True)

        acc = (pltpu.roll(dq_sum, LANES - KEY_SUM_LANE, axis=1) - pltpu.roll(dk_sum, LANES - QUERY_SUM_LANE, axis=1))
        acc = jnp.where(lane < FOX_HEADS, acc, 0.0)
        sh = 1
        while sh < t:
            acc = acc + jnp.where(row < t - sh, pltpu.roll(acc, t - sh, axis=0), 0.0)
            sh *= 2
        dlogf = acc + carry_ref[...]
        dfb_ref[...] = dlogf
        carry_ref[...] = dfb_ref[0:1, :]
        z = fb_ref[...] + bf_ref[...]
        dz = jnp.where(lane < FOX_HEADS, dlogf * (1.0 / (1.0 + jnp.exp(z))), 0.0)
        dfb_ref[...] = dz
        dbf_ref[...] += jnp.sum(dz, axis=0, keepdims=True)

        @pl.when(i == n - 1)
        def _():
            dqg_ref[...] = _fold_heads(qacc_ref[...])
            dkg_ref[...] = _fold_heads(kacc_ref[...])

    return pl.pallas_call(
        body, name="fox_post", grid=(n,),
        out_shape=(jax.ShapeDtypeStruct((s, 2 * FOX_WIDTH), BF16), jax.ShapeDtypeStruct((s, FOX_WIDTH), BF16),
                   jax.ShapeDtypeStruct((s, LANES), F32), jax.ShapeDtypeStruct((1, LANES), F32),
                   jax.ShapeDtypeStruct((1, LANES), F32), jax.ShapeDtypeStruct((1, LANES), F32)),
        in_specs=[_rows_rev(t, HEAD_BLOCKS, n), _rows_rev(t, HEAD_BLOCKS, n), _rows_rev(t, HEAD_BLOCKS, n),
                  _rows_rev(t, 2 * FOX_WIDTH, n), _rows_rev(t, LANES, n), _full((1, LANES)),
                  _full((1, FOX_WIDTH)), _full((1, FOX_WIDTH))],
        out_specs=(_rows_rev(t, 2 * FOX_WIDTH, n), _rows_rev(t, FOX_WIDTH, n), _rows_rev(t, LANES, n),
                   _full((1, LANES)), _full((1, LANES)), _full((1, LANES))),
        scratch_shapes=[pltpu.VMEM((1, FOX_WIDTH), F32), pltpu.VMEM((1, FOX_WIDTH), F32), pltpu.VMEM((1, LANES), F32)],
        compiler_params=_params(),
    )(dqa, dka, dva, qk, fb, bf_pad, fq_g, fk_g)


def _assemble_dproj(dp_ref, dpa_ref, dqk_ref, dv_ref, dgb_ref, dpm_ref, dfb_ref):
    dp_ref[:, PA_LO:QB_LO] = dpa_ref[...]
    dp_ref[:, QB_LO:VB_LO] = dqk_ref[...]
    dp_ref[:, VB_LO:GB_LO] = dv_ref[...]
    dp_ref[:, GB_LO:PM_LO] = dgb_ref[...]
    dp_ref[:, PM_LO:FB_LO] = dpm_ref[...]
    dp_ref[:, FB_LO:PROJ_PAD] = dfb_ref[...].astype(BF16)


def _dproj_specs(t):
    return [_rows(t, 512), _rows(t, 2 * FOX_WIDTH), _rows(t, FOX_WIDTH), _rows(t, FOX_WIDTH), _rows(t, 512),
            _rows(t, LANES)]


IN_BWD_X_TILE = 256


def _in_bwd_x(x, dy, norm_g, wp, dparts, gparts, axes, smalls):
    s = x.shape[0]
    t = IN_BWD_X_TILE
    n = s // t
    na = len(gparts)
    n_dp = len(dparts)
    vec_leaves, loss_row, dw4 = smalls if smalls is not None else ((), None, None)
    nv = len(vec_leaves)
    n_small = nv + 2 if smalls is not None else 0
    small_base = _ShardReduce.SEMS * na

    def body(*refs):
        x_ref, dy_ref, g_ref, wp_ref = refs[0:4]
        dp_parts = refs[4:4 + n_dp]
        o = 4 + n_dp
        g_refs = refs[o:o + na]
        small_in = refs[o + na:o + na + n_small]
        o += na + n_small
        gx_ref, dg_ref = refs[o:o + 2]
        out_refs = refs[o + 2:o + 2 + na]
        small_out = refs[o + 2 + na:o + 2 + na + (2 if smalls is not None else 0)]
        o += 2 + na + len(small_out)
        dp_ref = refs[o]
        bufs = tuple(refs[o + 1 + k * na:o + 1 + (k + 1) * na] for k in range(5))
        rest = refs[o + 1 + 5 * na:]

        i = pl.program_id(0)
        if na or smalls is not None:
            send_sems, recv_sems, local_sems = rest[-3:]
        red = _ShardReduce(g_refs, out_refs, axes, bufs, send_sems, recv_sems, local_sems) if na else None

        @pl.when(i == 0)
        def _():
            dg_ref[...] = jnp.zeros_like(dg_ref)
            if red is not None:
                red.exchange_with_sibling()

        if red is not None:
            for k in (1, 2, 3):
                pl.when(i == k)(functools.partial(red.send_to_chip, k))
            pl.when(i == 4)(red.keep_mine)

        _assemble_dproj(dp_ref, *dp_parts)
        dh = _dot(dp_ref[...], wp_ref[...])
        xv = x_ref[...]
        rr = lax.rsqrt(jnp.mean(xv * xv, axis=-1, keepdims=True) + EPS)
        xhat = xv * rr
        scaled = dh * g_ref[...]
        gx_ref[...] = dy_ref[...] + rr * (scaled - xhat * jnp.mean(xhat * scaled, axis=-1, keepdims=True))
        dg_ref[...] += jnp.sum(dh * xhat, axis=0, keepdims=True)

        def small_all_reduce():
            leaf_refs, (loss_ref, dw4_ref) = small_in[0:nv], small_in[nv:]
            vec_out, dw4_out = small_out
            vec_mine, vec_recv, dw4_recv = rest[0:3]
            cx, cy, c = _my_place()
            me_lin = 4 * cx + 2 * cy + c

            def copy(k, src, dst, base):
                peer = (me_lin + k) % 8
                return pltpu.make_async_remote_copy(
                    src_ref=src, dst_ref=dst.at[me_lin], send_sem=send_sems.at[base + k - 1],
                    recv_sem=recv_sems.at[base + k - 1], device_id=(peer // 4, (peer // 2) % 2, peer % 2),
                    device_id_type=MESH)

            vec_mine[...] = jnp.zeros_like(vec_mine)
            vec_mine[0:1, :] = dg_ref[...]
            for (_, row, _), ref in zip(VEC_LEAVES[1:], leaf_refs):
                vec_mine[row:row + 1, 0:ref.shape[1]] = ref[...]
            vec_mine[VEC_LOSS_ROW:VEC_LOSS_ROW + 1, 0:LANES] = loss_ref[...]
            copies = [copy(k, src, dst, base) for k in range(1, 8)
                      for src, dst, base in ((vec_mine, vec_recv, small_base), (dw4_ref, dw4_recv, small_base + 7))]
            for cp in copies:
                cp.start()
            for cp in copies:
                cp.wait_recv()
            vec_recv[me_lin] = vec_mine[...]
            dw4_recv[me_lin] = dw4_ref[...]
            vtot, wtot = vec_recv[0], dw4_recv[0]
            for d in range(1, 8):
                vtot = vtot + vec_recv[d]
                wtot = wtot + dw4_recv[d]
            vec_out[...] = vtot
            dw4_out[...] = wtot
            for cp in copies:
                cp.wait_send()

        @pl.when(i == n - 1)
        def _():
            if red is not None:
                red.sum_and_share()
            if smalls is not None:
                small_all_reduce()
            if red is not None:
                red.finish()

    any_spec = pl.BlockSpec(memory_space=pl.ANY)
    scratch = [pltpu.VMEM((t, PROJ_PAD), BF16)] + _ShardReduce.scratch(gparts, axes)
    out_shape = [jax.ShapeDtypeStruct((s, D_MODEL), F32), jax.ShapeDtypeStruct((1, D_MODEL), F32)]
    out_shape += [jax.ShapeDtypeStruct(g.shape[1:], F32) for g in gparts]
    out_specs = [_rows(t, D_MODEL), _full((1, D_MODEL))] + [any_spec] * na
    small_args = []
    if smalls is not None:
        small_args = [*vec_leaves, loss_row, dw4]
        out_shape += [jax.ShapeDtypeStruct((VEC_ROWS, D_MODEL), F32), jax.ShapeDtypeStruct(dw4.shape, F32)]
        out_specs += [_full((VEC_ROWS, D_MODEL)), _full(dw4.shape)]
        scratch += [pltpu.VMEM((VEC_ROWS, D_MODEL), F32), pltpu.VMEM((8, VEC_ROWS, D_MODEL), F32),
                    pltpu.VMEM((8,) + dw4.shape, F32)]
    if na or smalls is not None:
        n_sems = small_base + 14
        scratch += [pltpu.SemaphoreType.DMA((n_sems,)), pltpu.SemaphoreType.DMA((n_sems,)),
                    pltpu.SemaphoreType.DMA((max(_ShardReduce.LOCAL * na, 1),))]
    return pl.pallas_call(
        body, name="in_bwd_x", grid=(n,), out_shape=tuple(out_shape),
        in_specs=[_rows(t, D_MODEL), _rows(t, D_MODEL), _full((1, D_MODEL)),
                  pl.BlockSpec((PROJ_PAD, D_MODEL), lambda i: (0, 0), pipeline_mode=pl.Buffered(1))]
        + _dproj_specs(t) + [any_spec] * na + [_full(a.shape) for a in small_args],
        out_specs=tuple(out_specs), scratch_shapes=scratch, compiler_params=_params(),
    )(x, dy, norm_g, wp, *dparts, *gparts, *small_args)


def _in_bwd_w(hb, dparts, gparts, axes):
    s = hb.shape[0]
    t = TILE
    n = s // t
    na = len(gparts)
    f_hi = F_ORIG_LO + FOX_HEADS

    def body(*refs):
        h_ref, dpa_ref, dqk_ref, dv_ref, dgb_ref, dpm_ref, dfb_ref = refs[0:7]
        g_refs = refs[7:7 + na]
        dw_ref = refs[7 + na]
        out_refs = refs[8 + na:8 + 2 * na]
        o = 8 + 2 * na
        bufs = tuple(refs[o + k * na:o + (k + 1) * na] for k in range(5))
        i = pl.program_id(0)
        red = _ShardReduce(g_refs, out_refs, axes, bufs, *refs[o + 5 * na:]) if na else None

        @pl.when(i == 0)
        def _():
            dw_ref[...] = jnp.zeros_like(dw_ref)
            if red is not None:
                red.exchange_with_sibling()

        if red is not None:
            @pl.when(i == 1)
            def _():
                for k in (1, 2, 3):
                    red.send_to_chip(k)
                red.keep_mine()

        hv = h_ref[...]
        for lo, ref in ((0, dpa_ref), (QB_LO, dqk_ref), (VB_LO, dv_ref), (f_hi, dgb_ref), (f_hi + FOX_WIDTH, dpm_ref)):
            dw_ref[lo:lo + ref.shape[1], :] += _dot(ref[...], hv, TN)
        dw_ref[F_ORIG_LO:f_hi, :] += _dot(dfb_ref[...].astype(BF16), hv, TN)[0:FOX_HEADS, :]

        if red is not None:
            @pl.when(i == n - 1)
            def _():
                red.sum_and_share()
                red.finish()

    any_spec = pl.BlockSpec(memory_space=pl.ANY)
    scratch = _ShardReduce.scratch(gparts, axes)
    if na:
        scratch += [pltpu.SemaphoreType.DMA((_ShardReduce.SEMS * na,)), pltpu.SemaphoreType.DMA((_ShardReduce.SEMS * na,)),
                    pltpu.SemaphoreType.DMA((_ShardReduce.LOCAL * na,))]
    return pl.pallas_call(
        body, name="in_bwd_w", grid=(n,),
        out_shape=(jax.ShapeDtypeStruct((IN_WIDTH, D_MODEL), F32),)
        + tuple(jax.ShapeDtypeStruct(g.shape[1:], F32) for g in gparts),
        in_specs=[_rows(t, D_MODEL)] + _dproj_specs(t) + [any_spec] * na,
        out_specs=(_full((IN_WIDTH, D_MODEL)),) + (any_spec,) * na,
        scratch_shapes=scratch, compiler_params=_params(),
    )(hb, *dparts, *gparts)


def _adamw_math(w_ref, gv, m_ref, v_ref, d_ref, nm_ref, nv_ref):
    nm = ADAM_B1 * m_ref[...] + (1.0 - ADAM_B1) * gv
    nv = ADAM_B2 * v_ref[...] + (1.0 - ADAM_B2) * (gv * gv)
    m_hat = nm / (1.0 - ADAM_B1 ** ADAM_STEP)
    v_hat = nv / (1.0 - ADAM_B2 ** ADAM_STEP)
    d_ref[...] = -ADAM_LR * (m_hat / (jnp.sqrt(v_hat) + ADAM_EPS) + ADAM_WD * w_ref[...])
    nm_ref[...] = nm
    nv_ref[...] = nv


def _adamw(name, w, g, m, v):
    rows, cols = w.shape
    tc = 256 if rows * cols > 256 * 1024 else cols
    n = cols // tc

    def body(w_ref, g_ref, m_ref, v_ref, d_ref, nm_ref, nv_ref):
        _adamw_math(w_ref, g_ref[...], m_ref, v_ref, d_ref, nm_ref, nv_ref)

    spec = pl.BlockSpec((rows, tc), lambda i: (0, i))
    return pl.pallas_call(
        body, name=name, grid=(n,),
        out_shape=(jax.ShapeDtypeStruct((rows, cols), F32),) * 3,
        in_specs=[spec] * 4, out_specs=(spec,) * 3,
        compiler_params=_params(),
    )(w, g, m, v)


def _adamw_small(vec, dw4, leaves, pool):
    nl = len(VEC_LEAVES) + 1

    def body(*refs):
        vec_ref, dw4_ref = refs[0:2]
        wmv = refs[2:2 + 3 * nl]
        loss_ref = refs[2 + 3 * nl]
        outs = refs[3 + 3 * nl:]
        loss_ref[...] = vec_ref[VEC_LOSS_ROW:VEC_LOSS_ROW + 1, 0:1]
        for k in range(nl):
            if k < nl - 1:
                _, row, width = VEC_LEAVES[k]
                gv = vec_ref[row:row + 1, 0:width]
            else:
                gv = dw4_ref[...]
            w_ref, m_ref, v_ref = wmv[3 * k:3 * k + 3]
            g_ref, d_ref, nm_ref, nv_ref = outs[4 * k:4 * k + 4]
            g_ref[...] = gv
            _adamw_math(w_ref, gv, m_ref, v_ref, d_ref, nm_ref, nv_ref)

    shapes = [jax.ShapeDtypeStruct((1, width), F32) for _, _, width in VEC_LEAVES] + [
        jax.ShapeDtypeStruct(dw4.shape, F32)]
    flat_in = [a for triple in list(leaves) + [pool] for a in triple]
    res = pl.pallas_call(
        body, name="adamw_small",
        out_shape=(jax.ShapeDtypeStruct((1, 1), F32),) + tuple(s for s in shapes for _ in range(4)),
        compiler_params=pltpu.CompilerParams(vmem_limit_bytes=VMEM_LIMIT),
    )(vec, dw4, *flat_in)
    per = [res[1 + 4 * k:5 + 4 * k] for k in range(nl)]
    return res[0], [p[0] for p in per], [p[1] for p in per], [p[2] for p in per], [p[3] for p in per]


def _full_w_in_padded(halves):
    cols = IN_WIDTH // 4
    w_t = halves.reshape(4, 2, cols, D_MODEL // 2).transpose(0, 2, 1, 3).reshape(IN_WIDTH, D_MODEL)
    return jnp.concatenate([
        w_t[0:F_ORIG_LO], w_t[F_ORIG_LO + FOX_HEADS:], w_t[F_ORIG_LO:F_ORIG_LO + FOX_HEADS],
        jnp.zeros((PROJ_PAD - IN_WIDTH, D_MODEL), w_t.dtype)], axis=0)


def _tile_heads(g, n):
    return jnp.tile(g.reshape(1, HEAD_DIM), (1, n))


def kernel(x, mem, norm_g, w_in, b_f, w_pool, pool_scale, fox_q_g, fox_k_g, mem_norm_g, w_mem_kv, mem_q_g, mem_k_g, w_out, loss_target, m_norm_g, m_w_in, m_b_f, m_w_pool, m_pool_scale, m_fox_q_g, m_fox_k_g, m_mem_norm_g, m_w_mem_kv, m_mem_q_g, m_mem_k_g, m_w_out, v_norm_g, v_w_in, v_b_f, v_w_pool, v_pool_scale, v_fox_q_g, v_fox_k_g, v_mem_norm_g, v_w_mem_kv, v_mem_q_g, v_mem_k_g, v_w_out):
    w_in_t, m_w_in_t, v_w_in_t = w_in[0].T, m_w_in[0].T, v_w_in[0].T
    axes = (1, 0, 0)

    g_in, g_kv, g_out = _all_gather_weights([w_in_t, w_mem_kv[0], w_out[0]], axes)
    wp = _full_w_in_padded(g_in)
    tiled = _tiled_params(b_f, fox_q_g, fox_k_g, mem_q_g, mem_k_g)
    fwd = _fwd_in(x[0], norm_g, wp, *tiled[0:3])
    w_kv_b = g_kv.reshape(D_MODEL, 2 * MEM_WIDTH)
    w_out_b = g_out.reshape(D_MODEL, D_MODEL)
    w4 = w_pool.reshape(POOL_ROWS, HEAD_DIM)
    dy, hb, dparts, dw_kv, dw_out, vec_leaves, loss_row, dw4 = _local_partials(
        x[0], mem[0], loss_target[0], fwd, w_kv_b, w_out_b, tiled, w4, pool_scale, mem_norm_g)

    early = [dw_kv.reshape(4, D_MODEL // 4, 2 * MEM_WIDTH), dw_out.reshape(4, D_MODEL // 4, D_MODEL)]
    dwp, g_w_kv, g_w_out = _in_bwd_w(hb, dparts, early, axes[1:])
    grad_x, _, g_w_in_t, vec, dw4_sum = _in_bwd_x(
        x[0], dy, norm_g, wp, dparts, [dwp.reshape(4, IN_WIDTH // 4, D_MODEL)], axes[0:1], (vec_leaves, loss_row, dw4))

    small_wmv = [(norm_g, m_norm_g, v_norm_g), (mem_norm_g, m_mem_norm_g, v_mem_norm_g),
                 (pool_scale, m_pool_scale, v_pool_scale), (b_f, m_b_f, v_b_f), (fox_q_g, m_fox_q_g, v_fox_q_g),
                 (fox_k_g, m_fox_k_g, v_fox_k_g), (mem_q_g, m_mem_q_g, v_mem_q_g), (mem_k_g, m_mem_k_g, v_mem_k_g)]
    pool_wmv = tuple(a.reshape(POOL_ROWS, HEAD_DIM) for a in (w_pool, m_w_pool, v_w_pool))
    loss, *small_out = _adamw_small(vec, dw4_sum, small_wmv, pool_wmv)
    big = [[g_w_in_t.T[None], g_w_kv[None], g_w_out[None]]]
    upd = [[a.T for a in _adamw("adamw_w_in", w_in_t, g_w_in_t, m_w_in_t, v_w_in_t)],
           _adamw("adamw_w_mem_kv", w_mem_kv[0], g_w_kv, m_w_mem_kv[0], v_w_mem_kv[0]),
           _adamw("adamw_w_out", w_out[0], g_w_out, m_w_out[0], v_w_out[0])]
    big += [[u[k][None] for u in upd] for k in range(3)]

    def leaves(k):
        sm = small_out[k]
        b_in, b_kv, b_out = big[k]
        return (sm[0], b_in, sm[3], sm[8].reshape(w_pool.shape), sm[2], sm[4], sm[5], sm[1], b_kv, sm[6], sm[7], b_out)

    return (loss.reshape(()), grad_x[None], *leaves(0), *leaves(1), *leaves(2), *leaves(3))


def _tiled_params(b_f, fox_q_g, fox_k_g, mem_q_g, mem_k_g):
    return (jnp.pad(b_f, ((0, 0), (0, LANES - FOX_HEADS))), _tile_heads(fox_q_g, FOX_HEADS),
            _tile_heads(fox_k_g, FOX_HEADS), _tile_heads(mem_q_g, 4), _tile_heads(mem_k_g, 4))


def _local_partials(xs, mems, tgt, fwd, w_kv_b, w_out_b, tiled, w4, pool_scale, mem_norm_g):
    hb, pa, qk, qa, ka, va, gb, pm, fb = fwd
    bf_pad, fq_g, fk_g, mq_g, mk_g = tiled

    mnb, kv, kmn, vmb = _mem_fwd(mems, mem_norm_g, w_kv_b, mk_g)
    ma, db = _pool_fwd(pa, w4, pool_scale)
    mm = _mem_attn_fwd(pm, kmn, vmb, mq_g)
    o, mb, r4 = _fox_fwd(qa, ka, va, gb)
    dy, dma, dmm, dw_out, loss_row, doa, dgb, rr = _out_loss(xs, tgt, ma, mb, mm, w_out_b, gb, o, r4)

    dpm, dkmn, dvm, dmq_g = _mem_attn_bwd(pm, dmm, kmn, vmb, mq_g)
    dw_kv, dmemnorm_g, dmk_g = _mem_bwd(dkmn, dvm, kv, mnb, mems, w_kv_b, mk_g, mem_norm_g)
    dpa, dw4, dpscale = _pool_bwd(pa, db, dma, w4, pool_scale)
    dka, dva, dqa = _fox_bwd(ka, va, qa, doa, rr)
    dqk, dvb, dfb, dfq_g, dfk_g, dbf = _fox_post(dqa, dka, dva, qk, fb, bf_pad, fq_g, fk_g)
    dparts = (dpa, dqk, dvb, dgb, dpm, dfb)
    leaves = (dmemnorm_g, dpscale, dbf, dfq_g, dfk_g, dmq_g, dmk_g)
    return dy, hb, dparts, dw_kv, dw_out, leaves, loss_row, dw4
```

```python
import functools

import jax
import jax.numpy as jnp
from jax import lax
from jax.experimental import pallas as pl
from jax.experimental.pallas import tpu as pltpu

F32 = jnp.float32
BF16 = jnp.bfloat16
MESH = pl.DeviceIdType.MESH

D_MODEL = 1024
HEAD_DIM = 64
POOL_WIDTH = 256
FOX_WIDTH = 512
FOX_HEADS = 8
MEM_WIDTH = 256
N_MEM = 256
IN_WIDTH = 3080
EPS = 1e-6
ATT_SCALE = 0.125

ADAM_LR = 0.001
ADAM_B1 = 0.9
ADAM_B2 = 0.999
ADAM_EPS = 1e-08
ADAM_WD = 0.01
ADAM_STEP = 10

LANES = 128
PA_LO, QB_LO, KB_LO, VB_LO, GB_LO, PM_LO, FB_LO, PROJ_PAD = 0, 512, 1024, 1536, 2048, 2560, 3072, 3200
F_ORIG_LO = 2048

TILE = 512
VMEM_LIMIT = 56 * 1024 * 1024

VEC_LEAVES = (("norm_g", 0, 1024), ("mem_norm_g", 1, 1024), ("pool_scale", 2, 256), ("b_f", 3, 8),
              ("fox_q_g", 4, 64), ("fox_k_g", 5, 64), ("mem_q_g", 6, 64), ("mem_k_g", 7, 64))
VEC_LOSS_ROW = 8
VEC_ROWS = 16
POOL_ROWS = 256


def _params(n_grid=1, vmem=VMEM_LIMIT):
    return pltpu.CompilerParams(dimension_semantics=("arbitrary",) * n_grid, vmem_limit_bytes=vmem)


def _rows(t, w):
    return pl.BlockSpec((t, w), lambda i: (i, 0))


def _rows_rev(t, w, n):
    return pl.BlockSpec((t, w), lambda i: (n - 1 - i, 0))


def _full(shape):
    return pl.BlockSpec(shape, lambda i: (0,) * len(shape))


def _sig(x):
    return 1.0 / (1.0 + jnp.exp(-x))


def _lane_lo(shape):
    return lax.broadcasted_iota(jnp.int32, shape, 1) < HEAD_DIM


def _pair_sum(v, lo):
    s0 = jnp.sum(jnp.where(lo, v, 0.0), axis=-1, keepdims=True)
    s1 = jnp.sum(jnp.where(lo, 0.0, v), axis=-1, keepdims=True)
    return jnp.where(lo, s0, s1)


def _head_rms(blk, lo):
    return lax.rsqrt(_pair_sum(blk * blk, lo) * (1.0 / HEAD_DIM) + EPS)


def _head_norm_bwd(dyn, xhat, rr, g, lo):
    a = dyn * g
    return rr * (a - xhat * (_pair_sum(xhat * a, lo) * (1.0 / HEAD_DIM)))


def _fold_heads(acc):
    tot = acc[:, 0:LANES]
    for p in range(1, acc.shape[1] // LANES):
        tot = tot + acc[:, p * LANES:(p + 1) * LANES]
    return tot + pltpu.roll(tot, HEAD_DIM, axis=1)


def _lane_pick(v, lane, idx):
    return jnp.sum(jnp.where(lane == idx, v, 0.0), axis=-1, keepdims=True)


NT = (((1,), (1,)), ((), ()))
TN = (((0,), (0,)), ((), ()))


def _dot(a, b, dims=None):
    if dims is None:
        return jnp.dot(a, b, preferred_element_type=F32)
    return lax.dot_general(a, b, dims, preferred_element_type=F32)


def _my_place():
    return lax.axis_index("x"), lax.axis_index("y"), lax.axis_index("c")


def _half_dims(shape, axis):
    return (shape[0] // 2, shape[1]) if axis == 0 else (shape[0], shape[1] // 2)


def _half_of(ref, axis, core, lead=False):
    rows, cols = ref.shape[-2:]
    if axis == 0:
        idx = (pl.ds(pl.multiple_of(core * (rows // 2), 16), rows // 2), slice(None))
    else:
        idx = (slice(None), pl.ds(pl.multiple_of(core * (cols // 2), LANES), cols // 2))
    return ref.at[(slice(None),) + idx] if lead else ref.at[idx]


class _HalfGather:
    def __init__(self, ins, outs, axes, f32_bufs, bf_bufs, send_sems, recv_sems, local_sems):
        self.ins, self.outs, self.axes = ins, outs, axes
        self.f32_bufs, self.bf_bufs = f32_bufs, bf_bufs
        self.send_sems, self.recv_sems, self.local_sems = send_sems, recv_sems, local_sems
        self.n = len(ins)
        x, y, self.c = _my_place()
        self.me, self.sibling = (x, y, self.c), (x, y, 1 - self.c)
        self.chips = [(1 - x, y), (x, 1 - y), (1 - x, 1 - y)]

    @staticmethod
    def scratch(shards, axes):
        dims = [_half_dims(a.shape, axis) for a, axis in zip(shards, axes)]
        n = len(shards)
        return [pltpu.VMEM(d, F32) for d in dims] + [pltpu.VMEM(d, BF16) for d in dims] + [
            pltpu.SemaphoreType.DMA((7 * n,)), pltpu.SemaphoreType.DMA((7 * n,)), pltpu.SemaphoreType.DMA((2 * n,))]

    @staticmethod
    def out_shapes(shards, axes):
        return tuple(jax.ShapeDtypeStruct((8,) + _half_dims(a.shape, axis), BF16) for a, axis in zip(shards, axes))

    def _blk(self, a, px, py, pc):
        return self.outs[a].at[4 * px + 2 * py + pc]

    def _copy(self, a, k, block, to, src=None):
        return pltpu.make_async_remote_copy(
            src_ref=self._blk(a, *block) if src is None else src, dst_ref=self._blk(a, *block),
            send_sem=self.send_sems.at[7 * a + k], recv_sem=self.recv_sems.at[7 * a + k], device_id=to,
            device_id_type=MESH)

    def _keep(self, a):
        return pltpu.make_async_copy(self.bf_bufs[a], self._blk(a, *self.me), self.local_sems.at[self.n + a])

    def _first(self, a):
        mine = [self._copy(a, 0, self.me, self.sibling, src=self.bf_bufs[a])]
        return mine + [self._copy(a, 1 + j, self.me, (*chip, self.c), src=self.bf_bufs[a])
                       for j, chip in enumerate(self.chips)]

    def send_mine(self):
        loads = [pltpu.make_async_copy(_half_of(self.ins[a], self.axes[a], self.c), self.f32_bufs[a],
                                       self.local_sems.at[a]) for a in range(self.n)]
        for cp in loads:
            cp.start()
        for a in range(self.n):
            loads[a].wait()
            self.bf_bufs[a][...] = self.f32_bufs[a][...].astype(BF16)
            self._keep(a).start()
            for cp in self._first(a):
                cp.start()

    def pass_on(self):
        for a in range(self.n):
            for j, chip in enumerate(self.chips):
                self._copy(a, 1 + j, (*chip, self.c), self.me).wait_recv()
                self._copy(a, 4 + j, (*chip, self.c), self.sibling).start()

    def finish(self):
        for a in range(self.n):
            self._copy(a, 0, self.sibling, self.me).wait_recv()
            for j, chip in enumerate(self.chips):
                self._copy(a, 4 + j, (*chip, 1 - self.c), self.me).wait_recv()
        for a in range(self.n):
            for cp in self._first(a):
                cp.wait_send()
            for j, chip in enumerate(self.chips):
                self._copy(a, 4 + j, (*chip, self.c), self.sibling).wait_send()
            self._keep(a).wait()


def _all_gather_weights(shards, axes):
    n = len(shards)

    def body(*refs):
        gather = _HalfGather(refs[0:n], refs[n:2 * n], axes, refs[2 * n:3 * n], refs[3 * n:4 * n], *refs[4 * n:])
        gather.send_mine()
        gather.pass_on()
        gather.finish()

    any_spec = pl.BlockSpec(memory_space=pl.ANY)
    return pl.pallas_call(
        body, name="weights_all_gather", out_shape=_HalfGather.out_shapes(shards, axes),
        in_specs=[any_spec] * n, out_specs=(any_spec,) * n, scratch_shapes=_HalfGather.scratch(shards, axes),
        compiler_params=pltpu.CompilerParams(vmem_limit_bytes=VMEM_LIMIT),
    )(*shards)


class _ShardReduce:
    SEMS = 8
    LOCAL = 5

    def __init__(self, g_refs, out_refs, axes, bufs, send_sems, recv_sems, local_sems):
        self.g_refs, self.out_refs, self.axes = g_refs, out_refs, axes
        self.recv_a, self.own_a, self.send_b, self.recv_b, self.fin = bufs
        self.send_sems, self.recv_sems, self.local_sems = send_sems, recv_sems, local_sems
        self.n = len(g_refs)
        x, y, self.c = _my_place()
        self.chip = 2 * x + y
        self.sibling = (x, y, 1 - self.c)

    @staticmethod
    def scratch(gparts, axes):
        dims = [_half_dims(g.shape[1:], axis) for g, axis in zip(gparts, axes)]
        shapes = []
        for dtype, lead in ((F32, (4,)), (F32, (4,)), (BF16, (4,)), (BF16, (4,)), (F32, ())):
            shapes += [pltpu.VMEM(lead + d, dtype) for d in dims]
        return shapes

    def _to_sibling(self, a, j):
        return pltpu.make_async_remote_copy(
            src_ref=_half_of(self.g_refs[a].at[j], self.axes[a], 1 - self.c), dst_ref=self.recv_a[a].at[j],
            send_sem=self.send_sems.at[self.SEMS * a + j], recv_sem=self.recv_sems.at[self.SEMS * a + j], device_id=self.sibling,
            device_id_type=MESH)

    def _own(self, a, j):
        return pltpu.make_async_copy(_half_of(self.g_refs[a].at[j], self.axes[a], self.c), self.own_a[a].at[j],
                                     self.local_sems.at[self.LOCAL * a + j])

    def _to_chip(self, a, k):
        dest = (self.chip + k) % 4
        return pltpu.make_async_remote_copy(
            src_ref=self.send_b[a].at[dest], dst_ref=self.recv_b[a].at[self.chip],
            send_sem=self.send_sems.at[self.SEMS * a + 3 + k], recv_sem=self.recv_sems.at[self.SEMS * a + 3 + k],
            device_id=(dest // 2, dest % 2, self.c), device_id_type=MESH)

    def _give(self, a):
        return pltpu.make_async_remote_copy(
            src_ref=self.fin[a], dst_ref=_half_of(self.out_refs[a], self.axes[a], self.c),
            send_sem=self.send_sems.at[self.SEMS * a + 7], recv_sem=self.recv_sems.at[self.SEMS * a + 7], device_id=self.sibling,
            device_id_type=MESH)

    def _mine(self, a):
        return pltpu.make_async_copy(self.fin[a], _half_of(self.out_refs[a], self.axes[a], self.c),
                                     self.local_sems.at[self.LOCAL * a])

    def exchange_with_sibling(self):
        for k in (1, 2, 3, 0):
            j = (self.chip + k) % 4
            for a in range(self.n):
                self._to_sibling(a, j).start()
                self._own(a, j).start()

    def _chip_partial(self, a, j):
        self._own(a, j).wait()
        self._to_sibling(a, j).wait_recv()
        self.send_b[a][j] = (self.own_a[a][j] + self.recv_a[a][j]).astype(BF16)

    def send_to_chip(self, k):
        for a in range(self.n):
            self._chip_partial(a, (self.chip + k) % 4)
            self._to_chip(a, k).start()

    def keep_mine(self):
        for a in range(self.n):
            self._chip_partial(a, self.chip)
            keep = pltpu.make_async_copy(self.send_b[a].at[self.chip], self.recv_b[a].at[self.chip],
                                         self.local_sems.at[self.LOCAL * a + 4])
            keep.start()
            keep.wait()

    def sum_and_share(self):
        for a in range(self.n):
            for k in range(1, 4):
                self._to_chip(a, k).wait_recv()
            tot = self.recv_b[a][0].astype(F32) + self.recv_b[a][1].astype(F32)
            tot = tot + self.recv_b[a][2].astype(F32)
            self.fin[a][...] = tot + self.recv_b[a][3].astype(F32)
            self._give(a).start()
            self._mine(a).start()

    def finish(self):
        for a in range(self.n):
            self._give(a).wait_recv()
            self._mine(a).wait()
            self._give(a).wait_send()
            for j in range(4):
                self._to_sibling(a, j).wait_send()
            for k in range(1, 4):
                self._to_chip(a, k).wait_send()


def _mem_fwd(mem, mem_norm_g, w_kv, mk_g):
    n = mem.shape[0]

    def body(mem_ref, g_ref, w_ref, kg_ref, mn_ref, kv_ref, kn_ref, vm_ref):
        xm = mem_ref[...]
        rr = lax.rsqrt(jnp.mean(xm * xm, axis=-1, keepdims=True) + EPS)
        mnb = ((xm * rr) * g_ref[...]).astype(BF16)
        mn_ref[...] = mnb
        kv = _dot(mnb, w_ref[...])
        kv_ref[...] = kv
        lo = _lane_lo((n, LANES))
        for p in range(MEM_WIDTH // LANES):
            sl = slice(p * LANES, (p + 1) * LANES)
            kb = kv[:, sl]
            kn_ref[:, sl] = ((kb * _head_rms(kb, lo)) * kg_ref[:, sl]).astype(BF16)
        vm_ref[...] = kv[:, MEM_WIDTH:].astype(BF16)

    return pl.pallas_call(
        body, name="mem_fwd",
        out_shape=(jax.ShapeDtypeStruct((n, D_MODEL), BF16), jax.ShapeDtypeStruct((n, 2 * MEM_WIDTH), F32),
                   jax.ShapeDtypeStruct((n, MEM_WIDTH), BF16), jax.ShapeDtypeStruct((n, MEM_WIDTH), BF16)),
        compiler_params=pltpu.CompilerParams(vmem_limit_bytes=VMEM_LIMIT),
    )(mem, mem_norm_g, w_kv, mk_g)


AUG_LO = 64
KEY_SUM_LANE = 72
QUERY_SUM_LANE = 80
HEAD_BLOCKS = FOX_HEADS * LANES


def _ones3(lane):
    return jnp.where((lane >= AUG_LO) & (lane < AUG_LO + 3), 1.0, 0.0)


def _spread3(cols):
    hi = cols.astype(BF16)
    rest = cols - hi.astype(F32)
    mid = rest.astype(BF16)
    low = (rest - mid.astype(F32)).astype(BF16)
    r = lax.broadcasted_iota(jnp.int32, (LANES, HEAD_BLOCKS), 0)
    c = lax.broadcasted_iota(jnp.int32, (LANES, HEAD_BLOCKS), 1)
    out = None
    for k, part in enumerate((hi, mid, low)):
        term = _dot(part, jnp.where(c == r * LANES + (AUG_LO + k), 1.0, 0.0).astype(BF16))
        out = term if out is None else out + term
    return out


def _head_block(pair_blk, hh, lo, extras):
    src = pair_blk if hh == 0 else pltpu.roll(pair_blk, HEAD_DIM, axis=1)
    return jnp.where(lo, src, extras).astype(BF16)


def _pair_block(blk0, blk1, lo):
    return jnp.where(lo, blk0, pltpu.roll(blk1, HEAD_DIM, axis=1))


def _fwd_in(x, norm_g, wp, bf_pad, fq_g, fk_g):
    s = x.shape[0]
    t = TILE
    n = s // t

    def body(x_ref, ng_ref, wp_ref, bf_ref, qg_ref, kg_ref,
             h_ref, pa_ref, qk_ref, qa_ref, ka_ref, va_ref, gb_ref, pm_ref, fb_ref, carry_ref, fcol_ref):
        @pl.when(pl.program_id(0) == 0)
        def _():
            carry_ref[...] = jnp.zeros_like(carry_ref)

        xv = x_ref[...]
        rr = lax.rsqrt(jnp.mean(xv * xv, axis=-1, keepdims=True) + EPS)
        hb = ((xv * rr) * ng_ref[...]).astype(BF16)
        h_ref[...] = hb

        def proj(lo, hi):
            return _dot(hb, wp_ref[lo:hi, :], NT)

        pa_ref[...] = proj(PA_LO, QB_LO)
        gb_ref[...] = proj(GB_LO, PM_LO)
        pm_ref[...] = proj(PM_LO, FB_LO)
        fb = proj(FB_LO, PROJ_PAD)
        fb_ref[...] = fb

        lane = lax.broadcasted_iota(jnp.int32, (t, LANES), 1)
        row = lax.broadcasted_iota(jnp.int32, (t, LANES), 0)
        lo = lane < HEAD_DIM
        z = fb + bf_ref[...]
        lf = -(jnp.maximum(-z, 0.0) + jnp.log1p(jnp.exp(-jnp.abs(z))))
        lf = jnp.where(lane < FOX_HEADS, lf, 0.0)
        sh = 1
        while sh < t:
            lf = lf + jnp.where(row >= sh, pltpu.roll(lf, sh, axis=0), 0.0)
            sh *= 2
        fcum = lf + carry_ref[...]
        fcol_ref[...] = fcum
        carry_ref[...] = fcol_ref[t - 1:t, :]

        ones3 = _ones3(lane)
        minus_f = _spread3(-fcum)
        for seg, g_ref, out_ref, scale in ((QB_LO, qg_ref, qa_ref, ATT_SCALE), (KB_LO, kg_ref, ka_ref, 1.0)):
            raw = proj(seg, seg + FOX_WIDTH)
            qk_ref[:, seg - QB_LO:seg - QB_LO + FOX_WIDTH] = raw
            for p in range(FOX_WIDTH // LANES):
                sl = slice(p * LANES, (p + 1) * LANES)
                blk = raw[:, sl]
                normed = ((blk * _head_rms(blk, lo)) * g_ref[:, sl]) * scale
                for hh in range(2):
                    h = 2 * p + hh
                    if seg == QB_LO:
                        extras = jnp.where(lane == QUERY_SUM_LANE + h, 1.0, ones3)
                    else:
                        extras = jnp.where(lane == KEY_SUM_LANE + h, 1.0, minus_f[:, h * LANES:(h + 1) * LANES])
                    out_ref[:, h * LANES:(h + 1) * LANES] = _head_block(normed, hh, lo, extras)
        vraw = proj(VB_LO, GB_LO)
        for h in range(FOX_HEADS):
            va_ref[:, h * LANES:(h + 1) * LANES] = _head_block(vraw[:, (h // 2) * LANES:(h // 2 + 1) * LANES], h % 2, lo, ones3)

    outs = (
        jax.ShapeDtypeStruct((s, D_MODEL), BF16),
        jax.ShapeDtypeStruct((s, 512), F32),
        jax.ShapeDtypeStruct((s, 2 * FOX_WIDTH), F32),
        jax.ShapeDtypeStruct((s, HEAD_BLOCKS), BF16),
        jax.ShapeDtypeStruct((s, HEAD_BLOCKS), BF16),
        jax.ShapeDtypeStruct((s, HEAD_BLOCKS), BF16),
        jax.ShapeDtypeStruct((s, FOX_WIDTH), F32),
        jax.ShapeDtypeStruct((s, 512), F32),
        jax.ShapeDtypeStruct((s, LANES), F32),
    )
    return pl.pallas_call(
        body, name="fwd_in", grid=(n,), out_shape=outs,
        in_specs=[_rows(t, D_MODEL), _full((1, D_MODEL)), _full((PROJ_PAD, D_MODEL)), _full((1, LANES)),
                  _full((1, FOX_WIDTH)), _full((1, FOX_WIDTH))],
        out_specs=(_rows(t, D_MODEL), _rows(t, 512), _rows(t, 2 * FOX_WIDTH), _rows(t, HEAD_BLOCKS),
                   _rows(t, HEAD_BLOCKS), _rows(t, HEAD_BLOCKS), _rows(t, FOX_WIDTH), _rows(t, 512),
                   _rows(t, LANES)),
        scratch_shapes=[pltpu.VMEM((1, LANES), F32), pltpu.VMEM((t, LANES), F32)],
        compiler_params=_params(),
    )(x, norm_g, wp, bf_pad, fq_g, fk_g)


POOL_HALO = 16


def _pool_window(lane):
    return jnp.where(lane < 64, 2.0, jnp.where(lane < 128, 4.0, jnp.where(lane < 192, 8.0, 16.0)))


def _pool_pick(lane, s2, s4, s8, s16):
    return jnp.where(lane < 64, s2, jnp.where(lane < 128, s4, jnp.where(lane < 192, s8, s16)))


def _group_onehot(shape, row_is_group_lane):
    r = lax.broadcasted_iota(jnp.int32, shape, 0)
    c = lax.broadcasted_iota(jnp.int32, shape, 1)
    hit = (r % HEAD_DIM == c) if row_is_group_lane else (c % HEAD_DIM == r)
    return jnp.where(hit, 1.0, 0.0).astype(F32)


def _same_group(shape):
    r = lax.broadcasted_iota(jnp.int32, shape, 0)
    c = lax.broadcasted_iota(jnp.int32, shape, 1)
    return (r // HEAD_DIM) == (c // HEAD_DIM)


def _pool_block_diag(w4):
    spread = jnp.dot(w4, _group_onehot((HEAD_DIM, POOL_WIDTH), False), preferred_element_type=F32,
                     precision=lax.Precision.HIGHEST)
    return jnp.where(_same_group((POOL_WIDTH, POOL_WIDTH)), spread, 0.0).astype(BF16)


def _pool_fwd(pa, w4, pscale):
    s = pa.shape[0]
    t = TILE
    n = s // t
    ext = t + POOL_HALO

    def body(pa_ref, w4_ref, sc_ref, ma_ref, d_ref, ext_ref, w_ref):
        i = pl.program_id(0)

        @pl.when(i == 0)
        def _():
            ext_ref[0:POOL_HALO, :] = jnp.zeros((POOL_HALO, POOL_WIDTH), F32)
            w_ref[...] = _pool_block_diag(w4_ref[...])

        u = pa_ref[:, 0:POOL_WIDTH]
        ext_ref[POOL_HALO:ext, :] = u
        e = ext_ref[...]
        s2 = e + pltpu.roll(e, 1, axis=0)
        s4 = s2 + pltpu.roll(s2, 2, axis=0)
        s8 = s4 + pltpu.roll(s4, 4, axis=0)
        s16 = s8 + pltpu.roll(s8, 8, axis=0)
        lane_e = lax.broadcasted_iota(jnp.int32, (ext, POOL_WIDTH), 1)
        win = _pool_pick(lane_e, s2, s4, s8, s16)[POOL_HALO:ext, :]
        lane = lax.broadcasted_iota(jnp.int32, (t, POOL_WIDTH), 1)
        pos = (lax.broadcasted_iota(jnp.int32, (t, POOL_WIDTH), 0) + (i * t + 1)).astype(F32)
        d = win / jnp.minimum(pos, _pool_window(lane)) - u
        db = d.astype(BF16)
        d_ref[...] = db
        ya = _dot(db, w_ref[...]) * sc_ref[...]
        ga = pa_ref[:, POOL_WIDTH:2 * POOL_WIDTH]
        ma_ref[...] = (ya * (ga * _sig(ga))).astype(BF16)
        ext_ref[0:POOL_HALO, :] = ext_ref[t:ext, :]

    return pl.pallas_call(
        body, name="pool_fwd", grid=(n,),
        out_shape=(jax.ShapeDtypeStruct((s, POOL_WIDTH), BF16), jax.ShapeDtypeStruct((s, POOL_WIDTH), BF16)),
        in_specs=[_rows(t, 512), _full((POOL_ROWS, HEAD_DIM)), _full((1, POOL_WIDTH))],
        out_specs=(_rows(t, POOL_WIDTH), _rows(t, POOL_WIDTH)),
        scratch_shapes=[pltpu.VMEM((ext, POOL_WIDTH), F32), pltpu.VMEM((POOL_WIDTH, POOL_WIDTH), BF16)],
        compiler_params=_params(),
    )(pa, w4, pscale)


def _mem_softmax(qm, kp):
    sc = _dot(qm, kp, NT)
    e = jnp.exp(sc - jnp.max(sc, axis=-1, keepdims=True))
    return e * (1.0 / jnp.sum(e, axis=-1, keepdims=True))


def _mem_attn_fwd(pm, kmn, vmb, mq_g):
    s = pm.shape[0]
    t = TILE
    n = s // t

    def body(pm_ref, k_ref, v_ref, g_ref, mm_ref):
        lo = _lane_lo((t, LANES))
        for p in range(MEM_WIDTH // LANES):
            sl = slice(p * LANES, (p + 1) * LANES)
            qb = pm_ref[:, sl]
            qs = (((qb * _head_rms(qb, lo)) * g_ref[:, sl]) * ATT_SCALE).astype(BF16)
            kp = k_ref[:, sl]
            vp = v_ref[:, sl]
            outs = []
            for hh in range(2):
                msk = lo if hh == 0 else jnp.logical_not(lo)
                prob = _mem_softmax(jnp.where(msk, qs, jnp.zeros_like(qs)), kp)
                outs.append(_dot(prob.astype(BF16), vp))
            o = jnp.where(lo, outs[0], outs[1])
            gm = pm_ref[:, MEM_WIDTH + p * LANES:MEM_WIDTH + (p + 1) * LANES]
            mm_ref[:, sl] = (o * (gm * _sig(gm))).astype(BF16)

    return pl.pallas_call(
        body, name="mem_attn_fwd", grid=(n,),
        out_shape=jax.ShapeDtypeStruct((s, MEM_WIDTH), BF16),
        in_specs=[_rows(t, 512), _full((N_MEM, MEM_WIDTH)), _full((N_MEM, MEM_WIDTH)), _full((1, MEM_WIDTH))],
        out_specs=_rows(t, MEM_WIDTH),
        compiler_params=_params(),
    )(pm, kmn, vmb, mq_g)


FOX_FWD_HEADS = 4


def _fox_fwd(qa, ka, va, gb):
    s = qa.shape[0]
    t = TILE
    n = s // t
    heads = FOX_FWD_HEADS
    pairs = heads // 2
    group_w = heads * LANES

    def body(qa_ref, ka_ref, va_ref, gb_ref, o_ref, mb_ref, r_ref):
        i = pl.program_id(1)
        lane = lax.broadcasted_iota(jnp.int32, (t, LANES), 1)
        lo = lane < HEAD_DIM
        causal = lax.broadcasted_iota(jnp.int32, (t, t), 1) <= lax.broadcasted_iota(jnp.int32, (t, t), 0)
        qas = [qa_ref[:, hh * LANES:(hh + 1) * LANES] for hh in range(heads)]

        def step(j, carry, masked):
            rows = pl.ds(pl.multiple_of(j * t, t), t)
            new = []
            for hh in range(heads):
                cols = slice(hh * LANES, (hh + 1) * LANES)
                m, acc = carry[hh]
                sc = _dot(qas[hh], ka_ref[rows, cols], NT)
                if masked:
                    sc = jnp.where(causal, sc, -1e30)
                m_new = jnp.maximum(m, jnp.max(sc, axis=-1, keepdims=True))
                acc = jnp.exp(m - m_new) * acc + _dot(jnp.exp(sc - m_new).astype(BF16), va_ref[rows, cols])
                new.append((m_new, acc))
            return tuple(new)

        init = (jnp.full((t, 1), -1e30, F32), jnp.zeros((t, LANES), F32))
        carry = lax.fori_loop(0, i, functools.partial(step, masked=False), (init,) * heads)
        res = step(i, carry, masked=True)
        for p in range(pairs):
            outs = []
            rcol = jnp.zeros((t, LANES), F32)
            for hh in range(2):
                m, acc = res[2 * p + hh]
                l = _lane_pick(acc, lane, AUG_LO)
                outs.append(acc * (1.0 / l))
                rcol = jnp.where(lane == hh, m + jnp.log(l), rcol)
            o = _pair_block(outs[0], outs[1], lo)
            sl = slice(p * LANES, (p + 1) * LANES)
            o_ref[:, sl] = o
            g = gb_ref[:, sl]
            mb_ref[:, sl] = (o * (g * _sig(g))).astype(BF16)
            r_ref[p] = rcol

    tile_spec = pl.BlockSpec((t, pairs * LANES), lambda p, i: (i, p))
    full_spec = pl.BlockSpec((s, group_w), lambda p, i: (0, p))
    return pl.pallas_call(
        body, name="fox_fwd", grid=(FOX_HEADS // heads, n),
        out_shape=(jax.ShapeDtypeStruct((s, FOX_WIDTH), F32), jax.ShapeDtypeStruct((s, FOX_WIDTH), BF16),
                   jax.ShapeDtypeStruct((FOX_HEADS // 2, s, LANES), F32)),
        in_specs=[pl.BlockSpec((t, group_w), lambda p, i: (i, p)), full_spec, full_spec, tile_spec],
        out_specs=(tile_spec, tile_spec, pl.BlockSpec((pairs, t, LANES), lambda p, i: (p, i, 0))),
        compiler_params=_params(2),
    )(qa, ka, va, gb)


def _out_loss(x, tgt, ma, mb, mm, wout, gb, o, r4):
    s = x.shape[0]
    t = TILE
    n = s // t
    pairs = FOX_HEADS // 2

    def body(x_ref, t_ref, ma_ref, mb_ref, mm_ref, w_ref, gb_ref, o_ref, r_ref,
             dy_ref, dma_ref, dmm_ref, dw_ref, loss_ref, doa_ref, dgb_ref, rr_ref, mix_ref):
        @pl.when(pl.program_id(0) == 0)
        def _():
            dw_ref[...] = jnp.zeros_like(dw_ref)
            loss_ref[...] = jnp.zeros_like(loss_ref)

        mix_ref[:, 0:256] = ma_ref[...]
        mix_ref[:, 256:768] = mb_ref[...]
        mix_ref[:, 768:1024] = mm_ref[...]
        mix = mix_ref[...]
        err = (x_ref[...] + _dot(mix, w_ref[...])) - t_ref[...]
        row_mean = jnp.sum(err * err, axis=-1, keepdims=True) * (1.0 / D_MODEL)
        loss_ref[...] += 0.5 * jnp.sum(row_mean, axis=0, keepdims=True)
        dy = err * (1.0 / D_MODEL)
        dy_ref[...] = dy
        dyb = dy.astype(BF16)
        dmix = _dot(dyb, w_ref[...], NT)
        dma_ref[...] = dmix[:, 0:256]
        dmm_ref[...] = dmix[:, 768:1024]
        dw_ref[...] += _dot(mix, dyb, TN)

        lane = lax.broadcasted_iota(jnp.int32, (t, LANES), 1)
        lo = lane < HEAD_DIM
        d_os = []
        delta = jnp.zeros((t, LANES), F32)
        for p in range(pairs):
            sl = slice(p * LANES, (p + 1) * LANES)
            g = gb_ref[:, sl]
            sg = _sig(g)
            dm = dmix[:, 256 + p * LANES:256 + (p + 1) * LANES]
            ov = o_ref[:, sl]
            d_o = dm * (g * sg)
            d_os.append(d_o)
            dgb_ref[:, sl] = (dm * ov * (sg * (1.0 + g * (1.0 - sg)))).astype(BF16)
            prod = d_o * ov
            delta = jnp.where(lane == 2 * p, jnp.sum(jnp.where(lo, prod, 0.0), axis=-1, keepdims=True), delta)
            delta = jnp.where(lane == 2 * p + 1, jnp.sum(jnp.where(lo, 0.0, prod), axis=-1, keepdims=True), delta)
            rr_ref[p, 0] = r_ref[p].T[0:8, :]
        minus_delta = _spread3(-delta)
        for h in range(FOX_HEADS):
            blk = slice(h * LANES, (h + 1) * LANES)
            doa_ref[:, blk] = _head_block(d_os[h // 2], h % 2, lo, minus_delta[:, blk])

    return pl.pallas_call(
        body, name="out_loss", grid=(n,),
        out_shape=(jax.ShapeDtypeStruct((s, D_MODEL), F32), jax.ShapeDtypeStruct((s, 256), F32),
                   jax.ShapeDtypeStruct((s, 256), F32), jax.ShapeDtypeStruct((D_MODEL, D_MODEL), F32),
                   jax.ShapeDtypeStruct((1, LANES), F32), jax.ShapeDtypeStruct((s, HEAD_BLOCKS), BF16),
                   jax.ShapeDtypeStruct((s, FOX_WIDTH), BF16), jax.ShapeDtypeStruct((pairs, n, 8, t), F32)),
        in_specs=[_rows(t, D_MODEL), _rows(t, D_MODEL), _rows(t, 256), _rows(t, 512), _rows(t, 256),
                  _full((D_MODEL, D_MODEL)), _rows(t, FOX_WIDTH), _rows(t, FOX_WIDTH),
                  pl.BlockSpec((pairs, t, LANES), lambda i: (0, i, 0))],
        out_specs=(_rows(t, D_MODEL), _rows(t, 256), _rows(t, 256), _full((D_MODEL, D_MODEL)), _full((1, LANES)),
                   _rows(t, HEAD_BLOCKS), _rows(t, FOX_WIDTH), pl.BlockSpec((pairs, 1, 8, t), lambda i: (0, i, 0, 0))),
        scratch_shapes=[pltpu.VMEM((t, D_MODEL), BF16)],
        compiler_params=_params(),
    )(x, tgt, ma, mb, mm, wout, gb, o, r4)


def _mem_attn_bwd(pm, dmm, kmn, vmb, mq_g):
    s = pm.shape[0]
    t = TILE
    n = s // t

    def body(pm_ref, dmm_ref, k_ref, v_ref, g_ref, dpm_ref, dk_ref, dv_ref, dg_ref, gacc_ref):
        @pl.when(pl.program_id(0) == 0)
        def _():
            dk_ref[...] = jnp.zeros_like(dk_ref)
            dv_ref[...] = jnp.zeros_like(dv_ref)
            gacc_ref[...] = jnp.zeros_like(gacc_ref)

        lo = _lane_lo((t, LANES))
        for p in range(MEM_WIDTH // LANES):
            sl = slice(p * LANES, (p + 1) * LANES)
            qb = pm_ref[:, sl]
            rr = _head_rms(qb, lo)
            qhat = qb * rr
            g = g_ref[:, sl]
            qs = ((qhat * g) * ATT_SCALE).astype(BF16)
            gm = pm_ref[:, MEM_WIDTH + p * LANES:MEM_WIDTH + (p + 1) * LANES]
            sg = _sig(gm)
            dmo = dmm_ref[:, sl]
            d_o = dmo * (gm * sg)
            kp = k_ref[:, sl]
            vp = v_ref[:, sl]
            outs, dqs = [], []
            for hh in range(2):
                msk = lo if hh == 0 else jnp.logical_not(lo)
                qm = jnp.where(msk, qs, jnp.zeros_like(qs))
                prob = _mem_softmax(qm, kp)
                pb = prob.astype(BF16)
                outs.append(_dot(pb, vp))
                dom = jnp.where(msk, d_o, 0.0).astype(BF16)
                dp = _dot(dom, vp, NT)
                ds = (prob * (dp - jnp.sum(prob * dp, axis=-1, keepdims=True))).astype(BF16)
                dqs.append(_dot(ds, kp))
                dk_ref[:, sl] += _dot(ds, qm, TN)
                dv_ref[:, sl] += _dot(pb, dom, TN)
            o = jnp.where(lo, outs[0], outs[1])
            dqn = jnp.where(lo, dqs[0], dqs[1]) * ATT_SCALE
            dpm_ref[:, sl] = _head_norm_bwd(dqn, qhat, rr, g, lo).astype(BF16)
            dpm_ref[:, MEM_WIDTH + p * LANES:MEM_WIDTH + (p + 1) * LANES] = (
                dmo * o * (sg * (1.0 + gm * (1.0 - sg)))).astype(BF16)
            gacc_ref[:, sl] += jnp.sum(dqn * qhat, axis=0, keepdims=True)

        @pl.when(pl.program_id(0) == n - 1)
        def _():
            dg_ref[...] = _fold_heads(gacc_ref[...])

    return pl.pallas_call(
        body, name="mem_attn_bwd", grid=(n,),
        out_shape=(jax.ShapeDtypeStruct((s, 512), BF16), jax.ShapeDtypeStruct((N_MEM, MEM_WIDTH), F32),
                   jax.ShapeDtypeStruct((N_MEM, MEM_WIDTH), F32), jax.ShapeDtypeStruct((1, LANES), F32)),
        in_specs=[_rows(t, 512), _rows(t, MEM_WIDTH), _full((N_MEM, MEM_WIDTH)), _full((N_MEM, MEM_WIDTH)),
                  _full((1, MEM_WIDTH))],
        out_specs=(_rows(t, 512), _full((N_MEM, MEM_WIDTH)), _full((N_MEM, MEM_WIDTH)), _full((1, LANES))),
        scratch_shapes=[pltpu.VMEM((1, MEM_WIDTH), F32)],
        compiler_params=_params(),
    )(pm, dmm, kmn, vmb, mq_g)


def _mem_bwd(dkn, dvm, kv, mnb, mem, w_kv, mk_g, mem_norm_g):
    n = mem.shape[0]

    def body(dkn_ref, dvm_ref, kv_ref, mn_ref, mem_ref, w_ref, kg_ref, g_ref, dw_ref, dg_ref, dkg_ref, dkv_ref):
        lo = _lane_lo((n, LANES))
        gacc = []
        for p in range(MEM_WIDTH // LANES):
            sl = slice(p * LANES, (p + 1) * LANES)
            kb = kv_ref[:, sl]
            rr = _head_rms(kb, lo)
            khat = kb * rr
            dk = dkn_ref[:, sl]
            dkv_ref[:, sl] = _head_norm_bwd(dk, khat, rr, kg_ref[:, sl], lo).astype(BF16)
            gacc.append(jnp.sum(dk * khat, axis=0, keepdims=True))
        dkg_ref[...] = _fold_heads(jnp.concatenate(gacc, axis=1))
        dkv_ref[:, MEM_WIDTH:] = dvm_ref[...].astype(BF16)
        dkv = dkv_ref[...]
        dw_ref[...] = _dot(mn_ref[...], dkv, TN)
        dmn = _dot(dkv, w_ref[...], NT)
        xm = mem_ref[...]
        rr = lax.rsqrt(jnp.mean(xm * xm, axis=-1, keepdims=True) + EPS)
        dg_ref[...] = jnp.sum(dmn * (xm * rr), axis=0, keepdims=True)

    return pl.pallas_call(
        body, name="mem_bwd",
        out_shape=(jax.ShapeDtypeStruct((D_MODEL, 2 * MEM_WIDTH), F32), jax.ShapeDtypeStruct((1, D_MODEL), F32),
                   jax.ShapeDtypeStruct((1, LANES), F32)),
        scratch_shapes=[pltpu.VMEM((n, 2 * MEM_WIDTH), BF16)],
        compiler_params=pltpu.CompilerParams(vmem_limit_bytes=VMEM_LIMIT),
    )(dkn, dvm, kv, mnb, mem, w_kv, mk_g, mem_norm_g)


def _pool_bwd(pa, db, dma, w4, pscale):
    s = pa.shape[0]
    t = TILE
    n = s // t
    ext = t + POOL_HALO

    def body(pa_ref, d_ref, dma_ref, w4_ref, sc_ref, dpa_ref, dw4_ref, dsc_ref, ext_ref, w_ref, dw_ref):
        i = pl.program_id(0)

        @pl.when(i == 0)
        def _():
            dw_ref[...] = jnp.zeros_like(dw_ref)
            dsc_ref[...] = jnp.zeros_like(dsc_ref)
            ext_ref[t:ext, :] = jnp.zeros((POOL_HALO, POOL_WIDTH), F32)
            w_ref[...] = _pool_block_diag(w4_ref[...])

        dbv = d_ref[...]
        z = _dot(dbv, w_ref[...])
        ga = pa_ref[:, POOL_WIDTH:2 * POOL_WIDTH]
        sg = _sig(ga)
        dma_v = dma_ref[...]
        dya = dma_v * (ga * sg)
        dpa_ref[:, POOL_WIDTH:2 * POOL_WIDTH] = (dma_v * (z * sc_ref[...]) * (sg * (1.0 + ga * (1.0 - sg)))).astype(BF16)
        dsc_ref[...] += jnp.sum(dya * z, axis=0, keepdims=True)
        dzb = (dya * sc_ref[...]).astype(BF16)
        dw_ref[...] += _dot(dbv, dzb, TN)
        dd = _dot(dzb, w_ref[...], NT)
        lane = lax.broadcasted_iota(jnp.int32, (t, POOL_WIDTH), 1)
        pos = (lax.broadcasted_iota(jnp.int32, (t, POOL_WIDTH), 0) + ((n - 1 - i) * t + 1)).astype(F32)
        ext_ref[0:t, :] = dd / jnp.minimum(pos, _pool_window(lane))
        e = ext_ref[...]
        s2 = e + pltpu.roll(e, ext - 1, axis=0)
        s4 = s2 + pltpu.roll(s2, ext - 2, axis=0)
        s8 = s4 + pltpu.roll(s4, ext - 4, axis=0)
        s16 = s8 + pltpu.roll(s8, ext - 8, axis=0)
        lane_e = lax.broadcasted_iota(jnp.int32, (ext, POOL_WIDTH), 1)
        win = _pool_pick(lane_e, s2, s4, s8, s16)[0:t, :]
        dpa_ref[:, 0:POOL_WIDTH] = (win - dd).astype(BF16)
        ext_ref[t:ext, :] = ext_ref[0:POOL_HALO, :]

        @pl.when(i == n - 1)
        def _():
            own = jnp.where(_same_group((POOL_WIDTH, POOL_WIDTH)), dw_ref[...], 0.0)
            dw4_ref[...] = jnp.dot(own, _group_onehot((POOL_WIDTH, HEAD_DIM), True), preferred_element_type=F32,
                                   precision=lax.Precision.HIGHEST)

    return pl.pallas_call(
        body, name="pool_bwd", grid=(n,),
        out_shape=(jax.ShapeDtypeStruct((s, 512), BF16), jax.ShapeDtypeStruct((POOL_ROWS, HEAD_DIM), F32),
                   jax.ShapeDtypeStruct((1, POOL_WIDTH), F32)),
        in_specs=[_rows_rev(t, 512, n), _rows_rev(t, POOL_WIDTH, n), _rows_rev(t, POOL_WIDTH, n),
                  _full((POOL_ROWS, HEAD_DIM)), _full((1, POOL_WIDTH))],
        out_specs=(_rows_rev(t, 512, n), _full((POOL_ROWS, HEAD_DIM)), _full((1, POOL_WIDTH))),
        scratch_shapes=[pltpu.VMEM((ext, POOL_WIDTH), F32), pltpu.VMEM((POOL_WIDTH, POOL_WIDTH), BF16),
                        pltpu.VMEM((POOL_WIDTH, POOL_WIDTH), F32)],
        compiler_params=_params(),
    )(pa, db, dma, w4, pscale)


FOX_BWD_HEADS = 4


def _fox_bwd(ka, va, qa, doa, rr):
    s = ka.shape[0]
    t = TILE
    n = s // t
    heads = FOX_BWD_HEADS
    group_w = heads * LANES

    def body(ka_ref, va_ref, qa_ref, doa_ref, rr_ref, dka_ref, dva_ref, dqa_ref):
        j = pl.program_id(1)

        @pl.when(j == 0)
        def _():
            dqa_ref[...] = jnp.zeros_like(dqa_ref)

        causal = lax.broadcasted_iota(jnp.int32, (t, t), 0) <= lax.broadcasted_iota(jnp.int32, (t, t), 1)
        kas = [ka_ref[:, hh * LANES:(hh + 1) * LANES] for hh in range(heads)]
        vas = [va_ref[:, hh * LANES:(hh + 1) * LANES] for hh in range(heads)]

        def step(i, carry, masked):
            rows = pl.ds(pl.multiple_of(i * t, t), t)
            new = []
            for hh in range(heads):
                cols = slice(hh * LANES, (hh + 1) * LANES)
                dk_a, dv_a = carry[hh]
                qb = qa_ref[rows, cols]
                d_o = doa_ref[rows, cols]
                arg = _dot(kas[hh], qb, NT) - rr_ref[hh // 2, i, hh % 2:hh % 2 + 1, :]
                if masked:
                    arg = jnp.where(causal, arg, -1e30)
                pt = jnp.exp(arg)
                dst = (pt * _dot(vas[hh], d_o, NT)).astype(BF16)
                dv_a = dv_a + _dot(pt.astype(BF16), d_o)
                dk_a = dk_a + _dot(dst, qb)
                dqa_ref[rows, cols] += _dot(dst, kas[hh], TN)
                new.append((dk_a, dv_a))
            return tuple(new)

        zero = jnp.zeros((t, LANES), F32)
        carry = step(j, ((zero, zero),) * heads, masked=True)
        res = lax.fori_loop(j + 1, n, functools.partial(step, masked=False), carry)
        for hh in range(heads):
            cols = slice(hh * LANES, (hh + 1) * LANES)
            dka_ref[:, cols] = res[hh][0]
            dva_ref[:, cols] = res[hh][1]

    tile_spec = pl.BlockSpec((t, group_w), lambda p, j: (j, p))
    full_spec = pl.BlockSpec((s, group_w), lambda p, j: (0, p))
    return pl.pallas_call(
        body, name="fox_bwd", grid=(FOX_HEADS // heads, n),
        out_shape=(jax.ShapeDtypeStruct((s, HEAD_BLOCKS), F32),) * 3,
        in_specs=[tile_spec, tile_spec, full_spec, full_spec,
                  pl.BlockSpec((heads // 2, n, 8, t), lambda p, j: (p, 0, 0, 0))],
        out_specs=(tile_spec, tile_spec, full_spec),
        compiler_params=_params(2),
    )(ka, va, qa, doa, rr)


def _fox_post_tile(i, n, t, dqa_ref, dka_ref, dva_ref, qk_ref, fb_ref, bf_ref, qg_ref, kg_ref,
                   dqk_ref, dv_ref, dfb_ref, dqg_ref, dkg_ref, dbf_ref, qacc_ref, kacc_ref, carry_ref):
    @pl.when(i == 0)
    def _():
        qacc_ref[...] = jnp.zeros_like(qacc_ref)
        kacc_ref[...] = jnp.zeros_like(kacc_ref)
        dbf_ref[...] = jnp.zeros_like(dbf_ref)
        carry_ref[...] = jnp.zeros_like(carry_ref)

    lane = lax.broadcasted_iota(jnp.int32, (t, LANES), 1)
    row = lax.broadcasted_iota(jnp.int32, (t, LANES), 0)
    lo = lane < HEAD_DIM

    def head_blocks(ref, p):
        return ref[:, 2 * p * LANES:(2 * p + 1) * LANES], ref[:, (2 * p + 1) * LANES:(2 * p + 2) * LANES]

    dq_sum = jnp.zeros((t, LANES), F32)
    dk_sum = jnp.zeros((t, LANES), F32)
    for p in range(FOX_WIDTH // LANES):
        sl = slice(p * LANES, (p + 1) * LANES)
        dq0, dq1 = head_blocks(dqa_ref, p)
        dk0, dk1 = head_blocks(dka_ref, p)
        dv0, dv1 = head_blocks(dva_ref, p)
        dv_ref[:, sl] = _pair_block(dv0, dv1, lo).astype(BF16)
        dq_sum = dq_sum + (dq0 + dq1)
        dk_sum = dk_sum + (dk0 + dk1)
        for off, pair, g_ref, acc_ref, scale in ((0, _pair_block(dq0, dq1, lo), qg_ref, qacc_ref, ATT_SCALE),
                                                 (FOX_WIDTH, _pair_block(dk0, dk1, lo), kg_ref, kacc_ref, 1.0)):
            raw = qk_ref[:, off + p * LANES:off + (p + 1) * LANES]
            rr = _head_rms(raw, lo)
            xhat = raw * rr
            dn = pair * scale
            dqk_ref[:, off + p * LANES:off + (p + 1) * LANES] = _head_norm_bwd(
                dn, xhat, rr, g_ref[:, sl], lo).astype(BF16)
            acc_ref[:, sl] += jnp.sum(dn * xhat, axis=0, keepdims=True)

    acc = (pltpu.roll(dq_sum, LANES - KEY_SUM_LANE, axis=1) - pltpu.roll(dk_sum, LANES - QUERY_SUM_LANE, axis=1))
    acc = jnp.where(lane < FOX_HEADS, acc, 0.0)
    sh = 1
    while sh < t:
        acc = acc + jnp.where(row < t - sh, pltpu.roll(acc, t - sh, axis=0), 0.0)
        sh *= 2
    dlogf = acc + carry_ref[...]
    dfb_ref[...] = dlogf
    carry_ref[...] = dfb_ref[0:1, :]
    z = fb_ref[...] + bf_ref[...]
    dz = jnp.where(lane < FOX_HEADS, dlogf * (1.0 / (1.0 + jnp.exp(z))), 0.0)
    dfb_ref[...] = dz
    dbf_ref[...] += jnp.sum(dz, axis=0, keepdims=True)

    @pl.when(i == n - 1)
    def _():
        dqg_ref[...] = _fold_heads(qacc_ref[...])
        dkg_ref[...] = _fold_heads(kacc_ref[...])


def _assemble_dproj(dp_ref, dpa_ref, dqk_ref, dv_ref, dgb_ref, dpm_ref, dfb_ref):
    dp_ref[:, PA_LO:QB_LO] = dpa_ref[...]
    dp_ref[:, QB_LO:VB_LO] = dqk_ref[...]
    dp_ref[:, VB_LO:GB_LO] = dv_ref[...]
    dp_ref[:, GB_LO:PM_LO] = dgb_ref[...]
    dp_ref[:, PM_LO:FB_LO] = dpm_ref[...]
    dp_ref[:, FB_LO:PROJ_PAD] = dfb_ref[...].astype(BF16)


def _dproj_specs(t):
    return [_rows(t, 512), _rows(t, 2 * FOX_WIDTH), _rows(t, FOX_WIDTH), _rows(t, FOX_WIDTH), _rows(t, 512),
            _rows(t, LANES)]


IN_BWD_X_TILE = 256


def _in_bwd_x(x, dy, norm_g, wp, dparts, gparts, axes, smalls):
    s = x.shape[0]
    t = IN_BWD_X_TILE
    n = s // t
    na = len(gparts)
    n_dp = len(dparts)
    vec_leaves, loss_row, dw4 = smalls if smalls is not None else ((), None, None)
    nv = len(vec_leaves)
    n_small = nv + 2 if smalls is not None else 0
    small_base = _ShardReduce.SEMS * na

    def body(*refs):
        x_ref, dy_ref, g_ref, wp_ref = refs[0:4]
        dp_parts = refs[4:4 + n_dp]
        o = 4 + n_dp
        g_refs = refs[o:o + na]
        small_in = refs[o + na:o + na + n_small]
        o += na + n_small
        gx_ref, dg_ref = refs[o:o + 2]
        out_refs = refs[o + 2:o + 2 + na]
        small_out = refs[o + 2 + na:o + 2 + na + (2 if smalls is not None else 0)]
        o += 2 + na + len(small_out)
        dp_ref = refs[o]
        bufs = tuple(refs[o + 1 + k * na:o + 1 + (k + 1) * na] for k in range(5))
        rest = refs[o + 1 + 5 * na:]

        i = pl.program_id(0)
        if na or smalls is not None:
            send_sems, recv_sems, local_sems = rest[-3:]
        red = _ShardReduce(g_refs, out_refs, axes, bufs, send_sems, recv_sems, local_sems) if na else None

        @pl.when(i == 0)
        def _():
            dg_ref[...] = jnp.zeros_like(dg_ref)
            if red is not None:
                red.exchange_with_sibling()

        if red is not None:
            for k in (1, 2, 3):
                pl.when(i == k)(functools.partial(red.send_to_chip, k))
            pl.when(i == 4)(red.keep_mine)

        _assemble_dproj(dp_ref, *dp_parts)
        dh = _dot(dp_ref[...], wp_ref[...])
        xv = x_ref[...]
        rr = lax.rsqrt(jnp.mean(xv * xv, axis=-1, keepdims=True) + EPS)
        xhat = xv * rr
        scaled = dh * g_ref[...]
        gx_ref[...] = dy_ref[...] + rr * (scaled - xhat * jnp.mean(xhat * scaled, axis=-1, keepdims=True))
        dg_ref[...] += jnp.sum(dh * xhat, axis=0, keepdims=True)

        def small_all_reduce():
            leaf_refs, (loss_ref, dw4_ref) = small_in[0:nv], small_in[nv:]
            vec_out, dw4_out = small_out
            vec_mine, vec_recv, dw4_recv = rest[0:3]
            cx, cy, c = _my_place()
            me_lin = 4 * cx + 2 * cy + c

            def copy(k, src, dst, base):
                peer = (me_lin + k) % 8
                return pltpu.make_async_remote_copy(
                    src_ref=src, dst_ref=dst.at[me_lin], send_sem=send_sems.at[base + k - 1],
                    recv_sem=recv_sems.at[base + k - 1], device_id=(peer // 4, (peer // 2) % 2, peer % 2),
                    device_id_type=MESH)

            vec_mine[...] = jnp.zeros_like(vec_mine)
            vec_mine[0:1, :] = dg_ref[...]
            for (_, row, _), ref in zip(VEC_LEAVES[1:], leaf_refs):
                vec_mine[row:row + 1, 0:ref.shape[1]] = ref[...]
            vec_mine[VEC_LOSS_ROW:VEC_LOSS_ROW + 1, 0:LANES] = loss_ref[...]
            copies = [copy(k, src, dst, base) for k in range(1, 8)
                      for src, dst, base in ((vec_mine, vec_recv, small_base), (dw4_ref, dw4_recv, small_base + 7))]
            for cp in copies:
                cp.start()
            for cp in copies:
                cp.wait_recv()
            vec_recv[me_lin] = vec_mine[...]
            dw4_recv[me_lin] = dw4_ref[...]
            vtot, wtot = vec_recv[0], dw4_recv[0]
            for d in range(1, 8):
                vtot = vtot + vec_recv[d]
                wtot = wtot + dw4_recv[d]
            vec_out[...] = vtot
            dw4_out[...] = wtot
            for cp in copies:
                cp.wait_send()

        @pl.when(i == n - 1)
        def _():
            if red is not None:
                red.sum_and_share()
            if smalls is not None:
                small_all_reduce()
            if red is not None:
                red.finish()

    any_spec = pl.BlockSpec(memory_space=pl.ANY)
    scratch = [pltpu.VMEM((t, PROJ_PAD), BF16)] + _ShardReduce.scratch(gparts, axes)
    out_shape = [jax.ShapeDtypeStruct((s, D_MODEL), F32), jax.ShapeDtypeStruct((1, D_MODEL), F32)]
    out_shape += [jax.ShapeDtypeStruct(g.shape[1:], F32) for g in gparts]
    out_specs = [_rows(t, D_MODEL), _full((1, D_MODEL))] + [any_spec] * na
    small_args = []
    if smalls is not None:
        small_args = [*vec_leaves, loss_row, dw4]
        out_shape += [jax.ShapeDtypeStruct((VEC_ROWS, D_MODEL), F32), jax.ShapeDtypeStruct(dw4.shape, F32)]
        out_specs += [_full((VEC_ROWS, D_MODEL)), _full(dw4.shape)]
        scratch += [pltpu.VMEM((VEC_ROWS, D_MODEL), F32), pltpu.VMEM((8, VEC_ROWS, D_MODEL), F32),
                    pltpu.VMEM((8,) + dw4.shape, F32)]
    if na or smalls is not None:
        n_sems = small_base + 14
        scratch += [pltpu.SemaphoreType.DMA((n_sems,)), pltpu.SemaphoreType.DMA((n_sems,)),
                    pltpu.SemaphoreType.DMA((max(_ShardReduce.LOCAL * na, 1),))]
    return pl.pallas_call(
        body, name="in_bwd_x", grid=(n,), out_shape=tuple(out_shape),
        in_specs=[_rows(t, D_MODEL), _rows(t, D_MODEL), _full((1, D_MODEL)),
                  pl.BlockSpec((PROJ_PAD, D_MODEL), lambda i: (0, 0), pipeline_mode=pl.Buffered(1))]
        + _dproj_specs(t) + [any_spec] * na + [_full(a.shape) for a in small_args],
        out_specs=tuple(out_specs), scratch_shapes=scratch, compiler_params=_params(),
    )(x, dy, norm_g, wp, *dparts, *gparts, *small_args)


def _in_bwd_w(hb, dpa, dgb, dpm, fox, gparts, axes):
    s = hb.shape[0]
    t = TILE
    n = s // t
    na = len(gparts)
    f_hi = F_ORIG_LO + FOX_HEADS
    n_in = 4 + len(fox)

    def body(*refs):
        h_ref, dpa_ref, dgb_ref, dpm_ref = refs[0:4]
        fox_refs = refs[4:n_in]
        g_refs = refs[n_in:n_in + na]
        o = n_in + na
        dw_ref, dqk_ref, dv_ref, dfb_ref, dqg_ref, dkg_ref, dbf_ref = refs[o:o + 7]
        out_refs = refs[o + 7:o + 7 + na]
        o += 7 + na
        fox_scratch = refs[o:o + 3]
        bufs = tuple(refs[o + 3 + k * na:o + 3 + (k + 1) * na] for k in range(5))
        i = pl.program_id(0)
        red = _ShardReduce(g_refs, out_refs, axes, bufs, *refs[o + 3 + 5 * na:]) if na else None

        @pl.when(i == 0)
        def _():
            dw_ref[...] = jnp.zeros_like(dw_ref)
            if red is not None:
                red.exchange_with_sibling()

        if red is not None:
            @pl.when(i == 1)
            def _():
                for k in (1, 2, 3):
                    red.send_to_chip(k)
                red.keep_mine()

        _fox_post_tile(i, n, t, *fox_refs, dqk_ref, dv_ref, dfb_ref, dqg_ref, dkg_ref, dbf_ref, *fox_scratch)

        hv = h_ref[...]
        for lo, ref in ((0, dpa_ref), (QB_LO, dqk_ref), (VB_LO, dv_ref), (f_hi, dgb_ref), (f_hi + FOX_WIDTH, dpm_ref)):
            dw_ref[lo:lo + ref.shape[1], :] += _dot(ref[...], hv, TN)
        dw_ref[F_ORIG_LO:f_hi, :] += _dot(dfb_ref[...].astype(BF16), hv, TN)[0:FOX_HEADS, :]

        if red is not None:
            @pl.when(i == n - 1)
            def _():
                red.sum_and_share()
                red.finish()

    def rev(w):
        return _rows_rev(t, w, n)

    any_spec = pl.BlockSpec(memory_space=pl.ANY)
    row = jax.ShapeDtypeStruct((1, LANES), F32)
    scratch = [pltpu.VMEM((1, FOX_WIDTH), F32), pltpu.VMEM((1, FOX_WIDTH), F32), pltpu.VMEM((1, LANES), F32)]
    scratch += _ShardReduce.scratch(gparts, axes)
    if na:
        scratch += [pltpu.SemaphoreType.DMA((_ShardReduce.SEMS * na,)), pltpu.SemaphoreType.DMA((_ShardReduce.SEMS * na,)),
                    pltpu.SemaphoreType.DMA((_ShardReduce.LOCAL * na,))]
    return pl.pallas_call(
        body, name="in_bwd_w", grid=(n,),
        out_shape=(jax.ShapeDtypeStruct((IN_WIDTH, D_MODEL), F32), jax.ShapeDtypeStruct((s, 2 * FOX_WIDTH), BF16),
                   jax.ShapeDtypeStruct((s, FOX_WIDTH), BF16), jax.ShapeDtypeStruct((s, LANES), F32), row, row, row)
        + tuple(jax.ShapeDtypeStruct(g.shape[1:], F32) for g in gparts),
        in_specs=[rev(D_MODEL), rev(512), rev(FOX_WIDTH), rev(512), rev(HEAD_BLOCKS), rev(HEAD_BLOCKS),
                  rev(HEAD_BLOCKS), rev(2 * FOX_WIDTH), rev(LANES), _full((1, LANES)), _full((1, FOX_WIDTH)),
                  _full((1, FOX_WIDTH))] + [any_spec] * na,
        out_specs=(pl.BlockSpec((IN_WIDTH, D_MODEL), lambda i: (0, 0), pipeline_mode=pl.Buffered(1)),
                   rev(2 * FOX_WIDTH), rev(FOX_WIDTH), rev(LANES), _full((1, LANES)), _full((1, LANES)),
                   _full((1, LANES))) + (any_spec,) * na,
        scratch_shapes=scratch, compiler_params=_params(),
    )(hb, dpa, dgb, dpm, *fox, *gparts)


def _adamw_math(w_ref, gv, m_ref, v_ref, d_ref, nm_ref, nv_ref):
    nm = ADAM_B1 * m_ref[...] + (1.0 - ADAM_B1) * gv
    nv = ADAM_B2 * v_ref[...] + (1.0 - ADAM_B2) * (gv * gv)
    m_hat = nm / (1.0 - ADAM_B1 ** ADAM_STEP)
    v_hat = nv / (1.0 - ADAM_B2 ** ADAM_STEP)
    d_ref[...] = -ADAM_LR * (m_hat / (jnp.sqrt(v_hat) + ADAM_EPS) + ADAM_WD * w_ref[...])
    nm_ref[...] = nm
    nv_ref[...] = nv


def _adamw(name, w, g, m, v):
    rows, cols = w.shape
    tc = 256 if rows * cols > 256 * 1024 else cols
    n = cols // tc

    def body(w_ref, g_ref, m_ref, v_ref, d_ref, nm_ref, nv_ref):
        _adamw_math(w_ref, g_ref[...], m_ref, v_ref, d_ref, nm_ref, nv_ref)

    spec = pl.BlockSpec((rows, tc), lambda i: (0, i))
    return pl.pallas_call(
        body, name=name, grid=(n,),
        out_shape=(jax.ShapeDtypeStruct((rows, cols), F32),) * 3,
        in_specs=[spec] * 4, out_specs=(spec,) * 3,
        compiler_params=_params(),
    )(w, g, m, v)


def _adamw_small(vec, dw4, leaves, pool):
    nl = len(VEC_LEAVES) + 1

    def body(*refs):
        vec_ref, dw4_ref = refs[0:2]
        wmv = refs[2:2 + 3 * nl]
        loss_ref = refs[2 + 3 * nl]
        outs = refs[3 + 3 * nl:]
        loss_ref[...] = vec_ref[VEC_LOSS_ROW:VEC_LOSS_ROW + 1, 0:1]
        for k in range(nl):
            if k < nl - 1:
                _, row, width = VEC_LEAVES[k]
                gv = vec_ref[row:row + 1, 0:width]
            else:
                gv = dw4_ref[...]
            w_ref, m_ref, v_ref = wmv[3 * k:3 * k + 3]
            g_ref, d_ref, nm_ref, nv_ref = outs[4 * k:4 * k + 4]
            g_ref[...] = gv
            _adamw_math(w_ref, gv, m_ref, v_ref, d_ref, nm_ref, nv_ref)

    shapes = [jax.ShapeDtypeStruct((1, width), F32) for _, _, width in VEC_LEAVES] + [
        jax.ShapeDtypeStruct(dw4.shape, F32)]
    flat_in = [a for triple in list(leaves) + [pool] for a in triple]
    res = pl.pallas_call(
        body, name="adamw_small",
        out_shape=(jax.ShapeDtypeStruct((1, 1), F32),) + tuple(s for s in shapes for _ in range(4)),
        compiler_params=pltpu.CompilerParams(vmem_limit_bytes=VMEM_LIMIT),
    )(vec, dw4, *flat_in)
    per = [res[1 + 4 * k:5 + 4 * k] for k in range(nl)]
    return res[0], [p[0] for p in per], [p[1] for p in per], [p[2] for p in per], [p[3] for p in per]


def _full_w_in_padded(halves):
    cols = IN_WIDTH // 4
    w_t = halves.reshape(4, 2, cols, D_MODEL // 2).transpose(0, 2, 1, 3).reshape(IN_WIDTH, D_MODEL)
    return jnp.concatenate([
        w_t[0:F_ORIG_LO], w_t[F_ORIG_LO + FOX_HEADS:], w_t[F_ORIG_LO:F_ORIG_LO + FOX_HEADS],
        jnp.zeros((PROJ_PAD - IN_WIDTH, D_MODEL), w_t.dtype)], axis=0)


def _tile_heads(g, n):
    return jnp.tile(g.reshape(1, HEAD_DIM), (1, n))


def kernel(x, mem, norm_g, w_in, b_f, w_pool, pool_scale, fox_q_g, fox_k_g, mem_norm_g, w_mem_kv, mem_q_g, mem_k_g, w_out, loss_target, m_norm_g, m_w_in, m_b_f, m_w_pool, m_pool_scale, m_fox_q_g, m_fox_k_g, m_mem_norm_g, m_w_mem_kv, m_mem_q_g, m_mem_k_g, m_w_out, v_norm_g, v_w_in, v_b_f, v_w_pool, v_pool_scale, v_fox_q_g, v_fox_k_g, v_mem_norm_g, v_w_mem_kv, v_mem_q_g, v_mem_k_g, v_w_out):
    w_in_t, m_w_in_t, v_w_in_t = w_in[0].T, m_w_in[0].T, v_w_in[0].T
    axes = (1, 0, 0)

    g_in, g_kv, g_out = _all_gather_weights([w_in_t, w_mem_kv[0], w_out[0]], axes)
    wp = _full_w_in_padded(g_in)
    tiled = _tiled_params(b_f, fox_q_g, fox_k_g, mem_q_g, mem_k_g)
    fwd = _fwd_in(x[0], norm_g, wp, *tiled[0:3])
    w_kv_b = g_kv.reshape(D_MODEL, 2 * MEM_WIDTH)
    w_out_b = g_out.reshape(D_MODEL, D_MODEL)
    w4 = w_pool.reshape(POOL_ROWS, HEAD_DIM)
    dy, hb, dpa, dgb, dpm, fox, dw_kv, dw_out, (dmemnorm_g, dpscale, dmq_g, dmk_g), loss_row, dw4 = _local_partials(
        x[0], mem[0], loss_target[0], fwd, w_kv_b, w_out_b, tiled, w4, pool_scale, mem_norm_g)

    early = [dw_kv.reshape(4, D_MODEL // 4, 2 * MEM_WIDTH), dw_out.reshape(4, D_MODEL // 4, D_MODEL)]
    dwp, dqk, dvb, dfb, dfq_g, dfk_g, dbf, g_w_kv, g_w_out = _in_bwd_w(hb, dpa, dgb, dpm, fox, early, axes[1:])
    dparts = (dpa, dqk, dvb, dgb, dpm, dfb)
    vec_leaves = (dmemnorm_g, dpscale, dbf, dfq_g, dfk_g, dmq_g, dmk_g)
    grad_x, _, g_w_in_t, vec, dw4_sum = _in_bwd_x(
        x[0], dy, norm_g, wp, dparts, [dwp.reshape(4, IN_WIDTH // 4, D_MODEL)], axes[0:1], (vec_leaves, loss_row, dw4))

    small_wmv = [(norm_g, m_norm_g, v_norm_g), (mem_norm_g, m_mem_norm_g, v_mem_norm_g),
                 (pool_scale, m_pool_scale, v_pool_scale), (b_f, m_b_f, v_b_f), (fox_q_g, m_fox_q_g, v_fox_q_g),
                 (fox_k_g, m_fox_k_g, v_fox_k_g), (mem_q_g, m_mem_q_g, v_mem_q_g), (mem_k_g, m_mem_k_g, v_mem_k_g)]
    pool_wmv = tuple(a.reshape(POOL_ROWS, HEAD_DIM) for a in (w_pool, m_w_pool, v_w_pool))
    loss, *small_out = _adamw_small(vec, dw4_sum, small_wmv, pool_wmv)
    big = [[g_w_in_t.T[None], g_w_kv[None], g_w_out[None]]]
    upd = [[a.T for a in _adamw("adamw_w_in", w_in_t, g_w_in_t, m_w_in_t, v_w_in_t)],
           _adamw("adamw_w_mem_kv", w_mem_kv[0], g_w_kv, m_w_mem_kv[0], v_w_mem_kv[0]),
           _adamw("adamw_w_out", w_out[0], g_w_out, m_w_out[0], v_w_out[0])]
    big += [[u[k][None] for u in upd] for k in range(3)]

    def leaves(k):
        sm = small_out[k]
        b_in, b_kv, b_out = big[k]
        return (sm[0], b_in, sm[3], sm[8].reshape(w_pool.shape), sm[2], sm[4], sm[5], sm[1], b_kv, sm[6], sm[7], b_out)

    return (loss.reshape(()), grad_x[None], *leaves(0), *leaves(1), *leaves(2), *leaves(3))


def _tiled_params(b_f, fox_q_g, fox_k_g, mem_q_g, mem_k_g):
    return (jnp.pad(b_f, ((0, 0), (0, LANES - FOX_HEADS))), _tile_heads(fox_q_g, FOX_HEADS),
            _tile_heads(fox_k_g, FOX_HEADS), _tile_heads(mem_q_g, 4), _tile_heads(mem_k_g, 4))


def _local_partials(xs, mems, tgt, fwd, w_kv_b, w_out_b, tiled, w4, pool_scale, mem_norm_g):
    hb, pa, qk, qa, ka, va, gb, pm, fb = fwd
    bf_pad, fq_g, fk_g, mq_g, mk_g = tiled

    mnb, kv, kmn, vmb = _mem_fwd(mems, mem_norm_g, w_kv_b, mk_g)
    ma, db = _pool_fwd(pa, w4, pool_scale)
    mm = _mem_attn_fwd(pm, kmn, vmb, mq_g)
    o, mb, r4 = _fox_fwd(qa, ka, va, gb)
    dy, dma, dmm, dw_out, loss_row, doa, dgb, rr = _out_loss(xs, tgt, ma, mb, mm, w_out_b, gb, o, r4)

    dpm, dkmn, dvm, dmq_g = _mem_attn_bwd(pm, dmm, kmn, vmb, mq_g)
    dw_kv, dmemnorm_g, dmk_g = _mem_bwd(dkmn, dvm, kv, mnb, mems, w_kv_b, mk_g, mem_norm_g)
    dpa, dw4, dpscale = _pool_bwd(pa, db, dma, w4, pool_scale)
    dka, dva, dqa = _fox_bwd(ka, va, qa, doa, rr)
    fox = (dqa, dka, dva, qk, fb, bf_pad, fq_g, fk_g)
    return dy, hb, dpa, dgb, dpm, fox, dw_kv, dw_out, (dmemnorm_g, dpscale, dmq_g, dmk_g), loss_row, dw4
```

```python
import functools

import jax
import jax.numpy as jnp
from jax import lax
from jax.experimental import pallas as pl
from jax.experimental.pallas import tpu as pltpu

F32 = jnp.float32
BF16 = jnp.bfloat16
MESH = pl.DeviceIdType.MESH

D_MODEL = 1024
HEAD_DIM = 64
POOL_WIDTH = 256
FOX_WIDTH = 512
FOX_HEADS = 8
MEM_WIDTH = 256
N_MEM = 256
IN_WIDTH = 3080
EPS = 1e-6
ATT_SCALE = 0.125

ADAM_LR = 0.001
ADAM_B1 = 0.9
ADAM_B2 = 0.999
ADAM_EPS = 1e-08
ADAM_WD = 0.01
ADAM_STEP = 10

LANES = 128
PA_LO, QB_LO, KB_LO, VB_LO, GB_LO, PM_LO, FB_LO, PROJ_PAD = 0, 512, 1024, 1536, 2048, 2560, 3072, 3200
F_ORIG_LO = 2048

TILE = 512
VMEM_LIMIT = 56 * 1024 * 1024

VEC_LEAVES = (("norm_g", 0, 1024), ("mem_norm_g", 1, 1024), ("pool_scale", 2, 256), ("b_f", 3, 8),
              ("fox_q_g", 4, 64), ("fox_k_g", 5, 64), ("mem_q_g", 6, 64), ("mem_k_g", 7, 64))
VEC_LOSS_ROW = 8
VEC_ROWS = 16
POOL_ROWS = 256


def _params(n_grid=1, vmem=VMEM_LIMIT):
    return pltpu.CompilerParams(dimension_semantics=("arbitrary",) * n_grid, vmem_limit_bytes=vmem)


def _rows(t, w):
    return pl.BlockSpec((t, w), lambda i: (i, 0))


def _rows_rev(t, w, n):
    return pl.BlockSpec((t, w), lambda i: (n - 1 - i, 0))


def _full(shape):
    return pl.BlockSpec(shape, lambda i: (0,) * len(shape))


def _sig(x):
    return 1.0 / (1.0 + jnp.exp(-x))


def _lane_lo(shape):
    return lax.broadcasted_iota(jnp.int32, shape, 1) < HEAD_DIM


def _pair_sum(v, lo):
    s0 = jnp.sum(jnp.where(lo, v, 0.0), axis=-1, keepdims=True)
    s1 = jnp.sum(jnp.where(lo, 0.0, v), axis=-1, keepdims=True)
    return jnp.where(lo, s0, s1)


def _head_rms(blk, lo):
    return lax.rsqrt(_pair_sum(blk * blk, lo) * (1.0 / HEAD_DIM) + EPS)


def _head_norm_bwd(dyn, xhat, rr, g, lo):
    a = dyn * g
    return rr * (a - xhat * (_pair_sum(xhat * a, lo) * (1.0 / HEAD_DIM)))


def _fold_heads(acc):
    tot = acc[:, 0:LANES]
    for p in range(1, acc.shape[1] // LANES):
        tot = tot + acc[:, p * LANES:(p + 1) * LANES]
    return tot + pltpu.roll(tot, HEAD_DIM, axis=1)


def _lane_pick(v, lane, idx):
    return jnp.sum(jnp.where(lane == idx, v, 0.0), axis=-1, keepdims=True)


NT = (((1,), (1,)), ((), ()))
TN = (((0,), (0,)), ((), ()))


def _dot(a, b, dims=None):
    if dims is None:
        return jnp.dot(a, b, preferred_element_type=F32)
    return lax.dot_general(a, b, dims, preferred_element_type=F32)


def _my_place():
    return lax.axis_index("x"), lax.axis_index("y"), lax.axis_index("c")


def _half_dims(shape, axis):
    return (shape[0] // 2, shape[1]) if axis == 0 else (shape[0], shape[1] // 2)


def _half_of(ref, axis, core, lead=False):
    rows, cols = ref.shape[-2:]
    if axis == 0:
        idx = (pl.ds(pl.multiple_of(core * (rows // 2), 16), rows // 2), slice(None))
    else:
        idx = (slice(None), pl.ds(pl.multiple_of(core * (cols // 2), LANES), cols // 2))
    return ref.at[(slice(None),) + idx] if lead else ref.at[idx]


class _HalfGather:
    def __init__(self, ins, outs, axes, f32_bufs, bf_bufs, send_sems, recv_sems, local_sems):
        self.ins, self.outs, self.axes = ins, outs, axes
        self.f32_bufs, self.bf_bufs = f32_bufs, bf_bufs
        self.send_sems, self.recv_sems, self.local_sems = send_sems, recv_sems, local_sems
        self.n = len(ins)
        x, y, self.c = _my_place()
        self.me, self.sibling = (x, y, self.c), (x, y, 1 - self.c)
        self.chips = [(1 - x, y), (x, 1 - y), (1 - x, 1 - y)]

    @staticmethod
    def scratch(shards, axes):
        dims = [_half_dims(a.shape, axis) for a, axis in zip(shards, axes)]
        n = len(shards)
        return [pltpu.VMEM(d, F32) for d in dims] + [pltpu.VMEM(d, BF16) for d in dims] + [
            pltpu.SemaphoreType.DMA((7 * n,)), pltpu.SemaphoreType.DMA((7 * n,)), pltpu.SemaphoreType.DMA((2 * n,))]

    @staticmethod
    def out_shapes(shards, axes):
        return tuple(jax.ShapeDtypeStruct((8,) + _half_dims(a.shape, axis), BF16) for a, axis in zip(shards, axes))

    def _blk(self, a, px, py, pc):
        return self.outs[a].at[4 * px + 2 * py + pc]

    def _copy(self, a, k, block, to, src=None):
        return pltpu.make_async_remote_copy(
            src_ref=self._blk(a, *block) if src is None else src, dst_ref=self._blk(a, *block),
            send_sem=self.send_sems.at[7 * a + k], recv_sem=self.recv_sems.at[7 * a + k], device_id=to,
            device_id_type=MESH)

    def _keep(self, a):
        return pltpu.make_async_copy(self.bf_bufs[a], self._blk(a, *self.me), self.local_sems.at[self.n + a])

    def _first(self, a):
        mine = [self._copy(a, 0, self.me, self.sibling, src=self.bf_bufs[a])]
        return mine + [self._copy(a, 1 + j, self.me, (*chip, self.c), src=self.bf_bufs[a])
                       for j, chip in enumerate(self.chips)]

    def send_mine(self):
        loads = [pltpu.make_async_copy(_half_of(self.ins[a], self.axes[a], self.c), self.f32_bufs[a],
                                       self.local_sems.at[a]) for a in range(self.n)]
        for cp in loads:
            cp.start()
        for a in range(self.n):
            loads[a].wait()
            self.bf_bufs[a][...] = self.f32_bufs[a][...].astype(BF16)
            self._keep(a).start()
            for cp in self._first(a):
                cp.start()

    def pass_on(self):
        for a in range(self.n):
            for j, chip in enumerate(self.chips):
                self._copy(a, 1 + j, (*chip, self.c), self.me).wait_recv()
                self._copy(a, 4 + j, (*chip, self.c), self.sibling).start()

    def finish(self):
        for a in range(self.n):
            self._copy(a, 0, self.sibling, self.me).wait_recv()
            for j, chip in enumerate(self.chips):
                self._copy(a, 4 + j, (*chip, 1 - self.c), self.me).wait_recv()
        for a in range(self.n):
            for cp in self._first(a):
                cp.wait_send()
            for j, chip in enumerate(self.chips):
                self._copy(a, 4 + j, (*chip, self.c), self.sibling).wait_send()
            self._keep(a).wait()


def _all_gather_weights(shards, axes):
    n = len(shards)

    def body(*refs):
        gather = _HalfGather(refs[0:n], refs[n:2 * n], axes, refs[2 * n:3 * n], refs[3 * n:4 * n], *refs[4 * n:])
        gather.send_mine()
        gather.pass_on()
        gather.finish()

    any_spec = pl.BlockSpec(memory_space=pl.ANY)
    return pl.pallas_call(
        body, name="weights_all_gather", out_shape=_HalfGather.out_shapes(shards, axes),
        in_specs=[any_spec] * n, out_specs=(any_spec,) * n, scratch_shapes=_HalfGather.scratch(shards, axes),
        compiler_params=pltpu.CompilerParams(vmem_limit_bytes=VMEM_LIMIT),
    )(*shards)


class _ShardReduce:
    SEMS = 8
    LOCAL = 5

    def __init__(self, g_refs, out_refs, axes, bufs, send_sems, recv_sems, local_sems):
        self.g_refs, self.out_refs, self.axes = g_refs, out_refs, axes
        self.recv_a, self.own_a, self.send_b, self.recv_b, self.fin = bufs
        self.send_sems, self.recv_sems, self.local_sems = send_sems, recv_sems, local_sems
        self.n = len(g_refs)
        x, y, self.c = _my_place()
        self.chip = 2 * x + y
        self.sibling = (x, y, 1 - self.c)

    @staticmethod
    def scratch(gparts, axes):
        dims = [_half_dims(g.shape[1:], axis) for g, axis in zip(gparts, axes)]
        shapes = []
        for dtype, lead in ((F32, (4,)), (F32, (4,)), (BF16, (4,)), (BF16, (4,)), (F32, ())):
            shapes += [pltpu.VMEM(lead + d, dtype) for d in dims]
        return shapes

    def _to_sibling(self, a, j):
        return pltpu.make_async_remote_copy(
            src_ref=_half_of(self.g_refs[a].at[j], self.axes[a], 1 - self.c), dst_ref=self.recv_a[a].at[j],
            send_sem=self.send_sems.at[self.SEMS * a + j], recv_sem=self.recv_sems.at[self.SEMS * a + j], device_id=self.sibling,
            device_id_type=MESH)

    def _own(self, a, j):
        return pltpu.make_async_copy(_half_of(self.g_refs[a].at[j], self.axes[a], self.c), self.own_a[a].at[j],
                                     self.local_sems.at[self.LOCAL * a + j])

    def _to_chip(self, a, k):
        dest = (self.chip + k) % 4
        return pltpu.make_async_remote_copy(
            src_ref=self.send_b[a].at[dest], dst_ref=self.recv_b[a].at[self.chip],
            send_sem=self.send_sems.at[self.SEMS * a + 3 + k], recv_sem=self.recv_sems.at[self.SEMS * a + 3 + k],
            device_id=(dest // 2, dest % 2, self.c), device_id_type=MESH)

    def _give(self, a):
        return pltpu.make_async_remote_copy(
            src_ref=self.fin[a], dst_ref=_half_of(self.out_refs[a], self.axes[a], self.c),
            send_sem=self.send_sems.at[self.SEMS * a + 7], recv_sem=self.recv_sems.at[self.SEMS * a + 7], device_id=self.sibling,
            device_id_type=MESH)

    def _mine(self, a):
        return pltpu.make_async_copy(self.fin[a], _half_of(self.out_refs[a], self.axes[a], self.c),
                                     self.local_sems.at[self.LOCAL * a])

    def exchange_with_sibling(self):
        for k in (1, 2, 3, 0):
            j = (self.chip + k) % 4
            for a in range(self.n):
                self._to_sibling(a, j).start()
                self._own(a, j).start()

    def _chip_partial(self, a, j):
        self._own(a, j).wait()
        self._to_sibling(a, j).wait_recv()
        self.send_b[a][j] = (self.own_a[a][j] + self.recv_a[a][j]).astype(BF16)

    def send_to_chip(self, k):
        for a in range(self.n):
            self._chip_partial(a, (self.chip + k) % 4)
            self._to_chip(a, k).start()

    def keep_mine(self):
        for a in range(self.n):
            self._chip_partial(a, self.chip)
            keep = pltpu.make_async_copy(self.send_b[a].at[self.chip], self.recv_b[a].at[self.chip],
                                         self.local_sems.at[self.LOCAL * a + 4])
            keep.start()
            keep.wait()

    def sum_and_share(self):
        for a in range(self.n):
            for k in range(1, 4):
                self._to_chip(a, k).wait_recv()
            tot = self.recv_b[a][0].astype(F32) + self.recv_b[a][1].astype(F32)
            tot = tot + self.recv_b[a][2].astype(F32)
            self.fin[a][...] = tot + self.recv_b[a][3].astype(F32)
            self._give(a).start()
            self._mine(a).start()

    def finish(self):
        for a in range(self.n):
            self._give(a).wait_recv()
            self._mine(a).wait()
            self._give(a).wait_send()
            for j in range(4):
                self._to_sibling(a, j).wait_send()
            for k in range(1, 4):
                self._to_chip(a, k).wait_send()


def _mem_tokens_fwd(mem_ref, g_ref, w_ref, kg_ref, mn_ref, kv_ref, kn_ref, vm_ref):
    xm = mem_ref[...]
    rr = lax.rsqrt(jnp.mean(xm * xm, axis=-1, keepdims=True) + EPS)
    mnb = ((xm * rr) * g_ref[...]).astype(BF16)
    mn_ref[...] = mnb
    kv = _dot(mnb, w_ref[...])
    kv_ref[...] = kv
    lo = _lane_lo((xm.shape[0], LANES))
    for p in range(MEM_WIDTH // LANES):
        sl = slice(p * LANES, (p + 1) * LANES)
        kb = kv[:, sl]
        kn_ref[:, sl] = ((kb * _head_rms(kb, lo)) * kg_ref[:, sl]).astype(BF16)
    vm_ref[...] = kv[:, MEM_WIDTH:].astype(BF16)


AUG_LO = 64
KEY_SUM_LANE = 72
QUERY_SUM_LANE = 80
HEAD_BLOCKS = FOX_HEADS * LANES


def _ones3(lane):
    return jnp.where((lane >= AUG_LO) & (lane < AUG_LO + 3), 1.0, 0.0)


def _spread3(cols):
    hi = cols.astype(BF16)
    rest = cols - hi.astype(F32)
    mid = rest.astype(BF16)
    low = (rest - mid.astype(F32)).astype(BF16)
    r = lax.broadcasted_iota(jnp.int32, (LANES, HEAD_BLOCKS), 0)
    c = lax.broadcasted_iota(jnp.int32, (LANES, HEAD_BLOCKS), 1)
    out = None
    for k, part in enumerate((hi, mid, low)):
        term = _dot(part, jnp.where(c == r * LANES + (AUG_LO + k), 1.0, 0.0).astype(BF16))
        out = term if out is None else out + term
    return out


def _head_block(pair_blk, hh, lo, extras):
    src = pair_blk if hh == 0 else pltpu.roll(pair_blk, HEAD_DIM, axis=1)
    return jnp.where(lo, src, extras).astype(BF16)


def _pair_block(blk0, blk1, lo):
    return jnp.where(lo, blk0, pltpu.roll(blk1, HEAD_DIM, axis=1))


def _fwd_in(x, norm_g, wp, bf_pad, fq_g, fk_g):
    s = x.shape[0]
    t = TILE
    n = s // t

    def body(x_ref, ng_ref, wp_ref, bf_ref, qg_ref, kg_ref,
             h_ref, pa_ref, qk_ref, qa_ref, ka_ref, va_ref, gb_ref, pm_ref, fb_ref, carry_ref, fcol_ref):
        @pl.when(pl.program_id(0) == 0)
        def _():
            carry_ref[...] = jnp.zeros_like(carry_ref)

        xv = x_ref[...]
        rr = lax.rsqrt(jnp.mean(xv * xv, axis=-1, keepdims=True) + EPS)
        hb = ((xv * rr) * ng_ref[...]).astype(BF16)
        h_ref[...] = hb

        def proj(lo, hi):
            return _dot(hb, wp_ref[lo:hi, :], NT)

        pa_ref[...] = proj(PA_LO, QB_LO)
        gb_ref[...] = proj(GB_LO, PM_LO)
        pm_ref[...] = proj(PM_LO, FB_LO)
        fb = proj(FB_LO, PROJ_PAD)
        fb_ref[...] = fb

        lane = lax.broadcasted_iota(jnp.int32, (t, LANES), 1)
        row = lax.broadcasted_iota(jnp.int32, (t, LANES), 0)
        lo = lane < HEAD_DIM
        z = fb + bf_ref[...]
        lf = -(jnp.maximum(-z, 0.0) + jnp.log1p(jnp.exp(-jnp.abs(z))))
        lf = jnp.where(lane < FOX_HEADS, lf, 0.0)
        sh = 1
        while sh < t:
            lf = lf + jnp.where(row >= sh, pltpu.roll(lf, sh, axis=0), 0.0)
            sh *= 2
        fcum = lf + carry_ref[...]
        fcol_ref[...] = fcum
        carry_ref[...] = fcol_ref[t - 1:t, :]

        ones3 = _ones3(lane)
        minus_f = _spread3(-fcum)
        for seg, g_ref, out_ref, scale in ((QB_LO, qg_ref, qa_ref, ATT_SCALE), (KB_LO, kg_ref, ka_ref, 1.0)):
            raw = proj(seg, seg + FOX_WIDTH)
            qk_ref[:, seg - QB_LO:seg - QB_LO + FOX_WIDTH] = raw
            for p in range(FOX_WIDTH // LANES):
                sl = slice(p * LANES, (p + 1) * LANES)
                blk = raw[:, sl]
                normed = ((blk * _head_rms(blk, lo)) * g_ref[:, sl]) * scale
                for hh in range(2):
                    h = 2 * p + hh
                    if seg == QB_LO:
                        extras = jnp.where(lane == QUERY_SUM_LANE + h, 1.0, ones3)
                    else:
                        extras = jnp.where(lane == KEY_SUM_LANE + h, 1.0, minus_f[:, h * LANES:(h + 1) * LANES])
                    out_ref[:, h * LANES:(h + 1) * LANES] = _head_block(normed, hh, lo, extras)
        vraw = proj(VB_LO, GB_LO)
        for h in range(FOX_HEADS):
            va_ref[:, h * LANES:(h + 1) * LANES] = _head_block(vraw[:, (h // 2) * LANES:(h // 2 + 1) * LANES], h % 2, lo, ones3)

    outs = (
        jax.ShapeDtypeStruct((s, D_MODEL), BF16),
        jax.ShapeDtypeStruct((s, 512), F32),
        jax.ShapeDtypeStruct((s, 2 * FOX_WIDTH), F32),
        jax.ShapeDtypeStruct((s, HEAD_BLOCKS), BF16),
        jax.ShapeDtypeStruct((s, HEAD_BLOCKS), BF16),
        jax.ShapeDtypeStruct((s, HEAD_BLOCKS), BF16),
        jax.ShapeDtypeStruct((s, FOX_WIDTH), F32),
        jax.ShapeDtypeStruct((s, 512), F32),
        jax.ShapeDtypeStruct((s, LANES), F32),
    )
    return pl.pallas_call(
        body, name="fwd_in", grid=(n,), out_shape=outs,
        in_specs=[_rows(t, D_MODEL), _full((1, D_MODEL)), _full((PROJ_PAD, D_MODEL)), _full((1, LANES)),
                  _full((1, FOX_WIDTH)), _full((1, FOX_WIDTH))],
        out_specs=(_rows(t, D_MODEL), _rows(t, 512), _rows(t, 2 * FOX_WIDTH), _rows(t, HEAD_BLOCKS),
                   _rows(t, HEAD_BLOCKS), _rows(t, HEAD_BLOCKS), _rows(t, FOX_WIDTH), _rows(t, 512),
                   _rows(t, LANES)),
        scratch_shapes=[pltpu.VMEM((1, LANES), F32), pltpu.VMEM((t, LANES), F32)],
        compiler_params=_params(),
    )(x, norm_g, wp, bf_pad, fq_g, fk_g)


POOL_HALO = 16


def _pool_window(lane):
    return jnp.where(lane < 64, 2.0, jnp.where(lane < 128, 4.0, jnp.where(lane < 192, 8.0, 16.0)))


def _pool_pick(lane, s2, s4, s8, s16):
    return jnp.where(lane < 64, s2, jnp.where(lane < 128, s4, jnp.where(lane < 192, s8, s16)))


def _group_onehot(shape, row_is_group_lane):
    r = lax.broadcasted_iota(jnp.int32, shape, 0)
    c = lax.broadcasted_iota(jnp.int32, shape, 1)
    hit = (r % HEAD_DIM == c) if row_is_group_lane else (c % HEAD_DIM == r)
    return jnp.where(hit, 1.0, 0.0).astype(F32)


def _same_group(shape):
    r = lax.broadcasted_iota(jnp.int32, shape, 0)
    c = lax.broadcasted_iota(jnp.int32, shape, 1)
    return (r // HEAD_DIM) == (c // HEAD_DIM)


def _pool_block_diag(w4):
    spread = jnp.dot(w4, _group_onehot((HEAD_DIM, POOL_WIDTH), False), preferred_element_type=F32,
                     precision=lax.Precision.HIGHEST)
    return jnp.where(_same_group((POOL_WIDTH, POOL_WIDTH)), spread, 0.0).astype(BF16)


def _mem_softmax(qm, kp):
    sc = _dot(qm, kp, NT)
    e = jnp.exp(sc - jnp.max(sc, axis=-1, keepdims=True))
    return e * (1.0 / jnp.sum(e, axis=-1, keepdims=True))


def _side_fwd(pa, pm, w4, pscale, mq_g, mem, mem_norm_g, w_kv, mk_g):
    s = pa.shape[0]
    t = TILE
    n = s // t
    ext = t + POOL_HALO
    nm = mem.shape[0]

    def body(pa_ref, pm_ref, w4_ref, sc_ref, g_ref, mem_ref, mg_ref, wkv_ref, kg_ref,
             ma_ref, d_ref, mm_ref, mn_ref, kv_ref, k_ref, v_ref, ext_ref, w_ref):
        i = pl.program_id(0)

        @pl.when(i == 0)
        def _():
            ext_ref[0:POOL_HALO, :] = jnp.zeros((POOL_HALO, POOL_WIDTH), F32)
            w_ref[...] = _pool_block_diag(w4_ref[...])
            _mem_tokens_fwd(mem_ref, mg_ref, wkv_ref, kg_ref, mn_ref, kv_ref, k_ref, v_ref)

        u = pa_ref[:, 0:POOL_WIDTH]
        ext_ref[POOL_HALO:ext, :] = u
        e = ext_ref[...]
        s2 = e + pltpu.roll(e, 1, axis=0)
        s4 = s2 + pltpu.roll(s2, 2, axis=0)
        s8 = s4 + pltpu.roll(s4, 4, axis=0)
        s16 = s8 + pltpu.roll(s8, 8, axis=0)
        lane_e = lax.broadcasted_iota(jnp.int32, (ext, POOL_WIDTH), 1)
        win = _pool_pick(lane_e, s2, s4, s8, s16)[POOL_HALO:ext, :]
        lane = lax.broadcasted_iota(jnp.int32, (t, POOL_WIDTH), 1)
        pos = (lax.broadcasted_iota(jnp.int32, (t, POOL_WIDTH), 0) + (i * t + 1)).astype(F32)
        d = win / jnp.minimum(pos, _pool_window(lane)) - u
        db = d.astype(BF16)
        d_ref[...] = db
        ya = _dot(db, w_ref[...]) * sc_ref[...]
        ga = pa_ref[:, POOL_WIDTH:2 * POOL_WIDTH]
        ma_ref[...] = (ya * (ga * _sig(ga))).astype(BF16)
        ext_ref[0:POOL_HALO, :] = ext_ref[t:ext, :]

        lo = _lane_lo((t, LANES))
        for p in range(MEM_WIDTH // LANES):
            sl = slice(p * LANES, (p + 1) * LANES)
            qb = pm_ref[:, sl]
            qs = (((qb * _head_rms(qb, lo)) * g_ref[:, sl]) * ATT_SCALE).astype(BF16)
            kp = k_ref[:, sl]
            vp = v_ref[:, sl]
            outs = []
            for hh in range(2):
                msk = lo if hh == 0 else jnp.logical_not(lo)
                prob = _mem_softmax(jnp.where(msk, qs, jnp.zeros_like(qs)), kp)
                outs.append(_dot(prob.astype(BF16), vp))
            o = jnp.where(lo, outs[0], outs[1])
            gm = pm_ref[:, MEM_WIDTH + p * LANES:MEM_WIDTH + (p + 1) * LANES]
            mm_ref[:, sl] = (o * (gm * _sig(gm))).astype(BF16)

    return pl.pallas_call(
        body, name="side_fwd", grid=(n,),
        out_shape=(jax.ShapeDtypeStruct((s, POOL_WIDTH), BF16), jax.ShapeDtypeStruct((s, POOL_WIDTH), BF16),
                   jax.ShapeDtypeStruct((s, MEM_WIDTH), BF16), jax.ShapeDtypeStruct((nm, D_MODEL), BF16),
                   jax.ShapeDtypeStruct((nm, 2 * MEM_WIDTH), F32), jax.ShapeDtypeStruct((nm, MEM_WIDTH), BF16),
                   jax.ShapeDtypeStruct((nm, MEM_WIDTH), BF16)),
        in_specs=[_rows(t, 512), _rows(t, 512), _full((POOL_ROWS, HEAD_DIM)), _full((1, POOL_WIDTH)),
                  _full((1, MEM_WIDTH)), _full((nm, D_MODEL)), _full((1, D_MODEL)), _full((D_MODEL, 2 * MEM_WIDTH)),
                  _full((1, MEM_WIDTH))],
        out_specs=(_rows(t, POOL_WIDTH), _rows(t, POOL_WIDTH), _rows(t, MEM_WIDTH), _full((nm, D_MODEL)),
                   _full((nm, 2 * MEM_WIDTH)), _full((nm, MEM_WIDTH)), _full((nm, MEM_WIDTH))),
        scratch_shapes=[pltpu.VMEM((ext, POOL_WIDTH), F32), pltpu.VMEM((POOL_WIDTH, POOL_WIDTH), BF16)],
        compiler_params=_params(),
    )(pa, pm, w4, pscale, mq_g, mem, mem_norm_g, w_kv, mk_g)


FOX_FWD_HEADS = 4


def _fox_fwd(qa, ka, va, gb):
    s = qa.shape[0]
    t = TILE
    n = s // t
    heads = FOX_FWD_HEADS
    pairs = heads // 2
    group_w = heads * LANES

    def body(qa_ref, ka_ref, va_ref, gb_ref, o_ref, mb_ref, r_ref):
        i = pl.program_id(1)
        lane = lax.broadcasted_iota(jnp.int32, (t, LANES), 1)
        lo = lane < HEAD_DIM
        causal = lax.broadcasted_iota(jnp.int32, (t, t), 1) <= lax.broadcasted_iota(jnp.int32, (t, t), 0)
        qas = [qa_ref[:, hh * LANES:(hh + 1) * LANES] for hh in range(heads)]

        def step(j, carry, masked):
            rows = pl.ds(pl.multiple_of(j * t, t), t)
            new = []
            for hh in range(heads):
                cols = slice(hh * LANES, (hh + 1) * LANES)
                m, acc = carry[hh]
                sc = _dot(qas[hh], ka_ref[rows, cols], NT)
                if masked:
                    sc = jnp.where(causal, sc, -1e30)
                m_new = jnp.maximum(m, jnp.max(sc, axis=-1, keepdims=True))
                acc = jnp.exp(m - m_new) * acc + _dot(jnp.exp(sc - m_new).astype(BF16), va_ref[rows, cols])
                new.append((m_new, acc))
            return tuple(new)

        init = (jnp.full((t, 1), -1e30, F32), jnp.zeros((t, LANES), F32))
        carry = lax.fori_loop(0, i, functools.partial(step, masked=False), (init,) * heads)
        res = step(i, carry, masked=True)
        for p in range(pairs):
            outs = []
            rcol = jnp.zeros((t, LANES), F32)
            for hh in range(2):
                m, acc = res[2 * p + hh]
                l = _lane_pick(acc, lane, AUG_LO)
                outs.append(acc * (1.0 / l))
                rcol = jnp.where(lane == hh, m + jnp.log(l), rcol)
            o = _pair_block(outs[0], outs[1], lo)
            sl = slice(p * LANES, (p + 1) * LANES)
            o_ref[:, sl] = o
            g = gb_ref[:, sl]
            mb_ref[:, sl] = (o * (g * _sig(g))).astype(BF16)
            r_ref[p] = rcol

    tile_spec = pl.BlockSpec((t, pairs * LANES), lambda p, i: (i, p))
    full_spec = pl.BlockSpec((s, group_w), lambda p, i: (0, p))
    return pl.pallas_call(
        body, name="fox_fwd", grid=(FOX_HEADS // heads, n),
        out_shape=(jax.ShapeDtypeStruct((s, FOX_WIDTH), F32), jax.ShapeDtypeStruct((s, FOX_WIDTH), BF16),
                   jax.ShapeDtypeStruct((FOX_HEADS // 2, s, LANES), F32)),
        in_specs=[pl.BlockSpec((t, group_w), lambda p, i: (i, p)), full_spec, full_spec, tile_spec],
        out_specs=(tile_spec, tile_spec, pl.BlockSpec((pairs, t, LANES), lambda p, i: (p, i, 0))),
        compiler_params=_params(2),
    )(qa, ka, va, gb)


def _out_loss(x, tgt, ma, mb, mm, wout, gb, o, r4):
    s = x.shape[0]
    t = TILE
    n = s // t
    pairs = FOX_HEADS // 2

    def body(x_ref, t_ref, ma_ref, mb_ref, mm_ref, w_ref, gb_ref, o_ref, r_ref,
             dy_ref, dma_ref, dmm_ref, dw_ref, loss_ref, doa_ref, dgb_ref, rr_ref, mix_ref):
        @pl.when(pl.program_id(0) == 0)
        def _():
            dw_ref[...] = jnp.zeros_like(dw_ref)
            loss_ref[...] = jnp.zeros_like(loss_ref)

        mix_ref[:, 0:256] = ma_ref[...]
        mix_ref[:, 256:768] = mb_ref[...]
        mix_ref[:, 768:1024] = mm_ref[...]
        mix = mix_ref[...]
        err = (x_ref[...] + _dot(mix, w_ref[...])) - t_ref[...]
        row_mean = jnp.sum(err * err, axis=-1, keepdims=True) * (1.0 / D_MODEL)
        loss_ref[...] += 0.5 * jnp.sum(row_mean, axis=0, keepdims=True)
        dy = err * (1.0 / D_MODEL)
        dy_ref[...] = dy
        dyb = dy.astype(BF16)
        dmix = _dot(dyb, w_ref[...], NT)
        dma_ref[...] = dmix[:, 0:256]
        dmm_ref[...] = dmix[:, 768:1024]
        dw_ref[...] += _dot(mix, dyb, TN)

        lane = lax.broadcasted_iota(jnp.int32, (t, LANES), 1)
        lo = lane < HEAD_DIM
        d_os = []
        delta = jnp.zeros((t, LANES), F32)
        for p in range(pairs):
            sl = slice(p * LANES, (p + 1) * LANES)
            g = gb_ref[:, sl]
            sg = _sig(g)
            dm = dmix[:, 256 + p * LANES:256 + (p + 1) * LANES]
            ov = o_ref[:, sl]
            d_o = dm * (g * sg)
            d_os.append(d_o)
            dgb_ref[:, sl] = (dm * ov * (sg * (1.0 + g * (1.0 - sg)))).astype(BF16)
            prod = d_o * ov
            delta = jnp.where(lane == 2 * p, jnp.sum(jnp.where(lo, prod, 0.0), axis=-1, keepdims=True), delta)
            delta = jnp.where(lane == 2 * p + 1, jnp.sum(jnp.where(lo, 0.0, prod), axis=-1, keepdims=True), delta)
            rr_ref[p, 0] = r_ref[p].T[0:8, :]
        minus_delta = _spread3(-delta)
        for h in range(FOX_HEADS):
            blk = slice(h * LANES, (h + 1) * LANES)
            doa_ref[:, blk] = _head_block(d_os[h // 2], h % 2, lo, minus_delta[:, blk])

    return pl.pallas_call(
        body, name="out_loss", grid=(n,),
        out_shape=(jax.ShapeDtypeStruct((s, D_MODEL), F32), jax.ShapeDtypeStruct((s, 256), F32),
                   jax.ShapeDtypeStruct((s, 256), F32), jax.ShapeDtypeStruct((D_MODEL, D_MODEL), F32),
                   jax.ShapeDtypeStruct((1, LANES), F32), jax.ShapeDtypeStruct((s, HEAD_BLOCKS), BF16),
                   jax.ShapeDtypeStruct((s, FOX_WIDTH), BF16), jax.ShapeDtypeStruct((pairs, n, 8, t), F32)),
        in_specs=[_rows(t, D_MODEL), _rows(t, D_MODEL), _rows(t, 256), _rows(t, 512), _rows(t, 256),
                  _full((D_MODEL, D_MODEL)), _rows(t, FOX_WIDTH), _rows(t, FOX_WIDTH),
                  pl.BlockSpec((pairs, t, LANES), lambda i: (0, i, 0))],
        out_specs=(_rows(t, D_MODEL), _rows(t, 256), _rows(t, 256), _full((D_MODEL, D_MODEL)), _full((1, LANES)),
                   _rows(t, HEAD_BLOCKS), _rows(t, FOX_WIDTH), pl.BlockSpec((pairs, 1, 8, t), lambda i: (0, i, 0, 0))),
        scratch_shapes=[pltpu.VMEM((t, D_MODEL), BF16)],
        compiler_params=_params(),
    )(x, tgt, ma, mb, mm, wout, gb, o, r4)


def _side_bwd(pa, db, dma, w4, pscale, pm, dmm, kmn, vmb, mq_g, kv, mnb, mem, w_kv, mk_g, mem_norm_g):
    s = pa.shape[0]
    t = TILE
    n = s // t
    ext = t + POOL_HALO
    nm = mem.shape[0]

    def body(pa_ref, d_ref, dma_ref, w4_ref, sc_ref, pm_ref, dmm_ref, k_ref, v_ref, g_ref,
             kv_ref, mn_ref, mem_ref, wkv_ref, kg_ref, mg_ref,
             dpa_ref, dpm_ref, dw4_ref, dsc_ref, dg_ref, dwkv_ref, dmg_ref, dkg_ref,
             ext_ref, w_ref, dw_ref, dk_ref, dv_ref, gacc_ref, dkv_ref):
        i = pl.program_id(0)

        @pl.when(i == 0)
        def _():
            dw_ref[...] = jnp.zeros_like(dw_ref)
            dsc_ref[...] = jnp.zeros_like(dsc_ref)
            ext_ref[t:ext, :] = jnp.zeros((POOL_HALO, POOL_WIDTH), F32)
            w_ref[...] = _pool_block_diag(w4_ref[...])
            dk_ref[...] = jnp.zeros_like(dk_ref)
            dv_ref[...] = jnp.zeros_like(dv_ref)
            gacc_ref[...] = jnp.zeros_like(gacc_ref)

        dbv = d_ref[...]
        z = _dot(dbv, w_ref[...])
        ga = pa_ref[:, POOL_WIDTH:2 * POOL_WIDTH]
        sg = _sig(ga)
        dma_v = dma_ref[...]
        dya = dma_v * (ga * sg)
        dpa_ref[:, POOL_WIDTH:2 * POOL_WIDTH] = (dma_v * (z * sc_ref[...]) * (sg * (1.0 + ga * (1.0 - sg)))).astype(BF16)
        dsc_ref[...] += jnp.sum(dya * z, axis=0, keepdims=True)
        dzb = (dya * sc_ref[...]).astype(BF16)
        dw_ref[...] += _dot(dbv, dzb, TN)
        dd = _dot(dzb, w_ref[...], NT)
        lane = lax.broadcasted_iota(jnp.int32, (t, POOL_WIDTH), 1)
        pos = (lax.broadcasted_iota(jnp.int32, (t, POOL_WIDTH), 0) + ((n - 1 - i) * t + 1)).astype(F32)
        ext_ref[0:t, :] = dd / jnp.minimum(pos, _pool_window(lane))
        e = ext_ref[...]
        s2 = e + pltpu.roll(e, ext - 1, axis=0)
        s4 = s2 + pltpu.roll(s2, ext - 2, axis=0)
        s8 = s4 + pltpu.roll(s4, ext - 4, axis=0)
        s16 = s8 + pltpu.roll(s8, ext - 8, axis=0)
        lane_e = lax.broadcasted_iota(jnp.int32, (ext, POOL_WIDTH), 1)
        win = _pool_pick(lane_e, s2, s4, s8, s16)[0:t, :]
        dpa_ref[:, 0:POOL_WIDTH] = (win - dd).astype(BF16)
        ext_ref[t:ext, :] = ext_ref[0:POOL_HALO, :]

        lo = _lane_lo((t, LANES))
        for p in range(MEM_WIDTH // LANES):
            sl = slice(p * LANES, (p + 1) * LANES)
            qb = pm_ref[:, sl]
            rr = _head_rms(qb, lo)
            qhat = qb * rr
            g = g_ref[:, sl]
            qs = ((qhat * g) * ATT_SCALE).astype(BF16)
            gm = pm_ref[:, MEM_WIDTH + p * LANES:MEM_WIDTH + (p + 1) * LANES]
            sg = _sig(gm)
            dmo = dmm_ref[:, sl]
            d_o = dmo * (gm * sg)
            kp = k_ref[:, sl]
            vp = v_ref[:, sl]
            outs, dqs = [], []
            for hh in range(2):
                msk = lo if hh == 0 else jnp.logical_not(lo)
                qm = jnp.where(msk, qs, jnp.zeros_like(qs))
                prob = _mem_softmax(qm, kp)
                pb = prob.astype(BF16)
                outs.append(_dot(pb, vp))
                dom = jnp.where(msk, d_o, 0.0).astype(BF16)
                dp = _dot(dom, vp, NT)
                ds = (prob * (dp - jnp.sum(prob * dp, axis=-1, keepdims=True))).astype(BF16)
                dqs.append(_dot(ds, kp))
                dk_ref[:, sl] += _dot(ds, qm, TN)
                dv_ref[:, sl] += _dot(pb, dom, TN)
            o = jnp.where(lo, outs[0], outs[1])
            dqn = jnp.where(lo, dqs[0], dqs[1]) * ATT_SCALE
            dpm_ref[:, sl] = _head_norm_bwd(dqn, qhat, rr, g, lo).astype(BF16)
            dpm_ref[:, MEM_WIDTH + p * LANES:MEM_WIDTH + (p + 1) * LANES] = (
                dmo * o * (sg * (1.0 + gm * (1.0 - sg)))).astype(BF16)
            gacc_ref[:, sl] += jnp.sum(dqn * qhat, axis=0, keepdims=True)

        @pl.when(i == n - 1)
        def _():
            own = jnp.where(_same_group((POOL_WIDTH, POOL_WIDTH)), dw_ref[...], 0.0)
            dw4_ref[...] = jnp.dot(own, _group_onehot((POOL_WIDTH, HEAD_DIM), True), preferred_element_type=F32,
                                   precision=lax.Precision.HIGHEST)
            dg_ref[...] = _fold_heads(gacc_ref[...])

            lo_m = _lane_lo((nm, LANES))
            kacc = []
            for p in range(MEM_WIDTH // LANES):
                sl = slice(p * LANES, (p + 1) * LANES)
                kb = kv_ref[:, sl]
                rr = _head_rms(kb, lo_m)
                khat = kb * rr
                dk = dk_ref[:, sl]
                dkv_ref[:, sl] = _head_norm_bwd(dk, khat, rr, kg_ref[:, sl], lo_m).astype(BF16)
                kacc.append(jnp.sum(dk * khat, axis=0, keepdims=True))
            dkg_ref[...] = _fold_heads(jnp.concatenate(kacc, axis=1))
            dkv_ref[:, MEM_WIDTH:] = dv_ref[...].astype(BF16)
            dkv = dkv_ref[...]
            dwkv_ref[...] = _dot(mn_ref[...], dkv, TN)
            dmn = _dot(dkv, wkv_ref[...], NT)
            xm = mem_ref[...]
            rr = lax.rsqrt(jnp.mean(xm * xm, axis=-1, keepdims=True) + EPS)
            dmg_ref[...] = jnp.sum(dmn * (xm * rr), axis=0, keepdims=True)

    def rev(w):
        return _rows_rev(t, w, n)

    row = jax.ShapeDtypeStruct((1, LANES), F32)
    return pl.pallas_call(
        body, name="side_bwd", grid=(n,),
        out_shape=(jax.ShapeDtypeStruct((s, 512), BF16), jax.ShapeDtypeStruct((s, 512), BF16),
                   jax.ShapeDtypeStruct((POOL_ROWS, HEAD_DIM), F32), jax.ShapeDtypeStruct((1, POOL_WIDTH), F32), row,
                   jax.ShapeDtypeStruct((D_MODEL, 2 * MEM_WIDTH), F32), jax.ShapeDtypeStruct((1, D_MODEL), F32), row),
        in_specs=[rev(512), rev(POOL_WIDTH), rev(POOL_WIDTH), _full((POOL_ROWS, HEAD_DIM)), _full((1, POOL_WIDTH)),
                  rev(512), rev(MEM_WIDTH), _full((N_MEM, MEM_WIDTH)), _full((N_MEM, MEM_WIDTH)), _full((1, MEM_WIDTH)),
                  _full((nm, 2 * MEM_WIDTH)), _full((nm, D_MODEL)), _full((nm, D_MODEL)),
                  _full((D_MODEL, 2 * MEM_WIDTH)), _full((1, MEM_WIDTH)), _full((1, D_MODEL))],
        out_specs=(rev(512), rev(512), _full((POOL_ROWS, HEAD_DIM)), _full((1, POOL_WIDTH)), _full((1, LANES)),
                   _full((D_MODEL, 2 * MEM_WIDTH)), _full((1, D_MODEL)), _full((1, LANES))),
        scratch_shapes=[pltpu.VMEM((ext, POOL_WIDTH), F32), pltpu.VMEM((POOL_WIDTH, POOL_WIDTH), BF16),
                        pltpu.VMEM((POOL_WIDTH, POOL_WIDTH), F32), pltpu.VMEM((N_MEM, MEM_WIDTH), F32),
                        pltpu.VMEM((N_MEM, MEM_WIDTH), F32), pltpu.VMEM((1, MEM_WIDTH), F32),
                        pltpu.VMEM((nm, 2 * MEM_WIDTH), BF16)],
        compiler_params=_params(),
    )(pa, db, dma, w4, pscale, pm, dmm, kmn, vmb, mq_g, kv, mnb, mem, w_kv, mk_g, mem_norm_g)


FOX_BWD_HEADS = 4


def _fox_bwd(ka, va, qa, doa, rr):
    s = ka.shape[0]
    t = TILE
    n = s // t
    heads = FOX_BWD_HEADS
    group_w = heads * LANES

    def body(ka_ref, va_ref, qa_ref, doa_ref, rr_ref, dka_ref, dva_ref, dqa_ref):
        j = pl.program_id(1)

        @pl.when(j == 0)
        def _():
            dqa_ref[...] = jnp.zeros_like(dqa_ref)

        causal = lax.broadcasted_iota(jnp.int32, (t, t), 0) <= lax.broadcasted_iota(jnp.int32, (t, t), 1)
        kas = [ka_ref[:, hh * LANES:(hh + 1) * LANES] for hh in range(heads)]
        vas = [va_ref[:, hh * LANES:(hh + 1) * LANES] for hh in range(heads)]

        def step(i, carry, masked):
            rows = pl.ds(pl.multiple_of(i * t, t), t)
            new = []
            for hh in range(heads):
                cols = slice(hh * LANES, (hh + 1) * LANES)
                dk_a, dv_a = carry[hh]
                qb = qa_ref[rows, cols]
                d_o = doa_ref[rows, cols]
                arg = _dot(kas[hh], qb, NT) - rr_ref[hh // 2, i, hh % 2:hh % 2 + 1, :]
                if masked:
                    arg = jnp.where(causal, arg, -1e30)
                pt = jnp.exp(arg)
                dst = (pt * _dot(vas[hh], d_o, NT)).astype(BF16)
                dv_a = dv_a + _dot(pt.astype(BF16), d_o)
                dk_a = dk_a + _dot(dst, qb)
                dqa_ref[rows, cols] += _dot(dst, kas[hh], TN)
                new.append((dk_a, dv_a))
            return tuple(new)

        zero = jnp.zeros((t, LANES), F32)
        carry = step(j, ((zero, zero),) * heads, masked=True)
        res = lax.fori_loop(j + 1, n, functools.partial(step, masked=False), carry)
        for hh in range(heads):
            cols = slice(hh * LANES, (hh + 1) * LANES)
            dka_ref[:, cols] = res[hh][0]
            dva_ref[:, cols] = res[hh][1]

    tile_spec = pl.BlockSpec((t, group_w), lambda p, j: (j, p))
    full_spec = pl.BlockSpec((s, group_w), lambda p, j: (0, p))
    return pl.pallas_call(
        body, name="fox_bwd", grid=(FOX_HEADS // heads, n),
        out_shape=(jax.ShapeDtypeStruct((s, HEAD_BLOCKS), F32),) * 3,
        in_specs=[tile_spec, tile_spec, full_spec, full_spec,
                  pl.BlockSpec((heads // 2, n, 8, t), lambda p, j: (p, 0, 0, 0))],
        out_specs=(tile_spec, tile_spec, full_spec),
        compiler_params=_params(2),
    )(ka, va, qa, doa, rr)


def _fox_post_tile(i, n, t, dqa_ref, dka_ref, dva_ref, qk_ref, fb_ref, bf_ref, qg_ref, kg_ref,
                   dqk_ref, dv_ref, dfb_ref, dqg_ref, dkg_ref, dbf_ref, qacc_ref, kacc_ref, carry_ref):
    @pl.when(i == 0)
    def _():
        qacc_ref[...] = jnp.zeros_like(qacc_ref)
        kacc_ref[...] = jnp.zeros_like(kacc_ref)
        dbf_ref[...] = jnp.zeros_like(dbf_ref)
        carry_ref[...] = jnp.zeros_like(carry_ref)

    lane = lax.broadcasted_iota(jnp.int32, (t, LANES), 1)
    row = lax.broadcasted_iota(jnp.int32, (t, LANES), 0)
    lo = lane < HEAD_DIM

    def head_blocks(ref, p):
        return ref[:, 2 * p * LANES:(2 * p + 1) * LANES], ref[:, (2 * p + 1) * LANES:(2 * p + 2) * LANES]

    dq_sum = jnp.zeros((t, LANES), F32)
    dk_sum = jnp.zeros((t, LANES), F32)
    for p in range(FOX_WIDTH // LANES):
        sl = slice(p * LANES, (p + 1) * LANES)
        dq0, dq1 = head_blocks(dqa_ref, p)
        dk0, dk1 = head_blocks(dka_ref, p)
        dv0, dv1 = head_blocks(dva_ref, p)
        dv_ref[:, sl] = _pair_block(dv0, dv1, lo).astype(BF16)
        dq_sum = dq_sum + (dq0 + dq1)
        dk_sum = dk_sum + (dk0 + dk1)
        for off, pair, g_ref, acc_ref, scale in ((0, _pair_block(dq0, dq1, lo), qg_ref, qacc_ref, ATT_SCALE),
                                                 (FOX_WIDTH, _pair_block(dk0, dk1, lo), kg_ref, kacc_ref, 1.0)):
            raw = qk_ref[:, off + p * LANES:off + (p + 1) * LANES]
            rr = _head_rms(raw, lo)
            xhat = raw * rr
            dn = pair * scale
            dqk_ref[:, off + p * LANES:off + (p + 1) * LANES] = _head_norm_bwd(
                dn, xhat, rr, g_ref[:, sl], lo).astype(BF16)
            acc_ref[:, sl] += jnp.sum(dn * xhat, axis=0, keepdims=True)

    acc = (pltpu.roll(dq_sum, LANES - KEY_SUM_LANE, axis=1) - pltpu.roll(dk_sum, LANES - QUERY_SUM_LANE, axis=1))
    acc = jnp.where(lane < FOX_HEADS, acc, 0.0)
    sh = 1
    while sh < t:
        acc = acc + jnp.where(row < t - sh, pltpu.roll(acc, t - sh, axis=0), 0.0)
        sh *= 2
    dlogf = acc + carry_ref[...]
    dfb_ref[...] = dlogf
    carry_ref[...] = dfb_ref[0:1, :]
    z = fb_ref[...] + bf_ref[...]
    dz = jnp.where(lane < FOX_HEADS, dlogf * (1.0 / (1.0 + jnp.exp(z))), 0.0)
    dfb_ref[...] = dz
    dbf_ref[...] += jnp.sum(dz, axis=0, keepdims=True)

    @pl.when(i == n - 1)
    def _():
        dqg_ref[...] = _fold_heads(qacc_ref[...])
        dkg_ref[...] = _fold_heads(kacc_ref[...])


def _assemble_dproj(dp_ref, dpa_ref, dqk_ref, dv_ref, dgb_ref, dpm_ref, dfb_ref):
    dp_ref[:, PA_LO:QB_LO] = dpa_ref[...]
    dp_ref[:, QB_LO:VB_LO] = dqk_ref[...]
    dp_ref[:, VB_LO:GB_LO] = dv_ref[...]
    dp_ref[:, GB_LO:PM_LO] = dgb_ref[...]
    dp_ref[:, PM_LO:FB_LO] = dpm_ref[...]
    dp_ref[:, FB_LO:PROJ_PAD] = dfb_ref[...].astype(BF16)


def _dproj_specs(t):
    return [_rows(t, 512), _rows(t, 2 * FOX_WIDTH), _rows(t, FOX_WIDTH), _rows(t, FOX_WIDTH), _rows(t, 512),
            _rows(t, LANES)]


IN_BWD_X_TILE = 256


def _in_bwd_x(x, dy, norm_g, wp, dparts, gparts, axes, smalls):
    s = x.shape[0]
    t = IN_BWD_X_TILE
    n = s // t
    na = len(gparts)
    n_dp = len(dparts)
    vec_leaves, loss_row, dw4 = smalls if smalls is not None else ((), None, None)
    nv = len(vec_leaves)
    n_small = nv + 2 if smalls is not None else 0
    small_base = _ShardReduce.SEMS * na

    def body(*refs):
        x_ref, dy_ref, g_ref, wp_ref = refs[0:4]
        dp_parts = refs[4:4 + n_dp]
        o = 4 + n_dp
        g_refs = refs[o:o + na]
        small_in = refs[o + na:o + na + n_small]
        o += na + n_small
        gx_ref, dg_ref = refs[o:o + 2]
        out_refs = refs[o + 2:o + 2 + na]
        small_out = refs[o + 2 + na:o + 2 + na + (2 if smalls is not None else 0)]
        o += 2 + na + len(small_out)
        dp_ref = refs[o]
        bufs = tuple(refs[o + 1 + k * na:o + 1 + (k + 1) * na] for k in range(5))
        rest = refs[o + 1 + 5 * na:]

        i = pl.program_id(0)
        if na or smalls is not None:
            send_sems, recv_sems, local_sems = rest[-3:]
        red = _ShardReduce(g_refs, out_refs, axes, bufs, send_sems, recv_sems, local_sems) if na else None

        @pl.when(i == 0)
        def _():
            dg_ref[...] = jnp.zeros_like(dg_ref)
            if red is not None:
                red.exchange_with_sibling()

        if red is not None:
            for k in (1, 2, 3):
                pl.when(i == k)(functools.partial(red.send_to_chip, k))
            pl.when(i == 4)(red.keep_mine)

        _assemble_dproj(dp_ref, *dp_parts)
        dh = _dot(dp_ref[...], wp_ref[...])
        xv = x_ref[...]
        rr = lax.rsqrt(jnp.mean(xv * xv, axis=-1, keepdims=True) + EPS)
        xhat = xv * rr
        scaled = dh * g_ref[...]
        gx_ref[...] = dy_ref[...] + rr * (scaled - xhat * jnp.mean(xhat * scaled, axis=-1, keepdims=True))
        dg_ref[...] += jnp.sum(dh * xhat, axis=0, keepdims=True)

        def small_all_reduce():
            leaf_refs, (loss_ref, dw4_ref) = small_in[0:nv], small_in[nv:]
            vec_out, dw4_out = small_out
            vec_mine, vec_recv, dw4_recv = rest[0:3]
            cx, cy, c = _my_place()
            me_lin = 4 * cx + 2 * cy + c

            def copy(k, src, dst, base):
                peer = (me_lin + k) % 8
                return pltpu.make_async_remote_copy(
                    src_ref=src, dst_ref=dst.at[me_lin], send_sem=send_sems.at[base + k - 1],
                    recv_sem=recv_sems.at[base + k - 1], device_id=(peer // 4, (peer // 2) % 2, peer % 2),
                    device_id_type=MESH)

            vec_mine[...] = jnp.zeros_like(vec_mine)
            vec_mine[0:1, :] = dg_ref[...]
            for (_, row, _), ref in zip(VEC_LEAVES[1:], leaf_refs):
                vec_mine[row:row + 1, 0:ref.shape[1]] = ref[...]
            vec_mine[VEC_LOSS_ROW:VEC_LOSS_ROW + 1, 0:LANES] = loss_ref[...]
            copies = [copy(k, src, dst, base) for k in range(1, 8)
                      for src, dst, base in ((vec_mine, vec_recv, small_base), (dw4_ref, dw4_recv, small_base + 7))]
            for cp in copies:
                cp.start()
            for cp in copies:
                cp.wait_recv()
            vec_recv[me_lin] = vec_mine[...]
            dw4_recv[me_lin] = dw4_ref[...]
            vtot, wtot = vec_recv[0], dw4_recv[0]
            for d in range(1, 8):
                vtot = vtot + vec_recv[d]
                wtot = wtot + dw4_recv[d]
            vec_out[...] = vtot
            dw4_out[...] = wtot
            for cp in copies:
                cp.wait_send()

        @pl.when(i == n - 1)
        def _():
            if red is not None:
                red.sum_and_share()
            if smalls is not None:
                small_all_reduce()
            if red is not None:
                red.finish()

    any_spec = pl.BlockSpec(memory_space=pl.ANY)
    scratch = [pltpu.VMEM((t, PROJ_PAD), BF16)] + _ShardReduce.scratch(gparts, axes)
    out_shape = [jax.ShapeDtypeStruct((s, D_MODEL), F32), jax.ShapeDtypeStruct((1, D_MODEL), F32)]
    out_shape += [jax.ShapeDtypeStruct(g.shape[1:], F32) for g in gparts]
    out_specs = [_rows(t, D_MODEL), _full((1, D_MODEL))] + [any_spec] * na
    small_args = []
    if smalls is not None:
        small_args = [*vec_leaves, loss_row, dw4]
        out_shape += [jax.ShapeDtypeStruct((VEC_ROWS, D_MODEL), F32), jax.ShapeDtypeStruct(dw4.shape, F32)]
        out_specs += [_full((VEC_ROWS, D_MODEL)), _full(dw4.shape)]
        scratch += [pltpu.VMEM((VEC_ROWS, D_MODEL), F32), pltpu.VMEM((8, VEC_ROWS, D_MODEL), F32),
                    pltpu.VMEM((8,) + dw4.shape, F32)]
    if na or smalls is not None:
        n_sems = small_base + 14
        scratch += [pltpu.SemaphoreType.DMA((n_sems,)), pltpu.SemaphoreType.DMA((n_sems,)),
                    pltpu.SemaphoreType.DMA((max(_ShardReduce.LOCAL * na, 1),))]
    return pl.pallas_call(
        body, name="in_bwd_x", grid=(n,), out_shape=tuple(out_shape),
        in_specs=[_rows(t, D_MODEL), _rows(t, D_MODEL), _full((1, D_MODEL)),
                  pl.BlockSpec((PROJ_PAD, D_MODEL), lambda i: (0, 0), pipeline_mode=pl.Buffered(1))]
        + _dproj_specs(t) + [any_spec] * na + [_full(a.shape) for a in small_args],
        out_specs=tuple(out_specs), scratch_shapes=scratch, compiler_params=_params(),
    )(x, dy, norm_g, wp, *dparts, *gparts, *small_args)


def _in_bwd_w(hb, dpa, dgb, dpm, fox, gparts, axes):
    s = hb.shape[0]
    t = TILE
    n = s // t
    na = len(gparts)
    f_hi = F_ORIG_LO + FOX_HEADS
    n_in = 4 + len(fox)

    def body(*refs):
        h_ref, dpa_ref, dgb_ref, dpm_ref = refs[0:4]
        fox_refs = refs[4:n_in]
        g_refs = refs[n_in:n_in + na]
        o = n_in + na
        dw_ref, dqk_ref, dv_ref, dfb_ref, dqg_ref, dkg_ref, dbf_ref = refs[o:o + 7]
        out_refs = refs[o + 7:o + 7 + na]
        o += 7 + na
        fox_scratch = refs[o:o + 3]
        bufs = tuple(refs[o + 3 + k * na:o + 3 + (k + 1) * na] for k in range(5))
        i = pl.program_id(0)
        red = _ShardReduce(g_refs, out_refs, axes, bufs, *refs[o + 3 + 5 * na:]) if na else None

        @pl.when(i == 0)
        def _():
            dw_ref[...] = jnp.zeros_like(dw_ref)
            if red is not None:
                red.exchange_with_sibling()

        if red is not None:
            @pl.when(i == 1)
            def _():
                for k in (1, 2, 3):
                    red.send_to_chip(k)
                red.keep_mine()

        _fox_post_tile(i, n, t, *fox_refs, dqk_ref, dv_ref, dfb_ref, dqg_ref, dkg_ref, dbf_ref, *fox_scratch)

        hv = h_ref[...]
        for lo, ref in ((0, dpa_ref), (QB_LO, dqk_ref), (VB_LO, dv_ref), (f_hi, dgb_ref), (f_hi + FOX_WIDTH, dpm_ref)):
            dw_ref[lo:lo + ref.shape[1], :] += _dot(ref[...], hv, TN)
        dw_ref[F_ORIG_LO:f_hi, :] += _dot(dfb_ref[...].astype(BF16), hv, TN)[0:FOX_HEADS, :]

        if red is not None:
            @pl.when(i == n - 1)
            def _():
                red.sum_and_share()
                red.finish()

    def rev(w):
        return _rows_rev(t, w, n)

    any_spec = pl.BlockSpec(memory_space=pl.ANY)
    row = jax.ShapeDtypeStruct((1, LANES), F32)
    scratch = [pltpu.VMEM((1, FOX_WIDTH), F32), pltpu.VMEM((1, FOX_WIDTH), F32), pltpu.VMEM((1, LANES), F32)]
    scratch += _ShardReduce.scratch(gparts, axes)
    if na:
        scratch += [pltpu.SemaphoreType.DMA((_ShardReduce.SEMS * na,)), pltpu.SemaphoreType.DMA((_ShardReduce.SEMS * na,)),
                    pltpu.SemaphoreType.DMA((_ShardReduce.LOCAL * na,))]
    return pl.pallas_call(
        body, name="in_bwd_w", grid=(n,),
        out_shape=(jax.ShapeDtypeStruct((IN_WIDTH, D_MODEL), F32), jax.ShapeDtypeStruct((s, 2 * FOX_WIDTH), BF16),
                   jax.ShapeDtypeStruct((s, FOX_WIDTH), BF16), jax.ShapeDtypeStruct((s, LANES), F32), row, row, row)
        + tuple(jax.ShapeDtypeStruct(g.shape[1:], F32) for g in gparts),
        in_specs=[rev(D_MODEL), rev(512), rev(FOX_WIDTH), rev(512), rev(HEAD_BLOCKS), rev(HEAD_BLOCKS),
                  rev(HEAD_BLOCKS), rev(2 * FOX_WIDTH), rev(LANES), _full((1, LANES)), _full((1, FOX_WIDTH)),
                  _full((1, FOX_WIDTH))] + [any_spec] * na,
        out_specs=(pl.BlockSpec((IN_WIDTH, D_MODEL), lambda i: (0, 0), pipeline_mode=pl.Buffered(1)),
                   rev(2 * FOX_WIDTH), rev(FOX_WIDTH), rev(LANES), _full((1, LANES)), _full((1, LANES)),
                   _full((1, LANES))) + (any_spec,) * na,
        scratch_shapes=scratch, compiler_params=_params(),
    )(hb, dpa, dgb, dpm, *fox, *gparts)


def _adamw_math(w_ref, gv, m_ref, v_ref, d_ref, nm_ref, nv_ref):
    nm = ADAM_B1 * m_ref[...] + (1.0 - ADAM_B1) * gv
    nv = ADAM_B2 * v_ref[...] + (1.0 - ADAM_B2) * (gv * gv)
    m_hat = nm / (1.0 - ADAM_B1 ** ADAM_STEP)
    v_hat = nv / (1.0 - ADAM_B2 ** ADAM_STEP)
    d_ref[...] = -ADAM_LR * (m_hat / (jnp.sqrt(v_hat) + ADAM_EPS) + ADAM_WD * w_ref[...])
    nm_ref[...] = nm
    nv_ref[...] = nv


def _adamw(name, w, g, m, v):
    rows, cols = w.shape
    tc = 256 if rows * cols > 256 * 1024 else cols
    n = cols // tc

    def body(w_ref, g_ref, m_ref, v_ref, d_ref, nm_ref, nv_ref):
        _adamw_math(w_ref, g_ref[...], m_ref, v_ref, d_ref, nm_ref, nv_ref)

    spec = pl.BlockSpec((rows, tc), lambda i: (0, i))
    return pl.pallas_call(
        body, name=name, grid=(n,),
        out_shape=(jax.ShapeDtypeStruct((rows, cols), F32),) * 3,
        in_specs=[spec] * 4, out_specs=(spec,) * 3,
        compiler_params=_params(),
    )(w, g, m, v)


def _adamw_rest(vec, dw4, leaves, pool, shards):
    nl = len(VEC_LEAVES) + 1
    ns = len(shards)

    def body(*refs):
        vec_ref, dw4_ref = refs[0:2]
        wmv = refs[2:2 + 3 * nl]
        shard_in = refs[2 + 3 * nl:2 + 3 * nl + 4 * ns]
        o = 2 + 3 * nl + 4 * ns
        loss_ref = refs[o]
        outs = refs[o + 1:o + 1 + 4 * nl]
        shard_out = refs[o + 1 + 4 * nl:]
        loss_ref[...] = vec_ref[VEC_LOSS_ROW:VEC_LOSS_ROW + 1, 0:1]
        for k in range(nl):
            if k < nl - 1:
                _, row, width = VEC_LEAVES[k]
                gv = vec_ref[row:row + 1, 0:width]
            else:
                gv = dw4_ref[...]
            w_ref, m_ref, v_ref = wmv[3 * k:3 * k + 3]
            g_ref, d_ref, nm_ref, nv_ref = outs[4 * k:4 * k + 4]
            g_ref[...] = gv
            _adamw_math(w_ref, gv, m_ref, v_ref, d_ref, nm_ref, nv_ref)
        for k in range(ns):
            w_ref, g_ref, m_ref, v_ref = shard_in[4 * k:4 * k + 4]
            _adamw_math(w_ref, g_ref[...], m_ref, v_ref, *shard_out[3 * k:3 * k + 3])

    shapes = [jax.ShapeDtypeStruct((1, width), F32) for _, _, width in VEC_LEAVES] + [
        jax.ShapeDtypeStruct(dw4.shape, F32)]
    flat_in = [a for triple in list(leaves) + [pool] for a in triple] + [a for quad in shards for a in quad]
    res = pl.pallas_call(
        body, name="adamw_rest",
        out_shape=(jax.ShapeDtypeStruct((1, 1), F32),) + tuple(s for s in shapes for _ in range(4))
        + tuple(jax.ShapeDtypeStruct(quad[0].shape, F32) for quad in shards for _ in range(3)),
        compiler_params=pltpu.CompilerParams(vmem_limit_bytes=VMEM_LIMIT),
    )(vec, dw4, *flat_in)
    per = [res[1 + 4 * k:5 + 4 * k] for k in range(nl)]
    big = res[1 + 4 * nl:]
    return (res[0], [p[0] for p in per], [p[1] for p in per], [p[2] for p in per], [p[3] for p in per],
            [big[3 * k:3 * k + 3] for k in range(ns)])


def _full_w_in_padded(halves):
    cols = IN_WIDTH // 4
    w_t = halves.reshape(4, 2, cols, D_MODEL // 2).transpose(0, 2, 1, 3).reshape(IN_WIDTH, D_MODEL)
    return jnp.concatenate([
        w_t[0:F_ORIG_LO], w_t[F_ORIG_LO + FOX_HEADS:], w_t[F_ORIG_LO:F_ORIG_LO + FOX_HEADS],
        jnp.zeros((PROJ_PAD - IN_WIDTH, D_MODEL), w_t.dtype)], axis=0)


def _tile_heads(g, n):
    return jnp.tile(g.reshape(1, HEAD_DIM), (1, n))


def kernel(x, mem, norm_g, w_in, b_f, w_pool, pool_scale, fox_q_g, fox_k_g, mem_norm_g, w_mem_kv, mem_q_g, mem_k_g, w_out, loss_target, m_norm_g, m_w_in, m_b_f, m_w_pool, m_pool_scale, m_fox_q_g, m_fox_k_g, m_mem_norm_g, m_w_mem_kv, m_mem_q_g, m_mem_k_g, m_w_out, v_norm_g, v_w_in, v_b_f, v_w_pool, v_pool_scale, v_fox_q_g, v_fox_k_g, v_mem_norm_g, v_w_mem_kv, v_mem_q_g, v_mem_k_g, v_w_out):
    w_in_t, m_w_in_t, v_w_in_t = w_in[0].T, m_w_in[0].T, v_w_in[0].T
    axes = (1, 0, 0)

    g_in, g_kv, g_out = _all_gather_weights([w_in_t, w_mem_kv[0], w_out[0]], axes)
    wp = _full_w_in_padded(g_in)
    tiled = _tiled_params(b_f, fox_q_g, fox_k_g, mem_q_g, mem_k_g)
    fwd = _fwd_in(x[0], norm_g, wp, *tiled[0:3])
    w_kv_b = g_kv.reshape(D_MODEL, 2 * MEM_WIDTH)
    w_out_b = g_out.reshape(D_MODEL, D_MODEL)
    w4 = w_pool.reshape(POOL_ROWS, HEAD_DIM)
    dy, hb, dpa, dgb, dpm, fox, dw_kv, dw_out, (dmemnorm_g, dpscale, dmq_g, dmk_g), loss_row, dw4 = _local_partials(
        x[0], mem[0], loss_target[0], fwd, w_kv_b, w_out_b, tiled, w4, pool_scale, mem_norm_g)

    early = [dw_kv.reshape(4, D_MODEL // 4, 2 * MEM_WIDTH), dw_out.reshape(4, D_MODEL // 4, D_MODEL)]
    dwp, dqk, dvb, dfb, dfq_g, dfk_g, dbf, g_w_kv, g_w_out = _in_bwd_w(hb, dpa, dgb, dpm, fox, early, axes[1:])
    dparts = (dpa, dqk, dvb, dgb, dpm, dfb)
    vec_leaves = (dmemnorm_g, dpscale, dbf, dfq_g, dfk_g, dmq_g, dmk_g)
    grad_x, _, g_w_in_t, vec, dw4_sum = _in_bwd_x(
        x[0], dy, norm_g, wp, dparts, [dwp.reshape(4, IN_WIDTH // 4, D_MODEL)], axes[0:1], (vec_leaves, loss_row, dw4))

    small_wmv = [(norm_g, m_norm_g, v_norm_g), (mem_norm_g, m_mem_norm_g, v_mem_norm_g),
                 (pool_scale, m_pool_scale, v_pool_scale), (b_f, m_b_f, v_b_f), (fox_q_g, m_fox_q_g, v_fox_q_g),
                 (fox_k_g, m_fox_k_g, v_fox_k_g), (mem_q_g, m_mem_q_g, v_mem_q_g), (mem_k_g, m_mem_k_g, v_mem_k_g)]
    pool_wmv = tuple(a.reshape(POOL_ROWS, HEAD_DIM) for a in (w_pool, m_w_pool, v_w_pool))
    loss, *small_out, (upd_kv, upd_out) = _adamw_rest(
        vec, dw4_sum, small_wmv, pool_wmv, [(w_mem_kv[0], g_w_kv, m_w_mem_kv[0], v_w_mem_kv[0]),
                                             (w_out[0], g_w_out, m_w_out[0], v_w_out[0])])
    big = [[g_w_in_t.T[None], g_w_kv[None], g_w_out[None]]]
    upd = [[a.T for a in _adamw("adamw_w_in", w_in_t, g_w_in_t, m_w_in_t, v_w_in_t)], upd_kv, upd_out]
    big += [[u[k][None] for u in upd] for k in range(3)]

    def leaves(k):
        sm = small_out[k]
        b_in, b_kv, b_out = big[k]
        return (sm[0], b_in, sm[3], sm[8].reshape(w_pool.shape), sm[2], sm[4], sm[5], sm[1], b_kv, sm[6], sm[7], b_out)

    return (loss.reshape(()), grad_x[None], *leaves(0), *leaves(1), *leaves(2), *leaves(3))


def _tiled_params(b_f, fox_q_g, fox_k_g, mem_q_g, mem_k_g):
    return (jnp.pad(b_f, ((0, 0), (0, LANES - FOX_HEADS))), _tile_heads(fox_q_g, FOX_HEADS),
            _tile_heads(fox_k_g, FOX_HEADS), _tile_heads(mem_q_g, 4), _tile_heads(mem_k_g, 4))


def _local_partials(xs, mems, tgt, fwd, w_kv_b, w_out_b, tiled, w4, pool_scale, mem_norm_g):
    hb, pa, qk, qa, ka, va, gb, pm, fb = fwd
    bf_pad, fq_g, fk_g, mq_g, mk_g = tiled

    ma, db, mm, mnb, kv, kmn, vmb = _side_fwd(pa, pm, w4, pool_scale, mq_g, mems, mem_norm_g, w_kv_b, mk_g)
    o, mb, r4 = _fox_fwd(qa, ka, va, gb)
    dy, dma, dmm, dw_out, loss_row, doa, dgb, rr = _out_loss(xs, tgt, ma, mb, mm, w_out_b, gb, o, r4)

    dpa, dpm, dw4, dpscale, dmq_g, dw_kv, dmemnorm_g, dmk_g = _side_bwd(
        pa, db, dma, w4, pool_scale, pm, dmm, kmn, vmb, mq_g, kv, mnb, mems, w_kv_b, mk_g, mem_norm_g)
    dka, dva, dqa = _fox_bwd(ka, va, qa, doa, rr)
    fox = (dqa, dka, dva, qk, fb, bf_pad, fq_g, fk_g)
    return dy, hb, dpa, dgb, dpm, fox, dw_kv, dw_out, (dmemnorm_g, dpscale, dmq_g, dmk_g), loss_row, dw4
```

```python
import functools

import jax
import jax.numpy as jnp
from jax import lax
from jax.experimental import pallas as pl
from jax.experimental.pallas import tpu as pltpu

F32 = jnp.float32
BF16 = jnp.bfloat16
MESH = pl.DeviceIdType.MESH

D_MODEL = 1024
HEAD_DIM = 64
POOL_WIDTH = 256
FOX_WIDTH = 512
FOX_HEADS = 8
MEM_WIDTH = 256
N_MEM = 256
IN_WIDTH = 3080
EPS = 1e-6
ATT_SCALE = 0.125

ADAM_LR = 0.001
ADAM_B1 = 0.9
ADAM_B2 = 0.999
ADAM_EPS = 1e-08
ADAM_WD = 0.01
ADAM_STEP = 10

LANES = 128
PA_LO, QB_LO, KB_LO, VB_LO, GB_LO, PM_LO, FB_LO, PROJ_PAD = 0, 512, 1024, 1536, 2048, 2560, 3072, 3200
F_ORIG_LO = 2048

TILE = 512
VMEM_LIMIT = 56 * 1024 * 1024

VEC_LEAVES = (("norm_g", 0, 1024), ("mem_norm_g", 1, 1024), ("pool_scale", 2, 256), ("b_f", 3, 8),
              ("fox_q_g", 4, 64), ("fox_k_g", 5, 64), ("mem_q_g", 6, 64), ("mem_k_g", 7, 64))
VEC_LOSS_ROW = 8
VEC_ROWS = 16
POOL_ROWS = 256


def _params(n_grid=1, vmem=VMEM_LIMIT):
    return pltpu.CompilerParams(dimension_semantics=("arbitrary",) * n_grid, vmem_limit_bytes=vmem)


def _rows(t, w):
    return pl.BlockSpec((t, w), lambda i: (i, 0))


def _rows_rev(t, w, n):
    return pl.BlockSpec((t, w), lambda i: (n - 1 - i, 0))


def _full(shape):
    return pl.BlockSpec(shape, lambda i: (0,) * len(shape))


def _sig(x):
    return 1.0 / (1.0 + jnp.exp(-x))


def _lane_lo(shape):
    return lax.broadcasted_iota(jnp.int32, shape, 1) < HEAD_DIM


def _pair_sum(v, lo):
    s0 = jnp.sum(jnp.where(lo, v, 0.0), axis=-1, keepdims=True)
    s1 = jnp.sum(jnp.where(lo, 0.0, v), axis=-1, keepdims=True)
    return jnp.where(lo, s0, s1)


def _head_rms(blk, lo):
    return lax.rsqrt(_pair_sum(blk * blk, lo) * (1.0 / HEAD_DIM) + EPS)


def _head_norm_bwd(dyn, xhat, rr, g, lo):
    a = dyn * g
    return rr * (a - xhat * (_pair_sum(xhat * a, lo) * (1.0 / HEAD_DIM)))


def _fold_heads(acc):
    tot = acc[:, 0:LANES]
    for p in range(1, acc.shape[1] // LANES):
        tot = tot + acc[:, p * LANES:(p + 1) * LANES]
    return tot + pltpu.roll(tot, HEAD_DIM, axis=1)


def _lane_pick(v, lane, idx):
    return jnp.sum(jnp.where(lane == idx, v, 0.0), axis=-1, keepdims=True)


NT = (((1,), (1,)), ((), ()))
TN = (((0,), (0,)), ((), ()))


def _dot(a, b, dims=None):
    if dims is None:
        return jnp.dot(a, b, preferred_element_type=F32)
    return lax.dot_general(a, b, dims, preferred_element_type=F32)


def _my_place():
    return lax.axis_index("x"), lax.axis_index("y"), lax.axis_index("c")


def _half_dims(shape, axis):
    return (shape[0] // 2, shape[1]) if axis == 0 else (shape[0], shape[1] // 2)


def _half_of(ref, axis, core, lead=False):
    rows, cols = ref.shape[-2:]
    if axis == 0:
        idx = (pl.ds(pl.multiple_of(core * (rows // 2), 16), rows // 2), slice(None))
    else:
        idx = (slice(None), pl.ds(pl.multiple_of(core * (cols // 2), LANES), cols // 2))
    return ref.at[(slice(None),) + idx] if lead else ref.at[idx]


class _HalfGather:
    def __init__(self, ins, outs, axes, f32_bufs, bf_bufs, send_sems, recv_sems, local_sems):
        self.ins, self.outs, self.axes = ins, outs, axes
        self.f32_bufs, self.bf_bufs = f32_bufs, bf_bufs
        self.send_sems, self.recv_sems, self.local_sems = send_sems, recv_sems, local_sems
        self.n = len(ins)
        x, y, self.c = _my_place()
        self.me, self.sibling = (x, y, self.c), (x, y, 1 - self.c)
        self.chips = [(1 - x, y), (x, 1 - y), (1 - x, 1 - y)]

    @staticmethod
    def scratch(shards, axes):
        dims = [_half_dims(a.shape, axis) for a, axis in zip(shards, axes)]
        n = len(shards)
        return [pltpu.VMEM(d, F32) for d in dims] + [pltpu.VMEM(d, BF16) for d in dims] + [
            pltpu.SemaphoreType.DMA((7 * n,)), pltpu.SemaphoreType.DMA((7 * n,)), pltpu.SemaphoreType.DMA((2 * n,))]

    @staticmethod
    def out_shapes(shards, axes):
        return tuple(jax.ShapeDtypeStruct((8,) + _half_dims(a.shape, axis), BF16) for a, axis in zip(shards, axes))

    def _blk(self, a, px, py, pc):
        return self.outs[a].at[4 * px + 2 * py + pc]

    def _copy(self, a, k, block, to, src=None):
        return pltpu.make_async_remote_copy(
            src_ref=self._blk(a, *block) if src is None else src, dst_ref=self._blk(a, *block),
            send_sem=self.send_sems.at[7 * a + k], recv_sem=self.recv_sems.at[7 * a + k], device_id=to,
            device_id_type=MESH)

    def _keep(self, a):
        return pltpu.make_async_copy(self.bf_bufs[a], self._blk(a, *self.me), self.local_sems.at[self.n + a])

    def _first(self, a):
        mine = [self._copy(a, 0, self.me, self.sibling, src=self.bf_bufs[a])]
        return mine + [self._copy(a, 1 + j, self.me, (*chip, self.c), src=self.bf_bufs[a])
                       for j, chip in enumerate(self.chips)]

    def send_mine(self):
        loads = [pltpu.make_async_copy(_half_of(self.ins[a], self.axes[a], self.c), self.f32_bufs[a],
                                       self.local_sems.at[a]) for a in range(self.n)]
        for cp in loads:
            cp.start()
        for a in range(self.n):
            loads[a].wait()
            self.bf_bufs[a][...] = self.f32_bufs[a][...].astype(BF16)
            self._keep(a).start()
            for cp in self._first(a):
                cp.start()

    def pass_on(self):
        for a in range(self.n):
            for j, chip in enumerate(self.chips):
                self._copy(a, 1 + j, (*chip, self.c), self.me).wait_recv()
                self._copy(a, 4 + j, (*chip, self.c), self.sibling).start()

    def finish(self):
        for a in range(self.n):
            self._copy(a, 0, self.sibling, self.me).wait_recv()
            for j, chip in enumerate(self.chips):
                self._copy(a, 4 + j, (*chip, 1 - self.c), self.me).wait_recv()
        for a in range(self.n):
            for cp in self._first(a):
                cp.wait_send()
            for j, chip in enumerate(self.chips):
                self._copy(a, 4 + j, (*chip, self.c), self.sibling).wait_send()
            self._keep(a).wait()


def _all_gather_weights(shards, axes):
    n = len(shards)

    def body(*refs):
        gather = _HalfGather(refs[0:n], refs[n:2 * n], axes, refs[2 * n:3 * n], refs[3 * n:4 * n], *refs[4 * n:])
        gather.send_mine()
        gather.pass_on()
        gather.finish()

    any_spec = pl.BlockSpec(memory_space=pl.ANY)
    return pl.pallas_call(
        body, name="weights_all_gather", out_shape=_HalfGather.out_shapes(shards, axes),
        in_specs=[any_spec] * n, out_specs=(any_spec,) * n, scratch_shapes=_HalfGather.scratch(shards, axes),
        compiler_params=pltpu.CompilerParams(vmem_limit_bytes=VMEM_LIMIT),
    )(*shards)


class _ShardReduce:
    SEMS = 8
    LOCAL = 5

    def __init__(self, g_refs, out_refs, axes, bufs, send_sems, recv_sems, local_sems):
        self.g_refs, self.out_refs, self.axes = g_refs, out_refs, axes
        self.recv_a, self.own_a, self.send_b, self.recv_b, self.fin = bufs
        self.send_sems, self.recv_sems, self.local_sems = send_sems, recv_sems, local_sems
        self.n = len(g_refs)
        x, y, self.c = _my_place()
        self.chip = 2 * x + y
        self.sibling = (x, y, 1 - self.c)

    @staticmethod
    def scratch(gparts, axes):
        dims = [_half_dims(g.shape[1:], axis) for g, axis in zip(gparts, axes)]
        shapes = []
        for dtype, lead in ((F32, (4,)), (F32, (4,)), (BF16, (4,)), (BF16, (4,)), (F32, ())):
            shapes += [pltpu.VMEM(lead + d, dtype) for d in dims]
        return shapes

    def _to_sibling(self, a, j):
        return pltpu.make_async_remote_copy(
            src_ref=_half_of(self.g_refs[a].at[j], self.axes[a], 1 - self.c), dst_ref=self.recv_a[a].at[j],
            send_sem=self.send_sems.at[self.SEMS * a + j], recv_sem=self.recv_sems.at[self.SEMS * a + j], device_id=self.sibling,
            device_id_type=MESH)

    def _own(self, a, j):
        return pltpu.make_async_copy(_half_of(self.g_refs[a].at[j], self.axes[a], self.c), self.own_a[a].at[j],
                                     self.local_sems.at[self.LOCAL * a + j])

    def _to_chip(self, a, k):
        dest = (self.chip + k) % 4
        return pltpu.make_async_remote_copy(
            src_ref=self.send_b[a].at[dest], dst_ref=self.recv_b[a].at[self.chip],
            send_sem=self.send_sems.at[self.SEMS * a + 3 + k], recv_sem=self.recv_sems.at[self.SEMS * a + 3 + k],
            device_id=(dest // 2, dest % 2, self.c), device_id_type=MESH)

    def _give(self, a):
        return pltpu.make_async_remote_copy(
            src_ref=self.fin[a], dst_ref=_half_of(self.out_refs[a], self.axes[a], self.c),
            send_sem=self.send_sems.at[self.SEMS * a + 7], recv_sem=self.recv_sems.at[self.SEMS * a + 7], device_id=self.sibling,
            device_id_type=MESH)

    def _mine(self, a):
        return pltpu.make_async_copy(self.fin[a], _half_of(self.out_refs[a], self.axes[a], self.c),
                                     self.local_sems.at[self.LOCAL * a])

    def exchange_with_sibling(self):
        for k in (1, 2, 3, 0):
            j = (self.chip + k) % 4
            for a in range(self.n):
                self._to_sibling(a, j).start()
                self._own(a, j).start()

    def _chip_partial(self, a, j):
        self._own(a, j).wait()
        self._to_sibling(a, j).wait_recv()
        self.send_b[a][j] = (self.own_a[a][j] + self.recv_a[a][j]).astype(BF16)

    def send_to_chip(self, k):
        for a in range(self.n):
            self._chip_partial(a, (self.chip + k) % 4)
            self._to_chip(a, k).start()

    def keep_mine(self):
        for a in range(self.n):
            self._chip_partial(a, self.chip)
            keep = pltpu.make_async_copy(self.send_b[a].at[self.chip], self.recv_b[a].at[self.chip],
                                         self.local_sems.at[self.LOCAL * a + 4])
            keep.start()
            keep.wait()

    def sum_and_share(self):
        for a in range(self.n):
            for k in range(1, 4):
                self._to_chip(a, k).wait_recv()
            tot = self.recv_b[a][0].astype(F32) + self.recv_b[a][1].astype(F32)
            tot = tot + self.recv_b[a][2].astype(F32)
            self.fin[a][...] = tot + self.recv_b[a][3].astype(F32)
            self._give(a).start()
            self._mine(a).start()

    def finish(self):
        for a in range(self.n):
            self._give(a).wait_recv()
            self._mine(a).wait()
            self._give(a).wait_send()
            for j in range(4):
                self._to_sibling(a, j).wait_send()
            for k in range(1, 4):
                self._to_chip(a, k).wait_send()


def _mem_tokens_fwd(mem_ref, g_ref, w_ref, kg_ref, mn_ref, kv_ref, kn_ref, vm_ref):
    xm = mem_ref[...]
    rr = lax.rsqrt(jnp.mean(xm * xm, axis=-1, keepdims=True) + EPS)
    mnb = ((xm * rr) * g_ref[...]).astype(BF16)
    mn_ref[...] = mnb
    kv = _dot(mnb, w_ref[...])
    kv_ref[...] = kv
    lo = _lane_lo((xm.shape[0], LANES))
    for p in range(MEM_WIDTH // LANES):
        sl = slice(p * LANES, (p + 1) * LANES)
        kb = kv[:, sl]
        kn_ref[:, sl] = ((kb * _head_rms(kb, lo)) * kg_ref[:, sl]).astype(BF16)
    vm_ref[...] = kv[:, MEM_WIDTH:].astype(BF16)


AUG_LO = 64
KEY_SUM_LANE = 72
QUERY_SUM_LANE = 80
HEAD_BLOCKS = FOX_HEADS * LANES


def _ones3(lane):
    return jnp.where((lane >= AUG_LO) & (lane < AUG_LO + 3), 1.0, 0.0)


def _spread3(cols):
    hi = cols.astype(BF16)
    rest = cols - hi.astype(F32)
    mid = rest.astype(BF16)
    low = (rest - mid.astype(F32)).astype(BF16)
    r = lax.broadcasted_iota(jnp.int32, (LANES, HEAD_BLOCKS), 0)
    c = lax.broadcasted_iota(jnp.int32, (LANES, HEAD_BLOCKS), 1)
    out = None
    for k, part in enumerate((hi, mid, low)):
        term = _dot(part, jnp.where(c == r * LANES + (AUG_LO + k), 1.0, 0.0).astype(BF16))
        out = term if out is None else out + term
    return out


def _head_block(pair_blk, hh, lo, extras):
    src = pair_blk if hh == 0 else pltpu.roll(pair_blk, HEAD_DIM, axis=1)
    return jnp.where(lo, src, extras).astype(BF16)


def _pair_block(blk0, blk1, lo):
    return jnp.where(lo, blk0, pltpu.roll(blk1, HEAD_DIM, axis=1))


def _fwd_in(x, norm_g, wp, bf_pad, fq_g, fk_g):
    s = x.shape[0]
    t = TILE
    n = s // t

    def body(x_ref, ng_ref, wp_ref, bf_ref, qg_ref, kg_ref,
             h_ref, pa_ref, qk_ref, qa_ref, ka_ref, va_ref, gb_ref, pm_ref, fb_ref, carry_ref, fcol_ref):
        @pl.when(pl.program_id(0) == 0)
        def _():
            carry_ref[...] = jnp.zeros_like(carry_ref)

        xv = x_ref[...]
        rr = lax.rsqrt(jnp.mean(xv * xv, axis=-1, keepdims=True) + EPS)
        hb = ((xv * rr) * ng_ref[...]).astype(BF16)
        h_ref[...] = hb

        def proj(lo, hi):
            return _dot(hb, wp_ref[lo:hi, :], NT)

        pa_ref[...] = proj(PA_LO, QB_LO)
        gb_ref[...] = proj(GB_LO, PM_LO)
        pm_ref[...] = proj(PM_LO, FB_LO)
        fb = proj(FB_LO, PROJ_PAD)
        fb_ref[...] = fb

        lane = lax.broadcasted_iota(jnp.int32, (t, LANES), 1)
        row = lax.broadcasted_iota(jnp.int32, (t, LANES), 0)
        lo = lane < HEAD_DIM
        z = fb + bf_ref[...]
        lf = -(jnp.maximum(-z, 0.0) + jnp.log1p(jnp.exp(-jnp.abs(z))))
        lf = jnp.where(lane < FOX_HEADS, lf, 0.0)
        sh = 1
        while sh < t:
            lf = lf + jnp.where(row >= sh, pltpu.roll(lf, sh, axis=0), 0.0)
            sh *= 2
        fcum = lf + carry_ref[...]
        fcol_ref[...] = fcum
        carry_ref[...] = fcol_ref[t - 1:t, :]

        ones3 = _ones3(lane)
        minus_f = _spread3(-fcum)
        for seg, g_ref, out_ref, scale in ((QB_LO, qg_ref, qa_ref, ATT_SCALE), (KB_LO, kg_ref, ka_ref, 1.0)):
            raw = proj(seg, seg + FOX_WIDTH)
            qk_ref[:, seg - QB_LO:seg - QB_LO + FOX_WIDTH] = raw
            for p in range(FOX_WIDTH // LANES):
                sl = slice(p * LANES, (p + 1) * LANES)
                blk = raw[:, sl]
                normed = ((blk * _head_rms(blk, lo)) * g_ref[:, sl]) * scale
                for hh in range(2):
                    h = 2 * p + hh
                    if seg == QB_LO:
                        extras = jnp.where(lane == QUERY_SUM_LANE + h, 1.0, ones3)
                    else:
                        extras = jnp.where(lane == KEY_SUM_LANE + h, 1.0, minus_f[:, h * LANES:(h + 1) * LANES])
                    out_ref[:, h * LANES:(h + 1) * LANES] = _head_block(normed, hh, lo, extras)
        vraw = proj(VB_LO, GB_LO)
        for h in range(FOX_HEADS):
            va_ref[:, h * LANES:(h + 1) * LANES] = _head_block(vraw[:, (h // 2) * LANES:(h // 2 + 1) * LANES], h % 2, lo, ones3)

    outs = (
        jax.ShapeDtypeStruct((s, D_MODEL), BF16),
        jax.ShapeDtypeStruct((s, 512), F32),
        jax.ShapeDtypeStruct((s, 2 * FOX_WIDTH), F32),
        jax.ShapeDtypeStruct((s, HEAD_BLOCKS), BF16),
        jax.ShapeDtypeStruct((s, HEAD_BLOCKS), BF16),
        jax.ShapeDtypeStruct((s, HEAD_BLOCKS), BF16),
        jax.ShapeDtypeStruct((s, FOX_WIDTH), F32),
        jax.ShapeDtypeStruct((s, 512), F32),
        jax.ShapeDtypeStruct((s, LANES), F32),
    )
    return pl.pallas_call(
        body, name="fwd_in", grid=(n,), out_shape=outs,
        in_specs=[_rows(t, D_MODEL), _full((1, D_MODEL)), _full((PROJ_PAD, D_MODEL)), _full((1, LANES)),
                  _full((1, FOX_WIDTH)), _full((1, FOX_WIDTH))],
        out_specs=(_rows(t, D_MODEL), _rows(t, 512), _rows(t, 2 * FOX_WIDTH), _rows(t, HEAD_BLOCKS),
                   _rows(t, HEAD_BLOCKS), _rows(t, HEAD_BLOCKS), _rows(t, FOX_WIDTH), _rows(t, 512),
                   _rows(t, LANES)),
        scratch_shapes=[pltpu.VMEM((1, LANES), F32), pltpu.VMEM((t, LANES), F32)],
        compiler_params=_params(),
    )(x, norm_g, wp, bf_pad, fq_g, fk_g)


POOL_HALO = 16


def _pool_window(lane):
    return jnp.where(lane < 64, 2.0, jnp.where(lane < 128, 4.0, jnp.where(lane < 192, 8.0, 16.0)))


def _pool_pick(lane, s2, s4, s8, s16):
    return jnp.where(lane < 64, s2, jnp.where(lane < 128, s4, jnp.where(lane < 192, s8, s16)))


def _group_onehot(shape, row_is_group_lane):
    r = lax.broadcasted_iota(jnp.int32, shape, 0)
    c = lax.broadcasted_iota(jnp.int32, shape, 1)
    hit = (r % HEAD_DIM == c) if row_is_group_lane else (c % HEAD_DIM == r)
    return jnp.where(hit, 1.0, 0.0).astype(F32)


def _same_group(shape):
    r = lax.broadcasted_iota(jnp.int32, shape, 0)
    c = lax.broadcasted_iota(jnp.int32, shape, 1)
    return (r // HEAD_DIM) == (c // HEAD_DIM)


def _pool_block_diag(w4):
    spread = jnp.dot(w4, _group_onehot((HEAD_DIM, POOL_WIDTH), False), preferred_element_type=F32,
                     precision=lax.Precision.HIGHEST)
    return jnp.where(_same_group((POOL_WIDTH, POOL_WIDTH)), spread, 0.0).astype(BF16)


def _mem_softmax(qm, kp):
    sc = _dot(qm, kp, NT)
    e = jnp.exp(sc - jnp.max(sc, axis=-1, keepdims=True))
    return e * (1.0 / jnp.sum(e, axis=-1, keepdims=True))


def _side_fwd(pa, pm, w4, pscale, mq_g, mem, mem_norm_g, w_kv, mk_g):
    s = pa.shape[0]
    t = TILE
    n = s // t
    ext = t + POOL_HALO
    nm = mem.shape[0]

    def body(pa_ref, pm_ref, w4_ref, sc_ref, g_ref, mem_ref, mg_ref, wkv_ref, kg_ref,
             ma_ref, d_ref, mm_ref, mn_ref, kv_ref, k_ref, v_ref, ext_ref, w_ref):
        i = pl.program_id(0)

        @pl.when(i == 0)
        def _():
            ext_ref[0:POOL_HALO, :] = jnp.zeros((POOL_HALO, POOL_WIDTH), F32)
            w_ref[...] = _pool_block_diag(w4_ref[...])
            _mem_tokens_fwd(mem_ref, mg_ref, wkv_ref, kg_ref, mn_ref, kv_ref, k_ref, v_ref)

        u = pa_ref[:, 0:POOL_WIDTH]
        ext_ref[POOL_HALO:ext, :] = u
        e = ext_ref[...]
        s2 = e + pltpu.roll(e, 1, axis=0)
        s4 = s2 + pltpu.roll(s2, 2, axis=0)
        s8 = s4 + pltpu.roll(s4, 4, axis=0)
        s16 = s8 + pltpu.roll(s8, 8, axis=0)
        lane_e = lax.broadcasted_iota(jnp.int32, (ext, POOL_WIDTH), 1)
        win = _pool_pick(lane_e, s2, s4, s8, s16)[POOL_HALO:ext, :]
        lane = lax.broadcasted_iota(jnp.int32, (t, POOL_WIDTH), 1)
        pos = (lax.broadcasted_iota(jnp.int32, (t, POOL_WIDTH), 0) + (i * t + 1)).astype(F32)
        d = win / jnp.minimum(pos, _pool_window(lane)) - u
        db = d.astype(BF16)
        d_ref[...] = db
        ya = _dot(db, w_ref[...]) * sc_ref[...]
        ga = pa_ref[:, POOL_WIDTH:2 * POOL_WIDTH]
        ma_ref[...] = (ya * (ga * _sig(ga))).astype(BF16)
        ext_ref[0:POOL_HALO, :] = ext_ref[t:ext, :]

        lo = _lane_lo((t, LANES))
        for p in range(MEM_WIDTH // LANES):
            sl = slice(p * LANES, (p + 1) * LANES)
            qb = pm_ref[:, sl]
            qs = (((qb * _head_rms(qb, lo)) * g_ref[:, sl]) * ATT_SCALE).astype(BF16)
            kp = k_ref[:, sl]
            vp = v_ref[:, sl]
            outs = []
            for hh in range(2):
                msk = lo if hh == 0 else jnp.logical_not(lo)
                prob = _mem_softmax(jnp.where(msk, qs, jnp.zeros_like(qs)), kp)
                outs.append(_dot(prob.astype(BF16), vp))
            o = jnp.where(lo, outs[0], outs[1])
            gm = pm_ref[:, MEM_WIDTH + p * LANES:MEM_WIDTH + (p + 1) * LANES]
            mm_ref[:, sl] = (o * (gm * _sig(gm))).astype(BF16)

    return pl.pallas_call(
        body, name="side_fwd", grid=(n,),
        out_shape=(jax.ShapeDtypeStruct((s, POOL_WIDTH), BF16), jax.ShapeDtypeStruct((s, POOL_WIDTH), BF16),
                   jax.ShapeDtypeStruct((s, MEM_WIDTH), BF16), jax.ShapeDtypeStruct((nm, D_MODEL), BF16),
                   jax.ShapeDtypeStruct((nm, 2 * MEM_WIDTH), F32), jax.ShapeDtypeStruct((nm, MEM_WIDTH), BF16),
                   jax.ShapeDtypeStruct((nm, MEM_WIDTH), BF16)),
        in_specs=[_rows(t, 512), _rows(t, 512), _full((POOL_ROWS, HEAD_DIM)), _full((1, POOL_WIDTH)),
                  _full((1, MEM_WIDTH)), _full((nm, D_MODEL)), _full((1, D_MODEL)), _full((D_MODEL, 2 * MEM_WIDTH)),
                  _full((1, MEM_WIDTH))],
        out_specs=(_rows(t, POOL_WIDTH), _rows(t, POOL_WIDTH), _rows(t, MEM_WIDTH), _full((nm, D_MODEL)),
                   _full((nm, 2 * MEM_WIDTH)), _full((nm, MEM_WIDTH)), _full((nm, MEM_WIDTH))),
        scratch_shapes=[pltpu.VMEM((ext, POOL_WIDTH), F32), pltpu.VMEM((POOL_WIDTH, POOL_WIDTH), BF16)],
        compiler_params=_params(),
    )(pa, pm, w4, pscale, mq_g, mem, mem_norm_g, w_kv, mk_g)


FOX_FWD_HEADS = 4


def _fox_fwd(qa, ka, va, gb):
    s = qa.shape[0]
    t = TILE
    n = s // t
    heads = FOX_FWD_HEADS
    pairs = heads // 2
    group_w = heads * LANES

    def body(qa_ref, ka_ref, va_ref, gb_ref, o_ref, mb_ref, r_ref):
        i = pl.program_id(1)
        lane = lax.broadcasted_iota(jnp.int32, (t, LANES), 1)
        lo = lane < HEAD_DIM
        causal = lax.broadcasted_iota(jnp.int32, (t, t), 1) <= lax.broadcasted_iota(jnp.int32, (t, t), 0)
        qas = [qa_ref[:, hh * LANES:(hh + 1) * LANES] for hh in range(heads)]

        def step(j, carry, masked):
            rows = pl.ds(pl.multiple_of(j * t, t), t)
            new = []
            for hh in range(heads):
                cols = slice(hh * LANES, (hh + 1) * LANES)
                m, acc = carry[hh]
                sc = _dot(qas[hh], ka_ref[rows, cols], NT)
                if masked:
                    sc = jnp.where(causal, sc, -1e30)
                m_new = jnp.maximum(m, jnp.max(sc, axis=-1, keepdims=True))
                acc = jnp.exp(m - m_new) * acc + _dot(jnp.exp(sc - m_new).astype(BF16), va_ref[rows, cols])
                new.append((m_new, acc))
            return tuple(new)

        init = (jnp.full((t, 1), -1e30, F32), jnp.zeros((t, LANES), F32))
        carry = lax.fori_loop(0, i, functools.partial(step, masked=False), (init,) * heads)
        res = step(i, carry, masked=True)
        for p in range(pairs):
            outs = []
            rcol = jnp.zeros((t, LANES), F32)
            for hh in range(2):
                m, acc = res[2 * p + hh]
                l = _lane_pick(acc, lane, AUG_LO)
                outs.append(acc * (1.0 / l))
                rcol = jnp.where(lane == hh, m + jnp.log(l), rcol)
            o = _pair_block(outs[0], outs[1], lo)
            sl = slice(p * LANES, (p + 1) * LANES)
            o_ref[:, sl] = o
            g = gb_ref[:, sl]
            mb_ref[:, sl] = (o * (g * _sig(g))).astype(BF16)
            r_ref[p] = rcol

    tile_spec = pl.BlockSpec((t, pairs * LANES), lambda p, i: (i, p))
    full_spec = pl.BlockSpec((s, group_w), lambda p, i: (0, p))
    return pl.pallas_call(
        body, name="fox_fwd", grid=(FOX_HEADS // heads, n),
        out_shape=(jax.ShapeDtypeStruct((s, FOX_WIDTH), F32), jax.ShapeDtypeStruct((s, FOX_WIDTH), BF16),
                   jax.ShapeDtypeStruct((FOX_HEADS // 2, s, LANES), F32)),
        in_specs=[pl.BlockSpec((t, group_w), lambda p, i: (i, p)), full_spec, full_spec, tile_spec],
        out_specs=(tile_spec, tile_spec, pl.BlockSpec((pairs, t, LANES), lambda p, i: (p, i, 0))),
        compiler_params=_params(2),
    )(qa, ka, va, gb)


def _out_loss(x, tgt, ma, mb, mm, wout, gb, o, r4):
    s = x.shape[0]
    t = TILE
    n = s // t
    pairs = FOX_HEADS // 2

    def body(x_ref, t_ref, ma_ref, mb_ref, mm_ref, w_ref, gb_ref, o_ref, r_ref,
             dy_ref, dma_ref, dmm_ref, dw_ref, loss_ref, doa_ref, dgb_ref, rr_ref, mix_ref):
        @pl.when(pl.program_id(0) == 0)
        def _():
            dw_ref[...] = jnp.zeros_like(dw_ref)
            loss_ref[...] = jnp.zeros_like(loss_ref)

        mix_ref[:, 0:256] = ma_ref[...]
        mix_ref[:, 256:768] = mb_ref[...]
        mix_ref[:, 768:1024] = mm_ref[...]
        mix = mix_ref[...]
        err = (x_ref[...] + _dot(mix, w_ref[...])) - t_ref[...]
        row_mean = jnp.sum(err * err, axis=-1, keepdims=True) * (1.0 / D_MODEL)
        loss_ref[...] += 0.5 * jnp.sum(row_mean, axis=0, keepdims=True)
        dy = err * (1.0 / D_MODEL)
        dy_ref[...] = dy
        dyb = dy.astype(BF16)
        dmix = _dot(dyb, w_ref[...], NT)
        dma_ref[...] = dmix[:, 0:256]
        dmm_ref[...] = dmix[:, 768:1024]
        dw_ref[...] += _dot(mix, dyb, TN)

        lane = lax.broadcasted_iota(jnp.int32, (t, LANES), 1)
        lo = lane < HEAD_DIM
        d_os = []
        delta = jnp.zeros((t, LANES), F32)
        for p in range(pairs):
            sl = slice(p * LANES, (p + 1) * LANES)
            g = gb_ref[:, sl]
            sg = _sig(g)
            dm = dmix[:, 256 + p * LANES:256 + (p + 1) * LANES]
            ov = o_ref[:, sl]
            d_o = dm * (g * sg)
            d_os.append(d_o)
            dgb_ref[:, sl] = (dm * ov * (sg * (1.0 + g * (1.0 - sg)))).astype(BF16)
            prod = d_o * ov
            delta = jnp.where(lane == 2 * p, jnp.sum(jnp.where(lo, prod, 0.0), axis=-1, keepdims=True), delta)
            delta = jnp.where(lane == 2 * p + 1, jnp.sum(jnp.where(lo, 0.0, prod), axis=-1, keepdims=True), delta)
            rr_ref[p, 0] = r_ref[p].T[0:8, :]
        minus_delta = _spread3(-delta)
        for h in range(FOX_HEADS):
            blk = slice(h * LANES, (h + 1) * LANES)
            doa_ref[:, blk] = _head_block(d_os[h // 2], h % 2, lo, minus_delta[:, blk])

    return pl.pallas_call(
        body, name="out_loss", grid=(n,),
        out_shape=(jax.ShapeDtypeStruct((s, D_MODEL), F32), jax.ShapeDtypeStruct((s, 256), F32),
                   jax.ShapeDtypeStruct((s, 256), F32), jax.ShapeDtypeStruct((D_MODEL, D_MODEL), F32),
                   jax.ShapeDtypeStruct((1, LANES), F32), jax.ShapeDtypeStruct((s, HEAD_BLOCKS), BF16),
                   jax.ShapeDtypeStruct((s, FOX_WIDTH), BF16), jax.ShapeDtypeStruct((pairs, n, 8, t), F32)),
        in_specs=[_rows(t, D_MODEL), _rows(t, D_MODEL), _rows(t, 256), _rows(t, 512), _rows(t, 256),
                  _full((D_MODEL, D_MODEL)), _rows(t, FOX_WIDTH), _rows(t, FOX_WIDTH),
                  pl.BlockSpec((pairs, t, LANES), lambda i: (0, i, 0))],
        out_specs=(_rows(t, D_MODEL), _rows(t, 256), _rows(t, 256), _full((D_MODEL, D_MODEL)), _full((1, LANES)),
                   _rows(t, HEAD_BLOCKS), _rows(t, FOX_WIDTH), pl.BlockSpec((pairs, 1, 8, t), lambda i: (0, i, 0, 0))),
        scratch_shapes=[pltpu.VMEM((t, D_MODEL), BF16)],
        compiler_params=_params(),
    )(x, tgt, ma, mb, mm, wout, gb, o, r4)


def _side_bwd(pa, db, dma, w4, pscale, pm, dmm, kmn, vmb, mq_g, kv, mnb, mem, w_kv, mk_g, mem_norm_g):
    s = pa.shape[0]
    t = TILE
    n = s // t
    ext = t + POOL_HALO
    nm = mem.shape[0]

    def body(pa_ref, d_ref, dma_ref, w4_ref, sc_ref, pm_ref, dmm_ref, k_ref, v_ref, g_ref,
             kv_ref, mn_ref, mem_ref, wkv_ref, kg_ref, mg_ref,
             dpa_ref, dpm_ref, dw4_ref, dsc_ref, dg_ref, dwkv_ref, dmg_ref, dkg_ref,
             ext_ref, w_ref, dw_ref, dk_ref, dv_ref, gacc_ref, dkv_ref):
        i = pl.program_id(0)

        @pl.when(i == 0)
        def _():
            dw_ref[...] = jnp.zeros_like(dw_ref)
            dsc_ref[...] = jnp.zeros_like(dsc_ref)
            ext_ref[t:ext, :] = jnp.zeros((POOL_HALO, POOL_WIDTH), F32)
            w_ref[...] = _pool_block_diag(w4_ref[...])
            dk_ref[...] = jnp.zeros_like(dk_ref)
            dv_ref[...] = jnp.zeros_like(dv_ref)
            gacc_ref[...] = jnp.zeros_like(gacc_ref)

        dbv = d_ref[...]
        z = _dot(dbv, w_ref[...])
        ga = pa_ref[:, POOL_WIDTH:2 * POOL_WIDTH]
        sg = _sig(ga)
        dma_v = dma_ref[...]
        dya = dma_v * (ga * sg)
        dpa_ref[:, POOL_WIDTH:2 * POOL_WIDTH] = (dma_v * (z * sc_ref[...]) * (sg * (1.0 + ga * (1.0 - sg)))).astype(BF16)
        dsc_ref[...] += jnp.sum(dya * z, axis=0, keepdims=True)
        dzb = (dya * sc_ref[...]).astype(BF16)
        dw_ref[...] += _dot(dbv, dzb, TN)
        dd = _dot(dzb, w_ref[...], NT)
        lane = lax.broadcasted_iota(jnp.int32, (t, POOL_WIDTH), 1)
        pos = (lax.broadcasted_iota(jnp.int32, (t, POOL_WIDTH), 0) + ((n - 1 - i) * t + 1)).astype(F32)
        ext_ref[0:t, :] = dd / jnp.minimum(pos, _pool_window(lane))
        e = ext_ref[...]
        s2 = e + pltpu.roll(e, ext - 1, axis=0)
        s4 = s2 + pltpu.roll(s2, ext - 2, axis=0)
        s8 = s4 + pltpu.roll(s4, ext - 4, axis=0)
        s16 = s8 + pltpu.roll(s8, ext - 8, axis=0)
        lane_e = lax.broadcasted_iota(jnp.int32, (ext, POOL_WIDTH), 1)
        win = _pool_pick(lane_e, s2, s4, s8, s16)[0:t, :]
        dpa_ref[:, 0:POOL_WIDTH] = (win - dd).astype(BF16)
        ext_ref[t:ext, :] = ext_ref[0:POOL_HALO, :]

        lo = _lane_lo((t, LANES))
        for p in range(MEM_WIDTH // LANES):
            sl = slice(p * LANES, (p + 1) * LANES)
            qb = pm_ref[:, sl]
            rr = _head_rms(qb, lo)
            qhat = qb * rr
            g = g_ref[:, sl]
            qs = ((qhat * g) * ATT_SCALE).astype(BF16)
            gm = pm_ref[:, MEM_WIDTH + p * LANES:MEM_WIDTH + (p + 1) * LANES]
            sg = _sig(gm)
            dmo = dmm_ref[:, sl]
            d_o = dmo * (gm * sg)
            kp = k_ref[:, sl]
            vp = v_ref[:, sl]
            outs, dqs = [], []
            for hh in range(2):
                msk = lo if hh == 0 else jnp.logical_not(lo)
                qm = jnp.where(msk, qs, jnp.zeros_like(qs))
                prob = _mem_softmax(qm, kp)
                pb = prob.astype(BF16)
                outs.append(_dot(pb, vp))
                dom = jnp.where(msk, d_o, 0.0).astype(BF16)
                dp = _dot(dom, vp, NT)
                ds = (prob * (dp - jnp.sum(prob * dp, axis=-1, keepdims=True))).astype(BF16)
                dqs.append(_dot(ds, kp))
                dk_ref[:, sl] += _dot(ds, qm, TN)
                dv_ref[:, sl] += _dot(pb, dom, TN)
            o = jnp.where(lo, outs[0], outs[1])
            dqn = jnp.where(lo, dqs[0], dqs[1]) * ATT_SCALE
            dpm_ref[:, sl] = _head_norm_bwd(dqn, qhat, rr, g, lo).astype(BF16)
            dpm_ref[:, MEM_WIDTH + p * LANES:MEM_WIDTH + (p + 1) * LANES] = (
                dmo * o * (sg * (1.0 + gm * (1.0 - sg)))).astype(BF16)
            gacc_ref[:, sl] += jnp.sum(dqn * qhat, axis=0, keepdims=True)

        @pl.when(i == n - 1)
        def _():
            own = jnp.where(_same_group((POOL_WIDTH, POOL_WIDTH)), dw_ref[...], 0.0)
            dw4_ref[...] = jnp.dot(own, _group_onehot((POOL_WIDTH, HEAD_DIM), True), preferred_element_type=F32,
                                   precision=lax.Precision.HIGHEST)
            dg_ref[...] = _fold_heads(gacc_ref[...])

            lo_m = _lane_lo((nm, LANES))
            kacc = []
            for p in range(MEM_WIDTH // LANES):
                sl = slice(p * LANES, (p + 1) * LANES)
                kb = kv_ref[:, sl]
                rr = _head_rms(kb, lo_m)
                khat = kb * rr
                dk = dk_ref[:, sl]
                dkv_ref[:, sl] = _head_norm_bwd(dk, khat, rr, kg_ref[:, sl], lo_m).astype(BF16)
                kacc.append(jnp.sum(dk * khat, axis=0, keepdims=True))
            dkg_ref[...] = _fold_heads(jnp.concatenate(kacc, axis=1))
            dkv_ref[:, MEM_WIDTH:] = dv_ref[...].astype(BF16)
            dkv = dkv_ref[...]
            dwkv_ref[...] = _dot(mn_ref[...], dkv, TN)
            dmn = _dot(dkv, wkv_ref[...], NT)
            xm = mem_ref[...]
            rr = lax.rsqrt(jnp.mean(xm * xm, axis=-1, keepdims=True) + EPS)
            dmg_ref[...] = jnp.sum(dmn * (xm * rr), axis=0, keepdims=True)

    def rev(w):
        return _rows_rev(t, w, n)

    row = jax.ShapeDtypeStruct((1, LANES), F32)
    return pl.pallas_call(
        body, name="side_bwd", grid=(n,),
        out_shape=(jax.ShapeDtypeStruct((s, 512), BF16), jax.ShapeDtypeStruct((s, 512), BF16),
                   jax.ShapeDtypeStruct((POOL_ROWS, HEAD_DIM), F32), jax.ShapeDtypeStruct((1, POOL_WIDTH), F32), row,
                   jax.ShapeDtypeStruct((D_MODEL, 2 * MEM_WIDTH), F32), jax.ShapeDtypeStruct((1, D_MODEL), F32), row),
        in_specs=[rev(512), rev(POOL_WIDTH), rev(POOL_WIDTH), _full((POOL_ROWS, HEAD_DIM)), _full((1, POOL_WIDTH)),
                  rev(512), rev(MEM_WIDTH), _full((N_MEM, MEM_WIDTH)), _full((N_MEM, MEM_WIDTH)), _full((1, MEM_WIDTH)),
                  _full((nm, 2 * MEM_WIDTH)), _full((nm, D_MODEL)), _full((nm, D_MODEL)),
                  _full((D_MODEL, 2 * MEM_WIDTH)), _full((1, MEM_WIDTH)), _full((1, D_MODEL))],
        out_specs=(rev(512), rev(512), _full((POOL_ROWS, HEAD_DIM)), _full((1, POOL_WIDTH)), _full((1, LANES)),
                   _full((D_MODEL, 2 * MEM_WIDTH)), _full((1, D_MODEL)), _full((1, LANES))),
        scratch_shapes=[pltpu.VMEM((ext, POOL_WIDTH), F32), pltpu.VMEM((POOL_WIDTH, POOL_WIDTH), BF16),
                        pltpu.VMEM((POOL_WIDTH, POOL_WIDTH), F32), pltpu.VMEM((N_MEM, MEM_WIDTH), F32),
                        pltpu.VMEM((N_MEM, MEM_WIDTH), F32), pltpu.VMEM((1, MEM_WIDTH), F32),
                        pltpu.VMEM((nm, 2 * MEM_WIDTH), BF16)],
        compiler_params=_params(),
    )(pa, db, dma, w4, pscale, pm, dmm, kmn, vmb, mq_g, kv, mnb, mem, w_kv, mk_g, mem_norm_g)


FOX_BWD_HEADS = 4


def _fox_bwd(ka, va, qa, doa, rr):
    s = ka.shape[0]
    t = TILE
    n = s // t
    heads = FOX_BWD_HEADS
    group_w = heads * LANES

    def body(ka_ref, va_ref, qa_ref, doa_ref, rr_ref, dka_ref, dva_ref, dqa_ref):
        j = pl.program_id(1)

        @pl.when(j == 0)
        def _():
            dqa_ref[...] = jnp.zeros_like(dqa_ref)

        causal = lax.broadcasted_iota(jnp.int32, (t, t), 0) <= lax.broadcasted_iota(jnp.int32, (t, t), 1)
        kas = [ka_ref[:, hh * LANES:(hh + 1) * LANES] for hh in range(heads)]
        vas = [va_ref[:, hh * LANES:(hh + 1) * LANES] for hh in range(heads)]

        def step(i, carry, masked):
            rows = pl.ds(pl.multiple_of(i * t, t), t)
            new = []
            for hh in range(heads):
                cols = slice(hh * LANES, (hh + 1) * LANES)
                dk_a, dv_a = carry[hh]
                qb = qa_ref[rows, cols]
                d_o = doa_ref[rows, cols]
                arg = _dot(kas[hh], qb, NT) - rr_ref[hh // 2, i, hh % 2:hh % 2 + 1, :]
                if masked:
                    arg = jnp.where(causal, arg, -1e30)
                pt = jnp.exp(arg)
                dst = (pt * _dot(vas[hh], d_o, NT)).astype(BF16)
                dv_a = dv_a + _dot(pt.astype(BF16), d_o)
                dk_a = dk_a + _dot(dst, qb)
                dqa_ref[rows, cols] += _dot(dst, kas[hh], TN)
                new.append((dk_a, dv_a))
            return tuple(new)

        zero = jnp.zeros((t, LANES), F32)
        carry = step(j, ((zero, zero),) * heads, masked=True)
        res = lax.fori_loop(j + 1, n, functools.partial(step, masked=False), carry)
        for hh in range(heads):
            cols = slice(hh * LANES, (hh + 1) * LANES)
            dka_ref[:, cols] = res[hh][0]
            dva_ref[:, cols] = res[hh][1]

    tile_spec = pl.BlockSpec((t, group_w), lambda p, j: (j, p))
    full_spec = pl.BlockSpec((s, group_w), lambda p, j: (0, p))
    return pl.pallas_call(
        body, name="fox_bwd", grid=(FOX_HEADS // heads, n),
        out_shape=(jax.ShapeDtypeStruct((s, HEAD_BLOCKS), F32),) * 3,
        in_specs=[tile_spec, tile_spec, full_spec, full_spec,
                  pl.BlockSpec((heads // 2, n, 8, t), lambda p, j: (p, 0, 0, 0))],
        out_specs=(tile_spec, tile_spec, full_spec),
        compiler_params=_params(2),
    )(ka, va, qa, doa, rr)


def _fox_post_tile(i, n, t, dqa_ref, dka_ref, dva_ref, qk_ref, fb_ref, bf_ref, qg_ref, kg_ref,
                   dqk_ref, dv_ref, dfb_ref, dqg_ref, dkg_ref, dbf_ref, qacc_ref, kacc_ref, carry_ref,
                   between):
    @pl.when(i == 0)
    def _():
        qacc_ref[...] = jnp.zeros_like(qacc_ref)
        kacc_ref[...] = jnp.zeros_like(kacc_ref)
        dbf_ref[...] = jnp.zeros_like(dbf_ref)
        carry_ref[...] = jnp.zeros_like(carry_ref)

    lane = lax.broadcasted_iota(jnp.int32, (t, LANES), 1)
    row = lax.broadcasted_iota(jnp.int32, (t, LANES), 0)
    lo = lane < HEAD_DIM

    def head_blocks(ref, p):
        return ref[:, 2 * p * LANES:(2 * p + 1) * LANES], ref[:, (2 * p + 1) * LANES:(2 * p + 2) * LANES]

    def issue(k):
        if between[k] is not None:
            between[k]()

    sums = []
    pairs = FOX_WIDTH // LANES
    for side, (src_ref, g_ref, acc_ref, scale) in enumerate(((dqa_ref, qg_ref, qacc_ref, ATT_SCALE),
                                                             (dka_ref, kg_ref, kacc_ref, 1.0))):
        total = jnp.zeros((t, LANES), F32)
        for p in range(pairs):
            issue(side * pairs + p)
            sl = slice(p * LANES, (p + 1) * LANES)
            cols = slice(side * FOX_WIDTH + p * LANES, side * FOX_WIDTH + (p + 1) * LANES)
            if side == 0:
                dv_ref[:, sl] = _pair_block(*head_blocks(dva_ref, p), lo).astype(BF16)
            d0, d1 = head_blocks(src_ref, p)
            total = total + (d0 + d1)
            raw = qk_ref[:, cols]
            rr = _head_rms(raw, lo)
            xhat = raw * rr
            dn = _pair_block(d0, d1, lo) * scale
            dqk_ref[:, cols] = _head_norm_bwd(dn, xhat, rr, g_ref[:, sl], lo).astype(BF16)
            acc_ref[:, sl] += jnp.sum(dn * xhat, axis=0, keepdims=True)
        sums.append(total)
    issue(2 * pairs)
    dq_sum, dk_sum = sums

    acc = (pltpu.roll(dq_sum, LANES - KEY_SUM_LANE, axis=1) - pltpu.roll(dk_sum, LANES - QUERY_SUM_LANE, axis=1))
    acc = jnp.where(lane < FOX_HEADS, acc, 0.0)
    sh = 1
    while sh < t:
        acc = acc + jnp.where(row < t - sh, pltpu.roll(acc, t - sh, axis=0), 0.0)
        sh *= 2
    dlogf = acc + carry_ref[...]
    dfb_ref[...] = dlogf
    carry_ref[...] = dfb_ref[0:1, :]
    z = fb_ref[...] + bf_ref[...]
    dz = jnp.where(lane < FOX_HEADS, dlogf * (1.0 / (1.0 + jnp.exp(z))), 0.0)
    dfb_ref[...] = dz
    dbf_ref[...] += jnp.sum(dz, axis=0, keepdims=True)

    @pl.when(i == n - 1)
    def _():
        dqg_ref[...] = _fold_heads(qacc_ref[...])
        dkg_ref[...] = _fold_heads(kacc_ref[...])


def _assemble_dproj(dp_ref, dpa_ref, dqk_ref, dv_ref, dgb_ref, dpm_ref, dfb_ref):
    dp_ref[:, PA_LO:QB_LO] = dpa_ref[...]
    dp_ref[:, QB_LO:VB_LO] = dqk_ref[...]
    dp_ref[:, VB_LO:GB_LO] = dv_ref[...]
    dp_ref[:, GB_LO:PM_LO] = dgb_ref[...]
    dp_ref[:, PM_LO:FB_LO] = dpm_ref[...]
    dp_ref[:, FB_LO:PROJ_PAD] = dfb_ref[...].astype(BF16)


def _dproj_specs(t):
    return [_rows(t, 512), _rows(t, 2 * FOX_WIDTH), _rows(t, FOX_WIDTH), _rows(t, FOX_WIDTH), _rows(t, 512),
            _rows(t, LANES)]


IN_BWD_X_TILE = 256


def _in_bwd_x(x, dy, norm_g, wp, dparts, gparts, axes, smalls):
    s = x.shape[0]
    t = IN_BWD_X_TILE
    n = s // t
    na = len(gparts)
    n_dp = len(dparts)
    vec_leaves, loss_row, dw4 = smalls if smalls is not None else ((), None, None)
    nv = len(vec_leaves)
    n_small = nv + 2 if smalls is not None else 0
    small_base = _ShardReduce.SEMS * na

    def body(*refs):
        x_ref, dy_ref, g_ref, wp_ref = refs[0:4]
        dp_parts = refs[4:4 + n_dp]
        o = 4 + n_dp
        g_refs = refs[o:o + na]
        small_in = refs[o + na:o + na + n_small]
        o += na + n_small
        gx_ref, dg_ref = refs[o:o + 2]
        out_refs = refs[o + 2:o + 2 + na]
        small_out = refs[o + 2 + na:o + 2 + na + (2 if smalls is not None else 0)]
        o += 2 + na + len(small_out)
        dp_ref = refs[o]
        bufs = tuple(refs[o + 1 + k * na:o + 1 + (k + 1) * na] for k in range(5))
        rest = refs[o + 1 + 5 * na:]

        i = pl.program_id(0)
        if na or smalls is not None:
            send_sems, recv_sems, local_sems = rest[-3:]
        red = _ShardReduce(g_refs, out_refs, axes, bufs, send_sems, recv_sems, local_sems) if na else None

        @pl.when(i == 0)
        def _():
            dg_ref[...] = jnp.zeros_like(dg_ref)
            if red is not None:
                red.exchange_with_sibling()

        if red is not None:
            for k in (1, 2, 3):
                pl.when(i == k)(functools.partial(red.send_to_chip, k))
            pl.when(i == 4)(red.keep_mine)

        _assemble_dproj(dp_ref, *dp_parts)
        dh = _dot(dp_ref[...], wp_ref[...])
        xv = x_ref[...]
        rr = lax.rsqrt(jnp.mean(xv * xv, axis=-1, keepdims=True) + EPS)
        xhat = xv * rr
        scaled = dh * g_ref[...]
        gx_ref[...] = dy_ref[...] + rr * (scaled - xhat * jnp.mean(xhat * scaled, axis=-1, keepdims=True))
        dg_ref[...] += jnp.sum(dh * xhat, axis=0, keepdims=True)

        def small_all_reduce():
            leaf_refs, (loss_ref, dw4_ref) = small_in[0:nv], small_in[nv:]
            vec_out, dw4_out = small_out
            vec_mine, vec_recv, dw4_recv = rest[0:3]
            cx, cy, c = _my_place()
            me_lin = 4 * cx + 2 * cy + c

            def copy(k, src, dst, base):
                peer = (me_lin + k) % 8
                return pltpu.make_async_remote_copy(
                    src_ref=src, dst_ref=dst.at[me_lin], send_sem=send_sems.at[base + k - 1],
                    recv_sem=recv_sems.at[base + k - 1], device_id=(peer // 4, (peer // 2) % 2, peer % 2),
                    device_id_type=MESH)

            vec_mine[...] = jnp.zeros_like(vec_mine)
            vec_mine[0:1, :] = dg_ref[...]
            for (_, row, _), ref in zip(VEC_LEAVES[1:], leaf_refs):
                vec_mine[row:row + 1, 0:ref.shape[1]] = ref[...]
            vec_mine[VEC_LOSS_ROW:VEC_LOSS_ROW + 1, 0:LANES] = loss_ref[...]
            copies = [copy(k, src, dst, base) for k in range(1, 8)
                      for src, dst, base in ((vec_mine, vec_recv, small_base), (dw4_ref, dw4_recv, small_base + 7))]
            for cp in copies:
                cp.start()
            for cp in copies:
                cp.wait_recv()
            vec_recv[me_lin] = vec_mine[...]
            dw4_recv[me_lin] = dw4_ref[...]
            vtot, wtot = vec_recv[0], dw4_recv[0]
            for d in range(1, 8):
                vtot = vtot + vec_recv[d]
                wtot = wtot + dw4_recv[d]
            vec_out[...] = vtot
            dw4_out[...] = wtot
            for cp in copies:
                cp.wait_send()

        @pl.when(i == n - 1)
        def _():
            if red is not None:
                red.sum_and_share()
            if smalls is not None:
                small_all_reduce()
            if red is not None:
                red.finish()

    any_spec = pl.BlockSpec(memory_space=pl.ANY)
    scratch = [pltpu.VMEM((t, PROJ_PAD), BF16)] + _ShardReduce.scratch(gparts, axes)
    out_shape = [jax.ShapeDtypeStruct((s, D_MODEL), F32), jax.ShapeDtypeStruct((1, D_MODEL), F32)]
    out_shape += [jax.ShapeDtypeStruct(g.shape[1:], F32) for g in gparts]
    out_specs = [_rows(t, D_MODEL), _full((1, D_MODEL))] + [any_spec] * na
    small_args = []
    if smalls is not None:
        small_args = [*vec_leaves, loss_row, dw4]
        out_shape += [jax.ShapeDtypeStruct((VEC_ROWS, D_MODEL), F32), jax.ShapeDtypeStruct(dw4.shape, F32)]
        out_specs += [_full((VEC_ROWS, D_MODEL)), _full(dw4.shape)]
        scratch += [pltpu.VMEM((VEC_ROWS, D_MODEL), F32), pltpu.VMEM((8, VEC_ROWS, D_MODEL), F32),
                    pltpu.VMEM((8,) + dw4.shape, F32)]
    if na or smalls is not None:
        n_sems = small_base + 14
        scratch += [pltpu.SemaphoreType.DMA((n_sems,)), pltpu.SemaphoreType.DMA((n_sems,)),
                    pltpu.SemaphoreType.DMA((max(_ShardReduce.LOCAL * na, 1),))]
    return pl.pallas_call(
        body, name="in_bwd_x", grid=(n,), out_shape=tuple(out_shape),
        in_specs=[_rows(t, D_MODEL), _rows(t, D_MODEL), _full((1, D_MODEL)),
                  pl.BlockSpec((PROJ_PAD, D_MODEL), lambda i: (0, 0), pipeline_mode=pl.Buffered(1))]
        + _dproj_specs(t) + [any_spec] * na + [_full(a.shape) for a in small_args],
        out_specs=tuple(out_specs), scratch_shapes=scratch, compiler_params=_params(),
    )(x, dy, norm_g, wp, *dparts, *gparts, *small_args)


def _in_bwd_w(hb, dpa, dgb, dpm, fox, gparts, axes):
    s = hb.shape[0]
    t = TILE
    n = s // t
    na = len(gparts)
    f_hi = F_ORIG_LO + FOX_HEADS
    n_in = 4 + len(fox)

    def body(*refs):
        h_ref, dpa_ref, dgb_ref, dpm_ref = refs[0:4]
        fox_refs = refs[4:n_in]
        g_refs = refs[n_in:n_in + na]
        o = n_in + na
        dw_ref, dqk_ref, dv_ref, dfb_ref, dqg_ref, dkg_ref, dbf_ref = refs[o:o + 7]
        out_refs = refs[o + 7:o + 7 + na]
        o += 7 + na
        fox_scratch = refs[o:o + 3]
        bufs = tuple(refs[o + 3 + k * na:o + 3 + (k + 1) * na] for k in range(5))
        i = pl.program_id(0)
        red = _ShardReduce(g_refs, out_refs, axes, bufs, *refs[o + 3 + 5 * na:]) if na else None

        @pl.when(i == 0)
        def _():
            dw_ref[...] = jnp.zeros_like(dw_ref)
            if red is not None:
                red.exchange_with_sibling()

        if red is not None:
            @pl.when(i == 1)
            def _():
                for k in (1, 2, 3):
                    red.send_to_chip(k)
                red.keep_mine()

        hv = h_ref[...]

        def rows_of(lo, ref, cols=slice(None)):
            def add():
                dproj = ref[:, cols]
                dw_ref[lo:lo + dproj.shape[1], :] += _dot(dproj, hv, TN)
            return add

        q_cols, k_cols = slice(0, FOX_WIDTH), slice(FOX_WIDTH, 2 * FOX_WIDTH)
        between = (rows_of(0, dpa_ref), rows_of(f_hi, dgb_ref), rows_of(f_hi + FOX_WIDTH, dpm_ref), None,
                   rows_of(QB_LO, dqk_ref, q_cols), rows_of(VB_LO, dv_ref), None, None, rows_of(KB_LO, dqk_ref, k_cols))
        _fox_post_tile(i, n, t, *fox_refs, dqk_ref, dv_ref, dfb_ref, dqg_ref, dkg_ref, dbf_ref, *fox_scratch, between)
        dw_ref[F_ORIG_LO:f_hi, :] += _dot(dfb_ref[...].astype(BF16), hv, TN)[0:FOX_HEADS, :]

        if red is not None:
            @pl.when(i == n - 1)
            def _():
                red.sum_and_share()
                red.finish()

    def rev(w):
        return _rows_rev(t, w, n)

    any_spec = pl.BlockSpec(memory_space=pl.ANY)
    row = jax.ShapeDtypeStruct((1, LANES), F32)
    scratch = [pltpu.VMEM((1, FOX_WIDTH), F32), pltpu.VMEM((1, FOX_WIDTH), F32), pltpu.VMEM((1, LANES), F32)]
    scratch += _ShardReduce.scratch(gparts, axes)
    if na:
        scratch += [pltpu.SemaphoreType.DMA((_ShardReduce.SEMS * na,)), pltpu.SemaphoreType.DMA((_ShardReduce.SEMS * na,)),
                    pltpu.SemaphoreType.DMA((_ShardReduce.LOCAL * na,))]
    return pl.pallas_call(
        body, name="in_bwd_w", grid=(n,),
        out_shape=(jax.ShapeDtypeStruct((IN_WIDTH, D_MODEL), F32), jax.ShapeDtypeStruct((s, 2 * FOX_WIDTH), BF16),
                   jax.ShapeDtypeStruct((s, FOX_WIDTH), BF16), jax.ShapeDtypeStruct((s, LANES), F32), row, row, row)
        + tuple(jax.ShapeDtypeStruct(g.shape[1:], F32) for g in gparts),
        in_specs=[rev(D_MODEL), rev(512), rev(FOX_WIDTH), rev(512), rev(HEAD_BLOCKS), rev(HEAD_BLOCKS),
                  rev(HEAD_BLOCKS), rev(2 * FOX_WIDTH), rev(LANES), _full((1, LANES)), _full((1, FOX_WIDTH)),
                  _full((1, FOX_WIDTH))] + [any_spec] * na,
        out_specs=(pl.BlockSpec((IN_WIDTH, D_MODEL), lambda i: (0, 0), pipeline_mode=pl.Buffered(1)),
                   rev(2 * FOX_WIDTH), rev(FOX_WIDTH), rev(LANES), _full((1, LANES)), _full((1, LANES)),
                   _full((1, LANES))) + (any_spec,) * na,
        scratch_shapes=scratch, compiler_params=_params(),
    )(hb, dpa, dgb, dpm, *fox, *gparts)


def _adamw_math(w_ref, gv, m_ref, v_ref, d_ref, nm_ref, nv_ref):
    nm = ADAM_B1 * m_ref[...] + (1.0 - ADAM_B1) * gv
    nv = ADAM_B2 * v_ref[...] + (1.0 - ADAM_B2) * (gv * gv)
    m_hat = nm / (1.0 - ADAM_B1 ** ADAM_STEP)
    v_hat = nv / (1.0 - ADAM_B2 ** ADAM_STEP)
    d_ref[...] = -ADAM_LR * (m_hat / (jnp.sqrt(v_hat) + ADAM_EPS) + ADAM_WD * w_ref[...])
    nm_ref[...] = nm
    nv_ref[...] = nv


def _adamw(name, w, g, m, v):
    rows, cols = w.shape
    tc = 256 if rows * cols > 256 * 1024 else cols
    n = cols // tc

    def body(w_ref, g_ref, m_ref, v_ref, d_ref, nm_ref, nv_ref):
        _adamw_math(w_ref, g_ref[...], m_ref, v_ref, d_ref, nm_ref, nv_ref)

    spec = pl.BlockSpec((rows, tc), lambda i: (0, i))
    return pl.pallas_call(
        body, name=name, grid=(n,),
        out_shape=(jax.ShapeDtypeStruct((rows, cols), F32),) * 3,
        in_specs=[spec] * 4, out_specs=(spec,) * 3,
        compiler_params=_params(),
    )(w, g, m, v)


def _adamw_rest(vec, dw4, leaves, pool, shards):
    nl = len(VEC_LEAVES) + 1
    ns = len(shards)

    def body(*refs):
        vec_ref, dw4_ref = refs[0:2]
        wmv = refs[2:2 + 3 * nl]
        shard_in = refs[2 + 3 * nl:2 + 3 * nl + 4 * ns]
        o = 2 + 3 * nl + 4 * ns
        loss_ref = refs[o]
        outs = refs[o + 1:o + 1 + 4 * nl]
        shard_out = refs[o + 1 + 4 * nl:]
        loss_ref[...] = vec_ref[VEC_LOSS_ROW:VEC_LOSS_ROW + 1, 0:1]
        for k in range(nl):
            if k < nl - 1:
                _, row, width = VEC_LEAVES[k]
                gv = vec_ref[row:row + 1, 0:width]
            else:
                gv = dw4_ref[...]
            w_ref, m_ref, v_ref = wmv[3 * k:3 * k + 3]
            g_ref, d_ref, nm_ref, nv_ref = outs[4 * k:4 * k + 4]
            g_ref[...] = gv
            _adamw_math(w_ref, gv, m_ref, v_ref, d_ref, nm_ref, nv_ref)
        for k in range(ns):
            w_ref, g_ref, m_ref, v_ref = shard_in[4 * k:4 * k + 4]
            _adamw_math(w_ref, g_ref[...], m_ref, v_ref, *shard_out[3 * k:3 * k + 3])

    shapes = [jax.ShapeDtypeStruct((1, width), F32) for _, _, width in VEC_LEAVES] + [
        jax.ShapeDtypeStruct(dw4.shape, F32)]
    flat_in = [a for triple in list(leaves) + [pool] for a in triple] + [a for quad in shards for a in quad]
    res = pl.pallas_call(
        body, name="adamw_rest",
        out_shape=(jax.ShapeDtypeStruct((1, 1), F32),) + tuple(s for s in shapes for _ in range(4))
        + tuple(jax.ShapeDtypeStruct(quad[0].shape, F32) for quad in shards for _ in range(3)),
        compiler_params=pltpu.CompilerParams(vmem_limit_bytes=VMEM_LIMIT),
    )(vec, dw4, *flat_in)
    per = [res[1 + 4 * k:5 + 4 * k] for k in range(nl)]
    big = res[1 + 4 * nl:]
    return (res[0], [p[0] for p in per], [p[1] for p in per], [p[2] for p in per], [p[3] for p in per],
            [big[3 * k:3 * k + 3] for k in range(ns)])


def _full_w_in_padded(halves):
    cols = IN_WIDTH // 4
    w_t = halves.reshape(4, 2, cols, D_MODEL // 2).transpose(0, 2, 1, 3).reshape(IN_WIDTH, D_MODEL)
    return jnp.concatenate([
        w_t[0:F_ORIG_LO], w_t[F_ORIG_LO + FOX_HEADS:], w_t[F_ORIG_LO:F_ORIG_LO + FOX_HEADS],
        jnp.zeros((PROJ_PAD - IN_WIDTH, D_MODEL), w_t.dtype)], axis=0)


def _tile_heads(g, n):
    return jnp.tile(g.reshape(1, HEAD_DIM), (1, n))


def kernel(x, mem, norm_g, w_in, b_f, w_pool, pool_scale, fox_q_g, fox_k_g, mem_norm_g, w_mem_kv, mem_q_g, mem_k_g, w_out, loss_target, m_norm_g, m_w_in, m_b_f, m_w_pool, m_pool_scale, m_fox_q_g, m_fox_k_g, m_mem_norm_g, m_w_mem_kv, m_mem_q_g, m_mem_k_g, m_w_out, v_norm_g, v_w_in, v_b_f, v_w_pool, v_pool_scale, v_fox_q_g, v_fox_k_g, v_mem_norm_g, v_w_mem_kv, v_mem_q_g, v_mem_k_g, v_w_out):
    w_in_t, m_w_in_t, v_w_in_t = w_in[0].T, m_w_in[0].T, v_w_in[0].T
    axes = (1, 0, 0)

    g_in, g_kv, g_out = _all_gather_weights([w_in_t, w_mem_kv[0], w_out[0]], axes)
    wp = _full_w_in_padded(g_in)
    tiled = _tiled_params(b_f, fox_q_g, fox_k_g, mem_q_g, mem_k_g)
    fwd = _fwd_in(x[0], norm_g, wp, *tiled[0:3])
    w_kv_b = g_kv.reshape(D_MODEL, 2 * MEM_WIDTH)
    w_out_b = g_out.reshape(D_MODEL, D_MODEL)
    w4 = w_pool.reshape(POOL_ROWS, HEAD_DIM)
    dy, hb, dpa, dgb, dpm, fox, dw_kv, dw_out, (dmemnorm_g, dpscale, dmq_g, dmk_g), loss_row, dw4 = _local_partials(
        x[0], mem[0], loss_target[0], fwd, w_kv_b, w_out_b, tiled, w4, pool_scale, mem_norm_g)

    early = [dw_kv.reshape(4, D_MODEL // 4, 2 * MEM_WIDTH), dw_out.reshape(4, D_MODEL // 4, D_MODEL)]
    dwp, dqk, dvb, dfb, dfq_g, dfk_g, dbf, g_w_kv, g_w_out = _in_bwd_w(hb, dpa, dgb, dpm, fox, early, axes[1:])
    dparts = (dpa, dqk, dvb, dgb, dpm, dfb)
    vec_leaves = (dmemnorm_g, dpscale, dbf, dfq_g, dfk_g, dmq_g, dmk_g)
    grad_x, _, g_w_in_t, vec, dw4_sum = _in_bwd_x(
        x[0], dy, norm_g, wp, dparts, [dwp.reshape(4, IN_WIDTH // 4, D_MODEL)], axes[0:1], (vec_leaves, loss_row, dw4))

    small_wmv = [(norm_g, m_norm_g, v_norm_g), (mem_norm_g, m_mem_norm_g, v_mem_norm_g),
                 (pool_scale, m_pool_scale, v_pool_scale), (b_f, m_b_f, v_b_f), (fox_q_g, m_fox_q_g, v_fox_q_g),
                 (fox_k_g, m_fox_k_g, v_fox_k_g), (mem_q_g, m_mem_q_g, v_mem_q_g), (mem_k_g, m_mem_k_g, v_mem_k_g)]
    pool_wmv = tuple(a.reshape(POOL_ROWS, HEAD_DIM) for a in (w_pool, m_w_pool, v_w_pool))
    loss, *small_out, (upd_kv, upd_out) = _adamw_rest(
        vec, dw4_sum, small_wmv, pool_wmv, [(w_mem_kv[0], g_w_kv, m_w_mem_kv[0], v_w_mem_kv[0]),
                                             (w_out[0], g_w_out, m_w_out[0], v_w_out[0])])
    big = [[g_w_in_t.T[None], g_w_kv[None], g_w_out[None]]]
    upd = [[a.T for a in _adamw("adamw_w_in", w_in_t, g_w_in_t, m_w_in_t, v_w_in_t)], upd_kv, upd_out]
    big += [[u[k][None] for u in upd] for k in range(3)]

    def leaves(k):
        sm = small_out[k]
        b_in, b_kv, b_out = big[k]
        return (sm[0], b_in, sm[3], sm[8].reshape(w_pool.shape), sm[2], sm[4], sm[5], sm[1], b_kv, sm[6], sm[7], b_out)

    return (loss.reshape(()), grad_x[None], *leaves(0), *leaves(1), *leaves(2), *leaves(3))


def _tiled_params(b_f, fox_q_g, fox_k_g, mem_q_g, mem_k_g):
    return (jnp.pad(b_f, ((0, 0), (0, LANES - FOX_HEADS))), _tile_heads(fox_q_g, FOX_HEADS),
            _tile_heads(fox_k_g, FOX_HEADS), _tile_heads(mem_q_g, 4), _tile_heads(mem_k_g, 4))


def _local_partials(xs, mems, tgt, fwd, w_kv_b, w_out_b, tiled, w4, pool_scale, mem_norm_g):
    hb, pa, qk, qa, ka, va, gb, pm, fb = fwd
    bf_pad, fq_g, fk_g, mq_g, mk_g = tiled

    ma, db, mm, mnb, kv, kmn, vmb = _side_fwd(pa, pm, w4, pool_scale, mq_g, mems, mem_norm_g, w_kv_b, mk_g)
    o, mb, r4 = _fox_fwd(qa, ka, va, gb)
    dy, dma, dmm, dw_out, loss_row, doa, dgb, rr = _out_loss(xs, tgt, ma, mb, mm, w_out_b, gb, o, r4)

    dpa, dpm, dw4, dpscale, dmq_g, dw_kv, dmemnorm_g, dmk_g = _side_bwd(
        pa, db, dma, w4, pool_scale, pm, dmm, kmn, vmb, mq_g, kv, mnb, mems, w_kv_b, mk_g, mem_norm_g)
    dka, dva, dqa = _fox_bwd(ka, va, qa, doa, rr)
    fox = (dqa, dka, dva, qk, fb, bf_pad, fq_g, fk_g)
    return dy, hb, dpa, dgb, dpm, fox, dw_kv, dw_out, (dmemnorm_g, dpscale, dmq_g, dmk_g), loss_row, dw4
```

```python
import functools

import jax
import jax.numpy as jnp
from jax import lax
from jax.experimental import pallas as pl
from jax.experimental.pallas import tpu as pltpu

F32 = jnp.float32
BF16 = jnp.bfloat16
MESH = pl.DeviceIdType.MESH

D_MODEL = 1024
HEAD_DIM = 64
POOL_WIDTH = 256
FOX_WIDTH = 512
FOX_HEADS = 8
MEM_WIDTH = 256
N_MEM = 256
IN_WIDTH = 3080
EPS = 1e-6
ATT_SCALE = 0.125

ADAM_LR = 0.001
ADAM_B1 = 0.9
ADAM_B2 = 0.999
ADAM_EPS = 1e-08
ADAM_WD = 0.01
ADAM_STEP = 10

LANES = 128
PA_LO, QB_LO, KB_LO, VB_LO, GB_LO, PM_LO, FB_LO, PROJ_PAD = 0, 512, 1024, 1536, 2048, 2560, 3072, 3200
F_ORIG_LO = 2048

TILE = 512
VMEM_LIMIT = 56 * 1024 * 1024

VEC_LEAVES = (("norm_g", 0, 1024), ("mem_norm_g", 1, 1024), ("pool_scale", 2, 256), ("b_f", 3, 8),
              ("fox_q_g", 4, 64), ("fox_k_g", 5, 64), ("mem_q_g", 6, 64), ("mem_k_g", 7, 64))
VEC_LOSS_ROW = 8
VEC_ROWS = 16
POOL_ROWS = 256


def _params(n_grid=1, vmem=VMEM_LIMIT):
    return pltpu.CompilerParams(dimension_semantics=("arbitrary",) * n_grid, vmem_limit_bytes=vmem)


def _rows(t, w):
    return pl.BlockSpec((t, w), lambda i: (i, 0))


def _rows_rev(t, w, n):
    return pl.BlockSpec((t, w), lambda i: (n - 1 - i, 0))


def _full(shape):
    return pl.BlockSpec(shape, lambda i: (0,) * len(shape))


def _sig(x):
    return 1.0 / (1.0 + jnp.exp(-x))


def _lane_lo(shape):
    return lax.broadcasted_iota(jnp.int32, shape, 1) < HEAD_DIM


def _pair_sum(v, lo):
    s0 = jnp.sum(jnp.where(lo, v, 0.0), axis=-1, keepdims=True)
    s1 = jnp.sum(jnp.where(lo, 0.0, v), axis=-1, keepdims=True)
    return jnp.where(lo, s0, s1)


def _head_rms(blk, lo):
    return lax.rsqrt(_pair_sum(blk * blk, lo) * (1.0 / HEAD_DIM) + EPS)


def _head_norm_bwd(dyn, xhat, rr, g, lo):
    a = dyn * g
    return rr * (a - xhat * (_pair_sum(xhat * a, lo) * (1.0 / HEAD_DIM)))


def _fold_heads(acc):
    tot = acc[:, 0:LANES]
    for p in range(1, acc.shape[1] // LANES):
        tot = tot + acc[:, p * LANES:(p + 1) * LANES]
    return tot + pltpu.roll(tot, HEAD_DIM, axis=1)


def _lane_pick(v, lane, idx):
    return jnp.sum(jnp.where(lane == idx, v, 0.0), axis=-1, keepdims=True)


NT = (((1,), (1,)), ((), ()))
TN = (((0,), (0,)), ((), ()))


def _dot(a, b, dims=None):
    if dims is None:
        return jnp.dot(a, b, preferred_element_type=F32)
    return lax.dot_general(a, b, dims, preferred_element_type=F32)


def _my_place():
    return lax.axis_index("x"), lax.axis_index("y"), lax.axis_index("c")


def _half_dims(shape, axis):
    return (shape[0] // 2, shape[1]) if axis == 0 else (shape[0], shape[1] // 2)


def _half_of(ref, axis, core, lead=False):
    rows, cols = ref.shape[-2:]
    if axis == 0:
        idx = (pl.ds(pl.multiple_of(core * (rows // 2), 16), rows // 2), slice(None))
    else:
        idx = (slice(None), pl.ds(pl.multiple_of(core * (cols // 2), LANES), cols // 2))
    return ref.at[(slice(None),) + idx] if lead else ref.at[idx]


class _HalfGather:
    def __init__(self, ins, outs, axes, f32_bufs, bf_bufs, send_sems, recv_sems, local_sems):
        self.ins, self.outs, self.axes = ins, outs, axes
        self.f32_bufs, self.bf_bufs = f32_bufs, bf_bufs
        self.send_sems, self.recv_sems, self.local_sems = send_sems, recv_sems, local_sems
        self.n = len(ins)
        x, y, self.c = _my_place()
        self.me, self.sibling = (x, y, self.c), (x, y, 1 - self.c)
        self.chips = [(1 - x, y), (x, 1 - y), (1 - x, 1 - y)]

    @staticmethod
    def scratch(shards, axes):
        dims = [_half_dims(a.shape, axis) for a, axis in zip(shards, axes)]
        n = len(shards)
        return [pltpu.VMEM(d, F32) for d in dims] + [pltpu.VMEM(d, BF16) for d in dims] + [
            pltpu.SemaphoreType.DMA((7 * n,)), pltpu.SemaphoreType.DMA((7 * n,)), pltpu.SemaphoreType.DMA((2 * n,))]

    @staticmethod
    def out_shapes(shards, axes):
        return tuple(jax.ShapeDtypeStruct((8,) + _half_dims(a.shape, axis), BF16) for a, axis in zip(shards, axes))

    def _blk(self, a, px, py, pc):
        return self.outs[a].at[4 * px + 2 * py + pc]

    def _copy(self, a, k, block, to, src=None):
        return pltpu.make_async_remote_copy(
            src_ref=self._blk(a, *block) if src is None else src, dst_ref=self._blk(a, *block),
            send_sem=self.send_sems.at[7 * a + k], recv_sem=self.recv_sems.at[7 * a + k], device_id=to,
            device_id_type=MESH)

    def _keep(self, a):
        return pltpu.make_async_copy(self.bf_bufs[a], self._blk(a, *self.me), self.local_sems.at[self.n + a])

    def _first(self, a):
        mine = [self._copy(a, 0, self.me, self.sibling, src=self.bf_bufs[a])]
        return mine + [self._copy(a, 1 + j, self.me, (*chip, self.c), src=self.bf_bufs[a])
                       for j, chip in enumerate(self.chips)]

    def send_mine(self):
        loads = [pltpu.make_async_copy(_half_of(self.ins[a], self.axes[a], self.c), self.f32_bufs[a],
                                       self.local_sems.at[a]) for a in range(self.n)]
        for cp in loads:
            cp.start()
        for a in range(self.n):
            loads[a].wait()
            self.bf_bufs[a][...] = self.f32_bufs[a][...].astype(BF16)
            self._keep(a).start()
            for cp in self._first(a):
                cp.start()

    def pass_on(self):
        for a in range(self.n):
            for j, chip in enumerate(self.chips):
                self._copy(a, 1 + j, (*chip, self.c), self.me).wait_recv()
                self._copy(a, 4 + j, (*chip, self.c), self.sibling).start()

    def finish(self):
        for a in range(self.n):
            self._copy(a, 0, self.sibling, self.me).wait_recv()
            for j, chip in enumerate(self.chips):
                self._copy(a, 4 + j, (*chip, 1 - self.c), self.me).wait_recv()
        for a in range(self.n):
            for cp in self._first(a):
                cp.wait_send()
            for j, chip in enumerate(self.chips):
                self._copy(a, 4 + j, (*chip, self.c), self.sibling).wait_send()
            self._keep(a).wait()


def _all_gather_weights(shards, axes):
    n = len(shards)

    def body(*refs):
        gather = _HalfGather(refs[0:n], refs[n:2 * n], axes, refs[2 * n:3 * n], refs[3 * n:4 * n], *refs[4 * n:])
        gather.send_mine()
        gather.pass_on()
        gather.finish()

    any_spec = pl.BlockSpec(memory_space=pl.ANY)
    return pl.pallas_call(
        body, name="weights_all_gather", out_shape=_HalfGather.out_shapes(shards, axes),
        in_specs=[any_spec] * n, out_specs=(any_spec,) * n, scratch_shapes=_HalfGather.scratch(shards, axes),
        compiler_params=pltpu.CompilerParams(vmem_limit_bytes=VMEM_LIMIT),
    )(*shards)


class _ShardReduce:
    SEMS = 8
    LOCAL = 5

    def __init__(self, g_refs, out_refs, axes, bufs, send_sems, recv_sems, local_sems):
        self.g_refs, self.out_refs, self.axes = g_refs, out_refs, axes
        self.recv_a, self.own_a, self.send_b, self.recv_b, self.fin = bufs
        self.send_sems, self.recv_sems, self.local_sems = send_sems, recv_sems, local_sems
        self.n = len(g_refs)
        x, y, self.c = _my_place()
        self.chip = 2 * x + y
        self.sibling = (x, y, 1 - self.c)

    @staticmethod
    def scratch(gparts, axes):
        dims = [_half_dims(g.shape[1:], axis) for g, axis in zip(gparts, axes)]
        shapes = []
        for dtype, lead in ((F32, (4,)), (F32, (4,)), (BF16, (4,)), (BF16, (4,)), (F32, ())):
            shapes += [pltpu.VMEM(lead + d, dtype) for d in dims]
        return shapes

    def _to_sibling(self, a, j):
        return pltpu.make_async_remote_copy(
            src_ref=_half_of(self.g_refs[a].at[j], self.axes[a], 1 - self.c), dst_ref=self.recv_a[a].at[j],
            send_sem=self.send_sems.at[self.SEMS * a + j], recv_sem=self.recv_sems.at[self.SEMS * a + j], device_id=self.sibling,
            device_id_type=MESH)

    def _own(self, a, j):
        return pltpu.make_async_copy(_half_of(self.g_refs[a].at[j], self.axes[a], self.c), self.own_a[a].at[j],
                                     self.local_sems.at[self.LOCAL * a + j])

    def _to_chip(self, a, k):
        dest = (self.chip + k) % 4
        return pltpu.make_async_remote_copy(
            src_ref=self.send_b[a].at[dest], dst_ref=self.recv_b[a].at[self.chip],
            send_sem=self.send_sems.at[self.SEMS * a + 3 + k], recv_sem=self.recv_sems.at[self.SEMS * a + 3 + k],
            device_id=(dest // 2, dest % 2, self.c), device_id_type=MESH)

    def _give(self, a):
        return pltpu.make_async_remote_copy(
            src_ref=self.fin[a], dst_ref=_half_of(self.out_refs[a], self.axes[a], self.c),
            send_sem=self.send_sems.at[self.SEMS * a + 7], recv_sem=self.recv_sems.at[self.SEMS * a + 7], device_id=self.sibling,
            device_id_type=MESH)

    def _mine(self, a):
        return pltpu.make_async_copy(self.fin[a], _half_of(self.out_refs[a], self.axes[a], self.c),
                                     self.local_sems.at[self.LOCAL * a])

    def exchange_with_sibling(self):
        for k in (1, 2, 3, 0):
            j = (self.chip + k) % 4
            for a in range(self.n):
                self._to_sibling(a, j).start()
                self._own(a, j).start()

    def _chip_partial(self, a, j):
        self._own(a, j).wait()
        self._to_sibling(a, j).wait_recv()
        self.send_b[a][j] = (self.own_a[a][j] + self.recv_a[a][j]).astype(BF16)

    def send_to_chip(self, k):
        for a in range(self.n):
            self._chip_partial(a, (self.chip + k) % 4)
            self._to_chip(a, k).start()

    def keep_mine(self):
        for a in range(self.n):
            self._chip_partial(a, self.chip)
            keep = pltpu.make_async_copy(self.send_b[a].at[self.chip], self.recv_b[a].at[self.chip],
                                         self.local_sems.at[self.LOCAL * a + 4])
            keep.start()
            keep.wait()

    def sum_and_share(self):
        for a in range(self.n):
            for k in range(1, 4):
                self._to_chip(a, k).wait_recv()
            tot = self.recv_b[a][0].astype(F32) + self.recv_b[a][1].astype(F32)
            tot = tot + self.recv_b[a][2].astype(F32)
            self.fin[a][...] = tot + self.recv_b[a][3].astype(F32)
            self._give(a).start()
            self._mine(a).start()

    def finish(self):
        for a in range(self.n):
            self._give(a).wait_recv()
            self._mine(a).wait()
            self._give(a).wait_send()
            for j in range(4):
                self._to_sibling(a, j).wait_send()
            for k in range(1, 4):
                self._to_chip(a, k).wait_send()


def _mem_tokens_fwd(mem_ref, g_ref, w_ref, kg_ref, mn_ref, kv_ref, kn_ref, vm_ref):
    xm = mem_ref[...]
    rr = lax.rsqrt(jnp.mean(xm * xm, axis=-1, keepdims=True) + EPS)
    mnb = ((xm * rr) * g_ref[...]).astype(BF16)
    mn_ref[...] = mnb
    kv = _dot(mnb, w_ref[...])
    kv_ref[...] = kv
    lo = _lane_lo((xm.shape[0], LANES))
    for p in range(MEM_WIDTH // LANES):
        sl = slice(p * LANES, (p + 1) * LANES)
        kb = kv[:, sl]
        kn_ref[:, sl] = ((kb * _head_rms(kb, lo)) * kg_ref[:, sl]).astype(BF16)
    vm_ref[...] = kv[:, MEM_WIDTH:].astype(BF16)


AUG_LO = 64
KEY_SUM_LANE = 72
QUERY_SUM_LANE = 80
HEAD_BLOCKS = FOX_HEADS * LANES


def _ones3(lane):
    return jnp.where((lane >= AUG_LO) & (lane < AUG_LO + 3), 1.0, 0.0)


def _spread3(cols):
    hi = cols.astype(BF16)
    rest = cols - hi.astype(F32)
    mid = rest.astype(BF16)
    low = (rest - mid.astype(F32)).astype(BF16)
    r = lax.broadcasted_iota(jnp.int32, (LANES, HEAD_BLOCKS), 0)
    c = lax.broadcasted_iota(jnp.int32, (LANES, HEAD_BLOCKS), 1)
    out = None
    for k, part in enumerate((hi, mid, low)):
        term = _dot(part, jnp.where(c == r * LANES + (AUG_LO + k), 1.0, 0.0).astype(BF16))
        out = term if out is None else out + term
    return out


def _head_block(pair_blk, hh, lo, extras):
    src = pair_blk if hh == 0 else pltpu.roll(pair_blk, HEAD_DIM, axis=1)
    return jnp.where(lo, src, extras).astype(BF16)


def _pair_block(blk0, blk1, lo):
    return jnp.where(lo, blk0, pltpu.roll(blk1, HEAD_DIM, axis=1))


def _fwd_in(x, norm_g, wp, bf_pad, fq_g, fk_g):
    s = x.shape[0]
    t = TILE
    n = s // t

    def body(x_ref, ng_ref, wp_ref, bf_ref, qg_ref, kg_ref,
             h_ref, pa_ref, qk_ref, qa_ref, ka_ref, va_ref, gb_ref, pm_ref, fb_ref, carry_ref, fcol_ref):
        @pl.when(pl.program_id(0) == 0)
        def _():
            carry_ref[...] = jnp.zeros_like(carry_ref)

        xv = x_ref[...]
        rr = lax.rsqrt(jnp.mean(xv * xv, axis=-1, keepdims=True) + EPS)
        hb = ((xv * rr) * ng_ref[...]).astype(BF16)
        h_ref[...] = hb

        def proj(lo, hi):
            return _dot(hb, wp_ref[lo:hi, :], NT)

        fb = proj(FB_LO, PROJ_PAD)
        fb_ref[...] = fb
        qk_ref[:, 0:FOX_WIDTH] = proj(QB_LO, KB_LO)

        lane = lax.broadcasted_iota(jnp.int32, (t, LANES), 1)
        row = lax.broadcasted_iota(jnp.int32, (t, LANES), 0)
        lo = lane < HEAD_DIM
        z = fb + bf_ref[...]
        lf = -(jnp.maximum(-z, 0.0) + jnp.log1p(jnp.exp(-jnp.abs(z))))
        lf = jnp.where(lane < FOX_HEADS, lf, 0.0)
        sh = 1
        while sh < t:
            lf = lf + jnp.where(row >= sh, pltpu.roll(lf, sh, axis=0), 0.0)
            sh *= 2
        fcum = lf + carry_ref[...]
        fcol_ref[...] = fcum
        carry_ref[...] = fcol_ref[t - 1:t, :]

        ones3 = _ones3(lane)
        minus_f = _spread3(-fcum)

        def head_blocks(seg, g_ref, out_ref, scale):
            for p in range(FOX_WIDTH // LANES):
                sl = slice(p * LANES, (p + 1) * LANES)
                blk = qk_ref[:, seg - QB_LO + p * LANES:seg - QB_LO + (p + 1) * LANES]
                normed = ((blk * _head_rms(blk, lo)) * g_ref[:, sl]) * scale
                for hh in range(2):
                    h = 2 * p + hh
                    if seg == QB_LO:
                        extras = jnp.where(lane == QUERY_SUM_LANE + h, 1.0, ones3)
                    else:
                        extras = jnp.where(lane == KEY_SUM_LANE + h, 1.0, minus_f[:, h * LANES:(h + 1) * LANES])
                    out_ref[:, h * LANES:(h + 1) * LANES] = _head_block(normed, hh, lo, extras)

        qk_ref[:, FOX_WIDTH:2 * FOX_WIDTH] = proj(KB_LO, VB_LO)
        pa_ref[...] = proj(PA_LO, QB_LO)
        head_blocks(QB_LO, qg_ref, qa_ref, ATT_SCALE)
        vraw = proj(VB_LO, GB_LO)
        gb_ref[...] = proj(GB_LO, PM_LO)
        head_blocks(KB_LO, kg_ref, ka_ref, 1.0)
        pm_ref[...] = proj(PM_LO, FB_LO)
        for h in range(FOX_HEADS):
            va_ref[:, h * LANES:(h + 1) * LANES] = _head_block(vraw[:, (h // 2) * LANES:(h // 2 + 1) * LANES], h % 2, lo, ones3)

    outs = (
        jax.ShapeDtypeStruct((s, D_MODEL), BF16),
        jax.ShapeDtypeStruct((s, 512), F32),
        jax.ShapeDtypeStruct((s, 2 * FOX_WIDTH), F32),
        jax.ShapeDtypeStruct((s, HEAD_BLOCKS), BF16),
        jax.ShapeDtypeStruct((s, HEAD_BLOCKS), BF16),
        jax.ShapeDtypeStruct((s, HEAD_BLOCKS), BF16),
        jax.ShapeDtypeStruct((s, FOX_WIDTH), F32),
        jax.ShapeDtypeStruct((s, 512), F32),
        jax.ShapeDtypeStruct((s, LANES), F32),
    )
    return pl.pallas_call(
        body, name="fwd_in", grid=(n,), out_shape=outs,
        in_specs=[_rows(t, D_MODEL), _full((1, D_MODEL)), _full((PROJ_PAD, D_MODEL)), _full((1, LANES)),
                  _full((1, FOX_WIDTH)), _full((1, FOX_WIDTH))],
        out_specs=(_rows(t, D_MODEL), _rows(t, 512), _rows(t, 2 * FOX_WIDTH), _rows(t, HEAD_BLOCKS),
                   _rows(t, HEAD_BLOCKS), _rows(t, HEAD_BLOCKS), _rows(t, FOX_WIDTH), _rows(t, 512),
                   _rows(t, LANES)),
        scratch_shapes=[pltpu.VMEM((1, LANES), F32), pltpu.VMEM((t, LANES), F32)],
        compiler_params=_params(),
    )(x, norm_g, wp, bf_pad, fq_g, fk_g)


POOL_HALO = 16


def _pool_window(lane):
    return jnp.where(lane < 64, 2.0, jnp.where(lane < 128, 4.0, jnp.where(lane < 192, 8.0, 16.0)))


def _pool_pick(lane, s2, s4, s8, s16):
    return jnp.where(lane < 64, s2, jnp.where(lane < 128, s4, jnp.where(lane < 192, s8, s16)))


def _group_onehot(shape, row_is_group_lane):
    r = lax.broadcasted_iota(jnp.int32, shape, 0)
    c = lax.broadcasted_iota(jnp.int32, shape, 1)
    hit = (r % HEAD_DIM == c) if row_is_group_lane else (c % HEAD_DIM == r)
    return jnp.where(hit, 1.0, 0.0).astype(F32)


def _same_group(shape):
    r = lax.broadcasted_iota(jnp.int32, shape, 0)
    c = lax.broadcasted_iota(jnp.int32, shape, 1)
    return (r // HEAD_DIM) == (c // HEAD_DIM)


def _pool_block_diag(w4):
    spread = jnp.dot(w4, _group_onehot((HEAD_DIM, POOL_WIDTH), False), preferred_element_type=F32,
                     precision=lax.Precision.HIGHEST)
    return jnp.where(_same_group((POOL_WIDTH, POOL_WIDTH)), spread, 0.0).astype(BF16)


def _mem_softmax(qm, kp):
    sc = _dot(qm, kp, NT)
    e = jnp.exp(sc - jnp.max(sc, axis=-1, keepdims=True))
    return e * (1.0 / jnp.sum(e, axis=-1, keepdims=True))


def _side_fwd(pa, pm, w4, pscale, mq_g, mem, mem_norm_g, w_kv, mk_g):
    s = pa.shape[0]
    t = TILE
    n = s // t
    ext = t + POOL_HALO
    nm = mem.shape[0]

    def body(pa_ref, pm_ref, w4_ref, sc_ref, g_ref, mem_ref, mg_ref, wkv_ref, kg_ref,
             ma_ref, d_ref, mm_ref, mn_ref, kv_ref, k_ref, v_ref, ext_ref, w_ref):
        i = pl.program_id(0)

        @pl.when(i == 0)
        def _():
            ext_ref[0:POOL_HALO, :] = jnp.zeros((POOL_HALO, POOL_WIDTH), F32)
            w_ref[...] = _pool_block_diag(w4_ref[...])
            _mem_tokens_fwd(mem_ref, mg_ref, wkv_ref, kg_ref, mn_ref, kv_ref, k_ref, v_ref)

        u = pa_ref[:, 0:POOL_WIDTH]
        ext_ref[POOL_HALO:ext, :] = u
        e = ext_ref[...]
        s2 = e + pltpu.roll(e, 1, axis=0)
        s4 = s2 + pltpu.roll(s2, 2, axis=0)
        s8 = s4 + pltpu.roll(s4, 4, axis=0)
        s16 = s8 + pltpu.roll(s8, 8, axis=0)
        lane_e = lax.broadcasted_iota(jnp.int32, (ext, POOL_WIDTH), 1)
        win = _pool_pick(lane_e, s2, s4, s8, s16)[POOL_HALO:ext, :]
        lane = lax.broadcasted_iota(jnp.int32, (t, POOL_WIDTH), 1)
        pos = (lax.broadcasted_iota(jnp.int32, (t, POOL_WIDTH), 0) + (i * t + 1)).astype(F32)
        d = win / jnp.minimum(pos, _pool_window(lane)) - u
        db = d.astype(BF16)
        d_ref[...] = db
        ya = _dot(db, w_ref[...]) * sc_ref[...]
        ga = pa_ref[:, POOL_WIDTH:2 * POOL_WIDTH]
        ma_ref[...] = (ya * (ga * _sig(ga))).astype(BF16)
        ext_ref[0:POOL_HALO, :] = ext_ref[t:ext, :]

        lo = _lane_lo((t, LANES))
        for p in range(MEM_WIDTH // LANES):
            sl = slice(p * LANES, (p + 1) * LANES)
            qb = pm_ref[:, sl]
            qs = (((qb * _head_rms(qb, lo)) * g_ref[:, sl]) * ATT_SCALE).astype(BF16)
            kp = k_ref[:, sl]
            vp = v_ref[:, sl]
            outs = []
            for hh in range(2):
                msk = lo if hh == 0 else jnp.logical_not(lo)
                prob = _mem_softmax(jnp.where(msk, qs, jnp.zeros_like(qs)), kp)
                outs.append(_dot(prob.astype(BF16), vp))
            o = jnp.where(lo, outs[0], outs[1])
            gm = pm_ref[:, MEM_WIDTH + p * LANES:MEM_WIDTH + (p + 1) * LANES]
            mm_ref[:, sl] = (o * (gm * _sig(gm))).astype(BF16)

    return pl.pallas_call(
        body, name="side_fwd", grid=(n,),
        out_shape=(jax.ShapeDtypeStruct((s, POOL_WIDTH), BF16), jax.ShapeDtypeStruct((s, POOL_WIDTH), BF16),
                   jax.ShapeDtypeStruct((s, MEM_WIDTH), BF16), jax.ShapeDtypeStruct((nm, D_MODEL), BF16),
                   jax.ShapeDtypeStruct((nm, 2 * MEM_WIDTH), F32), jax.ShapeDtypeStruct((nm, MEM_WIDTH), BF16),
                   jax.ShapeDtypeStruct((nm, MEM_WIDTH), BF16)),
        in_specs=[_rows(t, 512), _rows(t, 512), _full((POOL_ROWS, HEAD_DIM)), _full((1, POOL_WIDTH)),
                  _full((1, MEM_WIDTH)), _full((nm, D_MODEL)), _full((1, D_MODEL)), _full((D_MODEL, 2 * MEM_WIDTH)),
                  _full((1, MEM_WIDTH))],
        out_specs=(_rows(t, POOL_WIDTH), _rows(t, POOL_WIDTH), _rows(t, MEM_WIDTH), _full((nm, D_MODEL)),
                   _full((nm, 2 * MEM_WIDTH)), _full((nm, MEM_WIDTH)), _full((nm, MEM_WIDTH))),
        scratch_shapes=[pltpu.VMEM((ext, POOL_WIDTH), F32), pltpu.VMEM((POOL_WIDTH, POOL_WIDTH), BF16)],
        compiler_params=_params(),
    )(pa, pm, w4, pscale, mq_g, mem, mem_norm_g, w_kv, mk_g)


FOX_FWD_HEADS = 4


def _fox_fwd(qa, ka, va, gb):
    s = qa.shape[0]
    t = TILE
    n = s // t
    heads = FOX_FWD_HEADS
    pairs = heads // 2
    group_w = heads * LANES

    def body(qa_ref, ka_ref, va_ref, gb_ref, o_ref, mb_ref, r_ref):
        i = pl.program_id(1)
        lane = lax.broadcasted_iota(jnp.int32, (t, LANES), 1)
        lo = lane < HEAD_DIM
        causal = lax.broadcasted_iota(jnp.int32, (t, t), 1) <= lax.broadcasted_iota(jnp.int32, (t, t), 0)
        qas = [qa_ref[:, hh * LANES:(hh + 1) * LANES] for hh in range(heads)]

        def step(j, carry, masked):
            rows = pl.ds(pl.multiple_of(j * t, t), t)
            def logits(hh):
                sc = _dot(qas[hh], ka_ref[rows, hh * LANES:(hh + 1) * LANES], NT)
                return jnp.where(causal, sc, -1e30) if masked else sc

            def advance(hh, sc):
                m, acc = carry[hh]
                m_new = jnp.maximum(m, jnp.max(sc, axis=-1, keepdims=True))
                p = jnp.exp(sc - m_new).astype(BF16)
                return m_new, jnp.exp(m - m_new) * acc + _dot(p, va_ref[rows, hh * LANES:(hh + 1) * LANES])

            new = []
            sc = logits(0)
            for hh in range(heads):
                sc_next = logits(hh + 1) if hh + 1 < heads else None
                new.append(advance(hh, sc))
                sc = sc_next
            return tuple(new)

        init = (jnp.full((t, 1), -1e30, F32), jnp.zeros((t, LANES), F32))
        carry = lax.fori_loop(0, i, functools.partial(step, masked=False), (init,) * heads)
        res = step(i, carry, masked=True)
        for p in range(pairs):
            outs = []
            rcol = jnp.zeros((t, LANES), F32)
            for hh in range(2):
                m, acc = res[2 * p + hh]
                l = _lane_pick(acc, lane, AUG_LO)
                outs.append(acc * (1.0 / l))
                rcol = jnp.where(lane == hh, m + jnp.log(l), rcol)
            o = _pair_block(outs[0], outs[1], lo)
            sl = slice(p * LANES, (p + 1) * LANES)
            o_ref[:, sl] = o
            g = gb_ref[:, sl]
            mb_ref[:, sl] = (o * (g * _sig(g))).astype(BF16)
            r_ref[p] = rcol

    tile_spec = pl.BlockSpec((t, pairs * LANES), lambda p, i: (i, p))
    full_spec = pl.BlockSpec((s, group_w), lambda p, i: (0, p))
    return pl.pallas_call(
        body, name="fox_fwd", grid=(FOX_HEADS // heads, n),
        out_shape=(jax.ShapeDtypeStruct((s, FOX_WIDTH), F32), jax.ShapeDtypeStruct((s, FOX_WIDTH), BF16),
                   jax.ShapeDtypeStruct((FOX_HEADS // 2, s, LANES), F32)),
        in_specs=[pl.BlockSpec((t, group_w), lambda p, i: (i, p)), full_spec, full_spec, tile_spec],
        out_specs=(tile_spec, tile_spec, pl.BlockSpec((pairs, t, LANES), lambda p, i: (p, i, 0))),
        compiler_params=_params(2),
    )(qa, ka, va, gb)


def _out_loss(x, tgt, ma, mb, mm, wout, gb, o, r4):
    s = x.shape[0]
    t = TILE
    n = s // t
    pairs = FOX_HEADS // 2

    def body(x_ref, t_ref, ma_ref, mb_ref, mm_ref, w_ref, gb_ref, o_ref, r_ref,
             dy_ref, dma_ref, dmm_ref, dw_ref, loss_ref, doa_ref, dgb_ref, rr_ref, mix_ref):
        @pl.when(pl.program_id(0) == 0)
        def _():
            dw_ref[...] = jnp.zeros_like(dw_ref)
            loss_ref[...] = jnp.zeros_like(loss_ref)

        mix_ref[:, 0:256] = ma_ref[...]
        mix_ref[:, 256:768] = mb_ref[...]
        mix_ref[:, 768:1024] = mm_ref[...]
        mix = mix_ref[...]
        err = (x_ref[...] + _dot(mix, w_ref[...])) - t_ref[...]
        row_mean = jnp.sum(err * err, axis=-1, keepdims=True) * (1.0 / D_MODEL)
        loss_ref[...] += 0.5 * jnp.sum(row_mean, axis=0, keepdims=True)
        dy = err * (1.0 / D_MODEL)
        dy_ref[...] = dy
        dyb = dy.astype(BF16)
        dmix = _dot(dyb, w_ref[...], NT)
        dma_ref[...] = dmix[:, 0:256]
        dmm_ref[...] = dmix[:, 768:1024]
        dw_ref[...] += _dot(mix, dyb, TN)

        lane = lax.broadcasted_iota(jnp.int32, (t, LANES), 1)
        lo = lane < HEAD_DIM
        d_os = []
        delta = jnp.zeros((t, LANES), F32)
        for p in range(pairs):
            sl = slice(p * LANES, (p + 1) * LANES)
            g = gb_ref[:, sl]
            sg = _sig(g)
            dm = dmix[:, 256 + p * LANES:256 + (p + 1) * LANES]
            ov = o_ref[:, sl]
            d_o = dm * (g * sg)
            d_os.append(d_o)
            dgb_ref[:, sl] = (dm * ov * (sg * (1.0 + g * (1.0 - sg)))).astype(BF16)
            prod = d_o * ov
            delta = jnp.where(lane == 2 * p, jnp.sum(jnp.where(lo, prod, 0.0), axis=-1, keepdims=True), delta)
            delta = jnp.where(lane == 2 * p + 1, jnp.sum(jnp.where(lo, 0.0, prod), axis=-1, keepdims=True), delta)
            rr_ref[p, 0] = r_ref[p].T[0:8, :]
        minus_delta = _spread3(-delta)
        for h in range(FOX_HEADS):
            blk = slice(h * LANES, (h + 1) * LANES)
            doa_ref[:, blk] = _head_block(d_os[h // 2], h % 2, lo, minus_delta[:, blk])

    return pl.pallas_call(
        body, name="out_loss", grid=(n,),
        out_shape=(jax.ShapeDtypeStruct((s, D_MODEL), F32), jax.ShapeDtypeStruct((s, 256), F32),
                   jax.ShapeDtypeStruct((s, 256), F32), jax.ShapeDtypeStruct((D_MODEL, D_MODEL), F32),
                   jax.ShapeDtypeStruct((1, LANES), F32), jax.ShapeDtypeStruct((s, HEAD_BLOCKS), BF16),
                   jax.ShapeDtypeStruct((s, FOX_WIDTH), BF16), jax.ShapeDtypeStruct((pairs, n, 8, t), F32)),
        in_specs=[_rows(t, D_MODEL), _rows(t, D_MODEL), _rows(t, 256), _rows(t, 512), _rows(t, 256),
                  _full((D_MODEL, D_MODEL)), _rows(t, FOX_WIDTH), _rows(t, FOX_WIDTH),
                  pl.BlockSpec((pairs, t, LANES), lambda i: (0, i, 0))],
        out_specs=(_rows(t, D_MODEL), _rows(t, 256), _rows(t, 256), _full((D_MODEL, D_MODEL)), _full((1, LANES)),
                   _rows(t, HEAD_BLOCKS), _rows(t, FOX_WIDTH), pl.BlockSpec((pairs, 1, 8, t), lambda i: (0, i, 0, 0))),
        scratch_shapes=[pltpu.VMEM((t, D_MODEL), BF16)],
        compiler_params=_params(),
    )(x, tgt, ma, mb, mm, wout, gb, o, r4)


def _side_bwd(pa, db, dma, w4, pscale, pm, dmm, kmn, vmb, mq_g, kv, mnb, mem, w_kv, mk_g, mem_norm_g):
    s = pa.shape[0]
    t = TILE
    n = s // t
    ext = t + POOL_HALO
    nm = mem.shape[0]

    def body(pa_ref, d_ref, dma_ref, w4_ref, sc_ref, pm_ref, dmm_ref, k_ref, v_ref, g_ref,
             kv_ref, mn_ref, mem_ref, wkv_ref, kg_ref, mg_ref,
             dpa_ref, dpm_ref, dw4_ref, dsc_ref, dg_ref, dwkv_ref, dmg_ref, dkg_ref,
             ext_ref, w_ref, dw_ref, dk_ref, dv_ref, gacc_ref, dkv_ref):
        i = pl.program_id(0)

        @pl.when(i == 0)
        def _():
            dw_ref[...] = jnp.zeros_like(dw_ref)
            dsc_ref[...] = jnp.zeros_like(dsc_ref)
            ext_ref[t:ext, :] = jnp.zeros((POOL_HALO, POOL_WIDTH), F32)
            w_ref[...] = _pool_block_diag(w4_ref[...])
            dk_ref[...] = jnp.zeros_like(dk_ref)
            dv_ref[...] = jnp.zeros_like(dv_ref)
            gacc_ref[...] = jnp.zeros_like(gacc_ref)

        dbv = d_ref[...]
        z = _dot(dbv, w_ref[...])
        ga = pa_ref[:, POOL_WIDTH:2 * POOL_WIDTH]
        sg = _sig(ga)
        dma_v = dma_ref[...]
        dya = dma_v * (ga * sg)
        dpa_ref[:, POOL_WIDTH:2 * POOL_WIDTH] = (dma_v * (z * sc_ref[...]) * (sg * (1.0 + ga * (1.0 - sg)))).astype(BF16)
        dsc_ref[...] += jnp.sum(dya * z, axis=0, keepdims=True)
        dzb = (dya * sc_ref[...]).astype(BF16)
        dw_ref[...] += _dot(dbv, dzb, TN)
        dd = _dot(dzb, w_ref[...], NT)
        lane = lax.broadcasted_iota(jnp.int32, (t, POOL_WIDTH), 1)
        pos = (lax.broadcasted_iota(jnp.int32, (t, POOL_WIDTH), 0) + ((n - 1 - i) * t + 1)).astype(F32)
        ext_ref[0:t, :] = dd / jnp.minimum(pos, _pool_window(lane))
        e = ext_ref[...]
        s2 = e + pltpu.roll(e, ext - 1, axis=0)
        s4 = s2 + pltpu.roll(s2, ext - 2, axis=0)
        s8 = s4 + pltpu.roll(s4, ext - 4, axis=0)
        s16 = s8 + pltpu.roll(s8, ext - 8, axis=0)
        lane_e = lax.broadcasted_iota(jnp.int32, (ext, POOL_WIDTH), 1)
        win = _pool_pick(lane_e, s2, s4, s8, s16)[0:t, :]
        dpa_ref[:, 0:POOL_WIDTH] = (win - dd).astype(BF16)
        ext_ref[t:ext, :] = ext_ref[0:POOL_HALO, :]

        lo = _lane_lo((t, LANES))
        for p in range(MEM_WIDTH // LANES):
            sl = slice(p * LANES, (p + 1) * LANES)
            qb = pm_ref[:, sl]
            rr = _head_rms(qb, lo)
            qhat = qb * rr
            g = g_ref[:, sl]
            qs = ((qhat * g) * ATT_SCALE).astype(BF16)
            gm = pm_ref[:, MEM_WIDTH + p * LANES:MEM_WIDTH + (p + 1) * LANES]
            sg = _sig(gm)
            dmo = dmm_ref[:, sl]
            d_o = dmo * (gm * sg)
            kp = k_ref[:, sl]
            vp = v_ref[:, sl]
            outs, dqs = [], []
            for hh in range(2):
                msk = lo if hh == 0 else jnp.logical_not(lo)
                qm = jnp.where(msk, qs, jnp.zeros_like(qs))
                prob = _mem_softmax(qm, kp)
                pb = prob.astype(BF16)
                outs.append(_dot(pb, vp))
                dom = jnp.where(msk, d_o, 0.0).astype(BF16)
                dp = _dot(dom, vp, NT)
                ds = (prob * (dp - jnp.sum(prob * dp, axis=-1, keepdims=True))).astype(BF16)
                dqs.append(_dot(ds, kp))
                dk_ref[:, sl] += _dot(ds, qm, TN)
                dv_ref[:, sl] += _dot(pb, dom, TN)
            o = jnp.where(lo, outs[0], outs[1])
            dqn = jnp.where(lo, dqs[0], dqs[1]) * ATT_SCALE
            dpm_ref[:, sl] = _head_norm_bwd(dqn, qhat, rr, g, lo).astype(BF16)
            dpm_ref[:, MEM_WIDTH + p * LANES:MEM_WIDTH + (p + 1) * LANES] = (
                dmo * o * (sg * (1.0 + gm * (1.0 - sg)))).astype(BF16)
            gacc_ref[:, sl] += jnp.sum(dqn * qhat, axis=0, keepdims=True)

        @pl.when(i == n - 1)
        def _():
            own = jnp.where(_same_group((POOL_WIDTH, POOL_WIDTH)), dw_ref[...], 0.0)
            dw4_ref[...] = jnp.dot(own, _group_onehot((POOL_WIDTH, HEAD_DIM), True), preferred_element_type=F32,
                                   precision=lax.Precision.HIGHEST)
            dg_ref[...] = _fold_heads(gacc_ref[...])

            lo_m = _lane_lo((nm, LANES))
            kacc = []
            for p in range(MEM_WIDTH // LANES):
                sl = slice(p * LANES, (p + 1) * LANES)
                kb = kv_ref[:, sl]
                rr = _head_rms(kb, lo_m)
                khat = kb * rr
                dk = dk_ref[:, sl]
                dkv_ref[:, sl] = _head_norm_bwd(dk, khat, rr, kg_ref[:, sl], lo_m).astype(BF16)
                kacc.append(jnp.sum(dk * khat, axis=0, keepdims=True))
            dkg_ref[...] = _fold_heads(jnp.concatenate(kacc, axis=1))
            dkv_ref[:, MEM_WIDTH:] = dv_ref[...].astype(BF16)
            dkv = dkv_ref[...]
            dwkv_ref[...] = _dot(mn_ref[...], dkv, TN)
            dmn = _dot(dkv, wkv_ref[...], NT)
            xm = mem_ref[...]
            rr = lax.rsqrt(jnp.mean(xm * xm, axis=-1, keepdims=True) + EPS)
            dmg_ref[...] = jnp.sum(dmn * (xm * rr), axis=0, keepdims=True)

    def rev(w):
        return _rows_rev(t, w, n)

    row = jax.ShapeDtypeStruct((1, LANES), F32)
    return pl.pallas_call(
        body, name="side_bwd", grid=(n,),
        out_shape=(jax.ShapeDtypeStruct((s, 512), BF16), jax.ShapeDtypeStruct((s, 512), BF16),
                   jax.ShapeDtypeStruct((POOL_ROWS, HEAD_DIM), F32), jax.ShapeDtypeStruct((1, POOL_WIDTH), F32), row,
                   jax.ShapeDtypeStruct((D_MODEL, 2 * MEM_WIDTH), F32), jax.ShapeDtypeStruct((1, D_MODEL), F32), row),
        in_specs=[rev(512), rev(POOL_WIDTH), rev(POOL_WIDTH), _full((POOL_ROWS, HEAD_DIM)), _full((1, POOL_WIDTH)),
                  rev(512), rev(MEM_WIDTH), _full((N_MEM, MEM_WIDTH)), _full((N_MEM, MEM_WIDTH)), _full((1, MEM_WIDTH)),
                  _full((nm, 2 * MEM_WIDTH)), _full((nm, D_MODEL)), _full((nm, D_MODEL)),
                  _full((D_MODEL, 2 * MEM_WIDTH)), _full((1, MEM_WIDTH)), _full((1, D_MODEL))],
        out_specs=(rev(512), rev(512), _full((POOL_ROWS, HEAD_DIM)), _full((1, POOL_WIDTH)), _full((1, LANES)),
                   _full((D_MODEL, 2 * MEM_WIDTH)), _full((1, D_MODEL)), _full((1, LANES))),
        scratch_shapes=[pltpu.VMEM((ext, POOL_WIDTH), F32), pltpu.VMEM((POOL_WIDTH, POOL_WIDTH), BF16),
                        pltpu.VMEM((POOL_WIDTH, POOL_WIDTH), F32), pltpu.VMEM((N_MEM, MEM_WIDTH), F32),
                        pltpu.VMEM((N_MEM, MEM_WIDTH), F32), pltpu.VMEM((1, MEM_WIDTH), F32),
                        pltpu.VMEM((nm, 2 * MEM_WIDTH), BF16)],
        compiler_params=_params(),
    )(pa, db, dma, w4, pscale, pm, dmm, kmn, vmb, mq_g, kv, mnb, mem, w_kv, mk_g, mem_norm_g)


FOX_BWD_HEADS = 4


def _fox_bwd(ka, va, qa, doa, rr):
    s = ka.shape[0]
    t = TILE
    n = s // t
    heads = FOX_BWD_HEADS
    group_w = heads * LANES

    def body(ka_ref, va_ref, qa_ref, doa_ref, rr_ref, dka_ref, dva_ref, dqa_ref):
        j = pl.program_id(1)

        @pl.when(j == 0)
        def _():
            dqa_ref[...] = jnp.zeros_like(dqa_ref)

        causal = lax.broadcasted_iota(jnp.int32, (t, t), 0) <= lax.broadcasted_iota(jnp.int32, (t, t), 1)
        kas = [ka_ref[:, hh * LANES:(hh + 1) * LANES] for hh in range(heads)]
        vas = [va_ref[:, hh * LANES:(hh + 1) * LANES] for hh in range(heads)]

        def step(i, carry, masked):
            rows = pl.ds(pl.multiple_of(i * t, t), t)
            new = []
            for hh in range(heads):
                cols = slice(hh * LANES, (hh + 1) * LANES)
                dk_a, dv_a = carry[hh]
                qb = qa_ref[rows, cols]
                d_o = doa_ref[rows, cols]
                arg = _dot(kas[hh], qb, NT) - rr_ref[hh // 2, i, hh % 2:hh % 2 + 1, :]
                if masked:
                    arg = jnp.where(causal, arg, -1e30)
                pt = jnp.exp(arg)
                dst = (pt * _dot(vas[hh], d_o, NT)).astype(BF16)
                dv_a = dv_a + _dot(pt.astype(BF16), d_o)
                dk_a = dk_a + _dot(dst, qb)
                dqa_ref[rows, cols] += _dot(dst, kas[hh], TN)
                new.append((dk_a, dv_a))
            return tuple(new)

        zero = jnp.zeros((t, LANES), F32)
        carry = step(j, ((zero, zero),) * heads, masked=True)
        res = lax.fori_loop(j + 1, n, functools.partial(step, masked=False), carry)
        for hh in range(heads):
            cols = slice(hh * LANES, (hh + 1) * LANES)
            dka_ref[:, cols] = res[hh][0]
            dva_ref[:, cols] = res[hh][1]

    tile_spec = pl.BlockSpec((t, group_w), lambda p, j: (j, p))
    full_spec = pl.BlockSpec((s, group_w), lambda p, j: (0, p))
    return pl.pallas_call(
        body, name="fox_bwd", grid=(FOX_HEADS // heads, n),
        out_shape=(jax.ShapeDtypeStruct((s, HEAD_BLOCKS), F32),) * 3,
        in_specs=[tile_spec, tile_spec, full_spec, full_spec,
                  pl.BlockSpec((heads // 2, n, 8, t), lambda p, j: (p, 0, 0, 0))],
        out_specs=(tile_spec, tile_spec, full_spec),
        compiler_params=_params(2),
    )(ka, va, qa, doa, rr)


def _fox_post_tile(i, n, t, dqa_ref, dka_ref, dva_ref, qk_ref, fb_ref, bf_ref, qg_ref, kg_ref,
                   dqk_ref, dv_ref, dfb_ref, dqg_ref, dkg_ref, dbf_ref, qacc_ref, kacc_ref, carry_ref,
                   between):
    @pl.when(i == 0)
    def _():
        qacc_ref[...] = jnp.zeros_like(qacc_ref)
        kacc_ref[...] = jnp.zeros_like(kacc_ref)
        dbf_ref[...] = jnp.zeros_like(dbf_ref)
        carry_ref[...] = jnp.zeros_like(carry_ref)

    lane = lax.broadcasted_iota(jnp.int32, (t, LANES), 1)
    row = lax.broadcasted_iota(jnp.int32, (t, LANES), 0)
    lo = lane < HEAD_DIM

    def head_blocks(ref, p):
        return ref[:, 2 * p * LANES:(2 * p + 1) * LANES], ref[:, (2 * p + 1) * LANES:(2 * p + 2) * LANES]

    def issue(k):
        if between[k] is not None:
            between[k]()

    sums = []
    pairs = FOX_WIDTH // LANES
    for side, (src_ref, g_ref, acc_ref, scale) in enumerate(((dqa_ref, qg_ref, qacc_ref, ATT_SCALE),
                                                             (dka_ref, kg_ref, kacc_ref, 1.0))):
        total = jnp.zeros((t, LANES), F32)
        for p in range(pairs):
            issue(side * pairs + p)
            sl = slice(p * LANES, (p + 1) * LANES)
            cols = slice(side * FOX_WIDTH + p * LANES, side * FOX_WIDTH + (p + 1) * LANES)
            if side == 0:
                dv_ref[:, sl] = _pair_block(*head_blocks(dva_ref, p), lo).astype(BF16)
            d0, d1 = head_blocks(src_ref, p)
            total = total + (d0 + d1)
            raw = qk_ref[:, cols]
            rr = _head_rms(raw, lo)
            xhat = raw * rr
            dn = _pair_block(d0, d1, lo) * scale
            dqk_ref[:, cols] = _head_norm_bwd(dn, xhat, rr, g_ref[:, sl], lo).astype(BF16)
            acc_ref[:, sl] += jnp.sum(dn * xhat, axis=0, keepdims=True)
        sums.append(total)
    issue(2 * pairs)
    dq_sum, dk_sum = sums

    acc = (pltpu.roll(dq_sum, LANES - KEY_SUM_LANE, axis=1) - pltpu.roll(dk_sum, LANES - QUERY_SUM_LANE, axis=1))
    acc = jnp.where(lane < FOX_HEADS, acc, 0.0)
    sh = 1
    while sh < t:
        acc = acc + jnp.where(row < t - sh, pltpu.roll(acc, t - sh, axis=0), 0.0)
        sh *= 2
    dlogf = acc + carry_ref[...]
    dfb_ref[...] = dlogf
    carry_ref[...] = dfb_ref[0:1, :]
    z = fb_ref[...] + bf_ref[...]
    dz = jnp.where(lane < FOX_HEADS, dlogf * (1.0 / (1.0 + jnp.exp(z))), 0.0)
    dfb_ref[...] = dz
    dbf_ref[...] += jnp.sum(dz, axis=0, keepdims=True)

    @pl.when(i == n - 1)
    def _():
        dqg_ref[...] = _fold_heads(qacc_ref[...])
        dkg_ref[...] = _fold_heads(kacc_ref[...])


def _assemble_dproj(dp_ref, dpa_ref, dqk_ref, dv_ref, dgb_ref, dpm_ref, dfb_ref):
    dp_ref[:, PA_LO:QB_LO] = dpa_ref[...]
    dp_ref[:, QB_LO:VB_LO] = dqk_ref[...]
    dp_ref[:, VB_LO:GB_LO] = dv_ref[...]
    dp_ref[:, GB_LO:PM_LO] = dgb_ref[...]
    dp_ref[:, PM_LO:FB_LO] = dpm_ref[...]
    dp_ref[:, FB_LO:PROJ_PAD] = dfb_ref[...].astype(BF16)


def _dproj_specs(t):
    return [_rows(t, 512), _rows(t, 2 * FOX_WIDTH), _rows(t, FOX_WIDTH), _rows(t, FOX_WIDTH), _rows(t, 512),
            _rows(t, LANES)]


IN_BWD_X_TILE = 256


def _in_bwd_x(x, dy, norm_g, wp, dparts, gparts, axes, smalls):
    s = x.shape[0]
    t = IN_BWD_X_TILE
    n = s // t
    na = len(gparts)
    n_dp = len(dparts)
    vec_leaves, loss_row, dw4 = smalls if smalls is not None else ((), None, None)
    nv = len(vec_leaves)
    n_small = nv + 2 if smalls is not None else 0
    small_base = _ShardReduce.SEMS * na

    def body(*refs):
        x_ref, dy_ref, g_ref, wp_ref = refs[0:4]
        dp_parts = refs[4:4 + n_dp]
        o = 4 + n_dp
        g_refs = refs[o:o + na]
        small_in = refs[o + na:o + na + n_small]
        o += na + n_small
        gx_ref, dg_ref = refs[o:o + 2]
        out_refs = refs[o + 2:o + 2 + na]
        small_out = refs[o + 2 + na:o + 2 + na + (2 if smalls is not None else 0)]
        o += 2 + na + len(small_out)
        dp_ref = refs[o]
        bufs = tuple(refs[o + 1 + k * na:o + 1 + (k + 1) * na] for k in range(5))
        rest = refs[o + 1 + 5 * na:]

        i = pl.program_id(0)
        if na or smalls is not None:
            send_sems, recv_sems, local_sems = rest[-3:]
        red = _ShardReduce(g_refs, out_refs, axes, bufs, send_sems, recv_sems, local_sems) if na else None

        @pl.when(i == 0)
        def _():
            dg_ref[...] = jnp.zeros_like(dg_ref)
            if red is not None:
                red.exchange_with_sibling()

        if red is not None:
            for k in (1, 2, 3):
                pl.when(i == k)(functools.partial(red.send_to_chip, k))
            pl.when(i == 4)(red.keep_mine)

        _assemble_dproj(dp_ref, *dp_parts)
        dh = _dot(dp_ref[...], wp_ref[...])
        xv = x_ref[...]
        rr = lax.rsqrt(jnp.mean(xv * xv, axis=-1, keepdims=True) + EPS)
        xhat = xv * rr
        scaled = dh * g_ref[...]
        gx_ref[...] = dy_ref[...] + rr * (scaled - xhat * jnp.mean(xhat * scaled, axis=-1, keepdims=True))
        dg_ref[...] += jnp.sum(dh * xhat, axis=0, keepdims=True)

        def small_all_reduce():
            leaf_refs, (loss_ref, dw4_ref) = small_in[0:nv], small_in[nv:]
            vec_out, dw4_out = small_out
            vec_mine, vec_recv, dw4_recv = rest[0:3]
            cx, cy, c = _my_place()
            me_lin = 4 * cx + 2 * cy + c

            def copy(k, src, dst, base):
                peer = (me_lin + k) % 8
                return pltpu.make_async_remote_copy(
                    src_ref=src, dst_ref=dst.at[me_lin], send_sem=send_sems.at[base + k - 1],
                    recv_sem=recv_sems.at[base + k - 1], device_id=(peer // 4, (peer // 2) % 2, peer % 2),
                    device_id_type=MESH)

            vec_mine[...] = jnp.zeros_like(vec_mine)
            vec_mine[0:1, :] = dg_ref[...]
            for (_, row, _), ref in zip(VEC_LEAVES[1:], leaf_refs):
                vec_mine[row:row + 1, 0:ref.shape[1]] = ref[...]
            vec_mine[VEC_LOSS_ROW:VEC_LOSS_ROW + 1, 0:LANES] = loss_ref[...]
            copies = [copy(k, src, dst, base) for k in range(1, 8)
                      for src, dst, base in ((vec_mine, vec_recv, small_base), (dw4_ref, dw4_recv, small_base + 7))]
            for cp in copies:
                cp.start()
            for cp in copies:
                cp.wait_recv()
            vec_recv[me_lin] = vec_mine[...]
            dw4_recv[me_lin] = dw4_ref[...]
            vtot, wtot = vec_recv[0], dw4_recv[0]
            for d in range(1, 8):
                vtot = vtot + vec_recv[d]
                wtot = wtot + dw4_recv[d]
            vec_out[...] = vtot
            dw4_out[...] = wtot
            for cp in copies:
                cp.wait_send()

        @pl.when(i == n - 1)
        def _():
            if red is not None:
                red.sum_and_share()
            if smalls is not None:
                small_all_reduce()
            if red is not None:
                red.finish()

    any_spec = pl.BlockSpec(memory_space=pl.ANY)
    scratch = [pltpu.VMEM((t, PROJ_PAD), BF16)] + _ShardReduce.scratch(gparts, axes)
    out_shape = [jax.ShapeDtypeStruct((s, D_MODEL), F32), jax.ShapeDtypeStruct((1, D_MODEL), F32)]
    out_shape += [jax.ShapeDtypeStruct(g.shape[1:], F32) for g in gparts]
    out_specs = [_rows(t, D_MODEL), _full((1, D_MODEL))] + [any_spec] * na
    small_args = []
    if smalls is not None:
        small_args = [*vec_leaves, loss_row, dw4]
        out_shape += [jax.ShapeDtypeStruct((VEC_ROWS, D_MODEL), F32), jax.ShapeDtypeStruct(dw4.shape, F32)]
        out_specs += [_full((VEC_ROWS, D_MODEL)), _full(dw4.shape)]
        scratch += [pltpu.VMEM((VEC_ROWS, D_MODEL), F32), pltpu.VMEM((8, VEC_ROWS, D_MODEL), F32),
                    pltpu.VMEM((8,) + dw4.shape, F32)]
    if na or smalls is not None:
        n_sems = small_base + 14
        scratch += [pltpu.SemaphoreType.DMA((n_sems,)), pltpu.SemaphoreType.DMA((n_sems,)),
                    pltpu.SemaphoreType.DMA((max(_ShardReduce.LOCAL * na, 1),))]
    return pl.pallas_call(
        body, name="in_bwd_x", grid=(n,), out_shape=tuple(out_shape),
        in_specs=[_rows(t, D_MODEL), _rows(t, D_MODEL), _full((1, D_MODEL)),
                  pl.BlockSpec((PROJ_PAD, D_MODEL), lambda i: (0, 0), pipeline_mode=pl.Buffered(1))]
        + _dproj_specs(t) + [any_spec] * na + [_full(a.shape) for a in small_args],
        out_specs=tuple(out_specs), scratch_shapes=scratch, compiler_params=_params(),
    )(x, dy, norm_g, wp, *dparts, *gparts, *small_args)


def _in_bwd_w(hb, dpa, dgb, dpm, fox, gparts, axes):
    s = hb.shape[0]
    t = TILE
    n = s // t
    na = len(gparts)
    f_hi = F_ORIG_LO + FOX_HEADS
    n_in = 4 + len(fox)

    def body(*refs):
        h_ref, dpa_ref, dgb_ref, dpm_ref = refs[0:4]
        fox_refs = refs[4:n_in]
        g_refs = refs[n_in:n_in + na]
        o = n_in + na
        dw_ref, dqk_ref, dv_ref, dfb_ref, dqg_ref, dkg_ref, dbf_ref = refs[o:o + 7]
        out_refs = refs[o + 7:o + 7 + na]
        o += 7 + na
        fox_scratch = refs[o:o + 3]
        bufs = tuple(refs[o + 3 + k * na:o + 3 + (k + 1) * na] for k in range(5))
        i = pl.program_id(0)
        red = _ShardReduce(g_refs, out_refs, axes, bufs, *refs[o + 3 + 5 * na:]) if na else None

        @pl.when(i == 0)
        def _():
            dw_ref[...] = jnp.zeros_like(dw_ref)
            if red is not None:
                red.exchange_with_sibling()

        if red is not None:
            @pl.when(i == 1)
            def _():
                for k in (1, 2, 3):
                    red.send_to_chip(k)
                red.keep_mine()

        hv = h_ref[...]

        def rows_of(lo, ref, cols=slice(None)):
            def add():
                dproj = ref[:, cols]
                dw_ref[lo:lo + dproj.shape[1], :] += _dot(dproj, hv, TN)
            return add

        q_cols, k_cols = slice(0, FOX_WIDTH), slice(FOX_WIDTH, 2 * FOX_WIDTH)
        between = (rows_of(0, dpa_ref), rows_of(f_hi, dgb_ref), rows_of(f_hi + FOX_WIDTH, dpm_ref), None,
                   rows_of(QB_LO, dqk_ref, q_cols), rows_of(VB_LO, dv_ref), None, None, rows_of(KB_LO, dqk_ref, k_cols))
        _fox_post_tile(i, n, t, *fox_refs, dqk_ref, dv_ref, dfb_ref, dqg_ref, dkg_ref, dbf_ref, *fox_scratch, between)
        dw_ref[F_ORIG_LO:f_hi, :] += _dot(dfb_ref[...].astype(BF16), hv, TN)[0:FOX_HEADS, :]

        if red is not None:
            @pl.when(i == n - 1)
            def _():
                red.sum_and_share()
                red.finish()

    def rev(w):
        return _rows_rev(t, w, n)

    any_spec = pl.BlockSpec(memory_space=pl.ANY)
    row = jax.ShapeDtypeStruct((1, LANES), F32)
    scratch = [pltpu.VMEM((1, FOX_WIDTH), F32), pltpu.VMEM((1, FOX_WIDTH), F32), pltpu.VMEM((1, LANES), F32)]
    scratch += _ShardReduce.scratch(gparts, axes)
    if na:
        scratch += [pltpu.SemaphoreType.DMA((_ShardReduce.SEMS * na,)), pltpu.SemaphoreType.DMA((_ShardReduce.SEMS * na,)),
                    pltpu.SemaphoreType.DMA((_ShardReduce.LOCAL * na,))]
    return pl.pallas_call(
        body, name="in_bwd_w", grid=(n,),
        out_shape=(jax.ShapeDtypeStruct((IN_WIDTH, D_MODEL), F32), jax.ShapeDtypeStruct((s, 2 * FOX_WIDTH), BF16),
                   jax.ShapeDtypeStruct((s, FOX_WIDTH), BF16), jax.ShapeDtypeStruct((s, LANES), F32), row, row, row)
        + tuple(jax.ShapeDtypeStruct(g.shape[1:], F32) for g in gparts),
        in_specs=[rev(D_MODEL), rev(512), rev(FOX_WIDTH), rev(512), rev(HEAD_BLOCKS), rev(HEAD_BLOCKS),
                  rev(HEAD_BLOCKS), rev(2 * FOX_WIDTH), rev(LANES), _full((1, LANES)), _full((1, FOX_WIDTH)),
                  _full((1, FOX_WIDTH))] + [any_spec] * na,
        out_specs=(pl.BlockSpec((IN_WIDTH, D_MODEL), lambda i: (0, 0), pipeline_mode=pl.Buffered(1)),
                   rev(2 * FOX_WIDTH), rev(FOX_WIDTH), rev(LANES), _full((1, LANES)), _full((1, LANES)),
                   _full((1, LANES))) + (any_spec,) * na,
        scratch_shapes=scratch, compiler_params=_params(),
    )(hb, dpa, dgb, dpm, *fox, *gparts)


def _adamw_math(w_ref, gv, m_ref, v_ref, d_ref, nm_ref, nv_ref):
    nm = ADAM_B1 * m_ref[...] + (1.0 - ADAM_B1) * gv
    nv = ADAM_B2 * v_ref[...] + (1.0 - ADAM_B2) * (gv * gv)
    m_hat = nm / (1.0 - ADAM_B1 ** ADAM_STEP)
    v_hat = nv / (1.0 - ADAM_B2 ** ADAM_STEP)
    d_ref[...] = -ADAM_LR * (m_hat / (jnp.sqrt(v_hat) + ADAM_EPS) + ADAM_WD * w_ref[...])
    nm_ref[...] = nm
    nv_ref[...] = nv


def _adamw(name, w, g, m, v):
    rows, cols = w.shape
    tc = 256 if rows * cols > 256 * 1024 else cols
    n = cols // tc

    def body(w_ref, g_ref, m_ref, v_ref, d_ref, nm_ref, nv_ref):
        _adamw_math(w_ref, g_ref[...], m_ref, v_ref, d_ref, nm_ref, nv_ref)

    spec = pl.BlockSpec((rows, tc), lambda i: (0, i))
    return pl.pallas_call(
        body, name=name, grid=(n,),
        out_shape=(jax.ShapeDtypeStruct((rows, cols), F32),) * 3,
        in_specs=[spec] * 4, out_specs=(spec,) * 3,
        compiler_params=_params(),
    )(w, g, m, v)


def _adamw_rest(vec, dw4, leaves, pool, shards):
    nl = len(VEC_LEAVES) + 1
    ns = len(shards)

    def body(*refs):
        vec_ref, dw4_ref = refs[0:2]
        wmv = refs[2:2 + 3 * nl]
        shard_in = refs[2 + 3 * nl:2 + 3 * nl + 4 * ns]
        o = 2 + 3 * nl + 4 * ns
        loss_ref = refs[o]
        outs = refs[o + 1:o + 1 + 4 * nl]
        shard_out = refs[o + 1 + 4 * nl:]
        loss_ref[...] = vec_ref[VEC_LOSS_ROW:VEC_LOSS_ROW + 1, 0:1]
        for k in range(nl):
            if k < nl - 1:
                _, row, width = VEC_LEAVES[k]
                gv = vec_ref[row:row + 1, 0:width]
            else:
                gv = dw4_ref[...]
            w_ref, m_ref, v_ref = wmv[3 * k:3 * k + 3]
            g_ref, d_ref, nm_ref, nv_ref = outs[4 * k:4 * k + 4]
            g_ref[...] = gv
            _adamw_math(w_ref, gv, m_ref, v_ref, d_ref, nm_ref, nv_ref)
        for k in range(ns):
            w_ref, g_ref, m_ref, v_ref = shard_in[4 * k:4 * k + 4]
            _adamw_math(w_ref, g_ref[...], m_ref, v_ref, *shard_out[3 * k:3 * k + 3])

    shapes = [jax.ShapeDtypeStruct((1, width), F32) for _, _, width in VEC_LEAVES] + [
        jax.ShapeDtypeStruct(dw4.shape, F32)]
    flat_in = [a for triple in list(leaves) + [pool] for a in triple] + [a for quad in shards for a in quad]
    res = pl.pallas_call(
        body, name="adamw_rest",
        out_shape=(jax.ShapeDtypeStruct((1, 1), F32),) + tuple(s for s in shapes for _ in range(4))
        + tuple(jax.ShapeDtypeStruct(quad[0].shape, F32) for quad in shards for _ in range(3)),
        compiler_params=pltpu.CompilerParams(vmem_limit_bytes=VMEM_LIMIT),
    )(vec, dw4, *flat_in)
    per = [res[1 + 4 * k:5 + 4 * k] for k in range(nl)]
    big = res[1 + 4 * nl:]
    return (res[0], [p[0] for p in per], [p[1] for p in per], [p[2] for p in per], [p[3] for p in per],
            [big[3 * k:3 * k + 3] for k in range(ns)])


def _full_w_in_padded(halves):
    cols = IN_WIDTH // 4
    w_t = halves.reshape(4, 2, cols, D_MODEL // 2).transpose(0, 2, 1, 3).reshape(IN_WIDTH, D_MODEL)
    return jnp.concatenate([
        w_t[0:F_ORIG_LO], w_t[F_ORIG_LO + FOX_HEADS:], w_t[F_ORIG_LO:F_ORIG_LO + FOX_HEADS],
        jnp.zeros((PROJ_PAD - IN_WIDTH, D_MODEL), w_t.dtype)], axis=0)


def _tile_heads(g, n):
    return jnp.tile(g.reshape(1, HEAD_DIM), (1, n))


def kernel(x, mem, norm_g, w_in, b_f, w_pool, pool_scale, fox_q_g, fox_k_g, mem_norm_g, w_mem_kv, mem_q_g, mem_k_g, w_out, loss_target, m_norm_g, m_w_in, m_b_f, m_w_pool, m_pool_scale, m_fox_q_g, m_fox_k_g, m_mem_norm_g, m_w_mem_kv, m_mem_q_g, m_mem_k_g, m_w_out, v_norm_g, v_w_in, v_b_f, v_w_pool, v_pool_scale, v_fox_q_g, v_fox_k_g, v_mem_norm_g, v_w_mem_kv, v_mem_q_g, v_mem_k_g, v_w_out):
    w_in_t, m_w_in_t, v_w_in_t = w_in[0].T, m_w_in[0].T, v_w_in[0].T
    axes = (1, 0, 0)

    g_in, g_kv, g_out = _all_gather_weights([w_in_t, w_mem_kv[0], w_out[0]], axes)
    wp = _full_w_in_padded(g_in)
    tiled = _tiled_params(b_f, fox_q_g, fox_k_g, mem_q_g, mem_k_g)
    fwd = _fwd_in(x[0], norm_g, wp, *tiled[0:3])
    w_kv_b = g_kv.reshape(D_MODEL, 2 * MEM_WIDTH)
    w_out_b = g_out.reshape(D_MODEL, D_MODEL)
    w4 = w_pool.reshape(POOL_ROWS, HEAD_DIM)
    dy, hb, dpa, dgb, dpm, fox, dw_kv, dw_out, (dmemnorm_g, dpscale, dmq_g, dmk_g), loss_row, dw4 = _local_partials(
        x[0], mem[0], loss_target[0], fwd, w_kv_b, w_out_b, tiled, w4, pool_scale, mem_norm_g)

    early = [dw_kv.reshape(4, D_MODEL // 4, 2 * MEM_WIDTH), dw_out.reshape(4, D_MODEL // 4, D_MODEL)]
    dwp, dqk, dvb, dfb, dfq_g, dfk_g, dbf, g_w_kv, g_w_out = _in_bwd_w(hb, dpa, dgb, dpm, fox, early, axes[1:])
    dparts = (dpa, dqk, dvb, dgb, dpm, dfb)
    vec_leaves = (dmemnorm_g, dpscale, dbf, dfq_g, dfk_g, dmq_g, dmk_g)
    grad_x, _, g_w_in_t, vec, dw4_sum = _in_bwd_x(
        x[0], dy, norm_g, wp, dparts, [dwp.reshape(4, IN_WIDTH // 4, D_MODEL)], axes[0:1], (vec_leaves, loss_row, dw4))

    small_wmv = [(norm_g, m_norm_g, v_norm_g), (mem_norm_g, m_mem_norm_g, v_mem_norm_g),
                 (pool_scale, m_pool_scale, v_pool_scale), (b_f, m_b_f, v_b_f), (fox_q_g, m_fox_q_g, v_fox_q_g),
                 (fox_k_g, m_fox_k_g, v_fox_k_g), (mem_q_g, m_mem_q_g, v_mem_q_g), (mem_k_g, m_mem_k_g, v_mem_k_g)]
    pool_wmv = tuple(a.reshape(POOL_ROWS, HEAD_DIM) for a in (w_pool, m_w_pool, v_w_pool))
    loss, *small_out, (upd_kv, upd_out) = _adamw_rest(
        vec, dw4_sum, small_wmv, pool_wmv, [(w_mem_kv[0], g_w_kv, m_w_mem_kv[0], v_w_mem_kv[0]),
                                             (w_out[0], g_w_out, m_w_out[0], v_w_out[0])])
    big = [[g_w_in_t.T[None], g_w_kv[None], g_w_out[None]]]
    upd = [[a.T for a in _adamw("adamw_w_in", w_in_t, g_w_in_t, m_w_in_t, v_w_in_t)], upd_kv, upd_out]
    big += [[u[k][None] for u in upd] for k in range(3)]

    def leaves(k):
        sm = small_out[k]
        b_in, b_kv, b_out = big[k]
        return (sm[0], b_in, sm[3], sm[8].reshape(w_pool.shape), sm[2], sm[4], sm[5], sm[1], b_kv, sm[6], sm[7], b_out)

    return (loss.reshape(()), grad_x[None], *leaves(0), *leaves(1), *leaves(2), *leaves(3))


def _tiled_params(b_f, fox_q_g, fox_k_g, mem_q_g, mem_k_g):
    return (jnp.pad(b_f, ((0, 0), (0, LANES - FOX_HEADS))), _tile_heads(fox_q_g, FOX_HEADS),
            _tile_heads(fox_k_g, FOX_HEADS), _tile_heads(mem_q_g, 4), _tile_heads(mem_k_g, 4))


def _local_partials(xs, mems, tgt, fwd, w_kv_b, w_out_b, tiled, w4, pool_scale, mem_norm_g):
    hb, pa, qk, qa, ka, va, gb, pm, fb = fwd
    bf_pad, fq_g, fk_g, mq_g, mk_g = tiled

    ma, db, mm, mnb, kv, kmn, vmb = _side_fwd(pa, pm, w4, pool_scale, mq_g, mems, mem_norm_g, w_kv_b, mk_g)
    o, mb, r4 = _fox_fwd(qa, ka, va, gb)
    dy, dma, dmm, dw_out, loss_row, doa, dgb, rr = _out_loss(xs, tgt, ma, mb, mm, w_out_b, gb, o, r4)

    dpa, dpm, dw4, dpscale, dmq_g, dw_kv, dmemnorm_g, dmk_g = _side_bwd(
        pa, db, dma, w4, pool_scale, pm, dmm, kmn, vmb, mq_g, kv, mnb, mems, w_kv_b, mk_g, mem_norm_g)
    dka, dva, dqa = _fox_bwd(ka, va, qa, doa, rr)
    fox = (dqa, dka, dva, qk, fb, bf_pad, fq_g, fk_g)
    return dy, hb, dpa, dgb, dpm, fox, dw_kv, dw_out, (dmemnorm_g, dpscale, dmq_g, dmk_g), loss_row, dw4
```

```python
import functools

import jax
import jax.numpy as jnp
from jax import lax
from jax.experimental import pallas as pl
from jax.experimental.pallas import tpu as pltpu

F32 = jnp.float32
BF16 = jnp.bfloat16
MESH = pl.DeviceIdType.MESH

D_MODEL = 1024
HEAD_DIM = 64
POOL_WIDTH = 256
FOX_WIDTH = 512
FOX_HEADS = 8
MEM_WIDTH = 256
N_MEM = 256
IN_WIDTH = 3080
EPS = 1e-6
ATT_SCALE = 0.125

ADAM_LR = 0.001
ADAM_B1 = 0.9
ADAM_B2 = 0.999
ADAM_EPS = 1e-08
ADAM_WD = 0.01
ADAM_STEP = 10

LANES = 128
PA_LO, QB_LO, KB_LO, VB_LO, GB_LO, PM_LO, FB_LO, PROJ_PAD = 0, 512, 1024, 1536, 2048, 2560, 3072, 3200
F_ORIG_LO = 2048

TILE = 512
VMEM_LIMIT = 56 * 1024 * 1024

VEC_LEAVES = (("norm_g", 0, 1024), ("mem_norm_g", 1, 1024), ("pool_scale", 2, 256), ("b_f", 3, 8),
              ("fox_q_g", 4, 64), ("fox_k_g", 5, 64), ("mem_q_g", 6, 64), ("mem_k_g", 7, 64))
VEC_LOSS_ROW = 8
VEC_ROWS = 16
POOL_ROWS = 256


def _params(n_grid=1, vmem=VMEM_LIMIT):
    return pltpu.CompilerParams(dimension_semantics=("arbitrary",) * n_grid, vmem_limit_bytes=vmem)


def _rows(t, w):
    return pl.BlockSpec((t, w), lambda i: (i, 0))


def _rows_rev(t, w, n):
    return pl.BlockSpec((t, w), lambda i: (n - 1 - i, 0))


def _full(shape):
    return pl.BlockSpec(shape, lambda i: (0,) * len(shape))


def _sig(x):
    return 1.0 / (1.0 + jnp.exp(-x))


def _lane_lo(shape):
    return lax.broadcasted_iota(jnp.int32, shape, 1) < HEAD_DIM


def _pair_sum(v, lo):
    s0 = jnp.sum(jnp.where(lo, v, 0.0), axis=-1, keepdims=True)
    s1 = jnp.sum(jnp.where(lo, 0.0, v), axis=-1, keepdims=True)
    return jnp.where(lo, s0, s1)


def _head_rms(blk, lo):
    return lax.rsqrt(_pair_sum(blk * blk, lo) * (1.0 / HEAD_DIM) + EPS)


def _head_norm_bwd(dyn, xhat, rr, g, lo):
    a = dyn * g
    return rr * (a - xhat * (_pair_sum(xhat * a, lo) * (1.0 / HEAD_DIM)))


def _fold_heads(acc):
    tot = acc[:, 0:LANES]
    for p in range(1, acc.shape[1] // LANES):
        tot = tot + acc[:, p * LANES:(p + 1) * LANES]
    return tot + pltpu.roll(tot, HEAD_DIM, axis=1)


def _lane_pick(v, lane, idx):
    return jnp.sum(jnp.where(lane == idx, v, 0.0), axis=-1, keepdims=True)


NT = (((1,), (1,)), ((), ()))
TN = (((0,), (0,)), ((), ()))


def _dot(a, b, dims=None):
    if dims is None:
        return jnp.dot(a, b, preferred_element_type=F32)
    return lax.dot_general(a, b, dims, preferred_element_type=F32)


def _my_place():
    return lax.axis_index("x"), lax.axis_index("y"), lax.axis_index("c")


def _half_dims(shape, axis):
    return (shape[0] // 2, shape[1]) if axis == 0 else (shape[0], shape[1] // 2)


def _half_of(ref, axis, core, lead=False):
    rows, cols = ref.shape[-2:]
    if axis == 0:
        idx = (pl.ds(pl.multiple_of(core * (rows // 2), 16), rows // 2), slice(None))
    else:
        idx = (slice(None), pl.ds(pl.multiple_of(core * (cols // 2), LANES), cols // 2))
    return ref.at[(slice(None),) + idx] if lead else ref.at[idx]


class _HalfGather:
    def __init__(self, ins, outs, axes, f32_bufs, bf_bufs, send_sems, recv_sems, local_sems):
        self.ins, self.outs, self.axes = ins, outs, axes
        self.f32_bufs, self.bf_bufs = f32_bufs, bf_bufs
        self.send_sems, self.recv_sems, self.local_sems = send_sems, recv_sems, local_sems
        self.n = len(ins)
        x, y, self.c = _my_place()
        self.me, self.sibling = (x, y, self.c), (x, y, 1 - self.c)
        self.chips = [(1 - x, y), (x, 1 - y), (1 - x, 1 - y)]

    @staticmethod
    def scratch(shards, axes):
        dims = [_half_dims(a.shape, axis) for a, axis in zip(shards, axes)]
        n = len(shards)
        return [pltpu.VMEM(d, F32) for d in dims] + [pltpu.VMEM(d, BF16) for d in dims] + [
            pltpu.SemaphoreType.DMA((7 * n,)), pltpu.SemaphoreType.DMA((7 * n,)), pltpu.SemaphoreType.DMA((2 * n,))]

    @staticmethod
    def out_shapes(shards, axes):
        return tuple(jax.ShapeDtypeStruct((8,) + _half_dims(a.shape, axis), BF16) for a, axis in zip(shards, axes))

    def _blk(self, a, px, py, pc):
        return self.outs[a].at[4 * px + 2 * py + pc]

    def _copy(self, a, k, block, to, src=None):
        return pltpu.make_async_remote_copy(
            src_ref=self._blk(a, *block) if src is None else src, dst_ref=self._blk(a, *block),
            send_sem=self.send_sems.at[7 * a + k], recv_sem=self.recv_sems.at[7 * a + k], device_id=to,
            device_id_type=MESH)

    def _keep(self, a):
        return pltpu.make_async_copy(self.bf_bufs[a], self._blk(a, *self.me), self.local_sems.at[self.n + a])

    def _first(self, a):
        mine = [self._copy(a, 0, self.me, self.sibling, src=self.bf_bufs[a])]
        return mine + [self._copy(a, 1 + j, self.me, (*chip, self.c), src=self.bf_bufs[a])
                       for j, chip in enumerate(self.chips)]

    def send_mine(self):
        loads = [pltpu.make_async_copy(_half_of(self.ins[a], self.axes[a], self.c), self.f32_bufs[a],
                                       self.local_sems.at[a]) for a in range(self.n)]
        for cp in loads:
            cp.start()
        for a in range(self.n):
            loads[a].wait()
            self.bf_bufs[a][...] = self.f32_bufs[a][...].astype(BF16)
            self._keep(a).start()
            for cp in self._first(a):
                cp.start()

    def pass_on(self):
        for a in range(self.n):
            for j, chip in enumerate(self.chips):
                self._copy(a, 1 + j, (*chip, self.c), self.me).wait_recv()
                self._copy(a, 4 + j, (*chip, self.c), self.sibling).start()

    def finish(self):
        for a in range(self.n):
            self._copy(a, 0, self.sibling, self.me).wait_recv()
            for j, chip in enumerate(self.chips):
                self._copy(a, 4 + j, (*chip, 1 - self.c), self.me).wait_recv()
        for a in range(self.n):
            for cp in self._first(a):
                cp.wait_send()
            for j, chip in enumerate(self.chips):
                self._copy(a, 4 + j, (*chip, self.c), self.sibling).wait_send()
            self._keep(a).wait()


def _all_gather_weights(shards, axes):
    n = len(shards)

    def body(*refs):
        gather = _HalfGather(refs[0:n], refs[n:2 * n], axes, refs[2 * n:3 * n], refs[3 * n:4 * n], *refs[4 * n:])
        gather.send_mine()
        gather.pass_on()
        gather.finish()

    any_spec = pl.BlockSpec(memory_space=pl.ANY)
    return pl.pallas_call(
        body, name="weights_all_gather", out_shape=_HalfGather.out_shapes(shards, axes),
        in_specs=[any_spec] * n, out_specs=(any_spec,) * n, scratch_shapes=_HalfGather.scratch(shards, axes),
        compiler_params=pltpu.CompilerParams(vmem_limit_bytes=VMEM_LIMIT),
    )(*shards)


class _ShardReduce:
    SEMS = 8
    LOCAL = 5

    def __init__(self, g_refs, out_refs, axes, bufs, send_sems, recv_sems, local_sems):
        self.g_refs, self.out_refs, self.axes = g_refs, out_refs, axes
        self.recv_a, self.own_a, self.send_b, self.recv_b, self.fin = bufs
        self.send_sems, self.recv_sems, self.local_sems = send_sems, recv_sems, local_sems
        self.n = len(g_refs)
        x, y, self.c = _my_place()
        self.chip = 2 * x + y
        self.sibling = (x, y, 1 - self.c)

    @staticmethod
    def scratch(gparts, axes):
        dims = [_half_dims(g.shape[1:], axis) for g, axis in zip(gparts, axes)]
        shapes = []
        for dtype, lead in ((F32, (4,)), (F32, (4,)), (BF16, (4,)), (BF16, (4,)), (F32, ())):
            shapes += [pltpu.VMEM(lead + d, dtype) for d in dims]
        return shapes

    def _to_sibling(self, a, j):
        return pltpu.make_async_remote_copy(
            src_ref=_half_of(self.g_refs[a].at[j], self.axes[a], 1 - self.c), dst_ref=self.recv_a[a].at[j],
            send_sem=self.send_sems.at[self.SEMS * a + j], recv_sem=self.recv_sems.at[self.SEMS * a + j], device_id=self.sibling,
            device_id_type=MESH)

    def _own(self, a, j):
        return pltpu.make_async_copy(_half_of(self.g_refs[a].at[j], self.axes[a], self.c), self.own_a[a].at[j],
                                     self.local_sems.at[self.LOCAL * a + j])

    def _to_chip(self, a, k):
        dest = (self.chip + k) % 4
        return pltpu.make_async_remote_copy(
            src_ref=self.send_b[a].at[dest], dst_ref=self.recv_b[a].at[self.chip],
            send_sem=self.send_sems.at[self.SEMS * a + 3 + k], recv_sem=self.recv_sems.at[self.SEMS * a + 3 + k],
            device_id=(dest // 2, dest % 2, self.c), device_id_type=MESH)

    def _give(self, a):
        return pltpu.make_async_remote_copy(
            src_ref=self.fin[a], dst_ref=_half_of(self.out_refs[a], self.axes[a], self.c),
            send_sem=self.send_sems.at[self.SEMS * a + 7], recv_sem=self.recv_sems.at[self.SEMS * a + 7], device_id=self.sibling,
            device_id_type=MESH)

    def _mine(self, a):
        return pltpu.make_async_copy(self.fin[a], _half_of(self.out_refs[a], self.axes[a], self.c),
                                     self.local_sems.at[self.LOCAL * a])

    def exchange_with_sibling(self):
        for k in (1, 2, 3, 0):
            j = (self.chip + k) % 4
            for a in range(self.n):
                self._to_sibling(a, j).start()
                self._own(a, j).start()

    def _chip_partial(self, a, j):
        self._own(a, j).wait()
        self._to_sibling(a, j).wait_recv()
        self.send_b[a][j] = (self.own_a[a][j] + self.recv_a[a][j]).astype(BF16)

    def send_to_chip(self, k):
        for a in range(self.n):
            self._chip_partial(a, (self.chip + k) % 4)
            self._to_chip(a, k).start()

    def keep_mine(self):
        for a in range(self.n):
            self._chip_partial(a, self.chip)
            keep = pltpu.make_async_copy(self.send_b[a].at[self.chip], self.recv_b[a].at[self.chip],
                                         self.local_sems.at[self.LOCAL * a + 4])
            keep.start()
            keep.wait()

    def sum_and_share(self):
        for a in range(self.n):
            for k in range(1, 4):
                self._to_chip(a, k).wait_recv()
            tot = self.recv_b[a][0].astype(F32) + self.recv_b[a][1].astype(F32)
            tot = tot + self.recv_b[a][2].astype(F32)
            self.fin[a][...] = tot + self.recv_b[a][3].astype(F32)
            self._give(a).start()
            self._mine(a).start()

    def finish(self):
        for a in range(self.n):
            self._give(a).wait_recv()
            self._mine(a).wait()
            self._give(a).wait_send()
            for j in range(4):
                self._to_sibling(a, j).wait_send()
            for k in range(1, 4):
                self._to_chip(a, k).wait_send()


def _mem_tokens_fwd(mem_ref, g_ref, w_ref, kg_ref, mn_ref, kv_ref, kn_ref, vm_ref):
    xm = mem_ref[...]
    rr = lax.rsqrt(jnp.mean(xm * xm, axis=-1, keepdims=True) + EPS)
    mnb = ((xm * rr) * g_ref[...]).astype(BF16)
    mn_ref[...] = mnb
    kv = _dot(mnb, w_ref[...])
    kv_ref[...] = kv
    lo = _lane_lo((xm.shape[0], LANES))
    for p in range(MEM_WIDTH // LANES):
        sl = slice(p * LANES, (p + 1) * LANES)
        kb = kv[:, sl]
        kn_ref[:, sl] = ((kb * _head_rms(kb, lo)) * kg_ref[:, sl]).astype(BF16)
    vm_ref[...] = kv[:, MEM_WIDTH:].astype(BF16)


AUG_LO = 64
KEY_SUM_LANE = 72
QUERY_SUM_LANE = 80
HEAD_BLOCKS = FOX_HEADS * LANES


def _ones3(lane):
    return jnp.where((lane >= AUG_LO) & (lane < AUG_LO + 3), 1.0, 0.0)


def _spread3(cols):
    hi = cols.astype(BF16)
    rest = cols - hi.astype(F32)
    mid = rest.astype(BF16)
    low = (rest - mid.astype(F32)).astype(BF16)
    r = lax.broadcasted_iota(jnp.int32, (LANES, HEAD_BLOCKS), 0)
    c = lax.broadcasted_iota(jnp.int32, (LANES, HEAD_BLOCKS), 1)
    out = None
    for k, part in enumerate((hi, mid, low)):
        term = _dot(part, jnp.where(c == r * LANES + (AUG_LO + k), 1.0, 0.0).astype(BF16))
        out = term if out is None else out + term
    return out


def _head_block(pair_blk, hh, lo, extras):
    src = pair_blk if hh == 0 else pltpu.roll(pair_blk, HEAD_DIM, axis=1)
    return jnp.where(lo, src, extras).astype(BF16)


def _pair_block(blk0, blk1, lo):
    return jnp.where(lo, blk0, pltpu.roll(blk1, HEAD_DIM, axis=1))


def _assemble_w_in(halves_ref, words_ref, wp_ref):
    shard = IN_WIDTH // 4
    half = D_MODEL // 2
    f_hi = F_ORIG_LO + FOX_HEADS
    for j in range(4):
        blocks = [pltpu.bitcast(halves_ref[2 * j + c], jnp.uint32) for c in range(2)]
        for lo, hi, to in ((0, F_ORIG_LO, PA_LO), (F_ORIG_LO, f_hi, FB_LO), (f_hi, IN_WIDTH, GB_LO)):
            a, b = max(lo, shard * j), min(hi, shard * (j + 1))
            if a < b:
                for c in range(2):
                    words_ref[(to + a - lo) // 2:(to + b - lo) // 2, c * half:(c + 1) * half] = (
                        blocks[c][(a - shard * j) // 2:(b - shard * j) // 2, :])
    pad_lo = (FB_LO + FOX_HEADS) // 2
    words_ref[pad_lo:, :] = jnp.zeros((PROJ_PAD // 2 - pad_lo, D_MODEL), jnp.uint32)
    wp_ref[...] = pltpu.bitcast(words_ref[...], BF16)


def _fwd_in(x, norm_g, halves, bf_pad, fq_g, fk_g):
    s = x.shape[0]
    t = TILE
    n = s // t

    def body(x_ref, ng_ref, halves_ref, bf_ref, qg_ref, kg_ref,
             h_ref, pa_ref, qk_ref, qa_ref, ka_ref, va_ref, gb_ref, pm_ref, fb_ref, wp_ref,
             carry_ref, fcol_ref, words_ref):
        @pl.when(pl.program_id(0) == 0)
        def _():
            carry_ref[...] = jnp.zeros_like(carry_ref)
            _assemble_w_in(halves_ref, words_ref, wp_ref)

        xv = x_ref[...]
        rr = lax.rsqrt(jnp.mean(xv * xv, axis=-1, keepdims=True) + EPS)
        hb = ((xv * rr) * ng_ref[...]).astype(BF16)
        h_ref[...] = hb

        def proj(lo, hi):
            return _dot(hb, wp_ref[lo:hi, :], NT)

        fb = proj(FB_LO, PROJ_PAD)
        fb_ref[...] = fb
        qk_ref[:, 0:FOX_WIDTH] = proj(QB_LO, KB_LO)

        lane = lax.broadcasted_iota(jnp.int32, (t, LANES), 1)
        row = lax.broadcasted_iota(jnp.int32, (t, LANES), 0)
        lo = lane < HEAD_DIM
        z = fb + bf_ref[...]
        lf = -(jnp.maximum(-z, 0.0) + jnp.log1p(jnp.exp(-jnp.abs(z))))
        lf = jnp.where(lane < FOX_HEADS, lf, 0.0)
        sh = 1
        while sh < t:
            lf = lf + jnp.where(row >= sh, pltpu.roll(lf, sh, axis=0), 0.0)
            sh *= 2
        fcum = lf + carry_ref[...]
        fcol_ref[...] = fcum
        carry_ref[...] = fcol_ref[t - 1:t, :]

        ones3 = _ones3(lane)
        minus_f = _spread3(-fcum)

        def head_blocks(seg, g_ref, out_ref, scale):
            for p in range(FOX_WIDTH // LANES):
                sl = slice(p * LANES, (p + 1) * LANES)
                blk = qk_ref[:, seg - QB_LO + p * LANES:seg - QB_LO + (p + 1) * LANES]
                normed = ((blk * _head_rms(blk, lo)) * g_ref[:, sl]) * scale
                for hh in range(2):
                    h = 2 * p + hh
                    if seg == QB_LO:
                        extras = jnp.where(lane == QUERY_SUM_LANE + h, 1.0, ones3)
                    else:
                        extras = jnp.where(lane == KEY_SUM_LANE + h, 1.0, minus_f[:, h * LANES:(h + 1) * LANES])
                    out_ref[:, h * LANES:(h + 1) * LANES] = _head_block(normed, hh, lo, extras)

        qk_ref[:, FOX_WIDTH:2 * FOX_WIDTH] = proj(KB_LO, VB_LO)
        pa_ref[...] = proj(PA_LO, QB_LO)
        head_blocks(QB_LO, qg_ref, qa_ref, ATT_SCALE)
        vraw = proj(VB_LO, GB_LO)
        gb_ref[...] = proj(GB_LO, PM_LO)
        head_blocks(KB_LO, kg_ref, ka_ref, 1.0)
        pm_ref[...] = proj(PM_LO, FB_LO)
        for h in range(FOX_HEADS):
            va_ref[:, h * LANES:(h + 1) * LANES] = _head_block(vraw[:, (h // 2) * LANES:(h // 2 + 1) * LANES], h % 2, lo, ones3)

    outs = (
        jax.ShapeDtypeStruct((s, D_MODEL), BF16),
        jax.ShapeDtypeStruct((s, 512), F32),
        jax.ShapeDtypeStruct((s, 2 * FOX_WIDTH), F32),
        jax.ShapeDtypeStruct((s, HEAD_BLOCKS), BF16),
        jax.ShapeDtypeStruct((s, HEAD_BLOCKS), BF16),
        jax.ShapeDtypeStruct((s, HEAD_BLOCKS), BF16),
        jax.ShapeDtypeStruct((s, FOX_WIDTH), F32),
        jax.ShapeDtypeStruct((s, 512), F32),
        jax.ShapeDtypeStruct((s, LANES), F32),
        jax.ShapeDtypeStruct((PROJ_PAD, D_MODEL), BF16),
    )

    def resident(shape):
        return pl.BlockSpec(shape, lambda i: (0,) * len(shape), pipeline_mode=pl.Buffered(1))

    *fwd, wp = pl.pallas_call(
        body, name="fwd_in", grid=(n,), out_shape=outs,
        in_specs=[_rows(t, D_MODEL), _full((1, D_MODEL)), resident(halves.shape), _full((1, LANES)),
                  _full((1, FOX_WIDTH)), _full((1, FOX_WIDTH))],
        out_specs=(_rows(t, D_MODEL), _rows(t, 512), _rows(t, 2 * FOX_WIDTH), _rows(t, HEAD_BLOCKS),
                   _rows(t, HEAD_BLOCKS), _rows(t, HEAD_BLOCKS), _rows(t, FOX_WIDTH), _rows(t, 512),
                   _rows(t, LANES), resident((PROJ_PAD, D_MODEL))),
        scratch_shapes=[pltpu.VMEM((1, LANES), F32), pltpu.VMEM((t, LANES), F32),
                        pltpu.VMEM((PROJ_PAD // 2, D_MODEL), jnp.uint32)],
        compiler_params=_params(),
    )(x, norm_g, halves, bf_pad, fq_g, fk_g)
    return tuple(fwd), wp


POOL_HALO = 16


def _pool_window(lane):
    return jnp.where(lane < 64, 2.0, jnp.where(lane < 128, 4.0, jnp.where(lane < 192, 8.0, 16.0)))


def _pool_pick(lane, s2, s4, s8, s16):
    return jnp.where(lane < 64, s2, jnp.where(lane < 128, s4, jnp.where(lane < 192, s8, s16)))


def _group_onehot(shape, row_is_group_lane):
    r = lax.broadcasted_iota(jnp.int32, shape, 0)
    c = lax.broadcasted_iota(jnp.int32, shape, 1)
    hit = (r % HEAD_DIM == c) if row_is_group_lane else (c % HEAD_DIM == r)
    return jnp.where(hit, 1.0, 0.0).astype(F32)


def _same_group(shape):
    r = lax.broadcasted_iota(jnp.int32, shape, 0)
    c = lax.broadcasted_iota(jnp.int32, shape, 1)
    return (r // HEAD_DIM) == (c // HEAD_DIM)


def _pool_block_diag(w4):
    spread = jnp.dot(w4, _group_onehot((HEAD_DIM, POOL_WIDTH), False), preferred_element_type=F32,
                     precision=lax.Precision.HIGHEST)
    return jnp.where(_same_group((POOL_WIDTH, POOL_WIDTH)), spread, 0.0).astype(BF16)


def _mem_softmax(qm, kp):
    sc = _dot(qm, kp, NT)
    e = jnp.exp(sc - jnp.max(sc, axis=-1, keepdims=True))
    return e * (1.0 / jnp.sum(e, axis=-1, keepdims=True))


def _side_fwd(pa, pm, w4, pscale, mq_g, mem, mem_norm_g, w_kv, mk_g):
    s = pa.shape[0]
    t = TILE
    n = s // t
    ext = t + POOL_HALO
    nm = mem.shape[0]

    def body(pa_ref, pm_ref, w4_ref, sc_ref, g_ref, mem_ref, mg_ref, wkv_ref, kg_ref,
             ma_ref, d_ref, mm_ref, mn_ref, kv_ref, k_ref, v_ref, ext_ref, w_ref):
        i = pl.program_id(0)

        @pl.when(i == 0)
        def _():
            ext_ref[0:POOL_HALO, :] = jnp.zeros((POOL_HALO, POOL_WIDTH), F32)
            w_ref[...] = _pool_block_diag(w4_ref[...])
            _mem_tokens_fwd(mem_ref, mg_ref, wkv_ref, kg_ref, mn_ref, kv_ref, k_ref, v_ref)

        u = pa_ref[:, 0:POOL_WIDTH]
        ext_ref[POOL_HALO:ext, :] = u
        e = ext_ref[...]
        s2 = e + pltpu.roll(e, 1, axis=0)
        s4 = s2 + pltpu.roll(s2, 2, axis=0)
        s8 = s4 + pltpu.roll(s4, 4, axis=0)
        s16 = s8 + pltpu.roll(s8, 8, axis=0)
        lane_e = lax.broadcasted_iota(jnp.int32, (ext, POOL_WIDTH), 1)
        win = _pool_pick(lane_e, s2, s4, s8, s16)[POOL_HALO:ext, :]
        lane = lax.broadcasted_iota(jnp.int32, (t, POOL_WIDTH), 1)
        pos = (lax.broadcasted_iota(jnp.int32, (t, POOL_WIDTH), 0) + (i * t + 1)).astype(F32)
        d = win / jnp.minimum(pos, _pool_window(lane)) - u
        db = d.astype(BF16)
        d_ref[...] = db
        ya = _dot(db, w_ref[...]) * sc_ref[...]
        ga = pa_ref[:, POOL_WIDTH:2 * POOL_WIDTH]
        ma_ref[...] = (ya * (ga * _sig(ga))).astype(BF16)
        ext_ref[0:POOL_HALO, :] = ext_ref[t:ext, :]

        lo = _lane_lo((t, LANES))
        for p in range(MEM_WIDTH // LANES):
            sl = slice(p * LANES, (p + 1) * LANES)
            qb = pm_ref[:, sl]
            qs = (((qb * _head_rms(qb, lo)) * g_ref[:, sl]) * ATT_SCALE).astype(BF16)
            kp = k_ref[:, sl]
            vp = v_ref[:, sl]
            outs = []
            for hh in range(2):
                msk = lo if hh == 0 else jnp.logical_not(lo)
                prob = _mem_softmax(jnp.where(msk, qs, jnp.zeros_like(qs)), kp)
                outs.append(_dot(prob.astype(BF16), vp))
            o = jnp.where(lo, outs[0], outs[1])
            gm = pm_ref[:, MEM_WIDTH + p * LANES:MEM_WIDTH + (p + 1) * LANES]
            mm_ref[:, sl] = (o * (gm * _sig(gm))).astype(BF16)

    return pl.pallas_call(
        body, name="side_fwd", grid=(n,),
        out_shape=(jax.ShapeDtypeStruct((s, POOL_WIDTH), BF16), jax.ShapeDtypeStruct((s, POOL_WIDTH), BF16),
                   jax.ShapeDtypeStruct((s, MEM_WIDTH), BF16), jax.ShapeDtypeStruct((nm, D_MODEL), BF16),
                   jax.ShapeDtypeStruct((nm, 2 * MEM_WIDTH), F32), jax.ShapeDtypeStruct((nm, MEM_WIDTH), BF16),
                   jax.ShapeDtypeStruct((nm, MEM_WIDTH), BF16)),
        in_specs=[_rows(t, 512), _rows(t, 512), _full((POOL_ROWS, HEAD_DIM)), _full((1, POOL_WIDTH)),
                  _full((1, MEM_WIDTH)), _full((nm, D_MODEL)), _full((1, D_MODEL)), _full((D_MODEL, 2 * MEM_WIDTH)),
                  _full((1, MEM_WIDTH))],
        out_specs=(_rows(t, POOL_WIDTH), _rows(t, POOL_WIDTH), _rows(t, MEM_WIDTH), _full((nm, D_MODEL)),
                   _full((nm, 2 * MEM_WIDTH)), _full((nm, MEM_WIDTH)), _full((nm, MEM_WIDTH))),
        scratch_shapes=[pltpu.VMEM((ext, POOL_WIDTH), F32), pltpu.VMEM((POOL_WIDTH, POOL_WIDTH), BF16)],
        compiler_params=_params(),
    )(pa, pm, w4, pscale, mq_g, mem, mem_norm_g, w_kv, mk_g)


FOX_FWD_HEADS = 4


def _fox_fwd(qa, ka, va, gb):
    s = qa.shape[0]
    t = TILE
    n = s // t
    heads = FOX_FWD_HEADS
    pairs = heads // 2
    group_w = heads * LANES

    def body(qa_ref, ka_ref, va_ref, gb_ref, o_ref, mb_ref, r_ref):
        i = pl.program_id(1)
        lane = lax.broadcasted_iota(jnp.int32, (t, LANES), 1)
        lo = lane < HEAD_DIM
        causal = lax.broadcasted_iota(jnp.int32, (t, t), 1) <= lax.broadcasted_iota(jnp.int32, (t, t), 0)
        qas = [qa_ref[:, hh * LANES:(hh + 1) * LANES] for hh in range(heads)]

        def step(j, carry, masked):
            rows = pl.ds(pl.multiple_of(j * t, t), t)
            def logits(hh):
                sc = _dot(qas[hh], ka_ref[rows, hh * LANES:(hh + 1) * LANES], NT)
                return jnp.where(causal, sc, -1e30) if masked else sc

            def advance(hh, sc):
                m, acc = carry[hh]
                m_new = jnp.maximum(m, jnp.max(sc, axis=-1, keepdims=True))
                p = jnp.exp(sc - m_new).astype(BF16)
                return m_new, jnp.exp(m - m_new) * acc + _dot(p, va_ref[rows, hh * LANES:(hh + 1) * LANES])

            new = []
            sc = logits(0)
            for hh in range(heads):
                sc_next = logits(hh + 1) if hh + 1 < heads else None
                new.append(advance(hh, sc))
                sc = sc_next
            return tuple(new)

        init = (jnp.full((t, 1), -1e30, F32), jnp.zeros((t, LANES), F32))
        carry = lax.fori_loop(0, i, functools.partial(step, masked=False), (init,) * heads)
        res = step(i, carry, masked=True)
        for p in range(pairs):
            outs = []
            rcol = jnp.zeros((t, LANES), F32)
            for hh in range(2):
                m, acc = res[2 * p + hh]
                l = _lane_pick(acc, lane, AUG_LO)
                outs.append(acc * (1.0 / l))
                rcol = jnp.where(lane == hh, m + jnp.log(l), rcol)
            o = _pair_block(outs[0], outs[1], lo)
            sl = slice(p * LANES, (p + 1) * LANES)
            o_ref[:, sl] = o
            g = gb_ref[:, sl]
            mb_ref[:, sl] = (o * (g * _sig(g))).astype(BF16)
            r_ref[p] = rcol

    tile_spec = pl.BlockSpec((t, pairs * LANES), lambda p, i: (i, p))
    full_spec = pl.BlockSpec((s, group_w), lambda p, i: (0, p))
    return pl.pallas_call(
        body, name="fox_fwd", grid=(FOX_HEADS // heads, n),
        out_shape=(jax.ShapeDtypeStruct((s, FOX_WIDTH), F32), jax.ShapeDtypeStruct((s, FOX_WIDTH), BF16),
                   jax.ShapeDtypeStruct((FOX_HEADS // 2, s, LANES), F32)),
        in_specs=[pl.BlockSpec((t, group_w), lambda p, i: (i, p)), full_spec, full_spec, tile_spec],
        out_specs=(tile_spec, tile_spec, pl.BlockSpec((pairs, t, LANES), lambda p, i: (p, i, 0))),
        compiler_params=_params(2),
    )(qa, ka, va, gb)


def _out_loss(x, tgt, ma, mb, mm, wout, gb, o, r4):
    s = x.shape[0]
    t = TILE
    n = s // t
    pairs = FOX_HEADS // 2

    def body(x_ref, t_ref, ma_ref, mb_ref, mm_ref, w_ref, gb_ref, o_ref, r_ref,
             dy_ref, dma_ref, dmm_ref, dw_ref, loss_ref, doa_ref, dgb_ref, rr_ref, mix_ref):
        @pl.when(pl.program_id(0) == 0)
        def _():
            dw_ref[...] = jnp.zeros_like(dw_ref)
            loss_ref[...] = jnp.zeros_like(loss_ref)

        mix_ref[:, 0:256] = ma_ref[...]
        mix_ref[:, 256:768] = mb_ref[...]
        mix_ref[:, 768:1024] = mm_ref[...]
        mix = mix_ref[...]
        err = (x_ref[...] + _dot(mix, w_ref[...])) - t_ref[...]
        row_mean = jnp.sum(err * err, axis=-1, keepdims=True) * (1.0 / D_MODEL)
        loss_ref[...] += 0.5 * jnp.sum(row_mean, axis=0, keepdims=True)
        dy = err * (1.0 / D_MODEL)
        dy_ref[...] = dy
        dyb = dy.astype(BF16)
        dmix = _dot(dyb, w_ref[...], NT)
        dma_ref[...] = dmix[:, 0:256]
        dmm_ref[...] = dmix[:, 768:1024]
        dw_ref[...] += _dot(mix, dyb, TN)

        lane = lax.broadcasted_iota(jnp.int32, (t, LANES), 1)
        lo = lane < HEAD_DIM
        d_os = []
        delta = jnp.zeros((t, LANES), F32)
        for p in range(pairs):
            sl = slice(p * LANES, (p + 1) * LANES)
            g = gb_ref[:, sl]
            sg = _sig(g)
            dm = dmix[:, 256 + p * LANES:256 + (p + 1) * LANES]
            ov = o_ref[:, sl]
            d_o = dm * (g * sg)
            d_os.append(d_o)
            dgb_ref[:, sl] = (dm * ov * (sg * (1.0 + g * (1.0 - sg)))).astype(BF16)
            prod = d_o * ov
            delta = jnp.where(lane == 2 * p, jnp.sum(jnp.where(lo, prod, 0.0), axis=-1, keepdims=True), delta)
            delta = jnp.where(lane == 2 * p + 1, jnp.sum(jnp.where(lo, 0.0, prod), axis=-1, keepdims=True), delta)
            rr_ref[p, 0] = r_ref[p].T[0:8, :]
        minus_delta = _spread3(-delta)
        for h in range(FOX_HEADS):
            blk = slice(h * LANES, (h + 1) * LANES)
            doa_ref[:, blk] = _head_block(d_os[h // 2], h % 2, lo, minus_delta[:, blk])

    return pl.pallas_call(
        body, name="out_loss", grid=(n,),
        out_shape=(jax.ShapeDtypeStruct((s, D_MODEL), F32), jax.ShapeDtypeStruct((s, 256), F32),
                   jax.ShapeDtypeStruct((s, 256), F32), jax.ShapeDtypeStruct((D_MODEL, D_MODEL), F32),
                   jax.ShapeDtypeStruct((1, LANES), F32), jax.ShapeDtypeStruct((s, HEAD_BLOCKS), BF16),
                   jax.ShapeDtypeStruct((s, FOX_WIDTH), BF16), jax.ShapeDtypeStruct((pairs, n, 8, t), F32)),
        in_specs=[_rows(t, D_MODEL), _rows(t, D_MODEL), _rows(t, 256), _rows(t, 512), _rows(t, 256),
                  _full((D_MODEL, D_MODEL)), _rows(t, FOX_WIDTH), _rows(t, FOX_WIDTH),
                  pl.BlockSpec((pairs, t, LANES), lambda i: (0, i, 0))],
        out_specs=(_rows(t, D_MODEL), _rows(t, 256), _rows(t, 256), _full((D_MODEL, D_MODEL)), _full((1, LANES)),
                   _rows(t, HEAD_BLOCKS), _rows(t, FOX_WIDTH), pl.BlockSpec((pairs, 1, 8, t), lambda i: (0, i, 0, 0))),
        scratch_shapes=[pltpu.VMEM((t, D_MODEL), BF16)],
        compiler_params=_params(),
    )(x, tgt, ma, mb, mm, wout, gb, o, r4)


def _side_bwd(pa, db, dma, w4, pscale, pm, dmm, kmn, vmb, mq_g, kv, mnb, mem, w_kv, mk_g, mem_norm_g):
    s = pa.shape[0]
    t = TILE
    n = s // t
    ext = t + POOL_HALO
    nm = mem.shape[0]

    def body(pa_ref, d_ref, dma_ref, w4_ref, sc_ref, pm_ref, dmm_ref, k_ref, v_ref, g_ref,
             kv_ref, mn_ref, mem_ref, wkv_ref, kg_ref, mg_ref,
             dpa_ref, dpm_ref, dw4_ref, dsc_ref, dg_ref, dwkv_ref, dmg_ref, dkg_ref,
             ext_ref, w_ref, dw_ref, dk_ref, dv_ref, gacc_ref, dkv_ref):
        i = pl.program_id(0)

        @pl.when(i == 0)
        def _():
            dw_ref[...] = jnp.zeros_like(dw_ref)
            dsc_ref[...] = jnp.zeros_like(dsc_ref)
            ext_ref[t:ext, :] = jnp.zeros((POOL_HALO, POOL_WIDTH), F32)
            w_ref[...] = _pool_block_diag(w4_ref[...])
            dk_ref[...] = jnp.zeros_like(dk_ref)
            dv_ref[...] = jnp.zeros_like(dv_ref)
            gacc_ref[...] = jnp.zeros_like(gacc_ref)

        dbv = d_ref[...]
        z = _dot(dbv, w_ref[...])
        ga = pa_ref[:, POOL_WIDTH:2 * POOL_WIDTH]
        sg = _sig(ga)
        dma_v = dma_ref[...]
        dya = dma_v * (ga * sg)
        dpa_ref[:, POOL_WIDTH:2 * POOL_WIDTH] = (dma_v * (z * sc_ref[...]) * (sg * (1.0 + ga * (1.0 - sg)))).astype(BF16)
        dsc_ref[...] += jnp.sum(dya * z, axis=0, keepdims=True)
        dzb = (dya * sc_ref[...]).astype(BF16)
        dw_ref[...] += _dot(dbv, dzb, TN)
        dd = _dot(dzb, w_ref[...], NT)
        lane = lax.broadcasted_iota(jnp.int32, (t, POOL_WIDTH), 1)
        pos = (lax.broadcasted_iota(jnp.int32, (t, POOL_WIDTH), 0) + ((n - 1 - i) * t + 1)).astype(F32)
        ext_ref[0:t, :] = dd / jnp.minimum(pos, _pool_window(lane))
        e = ext_ref[...]
        s2 = e + pltpu.roll(e, ext - 1, axis=0)
        s4 = s2 + pltpu.roll(s2, ext - 2, axis=0)
        s8 = s4 + pltpu.roll(s4, ext - 4, axis=0)
        s16 = s8 + pltpu.roll(s8, ext - 8, axis=0)
        lane_e = lax.broadcasted_iota(jnp.int32, (ext, POOL_WIDTH), 1)
        win = _pool_pick(lane_e, s2, s4, s8, s16)[0:t, :]
        dpa_ref[:, 0:POOL_WIDTH] = (win - dd).astype(BF16)
        ext_ref[t:ext, :] = ext_ref[0:POOL_HALO, :]

        lo = _lane_lo((t, LANES))
        for p in range(MEM_WIDTH // LANES):
            sl = slice(p * LANES, (p + 1) * LANES)
            qb = pm_ref[:, sl]
            rr = _head_rms(qb, lo)
            qhat = qb * rr
            g = g_ref[:, sl]
            qs = ((qhat * g) * ATT_SCALE).astype(BF16)
            gm = pm_ref[:, MEM_WIDTH + p * LANES:MEM_WIDTH + (p + 1) * LANES]
            sg = _sig(gm)
            dmo = dmm_ref[:, sl]
            d_o = dmo * (gm * sg)
            kp = k_ref[:, sl]
            vp = v_ref[:, sl]
            outs, dqs = [], []
            for hh in range(2):
                msk = lo if hh == 0 else jnp.logical_not(lo)
                qm = jnp.where(msk, qs, jnp.zeros_like(qs))
                prob = _mem_softmax(qm, kp)
                pb = prob.astype(BF16)
                outs.append(_dot(pb, vp))
                dom = jnp.where(msk, d_o, 0.0).astype(BF16)
                dp = _dot(dom, vp, NT)
                ds = (prob * (dp - jnp.sum(prob * dp, axis=-1, keepdims=True))).astype(BF16)
                dqs.append(_dot(ds, kp))
                dk_ref[:, sl] += _dot(ds, qm, TN)
                dv_ref[:, sl] += _dot(pb, dom, TN)
            o = jnp.where(lo, outs[0], outs[1])
            dqn = jnp.where(lo, dqs[0], dqs[1]) * ATT_SCALE
            dpm_ref[:, sl] = _head_norm_bwd(dqn, qhat, rr, g, lo).astype(BF16)
            dpm_ref[:, MEM_WIDTH + p * LANES:MEM_WIDTH + (p + 1) * LANES] = (
                dmo * o * (sg * (1.0 + gm * (1.0 - sg)))).astype(BF16)
            gacc_ref[:, sl] += jnp.sum(dqn * qhat, axis=0, keepdims=True)

        @pl.when(i == n - 1)
        def _():
            own = jnp.where(_same_group((POOL_WIDTH, POOL_WIDTH)), dw_ref[...], 0.0)
            dw4_ref[...] = jnp.dot(own, _group_onehot((POOL_WIDTH, HEAD_DIM), True), preferred_element_type=F32,
                                   precision=lax.Precision.HIGHEST)
            dg_ref[...] = _fold_heads(gacc_ref[...])

            lo_m = _lane_lo((nm, LANES))
            kacc = []
            for p in range(MEM_WIDTH // LANES):
                sl = slice(p * LANES, (p + 1) * LANES)
                kb = kv_ref[:, sl]
                rr = _head_rms(kb, lo_m)
                khat = kb * rr
                dk = dk_ref[:, sl]
                dkv_ref[:, sl] = _head_norm_bwd(dk, khat, rr, kg_ref[:, sl], lo_m).astype(BF16)
                kacc.append(jnp.sum(dk * khat, axis=0, keepdims=True))
            dkg_ref[...] = _fold_heads(jnp.concatenate(kacc, axis=1))
            dkv_ref[:, MEM_WIDTH:] = dv_ref[...].astype(BF16)
            dkv = dkv_ref[...]
            dwkv_ref[...] = _dot(mn_ref[...], dkv, TN)
            dmn = _dot(dkv, wkv_ref[...], NT)
            xm = mem_ref[...]
            rr = lax.rsqrt(jnp.mean(xm * xm, axis=-1, keepdims=True) + EPS)
            dmg_ref[...] = jnp.sum(dmn * (xm * rr), axis=0, keepdims=True)

    def rev(w):
        return _rows_rev(t, w, n)

    row = jax.ShapeDtypeStruct((1, LANES), F32)
    return pl.pallas_call(
        body, name="side_bwd", grid=(n,),
        out_shape=(jax.ShapeDtypeStruct((s, 512), BF16), jax.ShapeDtypeStruct((s, 512), BF16),
                   jax.ShapeDtypeStruct((POOL_ROWS, HEAD_DIM), F32), jax.ShapeDtypeStruct((1, POOL_WIDTH), F32), row,
                   jax.ShapeDtypeStruct((D_MODEL, 2 * MEM_WIDTH), F32), jax.ShapeDtypeStruct((1, D_MODEL), F32), row),
        in_specs=[rev(512), rev(POOL_WIDTH), rev(POOL_WIDTH), _full((POOL_ROWS, HEAD_DIM)), _full((1, POOL_WIDTH)),
                  rev(512), rev(MEM_WIDTH), _full((N_MEM, MEM_WIDTH)), _full((N_MEM, MEM_WIDTH)), _full((1, MEM_WIDTH)),
                  _full((nm, 2 * MEM_WIDTH)), _full((nm, D_MODEL)), _full((nm, D_MODEL)),
                  _full((D_MODEL, 2 * MEM_WIDTH)), _full((1, MEM_WIDTH)), _full((1, D_MODEL))],
        out_specs=(rev(512), rev(512), _full((POOL_ROWS, HEAD_DIM)), _full((1, POOL_WIDTH)), _full((1, LANES)),
                   _full((D_MODEL, 2 * MEM_WIDTH)), _full((1, D_MODEL)), _full((1, LANES))),
        scratch_shapes=[pltpu.VMEM((ext, POOL_WIDTH), F32), pltpu.VMEM((POOL_WIDTH, POOL_WIDTH), BF16),
                        pltpu.VMEM((POOL_WIDTH, POOL_WIDTH), F32), pltpu.VMEM((N_MEM, MEM_WIDTH), F32),
                        pltpu.VMEM((N_MEM, MEM_WIDTH), F32), pltpu.VMEM((1, MEM_WIDTH), F32),
                        pltpu.VMEM((nm, 2 * MEM_WIDTH), BF16)],
        compiler_params=_params(),
    )(pa, db, dma, w4, pscale, pm, dmm, kmn, vmb, mq_g, kv, mnb, mem, w_kv, mk_g, mem_norm_g)


FOX_BWD_HEADS = 4


def _fox_bwd(ka, va, qa, doa, rr):
    s = ka.shape[0]
    t = TILE
    n = s // t
    heads = FOX_BWD_HEADS
    group_w = heads * LANES

    def body(ka_ref, va_ref, qa_ref, doa_ref, rr_ref, dka_ref, dva_ref, dqa_ref):
        j = pl.program_id(1)

        @pl.when(j == 0)
        def _():
            dqa_ref[...] = jnp.zeros_like(dqa_ref)

        causal = lax.broadcasted_iota(jnp.int32, (t, t), 0) <= lax.broadcasted_iota(jnp.int32, (t, t), 1)
        kas = [ka_ref[:, hh * LANES:(hh + 1) * LANES] for hh in range(heads)]
        vas = [va_ref[:, hh * LANES:(hh + 1) * LANES] for hh in range(heads)]

        def step(i, carry, masked):
            rows = pl.ds(pl.multiple_of(i * t, t), t)
            new = []
            for hh in range(heads):
                cols = slice(hh * LANES, (hh + 1) * LANES)
                dk_a, dv_a = carry[hh]
                qb = qa_ref[rows, cols]
                d_o = doa_ref[rows, cols]
                arg = _dot(kas[hh], qb, NT) - rr_ref[hh // 2, i, hh % 2:hh % 2 + 1, :]
                if masked:
                    arg = jnp.where(causal, arg, -1e30)
                pt = jnp.exp(arg)
                dst = (pt * _dot(vas[hh], d_o, NT)).astype(BF16)
                dv_a = dv_a + _dot(pt.astype(BF16), d_o)
                dk_a = dk_a + _dot(dst, qb)
                dqa_ref[rows, cols] += _dot(dst, kas[hh], TN)
                new.append((dk_a, dv_a))
            return tuple(new)

        zero = jnp.zeros((t, LANES), F32)
        carry = step(j, ((zero, zero),) * heads, masked=True)
        res = lax.fori_loop(j + 1, n, functools.partial(step, masked=False), carry)
        for hh in range(heads):
            cols = slice(hh * LANES, (hh + 1) * LANES)
            dka_ref[:, cols] = res[hh][0]
            dva_ref[:, cols] = res[hh][1]

    tile_spec = pl.BlockSpec((t, group_w), lambda p, j: (j, p))
    full_spec = pl.BlockSpec((s, group_w), lambda p, j: (0, p))
    return pl.pallas_call(
        body, name="fox_bwd", grid=(FOX_HEADS // heads, n),
        out_shape=(jax.ShapeDtypeStruct((s, HEAD_BLOCKS), F32),) * 3,
        in_specs=[tile_spec, tile_spec, full_spec, full_spec,
                  pl.BlockSpec((heads // 2, n, 8, t), lambda p, j: (p, 0, 0, 0))],
        out_specs=(tile_spec, tile_spec, full_spec),
        compiler_params=_params(2),
    )(ka, va, qa, doa, rr)


def _fox_post_tile(i, n, t, dqa_ref, dka_ref, dva_ref, qk_ref, fb_ref, bf_ref, qg_ref, kg_ref,
                   dqk_ref, dv_ref, dfb_ref, dqg_ref, dkg_ref, dbf_ref, qacc_ref, kacc_ref, carry_ref,
                   between):
    @pl.when(i == 0)
    def _():
        qacc_ref[...] = jnp.zeros_like(qacc_ref)
        kacc_ref[...] = jnp.zeros_like(kacc_ref)
        dbf_ref[...] = jnp.zeros_like(dbf_ref)
        carry_ref[...] = jnp.zeros_like(carry_ref)

    lane = lax.broadcasted_iota(jnp.int32, (t, LANES), 1)
    row = lax.broadcasted_iota(jnp.int32, (t, LANES), 0)
    lo = lane < HEAD_DIM

    def head_blocks(ref, p):
        return ref[:, 2 * p * LANES:(2 * p + 1) * LANES], ref[:, (2 * p + 1) * LANES:(2 * p + 2) * LANES]

    def issue(k):
        if between[k] is not None:
            between[k]()

    sums = []
    pairs = FOX_WIDTH // LANES
    for side, (src_ref, g_ref, acc_ref, scale) in enumerate(((dqa_ref, qg_ref, qacc_ref, ATT_SCALE),
                                                             (dka_ref, kg_ref, kacc_ref, 1.0))):
        total = jnp.zeros((t, LANES), F32)
        for p in range(pairs):
            issue(side * pairs + p)
            sl = slice(p * LANES, (p + 1) * LANES)
            cols = slice(side * FOX_WIDTH + p * LANES, side * FOX_WIDTH + (p + 1) * LANES)
            if side == 0:
                dv_ref[:, sl] = _pair_block(*head_blocks(dva_ref, p), lo).astype(BF16)
            d0, d1 = head_blocks(src_ref, p)
            total = total + (d0 + d1)
            raw = qk_ref[:, cols]
            rr = _head_rms(raw, lo)
            xhat = raw * rr
            dn = _pair_block(d0, d1, lo) * scale
            dqk_ref[:, cols] = _head_norm_bwd(dn, xhat, rr, g_ref[:, sl], lo).astype(BF16)
            acc_ref[:, sl] += jnp.sum(dn * xhat, axis=0, keepdims=True)
        sums.append(total)
    issue(2 * pairs)
    dq_sum, dk_sum = sums

    acc = (pltpu.roll(dq_sum, LANES - KEY_SUM_LANE, axis=1) - pltpu.roll(dk_sum, LANES - QUERY_SUM_LANE, axis=1))
    acc = jnp.where(lane < FOX_HEADS, acc, 0.0)
    sh = 1
    while sh < t:
        acc = acc + jnp.where(row < t - sh, pltpu.roll(acc, t - sh, axis=0), 0.0)
        sh *= 2
    dlogf = acc + carry_ref[...]
    dfb_ref[...] = dlogf
    carry_ref[...] = dfb_ref[0:1, :]
    z = fb_ref[...] + bf_ref[...]
    dz = jnp.where(lane < FOX_HEADS, dlogf * (1.0 / (1.0 + jnp.exp(z))), 0.0)
    dfb_ref[...] = dz
    dbf_ref[...] += jnp.sum(dz, axis=0, keepdims=True)

    @pl.when(i == n - 1)
    def _():
        dqg_ref[...] = _fold_heads(qacc_ref[...])
        dkg_ref[...] = _fold_heads(kacc_ref[...])


def _assemble_dproj(dp_ref, dpa_ref, dqk_ref, dv_ref, dgb_ref, dpm_ref, dfb_ref):
    dp_ref[:, PA_LO:QB_LO] = dpa_ref[...]
    dp_ref[:, QB_LO:VB_LO] = dqk_ref[...]
    dp_ref[:, VB_LO:GB_LO] = dv_ref[...]
    dp_ref[:, GB_LO:PM_LO] = dgb_ref[...]
    dp_ref[:, PM_LO:FB_LO] = dpm_ref[...]
    dp_ref[:, FB_LO:PROJ_PAD] = dfb_ref[...].astype(BF16)


def _dproj_specs(t):
    return [_rows(t, 512), _rows(t, 2 * FOX_WIDTH), _rows(t, FOX_WIDTH), _rows(t, FOX_WIDTH), _rows(t, 512),
            _rows(t, LANES)]


IN_BWD_X_TILE = 256


def _in_bwd_x(x, dy, norm_g, wp, dparts, gparts, axes, smalls):
    s = x.shape[0]
    t = IN_BWD_X_TILE
    n = s // t
    na = len(gparts)
    n_dp = len(dparts)
    vec_leaves, loss_row, dw4 = smalls if smalls is not None else ((), None, None)
    nv = len(vec_leaves)
    n_small = nv + 2 if smalls is not None else 0
    small_base = _ShardReduce.SEMS * na

    def body(*refs):
        x_ref, dy_ref, g_ref, wp_ref = refs[0:4]
        dp_parts = refs[4:4 + n_dp]
        o = 4 + n_dp
        g_refs = refs[o:o + na]
        small_in = refs[o + na:o + na + n_small]
        o += na + n_small
        gx_ref, dg_ref = refs[o:o + 2]
        out_refs = refs[o + 2:o + 2 + na]
        small_out = refs[o + 2 + na:o + 2 + na + (2 if smalls is not None else 0)]
        o += 2 + na + len(small_out)
        dp_ref = refs[o]
        bufs = tuple(refs[o + 1 + k * na:o + 1 + (k + 1) * na] for k in range(5))
        rest = refs[o + 1 + 5 * na:]

        i = pl.program_id(0)
        if na or smalls is not None:
            send_sems, recv_sems, local_sems = rest[-3:]
        red = _ShardReduce(g_refs, out_refs, axes, bufs, send_sems, recv_sems, local_sems) if na else None

        @pl.when(i == 0)
        def _():
            dg_ref[...] = jnp.zeros_like(dg_ref)
            if red is not None:
                red.exchange_with_sibling()

        if red is not None:
            for k in (1, 2, 3):
                pl.when(i == k)(functools.partial(red.send_to_chip, k))
            pl.when(i == 4)(red.keep_mine)

        _assemble_dproj(dp_ref, *dp_parts)
        dh = _dot(dp_ref[...], wp_ref[...])
        xv = x_ref[...]
        rr = lax.rsqrt(jnp.mean(xv * xv, axis=-1, keepdims=True) + EPS)
        xhat = xv * rr
        scaled = dh * g_ref[...]
        gx_ref[...] = dy_ref[...] + rr * (scaled - xhat * jnp.mean(xhat * scaled, axis=-1, keepdims=True))
        dg_ref[...] += jnp.sum(dh * xhat, axis=0, keepdims=True)

        def small_all_reduce():
            leaf_refs, (loss_ref, dw4_ref) = small_in[0:nv], small_in[nv:]
            vec_out, dw4_out = small_out
            vec_mine, vec_recv, dw4_recv = rest[0:3]
            cx, cy, c = _my_place()
            me_lin = 4 * cx + 2 * cy + c

            def copy(k, src, dst, base):
                peer = (me_lin + k) % 8
                return pltpu.make_async_remote_copy(
                    src_ref=src, dst_ref=dst.at[me_lin], send_sem=send_sems.at[base + k - 1],
                    recv_sem=recv_sems.at[base + k - 1], device_id=(peer // 4, (peer // 2) % 2, peer % 2),
                    device_id_type=MESH)

            vec_mine[...] = jnp.zeros_like(vec_mine)
            vec_mine[0:1, :] = dg_ref[...]
            for (_, row, _), ref in zip(VEC_LEAVES[1:], leaf_refs):
                vec_mine[row:row + 1, 0:ref.shape[1]] = ref[...]
            vec_mine[VEC_LOSS_ROW:VEC_LOSS_ROW + 1, 0:LANES] = loss_ref[...]
            copies = [copy(k, src, dst, base) for k in range(1, 8)
                      for src, dst, base in ((vec_mine, vec_recv, small_base), (dw4_ref, dw4_recv, small_base + 7))]
            for cp in copies:
                cp.start()
            for cp in copies:
                cp.wait_recv()
            vec_recv[me_lin] = vec_mine[...]
            dw4_recv[me_lin] = dw4_ref[...]
            vtot, wtot = vec_recv[0], dw4_recv[0]
            for d in range(1, 8):
                vtot = vtot + vec_recv[d]
                wtot = wtot + dw4_recv[d]
            vec_out[...] = vtot
            dw4_out[...] = wtot
            for cp in copies:
                cp.wait_send()

        @pl.when(i == n - 1)
        def _():
            if red is not None:
                red.sum_and_share()
            if smalls is not None:
                small_all_reduce()
            if red is not None:
                red.finish()

    any_spec = pl.BlockSpec(memory_space=pl.ANY)
    scratch = [pltpu.VMEM((t, PROJ_PAD), BF16)] + _ShardReduce.scratch(gparts, axes)
    out_shape = [jax.ShapeDtypeStruct((s, D_MODEL), F32), jax.ShapeDtypeStruct((1, D_MODEL), F32)]
    out_shape += [jax.ShapeDtypeStruct(g.shape[1:], F32) for g in gparts]
    out_specs = [_rows(t, D_MODEL), _full((1, D_MODEL))] + [any_spec] * na
    small_args = []
    if smalls is not None:
        small_args = [*vec_leaves, loss_row, dw4]
        out_shape += [jax.ShapeDtypeStruct((VEC_ROWS, D_MODEL), F32), jax.ShapeDtypeStruct(dw4.shape, F32)]
        out_specs += [_full((VEC_ROWS, D_MODEL)), _full(dw4.shape)]
        scratch += [pltpu.VMEM((VEC_ROWS, D_MODEL), F32), pltpu.VMEM((8, VEC_ROWS, D_MODEL), F32),
                    pltpu.VMEM((8,) + dw4.shape, F32)]
    if na or smalls is not None:
        n_sems = small_base + 14
        scratch += [pltpu.SemaphoreType.DMA((n_sems,)), pltpu.SemaphoreType.DMA((n_sems,)),
                    pltpu.SemaphoreType.DMA((max(_ShardReduce.LOCAL * na, 1),))]
    return pl.pallas_call(
        body, name="in_bwd_x", grid=(n,), out_shape=tuple(out_shape),
        in_specs=[_rows(t, D_MODEL), _rows(t, D_MODEL), _full((1, D_MODEL)),
                  pl.BlockSpec((PROJ_PAD, D_MODEL), lambda i: (0, 0), pipeline_mode=pl.Buffered(1))]
        + _dproj_specs(t) + [any_spec] * na + [_full(a.shape) for a in small_args],
        out_specs=tuple(out_specs), scratch_shapes=scratch, compiler_params=_params(),
    )(x, dy, norm_g, wp, *dparts, *gparts, *small_args)


def _in_bwd_w(hb, dpa, dgb, dpm, fox, gparts, axes):
    s = hb.shape[0]
    t = TILE
    n = s // t
    na = len(gparts)
    f_hi = F_ORIG_LO + FOX_HEADS
    n_in = 4 + len(fox)

    def body(*refs):
        h_ref, dpa_ref, dgb_ref, dpm_ref = refs[0:4]
        fox_refs = refs[4:n_in]
        g_refs = refs[n_in:n_in + na]
        o = n_in + na
        dw_ref, dqk_ref, dv_ref, dfb_ref, dqg_ref, dkg_ref, dbf_ref = refs[o:o + 7]
        out_refs = refs[o + 7:o + 7 + na]
        o += 7 + na
        fox_scratch = refs[o:o + 3]
        bufs = tuple(refs[o + 3 + k * na:o + 3 + (k + 1) * na] for k in range(5))
        i = pl.program_id(0)
        red = _ShardReduce(g_refs, out_refs, axes, bufs, *refs[o + 3 + 5 * na:]) if na else None

        @pl.when(i == 0)
        def _():
            dw_ref[...] = jnp.zeros_like(dw_ref)
            if red is not None:
                red.exchange_with_sibling()

        if red is not None:
            @pl.when(i == 1)
            def _():
                for k in (1, 2, 3):
                    red.send_to_chip(k)
                red.keep_mine()

        hv = h_ref[...]

        def rows_of(lo, ref, cols=slice(None)):
            def add():
                dproj = ref[:, cols]
                dw_ref[lo:lo + dproj.shape[1], :] += _dot(dproj, hv, TN)
            return add

        q_cols, k_cols = slice(0, FOX_WIDTH), slice(FOX_WIDTH, 2 * FOX_WIDTH)
        between = (rows_of(0, dpa_ref), rows_of(f_hi, dgb_ref), rows_of(f_hi + FOX_WIDTH, dpm_ref), None,
                   rows_of(QB_LO, dqk_ref, q_cols), rows_of(VB_LO, dv_ref), None, None, rows_of(KB_LO, dqk_ref, k_cols))
        _fox_post_tile(i, n, t, *fox_refs, dqk_ref, dv_ref, dfb_ref, dqg_ref, dkg_ref, dbf_ref, *fox_scratch, between)
        dw_ref[F_ORIG_LO:f_hi, :] += _dot(dfb_ref[...].astype(BF16), hv, TN)[0:FOX_HEADS, :]

        if red is not None:
            @pl.when(i == n - 1)
            def _():
                red.sum_and_share()
                red.finish()

    def rev(w):
        return _rows_rev(t, w, n)

    any_spec = pl.BlockSpec(memory_space=pl.ANY)
    row = jax.ShapeDtypeStruct((1, LANES), F32)
    scratch = [pltpu.VMEM((1, FOX_WIDTH), F32), pltpu.VMEM((1, FOX_WIDTH), F32), pltpu.VMEM((1, LANES), F32)]
    scratch += _ShardReduce.scratch(gparts, axes)
    if na:
        scratch += [pltpu.SemaphoreType.DMA((_ShardReduce.SEMS * na,)), pltpu.SemaphoreType.DMA((_ShardReduce.SEMS * na,)),
                    pltpu.SemaphoreType.DMA((_ShardReduce.LOCAL * na,))]
    return pl.pallas_call(
        body, name="in_bwd_w", grid=(n,),
        out_shape=(jax.ShapeDtypeStruct((IN_WIDTH, D_MODEL), F32), jax.ShapeDtypeStruct((s, 2 * FOX_WIDTH), BF16),
                   jax.ShapeDtypeStruct((s, FOX_WIDTH), BF16), jax.ShapeDtypeStruct((s, LANES), F32), row, row, row)
        + tuple(jax.ShapeDtypeStruct(g.shape[1:], F32) for g in gparts),
        in_specs=[rev(D_MODEL), rev(512), rev(FOX_WIDTH), rev(512), rev(HEAD_BLOCKS), rev(HEAD_BLOCKS),
                  rev(HEAD_BLOCKS), rev(2 * FOX_WIDTH), rev(LANES), _full((1, LANES)), _full((1, FOX_WIDTH)),
                  _full((1, FOX_WIDTH))] + [any_spec] * na,
        out_specs=(pl.BlockSpec((IN_WIDTH, D_MODEL), lambda i: (0, 0), pipeline_mode=pl.Buffered(1)),
                   rev(2 * FOX_WIDTH), rev(FOX_WIDTH), rev(LANES), _full((1, LANES)), _full((1, LANES)),
                   _full((1, LANES))) + (any_spec,) * na,
        scratch_shapes=scratch, compiler_params=_params(),
    )(hb, dpa, dgb, dpm, *fox, *gparts)


def _adamw_math(w_ref, gv, m_ref, v_ref, d_ref, nm_ref, nv_ref):
    nm = ADAM_B1 * m_ref[...] + (1.0 - ADAM_B1) * gv
    nv = ADAM_B2 * v_ref[...] + (1.0 - ADAM_B2) * (gv * gv)
    m_hat = nm / (1.0 - ADAM_B1 ** ADAM_STEP)
    v_hat = nv / (1.0 - ADAM_B2 ** ADAM_STEP)
    d_ref[...] = -ADAM_LR * (m_hat / (jnp.sqrt(v_hat) + ADAM_EPS) + ADAM_WD * w_ref[...])
    nm_ref[...] = nm
    nv_ref[...] = nv


def _adamw(name, w, g, m, v):
    rows, cols = w.shape
    tc = 256 if rows * cols > 256 * 1024 else cols
    n = cols // tc

    def body(w_ref, g_ref, m_ref, v_ref, d_ref, nm_ref, nv_ref):
        _adamw_math(w_ref, g_ref[...], m_ref, v_ref, d_ref, nm_ref, nv_ref)

    spec = pl.BlockSpec((rows, tc), lambda i: (0, i))
    return pl.pallas_call(
        body, name=name, grid=(n,),
        out_shape=(jax.ShapeDtypeStruct((rows, cols), F32),) * 3,
        in_specs=[spec] * 4, out_specs=(spec,) * 3,
        compiler_params=_params(),
    )(w, g, m, v)


def _adamw_rest(vec, dw4, leaves, pool, shards):
    nl = len(VEC_LEAVES) + 1
    ns = len(shards)

    def body(*refs):
        vec_ref, dw4_ref = refs[0:2]
        wmv = refs[2:2 + 3 * nl]
        shard_in = refs[2 + 3 * nl:2 + 3 * nl + 4 * ns]
        o = 2 + 3 * nl + 4 * ns
        loss_ref = refs[o]
        outs = refs[o + 1:o + 1 + 4 * nl]
        shard_out = refs[o + 1 + 4 * nl:]
        loss_ref[...] = vec_ref[VEC_LOSS_ROW:VEC_LOSS_ROW + 1, 0:1]
        for k in range(nl):
            if k < nl - 1:
                _, row, width = VEC_LEAVES[k]
                gv = vec_ref[row:row + 1, 0:width]
            else:
                gv = dw4_ref[...]
            w_ref, m_ref, v_ref = wmv[3 * k:3 * k + 3]
            g_ref, d_ref, nm_ref, nv_ref = outs[4 * k:4 * k + 4]
            g_ref[...] = gv
            _adamw_math(w_ref, gv, m_ref, v_ref, d_ref, nm_ref, nv_ref)
        for k in range(ns):
            w_ref, g_ref, m_ref, v_ref = shard_in[4 * k:4 * k + 4]
            _adamw_math(w_ref, g_ref[...], m_ref, v_ref, *shard_out[3 * k:3 * k + 3])

    shapes = [jax.ShapeDtypeStruct((1, width), F32) for _, _, width in VEC_LEAVES] + [
        jax.ShapeDtypeStruct(dw4.shape, F32)]
    flat_in = [a for triple in list(leaves) + [pool] for a in triple] + [a for quad in shards for a in quad]
    res = pl.pallas_call(
        body, name="adamw_rest",
        out_shape=(jax.ShapeDtypeStruct((1, 1), F32),) + tuple(s for s in shapes for _ in range(4))
        + tuple(jax.ShapeDtypeStruct(quad[0].shape, F32) for quad in shards for _ in range(3)),
        compiler_params=pltpu.CompilerParams(vmem_limit_bytes=VMEM_LIMIT),
    )(vec, dw4, *flat_in)
    per = [res[1 + 4 * k:5 + 4 * k] for k in range(nl)]
    big = res[1 + 4 * nl:]
    return (res[0], [p[0] for p in per], [p[1] for p in per], [p[2] for p in per], [p[3] for p in per],
            [big[3 * k:3 * k + 3] for k in range(ns)])


def _tile_heads(g, n):
    return jnp.tile(g.reshape(1, HEAD_DIM), (1, n))


def kernel(x, mem, norm_g, w_in, b_f, w_pool, pool_scale, fox_q_g, fox_k_g, mem_norm_g, w_mem_kv, mem_q_g, mem_k_g, w_out, loss_target, m_norm_g, m_w_in, m_b_f, m_w_pool, m_pool_scale, m_fox_q_g, m_fox_k_g, m_mem_norm_g, m_w_mem_kv, m_mem_q_g, m_mem_k_g, m_w_out, v_norm_g, v_w_in, v_b_f, v_w_pool, v_pool_scale, v_fox_q_g, v_fox_k_g, v_mem_norm_g, v_w_mem_kv, v_mem_q_g, v_mem_k_g, v_w_out):
    w_in_t, m_w_in_t, v_w_in_t = w_in[0].T, m_w_in[0].T, v_w_in[0].T
    axes = (1, 0, 0)

    g_in, g_kv, g_out = _all_gather_weights([w_in_t, w_mem_kv[0], w_out[0]], axes)
    tiled = _tiled_params(b_f, fox_q_g, fox_k_g, mem_q_g, mem_k_g)
    fwd, wp = _fwd_in(x[0], norm_g, g_in, *tiled[0:3])
    w_kv_b = g_kv.reshape(D_MODEL, 2 * MEM_WIDTH)
    w_out_b = g_out.reshape(D_MODEL, D_MODEL)
    w4 = w_pool.reshape(POOL_ROWS, HEAD_DIM)
    dy, hb, dpa, dgb, dpm, fox, dw_kv, dw_out, (dmemnorm_g, dpscale, dmq_g, dmk_g), loss_row, dw4 = _local_partials(
        x[0], mem[0], loss_target[0], fwd, w_kv_b, w_out_b, tiled, w4, pool_scale, mem_norm_g)

    early = [dw_kv.reshape(4, D_MODEL // 4, 2 * MEM_WIDTH), dw_out.reshape(4, D_MODEL // 4, D_MODEL)]
    dwp, dqk, dvb, dfb, dfq_g, dfk_g, dbf, g_w_kv, g_w_out = _in_bwd_w(hb, dpa, dgb, dpm, fox, early, axes[1:])
    dparts = (dpa, dqk, dvb, dgb, dpm, dfb)
    vec_leaves = (dmemnorm_g, dpscale, dbf, dfq_g, dfk_g, dmq_g, dmk_g)
    grad_x, _, g_w_in_t, vec, dw4_sum = _in_bwd_x(
        x[0], dy, norm_g, wp, dparts, [dwp.reshape(4, IN_WIDTH // 4, D_MODEL)], axes[0:1], (vec_leaves, loss_row, dw4))

    small_wmv = [(norm_g, m_norm_g, v_norm_g), (mem_norm_g, m_mem_norm_g, v_mem_norm_g),
                 (pool_scale, m_pool_scale, v_pool_scale), (b_f, m_b_f, v_b_f), (fox_q_g, m_fox_q_g, v_fox_q_g),
                 (fox_k_g, m_fox_k_g, v_fox_k_g), (mem_q_g, m_mem_q_g, v_mem_q_g), (mem_k_g, m_mem_k_g, v_mem_k_g)]
    pool_wmv = tuple(a.reshape(POOL_ROWS, HEAD_DIM) for a in (w_pool, m_w_pool, v_w_pool))
    loss, *small_out, (upd_kv, upd_out) = _adamw_rest(
        vec, dw4_sum, small_wmv, pool_wmv, [(w_mem_kv[0], g_w_kv, m_w_mem_kv[0], v_w_mem_kv[0]),
                                             (w_out[0], g_w_out, m_w_out[0], v_w_out[0])])
    big = [[g_w_in_t.T[None], g_w_kv[None], g_w_out[None]]]
    upd = [[a.T for a in _adamw("adamw_w_in", w_in_t, g_w_in_t, m_w_in_t, v_w_in_t)], upd_kv, upd_out]
    big += [[u[k][None] for u in upd] for k in range(3)]

    def leaves(k):
        sm = small_out[k]
        b_in, b_kv, b_out = big[k]
        return (sm[0], b_in, sm[3], sm[8].reshape(w_pool.shape), sm[2], sm[4], sm[5], sm[1], b_kv, sm[6], sm[7], b_out)

    return (loss.reshape(()), grad_x[None], *leaves(0), *leaves(1), *leaves(2), *leaves(3))


def _tiled_params(b_f, fox_q_g, fox_k_g, mem_q_g, mem_k_g):
    return (jnp.pad(b_f, ((0, 0), (0, LANES - FOX_HEADS))), _tile_heads(fox_q_g, FOX_HEADS),
            _tile_heads(fox_k_g, FOX_HEADS), _tile_heads(mem_q_g, 4), _tile_heads(mem_k_g, 4))


def _local_partials(xs, mems, tgt, fwd, w_kv_b, w_out_b, tiled, w4, pool_scale, mem_norm_g):
    hb, pa, qk, qa, ka, va, gb, pm, fb = fwd
    bf_pad, fq_g, fk_g, mq_g, mk_g = tiled

    ma, db, mm, mnb, kv, kmn, vmb = _side_fwd(pa, pm, w4, pool_scale, mq_g, mems, mem_norm_g, w_kv_b, mk_g)
    o, mb, r4 = _fox_fwd(qa, ka, va, gb)
    dy, dma, dmm, dw_out, loss_row, doa, dgb, rr = _out_loss(xs, tgt, ma, mb, mm, w_out_b, gb, o, r4)

    dpa, dpm, dw4, dpscale, dmq_g, dw_kv, dmemnorm_g, dmk_g = _side_bwd(
        pa, db, dma, w4, pool_scale, pm, dmm, kmn, vmb, mq_g, kv, mnb, mems, w_kv_b, mk_g, mem_norm_g)
    dka, dva, dqa = _fox_bwd(ka, va, qa, doa, rr)
    fox = (dqa, dka, dva, qk, fb, bf_pad, fq_g, fk_g)
    return dy, hb, dpa, dgb, dpm, fox, dw_kv, dw_out, (dmemnorm_g, dpscale, dmq_g, dmk_g), loss_row, dw4
```

```python
import functools

import jax
import jax.numpy as jnp
from jax import lax
from jax.experimental import pallas as pl
from jax.experimental.pallas import tpu as pltpu

F32 = jnp.float32
BF16 = jnp.bfloat16
MESH = pl.DeviceIdType.MESH

D_MODEL = 1024
HEAD_DIM = 64
POOL_WIDTH = 256
FOX_WIDTH = 512
FOX_HEADS = 8
MEM_WIDTH = 256
N_MEM = 256
IN_WIDTH = 3080
EPS = 1e-6
ATT_SCALE = 0.125

ADAM_LR = 0.001
ADAM_B1 = 0.9
ADAM_B2 = 0.999
ADAM_EPS = 1e-08
ADAM_WD = 0.01
ADAM_STEP = 10

LANES = 128
PA_LO, QB_LO, KB_LO, VB_LO, GB_LO, PM_LO, FB_LO, PROJ_PAD = 0, 512, 1024, 1536, 2048, 2560, 3072, 3200
F_ORIG_LO = 2048

TILE = 512
VMEM_LIMIT = 56 * 1024 * 1024

VEC_LEAVES = (("norm_g", 0, 1024), ("mem_norm_g", 1, 1024), ("pool_scale", 2, 256), ("b_f", 3, 8),
              ("fox_q_g", 4, 64), ("fox_k_g", 5, 64), ("mem_q_g", 6, 64), ("mem_k_g", 7, 64))
VEC_LOSS_ROW = 8
VEC_ROWS = 16
POOL_ROWS = 256


def _params(n_grid=1, vmem=VMEM_LIMIT):
    return pltpu.CompilerParams(dimension_semantics=("arbitrary",) * n_grid, vmem_limit_bytes=vmem)


def _rows(t, w):
    return pl.BlockSpec((t, w), lambda i: (i, 0))


def _rows_rev(t, w, n):
    return pl.BlockSpec((t, w), lambda i: (n - 1 - i, 0))


def _full(shape):
    return pl.BlockSpec(shape, lambda i: (0,) * len(shape))


def _sig(x):
    return 1.0 / (1.0 + jnp.exp(-x))


def _lane_lo(shape):
    return lax.broadcasted_iota(jnp.int32, shape, 1) < HEAD_DIM


def _pair_sum(v, lo):
    s0 = jnp.sum(jnp.where(lo, v, 0.0), axis=-1, keepdims=True)
    s1 = jnp.sum(jnp.where(lo, 0.0, v), axis=-1, keepdims=True)
    return jnp.where(lo, s0, s1)


def _head_rms(blk, lo):
    return lax.rsqrt(_pair_sum(blk * blk, lo) * (1.0 / HEAD_DIM) + EPS)


def _head_norm_bwd(dyn, xhat, rr, g, lo):
    a = dyn * g
    return rr * (a - xhat * (_pair_sum(xhat * a, lo) * (1.0 / HEAD_DIM)))


def _fold_heads(acc):
    tot = acc[:, 0:LANES]
    for p in range(1, acc.shape[1] // LANES):
        tot = tot + acc[:, p * LANES:(p + 1) * LANES]
    return tot + pltpu.roll(tot, HEAD_DIM, axis=1)


def _lane_pick(v, lane, idx):
    return jnp.sum(jnp.where(lane == idx, v, 0.0), axis=-1, keepdims=True)


NT = (((1,), (1,)), ((), ()))
TN = (((0,), (0,)), ((), ()))


def _dot(a, b, dims=None):
    if dims is None:
        return jnp.dot(a, b, preferred_element_type=F32)
    return lax.dot_general(a, b, dims, preferred_element_type=F32)


def _my_place():
    return lax.axis_index("x"), lax.axis_index("y"), lax.axis_index("c")


def _half_dims(shape, axis):
    return (shape[0] // 2, shape[1]) if axis == 0 else (shape[0], shape[1] // 2)


def _shard_shape(g):
    return tuple(g.shape[1:]) if len(g.shape) == 3 else (g.shape[0] // 4, g.shape[1])


F32_ROWS = 8


def _shard_window(g):
    rows = _shard_shape(g)[0]
    if len(g.shape) == 3:
        return rows
    skew = max((j * rows) % F32_ROWS for j in range(4))
    return -(-(rows + skew) // F32_ROWS) * F32_ROWS


def _half_of(ref, axis, core, lead=False):
    rows, cols = ref.shape[-2:]
    if axis == 0:
        idx = (pl.ds(pl.multiple_of(core * (rows // 2), 16), rows // 2), slice(None))
    else:
        idx = (slice(None), pl.ds(pl.multiple_of(core * (cols // 2), LANES), cols // 2))
    return ref.at[(slice(None),) + idx] if lead else ref.at[idx]


class _HalfGather:
    def __init__(self, ins, outs, axes, f32_bufs, bf_bufs, send_sems, recv_sems, local_sems):
        self.ins, self.outs, self.axes = ins, outs, axes
        self.f32_bufs, self.bf_bufs = f32_bufs, bf_bufs
        self.send_sems, self.recv_sems, self.local_sems = send_sems, recv_sems, local_sems
        self.n = len(ins)
        x, y, self.c = _my_place()
        self.me, self.sibling = (x, y, self.c), (x, y, 1 - self.c)
        self.chips = [(1 - x, y), (x, 1 - y), (1 - x, 1 - y)]

    @staticmethod
    def scratch(shards, axes):
        dims = [_half_dims(a.shape, axis) for a, axis in zip(shards, axes)]
        n = len(shards)
        return [pltpu.VMEM(d, F32) for d in dims] + [pltpu.VMEM(d, BF16) for d in dims] + [
            pltpu.SemaphoreType.DMA((7 * n,)), pltpu.SemaphoreType.DMA((7 * n,)), pltpu.SemaphoreType.DMA((2 * n,))]

    @staticmethod
    def out_shapes(shards, axes):
        return tuple(jax.ShapeDtypeStruct((8,) + _half_dims(a.shape, axis), BF16) for a, axis in zip(shards, axes))

    def _blk(self, a, px, py, pc):
        return self.outs[a].at[4 * px + 2 * py + pc]

    def _copy(self, a, k, block, to, src=None):
        return pltpu.make_async_remote_copy(
            src_ref=self._blk(a, *block) if src is None else src, dst_ref=self._blk(a, *block),
            send_sem=self.send_sems.at[7 * a + k], recv_sem=self.recv_sems.at[7 * a + k], device_id=to,
            device_id_type=MESH)

    def _keep(self, a):
        return pltpu.make_async_copy(self.bf_bufs[a], self._blk(a, *self.me), self.local_sems.at[self.n + a])

    def _first(self, a):
        mine = [self._copy(a, 0, self.me, self.sibling, src=self.bf_bufs[a])]
        return mine + [self._copy(a, 1 + j, self.me, (*chip, self.c), src=self.bf_bufs[a])
                       for j, chip in enumerate(self.chips)]

    def send_mine(self):
        loads = [pltpu.make_async_copy(_half_of(self.ins[a], self.axes[a], self.c), self.f32_bufs[a],
                                       self.local_sems.at[a]) for a in range(self.n)]
        for cp in loads:
            cp.start()
        for a in range(self.n):
            loads[a].wait()
            self.bf_bufs[a][...] = self.f32_bufs[a][...].astype(BF16)
            self._keep(a).start()
            for cp in self._first(a):
                cp.start()

    def pass_on(self):
        for a in range(self.n):
            for j, chip in enumerate(self.chips):
                self._copy(a, 1 + j, (*chip, self.c), self.me).wait_recv()
                self._copy(a, 4 + j, (*chip, self.c), self.sibling).start()

    def finish(self):
        for a in range(self.n):
            self._copy(a, 0, self.sibling, self.me).wait_recv()
            for j, chip in enumerate(self.chips):
                self._copy(a, 4 + j, (*chip, 1 - self.c), self.me).wait_recv()
        for a in range(self.n):
            for cp in self._first(a):
                cp.wait_send()
            for j, chip in enumerate(self.chips):
                self._copy(a, 4 + j, (*chip, self.c), self.sibling).wait_send()
            self._keep(a).wait()


def _all_gather_weights(shards, axes):
    n = len(shards)

    def body(*refs):
        gather = _HalfGather(refs[0:n], refs[n:2 * n], axes, refs[2 * n:3 * n], refs[3 * n:4 * n], *refs[4 * n:])
        gather.send_mine()
        gather.pass_on()
        gather.finish()

    any_spec = pl.BlockSpec(memory_space=pl.ANY)
    return pl.pallas_call(
        body, name="weights_all_gather", out_shape=_HalfGather.out_shapes(shards, axes),
        in_specs=[any_spec] * n, out_specs=(any_spec,) * n, scratch_shapes=_HalfGather.scratch(shards, axes),
        compiler_params=pltpu.CompilerParams(vmem_limit_bytes=VMEM_LIMIT),
    )(*shards)


class _ShardReduce:
    SEMS = 8
    LOCAL = 5

    def __init__(self, g_refs, out_refs, axes, bufs, send_sems, recv_sems, local_sems):
        self.g_refs, self.out_refs, self.axes = g_refs, out_refs, axes
        self.recv_a, self.own_a, self.send_b, self.recv_b, self.fin = bufs
        self.send_sems, self.recv_sems, self.local_sems = send_sems, recv_sems, local_sems
        self.n = len(g_refs)
        x, y, self.c = _my_place()
        self.chip = 2 * x + y
        self.sibling = (x, y, 1 - self.c)

    @staticmethod
    def scratch(gparts, axes):
        assert all(len(g.shape) == 3 or axis == 1 for g, axis in zip(gparts, axes))
        dims = [_half_dims(_shard_shape(g), axis) for g, axis in zip(gparts, axes)]
        windows = [d if len(g.shape) == 3 else (_shard_window(g),) + d[1:] for g, d in zip(gparts, dims)]
        shapes = []
        for dtype, lead, per_array in ((F32, (4,), windows), (F32, (4,), windows), (BF16, (4,), dims),
                                       (BF16, (4,), dims), (F32, (), dims)):
            shapes += [pltpu.VMEM(lead + d, dtype) for d in per_array]
        return shapes

    def _shard_half(self, a, j, core):
        g = self.g_refs[a]
        if len(g.shape) == 3:
            return _half_of(g.at[j], self.axes[a], core)
        start = (j * _shard_shape(g)[0]) // F32_ROWS * F32_ROWS
        return _half_of(g.at[pl.ds(pl.multiple_of(start, F32_ROWS), _shard_window(g))], self.axes[a], core)

    def _to_sibling(self, a, j):
        return pltpu.make_async_remote_copy(
            src_ref=self._shard_half(a, j, 1 - self.c), dst_ref=self.recv_a[a].at[j],
            send_sem=self.send_sems.at[self.SEMS * a + j], recv_sem=self.recv_sems.at[self.SEMS * a + j], device_id=self.sibling,
            device_id_type=MESH)

    def _own(self, a, j):
        return pltpu.make_async_copy(self._shard_half(a, j, self.c), self.own_a[a].at[j],
                                     self.local_sems.at[self.LOCAL * a + j])

    def _to_chip(self, a, k):
        dest = (self.chip + k) % 4
        return pltpu.make_async_remote_copy(
            src_ref=self.send_b[a].at[dest], dst_ref=self.recv_b[a].at[self.chip],
            send_sem=self.send_sems.at[self.SEMS * a + 3 + k], recv_sem=self.recv_sems.at[self.SEMS * a + 3 + k],
            device_id=(dest // 2, dest % 2, self.c), device_id_type=MESH)

    def _give(self, a):
        return pltpu.make_async_remote_copy(
            src_ref=self.fin[a], dst_ref=_half_of(self.out_refs[a], self.axes[a], self.c),
            send_sem=self.send_sems.at[self.SEMS * a + 7], recv_sem=self.recv_sems.at[self.SEMS * a + 7], device_id=self.sibling,
            device_id_type=MESH)

    def _mine(self, a):
        return pltpu.make_async_copy(self.fin[a], _half_of(self.out_refs[a], self.axes[a], self.c),
                                     self.local_sems.at[self.LOCAL * a])

    def exchange_with_sibling(self):
        for k in (1, 2, 3, 0):
            j = (self.chip + k) % 4
            for a in range(self.n):
                self._to_sibling(a, j).start()
                self._own(a, j).start()

    def _chip_partial(self, a, j):
        self._own(a, j).wait()
        self._to_sibling(a, j).wait_recv()
        g = self.g_refs[a]
        if len(g.shape) == 3:
            self.send_b[a][j] = (self.own_a[a][j] + self.recv_a[a][j]).astype(BF16)
            return
        rows = _shard_shape(g)[0]
        for shard in range(4):
            @pl.when(j == shard)
            def _():
                at = pl.ds((shard * rows) % F32_ROWS, rows)
                self.send_b[a][shard] = (self.own_a[a][shard, at, :] + self.recv_a[a][shard, at, :]).astype(BF16)

    def send_to_chip(self, k):
        for a in range(self.n):
            self._chip_partial(a, (self.chip + k) % 4)
            self._to_chip(a, k).start()

    def keep_mine(self):
        for a in range(self.n):
            self._chip_partial(a, self.chip)
            keep = pltpu.make_async_copy(self.send_b[a].at[self.chip], self.recv_b[a].at[self.chip],
                                         self.local_sems.at[self.LOCAL * a + 4])
            keep.start()
            keep.wait()

    def sum_and_share(self):
        for a in range(self.n):
            for k in range(1, 4):
                self._to_chip(a, k).wait_recv()
            tot = self.recv_b[a][0].astype(F32) + self.recv_b[a][1].astype(F32)
            tot = tot + self.recv_b[a][2].astype(F32)
            self.fin[a][...] = tot + self.recv_b[a][3].astype(F32)
            self._give(a).start()
            self._mine(a).start()

    def finish(self):
        for a in range(self.n):
            self._give(a).wait_recv()
            self._mine(a).wait()
            self._give(a).wait_send()
            for j in range(4):
                self._to_sibling(a, j).wait_send()
            for k in range(1, 4):
                self._to_chip(a, k).wait_send()


def _mem_tokens_fwd(mem_ref, g_ref, w_ref, kg_ref, mn_ref, kv_ref, kn_ref, vm_ref):
    xm = mem_ref[...]
    rr = lax.rsqrt(jnp.mean(xm * xm, axis=-1, keepdims=True) + EPS)
    mnb = ((xm * rr) * g_ref[...]).astype(BF16)
    mn_ref[...] = mnb
    kv = _dot(mnb, w_ref[...])
    kv_ref[...] = kv
    lo = _lane_lo((xm.shape[0], LANES))
    for p in range(MEM_WIDTH // LANES):
        sl = slice(p * LANES, (p + 1) * LANES)
        kb = kv[:, sl]
        kn_ref[:, sl] = ((kb * _head_rms(kb, lo)) * kg_ref[:, sl]).astype(BF16)
    vm_ref[...] = kv[:, MEM_WIDTH:].astype(BF16)


AUG_LO = 64
KEY_SUM_LANE = 72
QUERY_SUM_LANE = 80
HEAD_BLOCKS = FOX_HEADS * LANES


def _ones3(lane):
    return jnp.where((lane >= AUG_LO) & (lane < AUG_LO + 3), 1.0, 0.0)


def _spread3(cols):
    hi = cols.astype(BF16)
    rest = cols - hi.astype(F32)
    mid = rest.astype(BF16)
    low = (rest - mid.astype(F32)).astype(BF16)
    r = lax.broadcasted_iota(jnp.int32, (LANES, HEAD_BLOCKS), 0)
    c = lax.broadcasted_iota(jnp.int32, (LANES, HEAD_BLOCKS), 1)
    out = None
    for k, part in enumerate((hi, mid, low)):
        term = _dot(part, jnp.where(c == r * LANES + (AUG_LO + k), 1.0, 0.0).astype(BF16))
        out = term if out is None else out + term
    return out


def _head_block(pair_blk, hh, lo, extras):
    src = pair_blk if hh == 0 else pltpu.roll(pair_blk, HEAD_DIM, axis=1)
    return jnp.where(lo, src, extras).astype(BF16)


def _pair_block(blk0, blk1, lo):
    return jnp.where(lo, blk0, pltpu.roll(blk1, HEAD_DIM, axis=1))


def _assemble_w_in(halves_ref, words_ref, wp_ref):
    shard = IN_WIDTH // 4
    half = D_MODEL // 2
    f_hi = F_ORIG_LO + FOX_HEADS
    for j in range(4):
        blocks = [pltpu.bitcast(halves_ref[2 * j + c], jnp.uint32) for c in range(2)]
        for lo, hi, to in ((0, F_ORIG_LO, PA_LO), (F_ORIG_LO, f_hi, FB_LO), (f_hi, IN_WIDTH, GB_LO)):
            a, b = max(lo, shard * j), min(hi, shard * (j + 1))
            if a < b:
                for c in range(2):
                    words_ref[(to + a - lo) // 2:(to + b - lo) // 2, c * half:(c + 1) * half] = (
                        blocks[c][(a - shard * j) // 2:(b - shard * j) // 2, :])
    pad_lo = (FB_LO + FOX_HEADS) // 2
    words_ref[pad_lo:, :] = jnp.zeros((PROJ_PAD // 2 - pad_lo, D_MODEL), jnp.uint32)
    wp_ref[...] = pltpu.bitcast(words_ref[...], BF16)


def _fwd_in(x, norm_g, halves, bf_pad, fq_g, fk_g):
    s = x.shape[0]
    t = TILE
    n = s // t

    def body(x_ref, ng_ref, halves_ref, bf_ref, qg_ref, kg_ref,
             h_ref, pa_ref, qk_ref, qa_ref, ka_ref, va_ref, gb_ref, pm_ref, fb_ref, wp_ref,
             carry_ref, fcol_ref, words_ref):
        @pl.when(pl.program_id(0) == 0)
        def _():
            carry_ref[...] = jnp.zeros_like(carry_ref)
            _assemble_w_in(halves_ref, words_ref, wp_ref)

        xv = x_ref[...]
        rr = lax.rsqrt(jnp.mean(xv * xv, axis=-1, keepdims=True) + EPS)
        hb = ((xv * rr) * ng_ref[...]).astype(BF16)
        h_ref[...] = hb

        def proj(lo, hi):
            return _dot(hb, wp_ref[lo:hi, :], NT)

        fb = proj(FB_LO, PROJ_PAD)
        fb_ref[...] = fb
        qk_ref[:, 0:FOX_WIDTH] = proj(QB_LO, KB_LO)

        lane = lax.broadcasted_iota(jnp.int32, (t, LANES), 1)
        row = lax.broadcasted_iota(jnp.int32, (t, LANES), 0)
        lo = lane < HEAD_DIM
        z = fb + bf_ref[...]
        lf = -(jnp.maximum(-z, 0.0) + jnp.log1p(jnp.exp(-jnp.abs(z))))
        lf = jnp.where(lane < FOX_HEADS, lf, 0.0)
        sh = 1
        while sh < t:
            lf = lf + jnp.where(row >= sh, pltpu.roll(lf, sh, axis=0), 0.0)
            sh *= 2
        fcum = lf + carry_ref[...]
        fcol_ref[...] = fcum
        carry_ref[...] = fcol_ref[t - 1:t, :]

        ones3 = _ones3(lane)
        minus_f = _spread3(-fcum)

        def head_blocks(seg, g_ref, out_ref, scale):
            for p in range(FOX_WIDTH // LANES):
                sl = slice(p * LANES, (p + 1) * LANES)
                blk = qk_ref[:, seg - QB_LO + p * LANES:seg - QB_LO + (p + 1) * LANES]
                normed = ((blk * _head_rms(blk, lo)) * g_ref[:, sl]) * scale
                for hh in range(2):
                    h = 2 * p + hh
                    if seg == QB_LO:
                        extras = jnp.where(lane == QUERY_SUM_LANE + h, 1.0, ones3)
                    else:
                        extras = jnp.where(lane == KEY_SUM_LANE + h, 1.0, minus_f[:, h * LANES:(h + 1) * LANES])
                    out_ref[:, h * LANES:(h + 1) * LANES] = _head_block(normed, hh, lo, extras)

        qk_ref[:, FOX_WIDTH:2 * FOX_WIDTH] = proj(KB_LO, VB_LO)
        pa_ref[...] = proj(PA_LO, QB_LO)
        head_blocks(QB_LO, qg_ref, qa_ref, ATT_SCALE)
        vraw = proj(VB_LO, GB_LO)
        gb_ref[...] = proj(GB_LO, PM_LO)
        head_blocks(KB_LO, kg_ref, ka_ref, 1.0)
        pm_ref[...] = proj(PM_LO, FB_LO)
        for h in range(FOX_HEADS):
            va_ref[:, h * LANES:(h + 1) * LANES] = _head_block(vraw[:, (h // 2) * LANES:(h // 2 + 1) * LANES], h % 2, lo, ones3)

    outs = (
        jax.ShapeDtypeStruct((s, D_MODEL), BF16),
        jax.ShapeDtypeStruct((s, 512), F32),
        jax.ShapeDtypeStruct((s, 2 * FOX_WIDTH), F32),
        jax.ShapeDtypeStruct((s, HEAD_BLOCKS), BF16),
        jax.ShapeDtypeStruct((s, HEAD_BLOCKS), BF16),
        jax.ShapeDtypeStruct((s, HEAD_BLOCKS), BF16),
        jax.ShapeDtypeStruct((s, FOX_WIDTH), F32),
        jax.ShapeDtypeStruct((s, 512), F32),
        jax.ShapeDtypeStruct((s, LANES), F32),
        jax.ShapeDtypeStruct((PROJ_PAD, D_MODEL), BF16),
    )

    def resident(shape):
        return pl.BlockSpec(shape, lambda i: (0,) * len(shape), pipeline_mode=pl.Buffered(1))

    *fwd, wp = pl.pallas_call(
        body, name="fwd_in", grid=(n,), out_shape=outs,
        in_specs=[_rows(t, D_MODEL), _full((1, D_MODEL)), resident(halves.shape), _full((1, LANES)),
                  _full((1, FOX_WIDTH)), _full((1, FOX_WIDTH))],
        out_specs=(_rows(t, D_MODEL), _rows(t, 512), _rows(t, 2 * FOX_WIDTH), _rows(t, HEAD_BLOCKS),
                   _rows(t, HEAD_BLOCKS), _rows(t, HEAD_BLOCKS), _rows(t, FOX_WIDTH), _rows(t, 512),
                   _rows(t, LANES), resident((PROJ_PAD, D_MODEL))),
        scratch_shapes=[pltpu.VMEM((1, LANES), F32), pltpu.VMEM((t, LANES), F32),
                        pltpu.VMEM((PROJ_PAD // 2, D_MODEL), jnp.uint32)],
        compiler_params=_params(),
    )(x, norm_g, halves, bf_pad, fq_g, fk_g)
    return tuple(fwd), wp


POOL_HALO = 16


def _pool_window(lane):
    return jnp.where(lane < 64, 2.0, jnp.where(lane < 128, 4.0, jnp.where(lane < 192, 8.0, 16.0)))


def _pool_pick(lane, s2, s4, s8, s16):
    return jnp.where(lane < 64, s2, jnp.where(lane < 128, s4, jnp.where(lane < 192, s8, s16)))


def _group_onehot(shape, row_is_group_lane):
    r = lax.broadcasted_iota(jnp.int32, shape, 0)
    c = lax.broadcasted_iota(jnp.int32, shape, 1)
    hit = (r % HEAD_DIM == c) if row_is_group_lane else (c % HEAD_DIM == r)
    return jnp.where(hit, 1.0, 0.0).astype(F32)


def _same_group(shape):
    r = lax.broadcasted_iota(jnp.int32, shape, 0)
    c = lax.broadcasted_iota(jnp.int32, shape, 1)
    return (r // HEAD_DIM) == (c // HEAD_DIM)


def _pool_block_diag(w4):
    spread = jnp.dot(w4, _group_onehot((HEAD_DIM, POOL_WIDTH), False), preferred_element_type=F32,
                     precision=lax.Precision.HIGHEST)
    return jnp.where(_same_group((POOL_WIDTH, POOL_WIDTH)), spread, 0.0).astype(BF16)


def _mem_softmax(qm, kp):
    sc = _dot(qm, kp, NT)
    e = jnp.exp(sc - jnp.max(sc, axis=-1, keepdims=True))
    return e * (1.0 / jnp.sum(e, axis=-1, keepdims=True))


def _side_fwd(pa, pm, w4, pscale, mq_g, mem, mem_norm_g, w_kv, mk_g):
    s = pa.shape[0]
    t = TILE
    n = s // t
    ext = t + POOL_HALO
    nm = mem.shape[0]

    def body(pa_ref, pm_ref, w4_ref, sc_ref, g_ref, mem_ref, mg_ref, wkv_ref, kg_ref,
             ma_ref, d_ref, mm_ref, mn_ref, kv_ref, k_ref, v_ref, ext_ref, w_ref):
        i = pl.program_id(0)

        @pl.when(i == 0)
        def _():
            ext_ref[0:POOL_HALO, :] = jnp.zeros((POOL_HALO, POOL_WIDTH), F32)
            w_ref[...] = _pool_block_diag(w4_ref[...])
            _mem_tokens_fwd(mem_ref, mg_ref, wkv_ref, kg_ref, mn_ref, kv_ref, k_ref, v_ref)

        u = pa_ref[:, 0:POOL_WIDTH]
        ext_ref[POOL_HALO:ext, :] = u
        e = ext_ref[...]
        s2 = e + pltpu.roll(e, 1, axis=0)
        s4 = s2 + pltpu.roll(s2, 2, axis=0)
        s8 = s4 + pltpu.roll(s4, 4, axis=0)
        s16 = s8 + pltpu.roll(s8, 8, axis=0)
        lane_e = lax.broadcasted_iota(jnp.int32, (ext, POOL_WIDTH), 1)
        win = _pool_pick(lane_e, s2, s4, s8, s16)[POOL_HALO:ext, :]
        lane = lax.broadcasted_iota(jnp.int32, (t, POOL_WIDTH), 1)
        pos = (lax.broadcasted_iota(jnp.int32, (t, POOL_WIDTH), 0) + (i * t + 1)).astype(F32)
        d = win / jnp.minimum(pos, _pool_window(lane)) - u
        db = d.astype(BF16)
        d_ref[...] = db
        ya = _dot(db, w_ref[...]) * sc_ref[...]
        ga = pa_ref[:, POOL_WIDTH:2 * POOL_WIDTH]
        ma_ref[...] = (ya * (ga * _sig(ga))).astype(BF16)
        ext_ref[0:POOL_HALO, :] = ext_ref[t:ext, :]

        lo = _lane_lo((t, LANES))
        for p in range(MEM_WIDTH // LANES):
            sl = slice(p * LANES, (p + 1) * LANES)
            qb = pm_ref[:, sl]
            qs = (((qb * _head_rms(qb, lo)) * g_ref[:, sl]) * ATT_SCALE).astype(BF16)
            kp = k_ref[:, sl]
            vp = v_ref[:, sl]
            outs = []
            for hh in range(2):
                msk = lo if hh == 0 else jnp.logical_not(lo)
                prob = _mem_softmax(jnp.where(msk, qs, jnp.zeros_like(qs)), kp)
                outs.append(_dot(prob.astype(BF16), vp))
            o = jnp.where(lo, outs[0], outs[1])
            gm = pm_ref[:, MEM_WIDTH + p * LANES:MEM_WIDTH + (p + 1) * LANES]
            mm_ref[:, sl] = (o * (gm * _sig(gm))).astype(BF16)

    return pl.pallas_call(
        body, name="side_fwd", grid=(n,),
        out_shape=(jax.ShapeDtypeStruct((s, POOL_WIDTH), BF16), jax.ShapeDtypeStruct((s, POOL_WIDTH), BF16),
                   jax.ShapeDtypeStruct((s, MEM_WIDTH), BF16), jax.ShapeDtypeStruct((nm, D_MODEL), BF16),
                   jax.ShapeDtypeStruct((nm, 2 * MEM_WIDTH), F32), jax.ShapeDtypeStruct((nm, MEM_WIDTH), BF16),
                   jax.ShapeDtypeStruct((nm, MEM_WIDTH), BF16)),
        in_specs=[_rows(t, 512), _rows(t, 512), _full((POOL_ROWS, HEAD_DIM)), _full((1, POOL_WIDTH)),
                  _full((1, MEM_WIDTH)), _full((nm, D_MODEL)), _full((1, D_MODEL)), _full((D_MODEL, 2 * MEM_WIDTH)),
                  _full((1, MEM_WIDTH))],
        out_specs=(_rows(t, POOL_WIDTH), _rows(t, POOL_WIDTH), _rows(t, MEM_WIDTH), _full((nm, D_MODEL)),
                   _full((nm, 2 * MEM_WIDTH)), _full((nm, MEM_WIDTH)), _full((nm, MEM_WIDTH))),
        scratch_shapes=[pltpu.VMEM((ext, POOL_WIDTH), F32), pltpu.VMEM((POOL_WIDTH, POOL_WIDTH), BF16)],
        compiler_params=_params(),
    )(pa, pm, w4, pscale, mq_g, mem, mem_norm_g, w_kv, mk_g)


FOX_FWD_HEADS = 4


def _fox_fwd(qa, ka, va, gb):
    s = qa.shape[0]
    t = TILE
    n = s // t
    heads = FOX_FWD_HEADS
    pairs = heads // 2
    group_w = heads * LANES

    def body(qa_ref, ka_ref, va_ref, gb_ref, o_ref, mb_ref, r_ref):
        i = pl.program_id(1)
        lane = lax.broadcasted_iota(jnp.int32, (t, LANES), 1)
        lo = lane < HEAD_DIM
        causal = lax.broadcasted_iota(jnp.int32, (t, t), 1) <= lax.broadcasted_iota(jnp.int32, (t, t), 0)
        qas = [qa_ref[:, hh * LANES:(hh + 1) * LANES] for hh in range(heads)]

        def step(j, carry, masked):
            rows = pl.ds(pl.multiple_of(j * t, t), t)
            def logits(hh):
                sc = _dot(qas[hh], ka_ref[rows, hh * LANES:(hh + 1) * LANES], NT)
                return jnp.where(causal, sc, -1e30) if masked else sc

            def advance(hh, sc):
                m, acc = carry[hh]
                m_new = jnp.maximum(m, jnp.max(sc, axis=-1, keepdims=True))
                p = jnp.exp(sc - m_new).astype(BF16)
                return m_new, jnp.exp(m - m_new) * acc + _dot(p, va_ref[rows, hh * LANES:(hh + 1) * LANES])

            new = []
            sc = logits(0)
            for hh in range(heads):
                sc_next = logits(hh + 1) if hh + 1 < heads else None
                new.append(advance(hh, sc))
                sc = sc_next
            return tuple(new)

        init = (jnp.full((t, 1), -1e30, F32), jnp.zeros((t, LANES), F32))
        carry = lax.fori_loop(0, i, functools.partial(step, masked=False), (init,) * heads)
        res = step(i, carry, masked=True)
        for p in range(pairs):
            outs = []
            rcol = jnp.zeros((t, LANES), F32)
            for hh in range(2):
                m, acc = res[2 * p + hh]
                l = _lane_pick(acc, lane, AUG_LO)
                outs.append(acc * (1.0 / l))
                rcol = jnp.where(lane == hh, m + jnp.log(l), rcol)
            o = _pair_block(outs[0], outs[1], lo)
            sl = slice(p * LANES, (p + 1) * LANES)
            o_ref[:, sl] = o
            g = gb_ref[:, sl]
            mb_ref[:, sl] = (o * (g * _sig(g))).astype(BF16)
            r_ref[p] = rcol

    tile_spec = pl.BlockSpec((t, pairs * LANES), lambda p, i: (i, p))
    full_spec = pl.BlockSpec((s, group_w), lambda p, i: (0, p))
    return pl.pallas_call(
        body, name="fox_fwd", grid=(FOX_HEADS // heads, n),
        out_shape=(jax.ShapeDtypeStruct((s, FOX_WIDTH), F32), jax.ShapeDtypeStruct((s, FOX_WIDTH), BF16),
                   jax.ShapeDtypeStruct((FOX_HEADS // 2, s, LANES), F32)),
        in_specs=[pl.BlockSpec((t, group_w), lambda p, i: (i, p)), full_spec, full_spec, tile_spec],
        out_specs=(tile_spec, tile_spec, pl.BlockSpec((pairs, t, LANES), lambda p, i: (p, i, 0))),
        compiler_params=_params(2),
    )(qa, ka, va, gb)


def _out_loss(x, tgt, ma, mb, mm, wout, gb, o, r4):
    s = x.shape[0]
    t = TILE
    n = s // t
    pairs = FOX_HEADS // 2

    def body(x_ref, t_ref, ma_ref, mb_ref, mm_ref, w_ref, gb_ref, o_ref, r_ref,
             dy_ref, dma_ref, dmm_ref, dw_ref, loss_ref, doa_ref, dgb_ref, rr_ref, mix_ref):
        @pl.when(pl.program_id(0) == 0)
        def _():
            dw_ref[...] = jnp.zeros_like(dw_ref)
            loss_ref[...] = jnp.zeros_like(loss_ref)

        mix_ref[:, 0:256] = ma_ref[...]
        mix_ref[:, 256:768] = mb_ref[...]
        mix_ref[:, 768:1024] = mm_ref[...]
        mix = mix_ref[...]
        err = (x_ref[...] + _dot(mix, w_ref[...])) - t_ref[...]
        row_mean = jnp.sum(err * err, axis=-1, keepdims=True) * (1.0 / D_MODEL)
        loss_ref[...] += 0.5 * jnp.sum(row_mean, axis=0, keepdims=True)
        dy = err * (1.0 / D_MODEL)
        dy_ref[...] = dy
        dyb = dy.astype(BF16)
        dmix = _dot(dyb, w_ref[...], NT)
        dma_ref[...] = dmix[:, 0:256]
        dmm_ref[...] = dmix[:, 768:1024]
        dw_ref[...] += _dot(mix, dyb, TN)

        lane = lax.broadcasted_iota(jnp.int32, (t, LANES), 1)
        lo = lane < HEAD_DIM
        d_os = []
        delta = jnp.zeros((t, LANES), F32)
        for p in range(pairs):
            sl = slice(p * LANES, (p + 1) * LANES)
            g = gb_ref[:, sl]
            sg = _sig(g)
            dm = dmix[:, 256 + p * LANES:256 + (p + 1) * LANES]
            ov = o_ref[:, sl]
            d_o = dm * (g * sg)
            d_os.append(d_o)
            dgb_ref[:, sl] = (dm * ov * (sg * (1.0 + g * (1.0 - sg)))).astype(BF16)
            prod = d_o * ov
            delta = jnp.where(lane == 2 * p, jnp.sum(jnp.where(lo, prod, 0.0), axis=-1, keepdims=True), delta)
            delta = jnp.where(lane == 2 * p + 1, jnp.sum(jnp.where(lo, 0.0, prod), axis=-1, keepdims=True), delta)
            rr_ref[p, 0] = r_ref[p].T[0:8, :]
        minus_delta = _spread3(-delta)
        for h in range(FOX_HEADS):
            blk = slice(h * LANES, (h + 1) * LANES)
            doa_ref[:, blk] = _head_block(d_os[h // 2], h % 2, lo, minus_delta[:, blk])

    return pl.pallas_call(
        body, name="out_loss", grid=(n,),
        out_shape=(jax.ShapeDtypeStruct((s, D_MODEL), F32), jax.ShapeDtypeStruct((s, 256), F32),
                   jax.ShapeDtypeStruct((s, 256), F32), jax.ShapeDtypeStruct((D_MODEL, D_MODEL), F32),
                   jax.ShapeDtypeStruct((1, LANES), F32), jax.ShapeDtypeStruct((s, HEAD_BLOCKS), BF16),
                   jax.ShapeDtypeStruct((s, FOX_WIDTH), BF16), jax.ShapeDtypeStruct((pairs, n, 8, t), F32)),
        in_specs=[_rows(t, D_MODEL), _rows(t, D_MODEL), _rows(t, 256), _rows(t, 512), _rows(t, 256),
                  _full((D_MODEL, D_MODEL)), _rows(t, FOX_WIDTH), _rows(t, FOX_WIDTH),
                  pl.BlockSpec((pairs, t, LANES), lambda i: (0, i, 0))],
        out_specs=(_rows(t, D_MODEL), _rows(t, 256), _rows(t, 256), _full((D_MODEL, D_MODEL)), _full((1, LANES)),
                   _rows(t, HEAD_BLOCKS), _rows(t, FOX_WIDTH), pl.BlockSpec((pairs, 1, 8, t), lambda i: (0, i, 0, 0))),
        scratch_shapes=[pltpu.VMEM((t, D_MODEL), BF16)],
        compiler_params=_params(),
    )(x, tgt, ma, mb, mm, wout, gb, o, r4)


def _side_bwd(pa, db, dma, w4, pscale, pm, dmm, kmn, vmb, mq_g, kv, mnb, mem, w_kv, mk_g, mem_norm_g):
    s = pa.shape[0]
    t = TILE
    n = s // t
    ext = t + POOL_HALO
    nm = mem.shape[0]

    def body(pa_ref, d_ref, dma_ref, w4_ref, sc_ref, pm_ref, dmm_ref, k_ref, v_ref, g_ref,
             kv_ref, mn_ref, mem_ref, wkv_ref, kg_ref, mg_ref,
             dpa_ref, dpm_ref, dw4_ref, dsc_ref, dg_ref, dwkv_ref, dmg_ref, dkg_ref,
             ext_ref, w_ref, dw_ref, dk_ref, dv_ref, gacc_ref, dkv_ref):
        i = pl.program_id(0)

        @pl.when(i == 0)
        def _():
            dw_ref[...] = jnp.zeros_like(dw_ref)
            dsc_ref[...] = jnp.zeros_like(dsc_ref)
            ext_ref[t:ext, :] = jnp.zeros((POOL_HALO, POOL_WIDTH), F32)
            w_ref[...] = _pool_block_diag(w4_ref[...])
            dk_ref[...] = jnp.zeros_like(dk_ref)
            dv_ref[...] = jnp.zeros_like(dv_ref)
            gacc_ref[...] = jnp.zeros_like(gacc_ref)

        dbv = d_ref[...]
        z = _dot(dbv, w_ref[...])
        ga = pa_ref[:, POOL_WIDTH:2 * POOL_WIDTH]
        sg = _sig(ga)
        dma_v = dma_ref[...]
        dya = dma_v * (ga * sg)
        dpa_ref[:, POOL_WIDTH:2 * POOL_WIDTH] = (dma_v * (z * sc_ref[...]) * (sg * (1.0 + ga * (1.0 - sg)))).astype(BF16)
        dsc_ref[...] += jnp.sum(dya * z, axis=0, keepdims=True)
        dzb = (dya * sc_ref[...]).astype(BF16)
        dw_ref[...] += _dot(dbv, dzb, TN)
        dd = _dot(dzb, w_ref[...], NT)
        lane = lax.broadcasted_iota(jnp.int32, (t, POOL_WIDTH), 1)
        pos = (lax.broadcasted_iota(jnp.int32, (t, POOL_WIDTH), 0) + ((n - 1 - i) * t + 1)).astype(F32)
        ext_ref[0:t, :] = dd / jnp.minimum(pos, _pool_window(lane))
        e = ext_ref[...]
        s2 = e + pltpu.roll(e, ext - 1, axis=0)
        s4 = s2 + pltpu.roll(s2, ext - 2, axis=0)
        s8 = s4 + pltpu.roll(s4, ext - 4, axis=0)
        s16 = s8 + pltpu.roll(s8, ext - 8, axis=0)
        lane_e = lax.broadcasted_iota(jnp.int32, (ext, POOL_WIDTH), 1)
        win = _pool_pick(lane_e, s2, s4, s8, s16)[0:t, :]
        dpa_ref[:, 0:POOL_WIDTH] = (win - dd).astype(BF16)
        ext_ref[t:ext, :] = ext_ref[0:POOL_HALO, :]

        lo = _lane_lo((t, LANES))
        for p in range(MEM_WIDTH // LANES):
            sl = slice(p * LANES, (p + 1) * LANES)
            qb = pm_ref[:, sl]
            rr = _head_rms(qb, lo)
            qhat = qb * rr
            g = g_ref[:, sl]
            qs = ((qhat * g) * ATT_SCALE).astype(BF16)
            gm = pm_ref[:, MEM_WIDTH + p * LANES:MEM_WIDTH + (p + 1) * LANES]
            sg = _sig(gm)
            dmo = dmm_ref[:, sl]
            d_o = dmo * (gm * sg)
            kp = k_ref[:, sl]
            vp = v_ref[:, sl]
            outs, dqs = [], []
            for hh in range(2):
                msk = lo if hh == 0 else jnp.logical_not(lo)
                qm = jnp.where(msk, qs, jnp.zeros_like(qs))
                prob = _mem_softmax(qm, kp)
                pb = prob.astype(BF16)
                outs.append(_dot(pb, vp))
                dom = jnp.where(msk, d_o, 0.0).astype(BF16)
                dp = _dot(dom, vp, NT)
                ds = (prob * (dp - jnp.sum(prob * dp, axis=-1, keepdims=True))).astype(BF16)
                dqs.append(_dot(ds, kp))
                dk_ref[:, sl] += _dot(ds, qm, TN)
                dv_ref[:, sl] += _dot(pb, dom, TN)
            o = jnp.where(lo, outs[0], outs[1])
            dqn = jnp.where(lo, dqs[0], dqs[1]) * ATT_SCALE
            dpm_ref[:, sl] = _head_norm_bwd(dqn, qhat, rr, g, lo).astype(BF16)
            dpm_ref[:, MEM_WIDTH + p * LANES:MEM_WIDTH + (p + 1) * LANES] = (
                dmo * o * (sg * (1.0 + gm * (1.0 - sg)))).astype(BF16)
            gacc_ref[:, sl] += jnp.sum(dqn * qhat, axis=0, keepdims=True)

        @pl.when(i == n - 1)
        def _():
            own = jnp.where(_same_group((POOL_WIDTH, POOL_WIDTH)), dw_ref[...], 0.0)
            dw4_ref[...] = jnp.dot(own, _group_onehot((POOL_WIDTH, HEAD_DIM), True), preferred_element_type=F32,
                                   precision=lax.Precision.HIGHEST)
            dg_ref[...] = _fold_heads(gacc_ref[...])

            lo_m = _lane_lo((nm, LANES))
            kacc = []
            for p in range(MEM_WIDTH // LANES):
                sl = slice(p * LANES, (p + 1) * LANES)
                kb = kv_ref[:, sl]
                rr = _head_rms(kb, lo_m)
                khat = kb * rr
                dk = dk_ref[:, sl]
                dkv_ref[:, sl] = _head_norm_bwd(dk, khat, rr, kg_ref[:, sl], lo_m).astype(BF16)
                kacc.append(jnp.sum(dk * khat, axis=0, keepdims=True))
            dkg_ref[...] = _fold_heads(jnp.concatenate(kacc, axis=1))
            dkv_ref[:, MEM_WIDTH:] = dv_ref[...].astype(BF16)
            dkv = dkv_ref[...]
            dwkv_ref[...] = _dot(mn_ref[...], dkv, TN)
            dmn = _dot(dkv, wkv_ref[...], NT)
            xm = mem_ref[...]
            rr = lax.rsqrt(jnp.mean(xm * xm, axis=-1, keepdims=True) + EPS)
            dmg_ref[...] = jnp.sum(dmn * (xm * rr), axis=0, keepdims=True)

    def rev(w):
        return _rows_rev(t, w, n)

    row = jax.ShapeDtypeStruct((1, LANES), F32)
    return pl.pallas_call(
        body, name="side_bwd", grid=(n,),
        out_shape=(jax.ShapeDtypeStruct((s, 512), BF16), jax.ShapeDtypeStruct((s, 512), BF16),
                   jax.ShapeDtypeStruct((POOL_ROWS, HEAD_DIM), F32), jax.ShapeDtypeStruct((1, POOL_WIDTH), F32), row,
                   jax.ShapeDtypeStruct((D_MODEL, 2 * MEM_WIDTH), F32), jax.ShapeDtypeStruct((1, D_MODEL), F32), row),
        in_specs=[rev(512), rev(POOL_WIDTH), rev(POOL_WIDTH), _full((POOL_ROWS, HEAD_DIM)), _full((1, POOL_WIDTH)),
                  rev(512), rev(MEM_WIDTH), _full((N_MEM, MEM_WIDTH)), _full((N_MEM, MEM_WIDTH)), _full((1, MEM_WIDTH)),
                  _full((nm, 2 * MEM_WIDTH)), _full((nm, D_MODEL)), _full((nm, D_MODEL)),
                  _full((D_MODEL, 2 * MEM_WIDTH)), _full((1, MEM_WIDTH)), _full((1, D_MODEL))],
        out_specs=(rev(512), rev(512), _full((POOL_ROWS, HEAD_DIM)), _full((1, POOL_WIDTH)), _full((1, LANES)),
                   _full((D_MODEL, 2 * MEM_WIDTH)), _full((1, D_MODEL)), _full((1, LANES))),
        scratch_shapes=[pltpu.VMEM((ext, POOL_WIDTH), F32), pltpu.VMEM((POOL_WIDTH, POOL_WIDTH), BF16),
                        pltpu.VMEM((POOL_WIDTH, POOL_WIDTH), F32), pltpu.VMEM((N_MEM, MEM_WIDTH), F32),
                        pltpu.VMEM((N_MEM, MEM_WIDTH), F32), pltpu.VMEM((1, MEM_WIDTH), F32),
                        pltpu.VMEM((nm, 2 * MEM_WIDTH), BF16)],
        compiler_params=_params(),
    )(pa, db, dma, w4, pscale, pm, dmm, kmn, vmb, mq_g, kv, mnb, mem, w_kv, mk_g, mem_norm_g)


FOX_BWD_HEADS = 4


def _fox_bwd(ka, va, qa, doa, rr):
    s = ka.shape[0]
    t = TILE
    n = s // t
    heads = FOX_BWD_HEADS
    group_w = heads * LANES

    def body(ka_ref, va_ref, qa_ref, doa_ref, rr_ref, dka_ref, dva_ref, dqa_ref):
        j = pl.program_id(1)

        @pl.when(j == 0)
        def _():
            dqa_ref[...] = jnp.zeros_like(dqa_ref)

        causal = lax.broadcasted_iota(jnp.int32, (t, t), 0) <= lax.broadcasted_iota(jnp.int32, (t, t), 1)
        kas = [ka_ref[:, hh * LANES:(hh + 1) * LANES] for hh in range(heads)]
        vas = [va_ref[:, hh * LANES:(hh + 1) * LANES] for hh in range(heads)]

        def step(i, carry, masked):
            rows = pl.ds(pl.multiple_of(i * t, t), t)
            new = []
            for hh in range(heads):
                cols = slice(hh * LANES, (hh + 1) * LANES)
                dk_a, dv_a = carry[hh]
                qb = qa_ref[rows, cols]
                d_o = doa_ref[rows, cols]
                arg = _dot(kas[hh], qb, NT) - rr_ref[hh // 2, i, hh % 2:hh % 2 + 1, :]
                if masked:
                    arg = jnp.where(causal, arg, -1e30)
                pt = jnp.exp(arg)
                dst = (pt * _dot(vas[hh], d_o, NT)).astype(BF16)
                dv_a = dv_a + _dot(pt.astype(BF16), d_o)
                dk_a = dk_a + _dot(dst, qb)
                dqa_ref[rows, cols] += _dot(dst, kas[hh], TN)
                new.append((dk_a, dv_a))
            return tuple(new)

        zero = jnp.zeros((t, LANES), F32)
        carry = step(j, ((zero, zero),) * heads, masked=True)
        res = lax.fori_loop(j + 1, n, functools.partial(step, masked=False), carry)
        for hh in range(heads):
            cols = slice(hh * LANES, (hh + 1) * LANES)
            dka_ref[:, cols] = res[hh][0]
            dva_ref[:, cols] = res[hh][1]

    tile_spec = pl.BlockSpec((t, group_w), lambda p, j: (j, p))
    full_spec = pl.BlockSpec((s, group_w), lambda p, j: (0, p))
    return pl.pallas_call(
        body, name="fox_bwd", grid=(FOX_HEADS // heads, n),
        out_shape=(jax.ShapeDtypeStruct((s, HEAD_BLOCKS), F32),) * 3,
        in_specs=[tile_spec, tile_spec, full_spec, full_spec,
                  pl.BlockSpec((heads // 2, n, 8, t), lambda p, j: (p, 0, 0, 0))],
        out_specs=(tile_spec, tile_spec, full_spec),
        compiler_params=_params(2),
    )(ka, va, qa, doa, rr)


def _fox_post_tile(i, n, t, dqa_ref, dka_ref, dva_ref, qk_ref, fb_ref, bf_ref, qg_ref, kg_ref,
                   dqk_ref, dv_ref, dfb_ref, dqg_ref, dkg_ref, dbf_ref, qacc_ref, kacc_ref, carry_ref,
                   between):
    @pl.when(i == 0)
    def _():
        qacc_ref[...] = jnp.zeros_like(qacc_ref)
        kacc_ref[...] = jnp.zeros_like(kacc_ref)
        dbf_ref[...] = jnp.zeros_like(dbf_ref)
        carry_ref[...] = jnp.zeros_like(carry_ref)

    lane = lax.broadcasted_iota(jnp.int32, (t, LANES), 1)
    row = lax.broadcasted_iota(jnp.int32, (t, LANES), 0)
    lo = lane < HEAD_DIM

    def head_blocks(ref, p):
        return ref[:, 2 * p * LANES:(2 * p + 1) * LANES], ref[:, (2 * p + 1) * LANES:(2 * p + 2) * LANES]

    def issue(k):
        if between[k] is not None:
            between[k]()

    sums = []
    pairs = FOX_WIDTH // LANES
    for side, (src_ref, g_ref, acc_ref, scale) in enumerate(((dqa_ref, qg_ref, qacc_ref, ATT_SCALE),
                                                             (dka_ref, kg_ref, kacc_ref, 1.0))):
        total = jnp.zeros((t, LANES), F32)
        for p in range(pairs):
            issue(side * pairs + p)
            sl = slice(p * LANES, (p + 1) * LANES)
            cols = slice(side * FOX_WIDTH + p * LANES, side * FOX_WIDTH + (p + 1) * LANES)
            if side == 0:
                dv_ref[:, sl] = _pair_block(*head_blocks(dva_ref, p), lo).astype(BF16)
            d0, d1 = head_blocks(src_ref, p)
            total = total + (d0 + d1)
            raw = qk_ref[:, cols]
            rr = _head_rms(raw, lo)
            xhat = raw * rr
            dn = _pair_block(d0, d1, lo) * scale
            dqk_ref[:, cols] = _head_norm_bwd(dn, xhat, rr, g_ref[:, sl], lo).astype(BF16)
            acc_ref[:, sl] += jnp.sum(dn * xhat, axis=0, keepdims=True)
        sums.append(total)
    issue(2 * pairs)
    dq_sum, dk_sum = sums

    acc = (pltpu.roll(dq_sum, LANES - KEY_SUM_LANE, axis=1) - pltpu.roll(dk_sum, LANES - QUERY_SUM_LANE, axis=1))
    acc = jnp.where(lane < FOX_HEADS, acc, 0.0)
    sh = 1
    while sh < t:
        acc = acc + jnp.where(row < t - sh, pltpu.roll(acc, t - sh, axis=0), 0.0)
        sh *= 2
    dlogf = acc + carry_ref[...]
    dfb_ref[...] = dlogf
    carry_ref[...] = dfb_ref[0:1, :]
    z = fb_ref[...] + bf_ref[...]
    dz = jnp.where(lane < FOX_HEADS, dlogf * (1.0 / (1.0 + jnp.exp(z))), 0.0)
    dfb_ref[...] = dz
    dbf_ref[...] += jnp.sum(dz, axis=0, keepdims=True)

    @pl.when(i == n - 1)
    def _():
        dqg_ref[...] = _fold_heads(qacc_ref[...])
        dkg_ref[...] = _fold_heads(kacc_ref[...])


def _assemble_dproj(dp_ref, dpa_ref, dqk_ref, dv_ref, dgb_ref, dpm_ref, dfb_ref):
    dp_ref[:, PA_LO:QB_LO] = dpa_ref[...]
    dp_ref[:, QB_LO:VB_LO] = dqk_ref[...]
    dp_ref[:, VB_LO:GB_LO] = dv_ref[...]
    dp_ref[:, GB_LO:PM_LO] = dgb_ref[...]
    dp_ref[:, PM_LO:FB_LO] = dpm_ref[...]
    dp_ref[:, FB_LO:PROJ_PAD] = dfb_ref[...].astype(BF16)


def _dproj_specs(t):
    return [_rows(t, 512), _rows(t, 2 * FOX_WIDTH), _rows(t, FOX_WIDTH), _rows(t, FOX_WIDTH), _rows(t, 512),
            _rows(t, LANES)]


IN_BWD_X_TILE = 256


def _in_bwd_x(x, dy, norm_g, wp, dparts, gparts, axes, smalls):
    s = x.shape[0]
    t = IN_BWD_X_TILE
    n = s // t
    na = len(gparts)
    n_dp = len(dparts)
    vec_leaves, loss_row, dw4 = smalls if smalls is not None else ((), None, None)
    nv = len(vec_leaves)
    n_small = nv + 2 if smalls is not None else 0
    small_base = _ShardReduce.SEMS * na

    def body(*refs):
        x_ref, dy_ref, g_ref, wp_ref = refs[0:4]
        dp_parts = refs[4:4 + n_dp]
        o = 4 + n_dp
        g_refs = refs[o:o + na]
        small_in = refs[o + na:o + na + n_small]
        o += na + n_small
        gx_ref, dg_ref = refs[o:o + 2]
        out_refs = refs[o + 2:o + 2 + na]
        small_out = refs[o + 2 + na:o + 2 + na + (2 if smalls is not None else 0)]
        o += 2 + na + len(small_out)
        dp_ref = refs[o]
        bufs = tuple(refs[o + 1 + k * na:o + 1 + (k + 1) * na] for k in range(5))
        rest = refs[o + 1 + 5 * na:]

        i = pl.program_id(0)
        if na or smalls is not None:
            send_sems, recv_sems, local_sems = rest[-3:]
        red = _ShardReduce(g_refs, out_refs, axes, bufs, send_sems, recv_sems, local_sems) if na else None

        @pl.when(i == 0)
        def _():
            dg_ref[...] = jnp.zeros_like(dg_ref)
            if red is not None:
                red.exchange_with_sibling()

        if red is not None:
            for k in (1, 2, 3):
                pl.when(i == k)(functools.partial(red.send_to_chip, k))
            pl.when(i == 4)(red.keep_mine)

        _assemble_dproj(dp_ref, *dp_parts)
        dh = _dot(dp_ref[...], wp_ref[...])
        xv = x_ref[...]
        rr = lax.rsqrt(jnp.mean(xv * xv, axis=-1, keepdims=True) + EPS)
        xhat = xv * rr
        scaled = dh * g_ref[...]
        gx_ref[...] = dy_ref[...] + rr * (scaled - xhat * jnp.mean(xhat * scaled, axis=-1, keepdims=True))
        dg_ref[...] += jnp.sum(dh * xhat, axis=0, keepdims=True)

        def small_all_reduce():
            leaf_refs, (loss_ref, dw4_ref) = small_in[0:nv], small_in[nv:]
            vec_out, dw4_out = small_out
            vec_mine, vec_recv, dw4_recv = rest[0:3]
            cx, cy, c = _my_place()
            me_lin = 4 * cx + 2 * cy + c

            def copy(k, src, dst, base):
                peer = (me_lin + k) % 8
                return pltpu.make_async_remote_copy(
                    src_ref=src, dst_ref=dst.at[me_lin], send_sem=send_sems.at[base + k - 1],
                    recv_sem=recv_sems.at[base + k - 1], device_id=(peer // 4, (peer // 2) % 2, peer % 2),
                    device_id_type=MESH)

            vec_mine[...] = jnp.zeros_like(vec_mine)
            vec_mine[0:1, :] = dg_ref[...]
            for (_, row, _), ref in zip(VEC_LEAVES[1:], leaf_refs):
                vec_mine[row:row + 1, 0:ref.shape[1]] = ref[...]
            vec_mine[VEC_LOSS_ROW:VEC_LOSS_ROW + 1, 0:LANES] = loss_ref[...]
            copies = [copy(k, src, dst, base) for k in range(1, 8)
                      for src, dst, base in ((vec_mine, vec_recv, small_base), (dw4_ref, dw4_recv, small_base + 7))]
            for cp in copies:
                cp.start()
            for cp in copies:
                cp.wait_recv()
            vec_recv[me_lin] = vec_mine[...]
            dw4_recv[me_lin] = dw4_ref[...]
            vtot, wtot = vec_recv[0], dw4_recv[0]
            for d in range(1, 8):
                vtot = vtot + vec_recv[d]
                wtot = wtot + dw4_recv[d]
            vec_out[...] = vtot
            dw4_out[...] = wtot
            for cp in copies:
                cp.wait_send()

        @pl.when(i == n - 1)
        def _():
            if red is not None:
                red.sum_and_share()
            if smalls is not None:
                small_all_reduce()
            if red is not None:
                red.finish()

    any_spec = pl.BlockSpec(memory_space=pl.ANY)
    scratch = [pltpu.VMEM((t, PROJ_PAD), BF16)] + _ShardReduce.scratch(gparts, axes)
    out_shape = [jax.ShapeDtypeStruct((s, D_MODEL), F32), jax.ShapeDtypeStruct((1, D_MODEL), F32)]
    out_shape += [jax.ShapeDtypeStruct(_shard_shape(g), F32) for g in gparts]
    out_specs = [_rows(t, D_MODEL), _full((1, D_MODEL))] + [any_spec] * na
    small_args = []
    if smalls is not None:
        small_args = [*vec_leaves, loss_row, dw4]
        out_shape += [jax.ShapeDtypeStruct((VEC_ROWS, D_MODEL), F32), jax.ShapeDtypeStruct(dw4.shape, F32)]
        out_specs += [_full((VEC_ROWS, D_MODEL)), _full(dw4.shape)]
        scratch += [pltpu.VMEM((VEC_ROWS, D_MODEL), F32), pltpu.VMEM((8, VEC_ROWS, D_MODEL), F32),
                    pltpu.VMEM((8,) + dw4.shape, F32)]
    if na or smalls is not None:
        n_sems = small_base + 14
        scratch += [pltpu.SemaphoreType.DMA((n_sems,)), pltpu.SemaphoreType.DMA((n_sems,)),
                    pltpu.SemaphoreType.DMA((max(_ShardReduce.LOCAL * na, 1),))]
    return pl.pallas_call(
        body, name="in_bwd_x", grid=(n,), out_shape=tuple(out_shape),
        in_specs=[_rows(t, D_MODEL), _rows(t, D_MODEL), _full((1, D_MODEL)),
                  pl.BlockSpec((PROJ_PAD, D_MODEL), lambda i: (0, 0), pipeline_mode=pl.Buffered(1))]
        + _dproj_specs(t) + [any_spec] * na + [_full(a.shape) for a in small_args],
        out_specs=tuple(out_specs), scratch_shapes=scratch, compiler_params=_params(),
    )(x, dy, norm_g, wp, *dparts, *gparts, *small_args)


def _in_bwd_w(hb, dpa, dgb, dpm, fox, gparts, axes):
    s = hb.shape[0]
    t = TILE
    n = s // t
    na = len(gparts)
    f_hi = F_ORIG_LO + FOX_HEADS
    n_in = 4 + len(fox)

    def body(*refs):
        h_ref, dpa_ref, dgb_ref, dpm_ref = refs[0:4]
        fox_refs = refs[4:n_in]
        g_refs = refs[n_in:n_in + na]
        o = n_in + na
        dw_ref, dqk_ref, dv_ref, dfb_ref, dqg_ref, dkg_ref, dbf_ref = refs[o:o + 7]
        out_refs = refs[o + 7:o + 7 + na]
        o += 7 + na
        fox_scratch = refs[o:o + 3]
        bufs = tuple(refs[o + 3 + k * na:o + 3 + (k + 1) * na] for k in range(5))
        i = pl.program_id(0)
        red = _ShardReduce(g_refs, out_refs, axes, bufs, *refs[o + 3 + 5 * na:]) if na else None

        @pl.when(i == 0)
        def _():
            dw_ref[...] = jnp.zeros_like(dw_ref)
            if red is not None:
                red.exchange_with_sibling()

        if red is not None:
            @pl.when(i == 1)
            def _():
                for k in (1, 2, 3):
                    red.send_to_chip(k)
                red.keep_mine()

        hv = h_ref[...]

        def rows_of(lo, ref, cols=slice(None)):
            def add():
                dproj = ref[:, cols]
                dw_ref[lo:lo + dproj.shape[1], :] += _dot(dproj, hv, TN)
            return add

        q_cols, k_cols = slice(0, FOX_WIDTH), slice(FOX_WIDTH, 2 * FOX_WIDTH)
        between = (rows_of(0, dpa_ref), rows_of(f_hi, dgb_ref), rows_of(f_hi + FOX_WIDTH, dpm_ref), None,
                   rows_of(QB_LO, dqk_ref, q_cols), rows_of(VB_LO, dv_ref), None, None, rows_of(KB_LO, dqk_ref, k_cols))
        _fox_post_tile(i, n, t, *fox_refs, dqk_ref, dv_ref, dfb_ref, dqg_ref, dkg_ref, dbf_ref, *fox_scratch, between)
        dw_ref[F_ORIG_LO:f_hi, :] += _dot(dfb_ref[...].astype(BF16), hv, TN)[0:FOX_HEADS, :]

        if red is not None:
            @pl.when(i == n - 1)
            def _():
                red.sum_and_share()
                red.finish()

    def rev(w):
        return _rows_rev(t, w, n)

    any_spec = pl.BlockSpec(memory_space=pl.ANY)
    row = jax.ShapeDtypeStruct((1, LANES), F32)
    scratch = [pltpu.VMEM((1, FOX_WIDTH), F32), pltpu.VMEM((1, FOX_WIDTH), F32), pltpu.VMEM((1, LANES), F32)]
    scratch += _ShardReduce.scratch(gparts, axes)
    if na:
        scratch += [pltpu.SemaphoreType.DMA((_ShardReduce.SEMS * na,)), pltpu.SemaphoreType.DMA((_ShardReduce.SEMS * na,)),
                    pltpu.SemaphoreType.DMA((_ShardReduce.LOCAL * na,))]
    return pl.pallas_call(
        body, name="in_bwd_w", grid=(n,),
        out_shape=(jax.ShapeDtypeStruct((IN_WIDTH, D_MODEL), F32), jax.ShapeDtypeStruct((s, 2 * FOX_WIDTH), BF16),
                   jax.ShapeDtypeStruct((s, FOX_WIDTH), BF16), jax.ShapeDtypeStruct((s, LANES), F32), row, row, row)
        + tuple(jax.ShapeDtypeStruct(_shard_shape(g), F32) for g in gparts),
        in_specs=[rev(D_MODEL), rev(512), rev(FOX_WIDTH), rev(512), rev(HEAD_BLOCKS), rev(HEAD_BLOCKS),
                  rev(HEAD_BLOCKS), rev(2 * FOX_WIDTH), rev(LANES), _full((1, LANES)), _full((1, FOX_WIDTH)),
                  _full((1, FOX_WIDTH))] + [any_spec] * na,
        out_specs=(pl.BlockSpec((IN_WIDTH, D_MODEL), lambda i: (0, 0), pipeline_mode=pl.Buffered(1)),
                   rev(2 * FOX_WIDTH), rev(FOX_WIDTH), rev(LANES), _full((1, LANES)), _full((1, LANES)),
                   _full((1, LANES))) + (any_spec,) * na,
        scratch_shapes=scratch, compiler_params=_params(),
    )(hb, dpa, dgb, dpm, *fox, *gparts)


def _adamw_math(w_ref, gv, m_ref, v_ref, d_ref, nm_ref, nv_ref):
    nm = ADAM_B1 * m_ref[...] + (1.0 - ADAM_B1) * gv
    nv = ADAM_B2 * v_ref[...] + (1.0 - ADAM_B2) * (gv * gv)
    m_hat = nm / (1.0 - ADAM_B1 ** ADAM_STEP)
    v_hat = nv / (1.0 - ADAM_B2 ** ADAM_STEP)
    d_ref[...] = -ADAM_LR * (m_hat / (jnp.sqrt(v_hat) + ADAM_EPS) + ADAM_WD * w_ref[...])
    nm_ref[...] = nm
    nv_ref[...] = nv


def _adamw(name, w, g, m, v):
    rows, cols = w.shape
    tc = 256 if rows * cols > 256 * 1024 else cols
    n = cols // tc

    def body(w_ref, g_ref, m_ref, v_ref, d_ref, nm_ref, nv_ref):
        _adamw_math(w_ref, g_ref[...], m_ref, v_ref, d_ref, nm_ref, nv_ref)

    spec = pl.BlockSpec((rows, tc), lambda i: (0, i))
    return pl.pallas_call(
        body, name=name, grid=(n,),
        out_shape=(jax.ShapeDtypeStruct((rows, cols), F32),) * 3,
        in_specs=[spec] * 4, out_specs=(spec,) * 3,
        compiler_params=_params(),
    )(w, g, m, v)


def _adamw_rest(vec, dw4, leaves, pool, shards):
    nl = len(VEC_LEAVES) + 1
    ns = len(shards)

    def body(*refs):
        vec_ref, dw4_ref = refs[0:2]
        wmv = refs[2:2 + 3 * nl]
        shard_in = refs[2 + 3 * nl:2 + 3 * nl + 4 * ns]
        o = 2 + 3 * nl + 4 * ns
        loss_ref = refs[o]
        outs = refs[o + 1:o + 1 + 4 * nl]
        shard_out = refs[o + 1 + 4 * nl:]
        loss_ref[...] = vec_ref[VEC_LOSS_ROW:VEC_LOSS_ROW + 1, 0:1]
        for k in range(nl):
            if k < nl - 1:
                _, row, width = VEC_LEAVES[k]
                gv = vec_ref[row:row + 1, 0:width]
            else:
                gv = dw4_ref[...]
            w_ref, m_ref, v_ref = wmv[3 * k:3 * k + 3]
            g_ref, d_ref, nm_ref, nv_ref = outs[4 * k:4 * k + 4]
            g_ref[...] = gv
            _adamw_math(w_ref, gv, m_ref, v_ref, d_ref, nm_ref, nv_ref)
        for k in range(ns):
            w_ref, g_ref, m_ref, v_ref = shard_in[4 * k:4 * k + 4]
            _adamw_math(w_ref, g_ref[...], m_ref, v_ref, *shard_out[3 * k:3 * k + 3])

    shapes = [jax.ShapeDtypeStruct((1, width), F32) for _, _, width in VEC_LEAVES] + [
        jax.ShapeDtypeStruct(dw4.shape, F32)]
    flat_in = [a for triple in list(leaves) + [pool] for a in triple] + [a for quad in shards for a in quad]
    res = pl.pallas_call(
        body, name="adamw_rest",
        out_shape=(jax.ShapeDtypeStruct((1, 1), F32),) + tuple(s for s in shapes for _ in range(4))
        + tuple(jax.ShapeDtypeStruct(quad[0].shape, F32) for quad in shards for _ in range(3)),
        compiler_params=pltpu.CompilerParams(vmem_limit_bytes=VMEM_LIMIT),
    )(vec, dw4, *flat_in)
    per = [res[1 + 4 * k:5 + 4 * k] for k in range(nl)]
    big = res[1 + 4 * nl:]
    return (res[0], [p[0] for p in per], [p[1] for p in per], [p[2] for p in per], [p[3] for p in per],
            [big[3 * k:3 * k + 3] for k in range(ns)])


def _tile_heads(g, n):
    return jnp.tile(g.reshape(1, HEAD_DIM), (1, n))


def kernel(x, mem, norm_g, w_in, b_f, w_pool, pool_scale, fox_q_g, fox_k_g, mem_norm_g, w_mem_kv, mem_q_g, mem_k_g, w_out, loss_target, m_norm_g, m_w_in, m_b_f, m_w_pool, m_pool_scale, m_fox_q_g, m_fox_k_g, m_mem_norm_g, m_w_mem_kv, m_mem_q_g, m_mem_k_g, m_w_out, v_norm_g, v_w_in, v_b_f, v_w_pool, v_pool_scale, v_fox_q_g, v_fox_k_g, v_mem_norm_g, v_w_mem_kv, v_mem_q_g, v_mem_k_g, v_w_out):
    w_in_t, m_w_in_t, v_w_in_t = w_in[0].T, m_w_in[0].T, v_w_in[0].T
    axes = (1, 0, 0)

    g_in, g_kv, g_out = _all_gather_weights([w_in_t, w_mem_kv[0], w_out[0]], axes)
    tiled = _tiled_params(b_f, fox_q_g, fox_k_g, mem_q_g, mem_k_g)
    fwd, wp = _fwd_in(x[0], norm_g, g_in, *tiled[0:3])
    w_kv_b = g_kv.reshape(D_MODEL, 2 * MEM_WIDTH)
    w_out_b = g_out.reshape(D_MODEL, D_MODEL)
    w4 = w_pool.reshape(POOL_ROWS, HEAD_DIM)
    dy, hb, dpa, dgb, dpm, fox, dw_kv, dw_out, (dmemnorm_g, dpscale, dmq_g, dmk_g), loss_row, dw4 = _local_partials(
        x[0], mem[0], loss_target[0], fwd, w_kv_b, w_out_b, tiled, w4, pool_scale, mem_norm_g)

    early = [dw_kv.reshape(4, D_MODEL // 4, 2 * MEM_WIDTH), dw_out.reshape(4, D_MODEL // 4, D_MODEL)]
    dwp, dqk, dvb, dfb, dfq_g, dfk_g, dbf, g_w_kv, g_w_out = _in_bwd_w(hb, dpa, dgb, dpm, fox, early, axes[1:])
    dparts = (dpa, dqk, dvb, dgb, dpm, dfb)
    vec_leaves = (dmemnorm_g, dpscale, dbf, dfq_g, dfk_g, dmq_g, dmk_g)
    grad_x, _, g_w_in_t, vec, dw4_sum = _in_bwd_x(
        x[0], dy, norm_g, wp, dparts, [dwp], axes[0:1], (vec_leaves, loss_row, dw4))

    small_wmv = [(norm_g, m_norm_g, v_norm_g), (mem_norm_g, m_mem_norm_g, v_mem_norm_g),
                 (pool_scale, m_pool_scale, v_pool_scale), (b_f, m_b_f, v_b_f), (fox_q_g, m_fox_q_g, v_fox_q_g),
                 (fox_k_g, m_fox_k_g, v_fox_k_g), (mem_q_g, m_mem_q_g, v_mem_q_g), (mem_k_g, m_mem_k_g, v_mem_k_g)]
    pool_wmv = tuple(a.reshape(POOL_ROWS, HEAD_DIM) for a in (w_pool, m_w_pool, v_w_pool))
    loss, *small_out, (upd_kv, upd_out) = _adamw_rest(
        vec, dw4_sum, small_wmv, pool_wmv, [(w_mem_kv[0], g_w_kv, m_w_mem_kv[0], v_w_mem_kv[0]),
                                             (w_out[0], g_w_out, m_w_out[0], v_w_out[0])])
    big = [[g_w_in_t.T[None], g_w_kv[None], g_w_out[None]]]
    upd = [[a.T for a in _adamw("adamw_w_in", w_in_t, g_w_in_t, m_w_in_t, v_w_in_t)], upd_kv, upd_out]
    big += [[u[k][None] for u in upd] for k in range(3)]

    def leaves(k):
        sm = small_out[k]
        b_in, b_kv, b_out = big[k]
        return (sm[0], b_in, sm[3], sm[8].reshape(w_pool.shape), sm[2], sm[4], sm[5], sm[1], b_kv, sm[6], sm[7], b_out)

    return (loss.reshape(()), grad_x[None], *leaves(0), *leaves(1), *leaves(2), *leaves(3))


def _tiled_params(b_f, fox_q_g, fox_k_g, mem_q_g, mem_k_g):
    return (jnp.pad(b_f, ((0, 0), (0, LANES - FOX_HEADS))), _tile_heads(fox_q_g, FOX_HEADS),
            _tile_heads(fox_k_g, FOX_HEADS), _tile_heads(mem_q_g, 4), _tile_heads(mem_k_g, 4))


def _local_partials(xs, mems, tgt, fwd, w_kv_b, w_out_b, tiled, w4, pool_scale, mem_norm_g):
    hb, pa, qk, qa, ka, va, gb, pm, fb = fwd
    bf_pad, fq_g, fk_g, mq_g, mk_g = tiled

    ma, db, mm, mnb, kv, kmn, vmb = _side_fwd(pa, pm, w4, pool_scale, mq_g, mems, mem_norm_g, w_kv_b, mk_g)
    o, mb, r4 = _fox_fwd(qa, ka, va, gb)
    dy, dma, dmm, dw_out, loss_row, doa, dgb, rr = _out_loss(xs, tgt, ma, mb, mm, w_out_b, gb, o, r4)

    dpa, dpm, dw4, dpscale, dmq_g, dw_kv, dmemnorm_g, dmk_g = _side_bwd(
        pa, db, dma, w4, pool_scale, pm, dmm, kmn, vmb, mq_g, kv, mnb, mems, w_kv_b, mk_g, mem_norm_g)
    dka, dva, dqa = _fox_bwd(ka, va, qa, doa, rr)
    fox = (dqa, dka, dva, qk, fb, bf_pad, fq_g, fk_g)
    return dy, hb, dpa, dgb, dpm, fox, dw_kv, dw_out, (dmemnorm_g, dpscale, dmq_g, dmk_g), loss_row, dw4
```

```python
import functools

import jax
import jax.numpy as jnp
from jax import lax
from jax.experimental import pallas as pl
from jax.experimental.pallas import tpu as pltpu

F32 = jnp.float32
BF16 = jnp.bfloat16
MESH = pl.DeviceIdType.MESH

D_MODEL = 1024
HEAD_DIM = 64
POOL_WIDTH = 256
FOX_WIDTH = 512
FOX_HEADS = 8
MEM_WIDTH = 256
N_MEM = 256
IN_WIDTH = 3080
EPS = 1e-6
ATT_SCALE = 0.125

ADAM_LR = 0.001
ADAM_B1 = 0.9
ADAM_B2 = 0.999
ADAM_EPS = 1e-08
ADAM_WD = 0.01
ADAM_STEP = 10

LANES = 128
PA_LO, QB_LO, KB_LO, VB_LO, GB_LO, PM_LO, FB_LO, PROJ_PAD = 0, 512, 1024, 1536, 2048, 2560, 3072, 3200
F_ORIG_LO = 2048

TILE = 512
VMEM_LIMIT = 56 * 1024 * 1024

VEC_LEAVES = (("norm_g", 0, 1024), ("mem_norm_g", 1, 1024), ("pool_scale", 2, 256), ("b_f", 3, 8),
              ("fox_q_g", 4, 64), ("fox_k_g", 5, 64), ("mem_q_g", 6, 64), ("mem_k_g", 7, 64))
VEC_LOSS_ROW = 8
VEC_ROWS = 16
POOL_ROWS = 256


def _params(n_grid=1, vmem=VMEM_LIMIT):
    return pltpu.CompilerParams(dimension_semantics=("arbitrary",) * n_grid, vmem_limit_bytes=vmem)


def _rows(t, w):
    return pl.BlockSpec((t, w), lambda i: (i, 0))


def _rows_rev(t, w, n):
    return pl.BlockSpec((t, w), lambda i: (n - 1 - i, 0))


def _full(shape):
    return pl.BlockSpec(shape, lambda i: (0,) * len(shape))


def _sig(x):
    return 1.0 / (1.0 + jnp.exp(-x))


def _lane_lo(shape):
    return lax.broadcasted_iota(jnp.int32, shape, 1) < HEAD_DIM


def _pair_sum(v, lo):
    s0 = jnp.sum(jnp.where(lo, v, 0.0), axis=-1, keepdims=True)
    s1 = jnp.sum(jnp.where(lo, 0.0, v), axis=-1, keepdims=True)
    return jnp.where(lo, s0, s1)


def _head_rms(blk, lo):
    return lax.rsqrt(_pair_sum(blk * blk, lo) * (1.0 / HEAD_DIM) + EPS)


def _head_norm_bwd(dyn, xhat, rr, g, lo):
    a = dyn * g
    return rr * (a - xhat * (_pair_sum(xhat * a, lo) * (1.0 / HEAD_DIM)))


def _fold_heads(acc):
    tot = acc[:, 0:LANES]
    for p in range(1, acc.shape[1] // LANES):
        tot = tot + acc[:, p * LANES:(p + 1) * LANES]
    return tot + pltpu.roll(tot, HEAD_DIM, axis=1)


def _lane_pick(v, lane, idx):
    return jnp.sum(jnp.where(lane == idx, v, 0.0), axis=-1, keepdims=True)


NT = (((1,), (1,)), ((), ()))
TN = (((0,), (0,)), ((), ()))


def _dot(a, b, dims=None):
    if dims is None:
        return jnp.dot(a, b, preferred_element_type=F32)
    return lax.dot_general(a, b, dims, preferred_element_type=F32)


def _my_place():
    return lax.axis_index("x"), lax.axis_index("y"), lax.axis_index("c")


def _half_dims(shape, axis):
    return (shape[0] // 2, shape[1]) if axis == 0 else (shape[0], shape[1] // 2)


def _shard_shape(g):
    return tuple(g.shape[1:]) if len(g.shape) == 3 else (g.shape[0] // 4, g.shape[1])


F32_ROWS = 8


def _shard_window(g):
    rows = _shard_shape(g)[0]
    if len(g.shape) == 3:
        return rows
    skew = max((j * rows) % F32_ROWS for j in range(4))
    return -(-(rows + skew) // F32_ROWS) * F32_ROWS


def _half_of(ref, axis, core, lead=False):
    rows, cols = ref.shape[-2:]
    if axis == 0:
        idx = (pl.ds(pl.multiple_of(core * (rows // 2), 16), rows // 2), slice(None))
    else:
        idx = (slice(None), pl.ds(pl.multiple_of(core * (cols // 2), LANES), cols // 2))
    return ref.at[(slice(None),) + idx] if lead else ref.at[idx]


class _HalfGather:
    def __init__(self, ins, outs, axes, f32_bufs, bf_bufs, send_sems, recv_sems, local_sems):
        self.ins, self.outs, self.axes = ins, outs, axes
        self.f32_bufs, self.bf_bufs = f32_bufs, bf_bufs
        self.send_sems, self.recv_sems, self.local_sems = send_sems, recv_sems, local_sems
        self.n = len(ins)
        x, y, self.c = _my_place()
        self.me, self.sibling = (x, y, self.c), (x, y, 1 - self.c)
        self.chips = [(1 - x, y), (x, 1 - y), (1 - x, 1 - y)]

    @staticmethod
    def scratch(shards, axes):
        dims = [_half_dims(a.shape, axis) for a, axis in zip(shards, axes)]
        n = len(shards)
        return [pltpu.VMEM(d, F32) for d in dims] + [pltpu.VMEM(d, BF16) for d in dims] + [
            pltpu.SemaphoreType.DMA((7 * n,)), pltpu.SemaphoreType.DMA((7 * n,)), pltpu.SemaphoreType.DMA((2 * n,))]

    @staticmethod
    def out_shapes(shards, axes):
        return tuple(jax.ShapeDtypeStruct((8,) + _half_dims(a.shape, axis), BF16) for a, axis in zip(shards, axes))

    def _blk(self, a, px, py, pc):
        return self.outs[a].at[4 * px + 2 * py + pc]

    def _copy(self, a, k, block, to, src=None):
        return pltpu.make_async_remote_copy(
            src_ref=self._blk(a, *block) if src is None else src, dst_ref=self._blk(a, *block),
            send_sem=self.send_sems.at[7 * a + k], recv_sem=self.recv_sems.at[7 * a + k], device_id=to,
            device_id_type=MESH)

    def _keep(self, a):
        return pltpu.make_async_copy(self.bf_bufs[a], self._blk(a, *self.me), self.local_sems.at[self.n + a])

    def _first(self, a):
        mine = [self._copy(a, 0, self.me, self.sibling, src=self.bf_bufs[a])]
        return mine + [self._copy(a, 1 + j, self.me, (*chip, self.c), src=self.bf_bufs[a])
                       for j, chip in enumerate(self.chips)]

    def send_mine(self):
        loads = [pltpu.make_async_copy(_half_of(self.ins[a], self.axes[a], self.c), self.f32_bufs[a],
                                       self.local_sems.at[a]) for a in range(self.n)]
        for cp in loads:
            cp.start()
        for a in range(self.n):
            loads[a].wait()
            self.bf_bufs[a][...] = self.f32_bufs[a][...].astype(BF16)
            self._keep(a).start()
            for cp in self._first(a):
                cp.start()

    def pass_on(self):
        for a in range(self.n):
            for j, chip in enumerate(self.chips):
                self._copy(a, 1 + j, (*chip, self.c), self.me).wait_recv()
                self._copy(a, 4 + j, (*chip, self.c), self.sibling).start()

    def finish(self):
        for a in range(self.n):
            self._copy(a, 0, self.sibling, self.me).wait_recv()
            for j, chip in enumerate(self.chips):
                self._copy(a, 4 + j, (*chip, 1 - self.c), self.me).wait_recv()
        for a in range(self.n):
            for cp in self._first(a):
                cp.wait_send()
            for j, chip in enumerate(self.chips):
                self._copy(a, 4 + j, (*chip, self.c), self.sibling).wait_send()
            self._keep(a).wait()


def _all_gather_weights(shards, axes):
    n = len(shards)

    def body(*refs):
        gather = _HalfGather(refs[0:n], refs[n:2 * n], axes, refs[2 * n:3 * n], refs[3 * n:4 * n], *refs[4 * n:])
        gather.send_mine()
        gather.pass_on()
        gather.finish()

    any_spec = pl.BlockSpec(memory_space=pl.ANY)
    return pl.pallas_call(
        body, name="weights_all_gather", out_shape=_HalfGather.out_shapes(shards, axes),
        in_specs=[any_spec] * n, out_specs=(any_spec,) * n, scratch_shapes=_HalfGather.scratch(shards, axes),
        compiler_params=pltpu.CompilerParams(vmem_limit_bytes=VMEM_LIMIT),
    )(*shards)


class _ShardReduce:
    SEMS = 8
    LOCAL = 5

    def __init__(self, g_refs, out_refs, axes, bufs, send_sems, recv_sems, local_sems):
        self.g_refs, self.out_refs, self.axes = g_refs, out_refs, axes
        self.recv_a, self.own_a, self.send_b, self.recv_b, self.fin = bufs
        self.send_sems, self.recv_sems, self.local_sems = send_sems, recv_sems, local_sems
        self.n = len(g_refs)
        x, y, self.c = _my_place()
        self.chip = 2 * x + y
        self.sibling = (x, y, 1 - self.c)

    @staticmethod
    def scratch(gparts, axes):
        assert all(len(g.shape) == 3 or axis == 1 for g, axis in zip(gparts, axes))
        dims = [_half_dims(_shard_shape(g), axis) for g, axis in zip(gparts, axes)]
        windows = [d if len(g.shape) == 3 else (_shard_window(g),) + d[1:] for g, d in zip(gparts, dims)]
        shapes = []
        for dtype, lead, per_array in ((F32, (4,), windows), (F32, (4,), windows), (BF16, (4,), dims),
                                       (BF16, (4,), dims), (F32, (), dims)):
            shapes += [pltpu.VMEM(lead + d, dtype) for d in per_array]
        return shapes

    def _shard_half(self, a, j, core):
        g = self.g_refs[a]
        if len(g.shape) == 3:
            return _half_of(g.at[j], self.axes[a], core)
        start = (j * _shard_shape(g)[0]) // F32_ROWS * F32_ROWS
        return _half_of(g.at[pl.ds(pl.multiple_of(start, F32_ROWS), _shard_window(g))], self.axes[a], core)

    def _to_sibling(self, a, j):
        return pltpu.make_async_remote_copy(
            src_ref=self._shard_half(a, j, 1 - self.c), dst_ref=self.recv_a[a].at[j],
            send_sem=self.send_sems.at[self.SEMS * a + j], recv_sem=self.recv_sems.at[self.SEMS * a + j], device_id=self.sibling,
            device_id_type=MESH)

    def _own(self, a, j):
        return pltpu.make_async_copy(self._shard_half(a, j, self.c), self.own_a[a].at[j],
                                     self.local_sems.at[self.LOCAL * a + j])

    def _to_chip(self, a, k):
        dest = (self.chip + k) % 4
        return pltpu.make_async_remote_copy(
            src_ref=self.send_b[a].at[dest], dst_ref=self.recv_b[a].at[self.chip],
            send_sem=self.send_sems.at[self.SEMS * a + 3 + k], recv_sem=self.recv_sems.at[self.SEMS * a + 3 + k],
            device_id=(dest // 2, dest % 2, self.c), device_id_type=MESH)

    def _give(self, a):
        return pltpu.make_async_remote_copy(
            src_ref=self.fin[a], dst_ref=_half_of(self.out_refs[a], self.axes[a], self.c),
            send_sem=self.send_sems.at[self.SEMS * a + 7], recv_sem=self.recv_sems.at[self.SEMS * a + 7], device_id=self.sibling,
            device_id_type=MESH)

    def _mine(self, a):
        return pltpu.make_async_copy(self.fin[a], _half_of(self.out_refs[a], self.axes[a], self.c),
                                     self.local_sems.at[self.LOCAL * a])

    def exchange_with_sibling(self):
        for k in (1, 2, 3, 0):
            j = (self.chip + k) % 4
            for a in range(self.n):
                self._to_sibling(a, j).start()
                self._own(a, j).start()

    def _chip_partial(self, a, j):
        self._own(a, j).wait()
        self._to_sibling(a, j).wait_recv()
        g = self.g_refs[a]
        if len(g.shape) == 3:
            self.send_b[a][j] = (self.own_a[a][j] + self.recv_a[a][j]).astype(BF16)
            return
        rows = _shard_shape(g)[0]
        for shard in range(4):
            @pl.when(j == shard)
            def _():
                at = pl.ds((shard * rows) % F32_ROWS, rows)
                self.send_b[a][shard] = (self.own_a[a][shard, at, :] + self.recv_a[a][shard, at, :]).astype(BF16)

    def send_to_chip(self, k):
        for a in range(self.n):
            self._chip_partial(a, (self.chip + k) % 4)
            self._to_chip(a, k).start()

    def keep_mine(self):
        for a in range(self.n):
            self._chip_partial(a, self.chip)
            keep = pltpu.make_async_copy(self.send_b[a].at[self.chip], self.recv_b[a].at[self.chip],
                                         self.local_sems.at[self.LOCAL * a + 4])
            keep.start()
            keep.wait()

    def sum_and_share(self):
        for a in range(self.n):
            for k in range(1, 4):
                self._to_chip(a, k).wait_recv()
            tot = self.recv_b[a][0].astype(F32) + self.recv_b[a][1].astype(F32)
            tot = tot + self.recv_b[a][2].astype(F32)
            self.fin[a][...] = tot + self.recv_b[a][3].astype(F32)
            self._give(a).start()
            self._mine(a).start()

    def finish(self):
        for a in range(self.n):
            self._give(a).wait_recv()
            self._mine(a).wait()
            self._give(a).wait_send()
            for j in range(4):
                self._to_sibling(a, j).wait_send()
            for k in range(1, 4):
                self._to_chip(a, k).wait_send()


def _mem_tokens_fwd(mem_ref, g_ref, w_ref, kg_ref, mn_ref, kv_ref, kn_ref, vm_ref):
    xm = mem_ref[...]
    rr = lax.rsqrt(jnp.mean(xm * xm, axis=-1, keepdims=True) + EPS)
    mnb = ((xm * rr) * g_ref[...]).astype(BF16)
    mn_ref[...] = mnb
    kv = _dot(mnb, w_ref[...])
    kv_ref[...] = kv
    lo = _lane_lo((xm.shape[0], LANES))
    for p in range(MEM_WIDTH // LANES):
        sl = slice(p * LANES, (p + 1) * LANES)
        kb = kv[:, sl]
        kn_ref[:, sl] = ((kb * _head_rms(kb, lo)) * kg_ref[:, sl]).astype(BF16)
    vm_ref[...] = kv[:, MEM_WIDTH:].astype(BF16)


AUG_LO = 64
KEY_SUM_LANE = 72
QUERY_SUM_LANE = 80
HEAD_BLOCKS = FOX_HEADS * LANES


def _ones3(lane):
    return jnp.where((lane >= AUG_LO) & (lane < AUG_LO + 3), 1.0, 0.0)


def _spread3(cols):
    hi = cols.astype(BF16)
    rest = cols - hi.astype(F32)
    mid = rest.astype(BF16)
    low = (rest - mid.astype(F32)).astype(BF16)
    r = lax.broadcasted_iota(jnp.int32, (LANES, HEAD_BLOCKS), 0)
    c = lax.broadcasted_iota(jnp.int32, (LANES, HEAD_BLOCKS), 1)
    out = None
    for k, part in enumerate((hi, mid, low)):
        term = _dot(part, jnp.where(c == r * LANES + (AUG_LO + k), 1.0, 0.0).astype(BF16))
        out = term if out is None else out + term
    return out


def _head_block(pair_blk, hh, lo, extras):
    src = pair_blk if hh == 0 else pltpu.roll(pair_blk, HEAD_DIM, axis=1)
    return jnp.where(lo, src, extras).astype(BF16)


def _pair_block(blk0, blk1, lo):
    return jnp.where(lo, blk0, pltpu.roll(blk1, HEAD_DIM, axis=1))


def _assemble_w_in(halves_ref, words_ref, wp_ref):
    shard = IN_WIDTH // 4
    half = D_MODEL // 2
    f_hi = F_ORIG_LO + FOX_HEADS
    for j in range(4):
        blocks = [pltpu.bitcast(halves_ref[2 * j + c], jnp.uint32) for c in range(2)]
        for lo, hi, to in ((0, F_ORIG_LO, PA_LO), (F_ORIG_LO, f_hi, FB_LO), (f_hi, IN_WIDTH, GB_LO)):
            a, b = max(lo, shard * j), min(hi, shard * (j + 1))
            if a < b:
                for c in range(2):
                    words_ref[(to + a - lo) // 2:(to + b - lo) // 2, c * half:(c + 1) * half] = (
                        blocks[c][(a - shard * j) // 2:(b - shard * j) // 2, :])
    pad_lo = (FB_LO + FOX_HEADS) // 2
    words_ref[pad_lo:, :] = jnp.zeros((PROJ_PAD // 2 - pad_lo, D_MODEL), jnp.uint32)
    wp_ref[...] = pltpu.bitcast(words_ref[...], BF16)


def _fwd_in(x, norm_g, halves, bf_pad, fq_g, fk_g):
    s = x.shape[0]
    t = TILE
    n = s // t

    def body(x_ref, ng_ref, halves_ref, bf_ref, qg_ref, kg_ref,
             h_ref, pa_ref, qk_ref, qa_ref, ka_ref, va_ref, gb_ref, pm_ref, fb_ref, wp_ref,
             carry_ref, fcol_ref, words_ref):
        @pl.when(pl.program_id(0) == 0)
        def _():
            carry_ref[...] = jnp.zeros_like(carry_ref)
            _assemble_w_in(halves_ref, words_ref, wp_ref)

        xv = x_ref[...]
        rr = lax.rsqrt(jnp.mean(xv * xv, axis=-1, keepdims=True) + EPS)
        hb = ((xv * rr) * ng_ref[...]).astype(BF16)
        h_ref[...] = hb

        def proj(lo, hi):
            return _dot(hb, wp_ref[lo:hi, :], NT)

        fb = proj(FB_LO, PROJ_PAD)
        fb_ref[...] = fb
        qk_ref[:, 0:FOX_WIDTH] = proj(QB_LO, KB_LO)

        lane = lax.broadcasted_iota(jnp.int32, (t, LANES), 1)
        row = lax.broadcasted_iota(jnp.int32, (t, LANES), 0)
        lo = lane < HEAD_DIM
        z = fb + bf_ref[...]
        lf = -(jnp.maximum(-z, 0.0) + jnp.log1p(jnp.exp(-jnp.abs(z))))
        lf = jnp.where(lane < FOX_HEADS, lf, 0.0)
        sh = 1
        while sh < t:
            lf = lf + jnp.where(row >= sh, pltpu.roll(lf, sh, axis=0), 0.0)
            sh *= 2
        fcum = lf + carry_ref[...]
        fcol_ref[...] = fcum
        carry_ref[...] = fcol_ref[t - 1:t, :]

        ones3 = _ones3(lane)
        minus_f = _spread3(-fcum)

        def head_blocks(seg, g_ref, out_ref, scale):
            for p in range(FOX_WIDTH // LANES):
                sl = slice(p * LANES, (p + 1) * LANES)
                blk = qk_ref[:, seg - QB_LO + p * LANES:seg - QB_LO + (p + 1) * LANES]
                normed = ((blk * _head_rms(blk, lo)) * g_ref[:, sl]) * scale
                for hh in range(2):
                    h = 2 * p + hh
                    if seg == QB_LO:
                        extras = jnp.where(lane == QUERY_SUM_LANE + h, 1.0, ones3)
                    else:
                        extras = jnp.where(lane == KEY_SUM_LANE + h, 1.0, minus_f[:, h * LANES:(h + 1) * LANES])
                    out_ref[:, h * LANES:(h + 1) * LANES] = _head_block(normed, hh, lo, extras)

        qk_ref[:, FOX_WIDTH:2 * FOX_WIDTH] = proj(KB_LO, VB_LO)
        pa_ref[...] = proj(PA_LO, QB_LO)
        head_blocks(QB_LO, qg_ref, qa_ref, ATT_SCALE)
        vraw = proj(VB_LO, GB_LO)
        gb_ref[...] = proj(GB_LO, PM_LO)
        head_blocks(KB_LO, kg_ref, ka_ref, 1.0)
        pm_ref[...] = proj(PM_LO, FB_LO)
        for h in range(FOX_HEADS):
            va_ref[:, h * LANES:(h + 1) * LANES] = _head_block(vraw[:, (h // 2) * LANES:(h // 2 + 1) * LANES], h % 2, lo, ones3)

    outs = (
        jax.ShapeDtypeStruct((s, D_MODEL), BF16),
        jax.ShapeDtypeStruct((s, 512), F32),
        jax.ShapeDtypeStruct((s, 2 * FOX_WIDTH), F32),
        jax.ShapeDtypeStruct((s, HEAD_BLOCKS), BF16),
        jax.ShapeDtypeStruct((s, HEAD_BLOCKS), BF16),
        jax.ShapeDtypeStruct((s, HEAD_BLOCKS), BF16),
        jax.ShapeDtypeStruct((s, FOX_WIDTH), F32),
        jax.ShapeDtypeStruct((s, 512), F32),
        jax.ShapeDtypeStruct((s, LANES), F32),
        jax.ShapeDtypeStruct((PROJ_PAD, D_MODEL), BF16),
    )

    def resident(shape):
        return pl.BlockSpec(shape, lambda i: (0,) * len(shape), pipeline_mode=pl.Buffered(1))

    *fwd, wp = pl.pallas_call(
        body, name="fwd_in", grid=(n,), out_shape=outs,
        in_specs=[_rows(t, D_MODEL), _full((1, D_MODEL)), resident(halves.shape), _full((1, LANES)),
                  _full((1, FOX_WIDTH)), _full((1, FOX_WIDTH))],
        out_specs=(_rows(t, D_MODEL), _rows(t, 512), _rows(t, 2 * FOX_WIDTH), _rows(t, HEAD_BLOCKS),
                   _rows(t, HEAD_BLOCKS), _rows(t, HEAD_BLOCKS), _rows(t, FOX_WIDTH), _rows(t, 512),
                   _rows(t, LANES), resident((PROJ_PAD, D_MODEL))),
        scratch_shapes=[pltpu.VMEM((1, LANES), F32), pltpu.VMEM((t, LANES), F32),
                        pltpu.VMEM((PROJ_PAD // 2, D_MODEL), jnp.uint32)],
        compiler_params=_params(),
    )(x, norm_g, halves, bf_pad, fq_g, fk_g)
    return tuple(fwd), wp


POOL_HALO = 16


def _pool_window(lane):
    return jnp.where(lane < 64, 2.0, jnp.where(lane < 128, 4.0, jnp.where(lane < 192, 8.0, 16.0)))


def _pool_pick(lane, s2, s4, s8, s16):
    return jnp.where(lane < 64, s2, jnp.where(lane < 128, s4, jnp.where(lane < 192, s8, s16)))


def _group_onehot(shape, row_is_group_lane):
    r = lax.broadcasted_iota(jnp.int32, shape, 0)
    c = lax.broadcasted_iota(jnp.int32, shape, 1)
    hit = (r % HEAD_DIM == c) if row_is_group_lane else (c % HEAD_DIM == r)
    return jnp.where(hit, 1.0, 0.0).astype(F32)


def _same_group(shape):
    r = lax.broadcasted_iota(jnp.int32, shape, 0)
    c = lax.broadcasted_iota(jnp.int32, shape, 1)
    return (r // HEAD_DIM) == (c // HEAD_DIM)


def _pool_block_diag(w4):
    spread = jnp.dot(w4, _group_onehot((HEAD_DIM, POOL_WIDTH), False), preferred_element_type=F32,
                     precision=lax.Precision.HIGHEST)
    return jnp.where(_same_group((POOL_WIDTH, POOL_WIDTH)), spread, 0.0).astype(BF16)


def _mem_softmax(qm, kp):
    sc = _dot(qm, kp, NT)
    e = jnp.exp(sc - jnp.max(sc, axis=-1, keepdims=True))
    return e * (1.0 / jnp.sum(e, axis=-1, keepdims=True))


def _side_fwd(pa, pm, w4, pscale, mq_g, mem, mem_norm_g, w_kv, mk_g):
    s = pa.shape[0]
    t = TILE
    n = s // t
    ext = t + POOL_HALO
    nm = mem.shape[0]

    def body(pa_ref, pm_ref, w4_ref, sc_ref, g_ref, mem_ref, mg_ref, wkv_ref, kg_ref,
             ma_ref, d_ref, mm_ref, mn_ref, kv_ref, k_ref, v_ref, ext_ref, w_ref):
        i = pl.program_id(0)

        @pl.when(i == 0)
        def _():
            ext_ref[0:POOL_HALO, :] = jnp.zeros((POOL_HALO, POOL_WIDTH), F32)
            w_ref[...] = _pool_block_diag(w4_ref[...])
            _mem_tokens_fwd(mem_ref, mg_ref, wkv_ref, kg_ref, mn_ref, kv_ref, k_ref, v_ref)

        u = pa_ref[:, 0:POOL_WIDTH]
        ext_ref[POOL_HALO:ext, :] = u
        e = ext_ref[...]
        s2 = e + pltpu.roll(e, 1, axis=0)
        s4 = s2 + pltpu.roll(s2, 2, axis=0)
        s8 = s4 + pltpu.roll(s4, 4, axis=0)
        s16 = s8 + pltpu.roll(s8, 8, axis=0)
        lane_e = lax.broadcasted_iota(jnp.int32, (ext, POOL_WIDTH), 1)
        win = _pool_pick(lane_e, s2, s4, s8, s16)[POOL_HALO:ext, :]
        lane = lax.broadcasted_iota(jnp.int32, (t, POOL_WIDTH), 1)
        pos = (lax.broadcasted_iota(jnp.int32, (t, POOL_WIDTH), 0) + (i * t + 1)).astype(F32)
        d = win / jnp.minimum(pos, _pool_window(lane)) - u
        db = d.astype(BF16)
        d_ref[...] = db
        ya = _dot(db, w_ref[...]) * sc_ref[...]
        ga = pa_ref[:, POOL_WIDTH:2 * POOL_WIDTH]
        ma_ref[...] = (ya * (ga * _sig(ga))).astype(BF16)
        ext_ref[0:POOL_HALO, :] = ext_ref[t:ext, :]

        lo = _lane_lo((t, LANES))
        for p in range(MEM_WIDTH // LANES):
            sl = slice(p * LANES, (p + 1) * LANES)
            qb = pm_ref[:, sl]
            qs = (((qb * _head_rms(qb, lo)) * g_ref[:, sl]) * ATT_SCALE).astype(BF16)
            kp = k_ref[:, sl]
            vp = v_ref[:, sl]
            outs = []
            for hh in range(2):
                msk = lo if hh == 0 else jnp.logical_not(lo)
                prob = _mem_softmax(jnp.where(msk, qs, jnp.zeros_like(qs)), kp)
                outs.append(_dot(prob.astype(BF16), vp))
            o = jnp.where(lo, outs[0], outs[1])
            gm = pm_ref[:, MEM_WIDTH + p * LANES:MEM_WIDTH + (p + 1) * LANES]
            mm_ref[:, sl] = (o * (gm * _sig(gm))).astype(BF16)

    return pl.pallas_call(
        body, name="side_fwd", grid=(n,),
        out_shape=(jax.ShapeDtypeStruct((s, POOL_WIDTH), BF16), jax.ShapeDtypeStruct((s, POOL_WIDTH), BF16),
                   jax.ShapeDtypeStruct((s, MEM_WIDTH), BF16), jax.ShapeDtypeStruct((nm, D_MODEL), BF16),
                   jax.ShapeDtypeStruct((nm, 2 * MEM_WIDTH), F32), jax.ShapeDtypeStruct((nm, MEM_WIDTH), BF16),
                   jax.ShapeDtypeStruct((nm, MEM_WIDTH), BF16)),
        in_specs=[_rows(t, 512), _rows(t, 512), _full((POOL_ROWS, HEAD_DIM)), _full((1, POOL_WIDTH)),
                  _full((1, MEM_WIDTH)), _full((nm, D_MODEL)), _full((1, D_MODEL)), _full((D_MODEL, 2 * MEM_WIDTH)),
                  _full((1, MEM_WIDTH))],
        out_specs=(_rows(t, POOL_WIDTH), _rows(t, POOL_WIDTH), _rows(t, MEM_WIDTH), _full((nm, D_MODEL)),
                   _full((nm, 2 * MEM_WIDTH)), _full((nm, MEM_WIDTH)), _full((nm, MEM_WIDTH))),
        scratch_shapes=[pltpu.VMEM((ext, POOL_WIDTH), F32), pltpu.VMEM((POOL_WIDTH, POOL_WIDTH), BF16)],
        compiler_params=_params(),
    )(pa, pm, w4, pscale, mq_g, mem, mem_norm_g, w_kv, mk_g)


FOX_FWD_HEADS = 4


def _fox_fwd(qa, ka, va, gb):
    s = qa.shape[0]
    t = TILE
    n = s // t
    heads = FOX_FWD_HEADS
    pairs = heads // 2
    group_w = heads * LANES

    def body(qa_ref, ka_ref, va_ref, gb_ref, o_ref, mb_ref, r_ref):
        i = pl.program_id(1)
        lane = lax.broadcasted_iota(jnp.int32, (t, LANES), 1)
        lo = lane < HEAD_DIM
        causal = lax.broadcasted_iota(jnp.int32, (t, t), 1) <= lax.broadcasted_iota(jnp.int32, (t, t), 0)
        qas = [qa_ref[:, hh * LANES:(hh + 1) * LANES] for hh in range(heads)]

        def step(j, carry, masked):
            rows = pl.ds(pl.multiple_of(j * t, t), t)
            def logits(hh):
                sc = _dot(qas[hh], ka_ref[rows, hh * LANES:(hh + 1) * LANES], NT)
                return jnp.where(causal, sc, -1e30) if masked else sc

            def advance(hh, sc):
                m, acc = carry[hh]
                m_new = jnp.maximum(m, jnp.max(sc, axis=-1, keepdims=True))
                p = jnp.exp(sc - m_new).astype(BF16)
                return m_new, jnp.exp(m - m_new) * acc + _dot(p, va_ref[rows, hh * LANES:(hh + 1) * LANES])

            new = []
            sc = logits(0)
            for hh in range(heads):
                sc_next = logits(hh + 1) if hh + 1 < heads else None
                new.append(advance(hh, sc))
                sc = sc_next
            return tuple(new)

        init = (jnp.full((t, 1), -1e30, F32), jnp.zeros((t, LANES), F32))
        carry = lax.fori_loop(0, i, functools.partial(step, masked=False), (init,) * heads)
        res = step(i, carry, masked=True)
        for p in range(pairs):
            outs = []
            rcol = jnp.zeros((t, LANES), F32)
            for hh in range(2):
                m, acc = res[2 * p + hh]
                l = _lane_pick(acc, lane, AUG_LO)
                outs.append(acc * (1.0 / l))
                rcol = jnp.where(lane == hh, m + jnp.log(l), rcol)
            o = _pair_block(outs[0], outs[1], lo)
            sl = slice(p * LANES, (p + 1) * LANES)
            o_ref[:, sl] = o
            g = gb_ref[:, sl]
            mb_ref[:, sl] = (o * (g * _sig(g))).astype(BF16)
            r_ref[p] = rcol

    tile_spec = pl.BlockSpec((t, pairs * LANES), lambda p, i: (i, p))
    full_spec = pl.BlockSpec((s, group_w), lambda p, i: (0, p))
    return pl.pallas_call(
        body, name="fox_fwd", grid=(FOX_HEADS // heads, n),
        out_shape=(jax.ShapeDtypeStruct((s, FOX_WIDTH), F32), jax.ShapeDtypeStruct((s, FOX_WIDTH), BF16),
                   jax.ShapeDtypeStruct((FOX_HEADS // 2, s, LANES), F32)),
        in_specs=[pl.BlockSpec((t, group_w), lambda p, i: (i, p)), full_spec, full_spec, tile_spec],
        out_specs=(tile_spec, tile_spec, pl.BlockSpec((pairs, t, LANES), lambda p, i: (p, i, 0))),
        compiler_params=_params(2),
    )(qa, ka, va, gb)


def _out_loss(x, tgt, ma, mb, mm, wout, gb, o, r4):
    s = x.shape[0]
    t = TILE
    n = s // t
    pairs = FOX_HEADS // 2

    def body(x_ref, t_ref, ma_ref, mb_ref, mm_ref, w_ref, gb_ref, o_ref, r_ref,
             dy_ref, dma_ref, dmm_ref, dw_ref, loss_ref, doa_ref, dgb_ref, rr_ref, mix_ref):
        @pl.when(pl.program_id(0) == 0)
        def _():
            dw_ref[...] = jnp.zeros_like(dw_ref)
            loss_ref[...] = jnp.zeros_like(loss_ref)

        mix_ref[:, 0:256] = ma_ref[...]
        mix_ref[:, 256:768] = mb_ref[...]
        mix_ref[:, 768:1024] = mm_ref[...]
        mix = mix_ref[...]
        err = (x_ref[...] + _dot(mix, w_ref[...])) - t_ref[...]
        row_mean = jnp.sum(err * err, axis=-1, keepdims=True) * (1.0 / D_MODEL)
        loss_ref[...] += 0.5 * jnp.sum(row_mean, axis=0, keepdims=True)
        dy = err * (1.0 / D_MODEL)
        dy_ref[...] = dy
        dyb = dy.astype(BF16)
        dmix = _dot(dyb, w_ref[...], NT)
        dma_ref[...] = dmix[:, 0:256]
        dmm_ref[...] = dmix[:, 768:1024]
        dw_ref[...] += _dot(mix, dyb, TN)

        lane = lax.broadcasted_iota(jnp.int32, (t, LANES), 1)
        lo = lane < HEAD_DIM
        d_os = []
        delta = jnp.zeros((t, LANES), F32)
        for p in range(pairs):
            sl = slice(p * LANES, (p + 1) * LANES)
            g = gb_ref[:, sl]
            sg = _sig(g)
            dm = dmix[:, 256 + p * LANES:256 + (p + 1) * LANES]
            ov = o_ref[:, sl]
            d_o = dm * (g * sg)
            d_os.append(d_o)
            dgb_ref[:, sl] = (dm * ov * (sg * (1.0 + g * (1.0 - sg)))).astype(BF16)
            prod = d_o * ov
            delta = jnp.where(lane == 2 * p, jnp.sum(jnp.where(lo, prod, 0.0), axis=-1, keepdims=True), delta)
            delta = jnp.where(lane == 2 * p + 1, jnp.sum(jnp.where(lo, 0.0, prod), axis=-1, keepdims=True), delta)
            rr_ref[p, 0] = r_ref[p].T[0:8, :]
        minus_delta = _spread3(-delta)
        for h in range(FOX_HEADS):
            blk = slice(h * LANES, (h + 1) * LANES)
            doa_ref[:, blk] = _head_block(d_os[h // 2], h % 2, lo, minus_delta[:, blk])

    return pl.pallas_call(
        body, name="out_loss", grid=(n,),
        out_shape=(jax.ShapeDtypeStruct((s, D_MODEL), F32), jax.ShapeDtypeStruct((s, 256), F32),
                   jax.ShapeDtypeStruct((s, 256), F32), jax.ShapeDtypeStruct((D_MODEL, D_MODEL), F32),
                   jax.ShapeDtypeStruct((1, LANES), F32), jax.ShapeDtypeStruct((s, HEAD_BLOCKS), BF16),
                   jax.ShapeDtypeStruct((s, FOX_WIDTH), BF16), jax.ShapeDtypeStruct((pairs, n, 8, t), F32)),
        in_specs=[_rows(t, D_MODEL), _rows(t, D_MODEL), _rows(t, 256), _rows(t, 512), _rows(t, 256),
                  _full((D_MODEL, D_MODEL)), _rows(t, FOX_WIDTH), _rows(t, FOX_WIDTH),
                  pl.BlockSpec((pairs, t, LANES), lambda i: (0, i, 0))],
        out_specs=(_rows(t, D_MODEL), _rows(t, 256), _rows(t, 256), _full((D_MODEL, D_MODEL)), _full((1, LANES)),
                   _rows(t, HEAD_BLOCKS), _rows(t, FOX_WIDTH), pl.BlockSpec((pairs, 1, 8, t), lambda i: (0, i, 0, 0))),
        scratch_shapes=[pltpu.VMEM((t, D_MODEL), BF16)],
        compiler_params=_params(),
    )(x, tgt, ma, mb, mm, wout, gb, o, r4)


def _side_bwd(pa, db, dma, w4, pscale, pm, dmm, kmn, vmb, mq_g, kv, mnb, mem, w_kv, mk_g, mem_norm_g):
    s = pa.shape[0]
    t = TILE
    n = s // t
    ext = t + POOL_HALO
    nm = mem.shape[0]

    def body(pa_ref, d_ref, dma_ref, w4_ref, sc_ref, pm_ref, dmm_ref, k_ref, v_ref, g_ref,
             kv_ref, mn_ref, mem_ref, wkv_ref, kg_ref, mg_ref,
             dpa_ref, dpm_ref, dw4_ref, dsc_ref, dg_ref, dwkv_ref, dmg_ref, dkg_ref,
             ext_ref, w_ref, dw_ref, dk_ref, dv_ref, gacc_ref, dkv_ref):
        i = pl.program_id(0)

        @pl.when(i == 0)
        def _():
            dw_ref[...] = jnp.zeros_like(dw_ref)
            dsc_ref[...] = jnp.zeros_like(dsc_ref)
            ext_ref[t:ext, :] = jnp.zeros((POOL_HALO, POOL_WIDTH), F32)
            w_ref[...] = _pool_block_diag(w4_ref[...])
            dk_ref[...] = jnp.zeros_like(dk_ref)
            dv_ref[...] = jnp.zeros_like(dv_ref)
            gacc_ref[...] = jnp.zeros_like(gacc_ref)

        dbv = d_ref[...]
        z = _dot(dbv, w_ref[...])
        ga = pa_ref[:, POOL_WIDTH:2 * POOL_WIDTH]
        sg = _sig(ga)
        dma_v = dma_ref[...]
        dya = dma_v * (ga * sg)
        dpa_ref[:, POOL_WIDTH:2 * POOL_WIDTH] = (dma_v * (z * sc_ref[...]) * (sg * (1.0 + ga * (1.0 - sg)))).astype(BF16)
        dsc_ref[...] += jnp.sum(dya * z, axis=0, keepdims=True)
        dzb = (dya * sc_ref[...]).astype(BF16)
        dw_ref[...] += _dot(dbv, dzb, TN)
        dd = _dot(dzb, w_ref[...], NT)
        lane = lax.broadcasted_iota(jnp.int32, (t, POOL_WIDTH), 1)
        pos = (lax.broadcasted_iota(jnp.int32, (t, POOL_WIDTH), 0) + ((n - 1 - i) * t + 1)).astype(F32)
        ext_ref[0:t, :] = dd / jnp.minimum(pos, _pool_window(lane))
        e = ext_ref[...]
        s2 = e + pltpu.roll(e, ext - 1, axis=0)
        s4 = s2 + pltpu.roll(s2, ext - 2, axis=0)
        s8 = s4 + pltpu.roll(s4, ext - 4, axis=0)
        s16 = s8 + pltpu.roll(s8, ext - 8, axis=0)
        lane_e = lax.broadcasted_iota(jnp.int32, (ext, POOL_WIDTH), 1)
        win = _pool_pick(lane_e, s2, s4, s8, s16)[0:t, :]
        dpa_ref[:, 0:POOL_WIDTH] = (win - dd).astype(BF16)
        ext_ref[t:ext, :] = ext_ref[0:POOL_HALO, :]

        lo = _lane_lo((t, LANES))
        for p in range(MEM_WIDTH // LANES):
            sl = slice(p * LANES, (p + 1) * LANES)
            qb = pm_ref[:, sl]
            rr = _head_rms(qb, lo)
            qhat = qb * rr
            g = g_ref[:, sl]
            qs = ((qhat * g) * ATT_SCALE).astype(BF16)
            gm = pm_ref[:, MEM_WIDTH + p * LANES:MEM_WIDTH + (p + 1) * LANES]
            sg = _sig(gm)
            dmo = dmm_ref[:, sl]
            d_o = dmo * (gm * sg)
            kp = k_ref[:, sl]
            vp = v_ref[:, sl]
            outs, dqs = [], []
            for hh in range(2):
                msk = lo if hh == 0 else jnp.logical_not(lo)
                qm = jnp.where(msk, qs, jnp.zeros_like(qs))
                prob = _mem_softmax(qm, kp)
                pb = prob.astype(BF16)
                outs.append(_dot(pb, vp))
                dom = jnp.where(msk, d_o, 0.0).astype(BF16)
                dp = _dot(dom, vp, NT)
                ds = (prob * (dp - jnp.sum(prob * dp, axis=-1, keepdims=True))).astype(BF16)
                dqs.append(_dot(ds, kp))
                dk_ref[:, sl] += _dot(ds, qm, TN)
                dv_ref[:, sl] += _dot(pb, dom, TN)
            o = jnp.where(lo, outs[0], outs[1])
            dqn = jnp.where(lo, dqs[0], dqs[1]) * ATT_SCALE
            dpm_ref[:, sl] = _head_norm_bwd(dqn, qhat, rr, g, lo).astype(BF16)
            dpm_ref[:, MEM_WIDTH + p * LANES:MEM_WIDTH + (p + 1) * LANES] = (
                dmo * o * (sg * (1.0 + gm * (1.0 - sg)))).astype(BF16)
            gacc_ref[:, sl] += jnp.sum(dqn * qhat, axis=0, keepdims=True)

        @pl.when(i == n - 1)
        def _():
            own = jnp.where(_same_group((POOL_WIDTH, POOL_WIDTH)), dw_ref[...], 0.0)
            dw4_ref[...] = jnp.dot(own, _group_onehot((POOL_WIDTH, HEAD_DIM), True), preferred_element_type=F32,
                                   precision=lax.Precision.HIGHEST)
            dg_ref[...] = _fold_heads(gacc_ref[...])

            lo_m = _lane_lo((nm, LANES))
            kacc = []
            for p in range(MEM_WIDTH // LANES):
                sl = slice(p * LANES, (p + 1) * LANES)
                kb = kv_ref[:, sl]
                rr = _head_rms(kb, lo_m)
                khat = kb * rr
                dk = dk_ref[:, sl]
                dkv_ref[:, sl] = _head_norm_bwd(dk, khat, rr, kg_ref[:, sl], lo_m).astype(BF16)
                kacc.append(jnp.sum(dk * khat, axis=0, keepdims=True))
            dkg_ref[...] = _fold_heads(jnp.concatenate(kacc, axis=1))
            dkv_ref[:, MEM_WIDTH:] = dv_ref[...].astype(BF16)
            dkv = dkv_ref[...]
            dwkv_ref[...] = _dot(mn_ref[...], dkv, TN)
            dmn = _dot(dkv, wkv_ref[...], NT)
            xm = mem_ref[...]
            rr = lax.rsqrt(jnp.mean(xm * xm, axis=-1, keepdims=True) + EPS)
            dmg_ref[...] = jnp.sum(dmn * (xm * rr), axis=0, keepdims=True)

    def rev(w):
        return _rows_rev(t, w, n)

    row = jax.ShapeDtypeStruct((1, LANES), F32)
    return pl.pallas_call(
        body, name="side_bwd", grid=(n,),
        out_shape=(jax.ShapeDtypeStruct((s, 512), BF16), jax.ShapeDtypeStruct((s, 512), BF16),
                   jax.ShapeDtypeStruct((POOL_ROWS, HEAD_DIM), F32), jax.ShapeDtypeStruct((1, POOL_WIDTH), F32), row,
                   jax.ShapeDtypeStruct((D_MODEL, 2 * MEM_WIDTH), F32), jax.ShapeDtypeStruct((1, D_MODEL), F32), row),
        in_specs=[rev(512), rev(POOL_WIDTH), rev(POOL_WIDTH), _full((POOL_ROWS, HEAD_DIM)), _full((1, POOL_WIDTH)),
                  rev(512), rev(MEM_WIDTH), _full((N_MEM, MEM_WIDTH)), _full((N_MEM, MEM_WIDTH)), _full((1, MEM_WIDTH)),
                  _full((nm, 2 * MEM_WIDTH)), _full((nm, D_MODEL)), _full((nm, D_MODEL)),
                  _full((D_MODEL, 2 * MEM_WIDTH)), _full((1, MEM_WIDTH)), _full((1, D_MODEL))],
        out_specs=(rev(512), rev(512), _full((POOL_ROWS, HEAD_DIM)), _full((1, POOL_WIDTH)), _full((1, LANES)),
                   _full((D_MODEL, 2 * MEM_WIDTH)), _full((1, D_MODEL)), _full((1, LANES))),
        scratch_shapes=[pltpu.VMEM((ext, POOL_WIDTH), F32), pltpu.VMEM((POOL_WIDTH, POOL_WIDTH), BF16),
                        pltpu.VMEM((POOL_WIDTH, POOL_WIDTH), F32), pltpu.VMEM((N_MEM, MEM_WIDTH), F32),
                        pltpu.VMEM((N_MEM, MEM_WIDTH), F32), pltpu.VMEM((1, MEM_WIDTH), F32),
                        pltpu.VMEM((nm, 2 * MEM_WIDTH), BF16)],
        compiler_params=_params(),
    )(pa, db, dma, w4, pscale, pm, dmm, kmn, vmb, mq_g, kv, mnb, mem, w_kv, mk_g, mem_norm_g)


FOX_BWD_HEADS = 4


def _fox_bwd(ka, va, qa, doa, rr):
    s = ka.shape[0]
    t = TILE
    n = s // t
    heads = FOX_BWD_HEADS
    group_w = heads * LANES

    def body(ka_ref, va_ref, qa_ref, doa_ref, rr_ref, dka_ref, dva_ref, dqa_ref):
        j = pl.program_id(1)

        @pl.when(j == 0)
        def _():
            dqa_ref[...] = jnp.zeros_like(dqa_ref)

        causal = lax.broadcasted_iota(jnp.int32, (t, t), 0) <= lax.broadcasted_iota(jnp.int32, (t, t), 1)
        kas = [ka_ref[:, hh * LANES:(hh + 1) * LANES] for hh in range(heads)]
        vas = [va_ref[:, hh * LANES:(hh + 1) * LANES] for hh in range(heads)]

        def step(i, carry, masked):
            rows = pl.ds(pl.multiple_of(i * t, t), t)
            new = []
            for hh in range(heads):
                cols = slice(hh * LANES, (hh + 1) * LANES)
                dk_a, dv_a = carry[hh]
                qb = qa_ref[rows, cols]
                d_o = doa_ref[rows, cols]
                arg = _dot(kas[hh], qb, NT) - rr_ref[hh // 2, i, hh % 2:hh % 2 + 1, :]
                if masked:
                    arg = jnp.where(causal, arg, -1e30)
                pt = jnp.exp(arg)
                dst = (pt * _dot(vas[hh], d_o, NT)).astype(BF16)
                dv_a = dv_a + _dot(pt.astype(BF16), d_o)
                dk_a = dk_a + _dot(dst, qb)
                dqa_ref[rows, cols] += _dot(dst, kas[hh], TN)
                new.append((dk_a, dv_a))
            return tuple(new)

        zero = jnp.zeros((t, LANES), F32)
        carry = step(j, ((zero, zero),) * heads, masked=True)
        res = lax.fori_loop(j + 1, n, functools.partial(step, masked=False), carry)
        for hh in range(heads):
            cols = slice(hh * LANES, (hh + 1) * LANES)
            dka_ref[:, cols] = res[hh][0]
            dva_ref[:, cols] = res[hh][1]

    tile_spec = pl.BlockSpec((t, group_w), lambda p, j: (j, p))
    full_spec = pl.BlockSpec((s, group_w), lambda p, j: (0, p))
    return pl.pallas_call(
        body, name="fox_bwd", grid=(FOX_HEADS // heads, n),
        out_shape=(jax.ShapeDtypeStruct((s, HEAD_BLOCKS), F32),) * 3,
        in_specs=[tile_spec, tile_spec, full_spec, full_spec,
                  pl.BlockSpec((heads // 2, n, 8, t), lambda p, j: (p, 0, 0, 0))],
        out_specs=(tile_spec, tile_spec, full_spec),
        compiler_params=_params(2),
    )(ka, va, qa, doa, rr)


def _fox_post_tile(i, n, t, dqa_ref, dka_ref, dva_ref, qk_ref, fb_ref, bf_ref, qg_ref, kg_ref,
                   dqk_ref, dv_ref, dfb_ref, dqg_ref, dkg_ref, dbf_ref, qacc_ref, kacc_ref, carry_ref,
                   between):
    @pl.when(i == 0)
    def _():
        qacc_ref[...] = jnp.zeros_like(qacc_ref)
        kacc_ref[...] = jnp.zeros_like(kacc_ref)
        dbf_ref[...] = jnp.zeros_like(dbf_ref)
        carry_ref[...] = jnp.zeros_like(carry_ref)

    lane = lax.broadcasted_iota(jnp.int32, (t, LANES), 1)
    row = lax.broadcasted_iota(jnp.int32, (t, LANES), 0)
    lo = lane < HEAD_DIM

    def head_blocks(ref, p):
        return ref[:, 2 * p * LANES:(2 * p + 1) * LANES], ref[:, (2 * p + 1) * LANES:(2 * p + 2) * LANES]

    def issue(k):
        if between[k] is not None:
            between[k]()

    sums = []
    pairs = FOX_WIDTH // LANES
    for side, (src_ref, g_ref, acc_ref, scale) in enumerate(((dqa_ref, qg_ref, qacc_ref, ATT_SCALE),
                                                             (dka_ref, kg_ref, kacc_ref, 1.0))):
        total = jnp.zeros((t, LANES), F32)
        for p in range(pairs):
            issue(side * pairs + p)
            sl = slice(p * LANES, (p + 1) * LANES)
            cols = slice(side * FOX_WIDTH + p * LANES, side * FOX_WIDTH + (p + 1) * LANES)
            if side == 0:
                dv_ref[:, sl] = _pair_block(*head_blocks(dva_ref, p), lo).astype(BF16)
            d0, d1 = head_blocks(src_ref, p)
            total = total + (d0 + d1)
            raw = qk_ref[:, cols]
            rr = _head_rms(raw, lo)
            xhat = raw * rr
            dn = _pair_block(d0, d1, lo) * scale
            dqk_ref[:, cols] = _head_norm_bwd(dn, xhat, rr, g_ref[:, sl], lo).astype(BF16)
            acc_ref[:, sl] += jnp.sum(dn * xhat, axis=0, keepdims=True)
        sums.append(total)
    issue(2 * pairs)
    dq_sum, dk_sum = sums

    acc = (pltpu.roll(dq_sum, LANES - KEY_SUM_LANE, axis=1) - pltpu.roll(dk_sum, LANES - QUERY_SUM_LANE, axis=1))
    acc = jnp.where(lane < FOX_HEADS, acc, 0.0)
    sh = 1
    while sh < t:
        acc = acc + jnp.where(row < t - sh, pltpu.roll(acc, t - sh, axis=0), 0.0)
        sh *= 2
    dlogf = acc + carry_ref[...]
    dfb_ref[...] = dlogf
    carry_ref[...] = dfb_ref[0:1, :]
    z = fb_ref[...] + bf_ref[...]
    dz = jnp.where(lane < FOX_HEADS, dlogf * (1.0 / (1.0 + jnp.exp(z))), 0.0)
    dfb_ref[...] = dz
    dbf_ref[...] += jnp.sum(dz, axis=0, keepdims=True)

    @pl.when(i == n - 1)
    def _():
        dqg_ref[...] = _fold_heads(qacc_ref[...])
        dkg_ref[...] = _fold_heads(kacc_ref[...])


def _assemble_dproj(dp_ref, dpa_ref, dqk_ref, dv_ref, dgb_ref, dpm_ref, dfb_ref):
    dp_ref[:, PA_LO:QB_LO] = dpa_ref[...]
    dp_ref[:, QB_LO:VB_LO] = dqk_ref[...]
    dp_ref[:, VB_LO:GB_LO] = dv_ref[...]
    dp_ref[:, GB_LO:PM_LO] = dgb_ref[...]
    dp_ref[:, PM_LO:FB_LO] = dpm_ref[...]
    dp_ref[:, FB_LO:PROJ_PAD] = dfb_ref[...].astype(BF16)


def _dproj_specs(t):
    return [_rows(t, 512), _rows(t, 2 * FOX_WIDTH), _rows(t, FOX_WIDTH), _rows(t, FOX_WIDTH), _rows(t, 512),
            _rows(t, LANES)]


IN_BWD_X_TILE = 256


def _in_bwd_x(x, dy, norm_g, wp, dparts, gparts, axes, smalls):
    s = x.shape[0]
    t = IN_BWD_X_TILE
    n = s // t
    na = len(gparts)
    n_dp = len(dparts)
    vec_leaves, loss_row, dw4 = smalls if smalls is not None else ((), None, None)
    nv = len(vec_leaves)
    n_small = nv + 2 if smalls is not None else 0
    small_base = _ShardReduce.SEMS * na

    def body(*refs):
        x_ref, dy_ref, g_ref, wp_ref = refs[0:4]
        dp_parts = refs[4:4 + n_dp]
        o = 4 + n_dp
        g_refs = refs[o:o + na]
        small_in = refs[o + na:o + na + n_small]
        o += na + n_small
        gx_ref, dg_ref = refs[o:o + 2]
        out_refs = refs[o + 2:o + 2 + na]
        small_out = refs[o + 2 + na:o + 2 + na + (2 if smalls is not None else 0)]
        o += 2 + na + len(small_out)
        dp_ref = refs[o]
        bufs = tuple(refs[o + 1 + k * na:o + 1 + (k + 1) * na] for k in range(5))
        rest = refs[o + 1 + 5 * na:]

        i = pl.program_id(0)
        if na or smalls is not None:
            send_sems, recv_sems, local_sems = rest[-3:]
        red = _ShardReduce(g_refs, out_refs, axes, bufs, send_sems, recv_sems, local_sems) if na else None

        @pl.when(i == 0)
        def _():
            dg_ref[...] = jnp.zeros_like(dg_ref)
            if red is not None:
                red.exchange_with_sibling()

        if red is not None:
            for k in (1, 2, 3):
                pl.when(i == k)(functools.partial(red.send_to_chip, k))
            pl.when(i == 4)(red.keep_mine)

        _assemble_dproj(dp_ref, *dp_parts)
        dh = _dot(dp_ref[...], wp_ref[...])
        xv = x_ref[...]
        rr = lax.rsqrt(jnp.mean(xv * xv, axis=-1, keepdims=True) + EPS)
        xhat = xv * rr
        scaled = dh * g_ref[...]
        gx_ref[...] = dy_ref[...] + rr * (scaled - xhat * jnp.mean(xhat * scaled, axis=-1, keepdims=True))
        dg_ref[...] += jnp.sum(dh * xhat, axis=0, keepdims=True)

        def small_all_reduce():
            leaf_refs, (loss_ref, dw4_ref) = small_in[0:nv], small_in[nv:]
            vec_out, dw4_out = small_out
            vec_mine, vec_recv, dw4_recv = rest[0:3]
            cx, cy, c = _my_place()
            me_lin = 4 * cx + 2 * cy + c

            def copy(k, src, dst, base):
                peer = (me_lin + k) % 8
                return pltpu.make_async_remote_copy(
                    src_ref=src, dst_ref=dst.at[me_lin], send_sem=send_sems.at[base + k - 1],
                    recv_sem=recv_sems.at[base + k - 1], device_id=(peer // 4, (peer // 2) % 2, peer % 2),
                    device_id_type=MESH)

            vec_mine[...] = jnp.zeros_like(vec_mine)
            vec_mine[0:1, :] = dg_ref[...]
            for (_, row, _), ref in zip(VEC_LEAVES[1:], leaf_refs):
                vec_mine[row:row + 1, 0:ref.shape[1]] = ref[...]
            vec_mine[VEC_LOSS_ROW:VEC_LOSS_ROW + 1, 0:LANES] = loss_ref[...]
            copies = [copy(k, src, dst, base) for k in range(1, 8)
                      for src, dst, base in ((vec_mine, vec_recv, small_base), (dw4_ref, dw4_recv, small_base + 7))]
            for cp in copies:
                cp.start()
            for cp in copies:
                cp.wait_recv()
            vec_recv[me_lin] = vec_mine[...]
            dw4_recv[me_lin] = dw4_ref[...]
            vtot, wtot = vec_recv[0], dw4_recv[0]
            for d in range(1, 8):
                vtot = vtot + vec_recv[d]
                wtot = wtot + dw4_recv[d]
            vec_out[...] = vtot
            dw4_out[...] = wtot
            for cp in copies:
                cp.wait_send()

        @pl.when(i == n - 1)
        def _():
            if red is not None:
                red.sum_and_share()
            if smalls is not None:
                small_all_reduce()
            if red is not None:
                red.finish()

    any_spec = pl.BlockSpec(memory_space=pl.ANY)
    scratch = [pltpu.VMEM((t, PROJ_PAD), BF16)] + _ShardReduce.scratch(gparts, axes)
    out_shape = [jax.ShapeDtypeStruct((s, D_MODEL), F32), jax.ShapeDtypeStruct((1, D_MODEL), F32)]
    out_shape += [jax.ShapeDtypeStruct(_shard_shape(g), F32) for g in gparts]
    out_specs = [_rows(t, D_MODEL), _full((1, D_MODEL))] + [any_spec] * na
    small_args = []
    if smalls is not None:
        small_args = [*vec_leaves, loss_row, dw4]
        out_shape += [jax.ShapeDtypeStruct((VEC_ROWS, D_MODEL), F32), jax.ShapeDtypeStruct(dw4.shape, F32)]
        out_specs += [_full((VEC_ROWS, D_MODEL)), _full(dw4.shape)]
        scratch += [pltpu.VMEM((VEC_ROWS, D_MODEL), F32), pltpu.VMEM((8, VEC_ROWS, D_MODEL), F32),
                    pltpu.VMEM((8,) + dw4.shape, F32)]
    if na or smalls is not None:
        n_sems = small_base + 14
        scratch += [pltpu.SemaphoreType.DMA((n_sems,)), pltpu.SemaphoreType.DMA((n_sems,)),
                    pltpu.SemaphoreType.DMA((max(_ShardReduce.LOCAL * na, 1),))]
    return pl.pallas_call(
        body, name="in_bwd_x", grid=(n,), out_shape=tuple(out_shape),
        in_specs=[_rows(t, D_MODEL), _rows(t, D_MODEL), _full((1, D_MODEL)),
                  pl.BlockSpec((PROJ_PAD, D_MODEL), lambda i: (0, 0), pipeline_mode=pl.Buffered(1))]
        + _dproj_specs(t) + [any_spec] * na + [_full(a.shape) for a in small_args],
        out_specs=tuple(out_specs), scratch_shapes=scratch, compiler_params=_params(),
    )(x, dy, norm_g, wp, *dparts, *gparts, *small_args)


def _in_bwd_w(hb, dpa, dgb, dpm, fox, gparts, axes):
    s = hb.shape[0]
    t = TILE
    n = s // t
    na = len(gparts)
    f_hi = F_ORIG_LO + FOX_HEADS
    n_in = 4 + len(fox)

    def body(*refs):
        h_ref, dpa_ref, dgb_ref, dpm_ref = refs[0:4]
        fox_refs = refs[4:n_in]
        g_refs = refs[n_in:n_in + na]
        o = n_in + na
        dw_ref, dqk_ref, dv_ref, dfb_ref, dqg_ref, dkg_ref, dbf_ref = refs[o:o + 7]
        out_refs = refs[o + 7:o + 7 + na]
        o += 7 + na
        fox_scratch = refs[o:o + 3]
        bufs = tuple(refs[o + 3 + k * na:o + 3 + (k + 1) * na] for k in range(5))
        i = pl.program_id(0)
        red = _ShardReduce(g_refs, out_refs, axes, bufs, *refs[o + 3 + 5 * na:]) if na else None

        @pl.when(i == 0)
        def _():
            dw_ref[...] = jnp.zeros_like(dw_ref)
            if red is not None:
                red.exchange_with_sibling()

        if red is not None:
            @pl.when(i == 1)
            def _():
                for k in (1, 2, 3):
                    red.send_to_chip(k)
                red.keep_mine()

        hv = h_ref[...]

        def rows_of(lo, ref, cols=slice(None)):
            def add():
                dproj = ref[:, cols]
                dw_ref[lo:lo + dproj.shape[1], :] += _dot(dproj, hv, TN)
            return add

        q_cols, k_cols = slice(0, FOX_WIDTH), slice(FOX_WIDTH, 2 * FOX_WIDTH)
        between = (rows_of(0, dpa_ref), rows_of(f_hi, dgb_ref), rows_of(f_hi + FOX_WIDTH, dpm_ref), None,
                   rows_of(QB_LO, dqk_ref, q_cols), rows_of(VB_LO, dv_ref), None, None, rows_of(KB_LO, dqk_ref, k_cols))
        _fox_post_tile(i, n, t, *fox_refs, dqk_ref, dv_ref, dfb_ref, dqg_ref, dkg_ref, dbf_ref, *fox_scratch, between)
        dw_ref[F_ORIG_LO:f_hi, :] += _dot(dfb_ref[...].astype(BF16), hv, TN)[0:FOX_HEADS, :]

        if red is not None:
            @pl.when(i == n - 1)
            def _():
                red.sum_and_share()
                red.finish()

    def rev(w):
        return _rows_rev(t, w, n)

    any_spec = pl.BlockSpec(memory_space=pl.ANY)
    row = jax.ShapeDtypeStruct((1, LANES), F32)
    scratch = [pltpu.VMEM((1, FOX_WIDTH), F32), pltpu.VMEM((1, FOX_WIDTH), F32), pltpu.VMEM((1, LANES), F32)]
    scratch += _ShardReduce.scratch(gparts, axes)
    if na:
        scratch += [pltpu.SemaphoreType.DMA((_ShardReduce.SEMS * na,)), pltpu.SemaphoreType.DMA((_ShardReduce.SEMS * na,)),
                    pltpu.SemaphoreType.DMA((_ShardReduce.LOCAL * na,))]
    return pl.pallas_call(
        body, name="in_bwd_w", grid=(n,),
        out_shape=(jax.ShapeDtypeStruct((IN_WIDTH, D_MODEL), F32), jax.ShapeDtypeStruct((s, 2 * FOX_WIDTH), BF16),
                   jax.ShapeDtypeStruct((s, FOX_WIDTH), BF16), jax.ShapeDtypeStruct((s, LANES), F32), row, row, row)
        + tuple(jax.ShapeDtypeStruct(_shard_shape(g), F32) for g in gparts),
        in_specs=[rev(D_MODEL), rev(512), rev(FOX_WIDTH), rev(512), rev(HEAD_BLOCKS), rev(HEAD_BLOCKS),
                  rev(HEAD_BLOCKS), rev(2 * FOX_WIDTH), rev(LANES), _full((1, LANES)), _full((1, FOX_WIDTH)),
                  _full((1, FOX_WIDTH))] + [any_spec] * na,
        out_specs=(pl.BlockSpec((IN_WIDTH, D_MODEL), lambda i: (0, 0), pipeline_mode=pl.Buffered(1)),
                   rev(2 * FOX_WIDTH), rev(FOX_WIDTH), rev(LANES), _full((1, LANES)), _full((1, LANES)),
                   _full((1, LANES))) + (any_spec,) * na,
        scratch_shapes=scratch, compiler_params=_params(),
    )(hb, dpa, dgb, dpm, *fox, *gparts)


def _adamw_math(w_ref, gv, m_ref, v_ref, d_ref, nm_ref, nv_ref):
    nm = ADAM_B1 * m_ref[...] + (1.0 - ADAM_B1) * gv
    nv = ADAM_B2 * v_ref[...] + (1.0 - ADAM_B2) * (gv * gv)
    m_hat = nm / (1.0 - ADAM_B1 ** ADAM_STEP)
    v_hat = nv / (1.0 - ADAM_B2 ** ADAM_STEP)
    d_ref[...] = -ADAM_LR * (m_hat / (jnp.sqrt(v_hat) + ADAM_EPS) + ADAM_WD * w_ref[...])
    nm_ref[...] = nm
    nv_ref[...] = nv


def _adamw_flat(name, w, g, m, v):
    rows, cols = g.shape
    per_row = cols // LANES

    def body(w_ref, g_ref, m_ref, v_ref, gf_ref, d_ref, nm_ref, nv_ref):
        for k in range(per_row):
            gf_ref[pl.ds(k, rows, stride=per_row), :] = g_ref[:, k * LANES:(k + 1) * LANES]
        _adamw_math(w_ref, gf_ref[...], m_ref, v_ref, d_ref, nm_ref, nv_ref)

    def whole(shape):
        return pl.BlockSpec(shape, lambda i: (0, 0), pipeline_mode=pl.Buffered(1))

    return pl.pallas_call(
        body, name=name, grid=(1,),
        out_shape=(jax.ShapeDtypeStruct(w.shape, F32),) * 4,
        in_specs=[whole(w.shape), whole(g.shape), whole(w.shape), whole(w.shape)], out_specs=(whole(w.shape),) * 4,
        compiler_params=_params(),
    )(w, g, m, v)


def _adamw_rest(vec, dw4, leaves, pool, shards):
    nl = len(VEC_LEAVES) + 1
    ns = len(shards)

    def body(*refs):
        vec_ref, dw4_ref = refs[0:2]
        wmv = refs[2:2 + 3 * nl]
        shard_in = refs[2 + 3 * nl:2 + 3 * nl + 4 * ns]
        o = 2 + 3 * nl + 4 * ns
        loss_ref = refs[o]
        outs = refs[o + 1:o + 1 + 4 * nl]
        shard_out = refs[o + 1 + 4 * nl:]
        loss_ref[...] = vec_ref[VEC_LOSS_ROW:VEC_LOSS_ROW + 1, 0:1]
        for k in range(nl):
            if k < nl - 1:
                _, row, width = VEC_LEAVES[k]
                gv = vec_ref[row:row + 1, 0:width]
            else:
                gv = dw4_ref[...]
            w_ref, m_ref, v_ref = wmv[3 * k:3 * k + 3]
            g_ref, d_ref, nm_ref, nv_ref = outs[4 * k:4 * k + 4]
            g_ref[...] = gv
            _adamw_math(w_ref, gv, m_ref, v_ref, d_ref, nm_ref, nv_ref)
        for k in range(ns):
            w_ref, g_ref, m_ref, v_ref = shard_in[4 * k:4 * k + 4]
            _adamw_math(w_ref, g_ref[...], m_ref, v_ref, *shard_out[3 * k:3 * k + 3])

    shapes = [jax.ShapeDtypeStruct((1, width), F32) for _, _, width in VEC_LEAVES] + [
        jax.ShapeDtypeStruct(dw4.shape, F32)]
    flat_in = [a for triple in list(leaves) + [pool] for a in triple] + [a for quad in shards for a in quad]
    res = pl.pallas_call(
        body, name="adamw_rest",
        out_shape=(jax.ShapeDtypeStruct((1, 1), F32),) + tuple(s for s in shapes for _ in range(4))
        + tuple(jax.ShapeDtypeStruct(quad[0].shape, F32) for quad in shards for _ in range(3)),
        compiler_params=pltpu.CompilerParams(vmem_limit_bytes=VMEM_LIMIT),
    )(vec, dw4, *flat_in)
    per = [res[1 + 4 * k:5 + 4 * k] for k in range(nl)]
    big = res[1 + 4 * nl:]
    return (res[0], [p[0] for p in per], [p[1] for p in per], [p[2] for p in per], [p[3] for p in per],
            [big[3 * k:3 * k + 3] for k in range(ns)])


def _tile_heads(g, n):
    return jnp.tile(g.reshape(1, HEAD_DIM), (1, n))


def kernel(x, mem, norm_g, w_in, b_f, w_pool, pool_scale, fox_q_g, fox_k_g, mem_norm_g, w_mem_kv, mem_q_g, mem_k_g, w_out, loss_target, m_norm_g, m_w_in, m_b_f, m_w_pool, m_pool_scale, m_fox_q_g, m_fox_k_g, m_mem_norm_g, m_w_mem_kv, m_mem_q_g, m_mem_k_g, m_w_out, v_norm_g, v_w_in, v_b_f, v_w_pool, v_pool_scale, v_fox_q_g, v_fox_k_g, v_mem_norm_g, v_w_mem_kv, v_mem_q_g, v_mem_k_g, v_w_out):
    w_in_t = w_in[0].T
    axes = (1, 0, 0)

    g_in, g_kv, g_out = _all_gather_weights([w_in_t, w_mem_kv[0], w_out[0]], axes)
    tiled = _tiled_params(b_f, fox_q_g, fox_k_g, mem_q_g, mem_k_g)
    fwd, wp = _fwd_in(x[0], norm_g, g_in, *tiled[0:3])
    w_kv_b = g_kv.reshape(D_MODEL, 2 * MEM_WIDTH)
    w_out_b = g_out.reshape(D_MODEL, D_MODEL)
    w4 = w_pool.reshape(POOL_ROWS, HEAD_DIM)
    dy, hb, dpa, dgb, dpm, fox, dw_kv, dw_out, (dmemnorm_g, dpscale, dmq_g, dmk_g), loss_row, dw4 = _local_partials(
        x[0], mem[0], loss_target[0], fwd, w_kv_b, w_out_b, tiled, w4, pool_scale, mem_norm_g)

    early = [dw_kv.reshape(4, D_MODEL // 4, 2 * MEM_WIDTH), dw_out.reshape(4, D_MODEL // 4, D_MODEL)]
    dwp, dqk, dvb, dfb, dfq_g, dfk_g, dbf, g_w_kv, g_w_out = _in_bwd_w(hb, dpa, dgb, dpm, fox, early, axes[1:])
    dparts = (dpa, dqk, dvb, dgb, dpm, dfb)
    vec_leaves = (dmemnorm_g, dpscale, dbf, dfq_g, dfk_g, dmq_g, dmk_g)
    grad_x, _, g_w_in_t, vec, dw4_sum = _in_bwd_x(
        x[0], dy, norm_g, wp, dparts, [dwp], axes[0:1], (vec_leaves, loss_row, dw4))

    small_wmv = [(norm_g, m_norm_g, v_norm_g), (mem_norm_g, m_mem_norm_g, v_mem_norm_g),
                 (pool_scale, m_pool_scale, v_pool_scale), (b_f, m_b_f, v_b_f), (fox_q_g, m_fox_q_g, v_fox_q_g),
                 (fox_k_g, m_fox_k_g, v_fox_k_g), (mem_q_g, m_mem_q_g, v_mem_q_g), (mem_k_g, m_mem_k_g, v_mem_k_g)]
    pool_wmv = tuple(a.reshape(POOL_ROWS, HEAD_DIM) for a in (w_pool, m_w_pool, v_w_pool))
    loss, *small_out, (upd_kv, upd_out) = _adamw_rest(
        vec, dw4_sum, small_wmv, pool_wmv, [(w_mem_kv[0], g_w_kv, m_w_mem_kv[0], v_w_mem_kv[0]),
                                             (w_out[0], g_w_out, m_w_out[0], v_w_out[0])])
    tiles = D_MODEL // LANES

    def flat(a):
        return a.reshape(tiles, LANES, -1).transpose(2, 0, 1).reshape(-1, LANES)

    def unflat(a):
        return a.reshape(-1, tiles, LANES).transpose(1, 2, 0).reshape(w_in.shape)

    g_in_flat, *upd_in = _adamw_flat("adamw_w_in", flat(w_in), g_w_in_t, flat(m_w_in), flat(v_w_in))
    big = [[unflat(g_in_flat), g_w_kv[None], g_w_out[None]]]
    big += [[unflat(upd_in[k]), upd_kv[k][None], upd_out[k][None]] for k in range(3)]

    def leaves(k):
        sm = small_out[k]
        b_in, b_kv, b_out = big[k]
        return (sm[0], b_in, sm[3], sm[8].reshape(w_pool.shape), sm[2], sm[4], sm[5], sm[1], b_kv, sm[6], sm[7], b_out)

    return (loss.reshape(()), grad_x[None], *leaves(0), *leaves(1), *leaves(2), *leaves(3))


def _tiled_params(b_f, fox_q_g, fox_k_g, mem_q_g, mem_k_g):
    return (jnp.pad(b_f, ((0, 0), (0, LANES - FOX_HEADS))), _tile_heads(fox_q_g, FOX_HEADS),
            _tile_heads(fox_k_g, FOX_HEADS), _tile_heads(mem_q_g, 4), _tile_heads(mem_k_g, 4))


def _local_partials(xs, mems, tgt, fwd, w_kv_b, w_out_b, tiled, w4, pool_scale, mem_norm_g):
    hb, pa, qk, qa, ka, va, gb, pm, fb = fwd
    bf_pad, fq_g, fk_g, mq_g, mk_g = tiled

    ma, db, mm, mnb, kv, kmn, vmb = _side_fwd(pa, pm, w4, pool_scale, mq_g, mems, mem_norm_g, w_kv_b, mk_g)
    o, mb, r4 = _fox_fwd(qa, ka, va, gb)
    dy, dma, dmm, dw_out, loss_row, doa, dgb, rr = _out_loss(xs, tgt, ma, mb, mm, w_out_b, gb, o, r4)

    dpa, dpm, dw4, dpscale, dmq_g, dw_kv, dmemnorm_g, dmk_g = _side_bwd(
        pa, db, dma, w4, pool_scale, pm, dmm, kmn, vmb, mq_g, kv, mnb, mems, w_kv_b, mk_g, mem_norm_g)
    dka, dva, dqa = _fox_bwd(ka, va, qa, doa, rr)
    fox = (dqa, dka, dva, qk, fb, bf_pad, fq_g, fk_g)
    return dy, hb, dpa, dgb, dpm, fox, dw_kv, dw_out, (dmemnorm_g, dpscale, dmq_g, dmk_g), loss_row, dw4
```

```python
import functools

import jax
import jax.numpy as jnp
from jax import lax
from jax.experimental import pallas as pl
from jax.experimental.pallas import tpu as pltpu

F32 = jnp.float32
BF16 = jnp.bfloat16
MESH = pl.DeviceIdType.MESH

D_MODEL = 1024
HEAD_DIM = 64
POOL_WIDTH = 256
FOX_WIDTH = 512
FOX_HEADS = 8
MEM_WIDTH = 256
N_MEM = 256
IN_WIDTH = 3080
EPS = 1e-6
ATT_SCALE = 0.125

ADAM_LR = 0.001
ADAM_B1 = 0.9
ADAM_B2 = 0.999
ADAM_EPS = 1e-08
ADAM_WD = 0.01
ADAM_STEP = 10

LANES = 128
PA_LO, QB_LO, KB_LO, VB_LO, GB_LO, PM_LO, FB_LO, PROJ_PAD = 0, 512, 1024, 1536, 2048, 2560, 3072, 3200
F_ORIG_LO = 2048

TILE = 512
VMEM_LIMIT = 56 * 1024 * 1024
VMEM_LIMIT_FOX_BWD = 58 * 1024 * 1024

VEC_LEAVES = (("norm_g", 0, 1024), ("mem_norm_g", 1, 1024), ("pool_scale", 2, 256), ("b_f", 3, 8),
              ("fox_q_g", 4, 64), ("fox_k_g", 5, 64), ("mem_q_g", 6, 64), ("mem_k_g", 7, 64))
VEC_LOSS_ROW = 8
VEC_ROWS = 16
POOL_ROWS = 256


def _params(n_grid=1, vmem=VMEM_LIMIT):
    return pltpu.CompilerParams(dimension_semantics=("arbitrary",) * n_grid, vmem_limit_bytes=vmem)


def _rows(t, w):
    return pl.BlockSpec((t, w), lambda i: (i, 0))


def _rows_rev(t, w, n):
    return pl.BlockSpec((t, w), lambda i: (n - 1 - i, 0))


def _full(shape):
    return pl.BlockSpec(shape, lambda i: (0,) * len(shape))


def _sig(x):
    return 1.0 / (1.0 + jnp.exp(-x))


def _lane_lo(shape):
    return lax.broadcasted_iota(jnp.int32, shape, 1) < HEAD_DIM


def _pair_sum(v, lo):
    s0 = jnp.sum(jnp.where(lo, v, 0.0), axis=-1, keepdims=True)
    s1 = jnp.sum(jnp.where(lo, 0.0, v), axis=-1, keepdims=True)
    return jnp.where(lo, s0, s1)


def _head_rms(blk, lo):
    return lax.rsqrt(_pair_sum(blk * blk, lo) * (1.0 / HEAD_DIM) + EPS)


def _head_norm_bwd(dyn, xhat, rr, g, lo):
    a = dyn * g
    return rr * (a - xhat * (_pair_sum(xhat * a, lo) * (1.0 / HEAD_DIM)))


def _fold_heads(acc):
    tot = acc[:, 0:LANES]
    for p in range(1, acc.shape[1] // LANES):
        tot = tot + acc[:, p * LANES:(p + 1) * LANES]
    return tot + pltpu.roll(tot, HEAD_DIM, axis=1)


def _lane_pick(v, lane, idx):
    return jnp.sum(jnp.where(lane == idx, v, 0.0), axis=-1, keepdims=True)


NT = (((1,), (1,)), ((), ()))
TN = (((0,), (0,)), ((), ()))


def _dot(a, b, dims=None):
    if dims is None:
        return jnp.dot(a, b, preferred_element_type=F32)
    return lax.dot_general(a, b, dims, preferred_element_type=F32)


def _my_place():
    return lax.axis_index("x"), lax.axis_index("y"), lax.axis_index("c")


def _half_dims(shape, axis):
    return (shape[0] // 2, shape[1]) if axis == 0 else (shape[0], shape[1] // 2)


def _shard_shape(g):
    return tuple(g.shape[1:]) if len(g.shape) == 3 else (g.shape[0] // 4, g.shape[1])


F32_ROWS = 8


def _shard_window(g):
    rows = _shard_shape(g)[0]
    if len(g.shape) == 3:
        return rows
    skew = max((j * rows) % F32_ROWS for j in range(4))
    return -(-(rows + skew) // F32_ROWS) * F32_ROWS


def _half_of(ref, axis, core, lead=False):
    rows, cols = ref.shape[-2:]
    if axis == 0:
        idx = (pl.ds(pl.multiple_of(core * (rows // 2), 16), rows // 2), slice(None))
    else:
        idx = (slice(None), pl.ds(pl.multiple_of(core * (cols // 2), LANES), cols // 2))
    return ref.at[(slice(None),) + idx] if lead else ref.at[idx]


class _HalfGather:
    def __init__(self, ins, outs, axes, f32_bufs, bf_bufs, send_sems, recv_sems, local_sems):
        self.ins, self.outs, self.axes = ins, outs, axes
        self.f32_bufs, self.bf_bufs = f32_bufs, bf_bufs
        self.send_sems, self.recv_sems, self.local_sems = send_sems, recv_sems, local_sems
        self.n = len(ins)
        x, y, self.c = _my_place()
        self.me, self.sibling = (x, y, self.c), (x, y, 1 - self.c)
        self.chips = [(1 - x, y), (x, 1 - y), (1 - x, 1 - y)]

    @staticmethod
    def scratch(shards, axes):
        dims = [_half_dims(a.shape, axis) for a, axis in zip(shards, axes)]
        n = len(shards)
        return [pltpu.VMEM(d, F32) for d in dims] + [pltpu.VMEM(d, BF16) for d in dims] + [
            pltpu.SemaphoreType.DMA((7 * n,)), pltpu.SemaphoreType.DMA((7 * n,)), pltpu.SemaphoreType.DMA((2 * n,))]

    @staticmethod
    def out_shapes(shards, axes):
        return tuple(jax.ShapeDtypeStruct((8,) + _half_dims(a.shape, axis), BF16) for a, axis in zip(shards, axes))

    def _blk(self, a, px, py, pc):
        return self.outs[a].at[4 * px + 2 * py + pc]

    def _copy(self, a, k, block, to, src=None):
        return pltpu.make_async_remote_copy(
            src_ref=self._blk(a, *block) if src is None else src, dst_ref=self._blk(a, *block),
            send_sem=self.send_sems.at[7 * a + k], recv_sem=self.recv_sems.at[7 * a + k], device_id=to,
            device_id_type=MESH)

    def _keep(self, a):
        return pltpu.make_async_copy(self.bf_bufs[a], self._blk(a, *self.me), self.local_sems.at[self.n + a])

    def _first(self, a):
        mine = [self._copy(a, 0, self.me, self.sibling, src=self.bf_bufs[a])]
        return mine + [self._copy(a, 1 + j, self.me, (*chip, self.c), src=self.bf_bufs[a])
                       for j, chip in enumerate(self.chips)]

    def send_mine(self):
        loads = [pltpu.make_async_copy(_half_of(self.ins[a], self.axes[a], self.c), self.f32_bufs[a],
                                       self.local_sems.at[a]) for a in range(self.n)]
        for cp in loads:
            cp.start()
        for a in range(self.n):
            loads[a].wait()
            self.bf_bufs[a][...] = self.f32_bufs[a][...].astype(BF16)
            self._keep(a).start()
            for cp in self._first(a):
                cp.start()

    def pass_on(self):
        for a in range(self.n):
            for j, chip in enumerate(self.chips):
                self._copy(a, 1 + j, (*chip, self.c), self.me).wait_recv()
                self._copy(a, 4 + j, (*chip, self.c), self.sibling).start()

    def finish(self):
        for a in range(self.n):
            self._copy(a, 0, self.sibling, self.me).wait_recv()
            for j, chip in enumerate(self.chips):
                self._copy(a, 4 + j, (*chip, 1 - self.c), self.me).wait_recv()
        for a in range(self.n):
            for cp in self._first(a):
                cp.wait_send()
            for j, chip in enumerate(self.chips):
                self._copy(a, 4 + j, (*chip, self.c), self.sibling).wait_send()
            self._keep(a).wait()


def _all_gather_weights(shards, axes):
    n = len(shards)

    def body(*refs):
        gather = _HalfGather(refs[0:n], refs[n:2 * n], axes, refs[2 * n:3 * n], refs[3 * n:4 * n], *refs[4 * n:])
        gather.send_mine()
        gather.pass_on()
        gather.finish()

    any_spec = pl.BlockSpec(memory_space=pl.ANY)
    return pl.pallas_call(
        body, name="weights_all_gather", out_shape=_HalfGather.out_shapes(shards, axes),
        in_specs=[any_spec] * n, out_specs=(any_spec,) * n, scratch_shapes=_HalfGather.scratch(shards, axes),
        compiler_params=pltpu.CompilerParams(vmem_limit_bytes=VMEM_LIMIT),
    )(*shards)


class _ShardReduce:
    SEMS = 8
    LOCAL = 5

    def __init__(self, g_refs, out_refs, axes, bufs, send_sems, recv_sems, local_sems):
        self.g_refs, self.out_refs, self.axes = g_refs, out_refs, axes
        self.recv_a, self.own_a, self.send_b, self.recv_b, self.fin = bufs
        self.send_sems, self.recv_sems, self.local_sems = send_sems, recv_sems, local_sems
        self.n = len(g_refs)
        x, y, self.c = _my_place()
        self.chip = 2 * x + y
        self.sibling = (x, y, 1 - self.c)

    @staticmethod
    def scratch(gparts, axes):
        assert all(len(g.shape) == 3 or axis == 1 for g, axis in zip(gparts, axes))
        dims = [_half_dims(_shard_shape(g), axis) for g, axis in zip(gparts, axes)]
        windows = [d if len(g.shape) == 3 else (_shard_window(g),) + d[1:] for g, d in zip(gparts, dims)]
        shapes = []
        for dtype, lead, per_array in ((F32, (4,), windows), (F32, (4,), windows), (BF16, (4,), dims),
                                       (BF16, (4,), dims), (F32, (), dims)):
            shapes += [pltpu.VMEM(lead + d, dtype) for d in per_array]
        return shapes

    def _shard_half(self, a, j, core):
        g = self.g_refs[a]
        if len(g.shape) == 3:
            return _half_of(g.at[j], self.axes[a], core)
        start = (j * _shard_shape(g)[0]) // F32_ROWS * F32_ROWS
        return _half_of(g.at[pl.ds(pl.multiple_of(start, F32_ROWS), _shard_window(g))], self.axes[a], core)

    def _to_sibling(self, a, j):
        return pltpu.make_async_remote_copy(
            src_ref=self._shard_half(a, j, 1 - self.c), dst_ref=self.recv_a[a].at[j],
            send_sem=self.send_sems.at[self.SEMS * a + j], recv_sem=self.recv_sems.at[self.SEMS * a + j], device_id=self.sibling,
            device_id_type=MESH)

    def _own(self, a, j):
        return pltpu.make_async_copy(self._shard_half(a, j, self.c), self.own_a[a].at[j],
                                     self.local_sems.at[self.LOCAL * a + j])

    def _to_chip(self, a, k):
        dest = (self.chip + k) % 4
        return pltpu.make_async_remote_copy(
            src_ref=self.send_b[a].at[dest], dst_ref=self.recv_b[a].at[self.chip],
            send_sem=self.send_sems.at[self.SEMS * a + 3 + k], recv_sem=self.recv_sems.at[self.SEMS * a + 3 + k],
            device_id=(dest // 2, dest % 2, self.c), device_id_type=MESH)

    def _give(self, a):
        return pltpu.make_async_remote_copy(
            src_ref=self.fin[a], dst_ref=_half_of(self.out_refs[a], self.axes[a], self.c),
            send_sem=self.send_sems.at[self.SEMS * a + 7], recv_sem=self.recv_sems.at[self.SEMS * a + 7], device_id=self.sibling,
            device_id_type=MESH)

    def _mine(self, a):
        return pltpu.make_async_copy(self.fin[a], _half_of(self.out_refs[a], self.axes[a], self.c),
                                     self.local_sems.at[self.LOCAL * a])

    def exchange_with_sibling(self):
        for k in (1, 2, 3, 0):
            j = (self.chip + k) % 4
            for a in range(self.n):
                self._to_sibling(a, j).start()
                self._own(a, j).start()

    def _chip_partial(self, a, j):
        self._own(a, j).wait()
        self._to_sibling(a, j).wait_recv()
        g = self.g_refs[a]
        if len(g.shape) == 3:
            self.send_b[a][j] = (self.own_a[a][j] + self.recv_a[a][j]).astype(BF16)
            return
        rows = _shard_shape(g)[0]
        for shard in range(4):
            @pl.when(j == shard)
            def _():
                at = pl.ds((shard * rows) % F32_ROWS, rows)
                self.send_b[a][shard] = (self.own_a[a][shard, at, :] + self.recv_a[a][shard, at, :]).astype(BF16)

    def send_to_chip(self, k):
        for a in range(self.n):
            self._chip_partial(a, (self.chip + k) % 4)
            self._to_chip(a, k).start()

    def keep_mine(self):
        for a in range(self.n):
            self._chip_partial(a, self.chip)
            keep = pltpu.make_async_copy(self.send_b[a].at[self.chip], self.recv_b[a].at[self.chip],
                                         self.local_sems.at[self.LOCAL * a + 4])
            keep.start()
            keep.wait()

    def sum_and_share(self):
        for a in range(self.n):
            for k in range(1, 4):
                self._to_chip(a, k).wait_recv()
            tot = self.recv_b[a][0].astype(F32) + self.recv_b[a][1].astype(F32)
            tot = tot + self.recv_b[a][2].astype(F32)
            self.fin[a][...] = tot + self.recv_b[a][3].astype(F32)
            self._give(a).start()
            self._mine(a).start()

    def finish(self):
        for a in range(self.n):
            self._give(a).wait_recv()
            self._mine(a).wait()
            self._give(a).wait_send()
            for j in range(4):
                self._to_sibling(a, j).wait_send()
            for k in range(1, 4):
                self._to_chip(a, k).wait_send()


def _mem_tokens_fwd(mem_ref, g_ref, w_ref, kg_ref, mn_ref, kv_ref, kn_ref, vm_ref):
    xm = mem_ref[...]
    rr = lax.rsqrt(jnp.mean(xm * xm, axis=-1, keepdims=True) + EPS)
    mnb = ((xm * rr) * g_ref[...]).astype(BF16)
    mn_ref[...] = mnb
    kv = _dot(mnb, w_ref[...])
    kv_ref[...] = kv
    lo = _lane_lo((xm.shape[0], LANES))
    for p in range(MEM_WIDTH // LANES):
        sl = slice(p * LANES, (p + 1) * LANES)
        kb = kv[:, sl]
        kn_ref[:, sl] = ((kb * _head_rms(kb, lo)) * kg_ref[:, sl]).astype(BF16)
    vm_ref[...] = kv[:, MEM_WIDTH:].astype(BF16)


AUG_LO = 64
KEY_SUM_LANE = 72
QUERY_SUM_LANE = 80
HEAD_BLOCKS = FOX_HEADS * LANES


def _ones3(lane):
    return jnp.where((lane >= AUG_LO) & (lane < AUG_LO + 3), 1.0, 0.0)


def _spread3(cols):
    hi = cols.astype(BF16)
    rest = cols - hi.astype(F32)
    mid = rest.astype(BF16)
    low = (rest - mid.astype(F32)).astype(BF16)
    r = lax.broadcasted_iota(jnp.int32, (LANES, HEAD_BLOCKS), 0)
    c = lax.broadcasted_iota(jnp.int32, (LANES, HEAD_BLOCKS), 1)
    out = None
    for k, part in enumerate((hi, mid, low)):
        term = _dot(part, jnp.where(c == r * LANES + (AUG_LO + k), 1.0, 0.0).astype(BF16))
        out = term if out is None else out + term
    return out


def _head_block(pair_blk, hh, lo, extras):
    src = pair_blk if hh == 0 else pltpu.roll(pair_blk, HEAD_DIM, axis=1)
    return jnp.where(lo, src, extras).astype(BF16)


def _pair_block(blk0, blk1, lo):
    return jnp.where(lo, blk0, pltpu.roll(blk1, HEAD_DIM, axis=1))


def _assemble_w_in(halves_ref, words_ref, wp_ref):
    shard = IN_WIDTH // 4
    half = D_MODEL // 2
    f_hi = F_ORIG_LO + FOX_HEADS
    for j in range(4):
        blocks = [pltpu.bitcast(halves_ref[2 * j + c], jnp.uint32) for c in range(2)]
        for lo, hi, to in ((0, F_ORIG_LO, PA_LO), (F_ORIG_LO, f_hi, FB_LO), (f_hi, IN_WIDTH, GB_LO)):
            a, b = max(lo, shard * j), min(hi, shard * (j + 1))
            if a < b:
                for c in range(2):
                    words_ref[(to + a - lo) // 2:(to + b - lo) // 2, c * half:(c + 1) * half] = (
                        blocks[c][(a - shard * j) // 2:(b - shard * j) // 2, :])
    pad_lo = (FB_LO + FOX_HEADS) // 2
    words_ref[pad_lo:, :] = jnp.zeros((PROJ_PAD // 2 - pad_lo, D_MODEL), jnp.uint32)
    wp_ref[...] = pltpu.bitcast(words_ref[...], BF16)


def _fwd_in(x, norm_g, halves, bf_pad, fq_g, fk_g):
    s = x.shape[0]
    t = TILE
    n = s // t

    def body(x_ref, ng_ref, halves_ref, bf_ref, qg_ref, kg_ref,
             h_ref, pa_ref, qk_ref, qa_ref, ka_ref, va_ref, gb_ref, pm_ref, fb_ref, wp_ref,
             carry_ref, fcol_ref, words_ref):
        @pl.when(pl.program_id(0) == 0)
        def _():
            carry_ref[...] = jnp.zeros_like(carry_ref)
            _assemble_w_in(halves_ref, words_ref, wp_ref)

        xv = x_ref[...]
        rr = lax.rsqrt(jnp.mean(xv * xv, axis=-1, keepdims=True) + EPS)
        hb = ((xv * rr) * ng_ref[...]).astype(BF16)
        h_ref[...] = hb

        def proj(lo, hi):
            return _dot(hb, wp_ref[lo:hi, :], NT)

        fb = proj(FB_LO, PROJ_PAD)
        fb_ref[...] = fb
        qk_ref[:, 0:FOX_WIDTH] = proj(QB_LO, KB_LO)

        lane = lax.broadcasted_iota(jnp.int32, (t, LANES), 1)
        row = lax.broadcasted_iota(jnp.int32, (t, LANES), 0)
        lo = lane < HEAD_DIM
        z = fb + bf_ref[...]
        lf = -(jnp.maximum(-z, 0.0) + jnp.log1p(jnp.exp(-jnp.abs(z))))
        lf = jnp.where(lane < FOX_HEADS, lf, 0.0)
        sh = 1
        while sh < t:
            lf = lf + jnp.where(row >= sh, pltpu.roll(lf, sh, axis=0), 0.0)
            sh *= 2
        fcum = lf + carry_ref[...]
        fcol_ref[...] = fcum
        carry_ref[...] = fcol_ref[t - 1:t, :]

        ones3 = _ones3(lane)
        minus_f = _spread3(-fcum)

        def head_blocks(seg, g_ref, out_ref, scale):
            for p in range(FOX_WIDTH // LANES):
                sl = slice(p * LANES, (p + 1) * LANES)
                blk = qk_ref[:, seg - QB_LO + p * LANES:seg - QB_LO + (p + 1) * LANES]
                normed = ((blk * _head_rms(blk, lo)) * g_ref[:, sl]) * scale
                for hh in range(2):
                    h = 2 * p + hh
                    if seg == QB_LO:
                        extras = jnp.where(lane == QUERY_SUM_LANE + h, 1.0, ones3)
                    else:
                        extras = jnp.where(lane == KEY_SUM_LANE + h, 1.0, minus_f[:, h * LANES:(h + 1) * LANES])
                    out_ref[:, h * LANES:(h + 1) * LANES] = _head_block(normed, hh, lo, extras)

        qk_ref[:, FOX_WIDTH:2 * FOX_WIDTH] = proj(KB_LO, VB_LO)
        pa_ref[...] = proj(PA_LO, QB_LO)
        head_blocks(QB_LO, qg_ref, qa_ref, ATT_SCALE)
        vraw = proj(VB_LO, GB_LO)
        gb_ref[...] = proj(GB_LO, PM_LO)
        head_blocks(KB_LO, kg_ref, ka_ref, 1.0)
        pm_ref[...] = proj(PM_LO, FB_LO)
        for h in range(FOX_HEADS):
            va_ref[:, h * LANES:(h + 1) * LANES] = _head_block(vraw[:, (h // 2) * LANES:(h // 2 + 1) * LANES], h % 2, lo, ones3)

    outs = (
        jax.ShapeDtypeStruct((s, D_MODEL), BF16),
        jax.ShapeDtypeStruct((s, 512), F32),
        jax.ShapeDtypeStruct((s, 2 * FOX_WIDTH), F32),
        jax.ShapeDtypeStruct((s, HEAD_BLOCKS), BF16),
        jax.ShapeDtypeStruct((s, HEAD_BLOCKS), BF16),
        jax.ShapeDtypeStruct((s, HEAD_BLOCKS), BF16),
        jax.ShapeDtypeStruct((s, FOX_WIDTH), F32),
        jax.ShapeDtypeStruct((s, 512), F32),
        jax.ShapeDtypeStruct((s, LANES), F32),
        jax.ShapeDtypeStruct((PROJ_PAD, D_MODEL), BF16),
    )

    def resident(shape):
        return pl.BlockSpec(shape, lambda i: (0,) * len(shape), pipeline_mode=pl.Buffered(1))

    *fwd, wp = pl.pallas_call(
        body, name="fwd_in", grid=(n,), out_shape=outs,
        in_specs=[_rows(t, D_MODEL), _full((1, D_MODEL)), resident(halves.shape), _full((1, LANES)),
                  _full((1, FOX_WIDTH)), _full((1, FOX_WIDTH))],
        out_specs=(_rows(t, D_MODEL), _rows(t, 512), _rows(t, 2 * FOX_WIDTH), _rows(t, HEAD_BLOCKS),
                   _rows(t, HEAD_BLOCKS), _rows(t, HEAD_BLOCKS), _rows(t, FOX_WIDTH), _rows(t, 512),
                   _rows(t, LANES), resident((PROJ_PAD, D_MODEL))),
        scratch_shapes=[pltpu.VMEM((1, LANES), F32), pltpu.VMEM((t, LANES), F32),
                        pltpu.VMEM((PROJ_PAD // 2, D_MODEL), jnp.uint32)],
        compiler_params=_params(),
    )(x, norm_g, halves, bf_pad, fq_g, fk_g)
    return tuple(fwd), wp


POOL_HALO = 16


def _pool_window(lane):
    return jnp.where(lane < 64, 2.0, jnp.where(lane < 128, 4.0, jnp.where(lane < 192, 8.0, 16.0)))


def _pool_pick(lane, s2, s4, s8, s16):
    return jnp.where(lane < 64, s2, jnp.where(lane < 128, s4, jnp.where(lane < 192, s8, s16)))


def _group_onehot(shape, row_is_group_lane):
    r = lax.broadcasted_iota(jnp.int32, shape, 0)
    c = lax.broadcasted_iota(jnp.int32, shape, 1)
    hit = (r % HEAD_DIM == c) if row_is_group_lane else (c % HEAD_DIM == r)
    return jnp.where(hit, 1.0, 0.0).astype(F32)


def _same_group(shape):
    r = lax.broadcasted_iota(jnp.int32, shape, 0)
    c = lax.broadcasted_iota(jnp.int32, shape, 1)
    return (r // HEAD_DIM) == (c // HEAD_DIM)


def _pool_block_diag(w4):
    spread = jnp.dot(w4, _group_onehot((HEAD_DIM, POOL_WIDTH), False), preferred_element_type=F32,
                     precision=lax.Precision.HIGHEST)
    return jnp.where(_same_group((POOL_WIDTH, POOL_WIDTH)), spread, 0.0).astype(BF16)


def _mem_softmax(qm, kp):
    sc = _dot(qm, kp, NT)
    e = jnp.exp(sc - jnp.max(sc, axis=-1, keepdims=True))
    return e * (1.0 / jnp.sum(e, axis=-1, keepdims=True))


def _side_fwd(pa, pm, w4, pscale, mq_g, mem, mem_norm_g, w_kv, mk_g):
    s = pa.shape[0]
    t = TILE
    n = s // t
    ext = t + POOL_HALO
    nm = mem.shape[0]

    def body(pa_ref, pm_ref, w4_ref, sc_ref, g_ref, mem_ref, mg_ref, wkv_ref, kg_ref,
             ma_ref, d_ref, mm_ref, mn_ref, kv_ref, k_ref, v_ref, ext_ref, w_ref):
        i = pl.program_id(0)

        @pl.when(i == 0)
        def _():
            ext_ref[0:POOL_HALO, :] = jnp.zeros((POOL_HALO, POOL_WIDTH), F32)
            w_ref[...] = _pool_block_diag(w4_ref[...])
            _mem_tokens_fwd(mem_ref, mg_ref, wkv_ref, kg_ref, mn_ref, kv_ref, k_ref, v_ref)

        u = pa_ref[:, 0:POOL_WIDTH]
        ext_ref[POOL_HALO:ext, :] = u
        e = ext_ref[...]
        s2 = e + pltpu.roll(e, 1, axis=0)
        s4 = s2 + pltpu.roll(s2, 2, axis=0)
        s8 = s4 + pltpu.roll(s4, 4, axis=0)
        s16 = s8 + pltpu.roll(s8, 8, axis=0)
        lane_e = lax.broadcasted_iota(jnp.int32, (ext, POOL_WIDTH), 1)
        win = _pool_pick(lane_e, s2, s4, s8, s16)[POOL_HALO:ext, :]
        lane = lax.broadcasted_iota(jnp.int32, (t, POOL_WIDTH), 1)
        pos = (lax.broadcasted_iota(jnp.int32, (t, POOL_WIDTH), 0) + (i * t + 1)).astype(F32)
        d = win / jnp.minimum(pos, _pool_window(lane)) - u
        db = d.astype(BF16)
        d_ref[...] = db
        ya = _dot(db, w_ref[...]) * sc_ref[...]
        ga = pa_ref[:, POOL_WIDTH:2 * POOL_WIDTH]
        ma_ref[...] = (ya * (ga * _sig(ga))).astype(BF16)
        ext_ref[0:POOL_HALO, :] = ext_ref[t:ext, :]

        lo = _lane_lo((t, LANES))
        for p in range(MEM_WIDTH // LANES):
            sl = slice(p * LANES, (p + 1) * LANES)
            qb = pm_ref[:, sl]
            qs = (((qb * _head_rms(qb, lo)) * g_ref[:, sl]) * ATT_SCALE).astype(BF16)
            kp = k_ref[:, sl]
            vp = v_ref[:, sl]
            outs = []
            for hh in range(2):
                msk = lo if hh == 0 else jnp.logical_not(lo)
                prob = _mem_softmax(jnp.where(msk, qs, jnp.zeros_like(qs)), kp)
                outs.append(_dot(prob.astype(BF16), vp))
            o = jnp.where(lo, outs[0], outs[1])
            gm = pm_ref[:, MEM_WIDTH + p * LANES:MEM_WIDTH + (p + 1) * LANES]
            mm_ref[:, sl] = (o * (gm * _sig(gm))).astype(BF16)

    return pl.pallas_call(
        body, name="side_fwd", grid=(n,),
        out_shape=(jax.ShapeDtypeStruct((s, POOL_WIDTH), BF16), jax.ShapeDtypeStruct((s, POOL_WIDTH), BF16),
                   jax.ShapeDtypeStruct((s, MEM_WIDTH), BF16), jax.ShapeDtypeStruct((nm, D_MODEL), BF16),
                   jax.ShapeDtypeStruct((nm, 2 * MEM_WIDTH), F32), jax.ShapeDtypeStruct((nm, MEM_WIDTH), BF16),
                   jax.ShapeDtypeStruct((nm, MEM_WIDTH), BF16)),
        in_specs=[_rows(t, 512), _rows(t, 512), _full((POOL_ROWS, HEAD_DIM)), _full((1, POOL_WIDTH)),
                  _full((1, MEM_WIDTH)), _full((nm, D_MODEL)), _full((1, D_MODEL)), _full((D_MODEL, 2 * MEM_WIDTH)),
                  _full((1, MEM_WIDTH))],
        out_specs=(_rows(t, POOL_WIDTH), _rows(t, POOL_WIDTH), _rows(t, MEM_WIDTH), _full((nm, D_MODEL)),
                   _full((nm, 2 * MEM_WIDTH)), _full((nm, MEM_WIDTH)), _full((nm, MEM_WIDTH))),
        scratch_shapes=[pltpu.VMEM((ext, POOL_WIDTH), F32), pltpu.VMEM((POOL_WIDTH, POOL_WIDTH), BF16)],
        compiler_params=_params(),
    )(pa, pm, w4, pscale, mq_g, mem, mem_norm_g, w_kv, mk_g)


FOX_FWD_HEADS = 4


def _fox_fwd(qa, ka, va, gb):
    s = qa.shape[0]
    t = TILE
    n = s // t
    heads = FOX_FWD_HEADS
    pairs = heads // 2
    group_w = heads * LANES

    def body(qa_ref, ka_ref, va_ref, gb_ref, o_ref, mb_ref, r_ref):
        i = pl.program_id(1)
        lane = lax.broadcasted_iota(jnp.int32, (t, LANES), 1)
        lo = lane < HEAD_DIM
        causal = lax.broadcasted_iota(jnp.int32, (t, t), 1) <= lax.broadcasted_iota(jnp.int32, (t, t), 0)
        qas = [qa_ref[:, hh * LANES:(hh + 1) * LANES] for hh in range(heads)]

        def step(j, carry, masked):
            rows = pl.ds(pl.multiple_of(j * t, t), t)
            def logits(hh):
                sc = _dot(qas[hh], ka_ref[rows, hh * LANES:(hh + 1) * LANES], NT)
                return jnp.where(causal, sc, -1e30) if masked else sc

            def advance(hh, sc):
                m, acc = carry[hh]
                m_new = jnp.maximum(m, jnp.max(sc, axis=-1, keepdims=True))
                p = jnp.exp(sc - m_new).astype(BF16)
                return m_new, jnp.exp(m - m_new) * acc + _dot(p, va_ref[rows, hh * LANES:(hh + 1) * LANES])

            new = []
            sc = logits(0)
            for hh in range(heads):
                sc_next = logits(hh + 1) if hh + 1 < heads else None
                new.append(advance(hh, sc))
                sc = sc_next
            return tuple(new)

        init = (jnp.full((t, 1), -1e30, F32), jnp.zeros((t, LANES), F32))
        carry = lax.fori_loop(0, i, functools.partial(step, masked=False), (init,) * heads)
        res = step(i, carry, masked=True)
        for p in range(pairs):
            outs = []
            rcol = jnp.zeros((t, LANES), F32)
            for hh in range(2):
                m, acc = res[2 * p + hh]
                l = _lane_pick(acc, lane, AUG_LO)
                outs.append(acc * (1.0 / l))
                rcol = jnp.where(lane == hh, m + jnp.log(l), rcol)
            o = _pair_block(outs[0], outs[1], lo)
            sl = slice(p * LANES, (p + 1) * LANES)
            o_ref[:, sl] = o
            g = gb_ref[:, sl]
            mb_ref[:, sl] = (o * (g * _sig(g))).astype(BF16)
            r_ref[p] = rcol

    tile_spec = pl.BlockSpec((t, pairs * LANES), lambda p, i: (i, p))
    full_spec = pl.BlockSpec((s, group_w), lambda p, i: (0, p))
    return pl.pallas_call(
        body, name="fox_fwd", grid=(FOX_HEADS // heads, n),
        out_shape=(jax.ShapeDtypeStruct((s, FOX_WIDTH), F32), jax.ShapeDtypeStruct((s, FOX_WIDTH), BF16),
                   jax.ShapeDtypeStruct((FOX_HEADS // 2, s, LANES), F32)),
        in_specs=[pl.BlockSpec((t, group_w), lambda p, i: (i, p)), full_spec, full_spec, tile_spec],
        out_specs=(tile_spec, tile_spec, pl.BlockSpec((pairs, t, LANES), lambda p, i: (p, i, 0))),
        compiler_params=_params(2),
    )(qa, ka, va, gb)


def _out_loss(x, tgt, ma, mb, mm, wout, gb, o, r4):
    s = x.shape[0]
    t = TILE
    n = s // t
    pairs = FOX_HEADS // 2

    def body(x_ref, t_ref, ma_ref, mb_ref, mm_ref, w_ref, gb_ref, o_ref, r_ref,
             dy_ref, dma_ref, dmm_ref, dw_ref, loss_ref, doa_ref, dgb_ref, rr_ref, mix_ref):
        @pl.when(pl.program_id(0) == 0)
        def _():
            dw_ref[...] = jnp.zeros_like(dw_ref)
            loss_ref[...] = jnp.zeros_like(loss_ref)

        mix_ref[:, 0:256] = ma_ref[...]
        mix_ref[:, 256:768] = mb_ref[...]
        mix_ref[:, 768:1024] = mm_ref[...]
        mix = mix_ref[...]
        err = (x_ref[...] + _dot(mix, w_ref[...])) - t_ref[...]
        row_mean = jnp.sum(err * err, axis=-1, keepdims=True) * (1.0 / D_MODEL)
        loss_ref[...] += 0.5 * jnp.sum(row_mean, axis=0, keepdims=True)
        dy = err * (1.0 / D_MODEL)
        dy_ref[...] = dy
        dyb = dy.astype(BF16)
        dmix = _dot(dyb, w_ref[...], NT)
        dma_ref[...] = dmix[:, 0:256]
        dmm_ref[...] = dmix[:, 768:1024]
        dw_ref[...] += _dot(mix, dyb, TN)

        lane = lax.broadcasted_iota(jnp.int32, (t, LANES), 1)
        lo = lane < HEAD_DIM
        d_os = []
        delta = jnp.zeros((t, LANES), F32)
        for p in range(pairs):
            sl = slice(p * LANES, (p + 1) * LANES)
            g = gb_ref[:, sl]
            sg = _sig(g)
            dm = dmix[:, 256 + p * LANES:256 + (p + 1) * LANES]
            ov = o_ref[:, sl]
            d_o = dm * (g * sg)
            d_os.append(d_o)
            dgb_ref[:, sl] = (dm * ov * (sg * (1.0 + g * (1.0 - sg)))).astype(BF16)
            prod = d_o * ov
            delta = jnp.where(lane == 2 * p, jnp.sum(jnp.where(lo, prod, 0.0), axis=-1, keepdims=True), delta)
            delta = jnp.where(lane == 2 * p + 1, jnp.sum(jnp.where(lo, 0.0, prod), axis=-1, keepdims=True), delta)
            rr_ref[p, 0] = r_ref[p].T[0:8, :]
        minus_delta = _spread3(-delta)
        for h in range(FOX_HEADS):
            blk = slice(h * LANES, (h + 1) * LANES)
            doa_ref[:, blk] = _head_block(d_os[h // 2], h % 2, lo, minus_delta[:, blk])

    return pl.pallas_call(
        body, name="out_loss", grid=(n,),
        out_shape=(jax.ShapeDtypeStruct((s, D_MODEL), F32), jax.ShapeDtypeStruct((s, 256), F32),
                   jax.ShapeDtypeStruct((s, 256), F32), jax.ShapeDtypeStruct((D_MODEL, D_MODEL), F32),
                   jax.ShapeDtypeStruct((1, LANES), F32), jax.ShapeDtypeStruct((s, HEAD_BLOCKS), BF16),
                   jax.ShapeDtypeStruct((s, FOX_WIDTH), BF16), jax.ShapeDtypeStruct((pairs, n, 8, t), F32)),
        in_specs=[_rows(t, D_MODEL), _rows(t, D_MODEL), _rows(t, 256), _rows(t, 512), _rows(t, 256),
                  _full((D_MODEL, D_MODEL)), _rows(t, FOX_WIDTH), _rows(t, FOX_WIDTH),
                  pl.BlockSpec((pairs, t, LANES), lambda i: (0, i, 0))],
        out_specs=(_rows(t, D_MODEL), _rows(t, 256), _rows(t, 256), _full((D_MODEL, D_MODEL)), _full((1, LANES)),
                   _rows(t, HEAD_BLOCKS), _rows(t, FOX_WIDTH), pl.BlockSpec((pairs, 1, 8, t), lambda i: (0, i, 0, 0))),
        scratch_shapes=[pltpu.VMEM((t, D_MODEL), BF16)],
        compiler_params=_params(),
    )(x, tgt, ma, mb, mm, wout, gb, o, r4)


def _side_bwd(pa, db, dma, w4, pscale, pm, dmm, kmn, vmb, mq_g, kv, mnb, mem, w_kv, mk_g, mem_norm_g):
    s = pa.shape[0]
    t = TILE
    n = s // t
    ext = t + POOL_HALO
    nm = mem.shape[0]

    def body(pa_ref, d_ref, dma_ref, w4_ref, sc_ref, pm_ref, dmm_ref, k_ref, v_ref, g_ref,
             kv_ref, mn_ref, mem_ref, wkv_ref, kg_ref, mg_ref,
             dpa_ref, dpm_ref, dw4_ref, dsc_ref, dg_ref, dwkv_ref, dmg_ref, dkg_ref,
             ext_ref, w_ref, dw_ref, dk_ref, dv_ref, gacc_ref, dkv_ref):
        i = pl.program_id(0)

        @pl.when(i == 0)
        def _():
            dw_ref[...] = jnp.zeros_like(dw_ref)
            dsc_ref[...] = jnp.zeros_like(dsc_ref)
            ext_ref[t:ext, :] = jnp.zeros((POOL_HALO, POOL_WIDTH), F32)
            w_ref[...] = _pool_block_diag(w4_ref[...])
            dk_ref[...] = jnp.zeros_like(dk_ref)
            dv_ref[...] = jnp.zeros_like(dv_ref)
            gacc_ref[...] = jnp.zeros_like(gacc_ref)

        dbv = d_ref[...]
        z = _dot(dbv, w_ref[...])
        ga = pa_ref[:, POOL_WIDTH:2 * POOL_WIDTH]
        sg = _sig(ga)
        dma_v = dma_ref[...]
        dya = dma_v * (ga * sg)
        dpa_ref[:, POOL_WIDTH:2 * POOL_WIDTH] = (dma_v * (z * sc_ref[...]) * (sg * (1.0 + ga * (1.0 - sg)))).astype(BF16)
        dsc_ref[...] += jnp.sum(dya * z, axis=0, keepdims=True)
        dzb = (dya * sc_ref[...]).astype(BF16)
        dw_ref[...] += _dot(dbv, dzb, TN)
        dd = _dot(dzb, w_ref[...], NT)
        lane = lax.broadcasted_iota(jnp.int32, (t, POOL_WIDTH), 1)
        pos = (lax.broadcasted_iota(jnp.int32, (t, POOL_WIDTH), 0) + ((n - 1 - i) * t + 1)).astype(F32)
        ext_ref[0:t, :] = dd / jnp.minimum(pos, _pool_window(lane))
        e = ext_ref[...]
        s2 = e + pltpu.roll(e, ext - 1, axis=0)
        s4 = s2 + pltpu.roll(s2, ext - 2, axis=0)
        s8 = s4 + pltpu.roll(s4, ext - 4, axis=0)
        s16 = s8 + pltpu.roll(s8, ext - 8, axis=0)
        lane_e = lax.broadcasted_iota(jnp.int32, (ext, POOL_WIDTH), 1)
        win = _pool_pick(lane_e, s2, s4, s8, s16)[0:t, :]
        dpa_ref[:, 0:POOL_WIDTH] = (win - dd).astype(BF16)
        ext_ref[t:ext, :] = ext_ref[0:POOL_HALO, :]

        lo = _lane_lo((t, LANES))
        for p in range(MEM_WIDTH // LANES):
            sl = slice(p * LANES, (p + 1) * LANES)
            qb = pm_ref[:, sl]
            rr = _head_rms(qb, lo)
            qhat = qb * rr
            g = g_ref[:, sl]
            qs = ((qhat * g) * ATT_SCALE).astype(BF16)
            gm = pm_ref[:, MEM_WIDTH + p * LANES:MEM_WIDTH + (p + 1) * LANES]
            sg = _sig(gm)
            dmo = dmm_ref[:, sl]
            d_o = dmo * (gm * sg)
            kp = k_ref[:, sl]
            vp = v_ref[:, sl]
            outs, dqs = [], []
            for hh in range(2):
                msk = lo if hh == 0 else jnp.logical_not(lo)
                qm = jnp.where(msk, qs, jnp.zeros_like(qs))
                prob = _mem_softmax(qm, kp)
                pb = prob.astype(BF16)
                outs.append(_dot(pb, vp))
                dom = jnp.where(msk, d_o, 0.0).astype(BF16)
                dp = _dot(dom, vp, NT)
                ds = (prob * (dp - jnp.sum(prob * dp, axis=-1, keepdims=True))).astype(BF16)
                dqs.append(_dot(ds, kp))
                dk_ref[:, sl] += _dot(ds, qm, TN)
                dv_ref[:, sl] += _dot(pb, dom, TN)
            o = jnp.where(lo, outs[0], outs[1])
            dqn = jnp.where(lo, dqs[0], dqs[1]) * ATT_SCALE
            dpm_ref[:, sl] = _head_norm_bwd(dqn, qhat, rr, g, lo).astype(BF16)
            dpm_ref[:, MEM_WIDTH + p * LANES:MEM_WIDTH + (p + 1) * LANES] = (
                dmo * o * (sg * (1.0 + gm * (1.0 - sg)))).astype(BF16)
            gacc_ref[:, sl] += jnp.sum(dqn * qhat, axis=0, keepdims=True)

        @pl.when(i == n - 1)
        def _():
            own = jnp.where(_same_group((POOL_WIDTH, POOL_WIDTH)), dw_ref[...], 0.0)
            dw4_ref[...] = jnp.dot(own, _group_onehot((POOL_WIDTH, HEAD_DIM), True), preferred_element_type=F32,
                                   precision=lax.Precision.HIGHEST)
            dg_ref[...] = _fold_heads(gacc_ref[...])

            lo_m = _lane_lo((nm, LANES))
            kacc = []
            for p in range(MEM_WIDTH // LANES):
                sl = slice(p * LANES, (p + 1) * LANES)
                kb = kv_ref[:, sl]
                rr = _head_rms(kb, lo_m)
                khat = kb * rr
                dk = dk_ref[:, sl]
                dkv_ref[:, sl] = _head_norm_bwd(dk, khat, rr, kg_ref[:, sl], lo_m).astype(BF16)
                kacc.append(jnp.sum(dk * khat, axis=0, keepdims=True))
            dkg_ref[...] = _fold_heads(jnp.concatenate(kacc, axis=1))
            dkv_ref[:, MEM_WIDTH:] = dv_ref[...].astype(BF16)
            dkv = dkv_ref[...]
            dwkv_ref[...] = _dot(mn_ref[...], dkv, TN)
            dmn = _dot(dkv, wkv_ref[...], NT)
            xm = mem_ref[...]
            rr = lax.rsqrt(jnp.mean(xm * xm, axis=-1, keepdims=True) + EPS)
            dmg_ref[...] = jnp.sum(dmn * (xm * rr), axis=0, keepdims=True)

    def rev(w):
        return _rows_rev(t, w, n)

    row = jax.ShapeDtypeStruct((1, LANES), F32)
    return pl.pallas_call(
        body, name="side_bwd", grid=(n,),
        out_shape=(jax.ShapeDtypeStruct((s, 512), BF16), jax.ShapeDtypeStruct((s, 512), BF16),
                   jax.ShapeDtypeStruct((POOL_ROWS, HEAD_DIM), F32), jax.ShapeDtypeStruct((1, POOL_WIDTH), F32), row,
                   jax.ShapeDtypeStruct((D_MODEL, 2 * MEM_WIDTH), F32), jax.ShapeDtypeStruct((1, D_MODEL), F32), row),
        in_specs=[rev(512), rev(POOL_WIDTH), rev(POOL_WIDTH), _full((POOL_ROWS, HEAD_DIM)), _full((1, POOL_WIDTH)),
                  rev(512), rev(MEM_WIDTH), _full((N_MEM, MEM_WIDTH)), _full((N_MEM, MEM_WIDTH)), _full((1, MEM_WIDTH)),
                  _full((nm, 2 * MEM_WIDTH)), _full((nm, D_MODEL)), _full((nm, D_MODEL)),
                  _full((D_MODEL, 2 * MEM_WIDTH)), _full((1, MEM_WIDTH)), _full((1, D_MODEL))],
        out_specs=(rev(512), rev(512), _full((POOL_ROWS, HEAD_DIM)), _full((1, POOL_WIDTH)), _full((1, LANES)),
                   _full((D_MODEL, 2 * MEM_WIDTH)), _full((1, D_MODEL)), _full((1, LANES))),
        scratch_shapes=[pltpu.VMEM((ext, POOL_WIDTH), F32), pltpu.VMEM((POOL_WIDTH, POOL_WIDTH), BF16),
                        pltpu.VMEM((POOL_WIDTH, POOL_WIDTH), F32), pltpu.VMEM((N_MEM, MEM_WIDTH), F32),
                        pltpu.VMEM((N_MEM, MEM_WIDTH), F32), pltpu.VMEM((1, MEM_WIDTH), F32),
                        pltpu.VMEM((nm, 2 * MEM_WIDTH), BF16)],
        compiler_params=_params(),
    )(pa, db, dma, w4, pscale, pm, dmm, kmn, vmb, mq_g, kv, mnb, mem, w_kv, mk_g, mem_norm_g)


FOX_BWD_HEADS = 4


def _fox_bwd(ka, va, qa, doa, rr, gparts, axes):
    s = ka.shape[0]
    t = TILE
    n = s // t
    heads = FOX_BWD_HEADS
    groups = FOX_HEADS // heads
    group_w = heads * LANES
    na = len(gparts)

    def body(*refs):
        ka_ref, va_ref, qa_ref, doa_ref, rr_ref = refs[0:5]
        g_refs = refs[5:5 + na]
        dka_ref, dva_ref, dqa_ref = refs[5 + na:8 + na]
        out_refs = refs[8 + na:8 + 2 * na]
        bufs = tuple(refs[8 + (2 + k) * na:8 + (3 + k) * na] for k in range(5))
        j = pl.program_id(1)
        step_id = pl.program_id(0) * n + j
        red = _ShardReduce(g_refs, out_refs, axes, bufs, *refs[8 + 7 * na:]) if na else None

        @pl.when(j == 0)
        def _():
            dqa_ref[...] = jnp.zeros_like(dqa_ref)

        if red is not None:
            pl.when(step_id == 0)(red.exchange_with_sibling)

            @pl.when(step_id == 1)
            def _():
                for k in (1, 2, 3):
                    red.send_to_chip(k)
                red.keep_mine()

        causal = lax.broadcasted_iota(jnp.int32, (t, t), 0) <= lax.broadcasted_iota(jnp.int32, (t, t), 1)
        kas = [ka_ref[:, hh * LANES:(hh + 1) * LANES] for hh in range(heads)]
        vas = [va_ref[:, hh * LANES:(hh + 1) * LANES] for hh in range(heads)]

        def step(i, carry, masked):
            rows = pl.ds(pl.multiple_of(i * t, t), t)
            new = []
            for hh in range(heads):
                cols = slice(hh * LANES, (hh + 1) * LANES)
                dk_a, dv_a = carry[hh]
                qb = qa_ref[rows, cols]
                d_o = doa_ref[rows, cols]
                arg = _dot(kas[hh], qb, NT) - rr_ref[hh // 2, i, hh % 2:hh % 2 + 1, :]
                if masked:
                    arg = jnp.where(causal, arg, -1e30)
                pt = jnp.exp(arg)
                dst = (pt * _dot(vas[hh], d_o, NT)).astype(BF16)
                dv_a = dv_a + _dot(pt.astype(BF16), d_o)
                dk_a = dk_a + _dot(dst, qb)
                dqa_ref[rows, cols] += _dot(dst, kas[hh], TN)
                new.append((dk_a, dv_a))
            return tuple(new)

        zero = jnp.zeros((t, LANES), F32)
        carry = step(j, ((zero, zero),) * heads, masked=True)
        res = lax.fori_loop(j + 1, n, functools.partial(step, masked=False), carry)
        for hh in range(heads):
            cols = slice(hh * LANES, (hh + 1) * LANES)
            dka_ref[:, cols] = res[hh][0]
            dva_ref[:, cols] = res[hh][1]

        if red is not None:
            @pl.when(step_id == groups * n - 1)
            def _():
                red.sum_and_share()
                red.finish()

    tile_spec = pl.BlockSpec((t, group_w), lambda p, j: (j, p))
    full_spec = pl.BlockSpec((s, group_w), lambda p, j: (0, p))
    any_spec = pl.BlockSpec(memory_space=pl.ANY)
    scratch = _ShardReduce.scratch(gparts, axes)
    if na:
        scratch += [pltpu.SemaphoreType.DMA((_ShardReduce.SEMS * na,)), pltpu.SemaphoreType.DMA((_ShardReduce.SEMS * na,)),
                    pltpu.SemaphoreType.DMA((_ShardReduce.LOCAL * na,))]
    return pl.pallas_call(
        body, name="fox_bwd", grid=(groups, n),
        out_shape=(jax.ShapeDtypeStruct((s, HEAD_BLOCKS), F32),) * 3
        + tuple(jax.ShapeDtypeStruct(_shard_shape(g), F32) for g in gparts),
        in_specs=[tile_spec, tile_spec, full_spec, full_spec,
                  pl.BlockSpec((heads // 2, n, 8, t), lambda p, j: (p, 0, 0, 0))] + [any_spec] * na,
        out_specs=(tile_spec, tile_spec, full_spec) + (any_spec,) * na,
        scratch_shapes=scratch, compiler_params=_params(2, VMEM_LIMIT_FOX_BWD),
    )(ka, va, qa, doa, rr, *gparts)


def _fox_post_tile(i, n, t, dqa_ref, dka_ref, dva_ref, qk_ref, fb_ref, bf_ref, qg_ref, kg_ref,
                   dqk_ref, dv_ref, dfb_ref, dqg_ref, dkg_ref, dbf_ref, qacc_ref, kacc_ref, carry_ref,
                   between):
    @pl.when(i == 0)
    def _():
        qacc_ref[...] = jnp.zeros_like(qacc_ref)
        kacc_ref[...] = jnp.zeros_like(kacc_ref)
        dbf_ref[...] = jnp.zeros_like(dbf_ref)
        carry_ref[...] = jnp.zeros_like(carry_ref)

    lane = lax.broadcasted_iota(jnp.int32, (t, LANES), 1)
    row = lax.broadcasted_iota(jnp.int32, (t, LANES), 0)
    lo = lane < HEAD_DIM

    def head_blocks(ref, p):
        return ref[:, 2 * p * LANES:(2 * p + 1) * LANES], ref[:, (2 * p + 1) * LANES:(2 * p + 2) * LANES]

    def issue(k):
        if between[k] is not None:
            between[k]()

    sums = []
    pairs = FOX_WIDTH // LANES
    for side, (src_ref, g_ref, acc_ref, scale) in enumerate(((dqa_ref, qg_ref, qacc_ref, ATT_SCALE),
                                                             (dka_ref, kg_ref, kacc_ref, 1.0))):
        total = jnp.zeros((t, LANES), F32)
        for p in range(pairs):
            issue(side * pairs + p)
            sl = slice(p * LANES, (p + 1) * LANES)
            cols = slice(side * FOX_WIDTH + p * LANES, side * FOX_WIDTH + (p + 1) * LANES)
            if side == 0:
                dv_ref[:, sl] = _pair_block(*head_blocks(dva_ref, p), lo).astype(BF16)
            d0, d1 = head_blocks(src_ref, p)
            total = total + (d0 + d1)
            raw = qk_ref[:, cols]
            rr = _head_rms(raw, lo)
            xhat = raw * rr
            dn = _pair_block(d0, d1, lo) * scale
            dqk_ref[:, cols] = _head_norm_bwd(dn, xhat, rr, g_ref[:, sl], lo).astype(BF16)
            acc_ref[:, sl] += jnp.sum(dn * xhat, axis=0, keepdims=True)
        sums.append(total)
    issue(2 * pairs)
    dq_sum, dk_sum = sums

    acc = (pltpu.roll(dq_sum, LANES - KEY_SUM_LANE, axis=1) - pltpu.roll(dk_sum, LANES - QUERY_SUM_LANE, axis=1))
    acc = jnp.where(lane < FOX_HEADS, acc, 0.0)
    sh = 1
    while sh < t:
        acc = acc + jnp.where(row < t - sh, pltpu.roll(acc, t - sh, axis=0), 0.0)
        sh *= 2
    dlogf = acc + carry_ref[...]
    dfb_ref[...] = dlogf
    carry_ref[...] = dfb_ref[0:1, :]
    z = fb_ref[...] + bf_ref[...]
    dz = jnp.where(lane < FOX_HEADS, dlogf * (1.0 / (1.0 + jnp.exp(z))), 0.0)
    dfb_ref[...] = dz
    dbf_ref[...] += jnp.sum(dz, axis=0, keepdims=True)

    @pl.when(i == n - 1)
    def _():
        dqg_ref[...] = _fold_heads(qacc_ref[...])
        dkg_ref[...] = _fold_heads(kacc_ref[...])


def _assemble_dproj(dp_ref, dpa_ref, dqk_ref, dv_ref, dgb_ref, dpm_ref, dfb_ref):
    dp_ref[:, PA_LO:QB_LO] = dpa_ref[...]
    dp_ref[:, QB_LO:VB_LO] = dqk_ref[...]
    dp_ref[:, VB_LO:GB_LO] = dv_ref[...]
    dp_ref[:, GB_LO:PM_LO] = dgb_ref[...]
    dp_ref[:, PM_LO:FB_LO] = dpm_ref[...]
    dp_ref[:, FB_LO:PROJ_PAD] = dfb_ref[...].astype(BF16)


def _dproj_specs(t):
    return [_rows(t, 512), _rows(t, 2 * FOX_WIDTH), _rows(t, FOX_WIDTH), _rows(t, FOX_WIDTH), _rows(t, 512),
            _rows(t, LANES)]


IN_BWD_X_TILE = 256


def _in_bwd_x(x, dy, norm_g, wp, dparts, gparts, axes, smalls):
    s = x.shape[0]
    t = IN_BWD_X_TILE
    n = s // t
    na = len(gparts)
    n_dp = len(dparts)
    vec_leaves, loss_row, dw4 = smalls if smalls is not None else ((), None, None)
    nv = len(vec_leaves)
    n_small = nv + 2 if smalls is not None else 0
    small_base = _ShardReduce.SEMS * na

    def body(*refs):
        x_ref, dy_ref, g_ref, wp_ref = refs[0:4]
        dp_parts = refs[4:4 + n_dp]
        o = 4 + n_dp
        g_refs = refs[o:o + na]
        small_in = refs[o + na:o + na + n_small]
        o += na + n_small
        gx_ref, dg_ref = refs[o:o + 2]
        out_refs = refs[o + 2:o + 2 + na]
        small_out = refs[o + 2 + na:o + 2 + na + (2 if smalls is not None else 0)]
        o += 2 + na + len(small_out)
        dp_ref = refs[o]
        bufs = tuple(refs[o + 1 + k * na:o + 1 + (k + 1) * na] for k in range(5))
        rest = refs[o + 1 + 5 * na:]

        i = pl.program_id(0)
        if na or smalls is not None:
            send_sems, recv_sems, local_sems = rest[-3:]
        red = _ShardReduce(g_refs, out_refs, axes, bufs, send_sems, recv_sems, local_sems) if na else None

        @pl.when(i == 0)
        def _():
            dg_ref[...] = jnp.zeros_like(dg_ref)
            if red is not None:
                red.exchange_with_sibling()

        if red is not None:
            for k in (1, 2, 3):
                pl.when(i == k)(functools.partial(red.send_to_chip, k))
            pl.when(i == 4)(red.keep_mine)

        _assemble_dproj(dp_ref, *dp_parts)
        dh = _dot(dp_ref[...], wp_ref[...])
        xv = x_ref[...]
        rr = lax.rsqrt(jnp.mean(xv * xv, axis=-1, keepdims=True) + EPS)
        xhat = xv * rr
        scaled = dh * g_ref[...]
        gx_ref[...] = dy_ref[...] + rr * (scaled - xhat * jnp.mean(xhat * scaled, axis=-1, keepdims=True))
        dg_ref[...] += jnp.sum(dh * xhat, axis=0, keepdims=True)

        def small_all_reduce():
            leaf_refs, (loss_ref, dw4_ref) = small_in[0:nv], small_in[nv:]
            vec_out, dw4_out = small_out
            vec_mine, vec_recv, dw4_recv = rest[0:3]
            cx, cy, c = _my_place()
            me_lin = 4 * cx + 2 * cy + c

            def copy(k, src, dst, base):
                peer = (me_lin + k) % 8
                return pltpu.make_async_remote_copy(
                    src_ref=src, dst_ref=dst.at[me_lin], send_sem=send_sems.at[base + k - 1],
                    recv_sem=recv_sems.at[base + k - 1], device_id=(peer // 4, (peer // 2) % 2, peer % 2),
                    device_id_type=MESH)

            vec_mine[...] = jnp.zeros_like(vec_mine)
            vec_mine[0:1, :] = dg_ref[...]
            for (_, row, _), ref in zip(VEC_LEAVES[1:], leaf_refs):
                vec_mine[row:row + 1, 0:ref.shape[1]] = ref[...]
            vec_mine[VEC_LOSS_ROW:VEC_LOSS_ROW + 1, 0:LANES] = loss_ref[...]
            copies = [copy(k, src, dst, base) for k in range(1, 8)
                      for src, dst, base in ((vec_mine, vec_recv, small_base), (dw4_ref, dw4_recv, small_base + 7))]
            for cp in copies:
                cp.start()
            for cp in copies:
                cp.wait_recv()
            vec_recv[me_lin] = vec_mine[...]
            dw4_recv[me_lin] = dw4_ref[...]
            vtot, wtot = vec_recv[0], dw4_recv[0]
            for d in range(1, 8):
                vtot = vtot + vec_recv[d]
                wtot = wtot + dw4_recv[d]
            vec_out[...] = vtot
            dw4_out[...] = wtot
            for cp in copies:
                cp.wait_send()

        @pl.when(i == n - 1)
        def _():
            if red is not None:
                red.sum_and_share()
            if smalls is not None:
                small_all_reduce()
            if red is not None:
                red.finish()

    any_spec = pl.BlockSpec(memory_space=pl.ANY)
    scratch = [pltpu.VMEM((t, PROJ_PAD), BF16)] + _ShardReduce.scratch(gparts, axes)
    out_shape = [jax.ShapeDtypeStruct((s, D_MODEL), F32), jax.ShapeDtypeStruct((1, D_MODEL), F32)]
    out_shape += [jax.ShapeDtypeStruct(_shard_shape(g), F32) for g in gparts]
    out_specs = [_rows(t, D_MODEL), _full((1, D_MODEL))] + [any_spec] * na
    small_args = []
    if smalls is not None:
        small_args = [*vec_leaves, loss_row, dw4]
        out_shape += [jax.ShapeDtypeStruct((VEC_ROWS, D_MODEL), F32), jax.ShapeDtypeStruct(dw4.shape, F32)]
        out_specs += [_full((VEC_ROWS, D_MODEL)), _full(dw4.shape)]
        scratch += [pltpu.VMEM((VEC_ROWS, D_MODEL), F32), pltpu.VMEM((8, VEC_ROWS, D_MODEL), F32),
                    pltpu.VMEM((8,) + dw4.shape, F32)]
    if na or smalls is not None:
        n_sems = small_base + 14
        scratch += [pltpu.SemaphoreType.DMA((n_sems,)), pltpu.SemaphoreType.DMA((n_sems,)),
                    pltpu.SemaphoreType.DMA((max(_ShardReduce.LOCAL * na, 1),))]
    return pl.pallas_call(
        body, name="in_bwd_x", grid=(n,), out_shape=tuple(out_shape),
        in_specs=[_rows(t, D_MODEL), _rows(t, D_MODEL), _full((1, D_MODEL)),
                  pl.BlockSpec((PROJ_PAD, D_MODEL), lambda i: (0, 0), pipeline_mode=pl.Buffered(1))]
        + _dproj_specs(t) + [any_spec] * na + [_full(a.shape) for a in small_args],
        out_specs=tuple(out_specs), scratch_shapes=scratch, compiler_params=_params(),
    )(x, dy, norm_g, wp, *dparts, *gparts, *small_args)


def _in_bwd_w(hb, dpa, dgb, dpm, fox):
    s = hb.shape[0]
    t = TILE
    n = s // t
    f_hi = F_ORIG_LO + FOX_HEADS
    n_in = 4 + len(fox)

    def body(*refs):
        h_ref, dpa_ref, dgb_ref, dpm_ref = refs[0:4]
        fox_refs = refs[4:n_in]
        dw_ref, dqk_ref, dv_ref, dfb_ref, dqg_ref, dkg_ref, dbf_ref = refs[n_in:n_in + 7]
        fox_scratch = refs[n_in + 7:]
        i = pl.program_id(0)

        @pl.when(i == 0)
        def _():
            dw_ref[...] = jnp.zeros_like(dw_ref)

        hv = h_ref[...]

        def rows_of(lo, ref, cols=slice(None)):
            def add():
                dproj = ref[:, cols]
                dw_ref[lo:lo + dproj.shape[1], :] += _dot(dproj, hv, TN)
            return add

        q_cols, k_cols = slice(0, FOX_WIDTH), slice(FOX_WIDTH, 2 * FOX_WIDTH)
        between = (rows_of(0, dpa_ref), rows_of(f_hi, dgb_ref), rows_of(f_hi + FOX_WIDTH, dpm_ref), None,
                   rows_of(QB_LO, dqk_ref, q_cols), rows_of(VB_LO, dv_ref), None, None, rows_of(KB_LO, dqk_ref, k_cols))
        _fox_post_tile(i, n, t, *fox_refs, dqk_ref, dv_ref, dfb_ref, dqg_ref, dkg_ref, dbf_ref, *fox_scratch, between)
        dw_ref[F_ORIG_LO:f_hi, :] += _dot(dfb_ref[...].astype(BF16), hv, TN)[0:FOX_HEADS, :]

    def rev(w):
        return _rows_rev(t, w, n)

    row = jax.ShapeDtypeStruct((1, LANES), F32)
    return pl.pallas_call(
        body, name="in_bwd_w", grid=(n,),
        out_shape=(jax.ShapeDtypeStruct((IN_WIDTH, D_MODEL), F32), jax.ShapeDtypeStruct((s, 2 * FOX_WIDTH), BF16),
                   jax.ShapeDtypeStruct((s, FOX_WIDTH), BF16), jax.ShapeDtypeStruct((s, LANES), F32), row, row, row),
        in_specs=[rev(D_MODEL), rev(512), rev(FOX_WIDTH), rev(512), rev(HEAD_BLOCKS), rev(HEAD_BLOCKS),
                  rev(HEAD_BLOCKS), rev(2 * FOX_WIDTH), rev(LANES), _full((1, LANES)), _full((1, FOX_WIDTH)),
                  _full((1, FOX_WIDTH))],
        out_specs=(pl.BlockSpec((IN_WIDTH, D_MODEL), lambda i: (0, 0), pipeline_mode=pl.Buffered(1)),
                   rev(2 * FOX_WIDTH), rev(FOX_WIDTH), rev(LANES), _full((1, LANES)), _full((1, LANES)),
                   _full((1, LANES))),
        scratch_shapes=[pltpu.VMEM((1, FOX_WIDTH), F32), pltpu.VMEM((1, FOX_WIDTH), F32), pltpu.VMEM((1, LANES), F32)],
        compiler_params=_params(),
    )(hb, dpa, dgb, dpm, *fox)


def _adamw_math(w_ref, gv, m_ref, v_ref, d_ref, nm_ref, nv_ref):
    nm = ADAM_B1 * m_ref[...] + (1.0 - ADAM_B1) * gv
    nv = ADAM_B2 * v_ref[...] + (1.0 - ADAM_B2) * (gv * gv)
    m_hat = nm / (1.0 - ADAM_B1 ** ADAM_STEP)
    v_hat = nv / (1.0 - ADAM_B2 ** ADAM_STEP)
    d_ref[...] = -ADAM_LR * (m_hat / (jnp.sqrt(v_hat) + ADAM_EPS) + ADAM_WD * w_ref[...])
    nm_ref[...] = nm
    nv_ref[...] = nv


def _adamw_flat(name, w, g, m, v):
    rows, cols = g.shape
    per_row = cols // LANES

    def body(w_ref, g_ref, m_ref, v_ref, gf_ref, d_ref, nm_ref, nv_ref):
        for k in range(per_row):
            gf_ref[pl.ds(k, rows, stride=per_row), :] = g_ref[:, k * LANES:(k + 1) * LANES]
        _adamw_math(w_ref, gf_ref[...], m_ref, v_ref, d_ref, nm_ref, nv_ref)

    def whole(shape):
        return pl.BlockSpec(shape, lambda i: (0, 0), pipeline_mode=pl.Buffered(1))

    return pl.pallas_call(
        body, name=name, grid=(1,),
        out_shape=(jax.ShapeDtypeStruct(w.shape, F32),) * 4,
        in_specs=[whole(w.shape), whole(g.shape), whole(w.shape), whole(w.shape)], out_specs=(whole(w.shape),) * 4,
        compiler_params=_params(),
    )(w, g, m, v)


def _adamw_rest(vec, dw4, leaves, pool, shards):
    nl = len(VEC_LEAVES) + 1
    ns = len(shards)

    def body(*refs):
        vec_ref, dw4_ref = refs[0:2]
        wmv = refs[2:2 + 3 * nl]
        shard_in = refs[2 + 3 * nl:2 + 3 * nl + 4 * ns]
        o = 2 + 3 * nl + 4 * ns
        loss_ref = refs[o]
        outs = refs[o + 1:o + 1 + 4 * nl]
        shard_out = refs[o + 1 + 4 * nl:]
        loss_ref[...] = vec_ref[VEC_LOSS_ROW:VEC_LOSS_ROW + 1, 0:1]
        for k in range(nl):
            if k < nl - 1:
                _, row, width = VEC_LEAVES[k]
                gv = vec_ref[row:row + 1, 0:width]
            else:
                gv = dw4_ref[...]
            w_ref, m_ref, v_ref = wmv[3 * k:3 * k + 3]
            g_ref, d_ref, nm_ref, nv_ref = outs[4 * k:4 * k + 4]
            g_ref[...] = gv
            _adamw_math(w_ref, gv, m_ref, v_ref, d_ref, nm_ref, nv_ref)
        for k in range(ns):
            w_ref, g_ref, m_ref, v_ref = shard_in[4 * k:4 * k + 4]
            _adamw_math(w_ref, g_ref[...], m_ref, v_ref, *shard_out[3 * k:3 * k + 3])

    shapes = [jax.ShapeDtypeStruct((1, width), F32) for _, _, width in VEC_LEAVES] + [
        jax.ShapeDtypeStruct(dw4.shape, F32)]
    flat_in = [a for triple in list(leaves) + [pool] for a in triple] + [a for quad in shards for a in quad]
    res = pl.pallas_call(
        body, name="adamw_rest",
        out_shape=(jax.ShapeDtypeStruct((1, 1), F32),) + tuple(s for s in shapes for _ in range(4))
        + tuple(jax.ShapeDtypeStruct(quad[0].shape, F32) for quad in shards for _ in range(3)),
        compiler_params=pltpu.CompilerParams(vmem_limit_bytes=VMEM_LIMIT),
    )(vec, dw4, *flat_in)
    per = [res[1 + 4 * k:5 + 4 * k] for k in range(nl)]
    big = res[1 + 4 * nl:]
    return (res[0], [p[0] for p in per], [p[1] for p in per], [p[2] for p in per], [p[3] for p in per],
            [big[3 * k:3 * k + 3] for k in range(ns)])


def _tile_heads(g, n):
    return jnp.tile(g.reshape(1, HEAD_DIM), (1, n))


def kernel(x, mem, norm_g, w_in, b_f, w_pool, pool_scale, fox_q_g, fox_k_g, mem_norm_g, w_mem_kv, mem_q_g, mem_k_g, w_out, loss_target, m_norm_g, m_w_in, m_b_f, m_w_pool, m_pool_scale, m_fox_q_g, m_fox_k_g, m_mem_norm_g, m_w_mem_kv, m_mem_q_g, m_mem_k_g, m_w_out, v_norm_g, v_w_in, v_b_f, v_w_pool, v_pool_scale, v_fox_q_g, v_fox_k_g, v_mem_norm_g, v_w_mem_kv, v_mem_q_g, v_mem_k_g, v_w_out):
    w_in_t = w_in[0].T
    axes = (1, 0, 0)

    g_in, g_kv, g_out = _all_gather_weights([w_in_t, w_mem_kv[0], w_out[0]], axes)
    tiled = _tiled_params(b_f, fox_q_g, fox_k_g, mem_q_g, mem_k_g)
    fwd, wp = _fwd_in(x[0], norm_g, g_in, *tiled[0:3])
    w_kv_b = g_kv.reshape(D_MODEL, 2 * MEM_WIDTH)
    w_out_b = g_out.reshape(D_MODEL, D_MODEL)
    w4 = w_pool.reshape(POOL_ROWS, HEAD_DIM)
    dy, hb, dpa, dgb, dpm, fox, g_w_kv, g_w_out, (dmemnorm_g, dpscale, dmq_g, dmk_g), loss_row, dw4 = _local_partials(
        x[0], mem[0], loss_target[0], fwd, w_kv_b, w_out_b, tiled, w4, pool_scale, mem_norm_g, axes[1:])
    dwp, dqk, dvb, dfb, dfq_g, dfk_g, dbf = _in_bwd_w(hb, dpa, dgb, dpm, fox)
    dparts = (dpa, dqk, dvb, dgb, dpm, dfb)
    vec_leaves = (dmemnorm_g, dpscale, dbf, dfq_g, dfk_g, dmq_g, dmk_g)
    grad_x, _, g_w_in_t, vec, dw4_sum = _in_bwd_x(
        x[0], dy, norm_g, wp, dparts, [dwp], axes[0:1], (vec_leaves, loss_row, dw4))

    small_wmv = [(norm_g, m_norm_g, v_norm_g), (mem_norm_g, m_mem_norm_g, v_mem_norm_g),
                 (pool_scale, m_pool_scale, v_pool_scale), (b_f, m_b_f, v_b_f), (fox_q_g, m_fox_q_g, v_fox_q_g),
                 (fox_k_g, m_fox_k_g, v_fox_k_g), (mem_q_g, m_mem_q_g, v_mem_q_g), (mem_k_g, m_mem_k_g, v_mem_k_g)]
    pool_wmv = tuple(a.reshape(POOL_ROWS, HEAD_DIM) for a in (w_pool, m_w_pool, v_w_pool))
    loss, *small_out, (upd_kv, upd_out) = _adamw_rest(
        vec, dw4_sum, small_wmv, pool_wmv, [(w_mem_kv[0], g_w_kv, m_w_mem_kv[0], v_w_mem_kv[0]),
                                             (w_out[0], g_w_out, m_w_out[0], v_w_out[0])])
    tiles = D_MODEL // LANES

    def flat(a):
        return a.reshape(tiles, LANES, -1).transpose(2, 0, 1).reshape(-1, LANES)

    def unflat(a):
        return a.reshape(-1, tiles, LANES).transpose(1, 2, 0).reshape(w_in.shape)

    g_in_flat, *upd_in = _adamw_flat("adamw_w_in", flat(w_in), g_w_in_t, flat(m_w_in), flat(v_w_in))
    big = [[unflat(g_in_flat), g_w_kv[None], g_w_out[None]]]
    big += [[unflat(upd_in[k]), upd_kv[k][None], upd_out[k][None]] for k in range(3)]

    def leaves(k):
        sm = small_out[k]
        b_in, b_kv, b_out = big[k]
        return (sm[0], b_in, sm[3], sm[8].reshape(w_pool.shape), sm[2], sm[4], sm[5], sm[1], b_kv, sm[6], sm[7], b_out)

    return (loss.reshape(()), grad_x[None], *leaves(0), *leaves(1), *leaves(2), *leaves(3))


def _tiled_params(b_f, fox_q_g, fox_k_g, mem_q_g, mem_k_g):
    return (jnp.pad(b_f, ((0, 0), (0, LANES - FOX_HEADS))), _tile_heads(fox_q_g, FOX_HEADS),
            _tile_heads(fox_k_g, FOX_HEADS), _tile_heads(mem_q_g, 4), _tile_heads(mem_k_g, 4))


def _local_partials(xs, mems, tgt, fwd, w_kv_b, w_out_b, tiled, w4, pool_scale, mem_norm_g, axes):
    hb, pa, qk, qa, ka, va, gb, pm, fb = fwd
    bf_pad, fq_g, fk_g, mq_g, mk_g = tiled

    ma, db, mm, mnb, kv, kmn, vmb = _side_fwd(pa, pm, w4, pool_scale, mq_g, mems, mem_norm_g, w_kv_b, mk_g)
    o, mb, r4 = _fox_fwd(qa, ka, va, gb)
    dy, dma, dmm, dw_out, loss_row, doa, dgb, rr = _out_loss(xs, tgt, ma, mb, mm, w_out_b, gb, o, r4)

    dpa, dpm, dw4, dpscale, dmq_g, dw_kv, dmemnorm_g, dmk_g = _side_bwd(
        pa, db, dma, w4, pool_scale, pm, dmm, kmn, vmb, mq_g, kv, mnb, mems, w_kv_b, mk_g, mem_norm_g)
    if axes:
        parts = [dw_kv.reshape(4, D_MODEL // 4, 2 * MEM_WIDTH), dw_out.reshape(4, D_MODEL // 4, D_MODEL)]
        dka, dva, dqa, dw_kv, dw_out = _fox_bwd(ka, va, qa, doa, rr, parts, axes)
    else:
        dka, dva, dqa = _fox_bwd(ka, va, qa, doa, rr, [], ())
    fox = (dqa, dka, dva, qk, fb, bf_pad, fq_g, fk_g)
    return dy, hb, dpa, dgb, dpm, fox, dw_kv, dw_out, (dmemnorm_g, dpscale, dmq_g, dmk_g), loss_row, dw4
```

```python
import functools

import jax
import jax.numpy as jnp
from jax import lax
from jax.experimental import pallas as pl
from jax.experimental.pallas import tpu as pltpu

F32 = jnp.float32
BF16 = jnp.bfloat16
MESH = pl.DeviceIdType.MESH

D_MODEL = 1024
HEAD_DIM = 64
POOL_WIDTH = 256
FOX_WIDTH = 512
FOX_HEADS = 8
MEM_WIDTH = 256
N_MEM = 256
IN_WIDTH = 3080
EPS = 1e-6
ATT_SCALE = 0.125

ADAM_LR = 0.001
ADAM_B1 = 0.9
ADAM_B2 = 0.999
ADAM_EPS = 1e-08
ADAM_WD = 0.01
ADAM_STEP = 10

LANES = 128
PA_LO, QB_LO, KB_LO, VB_LO, GB_LO, PM_LO, FB_LO, PROJ_PAD = 0, 512, 1024, 1536, 2048, 2560, 3072, 3200
F_ORIG_LO = 2048

TILE = 512
VMEM_LIMIT = 56 * 1024 * 1024
VMEM_LIMIT_FOX_BWD = 58 * 1024 * 1024

VEC_LEAVES = (("norm_g", 0, 1024), ("mem_norm_g", 1, 1024), ("pool_scale", 2, 256), ("b_f", 3, 8),
              ("fox_q_g", 4, 64), ("fox_k_g", 5, 64), ("mem_q_g", 6, 64), ("mem_k_g", 7, 64))
VEC_LOSS_ROW = 8
VEC_ROWS = 16
POOL_ROWS = 256


def _params(n_grid=1, vmem=VMEM_LIMIT):
    return pltpu.CompilerParams(dimension_semantics=("arbitrary",) * n_grid, vmem_limit_bytes=vmem)


def _rows(t, w):
    return pl.BlockSpec((t, w), lambda i: (i, 0))


def _rows_rev(t, w, n):
    return pl.BlockSpec((t, w), lambda i: (n - 1 - i, 0))


def _full(shape):
    return pl.BlockSpec(shape, lambda i: (0,) * len(shape))


def _sig(x):
    return 1.0 / (1.0 + jnp.exp(-x))


def _lane_lo(shape):
    return lax.broadcasted_iota(jnp.int32, shape, 1) < HEAD_DIM


def _pair_sum(v, lo):
    s0 = jnp.sum(jnp.where(lo, v, 0.0), axis=-1, keepdims=True)
    s1 = jnp.sum(jnp.where(lo, 0.0, v), axis=-1, keepdims=True)
    return jnp.where(lo, s0, s1)


def _head_rms(blk, lo):
    return lax.rsqrt(_pair_sum(blk * blk, lo) * (1.0 / HEAD_DIM) + EPS)


def _head_norm_bwd(dyn, xhat, rr, g, lo):
    a = dyn * g
    return rr * (a - xhat * (_pair_sum(xhat * a, lo) * (1.0 / HEAD_DIM)))


def _fold_heads(acc):
    tot = acc[:, 0:LANES]
    for p in range(1, acc.shape[1] // LANES):
        tot = tot + acc[:, p * LANES:(p + 1) * LANES]
    return tot + pltpu.roll(tot, HEAD_DIM, axis=1)


def _lane_pick(v, lane, idx):
    return jnp.sum(jnp.where(lane == idx, v, 0.0), axis=-1, keepdims=True)


NT = (((1,), (1,)), ((), ()))
TN = (((0,), (0,)), ((), ()))


def _dot(a, b, dims=None):
    if dims is None:
        return jnp.dot(a, b, preferred_element_type=F32)
    return lax.dot_general(a, b, dims, preferred_element_type=F32)


def _my_place():
    return lax.axis_index("x"), lax.axis_index("y"), lax.axis_index("c")


def _half_dims(shape, axis):
    return (shape[0] // 2, shape[1]) if axis == 0 else (shape[0], shape[1] // 2)


def _shard_shape(g):
    return tuple(g.shape[1:]) if len(g.shape) == 3 else (g.shape[0] // 4, g.shape[1])


F32_ROWS = 8


def _shard_window(g):
    rows = _shard_shape(g)[0]
    if len(g.shape) == 3:
        return rows
    skew = max((j * rows) % F32_ROWS for j in range(4))
    return -(-(rows + skew) // F32_ROWS) * F32_ROWS


def _half_of(ref, axis, core, lead=False):
    rows, cols = ref.shape[-2:]
    if axis == 0:
        idx = (pl.ds(pl.multiple_of(core * (rows // 2), 16), rows // 2), slice(None))
    else:
        idx = (slice(None), pl.ds(pl.multiple_of(core * (cols // 2), LANES), cols // 2))
    return ref.at[(slice(None),) + idx] if lead else ref.at[idx]


class _HalfGather:
    def __init__(self, ins, outs, axes, f32_bufs, bf_bufs, send_sems, recv_sems, local_sems):
        self.ins, self.outs, self.axes = ins, outs, axes
        self.f32_bufs, self.bf_bufs = f32_bufs, bf_bufs
        self.send_sems, self.recv_sems, self.local_sems = send_sems, recv_sems, local_sems
        self.n = len(ins)
        x, y, self.c = _my_place()
        self.me, self.sibling = (x, y, self.c), (x, y, 1 - self.c)
        self.chips = [(1 - x, y), (x, 1 - y), (1 - x, 1 - y)]

    @staticmethod
    def scratch(shards, axes):
        dims = [_half_dims(a.shape, axis) for a, axis in zip(shards, axes)]
        n = len(shards)
        return [pltpu.VMEM(d, F32) for d in dims] + [pltpu.VMEM(d, BF16) for d in dims] + [
            pltpu.SemaphoreType.DMA((7 * n,)), pltpu.SemaphoreType.DMA((7 * n,)), pltpu.SemaphoreType.DMA((2 * n,))]

    @staticmethod
    def out_shapes(shards, axes):
        return tuple(jax.ShapeDtypeStruct((8,) + _half_dims(a.shape, axis), BF16) for a, axis in zip(shards, axes))

    def _blk(self, a, px, py, pc):
        return self.outs[a].at[4 * px + 2 * py + pc]

    def _copy(self, a, k, block, to, src=None):
        return pltpu.make_async_remote_copy(
            src_ref=self._blk(a, *block) if src is None else src, dst_ref=self._blk(a, *block),
            send_sem=self.send_sems.at[7 * a + k], recv_sem=self.recv_sems.at[7 * a + k], device_id=to,
            device_id_type=MESH)

    def _keep(self, a):
        return pltpu.make_async_copy(self.bf_bufs[a], self._blk(a, *self.me), self.local_sems.at[self.n + a])

    def _first(self, a):
        mine = [self._copy(a, 0, self.me, self.sibling, src=self.bf_bufs[a])]
        return mine + [self._copy(a, 1 + j, self.me, (*chip, self.c), src=self.bf_bufs[a])
                       for j, chip in enumerate(self.chips)]

    def send_mine(self):
        loads = [pltpu.make_async_copy(_half_of(self.ins[a], self.axes[a], self.c), self.f32_bufs[a],
                                       self.local_sems.at[a]) for a in range(self.n)]
        for cp in loads:
            cp.start()
        for a in range(self.n):
            loads[a].wait()
            self.bf_bufs[a][...] = self.f32_bufs[a][...].astype(BF16)
            self._keep(a).start()
            for cp in self._first(a):
                cp.start()

    def pass_on(self):
        for a in range(self.n):
            for j, chip in enumerate(self.chips):
                self._copy(a, 1 + j, (*chip, self.c), self.me).wait_recv()
                self._copy(a, 4 + j, (*chip, self.c), self.sibling).start()

    def finish(self):
        for a in range(self.n):
            self._copy(a, 0, self.sibling, self.me).wait_recv()
            for j, chip in enumerate(self.chips):
                self._copy(a, 4 + j, (*chip, 1 - self.c), self.me).wait_recv()
        for a in range(self.n):
            for cp in self._first(a):
                cp.wait_send()
            for j, chip in enumerate(self.chips):
                self._copy(a, 4 + j, (*chip, self.c), self.sibling).wait_send()
            self._keep(a).wait()


def _all_gather_weights(shards, axes):
    n = len(shards)

    def body(*refs):
        gather = _HalfGather(refs[0:n], refs[n:2 * n], axes, refs[2 * n:3 * n], refs[3 * n:4 * n], *refs[4 * n:])
        gather.send_mine()
        gather.pass_on()
        gather.finish()

    any_spec = pl.BlockSpec(memory_space=pl.ANY)
    return pl.pallas_call(
        body, name="weights_all_gather", out_shape=_HalfGather.out_shapes(shards, axes),
        in_specs=[any_spec] * n, out_specs=(any_spec,) * n, scratch_shapes=_HalfGather.scratch(shards, axes),
        compiler_params=pltpu.CompilerParams(vmem_limit_bytes=VMEM_LIMIT),
    )(*shards)


class _ShardReduce:
    SEMS = 8
    LOCAL = 5

    def __init__(self, g_refs, out_refs, axes, bufs, send_sems, recv_sems, local_sems):
        self.g_refs, self.out_refs, self.axes = g_refs, out_refs, axes
        self.recv_a, self.own_a, self.send_b, self.recv_b, self.fin = bufs
        self.send_sems, self.recv_sems, self.local_sems = send_sems, recv_sems, local_sems
        self.n = len(g_refs)
        x, y, self.c = _my_place()
        self.chip = 2 * x + y
        self.sibling = (x, y, 1 - self.c)

    @staticmethod
    def scratch(gparts, axes):
        assert all(len(g.shape) == 3 or axis == 1 for g, axis in zip(gparts, axes))
        dims = [_half_dims(_shard_shape(g), axis) for g, axis in zip(gparts, axes)]
        windows = [d if len(g.shape) == 3 else (_shard_window(g),) + d[1:] for g, d in zip(gparts, dims)]
        shapes = []
        for dtype, lead, per_array in ((F32, (4,), windows), (F32, (4,), windows), (BF16, (4,), dims),
                                       (BF16, (4,), dims), (F32, (), dims)):
            shapes += [pltpu.VMEM(lead + d, dtype) for d in per_array]
        return shapes

    def _shard_half(self, a, j, core):
        g = self.g_refs[a]
        if len(g.shape) == 3:
            return _half_of(g.at[j], self.axes[a], core)
        start = (j * _shard_shape(g)[0]) // F32_ROWS * F32_ROWS
        return _half_of(g.at[pl.ds(pl.multiple_of(start, F32_ROWS), _shard_window(g))], self.axes[a], core)

    def _to_sibling(self, a, j):
        return pltpu.make_async_remote_copy(
            src_ref=self._shard_half(a, j, 1 - self.c), dst_ref=self.recv_a[a].at[j],
            send_sem=self.send_sems.at[self.SEMS * a + j], recv_sem=self.recv_sems.at[self.SEMS * a + j], device_id=self.sibling,
            device_id_type=MESH)

    def _own(self, a, j):
        return pltpu.make_async_copy(self._shard_half(a, j, self.c), self.own_a[a].at[j],
                                     self.local_sems.at[self.LOCAL * a + j])

    def _to_chip(self, a, k):
        dest = (self.chip + k) % 4
        return pltpu.make_async_remote_copy(
            src_ref=self.send_b[a].at[dest], dst_ref=self.recv_b[a].at[self.chip],
            send_sem=self.send_sems.at[self.SEMS * a + 3 + k], recv_sem=self.recv_sems.at[self.SEMS * a + 3 + k],
            device_id=(dest // 2, dest % 2, self.c), device_id_type=MESH)

    def _give(self, a):
        return pltpu.make_async_remote_copy(
            src_ref=self.fin[a], dst_ref=_half_of(self.out_refs[a], self.axes[a], self.c),
            send_sem=self.send_sems.at[self.SEMS * a + 7], recv_sem=self.recv_sems.at[self.SEMS * a + 7], device_id=self.sibling,
            device_id_type=MESH)

    def _mine(self, a):
        return pltpu.make_async_copy(self.fin[a], _half_of(self.out_refs[a], self.axes[a], self.c),
                                     self.local_sems.at[self.LOCAL * a])

    def exchange_with_sibling(self):
        for k in (1, 2, 3, 0):
            j = (self.chip + k) % 4
            for a in range(self.n):
                self._to_sibling(a, j).start()
                self._own(a, j).start()

    def _chip_partial(self, a, j):
        self._own(a, j).wait()
        self._to_sibling(a, j).wait_recv()
        g = self.g_refs[a]
        if len(g.shape) == 3:
            self.send_b[a][j] = (self.own_a[a][j] + self.recv_a[a][j]).astype(BF16)
            return
        rows = _shard_shape(g)[0]
        for shard in range(4):
            @pl.when(j == shard)
            def _():
                at = pl.ds((shard * rows) % F32_ROWS, rows)
                self.send_b[a][shard] = (self.own_a[a][shard, at, :] + self.recv_a[a][shard, at, :]).astype(BF16)

    def send_to_chip(self, k):
        for a in range(self.n):
            self._chip_partial(a, (self.chip + k) % 4)
            self._to_chip(a, k).start()

    def keep_mine(self):
        for a in range(self.n):
            self._chip_partial(a, self.chip)
            keep = pltpu.make_async_copy(self.send_b[a].at[self.chip], self.recv_b[a].at[self.chip],
                                         self.local_sems.at[self.LOCAL * a + 4])
            keep.start()
            keep.wait()

    def sum_and_share(self):
        for a in range(self.n):
            for k in range(1, 4):
                self._to_chip(a, k).wait_recv()
            tot = self.recv_b[a][0].astype(F32) + self.recv_b[a][1].astype(F32)
            tot = tot + self.recv_b[a][2].astype(F32)
            self.fin[a][...] = tot + self.recv_b[a][3].astype(F32)
            self._give(a).start()
            self._mine(a).start()

    def finish(self):
        for a in range(self.n):
            self._give(a).wait_recv()
            self._mine(a).wait()
            self._give(a).wait_send()
            for j in range(4):
                self._to_sibling(a, j).wait_send()
            for k in range(1, 4):
                self._to_chip(a, k).wait_send()


def _mem_tokens_fwd(mem_ref, g_ref, w_ref, kg_ref, mn_ref, kv_ref, kn_ref, vm_ref):
    xm = mem_ref[...]
    rr = lax.rsqrt(jnp.mean(xm * xm, axis=-1, keepdims=True) + EPS)
    mnb = ((xm * rr) * g_ref[...]).astype(BF16)
    mn_ref[...] = mnb
    kv = _dot(mnb, w_ref[...])
    kv_ref[...] = kv
    lo = _lane_lo((xm.shape[0], LANES))
    for p in range(MEM_WIDTH // LANES):
        sl = slice(p * LANES, (p + 1) * LANES)
        kb = kv[:, sl]
        kn_ref[:, sl] = ((kb * _head_rms(kb, lo)) * kg_ref[:, sl]).astype(BF16)
    vm_ref[...] = kv[:, MEM_WIDTH:].astype(BF16)


AUG_LO = 64
KEY_SUM_LANE = 72
QUERY_SUM_LANE = 80
HEAD_BLOCKS = FOX_HEADS * LANES


def _ones3(lane):
    return jnp.where((lane >= AUG_LO) & (lane < AUG_LO + 3), 1.0, 0.0)


def _spread3(cols):
    hi = cols.astype(BF16)
    rest = cols - hi.astype(F32)
    mid = rest.astype(BF16)
    low = (rest - mid.astype(F32)).astype(BF16)
    r = lax.broadcasted_iota(jnp.int32, (LANES, HEAD_BLOCKS), 0)
    c = lax.broadcasted_iota(jnp.int32, (LANES, HEAD_BLOCKS), 1)
    out = None
    for k, part in enumerate((hi, mid, low)):
        term = _dot(part, jnp.where(c == r * LANES + (AUG_LO + k), 1.0, 0.0).astype(BF16))
        out = term if out is None else out + term
    return out


def _head_block(pair_blk, hh, lo, extras):
    src = pair_blk if hh == 0 else pltpu.roll(pair_blk, HEAD_DIM, axis=1)
    return jnp.where(lo, src, extras).astype(BF16)


def _pair_block(blk0, blk1, lo):
    return jnp.where(lo, blk0, pltpu.roll(blk1, HEAD_DIM, axis=1))


def _assemble_w_in(halves_ref, words_ref, wp_ref):
    shard = IN_WIDTH // 4
    half = D_MODEL // 2
    f_hi = F_ORIG_LO + FOX_HEADS
    for j in range(4):
        blocks = [pltpu.bitcast(halves_ref[2 * j + c], jnp.uint32) for c in range(2)]
        for lo, hi, to in ((0, F_ORIG_LO, PA_LO), (F_ORIG_LO, f_hi, FB_LO), (f_hi, IN_WIDTH, GB_LO)):
            a, b = max(lo, shard * j), min(hi, shard * (j + 1))
            if a < b:
                for c in range(2):
                    words_ref[(to + a - lo) // 2:(to + b - lo) // 2, c * half:(c + 1) * half] = (
                        blocks[c][(a - shard * j) // 2:(b - shard * j) // 2, :])
    pad_lo = (FB_LO + FOX_HEADS) // 2
    words_ref[pad_lo:, :] = jnp.zeros((PROJ_PAD // 2 - pad_lo, D_MODEL), jnp.uint32)
    wp_ref[...] = pltpu.bitcast(words_ref[...], BF16)


def _fwd_in(x, norm_g, halves, bf_pad, fq_g, fk_g):
    s = x.shape[0]
    t = TILE
    n = s // t

    def body(x_ref, ng_ref, halves_ref, bf_ref, qg_ref, kg_ref,
             h_ref, pa_ref, qk_ref, qa_ref, ka_ref, va_ref, gb_ref, pm_ref, fb_ref, wp_ref,
             carry_ref, fcol_ref, words_ref):
        @pl.when(pl.program_id(0) == 0)
        def _():
            carry_ref[...] = jnp.zeros_like(carry_ref)
            _assemble_w_in(halves_ref, words_ref, wp_ref)

        xv = x_ref[...]
        rr = lax.rsqrt(jnp.mean(xv * xv, axis=-1, keepdims=True) + EPS)
        hb = ((xv * rr) * ng_ref[...]).astype(BF16)
        h_ref[...] = hb

        def proj(lo, hi):
            return _dot(hb, wp_ref[lo:hi, :], NT)

        fb = proj(FB_LO, PROJ_PAD)
        fb_ref[...] = fb
        qk_ref[:, 0:FOX_WIDTH] = proj(QB_LO, KB_LO)

        lane = lax.broadcasted_iota(jnp.int32, (t, LANES), 1)
        row = lax.broadcasted_iota(jnp.int32, (t, LANES), 0)
        lo = lane < HEAD_DIM
        z = fb + bf_ref[...]
        lf = -(jnp.maximum(-z, 0.0) + jnp.log1p(jnp.exp(-jnp.abs(z))))
        lf = jnp.where(lane < FOX_HEADS, lf, 0.0)
        sh = 1
        while sh < t:
            lf = lf + jnp.where(row >= sh, pltpu.roll(lf, sh, axis=0), 0.0)
            sh *= 2
        fcum = lf + carry_ref[...]
        fcol_ref[...] = fcum
        carry_ref[...] = fcol_ref[t - 1:t, :]

        ones3 = _ones3(lane)
        minus_f = _spread3(-fcum)

        def head_blocks(seg, g_ref, out_ref, scale):
            for p in range(FOX_WIDTH // LANES):
                sl = slice(p * LANES, (p + 1) * LANES)
                blk = qk_ref[:, seg - QB_LO + p * LANES:seg - QB_LO + (p + 1) * LANES]
                normed = ((blk * _head_rms(blk, lo)) * g_ref[:, sl]) * scale
                for hh in range(2):
                    h = 2 * p + hh
                    if seg == QB_LO:
                        extras = jnp.where(lane == QUERY_SUM_LANE + h, 1.0, ones3)
                    else:
                        extras = jnp.where(lane == KEY_SUM_LANE + h, 1.0, minus_f[:, h * LANES:(h + 1) * LANES])
                    out_ref[:, h * LANES:(h + 1) * LANES] = _head_block(normed, hh, lo, extras)

        qk_ref[:, FOX_WIDTH:2 * FOX_WIDTH] = proj(KB_LO, VB_LO)
        pa_ref[...] = proj(PA_LO, QB_LO)
        head_blocks(QB_LO, qg_ref, qa_ref, ATT_SCALE)
        vraw = proj(VB_LO, GB_LO)
        gb_ref[...] = proj(GB_LO, PM_LO)
        head_blocks(KB_LO, kg_ref, ka_ref, 1.0)
        pm_ref[...] = proj(PM_LO, FB_LO)
        for h in range(FOX_HEADS):
            va_ref[:, h * LANES:(h + 1) * LANES] = _head_block(vraw[:, (h // 2) * LANES:(h // 2 + 1) * LANES], h % 2, lo, ones3)

    outs = (
        jax.ShapeDtypeStruct((s, D_MODEL), BF16),
        jax.ShapeDtypeStruct((s, 512), F32),
        jax.ShapeDtypeStruct((s, 2 * FOX_WIDTH), F32),
        jax.ShapeDtypeStruct((s, HEAD_BLOCKS), BF16),
        jax.ShapeDtypeStruct((s, HEAD_BLOCKS), BF16),
        jax.ShapeDtypeStruct((s, HEAD_BLOCKS), BF16),
        jax.ShapeDtypeStruct((s, FOX_WIDTH), F32),
        jax.ShapeDtypeStruct((s, 512), F32),
        jax.ShapeDtypeStruct((s, LANES), F32),
        jax.ShapeDtypeStruct((PROJ_PAD, D_MODEL), BF16),
    )

    def resident(shape):
        return pl.BlockSpec(shape, lambda i: (0,) * len(shape), pipeline_mode=pl.Buffered(1))

    *fwd, wp = pl.pallas_call(
        body, name="fwd_in", grid=(n,), out_shape=outs,
        in_specs=[_rows(t, D_MODEL), _full((1, D_MODEL)), resident(halves.shape), _full((1, LANES)),
                  _full((1, FOX_WIDTH)), _full((1, FOX_WIDTH))],
        out_specs=(_rows(t, D_MODEL), _rows(t, 512), _rows(t, 2 * FOX_WIDTH), _rows(t, HEAD_BLOCKS),
                   _rows(t, HEAD_BLOCKS), _rows(t, HEAD_BLOCKS), _rows(t, FOX_WIDTH), _rows(t, 512),
                   _rows(t, LANES), resident((PROJ_PAD, D_MODEL))),
        scratch_shapes=[pltpu.VMEM((1, LANES), F32), pltpu.VMEM((t, LANES), F32),
                        pltpu.VMEM((PROJ_PAD // 2, D_MODEL), jnp.uint32)],
        compiler_params=_params(),
    )(x, norm_g, halves, bf_pad, fq_g, fk_g)
    return tuple(fwd), wp


POOL_HALO = 16


def _pool_window(lane):
    return jnp.where(lane < 64, 2.0, jnp.where(lane < 128, 4.0, jnp.where(lane < 192, 8.0, 16.0)))


def _pool_pick(lane, s2, s4, s8, s16):
    return jnp.where(lane < 64, s2, jnp.where(lane < 128, s4, jnp.where(lane < 192, s8, s16)))


def _group_onehot(shape, row_is_group_lane):
    r = lax.broadcasted_iota(jnp.int32, shape, 0)
    c = lax.broadcasted_iota(jnp.int32, shape, 1)
    hit = (r % HEAD_DIM == c) if row_is_group_lane else (c % HEAD_DIM == r)
    return jnp.where(hit, 1.0, 0.0).astype(F32)


def _same_group(shape):
    r = lax.broadcasted_iota(jnp.int32, shape, 0)
    c = lax.broadcasted_iota(jnp.int32, shape, 1)
    return (r // HEAD_DIM) == (c // HEAD_DIM)


def _pool_block_diag(w4):
    spread = jnp.dot(w4, _group_onehot((HEAD_DIM, POOL_WIDTH), False), preferred_element_type=F32,
                     precision=lax.Precision.HIGHEST)
    return jnp.where(_same_group((POOL_WIDTH, POOL_WIDTH)), spread, 0.0).astype(BF16)


def _mem_softmax(qm, kp):
    sc = _dot(qm, kp, NT)
    e = jnp.exp(sc - jnp.max(sc, axis=-1, keepdims=True))
    return e * (1.0 / jnp.sum(e, axis=-1, keepdims=True))


def _side_fwd(pa, pm, w4, pscale, mq_g, mem, mem_norm_g, w_kv, mk_g):
    s = pa.shape[0]
    t = TILE
    n = s // t
    ext = t + POOL_HALO
    nm = mem.shape[0]

    def body(pa_ref, pm_ref, w4_ref, sc_ref, g_ref, mem_ref, mg_ref, wkv_ref, kg_ref,
             ma_ref, d_ref, mm_ref, mn_ref, kv_ref, k_ref, v_ref, ext_ref, w_ref):
        i = pl.program_id(0)

        @pl.when(i == 0)
        def _():
            ext_ref[0:POOL_HALO, :] = jnp.zeros((POOL_HALO, POOL_WIDTH), F32)
            w_ref[...] = _pool_block_diag(w4_ref[...])
            _mem_tokens_fwd(mem_ref, mg_ref, wkv_ref, kg_ref, mn_ref, kv_ref, k_ref, v_ref)

        u = pa_ref[:, 0:POOL_WIDTH]
        ext_ref[POOL_HALO:ext, :] = u
        e = ext_ref[...]
        s2 = e + pltpu.roll(e, 1, axis=0)
        s4 = s2 + pltpu.roll(s2, 2, axis=0)
        s8 = s4 + pltpu.roll(s4, 4, axis=0)
        s16 = s8 + pltpu.roll(s8, 8, axis=0)
        lane_e = lax.broadcasted_iota(jnp.int32, (ext, POOL_WIDTH), 1)
        win = _pool_pick(lane_e, s2, s4, s8, s16)[POOL_HALO:ext, :]
        lane = lax.broadcasted_iota(jnp.int32, (t, POOL_WIDTH), 1)
        pos = (lax.broadcasted_iota(jnp.int32, (t, POOL_WIDTH), 0) + (i * t + 1)).astype(F32)
        d = win / jnp.minimum(pos, _pool_window(lane)) - u
        db = d.astype(BF16)
        d_ref[...] = db
        ya = _dot(db, w_ref[...]) * sc_ref[...]
        ga = pa_ref[:, POOL_WIDTH:2 * POOL_WIDTH]
        ma_ref[...] = (ya * (ga * _sig(ga))).astype(BF16)
        ext_ref[0:POOL_HALO, :] = ext_ref[t:ext, :]

        lo = _lane_lo((t, LANES))
        for p in range(MEM_WIDTH // LANES):
            sl = slice(p * LANES, (p + 1) * LANES)
            qb = pm_ref[:, sl]
            qs = (((qb * _head_rms(qb, lo)) * g_ref[:, sl]) * ATT_SCALE).astype(BF16)
            kp = k_ref[:, sl]
            vp = v_ref[:, sl]
            outs = []
            for hh in range(2):
                msk = lo if hh == 0 else jnp.logical_not(lo)
                prob = _mem_softmax(jnp.where(msk, qs, jnp.zeros_like(qs)), kp)
                outs.append(_dot(prob.astype(BF16), vp))
            o = jnp.where(lo, outs[0], outs[1])
            gm = pm_ref[:, MEM_WIDTH + p * LANES:MEM_WIDTH + (p + 1) * LANES]
            mm_ref[:, sl] = (o * (gm * _sig(gm))).astype(BF16)

    return pl.pallas_call(
        body, name="side_fwd", grid=(n,),
        out_shape=(jax.ShapeDtypeStruct((s, POOL_WIDTH), BF16), jax.ShapeDtypeStruct((s, POOL_WIDTH), BF16),
                   jax.ShapeDtypeStruct((s, MEM_WIDTH), BF16), jax.ShapeDtypeStruct((nm, D_MODEL), BF16),
                   jax.ShapeDtypeStruct((nm, 2 * MEM_WIDTH), F32), jax.ShapeDtypeStruct((nm, MEM_WIDTH), BF16),
                   jax.ShapeDtypeStruct((nm, MEM_WIDTH), BF16)),
        in_specs=[_rows(t, 512), _rows(t, 512), _full((POOL_ROWS, HEAD_DIM)), _full((1, POOL_WIDTH)),
                  _full((1, MEM_WIDTH)), _full((nm, D_MODEL)), _full((1, D_MODEL)), _full((D_MODEL, 2 * MEM_WIDTH)),
                  _full((1, MEM_WIDTH))],
        out_specs=(_rows(t, POOL_WIDTH), _rows(t, POOL_WIDTH), _rows(t, MEM_WIDTH), _full((nm, D_MODEL)),
                   _full((nm, 2 * MEM_WIDTH)), _full((nm, MEM_WIDTH)), _full((nm, MEM_WIDTH))),
        scratch_shapes=[pltpu.VMEM((ext, POOL_WIDTH), F32), pltpu.VMEM((POOL_WIDTH, POOL_WIDTH), BF16)],
        compiler_params=_params(),
    )(pa, pm, w4, pscale, mq_g, mem, mem_norm_g, w_kv, mk_g)


FOX_FWD_HEADS = 4


def _fox_fwd(qa, ka, va, gb):
    s = qa.shape[0]
    t = TILE
    n = s // t
    heads = FOX_FWD_HEADS
    pairs = heads // 2
    group_w = heads * LANES

    def body(qa_ref, ka_ref, va_ref, gb_ref, o_ref, mb_ref, r_ref):
        i = pl.program_id(1)
        lane = lax.broadcasted_iota(jnp.int32, (t, LANES), 1)
        lo = lane < HEAD_DIM
        causal = lax.broadcasted_iota(jnp.int32, (t, t), 1) <= lax.broadcasted_iota(jnp.int32, (t, t), 0)
        qas = [qa_ref[:, hh * LANES:(hh + 1) * LANES] for hh in range(heads)]

        def step(j, carry, masked):
            rows = pl.ds(pl.multiple_of(j * t, t), t)
            def logits(hh):
                sc = _dot(qas[hh], ka_ref[rows, hh * LANES:(hh + 1) * LANES], NT)
                return jnp.where(causal, sc, -1e30) if masked else sc

            def advance(hh, sc):
                m, acc = carry[hh]
                m_new = jnp.maximum(m, jnp.max(sc, axis=-1, keepdims=True))
                p = jnp.exp(sc - m_new).astype(BF16)
                return m_new, jnp.exp(m - m_new) * acc + _dot(p, va_ref[rows, hh * LANES:(hh + 1) * LANES])

            new = []
            sc = logits(0)
            for hh in range(heads):
                sc_next = logits(hh + 1) if hh + 1 < heads else None
                new.append(advance(hh, sc))
                sc = sc_next
            return tuple(new)

        init = (jnp.full((t, 1), -1e30, F32), jnp.zeros((t, LANES), F32))
        carry = lax.fori_loop(0, i, functools.partial(step, masked=False), (init,) * heads)
        res = step(i, carry, masked=True)
        for p in range(pairs):
            outs = []
            rcol = jnp.zeros((t, LANES), F32)
            for hh in range(2):
                m, acc = res[2 * p + hh]
                l = _lane_pick(acc, lane, AUG_LO)
                outs.append(acc * (1.0 / l))
                rcol = jnp.where(lane == hh, m + jnp.log(l), rcol)
            o = _pair_block(outs[0], outs[1], lo)
            sl = slice(p * LANES, (p + 1) * LANES)
            o_ref[:, sl] = o
            g = gb_ref[:, sl]
            mb_ref[:, sl] = (o * (g * _sig(g))).astype(BF16)
            r_ref[p] = rcol

    tile_spec = pl.BlockSpec((t, pairs * LANES), lambda p, i: (i, p))
    full_spec = pl.BlockSpec((s, group_w), lambda p, i: (0, p))
    return pl.pallas_call(
        body, name="fox_fwd", grid=(FOX_HEADS // heads, n),
        out_shape=(jax.ShapeDtypeStruct((s, FOX_WIDTH), F32), jax.ShapeDtypeStruct((s, FOX_WIDTH), BF16),
                   jax.ShapeDtypeStruct((FOX_HEADS // 2, s, LANES), F32)),
        in_specs=[pl.BlockSpec((t, group_w), lambda p, i: (i, p)), full_spec, full_spec, tile_spec],
        out_specs=(tile_spec, tile_spec, pl.BlockSpec((pairs, t, LANES), lambda p, i: (p, i, 0))),
        compiler_params=_params(2),
    )(qa, ka, va, gb)


def _out_loss(x, tgt, ma, mb, mm, wout, gb, o, r4):
    s = x.shape[0]
    t = TILE
    n = s // t
    pairs = FOX_HEADS // 2

    def body(x_ref, t_ref, ma_ref, mb_ref, mm_ref, w_ref, gb_ref, o_ref, r_ref,
             dy_ref, dma_ref, dmm_ref, dw_ref, loss_ref, doa_ref, dgb_ref, rr_ref, mix_ref):
        @pl.when(pl.program_id(0) == 0)
        def _():
            dw_ref[...] = jnp.zeros_like(dw_ref)
            loss_ref[...] = jnp.zeros_like(loss_ref)

        mix_ref[:, 0:256] = ma_ref[...]
        mix_ref[:, 256:768] = mb_ref[...]
        mix_ref[:, 768:1024] = mm_ref[...]
        mix = mix_ref[...]
        err = (x_ref[...] + _dot(mix, w_ref[...])) - t_ref[...]
        row_mean = jnp.sum(err * err, axis=-1, keepdims=True) * (1.0 / D_MODEL)
        loss_ref[...] += 0.5 * jnp.sum(row_mean, axis=0, keepdims=True)
        dy = err * (1.0 / D_MODEL)
        dy_ref[...] = dy
        dyb = dy.astype(BF16)
        dmix = _dot(dyb, w_ref[...], NT)
        dma_ref[...] = dmix[:, 0:256]
        dmm_ref[...] = dmix[:, 768:1024]
        dw_ref[...] += _dot(mix, dyb, TN)

        lane = lax.broadcasted_iota(jnp.int32, (t, LANES), 1)
        lo = lane < HEAD_DIM
        d_os = []
        delta = jnp.zeros((t, LANES), F32)
        for p in range(pairs):
            sl = slice(p * LANES, (p + 1) * LANES)
            g = gb_ref[:, sl]
            sg = _sig(g)
            dm = dmix[:, 256 + p * LANES:256 + (p + 1) * LANES]
            ov = o_ref[:, sl]
            d_o = dm * (g * sg)
            d_os.append(d_o)
            dgb_ref[:, sl] = (dm * ov * (sg * (1.0 + g * (1.0 - sg)))).astype(BF16)
            prod = d_o * ov
            delta = jnp.where(lane == 2 * p, jnp.sum(jnp.where(lo, prod, 0.0), axis=-1, keepdims=True), delta)
            delta = jnp.where(lane == 2 * p + 1, jnp.sum(jnp.where(lo, 0.0, prod), axis=-1, keepdims=True), delta)
            rr_ref[p, 0] = r_ref[p].T[0:8, :]
        minus_delta = _spread3(-delta)
        for h in range(FOX_HEADS):
            blk = slice(h * LANES, (h + 1) * LANES)
            doa_ref[:, blk] = _head_block(d_os[h // 2], h % 2, lo, minus_delta[:, blk])

    return pl.pallas_call(
        body, name="out_loss", grid=(n,),
        out_shape=(jax.ShapeDtypeStruct((s, D_MODEL), F32), jax.ShapeDtypeStruct((s, 256), F32),
                   jax.ShapeDtypeStruct((s, 256), F32), jax.ShapeDtypeStruct((D_MODEL, D_MODEL), F32),
                   jax.ShapeDtypeStruct((1, LANES), F32), jax.ShapeDtypeStruct((s, HEAD_BLOCKS), BF16),
                   jax.ShapeDtypeStruct((s, FOX_WIDTH), BF16), jax.ShapeDtypeStruct((pairs, n, 8, t), F32)),
        in_specs=[_rows(t, D_MODEL), _rows(t, D_MODEL), _rows(t, 256), _rows(t, 512), _rows(t, 256),
                  _full((D_MODEL, D_MODEL)), _rows(t, FOX_WIDTH), _rows(t, FOX_WIDTH),
                  pl.BlockSpec((pairs, t, LANES), lambda i: (0, i, 0))],
        out_specs=(_rows(t, D_MODEL), _rows(t, 256), _rows(t, 256), _full((D_MODEL, D_MODEL)), _full((1, LANES)),
                   _rows(t, HEAD_BLOCKS), _rows(t, FOX_WIDTH), pl.BlockSpec((pairs, 1, 8, t), lambda i: (0, i, 0, 0))),
        scratch_shapes=[pltpu.VMEM((t, D_MODEL), BF16)],
        compiler_params=_params(),
    )(x, tgt, ma, mb, mm, wout, gb, o, r4)


def _side_bwd(pa, db, dma, w4, pscale, pm, dmm, kmn, vmb, mq_g, kv, mnb, mem, w_kv, mk_g, mem_norm_g):
    s = pa.shape[0]
    t = TILE
    n = s // t
    ext = t + POOL_HALO
    nm = mem.shape[0]

    def body(pa_ref, d_ref, dma_ref, w4_ref, sc_ref, pm_ref, dmm_ref, k_ref, v_ref, g_ref,
             kv_ref, mn_ref, mem_ref, wkv_ref, kg_ref, mg_ref,
             dpa_ref, dpm_ref, dw4_ref, dsc_ref, dg_ref, dwkv_ref, dmg_ref, dkg_ref,
             ext_ref, w_ref, dw_ref, dk_ref, dv_ref, gacc_ref, dkv_ref):
        i = pl.program_id(0)

        @pl.when(i == 0)
        def _():
            dw_ref[...] = jnp.zeros_like(dw_ref)
            dsc_ref[...] = jnp.zeros_like(dsc_ref)
            ext_ref[t:ext, :] = jnp.zeros((POOL_HALO, POOL_WIDTH), F32)
            w_ref[...] = _pool_block_diag(w4_ref[...])
            dk_ref[...] = jnp.zeros_like(dk_ref)
            dv_ref[...] = jnp.zeros_like(dv_ref)
            gacc_ref[...] = jnp.zeros_like(gacc_ref)

        dbv = d_ref[...]
        z = _dot(dbv, w_ref[...])
        ga = pa_ref[:, POOL_WIDTH:2 * POOL_WIDTH]
        sg = _sig(ga)
        dma_v = dma_ref[...]
        dya = dma_v * (ga * sg)
        dpa_ref[:, POOL_WIDTH:2 * POOL_WIDTH] = (dma_v * (z * sc_ref[...]) * (sg * (1.0 + ga * (1.0 - sg)))).astype(BF16)
        dsc_ref[...] += jnp.sum(dya * z, axis=0, keepdims=True)
        dzb = (dya * sc_ref[...]).astype(BF16)
        dw_ref[...] += _dot(dbv, dzb, TN)
        dd = _dot(dzb, w_ref[...], NT)
        lane = lax.broadcasted_iota(jnp.int32, (t, POOL_WIDTH), 1)
        pos = (lax.broadcasted_iota(jnp.int32, (t, POOL_WIDTH), 0) + ((n - 1 - i) * t + 1)).astype(F32)
        ext_ref[0:t, :] = dd / jnp.minimum(pos, _pool_window(lane))
        e = ext_ref[...]
        s2 = e + pltpu.roll(e, ext - 1, axis=0)
        s4 = s2 + pltpu.roll(s2, ext - 2, axis=0)
        s8 = s4 + pltpu.roll(s4, ext - 4, axis=0)
        s16 = s8 + pltpu.roll(s8, ext - 8, axis=0)
        lane_e = lax.broadcasted_iota(jnp.int32, (ext, POOL_WIDTH), 1)
        win = _pool_pick(lane_e, s2, s4, s8, s16)[0:t, :]
        dpa_ref[:, 0:POOL_WIDTH] = (win - dd).astype(BF16)
        ext_ref[t:ext, :] = ext_ref[0:POOL_HALO, :]

        lo = _lane_lo((t, LANES))
        for p in range(MEM_WIDTH // LANES):
            sl = slice(p * LANES, (p + 1) * LANES)
            qb = pm_ref[:, sl]
            rr = _head_rms(qb, lo)
            qhat = qb * rr
            g = g_ref[:, sl]
            qs = ((qhat * g) * ATT_SCALE).astype(BF16)
            gm = pm_ref[:, MEM_WIDTH + p * LANES:MEM_WIDTH + (p + 1) * LANES]
            sg = _sig(gm)
            dmo = dmm_ref[:, sl]
            d_o = dmo * (gm * sg)
            kp = k_ref[:, sl]
            vp = v_ref[:, sl]
            outs, dqs = [], []
            for hh in range(2):
                msk = lo if hh == 0 else jnp.logical_not(lo)
                qm = jnp.where(msk, qs, jnp.zeros_like(qs))
                prob = _mem_softmax(qm, kp)
                pb = prob.astype(BF16)
                outs.append(_dot(pb, vp))
                dom = jnp.where(msk, d_o, 0.0).astype(BF16)
                dp = _dot(dom, vp, NT)
                ds = (prob * (dp - jnp.sum(prob * dp, axis=-1, keepdims=True))).astype(BF16)
                dqs.append(_dot(ds, kp))
                dk_ref[:, sl] += _dot(ds, qm, TN)
                dv_ref[:, sl] += _dot(pb, dom, TN)
            o = jnp.where(lo, outs[0], outs[1])
            dqn = jnp.where(lo, dqs[0], dqs[1]) * ATT_SCALE
            dpm_ref[:, sl] = _head_norm_bwd(dqn, qhat, rr, g, lo).astype(BF16)
            dpm_ref[:, MEM_WIDTH + p * LANES:MEM_WIDTH + (p + 1) * LANES] = (
                dmo * o * (sg * (1.0 + gm * (1.0 - sg)))).astype(BF16)
            gacc_ref[:, sl] += jnp.sum(dqn * qhat, axis=0, keepdims=True)

        @pl.when(i == n - 1)
        def _():
            own = jnp.where(_same_group((POOL_WIDTH, POOL_WIDTH)), dw_ref[...], 0.0)
            dw4_ref[...] = jnp.dot(own, _group_onehot((POOL_WIDTH, HEAD_DIM), True), preferred_element_type=F32,
                                   precision=lax.Precision.HIGHEST)
            dg_ref[...] = _fold_heads(gacc_ref[...])

            lo_m = _lane_lo((nm, LANES))
            kacc = []
            for p in range(MEM_WIDTH // LANES):
                sl = slice(p * LANES, (p + 1) * LANES)
                kb = kv_ref[:, sl]
                rr = _head_rms(kb, lo_m)
                khat = kb * rr
                dk = dk_ref[:, sl]
                dkv_ref[:, sl] = _head_norm_bwd(dk, khat, rr, kg_ref[:, sl], lo_m).astype(BF16)
                kacc.append(jnp.sum(dk * khat, axis=0, keepdims=True))
            dkg_ref[...] = _fold_heads(jnp.concatenate(kacc, axis=1))
            dkv_ref[:, MEM_WIDTH:] = dv_ref[...].astype(BF16)
            dkv = dkv_ref[...]
            dwkv_ref[...] = _dot(mn_ref[...], dkv, TN)
            dmn = _dot(dkv, wkv_ref[...], NT)
            xm = mem_ref[...]
            rr = lax.rsqrt(jnp.mean(xm * xm, axis=-1, keepdims=True) + EPS)
            dmg_ref[...] = jnp.sum(dmn * (xm * rr), axis=0, keepdims=True)

    def rev(w):
        return _rows_rev(t, w, n)

    row = jax.ShapeDtypeStruct((1, LANES), F32)
    return pl.pallas_call(
        body, name="side_bwd", grid=(n,),
        out_shape=(jax.ShapeDtypeStruct((s, 512), BF16), jax.ShapeDtypeStruct((s, 512), BF16),
                   jax.ShapeDtypeStruct((POOL_ROWS, HEAD_DIM), F32), jax.ShapeDtypeStruct((1, POOL_WIDTH), F32), row,
                   jax.ShapeDtypeStruct((D_MODEL, 2 * MEM_WIDTH), F32), jax.ShapeDtypeStruct((1, D_MODEL), F32), row),
        in_specs=[rev(512), rev(POOL_WIDTH), rev(POOL_WIDTH), _full((POOL_ROWS, HEAD_DIM)), _full((1, POOL_WIDTH)),
                  rev(512), rev(MEM_WIDTH), _full((N_MEM, MEM_WIDTH)), _full((N_MEM, MEM_WIDTH)), _full((1, MEM_WIDTH)),
                  _full((nm, 2 * MEM_WIDTH)), _full((nm, D_MODEL)), _full((nm, D_MODEL)),
                  _full((D_MODEL, 2 * MEM_WIDTH)), _full((1, MEM_WIDTH)), _full((1, D_MODEL))],
        out_specs=(rev(512), rev(512), _full((POOL_ROWS, HEAD_DIM)), _full((1, POOL_WIDTH)), _full((1, LANES)),
                   _full((D_MODEL, 2 * MEM_WIDTH)), _full((1, D_MODEL)), _full((1, LANES))),
        scratch_shapes=[pltpu.VMEM((ext, POOL_WIDTH), F32), pltpu.VMEM((POOL_WIDTH, POOL_WIDTH), BF16),
                        pltpu.VMEM((POOL_WIDTH, POOL_WIDTH), F32), pltpu.VMEM((N_MEM, MEM_WIDTH), F32),
                        pltpu.VMEM((N_MEM, MEM_WIDTH), F32), pltpu.VMEM((1, MEM_WIDTH), F32),
                        pltpu.VMEM((nm, 2 * MEM_WIDTH), BF16)],
        compiler_params=_params(),
    )(pa, db, dma, w4, pscale, pm, dmm, kmn, vmb, mq_g, kv, mnb, mem, w_kv, mk_g, mem_norm_g)


FOX_BWD_HEADS = 4


def _fox_bwd(ka, va, qa, doa, rr, gparts, axes):
    s = ka.shape[0]
    t = TILE
    n = s // t
    heads = FOX_BWD_HEADS
    groups = FOX_HEADS // heads
    group_w = heads * LANES
    na = len(gparts)

    def body(*refs):
        ka_ref, va_ref, qa_ref, doa_ref, rr_ref = refs[0:5]
        g_refs = refs[5:5 + na]
        dka_ref, dva_ref, dqa_ref = refs[5 + na:8 + na]
        out_refs = refs[8 + na:8 + 2 * na]
        bufs = tuple(refs[8 + (2 + k) * na:8 + (3 + k) * na] for k in range(5))
        j = pl.program_id(1)
        step_id = pl.program_id(0) * n + j
        red = _ShardReduce(g_refs, out_refs, axes, bufs, *refs[8 + 7 * na:]) if na else None

        @pl.when(j == 0)
        def _():
            dqa_ref[...] = jnp.zeros_like(dqa_ref)

        if red is not None:
            pl.when(step_id == 0)(red.exchange_with_sibling)

            @pl.when(step_id == 1)
            def _():
                for k in (1, 2, 3):
                    red.send_to_chip(k)
                red.keep_mine()

        causal = lax.broadcasted_iota(jnp.int32, (t, t), 0) <= lax.broadcasted_iota(jnp.int32, (t, t), 1)
        kas = [ka_ref[:, hh * LANES:(hh + 1) * LANES] for hh in range(heads)]
        vas = [va_ref[:, hh * LANES:(hh + 1) * LANES] for hh in range(heads)]

        def step(i, carry, masked):
            rows = pl.ds(pl.multiple_of(i * t, t), t)
            new = []
            for hh in range(heads):
                cols = slice(hh * LANES, (hh + 1) * LANES)
                dk_a, dv_a = carry[hh]
                qb = qa_ref[rows, cols]
                d_o = doa_ref[rows, cols]
                arg = _dot(kas[hh], qb, NT) - rr_ref[hh // 2, i, hh % 2:hh % 2 + 1, :]
                if masked:
                    arg = jnp.where(causal, arg, -1e30)
                pt = jnp.exp(arg)
                dst = (pt * _dot(vas[hh], d_o, NT)).astype(BF16)
                dv_a = dv_a + _dot(pt.astype(BF16), d_o)
                dk_a = dk_a + _dot(dst, qb)
                dqa_ref[rows, cols] += _dot(dst, kas[hh], TN)
                new.append((dk_a, dv_a))
            return tuple(new)

        zero = jnp.zeros((t, LANES), F32)
        carry = step(j, ((zero, zero),) * heads, masked=True)
        res = lax.fori_loop(j + 1, n, functools.partial(step, masked=False), carry)
        for hh in range(heads):
            cols = slice(hh * LANES, (hh + 1) * LANES)
            dka_ref[:, cols] = res[hh][0]
            dva_ref[:, cols] = res[hh][1]

        if red is not None:
            @pl.when(step_id == groups * n - 1)
            def _():
                red.sum_and_share()
                red.finish()

    tile_spec = pl.BlockSpec((t, group_w), lambda p, j: (j, p))
    full_spec = pl.BlockSpec((s, group_w), lambda p, j: (0, p))
    any_spec = pl.BlockSpec(memory_space=pl.ANY)
    scratch = _ShardReduce.scratch(gparts, axes)
    if na:
        scratch += [pltpu.SemaphoreType.DMA((_ShardReduce.SEMS * na,)), pltpu.SemaphoreType.DMA((_ShardReduce.SEMS * na,)),
                    pltpu.SemaphoreType.DMA((_ShardReduce.LOCAL * na,))]
    return pl.pallas_call(
        body, name="fox_bwd", grid=(groups, n),
        out_shape=(jax.ShapeDtypeStruct((s, HEAD_BLOCKS), F32),) * 3
        + tuple(jax.ShapeDtypeStruct(_shard_shape(g), F32) for g in gparts),
        in_specs=[tile_spec, tile_spec, full_spec, full_spec,
                  pl.BlockSpec((heads // 2, n, 8, t), lambda p, j: (p, 0, 0, 0))] + [any_spec] * na,
        out_specs=(tile_spec, tile_spec, full_spec) + (any_spec,) * na,
        scratch_shapes=scratch, compiler_params=_params(2, VMEM_LIMIT_FOX_BWD),
    )(ka, va, qa, doa, rr, *gparts)


def _fox_post_tile(i, n, t, dqa_ref, dka_ref, dva_ref, qk_ref, fb_ref, bf_ref, qg_ref, kg_ref,
                   dqk_ref, dv_ref, dfb_ref, dqg_ref, dkg_ref, dbf_ref, qacc_ref, kacc_ref, carry_ref,
                   between):
    @pl.when(i == 0)
    def _():
        qacc_ref[...] = jnp.zeros_like(qacc_ref)
        kacc_ref[...] = jnp.zeros_like(kacc_ref)
        dbf_ref[...] = jnp.zeros_like(dbf_ref)
        carry_ref[...] = jnp.zeros_like(carry_ref)

    lane = lax.broadcasted_iota(jnp.int32, (t, LANES), 1)
    row = lax.broadcasted_iota(jnp.int32, (t, LANES), 0)
    lo = lane < HEAD_DIM

    def head_blocks(ref, p):
        return ref[:, 2 * p * LANES:(2 * p + 1) * LANES], ref[:, (2 * p + 1) * LANES:(2 * p + 2) * LANES]

    def issue(k):
        if between[k] is not None:
            between[k]()

    sums = []
    pairs = FOX_WIDTH // LANES
    for side, (src_ref, g_ref, acc_ref, scale) in enumerate(((dqa_ref, qg_ref, qacc_ref, ATT_SCALE),
                                                             (dka_ref, kg_ref, kacc_ref, 1.0))):
        total = jnp.zeros((t, LANES), F32)
        for p in range(pairs):
            issue(side * pairs + p)
            sl = slice(p * LANES, (p + 1) * LANES)
            cols = slice(side * FOX_WIDTH + p * LANES, side * FOX_WIDTH + (p + 1) * LANES)
            if side == 0:
                dv_ref[:, sl] = _pair_block(*head_blocks(dva_ref, p), lo).astype(BF16)
            d0, d1 = head_blocks(src_ref, p)
            total = total + (d0 + d1)
            raw = qk_ref[:, cols]
            rr = _head_rms(raw, lo)
            xhat = raw * rr
            dn = _pair_block(d0, d1, lo) * scale
            dqk_ref[:, cols] = _head_norm_bwd(dn, xhat, rr, g_ref[:, sl], lo).astype(BF16)
            acc_ref[:, sl] += jnp.sum(dn * xhat, axis=0, keepdims=True)
        sums.append(total)
    issue(2 * pairs)
    dq_sum, dk_sum = sums

    acc = (pltpu.roll(dq_sum, LANES - KEY_SUM_LANE, axis=1) - pltpu.roll(dk_sum, LANES - QUERY_SUM_LANE, axis=1))
    acc = jnp.where(lane < FOX_HEADS, acc, 0.0)
    sh = 1
    while sh < t:
        acc = acc + jnp.where(row < t - sh, pltpu.roll(acc, t - sh, axis=0), 0.0)
        sh *= 2
    dlogf = acc + carry_ref[...]
    dfb_ref[...] = dlogf
    carry_ref[...] = dfb_ref[0:1, :]
    z = fb_ref[...] + bf_ref[...]
    dz = jnp.where(lane < FOX_HEADS, dlogf * (1.0 / (1.0 + jnp.exp(z))), 0.0)
    dfb_ref[...] = dz
    dbf_ref[...] += jnp.sum(dz, axis=0, keepdims=True)

    @pl.when(i == n - 1)
    def _():
        dqg_ref[...] = _fold_heads(qacc_ref[...])
        dkg_ref[...] = _fold_heads(kacc_ref[...])


def _assemble_dproj(dp_ref, dpa_ref, dqk_ref, dv_ref, dgb_ref, dpm_ref, dfb_ref):
    dp_ref[:, PA_LO:QB_LO] = dpa_ref[...]
    dp_ref[:, QB_LO:VB_LO] = dqk_ref[...]
    dp_ref[:, VB_LO:GB_LO] = dv_ref[...]
    dp_ref[:, GB_LO:PM_LO] = dgb_ref[...]
    dp_ref[:, PM_LO:FB_LO] = dpm_ref[...]
    dp_ref[:, FB_LO:PROJ_PAD] = dfb_ref[...].astype(BF16)


def _dproj_specs(t):
    return [_rows(t, 512), _rows(t, 2 * FOX_WIDTH), _rows(t, FOX_WIDTH), _rows(t, FOX_WIDTH), _rows(t, 512),
            _rows(t, LANES)]


IN_BWD_X_TILE = 256


def _in_bwd_x(x, dy, norm_g, wp, dparts, gparts, axes, smalls):
    s = x.shape[0]
    t = IN_BWD_X_TILE
    n = s // t
    na = len(gparts)
    n_dp = len(dparts)
    vec_leaves, loss_row, dw4 = smalls if smalls is not None else ((), None, None)
    nv = len(vec_leaves)
    n_small = nv + 2 if smalls is not None else 0
    small_base = _ShardReduce.SEMS * na

    def body(*refs):
        x_ref, dy_ref, g_ref, wp_ref = refs[0:4]
        dp_parts = refs[4:4 + n_dp]
        o = 4 + n_dp
        g_refs = refs[o:o + na]
        small_in = refs[o + na:o + na + n_small]
        o += na + n_small
        gx_ref, dg_ref = refs[o:o + 2]
        out_refs = refs[o + 2:o + 2 + na]
        small_out = refs[o + 2 + na:o + 2 + na + (2 if smalls is not None else 0)]
        o += 2 + na + len(small_out)
        dp_ref = refs[o]
        bufs = tuple(refs[o + 1 + k * na:o + 1 + (k + 1) * na] for k in range(5))
        rest = refs[o + 1 + 5 * na:]

        i = pl.program_id(0)
        if na or smalls is not None:
            send_sems, recv_sems, local_sems = rest[-3:]
        red = _ShardReduce(g_refs, out_refs, axes, bufs, send_sems, recv_sems, local_sems) if na else None

        @pl.when(i == 0)
        def _():
            dg_ref[...] = jnp.zeros_like(dg_ref)
            if red is not None:
                red.exchange_with_sibling()

        if red is not None:
            for k in (1, 2, 3):
                pl.when(i == k)(functools.partial(red.send_to_chip, k))
            pl.when(i == 4)(red.keep_mine)

        _assemble_dproj(dp_ref, *dp_parts)
        dh = _dot(dp_ref[...], wp_ref[...])
        xv = x_ref[...]
        rr = lax.rsqrt(jnp.mean(xv * xv, axis=-1, keepdims=True) + EPS)
        xhat = xv * rr
        scaled = dh * g_ref[...]
        gx_ref[...] = dy_ref[...] + rr * (scaled - xhat * jnp.mean(xhat * scaled, axis=-1, keepdims=True))
        dg_ref[...] += jnp.sum(dh * xhat, axis=0, keepdims=True)

        def small_all_reduce():
            leaf_refs, (loss_ref, dw4_ref) = small_in[0:nv], small_in[nv:]
            vec_out, dw4_out = small_out
            vec_mine, vec_recv, dw4_recv = rest[0:3]
            cx, cy, c = _my_place()
            me_lin = 4 * cx + 2 * cy + c

            def copy(k, src, dst, base):
                peer = (me_lin + k) % 8
                return pltpu.make_async_remote_copy(
                    src_ref=src, dst_ref=dst.at[me_lin], send_sem=send_sems.at[base + k - 1],
                    recv_sem=recv_sems.at[base + k - 1], device_id=(peer // 4, (peer // 2) % 2, peer % 2),
                    device_id_type=MESH)

            vec_mine[...] = jnp.zeros_like(vec_mine)
            vec_mine[0:1, :] = dg_ref[...]
            for (_, row, _), ref in zip(VEC_LEAVES[1:], leaf_refs):
                vec_mine[row:row + 1, 0:ref.shape[1]] = ref[...]
            vec_mine[VEC_LOSS_ROW:VEC_LOSS_ROW + 1, 0:LANES] = loss_ref[...]
            copies = [copy(k, src, dst, base) for k in range(1, 8)
                      for src, dst, base in ((vec_mine, vec_recv, small_base), (dw4_ref, dw4_recv, small_base + 7))]
            for cp in copies:
                cp.start()

            def complete():
                for cp in copies:
                    cp.wait_recv()
                vec_recv[me_lin] = vec_mine[...]
                dw4_recv[me_lin] = dw4_ref[...]
                vtot, wtot = vec_recv[0], dw4_recv[0]
                for d in range(1, 8):
                    vtot = vtot + vec_recv[d]
                    wtot = wtot + dw4_recv[d]
                vec_out[...] = vtot
                dw4_out[...] = wtot
                for cp in copies:
                    cp.wait_send()
            return complete

        @pl.when(i == n - 1)
        def _():
            complete_small = small_all_reduce() if smalls is not None else None
            if red is not None:
                red.sum_and_share()
            if complete_small is not None:
                complete_small()
            if red is not None:
                red.finish()

    any_spec = pl.BlockSpec(memory_space=pl.ANY)
    scratch = [pltpu.VMEM((t, PROJ_PAD), BF16)] + _ShardReduce.scratch(gparts, axes)
    out_shape = [jax.ShapeDtypeStruct((s, D_MODEL), F32), jax.ShapeDtypeStruct((1, D_MODEL), F32)]
    out_shape += [jax.ShapeDtypeStruct(_shard_shape(g), F32) for g in gparts]
    out_specs = [_rows(t, D_MODEL), _full((1, D_MODEL))] + [any_spec] * na
    small_args = []
    if smalls is not None:
        small_args = [*vec_leaves, loss_row, dw4]
        out_shape += [jax.ShapeDtypeStruct((VEC_ROWS, D_MODEL), F32), jax.ShapeDtypeStruct(dw4.shape, F32)]
        out_specs += [_full((VEC_ROWS, D_MODEL)), _full(dw4.shape)]
        scratch += [pltpu.VMEM((VEC_ROWS, D_MODEL), F32), pltpu.VMEM((8, VEC_ROWS, D_MODEL), F32),
                    pltpu.VMEM((8,) + dw4.shape, F32)]
    if na or smalls is not None:
        n_sems = small_base + 14
        scratch += [pltpu.SemaphoreType.DMA((n_sems,)), pltpu.SemaphoreType.DMA((n_sems,)),
                    pltpu.SemaphoreType.DMA((max(_ShardReduce.LOCAL * na, 1),))]
    return pl.pallas_call(
        body, name="in_bwd_x", grid=(n,), out_shape=tuple(out_shape),
        in_specs=[_rows(t, D_MODEL), _rows(t, D_MODEL), _full((1, D_MODEL)),
                  pl.BlockSpec((PROJ_PAD, D_MODEL), lambda i: (0, 0), pipeline_mode=pl.Buffered(1))]
        + _dproj_specs(t) + [any_spec] * na + [_full(a.shape) for a in small_args],
        out_specs=tuple(out_specs), scratch_shapes=scratch, compiler_params=_params(),
    )(x, dy, norm_g, wp, *dparts, *gparts, *small_args)


def _in_bwd_w(hb, dpa, dgb, dpm, fox):
    s = hb.shape[0]
    t = TILE
    n = s // t
    f_hi = F_ORIG_LO + FOX_HEADS
    n_in = 4 + len(fox)

    def body(*refs):
        h_ref, dpa_ref, dgb_ref, dpm_ref = refs[0:4]
        fox_refs = refs[4:n_in]
        dw_ref, dqk_ref, dv_ref, dfb_ref, dqg_ref, dkg_ref, dbf_ref = refs[n_in:n_in + 7]
        fox_scratch = refs[n_in + 7:]
        i = pl.program_id(0)

        @pl.when(i == 0)
        def _():
            dw_ref[...] = jnp.zeros_like(dw_ref)

        hv = h_ref[...]

        def rows_of(lo, ref, cols=slice(None)):
            def add():
                dproj = ref[:, cols]
                dw_ref[lo:lo + dproj.shape[1], :] += _dot(dproj, hv, TN)
            return add

        q_cols, k_cols = slice(0, FOX_WIDTH), slice(FOX_WIDTH, 2 * FOX_WIDTH)
        between = (rows_of(0, dpa_ref), rows_of(f_hi, dgb_ref), rows_of(f_hi + FOX_WIDTH, dpm_ref), None,
                   rows_of(QB_LO, dqk_ref, q_cols), rows_of(VB_LO, dv_ref), None, None, rows_of(KB_LO, dqk_ref, k_cols))
        _fox_post_tile(i, n, t, *fox_refs, dqk_ref, dv_ref, dfb_ref, dqg_ref, dkg_ref, dbf_ref, *fox_scratch, between)
        dw_ref[F_ORIG_LO:f_hi, :] += _dot(dfb_ref[...].astype(BF16), hv, TN)[0:FOX_HEADS, :]

    def rev(w):
        return _rows_rev(t, w, n)

    row = jax.ShapeDtypeStruct((1, LANES), F32)
    return pl.pallas_call(
        body, name="in_bwd_w", grid=(n,),
        out_shape=(jax.ShapeDtypeStruct((IN_WIDTH, D_MODEL), F32), jax.ShapeDtypeStruct((s, 2 * FOX_WIDTH), BF16),
                   jax.ShapeDtypeStruct((s, FOX_WIDTH), BF16), jax.ShapeDtypeStruct((s, LANES), F32), row, row, row),
        in_specs=[rev(D_MODEL), rev(512), rev(FOX_WIDTH), rev(512), rev(HEAD_BLOCKS), rev(HEAD_BLOCKS),
                  rev(HEAD_BLOCKS), rev(2 * FOX_WIDTH), rev(LANES), _full((1, LANES)), _full((1, FOX_WIDTH)),
                  _full((1, FOX_WIDTH))],
        out_specs=(pl.BlockSpec((IN_WIDTH, D_MODEL), lambda i: (0, 0), pipeline_mode=pl.Buffered(1)),
                   rev(2 * FOX_WIDTH), rev(FOX_WIDTH), rev(LANES), _full((1, LANES)), _full((1, LANES)),
                   _full((1, LANES))),
        scratch_shapes=[pltpu.VMEM((1, FOX_WIDTH), F32), pltpu.VMEM((1, FOX_WIDTH), F32), pltpu.VMEM((1, LANES), F32)],
        compiler_params=_params(),
    )(hb, dpa, dgb, dpm, *fox)


def _adamw_math(w_ref, gv, m_ref, v_ref, d_ref, nm_ref, nv_ref):
    nm = ADAM_B1 * m_ref[...] + (1.0 - ADAM_B1) * gv
    nv = ADAM_B2 * v_ref[...] + (1.0 - ADAM_B2) * (gv * gv)
    m_hat = nm / (1.0 - ADAM_B1 ** ADAM_STEP)
    v_hat = nv / (1.0 - ADAM_B2 ** ADAM_STEP)
    d_ref[...] = -ADAM_LR * (m_hat / (jnp.sqrt(v_hat) + ADAM_EPS) + ADAM_WD * w_ref[...])
    nm_ref[...] = nm
    nv_ref[...] = nv


def _adamw_flat(name, w, g, m, v):
    rows, cols = g.shape
    per_row = cols // LANES

    def body(w_ref, g_ref, m_ref, v_ref, gf_ref, d_ref, nm_ref, nv_ref):
        for k in range(per_row):
            gf_ref[pl.ds(k, rows, stride=per_row), :] = g_ref[:, k * LANES:(k + 1) * LANES]
        _adamw_math(w_ref, gf_ref[...], m_ref, v_ref, d_ref, nm_ref, nv_ref)

    def whole(shape):
        return pl.BlockSpec(shape, lambda i: (0, 0), pipeline_mode=pl.Buffered(1))

    return pl.pallas_call(
        body, name=name, grid=(1,),
        out_shape=(jax.ShapeDtypeStruct(w.shape, F32),) * 4,
        in_specs=[whole(w.shape), whole(g.shape), whole(w.shape), whole(w.shape)], out_specs=(whole(w.shape),) * 4,
        compiler_params=_params(),
    )(w, g, m, v)


def _adamw_rest(vec, dw4, leaves, pool, shards):
    nl = len(VEC_LEAVES) + 1
    ns = len(shards)

    def body(*refs):
        vec_ref, dw4_ref = refs[0:2]
        wmv = refs[2:2 + 3 * nl]
        shard_in = refs[2 + 3 * nl:2 + 3 * nl + 4 * ns]
        o = 2 + 3 * nl + 4 * ns
        loss_ref = refs[o]
        outs = refs[o + 1:o + 1 + 4 * nl]
        shard_out = refs[o + 1 + 4 * nl:]
        loss_ref[...] = vec_ref[VEC_LOSS_ROW:VEC_LOSS_ROW + 1, 0:1]
        for k in range(nl):
            if k < nl - 1:
                _, row, width = VEC_LEAVES[k]
                gv = vec_ref[row:row + 1, 0:width]
            else:
                gv = dw4_ref[...]
            w_ref, m_ref, v_ref = wmv[3 * k:3 * k + 3]
            g_ref, d_ref, nm_ref, nv_ref = outs[4 * k:4 * k + 4]
            g_ref[...] = gv
            _adamw_math(w_ref, gv, m_ref, v_ref, d_ref, nm_ref, nv_ref)
        for k in range(ns):
            w_ref, g_ref, m_ref, v_ref = shard_in[4 * k:4 * k + 4]
            _adamw_math(w_ref, g_ref[...], m_ref, v_ref, *shard_out[3 * k:3 * k + 3])

    shapes = [jax.ShapeDtypeStruct((1, width), F32) for _, _, width in VEC_LEAVES] + [
        jax.ShapeDtypeStruct(dw4.shape, F32)]
    flat_in = [a for triple in list(leaves) + [pool] for a in triple] + [a for quad in shards for a in quad]
    res = pl.pallas_call(
        body, name="adamw_rest",
        out_shape=(jax.ShapeDtypeStruct((1, 1), F32),) + tuple(s for s in shapes for _ in range(4))
        + tuple(jax.ShapeDtypeStruct(quad[0].shape, F32) for quad in shards for _ in range(3)),
        compiler_params=pltpu.CompilerParams(vmem_limit_bytes=VMEM_LIMIT),
    )(vec, dw4, *flat_in)
    per = [res[1 + 4 * k:5 + 4 * k] for k in range(nl)]
    big = res[1 + 4 * nl:]
    return (res[0], [p[0] for p in per], [p[1] for p in per], [p[2] for p in per], [p[3] for p in per],
            [big[3 * k:3 * k + 3] for k in range(ns)])


def _tile_heads(g, n):
    return jnp.tile(g.reshape(1, HEAD_DIM), (1, n))


def kernel(x, mem, norm_g, w_in, b_f, w_pool, pool_scale, fox_q_g, fox_k_g, mem_norm_g, w_mem_kv, mem_q_g, mem_k_g, w_out, loss_target, m_norm_g, m_w_in, m_b_f, m_w_pool, m_pool_scale, m_fox_q_g, m_fox_k_g, m_mem_norm_g, m_w_mem_kv, m_mem_q_g, m_mem_k_g, m_w_out, v_norm_g, v_w_in, v_b_f, v_w_pool, v_pool_scale, v_fox_q_g, v_fox_k_g, v_mem_norm_g, v_w_mem_kv, v_mem_q_g, v_mem_k_g, v_w_out):
    w_in_t = w_in[0].T
    axes = (1, 0, 0)

    g_in, g_kv, g_out = _all_gather_weights([w_in_t, w_mem_kv[0], w_out[0]], axes)
    tiled = _tiled_params(b_f, fox_q_g, fox_k_g, mem_q_g, mem_k_g)
    fwd, wp = _fwd_in(x[0], norm_g, g_in, *tiled[0:3])
    w_kv_b = g_kv.reshape(D_MODEL, 2 * MEM_WIDTH)
    w_out_b = g_out.reshape(D_MODEL, D_MODEL)
    w4 = w_pool.reshape(POOL_ROWS, HEAD_DIM)
    dy, hb, dpa, dgb, dpm, fox, g_w_kv, g_w_out, (dmemnorm_g, dpscale, dmq_g, dmk_g), loss_row, dw4 = _local_partials(
        x[0], mem[0], loss_target[0], fwd, w_kv_b, w_out_b, tiled, w4, pool_scale, mem_norm_g, axes[1:])
    dwp, dqk, dvb, dfb, dfq_g, dfk_g, dbf = _in_bwd_w(hb, dpa, dgb, dpm, fox)
    dparts = (dpa, dqk, dvb, dgb, dpm, dfb)
    vec_leaves = (dmemnorm_g, dpscale, dbf, dfq_g, dfk_g, dmq_g, dmk_g)
    grad_x, _, g_w_in_t, vec, dw4_sum = _in_bwd_x(
        x[0], dy, norm_g, wp, dparts, [dwp], axes[0:1], (vec_leaves, loss_row, dw4))

    small_wmv = [(norm_g, m_norm_g, v_norm_g), (mem_norm_g, m_mem_norm_g, v_mem_norm_g),
                 (pool_scale, m_pool_scale, v_pool_scale), (b_f, m_b_f, v_b_f), (fox_q_g, m_fox_q_g, v_fox_q_g),
                 (fox_k_g, m_fox_k_g, v_fox_k_g), (mem_q_g, m_mem_q_g, v_mem_q_g), (mem_k_g, m_mem_k_g, v_mem_k_g)]
    pool_wmv = tuple(a.reshape(POOL_ROWS, HEAD_DIM) for a in (w_pool, m_w_pool, v_w_pool))
    loss, *small_out, (upd_kv, upd_out) = _adamw_rest(
        vec, dw4_sum, small_wmv, pool_wmv, [(w_mem_kv[0], g_w_kv, m_w_mem_kv[0], v_w_mem_kv[0]),
                                             (w_out[0], g_w_out, m_w_out[0], v_w_out[0])])
    tiles = D_MODEL // LANES

    def flat(a):
        return a.reshape(tiles, LANES, -1).transpose(2, 0, 1).reshape(-1, LANES)

    def unflat(a):
        return a.reshape(-1, tiles, LANES).transpose(1, 2, 0).reshape(w_in.shape)

    g_in_flat, *upd_in = _adamw_flat("adamw_w_in", flat(w_in), g_w_in_t, flat(m_w_in), flat(v_w_in))
    big = [[unflat(g_in_flat), g_w_kv[None], g_w_out[None]]]
    big += [[unflat(upd_in[k]), upd_kv[k][None], upd_out[k][None]] for k in range(3)]

    def leaves(k):
        sm = small_out[k]
        b_in, b_kv, b_out = big[k]
        return (sm[0], b_in, sm[3], sm[8].reshape(w_pool.shape), sm[2], sm[4], sm[5], sm[1], b_kv, sm[6], sm[7], b_out)

    return (loss.reshape(()), grad_x[None], *leaves(0), *leaves(1), *leaves(2), *leaves(3))


def _tiled_params(b_f, fox_q_g, fox_k_g, mem_q_g, mem_k_g):
    return (jnp.pad(b_f, ((0, 0), (0, LANES - FOX_HEADS))), _tile_heads(fox_q_g, FOX_HEADS),
            _tile_heads(fox_k_g, FOX_HEADS), _tile_heads(mem_q_g, 4), _tile_heads(mem_k_g, 4))


def _local_partials(xs, mems, tgt, fwd, w_kv_b, w_out_b, tiled, w4, pool_scale, mem_norm_g, axes):
    hb, pa, qk, qa, ka, va, gb, pm, fb = fwd
    bf_pad, fq_g, fk_g, mq_g, mk_g = tiled

    ma, db, mm, mnb, kv, kmn, vmb = _side_fwd(pa, pm, w4, pool_scale, mq_g, mems, mem_norm_g, w_kv_b, mk_g)
    o, mb, r4 = _fox_fwd(qa, ka, va, gb)
    dy, dma, dmm, dw_out, loss_row, doa, dgb, rr = _out_loss(xs, tgt, ma, mb, mm, w_out_b, gb, o, r4)

    dpa, dpm, dw4, dpscale, dmq_g, dw_kv, dmemnorm_g, dmk_g = _side_bwd(
        pa, db, dma, w4, pool_scale, pm, dmm, kmn, vmb, mq_g, kv, mnb, mems, w_kv_b, mk_g, mem_norm_g)
    if axes:
        parts = [dw_kv.reshape(4, D_MODEL // 4, 2 * MEM_WIDTH), dw_out.reshape(4, D_MODEL // 4, D_MODEL)]
        dka, dva, dqa, dw_kv, dw_out = _fox_bwd(ka, va, qa, doa, rr, parts, axes)
    else:
        dka, dva, dqa = _fox_bwd(ka, va, qa, doa, rr, [], ())
    fox = (dqa, dka, dva, qk, fb, bf_pad, fq_g, fk_g)
    return dy, hb, dpa, dgb, dpm, fox, dw_kv, dw_out, (dmemnorm_g, dpscale, dmq_g, dmk_g), loss_row, dw4
```

```python
import functools

import jax
import jax.numpy as jnp
from jax import lax
from jax.experimental import pallas as pl
from jax.experimental.pallas import tpu as pltpu

F32 = jnp.float32
BF16 = jnp.bfloat16
MESH = pl.DeviceIdType.MESH

D_MODEL = 1024
HEAD_DIM = 64
POOL_WIDTH = 256
FOX_WIDTH = 512
FOX_HEADS = 8
MEM_WIDTH = 256
N_MEM = 256
IN_WIDTH = 3080
EPS = 1e-6
ATT_SCALE = 0.125

ADAM_LR = 0.001
ADAM_B1 = 0.9
ADAM_B2 = 0.999
ADAM_EPS = 1e-08
ADAM_WD = 0.01
ADAM_STEP = 10

LANES = 128
PA_LO, QB_LO, KB_LO, VB_LO, GB_LO, PM_LO, FB_LO, PROJ_PAD = 0, 512, 1024, 1536, 2048, 2560, 3072, 3200
F_ORIG_LO = 2048

TILE = 512
VMEM_LIMIT = 56 * 1024 * 1024
VMEM_LIMIT_FOX_BWD = 58 * 1024 * 1024

VEC_LEAVES = (("norm_g", 0, 1024), ("mem_norm_g", 1, 1024), ("pool_scale", 2, 256), ("b_f", 3, 8),
              ("fox_q_g", 4, 64), ("fox_k_g", 5, 64), ("mem_q_g", 6, 64), ("mem_k_g", 7, 64))
VEC_LOSS_ROW = 8
VEC_ROWS = 16
POOL_ROWS = 256


def _params(n_grid=1, vmem=VMEM_LIMIT):
    return pltpu.CompilerParams(dimension_semantics=("arbitrary",) * n_grid, vmem_limit_bytes=vmem)


def _rows(t, w):
    return pl.BlockSpec((t, w), lambda i: (i, 0))


def _rows_rev(t, w, n):
    return pl.BlockSpec((t, w), lambda i: (n - 1 - i, 0))


def _full(shape):
    return pl.BlockSpec(shape, lambda i: (0,) * len(shape))


def _sig(x):
    return 1.0 / (1.0 + jnp.exp(-x))


def _lane_lo(shape):
    return lax.broadcasted_iota(jnp.int32, shape, 1) < HEAD_DIM


def _pair_sum(v, lo):
    s0 = jnp.sum(jnp.where(lo, v, 0.0), axis=-1, keepdims=True)
    s1 = jnp.sum(jnp.where(lo, 0.0, v), axis=-1, keepdims=True)
    return jnp.where(lo, s0, s1)


def _head_rms(blk, lo):
    return lax.rsqrt(_pair_sum(blk * blk, lo) * (1.0 / HEAD_DIM) + EPS)


def _head_norm_bwd(dyn, xhat, rr, g, lo):
    a = dyn * g
    return rr * (a - xhat * (_pair_sum(xhat * a, lo) * (1.0 / HEAD_DIM)))


def _fold_heads(acc):
    tot = acc[:, 0:LANES]
    for p in range(1, acc.shape[1] // LANES):
        tot = tot + acc[:, p * LANES:(p + 1) * LANES]
    return tot + pltpu.roll(tot, HEAD_DIM, axis=1)


def _lane_pick(v, lane, idx):
    return jnp.sum(jnp.where(lane == idx, v, 0.0), axis=-1, keepdims=True)


NT = (((1,), (1,)), ((), ()))
TN = (((0,), (0,)), ((), ()))


def _dot(a, b, dims=None):
    if dims is None:
        return jnp.dot(a, b, preferred_element_type=F32)
    return lax.dot_general(a, b, dims, preferred_element_type=F32)


def _my_place():
    return lax.axis_index("x"), lax.axis_index("y"), lax.axis_index("c")


def _half_dims(shape, axis):
    return (shape[0] // 2, shape[1]) if axis == 0 else (shape[0], shape[1] // 2)


def _shard_shape(g):
    return tuple(g.shape[1:]) if len(g.shape) == 3 else (g.shape[0] // 4, g.shape[1])


F32_ROWS = 8


def _shard_window(g):
    rows = _shard_shape(g)[0]
    if len(g.shape) == 3:
        return rows
    skew = max((j * rows) % F32_ROWS for j in range(4))
    return -(-(rows + skew) // F32_ROWS) * F32_ROWS


def _half_of(ref, axis, core, lead=False):
    rows, cols = ref.shape[-2:]
    if axis == 0:
        idx = (pl.ds(pl.multiple_of(core * (rows // 2), 16), rows // 2), slice(None))
    else:
        idx = (slice(None), pl.ds(pl.multiple_of(core * (cols // 2), LANES), cols // 2))
    return ref.at[(slice(None),) + idx] if lead else ref.at[idx]


class _HalfGather:
    def __init__(self, ins, outs, axes, f32_bufs, bf_bufs, send_sems, recv_sems, local_sems):
        self.ins, self.outs, self.axes = ins, outs, axes
        self.f32_bufs, self.bf_bufs = f32_bufs, bf_bufs
        self.send_sems, self.recv_sems, self.local_sems = send_sems, recv_sems, local_sems
        self.n = len(ins)
        x, y, self.c = _my_place()
        self.me, self.sibling = (x, y, self.c), (x, y, 1 - self.c)
        self.chips = [(1 - x, y), (x, 1 - y), (1 - x, 1 - y)]

    @staticmethod
    def scratch(shards, axes):
        dims = [_half_dims(a.shape, axis) for a, axis in zip(shards, axes)]
        n = len(shards)
        return [pltpu.VMEM(d, F32) for d in dims] + [pltpu.VMEM(d, BF16) for d in dims] + [
            pltpu.SemaphoreType.DMA((7 * n,)), pltpu.SemaphoreType.DMA((7 * n,)), pltpu.SemaphoreType.DMA((2 * n,))]

    @staticmethod
    def out_shapes(shards, axes):
        return tuple(jax.ShapeDtypeStruct((8,) + _half_dims(a.shape, axis), BF16) for a, axis in zip(shards, axes))

    def _blk(self, a, px, py, pc):
        return self.outs[a].at[4 * px + 2 * py + pc]

    def _copy(self, a, k, block, to, src=None):
        return pltpu.make_async_remote_copy(
            src_ref=self._blk(a, *block) if src is None else src, dst_ref=self._blk(a, *block),
            send_sem=self.send_sems.at[7 * a + k], recv_sem=self.recv_sems.at[7 * a + k], device_id=to,
            device_id_type=MESH)

    def _keep(self, a):
        return pltpu.make_async_copy(self.bf_bufs[a], self._blk(a, *self.me), self.local_sems.at[self.n + a])

    def _first(self, a):
        mine = [self._copy(a, 0, self.me, self.sibling, src=self.bf_bufs[a])]
        return mine + [self._copy(a, 1 + j, self.me, (*chip, self.c), src=self.bf_bufs[a])
                       for j, chip in enumerate(self.chips)]

    def send_mine(self):
        loads = [pltpu.make_async_copy(_half_of(self.ins[a], self.axes[a], self.c), self.f32_bufs[a],
                                       self.local_sems.at[a]) for a in range(self.n)]
        for cp in loads:
            cp.start()
        for a in range(self.n):
            loads[a].wait()
            self.bf_bufs[a][...] = self.f32_bufs[a][...].astype(BF16)
            self._keep(a).start()
            for cp in self._first(a):
                cp.start()

    def pass_on(self):
        for a in range(self.n):
            for j, chip in enumerate(self.chips):
                self._copy(a, 1 + j, (*chip, self.c), self.me).wait_recv()
                self._copy(a, 4 + j, (*chip, self.c), self.sibling).start()

    def finish(self):
        for a in range(self.n):
            self._copy(a, 0, self.sibling, self.me).wait_recv()
            for j, chip in enumerate(self.chips):
                self._copy(a, 4 + j, (*chip, 1 - self.c), self.me).wait_recv()
        for a in range(self.n):
            for cp in self._first(a):
                cp.wait_send()
            for j, chip in enumerate(self.chips):
                self._copy(a, 4 + j, (*chip, self.c), self.sibling).wait_send()
            self._keep(a).wait()


def _all_gather_weights(shards, axes):
    n = len(shards)

    def body(*refs):
        gather = _HalfGather(refs[0:n], refs[n:2 * n], axes, refs[2 * n:3 * n], refs[3 * n:4 * n], *refs[4 * n:])
        gather.send_mine()
        gather.pass_on()
        gather.finish()

    any_spec = pl.BlockSpec(memory_space=pl.ANY)
    return pl.pallas_call(
        body, name="weights_all_gather", out_shape=_HalfGather.out_shapes(shards, axes),
        in_specs=[any_spec] * n, out_specs=(any_spec,) * n, scratch_shapes=_HalfGather.scratch(shards, axes),
        compiler_params=pltpu.CompilerParams(vmem_limit_bytes=VMEM_LIMIT),
    )(*shards)


class _ShardReduce:
    SEMS = 8
    LOCAL = 5

    def __init__(self, g_refs, out_refs, axes, bufs, send_sems, recv_sems, local_sems):
        self.g_refs, self.out_refs, self.axes = g_refs, out_refs, axes
        self.recv_a, self.own_a, self.send_b, self.recv_b, self.fin = bufs
        self.send_sems, self.recv_sems, self.local_sems = send_sems, recv_sems, local_sems
        self.n = len(g_refs)
        x, y, self.c = _my_place()
        self.chip = 2 * x + y
        self.sibling = (x, y, 1 - self.c)

    @staticmethod
    def scratch(gparts, axes):
        assert all(len(g.shape) == 3 or axis == 1 for g, axis in zip(gparts, axes))
        dims = [_half_dims(_shard_shape(g), axis) for g, axis in zip(gparts, axes)]
        windows = [d if len(g.shape) == 3 else (_shard_window(g),) + d[1:] for g, d in zip(gparts, dims)]
        shapes = []
        for dtype, lead, per_array in ((F32, (4,), windows), (F32, (4,), windows), (BF16, (4,), dims),
                                       (BF16, (4,), dims), (F32, (), dims)):
            shapes += [pltpu.VMEM(lead + d, dtype) for d in per_array]
        return shapes

    def _shard_half(self, a, j, core):
        g = self.g_refs[a]
        if len(g.shape) == 3:
            return _half_of(g.at[j], self.axes[a], core)
        start = (j * _shard_shape(g)[0]) // F32_ROWS * F32_ROWS
        return _half_of(g.at[pl.ds(pl.multiple_of(start, F32_ROWS), _shard_window(g))], self.axes[a], core)

    def _to_sibling(self, a, j):
        return pltpu.make_async_remote_copy(
            src_ref=self._shard_half(a, j, 1 - self.c), dst_ref=self.recv_a[a].at[j],
            send_sem=self.send_sems.at[self.SEMS * a + j], recv_sem=self.recv_sems.at[self.SEMS * a + j], device_id=self.sibling,
            device_id_type=MESH)

    def _own(self, a, j):
        return pltpu.make_async_copy(self._shard_half(a, j, self.c), self.own_a[a].at[j],
                                     self.local_sems.at[self.LOCAL * a + j])

    def _to_chip(self, a, k):
        dest = (self.chip + k) % 4
        return pltpu.make_async_remote_copy(
            src_ref=self.send_b[a].at[dest], dst_ref=self.recv_b[a].at[self.chip],
            send_sem=self.send_sems.at[self.SEMS * a + 3 + k], recv_sem=self.recv_sems.at[self.SEMS * a + 3 + k],
            device_id=(dest // 2, dest % 2, self.c), device_id_type=MESH)

    def _give(self, a):
        return pltpu.make_async_remote_copy(
            src_ref=self.fin[a], dst_ref=_half_of(self.out_refs[a], self.axes[a], self.c),
            send_sem=self.send_sems.at[self.SEMS * a + 7], recv_sem=self.recv_sems.at[self.SEMS * a + 7], device_id=self.sibling,
            device_id_type=MESH)

    def _mine(self, a):
        return pltpu.make_async_copy(self.fin[a], _half_of(self.out_refs[a], self.axes[a], self.c),
                                     self.local_sems.at[self.LOCAL * a])

    def exchange_with_sibling(self):
        for k in (1, 2, 3, 0):
            j = (self.chip + k) % 4
            for a in range(self.n):
                self._to_sibling(a, j).start()
                self._own(a, j).start()

    def _chip_partial(self, a, j):
        self._own(a, j).wait()
        self._to_sibling(a, j).wait_recv()
        g = self.g_refs[a]
        if len(g.shape) == 3:
            self.send_b[a][j] = (self.own_a[a][j] + self.recv_a[a][j]).astype(BF16)
            return
        rows = _shard_shape(g)[0]
        for shard in range(4):
            @pl.when(j == shard)
            def _():
                at = pl.ds((shard * rows) % F32_ROWS, rows)
                self.send_b[a][shard] = (self.own_a[a][shard, at, :] + self.recv_a[a][shard, at, :]).astype(BF16)

    def send_to_chip(self, k):
        for a in range(self.n):
            self._chip_partial(a, (self.chip + k) % 4)
            self._to_chip(a, k).start()

    def keep_mine(self):
        for a in range(self.n):
            self._chip_partial(a, self.chip)
            keep = pltpu.make_async_copy(self.send_b[a].at[self.chip], self.recv_b[a].at[self.chip],
                                         self.local_sems.at[self.LOCAL * a + 4])
            keep.start()
            keep.wait()

    def sum_and_share(self):
        for a in range(self.n):
            for k in range(1, 4):
                self._to_chip(a, k).wait_recv()
            tot = self.recv_b[a][0].astype(F32) + self.recv_b[a][1].astype(F32)
            tot = tot + self.recv_b[a][2].astype(F32)
            self.fin[a][...] = tot + self.recv_b[a][3].astype(F32)
            self._give(a).start()
            self._mine(a).start()

    def finish(self):
        for a in range(self.n):
            self._give(a).wait_recv()
            self._mine(a).wait()
            self._give(a).wait_send()
            for j in range(4):
                self._to_sibling(a, j).wait_send()
            for k in range(1, 4):
                self._to_chip(a, k).wait_send()


def _mem_tokens_fwd(mem_ref, g_ref, w_ref, kg_ref, mn_ref, kv_ref, kn_ref, vm_ref):
    xm = mem_ref[...]
    rr = lax.rsqrt(jnp.mean(xm * xm, axis=-1, keepdims=True) + EPS)
    mnb = ((xm * rr) * g_ref[...]).astype(BF16)
    mn_ref[...] = mnb
    kv = _dot(mnb, w_ref[...])
    kv_ref[...] = kv
    lo = _lane_lo((xm.shape[0], LANES))
    for p in range(MEM_WIDTH // LANES):
        sl = slice(p * LANES, (p + 1) * LANES)
        kb = kv[:, sl]
        kn_ref[:, sl] = ((kb * _head_rms(kb, lo)) * kg_ref[:, sl]).astype(BF16)
    vm_ref[...] = kv[:, MEM_WIDTH:].astype(BF16)


AUG_LO = 64
KEY_SUM_LANE = 72
QUERY_SUM_LANE = 80
HEAD_BLOCKS = FOX_HEADS * LANES


def _ones3(lane):
    return jnp.where((lane >= AUG_LO) & (lane < AUG_LO + 3), 1.0, 0.0)


def _spread3(cols):
    hi = cols.astype(BF16)
    rest = cols - hi.astype(F32)
    mid = rest.astype(BF16)
    low = (rest - mid.astype(F32)).astype(BF16)
    r = lax.broadcasted_iota(jnp.int32, (LANES, HEAD_BLOCKS), 0)
    c = lax.broadcasted_iota(jnp.int32, (LANES, HEAD_BLOCKS), 1)
    out = None
    for k, part in enumerate((hi, mid, low)):
        term = _dot(part, jnp.where(c == r * LANES + (AUG_LO + k), 1.0, 0.0).astype(BF16))
        out = term if out is None else out + term
    return out


def _head_block(pair_blk, hh, lo, extras):
    src = pair_blk if hh == 0 else pltpu.roll(pair_blk, HEAD_DIM, axis=1)
    return jnp.where(lo, src, extras).astype(BF16)


def _pair_block(blk0, blk1, lo):
    return jnp.where(lo, blk0, pltpu.roll(blk1, HEAD_DIM, axis=1))


def _assemble_w_in(halves_ref, words_ref, wp_ref):
    shard = IN_WIDTH // 4
    half = D_MODEL // 2
    f_hi = F_ORIG_LO + FOX_HEADS
    for j in range(4):
        blocks = [pltpu.bitcast(halves_ref[2 * j + c], jnp.uint32) for c in range(2)]
        for lo, hi, to in ((0, F_ORIG_LO, PA_LO), (F_ORIG_LO, f_hi, FB_LO), (f_hi, IN_WIDTH, GB_LO)):
            a, b = max(lo, shard * j), min(hi, shard * (j + 1))
            if a < b:
                for c in range(2):
                    words_ref[(to + a - lo) // 2:(to + b - lo) // 2, c * half:(c + 1) * half] = (
                        blocks[c][(a - shard * j) // 2:(b - shard * j) // 2, :])
    pad_lo = (FB_LO + FOX_HEADS) // 2
    words_ref[pad_lo:, :] = jnp.zeros((PROJ_PAD // 2 - pad_lo, D_MODEL), jnp.uint32)
    wp_ref[...] = pltpu.bitcast(words_ref[...], BF16)


def _fwd_in(x, norm_g, halves, bf_pad, fq_g, fk_g):
    s = x.shape[0]
    t = TILE
    n = s // t

    def body(x_ref, ng_ref, halves_ref, bf_ref, qg_ref, kg_ref,
             h_ref, pa_ref, qk_ref, qa_ref, ka_ref, va_ref, gb_ref, pm_ref, fb_ref, wp_ref,
             carry_ref, fcol_ref, words_ref):
        @pl.when(pl.program_id(0) == 0)
        def _():
            carry_ref[...] = jnp.zeros_like(carry_ref)
            _assemble_w_in(halves_ref, words_ref, wp_ref)

        xv = x_ref[...]
        rr = lax.rsqrt(jnp.mean(xv * xv, axis=-1, keepdims=True) + EPS)
        hb = ((xv * rr) * ng_ref[...]).astype(BF16)
        h_ref[...] = hb

        def proj(lo, hi):
            return _dot(hb, wp_ref[lo:hi, :], NT)

        fb = proj(FB_LO, PROJ_PAD)
        fb_ref[...] = fb
        qk_ref[:, 0:FOX_WIDTH] = proj(QB_LO, KB_LO)

        lane = lax.broadcasted_iota(jnp.int32, (t, LANES), 1)
        row = lax.broadcasted_iota(jnp.int32, (t, LANES), 0)
        lo = lane < HEAD_DIM
        z = fb + bf_ref[...]
        lf = -(jnp.maximum(-z, 0.0) + jnp.log1p(jnp.exp(-jnp.abs(z))))
        lf = jnp.where(lane < FOX_HEADS, lf, 0.0)
        sh = 1
        while sh < t:
            lf = lf + jnp.where(row >= sh, pltpu.roll(lf, sh, axis=0), 0.0)
            sh *= 2
        fcum = lf + carry_ref[...]
        fcol_ref[...] = fcum
        carry_ref[...] = fcol_ref[t - 1:t, :]

        ones3 = _ones3(lane)
        minus_f = _spread3(-fcum)

        def head_blocks(seg, g_ref, out_ref, scale):
            for p in range(FOX_WIDTH // LANES):
                sl = slice(p * LANES, (p + 1) * LANES)
                blk = qk_ref[:, seg - QB_LO + p * LANES:seg - QB_LO + (p + 1) * LANES]
                normed = ((blk * _head_rms(blk, lo)) * g_ref[:, sl]) * scale
                for hh in range(2):
                    h = 2 * p + hh
                    if seg == QB_LO:
                        extras = jnp.where(lane == QUERY_SUM_LANE + h, 1.0, ones3)
                    else:
                        extras = jnp.where(lane == KEY_SUM_LANE + h, 1.0, minus_f[:, h * LANES:(h + 1) * LANES])
                    out_ref[:, h * LANES:(h + 1) * LANES] = _head_block(normed, hh, lo, extras)

        qk_ref[:, FOX_WIDTH:2 * FOX_WIDTH] = proj(KB_LO, VB_LO)
        pa_ref[...] = proj(PA_LO, QB_LO)
        head_blocks(QB_LO, qg_ref, qa_ref, ATT_SCALE)
        vraw = proj(VB_LO, GB_LO)
        gb_ref[...] = proj(GB_LO, PM_LO)
        head_blocks(KB_LO, kg_ref, ka_ref, 1.0)
        pm_ref[...] = proj(PM_LO, FB_LO)
        for h in range(FOX_HEADS):
            va_ref[:, h * LANES:(h + 1) * LANES] = _head_block(vraw[:, (h // 2) * LANES:(h // 2 + 1) * LANES], h % 2, lo, ones3)

    outs = (
        jax.ShapeDtypeStruct((s, D_MODEL), BF16),
        jax.ShapeDtypeStruct((s, 512), F32),
        jax.ShapeDtypeStruct((s, 2 * FOX_WIDTH), F32),
        jax.ShapeDtypeStruct((s, HEAD_BLOCKS), BF16),
        jax.ShapeDtypeStruct((s, HEAD_BLOCKS), BF16),
        jax.ShapeDtypeStruct((s, HEAD_BLOCKS), BF16),
        jax.ShapeDtypeStruct((s, FOX_WIDTH), F32),
        jax.ShapeDtypeStruct((s, 512), F32),
        jax.ShapeDtypeStruct((s, LANES), F32),
        jax.ShapeDtypeStruct((PROJ_PAD, D_MODEL), BF16),
    )

    def resident(shape):
        return pl.BlockSpec(shape, lambda i: (0,) * len(shape), pipeline_mode=pl.Buffered(1))

    *fwd, wp = pl.pallas_call(
        body, name="fwd_in", grid=(n,), out_shape=outs,
        in_specs=[_rows(t, D_MODEL), _full((1, D_MODEL)), resident(halves.shape), _full((1, LANES)),
                  _full((1, FOX_WIDTH)), _full((1, FOX_WIDTH))],
        out_specs=(_rows(t, D_MODEL), _rows(t, 512), _rows(t, 2 * FOX_WIDTH), _rows(t, HEAD_BLOCKS),
                   _rows(t, HEAD_BLOCKS), _rows(t, HEAD_BLOCKS), _rows(t, FOX_WIDTH), _rows(t, 512),
                   _rows(t, LANES), resident((PROJ_PAD, D_MODEL))),
        scratch_shapes=[pltpu.VMEM((1, LANES), F32), pltpu.VMEM((t, LANES), F32),
                        pltpu.VMEM((PROJ_PAD // 2, D_MODEL), jnp.uint32)],
        compiler_params=_params(),
    )(x, norm_g, halves, bf_pad, fq_g, fk_g)
    return tuple(fwd), wp


POOL_HALO = 16


def _pool_window(lane):
    return jnp.where(lane < 64, 2.0, jnp.where(lane < 128, 4.0, jnp.where(lane < 192, 8.0, 16.0)))


def _pool_pick(lane, s2, s4, s8, s16):
    return jnp.where(lane < 64, s2, jnp.where(lane < 128, s4, jnp.where(lane < 192, s8, s16)))


def _group_onehot(shape, row_is_group_lane):
    r = lax.broadcasted_iota(jnp.int32, shape, 0)
    c = lax.broadcasted_iota(jnp.int32, shape, 1)
    hit = (r % HEAD_DIM == c) if row_is_group_lane else (c % HEAD_DIM == r)
    return jnp.where(hit, 1.0, 0.0).astype(F32)


def _same_group(shape):
    r = lax.broadcasted_iota(jnp.int32, shape, 0)
    c = lax.broadcasted_iota(jnp.int32, shape, 1)
    return (r // HEAD_DIM) == (c // HEAD_DIM)


def _pool_block_diag(w4):
    spread = jnp.dot(w4, _group_onehot((HEAD_DIM, POOL_WIDTH), False), preferred_element_type=F32,
                     precision=lax.Precision.HIGHEST)
    return jnp.where(_same_group((POOL_WIDTH, POOL_WIDTH)), spread, 0.0).astype(BF16)


def _mem_softmax(qm, kp):
    sc = _dot(qm, kp, NT)
    e = jnp.exp(sc - jnp.max(sc, axis=-1, keepdims=True))
    return e * (1.0 / jnp.sum(e, axis=-1, keepdims=True))


def _side_fwd(pa, pm, w4, pscale, mq_g, mem, mem_norm_g, w_kv, mk_g):
    s = pa.shape[0]
    t = TILE
    n = s // t
    ext = t + POOL_HALO
    nm = mem.shape[0]

    def body(pa_ref, pm_ref, w4_ref, sc_ref, g_ref, mem_ref, mg_ref, wkv_ref, kg_ref,
             ma_ref, d_ref, mm_ref, mn_ref, kv_ref, k_ref, v_ref, ext_ref, w_ref):
        i = pl.program_id(0)

        @pl.when(i == 0)
        def _():
            ext_ref[0:POOL_HALO, :] = jnp.zeros((POOL_HALO, POOL_WIDTH), F32)
            w_ref[...] = _pool_block_diag(w4_ref[...])
            _mem_tokens_fwd(mem_ref, mg_ref, wkv_ref, kg_ref, mn_ref, kv_ref, k_ref, v_ref)

        u = pa_ref[:, 0:POOL_WIDTH]
        ext_ref[POOL_HALO:ext, :] = u
        e = ext_ref[...]
        s2 = e + pltpu.roll(e, 1, axis=0)
        s4 = s2 + pltpu.roll(s2, 2, axis=0)
        s8 = s4 + pltpu.roll(s4, 4, axis=0)
        s16 = s8 + pltpu.roll(s8, 8, axis=0)
        lane_e = lax.broadcasted_iota(jnp.int32, (ext, POOL_WIDTH), 1)
        win = _pool_pick(lane_e, s2, s4, s8, s16)[POOL_HALO:ext, :]
        lane = lax.broadcasted_iota(jnp.int32, (t, POOL_WIDTH), 1)
        pos = (lax.broadcasted_iota(jnp.int32, (t, POOL_WIDTH), 0) + (i * t + 1)).astype(F32)
        d = win / jnp.minimum(pos, _pool_window(lane)) - u
        db = d.astype(BF16)
        d_ref[...] = db
        ya = _dot(db, w_ref[...]) * sc_ref[...]
        ga = pa_ref[:, POOL_WIDTH:2 * POOL_WIDTH]
        ma_ref[...] = (ya * (ga * _sig(ga))).astype(BF16)
        ext_ref[0:POOL_HALO, :] = ext_ref[t:ext, :]

        lo = _lane_lo((t, LANES))
        for p in range(MEM_WIDTH // LANES):
            sl = slice(p * LANES, (p + 1) * LANES)
            qb = pm_ref[:, sl]
            qs = (((qb * _head_rms(qb, lo)) * g_ref[:, sl]) * ATT_SCALE).astype(BF16)
            kp = k_ref[:, sl]
            vp = v_ref[:, sl]
            outs = []
            for hh in range(2):
                msk = lo if hh == 0 else jnp.logical_not(lo)
                prob = _mem_softmax(jnp.where(msk, qs, jnp.zeros_like(qs)), kp)
                outs.append(_dot(prob.astype(BF16), vp))
            o = jnp.where(lo, outs[0], outs[1])
            gm = pm_ref[:, MEM_WIDTH + p * LANES:MEM_WIDTH + (p + 1) * LANES]
            mm_ref[:, sl] = (o * (gm * _sig(gm))).astype(BF16)

    return pl.pallas_call(
        body, name="side_fwd", grid=(n,),
        out_shape=(jax.ShapeDtypeStruct((s, POOL_WIDTH), BF16), jax.ShapeDtypeStruct((s, POOL_WIDTH), BF16),
                   jax.ShapeDtypeStruct((s, MEM_WIDTH), BF16), jax.ShapeDtypeStruct((nm, D_MODEL), BF16),
                   jax.ShapeDtypeStruct((nm, 2 * MEM_WIDTH), F32), jax.ShapeDtypeStruct((nm, MEM_WIDTH), BF16),
                   jax.ShapeDtypeStruct((nm, MEM_WIDTH), BF16)),
        in_specs=[_rows(t, 512), _rows(t, 512), _full((POOL_ROWS, HEAD_DIM)), _full((1, POOL_WIDTH)),
                  _full((1, MEM_WIDTH)), _full((nm, D_MODEL)), _full((1, D_MODEL)), _full((D_MODEL, 2 * MEM_WIDTH)),
                  _full((1, MEM_WIDTH))],
        out_specs=(_rows(t, POOL_WIDTH), _rows(t, POOL_WIDTH), _rows(t, MEM_WIDTH), _full((nm, D_MODEL)),
                   _full((nm, 2 * MEM_WIDTH)), _full((nm, MEM_WIDTH)), _full((nm, MEM_WIDTH))),
        scratch_shapes=[pltpu.VMEM((ext, POOL_WIDTH), F32), pltpu.VMEM((POOL_WIDTH, POOL_WIDTH), BF16)],
        compiler_params=_params(),
    )(pa, pm, w4, pscale, mq_g, mem, mem_norm_g, w_kv, mk_g)


FOX_FWD_HEADS = 4


def _fox_fwd(qa, ka, va, gb):
    s = qa.shape[0]
    t = TILE
    n = s // t
    heads = FOX_FWD_HEADS
    pairs = heads // 2
    group_w = heads * LANES

    def body(qa_ref, ka_ref, va_ref, gb_ref, o_ref, mb_ref, r_ref):
        i = pl.program_id(1)
        lane = lax.broadcasted_iota(jnp.int32, (t, LANES), 1)
        lo = lane < HEAD_DIM
        causal = lax.broadcasted_iota(jnp.int32, (t, t), 1) <= lax.broadcasted_iota(jnp.int32, (t, t), 0)
        qas = [qa_ref[:, hh * LANES:(hh + 1) * LANES] for hh in range(heads)]

        def step(j, carry, masked):
            rows = pl.ds(pl.multiple_of(j * t, t), t)
            def logits(hh):
                sc = _dot(qas[hh], ka_ref[rows, hh * LANES:(hh + 1) * LANES], NT)
                return jnp.where(causal, sc, -1e30) if masked else sc

            def advance(hh, sc):
                m, acc = carry[hh]
                m_new = jnp.maximum(m, jnp.max(sc, axis=-1, keepdims=True))
                p = jnp.exp(sc - m_new).astype(BF16)
                return m_new, jnp.exp(m - m_new) * acc + _dot(p, va_ref[rows, hh * LANES:(hh + 1) * LANES])

            new = []
            sc = logits(0)
            for hh in range(heads):
                sc_next = logits(hh + 1) if hh + 1 < heads else None
                new.append(advance(hh, sc))
                sc = sc_next
            return tuple(new)

        init = (jnp.full((t, 1), -1e30, F32), jnp.zeros((t, LANES), F32))
        carry = lax.fori_loop(0, i, functools.partial(step, masked=False), (init,) * heads)
        res = step(i, carry, masked=True)
        for p in range(pairs):
            outs = []
            rcol = jnp.zeros((t, LANES), F32)
            for hh in range(2):
                m, acc = res[2 * p + hh]
                l = _lane_pick(acc, lane, AUG_LO)
                outs.append(acc * (1.0 / l))
                rcol = jnp.where(lane == hh, m + jnp.log(l), rcol)
            o = _pair_block(outs[0], outs[1], lo)
            sl = slice(p * LANES, (p + 1) * LANES)
            o_ref[:, sl] = o
            g = gb_ref[:, sl]
            mb_ref[:, sl] = (o * (g * _sig(g))).astype(BF16)
            r_ref[p] = rcol

    tile_spec = pl.BlockSpec((t, pairs * LANES), lambda p, i: (i, p))
    full_spec = pl.BlockSpec((s, group_w), lambda p, i: (0, p))
    return pl.pallas_call(
        body, name="fox_fwd", grid=(FOX_HEADS // heads, n),
        out_shape=(jax.ShapeDtypeStruct((s, FOX_WIDTH), F32), jax.ShapeDtypeStruct((s, FOX_WIDTH), BF16),
                   jax.ShapeDtypeStruct((FOX_HEADS // 2, s, LANES), F32)),
        in_specs=[pl.BlockSpec((t, group_w), lambda p, i: (i, p)), full_spec, full_spec, tile_spec],
        out_specs=(tile_spec, tile_spec, pl.BlockSpec((pairs, t, LANES), lambda p, i: (p, i, 0))),
        compiler_params=_params(2),
    )(qa, ka, va, gb)


def _out_loss(x, tgt, ma, mb, mm, wout, gb, o, r4):
    s = x.shape[0]
    t = TILE
    n = s // t
    pairs = FOX_HEADS // 2

    def body(x_ref, t_ref, ma_ref, mb_ref, mm_ref, w_ref, gb_ref, o_ref, r_ref,
             dy_ref, dma_ref, dmm_ref, dw_ref, loss_ref, doa_ref, dgb_ref, rr_ref, mix_ref):
        @pl.when(pl.program_id(0) == 0)
        def _():
            dw_ref[...] = jnp.zeros_like(dw_ref)
            loss_ref[...] = jnp.zeros_like(loss_ref)

        mix_ref[:, 0:256] = ma_ref[...]
        mix_ref[:, 256:768] = mb_ref[...]
        mix_ref[:, 768:1024] = mm_ref[...]
        mix = mix_ref[...]
        err = (x_ref[...] + _dot(mix, w_ref[...])) - t_ref[...]
        row_mean = jnp.sum(err * err, axis=-1, keepdims=True) * (1.0 / D_MODEL)
        loss_ref[...] += 0.5 * jnp.sum(row_mean, axis=0, keepdims=True)
        dy = err * (1.0 / D_MODEL)
        dy_ref[...] = dy
        dyb = dy.astype(BF16)
        dmix = _dot(dyb, w_ref[...], NT)
        dma_ref[...] = dmix[:, 0:256]
        dmm_ref[...] = dmix[:, 768:1024]
        dw_ref[...] += _dot(mix, dyb, TN)

        lane = lax.broadcasted_iota(jnp.int32, (t, LANES), 1)
        lo = lane < HEAD_DIM
        d_os = []
        delta = jnp.zeros((t, LANES), F32)
        for p in range(pairs):
            sl = slice(p * LANES, (p + 1) * LANES)
            g = gb_ref[:, sl]
            sg = _sig(g)
            dm = dmix[:, 256 + p * LANES:256 + (p + 1) * LANES]
            ov = o_ref[:, sl]
            d_o = dm * (g * sg)
            d_os.append(d_o)
            dgb_ref[:, sl] = (dm * ov * (sg * (1.0 + g * (1.0 - sg)))).astype(BF16)
            prod = d_o * ov
            delta = jnp.where(lane == 2 * p, jnp.sum(jnp.where(lo, prod, 0.0), axis=-1, keepdims=True), delta)
            delta = jnp.where(lane == 2 * p + 1, jnp.sum(jnp.where(lo, 0.0, prod), axis=-1, keepdims=True), delta)
            rr_ref[p, 0] = r_ref[p].T[0:8, :]
        minus_delta = _spread3(-delta)
        for h in range(FOX_HEADS):
            blk = slice(h * LANES, (h + 1) * LANES)
            doa_ref[:, blk] = _head_block(d_os[h // 2], h % 2, lo, minus_delta[:, blk])

    return pl.pallas_call(
        body, name="out_loss", grid=(n,),
        out_shape=(jax.ShapeDtypeStruct((s, D_MODEL), F32), jax.ShapeDtypeStruct((s, 256), F32),
                   jax.ShapeDtypeStruct((s, 256), F32), jax.ShapeDtypeStruct((D_MODEL, D_MODEL), F32),
                   jax.ShapeDtypeStruct((1, LANES), F32), jax.ShapeDtypeStruct((s, HEAD_BLOCKS), BF16),
                   jax.ShapeDtypeStruct((s, FOX_WIDTH), BF16), jax.ShapeDtypeStruct((pairs, n, 8, t), F32)),
        in_specs=[_rows(t, D_MODEL), _rows(t, D_MODEL), _rows(t, 256), _rows(t, 512), _rows(t, 256),
                  _full((D_MODEL, D_MODEL)), _rows(t, FOX_WIDTH), _rows(t, FOX_WIDTH),
                  pl.BlockSpec((pairs, t, LANES), lambda i: (0, i, 0))],
        out_specs=(_rows(t, D_MODEL), _rows(t, 256), _rows(t, 256), _full((D_MODEL, D_MODEL)), _full((1, LANES)),
                   _rows(t, HEAD_BLOCKS), _rows(t, FOX_WIDTH), pl.BlockSpec((pairs, 1, 8, t), lambda i: (0, i, 0, 0))),
        scratch_shapes=[pltpu.VMEM((t, D_MODEL), BF16)],
        compiler_params=_params(),
    )(x, tgt, ma, mb, mm, wout, gb, o, r4)


def _side_bwd(pa, db, dma, w4, pscale, pm, dmm, kmn, vmb, mq_g, kv, mnb, mem, w_kv, mk_g, mem_norm_g):
    s = pa.shape[0]
    t = TILE
    n = s // t
    ext = t + POOL_HALO
    nm = mem.shape[0]

    def body(pa_ref, d_ref, dma_ref, w4_ref, sc_ref, pm_ref, dmm_ref, k_ref, v_ref, g_ref,
             kv_ref, mn_ref, mem_ref, wkv_ref, kg_ref, mg_ref,
             dpa_ref, dpm_ref, dw4_ref, dsc_ref, dg_ref, dwkv_ref, dmg_ref, dkg_ref,
             ext_ref, w_ref, dw_ref, dk_ref, dv_ref, gacc_ref, dkv_ref):
        i = pl.program_id(0)

        @pl.when(i == 0)
        def _():
            dw_ref[...] = jnp.zeros_like(dw_ref)
            dsc_ref[...] = jnp.zeros_like(dsc_ref)
            ext_ref[t:ext, :] = jnp.zeros((POOL_HALO, POOL_WIDTH), F32)
            w_ref[...] = _pool_block_diag(w4_ref[...])
            dk_ref[...] = jnp.zeros_like(dk_ref)
            dv_ref[...] = jnp.zeros_like(dv_ref)
            gacc_ref[...] = jnp.zeros_like(gacc_ref)

        dbv = d_ref[...]
        z = _dot(dbv, w_ref[...])
        ga = pa_ref[:, POOL_WIDTH:2 * POOL_WIDTH]
        sg = _sig(ga)
        dma_v = dma_ref[...]
        dya = dma_v * (ga * sg)
        dpa_ref[:, POOL_WIDTH:2 * POOL_WIDTH] = (dma_v * (z * sc_ref[...]) * (sg * (1.0 + ga * (1.0 - sg)))).astype(BF16)
        dsc_ref[...] += jnp.sum(dya * z, axis=0, keepdims=True)
        dzb = (dya * sc_ref[...]).astype(BF16)
        dw_ref[...] += _dot(dbv, dzb, TN)
        dd = _dot(dzb, w_ref[...], NT)
        lane = lax.broadcasted_iota(jnp.int32, (t, POOL_WIDTH), 1)
        pos = (lax.broadcasted_iota(jnp.int32, (t, POOL_WIDTH), 0) + ((n - 1 - i) * t + 1)).astype(F32)
        ext_ref[0:t, :] = dd / jnp.minimum(pos, _pool_window(lane))
        e = ext_ref[...]
        s2 = e + pltpu.roll(e, ext - 1, axis=0)
        s4 = s2 + pltpu.roll(s2, ext - 2, axis=0)
        s8 = s4 + pltpu.roll(s4, ext - 4, axis=0)
        s16 = s8 + pltpu.roll(s8, ext - 8, axis=0)
        lane_e = lax.broadcasted_iota(jnp.int32, (ext, POOL_WIDTH), 1)
        win = _pool_pick(lane_e, s2, s4, s8, s16)[0:t, :]
        dpa_ref[:, 0:POOL_WIDTH] = (win - dd).astype(BF16)
        ext_ref[t:ext, :] = ext_ref[0:POOL_HALO, :]

        lo = _lane_lo((t, LANES))
        pairs = MEM_WIDTH // LANES
        pre = []
        for p in range(pairs):
            sl = slice(p * LANES, (p + 1) * LANES)
            qb = pm_ref[:, sl]
            rr = _head_rms(qb, lo)
            qhat = qb * rr
            g = g_ref[:, sl]
            qs = ((qhat * g) * ATT_SCALE).astype(BF16)
            gm = pm_ref[:, MEM_WIDTH + p * LANES:MEM_WIDTH + (p + 1) * LANES]
            sg = _sig(gm)
            dmo = dmm_ref[:, sl]
            pre.append((sl, rr, qhat, g, qs, gm, sg, dmo, dmo * (gm * sg)))

        def front(p, hh):
            sl, _, _, _, qs, _, _, _, d_o = pre[p]
            msk = lo if hh == 0 else jnp.logical_not(lo)
            qm = jnp.where(msk, qs, jnp.zeros_like(qs))
            prob = _mem_softmax(qm, k_ref[:, sl])
            dom = jnp.where(msk, d_o, 0.0).astype(BF16)
            return qm, prob, dom, _dot(dom, v_ref[:, sl], NT)

        def back(p, qm, prob, dom, dp):
            sl = pre[p][0]
            pb = prob.astype(BF16)
            out = _dot(pb, v_ref[:, sl])
            ds = (prob * (dp - jnp.sum(prob * dp, axis=-1, keepdims=True))).astype(BF16)
            dq = _dot(ds, k_ref[:, sl])
            dk_ref[:, sl] += _dot(ds, qm, TN)
            dv_ref[:, sl] += _dot(pb, dom, TN)
            return out, dq

        heads = [(p, hh) for p in range(pairs) for hh in range(2)]
        done = []
        ahead = front(*heads[0])
        for k, (p, _) in enumerate(heads):
            now, ahead = ahead, (front(*heads[k + 1]) if k + 1 < len(heads) else None)
            done.append(back(p, *now))
        for p in range(pairs):
            sl, rr, qhat, g, _, gm, sg, dmo, _ = pre[p]
            outs, dqs = zip(done[2 * p], done[2 * p + 1])
            o = jnp.where(lo, outs[0], outs[1])
            dqn = jnp.where(lo, dqs[0], dqs[1]) * ATT_SCALE
            dpm_ref[:, sl] = _head_norm_bwd(dqn, qhat, rr, g, lo).astype(BF16)
            dpm_ref[:, MEM_WIDTH + p * LANES:MEM_WIDTH + (p + 1) * LANES] = (
                dmo * o * (sg * (1.0 + gm * (1.0 - sg)))).astype(BF16)
            gacc_ref[:, sl] += jnp.sum(dqn * qhat, axis=0, keepdims=True)

        @pl.when(i == n - 1)
        def _():
            own = jnp.where(_same_group((POOL_WIDTH, POOL_WIDTH)), dw_ref[...], 0.0)
            dw4_ref[...] = jnp.dot(own, _group_onehot((POOL_WIDTH, HEAD_DIM), True), preferred_element_type=F32,
                                   precision=lax.Precision.HIGHEST)
            dg_ref[...] = _fold_heads(gacc_ref[...])

            lo_m = _lane_lo((nm, LANES))
            kacc = []
            for p in range(MEM_WIDTH // LANES):
                sl = slice(p * LANES, (p + 1) * LANES)
                kb = kv_ref[:, sl]
                rr = _head_rms(kb, lo_m)
                khat = kb * rr
                dk = dk_ref[:, sl]
                dkv_ref[:, sl] = _head_norm_bwd(dk, khat, rr, kg_ref[:, sl], lo_m).astype(BF16)
                kacc.append(jnp.sum(dk * khat, axis=0, keepdims=True))
            dkg_ref[...] = _fold_heads(jnp.concatenate(kacc, axis=1))
            dkv_ref[:, MEM_WIDTH:] = dv_ref[...].astype(BF16)
            dkv = dkv_ref[...]
            dwkv_ref[...] = _dot(mn_ref[...], dkv, TN)
            dmn = _dot(dkv, wkv_ref[...], NT)
            xm = mem_ref[...]
            rr = lax.rsqrt(jnp.mean(xm * xm, axis=-1, keepdims=True) + EPS)
            dmg_ref[...] = jnp.sum(dmn * (xm * rr), axis=0, keepdims=True)

    def rev(w):
        return _rows_rev(t, w, n)

    row = jax.ShapeDtypeStruct((1, LANES), F32)
    return pl.pallas_call(
        body, name="side_bwd", grid=(n,),
        out_shape=(jax.ShapeDtypeStruct((s, 512), BF16), jax.ShapeDtypeStruct((s, 512), BF16),
                   jax.ShapeDtypeStruct((POOL_ROWS, HEAD_DIM), F32), jax.ShapeDtypeStruct((1, POOL_WIDTH), F32), row,
                   jax.ShapeDtypeStruct((D_MODEL, 2 * MEM_WIDTH), F32), jax.ShapeDtypeStruct((1, D_MODEL), F32), row),
        in_specs=[rev(512), rev(POOL_WIDTH), rev(POOL_WIDTH), _full((POOL_ROWS, HEAD_DIM)), _full((1, POOL_WIDTH)),
                  rev(512), rev(MEM_WIDTH), _full((N_MEM, MEM_WIDTH)), _full((N_MEM, MEM_WIDTH)), _full((1, MEM_WIDTH)),
                  _full((nm, 2 * MEM_WIDTH)), _full((nm, D_MODEL)), _full((nm, D_MODEL)),
                  _full((D_MODEL, 2 * MEM_WIDTH)), _full((1, MEM_WIDTH)), _full((1, D_MODEL))],
        out_specs=(rev(512), rev(512), _full((POOL_ROWS, HEAD_DIM)), _full((1, POOL_WIDTH)), _full((1, LANES)),
                   _full((D_MODEL, 2 * MEM_WIDTH)), _full((1, D_MODEL)), _full((1, LANES))),
        scratch_shapes=[pltpu.VMEM((ext, POOL_WIDTH), F32), pltpu.VMEM((POOL_WIDTH, POOL_WIDTH), BF16),
                        pltpu.VMEM((POOL_WIDTH, POOL_WIDTH), F32), pltpu.VMEM((N_MEM, MEM_WIDTH), F32),
                        pltpu.VMEM((N_MEM, MEM_WIDTH), F32), pltpu.VMEM((1, MEM_WIDTH), F32),
                        pltpu.VMEM((nm, 2 * MEM_WIDTH), BF16)],
        compiler_params=_params(),
    )(pa, db, dma, w4, pscale, pm, dmm, kmn, vmb, mq_g, kv, mnb, mem, w_kv, mk_g, mem_norm_g)


FOX_BWD_HEADS = 4


def _fox_bwd(ka, va, qa, doa, rr, gparts, axes):
    s = ka.shape[0]
    t = TILE
    n = s // t
    heads = FOX_BWD_HEADS
    groups = FOX_HEADS // heads
    group_w = heads * LANES
    na = len(gparts)

    def body(*refs):
        ka_ref, va_ref, qa_ref, doa_ref, rr_ref = refs[0:5]
        g_refs = refs[5:5 + na]
        dka_ref, dva_ref, dqa_ref = refs[5 + na:8 + na]
        out_refs = refs[8 + na:8 + 2 * na]
        bufs = tuple(refs[8 + (2 + k) * na:8 + (3 + k) * na] for k in range(5))
        j = pl.program_id(1)
        step_id = pl.program_id(0) * n + j
        red = _ShardReduce(g_refs, out_refs, axes, bufs, *refs[8 + 7 * na:]) if na else None

        @pl.when(j == 0)
        def _():
            dqa_ref[...] = jnp.zeros_like(dqa_ref)

        if red is not None:
            pl.when(step_id == 0)(red.exchange_with_sibling)

            @pl.when(step_id == 1)
            def _():
                for k in (1, 2, 3):
                    red.send_to_chip(k)
                red.keep_mine()

        causal = lax.broadcasted_iota(jnp.int32, (t, t), 0) <= lax.broadcasted_iota(jnp.int32, (t, t), 1)
        kas = [ka_ref[:, hh * LANES:(hh + 1) * LANES] for hh in range(heads)]
        vas = [va_ref[:, hh * LANES:(hh + 1) * LANES] for hh in range(heads)]

        def step(i, carry, masked):
            rows = pl.ds(pl.multiple_of(i * t, t), t)
            new = []
            for hh in range(heads):
                cols = slice(hh * LANES, (hh + 1) * LANES)
                dk_a, dv_a = carry[hh]
                qb = qa_ref[rows, cols]
                d_o = doa_ref[rows, cols]
                arg = _dot(kas[hh], qb, NT) - rr_ref[hh // 2, i, hh % 2:hh % 2 + 1, :]
                if masked:
                    arg = jnp.where(causal, arg, -1e30)
                pt = jnp.exp(arg)
                dst = (pt * _dot(vas[hh], d_o, NT)).astype(BF16)
                dv_a = dv_a + _dot(pt.astype(BF16), d_o)
                dk_a = dk_a + _dot(dst, qb)
                dqa_ref[rows, cols] += _dot(dst, kas[hh], TN)
                new.append((dk_a, dv_a))
            return tuple(new)

        zero = jnp.zeros((t, LANES), F32)
        carry = step(j, ((zero, zero),) * heads, masked=True)
        res = lax.fori_loop(j + 1, n, functools.partial(step, masked=False), carry)
        for hh in range(heads):
            cols = slice(hh * LANES, (hh + 1) * LANES)
            dka_ref[:, cols] = res[hh][0]
            dva_ref[:, cols] = res[hh][1]

        if red is not None:
            @pl.when(step_id == groups * n - 1)
            def _():
                red.sum_and_share()
                red.finish()

    tile_spec = pl.BlockSpec((t, group_w), lambda p, j: (j, p))
    full_spec = pl.BlockSpec((s, group_w), lambda p, j: (0, p))
    any_spec = pl.BlockSpec(memory_space=pl.ANY)
    scratch = _ShardReduce.scratch(gparts, axes)
    if na:
        scratch += [pltpu.SemaphoreType.DMA((_ShardReduce.SEMS * na,)), pltpu.SemaphoreType.DMA((_ShardReduce.SEMS * na,)),
                    pltpu.SemaphoreType.DMA((_ShardReduce.LOCAL * na,))]
    return pl.pallas_call(
        body, name="fox_bwd", grid=(groups, n),
        out_shape=(jax.ShapeDtypeStruct((s, HEAD_BLOCKS), F32),) * 3
        + tuple(jax.ShapeDtypeStruct(_shard_shape(g), F32) for g in gparts),
        in_specs=[tile_spec, tile_spec, full_spec, full_spec,
                  pl.BlockSpec((heads // 2, n, 8, t), lambda p, j: (p, 0, 0, 0))] + [any_spec] * na,
        out_specs=(tile_spec, tile_spec, full_spec) + (any_spec,) * na,
        scratch_shapes=scratch, compiler_params=_params(2, VMEM_LIMIT_FOX_BWD),
    )(ka, va, qa, doa, rr, *gparts)


def _fox_post_tile(i, n, t, dqa_ref, dka_ref, dva_ref, qk_ref, fb_ref, bf_ref, qg_ref, kg_ref,
                   dqk_ref, dv_ref, dfb_ref, dqg_ref, dkg_ref, dbf_ref, qacc_ref, kacc_ref, carry_ref,
                   between):
    @pl.when(i == 0)
    def _():
        qacc_ref[...] = jnp.zeros_like(qacc_ref)
        kacc_ref[...] = jnp.zeros_like(kacc_ref)
        dbf_ref[...] = jnp.zeros_like(dbf_ref)
        carry_ref[...] = jnp.zeros_like(carry_ref)

    lane = lax.broadcasted_iota(jnp.int32, (t, LANES), 1)
    row = lax.broadcasted_iota(jnp.int32, (t, LANES), 0)
    lo = lane < HEAD_DIM

    def head_blocks(ref, p):
        return ref[:, 2 * p * LANES:(2 * p + 1) * LANES], ref[:, (2 * p + 1) * LANES:(2 * p + 2) * LANES]

    def issue(k):
        if between[k] is not None:
            between[k]()

    sums = []
    pairs = FOX_WIDTH // LANES
    for side, (src_ref, g_ref, acc_ref, scale) in enumerate(((dqa_ref, qg_ref, qacc_ref, ATT_SCALE),
                                                             (dka_ref, kg_ref, kacc_ref, 1.0))):
        total = jnp.zeros((t, LANES), F32)
        for p in range(pairs):
            issue(side * pairs + p)
            sl = slice(p * LANES, (p + 1) * LANES)
            cols = slice(side * FOX_WIDTH + p * LANES, side * FOX_WIDTH + (p + 1) * LANES)
            if side == 0:
                dv_ref[:, sl] = _pair_block(*head_blocks(dva_ref, p), lo).astype(BF16)
            d0, d1 = head_blocks(src_ref, p)
            total = total + (d0 + d1)
            raw = qk_ref[:, cols]
            rr = _head_rms(raw, lo)
            xhat = raw * rr
            dn = _pair_block(d0, d1, lo) * scale
            dqk_ref[:, cols] = _head_norm_bwd(dn, xhat, rr, g_ref[:, sl], lo).astype(BF16)
            acc_ref[:, sl] += jnp.sum(dn * xhat, axis=0, keepdims=True)
        sums.append(total)
    issue(2 * pairs)
    dq_sum, dk_sum = sums

    acc = (pltpu.roll(dq_sum, LANES - KEY_SUM_LANE, axis=1) - pltpu.roll(dk_sum, LANES - QUERY_SUM_LANE, axis=1))
    acc = jnp.where(lane < FOX_HEADS, acc, 0.0)
    sh = 1
    while sh < t:
        acc = acc + jnp.where(row < t - sh, pltpu.roll(acc, t - sh, axis=0), 0.0)
        sh *= 2
    dlogf = acc + carry_ref[...]
    dfb_ref[...] = dlogf
    carry_ref[...] = dfb_ref[0:1, :]
    z = fb_ref[...] + bf_ref[...]
    dz = jnp.where(lane < FOX_HEADS, dlogf * (1.0 / (1.0 + jnp.exp(z))), 0.0)
    dfb_ref[...] = dz
    dbf_ref[...] += jnp.sum(dz, axis=0, keepdims=True)

    @pl.when(i == n - 1)
    def _():
        dqg_ref[...] = _fold_heads(qacc_ref[...])
        dkg_ref[...] = _fold_heads(kacc_ref[...])


def _assemble_dproj(dp_ref, dpa_ref, dqk_ref, dv_ref, dgb_ref, dpm_ref, dfb_ref):
    dp_ref[:, PA_LO:QB_LO] = dpa_ref[...]
    dp_ref[:, QB_LO:VB_LO] = dqk_ref[...]
    dp_ref[:, VB_LO:GB_LO] = dv_ref[...]
    dp_ref[:, GB_LO:PM_LO] = dgb_ref[...]
    dp_ref[:, PM_LO:FB_LO] = dpm_ref[...]
    dp_ref[:, FB_LO:PROJ_PAD] = dfb_ref[...].astype(BF16)


def _dproj_specs(t):
    return [_rows(t, 512), _rows(t, 2 * FOX_WIDTH), _rows(t, FOX_WIDTH), _rows(t, FOX_WIDTH), _rows(t, 512),
            _rows(t, LANES)]


IN_BWD_X_TILE = 256


def _in_bwd_x(x, dy, norm_g, wp, dparts, gparts, axes, smalls):
    s = x.shape[0]
    t = IN_BWD_X_TILE
    n = s // t
    na = len(gparts)
    n_dp = len(dparts)
    vec_leaves, loss_row, dw4 = smalls if smalls is not None else ((), None, None)
    nv = len(vec_leaves)
    n_small = nv + 2 if smalls is not None else 0
    small_base = _ShardReduce.SEMS * na

    def body(*refs):
        x_ref, dy_ref, g_ref, wp_ref = refs[0:4]
        dp_parts = refs[4:4 + n_dp]
        o = 4 + n_dp
        g_refs = refs[o:o + na]
        small_in = refs[o + na:o + na + n_small]
        o += na + n_small
        gx_ref, dg_ref = refs[o:o + 2]
        out_refs = refs[o + 2:o + 2 + na]
        small_out = refs[o + 2 + na:o + 2 + na + (2 if smalls is not None else 0)]
        o += 2 + na + len(small_out)
        dp_ref = refs[o]
        bufs = tuple(refs[o + 1 + k * na:o + 1 + (k + 1) * na] for k in range(5))
        rest = refs[o + 1 + 5 * na:]

        i = pl.program_id(0)
        if na or smalls is not None:
            send_sems, recv_sems, local_sems = rest[-3:]
        red = _ShardReduce(g_refs, out_refs, axes, bufs, send_sems, recv_sems, local_sems) if na else None

        @pl.when(i == 0)
        def _():
            dg_ref[...] = jnp.zeros_like(dg_ref)
            if red is not None:
                red.exchange_with_sibling()

        if red is not None:
            for k in (1, 2, 3):
                pl.when(i == k)(functools.partial(red.send_to_chip, k))
            pl.when(i == 4)(red.keep_mine)

        _assemble_dproj(dp_ref, *dp_parts)
        dh = _dot(dp_ref[...], wp_ref[...])
        xv = x_ref[...]
        rr = lax.rsqrt(jnp.mean(xv * xv, axis=-1, keepdims=True) + EPS)
        xhat = xv * rr
        scaled = dh * g_ref[...]
        gx_ref[...] = dy_ref[...] + rr * (scaled - xhat * jnp.mean(xhat * scaled, axis=-1, keepdims=True))
        dg_ref[...] += jnp.sum(dh * xhat, axis=0, keepdims=True)

        def small_all_reduce():
            leaf_refs, (loss_ref, dw4_ref) = small_in[0:nv], small_in[nv:]
            vec_out, dw4_out = small_out
            vec_mine, vec_recv, dw4_recv = rest[0:3]
            cx, cy, c = _my_place()
            me_lin = 4 * cx + 2 * cy + c

            def copy(k, src, dst, base):
                peer = (me_lin + k) % 8
                return pltpu.make_async_remote_copy(
                    src_ref=src, dst_ref=dst.at[me_lin], send_sem=send_sems.at[base + k - 1],
                    recv_sem=recv_sems.at[base + k - 1], device_id=(peer // 4, (peer // 2) % 2, peer % 2),
                    device_id_type=MESH)

            vec_mine[...] = jnp.zeros_like(vec_mine)
            vec_mine[0:1, :] = dg_ref[...]
            for (_, row, _), ref in zip(VEC_LEAVES[1:], leaf_refs):
                vec_mine[row:row + 1, 0:ref.shape[1]] = ref[...]
            vec_mine[VEC_LOSS_ROW:VEC_LOSS_ROW + 1, 0:LANES] = loss_ref[...]
            copies = [copy(k, src, dst, base) for k in range(1, 8)
                      for src, dst, base in ((vec_mine, vec_recv, small_base), (dw4_ref, dw4_recv, small_base + 7))]
            for cp in copies:
                cp.start()
            for cp in copies:
                cp.wait_recv()
            vec_recv[me_lin] = vec_mine[...]
            dw4_recv[me_lin] = dw4_ref[...]
            vtot, wtot = vec_recv[0], dw4_recv[0]
            for d in range(1, 8):
                vtot = vtot + vec_recv[d]
                wtot = wtot + dw4_recv[d]
            vec_out[...] = vtot
            dw4_out[...] = wtot
            for cp in copies:
                cp.wait_send()

        @pl.when(i == n - 1)
        def _():
            if red is not None:
                red.sum_and_share()
            if smalls is not None:
                small_all_reduce()
            if red is not None:
                red.finish()

    any_spec = pl.BlockSpec(memory_space=pl.ANY)
    scratch = [pltpu.VMEM((t, PROJ_PAD), BF16)] + _ShardReduce.scratch(gparts, axes)
    out_shape = [jax.ShapeDtypeStruct((s, D_MODEL), F32), jax.ShapeDtypeStruct((1, D_MODEL), F32)]
    out_shape += [jax.ShapeDtypeStruct(_shard_shape(g), F32) for g in gparts]
    out_specs = [_rows(t, D_MODEL), _full((1, D_MODEL))] + [any_spec] * na
    small_args = []
    if smalls is not None:
        small_args = [*vec_leaves, loss_row, dw4]
        out_shape += [jax.ShapeDtypeStruct((VEC_ROWS, D_MODEL), F32), jax.ShapeDtypeStruct(dw4.shape, F32)]
        out_specs += [_full((VEC_ROWS, D_MODEL)), _full(dw4.shape)]
        scratch += [pltpu.VMEM((VEC_ROWS, D_MODEL), F32), pltpu.VMEM((8, VEC_ROWS, D_MODEL), F32),
                    pltpu.VMEM((8,) + dw4.shape, F32)]
    if na or smalls is not None:
        n_sems = small_base + 14
        scratch += [pltpu.SemaphoreType.DMA((n_sems,)), pltpu.SemaphoreType.DMA((n_sems,)),
                    pltpu.SemaphoreType.DMA((max(_ShardReduce.LOCAL * na, 1),))]
    return pl.pallas_call(
        body, name="in_bwd_x", grid=(n,), out_shape=tuple(out_shape),
        in_specs=[_rows(t, D_MODEL), _rows(t, D_MODEL), _full((1, D_MODEL)),
                  pl.BlockSpec((PROJ_PAD, D_MODEL), lambda i: (0, 0), pipeline_mode=pl.Buffered(1))]
        + _dproj_specs(t) + [any_spec] * na + [_full(a.shape) for a in small_args],
        out_specs=tuple(out_specs), scratch_shapes=scratch, compiler_params=_params(),
    )(x, dy, norm_g, wp, *dparts, *gparts, *small_args)


def _in_bwd_w(hb, dpa, dgb, dpm, fox):
    s = hb.shape[0]
    t = TILE
    n = s // t
    f_hi = F_ORIG_LO + FOX_HEADS
    n_in = 4 + len(fox)

    def body(*refs):
        h_ref, dpa_ref, dgb_ref, dpm_ref = refs[0:4]
        fox_refs = refs[4:n_in]
        dw_ref, dqk_ref, dv_ref, dfb_ref, dqg_ref, dkg_ref, dbf_ref = refs[n_in:n_in + 7]
        fox_scratch = refs[n_in + 7:]
        i = pl.program_id(0)

        @pl.when(i == 0)
        def _():
            dw_ref[...] = jnp.zeros_like(dw_ref)

        hv = h_ref[...]

        def rows_of(lo, ref, cols=slice(None)):
            def add():
                dproj = ref[:, cols]
                dw_ref[lo:lo + dproj.shape[1], :] += _dot(dproj, hv, TN)
            return add

        q_cols, k_cols = slice(0, FOX_WIDTH), slice(FOX_WIDTH, 2 * FOX_WIDTH)
        between = (rows_of(0, dpa_ref), rows_of(f_hi, dgb_ref), rows_of(f_hi + FOX_WIDTH, dpm_ref), None,
                   rows_of(QB_LO, dqk_ref, q_cols), rows_of(VB_LO, dv_ref), None, None, rows_of(KB_LO, dqk_ref, k_cols))
        _fox_post_tile(i, n, t, *fox_refs, dqk_ref, dv_ref, dfb_ref, dqg_ref, dkg_ref, dbf_ref, *fox_scratch, between)
        dw_ref[F_ORIG_LO:f_hi, :] += _dot(dfb_ref[...].astype(BF16), hv, TN)[0:FOX_HEADS, :]

    def rev(w):
        return _rows_rev(t, w, n)

    row = jax.ShapeDtypeStruct((1, LANES), F32)
    return pl.pallas_call(
        body, name="in_bwd_w", grid=(n,),
        out_shape=(jax.ShapeDtypeStruct((IN_WIDTH, D_MODEL), F32), jax.ShapeDtypeStruct((s, 2 * FOX_WIDTH), BF16),
                   jax.ShapeDtypeStruct((s, FOX_WIDTH), BF16), jax.ShapeDtypeStruct((s, LANES), F32), row, row, row),
        in_specs=[rev(D_MODEL), rev(512), rev(FOX_WIDTH), rev(512), rev(HEAD_BLOCKS), rev(HEAD_BLOCKS),
                  rev(HEAD_BLOCKS), rev(2 * FOX_WIDTH), rev(LANES), _full((1, LANES)), _full((1, FOX_WIDTH)),
                  _full((1, FOX_WIDTH))],
        out_specs=(pl.BlockSpec((IN_WIDTH, D_MODEL), lambda i: (0, 0), pipeline_mode=pl.Buffered(1)),
                   rev(2 * FOX_WIDTH), rev(FOX_WIDTH), rev(LANES), _full((1, LANES)), _full((1, LANES)),
                   _full((1, LANES))),
        scratch_shapes=[pltpu.VMEM((1, FOX_WIDTH), F32), pltpu.VMEM((1, FOX_WIDTH), F32), pltpu.VMEM((1, LANES), F32)],
        compiler_params=_params(),
    )(hb, dpa, dgb, dpm, *fox)


def _adamw_math(w_ref, gv, m_ref, v_ref, d_ref, nm_ref, nv_ref):
    nm = ADAM_B1 * m_ref[...] + (1.0 - ADAM_B1) * gv
    nv = ADAM_B2 * v_ref[...] + (1.0 - ADAM_B2) * (gv * gv)
    m_hat = nm / (1.0 - ADAM_B1 ** ADAM_STEP)
    v_hat = nv / (1.0 - ADAM_B2 ** ADAM_STEP)
    d_ref[...] = -ADAM_LR * (m_hat / (jnp.sqrt(v_hat) + ADAM_EPS) + ADAM_WD * w_ref[...])
    nm_ref[...] = nm
    nv_ref[...] = nv


def _adamw_flat(name, w, g, m, v):
    rows, cols = g.shape
    per_row = cols // LANES

    def body(w_ref, g_ref, m_ref, v_ref, gf_ref, d_ref, nm_ref, nv_ref):
        for k in range(per_row):
            gf_ref[pl.ds(k, rows, stride=per_row), :] = g_ref[:, k * LANES:(k + 1) * LANES]
        _adamw_math(w_ref, gf_ref[...], m_ref, v_ref, d_ref, nm_ref, nv_ref)

    def whole(shape):
        return pl.BlockSpec(shape, lambda i: (0, 0), pipeline_mode=pl.Buffered(1))

    return pl.pallas_call(
        body, name=name, grid=(1,),
        out_shape=(jax.ShapeDtypeStruct(w.shape, F32),) * 4,
        in_specs=[whole(w.shape), whole(g.shape), whole(w.shape), whole(w.shape)], out_specs=(whole(w.shape),) * 4,
        compiler_params=_params(),
    )(w, g, m, v)


def _adamw_rest(vec, dw4, leaves, pool, shards):
    nl = len(VEC_LEAVES) + 1
    ns = len(shards)

    def body(*refs):
        vec_ref, dw4_ref = refs[0:2]
        wmv = refs[2:2 + 3 * nl]
        shard_in = refs[2 + 3 * nl:2 + 3 * nl + 4 * ns]
        o = 2 + 3 * nl + 4 * ns
        loss_ref = refs[o]
        outs = refs[o + 1:o + 1 + 4 * nl]
        shard_out = refs[o + 1 + 4 * nl:]
        loss_ref[...] = vec_ref[VEC_LOSS_ROW:VEC_LOSS_ROW + 1, 0:1]
        for k in range(nl):
            if k < nl - 1:
                _, row, width = VEC_LEAVES[k]
                gv = vec_ref[row:row + 1, 0:width]
            else:
                gv = dw4_ref[...]
            w_ref, m_ref, v_ref = wmv[3 * k:3 * k + 3]
            g_ref, d_ref, nm_ref, nv_ref = outs[4 * k:4 * k + 4]
            g_ref[...] = gv
            _adamw_math(w_ref, gv, m_ref, v_ref, d_ref, nm_ref, nv_ref)
        for k in range(ns):
            w_ref, g_ref, m_ref, v_ref = shard_in[4 * k:4 * k + 4]
            _adamw_math(w_ref, g_ref[...], m_ref, v_ref, *shard_out[3 * k:3 * k + 3])

    shapes = [jax.ShapeDtypeStruct((1, width), F32) for _, _, width in VEC_LEAVES] + [
        jax.ShapeDtypeStruct(dw4.shape, F32)]
    flat_in = [a for triple in list(leaves) + [pool] for a in triple] + [a for quad in shards for a in quad]
    res = pl.pallas_call(
        body, name="adamw_rest",
        out_shape=(jax.ShapeDtypeStruct((1, 1), F32),) + tuple(s for s in shapes for _ in range(4))
        + tuple(jax.ShapeDtypeStruct(quad[0].shape, F32) for quad in shards for _ in range(3)),
        compiler_params=pltpu.CompilerParams(vmem_limit_bytes=VMEM_LIMIT),
    )(vec, dw4, *flat_in)
    per = [res[1 + 4 * k:5 + 4 * k] for k in range(nl)]
    big = res[1 + 4 * nl:]
    return (res[0], [p[0] for p in per], [p[1] for p in per], [p[2] for p in per], [p[3] for p in per],
            [big[3 * k:3 * k + 3] for k in range(ns)])


def _tile_heads(g, n):
    return jnp.tile(g.reshape(1, HEAD_DIM), (1, n))


def kernel(x, mem, norm_g, w_in, b_f, w_pool, pool_scale, fox_q_g, fox_k_g, mem_norm_g, w_mem_kv, mem_q_g, mem_k_g, w_out, loss_target, m_norm_g, m_w_in, m_b_f, m_w_pool, m_pool_scale, m_fox_q_g, m_fox_k_g, m_mem_norm_g, m_w_mem_kv, m_mem_q_g, m_mem_k_g, m_w_out, v_norm_g, v_w_in, v_b_f, v_w_pool, v_pool_scale, v_fox_q_g, v_fox_k_g, v_mem_norm_g, v_w_mem_kv, v_mem_q_g, v_mem_k_g, v_w_out):
    w_in_t = w_in[0].T
    axes = (1, 0, 0)

    g_in, g_kv, g_out = _all_gather_weights([w_in_t, w_mem_kv[0], w_out[0]], axes)
    tiled = _tiled_params(b_f, fox_q_g, fox_k_g, mem_q_g, mem_k_g)
    fwd, wp = _fwd_in(x[0], norm_g, g_in, *tiled[0:3])
    w_kv_b = g_kv.reshape(D_MODEL, 2 * MEM_WIDTH)
    w_out_b = g_out.reshape(D_MODEL, D_MODEL)
    w4 = w_pool.reshape(POOL_ROWS, HEAD_DIM)
    dy, hb, dpa, dgb, dpm, fox, g_w_kv, g_w_out, (dmemnorm_g, dpscale, dmq_g, dmk_g), loss_row, dw4 = _local_partials(
        x[0], mem[0], loss_target[0], fwd, w_kv_b, w_out_b, tiled, w4, pool_scale, mem_norm_g, axes[1:])
    dwp, dqk, dvb, dfb, dfq_g, dfk_g, dbf = _in_bwd_w(hb, dpa, dgb, dpm, fox)
    dparts = (dpa, dqk, dvb, dgb, dpm, dfb)
    vec_leaves = (dmemnorm_g, dpscale, dbf, dfq_g, dfk_g, dmq_g, dmk_g)
    grad_x, _, g_w_in_t, vec, dw4_sum = _in_bwd_x(
        x[0], dy, norm_g, wp, dparts, [dwp], axes[0:1], (vec_leaves, loss_row, dw4))

    small_wmv = [(norm_g, m_norm_g, v_norm_g), (mem_norm_g, m_mem_norm_g, v_mem_norm_g),
                 (pool_scale, m_pool_scale, v_pool_scale), (b_f, m_b_f, v_b_f), (fox_q_g, m_fox_q_g, v_fox_q_g),
                 (fox_k_g, m_fox_k_g, v_fox_k_g), (mem_q_g, m_mem_q_g, v_mem_q_g), (mem_k_g, m_mem_k_g, v_mem_k_g)]
    pool_wmv = tuple(a.reshape(POOL_ROWS, HEAD_DIM) for a in (w_pool, m_w_pool, v_w_pool))
    loss, *small_out, (upd_kv, upd_out) = _adamw_rest(
        vec, dw4_sum, small_wmv, pool_wmv, [(w_mem_kv[0], g_w_kv, m_w_mem_kv[0], v_w_mem_kv[0]),
                                             (w_out[0], g_w_out, m_w_out[0], v_w_out[0])])
    tiles = D_MODEL // LANES

    def flat(a):
        return a.reshape(tiles, LANES, -1).transpose(2, 0, 1).reshape(-1, LANES)

    def unflat(a):
        return a.reshape(-1, tiles, LANES).transpose(1, 2, 0).reshape(w_in.shape)

    g_in_flat, *upd_in = _adamw_flat("adamw_w_in", flat(w_in), g_w_in_t, flat(m_w_in), flat(v_w_in))
    big = [[unflat(g_in_flat), g_w_kv[None], g_w_out[None]]]
    big += [[unflat(upd_in[k]), upd_kv[k][None], upd_out[k][None]] for k in range(3)]

    def leaves(k):
        sm = small_out[k]
        b_in, b_kv, b_out = big[k]
        return (sm[0], b_in, sm[3], sm[8].reshape(w_pool.shape), sm[2], sm[4], sm[5], sm[1], b_kv, sm[6], sm[7], b_out)

    return (loss.reshape(()), grad_x[None], *leaves(0), *leaves(1), *leaves(2), *leaves(3))


def _tiled_params(b_f, fox_q_g, fox_k_g, mem_q_g, mem_k_g):
    return (jnp.pad(b_f, ((0, 0), (0, LANES - FOX_HEADS))), _tile_heads(fox_q_g, FOX_HEADS),
            _tile_heads(fox_k_g, FOX_HEADS), _tile_heads(mem_q_g, 4), _tile_heads(mem_k_g, 4))


def _local_partials(xs, mems, tgt, fwd, w_kv_b, w_out_b, tiled, w4, pool_scale, mem_norm_g, axes):
    hb, pa, qk, qa, ka, va, gb, pm, fb = fwd
    bf_pad, fq_g, fk_g, mq_g, mk_g = tiled

    ma, db, mm, mnb, kv, kmn, vmb = _side_fwd(pa, pm, w4, pool_scale, mq_g, mems, mem_norm_g, w_kv_b, mk_g)
    o, mb, r4 = _fox_fwd(qa, ka, va, gb)
    dy, dma, dmm, dw_out, loss_row, doa, dgb, rr = _out_loss(xs, tgt, ma, mb, mm, w_out_b, gb, o, r4)

    dpa, dpm, dw4, dpscale, dmq_g, dw_kv, dmemnorm_g, dmk_g = _side_bwd(
        pa, db, dma, w4, pool_scale, pm, dmm, kmn, vmb, mq_g, kv, mnb, mems, w_kv_b, mk_g, mem_norm_g)
    if axes:
        parts = [dw_kv.reshape(4, D_MODEL // 4, 2 * MEM_WIDTH), dw_out.reshape(4, D_MODEL // 4, D_MODEL)]
        dka, dva, dqa, dw_kv, dw_out = _fox_bwd(ka, va, qa, doa, rr, parts, axes)
    else:
        dka, dva, dqa = _fox_bwd(ka, va, qa, doa, rr, [], ())
    fox = (dqa, dka, dva, qk, fb, bf_pad, fq_g, fk_g)
    return dy, hb, dpa, dgb, dpm, fox, dw_kv, dw_out, (dmemnorm_g, dpscale, dmq_g, dmk_g), loss_row, dw4
```

```python
import functools

import jax
import jax.numpy as jnp
from jax import lax
from jax.experimental import pallas as pl
from jax.experimental.pallas import tpu as pltpu

F32 = jnp.float32
BF16 = jnp.bfloat16
MESH = pl.DeviceIdType.MESH

D_MODEL = 1024
HEAD_DIM = 64
POOL_WIDTH = 256
FOX_WIDTH = 512
FOX_HEADS = 8
MEM_WIDTH = 256
N_MEM = 256
IN_WIDTH = 3080
EPS = 1e-6
ATT_SCALE = 0.125

ADAM_LR = 0.001
ADAM_B1 = 0.9
ADAM_B2 = 0.999
ADAM_EPS = 1e-08
ADAM_WD = 0.01
ADAM_STEP = 10

LANES = 128
PA_LO, QB_LO, KB_LO, VB_LO, GB_LO, PM_LO, FB_LO, PROJ_PAD = 0, 512, 1024, 1536, 2048, 2560, 3072, 3200
F_ORIG_LO = 2048

TILE = 512
VMEM_LIMIT = 56 * 1024 * 1024
VMEM_LIMIT_FOX_BWD = 58 * 1024 * 1024

VEC_LEAVES = (("norm_g", 0, 1024), ("mem_norm_g", 1, 1024), ("pool_scale", 2, 256), ("b_f", 3, 8),
              ("fox_q_g", 4, 64), ("fox_k_g", 5, 64), ("mem_q_g", 6, 64), ("mem_k_g", 7, 64))
VEC_LOSS_ROW = 8
VEC_ROWS = 16
POOL_ROWS = 256


def _params(n_grid=1, vmem=VMEM_LIMIT):
    return pltpu.CompilerParams(dimension_semantics=("arbitrary",) * n_grid, vmem_limit_bytes=vmem)


def _rows(t, w):
    return pl.BlockSpec((t, w), lambda i: (i, 0))


def _rows_rev(t, w, n):
    return pl.BlockSpec((t, w), lambda i: (n - 1 - i, 0))


def _full(shape):
    return pl.BlockSpec(shape, lambda i: (0,) * len(shape))


def _sig(x):
    return 1.0 / (1.0 + jnp.exp(-x))


def _lane_lo(shape):
    return lax.broadcasted_iota(jnp.int32, shape, 1) < HEAD_DIM


def _pair_sum(v, lo):
    s0 = jnp.sum(jnp.where(lo, v, 0.0), axis=-1, keepdims=True)
    s1 = jnp.sum(jnp.where(lo, 0.0, v), axis=-1, keepdims=True)
    return jnp.where(lo, s0, s1)


def _head_rms(blk, lo):
    return lax.rsqrt(_pair_sum(blk * blk, lo) * (1.0 / HEAD_DIM) + EPS)


def _head_norm_bwd(dyn, xhat, rr, g, lo):
    a = dyn * g
    return rr * (a - xhat * (_pair_sum(xhat * a, lo) * (1.0 / HEAD_DIM)))


def _fold_heads(acc):
    tot = acc[:, 0:LANES]
    for p in range(1, acc.shape[1] // LANES):
        tot = tot + acc[:, p * LANES:(p + 1) * LANES]
    return tot + pltpu.roll(tot, HEAD_DIM, axis=1)


def _lane_pick(v, lane, idx):
    return jnp.sum(jnp.where(lane == idx, v, 0.0), axis=-1, keepdims=True)


NT = (((1,), (1,)), ((), ()))
TN = (((0,), (0,)), ((), ()))


def _dot(a, b, dims=None):
    if dims is None:
        return jnp.dot(a, b, preferred_element_type=F32)
    return lax.dot_general(a, b, dims, preferred_element_type=F32)


def _my_place():
    return lax.axis_index("x"), lax.axis_index("y"), lax.axis_index("c")


def _half_dims(shape, axis):
    return (shape[0] // 2, shape[1]) if axis == 0 else (shape[0], shape[1] // 2)


def _shard_shape(g):
    return tuple(g.shape[1:]) if len(g.shape) == 3 else (g.shape[0] // 4, g.shape[1])


F32_ROWS = 8


def _shard_window(g):
    rows = _shard_shape(g)[0]
    if len(g.shape) == 3:
        return rows
    skew = max((j * rows) % F32_ROWS for j in range(4))
    return -(-(rows + skew) // F32_ROWS) * F32_ROWS


def _half_of(ref, axis, core, lead=False):
    rows, cols = ref.shape[-2:]
    if axis == 0:
        idx = (pl.ds(pl.multiple_of(core * (rows // 2), 16), rows // 2), slice(None))
    else:
        idx = (slice(None), pl.ds(pl.multiple_of(core * (cols // 2), LANES), cols // 2))
    return ref.at[(slice(None),) + idx] if lead else ref.at[idx]


class _HalfGather:
    def __init__(self, ins, outs, axes, f32_bufs, bf_bufs, send_sems, recv_sems, local_sems):
        self.ins, self.outs, self.axes = ins, outs, axes
        self.f32_bufs, self.bf_bufs = f32_bufs, bf_bufs
        self.send_sems, self.recv_sems, self.local_sems = send_sems, recv_sems, local_sems
        self.n = len(ins)
        x, y, self.c = _my_place()
        self.me, self.sibling = (x, y, self.c), (x, y, 1 - self.c)
        self.chips = [(1 - x, y), (x, 1 - y), (1 - x, 1 - y)]

    @staticmethod
    def scratch(shards, axes):
        dims = [_half_dims(a.shape, axis) for a, axis in zip(shards, axes)]
        n = len(shards)
        return [pltpu.VMEM(d, F32) for d in dims] + [pltpu.VMEM(d, BF16) for d in dims] + [
            pltpu.SemaphoreType.DMA((7 * n,)), pltpu.SemaphoreType.DMA((7 * n,)), pltpu.SemaphoreType.DMA((2 * n,))]

    @staticmethod
    def out_shapes(shards, axes):
        return tuple(jax.ShapeDtypeStruct((8,) + _half_dims(a.shape, axis), BF16) for a, axis in zip(shards, axes))

    def _blk(self, a, px, py, pc):
        return self.outs[a].at[4 * px + 2 * py + pc]

    def _copy(self, a, k, block, to, src=None):
        return pltpu.make_async_remote_copy(
            src_ref=self._blk(a, *block) if src is None else src, dst_ref=self._blk(a, *block),
            send_sem=self.send_sems.at[7 * a + k], recv_sem=self.recv_sems.at[7 * a + k], device_id=to,
            device_id_type=MESH)

    def _keep(self, a):
        return pltpu.make_async_copy(self.bf_bufs[a], self._blk(a, *self.me), self.local_sems.at[self.n + a])

    def _first(self, a):
        mine = [self._copy(a, 0, self.me, self.sibling, src=self.bf_bufs[a])]
        return mine + [self._copy(a, 1 + j, self.me, (*chip, self.c), src=self.bf_bufs[a])
                       for j, chip in enumerate(self.chips)]

    def send_mine(self):
        loads = [pltpu.make_async_copy(_half_of(self.ins[a], self.axes[a], self.c), self.f32_bufs[a],
                                       self.local_sems.at[a]) for a in range(self.n)]
        for cp in loads:
            cp.start()
        for a in range(self.n):
            loads[a].wait()
            self.bf_bufs[a][...] = self.f32_bufs[a][...].astype(BF16)
            self._keep(a).start()
            for cp in self._first(a):
                cp.start()

    def pass_on(self):
        for a in range(self.n):
            for j, chip in enumerate(self.chips):
                self._copy(a, 1 + j, (*chip, self.c), self.me).wait_recv()
                self._copy(a, 4 + j, (*chip, self.c), self.sibling).start()

    def finish(self):
        for a in range(self.n):
            self._copy(a, 0, self.sibling, self.me).wait_recv()
            for j, chip in enumerate(self.chips):
                self._copy(a, 4 + j, (*chip, 1 - self.c), self.me).wait_recv()
        for a in range(self.n):
            for cp in self._first(a):
                cp.wait_send()
            for j, chip in enumerate(self.chips):
                self._copy(a, 4 + j, (*chip, self.c), self.sibling).wait_send()
            self._keep(a).wait()


def _all_gather_weights(shards, axes):
    n = len(shards)

    def body(*refs):
        gather = _HalfGather(refs[0:n], refs[n:2 * n], axes, refs[2 * n:3 * n], refs[3 * n:4 * n], *refs[4 * n:])
        gather.send_mine()
        gather.pass_on()
        gather.finish()

    any_spec = pl.BlockSpec(memory_space=pl.ANY)
    return pl.pallas_call(
        body, name="weights_all_gather", out_shape=_HalfGather.out_shapes(shards, axes),
        in_specs=[any_spec] * n, out_specs=(any_spec,) * n, scratch_shapes=_HalfGather.scratch(shards, axes),
        compiler_params=pltpu.CompilerParams(vmem_limit_bytes=VMEM_LIMIT),
    )(*shards)


class _ShardReduce:
    SEMS = 8
    LOCAL = 5

    def __init__(self, g_refs, out_refs, axes, bufs, send_sems, recv_sems, local_sems):
        self.g_refs, self.out_refs, self.axes = g_refs, out_refs, axes
        self.recv_a, self.own_a, self.send_b, self.recv_b, self.fin = bufs
        self.send_sems, self.recv_sems, self.local_sems = send_sems, recv_sems, local_sems
        self.n = len(g_refs)
        x, y, self.c = _my_place()
        self.chip = 2 * x + y
        self.sibling = (x, y, 1 - self.c)

    @staticmethod
    def scratch(gparts, axes):
        assert all(len(g.shape) == 3 or axis == 1 for g, axis in zip(gparts, axes))
        dims = [_half_dims(_shard_shape(g), axis) for g, axis in zip(gparts, axes)]
        windows = [d if len(g.shape) == 3 else (_shard_window(g),) + d[1:] for g, d in zip(gparts, dims)]
        shapes = []
        for dtype, lead, per_array in ((F32, (4,), windows), (F32, (4,), windows), (BF16, (4,), dims),
                                       (BF16, (4,), dims), (F32, (), dims)):
            shapes += [pltpu.VMEM(lead + d, dtype) for d in per_array]
        return shapes

    def _shard_half(self, a, j, core):
        g = self.g_refs[a]
        if len(g.shape) == 3:
            return _half_of(g.at[j], self.axes[a], core)
        start = (j * _shard_shape(g)[0]) // F32_ROWS * F32_ROWS
        return _half_of(g.at[pl.ds(pl.multiple_of(start, F32_ROWS), _shard_window(g))], self.axes[a], core)

    def _to_sibling(self, a, j):
        return pltpu.make_async_remote_copy(
            src_ref=self._shard_half(a, j, 1 - self.c), dst_ref=self.recv_a[a].at[j],
            send_sem=self.send_sems.at[self.SEMS * a + j], recv_sem=self.recv_sems.at[self.SEMS * a + j], device_id=self.sibling,
            device_id_type=MESH)

    def _own(self, a, j):
        return pltpu.make_async_copy(self._shard_half(a, j, self.c), self.own_a[a].at[j],
                                     self.local_sems.at[self.LOCAL * a + j])

    def _to_chip(self, a, k):
        dest = (self.chip + k) % 4
        return pltpu.make_async_remote_copy(
            src_ref=self.send_b[a].at[dest], dst_ref=self.recv_b[a].at[self.chip],
            send_sem=self.send_sems.at[self.SEMS * a + 3 + k], recv_sem=self.recv_sems.at[self.SEMS * a + 3 + k],
            device_id=(dest // 2, dest % 2, self.c), device_id_type=MESH)

    def _give(self, a):
        return pltpu.make_async_remote_copy(
            src_ref=self.fin[a], dst_ref=_half_of(self.out_refs[a], self.axes[a], self.c),
            send_sem=self.send_sems.at[self.SEMS * a + 7], recv_sem=self.recv_sems.at[self.SEMS * a + 7], device_id=self.sibling,
            device_id_type=MESH)

    def _mine(self, a):
        return pltpu.make_async_copy(self.fin[a], _half_of(self.out_refs[a], self.axes[a], self.c),
                                     self.local_sems.at[self.LOCAL * a])

    def exchange_with_sibling(self):
        for k in (1, 2, 3, 0):
            j = (self.chip + k) % 4
            for a in range(self.n):
                self._to_sibling(a, j).start()
                self._own(a, j).start()

    def _chip_partial(self, a, j):
        self._own(a, j).wait()
        self._to_sibling(a, j).wait_recv()
        g = self.g_refs[a]
        if len(g.shape) == 3:
            self.send_b[a][j] = (self.own_a[a][j] + self.recv_a[a][j]).astype(BF16)
            return
        rows = _shard_shape(g)[0]
        for shard in range(4):
            @pl.when(j == shard)
            def _():
                at = pl.ds((shard * rows) % F32_ROWS, rows)
                self.send_b[a][shard] = (self.own_a[a][shard, at, :] + self.recv_a[a][shard, at, :]).astype(BF16)

    def send_to_chip(self, k):
        for a in range(self.n):
            self._chip_partial(a, (self.chip + k) % 4)
            self._to_chip(a, k).start()

    def keep_mine(self):
        for a in range(self.n):
            self._chip_partial(a, self.chip)
            keep = pltpu.make_async_copy(self.send_b[a].at[self.chip], self.recv_b[a].at[self.chip],
                                         self.local_sems.at[self.LOCAL * a + 4])
            keep.start()
            keep.wait()

    def sum_and_share(self):
        for a in range(self.n):
            for k in range(1, 4):
                self._to_chip(a, k).wait_recv()
            tot = self.recv_b[a][0].astype(F32) + self.recv_b[a][1].astype(F32)
            tot = tot + self.recv_b[a][2].astype(F32)
            self.fin[a][...] = tot + self.recv_b[a][3].astype(F32)
            self._give(a).start()
            self._mine(a).start()

    def finish(self):
        for a in range(self.n):
            self._give(a).wait_recv()
            self._mine(a).wait()
            self._give(a).wait_send()
            for j in range(4):
                self._to_sibling(a, j).wait_send()
            for k in range(1, 4):
                self._to_chip(a, k).wait_send()


def _mem_tokens_fwd(mem_ref, g_ref, w_ref, kg_ref, mn_ref, kv_ref, kn_ref, vm_ref):
    xm = mem_ref[...]
    rr = lax.rsqrt(jnp.mean(xm * xm, axis=-1, keepdims=True) + EPS)
    mnb = ((xm * rr) * g_ref[...]).astype(BF16)
    mn_ref[...] = mnb
    kv = _dot(mnb, w_ref[...])
    kv_ref[...] = kv
    lo = _lane_lo((xm.shape[0], LANES))
    for p in range(MEM_WIDTH // LANES):
        sl = slice(p * LANES, (p + 1) * LANES)
        kb = kv[:, sl]
        kn_ref[:, sl] = ((kb * _head_rms(kb, lo)) * kg_ref[:, sl]).astype(BF16)
    vm_ref[...] = kv[:, MEM_WIDTH:].astype(BF16)


AUG_LO = 64
KEY_SUM_LANE = 72
QUERY_SUM_LANE = 80
HEAD_BLOCKS = FOX_HEADS * LANES


def _ones3(lane):
    return jnp.where((lane >= AUG_LO) & (lane < AUG_LO + 3), 1.0, 0.0)


def _spread3(cols):
    hi = cols.astype(BF16)
    rest = cols - hi.astype(F32)
    mid = rest.astype(BF16)
    low = (rest - mid.astype(F32)).astype(BF16)
    r = lax.broadcasted_iota(jnp.int32, (LANES, HEAD_BLOCKS), 0)
    c = lax.broadcasted_iota(jnp.int32, (LANES, HEAD_BLOCKS), 1)
    out = None
    for k, part in enumerate((hi, mid, low)):
        term = _dot(part, jnp.where(c == r * LANES + (AUG_LO + k), 1.0, 0.0).astype(BF16))
        out = term if out is None else out + term
    return out


def _head_block(pair_blk, hh, lo, extras):
    src = pair_blk if hh == 0 else pltpu.roll(pair_blk, HEAD_DIM, axis=1)
    return jnp.where(lo, src, extras).astype(BF16)


def _pair_block(blk0, blk1, lo):
    return jnp.where(lo, blk0, pltpu.roll(blk1, HEAD_DIM, axis=1))


def _assemble_w_in(halves_ref, words_ref, wp_ref):
    shard = IN_WIDTH // 4
    half = D_MODEL // 2
    f_hi = F_ORIG_LO + FOX_HEADS
    for j in range(4):
        blocks = [pltpu.bitcast(halves_ref[2 * j + c], jnp.uint32) for c in range(2)]
        for lo, hi, to in ((0, F_ORIG_LO, PA_LO), (F_ORIG_LO, f_hi, FB_LO), (f_hi, IN_WIDTH, GB_LO)):
            a, b = max(lo, shard * j), min(hi, shard * (j + 1))
            if a < b:
                for c in range(2):
                    words_ref[(to + a - lo) // 2:(to + b - lo) // 2, c * half:(c + 1) * half] = (
                        blocks[c][(a - shard * j) // 2:(b - shard * j) // 2, :])
    pad_lo = (FB_LO + FOX_HEADS) // 2
    words_ref[pad_lo:, :] = jnp.zeros((PROJ_PAD // 2 - pad_lo, D_MODEL), jnp.uint32)
    wp_ref[...] = pltpu.bitcast(words_ref[...], BF16)


def _fwd_in(x, norm_g, halves, bf_pad, fq_g, fk_g):
    s = x.shape[0]
    t = TILE
    n = s // t

    def body(x_ref, ng_ref, halves_ref, bf_ref, qg_ref, kg_ref,
             h_ref, pa_ref, qk_ref, qa_ref, ka_ref, va_ref, gb_ref, pm_ref, fb_ref, wp_ref,
             carry_ref, fcol_ref, words_ref):
        @pl.when(pl.program_id(0) == 0)
        def _():
            carry_ref[...] = jnp.zeros_like(carry_ref)
            _assemble_w_in(halves_ref, words_ref, wp_ref)

        xv = x_ref[...]
        rr = lax.rsqrt(jnp.mean(xv * xv, axis=-1, keepdims=True) + EPS)
        hb = ((xv * rr) * ng_ref[...]).astype(BF16)
        h_ref[...] = hb

        def proj(lo, hi):
            return _dot(hb, wp_ref[lo:hi, :], NT)

        fb = proj(FB_LO, PROJ_PAD)
        fb_ref[...] = fb
        qk_ref[:, 0:FOX_WIDTH] = proj(QB_LO, KB_LO)

        lane = lax.broadcasted_iota(jnp.int32, (t, LANES), 1)
        row = lax.broadcasted_iota(jnp.int32, (t, LANES), 0)
        lo = lane < HEAD_DIM
        z = fb + bf_ref[...]
        lf = -(jnp.maximum(-z, 0.0) + jnp.log1p(jnp.exp(-jnp.abs(z))))
        lf = jnp.where(lane < FOX_HEADS, lf, 0.0)
        sh = 1
        while sh < t:
            lf = lf + jnp.where(row >= sh, pltpu.roll(lf, sh, axis=0), 0.0)
            sh *= 2
        fcum = lf + carry_ref[...]
        fcol_ref[...] = fcum
        carry_ref[...] = fcol_ref[t - 1:t, :]

        ones3 = _ones3(lane)
        minus_f = _spread3(-fcum)

        def head_blocks(seg, g_ref, out_ref, scale):
            for p in range(FOX_WIDTH // LANES):
                sl = slice(p * LANES, (p + 1) * LANES)
                blk = qk_ref[:, seg - QB_LO + p * LANES:seg - QB_LO + (p + 1) * LANES]
                normed = ((blk * _head_rms(blk, lo)) * g_ref[:, sl]) * scale
                for hh in range(2):
                    h = 2 * p + hh
                    if seg == QB_LO:
                        extras = jnp.where(lane == QUERY_SUM_LANE + h, 1.0, ones3)
                    else:
                        extras = jnp.where(lane == KEY_SUM_LANE + h, 1.0, minus_f[:, h * LANES:(h + 1) * LANES])
                    out_ref[:, h * LANES:(h + 1) * LANES] = _head_block(normed, hh, lo, extras)

        qk_ref[:, FOX_WIDTH:2 * FOX_WIDTH] = proj(KB_LO, VB_LO)
        pa_ref[...] = proj(PA_LO, QB_LO)
        head_blocks(QB_LO, qg_ref, qa_ref, ATT_SCALE)
        vraw = proj(VB_LO, GB_LO)
        gb_ref[...] = proj(GB_LO, PM_LO)
        head_blocks(KB_LO, kg_ref, ka_ref, 1.0)
        pm_ref[...] = proj(PM_LO, FB_LO)
        for h in range(FOX_HEADS):
            va_ref[:, h * LANES:(h + 1) * LANES] = _head_block(vraw[:, (h // 2) * LANES:(h // 2 + 1) * LANES], h % 2, lo, ones3)

    outs = (
        jax.ShapeDtypeStruct((s, D_MODEL), BF16),
        jax.ShapeDtypeStruct((s, 512), F32),
        jax.ShapeDtypeStruct((s, 2 * FOX_WIDTH), F32),
        jax.ShapeDtypeStruct((s, HEAD_BLOCKS), BF16),
        jax.ShapeDtypeStruct((s, HEAD_BLOCKS), BF16),
        jax.ShapeDtypeStruct((s, HEAD_BLOCKS), BF16),
        jax.ShapeDtypeStruct((s, FOX_WIDTH), F32),
        jax.ShapeDtypeStruct((s, 512), F32),
        jax.ShapeDtypeStruct((s, LANES), F32),
        jax.ShapeDtypeStruct((PROJ_PAD, D_MODEL), BF16),
    )

    def resident(shape):
        return pl.BlockSpec(shape, lambda i: (0,) * len(shape), pipeline_mode=pl.Buffered(1))

    *fwd, wp = pl.pallas_call(
        body, name="fwd_in", grid=(n,), out_shape=outs,
        in_specs=[_rows(t, D_MODEL), _full((1, D_MODEL)), resident(halves.shape), _full((1, LANES)),
                  _full((1, FOX_WIDTH)), _full((1, FOX_WIDTH))],
        out_specs=(_rows(t, D_MODEL), _rows(t, 512), _rows(t, 2 * FOX_WIDTH), _rows(t, HEAD_BLOCKS),
                   _rows(t, HEAD_BLOCKS), _rows(t, HEAD_BLOCKS), _rows(t, FOX_WIDTH), _rows(t, 512),
                   _rows(t, LANES), resident((PROJ_PAD, D_MODEL))),
        scratch_shapes=[pltpu.VMEM((1, LANES), F32), pltpu.VMEM((t, LANES), F32),
                        pltpu.VMEM((PROJ_PAD // 2, D_MODEL), jnp.uint32)],
        compiler_params=_params(),
    )(x, norm_g, halves, bf_pad, fq_g, fk_g)
    return tuple(fwd), wp


POOL_HALO = 16


def _pool_window(lane):
    return jnp.where(lane < 64, 2.0, jnp.where(lane < 128, 4.0, jnp.where(lane < 192, 8.0, 16.0)))


def _pool_pick(lane, s2, s4, s8, s16):
    return jnp.where(lane < 64, s2, jnp.where(lane < 128, s4, jnp.where(lane < 192, s8, s16)))


def _group_onehot(shape, row_is_group_lane):
    r = lax.broadcasted_iota(jnp.int32, shape, 0)
    c = lax.broadcasted_iota(jnp.int32, shape, 1)
    hit = (r % HEAD_DIM == c) if row_is_group_lane else (c % HEAD_DIM == r)
    return jnp.where(hit, 1.0, 0.0).astype(F32)


def _same_group(shape):
    r = lax.broadcasted_iota(jnp.int32, shape, 0)
    c = lax.broadcasted_iota(jnp.int32, shape, 1)
    return (r // HEAD_DIM) == (c // HEAD_DIM)


def _pool_block_diag(w4):
    spread = jnp.dot(w4, _group_onehot((HEAD_DIM, POOL_WIDTH), False), preferred_element_type=F32,
                     precision=lax.Precision.HIGHEST)
    return jnp.where(_same_group((POOL_WIDTH, POOL_WIDTH)), spread, 0.0).astype(BF16)


def _mem_softmax(qm, kp):
    sc = _dot(qm, kp, NT)
    e = jnp.exp(sc - jnp.max(sc, axis=-1, keepdims=True))
    return e * (1.0 / jnp.sum(e, axis=-1, keepdims=True))


def _side_fwd(pa, pm, w4, pscale, mq_g, mem, mem_norm_g, w_kv, mk_g):
    s = pa.shape[0]
    t = TILE
    n = s // t
    ext = t + POOL_HALO
    nm = mem.shape[0]

    def body(pa_ref, pm_ref, w4_ref, sc_ref, g_ref, mem_ref, mg_ref, wkv_ref, kg_ref,
             ma_ref, d_ref, mm_ref, mn_ref, kv_ref, k_ref, v_ref, ext_ref, w_ref):
        i = pl.program_id(0)

        @pl.when(i == 0)
        def _():
            ext_ref[0:POOL_HALO, :] = jnp.zeros((POOL_HALO, POOL_WIDTH), F32)
            w_ref[...] = _pool_block_diag(w4_ref[...])
            _mem_tokens_fwd(mem_ref, mg_ref, wkv_ref, kg_ref, mn_ref, kv_ref, k_ref, v_ref)

        u = pa_ref[:, 0:POOL_WIDTH]
        ext_ref[POOL_HALO:ext, :] = u
        e = ext_ref[...]
        s2 = e + pltpu.roll(e, 1, axis=0)
        s4 = s2 + pltpu.roll(s2, 2, axis=0)
        s8 = s4 + pltpu.roll(s4, 4, axis=0)
        s16 = s8 + pltpu.roll(s8, 8, axis=0)
        lane_e = lax.broadcasted_iota(jnp.int32, (ext, POOL_WIDTH), 1)
        win = _pool_pick(lane_e, s2, s4, s8, s16)[POOL_HALO:ext, :]
        lane = lax.broadcasted_iota(jnp.int32, (t, POOL_WIDTH), 1)
        pos = (lax.broadcasted_iota(jnp.int32, (t, POOL_WIDTH), 0) + (i * t + 1)).astype(F32)
        d = win / jnp.minimum(pos, _pool_window(lane)) - u
        db = d.astype(BF16)
        d_ref[...] = db
        ya = _dot(db, w_ref[...]) * sc_ref[...]
        ga = pa_ref[:, POOL_WIDTH:2 * POOL_WIDTH]
        ma_ref[...] = (ya * (ga * _sig(ga))).astype(BF16)
        ext_ref[0:POOL_HALO, :] = ext_ref[t:ext, :]

        lo = _lane_lo((t, LANES))
        for p in range(MEM_WIDTH // LANES):
            sl = slice(p * LANES, (p + 1) * LANES)
            qb = pm_ref[:, sl]
            qs = (((qb * _head_rms(qb, lo)) * g_ref[:, sl]) * ATT_SCALE).astype(BF16)
            kp = k_ref[:, sl]
            vp = v_ref[:, sl]
            outs = []
            for hh in range(2):
                msk = lo if hh == 0 else jnp.logical_not(lo)
                prob = _mem_softmax(jnp.where(msk, qs, jnp.zeros_like(qs)), kp)
                outs.append(_dot(prob.astype(BF16), vp))
            o = jnp.where(lo, outs[0], outs[1])
            gm = pm_ref[:, MEM_WIDTH + p * LANES:MEM_WIDTH + (p + 1) * LANES]
            mm_ref[:, sl] = (o * (gm * _sig(gm))).astype(BF16)

    return pl.pallas_call(
        body, name="side_fwd", grid=(n,),
        out_shape=(jax.ShapeDtypeStruct((s, POOL_WIDTH), BF16), jax.ShapeDtypeStruct((s, POOL_WIDTH), BF16),
                   jax.ShapeDtypeStruct((s, MEM_WIDTH), BF16), jax.ShapeDtypeStruct((nm, D_MODEL), BF16),
                   jax.ShapeDtypeStruct((nm, 2 * MEM_WIDTH), F32), jax.ShapeDtypeStruct((nm, MEM_WIDTH), BF16),
                   jax.ShapeDtypeStruct((nm, MEM_WIDTH), BF16)),
        in_specs=[_rows(t, 512), _rows(t, 512), _full((POOL_ROWS, HEAD_DIM)), _full((1, POOL_WIDTH)),
                  _full((1, MEM_WIDTH)), _full((nm, D_MODEL)), _full((1, D_MODEL)), _full((D_MODEL, 2 * MEM_WIDTH)),
                  _full((1, MEM_WIDTH))],
        out_specs=(_rows(t, POOL_WIDTH), _rows(t, POOL_WIDTH), _rows(t, MEM_WIDTH), _full((nm, D_MODEL)),
                   _full((nm, 2 * MEM_WIDTH)), _full((nm, MEM_WIDTH)), _full((nm, MEM_WIDTH))),
        scratch_shapes=[pltpu.VMEM((ext, POOL_WIDTH), F32), pltpu.VMEM((POOL_WIDTH, POOL_WIDTH), BF16)],
        compiler_params=_params(),
    )(pa, pm, w4, pscale, mq_g, mem, mem_norm_g, w_kv, mk_g)


FOX_FWD_HEADS = 4


def _fox_fwd(qa, ka, va, gb):
    s = qa.shape[0]
    t = TILE
    n = s // t
    heads = FOX_FWD_HEADS
    pairs = heads // 2
    group_w = heads * LANES

    def body(qa_ref, ka_ref, va_ref, gb_ref, o_ref, mb_ref, r_ref):
        i = pl.program_id(1)
        lane = lax.broadcasted_iota(jnp.int32, (t, LANES), 1)
        lo = lane < HEAD_DIM
        causal = lax.broadcasted_iota(jnp.int32, (t, t), 1) <= lax.broadcasted_iota(jnp.int32, (t, t), 0)
        qas = [qa_ref[:, hh * LANES:(hh + 1) * LANES] for hh in range(heads)]

        def step(j, carry, masked):
            rows = pl.ds(pl.multiple_of(j * t, t), t)
            def logits(hh):
                sc = _dot(qas[hh], ka_ref[rows, hh * LANES:(hh + 1) * LANES], NT)
                return jnp.where(causal, sc, -1e30) if masked else sc

            def advance(hh, sc):
                m, acc = carry[hh]
                m_new = jnp.maximum(m, jnp.max(sc, axis=-1, keepdims=True))
                p = jnp.exp(sc - m_new).astype(BF16)
                return m_new, jnp.exp(m - m_new) * acc + _dot(p, va_ref[rows, hh * LANES:(hh + 1) * LANES])

            new = []
            sc = logits(0)
            for hh in range(heads):
                sc_next = logits(hh + 1) if hh + 1 < heads else None
                new.append(advance(hh, sc))
                sc = sc_next
            return tuple(new)

        init = (jnp.full((t, 1), -1e30, F32), jnp.zeros((t, LANES), F32))
        carry = lax.fori_loop(0, i, functools.partial(step, masked=False), (init,) * heads)
        res = step(i, carry, masked=True)
        for p in range(pairs):
            outs = []
            rcol = jnp.zeros((t, LANES), F32)
            for hh in range(2):
                m, acc = res[2 * p + hh]
                l = _lane_pick(acc, lane, AUG_LO)
                outs.append(acc * (1.0 / l))
                rcol = jnp.where(lane == hh, m + jnp.log(l), rcol)
            o = _pair_block(outs[0], outs[1], lo)
            sl = slice(p * LANES, (p + 1) * LANES)
            o_ref[:, sl] = o
            g = gb_ref[:, sl]
            mb_ref[:, sl] = (o * (g * _sig(g))).astype(BF16)
            r_ref[p] = rcol

    tile_spec = pl.BlockSpec((t, pairs * LANES), lambda p, i: (i, p))
    full_spec = pl.BlockSpec((s, group_w), lambda p, i: (0, p))
    return pl.pallas_call(
        body, name="fox_fwd", grid=(FOX_HEADS // heads, n),
        out_shape=(jax.ShapeDtypeStruct((s, FOX_WIDTH), F32), jax.ShapeDtypeStruct((s, FOX_WIDTH), BF16),
                   jax.ShapeDtypeStruct((FOX_HEADS // 2, s, LANES), F32)),
        in_specs=[pl.BlockSpec((t, group_w), lambda p, i: (i, p)), full_spec, full_spec, tile_spec],
        out_specs=(tile_spec, tile_spec, pl.BlockSpec((pairs, t, LANES), lambda p, i: (p, i, 0))),
        compiler_params=_params(2),
    )(qa, ka, va, gb)


def _out_loss(x, tgt, ma, mb, mm, wout, gb, o, r4):
    s = x.shape[0]
    t = TILE
    n = s // t
    pairs = FOX_HEADS // 2

    def body(x_ref, t_ref, ma_ref, mb_ref, mm_ref, w_ref, gb_ref, o_ref, r_ref,
             dy_ref, dma_ref, dmm_ref, dw_ref, loss_ref, doa_ref, dgb_ref, rr_ref, mix_ref):
        @pl.when(pl.program_id(0) == 0)
        def _():
            dw_ref[...] = jnp.zeros_like(dw_ref)
            loss_ref[...] = jnp.zeros_like(loss_ref)

        mix_ref[:, 0:256] = ma_ref[...]
        mix_ref[:, 256:768] = mb_ref[...]
        mix_ref[:, 768:1024] = mm_ref[...]
        mix = mix_ref[...]
        err = (x_ref[...] + _dot(mix, w_ref[...])) - t_ref[...]
        row_mean = jnp.sum(err * err, axis=-1, keepdims=True) * (1.0 / D_MODEL)
        loss_ref[...] += 0.5 * jnp.sum(row_mean, axis=0, keepdims=True)
        dy = err * (1.0 / D_MODEL)
        dy_ref[...] = dy
        dyb = dy.astype(BF16)
        dmix = _dot(dyb, w_ref[...], NT)
        dma_ref[...] = dmix[:, 0:256]
        dmm_ref[...] = dmix[:, 768:1024]
        dw_ref[...] += _dot(mix, dyb, TN)

        lane = lax.broadcasted_iota(jnp.int32, (t, LANES), 1)
        lo = lane < HEAD_DIM
        d_os = []
        delta = jnp.zeros((t, LANES), F32)
        for p in range(pairs):
            sl = slice(p * LANES, (p + 1) * LANES)
            g = gb_ref[:, sl]
            sg = _sig(g)
            dm = dmix[:, 256 + p * LANES:256 + (p + 1) * LANES]
            ov = o_ref[:, sl]
            d_o = dm * (g * sg)
            d_os.append(d_o)
            dgb_ref[:, sl] = (dm * ov * (sg * (1.0 + g * (1.0 - sg)))).astype(BF16)
            prod = d_o * ov
            delta = jnp.where(lane == 2 * p, jnp.sum(jnp.where(lo, prod, 0.0), axis=-1, keepdims=True), delta)
            delta = jnp.where(lane == 2 * p + 1, jnp.sum(jnp.where(lo, 0.0, prod), axis=-1, keepdims=True), delta)
            rr_ref[p, 0] = r_ref[p].T[0:8, :]
        minus_delta = _spread3(-delta)
        for h in range(FOX_HEADS):
            blk = slice(h * LANES, (h + 1) * LANES)
            doa_ref[:, blk] = _head_block(d_os[h // 2], h % 2, lo, minus_delta[:, blk])

    return pl.pallas_call(
        body, name="out_loss", grid=(n,),
        out_shape=(jax.ShapeDtypeStruct((s, D_MODEL), F32), jax.ShapeDtypeStruct((s, 256), F32),
                   jax.ShapeDtypeStruct((s, 256), F32), jax.ShapeDtypeStruct((D_MODEL, D_MODEL), F32),
                   jax.ShapeDtypeStruct((1, LANES), F32), jax.ShapeDtypeStruct((s, HEAD_BLOCKS), BF16),
                   jax.ShapeDtypeStruct((s, FOX_WIDTH), BF16), jax.ShapeDtypeStruct((pairs, n, 8, t), F32)),
        in_specs=[_rows(t, D_MODEL), _rows(t, D_MODEL), _rows(t, 256), _rows(t, 512), _rows(t, 256),
                  _full((D_MODEL, D_MODEL)), _rows(t, FOX_WIDTH), _rows(t, FOX_WIDTH),
                  pl.BlockSpec((pairs, t, LANES), lambda i: (0, i, 0))],
        out_specs=(_rows(t, D_MODEL), _rows(t, 256), _rows(t, 256), _full((D_MODEL, D_MODEL)), _full((1, LANES)),
                   _rows(t, HEAD_BLOCKS), _rows(t, FOX_WIDTH), pl.BlockSpec((pairs, 1, 8, t), lambda i: (0, i, 0, 0))),
        scratch_shapes=[pltpu.VMEM((t, D_MODEL), BF16)],
        compiler_params=_params(),
    )(x, tgt, ma, mb, mm, wout, gb, o, r4)


def _side_bwd(pa, db, dma, w4, pscale, pm, dmm, kmn, vmb, mq_g, kv, mnb, mem, w_kv, mk_g, mem_norm_g):
    s = pa.shape[0]
    t = TILE
    n = s // t
    ext = t + POOL_HALO
    nm = mem.shape[0]

    def body(pa_ref, d_ref, dma_ref, w4_ref, sc_ref, pm_ref, dmm_ref, k_ref, v_ref, g_ref,
             kv_ref, mn_ref, mem_ref, wkv_ref, kg_ref, mg_ref,
             dpa_ref, dpm_ref, dw4_ref, dsc_ref, dg_ref, dwkv_ref, dmg_ref, dkg_ref,
             ext_ref, w_ref, dw_ref, dk_ref, dv_ref, gacc_ref, dkv_ref):
        i = pl.program_id(0)

        @pl.when(i == 0)
        def _():
            dw_ref[...] = jnp.zeros_like(dw_ref)
            dsc_ref[...] = jnp.zeros_like(dsc_ref)
            ext_ref[t:ext, :] = jnp.zeros((POOL_HALO, POOL_WIDTH), F32)
            w_ref[...] = _pool_block_diag(w4_ref[...])
            dk_ref[...] = jnp.zeros_like(dk_ref)
            dv_ref[...] = jnp.zeros_like(dv_ref)
            gacc_ref[...] = jnp.zeros_like(gacc_ref)

        dbv = d_ref[...]
        z = _dot(dbv, w_ref[...])
        ga = pa_ref[:, POOL_WIDTH:2 * POOL_WIDTH]
        sg = _sig(ga)
        dma_v = dma_ref[...]
        dya = dma_v * (ga * sg)
        dpa_ref[:, POOL_WIDTH:2 * POOL_WIDTH] = (dma_v * (z * sc_ref[...]) * (sg * (1.0 + ga * (1.0 - sg)))).astype(BF16)
        dsc_ref[...] += jnp.sum(dya * z, axis=0, keepdims=True)
        dzb = (dya * sc_ref[...]).astype(BF16)
        dw_ref[...] += _dot(dbv, dzb, TN)
        dd = _dot(dzb, w_ref[...], NT)
        lane = lax.broadcasted_iota(jnp.int32, (t, POOL_WIDTH), 1)
        pos = (lax.broadcasted_iota(jnp.int32, (t, POOL_WIDTH), 0) + ((n - 1 - i) * t + 1)).astype(F32)
        ext_ref[0:t, :] = dd / jnp.minimum(pos, _pool_window(lane))
        e = ext_ref[...]
        s2 = e + pltpu.roll(e, ext - 1, axis=0)
        s4 = s2 + pltpu.roll(s2, ext - 2, axis=0)
        s8 = s4 + pltpu.roll(s4, ext - 4, axis=0)
        s16 = s8 + pltpu.roll(s8, ext - 8, axis=0)
        lane_e = lax.broadcasted_iota(jnp.int32, (ext, POOL_WIDTH), 1)
        win = _pool_pick(lane_e, s2, s4, s8, s16)[0:t, :]
        dpa_ref[:, 0:POOL_WIDTH] = (win - dd).astype(BF16)
        ext_ref[t:ext, :] = ext_ref[0:POOL_HALO, :]

        lo = _lane_lo((t, LANES))
        pairs = MEM_WIDTH // LANES
        pre = []
        for p in range(pairs):
            sl = slice(p * LANES, (p + 1) * LANES)
            qb = pm_ref[:, sl]
            rr = _head_rms(qb, lo)
            qhat = qb * rr
            g = g_ref[:, sl]
            qs = ((qhat * g) * ATT_SCALE).astype(BF16)
            gm = pm_ref[:, MEM_WIDTH + p * LANES:MEM_WIDTH + (p + 1) * LANES]
            sg = _sig(gm)
            dmo = dmm_ref[:, sl]
            pre.append((sl, rr, qhat, g, qs, gm, sg, dmo, dmo * (gm * sg)))

        def front(p, hh):
            sl, _, _, _, qs, _, _, _, d_o = pre[p]
            msk = lo if hh == 0 else jnp.logical_not(lo)
            qm = jnp.where(msk, qs, jnp.zeros_like(qs))
            prob = _mem_softmax(qm, k_ref[:, sl])
            dom = jnp.where(msk, d_o, 0.0).astype(BF16)
            return qm, prob, dom, _dot(dom, v_ref[:, sl], NT)

        def back(p, qm, prob, dom, dp):
            sl = pre[p][0]
            pb = prob.astype(BF16)
            out = _dot(pb, v_ref[:, sl])
            ds = (prob * (dp - jnp.sum(prob * dp, axis=-1, keepdims=True))).astype(BF16)
            dq = _dot(ds, k_ref[:, sl])
            dk_ref[:, sl] += _dot(ds, qm, TN)
            dv_ref[:, sl] += _dot(pb, dom, TN)
            return out, dq

        heads = [(p, hh) for p in range(pairs) for hh in range(2)]
        done = []
        fronts = [front(*head) for head in heads]
        for k, (p, _) in enumerate(heads):
            done.append(back(p, *fronts[k]))
        for p in range(pairs):
            sl, rr, qhat, g, _, gm, sg, dmo, _ = pre[p]
            outs, dqs = zip(done[2 * p], done[2 * p + 1])
            o = jnp.where(lo, outs[0], outs[1])
            dqn = jnp.where(lo, dqs[0], dqs[1]) * ATT_SCALE
            dpm_ref[:, sl] = _head_norm_bwd(dqn, qhat, rr, g, lo).astype(BF16)
            dpm_ref[:, MEM_WIDTH + p * LANES:MEM_WIDTH + (p + 1) * LANES] = (
                dmo * o * (sg * (1.0 + gm * (1.0 - sg)))).astype(BF16)
            gacc_ref[:, sl] += jnp.sum(dqn * qhat, axis=0, keepdims=True)

        @pl.when(i == n - 1)
        def _():
            own = jnp.where(_same_group((POOL_WIDTH, POOL_WIDTH)), dw_ref[...], 0.0)
            dw4_ref[...] = jnp.dot(own, _group_onehot((POOL_WIDTH, HEAD_DIM), True), preferred_element_type=F32,
                                   precision=lax.Precision.HIGHEST)
            dg_ref[...] = _fold_heads(gacc_ref[...])

            lo_m = _lane_lo((nm, LANES))
            kacc = []
            for p in range(MEM_WIDTH // LANES):
                sl = slice(p * LANES, (p + 1) * LANES)
                kb = kv_ref[:, sl]
                rr = _head_rms(kb, lo_m)
                khat = kb * rr
                dk = dk_ref[:, sl]
                dkv_ref[:, sl] = _head_norm_bwd(dk, khat, rr, kg_ref[:, sl], lo_m).astype(BF16)
                kacc.append(jnp.sum(dk * khat, axis=0, keepdims=True))
            dkg_ref[...] = _fold_heads(jnp.concatenate(kacc, axis=1))
            dkv_ref[:, MEM_WIDTH:] = dv_ref[...].astype(BF16)
            dkv = dkv_ref[...]
            dwkv_ref[...] = _dot(mn_ref[...], dkv, TN)
            dmn = _dot(dkv, wkv_ref[...], NT)
            xm = mem_ref[...]
            rr = lax.rsqrt(jnp.mean(xm * xm, axis=-1, keepdims=True) + EPS)
            dmg_ref[...] = jnp.sum(dmn * (xm * rr), axis=0, keepdims=True)

    def rev(w):
        return _rows_rev(t, w, n)

    row = jax.ShapeDtypeStruct((1, LANES), F32)
    return pl.pallas_call(
        body, name="side_bwd", grid=(n,),
        out_shape=(jax.ShapeDtypeStruct((s, 512), BF16), jax.ShapeDtypeStruct((s, 512), BF16),
                   jax.ShapeDtypeStruct((POOL_ROWS, HEAD_DIM), F32), jax.ShapeDtypeStruct((1, POOL_WIDTH), F32), row,
                   jax.ShapeDtypeStruct((D_MODEL, 2 * MEM_WIDTH), F32), jax.ShapeDtypeStruct((1, D_MODEL), F32), row),
        in_specs=[rev(512), rev(POOL_WIDTH), rev(POOL_WIDTH), _full((POOL_ROWS, HEAD_DIM)), _full((1, POOL_WIDTH)),
                  rev(512), rev(MEM_WIDTH), _full((N_MEM, MEM_WIDTH)), _full((N_MEM, MEM_WIDTH)), _full((1, MEM_WIDTH)),
                  _full((nm, 2 * MEM_WIDTH)), _full((nm, D_MODEL)), _full((nm, D_MODEL)),
                  _full((D_MODEL, 2 * MEM_WIDTH)), _full((1, MEM_WIDTH)), _full((1, D_MODEL))],
        out_specs=(rev(512), rev(512), _full((POOL_ROWS, HEAD_DIM)), _full((1, POOL_WIDTH)), _full((1, LANES)),
                   _full((D_MODEL, 2 * MEM_WIDTH)), _full((1, D_MODEL)), _full((1, LANES))),
        scratch_shapes=[pltpu.VMEM((ext, POOL_WIDTH), F32), pltpu.VMEM((POOL_WIDTH, POOL_WIDTH), BF16),
                        pltpu.VMEM((POOL_WIDTH, POOL_WIDTH), F32), pltpu.VMEM((N_MEM, MEM_WIDTH), F32),
                        pltpu.VMEM((N_MEM, MEM_WIDTH), F32), pltpu.VMEM((1, MEM_WIDTH), F32),
                        pltpu.VMEM((nm, 2 * MEM_WIDTH), BF16)],
        compiler_params=_params(),
    )(pa, db, dma, w4, pscale, pm, dmm, kmn, vmb, mq_g, kv, mnb, mem, w_kv, mk_g, mem_norm_g)


FOX_BWD_HEADS = 4


def _fox_bwd(ka, va, qa, doa, rr, gparts, axes):
    s = ka.shape[0]
    t = TILE
    n = s // t
    heads = FOX_BWD_HEADS
    groups = FOX_HEADS // heads
    group_w = heads * LANES
    na = len(gparts)

    def body(*refs):
        ka_ref, va_ref, qa_ref, doa_ref, rr_ref = refs[0:5]
        g_refs = refs[5:5 + na]
        dka_ref, dva_ref, dqa_ref = refs[5 + na:8 + na]
        out_refs = refs[8 + na:8 + 2 * na]
        bufs = tuple(refs[8 + (2 + k) * na:8 + (3 + k) * na] for k in range(5))
        j = pl.program_id(1)
        step_id = pl.program_id(0) * n + j
        red = _ShardReduce(g_refs, out_refs, axes, bufs, *refs[8 + 7 * na:]) if na else None

        @pl.when(j == 0)
        def _():
            dqa_ref[...] = jnp.zeros_like(dqa_ref)

        if red is not None:
            pl.when(step_id == 0)(red.exchange_with_sibling)

            @pl.when(step_id == 1)
            def _():
                for k in (1, 2, 3):
                    red.send_to_chip(k)
                red.keep_mine()

        causal = lax.broadcasted_iota(jnp.int32, (t, t), 0) <= lax.broadcasted_iota(jnp.int32, (t, t), 1)
        kas = [ka_ref[:, hh * LANES:(hh + 1) * LANES] for hh in range(heads)]
        vas = [va_ref[:, hh * LANES:(hh + 1) * LANES] for hh in range(heads)]

        def step(i, carry, masked):
            rows = pl.ds(pl.multiple_of(i * t, t), t)
            new = []
            for hh in range(heads):
                cols = slice(hh * LANES, (hh + 1) * LANES)
                dk_a, dv_a = carry[hh]
                qb = qa_ref[rows, cols]
                d_o = doa_ref[rows, cols]
                arg = _dot(kas[hh], qb, NT) - rr_ref[hh // 2, i, hh % 2:hh % 2 + 1, :]
                if masked:
                    arg = jnp.where(causal, arg, -1e30)
                pt = jnp.exp(arg)
                dst = (pt * _dot(vas[hh], d_o, NT)).astype(BF16)
                dv_a = dv_a + _dot(pt.astype(BF16), d_o)
                dk_a = dk_a + _dot(dst, qb)
                dqa_ref[rows, cols] += _dot(dst, kas[hh], TN)
                new.append((dk_a, dv_a))
            return tuple(new)

        zero = jnp.zeros((t, LANES), F32)
        carry = step(j, ((zero, zero),) * heads, masked=True)
        res = lax.fori_loop(j + 1, n, functools.partial(step, masked=False), carry)
        for hh in range(heads):
            cols = slice(hh * LANES, (hh + 1) * LANES)
            dka_ref[:, cols] = res[hh][0]
            dva_ref[:, cols] = res[hh][1]

        if red is not None:
            @pl.when(step_id == groups * n - 1)
            def _():
                red.sum_and_share()
                red.finish()

    tile_spec = pl.BlockSpec((t, group_w), lambda p, j: (j, p))
    full_spec = pl.BlockSpec((s, group_w), lambda p, j: (0, p))
    any_spec = pl.BlockSpec(memory_space=pl.ANY)
    scratch = _ShardReduce.scratch(gparts, axes)
    if na:
        scratch += [pltpu.SemaphoreType.DMA((_ShardReduce.SEMS * na,)), pltpu.SemaphoreType.DMA((_ShardReduce.SEMS * na,)),
                    pltpu.SemaphoreType.DMA((_ShardReduce.LOCAL * na,))]
    return pl.pallas_call(
        body, name="fox_bwd", grid=(groups, n),
        out_shape=(jax.ShapeDtypeStruct((s, HEAD_BLOCKS), F32),) * 3
        + tuple(jax.ShapeDtypeStruct(_shard_shape(g), F32) for g in gparts),
        in_specs=[tile_spec, tile_spec, full_spec, full_spec,
                  pl.BlockSpec((heads // 2, n, 8, t), lambda p, j: (p, 0, 0, 0))] + [any_spec] * na,
        out_specs=(tile_spec, tile_spec, full_spec) + (any_spec,) * na,
        scratch_shapes=scratch, compiler_params=_params(2, VMEM_LIMIT_FOX_BWD),
    )(ka, va, qa, doa, rr, *gparts)


def _fox_post_tile(i, n, t, dqa_ref, dka_ref, dva_ref, qk_ref, fb_ref, bf_ref, qg_ref, kg_ref,
                   dqk_ref, dv_ref, dfb_ref, dqg_ref, dkg_ref, dbf_ref, qacc_ref, kacc_ref, carry_ref,
                   between):
    @pl.when(i == 0)
    def _():
        qacc_ref[...] = jnp.zeros_like(qacc_ref)
        kacc_ref[...] = jnp.zeros_like(kacc_ref)
        dbf_ref[...] = jnp.zeros_like(dbf_ref)
        carry_ref[...] = jnp.zeros_like(carry_ref)

    lane = lax.broadcasted_iota(jnp.int32, (t, LANES), 1)
    row = lax.broadcasted_iota(jnp.int32, (t, LANES), 0)
    lo = lane < HEAD_DIM

    def head_blocks(ref, p):
        return ref[:, 2 * p * LANES:(2 * p + 1) * LANES], ref[:, (2 * p + 1) * LANES:(2 * p + 2) * LANES]

    def issue(k):
        if between[k] is not None:
            between[k]()

    sums = []
    pairs = FOX_WIDTH // LANES
    for side, (src_ref, g_ref, acc_ref, scale) in enumerate(((dqa_ref, qg_ref, qacc_ref, ATT_SCALE),
                                                             (dka_ref, kg_ref, kacc_ref, 1.0))):
        total = jnp.zeros((t, LANES), F32)
        for p in range(pairs):
            issue(side * pairs + p)
            sl = slice(p * LANES, (p + 1) * LANES)
            cols = slice(side * FOX_WIDTH + p * LANES, side * FOX_WIDTH + (p + 1) * LANES)
            if side == 0:
                dv_ref[:, sl] = _pair_block(*head_blocks(dva_ref, p), lo).astype(BF16)
            d0, d1 = head_blocks(src_ref, p)
            total = total + (d0 + d1)
            raw = qk_ref[:, cols]
            rr = _head_rms(raw, lo)
            xhat = raw * rr
            dn = _pair_block(d0, d1, lo) * scale
            dqk_ref[:, cols] = _head_norm_bwd(dn, xhat, rr, g_ref[:, sl], lo).astype(BF16)
            acc_ref[:, sl] += jnp.sum(dn * xhat, axis=0, keepdims=True)
        sums.append(total)
    issue(2 * pairs)
    dq_sum, dk_sum = sums

    acc = (pltpu.roll(dq_sum, LANES - KEY_SUM_LANE, axis=1) - pltpu.roll(dk_sum, LANES - QUERY_SUM_LANE, axis=1))
    acc = jnp.where(lane < FOX_HEADS, acc, 0.0)
    sh = 1
    while sh < t:
        acc = acc + jnp.where(row < t - sh, pltpu.roll(acc, t - sh, axis=0), 0.0)
        sh *= 2
    dlogf = acc + carry_ref[...]
    dfb_ref[...] = dlogf
    carry_ref[...] = dfb_ref[0:1, :]
    z = fb_ref[...] + bf_ref[...]
    dz = jnp.where(lane < FOX_HEADS, dlogf * (1.0 / (1.0 + jnp.exp(z))), 0.0)
    dfb_ref[...] = dz
    dbf_ref[...] += jnp.sum(dz, axis=0, keepdims=True)

    @pl.when(i == n - 1)
    def _():
        dqg_ref[...] = _fold_heads(qacc_ref[...])
        dkg_ref[...] = _fold_heads(kacc_ref[...])


def _assemble_dproj(dp_ref, dpa_ref, dqk_ref, dv_ref, dgb_ref, dpm_ref, dfb_ref):
    dp_ref[:, PA_LO:QB_LO] = dpa_ref[...]
    dp_ref[:, QB_LO:VB_LO] = dqk_ref[...]
    dp_ref[:, VB_LO:GB_LO] = dv_ref[...]
    dp_ref[:, GB_LO:PM_LO] = dgb_ref[...]
    dp_ref[:, PM_LO:FB_LO] = dpm_ref[...]
    dp_ref[:, FB_LO:PROJ_PAD] = dfb_ref[...].astype(BF16)


def _dproj_specs(t):
    return [_rows(t, 512), _rows(t, 2 * FOX_WIDTH), _rows(t, FOX_WIDTH), _rows(t, FOX_WIDTH), _rows(t, 512),
            _rows(t, LANES)]


IN_BWD_X_TILE = 256


def _in_bwd_x(x, dy, norm_g, wp, dparts, gparts, axes, smalls):
    s = x.shape[0]
    t = IN_BWD_X_TILE
    n = s // t
    na = len(gparts)
    n_dp = len(dparts)
    vec_leaves, loss_row, dw4 = smalls if smalls is not None else ((), None, None)
    nv = len(vec_leaves)
    n_small = nv + 2 if smalls is not None else 0
    small_base = _ShardReduce.SEMS * na

    def body(*refs):
        x_ref, dy_ref, g_ref, wp_ref = refs[0:4]
        dp_parts = refs[4:4 + n_dp]
        o = 4 + n_dp
        g_refs = refs[o:o + na]
        small_in = refs[o + na:o + na + n_small]
        o += na + n_small
        gx_ref, dg_ref = refs[o:o + 2]
        out_refs = refs[o + 2:o + 2 + na]
        small_out = refs[o + 2 + na:o + 2 + na + (2 if smalls is not None else 0)]
        o += 2 + na + len(small_out)
        dp_ref = refs[o]
        bufs = tuple(refs[o + 1 + k * na:o + 1 + (k + 1) * na] for k in range(5))
        rest = refs[o + 1 + 5 * na:]

        i = pl.program_id(0)
        if na or smalls is not None:
            send_sems, recv_sems, local_sems = rest[-3:]
        red = _ShardReduce(g_refs, out_refs, axes, bufs, send_sems, recv_sems, local_sems) if na else None

        @pl.when(i == 0)
        def _():
            dg_ref[...] = jnp.zeros_like(dg_ref)
            if red is not None:
                red.exchange_with_sibling()

        if red is not None:
            for k in (1, 2, 3):
                pl.when(i == k)(functools.partial(red.send_to_chip, k))
            pl.when(i == 4)(red.keep_mine)

        _assemble_dproj(dp_ref, *dp_parts)
        dh = _dot(dp_ref[...], wp_ref[...])
        xv = x_ref[...]
        rr = lax.rsqrt(jnp.mean(xv * xv, axis=-1, keepdims=True) + EPS)
        xhat = xv * rr
        scaled = dh * g_ref[...]
        gx_ref[...] = dy_ref[...] + rr * (scaled - xhat * jnp.mean(xhat * scaled, axis=-1, keepdims=True))
        dg_ref[...] += jnp.sum(dh * xhat, axis=0, keepdims=True)

        def small_all_reduce():
            leaf_refs, (loss_ref, dw4_ref) = small_in[0:nv], small_in[nv:]
            vec_out, dw4_out = small_out
            vec_mine, vec_recv, dw4_recv = rest[0:3]
            cx, cy, c = _my_place()
            me_lin = 4 * cx + 2 * cy + c

            def copy(k, src, dst, base):
                peer = (me_lin + k) % 8
                return pltpu.make_async_remote_copy(
                    src_ref=src, dst_ref=dst.at[me_lin], send_sem=send_sems.at[base + k - 1],
                    recv_sem=recv_sems.at[base + k - 1], device_id=(peer // 4, (peer // 2) % 2, peer % 2),
                    device_id_type=MESH)

            vec_mine[...] = jnp.zeros_like(vec_mine)
            vec_mine[0:1, :] = dg_ref[...]
            for (_, row, _), ref in zip(VEC_LEAVES[1:], leaf_refs):
                vec_mine[row:row + 1, 0:ref.shape[1]] = ref[...]
            vec_mine[VEC_LOSS_ROW:VEC_LOSS_ROW + 1, 0:LANES] = loss_ref[...]
            copies = [copy(k, src, dst, base) for k in range(1, 8)
                      for src, dst, base in ((vec_mine, vec_recv, small_base), (dw4_ref, dw4_recv, small_base + 7))]
            for cp in copies:
                cp.start()
            for cp in copies:
                cp.wait_recv()
            vec_recv[me_lin] = vec_mine[...]
            dw4_recv[me_lin] = dw4_ref[...]
            vtot, wtot = vec_recv[0], dw4_recv[0]
            for d in range(1, 8):
                vtot = vtot + vec_recv[d]
                wtot = wtot + dw4_recv[d]
            vec_out[...] = vtot
            dw4_out[...] = wtot
            for cp in copies:
                cp.wait_send()

        @pl.when(i == n - 1)
        def _():
            if red is not None:
                red.sum_and_share()
            if smalls is not None:
                small_all_reduce()
            if red is not None:
                red.finish()

    any_spec = pl.BlockSpec(memory_space=pl.ANY)
    scratch = [pltpu.VMEM((t, PROJ_PAD), BF16)] + _ShardReduce.scratch(gparts, axes)
    out_shape = [jax.ShapeDtypeStruct((s, D_MODEL), F32), jax.ShapeDtypeStruct((1, D_MODEL), F32)]
    out_shape += [jax.ShapeDtypeStruct(_shard_shape(g), F32) for g in gparts]
    out_specs = [_rows(t, D_MODEL), _full((1, D_MODEL))] + [any_spec] * na
    small_args = []
    if smalls is not None:
        small_args = [*vec_leaves, loss_row, dw4]
        out_shape += [jax.ShapeDtypeStruct((VEC_ROWS, D_MODEL), F32), jax.ShapeDtypeStruct(dw4.shape, F32)]
        out_specs += [_full((VEC_ROWS, D_MODEL)), _full(dw4.shape)]
        scratch += [pltpu.VMEM((VEC_ROWS, D_MODEL), F32), pltpu.VMEM((8, VEC_ROWS, D_MODEL), F32),
                    pltpu.VMEM((8,) + dw4.shape, F32)]
    if na or smalls is not None:
        n_sems = small_base + 14
        scratch += [pltpu.SemaphoreType.DMA((n_sems,)), pltpu.SemaphoreType.DMA((n_sems,)),
                    pltpu.SemaphoreType.DMA((max(_ShardReduce.LOCAL * na, 1),))]
    return pl.pallas_call(
        body, name="in_bwd_x", grid=(n,), out_shape=tuple(out_shape),
        in_specs=[_rows(t, D_MODEL), _rows(t, D_MODEL), _full((1, D_MODEL)),
                  pl.BlockSpec((PROJ_PAD, D_MODEL), lambda i: (0, 0), pipeline_mode=pl.Buffered(1))]
        + _dproj_specs(t) + [any_spec] * na + [_full(a.shape) for a in small_args],
        out_specs=tuple(out_specs), scratch_shapes=scratch, compiler_params=_params(),
    )(x, dy, norm_g, wp, *dparts, *gparts, *small_args)


def _in_bwd_w(hb, dpa, dgb, dpm, fox):
    s = hb.shape[0]
    t = TILE
    n = s // t
    f_hi = F_ORIG_LO + FOX_HEADS
    n_in = 4 + len(fox)

    def body(*refs):
        h_ref, dpa_ref, dgb_ref, dpm_ref = refs[0:4]
        fox_refs = refs[4:n_in]
        dw_ref, dqk_ref, dv_ref, dfb_ref, dqg_ref, dkg_ref, dbf_ref = refs[n_in:n_in + 7]
        fox_scratch = refs[n_in + 7:]
        i = pl.program_id(0)

        @pl.when(i == 0)
        def _():
            dw_ref[...] = jnp.zeros_like(dw_ref)

        hv = h_ref[...]

        def rows_of(lo, ref, cols=slice(None)):
            def add():
                dproj = ref[:, cols]
                dw_ref[lo:lo + dproj.shape[1], :] += _dot(dproj, hv, TN)
            return add

        q_cols, k_cols = slice(0, FOX_WIDTH), slice(FOX_WIDTH, 2 * FOX_WIDTH)
        between = (rows_of(0, dpa_ref), rows_of(f_hi, dgb_ref), rows_of(f_hi + FOX_WIDTH, dpm_ref), None,
                   rows_of(QB_LO, dqk_ref, q_cols), rows_of(VB_LO, dv_ref), None, None, rows_of(KB_LO, dqk_ref, k_cols))
        _fox_post_tile(i, n, t, *fox_refs, dqk_ref, dv_ref, dfb_ref, dqg_ref, dkg_ref, dbf_ref, *fox_scratch, between)
        dw_ref[F_ORIG_LO:f_hi, :] += _dot(dfb_ref[...].astype(BF16), hv, TN)[0:FOX_HEADS, :]

    def rev(w):
        return _rows_rev(t, w, n)

    row = jax.ShapeDtypeStruct((1, LANES), F32)
    return pl.pallas_call(
        body, name="in_bwd_w", grid=(n,),
        out_shape=(jax.ShapeDtypeStruct((IN_WIDTH, D_MODEL), F32), jax.ShapeDtypeStruct((s, 2 * FOX_WIDTH), BF16),
                   jax.ShapeDtypeStruct((s, FOX_WIDTH), BF16), jax.ShapeDtypeStruct((s, LANES), F32), row, row, row),
        in_specs=[rev(D_MODEL), rev(512), rev(FOX_WIDTH), rev(512), rev(HEAD_BLOCKS), rev(HEAD_BLOCKS),
                  rev(HEAD_BLOCKS), rev(2 * FOX_WIDTH), rev(LANES), _full((1, LANES)), _full((1, FOX_WIDTH)),
                  _full((1, FOX_WIDTH))],
        out_specs=(pl.BlockSpec((IN_WIDTH, D_MODEL), lambda i: (0, 0), pipeline_mode=pl.Buffered(1)),
                   rev(2 * FOX_WIDTH), rev(FOX_WIDTH), rev(LANES), _full((1, LANES)), _full((1, LANES)),
                   _full((1, LANES))),
        scratch_shapes=[pltpu.VMEM((1, FOX_WIDTH), F32), pltpu.VMEM((1, FOX_WIDTH), F32), pltpu.VMEM((1, LANES), F32)],
        compiler_params=_params(),
    )(hb, dpa, dgb, dpm, *fox)


def _adamw_math(w_ref, gv, m_ref, v_ref, d_ref, nm_ref, nv_ref):
    nm = ADAM_B1 * m_ref[...] + (1.0 - ADAM_B1) * gv
    nv = ADAM_B2 * v_ref[...] + (1.0 - ADAM_B2) * (gv * gv)
    m_hat = nm / (1.0 - ADAM_B1 ** ADAM_STEP)
    v_hat = nv / (1.0 - ADAM_B2 ** ADAM_STEP)
    d_ref[...] = -ADAM_LR * (m_hat / (jnp.sqrt(v_hat) + ADAM_EPS) + ADAM_WD * w_ref[...])
    nm_ref[...] = nm
    nv_ref[...] = nv


def _adamw_flat(name, w, g, m, v):
    rows, cols = g.shape
    per_row = cols // LANES

    def body(w_ref, g_ref, m_ref, v_ref, gf_ref, d_ref, nm_ref, nv_ref):
        for k in range(per_row):
            gf_ref[pl.ds(k, rows, stride=per_row), :] = g_ref[:, k * LANES:(k + 1) * LANES]
        _adamw_math(w_ref, gf_ref[...], m_ref, v_ref, d_ref, nm_ref, nv_ref)

    def whole(shape):
        return pl.BlockSpec(shape, lambda i: (0, 0), pipeline_mode=pl.Buffered(1))

    return pl.pallas_call(
        body, name=name, grid=(1,),
        out_shape=(jax.ShapeDtypeStruct(w.shape, F32),) * 4,
        in_specs=[whole(w.shape), whole(g.shape), whole(w.shape), whole(w.shape)], out_specs=(whole(w.shape),) * 4,
        compiler_params=_params(),
    )(w, g, m, v)


def _adamw_rest(vec, dw4, leaves, pool, shards):
    nl = len(VEC_LEAVES) + 1
    ns = len(shards)

    def body(*refs):
        vec_ref, dw4_ref = refs[0:2]
        wmv = refs[2:2 + 3 * nl]
        shard_in = refs[2 + 3 * nl:2 + 3 * nl + 4 * ns]
        o = 2 + 3 * nl + 4 * ns
        loss_ref = refs[o]
        outs = refs[o + 1:o + 1 + 4 * nl]
        shard_out = refs[o + 1 + 4 * nl:]
        loss_ref[...] = vec_ref[VEC_LOSS_ROW:VEC_LOSS_ROW + 1, 0:1]
        for k in range(nl):
            if k < nl - 1:
                _, row, width = VEC_LEAVES[k]
                gv = vec_ref[row:row + 1, 0:width]
            else:
                gv = dw4_ref[...]
            w_ref, m_ref, v_ref = wmv[3 * k:3 * k + 3]
            g_ref, d_ref, nm_ref, nv_ref = outs[4 * k:4 * k + 4]
            g_ref[...] = gv
            _adamw_math(w_ref, gv, m_ref, v_ref, d_ref, nm_ref, nv_ref)
        for k in range(ns):
            w_ref, g_ref, m_ref, v_ref = shard_in[4 * k:4 * k + 4]
            _adamw_math(w_ref, g_ref[...], m_ref, v_ref, *shard_out[3 * k:3 * k + 3])

    shapes = [jax.ShapeDtypeStruct((1, width), F32) for _, _, width in VEC_LEAVES] + [
        jax.ShapeDtypeStruct(dw4.shape, F32)]
    flat_in = [a for triple in list(leaves) + [pool] for a in triple] + [a for quad in shards for a in quad]
    res = pl.pallas_call(
        body, name="adamw_rest",
        out_shape=(jax.ShapeDtypeStruct((1, 1), F32),) + tuple(s for s in shapes for _ in range(4))
        + tuple(jax.ShapeDtypeStruct(quad[0].shape, F32) for quad in shards for _ in range(3)),
        compiler_params=pltpu.CompilerParams(vmem_limit_bytes=VMEM_LIMIT),
    )(vec, dw4, *flat_in)
    per = [res[1 + 4 * k:5 + 4 * k] for k in range(nl)]
    big = res[1 + 4 * nl:]
    return (res[0], [p[0] for p in per], [p[1] for p in per], [p[2] for p in per], [p[3] for p in per],
            [big[3 * k:3 * k + 3] for k in range(ns)])


def _tile_heads(g, n):
    return jnp.tile(g.reshape(1, HEAD_DIM), (1, n))


def kernel(x, mem, norm_g, w_in, b_f, w_pool, pool_scale, fox_q_g, fox_k_g, mem_norm_g, w_mem_kv, mem_q_g, mem_k_g, w_out, loss_target, m_norm_g, m_w_in, m_b_f, m_w_pool, m_pool_scale, m_fox_q_g, m_fox_k_g, m_mem_norm_g, m_w_mem_kv, m_mem_q_g, m_mem_k_g, m_w_out, v_norm_g, v_w_in, v_b_f, v_w_pool, v_pool_scale, v_fox_q_g, v_fox_k_g, v_mem_norm_g, v_w_mem_kv, v_mem_q_g, v_mem_k_g, v_w_out):
    w_in_t = w_in[0].T
    axes = (1, 0, 0)

    g_in, g_kv, g_out = _all_gather_weights([w_in_t, w_mem_kv[0], w_out[0]], axes)
    tiled = _tiled_params(b_f, fox_q_g, fox_k_g, mem_q_g, mem_k_g)
    fwd, wp = _fwd_in(x[0], norm_g, g_in, *tiled[0:3])
    w_kv_b = g_kv.reshape(D_MODEL, 2 * MEM_WIDTH)
    w_out_b = g_out.reshape(D_MODEL, D_MODEL)
    w4 = w_pool.reshape(POOL_ROWS, HEAD_DIM)
    dy, hb, dpa, dgb, dpm, fox, g_w_kv, g_w_out, (dmemnorm_g, dpscale, dmq_g, dmk_g), loss_row, dw4 = _local_partials(
        x[0], mem[0], loss_target[0], fwd, w_kv_b, w_out_b, tiled, w4, pool_scale, mem_norm_g, axes[1:])
    dwp, dqk, dvb, dfb, dfq_g, dfk_g, dbf = _in_bwd_w(hb, dpa, dgb, dpm, fox)
    dparts = (dpa, dqk, dvb, dgb, dpm, dfb)
    vec_leaves = (dmemnorm_g, dpscale, dbf, dfq_g, dfk_g, dmq_g, dmk_g)
    grad_x, _, g_w_in_t, vec, dw4_sum = _in_bwd_x(
        x[0], dy, norm_g, wp, dparts, [dwp], axes[0:1], (vec_leaves, loss_row, dw4))

    small_wmv = [(norm_g, m_norm_g, v_norm_g), (mem_norm_g, m_mem_norm_g, v_mem_norm_g),
                 (pool_scale, m_pool_scale, v_pool_scale), (b_f, m_b_f, v_b_f), (fox_q_g, m_fox_q_g, v_fox_q_g),
                 (fox_k_g, m_fox_k_g, v_fox_k_g), (mem_q_g, m_mem_q_g, v_mem_q_g), (mem_k_g, m_mem_k_g, v_mem_k_g)]
    pool_wmv = tuple(a.reshape(POOL_ROWS, HEAD_DIM) for a in (w_pool, m_w_pool, v_w_pool))
    loss, *small_out, (upd_kv, upd_out) = _adamw_rest(
        vec, dw4_sum, small_wmv, pool_wmv, [(w_mem_kv[0], g_w_kv, m_w_mem_kv[0], v_w_mem_kv[0]),
                                             (w_out[0], g_w_out, m_w_out[0], v_w_out[0])])
    tiles = D_MODEL // LANES

    def flat(a):
        return a.reshape(tiles, LANES, -1).transpose(2, 0, 1).reshape(-1, LANES)

    def unflat(a):
        return a.reshape(-1, tiles, LANES).transpose(1, 2, 0).reshape(w_in.shape)

    g_in_flat, *upd_in = _adamw_flat("adamw_w_in", flat(w_in), g_w_in_t, flat(m_w_in), flat(v_w_in))
    big = [[unflat(g_in_flat), g_w_kv[None], g_w_out[None]]]
    big += [[unflat(upd_in[k]), upd_kv[k][None], upd_out[k][None]] for k in range(3)]

    def leaves(k):
        sm = small_out[k]
        b_in, b_kv, b_out = big[k]
        return (sm[0], b_in, sm[3], sm[8].reshape(w_pool.shape), sm[2], sm[4], sm[5], sm[1], b_kv, sm[6], sm[7], b_out)

    return (loss.reshape(()), grad_x[None], *leaves(0), *leaves(1), *leaves(2), *leaves(3))


def _tiled_params(b_f, fox_q_g, fox_k_g, mem_q_g, mem_k_g):
    return (jnp.pad(b_f, ((0, 0), (0, LANES - FOX_HEADS))), _tile_heads(fox_q_g, FOX_HEADS),
            _tile_heads(fox_k_g, FOX_HEADS), _tile_heads(mem_q_g, 4), _tile_heads(mem_k_g, 4))


def _local_partials(xs, mems, tgt, fwd, w_kv_b, w_out_b, tiled, w4, pool_scale, mem_norm_g, axes):
    hb, pa, qk, qa, ka, va, gb, pm, fb = fwd
    bf_pad, fq_g, fk_g, mq_g, mk_g = tiled

    ma, db, mm, mnb, kv, kmn, vmb = _side_fwd(pa, pm, w4, pool_scale, mq_g, mems, mem_norm_g, w_kv_b, mk_g)
    o, mb, r4 = _fox_fwd(qa, ka, va, gb)
    dy, dma, dmm, dw_out, loss_row, doa, dgb, rr = _out_loss(xs, tgt, ma, mb, mm, w_out_b, gb, o, r4)

    dpa, dpm, dw4, dpscale, dmq_g, dw_kv, dmemnorm_g, dmk_g = _side_bwd(
        pa, db, dma, w4, pool_scale, pm, dmm, kmn, vmb, mq_g, kv, mnb, mems, w_kv_b, mk_g, mem_norm_g)
    if axes:
        parts = [dw_kv.reshape(4, D_MODEL // 4, 2 * MEM_WIDTH), dw_out.reshape(4, D_MODEL // 4, D_MODEL)]
        dka, dva, dqa, dw_kv, dw_out = _fox_bwd(ka, va, qa, doa, rr, parts, axes)
    else:
        dka, dva, dqa = _fox_bwd(ka, va, qa, doa, rr, [], ())
    fox = (dqa, dka, dva, qk, fb, bf_pad, fq_g, fk_g)
    return dy, hb, dpa, dgb, dpm, fox, dw_kv, dw_out, (dmemnorm_g, dpscale, dmq_g, dmk_g), loss_row, dw4
```

```python
import functools

import jax
import jax.numpy as jnp
from jax import lax
from jax.experimental import pallas as pl
from jax.experimental.pallas import tpu as pltpu

F32 = jnp.float32
BF16 = jnp.bfloat16
MESH = pl.DeviceIdType.MESH

D_MODEL = 1024
HEAD_DIM = 64
POOL_WIDTH = 256
FOX_WIDTH = 512
FOX_HEADS = 8
MEM_WIDTH = 256
N_MEM = 256
IN_WIDTH = 3080
EPS = 1e-6
ATT_SCALE = 0.125

ADAM_LR = 0.001
ADAM_B1 = 0.9
ADAM_B2 = 0.999
ADAM_EPS = 1e-08
ADAM_WD = 0.01
ADAM_STEP = 10

LANES = 128
PA_LO, QB_LO, KB_LO, VB_LO, GB_LO, PM_LO, FB_LO, PROJ_PAD = 0, 512, 1024, 1536, 2048, 2560, 3072, 3200
F_ORIG_LO = 2048

TILE = 512
VMEM_LIMIT = 56 * 1024 * 1024
VMEM_LIMIT_FOX_BWD = 58 * 1024 * 1024

VEC_LEAVES = (("norm_g", 0, 1024), ("mem_norm_g", 1, 1024), ("pool_scale", 2, 256), ("b_f", 3, 8),
              ("fox_q_g", 4, 64), ("fox_k_g", 5, 64), ("mem_q_g", 6, 64), ("mem_k_g", 7, 64))
VEC_LOSS_ROW = 8
VEC_ROWS = 16
POOL_ROWS = 256


def _params(n_grid=1, vmem=VMEM_LIMIT):
    return pltpu.CompilerParams(dimension_semantics=("arbitrary",) * n_grid, vmem_limit_bytes=vmem)


def _rows(t, w):
    return pl.BlockSpec((t, w), lambda i: (i, 0))


def _rows_rev(t, w, n):
    return pl.BlockSpec((t, w), lambda i: (n - 1 - i, 0))


def _full(shape):
    return pl.BlockSpec(shape, lambda i: (0,) * len(shape))


def _sig(x):
    return 1.0 / (1.0 + jnp.exp(-x))


def _lane_lo(shape):
    return lax.broadcasted_iota(jnp.int32, shape, 1) < HEAD_DIM


def _pair_sum(v, lo):
    s0 = jnp.sum(jnp.where(lo, v, 0.0), axis=-1, keepdims=True)
    s1 = jnp.sum(jnp.where(lo, 0.0, v), axis=-1, keepdims=True)
    return jnp.where(lo, s0, s1)


def _head_rms(blk, lo):
    return lax.rsqrt(_pair_sum(blk * blk, lo) * (1.0 / HEAD_DIM) + EPS)


def _head_norm_bwd(dyn, xhat, rr, g, lo):
    a = dyn * g
    return rr * (a - xhat * (_pair_sum(xhat * a, lo) * (1.0 / HEAD_DIM)))


def _fold_heads(acc):
    tot = acc[:, 0:LANES]
    for p in range(1, acc.shape[1] // LANES):
        tot = tot + acc[:, p * LANES:(p + 1) * LANES]
    return tot + pltpu.roll(tot, HEAD_DIM, axis=1)


def _lane_pick(v, lane, idx):
    return jnp.sum(jnp.where(lane == idx, v, 0.0), axis=-1, keepdims=True)


NT = (((1,), (1,)), ((), ()))
TN = (((0,), (0,)), ((), ()))


def _dot(a, b, dims=None):
    if dims is None:
        return jnp.dot(a, b, preferred_element_type=F32)
    return lax.dot_general(a, b, dims, preferred_element_type=F32)


def _my_place():
    return lax.axis_index("x"), lax.axis_index("y"), lax.axis_index("c")


def _half_dims(shape, axis):
    return (shape[0] // 2, shape[1]) if axis == 0 else (shape[0], shape[1] // 2)


def _shard_shape(g):
    return tuple(g.shape[1:]) if len(g.shape) == 3 else (g.shape[0] // 4, g.shape[1])


F32_ROWS = 8


def _shard_window(g):
    rows = _shard_shape(g)[0]
    if len(g.shape) == 3:
        return rows
    skew = max((j * rows) % F32_ROWS for j in range(4))
    return -(-(rows + skew) // F32_ROWS) * F32_ROWS


def _half_of(ref, axis, core, lead=False):
    rows, cols = ref.shape[-2:]
    if axis == 0:
        idx = (pl.ds(pl.multiple_of(core * (rows // 2), 16), rows // 2), slice(None))
    else:
        idx = (slice(None), pl.ds(pl.multiple_of(core * (cols // 2), LANES), cols // 2))
    return ref.at[(slice(None),) + idx] if lead else ref.at[idx]


class _HalfGather:
    def __init__(self, ins, outs, axes, f32_bufs, bf_bufs, send_sems, recv_sems, local_sems):
        self.ins, self.outs, self.axes = ins, outs, axes
        self.f32_bufs, self.bf_bufs = f32_bufs, bf_bufs
        self.send_sems, self.recv_sems, self.local_sems = send_sems, recv_sems, local_sems
        self.n = len(ins)
        x, y, self.c = _my_place()
        self.me, self.sibling = (x, y, self.c), (x, y, 1 - self.c)
        self.chips = [(1 - x, y), (x, 1 - y), (1 - x, 1 - y)]

    @staticmethod
    def scratch(shards, axes):
        dims = [_half_dims(a.shape, axis) for a, axis in zip(shards, axes)]
        n = len(shards)
        return [pltpu.VMEM(d, F32) for d in dims] + [pltpu.VMEM(d, BF16) for d in dims] + [
            pltpu.SemaphoreType.DMA((7 * n,)), pltpu.SemaphoreType.DMA((7 * n,)), pltpu.SemaphoreType.DMA((2 * n,))]

    @staticmethod
    def out_shapes(shards, axes):
        return tuple(jax.ShapeDtypeStruct((8,) + _half_dims(a.shape, axis), BF16) for a, axis in zip(shards, axes))

    def _blk(self, a, px, py, pc):
        return self.outs[a].at[4 * px + 2 * py + pc]

    def _copy(self, a, k, block, to, src=None):
        return pltpu.make_async_remote_copy(
            src_ref=self._blk(a, *block) if src is None else src, dst_ref=self._blk(a, *block),
            send_sem=self.send_sems.at[7 * a + k], recv_sem=self.recv_sems.at[7 * a + k], device_id=to,
            device_id_type=MESH)

    def _keep(self, a):
        return pltpu.make_async_copy(self.bf_bufs[a], self._blk(a, *self.me), self.local_sems.at[self.n + a])

    def _first(self, a):
        mine = [self._copy(a, 0, self.me, self.sibling, src=self.bf_bufs[a])]
        return mine + [self._copy(a, 1 + j, self.me, (*chip, self.c), src=self.bf_bufs[a])
                       for j, chip in enumerate(self.chips)]

    def send_mine(self):
        loads = [pltpu.make_async_copy(_half_of(self.ins[a], self.axes[a], self.c), self.f32_bufs[a],
                                       self.local_sems.at[a]) for a in range(self.n)]
        for cp in loads:
            cp.start()
        for a in range(self.n):
            loads[a].wait()
            self.bf_bufs[a][...] = self.f32_bufs[a][...].astype(BF16)
            self._keep(a).start()
            for cp in self._first(a):
                cp.start()

    def pass_on(self):
        for a in range(self.n):
            for j, chip in enumerate(self.chips):
                self._copy(a, 1 + j, (*chip, self.c), self.me).wait_recv()
                self._copy(a, 4 + j, (*chip, self.c), self.sibling).start()

    def finish(self):
        for a in range(self.n):
            self._copy(a, 0, self.sibling, self.me).wait_recv()
            for j, chip in enumerate(self.chips):
                self._copy(a, 4 + j, (*chip, 1 - self.c), self.me).wait_recv()
        for a in range(self.n):
            for cp in self._first(a):
                cp.wait_send()
            for j, chip in enumerate(self.chips):
                self._copy(a, 4 + j, (*chip, self.c), self.sibling).wait_send()
            self._keep(a).wait()


def _all_gather_weights(shards, axes):
    n = len(shards)

    def body(*refs):
        gather = _HalfGather(refs[0:n], refs[n:2 * n], axes, refs[2 * n:3 * n], refs[3 * n:4 * n], *refs[4 * n:])
        gather.send_mine()
        gather.pass_on()
        gather.finish()

    any_spec = pl.BlockSpec(memory_space=pl.ANY)
    return pl.pallas_call(
        body, name="weights_all_gather", out_shape=_HalfGather.out_shapes(shards, axes),
        in_specs=[any_spec] * n, out_specs=(any_spec,) * n, scratch_shapes=_HalfGather.scratch(shards, axes),
        compiler_params=pltpu.CompilerParams(vmem_limit_bytes=VMEM_LIMIT),
    )(*shards)


class _ShardReduce:
    SEMS = 8
    LOCAL = 5

    def __init__(self, g_refs, out_refs, axes, bufs, send_sems, recv_sems, local_sems):
        self.g_refs, self.out_refs, self.axes = g_refs, out_refs, axes
        self.recv_a, self.own_a, self.send_b, self.recv_b, self.fin = bufs
        self.send_sems, self.recv_sems, self.local_sems = send_sems, recv_sems, local_sems
        self.n = len(g_refs)
        x, y, self.c = _my_place()
        self.chip = 2 * x + y
        self.sibling = (x, y, 1 - self.c)

    @staticmethod
    def scratch(gparts, axes):
        assert all(len(g.shape) == 3 or axis == 1 for g, axis in zip(gparts, axes))
        dims = [_half_dims(_shard_shape(g), axis) for g, axis in zip(gparts, axes)]
        windows = [d if len(g.shape) == 3 else (_shard_window(g),) + d[1:] for g, d in zip(gparts, dims)]
        shapes = []
        for dtype, lead, per_array in ((F32, (4,), windows), (F32, (4,), windows), (BF16, (4,), dims),
                                       (BF16, (4,), dims), (F32, (), dims)):
            shapes += [pltpu.VMEM(lead + d, dtype) for d in per_array]
        return shapes

    def _shard_half(self, a, j, core):
        g = self.g_refs[a]
        if len(g.shape) == 3:
            return _half_of(g.at[j], self.axes[a], core)
        start = (j * _shard_shape(g)[0]) // F32_ROWS * F32_ROWS
        return _half_of(g.at[pl.ds(pl.multiple_of(start, F32_ROWS), _shard_window(g))], self.axes[a], core)

    def _to_sibling(self, a, j):
        return pltpu.make_async_remote_copy(
            src_ref=self._shard_half(a, j, 1 - self.c), dst_ref=self.recv_a[a].at[j],
            send_sem=self.send_sems.at[self.SEMS * a + j], recv_sem=self.recv_sems.at[self.SEMS * a + j], device_id=self.sibling,
            device_id_type=MESH)

    def _own(self, a, j):
        return pltpu.make_async_copy(self._shard_half(a, j, self.c), self.own_a[a].at[j],
                                     self.local_sems.at[self.LOCAL * a + j])

    def _to_chip(self, a, k):
        dest = (self.chip + k) % 4
        return pltpu.make_async_remote_copy(
            src_ref=self.send_b[a].at[dest], dst_ref=self.recv_b[a].at[self.chip],
            send_sem=self.send_sems.at[self.SEMS * a + 3 + k], recv_sem=self.recv_sems.at[self.SEMS * a + 3 + k],
            device_id=(dest // 2, dest % 2, self.c), device_id_type=MESH)

    def _give(self, a):
        return pltpu.make_async_remote_copy(
            src_ref=self.fin[a], dst_ref=_half_of(self.out_refs[a], self.axes[a], self.c),
            send_sem=self.send_sems.at[self.SEMS * a + 7], recv_sem=self.recv_sems.at[self.SEMS * a + 7], device_id=self.sibling,
            device_id_type=MESH)

    def _mine(self, a):
        return pltpu.make_async_copy(self.fin[a], _half_of(self.out_refs[a], self.axes[a], self.c),
                                     self.local_sems.at[self.LOCAL * a])

    def exchange_with_sibling(self):
        for k in (1, 2, 3, 0):
            j = (self.chip + k) % 4
            for a in range(self.n):
                self._to_sibling(a, j).start()
                self._own(a, j).start()

    def _chip_partial(self, a, j):
        self._own(a, j).wait()
        self._to_sibling(a, j).wait_recv()
        g = self.g_refs[a]
        if len(g.shape) == 3:
            self.send_b[a][j] = (self.own_a[a][j] + self.recv_a[a][j]).astype(BF16)
            return
        rows = _shard_shape(g)[0]
        for shard in range(4):
            @pl.when(j == shard)
            def _():
                at = pl.ds((shard * rows) % F32_ROWS, rows)
                self.send_b[a][shard] = (self.own_a[a][shard, at, :] + self.recv_a[a][shard, at, :]).astype(BF16)

    def send_to_chip(self, k):
        for a in range(self.n):
            self._chip_partial(a, (self.chip + k) % 4)
            self._to_chip(a, k).start()

    def keep_mine(self):
        for a in range(self.n):
            self._chip_partial(a, self.chip)
            keep = pltpu.make_async_copy(self.send_b[a].at[self.chip], self.recv_b[a].at[self.chip],
                                         self.local_sems.at[self.LOCAL * a + 4])
            keep.start()
            keep.wait()

    def sum_and_share(self):
        for a in range(self.n):
            for k in range(1, 4):
                self._to_chip(a, k).wait_recv()
            tot = self.recv_b[a][0].astype(F32) + self.recv_b[a][1].astype(F32)
            tot = tot + self.recv_b[a][2].astype(F32)
            self.fin[a][...] = tot + self.recv_b[a][3].astype(F32)
            self._give(a).start()
            self._mine(a).start()

    def finish(self):
        for a in range(self.n):
            self._give(a).wait_recv()
            self._mine(a).wait()
            self._give(a).wait_send()
            for j in range(4):
                self._to_sibling(a, j).wait_send()
            for k in range(1, 4):
                self._to_chip(a, k).wait_send()


def _mem_tokens_fwd(mem_ref, g_ref, w_ref, kg_ref, mn_ref, kv_ref, kn_ref, vm_ref):
    xm = mem_ref[...]
    rr = lax.rsqrt(jnp.mean(xm * xm, axis=-1, keepdims=True) + EPS)
    mnb = ((xm * rr) * g_ref[...]).astype(BF16)
    mn_ref[...] = mnb
    kv = _dot(mnb, w_ref[...])
    kv_ref[...] = kv
    lo = _lane_lo((xm.shape[0], LANES))
    for p in range(MEM_WIDTH // LANES):
        sl = slice(p * LANES, (p + 1) * LANES)
        kb = kv[:, sl]
        kn_ref[:, sl] = ((kb * _head_rms(kb, lo)) * kg_ref[:, sl]).astype(BF16)
    vm_ref[...] = kv[:, MEM_WIDTH:].astype(BF16)


AUG_LO = 64
KEY_SUM_LANE = 72
QUERY_SUM_LANE = 80
HEAD_BLOCKS = FOX_HEADS * LANES


def _ones3(lane):
    return jnp.where((lane >= AUG_LO) & (lane < AUG_LO + 3), 1.0, 0.0)


def _spread3(cols):
    hi = cols.astype(BF16)
    rest = cols - hi.astype(F32)
    mid = rest.astype(BF16)
    low = (rest - mid.astype(F32)).astype(BF16)
    r = lax.broadcasted_iota(jnp.int32, (LANES, HEAD_BLOCKS), 0)
    c = lax.broadcasted_iota(jnp.int32, (LANES, HEAD_BLOCKS), 1)
    out = None
    for k, part in enumerate((hi, mid, low)):
        term = _dot(part, jnp.where(c == r * LANES + (AUG_LO + k), 1.0, 0.0).astype(BF16))
        out = term if out is None else out + term
    return out


def _head_block(pair_blk, hh, lo, extras):
    src = pair_blk if hh == 0 else pltpu.roll(pair_blk, HEAD_DIM, axis=1)
    return jnp.where(lo, src, extras).astype(BF16)


def _pair_block(blk0, blk1, lo):
    return jnp.where(lo, blk0, pltpu.roll(blk1, HEAD_DIM, axis=1))


def _assemble_w_in(halves_ref, words_ref, wp_ref):
    shard = IN_WIDTH // 4
    half = D_MODEL // 2
    f_hi = F_ORIG_LO + FOX_HEADS
    for j in range(4):
        blocks = [pltpu.bitcast(halves_ref[2 * j + c], jnp.uint32) for c in range(2)]
        for lo, hi, to in ((0, F_ORIG_LO, PA_LO), (F_ORIG_LO, f_hi, FB_LO), (f_hi, IN_WIDTH, GB_LO)):
            a, b = max(lo, shard * j), min(hi, shard * (j + 1))
            if a < b:
                for c in range(2):
                    words_ref[(to + a - lo) // 2:(to + b - lo) // 2, c * half:(c + 1) * half] = (
                        blocks[c][(a - shard * j) // 2:(b - shard * j) // 2, :])
    pad_lo = (FB_LO + FOX_HEADS) // 2
    words_ref[pad_lo:, :] = jnp.zeros((PROJ_PAD // 2 - pad_lo, D_MODEL), jnp.uint32)
    wp_ref[...] = pltpu.bitcast(words_ref[...], BF16)


def _fwd_in(x, norm_g, halves, bf_pad, fq_g, fk_g):
    s = x.shape[0]
    t = TILE
    n = s // t

    def body(x_ref, ng_ref, halves_ref, bf_ref, qg_ref, kg_ref,
             h_ref, pa_ref, qk_ref, qa_ref, ka_ref, va_ref, gb_ref, pm_ref, fb_ref, wp_ref,
             carry_ref, fcol_ref, words_ref):
        @pl.when(pl.program_id(0) == 0)
        def _():
            carry_ref[...] = jnp.zeros_like(carry_ref)
            _assemble_w_in(halves_ref, words_ref, wp_ref)

        xv = x_ref[...]
        rr = lax.rsqrt(jnp.mean(xv * xv, axis=-1, keepdims=True) + EPS)
        hb = ((xv * rr) * ng_ref[...]).astype(BF16)
        h_ref[...] = hb

        def proj(lo, hi):
            return _dot(hb, wp_ref[lo:hi, :], NT)

        fb = proj(FB_LO, PROJ_PAD)
        fb_ref[...] = fb
        qk_ref[:, 0:FOX_WIDTH] = proj(QB_LO, KB_LO)

        lane = lax.broadcasted_iota(jnp.int32, (t, LANES), 1)
        row = lax.broadcasted_iota(jnp.int32, (t, LANES), 0)
        lo = lane < HEAD_DIM
        z = fb + bf_ref[...]
        lf = -(jnp.maximum(-z, 0.0) + jnp.log1p(jnp.exp(-jnp.abs(z))))
        lf = jnp.where(lane < FOX_HEADS, lf, 0.0)
        sh = 1
        while sh < t:
            lf = lf + jnp.where(row >= sh, pltpu.roll(lf, sh, axis=0), 0.0)
            sh *= 2
        fcum = lf + carry_ref[...]
        fcol_ref[...] = fcum
        carry_ref[...] = fcol_ref[t - 1:t, :]

        ones3 = _ones3(lane)
        minus_f = _spread3(-fcum)

        def head_blocks(seg, g_ref, out_ref, scale):
            for p in range(FOX_WIDTH // LANES):
                sl = slice(p * LANES, (p + 1) * LANES)
                blk = qk_ref[:, seg - QB_LO + p * LANES:seg - QB_LO + (p + 1) * LANES]
                normed = ((blk * _head_rms(blk, lo)) * g_ref[:, sl]) * scale
                for hh in range(2):
                    h = 2 * p + hh
                    if seg == QB_LO:
                        extras = jnp.where(lane == QUERY_SUM_LANE + h, 1.0, ones3)
                    else:
                        extras = jnp.where(lane == KEY_SUM_LANE + h, 1.0, minus_f[:, h * LANES:(h + 1) * LANES])
                    out_ref[:, h * LANES:(h + 1) * LANES] = _head_block(normed, hh, lo, extras)

        qk_ref[:, FOX_WIDTH:2 * FOX_WIDTH] = proj(KB_LO, VB_LO)
        pa_ref[...] = proj(PA_LO, QB_LO)
        head_blocks(QB_LO, qg_ref, qa_ref, ATT_SCALE)
        vraw = proj(VB_LO, GB_LO)
        gb_ref[...] = proj(GB_LO, PM_LO)
        head_blocks(KB_LO, kg_ref, ka_ref, 1.0)
        pm_ref[...] = proj(PM_LO, FB_LO)
        for h in range(FOX_HEADS):
            va_ref[:, h * LANES:(h + 1) * LANES] = _head_block(vraw[:, (h // 2) * LANES:(h // 2 + 1) * LANES], h % 2, lo, ones3)

    outs = (
        jax.ShapeDtypeStruct((s, D_MODEL), BF16),
        jax.ShapeDtypeStruct((s, 512), F32),
        jax.ShapeDtypeStruct((s, 2 * FOX_WIDTH), F32),
        jax.ShapeDtypeStruct((s, HEAD_BLOCKS), BF16),
        jax.ShapeDtypeStruct((s, HEAD_BLOCKS), BF16),
        jax.ShapeDtypeStruct((s, HEAD_BLOCKS), BF16),
        jax.ShapeDtypeStruct((s, FOX_WIDTH), F32),
        jax.ShapeDtypeStruct((s, 512), F32),
        jax.ShapeDtypeStruct((s, LANES), F32),
        jax.ShapeDtypeStruct((PROJ_PAD, D_MODEL), BF16),
    )

    def resident(shape):
        return pl.BlockSpec(shape, lambda i: (0,) * len(shape), pipeline_mode=pl.Buffered(1))

    *fwd, wp = pl.pallas_call(
        body, name="fwd_in", grid=(n,), out_shape=outs,
        in_specs=[_rows(t, D_MODEL), _full((1, D_MODEL)), resident(halves.shape), _full((1, LANES)),
                  _full((1, FOX_WIDTH)), _full((1, FOX_WIDTH))],
        out_specs=(_rows(t, D_MODEL), _rows(t, 512), _rows(t, 2 * FOX_WIDTH), _rows(t, HEAD_BLOCKS),
                   _rows(t, HEAD_BLOCKS), _rows(t, HEAD_BLOCKS), _rows(t, FOX_WIDTH), _rows(t, 512),
                   _rows(t, LANES), resident((PROJ_PAD, D_MODEL))),
        scratch_shapes=[pltpu.VMEM((1, LANES), F32), pltpu.VMEM((t, LANES), F32),
                        pltpu.VMEM((PROJ_PAD // 2, D_MODEL), jnp.uint32)],
        compiler_params=_params(),
    )(x, norm_g, halves, bf_pad, fq_g, fk_g)
    return tuple(fwd), wp


POOL_HALO = 16


def _pool_window(lane):
    return jnp.where(lane < 64, 2.0, jnp.where(lane < 128, 4.0, jnp.where(lane < 192, 8.0, 16.0)))


def _pool_pick(lane, s2, s4, s8, s16):
    return jnp.where(lane < 64, s2, jnp.where(lane < 128, s4, jnp.where(lane < 192, s8, s16)))


def _group_onehot(shape, row_is_group_lane):
    r = lax.broadcasted_iota(jnp.int32, shape, 0)
    c = lax.broadcasted_iota(jnp.int32, shape, 1)
    hit = (r % HEAD_DIM == c) if row_is_group_lane else (c % HEAD_DIM == r)
    return jnp.where(hit, 1.0, 0.0).astype(F32)


def _same_group(shape):
    r = lax.broadcasted_iota(jnp.int32, shape, 0)
    c = lax.broadcasted_iota(jnp.int32, shape, 1)
    return (r // HEAD_DIM) == (c // HEAD_DIM)


def _pool_block_diag(w4):
    spread = jnp.dot(w4, _group_onehot((HEAD_DIM, POOL_WIDTH), False), preferred_element_type=F32,
                     precision=lax.Precision.HIGHEST)
    return jnp.where(_same_group((POOL_WIDTH, POOL_WIDTH)), spread, 0.0).astype(BF16)


def _mem_softmax(qm, kp):
    sc = _dot(qm, kp, NT)
    e = jnp.exp(sc - jnp.max(sc, axis=-1, keepdims=True))
    return e * (1.0 / jnp.sum(e, axis=-1, keepdims=True))


def _side_fwd(pa, pm, w4, pscale, mq_g, mem, mem_norm_g, w_kv, mk_g):
    s = pa.shape[0]
    t = TILE
    n = s // t
    ext = t + POOL_HALO
    nm = mem.shape[0]

    def body(pa_ref, pm_ref, w4_ref, sc_ref, g_ref, mem_ref, mg_ref, wkv_ref, kg_ref,
             ma_ref, d_ref, mm_ref, mn_ref, kv_ref, k_ref, v_ref, ext_ref, w_ref):
        i = pl.program_id(0)

        @pl.when(i == 0)
        def _():
            ext_ref[0:POOL_HALO, :] = jnp.zeros((POOL_HALO, POOL_WIDTH), F32)
            w_ref[...] = _pool_block_diag(w4_ref[...])
            _mem_tokens_fwd(mem_ref, mg_ref, wkv_ref, kg_ref, mn_ref, kv_ref, k_ref, v_ref)

        u = pa_ref[:, 0:POOL_WIDTH]
        ext_ref[POOL_HALO:ext, :] = u
        e = ext_ref[...]
        s2 = e + pltpu.roll(e, 1, axis=0)
        s4 = s2 + pltpu.roll(s2, 2, axis=0)
        s8 = s4 + pltpu.roll(s4, 4, axis=0)
        s16 = s8 + pltpu.roll(s8, 8, axis=0)
        lane_e = lax.broadcasted_iota(jnp.int32, (ext, POOL_WIDTH), 1)
        win = _pool_pick(lane_e, s2, s4, s8, s16)[POOL_HALO:ext, :]
        lane = lax.broadcasted_iota(jnp.int32, (t, POOL_WIDTH), 1)
        pos = (lax.broadcasted_iota(jnp.int32, (t, POOL_WIDTH), 0) + (i * t + 1)).astype(F32)
        d = win / jnp.minimum(pos, _pool_window(lane)) - u
        db = d.astype(BF16)
        d_ref[...] = db
        ya = _dot(db, w_ref[...]) * sc_ref[...]
        ga = pa_ref[:, POOL_WIDTH:2 * POOL_WIDTH]
        ma_ref[...] = (ya * (ga * _sig(ga))).astype(BF16)
        ext_ref[0:POOL_HALO, :] = ext_ref[t:ext, :]

        lo = _lane_lo((t, LANES))
        for p in range(MEM_WIDTH // LANES):
            sl = slice(p * LANES, (p + 1) * LANES)
            qb = pm_ref[:, sl]
            qs = (((qb * _head_rms(qb, lo)) * g_ref[:, sl]) * ATT_SCALE).astype(BF16)
            kp = k_ref[:, sl]
            vp = v_ref[:, sl]
            outs = []
            for hh in range(2):
                msk = lo if hh == 0 else jnp.logical_not(lo)
                prob = _mem_softmax(jnp.where(msk, qs, jnp.zeros_like(qs)), kp)
                outs.append(_dot(prob.astype(BF16), vp))
            o = jnp.where(lo, outs[0], outs[1])
            gm = pm_ref[:, MEM_WIDTH + p * LANES:MEM_WIDTH + (p + 1) * LANES]
            mm_ref[:, sl] = (o * (gm * _sig(gm))).astype(BF16)

    return pl.pallas_call(
        body, name="side_fwd", grid=(n,),
        out_shape=(jax.ShapeDtypeStruct((s, POOL_WIDTH), BF16), jax.ShapeDtypeStruct((s, POOL_WIDTH), BF16),
                   jax.ShapeDtypeStruct((s, MEM_WIDTH), BF16), jax.ShapeDtypeStruct((nm, D_MODEL), BF16),
                   jax.ShapeDtypeStruct((nm, 2 * MEM_WIDTH), F32), jax.ShapeDtypeStruct((nm, MEM_WIDTH), BF16),
                   jax.ShapeDtypeStruct((nm, MEM_WIDTH), BF16)),
        in_specs=[_rows(t, 512), _rows(t, 512), _full((POOL_ROWS, HEAD_DIM)), _full((1, POOL_WIDTH)),
                  _full((1, MEM_WIDTH)), _full((nm, D_MODEL)), _full((1, D_MODEL)), _full((D_MODEL, 2 * MEM_WIDTH)),
                  _full((1, MEM_WIDTH))],
        out_specs=(_rows(t, POOL_WIDTH), _rows(t, POOL_WIDTH), _rows(t, MEM_WIDTH), _full((nm, D_MODEL)),
                   _full((nm, 2 * MEM_WIDTH)), _full((nm, MEM_WIDTH)), _full((nm, MEM_WIDTH))),
        scratch_shapes=[pltpu.VMEM((ext, POOL_WIDTH), F32), pltpu.VMEM((POOL_WIDTH, POOL_WIDTH), BF16)],
        compiler_params=_params(),
    )(pa, pm, w4, pscale, mq_g, mem, mem_norm_g, w_kv, mk_g)


FOX_FWD_HEADS = 4


def _fox_fwd(qa, ka, va, gb):
    s = qa.shape[0]
    t = TILE
    n = s // t
    heads = FOX_FWD_HEADS
    pairs = heads // 2
    group_w = heads * LANES

    def body(qa_ref, ka_ref, va_ref, gb_ref, o_ref, mb_ref, r_ref):
        i = pl.program_id(1)
        lane = lax.broadcasted_iota(jnp.int32, (t, LANES), 1)
        lo = lane < HEAD_DIM
        causal = lax.broadcasted_iota(jnp.int32, (t, t), 1) <= lax.broadcasted_iota(jnp.int32, (t, t), 0)
        qas = [qa_ref[:, hh * LANES:(hh + 1) * LANES] for hh in range(heads)]

        def step(j, carry, masked):
            rows = pl.ds(pl.multiple_of(j * t, t), t)
            def logits(hh):
                sc = _dot(qas[hh], ka_ref[rows, hh * LANES:(hh + 1) * LANES], NT)
                return jnp.where(causal, sc, -1e30) if masked else sc

            def advance(hh, sc):
                m, acc = carry[hh]
                m_new = jnp.maximum(m, jnp.max(sc, axis=-1, keepdims=True))
                p = jnp.exp(sc - m_new).astype(BF16)
                return m_new, jnp.exp(m - m_new) * acc + _dot(p, va_ref[rows, hh * LANES:(hh + 1) * LANES])

            new = []
            sc = logits(0)
            for hh in range(heads):
                sc_next = logits(hh + 1) if hh + 1 < heads else None
                new.append(advance(hh, sc))
                sc = sc_next
            return tuple(new)

        init = (jnp.full((t, 1), -1e30, F32), jnp.zeros((t, LANES), F32))
        carry = lax.fori_loop(0, i, functools.partial(step, masked=False), (init,) * heads)
        res = step(i, carry, masked=True)
        for p in range(pairs):
            outs = []
            rcol = jnp.zeros((t, LANES), F32)
            for hh in range(2):
                m, acc = res[2 * p + hh]
                l = _lane_pick(acc, lane, AUG_LO)
                outs.append(acc * (1.0 / l))
                rcol = jnp.where(lane == hh, m + jnp.log(l), rcol)
            o = _pair_block(outs[0], outs[1], lo)
            sl = slice(p * LANES, (p + 1) * LANES)
            o_ref[:, sl] = o
            g = gb_ref[:, sl]
            mb_ref[:, sl] = (o * (g * _sig(g))).astype(BF16)
            r_ref[p] = rcol

    tile_spec = pl.BlockSpec((t, pairs * LANES), lambda p, i: (i, p))
    full_spec = pl.BlockSpec((s, group_w), lambda p, i: (0, p))
    return pl.pallas_call(
        body, name="fox_fwd", grid=(FOX_HEADS // heads, n),
        out_shape=(jax.ShapeDtypeStruct((s, FOX_WIDTH), F32), jax.ShapeDtypeStruct((s, FOX_WIDTH), BF16),
                   jax.ShapeDtypeStruct((FOX_HEADS // 2, s, LANES), F32)),
        in_specs=[pl.BlockSpec((t, group_w), lambda p, i: (i, p)), full_spec, full_spec, tile_spec],
        out_specs=(tile_spec, tile_spec, pl.BlockSpec((pairs, t, LANES), lambda p, i: (p, i, 0))),
        compiler_params=_params(2),
    )(qa, ka, va, gb)


def _out_loss(x, tgt, ma, mb, mm, wout, gb, o, r4):
    s = x.shape[0]
    t = TILE
    n = s // t
    pairs = FOX_HEADS // 2

    def body(x_ref, t_ref, ma_ref, mb_ref, mm_ref, w_ref, gb_ref, o_ref, r_ref,
             dy_ref, dma_ref, dmm_ref, dw_ref, loss_ref, doa_ref, dgb_ref, rr_ref, mix_ref):
        @pl.when(pl.program_id(0) == 0)
        def _():
            dw_ref[...] = jnp.zeros_like(dw_ref)
            loss_ref[...] = jnp.zeros_like(loss_ref)

        mix_ref[:, 0:256] = ma_ref[...]
        mix_ref[:, 256:768] = mb_ref[...]
        mix_ref[:, 768:1024] = mm_ref[...]
        mix = mix_ref[...]
        err = (x_ref[...] + _dot(mix, w_ref[...])) - t_ref[...]
        row_mean = jnp.sum(err * err, axis=-1, keepdims=True) * (1.0 / D_MODEL)
        loss_ref[...] += 0.5 * jnp.sum(row_mean, axis=0, keepdims=True)
        dy = err * (1.0 / D_MODEL)
        dy_ref[...] = dy
        dyb = dy.astype(BF16)
        dmix = _dot(dyb, w_ref[...], NT)
        dma_ref[...] = dmix[:, 0:256]
        dmm_ref[...] = dmix[:, 768:1024]
        dw_ref[...] += _dot(mix, dyb, TN)

        lane = lax.broadcasted_iota(jnp.int32, (t, LANES), 1)
        lo = lane < HEAD_DIM
        d_os = []
        delta = jnp.zeros((t, LANES), F32)
        for p in range(pairs):
            sl = slice(p * LANES, (p + 1) * LANES)
            g = gb_ref[:, sl]
            sg = _sig(g)
            dm = dmix[:, 256 + p * LANES:256 + (p + 1) * LANES]
            ov = o_ref[:, sl]
            d_o = dm * (g * sg)
            d_os.append(d_o)
            dgb_ref[:, sl] = (dm * ov * (sg * (1.0 + g * (1.0 - sg)))).astype(BF16)
            prod = d_o * ov
            delta = jnp.where(lane == 2 * p, jnp.sum(jnp.where(lo, prod, 0.0), axis=-1, keepdims=True), delta)
            delta = jnp.where(lane == 2 * p + 1, jnp.sum(jnp.where(lo, 0.0, prod), axis=-1, keepdims=True), delta)
            rr_ref[p, 0] = r_ref[p].T[0:8, :]
        minus_delta = _spread3(-delta)
        for h in range(FOX_HEADS):
            blk = slice(h * LANES, (h + 1) * LANES)
            doa_ref[:, blk] = _head_block(d_os[h // 2], h % 2, lo, minus_delta[:, blk])

    return pl.pallas_call(
        body, name="out_loss", grid=(n,),
        out_shape=(jax.ShapeDtypeStruct((s, D_MODEL), F32), jax.ShapeDtypeStruct((s, 256), F32),
                   jax.ShapeDtypeStruct((s, 256), F32), jax.ShapeDtypeStruct((D_MODEL, D_MODEL), F32),
                   jax.ShapeDtypeStruct((1, LANES), F32), jax.ShapeDtypeStruct((s, HEAD_BLOCKS), BF16),
                   jax.ShapeDtypeStruct((s, FOX_WIDTH), BF16), jax.ShapeDtypeStruct((pairs, n, 8, t), F32)),
        in_specs=[_rows(t, D_MODEL), _rows(t, D_MODEL), _rows(t, 256), _rows(t, 512), _rows(t, 256),
                  _full((D_MODEL, D_MODEL)), _rows(t, FOX_WIDTH), _rows(t, FOX_WIDTH),
                  pl.BlockSpec((pairs, t, LANES), lambda i: (0, i, 0))],
        out_specs=(_rows(t, D_MODEL), _rows(t, 256), _rows(t, 256), _full((D_MODEL, D_MODEL)), _full((1, LANES)),
                   _rows(t, HEAD_BLOCKS), _rows(t, FOX_WIDTH), pl.BlockSpec((pairs, 1, 8, t), lambda i: (0, i, 0, 0))),
        scratch_shapes=[pltpu.VMEM((t, D_MODEL), BF16)],
        compiler_params=_params(),
    )(x, tgt, ma, mb, mm, wout, gb, o, r4)


def _side_bwd(pa, db, dma, w4, pscale, pm, dmm, kmn, vmb, mq_g, kv, mnb, mem, w_kv, mk_g, mem_norm_g):
    s = pa.shape[0]
    t = TILE
    n = s // t
    ext = t + POOL_HALO
    nm = mem.shape[0]

    def body(pa_ref, d_ref, dma_ref, w4_ref, sc_ref, pm_ref, dmm_ref, k_ref, v_ref, g_ref,
             kv_ref, mn_ref, mem_ref, wkv_ref, kg_ref, mg_ref,
             dpa_ref, dpm_ref, dw4_ref, dsc_ref, dg_ref, dwkv_ref, dmg_ref, dkg_ref,
             ext_ref, w_ref, dw_ref, dk_ref, dv_ref, gacc_ref, dkv_ref):
        i = pl.program_id(0)

        @pl.when(i == 0)
        def _():
            dw_ref[...] = jnp.zeros_like(dw_ref)
            dsc_ref[...] = jnp.zeros_like(dsc_ref)
            ext_ref[t:ext, :] = jnp.zeros((POOL_HALO, POOL_WIDTH), F32)
            w_ref[...] = _pool_block_diag(w4_ref[...])
            dk_ref[...] = jnp.zeros_like(dk_ref)
            dv_ref[...] = jnp.zeros_like(dv_ref)
            gacc_ref[...] = jnp.zeros_like(gacc_ref)

        dbv = d_ref[...]
        z = _dot(dbv, w_ref[...])
        ga = pa_ref[:, POOL_WIDTH:2 * POOL_WIDTH]
        sg = _sig(ga)
        dma_v = dma_ref[...]
        dya = dma_v * (ga * sg)
        dpa_ref[:, POOL_WIDTH:2 * POOL_WIDTH] = (dma_v * (z * sc_ref[...]) * (sg * (1.0 + ga * (1.0 - sg)))).astype(BF16)
        dsc_ref[...] += jnp.sum(dya * z, axis=0, keepdims=True)
        dzb = (dya * sc_ref[...]).astype(BF16)
        dw_ref[...] += _dot(dbv, dzb, TN)
        dd = _dot(dzb, w_ref[...], NT)
        lane = lax.broadcasted_iota(jnp.int32, (t, POOL_WIDTH), 1)
        pos = (lax.broadcasted_iota(jnp.int32, (t, POOL_WIDTH), 0) + ((n - 1 - i) * t + 1)).astype(F32)
        ext_ref[0:t, :] = dd / jnp.minimum(pos, _pool_window(lane))
        e = ext_ref[...]
        s2 = e + pltpu.roll(e, ext - 1, axis=0)
        s4 = s2 + pltpu.roll(s2, ext - 2, axis=0)
        s8 = s4 + pltpu.roll(s4, ext - 4, axis=0)
        s16 = s8 + pltpu.roll(s8, ext - 8, axis=0)
        lane_e = lax.broadcasted_iota(jnp.int32, (ext, POOL_WIDTH), 1)
        win = _pool_pick(lane_e, s2, s4, s8, s16)[0:t, :]
        dpa_ref[:, 0:POOL_WIDTH] = (win - dd).astype(BF16)
        ext_ref[t:ext, :] = ext_ref[0:POOL_HALO, :]

        lo = _lane_lo((t, LANES))
        pairs = MEM_WIDTH // LANES
        pre = []
        for p in range(pairs):
            sl = slice(p * LANES, (p + 1) * LANES)
            qb = pm_ref[:, sl]
            rr = _head_rms(qb, lo)
            qhat = qb * rr
            g = g_ref[:, sl]
            qs = ((qhat * g) * ATT_SCALE).astype(BF16)
            gm = pm_ref[:, MEM_WIDTH + p * LANES:MEM_WIDTH + (p + 1) * LANES]
            sg = _sig(gm)
            dmo = dmm_ref[:, sl]
            pre.append((sl, rr, qhat, g, qs, gm, sg, dmo, dmo * (gm * sg)))

        def front(p, hh):
            sl, _, _, _, qs, _, _, _, d_o = pre[p]
            msk = lo if hh == 0 else jnp.logical_not(lo)
            qm = jnp.where(msk, qs, jnp.zeros_like(qs))
            prob = _mem_softmax(qm, k_ref[:, sl])
            dom = jnp.where(msk, d_o, 0.0).astype(BF16)
            return qm, prob, dom, _dot(dom, v_ref[:, sl], NT)

        def back(p, qm, prob, dom, dp):
            sl = pre[p][0]
            pb = prob.astype(BF16)
            out = _dot(pb, v_ref[:, sl])
            ds = (prob * (dp - jnp.sum(prob * dp, axis=-1, keepdims=True))).astype(BF16)
            dq = _dot(ds, k_ref[:, sl])
            dk_ref[:, sl] += _dot(ds, qm, TN)
            dv_ref[:, sl] += _dot(pb, dom, TN)
            return out, dq

        heads = [(p, hh) for p in range(pairs) for hh in range(2)]
        done = []
        fronts = [front(*head) for head in heads]
        for k, (p, _) in enumerate(heads):
            done.append(back(p, *fronts[k]))
        for p in range(pairs):
            sl, rr, qhat, g, _, gm, sg, dmo, _ = pre[p]
            outs, dqs = zip(done[2 * p], done[2 * p + 1])
            o = jnp.where(lo, outs[0], outs[1])
            dqn = jnp.where(lo, dqs[0], dqs[1]) * ATT_SCALE
            dpm_ref[:, sl] = _head_norm_bwd(dqn, qhat, rr, g, lo).astype(BF16)
            dpm_ref[:, MEM_WIDTH + p * LANES:MEM_WIDTH + (p + 1) * LANES] = (
                dmo * o * (sg * (1.0 + gm * (1.0 - sg)))).astype(BF16)
            gacc_ref[:, sl] += jnp.sum(dqn * qhat, axis=0, keepdims=True)

        @pl.when(i == n - 1)
        def _():
            own = jnp.where(_same_group((POOL_WIDTH, POOL_WIDTH)), dw_ref[...], 0.0)
            dw4_ref[...] = jnp.dot(own, _group_onehot((POOL_WIDTH, HEAD_DIM), True), preferred_element_type=F32,
                                   precision=lax.Precision.HIGHEST)
            dg_ref[...] = _fold_heads(gacc_ref[...])

            lo_m = _lane_lo((nm, LANES))
            kacc = []
            for p in range(MEM_WIDTH // LANES):
                sl = slice(p * LANES, (p + 1) * LANES)
                kb = kv_ref[:, sl]
                rr = _head_rms(kb, lo_m)
                khat = kb * rr
                dk = dk_ref[:, sl]
                dkv_ref[:, sl] = _head_norm_bwd(dk, khat, rr, kg_ref[:, sl], lo_m).astype(BF16)
                kacc.append(jnp.sum(dk * khat, axis=0, keepdims=True))
            dkg_ref[...] = _fold_heads(jnp.concatenate(kacc, axis=1))
            dkv_ref[:, MEM_WIDTH:] = dv_ref[...].astype(BF16)
            dkv = dkv_ref[...]
            dwkv_ref[...] = _dot(mn_ref[...], dkv, TN)
            dmn = _dot(dkv, wkv_ref[...], NT)
            xm = mem_ref[...]
            rr = lax.rsqrt(jnp.mean(xm * xm, axis=-1, keepdims=True) + EPS)
            dmg_ref[...] = jnp.sum(dmn * (xm * rr), axis=0, keepdims=True)

    def rev(w):
        return _rows_rev(t, w, n)

    row = jax.ShapeDtypeStruct((1, LANES), F32)
    return pl.pallas_call(
        body, name="side_bwd", grid=(n,),
        out_shape=(jax.ShapeDtypeStruct((s, 512), BF16), jax.ShapeDtypeStruct((s, 512), BF16),
                   jax.ShapeDtypeStruct((POOL_ROWS, HEAD_DIM), F32), jax.ShapeDtypeStruct((1, POOL_WIDTH), F32), row,
                   jax.ShapeDtypeStruct((D_MODEL, 2 * MEM_WIDTH), F32), jax.ShapeDtypeStruct((1, D_MODEL), F32), row),
        in_specs=[rev(512), rev(POOL_WIDTH), rev(POOL_WIDTH), _full((POOL_ROWS, HEAD_DIM)), _full((1, POOL_WIDTH)),
                  rev(512), rev(MEM_WIDTH), _full((N_MEM, MEM_WIDTH)), _full((N_MEM, MEM_WIDTH)), _full((1, MEM_WIDTH)),
                  _full((nm, 2 * MEM_WIDTH)), _full((nm, D_MODEL)), _full((nm, D_MODEL)),
                  _full((D_MODEL, 2 * MEM_WIDTH)), _full((1, MEM_WIDTH)), _full((1, D_MODEL))],
        out_specs=(rev(512), rev(512), _full((POOL_ROWS, HEAD_DIM)), _full((1, POOL_WIDTH)), _full((1, LANES)),
                   _full((D_MODEL, 2 * MEM_WIDTH)), _full((1, D_MODEL)), _full((1, LANES))),
        scratch_shapes=[pltpu.VMEM((ext, POOL_WIDTH), F32), pltpu.VMEM((POOL_WIDTH, POOL_WIDTH), BF16),
                        pltpu.VMEM((POOL_WIDTH, POOL_WIDTH), F32), pltpu.VMEM((N_MEM, MEM_WIDTH), F32),
                        pltpu.VMEM((N_MEM, MEM_WIDTH), F32), pltpu.VMEM((1, MEM_WIDTH), F32),
                        pltpu.VMEM((nm, 2 * MEM_WIDTH), BF16)],
        compiler_params=_params(),
    )(pa, db, dma, w4, pscale, pm, dmm, kmn, vmb, mq_g, kv, mnb, mem, w_kv, mk_g, mem_norm_g)


FOX_BWD_HEADS = 4


def _fox_bwd(ka, va, qa, doa, rr, gparts, axes):
    s = ka.shape[0]
    t = TILE
    n = s // t
    heads = FOX_BWD_HEADS
    groups = FOX_HEADS // heads
    group_w = heads * LANES
    na = len(gparts)

    def body(*refs):
        ka_ref, va_ref, qa_ref, doa_ref, rr_ref = refs[0:5]
        g_refs = refs[5:5 + na]
        dka_ref, dva_ref, dqa_ref = refs[5 + na:8 + na]
        out_refs = refs[8 + na:8 + 2 * na]
        bufs = tuple(refs[8 + (2 + k) * na:8 + (3 + k) * na] for k in range(5))
        j = pl.program_id(1)
        step_id = pl.program_id(0) * n + j
        red = _ShardReduce(g_refs, out_refs, axes, bufs, *refs[8 + 7 * na:]) if na else None

        @pl.when(j == 0)
        def _():
            dqa_ref[...] = jnp.zeros_like(dqa_ref)

        if red is not None:
            pl.when(step_id == 0)(red.exchange_with_sibling)

            @pl.when(step_id == 1)
            def _():
                for k in (1, 2, 3):
                    red.send_to_chip(k)
                red.keep_mine()

        causal = lax.broadcasted_iota(jnp.int32, (t, t), 0) <= lax.broadcasted_iota(jnp.int32, (t, t), 1)
        kas = [ka_ref[:, hh * LANES:(hh + 1) * LANES] for hh in range(heads)]
        vas = [va_ref[:, hh * LANES:(hh + 1) * LANES] for hh in range(heads)]

        def step(i, carry, masked):
            rows = pl.ds(pl.multiple_of(i * t, t), t)
            new = []
            for hh in range(heads):
                cols = slice(hh * LANES, (hh + 1) * LANES)
                dk_a, dv_a = carry[hh]
                qb = qa_ref[rows, cols]
                d_o = doa_ref[rows, cols]
                arg = _dot(kas[hh], qb, NT) - rr_ref[hh // 2, i, hh % 2:hh % 2 + 1, :]
                if masked:
                    arg = jnp.where(causal, arg, -1e30)
                pt = jnp.exp(arg)
                dst = (pt * _dot(vas[hh], d_o, NT)).astype(BF16)
                dv_a = dv_a + _dot(pt.astype(BF16), d_o)
                dk_a = dk_a + _dot(dst, qb)
                dqa_ref[rows, cols] += _dot(dst, kas[hh], TN)
                new.append((dk_a, dv_a))
            return tuple(new)

        zero = jnp.zeros((t, LANES), F32)
        carry = step(j, ((zero, zero),) * heads, masked=True)
        res = lax.fori_loop(j + 1, n, functools.partial(step, masked=False), carry)
        for hh in range(heads):
            cols = slice(hh * LANES, (hh + 1) * LANES)
            dka_ref[:, cols] = res[hh][0]
            dva_ref[:, cols] = res[hh][1]

        if red is not None:
            @pl.when(step_id == groups * n - 1)
            def _():
                red.sum_and_share()
                red.finish()

    tile_spec = pl.BlockSpec((t, group_w), lambda p, j: (j, p))
    full_spec = pl.BlockSpec((s, group_w), lambda p, j: (0, p))
    any_spec = pl.BlockSpec(memory_space=pl.ANY)
    scratch = _ShardReduce.scratch(gparts, axes)
    if na:
        scratch += [pltpu.SemaphoreType.DMA((_ShardReduce.SEMS * na,)), pltpu.SemaphoreType.DMA((_ShardReduce.SEMS * na,)),
                    pltpu.SemaphoreType.DMA((_ShardReduce.LOCAL * na,))]
    return pl.pallas_call(
        body, name="fox_bwd", grid=(groups, n),
        out_shape=(jax.ShapeDtypeStruct((s, HEAD_BLOCKS), F32),) * 3
        + tuple(jax.ShapeDtypeStruct(_shard_shape(g), F32) for g in gparts),
        in_specs=[tile_spec, tile_spec, full_spec, full_spec,
                  pl.BlockSpec((heads // 2, n, 8, t), lambda p, j: (p, 0, 0, 0))] + [any_spec] * na,
        out_specs=(tile_spec, tile_spec, full_spec) + (any_spec,) * na,
        scratch_shapes=scratch, compiler_params=_params(2, VMEM_LIMIT_FOX_BWD),
    )(ka, va, qa, doa, rr, *gparts)


def _fox_post_tile(i, n, t, dqa_ref, dka_ref, dva_ref, qk_ref, fb_ref, bf_ref, qg_ref, kg_ref,
                   dqk_ref, dv_ref, dfb_ref, dqg_ref, dkg_ref, dbf_ref, qacc_ref, kacc_ref, carry_ref,
                   between):
    @pl.when(i == 0)
    def _():
        qacc_ref[...] = jnp.zeros_like(qacc_ref)
        kacc_ref[...] = jnp.zeros_like(kacc_ref)
        dbf_ref[...] = jnp.zeros_like(dbf_ref)
        carry_ref[...] = jnp.zeros_like(carry_ref)

    lane = lax.broadcasted_iota(jnp.int32, (t, LANES), 1)
    row = lax.broadcasted_iota(jnp.int32, (t, LANES), 0)
    lo = lane < HEAD_DIM

    def head_blocks(ref, p):
        return ref[:, 2 * p * LANES:(2 * p + 1) * LANES], ref[:, (2 * p + 1) * LANES:(2 * p + 2) * LANES]

    def issue(k):
        if between[k] is not None:
            between[k]()

    sums = []
    pairs = FOX_WIDTH // LANES
    for side, (src_ref, g_ref, acc_ref, scale) in enumerate(((dqa_ref, qg_ref, qacc_ref, ATT_SCALE),
                                                             (dka_ref, kg_ref, kacc_ref, 1.0))):
        total = jnp.zeros((t, LANES), F32)
        for p in range(pairs):
            issue(side * pairs + p)
            sl = slice(p * LANES, (p + 1) * LANES)
            cols = slice(side * FOX_WIDTH + p * LANES, side * FOX_WIDTH + (p + 1) * LANES)
            if side == 0:
                dv_ref[:, sl] = _pair_block(*head_blocks(dva_ref, p), lo).astype(BF16)
            d0, d1 = head_blocks(src_ref, p)
            total = total + (d0 + d1)
            raw = qk_ref[:, cols]
            rr = _head_rms(raw, lo)
            xhat = raw * rr
            dn = _pair_block(d0, d1, lo) * scale
            dqk_ref[:, cols] = _head_norm_bwd(dn, xhat, rr, g_ref[:, sl], lo).astype(BF16)
            acc_ref[:, sl] += jnp.sum(dn * xhat, axis=0, keepdims=True)
        sums.append(total)
    issue(2 * pairs)
    dq_sum, dk_sum = sums

    acc = (pltpu.roll(dq_sum, LANES - KEY_SUM_LANE, axis=1) - pltpu.roll(dk_sum, LANES - QUERY_SUM_LANE, axis=1))
    acc = jnp.where(lane < FOX_HEADS, acc, 0.0)
    sh = 1
    while sh < t:
        acc = acc + jnp.where(row < t - sh, pltpu.roll(acc, t - sh, axis=0), 0.0)
        sh *= 2
    dlogf = acc + carry_ref[...]
    dfb_ref[...] = dlogf
    carry_ref[...] = dfb_ref[0:1, :]
    z = fb_ref[...] + bf_ref[...]
    dz = jnp.where(lane < FOX_HEADS, dlogf * (1.0 / (1.0 + jnp.exp(z))), 0.0)
    dfb_ref[...] = dz
    dbf_ref[...] += jnp.sum(dz, axis=0, keepdims=True)

    @pl.when(i == n - 1)
    def _():
        dqg_ref[...] = _fold_heads(qacc_ref[...])
        dkg_ref[...] = _fold_heads(kacc_ref[...])


def _assemble_dproj(dp_ref, dpa_ref, dqk_ref, dv_ref, dgb_ref, dpm_ref, dfb_ref):
    dp_ref[:, PA_LO:QB_LO] = dpa_ref[...]
    dp_ref[:, QB_LO:VB_LO] = dqk_ref[...]
    dp_ref[:, VB_LO:GB_LO] = dv_ref[...]
    dp_ref[:, GB_LO:PM_LO] = dgb_ref[...]
    dp_ref[:, PM_LO:FB_LO] = dpm_ref[...]
    dp_ref[:, FB_LO:PROJ_PAD] = dfb_ref[...].astype(BF16)


def _dproj_specs(t):
    return [_rows(t, 512), _rows(t, 2 * FOX_WIDTH), _rows(t, FOX_WIDTH), _rows(t, FOX_WIDTH), _rows(t, 512),
            _rows(t, LANES)]


IN_BWD_X_TILE = 256


def _in_bwd_x(x, dy, norm_g, wp, dparts, gparts, axes, smalls):
    s = x.shape[0]
    t = IN_BWD_X_TILE
    n = s // t
    na = len(gparts)
    n_dp = len(dparts)
    vec_leaves, loss_row, dw4 = smalls if smalls is not None else ((), None, None)
    nv = len(vec_leaves)
    n_small = nv + 2 if smalls is not None else 0
    small_base = _ShardReduce.SEMS * na

    def body(*refs):
        x_ref, dy_ref, g_ref, wp_ref = refs[0:4]
        dp_parts = refs[4:4 + n_dp]
        o = 4 + n_dp
        g_refs = refs[o:o + na]
        small_in = refs[o + na:o + na + n_small]
        o += na + n_small
        gx_ref, dg_ref = refs[o:o + 2]
        out_refs = refs[o + 2:o + 2 + na]
        small_out = refs[o + 2 + na:o + 2 + na + (2 if smalls is not None else 0)]
        o += 2 + na + len(small_out)
        dp_ref = refs[o]
        bufs = tuple(refs[o + 1 + k * na:o + 1 + (k + 1) * na] for k in range(5))
        rest = refs[o + 1 + 5 * na:]

        i = pl.program_id(0)
        if na or smalls is not None:
            send_sems, recv_sems, local_sems = rest[-3:]
        red = _ShardReduce(g_refs, out_refs, axes, bufs, send_sems, recv_sems, local_sems) if na else None

        @pl.when(i == 0)
        def _():
            dg_ref[...] = jnp.zeros_like(dg_ref)
            if red is not None:
                red.exchange_with_sibling()

        if red is not None:
            for k in (1, 2, 3):
                pl.when(i == k)(functools.partial(red.send_to_chip, k))
            pl.when(i == 4)(red.keep_mine)

        _assemble_dproj(dp_ref, *dp_parts)
        dh = _dot(dp_ref[...], wp_ref[...])
        xv = x_ref[...]
        rr = lax.rsqrt(jnp.mean(xv * xv, axis=-1, keepdims=True) + EPS)
        xhat = xv * rr
        scaled = dh * g_ref[...]
        gx_ref[...] = dy_ref[...] + rr * (scaled - xhat * jnp.mean(xhat * scaled, axis=-1, keepdims=True))
        dg_ref[...] += jnp.sum(dh * xhat, axis=0, keepdims=True)

        def small_all_reduce():
            leaf_refs, (loss_ref, dw4_ref) = small_in[0:nv], small_in[nv:]
            vec_out, dw4_out = small_out
            vec_mine, vec_recv, dw4_recv = rest[0:3]
            cx, cy, c = _my_place()
            me_lin = 4 * cx + 2 * cy + c

            def copy(k, src, dst, base):
                peer = (me_lin + k) % 8
                return pltpu.make_async_remote_copy(
                    src_ref=src, dst_ref=dst.at[me_lin], send_sem=send_sems.at[base + k - 1],
                    recv_sem=recv_sems.at[base + k - 1], device_id=(peer // 4, (peer // 2) % 2, peer % 2),
                    device_id_type=MESH)

            vec_mine[...] = jnp.zeros_like(vec_mine)
            vec_mine[0:1, :] = dg_ref[...]
            for (_, row, _), ref in zip(VEC_LEAVES[1:], leaf_refs):
                vec_mine[row:row + 1, 0:ref.shape[1]] = ref[...]
            vec_mine[VEC_LOSS_ROW:VEC_LOSS_ROW + 1, 0:LANES] = loss_ref[...]
            copies = [copy(k, src, dst, base) for k in range(1, 8)
                      for src, dst, base in ((vec_mine, vec_recv, small_base), (dw4_ref, dw4_recv, small_base + 7))]
            for cp in copies:
                cp.start()
            for cp in copies:
                cp.wait_recv()
            vec_recv[me_lin] = vec_mine[...]
            dw4_recv[me_lin] = dw4_ref[...]
            vtot, wtot = vec_recv[0], dw4_recv[0]
            for d in range(1, 8):
                vtot = vtot + vec_recv[d]
                wtot = wtot + dw4_recv[d]
            vec_out[...] = vtot
            dw4_out[...] = wtot
            for cp in copies:
                cp.wait_send()

        @pl.when(i == n - 1)
        def _():
            if red is not None:
                red.sum_and_share()
            if smalls is not None:
                small_all_reduce()
            if red is not None:
                red.finish()

    any_spec = pl.BlockSpec(memory_space=pl.ANY)
    scratch = [pltpu.VMEM((t, PROJ_PAD), BF16)] + _ShardReduce.scratch(gparts, axes)
    out_shape = [jax.ShapeDtypeStruct((s, D_MODEL), F32), jax.ShapeDtypeStruct((1, D_MODEL), F32)]
    out_shape += [jax.ShapeDtypeStruct(_shard_shape(g), F32) for g in gparts]
    out_specs = [_rows(t, D_MODEL), _full((1, D_MODEL))] + [any_spec] * na
    small_args = []
    if smalls is not None:
        small_args = [*vec_leaves, loss_row, dw4]
        out_shape += [jax.ShapeDtypeStruct((VEC_ROWS, D_MODEL), F32), jax.ShapeDtypeStruct(dw4.shape, F32)]
        out_specs += [_full((VEC_ROWS, D_MODEL)), _full(dw4.shape)]
        scratch += [pltpu.VMEM((VEC_ROWS, D_MODEL), F32), pltpu.VMEM((8, VEC_ROWS, D_MODEL), F32),
                    pltpu.VMEM((8,) + dw4.shape, F32)]
    if na or smalls is not None:
        n_sems = small_base + 14
        scratch += [pltpu.SemaphoreType.DMA((n_sems,)), pltpu.SemaphoreType.DMA((n_sems,)),
                    pltpu.SemaphoreType.DMA((max(_ShardReduce.LOCAL * na, 1),))]
    return pl.pallas_call(
        body, name="in_bwd_x", grid=(n,), out_shape=tuple(out_shape),
        in_specs=[_rows(t, D_MODEL), _rows(t, D_MODEL), _full((1, D_MODEL)),
                  pl.BlockSpec((PROJ_PAD, D_MODEL), lambda i: (0, 0), pipeline_mode=pl.Buffered(1))]
        + _dproj_specs(t) + [any_spec] * na + [_full(a.shape) for a in small_args],
        out_specs=tuple(out_specs), scratch_shapes=scratch, compiler_params=_params(),
    )(x, dy, norm_g, wp, *dparts, *gparts, *small_args)


RING_STREAMS = 3
RING_SLOTS = 3


def _in_bwd_w(hb, dpa, dgb, dpm, fox):
    s = hb.shape[0]
    t = TILE
    n = s // t
    f_hi = F_ORIG_LO + FOX_HEADS
    n_in = 4 + len(fox)

    def body(*refs):
        h_ref, dpa_ref, dgb_ref, dpm_ref = refs[0:4]
        dw_ref, dqk_ref, dv_ref, dfb_ref, dqg_ref, dkg_ref, dbf_ref = refs[n_in:n_in + 7]
        fox_scratch = refs[n_in + 7:n_in + 10]
        rings, ring_sems = refs[n_in + 10:n_in + 10 + RING_STREAMS], refs[n_in + 10 + RING_STREAMS]
        i = pl.program_id(0)

        def fetch(k, step):
            first = (n - 1 - step) * t
            tile = pl.ds(first if isinstance(step, int) else pl.multiple_of(first, t), t)
            return pltpu.make_async_copy(refs[4 + k].at[tile], rings[k].at[step % RING_SLOTS],
                                         ring_sems.at[k, step % RING_SLOTS])

        @pl.when(i == 0)
        def _():
            dw_ref[...] = jnp.zeros_like(dw_ref)
            for k in range(RING_STREAMS):
                for step in range(RING_SLOTS - 1):
                    fetch(k, step).start()

        @pl.when(i + RING_SLOTS - 1 < n)
        def _():
            for k in range(RING_STREAMS):
                fetch(k, i + RING_SLOTS - 1).start()

        for k in range(RING_STREAMS):
            fetch(k, i).wait()
        fox_refs = tuple(ring.at[i % RING_SLOTS] for ring in rings) + tuple(refs[4 + RING_STREAMS:n_in])

        hv = h_ref[...]

        def rows_of(lo, ref, cols=slice(None)):
            def add():
                dproj = ref[:, cols]
                dw_ref[lo:lo + dproj.shape[1], :] += _dot(dproj, hv, TN)
            return add

        q_cols, k_cols = slice(0, FOX_WIDTH), slice(FOX_WIDTH, 2 * FOX_WIDTH)
        between = (rows_of(0, dpa_ref), rows_of(f_hi, dgb_ref), rows_of(f_hi + FOX_WIDTH, dpm_ref), None,
                   rows_of(QB_LO, dqk_ref, q_cols), rows_of(VB_LO, dv_ref), None, None, rows_of(KB_LO, dqk_ref, k_cols))
        _fox_post_tile(i, n, t, *fox_refs, dqk_ref, dv_ref, dfb_ref, dqg_ref, dkg_ref, dbf_ref, *fox_scratch, between)
        dw_ref[F_ORIG_LO:f_hi, :] += _dot(dfb_ref[...].astype(BF16), hv, TN)[0:FOX_HEADS, :]

    def rev(w):
        return _rows_rev(t, w, n)

    row = jax.ShapeDtypeStruct((1, LANES), F32)
    return pl.pallas_call(
        body, name="in_bwd_w", grid=(n,),
        out_shape=(jax.ShapeDtypeStruct((IN_WIDTH, D_MODEL), F32), jax.ShapeDtypeStruct((s, 2 * FOX_WIDTH), BF16),
                   jax.ShapeDtypeStruct((s, FOX_WIDTH), BF16), jax.ShapeDtypeStruct((s, LANES), F32), row, row, row),
        in_specs=[rev(D_MODEL), rev(512), rev(FOX_WIDTH), rev(512)] + [pl.BlockSpec(memory_space=pl.ANY)] * RING_STREAMS
        + [rev(2 * FOX_WIDTH), rev(LANES), _full((1, LANES)), _full((1, FOX_WIDTH)), _full((1, FOX_WIDTH))],
        out_specs=(pl.BlockSpec((IN_WIDTH, D_MODEL), lambda i: (0, 0), pipeline_mode=pl.Buffered(1)),
                   rev(2 * FOX_WIDTH), rev(FOX_WIDTH), rev(LANES), _full((1, LANES)), _full((1, LANES)),
                   _full((1, LANES))),
        scratch_shapes=[pltpu.VMEM((1, FOX_WIDTH), F32), pltpu.VMEM((1, FOX_WIDTH), F32), pltpu.VMEM((1, LANES), F32)]
        + [pltpu.VMEM((RING_SLOTS, t, HEAD_BLOCKS), F32)] * RING_STREAMS
        + [pltpu.SemaphoreType.DMA((RING_STREAMS, RING_SLOTS))],
        compiler_params=_params(),
    )(hb, dpa, dgb, dpm, *fox)


def _adamw_math(w_ref, gv, m_ref, v_ref, d_ref, nm_ref, nv_ref):
    nm = ADAM_B1 * m_ref[...] + (1.0 - ADAM_B1) * gv
    nv = ADAM_B2 * v_ref[...] + (1.0 - ADAM_B2) * (gv * gv)
    m_hat = nm / (1.0 - ADAM_B1 ** ADAM_STEP)
    v_hat = nv / (1.0 - ADAM_B2 ** ADAM_STEP)
    d_ref[...] = -ADAM_LR * (m_hat / (jnp.sqrt(v_hat) + ADAM_EPS) + ADAM_WD * w_ref[...])
    nm_ref[...] = nm
    nv_ref[...] = nv


def _adamw_flat(name, w, g, m, v):
    rows, cols = g.shape
    per_row = cols // LANES

    def body(w_ref, g_ref, m_ref, v_ref, gf_ref, d_ref, nm_ref, nv_ref):
        for k in range(per_row):
            gf_ref[pl.ds(k, rows, stride=per_row), :] = g_ref[:, k * LANES:(k + 1) * LANES]
        _adamw_math(w_ref, gf_ref[...], m_ref, v_ref, d_ref, nm_ref, nv_ref)

    def whole(shape):
        return pl.BlockSpec(shape, lambda i: (0, 0), pipeline_mode=pl.Buffered(1))

    return pl.pallas_call(
        body, name=name, grid=(1,),
        out_shape=(jax.ShapeDtypeStruct(w.shape, F32),) * 4,
        in_specs=[whole(w.shape), whole(g.shape), whole(w.shape), whole(w.shape)], out_specs=(whole(w.shape),) * 4,
        compiler_params=_params(),
    )(w, g, m, v)


def _adamw_rest(vec, dw4, leaves, pool, shards):
    nl = len(VEC_LEAVES) + 1
    ns = len(shards)

    def body(*refs):
        vec_ref, dw4_ref = refs[0:2]
        wmv = refs[2:2 + 3 * nl]
        shard_in = refs[2 + 3 * nl:2 + 3 * nl + 4 * ns]
        o = 2 + 3 * nl + 4 * ns
        loss_ref = refs[o]
        outs = refs[o + 1:o + 1 + 4 * nl]
        shard_out = refs[o + 1 + 4 * nl:]
        loss_ref[...] = vec_ref[VEC_LOSS_ROW:VEC_LOSS_ROW + 1, 0:1]
        for k in range(nl):
            if k < nl - 1:
                _, row, width = VEC_LEAVES[k]
                gv = vec_ref[row:row + 1, 0:width]
            else:
                gv = dw4_ref[...]
            w_ref, m_ref, v_ref = wmv[3 * k:3 * k + 3]
            g_ref, d_ref, nm_ref, nv_ref = outs[4 * k:4 * k + 4]
            g_ref[...] = gv
            _adamw_math(w_ref, gv, m_ref, v_ref, d_ref, nm_ref, nv_ref)
        for k in range(ns):
            w_ref, g_ref, m_ref, v_ref = shard_in[4 * k:4 * k + 4]
            _adamw_math(w_ref, g_ref[...], m_ref, v_ref, *shard_out[3 * k:3 * k + 3])

    shapes = [jax.ShapeDtypeStruct((1, width), F32) for _, _, width in VEC_LEAVES] + [
        jax.ShapeDtypeStruct(dw4.shape, F32)]
    flat_in = [a for triple in list(leaves) + [pool] for a in triple] + [a for quad in shards for a in quad]
    res = pl.pallas_call(
        body, name="adamw_rest",
        out_shape=(jax.ShapeDtypeStruct((1, 1), F32),) + tuple(s for s in shapes for _ in range(4))
        + tuple(jax.ShapeDtypeStruct(quad[0].shape, F32) for quad in shards for _ in range(3)),
        compiler_params=pltpu.CompilerParams(vmem_limit_bytes=VMEM_LIMIT),
    )(vec, dw4, *flat_in)
    per = [res[1 + 4 * k:5 + 4 * k] for k in range(nl)]
    big = res[1 + 4 * nl:]
    return (res[0], [p[0] for p in per], [p[1] for p in per], [p[2] for p in per], [p[3] for p in per],
            [big[3 * k:3 * k + 3] for k in range(ns)])


def _tile_heads(g, n):
    return jnp.tile(g.reshape(1, HEAD_DIM), (1, n))


def kernel(x, mem, norm_g, w_in, b_f, w_pool, pool_scale, fox_q_g, fox_k_g, mem_norm_g, w_mem_kv, mem_q_g, mem_k_g, w_out, loss_target, m_norm_g, m_w_in, m_b_f, m_w_pool, m_pool_scale, m_fox_q_g, m_fox_k_g, m_mem_norm_g, m_w_mem_kv, m_mem_q_g, m_mem_k_g, m_w_out, v_norm_g, v_w_in, v_b_f, v_w_pool, v_pool_scale, v_fox_q_g, v_fox_k_g, v_mem_norm_g, v_w_mem_kv, v_mem_q_g, v_mem_k_g, v_w_out):
    w_in_t = w_in[0].T
    axes = (1, 0, 0)

    g_in, g_kv, g_out = _all_gather_weights([w_in_t, w_mem_kv[0], w_out[0]], axes)
    tiled = _tiled_params(b_f, fox_q_g, fox_k_g, mem_q_g, mem_k_g)
    fwd, wp = _fwd_in(x[0], norm_g, g_in, *tiled[0:3])
    w_kv_b = g_kv.reshape(D_MODEL, 2 * MEM_WIDTH)
    w_out_b = g_out.reshape(D_MODEL, D_MODEL)
    w4 = w_pool.reshape(POOL_ROWS, HEAD_DIM)
    dy, hb, dpa, dgb, dpm, fox, g_w_kv, g_w_out, (dmemnorm_g, dpscale, dmq_g, dmk_g), loss_row, dw4 = _local_partials(
        x[0], mem[0], loss_target[0], fwd, w_kv_b, w_out_b, tiled, w4, pool_scale, mem_norm_g, axes[1:])
    dwp, dqk, dvb, dfb, dfq_g, dfk_g, dbf = _in_bwd_w(hb, dpa, dgb, dpm, fox)
    dparts = (dpa, dqk, dvb, dgb, dpm, dfb)
    vec_leaves = (dmemnorm_g, dpscale, dbf, dfq_g, dfk_g, dmq_g, dmk_g)
    grad_x, _, g_w_in_t, vec, dw4_sum = _in_bwd_x(
        x[0], dy, norm_g, wp, dparts, [dwp], axes[0:1], (vec_leaves, loss_row, dw4))

    small_wmv = [(norm_g, m_norm_g, v_norm_g), (mem_norm_g, m_mem_norm_g, v_mem_norm_g),
                 (pool_scale, m_pool_scale, v_pool_scale), (b_f, m_b_f, v_b_f), (fox_q_g, m_fox_q_g, v_fox_q_g),
                 (fox_k_g, m_fox_k_g, v_fox_k_g), (mem_q_g, m_mem_q_g, v_mem_q_g), (mem_k_g, m_mem_k_g, v_mem_k_g)]
    pool_wmv = tuple(a.reshape(POOL_ROWS, HEAD_DIM) for a in (w_pool, m_w_pool, v_w_pool))
    loss, *small_out, (upd_kv, upd_out) = _adamw_rest(
        vec, dw4_sum, small_wmv, pool_wmv, [(w_mem_kv[0], g_w_kv, m_w_mem_kv[0], v_w_mem_kv[0]),
                                             (w_out[0], g_w_out, m_w_out[0], v_w_out[0])])
    tiles = D_MODEL // LANES

    def flat(a):
        return a.reshape(tiles, LANES, -1).transpose(2, 0, 1).reshape(-1, LANES)

    def unflat(a):
        return a.reshape(-1, tiles, LANES).transpose(1, 2, 0).reshape(w_in.shape)

    g_in_flat, *upd_in = _adamw_flat("adamw_w_in", flat(w_in), g_w_in_t, flat(m_w_in), flat(v_w_in))
    big = [[unflat(g_in_flat), g_w_kv[None], g_w_out[None]]]
    big += [[unflat(upd_in[k]), upd_kv[k][None], upd_out[k][None]] for k in range(3)]

    def leaves(k):
        sm = small_out[k]
        b_in, b_kv, b_out = big[k]
        return (sm[0], b_in, sm[3], sm[8].reshape(w_pool.shape), sm[2], sm[4], sm[5], sm[1], b_kv, sm[6], sm[7], b_out)

    return (loss.reshape(()), grad_x[None], *leaves(0), *leaves(1), *leaves(2), *leaves(3))


def _tiled_params(b_f, fox_q_g, fox_k_g, mem_q_g, mem_k_g):
    return (jnp.pad(b_f, ((0, 0), (0, LANES - FOX_HEADS))), _tile_heads(fox_q_g, FOX_HEADS),
            _tile_heads(fox_k_g, FOX_HEADS), _tile_heads(mem_q_g, 4), _tile_heads(mem_k_g, 4))


def _local_partials(xs, mems, tgt, fwd, w_kv_b, w_out_b, tiled, w4, pool_scale, mem_norm_g, axes):
    hb, pa, qk, qa, ka, va, gb, pm, fb = fwd
    bf_pad, fq_g, fk_g, mq_g, mk_g = tiled

    ma, db, mm, mnb, kv, kmn, vmb = _side_fwd(pa, pm, w4, pool_scale, mq_g, mems, mem_norm_g, w_kv_b, mk_g)
    o, mb, r4 = _fox_fwd(qa, ka, va, gb)
    dy, dma, dmm, dw_out, loss_row, doa, dgb, rr = _out_loss(xs, tgt, ma, mb, mm, w_out_b, gb, o, r4)

    dpa, dpm, dw4, dpscale, dmq_g, dw_kv, dmemnorm_g, dmk_g = _side_bwd(
        pa, db, dma, w4, pool_scale, pm, dmm, kmn, vmb, mq_g, kv, mnb, mems, w_kv_b, mk_g, mem_norm_g)
    if axes:
        parts = [dw_kv.reshape(4, D_MODEL // 4, 2 * MEM_WIDTH), dw_out.reshape(4, D_MODEL // 4, D_MODEL)]
        dka, dva, dqa, dw_kv, dw_out = _fox_bwd(ka, va, qa, doa, rr, parts, axes)
    else:
        dka, dva, dqa = _fox_bwd(ka, va, qa, doa, rr, [], ())
    fox = (dqa, dka, dva, qk, fb, bf_pad, fq_g, fk_g)
    return dy, hb, dpa, dgb, dpm, fox, dw_kv, dw_out, (dmemnorm_g, dpscale, dmq_g, dmk_g), loss_row, dw4
```

```python
import functools

import jax
import jax.numpy as jnp
from jax import lax
from jax.experimental import pallas as pl
from jax.experimental.pallas import tpu as pltpu

F32 = jnp.float32
BF16 = jnp.bfloat16
MESH = pl.DeviceIdType.MESH

D_MODEL = 1024
HEAD_DIM = 64
POOL_WIDTH = 256
FOX_WIDTH = 512
FOX_HEADS = 8
MEM_WIDTH = 256
N_MEM = 256
IN_WIDTH = 3080
EPS = 1e-6
ATT_SCALE = 0.125

ADAM_LR = 0.001
ADAM_B1 = 0.9
ADAM_B2 = 0.999
ADAM_EPS = 1e-08
ADAM_WD = 0.01
ADAM_STEP = 10

LANES = 128
PA_LO, QB_LO, KB_LO, VB_LO, GB_LO, PM_LO, FB_LO, PROJ_PAD = 0, 512, 1024, 1536, 2048, 2560, 3072, 3200
F_ORIG_LO = 2048

TILE = 512
VMEM_LIMIT = 56 * 1024 * 1024
VMEM_LIMIT_FOX_BWD = 58 * 1024 * 1024

VEC_LEAVES = (("norm_g", 0, 1024), ("mem_norm_g", 1, 1024), ("pool_scale", 2, 256), ("b_f", 3, 8),
              ("fox_q_g", 4, 64), ("fox_k_g", 5, 64), ("mem_q_g", 6, 64), ("mem_k_g", 7, 64))
VEC_LOSS_ROW = 8
VEC_ROWS = 16
POOL_ROWS = 256


def _params(n_grid=1, vmem=VMEM_LIMIT):
    return pltpu.CompilerParams(dimension_semantics=("arbitrary",) * n_grid, vmem_limit_bytes=vmem)


def _rows(t, w):
    return pl.BlockSpec((t, w), lambda i: (i, 0))


def _rows_rev(t, w, n):
    return pl.BlockSpec((t, w), lambda i: (n - 1 - i, 0))


def _full(shape):
    return pl.BlockSpec(shape, lambda i: (0,) * len(shape))


def _sig(x):
    return 1.0 / (1.0 + jnp.exp(-x))


def _lane_lo(shape):
    return lax.broadcasted_iota(jnp.int32, shape, 1) < HEAD_DIM


def _pair_sum(v, lo):
    s0 = jnp.sum(jnp.where(lo, v, 0.0), axis=-1, keepdims=True)
    s1 = jnp.sum(jnp.where(lo, 0.0, v), axis=-1, keepdims=True)
    return jnp.where(lo, s0, s1)


def _head_rms(blk, lo):
    return lax.rsqrt(_pair_sum(blk * blk, lo) * (1.0 / HEAD_DIM) + EPS)


def _head_norm_bwd(dyn, xhat, rr, g, lo):
    a = dyn * g
    return rr * (a - xhat * (_pair_sum(xhat * a, lo) * (1.0 / HEAD_DIM)))


def _fold_heads(acc):
    tot = acc[:, 0:LANES]
    for p in range(1, acc.shape[1] // LANES):
        tot = tot + acc[:, p * LANES:(p + 1) * LANES]
    return tot + pltpu.roll(tot, HEAD_DIM, axis=1)


def _lane_pick(v, lane, idx):
    return jnp.sum(jnp.where(lane == idx, v, 0.0), axis=-1, keepdims=True)


NT = (((1,), (1,)), ((), ()))
TN = (((0,), (0,)), ((), ()))


def _dot(a, b, dims=None):
    if dims is None:
        return jnp.dot(a, b, preferred_element_type=F32)
    return lax.dot_general(a, b, dims, preferred_element_type=F32)


def _my_place():
    return lax.axis_index("x"), lax.axis_index("y"), lax.axis_index("c")


def _half_dims(shape, axis):
    return (shape[0] // 2, shape[1]) if axis == 0 else (shape[0], shape[1] // 2)


def _shard_shape(g):
    return tuple(g.shape[1:]) if len(g.shape) == 3 else (g.shape[0] // 4, g.shape[1])


F32_ROWS = 8


def _shard_window(g):
    rows = _shard_shape(g)[0]
    if len(g.shape) == 3:
        return rows
    skew = max((j * rows) % F32_ROWS for j in range(4))
    return -(-(rows + skew) // F32_ROWS) * F32_ROWS


def _half_of(ref, axis, core, lead=False):
    rows, cols = ref.shape[-2:]
    if axis == 0:
        idx = (pl.ds(pl.multiple_of(core * (rows // 2), 16), rows // 2), slice(None))
    else:
        idx = (slice(None), pl.ds(pl.multiple_of(core * (cols // 2), LANES), cols // 2))
    return ref.at[(slice(None),) + idx] if lead else ref.at[idx]


class _HalfGather:
    def __init__(self, ins, outs, axes, f32_bufs, bf_bufs, send_sems, recv_sems, local_sems):
        self.ins, self.outs, self.axes = ins, outs, axes
        self.f32_bufs, self.bf_bufs = f32_bufs, bf_bufs
        self.send_sems, self.recv_sems, self.local_sems = send_sems, recv_sems, local_sems
        self.n = len(ins)
        x, y, self.c = _my_place()
        self.me, self.sibling = (x, y, self.c), (x, y, 1 - self.c)
        self.chips = [(1 - x, y), (x, 1 - y), (1 - x, 1 - y)]

    @staticmethod
    def scratch(shards, axes):
        dims = [_half_dims(a.shape, axis) for a, axis in zip(shards, axes)]
        n = len(shards)
        return [pltpu.VMEM(d, F32) for d in dims] + [pltpu.VMEM(d, BF16) for d in dims] + [
            pltpu.SemaphoreType.DMA((7 * n,)), pltpu.SemaphoreType.DMA((7 * n,)), pltpu.SemaphoreType.DMA((2 * n,))]

    @staticmethod
    def out_shapes(shards, axes):
        return tuple(jax.ShapeDtypeStruct((8,) + _half_dims(a.shape, axis), BF16) for a, axis in zip(shards, axes))

    def _blk(self, a, px, py, pc):
        return self.outs[a].at[4 * px + 2 * py + pc]

    def _copy(self, a, k, block, to, src=None):
        return pltpu.make_async_remote_copy(
            src_ref=self._blk(a, *block) if src is None else src, dst_ref=self._blk(a, *block),
            send_sem=self.send_sems.at[7 * a + k], recv_sem=self.recv_sems.at[7 * a + k], device_id=to,
            device_id_type=MESH)

    def _keep(self, a):
        return pltpu.make_async_copy(self.bf_bufs[a], self._blk(a, *self.me), self.local_sems.at[self.n + a])

    def _first(self, a):
        mine = [self._copy(a, 0, self.me, self.sibling, src=self.bf_bufs[a])]
        return mine + [self._copy(a, 1 + j, self.me, (*chip, self.c), src=self.bf_bufs[a])
                       for j, chip in enumerate(self.chips)]

    def send_mine(self):
        loads = [pltpu.make_async_copy(_half_of(self.ins[a], self.axes[a], self.c), self.f32_bufs[a],
                                       self.local_sems.at[a]) for a in range(self.n)]
        for cp in loads:
            cp.start()
        for a in range(self.n):
            loads[a].wait()
            self.bf_bufs[a][...] = self.f32_bufs[a][...].astype(BF16)
            self._keep(a).start()
            for cp in self._first(a):
                cp.start()

    def pass_on(self):
        for a in range(self.n):
            for j, chip in enumerate(self.chips):
                self._copy(a, 1 + j, (*chip, self.c), self.me).wait_recv()
                self._copy(a, 4 + j, (*chip, self.c), self.sibling).start()

    def finish(self):
        for a in range(self.n):
            self._copy(a, 0, self.sibling, self.me).wait_recv()
            for j, chip in enumerate(self.chips):
                self._copy(a, 4 + j, (*chip, 1 - self.c), self.me).wait_recv()
        for a in range(self.n):
            for cp in self._first(a):
                cp.wait_send()
            for j, chip in enumerate(self.chips):
                self._copy(a, 4 + j, (*chip, self.c), self.sibling).wait_send()
            self._keep(a).wait()


def _all_gather_weights(shards, axes):
    n = len(shards)

    def body(*refs):
        gather = _HalfGather(refs[0:n], refs[n:2 * n], axes, refs[2 * n:3 * n], refs[3 * n:4 * n], *refs[4 * n:])
        gather.send_mine()
        gather.pass_on()
        gather.finish()

    any_spec = pl.BlockSpec(memory_space=pl.ANY)
    return pl.pallas_call(
        body, name="weights_all_gather", out_shape=_HalfGather.out_shapes(shards, axes),
        in_specs=[any_spec] * n, out_specs=(any_spec,) * n, scratch_shapes=_HalfGather.scratch(shards, axes),
        compiler_params=pltpu.CompilerParams(vmem_limit_bytes=VMEM_LIMIT),
    )(*shards)


class _ShardReduce:
    SEMS = 8
    LOCAL = 5

    def __init__(self, g_refs, out_refs, axes, bufs, send_sems, recv_sems, local_sems):
        self.g_refs, self.out_refs, self.axes = g_refs, out_refs, axes
        self.recv_a, self.own_a, self.send_b, self.recv_b, self.fin = bufs
        self.send_sems, self.recv_sems, self.local_sems = send_sems, recv_sems, local_sems
        self.n = len(g_refs)
        x, y, self.c = _my_place()
        self.chip = 2 * x + y
        self.sibling = (x, y, 1 - self.c)

    @staticmethod
    def scratch(gparts, axes):
        assert all(len(g.shape) == 3 or axis == 1 for g, axis in zip(gparts, axes))
        dims = [_half_dims(_shard_shape(g), axis) for g, axis in zip(gparts, axes)]
        windows = [d if len(g.shape) == 3 else (_shard_window(g),) + d[1:] for g, d in zip(gparts, dims)]
        shapes = []
        for dtype, lead, per_array in ((F32, (4,), windows), (F32, (4,), windows), (BF16, (4,), dims),
                                       (BF16, (4,), dims), (F32, (), dims)):
            shapes += [pltpu.VMEM(lead + d, dtype) for d in per_array]
        return shapes

    def _shard_half(self, a, j, core):
        g = self.g_refs[a]
        if len(g.shape) == 3:
            return _half_of(g.at[j], self.axes[a], core)
        start = (j * _shard_shape(g)[0]) // F32_ROWS * F32_ROWS
        return _half_of(g.at[pl.ds(pl.multiple_of(start, F32_ROWS), _shard_window(g))], self.axes[a], core)

    def _to_sibling(self, a, j):
        return pltpu.make_async_remote_copy(
            src_ref=self._shard_half(a, j, 1 - self.c), dst_ref=self.recv_a[a].at[j],
            send_sem=self.send_sems.at[self.SEMS * a + j], recv_sem=self.recv_sems.at[self.SEMS * a + j], device_id=self.sibling,
            device_id_type=MESH)

    def _own(self, a, j):
        return pltpu.make_async_copy(self._shard_half(a, j, self.c), self.own_a[a].at[j],
                                     self.local_sems.at[self.LOCAL * a + j])

    PEERS = (3, 1, 2)

    def _peer(self, k):
        return jnp.bitwise_xor(self.chip, k)

    def _to_chip(self, a, k):
        dest = self._peer(k)
        return pltpu.make_async_remote_copy(
            src_ref=self.send_b[a].at[dest], dst_ref=self.recv_b[a].at[self.chip],
            send_sem=self.send_sems.at[self.SEMS * a + 3 + k], recv_sem=self.recv_sems.at[self.SEMS * a + 3 + k],
            device_id=(dest // 2, dest % 2, self.c), device_id_type=MESH)

    def _give(self, a):
        return pltpu.make_async_remote_copy(
            src_ref=self.fin[a], dst_ref=_half_of(self.out_refs[a], self.axes[a], self.c),
            send_sem=self.send_sems.at[self.SEMS * a + 7], recv_sem=self.recv_sems.at[self.SEMS * a + 7], device_id=self.sibling,
            device_id_type=MESH)

    def _mine(self, a):
        return pltpu.make_async_copy(self.fin[a], _half_of(self.out_refs[a], self.axes[a], self.c),
                                     self.local_sems.at[self.LOCAL * a])

    def exchange_with_sibling(self):
        for k in self.PEERS + (0,):
            j = self._peer(k)
            for a in range(self.n):
                self._to_sibling(a, j).start()
                self._own(a, j).start()

    def _chip_partial(self, a, j):
        self._own(a, j).wait()
        self._to_sibling(a, j).wait_recv()
        g = self.g_refs[a]
        if len(g.shape) == 3:
            self.send_b[a][j] = (self.own_a[a][j] + self.recv_a[a][j]).astype(BF16)
            return
        rows = _shard_shape(g)[0]
        for shard in range(4):
            @pl.when(j == shard)
            def _():
                at = pl.ds((shard * rows) % F32_ROWS, rows)
                self.send_b[a][shard] = (self.own_a[a][shard, at, :] + self.recv_a[a][shard, at, :]).astype(BF16)

    def send_to_chip(self, k):
        for a in range(self.n):
            self._chip_partial(a, self._peer(k))
            self._to_chip(a, k).start()

    def keep_mine(self):
        for a in range(self.n):
            self._chip_partial(a, self.chip)
            keep = pltpu.make_async_copy(self.send_b[a].at[self.chip], self.recv_b[a].at[self.chip],
                                         self.local_sems.at[self.LOCAL * a + 4])
            keep.start()
            keep.wait()

    def sum_and_share(self):
        for a in range(self.n):
            for k in range(1, 4):
                self._to_chip(a, k).wait_recv()
            tot = self.recv_b[a][0].astype(F32) + self.recv_b[a][1].astype(F32)
            tot = tot + self.recv_b[a][2].astype(F32)
            self.fin[a][...] = tot + self.recv_b[a][3].astype(F32)
            self._give(a).start()
            self._mine(a).start()

    def finish(self):
        for a in range(self.n):
            self._give(a).wait_recv()
            self._mine(a).wait()
            self._give(a).wait_send()
            for j in range(4):
                self._to_sibling(a, j).wait_send()
            for k in range(1, 4):
                self._to_chip(a, k).wait_send()


def _mem_tokens_fwd(mem_ref, g_ref, w_ref, kg_ref, mn_ref, kv_ref, kn_ref, vm_ref):
    xm = mem_ref[...]
    rr = lax.rsqrt(jnp.mean(xm * xm, axis=-1, keepdims=True) + EPS)
    mnb = ((xm * rr) * g_ref[...]).astype(BF16)
    mn_ref[...] = mnb
    kv = _dot(mnb, w_ref[...])
    kv_ref[...] = kv
    lo = _lane_lo((xm.shape[0], LANES))
    for p in range(MEM_WIDTH // LANES):
        sl = slice(p * LANES, (p + 1) * LANES)
        kb = kv[:, sl]
        kn_ref[:, sl] = ((kb * _head_rms(kb, lo)) * kg_ref[:, sl]).astype(BF16)
    vm_ref[...] = kv[:, MEM_WIDTH:].astype(BF16)


AUG_LO = 64
KEY_SUM_LANE = 72
QUERY_SUM_LANE = 80
HEAD_BLOCKS = FOX_HEADS * LANES


def _ones3(lane):
    return jnp.where((lane >= AUG_LO) & (lane < AUG_LO + 3), 1.0, 0.0)


def _spread3(cols):
    hi = cols.astype(BF16)
    rest = cols - hi.astype(F32)
    mid = rest.astype(BF16)
    low = (rest - mid.astype(F32)).astype(BF16)
    r = lax.broadcasted_iota(jnp.int32, (LANES, HEAD_BLOCKS), 0)
    c = lax.broadcasted_iota(jnp.int32, (LANES, HEAD_BLOCKS), 1)
    out = None
    for k, part in enumerate((hi, mid, low)):
        term = _dot(part, jnp.where(c == r * LANES + (AUG_LO + k), 1.0, 0.0).astype(BF16))
        out = term if out is None else out + term
    return out


def _head_block(pair_blk, hh, lo, extras):
    src = pair_blk if hh == 0 else pltpu.roll(pair_blk, HEAD_DIM, axis=1)
    return jnp.where(lo, src, extras).astype(BF16)


def _pair_block(blk0, blk1, lo):
    return jnp.where(lo, blk0, pltpu.roll(blk1, HEAD_DIM, axis=1))


def _assemble_w_in(halves_ref, words_ref, wp_ref):
    shard = IN_WIDTH // 4
    half = D_MODEL // 2
    f_hi = F_ORIG_LO + FOX_HEADS
    for j in range(4):
        blocks = [pltpu.bitcast(halves_ref[2 * j + c], jnp.uint32) for c in range(2)]
        for lo, hi, to in ((0, F_ORIG_LO, PA_LO), (F_ORIG_LO, f_hi, FB_LO), (f_hi, IN_WIDTH, GB_LO)):
            a, b = max(lo, shard * j), min(hi, shard * (j + 1))
            if a < b:
                for c in range(2):
                    words_ref[(to + a - lo) // 2:(to + b - lo) // 2, c * half:(c + 1) * half] = (
                        blocks[c][(a - shard * j) // 2:(b - shard * j) // 2, :])
    pad_lo = (FB_LO + FOX_HEADS) // 2
    words_ref[pad_lo:, :] = jnp.zeros((PROJ_PAD // 2 - pad_lo, D_MODEL), jnp.uint32)
    wp_ref[...] = pltpu.bitcast(words_ref[...], BF16)


def _fwd_in(x, norm_g, halves, bf_pad, fq_g, fk_g):
    s = x.shape[0]
    t = TILE
    n = s // t

    def body(x_ref, ng_ref, halves_ref, bf_ref, qg_ref, kg_ref,
             h_ref, pa_ref, qk_ref, qa_ref, ka_ref, va_ref, gb_ref, pm_ref, fb_ref, wp_ref,
             carry_ref, fcol_ref, words_ref):
        @pl.when(pl.program_id(0) == 0)
        def _():
            carry_ref[...] = jnp.zeros_like(carry_ref)
            _assemble_w_in(halves_ref, words_ref, wp_ref)

        xv = x_ref[...]
        rr = lax.rsqrt(jnp.mean(xv * xv, axis=-1, keepdims=True) + EPS)
        hb = ((xv * rr) * ng_ref[...]).astype(BF16)
        h_ref[...] = hb

        def proj(lo, hi):
            return _dot(hb, wp_ref[lo:hi, :], NT)

        fb = proj(FB_LO, PROJ_PAD)
        fb_ref[...] = fb
        qk_ref[:, 0:FOX_WIDTH] = proj(QB_LO, KB_LO)

        lane = lax.broadcasted_iota(jnp.int32, (t, LANES), 1)
        row = lax.broadcasted_iota(jnp.int32, (t, LANES), 0)
        lo = lane < HEAD_DIM
        z = fb + bf_ref[...]
        lf = -(jnp.maximum(-z, 0.0) + jnp.log1p(jnp.exp(-jnp.abs(z))))
        lf = jnp.where(lane < FOX_HEADS, lf, 0.0)
        sh = 1
        while sh < t:
            lf = lf + jnp.where(row >= sh, pltpu.roll(lf, sh, axis=0), 0.0)
            sh *= 2
        fcum = lf + carry_ref[...]
        fcol_ref[...] = fcum
        carry_ref[...] = fcol_ref[t - 1:t, :]

        ones3 = _ones3(lane)
        minus_f = _spread3(-fcum)

        def head_blocks(seg, g_ref, out_ref, scale):
            for p in range(FOX_WIDTH // LANES):
                sl = slice(p * LANES, (p + 1) * LANES)
                blk = qk_ref[:, seg - QB_LO + p * LANES:seg - QB_LO + (p + 1) * LANES]
                normed = ((blk * _head_rms(blk, lo)) * g_ref[:, sl]) * scale
                for hh in range(2):
                    h = 2 * p + hh
                    if seg == QB_LO:
                        extras = jnp.where(lane == QUERY_SUM_LANE + h, 1.0, ones3)
                    else:
                        extras = jnp.where(lane == KEY_SUM_LANE + h, 1.0, minus_f[:, h * LANES:(h + 1) * LANES])
                    out_ref[:, h * LANES:(h + 1) * LANES] = _head_block(normed, hh, lo, extras)

        qk_ref[:, FOX_WIDTH:2 * FOX_WIDTH] = proj(KB_LO, VB_LO)
        pa_ref[...] = proj(PA_LO, QB_LO)
        head_blocks(QB_LO, qg_ref, qa_ref, ATT_SCALE)
        vraw = proj(VB_LO, GB_LO)
        gb_ref[...] = proj(GB_LO, PM_LO)
        head_blocks(KB_LO, kg_ref, ka_ref, 1.0)
        pm_ref[...] = proj(PM_LO, FB_LO)
        for h in range(FOX_HEADS):
            va_ref[:, h * LANES:(h + 1) * LANES] = _head_block(vraw[:, (h // 2) * LANES:(h // 2 + 1) * LANES], h % 2, lo, ones3)

    outs = (
        jax.ShapeDtypeStruct((s, D_MODEL), BF16),
        jax.ShapeDtypeStruct((s, 512), F32),
        jax.ShapeDtypeStruct((s, 2 * FOX_WIDTH), F32),
        jax.ShapeDtypeStruct((s, HEAD_BLOCKS), BF16),
        jax.ShapeDtypeStruct((s, HEAD_BLOCKS), BF16),
        jax.ShapeDtypeStruct((s, HEAD_BLOCKS), BF16),
        jax.ShapeDtypeStruct((s, FOX_WIDTH), F32),
        jax.ShapeDtypeStruct((s, 512), F32),
        jax.ShapeDtypeStruct((s, LANES), F32),
        jax.ShapeDtypeStruct((PROJ_PAD, D_MODEL), BF16),
    )

    def resident(shape):
        return pl.BlockSpec(shape, lambda i: (0,) * len(shape), pipeline_mode=pl.Buffered(1))

    *fwd, wp = pl.pallas_call(
        body, name="fwd_in", grid=(n,), out_shape=outs,
        in_specs=[_rows(t, D_MODEL), _full((1, D_MODEL)), resident(halves.shape), _full((1, LANES)),
                  _full((1, FOX_WIDTH)), _full((1, FOX_WIDTH))],
        out_specs=(_rows(t, D_MODEL), _rows(t, 512), _rows(t, 2 * FOX_WIDTH), _rows(t, HEAD_BLOCKS),
                   _rows(t, HEAD_BLOCKS), _rows(t, HEAD_BLOCKS), _rows(t, FOX_WIDTH), _rows(t, 512),
                   _rows(t, LANES), resident((PROJ_PAD, D_MODEL))),
        scratch_shapes=[pltpu.VMEM((1, LANES), F32), pltpu.VMEM((t, LANES), F32),
                        pltpu.VMEM((PROJ_PAD // 2, D_MODEL), jnp.uint32)],
        compiler_params=_params(),
    )(x, norm_g, halves, bf_pad, fq_g, fk_g)
    return tuple(fwd), wp


POOL_HALO = 16


def _pool_window(lane):
    return jnp.where(lane < 64, 2.0, jnp.where(lane < 128, 4.0, jnp.where(lane < 192, 8.0, 16.0)))


def _pool_pick(lane, s2, s4, s8, s16):
    return jnp.where(lane < 64, s2, jnp.where(lane < 128, s4, jnp.where(lane < 192, s8, s16)))


def _group_onehot(shape, row_is_group_lane):
    r = lax.broadcasted_iota(jnp.int32, shape, 0)
    c = lax.broadcasted_iota(jnp.int32, shape, 1)
    hit = (r % HEAD_DIM == c) if row_is_group_lane else (c % HEAD_DIM == r)
    return jnp.where(hit, 1.0, 0.0).astype(F32)


def _same_group(shape):
    r = lax.broadcasted_iota(jnp.int32, shape, 0)
    c = lax.broadcasted_iota(jnp.int32, shape, 1)
    return (r // HEAD_DIM) == (c // HEAD_DIM)


def _pool_block_diag(w4):
    spread = jnp.dot(w4, _group_onehot((HEAD_DIM, POOL_WIDTH), False), preferred_element_type=F32,
                     precision=lax.Precision.HIGHEST)
    return jnp.where(_same_group((POOL_WIDTH, POOL_WIDTH)), spread, 0.0).astype(BF16)


def _mem_softmax(qm, kp):
    sc = _dot(qm, kp, NT)
    e = jnp.exp(sc - jnp.max(sc, axis=-1, keepdims=True))
    return e * (1.0 / jnp.sum(e, axis=-1, keepdims=True))


def _side_fwd(pa, pm, w4, pscale, mq_g, mem, mem_norm_g, w_kv, mk_g):
    s = pa.shape[0]
    t = TILE
    n = s // t
    ext = t + POOL_HALO
    nm = mem.shape[0]

    def body(pa_ref, pm_ref, w4_ref, sc_ref, g_ref, mem_ref, mg_ref, wkv_ref, kg_ref,
             ma_ref, d_ref, mm_ref, mn_ref, kv_ref, k_ref, v_ref, ext_ref, w_ref):
        i = pl.program_id(0)

        @pl.when(i == 0)
        def _():
            ext_ref[0:POOL_HALO, :] = jnp.zeros((POOL_HALO, POOL_WIDTH), F32)
            w_ref[...] = _pool_block_diag(w4_ref[...])
            _mem_tokens_fwd(mem_ref, mg_ref, wkv_ref, kg_ref, mn_ref, kv_ref, k_ref, v_ref)

        u = pa_ref[:, 0:POOL_WIDTH]
        ext_ref[POOL_HALO:ext, :] = u
        e = ext_ref[...]
        s2 = e + pltpu.roll(e, 1, axis=0)
        s4 = s2 + pltpu.roll(s2, 2, axis=0)
        s8 = s4 + pltpu.roll(s4, 4, axis=0)
        s16 = s8 + pltpu.roll(s8, 8, axis=0)
        lane_e = lax.broadcasted_iota(jnp.int32, (ext, POOL_WIDTH), 1)
        win = _pool_pick(lane_e, s2, s4, s8, s16)[POOL_HALO:ext, :]
        lane = lax.broadcasted_iota(jnp.int32, (t, POOL_WIDTH), 1)
        pos = (lax.broadcasted_iota(jnp.int32, (t, POOL_WIDTH), 0) + (i * t + 1)).astype(F32)
        d = win / jnp.minimum(pos, _pool_window(lane)) - u
        db = d.astype(BF16)
        d_ref[...] = db
        ya = _dot(db, w_ref[...]) * sc_ref[...]
        ga = pa_ref[:, POOL_WIDTH:2 * POOL_WIDTH]
        ma_ref[...] = (ya * (ga * _sig(ga))).astype(BF16)
        ext_ref[0:POOL_HALO, :] = ext_ref[t:ext, :]

        lo = _lane_lo((t, LANES))
        for p in range(MEM_WIDTH // LANES):
            sl = slice(p * LANES, (p + 1) * LANES)
            qb = pm_ref[:, sl]
            qs = (((qb * _head_rms(qb, lo)) * g_ref[:, sl]) * ATT_SCALE).astype(BF16)
            kp = k_ref[:, sl]
            vp = v_ref[:, sl]
            outs = []
            for hh in range(2):
                msk = lo if hh == 0 else jnp.logical_not(lo)
                prob = _mem_softmax(jnp.where(msk, qs, jnp.zeros_like(qs)), kp)
                outs.append(_dot(prob.astype(BF16), vp))
            o = jnp.where(lo, outs[0], outs[1])
            gm = pm_ref[:, MEM_WIDTH + p * LANES:MEM_WIDTH + (p + 1) * LANES]
            mm_ref[:, sl] = (o * (gm * _sig(gm))).astype(BF16)

    return pl.pallas_call(
        body, name="side_fwd", grid=(n,),
        out_shape=(jax.ShapeDtypeStruct((s, POOL_WIDTH), BF16), jax.ShapeDtypeStruct((s, POOL_WIDTH), BF16),
                   jax.ShapeDtypeStruct((s, MEM_WIDTH), BF16), jax.ShapeDtypeStruct((nm, D_MODEL), BF16),
                   jax.ShapeDtypeStruct((nm, 2 * MEM_WIDTH), F32), jax.ShapeDtypeStruct((nm, MEM_WIDTH), BF16),
                   jax.ShapeDtypeStruct((nm, MEM_WIDTH), BF16)),
        in_specs=[_rows(t, 512), _rows(t, 512), _full((POOL_ROWS, HEAD_DIM)), _full((1, POOL_WIDTH)),
                  _full((1, MEM_WIDTH)), _full((nm, D_MODEL)), _full((1, D_MODEL)), _full((D_MODEL, 2 * MEM_WIDTH)),
                  _full((1, MEM_WIDTH))],
        out_specs=(_rows(t, POOL_WIDTH), _rows(t, POOL_WIDTH), _rows(t, MEM_WIDTH), _full((nm, D_MODEL)),
                   _full((nm, 2 * MEM_WIDTH)), _full((nm, MEM_WIDTH)), _full((nm, MEM_WIDTH))),
        scratch_shapes=[pltpu.VMEM((ext, POOL_WIDTH), F32), pltpu.VMEM((POOL_WIDTH, POOL_WIDTH), BF16)],
        compiler_params=_params(),
    )(pa, pm, w4, pscale, mq_g, mem, mem_norm_g, w_kv, mk_g)


FOX_FWD_HEADS = 4


def _fox_fwd(qa, ka, va, gb):
    s = qa.shape[0]
    t = TILE
    n = s // t
    heads = FOX_FWD_HEADS
    pairs = heads // 2
    group_w = heads * LANES

    def body(qa_ref, ka_ref, va_ref, gb_ref, o_ref, mb_ref, r_ref):
        i = pl.program_id(1)
        lane = lax.broadcasted_iota(jnp.int32, (t, LANES), 1)
        lo = lane < HEAD_DIM
        causal = lax.broadcasted_iota(jnp.int32, (t, t), 1) <= lax.broadcasted_iota(jnp.int32, (t, t), 0)
        qas = [qa_ref[:, hh * LANES:(hh + 1) * LANES] for hh in range(heads)]

        def step(j, carry, masked):
            rows = pl.ds(pl.multiple_of(j * t, t), t)
            def logits(hh):
                sc = _dot(qas[hh], ka_ref[rows, hh * LANES:(hh + 1) * LANES], NT)
                return jnp.where(causal, sc, -1e30) if masked else sc

            def advance(hh, sc):
                m, acc = carry[hh]
                m_new = jnp.maximum(m, jnp.max(sc, axis=-1, keepdims=True))
                p = jnp.exp(sc - m_new).astype(BF16)
                return m_new, jnp.exp(m - m_new) * acc + _dot(p, va_ref[rows, hh * LANES:(hh + 1) * LANES])

            new = []
            sc = logits(0)
            for hh in range(heads):
                sc_next = logits(hh + 1) if hh + 1 < heads else None
                new.append(advance(hh, sc))
                sc = sc_next
            return tuple(new)

        init = (jnp.full((t, 1), -1e30, F32), jnp.zeros((t, LANES), F32))
        carry = lax.fori_loop(0, i, functools.partial(step, masked=False), (init,) * heads)
        res = step(i, carry, masked=True)
        for p in range(pairs):
            outs = []
            rcol = jnp.zeros((t, LANES), F32)
            for hh in range(2):
                m, acc = res[2 * p + hh]
                l = _lane_pick(acc, lane, AUG_LO)
                outs.append(acc * (1.0 / l))
                rcol = jnp.where(lane == hh, m + jnp.log(l), rcol)
            o = _pair_block(outs[0], outs[1], lo)
            sl = slice(p * LANES, (p + 1) * LANES)
            o_ref[:, sl] = o
            g = gb_ref[:, sl]
            mb_ref[:, sl] = (o * (g * _sig(g))).astype(BF16)
            r_ref[p] = rcol

    tile_spec = pl.BlockSpec((t, pairs * LANES), lambda p, i: (i, p))
    full_spec = pl.BlockSpec((s, group_w), lambda p, i: (0, p))
    return pl.pallas_call(
        body, name="fox_fwd", grid=(FOX_HEADS // heads, n),
        out_shape=(jax.ShapeDtypeStruct((s, FOX_WIDTH), F32), jax.ShapeDtypeStruct((s, FOX_WIDTH), BF16),
                   jax.ShapeDtypeStruct((FOX_HEADS // 2, s, LANES), F32)),
        in_specs=[pl.BlockSpec((t, group_w), lambda p, i: (i, p)), full_spec, full_spec, tile_spec],
        out_specs=(tile_spec, tile_spec, pl.BlockSpec((pairs, t, LANES), lambda p, i: (p, i, 0))),
        compiler_params=_params(2),
    )(qa, ka, va, gb)


def _out_loss(x, tgt, ma, mb, mm, wout, gb, o, r4):
    s = x.shape[0]
    t = TILE
    n = s // t
    pairs = FOX_HEADS // 2

    def body(x_ref, t_ref, ma_ref, mb_ref, mm_ref, w_ref, gb_ref, o_ref, r_ref,
             dy_ref, dma_ref, dmm_ref, dw_ref, loss_ref, doa_ref, dgb_ref, rr_ref, mix_ref):
        @pl.when(pl.program_id(0) == 0)
        def _():
            dw_ref[...] = jnp.zeros_like(dw_ref)
            loss_ref[...] = jnp.zeros_like(loss_ref)

        mix_ref[:, 0:256] = ma_ref[...]
        mix_ref[:, 256:768] = mb_ref[...]
        mix_ref[:, 768:1024] = mm_ref[...]
        mix = mix_ref[...]
        err = (x_ref[...] + _dot(mix, w_ref[...])) - t_ref[...]
        row_mean = jnp.sum(err * err, axis=-1, keepdims=True) * (1.0 / D_MODEL)
        loss_ref[...] += 0.5 * jnp.sum(row_mean, axis=0, keepdims=True)
        dy = err * (1.0 / D_MODEL)
        dy_ref[...] = dy
        dyb = dy.astype(BF16)
        dmix = _dot(dyb, w_ref[...], NT)
        dma_ref[...] = dmix[:, 0:256]
        dmm_ref[...] = dmix[:, 768:1024]
        dw_ref[...] += _dot(mix, dyb, TN)

        lane = lax.broadcasted_iota(jnp.int32, (t, LANES), 1)
        lo = lane < HEAD_DIM
        d_os = []
        delta = jnp.zeros((t, LANES), F32)
        for p in range(pairs):
            sl = slice(p * LANES, (p + 1) * LANES)
            g = gb_ref[:, sl]
            sg = _sig(g)
            dm = dmix[:, 256 + p * LANES:256 + (p + 1) * LANES]
            ov = o_ref[:, sl]
            d_o = dm * (g * sg)
            d_os.append(d_o)
            dgb_ref[:, sl] = (dm * ov * (sg * (1.0 + g * (1.0 - sg)))).astype(BF16)
            prod = d_o * ov
            delta = jnp.where(lane == 2 * p, jnp.sum(jnp.where(lo, prod, 0.0), axis=-1, keepdims=True), delta)
            delta = jnp.where(lane == 2 * p + 1, jnp.sum(jnp.where(lo, 0.0, prod), axis=-1, keepdims=True), delta)
            rr_ref[p, 0] = r_ref[p].T[0:8, :]
        minus_delta = _spread3(-delta)
        for h in range(FOX_HEADS):
            blk = slice(h * LANES, (h + 1) * LANES)
            doa_ref[:, blk] = _head_block(d_os[h // 2], h % 2, lo, minus_delta[:, blk])

    return pl.pallas_call(
        body, name="out_loss", grid=(n,),
        out_shape=(jax.ShapeDtypeStruct((s, D_MODEL), F32), jax.ShapeDtypeStruct((s, 256), F32),
                   jax.ShapeDtypeStruct((s, 256), F32), jax.ShapeDtypeStruct((D_MODEL, D_MODEL), F32),
                   jax.ShapeDtypeStruct((1, LANES), F32), jax.ShapeDtypeStruct((s, HEAD_BLOCKS), BF16),
                   jax.ShapeDtypeStruct((s, FOX_WIDTH), BF16), jax.ShapeDtypeStruct((pairs, n, 8, t), F32)),
        in_specs=[_rows(t, D_MODEL), _rows(t, D_MODEL), _rows(t, 256), _rows(t, 512), _rows(t, 256),
                  _full((D_MODEL, D_MODEL)), _rows(t, FOX_WIDTH), _rows(t, FOX_WIDTH),
                  pl.BlockSpec((pairs, t, LANES), lambda i: (0, i, 0))],
        out_specs=(_rows(t, D_MODEL), _rows(t, 256), _rows(t, 256), _full((D_MODEL, D_MODEL)), _full((1, LANES)),
                   _rows(t, HEAD_BLOCKS), _rows(t, FOX_WIDTH), pl.BlockSpec((pairs, 1, 8, t), lambda i: (0, i, 0, 0))),
        scratch_shapes=[pltpu.VMEM((t, D_MODEL), BF16)],
        compiler_params=_params(),
    )(x, tgt, ma, mb, mm, wout, gb, o, r4)


def _side_bwd(pa, db, dma, w4, pscale, pm, dmm, kmn, vmb, mq_g, kv, mnb, mem, w_kv, mk_g, mem_norm_g):
    s = pa.shape[0]
    t = TILE
    n = s // t
    ext = t + POOL_HALO
    nm = mem.shape[0]

    def body(pa_ref, d_ref, dma_ref, w4_ref, sc_ref, pm_ref, dmm_ref, k_ref, v_ref, g_ref,
             kv_ref, mn_ref, mem_ref, wkv_ref, kg_ref, mg_ref,
             dpa_ref, dpm_ref, dw4_ref, dsc_ref, dg_ref, dwkv_ref, dmg_ref, dkg_ref,
             ext_ref, w_ref, dw_ref, dk_ref, dv_ref, gacc_ref, dkv_ref):
        i = pl.program_id(0)

        @pl.when(i == 0)
        def _():
            dw_ref[...] = jnp.zeros_like(dw_ref)
            dsc_ref[...] = jnp.zeros_like(dsc_ref)
            ext_ref[t:ext, :] = jnp.zeros((POOL_HALO, POOL_WIDTH), F32)
            w_ref[...] = _pool_block_diag(w4_ref[...])
            dk_ref[...] = jnp.zeros_like(dk_ref)
            dv_ref[...] = jnp.zeros_like(dv_ref)
            gacc_ref[...] = jnp.zeros_like(gacc_ref)

        dbv = d_ref[...]
        z = _dot(dbv, w_ref[...])
        ga = pa_ref[:, POOL_WIDTH:2 * POOL_WIDTH]
        sg = _sig(ga)
        dma_v = dma_ref[...]
        dya = dma_v * (ga * sg)
        dpa_ref[:, POOL_WIDTH:2 * POOL_WIDTH] = (dma_v * (z * sc_ref[...]) * (sg * (1.0 + ga * (1.0 - sg)))).astype(BF16)
        dsc_ref[...] += jnp.sum(dya * z, axis=0, keepdims=True)
        dzb = (dya * sc_ref[...]).astype(BF16)
        dw_ref[...] += _dot(dbv, dzb, TN)
        dd = _dot(dzb, w_ref[...], NT)
        lane = lax.broadcasted_iota(jnp.int32, (t, POOL_WIDTH), 1)
        pos = (lax.broadcasted_iota(jnp.int32, (t, POOL_WIDTH), 0) + ((n - 1 - i) * t + 1)).astype(F32)
        ext_ref[0:t, :] = dd / jnp.minimum(pos, _pool_window(lane))
        e = ext_ref[...]
        s2 = e + pltpu.roll(e, ext - 1, axis=0)
        s4 = s2 + pltpu.roll(s2, ext - 2, axis=0)
        s8 = s4 + pltpu.roll(s4, ext - 4, axis=0)
        s16 = s8 + pltpu.roll(s8, ext - 8, axis=0)
        lane_e = lax.broadcasted_iota(jnp.int32, (ext, POOL_WIDTH), 1)
        win = _pool_pick(lane_e, s2, s4, s8, s16)[0:t, :]
        dpa_ref[:, 0:POOL_WIDTH] = (win - dd).astype(BF16)
        ext_ref[t:ext, :] = ext_ref[0:POOL_HALO, :]

        lo = _lane_lo((t, LANES))
        pairs = MEM_WIDTH // LANES
        pre = []
        for p in range(pairs):
            sl = slice(p * LANES, (p + 1) * LANES)
            qb = pm_ref[:, sl]
            rr = _head_rms(qb, lo)
            qhat = qb * rr
            g = g_ref[:, sl]
            qs = ((qhat * g) * ATT_SCALE).astype(BF16)
            gm = pm_ref[:, MEM_WIDTH + p * LANES:MEM_WIDTH + (p + 1) * LANES]
            sg = _sig(gm)
            dmo = dmm_ref[:, sl]
            pre.append((sl, rr, qhat, g, qs, gm, sg, dmo, dmo * (gm * sg)))

        def front(p, hh):
            sl, _, _, _, qs, _, _, _, d_o = pre[p]
            msk = lo if hh == 0 else jnp.logical_not(lo)
            qm = jnp.where(msk, qs, jnp.zeros_like(qs))
            prob = _mem_softmax(qm, k_ref[:, sl])
            dom = jnp.where(msk, d_o, 0.0).astype(BF16)
            return qm, prob, dom, _dot(dom, v_ref[:, sl], NT)

        def back(p, qm, prob, dom, dp):
            sl = pre[p][0]
            pb = prob.astype(BF16)
            out = _dot(pb, v_ref[:, sl])
            ds = (prob * (dp - jnp.sum(prob * dp, axis=-1, keepdims=True))).astype(BF16)
            dq = _dot(ds, k_ref[:, sl])
            dk_ref[:, sl] += _dot(ds, qm, TN)
            dv_ref[:, sl] += _dot(pb, dom, TN)
            return out, dq

        heads = [(p, hh) for p in range(pairs) for hh in range(2)]
        done = []
        fronts = [front(*head) for head in heads]
        for k, (p, _) in enumerate(heads):
            done.append(back(p, *fronts[k]))
        for p in range(pairs):
            sl, rr, qhat, g, _, gm, sg, dmo, _ = pre[p]
            outs, dqs = zip(done[2 * p], done[2 * p + 1])
            o = jnp.where(lo, outs[0], outs[1])
            dqn = jnp.where(lo, dqs[0], dqs[1]) * ATT_SCALE
            dpm_ref[:, sl] = _head_norm_bwd(dqn, qhat, rr, g, lo).astype(BF16)
            dpm_ref[:, MEM_WIDTH + p * LANES:MEM_WIDTH + (p + 1) * LANES] = (
                dmo * o * (sg * (1.0 + gm * (1.0 - sg)))).astype(BF16)
            gacc_ref[:, sl] += jnp.sum(dqn * qhat, axis=0, keepdims=True)

        @pl.when(i == n - 1)
        def _():
            own = jnp.where(_same_group((POOL_WIDTH, POOL_WIDTH)), dw_ref[...], 0.0)
            dw4_ref[...] = jnp.dot(own, _group_onehot((POOL_WIDTH, HEAD_DIM), True), preferred_element_type=F32,
                                   precision=lax.Precision.HIGHEST)
            dg_ref[...] = _fold_heads(gacc_ref[...])

            lo_m = _lane_lo((nm, LANES))
            kacc = []
            for p in range(MEM_WIDTH // LANES):
                sl = slice(p * LANES, (p + 1) * LANES)
                kb = kv_ref[:, sl]
                rr = _head_rms(kb, lo_m)
                khat = kb * rr
                dk = dk_ref[:, sl]
                dkv_ref[:, sl] = _head_norm_bwd(dk, khat, rr, kg_ref[:, sl], lo_m).astype(BF16)
                kacc.append(jnp.sum(dk * khat, axis=0, keepdims=True))
            dkg_ref[...] = _fold_heads(jnp.concatenate(kacc, axis=1))
            dkv_ref[:, MEM_WIDTH:] = dv_ref[...].astype(BF16)
            dkv = dkv_ref[...]
            dwkv_ref[...] = _dot(mn_ref[...], dkv, TN)
            dmn = _dot(dkv, wkv_ref[...], NT)
            xm = mem_ref[...]
            rr = lax.rsqrt(jnp.mean(xm * xm, axis=-1, keepdims=True) + EPS)
            dmg_ref[...] = jnp.sum(dmn * (xm * rr), axis=0, keepdims=True)

    def rev(w):
        return _rows_rev(t, w, n)

    row = jax.ShapeDtypeStruct((1, LANES), F32)
    return pl.pallas_call(
        body, name="side_bwd", grid=(n,),
        out_shape=(jax.ShapeDtypeStruct((s, 512), BF16), jax.ShapeDtypeStruct((s, 512), BF16),
                   jax.ShapeDtypeStruct((POOL_ROWS, HEAD_DIM), F32), jax.ShapeDtypeStruct((1, POOL_WIDTH), F32), row,
                   jax.ShapeDtypeStruct((D_MODEL, 2 * MEM_WIDTH), F32), jax.ShapeDtypeStruct((1, D_MODEL), F32), row),
        in_specs=[rev(512), rev(POOL_WIDTH), rev(POOL_WIDTH), _full((POOL_ROWS, HEAD_DIM)), _full((1, POOL_WIDTH)),
                  rev(512), rev(MEM_WIDTH), _full((N_MEM, MEM_WIDTH)), _full((N_MEM, MEM_WIDTH)), _full((1, MEM_WIDTH)),
                  _full((nm, 2 * MEM_WIDTH)), _full((nm, D_MODEL)), _full((nm, D_MODEL)),
                  _full((D_MODEL, 2 * MEM_WIDTH)), _full((1, MEM_WIDTH)), _full((1, D_MODEL))],
        out_specs=(rev(512), rev(512), _full((POOL_ROWS, HEAD_DIM)), _full((1, POOL_WIDTH)), _full((1, LANES)),
                   _full((D_MODEL, 2 * MEM_WIDTH)), _full((1, D_MODEL)), _full((1, LANES))),
        scratch_shapes=[pltpu.VMEM((ext, POOL_WIDTH), F32), pltpu.VMEM((POOL_WIDTH, POOL_WIDTH), BF16),
                        pltpu.VMEM((POOL_WIDTH, POOL_WIDTH), F32), pltpu.VMEM((N_MEM, MEM_WIDTH), F32),
                        pltpu.VMEM((N_MEM, MEM_WIDTH), F32), pltpu.VMEM((1, MEM_WIDTH), F32),
                        pltpu.VMEM((nm, 2 * MEM_WIDTH), BF16)],
        compiler_params=_params(),
    )(pa, db, dma, w4, pscale, pm, dmm, kmn, vmb, mq_g, kv, mnb, mem, w_kv, mk_g, mem_norm_g)


FOX_BWD_HEADS = 4


def _fox_bwd(ka, va, qa, doa, rr, gparts, axes):
    s = ka.shape[0]
    t = TILE
    n = s // t
    heads = FOX_BWD_HEADS
    groups = FOX_HEADS // heads
    group_w = heads * LANES
    na = len(gparts)

    def body(*refs):
        ka_ref, va_ref, qa_ref, doa_ref, rr_ref = refs[0:5]
        g_refs = refs[5:5 + na]
        dka_ref, dva_ref, dqa_ref = refs[5 + na:8 + na]
        out_refs = refs[8 + na:8 + 2 * na]
        bufs = tuple(refs[8 + (2 + k) * na:8 + (3 + k) * na] for k in range(5))
        j = pl.program_id(1)
        step_id = pl.program_id(0) * n + j
        red = _ShardReduce(g_refs, out_refs, axes, bufs, *refs[8 + 7 * na:]) if na else None

        @pl.when(j == 0)
        def _():
            dqa_ref[...] = jnp.zeros_like(dqa_ref)

        if red is not None:
            pl.when(step_id == 0)(red.exchange_with_sibling)

            @pl.when(step_id == 1)
            def _():
                for k in red.PEERS:
                    red.send_to_chip(k)
                red.keep_mine()

        causal = lax.broadcasted_iota(jnp.int32, (t, t), 0) <= lax.broadcasted_iota(jnp.int32, (t, t), 1)
        kas = [ka_ref[:, hh * LANES:(hh + 1) * LANES] for hh in range(heads)]
        vas = [va_ref[:, hh * LANES:(hh + 1) * LANES] for hh in range(heads)]

        def step(i, carry, masked):
            rows = pl.ds(pl.multiple_of(i * t, t), t)
            new = []
            for hh in range(heads):
                cols = slice(hh * LANES, (hh + 1) * LANES)
                dk_a, dv_a = carry[hh]
                qb = qa_ref[rows, cols]
                d_o = doa_ref[rows, cols]
                arg = _dot(kas[hh], qb, NT) - rr_ref[hh // 2, i, hh % 2:hh % 2 + 1, :]
                if masked:
                    arg = jnp.where(causal, arg, -1e30)
                pt = jnp.exp(arg)
                dst = (pt * _dot(vas[hh], d_o, NT)).astype(BF16)
                dv_a = dv_a + _dot(pt.astype(BF16), d_o)
                dk_a = dk_a + _dot(dst, qb)
                dqa_ref[rows, cols] += _dot(dst, kas[hh], TN)
                new.append((dk_a, dv_a))
            return tuple(new)

        zero = jnp.zeros((t, LANES), F32)
        carry = step(j, ((zero, zero),) * heads, masked=True)
        res = lax.fori_loop(j + 1, n, functools.partial(step, masked=False), carry)
        for hh in range(heads):
            cols = slice(hh * LANES, (hh + 1) * LANES)
            dka_ref[:, cols] = res[hh][0]
            dva_ref[:, cols] = res[hh][1]

        if red is not None:
            @pl.when(step_id == groups * n - 1)
            def _():
                red.sum_and_share()
                red.finish()

    tile_spec = pl.BlockSpec((t, group_w), lambda p, j: (j, p))
    full_spec = pl.BlockSpec((s, group_w), lambda p, j: (0, p))
    any_spec = pl.BlockSpec(memory_space=pl.ANY)
    scratch = _ShardReduce.scratch(gparts, axes)
    if na:
        scratch += [pltpu.SemaphoreType.DMA((_ShardReduce.SEMS * na,)), pltpu.SemaphoreType.DMA((_ShardReduce.SEMS * na,)),
                    pltpu.SemaphoreType.DMA((_ShardReduce.LOCAL * na,))]
    return pl.pallas_call(
        body, name="fox_bwd", grid=(groups, n),
        out_shape=(jax.ShapeDtypeStruct((s, HEAD_BLOCKS), F32),) * 3
        + tuple(jax.ShapeDtypeStruct(_shard_shape(g), F32) for g in gparts),
        in_specs=[tile_spec, tile_spec, full_spec, full_spec,
                  pl.BlockSpec((heads // 2, n, 8, t), lambda p, j: (p, 0, 0, 0))] + [any_spec] * na,
        out_specs=(tile_spec, tile_spec, full_spec) + (any_spec,) * na,
        scratch_shapes=scratch, compiler_params=_params(2, VMEM_LIMIT_FOX_BWD),
    )(ka, va, qa, doa, rr, *gparts)


def _fox_post_tile(i, n, t, dqa_ref, dka_ref, dva_ref, qk_ref, fb_ref, bf_ref, qg_ref, kg_ref,
                   dqk_ref, dv_ref, dfb_ref, dqg_ref, dkg_ref, dbf_ref, qacc_ref, kacc_ref, carry_ref,
                   between):
    @pl.when(i == 0)
    def _():
        qacc_ref[...] = jnp.zeros_like(qacc_ref)
        kacc_ref[...] = jnp.zeros_like(kacc_ref)
        dbf_ref[...] = jnp.zeros_like(dbf_ref)
        carry_ref[...] = jnp.zeros_like(carry_ref)

    lane = lax.broadcasted_iota(jnp.int32, (t, LANES), 1)
    row = lax.broadcasted_iota(jnp.int32, (t, LANES), 0)
    lo = lane < HEAD_DIM

    def head_blocks(ref, p):
        return ref[:, 2 * p * LANES:(2 * p + 1) * LANES], ref[:, (2 * p + 1) * LANES:(2 * p + 2) * LANES]

    def issue(k):
        if between[k] is not None:
            between[k]()

    sums = []
    pairs = FOX_WIDTH // LANES
    for side, (src_ref, g_ref, acc_ref, scale) in enumerate(((dqa_ref, qg_ref, qacc_ref, ATT_SCALE),
                                                             (dka_ref, kg_ref, kacc_ref, 1.0))):
        total = jnp.zeros((t, LANES), F32)
        for p in range(pairs):
            issue(side * pairs + p)
            sl = slice(p * LANES, (p + 1) * LANES)
            cols = slice(side * FOX_WIDTH + p * LANES, side * FOX_WIDTH + (p + 1) * LANES)
            if side == 0:
                dv_ref[:, sl] = _pair_block(*head_blocks(dva_ref, p), lo).astype(BF16)
            d0, d1 = head_blocks(src_ref, p)
            total = total + (d0 + d1)
            raw = qk_ref[:, cols]
            rr = _head_rms(raw, lo)
            xhat = raw * rr
            dn = _pair_block(d0, d1, lo) * scale
            dqk_ref[:, cols] = _head_norm_bwd(dn, xhat, rr, g_ref[:, sl], lo).astype(BF16)
            acc_ref[:, sl] += jnp.sum(dn * xhat, axis=0, keepdims=True)
        sums.append(total)
    issue(2 * pairs)
    dq_sum, dk_sum = sums

    acc = (pltpu.roll(dq_sum, LANES - KEY_SUM_LANE, axis=1) - pltpu.roll(dk_sum, LANES - QUERY_SUM_LANE, axis=1))
    acc = jnp.where(lane < FOX_HEADS, acc, 0.0)
    sh = 1
    while sh < t:
        acc = acc + jnp.where(row < t - sh, pltpu.roll(acc, t - sh, axis=0), 0.0)
        sh *= 2
    dlogf = acc + carry_ref[...]
    dfb_ref[...] = dlogf
    carry_ref[...] = dfb_ref[0:1, :]
    z = fb_ref[...] + bf_ref[...]
    dz = jnp.where(lane < FOX_HEADS, dlogf * (1.0 / (1.0 + jnp.exp(z))), 0.0)
    dfb_ref[...] = dz
    dbf_ref[...] += jnp.sum(dz, axis=0, keepdims=True)

    @pl.when(i == n - 1)
    def _():
        dqg_ref[...] = _fold_heads(qacc_ref[...])
        dkg_ref[...] = _fold_heads(kacc_ref[...])


def _assemble_dproj(dp_ref, dpa_ref, dqk_ref, dv_ref, dgb_ref, dpm_ref, dfb_ref):
    dp_ref[:, PA_LO:QB_LO] = dpa_ref[...]
    dp_ref[:, QB_LO:VB_LO] = dqk_ref[...]
    dp_ref[:, VB_LO:GB_LO] = dv_ref[...]
    dp_ref[:, GB_LO:PM_LO] = dgb_ref[...]
    dp_ref[:, PM_LO:FB_LO] = dpm_ref[...]
    dp_ref[:, FB_LO:PROJ_PAD] = dfb_ref[...].astype(BF16)


def _dproj_specs(t):
    return [_rows(t, 512), _rows(t, 2 * FOX_WIDTH), _rows(t, FOX_WIDTH), _rows(t, FOX_WIDTH), _rows(t, 512),
            _rows(t, LANES)]


IN_BWD_X_TILE = 256


def _in_bwd_x(x, dy, norm_g, wp, dparts, gparts, axes, smalls):
    s = x.shape[0]
    t = IN_BWD_X_TILE
    n = s // t
    na = len(gparts)
    n_dp = len(dparts)
    vec_leaves, loss_row, dw4 = smalls if smalls is not None else ((), None, None)
    nv = len(vec_leaves)
    n_small = nv + 2 if smalls is not None else 0
    small_base = _ShardReduce.SEMS * na

    def body(*refs):
        x_ref, dy_ref, g_ref, wp_ref = refs[0:4]
        dp_parts = refs[4:4 + n_dp]
        o = 4 + n_dp
        g_refs = refs[o:o + na]
        small_in = refs[o + na:o + na + n_small]
        o += na + n_small
        gx_ref, dg_ref = refs[o:o + 2]
        out_refs = refs[o + 2:o + 2 + na]
        small_out = refs[o + 2 + na:o + 2 + na + (2 if smalls is not None else 0)]
        o += 2 + na + len(small_out)
        dp_ref = refs[o]
        bufs = tuple(refs[o + 1 + k * na:o + 1 + (k + 1) * na] for k in range(5))
        rest = refs[o + 1 + 5 * na:]

        i = pl.program_id(0)
        if na or smalls is not None:
            send_sems, recv_sems, local_sems = rest[-3:]
        red = _ShardReduce(g_refs, out_refs, axes, bufs, send_sems, recv_sems, local_sems) if na else None

        @pl.when(i == 0)
        def _():
            dg_ref[...] = jnp.zeros_like(dg_ref)
            if red is not None:
                red.exchange_with_sibling()

        if red is not None:
            for at_step, k in enumerate(red.PEERS, start=1):
                pl.when(i == at_step)(functools.partial(red.send_to_chip, k))
            pl.when(i == 4)(red.keep_mine)

        _assemble_dproj(dp_ref, *dp_parts)
        dh = _dot(dp_ref[...], wp_ref[...])
        xv = x_ref[...]
        rr = lax.rsqrt(jnp.mean(xv * xv, axis=-1, keepdims=True) + EPS)
        xhat = xv * rr
        scaled = dh * g_ref[...]
        gx_ref[...] = dy_ref[...] + rr * (scaled - xhat * jnp.mean(xhat * scaled, axis=-1, keepdims=True))
        dg_ref[...] += jnp.sum(dh * xhat, axis=0, keepdims=True)

        def small_all_reduce():
            leaf_refs, (loss_ref, dw4_ref) = small_in[0:nv], small_in[nv:]
            vec_out, dw4_out = small_out
            vec_mine, vec_recv, dw4_recv = rest[0:3]
            cx, cy, c = _my_place()
            me_lin = 4 * cx + 2 * cy + c

            def copy(k, src, dst, base):
                peer = (me_lin + k) % 8
                return pltpu.make_async_remote_copy(
                    src_ref=src, dst_ref=dst.at[me_lin], send_sem=send_sems.at[base + k - 1],
                    recv_sem=recv_sems.at[base + k - 1], device_id=(peer // 4, (peer // 2) % 2, peer % 2),
                    device_id_type=MESH)

            vec_mine[...] = jnp.zeros_like(vec_mine)
            vec_mine[0:1, :] = dg_ref[...]
            for (_, row, _), ref in zip(VEC_LEAVES[1:], leaf_refs):
                vec_mine[row:row + 1, 0:ref.shape[1]] = ref[...]
            vec_mine[VEC_LOSS_ROW:VEC_LOSS_ROW + 1, 0:LANES] = loss_ref[...]
            copies = [copy(k, src, dst, base) for k in range(1, 8)
                      for src, dst, base in ((vec_mine, vec_recv, small_base), (dw4_ref, dw4_recv, small_base + 7))]
            for cp in copies:
                cp.start()
            for cp in copies:
                cp.wait_recv()
            vec_recv[me_lin] = vec_mine[...]
            dw4_recv[me_lin] = dw4_ref[...]
            vtot, wtot = vec_recv[0], dw4_recv[0]
            for d in range(1, 8):
                vtot = vtot + vec_recv[d]
                wtot = wtot + dw4_recv[d]
            vec_out[...] = vtot
            dw4_out[...] = wtot
            for cp in copies:
                cp.wait_send()

        @pl.when(i == n - 1)
        def _():
            if red is not None:
                red.sum_and_share()
            if smalls is not None:
                small_all_reduce()
            if red is not None:
                red.finish()

    any_spec = pl.BlockSpec(memory_space=pl.ANY)
    scratch = [pltpu.VMEM((t, PROJ_PAD), BF16)] + _ShardReduce.scratch(gparts, axes)
    out_shape = [jax.ShapeDtypeStruct((s, D_MODEL), F32), jax.ShapeDtypeStruct((1, D_MODEL), F32)]
    out_shape += [jax.ShapeDtypeStruct(_shard_shape(g), F32) for g in gparts]
    out_specs = [_rows(t, D_MODEL), _full((1, D_MODEL))] + [any_spec] * na
    small_args = []
    if smalls is not None:
        small_args = [*vec_leaves, loss_row, dw4]
        out_shape += [jax.ShapeDtypeStruct((VEC_ROWS, D_MODEL), F32), jax.ShapeDtypeStruct(dw4.shape, F32)]
        out_specs += [_full((VEC_ROWS, D_MODEL)), _full(dw4.shape)]
        scratch += [pltpu.VMEM((VEC_ROWS, D_MODEL), F32), pltpu.VMEM((8, VEC_ROWS, D_MODEL), F32),
                    pltpu.VMEM((8,) + dw4.shape, F32)]
    if na or smalls is not None:
        n_sems = small_base + 14
        scratch += [pltpu.SemaphoreType.DMA((n_sems,)), pltpu.SemaphoreType.DMA((n_sems,)),
                    pltpu.SemaphoreType.DMA((max(_ShardReduce.LOCAL * na, 1),))]
    return pl.pallas_call(
        body, name="in_bwd_x", grid=(n,), out_shape=tuple(out_shape),
        in_specs=[_rows(t, D_MODEL), _rows(t, D_MODEL), _full((1, D_MODEL)),
                  pl.BlockSpec((PROJ_PAD, D_MODEL), lambda i: (0, 0), pipeline_mode=pl.Buffered(1))]
        + _dproj_specs(t) + [any_spec] * na + [_full(a.shape) for a in small_args],
        out_specs=tuple(out_specs), scratch_shapes=scratch, compiler_params=_params(),
    )(x, dy, norm_g, wp, *dparts, *gparts, *small_args)


def _in_bwd_w(hb, dpa, dgb, dpm, fox):
    s = hb.shape[0]
    t = TILE
    n = s // t
    f_hi = F_ORIG_LO + FOX_HEADS
    n_in = 4 + len(fox)

    def body(*refs):
        h_ref, dpa_ref, dgb_ref, dpm_ref = refs[0:4]
        fox_refs = refs[4:n_in]
        dw_ref, dqk_ref, dv_ref, dfb_ref, dqg_ref, dkg_ref, dbf_ref = refs[n_in:n_in + 7]
        fox_scratch = refs[n_in + 7:]
        i = pl.program_id(0)

        @pl.when(i == 0)
        def _():
            dw_ref[...] = jnp.zeros_like(dw_ref)

        hv = h_ref[...]

        def rows_of(lo, ref, cols=slice(None)):
            def add():
                dproj = ref[:, cols]
                dw_ref[lo:lo + dproj.shape[1], :] += _dot(dproj, hv, TN)
            return add

        q_cols, k_cols = slice(0, FOX_WIDTH), slice(FOX_WIDTH, 2 * FOX_WIDTH)
        between = (rows_of(0, dpa_ref), rows_of(f_hi, dgb_ref), rows_of(f_hi + FOX_WIDTH, dpm_ref), None,
                   rows_of(QB_LO, dqk_ref, q_cols), rows_of(VB_LO, dv_ref), None, None, rows_of(KB_LO, dqk_ref, k_cols))
        _fox_post_tile(i, n, t, *fox_refs, dqk_ref, dv_ref, dfb_ref, dqg_ref, dkg_ref, dbf_ref, *fox_scratch, between)
        dw_ref[F_ORIG_LO:f_hi, :] += _dot(dfb_ref[...].astype(BF16), hv, TN)[0:FOX_HEADS, :]

    def rev(w):
        return _rows_rev(t, w, n)

    row = jax.ShapeDtypeStruct((1, LANES), F32)
    return pl.pallas_call(
        body, name="in_bwd_w", grid=(n,),
        out_shape=(jax.ShapeDtypeStruct((IN_WIDTH, D_MODEL), F32), jax.ShapeDtypeStruct((s, 2 * FOX_WIDTH), BF16),
                   jax.ShapeDtypeStruct((s, FOX_WIDTH), BF16), jax.ShapeDtypeStruct((s, LANES), F32), row, row, row),
        in_specs=[rev(D_MODEL), rev(512), rev(FOX_WIDTH), rev(512), rev(HEAD_BLOCKS), rev(HEAD_BLOCKS),
                  rev(HEAD_BLOCKS), rev(2 * FOX_WIDTH), rev(LANES), _full((1, LANES)), _full((1, FOX_WIDTH)),
                  _full((1, FOX_WIDTH))],
        out_specs=(pl.BlockSpec((IN_WIDTH, D_MODEL), lambda i: (0, 0), pipeline_mode=pl.Buffered(1)),
                   rev(2 * FOX_WIDTH), rev(FOX_WIDTH), rev(LANES), _full((1, LANES)), _full((1, LANES)),
                   _full((1, LANES))),
        scratch_shapes=[pltpu.VMEM((1, FOX_WIDTH), F32), pltpu.VMEM((1, FOX_WIDTH), F32), pltpu.VMEM((1, LANES), F32)],
        compiler_params=_params(),
    )(hb, dpa, dgb, dpm, *fox)


def _adamw_math(w_ref, gv, m_ref, v_ref, d_ref, nm_ref, nv_ref):
    nm = ADAM_B1 * m_ref[...] + (1.0 - ADAM_B1) * gv
    nv = ADAM_B2 * v_ref[...] + (1.0 - ADAM_B2) * (gv * gv)
    m_hat = nm / (1.0 - ADAM_B1 ** ADAM_STEP)
    v_hat = nv / (1.0 - ADAM_B2 ** ADAM_STEP)
    d_ref[...] = -ADAM_LR * (m_hat / (jnp.sqrt(v_hat) + ADAM_EPS) + ADAM_WD * w_ref[...])
    nm_ref[...] = nm
    nv_ref[...] = nv


def _adamw_flat(name, w, g, m, v):
    rows, cols = g.shape
    per_row = cols // LANES

    def body(w_ref, g_ref, m_ref, v_ref, gf_ref, d_ref, nm_ref, nv_ref):
        for k in range(per_row):
            gf_ref[pl.ds(k, rows, stride=per_row), :] = g_ref[:, k * LANES:(k + 1) * LANES]
        _adamw_math(w_ref, gf_ref[...], m_ref, v_ref, d_ref, nm_ref, nv_ref)

    def whole(shape):
        return pl.BlockSpec(shape, lambda i: (0, 0), pipeline_mode=pl.Buffered(1))

    return pl.pallas_call(
        body, name=name, grid=(1,),
        out_shape=(jax.ShapeDtypeStruct(w.shape, F32),) * 4,
        in_specs=[whole(w.shape), whole(g.shape), whole(w.shape), whole(w.shape)], out_specs=(whole(w.shape),) * 4,
        compiler_params=_params(),
    )(w, g, m, v)


def _adamw_rest(vec, dw4, leaves, pool, shards):
    nl = len(VEC_LEAVES) + 1
    ns = len(shards)

    def body(*refs):
        vec_ref, dw4_ref = refs[0:2]
        wmv = refs[2:2 + 3 * nl]
        shard_in = refs[2 + 3 * nl:2 + 3 * nl + 4 * ns]
        o = 2 + 3 * nl + 4 * ns
        loss_ref = refs[o]
        outs = refs[o + 1:o + 1 + 4 * nl]
        shard_out = refs[o + 1 + 4 * nl:]
        loss_ref[...] = vec_ref[VEC_LOSS_ROW:VEC_LOSS_ROW + 1, 0:1]
        for k in range(nl):
            if k < nl - 1:
                _, row, width = VEC_LEAVES[k]
                gv = vec_ref[row:row + 1, 0:width]
            else:
                gv = dw4_ref[...]
            w_ref, m_ref, v_ref = wmv[3 * k:3 * k + 3]
            g_ref, d_ref, nm_ref, nv_ref = outs[4 * k:4 * k + 4]
            g_ref[...] = gv
            _adamw_math(w_ref, gv, m_ref, v_ref, d_ref, nm_ref, nv_ref)
        for k in range(ns):
            w_ref, g_ref, m_ref, v_ref = shard_in[4 * k:4 * k + 4]
            _adamw_math(w_ref, g_ref[...], m_ref, v_ref, *shard_out[3 * k:3 * k + 3])

    shapes = [jax.ShapeDtypeStruct((1, width), F32) for _, _, width in VEC_LEAVES] + [
        jax.ShapeDtypeStruct(dw4.shape, F32)]
    flat_in = [a for triple in list(leaves) + [pool] for a in triple] + [a for quad in shards for a in quad]
    res = pl.pallas_call(
        body, name="adamw_rest",
        out_shape=(jax.ShapeDtypeStruct((1, 1), F32),) + tuple(s for s in shapes for _ in range(4))
        + tuple(jax.ShapeDtypeStruct(quad[0].shape, F32) for quad in shards for _ in range(3)),
        compiler_params=pltpu.CompilerParams(vmem_limit_bytes=VMEM_LIMIT),
    )(vec, dw4, *flat_in)
    per = [res[1 + 4 * k:5 + 4 * k] for k in range(nl)]
    big = res[1 + 4 * nl:]
    return (res[0], [p[0] for p in per], [p[1] for p in per], [p[2] for p in per], [p[3] for p in per],
            [big[3 * k:3 * k + 3] for k in range(ns)])


def _tile_heads(g, n):
    return jnp.tile(g.reshape(1, HEAD_DIM), (1, n))


def kernel(x, mem, norm_g, w_in, b_f, w_pool, pool_scale, fox_q_g, fox_k_g, mem_norm_g, w_mem_kv, mem_q_g, mem_k_g, w_out, loss_target, m_norm_g, m_w_in, m_b_f, m_w_pool, m_pool_scale, m_fox_q_g, m_fox_k_g, m_mem_norm_g, m_w_mem_kv, m_mem_q_g, m_mem_k_g, m_w_out, v_norm_g, v_w_in, v_b_f, v_w_pool, v_pool_scale, v_fox_q_g, v_fox_k_g, v_mem_norm_g, v_w_mem_kv, v_mem_q_g, v_mem_k_g, v_w_out):
    w_in_t = w_in[0].T
    axes = (1, 0, 0)

    g_in, g_kv, g_out = _all_gather_weights([w_in_t, w_mem_kv[0], w_out[0]], axes)
    tiled = _tiled_params(b_f, fox_q_g, fox_k_g, mem_q_g, mem_k_g)
    fwd, wp = _fwd_in(x[0], norm_g, g_in, *tiled[0:3])
    w_kv_b = g_kv.reshape(D_MODEL, 2 * MEM_WIDTH)
    w_out_b = g_out.reshape(D_MODEL, D_MODEL)
    w4 = w_pool.reshape(POOL_ROWS, HEAD_DIM)
    dy, hb, dpa, dgb, dpm, fox, g_w_kv, g_w_out, (dmemnorm_g, dpscale, dmq_g, dmk_g), loss_row, dw4 = _local_partials(
        x[0], mem[0], loss_target[0], fwd, w_kv_b, w_out_b, tiled, w4, pool_scale, mem_norm_g, axes[1:])
    dwp, dqk, dvb, dfb, dfq_g, dfk_g, dbf = _in_bwd_w(hb, dpa, dgb, dpm, fox)
    dparts = (dpa, dqk, dvb, dgb, dpm, dfb)
    vec_leaves = (dmemnorm_g, dpscale, dbf, dfq_g, dfk_g, dmq_g, dmk_g)
    grad_x, _, g_w_in_t, vec, dw4_sum = _in_bwd_x(
        x[0], dy, norm_g, wp, dparts, [dwp], axes[0:1], (vec_leaves, loss_row, dw4))

    small_wmv = [(norm_g, m_norm_g, v_norm_g), (mem_norm_g, m_mem_norm_g, v_mem_norm_g),
                 (pool_scale, m_pool_scale, v_pool_scale), (b_f, m_b_f, v_b_f), (fox_q_g, m_fox_q_g, v_fox_q_g),
                 (fox_k_g, m_fox_k_g, v_fox_k_g), (mem_q_g, m_mem_q_g, v_mem_q_g), (mem_k_g, m_mem_k_g, v_mem_k_g)]
    pool_wmv = tuple(a.reshape(POOL_ROWS, HEAD_DIM) for a in (w_pool, m_w_pool, v_w_pool))
    loss, *small_out, (upd_kv, upd_out) = _adamw_rest(
        vec, dw4_sum, small_wmv, pool_wmv, [(w_mem_kv[0], g_w_kv, m_w_mem_kv[0], v_w_mem_kv[0]),
                                             (w_out[0], g_w_out, m_w_out[0], v_w_out[0])])
    tiles = D_MODEL // LANES

    def flat(a):
        return a.reshape(tiles, LANES, -1).transpose(2, 0, 1).reshape(-1, LANES)

    def unflat(a):
        return a.reshape(-1, tiles, LANES).transpose(1, 2, 0).reshape(w_in.shape)

    g_in_flat, *upd_in = _adamw_flat("adamw_w_in", flat(w_in), g_w_in_t, flat(m_w_in), flat(v_w_in))
    big = [[unflat(g_in_flat), g_w_kv[None], g_w_out[None]]]
    big += [[unflat(upd_in[k]), upd_kv[k][None], upd_out[k][None]] for k in range(3)]

    def leaves(k):
        sm = small_out[k]
        b_in, b_kv, b_out = big[k]
        return (sm[0], b_in, sm[3], sm[8].reshape(w_pool.shape), sm[2], sm[4], sm[5], sm[1], b_kv, sm[6], sm[7], b_out)

    return (loss.reshape(()), grad_x[None], *leaves(0), *leaves(1), *leaves(2), *leaves(3))


def _tiled_params(b_f, fox_q_g, fox_k_g, mem_q_g, mem_k_g):
    return (jnp.pad(b_f, ((0, 0), (0, LANES - FOX_HEADS))), _tile_heads(fox_q_g, FOX_HEADS),
            _tile_heads(fox_k_g, FOX_HEADS), _tile_heads(mem_q_g, 4), _tile_heads(mem_k_g, 4))


def _local_partials(xs, mems, tgt, fwd, w_kv_b, w_out_b, tiled, w4, pool_scale, mem_norm_g, axes):
    hb, pa, qk, qa, ka, va, gb, pm, fb = fwd
    bf_pad, fq_g, fk_g, mq_g, mk_g = tiled

    ma, db, mm, mnb, kv, kmn, vmb = _side_fwd(pa, pm, w4, pool_scale, mq_g, mems, mem_norm_g, w_kv_b, mk_g)
    o, mb, r4 = _fox_fwd(qa, ka, va, gb)
    dy, dma, dmm, dw_out, loss_row, doa, dgb, rr = _out_loss(xs, tgt, ma, mb, mm, w_out_b, gb, o, r4)

    dpa, dpm, dw4, dpscale, dmq_g, dw_kv, dmemnorm_g, dmk_g = _side_bwd(
        pa, db, dma, w4, pool_scale, pm, dmm, kmn, vmb, mq_g, kv, mnb, mems, w_kv_b, mk_g, mem_norm_g)
    if axes:
        parts = [dw_kv.reshape(4, D_MODEL // 4, 2 * MEM_WIDTH), dw_out.reshape(4, D_MODEL // 4, D_MODEL)]
        dka, dva, dqa, dw_kv, dw_out = _fox_bwd(ka, va, qa, doa, rr, parts, axes)
    else:
        dka, dva, dqa = _fox_bwd(ka, va, qa, doa, rr, [], ())
    fox = (dqa, dka, dva, qk, fb, bf_pad, fq_g, fk_g)
    return dy, hb, dpa, dgb, dpm, fox, dw_kv, dw_out, (dmemnorm_g, dpscale, dmq_g, dmk_g), loss_row, dw4
```
